```python
import math
import jax
import jax.numpy as jnp
from jax import lax

D_MODEL = 1024
BATCH = 16
SEQ = 4096
DEPTH = 1

D_FF = 2816
FFN_RESIDUAL_WEIGHT = 0.5
SSM_D_INNER = 2 * D_MODEL
SSM_HEAD_DIM = 64
SSM_HEADS = SSM_D_INNER // SSM_HEAD_DIM
SSM_GROUPS = 4
SSM_STATE = 128
SSM_CONV = 4
SSM_CHUNK = 128
SSM_CONV_DIM = SSM_D_INNER + 2 * SSM_GROUPS * SSM_STATE
ATTN_Q_HEADS = 16
ATTN_KV_HEADS = 4
ATTN_HEAD_DIM = 64
ATTN_WINDOW = 128
ATTN_BLOCK = 128
ATTN_Q_DIM = ATTN_Q_HEADS * ATTN_HEAD_DIM
ATTN_KV_DIM = ATTN_KV_HEADS * ATTN_HEAD_DIM
REL_BUCKETS = 32
REL_MAX_DISTANCE = 128
RMS_EPS = 1e-6
IN_SPLITS = (D_MODEL, D_MODEL, SSM_D_INNER, SSM_CONV_DIM, SSM_HEADS, ATTN_Q_DIM, ATTN_KV_DIM, ATTN_KV_DIM)
IN_COLS = D_MODEL + D_MODEL + SSM_D_INNER + SSM_CONV_DIM + SSM_HEADS + ATTN_Q_DIM + 2 * ATTN_KV_DIM

kernel_name = 'hybrid_ssd_swa_sink_macaron_block'


def rmsnorm(x, g):
    xf = x.astype(jnp.float32)
    y = xf * lax.rsqrt(jnp.mean(xf * xf, axis=-1, keepdims=True) + RMS_EPS)
    return (y * g.astype(jnp.float32)).astype(x.dtype)


def swiglu(x, w_gate, w_up, w_down):
    return (jax.nn.silu(x @ w_gate) * (x @ w_up)) @ w_down


def t5_causal_bucket(dist):
    max_exact = REL_BUCKETS // 2
    d = jnp.maximum(dist, 1).astype(jnp.float32)
    large = max_exact + (jnp.log(d / max_exact) / math.log(REL_MAX_DISTANCE / max_exact)
                         * (REL_BUCKETS - max_exact)).astype(jnp.int32)
    large = jnp.minimum(large, REL_BUCKETS - 1)
    return jnp.where(dist < max_exact, dist, large)


def causal_depthwise_conv(x, w, bias):
    y = lax.conv_general_dilated(
        x, w[:, None, :].astype(x.dtype), window_strides=(1,), padding=[(SSM_CONV - 1, 0)],
        dimension_numbers=('NWC', 'WIO', 'NWC'), feature_group_count=x.shape[-1])
    return y + bias.astype(x.dtype)


def ssd_chunked_scan(xs, dt, a, bm, cm):
    b, s = xs.shape[:2]
    nc = s // SSM_CHUNK
    q = SSM_CHUNK
    r = SSM_HEADS // SSM_GROUPS
    f32 = jnp.float32
    x = (xs.astype(f32) * dt[..., None]).reshape(b, nc, q, SSM_GROUPS, r, SSM_HEAD_DIM)
    a_cs = jnp.cumsum((dt * a).reshape(b, nc, q, SSM_GROUPS, r), axis=2)
    bc = bm.astype(f32).reshape(b, nc, q, SSM_GROUPS, SSM_STATE)
    cc = cm.astype(f32).reshape(b, nc, q, SSM_GROUPS, SSM_STATE)
    causal = jnp.tril(jnp.ones((q, q), dtype=bool))[:, :, None, None]
    seg = a_cs[:, :, :, None] - a_cs[:, :, None]
    decay = jnp.exp(jnp.where(causal, seg, -jnp.inf))
    scores = jnp.einsum('bcign,bcjgn->bcijg', cc, bc)
    y_diag = jnp.einsum('bcijgr,bcjgrp->bcigrp', scores[..., None] * decay, x)
    x_w = x * jnp.exp(a_cs[:, :, -1:] - a_cs)[..., None]
    states = jnp.einsum('bcjgn,bcjgrp->bcgrpn', bc, x_w)
    chunk_decay = jnp.exp(a_cs[:, :, -1])

    def step(h, inp):
        s_c, d_c = inp
        return h * d_c[..., None, None] + s_c, h

    h0 = jnp.zeros((b, SSM_GROUPS, r, SSM_HEAD_DIM, SSM_STATE), f32)
    _, prev = lax.scan(step, h0, (jnp.moveaxis(states, 1, 0), jnp.moveaxis(chunk_decay, 1, 0)))
    prev = jnp.moveaxis(prev, 0, 1)
    y_off = jnp.einsum('bcign,bcgrpn->bcigrp', cc, prev) * jnp.exp(a_cs)[..., None]
    return (y_diag + y_off).reshape(b, s, SSM_HEADS, SSM_HEAD_DIM)


def ssd_branch(z, xbc, dt_raw, conv_w, conv_b, dt_bias, a_log, d_skip, norm_g):
    b, s = z.shape[:2]
    f32 = jnp.float32
    xbc = jax.nn.silu(causal_depthwise_conv(xbc, conv_w, conv_b))
    xs = xbc[..., :SSM_D_INNER].reshape(b, s, SSM_HEADS, SSM_HEAD_DIM)
    bm = xbc[..., SSM_D_INNER:SSM_D_INNER + SSM_GROUPS * SSM_STATE].reshape(b, s, SSM_GROUPS, SSM_STATE)
    cm = xbc[..., SSM_D_INNER + SSM_GROUPS * SSM_STATE:].reshape(b, s, SSM_GROUPS, SSM_STATE)
    dt = jax.nn.softplus(dt_raw.astype(f32) + dt_bias.astype(f32))
    a = -jnp.exp(a_log.astype(f32))
    y = ssd_chunked_scan(xs, dt, a, bm, cm)
    y = y + d_skip.astype(f32)[:, None] * xs.astype(f32)
    yg = (y.reshape(b, s, SSM_D_INNER) * jax.nn.silu(z.astype(f32))).reshape(b, s, SSM_GROUPS, -1)
    yg = yg * lax.rsqrt(jnp.mean(yg * yg, axis=-1, keepdims=True) + RMS_EPS)
    return (yg.reshape(b, s, SSM_D_INNER) * norm_g.astype(f32)).astype(z.dtype)


def swa_sink_attention(q, k, v, sinks, rel_table):
    b, s = q.shape[:2]
    nb = s // ATTN_BLOCK
    r = ATTN_Q_HEADS // ATTN_KV_HEADS
    f32 = jnp.float32
    blk = ATTN_BLOCK
    qb = q.astype(f32).reshape(b, nb, blk, ATTN_KV_HEADS, r, ATTN_HEAD_DIM) * (ATTN_HEAD_DIM ** -0.5)
    kb = k.astype(f32).reshape(b, nb, blk, ATTN_KV_HEADS, ATTN_HEAD_DIM)
    vb = v.astype(f32).reshape(b, nb, blk, ATTN_KV_HEADS, ATTN_HEAD_DIM)

    def band(t):
        prev = jnp.concatenate([jnp.zeros_like(t[:, :1]), t[:, :-1]], axis=1)
        return jnp.concatenate([prev, t], axis=2)

    kk, vv = band(kb), band(vb)
    qi = jnp.arange(blk)[:, None]
    kj = jnp.arange(2 * blk)[None, :]
    dist = qi + blk - kj
    in_window = (dist >= 0) & (dist < ATTN_WINDOW)
    key_exists = (jnp.arange(nb)[:, None, None] > 0) | (kj >= blk)[None]
    mask = in_window[None] & key_exists
    bias = rel_table.astype(f32)[t5_causal_bucket(jnp.maximum(dist, 0))]
    bias = jnp.transpose(bias, (2, 0, 1)).reshape(ATTN_KV_HEADS, r, 1, blk, 2 * blk)
    logits = jnp.einsum('bnikrd,bnjkd->bkrnij', qb, kk) + bias
    logits = jnp.where(mask, logits, -jnp.inf)
    sink = sinks.astype(f32).reshape(ATTN_KV_HEADS, r, 1, 1)
    m = jnp.maximum(logits.max(axis=-1), sink)
    p = jnp.exp(logits - m[..., None])
    p = p / (p.sum(axis=-1) + jnp.exp(sink - m))[..., None]
    o = jnp.einsum('bkrnij,bnjkd->bnikrd', p, vv)
    return o.reshape(b, s, ATTN_Q_DIM).astype(q.dtype)


def hybrid_mixer(u, w_in, conv_w, conv_b, dt_bias, a_log, d_skip, ssm_norm_g, w_ssm_proj,
                 attn_sinks, rel_table, w_attn_proj, w_out):
    proj = u @ w_in
    parts = []
    start = 0
    for size in IN_SPLITS:
        parts.append(proj[..., start:start + size])
        start += size
    g_ssm, g_attn, z, xbc, dt_raw, q, k, v = parts
    y_ssm = ssd_branch(z, xbc, dt_raw, conv_w, conv_b, dt_bias, a_log, d_skip, ssm_norm_g) @ w_ssm_proj
    y_attn = swa_sink_attention(q, k, v, attn_sinks, rel_table) @ w_attn_proj
    merged = jax.nn.sigmoid(g_ssm) * y_ssm + jax.nn.sigmoid(g_attn) * y_attn
    return merged @ w_out


def _fwd_setup_inputs(seed: int = 0) -> dict:
    key = jax.random.key(seed)
    ks = jax.random.split(key, 32)
    f32 = jnp.float32

    def dense(k, fan_in, fan_out):
        return jax.random.normal(k, (DEPTH, fan_in, fan_out), f32) * fan_in ** -0.5

    def gain(k, n):
        return 1.0 + 0.05 * jax.random.normal(k, (DEPTH, n), f32)

    dt0 = jnp.exp(jax.random.uniform(ks[20], (DEPTH, SSM_HEADS), f32, math.log(1e-3), math.log(1e-1)))
    return {
        'x': jax.random.normal(ks[0], (BATCH, SEQ, D_MODEL), f32),
        'ffn1_pre_g': gain(ks[1], D_MODEL),
        'ffn1_w_gate': dense(ks[2], D_MODEL, D_FF),
        'ffn1_w_up': dense(ks[3], D_MODEL, D_FF),
        'ffn1_w_down': dense(ks[4], D_FF, D_MODEL),
        'ffn1_post_g': gain(ks[5], D_MODEL),
        'mix_pre_g': gain(ks[6], D_MODEL),
        'w_in': dense(ks[7], D_MODEL, IN_COLS),
        'conv_w': jax.random.normal(ks[8], (DEPTH, SSM_CONV, SSM_CONV_DIM), f32) * SSM_CONV ** -0.5,
        'conv_b': 0.02 * jax.random.normal(ks[9], (DEPTH, SSM_CONV_DIM), f32),
        'dt_bias': dt0 + jnp.log(-jnp.expm1(-dt0)),
        'a_log': jnp.log(jax.random.uniform(ks[10], (DEPTH, SSM_HEADS), f32, 1.0, 16.0)),
        'd_skip': 1.0 + 0.1 * jax.random.normal(ks[11], (DEPTH, SSM_HEADS), f32),
        'ssm_norm_g': gain(ks[12], SSM_D_INNER),
        'w_ssm_proj': dense(ks[13], SSM_D_INNER, D_MODEL),
        'attn_sinks': 0.5 * jax.random.normal(ks[14], (DEPTH, ATTN_Q_HEADS), f32),
        'rel_bias_table': 0.5 * jax.random.normal(ks[15], (REL_BUCKETS, ATTN_Q_HEADS), f32),
        'w_attn_proj': dense(ks[16], ATTN_Q_DIM, D_MODEL),
        'w_out': dense(ks[17], D_MODEL, D_MODEL),
        'mix_post_g': gain(ks[18], D_MODEL),
        'ffn2_pre_g': gain(ks[19], D_MODEL),
        'ffn2_w_gate': dense(ks[21], D_MODEL, D_FF),
        'ffn2_w_up': dense(ks[22], D_MODEL, D_FF),
        'ffn2_w_down': dense(ks[23], D_FF, D_MODEL),
        'ffn2_post_g': gain(ks[24], D_MODEL),
    }


def _fwd_reference(x, ffn1_pre_g, ffn1_w_gate, ffn1_w_up, ffn1_w_down, ffn1_post_g, mix_pre_g, w_in,
              conv_w, conv_b, dt_bias, a_log, d_skip, ssm_norm_g, w_ssm_proj, attn_sinks,
              rel_bias_table, w_attn_proj, w_out, mix_post_g, ffn2_pre_g, ffn2_w_gate, ffn2_w_up,
              ffn2_w_down, ffn2_post_g):
    h = x
    for l in range(DEPTH):
        f1 = swiglu(rmsnorm(h, ffn1_pre_g[l]), ffn1_w_gate[l], ffn1_w_up[l], ffn1_w_down[l])
        h = h + FFN_RESIDUAL_WEIGHT * rmsnorm(f1, ffn1_post_g[l])
        mix = hybrid_mixer(rmsnorm(h, mix_pre_g[l]), w_in[l], conv_w[l], conv_b[l], dt_bias[l],
                           a_log[l], d_skip[l], ssm_norm_g[l], w_ssm_proj[l], attn_sinks[l],
                           rel_bias_table, w_attn_proj[l], w_out[l])
        h = h + rmsnorm(mix, mix_post_g[l])
        f2 = swiglu(rmsnorm(h, ffn2_pre_g[l]), ffn2_w_gate[l], ffn2_w_up[l], ffn2_w_down[l])
        h = h + FFN_RESIDUAL_WEIGHT * rmsnorm(f2, ffn2_post_g[l])
    return h


import jax as _jax
import jax.numpy as _jnp

TWIN_FORMAT = 'train_step'
FWD_PARAMS = ['x', 'ffn1_pre_g', 'ffn1_w_gate', 'ffn1_w_up', 'ffn1_w_down', 'ffn1_post_g', 'mix_pre_g', 'w_in', 'conv_w', 'conv_b', 'dt_bias', 'a_log', 'd_skip', 'ssm_norm_g', 'w_ssm_proj', 'attn_sinks', 'rel_bias_table', 'w_attn_proj', 'w_out', 'mix_post_g', 'ffn2_pre_g', 'ffn2_w_gate', 'ffn2_w_up', 'ffn2_w_down', 'ffn2_post_g']
TWIN_WEIGHTS = ['ffn1_pre_g', 'ffn1_w_gate', 'ffn1_w_up', 'ffn1_w_down', 'ffn1_post_g', 'mix_pre_g', 'w_in', 'conv_w', 'conv_b', 'dt_bias', 'a_log', 'd_skip', 'ssm_norm_g', 'w_ssm_proj', 'attn_sinks', 'rel_bias_table', 'w_attn_proj', 'w_out', 'mix_post_g', 'ffn2_pre_g', 'ffn2_w_gate', 'ffn2_w_up', 'ffn2_w_down', 'ffn2_post_g']
TWIN_DIFF_INPUT = 'x'
TWIN_INPUTS = ['x', 'ffn1_pre_g', 'ffn1_w_gate', 'ffn1_w_up', 'ffn1_w_down', 'ffn1_post_g', 'mix_pre_g', 'w_in', 'conv_w', 'conv_b', 'dt_bias', 'a_log', 'd_skip', 'ssm_norm_g', 'w_ssm_proj', 'attn_sinks', 'rel_bias_table', 'w_attn_proj', 'w_out', 'mix_post_g', 'ffn2_pre_g', 'ffn2_w_gate', 'ffn2_w_up', 'ffn2_w_down', 'ffn2_post_g', 'loss_target', 'm_ffn1_pre_g', 'm_ffn1_w_gate', 'm_ffn1_w_up', 'm_ffn1_w_down', 'm_ffn1_post_g', 'm_mix_pre_g', 'm_w_in', 'm_conv_w', 'm_conv_b', 'm_dt_bias', 'm_a_log', 'm_d_skip', 'm_ssm_norm_g', 'm_w_ssm_proj', 'm_attn_sinks', 'm_rel_bias_table', 'm_w_attn_proj', 'm_w_out', 'm_mix_post_g', 'm_ffn2_pre_g', 'm_ffn2_w_gate', 'm_ffn2_w_up', 'm_ffn2_w_down', 'm_ffn2_post_g', 'v_ffn1_pre_g', 'v_ffn1_w_gate', 'v_ffn1_w_up', 'v_ffn1_w_down', 'v_ffn1_post_g', 'v_mix_pre_g', 'v_w_in', 'v_conv_w', 'v_conv_b', 'v_dt_bias', 'v_a_log', 'v_d_skip', 'v_ssm_norm_g', 'v_w_ssm_proj', 'v_attn_sinks', 'v_rel_bias_table', 'v_w_attn_proj', 'v_w_out', 'v_mix_post_g', 'v_ffn2_pre_g', 'v_ffn2_w_gate', 'v_ffn2_w_up', 'v_ffn2_w_down', 'v_ffn2_post_g']
TWIN_OUTPUTS = ['loss', 'grad_x', 'grad_ffn1_pre_g', 'grad_ffn1_w_gate', 'grad_ffn1_w_up', 'grad_ffn1_w_down', 'grad_ffn1_post_g', 'grad_mix_pre_g', 'grad_w_in', 'grad_conv_w', 'grad_conv_b', 'grad_dt_bias', 'grad_a_log', 'grad_d_skip', 'grad_ssm_norm_g', 'grad_w_ssm_proj', 'grad_attn_sinks', 'grad_rel_bias_table', 'grad_w_attn_proj', 'grad_w_out', 'grad_mix_post_g', 'grad_ffn2_pre_g', 'grad_ffn2_w_gate', 'grad_ffn2_w_up', 'grad_ffn2_w_down', 'grad_ffn2_post_g', 'delta_ffn1_pre_g', 'delta_ffn1_w_gate', 'delta_ffn1_w_up', 'delta_ffn1_w_down', 'delta_ffn1_post_g', 'delta_mix_pre_g', 'delta_w_in', 'delta_conv_w', 'delta_conv_b', 'delta_dt_bias', 'delta_a_log', 'delta_d_skip', 'delta_ssm_norm_g', 'delta_w_ssm_proj', 'delta_attn_sinks', 'delta_rel_bias_table', 'delta_w_attn_proj', 'delta_w_out', 'delta_mix_post_g', 'delta_ffn2_pre_g', 'delta_ffn2_w_gate', 'delta_ffn2_w_up', 'delta_ffn2_w_down', 'delta_ffn2_post_g', 'new_m_ffn1_pre_g', 'new_m_ffn1_w_gate', 'new_m_ffn1_w_up', 'new_m_ffn1_w_down', 'new_m_ffn1_post_g', 'new_m_mix_pre_g', 'new_m_w_in', 'new_m_conv_w', 'new_m_conv_b', 'new_m_dt_bias', 'new_m_a_log', 'new_m_d_skip', 'new_m_ssm_norm_g', 'new_m_w_ssm_proj', 'new_m_attn_sinks', 'new_m_rel_bias_table', 'new_m_w_attn_proj', 'new_m_w_out', 'new_m_mix_post_g', 'new_m_ffn2_pre_g', 'new_m_ffn2_w_gate', 'new_m_ffn2_w_up', 'new_m_ffn2_w_down', 'new_m_ffn2_post_g', 'new_v_ffn1_pre_g', 'new_v_ffn1_w_gate', 'new_v_ffn1_w_up', 'new_v_ffn1_w_down', 'new_v_ffn1_post_g', 'new_v_mix_pre_g', 'new_v_w_in', 'new_v_conv_w', 'new_v_conv_b', 'new_v_dt_bias', 'new_v_a_log', 'new_v_d_skip', 'new_v_ssm_norm_g', 'new_v_w_ssm_proj', 'new_v_attn_sinks', 'new_v_rel_bias_table', 'new_v_w_attn_proj', 'new_v_w_out', 'new_v_mix_post_g', 'new_v_ffn2_pre_g', 'new_v_ffn2_w_gate', 'new_v_ffn2_w_up', 'new_v_ffn2_w_down', 'new_v_ffn2_post_g']
TWIN_LEAF_KINDS = {'loss': 'loss', 'grad_x': 'grad_x', 'grad_ffn1_pre_g': 'grad_w', 'grad_ffn1_w_gate': 'grad_w', 'grad_ffn1_w_up': 'grad_w', 'grad_ffn1_w_down': 'grad_w', 'grad_ffn1_post_g': 'grad_w', 'grad_mix_pre_g': 'grad_w', 'grad_w_in': 'grad_w', 'grad_conv_w': 'grad_w', 'grad_conv_b': 'grad_w', 'grad_dt_bias': 'grad_w', 'grad_a_log': 'grad_w', 'grad_d_skip': 'grad_w', 'grad_ssm_norm_g': 'grad_w', 'grad_w_ssm_proj': 'grad_w', 'grad_attn_sinks': 'grad_w', 'grad_rel_bias_table': 'grad_w', 'grad_w_attn_proj': 'grad_w', 'grad_w_out': 'grad_w', 'grad_mix_post_g': 'grad_w', 'grad_ffn2_pre_g': 'grad_w', 'grad_ffn2_w_gate': 'grad_w', 'grad_ffn2_w_up': 'grad_w', 'grad_ffn2_w_down': 'grad_w', 'grad_ffn2_post_g': 'grad_w', 'delta_ffn1_pre_g': 'delta_w', 'delta_ffn1_w_gate': 'delta_w', 'delta_ffn1_w_up': 'delta_w', 'delta_ffn1_w_down': 'delta_w', 'delta_ffn1_post_g': 'delta_w', 'delta_mix_pre_g': 'delta_w', 'delta_w_in': 'delta_w', 'delta_conv_w': 'delta_w', 'delta_conv_b': 'delta_w', 'delta_dt_bias': 'delta_w', 'delta_a_log': 'delta_w', 'delta_d_skip': 'delta_w', 'delta_ssm_norm_g': 'delta_w', 'delta_w_ssm_proj': 'delta_w', 'delta_attn_sinks': 'delta_w', 'delta_rel_bias_table': 'delta_w', 'delta_w_attn_proj': 'delta_w', 'delta_w_out': 'delta_w', 'delta_mix_post_g': 'delta_w', 'delta_ffn2_pre_g': 'delta_w', 'delta_ffn2_w_gate': 'delta_w', 'delta_ffn2_w_up': 'delta_w', 'delta_ffn2_w_down': 'delta_w', 'delta_ffn2_post_g': 'delta_w', 'new_m_ffn1_pre_g': 'new_m', 'new_m_ffn1_w_gate': 'new_m', 'new_m_ffn1_w_up': 'new_m', 'new_m_ffn1_w_down': 'new_m', 'new_m_ffn1_post_g': 'new_m', 'new_m_mix_pre_g': 'new_m', 'new_m_w_in': 'new_m', 'new_m_conv_w': 'new_m', 'new_m_conv_b': 'new_m', 'new_m_dt_bias': 'new_m', 'new_m_a_log': 'new_m', 'new_m_d_skip': 'new_m', 'new_m_ssm_norm_g': 'new_m', 'new_m_w_ssm_proj': 'new_m', 'new_m_attn_sinks': 'new_m', 'new_m_rel_bias_table': 'new_m', 'new_m_w_attn_proj': 'new_m', 'new_m_w_out': 'new_m', 'new_m_mix_post_g': 'new_m', 'new_m_ffn2_pre_g': 'new_m', 'new_m_ffn2_w_gate': 'new_m', 'new_m_ffn2_w_up': 'new_m', 'new_m_ffn2_w_down': 'new_m', 'new_m_ffn2_post_g': 'new_m', 'new_v_ffn1_pre_g': 'new_v', 'new_v_ffn1_w_gate': 'new_v', 'new_v_ffn1_w_up': 'new_v', 'new_v_ffn1_w_down': 'new_v', 'new_v_ffn1_post_g': 'new_v', 'new_v_mix_pre_g': 'new_v', 'new_v_w_in': 'new_v', 'new_v_conv_w': 'new_v', 'new_v_conv_b': 'new_v', 'new_v_dt_bias': 'new_v', 'new_v_a_log': 'new_v', 'new_v_d_skip': 'new_v', 'new_v_ssm_norm_g': 'new_v', 'new_v_w_ssm_proj': 'new_v', 'new_v_attn_sinks': 'new_v', 'new_v_rel_bias_table': 'new_v', 'new_v_w_attn_proj': 'new_v', 'new_v_w_out': 'new_v', 'new_v_mix_post_g': 'new_v', 'new_v_ffn2_pre_g': 'new_v', 'new_v_ffn2_w_gate': 'new_v', 'new_v_ffn2_w_up': 'new_v', 'new_v_ffn2_w_down': 'new_v', 'new_v_ffn2_post_g': 'new_v'}


def _forward(args):
    return _fwd_reference(*[args[k] for k in FWD_PARAMS])


def _output_shape():
    out = _jax.eval_shape(lambda: _forward(_fwd_setup_inputs(0)))
    return out.shape, out.dtype

N_MICROBATCH = 1
ADAM_LR = 0.001
ADAM_B1 = 0.9
ADAM_B2 = 0.999
ADAM_EPS = 1e-08
ADAM_WD = 0.01
ADAM_STEP = 10
PER_EXAMPLE_BATCH_AXIS = {'x': 0, 'loss_target': 0}
SHARED_INPUTS = []
_WEIGHT_DTYPES = {'ffn1_pre_g': _jnp.float32, 'ffn1_w_gate': _jnp.float32, 'ffn1_w_up': _jnp.float32, 'ffn1_w_down': _jnp.float32, 'ffn1_post_g': _jnp.float32, 'mix_pre_g': _jnp.float32, 'w_in': _jnp.float32, 'conv_w': _jnp.float32, 'conv_b': _jnp.float32, 'dt_bias': _jnp.float32, 'a_log': _jnp.float32, 'd_skip': _jnp.float32, 'ssm_norm_g': _jnp.float32, 'w_ssm_proj': _jnp.float32, 'attn_sinks': _jnp.float32, 'rel_bias_table': _jnp.float32, 'w_attn_proj': _jnp.float32, 'w_out': _jnp.float32, 'mix_post_g': _jnp.float32, 'ffn2_pre_g': _jnp.float32, 'ffn2_w_gate': _jnp.float32, 'ffn2_w_up': _jnp.float32, 'ffn2_w_down': _jnp.float32, 'ffn2_post_g': _jnp.float32}
MOMENT_SCALE = {'ffn1_pre_g': 6.067712e-01, 'ffn1_w_gate': 2.564403e-01, 'ffn1_w_up': 2.675035e-01, 'ffn1_w_down': 4.488659e-01, 'ffn1_post_g': 1.583729e+01, 'mix_pre_g': 8.665837e-01, 'w_in': 2.856252e-01, 'conv_w': 3.638701e-01, 'conv_b': 8.618909e-01, 'dt_bias': 7.610886e-01, 'a_log': 1.817132e+00, 'd_skip': 2.236392e+00, 'ssm_norm_g': 4.861014e-01, 'w_ssm_proj': 7.044602e-01, 'attn_sinks': 6.408012e-02, 'rel_bias_table': 1.137494e-01, 'w_attn_proj': 9.912220e-02, 'w_out': 7.306106e-01, 'mix_post_g': 6.401351e+01, 'ffn2_pre_g': 5.313364e-01, 'ffn2_w_gate': 1.621203e-01, 'ffn2_w_up': 2.499535e-01, 'ffn2_w_down': 4.111881e-01, 'ffn2_post_g': 1.592585e+01}


def _to_microbatches(a, axis):
    t = _jnp.moveaxis(a, axis, 0)
    t = t.reshape((N_MICROBATCH, t.shape[0] // N_MICROBATCH) + t.shape[1:])
    return _jnp.moveaxis(t, 1, axis + 1)


def setup_inputs(seed: int = 0) -> dict:
    inp = _fwd_setup_inputs(seed)
    key = _jax.random.fold_in(_jax.random.key(seed), 7919)
    shape, _ = _output_shape()
    out = dict(inp)
    out["loss_target"] = _jax.random.normal(_jax.random.fold_in(key, 0), shape, _jnp.float32)
    for i, name in enumerate(TWIN_WEIGHTS):
        w = inp[name].astype(_jnp.float32)
        if MOMENT_SCALE is None:
            s = _jnp.sqrt(_jnp.mean(_jnp.square(w)) + 1e-30)
        else:
            s = MOMENT_SCALE[name]
        km, kv = _jax.random.split(_jax.random.fold_in(key, i + 1))
        out[name] = w
        out["m_" + name] = s * _jax.random.normal(km, w.shape, _jnp.float32)
        out["v_" + name] = (s * s) * _jax.random.uniform(kv, w.shape, _jnp.float32, 0.5, 1.5)
    if N_MICROBATCH > 1:
        for name, axis in PER_EXAMPLE_BATCH_AXIS.items():
            out[name] = _to_microbatches(out[name], axis)
    return {'x': out['x'], 'ffn1_pre_g': out['ffn1_pre_g'], 'ffn1_w_gate': out['ffn1_w_gate'], 'ffn1_w_up': out['ffn1_w_up'], 'ffn1_w_down': out['ffn1_w_down'], 'ffn1_post_g': out['ffn1_post_g'], 'mix_pre_g': out['mix_pre_g'], 'w_in': out['w_in'], 'conv_w': out['conv_w'], 'conv_b': out['conv_b'], 'dt_bias': out['dt_bias'], 'a_log': out['a_log'], 'd_skip': out['d_skip'], 'ssm_norm_g': out['ssm_norm_g'], 'w_ssm_proj': out['w_ssm_proj'], 'attn_sinks': out['attn_sinks'], 'rel_bias_table': out['rel_bias_table'], 'w_attn_proj': out['w_attn_proj'], 'w_out': out['w_out'], 'mix_post_g': out['mix_post_g'], 'ffn2_pre_g': out['ffn2_pre_g'], 'ffn2_w_gate': out['ffn2_w_gate'], 'ffn2_w_up': out['ffn2_w_up'], 'ffn2_w_down': out['ffn2_w_down'], 'ffn2_post_g': out['ffn2_post_g'], 'loss_target': out['loss_target'], 'm_ffn1_pre_g': out['m_ffn1_pre_g'], 'm_ffn1_w_gate': out['m_ffn1_w_gate'], 'm_ffn1_w_up': out['m_ffn1_w_up'], 'm_ffn1_w_down': out['m_ffn1_w_down'], 'm_ffn1_post_g': out['m_ffn1_post_g'], 'm_mix_pre_g': out['m_mix_pre_g'], 'm_w_in': out['m_w_in'], 'm_conv_w': out['m_conv_w'], 'm_conv_b': out['m_conv_b'], 'm_dt_bias': out['m_dt_bias'], 'm_a_log': out['m_a_log'], 'm_d_skip': out['m_d_skip'], 'm_ssm_norm_g': out['m_ssm_norm_g'], 'm_w_ssm_proj': out['m_w_ssm_proj'], 'm_attn_sinks': out['m_attn_sinks'], 'm_rel_bias_table': out['m_rel_bias_table'], 'm_w_attn_proj': out['m_w_attn_proj'], 'm_w_out': out['m_w_out'], 'm_mix_post_g': out['m_mix_post_g'], 'm_ffn2_pre_g': out['m_ffn2_pre_g'], 'm_ffn2_w_gate': out['m_ffn2_w_gate'], 'm_ffn2_w_up': out['m_ffn2_w_up'], 'm_ffn2_w_down': out['m_ffn2_w_down'], 'm_ffn2_post_g': out['m_ffn2_post_g'], 'v_ffn1_pre_g': out['v_ffn1_pre_g'], 'v_ffn1_w_gate': out['v_ffn1_w_gate'], 'v_ffn1_w_up': out['v_ffn1_w_up'], 'v_ffn1_w_down': out['v_ffn1_w_down'], 'v_ffn1_post_g': out['v_ffn1_post_g'], 'v_mix_pre_g': out['v_mix_pre_g'], 'v_w_in': out['v_w_in'], 'v_conv_w': out['v_conv_w'], 'v_conv_b': out['v_conv_b'], 'v_dt_bias': out['v_dt_bias'], 'v_a_log': out['v_a_log'], 'v_d_skip': out['v_d_skip'], 'v_ssm_norm_g': out['v_ssm_norm_g'], 'v_w_ssm_proj': out['v_w_ssm_proj'], 'v_attn_sinks': out['v_attn_sinks'], 'v_rel_bias_table': out['v_rel_bias_table'], 'v_w_attn_proj': out['v_w_attn_proj'], 'v_w_out': out['v_w_out'], 'v_mix_post_g': out['v_mix_post_g'], 'v_ffn2_pre_g': out['v_ffn2_pre_g'], 'v_ffn2_w_gate': out['v_ffn2_w_gate'], 'v_ffn2_w_up': out['v_ffn2_w_up'], 'v_ffn2_w_down': out['v_ffn2_w_down'], 'v_ffn2_post_g': out['v_ffn2_post_g']}


def _loss(weights, diff, rest, loss_target):
    with _jax.named_scope("forward"):
        args = {**rest, TWIN_DIFF_INPUT: diff, **{k: w.astype(_WEIGHT_DTYPES[k]) for k, w in weights.items()}}
        y = _forward(args)
    with _jax.named_scope("loss_head"):
        err = _jnp.square(y.astype(_jnp.float32) - loss_target)
        return 0.5 * _jnp.sum(_jnp.mean(err, axis=-1)) if err.ndim else 0.5 * err


def _adamw(w, g, m, v):
    m = ADAM_B1 * m + (1.0 - ADAM_B1) * g
    v = ADAM_B2 * v + (1.0 - ADAM_B2) * _jnp.square(g)
    m_hat = m / (1.0 - ADAM_B1 ** ADAM_STEP)
    v_hat = v / (1.0 - ADAM_B2 ** ADAM_STEP)
    delta = -ADAM_LR * (m_hat / (_jnp.sqrt(v_hat) + ADAM_EPS) + ADAM_WD * w)
    return delta, m, v


def reference(x, ffn1_pre_g, ffn1_w_gate, ffn1_w_up, ffn1_w_down, ffn1_post_g, mix_pre_g, w_in, conv_w, conv_b, dt_bias, a_log, d_skip, ssm_norm_g, w_ssm_proj, attn_sinks, rel_bias_table, w_attn_proj, w_out, mix_post_g, ffn2_pre_g, ffn2_w_gate, ffn2_w_up, ffn2_w_down, ffn2_post_g, loss_target, m_ffn1_pre_g, m_ffn1_w_gate, m_ffn1_w_up, m_ffn1_w_down, m_ffn1_post_g, m_mix_pre_g, m_w_in, m_conv_w, m_conv_b, m_dt_bias, m_a_log, m_d_skip, m_ssm_norm_g, m_w_ssm_proj, m_attn_sinks, m_rel_bias_table, m_w_attn_proj, m_w_out, m_mix_post_g, m_ffn2_pre_g, m_ffn2_w_gate, m_ffn2_w_up, m_ffn2_w_down, m_ffn2_post_g, v_ffn1_pre_g, v_ffn1_w_gate, v_ffn1_w_up, v_ffn1_w_down, v_ffn1_post_g, v_mix_pre_g, v_w_in, v_conv_w, v_conv_b, v_dt_bias, v_a_log, v_d_skip, v_ssm_norm_g, v_w_ssm_proj, v_attn_sinks, v_rel_bias_table, v_w_attn_proj, v_w_out, v_mix_post_g, v_ffn2_pre_g, v_ffn2_w_gate, v_ffn2_w_up, v_ffn2_w_down, v_ffn2_post_g):
    given = dict(x=x, ffn1_pre_g=ffn1_pre_g, ffn1_w_gate=ffn1_w_gate, ffn1_w_up=ffn1_w_up, ffn1_w_down=ffn1_w_down, ffn1_post_g=ffn1_post_g, mix_pre_g=mix_pre_g, w_in=w_in, conv_w=conv_w, conv_b=conv_b, dt_bias=dt_bias, a_log=a_log, d_skip=d_skip, ssm_norm_g=ssm_norm_g, w_ssm_proj=w_ssm_proj, attn_sinks=attn_sinks, rel_bias_table=rel_bias_table, w_attn_proj=w_attn_proj, w_out=w_out, mix_post_g=mix_post_g, ffn2_pre_g=ffn2_pre_g, ffn2_w_gate=ffn2_w_gate, ffn2_w_up=ffn2_w_up, ffn2_w_down=ffn2_w_down, ffn2_post_g=ffn2_post_g, loss_target=loss_target, m_ffn1_pre_g=m_ffn1_pre_g, m_ffn1_w_gate=m_ffn1_w_gate, m_ffn1_w_up=m_ffn1_w_up, m_ffn1_w_down=m_ffn1_w_down, m_ffn1_post_g=m_ffn1_post_g, m_mix_pre_g=m_mix_pre_g, m_w_in=m_w_in, m_conv_w=m_conv_w, m_conv_b=m_conv_b, m_dt_bias=m_dt_bias, m_a_log=m_a_log, m_d_skip=m_d_skip, m_ssm_norm_g=m_ssm_norm_g, m_w_ssm_proj=m_w_ssm_proj, m_attn_sinks=m_attn_sinks, m_rel_bias_table=m_rel_bias_table, m_w_attn_proj=m_w_attn_proj, m_w_out=m_w_out, m_mix_post_g=m_mix_post_g, m_ffn2_pre_g=m_ffn2_pre_g, m_ffn2_w_gate=m_ffn2_w_gate, m_ffn2_w_up=m_ffn2_w_up, m_ffn2_w_down=m_ffn2_w_down, m_ffn2_post_g=m_ffn2_post_g, v_ffn1_pre_g=v_ffn1_pre_g, v_ffn1_w_gate=v_ffn1_w_gate, v_ffn1_w_up=v_ffn1_w_up, v_ffn1_w_down=v_ffn1_w_down, v_ffn1_post_g=v_ffn1_post_g, v_mix_pre_g=v_mix_pre_g, v_w_in=v_w_in, v_conv_w=v_conv_w, v_conv_b=v_conv_b, v_dt_bias=v_dt_bias, v_a_log=v_a_log, v_d_skip=v_d_skip, v_ssm_norm_g=v_ssm_norm_g, v_w_ssm_proj=v_w_ssm_proj, v_attn_sinks=v_attn_sinks, v_rel_bias_table=v_rel_bias_table, v_w_attn_proj=v_w_attn_proj, v_w_out=v_w_out, v_mix_post_g=v_mix_post_g, v_ffn2_pre_g=v_ffn2_pre_g, v_ffn2_w_gate=v_ffn2_w_gate, v_ffn2_w_up=v_ffn2_w_up, v_ffn2_w_down=v_ffn2_w_down, v_ffn2_post_g=v_ffn2_post_g)
    weights = {n: given[n] for n in TWIN_WEIGHTS}
    shared = {n: given[n] for n in SHARED_INPUTS}
    per_example = {n: given[n] for n in ['x']}
    grad_fn = _jax.value_and_grad(_loss, argnums=(0, 1))

    def one_microbatch(ex, loss_target):
        ex = dict(ex)
        diff = ex.pop(TWIN_DIFF_INPUT)
        return grad_fn(weights, diff, {**shared, **ex}, loss_target)

    if N_MICROBATCH == 1:
        loss, (grad_w, grad_x) = one_microbatch(per_example, given["loss_target"])
    else:
        def body(carry, xs):
            loss_sum, grad_sum = carry
            l_k, (gw_k, gx_k) = one_microbatch(xs[0], xs[1])
            with _jax.named_scope("update"):
                return (loss_sum + l_k, _jax.tree.map(_jnp.add, grad_sum, gw_k)), gx_k

        init = (_jnp.zeros((), _jnp.float32), _jax.tree.map(_jnp.zeros_like, weights))
        (loss, grad_w), grad_x = _jax.lax.scan(body, init, (per_example, given["loss_target"]))
    with _jax.named_scope("update"):
        delta_w, new_m, new_v = {}, {}, {}
        for n in TWIN_WEIGHTS:
            delta_w[n], new_m[n], new_v[n] = _adamw(weights[n], grad_w[n], given["m_" + n], given["v_" + n])
    return (loss, grad_x, *[grad_w[n] for n in TWIN_WEIGHTS], *[delta_w[n] for n in TWIN_WEIGHTS],
            *[new_m[n] for n in TWIN_WEIGHTS], *[new_v[n] for n in TWIN_WEIGHTS])
```

```python
import functools
import math

import numpy as np
import jax
import jax.numpy as jnp
from jax import lax
from jax.experimental import pallas as pl
from jax.experimental.pallas import tpu as pltpu

F32 = jnp.float32
BF16 = jnp.bfloat16
MESH = pl.DeviceIdType.MESH
N_DEV = 8

SSM_HEAD_DIM = 64
SSM_GROUPS = 4
SSM_STATE = 128
SSM_CONV = 4
CHUNK = 128
ATTN_HEAD_DIM = 64
ATTN_Q_HEADS = 16
ATTN_KV_HEADS = 4
REL_BUCKETS = 32
REL_MAX_DISTANCE = 128
RMS_EPS = 1e-6
FFN_RESIDUAL_WEIGHT = 0.5
ADAM_LR, ADAM_B1, ADAM_B2, ADAM_EPS, ADAM_WD, ADAM_STEP = 0.001, 0.9, 0.999, 1e-08, 0.01, 10

LANES = 128
VMEM_LIMIT_BYTES = 56 * 1024 * 1024

NEG_INF = float("-inf")


def _params(*sem):
    return pltpu.CompilerParams(dimension_semantics=sem, vmem_limit_bytes=VMEM_LIMIT_BYTES)


def _tile(n, pref, mult=8):
    if n <= pref:
        return n
    t = (pref // mult) * mult
    while t >= mult:
        if n % t == 0:
            return t
        t -= mult
    return n


def _sigmoid(x):
    return 1.0 / (1.0 + jnp.exp(-x))


def _dot(a, b, dims):
    return lax.dot_general(a, b, (dims, ((), ())), preferred_element_type=F32)


NN = ((1,), (0,))
NT = ((1,), (1,))
TN = ((0,), (0,))


def _dot_hi(a, b, dims=NN):
    return lax.dot_general(a, b, (dims, ((), ())), preferred_element_type=F32, precision=lax.Precision.HIGHEST)


def _const_spec(shape):
    nd = len(shape)
    return pl.BlockSpec(shape, lambda *_: (0,) * nd)


def _mm_nt(a, bt, *, row_off=0, n_out=None, out_dtype=BF16, tm=512, tn=512, name):
    m, k = a.shape
    n_out = bt.shape[0] if n_out is None else n_out
    tm, tn = _tile(m, tm), _tile(n_out, tn, LANES)

    def body(a_ref, b_ref, o_ref):
        o_ref[...] = _dot(a_ref[...].astype(BF16), b_ref[...].astype(BF16), NT).astype(o_ref.dtype)

    return pl.pallas_call(
        body, out_shape=jax.ShapeDtypeStruct((m, n_out), out_dtype), grid=(m // tm, n_out // tn),
        in_specs=[pl.BlockSpec((tm, k), lambda i, j: (i, 0)), pl.BlockSpec((tn, k), lambda i, j: (j + row_off, 0))],
        out_specs=pl.BlockSpec((tm, tn), lambda i, j: (i, j)),
        name=name, compiler_params=_params("parallel", "arbitrary"))(a, bt)


def _mm_tn(a_list, b, *, out_dtype=F32, tm=512, tk=512, name):
    t, n = b.shape
    tk = _tile(t, tk)
    nk = t // tk
    widths = [a.shape[1] for a in a_list]
    tm = _tile(math.gcd(*widths), tm, LANES)
    assert all(w % tm == 0 for w in widths)
    starts = np.cumsum([0] + [w // tm for w in widths])
    nseg = len(a_list)

    def a_spec(s):
        lo, hi = int(starts[s]), int(starts[s + 1])

        def idx(i, k):
            active = jnp.logical_and(i >= lo, i < hi)
            return (jnp.where(active, k, 0), jnp.clip(i - lo, 0, hi - lo - 1))
        return pl.BlockSpec((tk, tm), idx)

    def body(*refs):
        a_refs, b_ref, o_ref, acc = refs[:nseg], refs[nseg], refs[nseg + 1], refs[nseg + 2]
        i, k = pl.program_id(0), pl.program_id(1)

        @pl.when(k == 0)
        def _():
            acc[...] = jnp.zeros_like(acc)

        bv = b_ref[...].astype(BF16)
        for s in range(nseg):
            lo, hi = int(starts[s]), int(starts[s + 1])

            @pl.when(jnp.logical_and(i >= lo, i < hi))
            def _(s=s):
                acc[...] += _dot(a_refs[s][...].astype(BF16), bv, TN)

        @pl.when(k == nk - 1)
        def _():
            o_ref[...] = acc[...].astype(o_ref.dtype)

    return pl.pallas_call(
        body, out_shape=jax.ShapeDtypeStruct((int(starts[-1]) * tm, n), out_dtype), grid=(int(starts[-1]), nk),
        in_specs=[a_spec(s) for s in range(nseg)] + [pl.BlockSpec((tk, n), lambda i, k: (k, 0))],
        out_specs=pl.BlockSpec((tm, n), lambda i, k: (i, 0)),
        scratch_shapes=[pltpu.VMEM((tm, n), F32)],
        name=name, compiler_params=_params("parallel", "arbitrary"))(*a_list, b)


def _mm_nn_rmsbwd(segs, x, g, dres, *, tm=512, name):
    t, d = x.shape
    tm = _tile(t, tm)
    counts = [a.shape[1] // tk for a, _, tk, _ in segs]
    starts = np.cumsum([0] + counts)
    nk = int(starts[-1])
    nseg = len(segs)

    def specs(s):
        lo, cnt = int(starts[s]), counts[s]
        tk, off = segs[s][2], segs[s][3]

        def rel(k):
            return jnp.clip(k - lo, 0, cnt - 1)
        return (pl.BlockSpec((tm, tk), lambda i, k: (i, rel(k))), pl.BlockSpec((tk, d), lambda i, k: (off + rel(k), 0)))

    def body(*refs):
        a_refs, b_refs = refs[0:2 * nseg:2], refs[1:2 * nseg:2]
        x_ref, g_ref, dres_ref, dx_ref, dg_ref, acc = refs[2 * nseg:]
        i, k = pl.program_id(0), pl.program_id(1)

        @pl.when(k == 0)
        def _():
            acc[...] = jnp.zeros_like(acc)

        for s in range(nseg):
            lo, hi = int(starts[s]), int(starts[s + 1])

            @pl.when(jnp.logical_and(k >= lo, k < hi))
            def _(s=s):
                acc[...] += _dot(a_refs[s][...].astype(BF16), b_refs[s][...].astype(BF16), NN)

        @pl.when(jnp.logical_and(i == 0, k == 0))
        def _():
            dg_ref[...] = jnp.zeros_like(dg_ref)

        @pl.when(k == nk - 1)
        def _():
            dn = acc[...]
            xv = x_ref[...]
            r = lax.rsqrt(jnp.mean(xv * xv, axis=-1, keepdims=True) + RMS_EPS)
            xhat = xv * r
            dyg = dn * g_ref[...]
            dx_ref[...] = dres_ref[...] + r * (dyg - xhat * jnp.mean(dyg * xhat, axis=-1, keepdims=True))
            dg_ref[...] += jnp.sum(dn * xhat, axis=0, keepdims=True)

    in_specs, operands = [], []
    for s in range(nseg):
        sa, sb = specs(s)
        in_specs += [sa, sb]
        operands += [segs[s][0], segs[s][1]]
    row = pl.BlockSpec((tm, d), lambda i, k: (i, 0))
    in_specs += [row, pl.BlockSpec((1, d), lambda i, k: (0, 0)), row]
    return pl.pallas_call(
        body, out_shape=(jax.ShapeDtypeStruct((t, d), F32), jax.ShapeDtypeStruct((1, d), F32)), grid=(t // tm, nk),
        in_specs=in_specs, out_specs=(row, pl.BlockSpec((1, d), lambda i, k: (0, 0))),
        scratch_shapes=[pltpu.VMEM((tm, d), F32)],
        name=name, compiler_params=_params("arbitrary", "arbitrary"))(*operands, x, g, dres)


def _rms_fwd(x, g, *, name):
    t, d = x.shape
    tm = _tile(t, 512)

    def body(x_ref, g_ref, o_ref):
        xv = x_ref[...]
        r = lax.rsqrt(jnp.mean(xv * xv, axis=-1, keepdims=True) + RMS_EPS)
        o_ref[...] = (xv * r * g_ref[...]).astype(o_ref.dtype)

    row = pl.BlockSpec((tm, d), lambda i: (i, 0))
    return pl.pallas_call(body, out_shape=jax.ShapeDtypeStruct((t, d), BF16), grid=(t // tm,),
                          in_specs=[row, _const_spec((1, d))], out_specs=row, name=name, compiler_params=_params("parallel"))(x, g)


def _ffn_up(n, wgt, wut, *, name):
    t, d = n.shape
    f = wgt.shape[0]
    tm, tn = _tile(t, 512), _tile(f, 256, LANES)

    def body(n_ref, wg_ref, wu_ref, g_ref, u_ref, h_ref):
        nv = n_ref[...]
        gv = _dot(nv, wg_ref[...], NT)
        uv = _dot(nv, wu_ref[...], NT)
        g_ref[...] = gv.astype(BF16)
        u_ref[...] = uv.astype(BF16)
        h_ref[...] = (gv * _sigmoid(gv) * uv).astype(BF16)

    w_spec = pl.BlockSpec((tn, d), lambda i, j: (j, 0))
    o_spec = pl.BlockSpec((tm, tn), lambda i, j: (i, j))
    shp = jax.ShapeDtypeStruct((t, f), BF16)
    return pl.pallas_call(body, out_shape=(shp, shp, shp), grid=(t // tm, f // tn),
                          in_specs=[pl.BlockSpec((tm, d), lambda i, j: (i, 0)), w_spec, w_spec], out_specs=(o_spec, o_spec, o_spec),
                          name=name, compiler_params=_params("parallel", "arbitrary"))(n, wgt, wut)


def _rms_residual(acc, h, gp, weight):
    r = lax.rsqrt(jnp.mean(acc * acc, axis=-1, keepdims=True) + RMS_EPS)
    return h + weight * (acc * r * gp)


def _ffn_down(hid, wd, h_in, gp, *, name):
    t, f = hid.shape
    d = wd.shape[1]
    tm = _tile(t, 256)

    def body(hid_ref, wd_ref, hin_ref, gp_ref, f_ref, hout_ref):
        acc = _dot(hid_ref[...], wd_ref[...], NN)
        f_ref[...] = acc
        hout_ref[...] = _rms_residual(acc, hin_ref[...], gp_ref[...], FFN_RESIDUAL_WEIGHT)

    row = pl.BlockSpec((tm, d), lambda i: (i, 0))
    shp = jax.ShapeDtypeStruct((t, d), F32)
    return pl.pallas_call(body, out_shape=(shp, shp), grid=(t // tm,),
                          in_specs=[pl.BlockSpec((tm, f), lambda i: (i, 0)), _const_spec((f, d)), row, _const_spec((1, d))],
                          out_specs=(row, row), name=name, compiler_params=_params("parallel"))(hid, wd, h_in, gp)


def _post_bwd(dh, f, gp, weight, *, name):
    t, d = f.shape
    tm = _tile(t, 512)

    def body(dh_ref, f_ref, gp_ref, df_ref, dgp_ref):
        @pl.when(pl.program_id(0) == 0)
        def _():
            dgp_ref[...] = jnp.zeros_like(dgp_ref)
        fv = f_ref[...]
        dy = weight * dh_ref[...]
        r = lax.rsqrt(jnp.mean(fv * fv, axis=-1, keepdims=True) + RMS_EPS)
        fhat = fv * r
        dyg = dy * gp_ref[...]
        df_ref[...] = (r * (dyg - fhat * jnp.mean(dyg * fhat, axis=-1, keepdims=True))).astype(BF16)
        dgp_ref[...] += jnp.sum(dy * fhat, axis=0, keepdims=True)

    row = pl.BlockSpec((tm, d), lambda i: (i, 0))
    return pl.pallas_call(body, out_shape=(jax.ShapeDtypeStruct((t, d), BF16), jax.ShapeDtypeStruct((1, d), F32)), grid=(t // tm,),
                          in_specs=[row, row, _const_spec((1, d))], out_specs=(row, _const_spec((1, d))),
                          name=name, compiler_params=_params("arbitrary"))(dh, f, gp)


def _ffn_dhid(df, wd, g, u, *, name):
    t, d = df.shape
    f = wd.shape[0]
    tm, tn = _tile(t, 512), _tile(f, 256, LANES)

    def body(df_ref, wd_ref, g_ref, u_ref, dg_ref, du_ref):
        dh = _dot(df_ref[...], wd_ref[...], NT)
        gv = g_ref[...].astype(F32)
        uv = u_ref[...].astype(F32)
        sg = _sigmoid(gv)
        dg_ref[...] = (dh * uv * (sg * (1.0 + gv * (1.0 - sg)))).astype(BF16)
        du_ref[...] = (dh * (gv * sg)).astype(BF16)

    o_spec = pl.BlockSpec((tm, tn), lambda i, j: (i, j))
    shp = jax.ShapeDtypeStruct((t, f), BF16)
    return pl.pallas_call(body, out_shape=(shp, shp), grid=(t // tm, f // tn),
                          in_specs=[pl.BlockSpec((tm, d), lambda i, j: (i, 0)), pl.BlockSpec((tn, d), lambda i, j: (j, 0)), o_spec, o_spec],
                          out_specs=(o_spec, o_spec), name=name, compiler_params=_params("parallel", "arbitrary"))(df, wd, g, u)


def _ffn_forward(h_in, g_pre, wgt, wut, wd, g_post, tag):
    n = _rms_fwd(h_in, g_pre, name=f"{tag}_prenorm")
    g, u, hid = _ffn_up(n, wgt, wut, name=f"{tag}_up")
    f, h_out = _ffn_down(hid, wd, h_in, g_post, name=f"{tag}_down")
    return h_out, (h_in, n, g, u, hid, f)


def _ffn_backward(dh_out, saved, g_pre, wgt, wut, wd, g_post, tag):
    h_in, n, g, u, hid, f = saved
    df, dg_post = _post_bwd(dh_out, f, g_post, FFN_RESIDUAL_WEIGHT, name=f"{tag}_post_bwd")
    dgate, dup = _ffn_dhid(df, wd, g, u, name=f"{tag}_dhid")
    d_wd = _mm_tn([hid], df, name=f"{tag}_dwd")
    d_wgt = _mm_tn([dgate], n, name=f"{tag}_dwg")
    d_wut = _mm_tn([dup], n, name=f"{tag}_dwu")
    tk = _tile(wgt.shape[0], 512, LANES)
    dh_in, dg_pre = _mm_nn_rmsbwd([(dgate, wgt, tk, 0), (dup, wut, tk, 0)], h_in, g_pre, dh_out, name=f"{tag}_dn")
    return dh_in, dg_pre, d_wgt, d_wut, d_wd, dg_post


def _shift_down(x, k, rows):
    return jnp.where(rows >= k, pltpu.roll(x, k, axis=0), 0.0)


def _shift_up(x, k, rows, n):
    return jnp.where(rows < n - k, pltpu.roll(x, n - k, axis=0), 0.0)


def _conv_pre(xv, w_ref, b_ref, rows):
    pre = xv * w_ref[SSM_CONV - 1:SSM_CONV, :] + b_ref[...]
    for k in range(1, SSM_CONV):
        pre = pre + _shift_down(xv, k, rows) * w_ref[SSM_CONV - 1 - k:SSM_CONV - k, :]
    return pre


def _conv_fwd(xbc, conv_w, conv_b, bl, *, name):
    t, c = xbc.shape
    s = t // bl
    tc = LANES

    def body(x_ref, w_ref, b_ref, o_ref):
        xv = x_ref[...].astype(F32)
        rows = lax.broadcasted_iota(jnp.int32, xv.shape, 0)
        pre = _conv_pre(xv, w_ref, b_ref, rows)
        o_ref[...] = (pre * _sigmoid(pre)).astype(o_ref.dtype)

    blk = pl.BlockSpec((s, tc), lambda b, j: (b, j))
    return pl.pallas_call(body, out_shape=jax.ShapeDtypeStruct((t, c), BF16), grid=(bl, c // tc),
                          in_specs=[blk, pl.BlockSpec((SSM_CONV, tc), lambda b, j: (0, j)), pl.BlockSpec((1, tc), lambda b, j: (0, j))],
                          out_specs=blk, name=name, compiler_params=_params("parallel", "arbitrary"))(xbc, conv_w, conv_b)


def _conv_bwd(dxc, xbc, conv_w, conv_b, bl, *, name):
    t, c = xbc.shape
    s = t // bl
    tc = LANES

    def body(dy_ref, x_ref, w_ref, b_ref, dx_ref, dw_ref, db_ref):
        @pl.when(pl.program_id(1) == 0)
        def _():
            dw_ref[...] = jnp.zeros_like(dw_ref)
            db_ref[...] = jnp.zeros_like(db_ref)
        xv = x_ref[...].astype(F32)
        rows = lax.broadcasted_iota(jnp.int32, xv.shape, 0)
        pre = _conv_pre(xv, w_ref, b_ref, rows)
        sg = _sigmoid(pre)
        dpre = dy_ref[...].astype(F32) * (sg * (1.0 + pre * (1.0 - sg)))
        dx = dpre * w_ref[SSM_CONV - 1:SSM_CONV, :]
        db_ref[...] += jnp.sum(dpre, axis=0, keepdims=True)
        dw_ref[SSM_CONV - 1:SSM_CONV, :] += jnp.sum(dpre * xv, axis=0, keepdims=True)
        for k in range(1, SSM_CONV):
            dx = dx + _shift_up(dpre, k, rows, s) * w_ref[SSM_CONV - 1 - k:SSM_CONV - k, :]
            dw_ref[SSM_CONV - 1 - k:SSM_CONV - k, :] += jnp.sum(dpre * _shift_down(xv, k, rows), axis=0, keepdims=True)
        dx_ref[...] = dx.astype(dx_ref.dtype)

    blk = pl.BlockSpec((s, tc), lambda j, b: (b, j))
    return pl.pallas_call(
        body, out_shape=(jax.ShapeDtypeStruct((t, c), BF16), jax.ShapeDtypeStruct((8, c), F32), jax.ShapeDtypeStruct((1, c), F32)),
        grid=(c // tc, bl),
        in_specs=[blk, blk, pl.BlockSpec((SSM_CONV, tc), lambda j, b: (0, j)), pl.BlockSpec((1, tc), lambda j, b: (0, j))],
        out_specs=(blk, pl.BlockSpec((8, tc), lambda j, b: (0, j)), pl.BlockSpec((1, tc), lambda j, b: (0, j))),
        name=name, compiler_params=_params("parallel", "arbitrary"))(dxc, xbc, conv_w, conv_b)


def _softplus(x):
    return jnp.maximum(x, 0.0) + jnp.log1p(jnp.exp(-jnp.abs(x)))


def _ssd_chunk_common(dtraw_ref, dtb_ref, alog_ref):
    q = CHUNK
    dt = _softplus(dtraw_ref[...] + dtb_ref[...])
    a = -jnp.exp(alog_ref[...])
    ii = lax.broadcasted_iota(jnp.int32, (q, q), 0)
    jj = lax.broadcasted_iota(jnp.int32, (q, q), 1)
    causal = ii >= jj
    tril = jnp.where(causal, 1.0, 0.0).astype(F32)
    triu = jnp.where(ii <= jj, 1.0, 0.0).astype(F32)
    a_cs = _dot_hi(tril, dt * a)
    a_cs_t = a_cs.T
    return dt, a, a_cs, a_cs_t, causal, triu


def _lane_onehot(h):
    return (lax.broadcasted_iota(jnp.int32, (1, LANES), 1) == h).astype(F32)


def _ssd_fwd(xc, dt_raw, dt_bias, a_log, d_skip, bl, n_heads, *, name):
    t = xc.shape[0]
    q, p, nst, grp = CHUNK, SSM_HEAD_DIM, SSM_STATE, SSM_GROUPS
    d_inner = n_heads * p
    hpg = n_heads // grp
    nc = t // bl // q
    assert d_inner % (grp * nst) == 0 and nst == LANES

    def body(xs_ref, b_ref, c_ref, dtraw_ref, dtb_ref, alog_ref, dsk_ref, y_ref, hprev_ref, state):
        @pl.when(pl.program_id(1) == 0)
        def _():
            state[...] = jnp.zeros_like(state)

        dt, a, a_cs, a_cs_t, causal, _ = _ssd_chunk_common(dtraw_ref, dtb_ref, alog_ref)
        for g in range(grp):
            bg = b_ref[:, g * nst:(g + 1) * nst]
            cg = c_ref[:, g * nst:(g + 1) * nst]
            scores = _dot(cg, bg, NT)
            for hh in range(hpg):
                h = g * hpg + hh
                col = a_cs[:, h:h + 1]
                row = a_cs_t[h:h + 1, :]
                decay = jnp.exp(jnp.where(causal, col - row, NEG_INF))
                xs = xs_ref[:, h * p:(h + 1) * p].astype(F32)
                xdt = xs * dt[:, h:h + 1]
                y = _dot((scores * decay).astype(BF16), xdt.astype(BF16), NN)
                a_last = a_cs[q - 1:q, h:h + 1]
                hprev = state[h]
                hprev_ref[h] = hprev
                y = y + jnp.exp(col) * _dot(cg, hprev.astype(BF16), NT)
                y_ref[:, h * p:(h + 1) * p] = y + dsk_ref[:, h:h + 1] * xs
                st = _dot((xdt * jnp.exp(a_last - col)).astype(BF16), bg, TN)
                state[h] = jnp.exp(a_last) * hprev + st

    gn = grp * nst

    def rowmap(b, c):
        return b * nc + c
    vec = pl.BlockSpec((1, LANES), lambda b, c: (0, 0))
    return pl.pallas_call(
        body,
        out_shape=(jax.ShapeDtypeStruct((t, d_inner), F32), jax.ShapeDtypeStruct((t // q, n_heads, p, nst), F32)),
        grid=(bl, nc),
        in_specs=[pl.BlockSpec((q, d_inner), lambda b, c: (rowmap(b, c), 0)),
                  pl.BlockSpec((q, gn), lambda b, c: (rowmap(b, c), d_inner // gn)),
                  pl.BlockSpec((q, gn), lambda b, c: (rowmap(b, c), d_inner // gn + 1)),
                  pl.BlockSpec((q, LANES), lambda b, c: (rowmap(b, c), 0)), vec, vec, vec],
        out_specs=(pl.BlockSpec((q, d_inner), lambda b, c: (rowmap(b, c), 0)),
                   pl.BlockSpec((None, n_heads, p, nst), lambda b, c: (rowmap(b, c), 0, 0, 0))),
        scratch_shapes=[pltpu.VMEM((n_heads, p, nst), F32)],
        name=name, compiler_params=_params("parallel", "arbitrary"))(xc, xc, xc, dt_raw, dt_bias, a_log, d_skip)


def _ssd_bwd(dy, xc, dt_raw, hprev_all, dt_bias, a_log, d_skip, bl, n_heads, *, name):
    t, c_dim = xc.shape
    q, p, nst, grp = CHUNK, SSM_HEAD_DIM, SSM_STATE, SSM_GROUPS
    d_inner = n_heads * p
    hpg = n_heads // grp
    nc = t // bl // q
    gn = grp * nst

    def body(dy_ref, xs_ref, b_ref, c_ref, dtraw_ref, hprev_ref, dtb_ref, alog_ref, dsk_ref,
             dxc_ref, ddtraw_ref, ddtb_ref, dalog_ref, ddsk_ref, dstate):
        @pl.when(jnp.logical_and(pl.program_id(0) == 0, pl.program_id(1) == 0))
        def _():
            ddtb_ref[...] = jnp.zeros_like(ddtb_ref)
            dalog_ref[...] = jnp.zeros_like(dalog_ref)
            ddsk_ref[...] = jnp.zeros_like(ddsk_ref)

        @pl.when(pl.program_id(1) == 0)
        def _():
            dstate[...] = jnp.zeros_like(dstate)

        dt, a, a_cs, a_cs_t, causal, triu = _ssd_chunk_common(dtraw_ref, dtb_ref, alog_ref)
        last_row = (lax.broadcasted_iota(jnp.int32, (q, 1), 0) == q - 1).astype(F32)
        da_col = jnp.zeros((q, LANES), F32)
        da_row = jnp.zeros((LANES, q), F32)
        ddt = jnp.zeros((q, LANES), F32)
        ddsk = jnp.zeros((q, LANES), F32)
        for g in range(grp):
            bg = b_ref[:, g * nst:(g + 1) * nst]
            cg = c_ref[:, g * nst:(g + 1) * nst]
            scores = _dot(cg, bg, NT)
            dscores = jnp.zeros((q, q), F32)
            dbg = jnp.zeros((q, nst), F32)
            dcg = jnp.zeros((q, nst), F32)
            for hh in range(hpg):
                h = g * hpg + hh
                onehot = _lane_onehot(h)
                sub_onehot = (lax.broadcasted_iota(jnp.int32, (LANES, 1), 0) == h).astype(F32)
                col = a_cs[:, h:h + 1]
                row = a_cs_t[h:h + 1, :]
                decay = jnp.exp(jnp.where(causal, col - row, NEG_INF))
                m = scores * decay
                m_b = m.astype(BF16)
                xs = xs_ref[:, h * p:(h + 1) * p].astype(F32)
                dt_col = dt[:, h:h + 1]
                xdt = xs * dt_col
                xdt_b = xdt.astype(BF16)
                dyh_b = dy_ref[:, h * p:(h + 1) * p]
                dyh = dyh_b.astype(F32)
                a_last = a_cs[q - 1:q, h:h + 1]
                e_col = jnp.exp(col)
                e_last = jnp.exp(a_last)
                dec_end = jnp.exp(a_last - col)
                hprev = hprev_ref[h]
                hprev_b = hprev.astype(BF16)
                dhn = dstate[h]
                dhn_b = dhn.astype(BF16)
                y_off = e_col * _dot(cg, hprev_b, NT)
                da_c = jnp.sum(dyh * y_off, axis=1, keepdims=True)
                dye_b = (dyh * e_col).astype(BF16)
                dcg = dcg + _dot(dye_b, hprev_b, NN)
                dhp = e_last * dhn + _dot(dye_b, cg, TN)
                bdh = _dot(bg, dhn_b, NT)
                dx = dec_end * bdh
                t1 = jnp.sum(xdt * bdh, axis=1, keepdims=True)
                da_c = da_c - dec_end * t1
                da_last = jnp.sum(dec_end * t1, axis=0, keepdims=True) + e_last * jnp.sum(dhn * hprev, keepdims=True)
                dbg = dbg + _dot((xdt * dec_end).astype(BF16), dhn_b, NN)
                dm = _dot(dyh_b, xdt_b, NT)
                dx = dx + _dot(m_b, dyh_b, TN)
                w = dm * m
                da_c = da_c + jnp.sum(w, axis=1, keepdims=True) + last_row * da_last
                da_col = da_col + da_c * onehot
                da_row = da_row + sub_onehot * jnp.sum(w, axis=0, keepdims=True)
                dscores = dscores + dm * decay
                dsk = dsk_ref[:, h:h + 1]
                dxc_ref[:, h * p:(h + 1) * p] = (dx * dt_col + dsk * dyh).astype(dxc_ref.dtype)
                ddt = ddt + jnp.sum(dx * xs, axis=1, keepdims=True) * onehot
                ddsk = ddsk + jnp.sum(dyh * xs, axis=1, keepdims=True) * onehot
                dstate[h] = dhp
            ds_b = dscores.astype(BF16)
            dcg = dcg + _dot(ds_b, bg, NN)
            dbg = dbg + _dot(ds_b, cg, TN)
            dxc_ref[:, d_inner + g * nst:d_inner + (g + 1) * nst] = dbg.astype(dxc_ref.dtype)
            dxc_ref[:, d_inner + gn + g * nst:d_inner + gn + (g + 1) * nst] = dcg.astype(dxc_ref.dtype)
        da = da_col - da_row.T
        ddta = _dot_hi(triu, da)
        ddt = ddt + ddta * a
        d_a = jnp.sum(ddta * dt, axis=0, keepdims=True)
        ddt_raw = ddt * _sigmoid(dtraw_ref[...] + dtb_ref[...])
        ddtraw_ref[...] = ddt_raw
        ddtb_ref[...] += jnp.sum(ddt_raw, axis=0, keepdims=True)
        dalog_ref[...] += d_a * a
        ddsk_ref[...] += jnp.sum(ddsk, axis=0, keepdims=True)

    def rowmap(b, c):
        return b * nc + (nc - 1 - c)
    vec = pl.BlockSpec((1, LANES), lambda b, c: (0, 0))
    vec_shape = jax.ShapeDtypeStruct((1, LANES), F32)
    return pl.pallas_call(
        body,
        out_shape=(jax.ShapeDtypeStruct((t, c_dim), BF16), jax.ShapeDtypeStruct((t, LANES), F32), vec_shape, vec_shape, vec_shape),
        grid=(bl, nc),
        in_specs=[pl.BlockSpec((q, d_inner), lambda b, c: (rowmap(b, c), 0)),
                  pl.BlockSpec((q, d_inner), lambda b, c: (rowmap(b, c), 0)),
                  pl.BlockSpec((q, gn), lambda b, c: (rowmap(b, c), d_inner // gn)),
                  pl.BlockSpec((q, gn), lambda b, c: (rowmap(b, c), d_inner // gn + 1)),
                  pl.BlockSpec((q, LANES), lambda b, c: (rowmap(b, c), 0)),
                  pl.BlockSpec((None, n_heads, p, nst), lambda b, c: (rowmap(b, c), 0, 0, 0)), vec, vec, vec],
        out_specs=(pl.BlockSpec((q, c_dim), lambda b, c: (rowmap(b, c), 0)),
                   pl.BlockSpec((q, LANES), lambda b, c: (rowmap(b, c), 0)), vec, vec, vec),
        scratch_shapes=[pltpu.VMEM((n_heads, p, nst), F32)],
        name=name, compiler_params=_params("arbitrary", "arbitrary"))(dy, xc, xc, xc, dt_raw, hprev_all, dt_bias, a_log, d_skip)


def _gated_norm_fwd(y, z, ng, *, name):
    t, d = y.shape
    tm = _tile(t, 256)
    gw = d // SSM_GROUPS

    def body(y_ref, z_ref, ng_ref, o_ref):
        for g in range(SSM_GROUPS):
            sl = slice(g * gw, (g + 1) * gw)
            zv = z_ref[:, sl].astype(F32)
            yg = y_ref[:, sl] * (zv * _sigmoid(zv))
            r = lax.rsqrt(jnp.mean(yg * yg, axis=-1, keepdims=True) + RMS_EPS)
            o_ref[:, sl] = (yg * r * ng_ref[:, sl]).astype(o_ref.dtype)

    row = pl.BlockSpec((tm, d), lambda i: (i, 0))
    return pl.pallas_call(body, out_shape=jax.ShapeDtypeStruct((t, d), BF16), grid=(t // tm,),
                          in_specs=[row, row, _const_spec((1, d))], out_specs=row, name=name, compiler_params=_params("parallel"))(y, z, ng)


def _gated_norm_bwd(dyn, y, z, ng, *, name):
    t, d = y.shape
    tm = _tile(t, 256)
    gw = d // SSM_GROUPS

    def body(dyn_ref, y_ref, z_ref, ng_ref, dy_ref, dz_ref, dng_ref):
        @pl.when(pl.program_id(0) == 0)
        def _():
            dng_ref[...] = jnp.zeros_like(dng_ref)
        for g in range(SSM_GROUPS):
            sl = slice(g * gw, (g + 1) * gw)
            zv = z_ref[:, sl].astype(F32)
            yv = y_ref[:, sl]
            sg = _sigmoid(zv)
            sz = zv * sg
            yg = yv * sz
            r = lax.rsqrt(jnp.mean(yg * yg, axis=-1, keepdims=True) + RMS_EPS)
            yhat = yg * r
            dn = dyn_ref[:, sl].astype(F32)
            dyg_n = dn * ng_ref[:, sl]
            dyg = r * (dyg_n - yhat * jnp.mean(dyg_n * yhat, axis=-1, keepdims=True))
            dy_ref[:, sl] = (dyg * sz).astype(dy_ref.dtype)
            dz_ref[:, sl] = (dyg * yv * (sg * (1.0 + zv * (1.0 - sg)))).astype(dz_ref.dtype)
            dng_ref[:, sl] += jnp.sum(dn * yhat, axis=0, keepdims=True)

    row = pl.BlockSpec((tm, d), lambda i: (i, 0))
    shp = jax.ShapeDtypeStruct((t, d), BF16)
    return pl.pallas_call(body, out_shape=(shp, shp, jax.ShapeDtypeStruct((1, d), F32)), grid=(t // tm,),
                          in_specs=[row, row, row, _const_spec((1, d))], out_specs=(row, row, _const_spec((1, d))),
                          name=name, compiler_params=_params("arbitrary"))(dyn, y, z, ng)


def _bucket_onehot():
    blk = CHUNK
    qi = jnp.arange(blk)[:, None]
    kj = jnp.arange(2 * blk)[None, :]
    dist = jnp.maximum(qi + blk - kj, 0)
    max_exact = REL_BUCKETS // 2
    d = jnp.maximum(dist, 1).astype(F32)
    large = max_exact + (jnp.log(d / max_exact) / math.log(REL_MAX_DISTANCE / max_exact) * (REL_BUCKETS - max_exact)).astype(jnp.int32)
    large = jnp.minimum(large, REL_BUCKETS - 1)
    bucket = jnp.where(dist < max_exact, dist, large).reshape(-1)
    return (bucket[None, :] == jnp.arange(REL_BUCKETS)[:, None]).astype(F32)


def _small_mm_hi(a, b, dims, *, name):
    def body(a_ref, b_ref, o_ref):
        o_ref[...] = _dot_hi(a_ref[...], b_ref[...], dims)
    n = b.shape[0] if dims == NT else b.shape[1]
    return pl.pallas_call(body, out_shape=jax.ShapeDtypeStruct((a.shape[0], n), F32), name=name)(a, b)


def _attn_masks():
    ii = lax.broadcasted_iota(jnp.int32, (CHUNK, CHUNK), 0)
    jj = lax.broadcasted_iota(jnp.int32, (CHUNK, CHUNK), 1)
    return jj > ii, jj <= ii


def _attn_fwd(q, kv, bias, sinks, bl, *, name):
    t, qd = q.shape
    blk, hd = CHUNK, ATTN_HEAD_DIM
    kvd = ATTN_KV_HEADS * hd
    rep = ATTN_Q_HEADS // ATTN_KV_HEADS
    nb = t // bl // blk
    scale = hd ** -0.5

    def body(q_ref, kp_ref, kc_ref, vp_ref, vc_ref, bias_ref, sink_ref, o_ref, lse_ref):
        n = pl.program_id(1)
        mask_p, mask_c = _attn_masks()
        mask_p = jnp.logical_and(mask_p, n > 0)
        lse = jnp.zeros((blk, LANES), F32)
        for h in range(ATTN_Q_HEADS):
            kvh = h // rep
            ks = slice(kvh * hd, (kvh + 1) * hd)
            qh = q_ref[:, h * hd:(h + 1) * hd]
            s_p = jnp.where(mask_p, _dot(qh, kp_ref[:, ks], NT) * scale + bias_ref[h, :, :blk], NEG_INF)
            s_c = jnp.where(mask_c, _dot(qh, kc_ref[:, ks], NT) * scale + bias_ref[h, :, blk:], NEG_INF)
            sink = sink_ref[:, h:h + 1]
            m = jnp.maximum(jnp.maximum(jnp.max(s_p, axis=1, keepdims=True), jnp.max(s_c, axis=1, keepdims=True)), sink)
            p_p = jnp.exp(s_p - m)
            p_c = jnp.exp(s_c - m)
            den = jnp.sum(p_p, axis=1, keepdims=True) + jnp.sum(p_c, axis=1, keepdims=True) + jnp.exp(sink - m)
            inv = 1.0 / den
            o = _dot((p_p * inv).astype(BF16), vp_ref[:, ks], NN) + _dot((p_c * inv).astype(BF16), vc_ref[:, ks], NN)
            o_ref[:, h * hd:(h + 1) * hd] = o.astype(o_ref.dtype)
            lse = lse + (m + jnp.log(den)) * _lane_onehot(h)
        lse_ref[...] = lse

    def cur(b, n):
        return b * nb + n

    def prev(b, n):
        return b * nb + jnp.maximum(n - 1, 0)
    return pl.pallas_call(
        body, out_shape=(jax.ShapeDtypeStruct((t, qd), BF16), jax.ShapeDtypeStruct((t, LANES), F32)), grid=(bl, nb),
        in_specs=[pl.BlockSpec((blk, qd), lambda b, n: (cur(b, n), 0)),
                  pl.BlockSpec((blk, kvd), lambda b, n: (prev(b, n), 0)), pl.BlockSpec((blk, kvd), lambda b, n: (cur(b, n), 0)),
                  pl.BlockSpec((blk, kvd), lambda b, n: (prev(b, n), 1)), pl.BlockSpec((blk, kvd), lambda b, n: (cur(b, n), 1)),
                  _const_spec(bias.shape), _const_spec((1, LANES))],
        out_specs=(pl.BlockSpec((blk, qd), lambda b, n: (cur(b, n), 0)), pl.BlockSpec((blk, LANES), lambda b, n: (cur(b, n), 0))),
        name=name, compiler_params=_params("parallel", "arbitrary"))(q, kv, kv, kv, kv, bias, sinks)


def _attn_bwd(do, q, kv, lse, bias, sinks, bl, *, name):
    t, qd = q.shape
    blk, hd = CHUNK, ATTN_HEAD_DIM
    kvd = ATTN_KV_HEADS * hd
    rep = ATTN_Q_HEADS // ATTN_KV_HEADS
    s_len = t // bl
    nb = s_len // blk
    scale = hd ** -0.5

    def body(do_ref, q_ref, kp_ref, kc_ref, vp_ref, vc_ref, lse_ref, bias_ref, sink_ref, dq_ref, dkv_ref, dbias_ref, dsink_ref):
        n = pl.program_id(1)

        @pl.when(jnp.logical_and(pl.program_id(0) == 0, n == 0))
        def _():
            dbias_ref[...] = jnp.zeros_like(dbias_ref)
            dsink_ref[...] = jnp.zeros_like(dsink_ref)

        mask_p, mask_c = _attn_masks()
        mask_p = jnp.logical_and(mask_p, n > 0)
        r_cur = pl.multiple_of(n * blk, blk)
        r_prev = pl.multiple_of(jnp.maximum(n - 1, 0) * blk, blk)
        dsink = jnp.zeros((1, LANES), F32)
        for kvh in range(ATTN_KV_HEADS):
            ks = slice(kvh * hd, (kvh + 1) * hd)
            kp, kc, vp, vc = kp_ref[:, ks], kc_ref[:, ks], vp_ref[:, ks], vc_ref[:, ks]
            dk_p = jnp.zeros((blk, hd), F32)
            dk_c = jnp.zeros((blk, hd), F32)
            dv_p = jnp.zeros((blk, hd), F32)
            dv_c = jnp.zeros((blk, hd), F32)
            for hh in range(rep):
                h = kvh * rep + hh
                hs = slice(h * hd, (h + 1) * hd)
                qh = q_ref[:, hs]
                doh = do_ref[:, hs]
                lse = lse_ref[:, h:h + 1]
                p_p = jnp.exp(jnp.where(mask_p, _dot(qh, kp, NT) * scale + bias_ref[h, :, :blk], NEG_INF) - lse)
                p_c = jnp.exp(jnp.where(mask_c, _dot(qh, kc, NT) * scale + bias_ref[h, :, blk:], NEG_INF) - lse)
                dp_p = _dot(doh, vp, NT)
                dp_c = _dot(doh, vc, NT)
                delta = jnp.sum(p_p * dp_p, axis=1, keepdims=True) + jnp.sum(p_c * dp_c, axis=1, keepdims=True)
                ds_p = p_p * (dp_p - delta)
                ds_c = p_c * (dp_c - delta)
                p_sink = jnp.exp(sink_ref[:, h:h + 1] - lse)
                dsink = dsink - jnp.sum(p_sink * delta, axis=0, keepdims=True) * _lane_onehot(h)
                dbias_ref[h, :, :blk] += ds_p
                dbias_ref[h, :, blk:] += ds_c
                ds_pb, ds_cb = ds_p.astype(BF16), ds_c.astype(BF16)
                dq_ref[:, hs] = ((_dot(ds_pb, kp, NN) + _dot(ds_cb, kc, NN)) * scale).astype(dq_ref.dtype)
                dk_p = dk_p + _dot(ds_pb, qh, TN)
                dk_c = dk_c + _dot(ds_cb, qh, TN)
                dv_p = dv_p + _dot(p_p.astype(BF16), doh, TN)
                dv_c = dv_c + _dot(p_c.astype(BF16), doh, TN)
            vs = slice(kvd + kvh * hd, kvd + (kvh + 1) * hd)
            dkv_ref[pl.ds(r_cur, blk), ks] = dk_c * scale
            dkv_ref[pl.ds(r_cur, blk), vs] = dv_c

            @pl.when(n > 0)
            def _():
                dkv_ref[pl.ds(r_prev, blk), ks] += dk_p * scale
                dkv_ref[pl.ds(r_prev, blk), vs] += dv_p
        dsink_ref[...] += dsink

    def cur(b, n):
        return b * nb + n

    def prev(b, n):
        return b * nb + jnp.maximum(n - 1, 0)
    qspec = pl.BlockSpec((blk, qd), lambda b, n: (cur(b, n), 0))
    return pl.pallas_call(
        body,
        out_shape=(jax.ShapeDtypeStruct((t, qd), BF16), jax.ShapeDtypeStruct((t, 2 * kvd), F32),
                   jax.ShapeDtypeStruct(bias.shape, F32), jax.ShapeDtypeStruct((1, LANES), F32)),
        grid=(bl, nb),
        in_specs=[qspec, qspec,
                  pl.BlockSpec((blk, kvd), lambda b, n: (prev(b, n), 0)), pl.BlockSpec((blk, kvd), lambda b, n: (cur(b, n), 0)),
                  pl.BlockSpec((blk, kvd), lambda b, n: (prev(b, n), 1)), pl.BlockSpec((blk, kvd), lambda b, n: (cur(b, n), 1)),
                  pl.BlockSpec((blk, LANES), lambda b, n: (cur(b, n), 0)), _const_spec(bias.shape), _const_spec((1, LANES))],
        out_specs=(qspec, pl.BlockSpec((s_len, 2 * kvd), lambda b, n: (b, 0)), _const_spec(bias.shape), _const_spec((1, LANES))),
        name=name, compiler_params=_params("arbitrary", "arbitrary"))(do, q, kv, kv, kv, kv, lse, bias, sinks)


def _merge_fwd(yn, o, gs, ga, w_ssm, w_attn, w_out, h_in, g_post, *, name):
    t, d = h_in.shape
    tm = _tile(t, 256)

    def body(yn_ref, o_ref, gs_ref, ga_ref, ws_ref, wa_ref, wo_ref, hin_ref, gp_ref, ys_ref, ya_ref, mg_ref, mix_ref, hout_ref):
        ys = _dot(yn_ref[...], ws_ref[...], NN)
        ya = _dot(o_ref[...], wa_ref[...], NN)
        merged = (_sigmoid(gs_ref[...].astype(F32)) * ys + _sigmoid(ga_ref[...].astype(F32)) * ya).astype(BF16)
        mix = _dot(merged, wo_ref[...], NN)
        ys_ref[...] = ys.astype(BF16)
        ya_ref[...] = ya.astype(BF16)
        mg_ref[...] = merged
        mix_ref[...] = mix
        hout_ref[...] = _rms_residual(mix, hin_ref[...], gp_ref[...], 1.0)

    def row(w):
        return pl.BlockSpec((tm, w), lambda i: (i, 0))
    bshape = jax.ShapeDtypeStruct((t, d), BF16)
    fshape = jax.ShapeDtypeStruct((t, d), F32)
    return pl.pallas_call(
        body, out_shape=(bshape, bshape, bshape, fshape, fshape), grid=(t // tm,),
        in_specs=[row(yn.shape[1]), row(o.shape[1]), row(d), row(d), _const_spec(w_ssm.shape), _const_spec(w_attn.shape),
                  _const_spec(w_out.shape), row(d), _const_spec((1, d))],
        out_specs=(row(d),) * 5, name=name, compiler_params=_params("parallel"))(yn, o, gs, ga, w_ssm, w_attn, w_out, h_in, g_post)


def _merge_bwd(dh, mix, g_post, gs, ga, ys, ya, w_ssm, w_attn, w_out, *, name):
    t, d = mix.shape
    tm = _tile(t, 256)
    d_ssm, d_attn = w_ssm.shape[0], w_attn.shape[0]

    def body(dh_ref, mix_ref, gp_ref, gs_ref, ga_ref, ys_ref, ya_ref, ws_ref, wa_ref, wo_ref,
             dmix_ref, dys_ref, dya_ref, dgs_ref, dga_ref, dyn_ref, do_ref, dgp_ref):
        @pl.when(pl.program_id(0) == 0)
        def _():
            dgp_ref[...] = jnp.zeros_like(dgp_ref)
        mv = mix_ref[...]
        dy = dh_ref[...]
        r = lax.rsqrt(jnp.mean(mv * mv, axis=-1, keepdims=True) + RMS_EPS)
        mhat = mv * r
        dyg = dy * gp_ref[...]
        dmix = (r * (dyg - mhat * jnp.mean(dyg * mhat, axis=-1, keepdims=True))).astype(BF16)
        dgp_ref[...] += jnp.sum(dy * mhat, axis=0, keepdims=True)
        dmix_ref[...] = dmix
        dmerged = _dot(dmix, wo_ref[...], NT)
        sgs = _sigmoid(gs_ref[...].astype(F32))
        sga = _sigmoid(ga_ref[...].astype(F32))
        dys = (dmerged * sgs).astype(BF16)
        dya = (dmerged * sga).astype(BF16)
        dys_ref[...] = dys
        dya_ref[...] = dya
        dgs_ref[...] = (dmerged * ys_ref[...].astype(F32) * sgs * (1.0 - sgs)).astype(BF16)
        dga_ref[...] = (dmerged * ya_ref[...].astype(F32) * sga * (1.0 - sga)).astype(BF16)
        dyn_ref[...] = _dot(dys, ws_ref[...], NT).astype(BF16)
        do_ref[...] = _dot(dya, wa_ref[...], NT).astype(BF16)

    def row(w):
        return pl.BlockSpec((tm, w), lambda i: (i, 0))

    def bshape(w):
        return jax.ShapeDtypeStruct((t, w), BF16)
    return pl.pallas_call(
        body, out_shape=(bshape(d),) * 5 + (bshape(d_ssm), bshape(d_attn), jax.ShapeDtypeStruct((1, d), F32)), grid=(t // tm,),
        in_specs=[row(d), row(d), _const_spec((1, d)), row(d), row(d), row(d), row(d),
                  _const_spec(w_ssm.shape), _const_spec(w_attn.shape), _const_spec(w_out.shape)],
        out_specs=(row(d),) * 5 + (row(d_ssm), row(d_attn), _const_spec((1, d))),
        name=name, compiler_params=_params("arbitrary"))(dh, mix, g_post, gs, ga, ys, ya, w_ssm, w_attn, w_out)


def _loss_grad(h, target, *, name):
    t, d = h.shape
    tm = _tile(t, 512)

    def body(h_ref, t_ref, dh_ref, loss_ref):
        @pl.when(pl.program_id(0) == 0)
        def _():
            loss_ref[...] = jnp.zeros_like(loss_ref)
        e = h_ref[...] - t_ref[...]
        dh_ref[...] = e * (1.0 / d)
        per_row = jnp.sum(e * e, axis=1, keepdims=True) * (1.0 / d)
        loss_ref[...] += 0.5 * jnp.sum(per_row, axis=0, keepdims=True)

    row = pl.BlockSpec((tm, d), lambda i: (i, 0))
    return pl.pallas_call(body, out_shape=(jax.ShapeDtypeStruct((t, d), F32), jax.ShapeDtypeStruct((1, LANES), F32)), grid=(t // tm,),
                          in_specs=[row, row], out_specs=(row, _const_spec((1, LANES))),
                          name=name, compiler_params=_params("arbitrary"))(h, target)


def _adamw(w, g, m, v, *, name):
    r, c = w.shape
    tm = _tile(r, 256)
    c1 = 1.0 - ADAM_B1 ** ADAM_STEP
    c2 = 1.0 - ADAM_B2 ** ADAM_STEP

    def body(w_ref, g_ref, m_ref, v_ref, d_ref, mo_ref, vo_ref):
        gv = g_ref[...]
        mn = ADAM_B1 * m_ref[...] + (1.0 - ADAM_B1) * gv
        vn = ADAM_B2 * v_ref[...] + (1.0 - ADAM_B2) * (gv * gv)
        mo_ref[...] = mn
        vo_ref[...] = vn
        d_ref[...] = -ADAM_LR * ((mn / c1) / (jnp.sqrt(vn / c2) + ADAM_EPS) + ADAM_WD * w_ref[...])

    blk = pl.BlockSpec((tm, c), lambda i: (i, 0))
    shp = jax.ShapeDtypeStruct((r, c), F32)
    return pl.pallas_call(body, out_shape=(shp, shp, shp), grid=(r // tm,), in_specs=[blk] * 4, out_specs=(blk,) * 3,
                          name=name, compiler_params=_params("parallel"))(w, g, m, v)


def _position():
    return lax.axis_index("x"), lax.axis_index("y"), lax.axis_index("c")


def _all_gather(shards, *, name):
    na = len(shards)

    def body(*refs):
        ins, outs = refs[:na], refs[na:2 * na]
        send_sems, recv_sems, local_sems = refs[2 * na:]
        x, y, c = _position()
        me, sibling = (x, y, c), (x, y, 1 - c)
        chips = [(1 - x, y), (x, 1 - y), (1 - x, 1 - y)]

        def slot(a, pos):
            return outs[a].at[4 * pos[0] + 2 * pos[1] + pos[2]]

        def copy(a, k, block, to, src=None):
            return pltpu.make_async_remote_copy(
                src_ref=slot(a, block) if src is None else src, dst_ref=slot(a, block),
                send_sem=send_sems.at[a, k], recv_sem=recv_sems.at[a, k], device_id=to, device_id_type=MESH)

        mine = [pltpu.make_async_copy(ins[a], slot(a, me), local_sems.at[a]) for a in range(na)]
        for cp in mine:
            cp.start()
        first = []
        for a in range(na):
            first.append(copy(a, 0, me, sibling, src=ins[a]))
            first += [copy(a, 1 + j, me, (*chip, c), src=ins[a]) for j, chip in enumerate(chips)]
        for cp in first:
            cp.start()
        passed = []
        for a in range(na):
            for j, chip in enumerate(chips):
                copy(a, 1 + j, (*chip, c), me).wait_recv()
                fwd = copy(a, 4 + j, (*chip, c), sibling)
                fwd.start()
                passed.append(fwd)
        for a in range(na):
            copy(a, 0, sibling, me).wait_recv()
            for j, chip in enumerate(chips):
                copy(a, 4 + j, (*chip, 1 - c), me).wait_recv()
        for cp in first + passed:
            cp.wait_send()
        for cp in mine:
            cp.wait()

    hbm = pl.BlockSpec(memory_space=pl.ANY)
    return pl.pallas_call(
        body, out_shape=tuple(jax.ShapeDtypeStruct((N_DEV,) + s.shape, s.dtype) for s in shards),
        in_specs=[hbm] * na, out_specs=tuple([hbm] * na),
        scratch_shapes=[pltpu.SemaphoreType.DMA((na, 7)), pltpu.SemaphoreType.DMA((na, 7)), pltpu.SemaphoreType.DMA((na,))],
        name=name)(*shards)


def _exchange_blocks(arrays, *, name):
    na = len(arrays)

    def body(*refs):
        ins, outs = refs[:na], refs[na:2 * na]
        send_sems, recv_sems = refs[2 * na:]
        x, y, c = _position()
        copies = []
        for a in range(na):
            for k in range(7):
                flip = k + 1
                peer = (x ^ (flip >> 2), y ^ ((flip >> 1) & 1), c ^ (flip & 1))
                peer_block = 4 * peer[0] + 2 * peer[1] + peer[2]
                cp = pltpu.make_async_remote_copy(
                    src_ref=ins[a].at[peer_block], dst_ref=outs[a].at[k],
                    send_sem=send_sems.at[a, k], recv_sem=recv_sems.at[a, k], device_id=peer, device_id_type=MESH)
                cp.start()
                copies.append(cp)
        for cp in copies:
            cp.wait()

    hbm = pl.BlockSpec(memory_space=pl.ANY)
    return pl.pallas_call(
        body, out_shape=tuple(jax.ShapeDtypeStruct((7,) + s.shape[1:], s.dtype) for s in arrays),
        in_specs=[hbm] * na, out_specs=tuple([hbm] * na),
        scratch_shapes=[pltpu.SemaphoreType.DMA((na, 7)), pltpu.SemaphoreType.DMA((na, 7))],
        name=name)(*arrays)


def _reduce_blocks(own, recv, *, name):
    r, c = own.shape
    tm = _tile(r, 256)

    def body(own_ref, recv_ref, o_ref):
        acc = own_ref[...]
        for k in range(7):
            acc = acc + recv_ref[k].astype(F32)
        o_ref[...] = acc

    return pl.pallas_call(
        body, out_shape=jax.ShapeDtypeStruct((r, c), F32), grid=(r // tm,),
        in_specs=[pl.BlockSpec((tm, c), lambda i: (i, 0)), pl.BlockSpec((7, tm, c), lambda i: (0, i, 0))],
        out_specs=pl.BlockSpec((tm, c), lambda i: (i, 0)), name=name, compiler_params=_params("parallel"))(own, recv)


def _all_reduce_small(vec, *, name):
    r, c = vec.shape

    def body(v_ref, o_ref, buf, send_sems, recv_sems):
        x, y, c_ = _position()
        me = 4 * x + 2 * y + c_
        buf[me] = v_ref[...]
        copies = []
        for k in range(7):
            flip = k + 1
            peer = (x ^ (flip >> 2), y ^ ((flip >> 1) & 1), c_ ^ (flip & 1))
            cp = pltpu.make_async_remote_copy(
                src_ref=v_ref, dst_ref=buf.at[me], send_sem=send_sems.at[k], recv_sem=recv_sems.at[k],
                device_id=peer, device_id_type=MESH)
            cp.start()
            copies.append(cp)
        for cp in copies:
            cp.wait()
        acc = buf[0]
        for d in range(1, N_DEV):
            acc = acc + buf[d]
        o_ref[...] = acc

    vm = pl.BlockSpec(memory_space=pltpu.VMEM)
    return pl.pallas_call(
        body, out_shape=jax.ShapeDtypeStruct((r, c), F32), in_specs=[vm], out_specs=vm,
        scratch_shapes=[pltpu.VMEM((N_DEV, r, c), F32), pltpu.SemaphoreType.DMA((7,)), pltpu.SemaphoreType.DMA((7,))],
        name=name)(vec)


def _pad_lanes(v, width=LANES):
    return jnp.pad(v, ((0, 0), (0, width - v.shape[1])))


def kernel(x, ffn1_pre_g, ffn1_w_gate, ffn1_w_up, ffn1_w_down, ffn1_post_g, mix_pre_g, w_in, conv_w, conv_b, dt_bias, a_log, d_skip, ssm_norm_g, w_ssm_proj, attn_sinks, rel_bias_table, w_attn_proj, w_out, mix_post_g, ffn2_pre_g, ffn2_w_gate, ffn2_w_up, ffn2_w_down, ffn2_post_g, loss_target, m_ffn1_pre_g, m_ffn1_w_gate, m_ffn1_w_up, m_ffn1_w_down, m_ffn1_post_g, m_mix_pre_g, m_w_in, m_conv_w, m_conv_b, m_dt_bias, m_a_log, m_d_skip, m_ssm_norm_g, m_w_ssm_proj, m_attn_sinks, m_rel_bias_table, m_w_attn_proj, m_w_out, m_mix_post_g, m_ffn2_pre_g, m_ffn2_w_gate, m_ffn2_w_up, m_ffn2_w_down, m_ffn2_post_g, v_ffn1_pre_g, v_ffn1_w_gate, v_ffn1_w_up, v_ffn1_w_down, v_ffn1_post_g, v_mix_pre_g, v_w_in, v_conv_w, v_conv_b, v_dt_bias, v_a_log, v_d_skip, v_ssm_norm_g, v_w_ssm_proj, v_attn_sinks, v_rel_bias_table, v_w_attn_proj, v_w_out, v_mix_post_g, v_ffn2_pre_g, v_ffn2_w_gate, v_ffn2_w_up, v_ffn2_w_down, v_ffn2_post_g):
    args = dict(locals())
    weight_names = ['ffn1_pre_g', 'ffn1_w_gate', 'ffn1_w_up', 'ffn1_w_down', 'ffn1_post_g', 'mix_pre_g', 'w_in', 'conv_w', 'conv_b',
                    'dt_bias', 'a_log', 'd_skip', 'ssm_norm_g', 'w_ssm_proj', 'attn_sinks', 'rel_bias_table', 'w_attn_proj', 'w_out',
                    'mix_post_g', 'ffn2_pre_g', 'ffn2_w_gate', 'ffn2_w_up', 'ffn2_w_down', 'ffn2_post_g']
    col_sharded = ('ffn1_w_gate', 'ffn1_w_up', 'w_in', 'ffn2_w_gate', 'ffn2_w_up')
    row_sharded = ('ffn1_w_down', 'w_ssm_proj', 'w_attn_proj', 'w_out', 'ffn2_w_down')
    big = col_sharded + row_sharded

    bl, s_len, d = x.shape
    t = bl * s_len
    d_inner = ssm_norm_g.shape[1]
    n_heads = dt_bias.shape[1]
    gn = SSM_GROUPS * SSM_STATE
    conv_dim = d_inner + 2 * gn
    q_dim = ATTN_Q_HEADS * ATTN_HEAD_DIM
    kv_dim = ATTN_KV_HEADS * ATTN_HEAD_DIM

    def local_2d(name, a):
        a = a[0]
        return a.T if name in col_sharded else a

    shards = [local_2d(n, args[n]).astype(BF16) for n in big] + [conv_w[0]]
    gathered = _all_gather(shards, name="gather_weights")
    full = {n: g.reshape(N_DEV * g.shape[1], g.shape[2]) for n, g in zip(big, gathered[:-1])}
    conv_w_full = jnp.transpose(gathered[-1], (1, 0, 2)).reshape(SSM_CONV, conv_dim)

    win_t = full['w_in']
    dt_lo = 2 * d + d_inner + conv_dim
    n_main = win_t.shape[0] - n_heads
    win_p = jnp.concatenate([win_t[:dt_lo], win_t[dt_lo + n_heads:], win_t[dt_lo:dt_lo + n_heads],
                             jnp.zeros((LANES - n_heads, d), BF16)], axis=0)
    off = {'gs': 0, 'ga': d, 'z': 2 * d, 'xbc': 2 * d + d_inner, 'q': dt_lo, 'kv': dt_lo + q_dim, 'dt': n_main}

    x2 = x.reshape(t, d)
    tgt2 = loss_target.reshape(t, d)

    h1, saved1 = _ffn_forward(x2, ffn1_pre_g, full['ffn1_w_gate'], full['ffn1_w_up'], full['ffn1_w_down'], ffn1_post_g, "ffn1")

    u = _rms_fwd(h1, mix_pre_g, name="mix_prenorm")

    def proj(key, width, out_dtype=BF16):
        tn = _tile(width, 512, LANES)
        assert off[key] % tn == 0
        return _mm_nt(u, win_p, row_off=off[key] // tn, n_out=width, out_dtype=out_dtype, tn=tn, name=f"proj_{key}")

    gs, ga, z, xbc = proj('gs', d), proj('ga', d), proj('z', d_inner), proj('xbc', conv_dim)
    q, kv = proj('q', q_dim), proj('kv', 2 * kv_dim)
    dt_raw = proj('dt', LANES, F32)

    dtb_p, alog_p, dsk_p, sinks_p = _pad_lanes(dt_bias), _pad_lanes(a_log), _pad_lanes(d_skip), _pad_lanes(attn_sinks)
    xc = _conv_fwd(xbc, conv_w_full, conv_b, bl, name="conv_fwd")
    y, hprev = _ssd_fwd(xc, dt_raw, dtb_p, alog_p, dsk_p, bl, n_heads, name="ssd_fwd")
    yn = _gated_norm_fwd(y, z, ssm_norm_g, name="gated_norm_fwd")

    onehot = _bucket_onehot()
    bias = _small_mm_hi(rel_bias_table.T, onehot, NN, name="rel_bias").reshape(ATTN_Q_HEADS, CHUNK, 2 * CHUNK)
    o, lse = _attn_fwd(q, kv, bias, sinks_p, bl, name="attn_fwd")

    ys, ya, merged, mix, h2 = _merge_fwd(yn, o, gs, ga, full['w_ssm_proj'], full['w_attn_proj'], full['w_out'], h1, mix_post_g,
                                         name="merge_fwd")

    h3, saved2 = _ffn_forward(h2, ffn2_pre_g, full['ffn2_w_gate'], full['ffn2_w_up'], full['ffn2_w_down'], ffn2_post_g, "ffn2")

    dh3, loss_vec = _loss_grad(h3, tgt2, name="loss")
    loss = lax.psum(loss_vec[0, 0], ("x", "y", "c"))

    grads = {}
    dh2, grads['ffn2_pre_g'], d_wgt2, d_wut2, d_wd2, grads['ffn2_post_g'] = _ffn_backward(
        dh3, saved2, ffn2_pre_g, full['ffn2_w_gate'], full['ffn2_w_up'], full['ffn2_w_down'], ffn2_post_g, "ffn2")

    dmix, dys, dya, dgs, dga, dyn, do, grads['mix_post_g'] = _merge_bwd(
        dh2, mix, mix_post_g, gs, ga, ys, ya, full['w_ssm_proj'], full['w_attn_proj'], full['w_out'], name="merge_bwd")
    d_wout = _mm_tn([merged], dmix, name="dw_out")
    d_wssm = _mm_tn([yn], dys, name="dw_ssm")
    d_wattn = _mm_tn([o], dya, name="dw_attn")

    dq, dkv, dbias, dsinks = _attn_bwd(do, q, kv, lse, bias, sinks_p, bl, name="attn_bwd")
    d_table = _small_mm_hi(onehot, dbias.reshape(ATTN_Q_HEADS, -1), NT, name="rel_bias_bwd")

    dy, dz, grads['ssm_norm_g'] = _gated_norm_bwd(dyn, y, z, ssm_norm_g, name="gated_norm_bwd")
    dxc, ddt_raw, ddtb, dalog, ddsk = _ssd_bwd(dy, xc, dt_raw, hprev, dtb_p, alog_p, dsk_p, bl, n_heads, name="ssd_bwd")
    dxbc, dconv_w8, grads['conv_b'] = _conv_bwd(dxc, xbc, conv_w_full, conv_b, bl, name="conv_bwd")

    proj_grads = [dgs, dga, dz, dxbc, dq, dkv]
    d_win_main = _mm_tn(proj_grads, u, name="dw_in")
    d_win_dt = _mm_tn([ddt_raw], u, name="dw_in_dt")
    d_win_t = jnp.concatenate([d_win_main[:dt_lo], d_win_dt[:n_heads], d_win_main[dt_lo:]], axis=0)

    tk = _tile(math.gcd(*[g_.shape[1] for g_ in proj_grads]), 512, LANES)
    segs = [(g_, win_p, tk, off[k_] // tk) for g_, k_ in zip(proj_grads, ('gs', 'ga', 'z', 'xbc', 'q', 'kv'))]
    segs.append((ddt_raw, win_p, LANES, off['dt'] // LANES))
    dh1, grads['mix_pre_g'] = _mm_nn_rmsbwd(segs, h1, mix_pre_g, dh2, name="mix_du")

    dx2, grads['ffn1_pre_g'], d_wgt1, d_wut1, d_wd1, grads['ffn1_post_g'] = _ffn_backward(
        dh1, saved1, ffn1_pre_g, full['ffn1_w_gate'], full['ffn1_w_up'], full['ffn1_w_down'], ffn1_post_g, "ffn1")

    big_grads = {'ffn1_w_gate': d_wgt1, 'ffn1_w_up': d_wut1, 'w_in': d_win_t, 'ffn2_w_gate': d_wgt2, 'ffn2_w_up': d_wut2,
                 'ffn1_w_down': d_wd1, 'w_ssm_proj': d_wssm, 'w_attn_proj': d_wattn, 'w_out': d_wout, 'ffn2_w_down': d_wd2}
    stacked = [big_grads[n].reshape(N_DEV, big_grads[n].shape[0] // N_DEV, big_grads[n].shape[1]) for n in big]
    dconv_w_blocks = jnp.transpose(dconv_w8[:SSM_CONV].reshape(SSM_CONV, N_DEV, conv_dim // N_DEV), (1, 0, 2))
    received = _exchange_blocks([s.astype(BF16) for s in stacked] + [dconv_w_blocks], name="scatter_grads")
    me = 4 * lax.axis_index("x") + 2 * lax.axis_index("y") + lax.axis_index("c")

    def own_block(a):
        return lax.dynamic_index_in_dim(a, me, 0, keepdims=False)
    reduced = {n: _reduce_blocks(own_block(own), rec, name=f"reduce_{n}") for n, own, rec in zip(big, stacked, received[:-1])}
    conv_sum = _reduce_blocks(own_block(dconv_w_blocks), received[-1], name="reduce_conv_w")

    grads['dt_bias'], grads['a_log'], grads['d_skip'] = ddtb[:, :n_heads], dalog[:, :n_heads], ddsk[:, :n_heads]
    grads['attn_sinks'] = dsinks[:, :ATTN_Q_HEADS]
    grads['rel_bias_table'] = d_table
    small = [n for n in weight_names if n not in big and n != 'conv_w']
    flat = jnp.concatenate([grads[n].reshape(-1) for n in small])
    n_small = flat.shape[0]
    rows = -(-n_small // (8 * LANES)) * 8
    flat = jnp.pad(flat, (0, rows * LANES - n_small)).reshape(rows, LANES)
    summed = _all_reduce_small(flat, name="allreduce_small").reshape(-1)
    pos = 0
    for n in small:
        size = grads[n].size
        grads[n] = summed[pos:pos + size].reshape(args[n].shape)
        pos += size

    out_g, out_d, out_m, out_v = {}, {}, {}, {}
    for n in big:
        w2, m2, v2 = local_2d(n, args[n]), local_2d(n, args['m_' + n]), local_2d(n, args['v_' + n])
        dlt, mn, vn = _adamw(w2, reduced[n], m2, v2, name=f"adamw_{n}")

        def back(a, n=n):
            return (a.T if n in col_sharded else a)[None]
        out_g[n], out_d[n], out_m[n], out_v[n] = back(reduced[n]), back(dlt), back(mn), back(vn)

    def pack(prefix):
        vals = [(grads[n] if prefix == 'g' else args[prefix + n]).reshape(-1) for n in small]
        vals.append((conv_sum if prefix == 'g' else args[prefix + 'conv_w']).reshape(-1))
        flat_ = jnp.concatenate(vals)
        rows_ = -(-flat_.shape[0] // (8 * LANES)) * 8
        return jnp.pad(flat_, (0, rows_ * LANES - flat_.shape[0])).reshape(rows_, LANES)

    g_small = pack('g')
    d_small, m_small, v_small = _adamw(pack(''), g_small, pack('m_'), pack('v_'), name="adamw_small")
    pos = 0
    for n in small + ['conv_w']:
        shape = args[n].shape
        size = int(np.prod(shape))
        for dst, src in ((out_g, g_small), (out_d, d_small), (out_m, m_small), (out_v, v_small)):
            dst[n] = src.reshape(-1)[pos:pos + size].reshape(shape)
        pos += size

    grad_x = dx2.reshape(bl, s_len, d)
    return (loss, grad_x, *[out_g[n] for n in weight_names], *[out_d[n] for n in weight_names],
            *[out_m[n] for n in weight_names], *[out_v[n] for n in weight_names])
```

```python
import functools
import math

import numpy as np
import jax
import jax.numpy as jnp
from jax import lax
from jax.experimental import pallas as pl
from jax.experimental.pallas import tpu as pltpu

F32 = jnp.float32
BF16 = jnp.bfloat16
MESH = pl.DeviceIdType.MESH
N_DEV = 8

SSM_HEAD_DIM = 64
SSM_GROUPS = 4
SSM_STATE = 128
SSM_CONV = 4
CHUNK = 128
ATTN_HEAD_DIM = 64
ATTN_Q_HEADS = 16
ATTN_KV_HEADS = 4
REL_BUCKETS = 32
REL_MAX_DISTANCE = 128
RMS_EPS = 1e-6
FFN_RESIDUAL_WEIGHT = 0.5
ADAM_LR, ADAM_B1, ADAM_B2, ADAM_EPS, ADAM_WD, ADAM_STEP = 0.001, 0.9, 0.999, 1e-08, 0.01, 10

LANES = 128
VMEM_LIMIT_BYTES = 56 * 1024 * 1024
FFN_COL_TILE = 1408

NEG_INF = float("-inf")


def _params(*sem):
    return pltpu.CompilerParams(dimension_semantics=sem, vmem_limit_bytes=VMEM_LIMIT_BYTES)


def _tile(n, pref, mult=8):
    if n <= pref:
        return n
    t = (pref // mult) * mult
    while t >= mult:
        if n % t == 0:
            return t
        t -= mult
    return n


def _sigmoid(x):
    return 1.0 / (1.0 + jnp.exp(-x))


def _dot(a, b, dims):
    return lax.dot_general(a, b, (dims, ((), ())), preferred_element_type=F32)


NN = ((1,), (0,))
NT = ((1,), (1,))
TN = ((0,), (0,))


def _dot_hi(a, b, dims=NN):
    return lax.dot_general(a, b, (dims, ((), ())), preferred_element_type=F32, precision=lax.Precision.HIGHEST)


def _const_spec(shape):
    nd = len(shape)
    return pl.BlockSpec(shape, lambda *_: (0,) * nd)


def _mm_nt(a, bt, *, row_off=0, n_out=None, out_dtype=BF16, tm=512, tn=512, name):
    m, k = a.shape
    n_out = bt.shape[0] if n_out is None else n_out
    tm, tn = _tile(m, tm), _tile(n_out, tn, LANES)

    def body(a_ref, b_ref, o_ref):
        o_ref[...] = _dot(a_ref[...].astype(BF16), b_ref[...].astype(BF16), NT).astype(o_ref.dtype)

    return pl.pallas_call(
        body, out_shape=jax.ShapeDtypeStruct((m, n_out), out_dtype), grid=(n_out // tn, m // tm),
        in_specs=[pl.BlockSpec((tm, k), lambda j, i: (i, 0)), pl.BlockSpec((tn, k), lambda j, i: (j + row_off, 0))],
        out_specs=pl.BlockSpec((tm, tn), lambda j, i: (i, j)),
        name=name, compiler_params=_params("parallel", "arbitrary"))(a, bt)


def _mm_tn(a_list, b, *, out_dtype=F32, tm=1408, tk=1024, name):
    t, n = b.shape
    tk = _tile(t, tk)
    nk = t // tk
    widths = [a.shape[1] for a in a_list]
    tm = _tile(math.gcd(*widths), tm, LANES)
    assert all(w % tm == 0 for w in widths)
    starts = np.cumsum([0] + [w // tm for w in widths])
    nseg = len(a_list)

    def a_spec(s):
        lo, hi = int(starts[s]), int(starts[s + 1])

        def idx(i, k):
            active = jnp.logical_and(i >= lo, i < hi)
            return (jnp.where(active, k, 0), jnp.clip(i - lo, 0, hi - lo - 1))
        return pl.BlockSpec((tk, tm), idx)

    def body(*refs):
        a_refs, b_ref, o_ref, acc = refs[:nseg], refs[nseg], refs[nseg + 1], refs[nseg + 2]
        i, k = pl.program_id(0), pl.program_id(1)

        @pl.when(k == 0)
        def _():
            acc[...] = jnp.zeros_like(acc)

        bv = b_ref[...].astype(BF16)
        for s in range(nseg):
            lo, hi = int(starts[s]), int(starts[s + 1])

            @pl.when(jnp.logical_and(i >= lo, i < hi))
            def _(s=s):
                acc[...] += _dot(a_refs[s][...].astype(BF16), bv, TN)

        @pl.when(k == nk - 1)
        def _():
            o_ref[...] = acc[...].astype(o_ref.dtype)

    return pl.pallas_call(
        body, out_shape=jax.ShapeDtypeStruct((int(starts[-1]) * tm, n), out_dtype), grid=(int(starts[-1]), nk),
        in_specs=[a_spec(s) for s in range(nseg)] + [pl.BlockSpec((tk, n), lambda i, k: (k, 0))],
        out_specs=pl.BlockSpec((tm, n), lambda i, k: (i, 0)),
        scratch_shapes=[pltpu.VMEM((tm, n), F32)],
        name=name, compiler_params=_params("parallel", "arbitrary"))(*a_list, b)


def _mm_nn_rmsbwd(segs, x, g, dres, *, tm=512, name):
    t, d = x.shape
    tm = _tile(t, tm)
    counts = [a.shape[1] // tk for a, _, tk, _ in segs]
    starts = np.cumsum([0] + counts)
    nk = int(starts[-1])
    nseg = len(segs)

    def specs(s):
        lo, cnt = int(starts[s]), counts[s]
        tk, off = segs[s][2], segs[s][3]

        def rel(k):
            return jnp.clip(k - lo, 0, cnt - 1)
        return (pl.BlockSpec((tm, tk), lambda i, k: (i, rel(k))), pl.BlockSpec((tk, d), lambda i, k: (off + rel(k), 0)))

    def body(*refs):
        a_refs, b_refs = refs[0:2 * nseg:2], refs[1:2 * nseg:2]
        x_ref, g_ref, dres_ref, dx_ref, dg_ref, acc = refs[2 * nseg:]
        i, k = pl.program_id(0), pl.program_id(1)

        @pl.when(k == 0)
        def _():
            acc[...] = jnp.zeros_like(acc)

        for s in range(nseg):
            lo, hi = int(starts[s]), int(starts[s + 1])

            @pl.when(jnp.logical_and(k >= lo, k < hi))
            def _(s=s):
                acc[...] += _dot(a_refs[s][...].astype(BF16), b_refs[s][...].astype(BF16), NN)

        @pl.when(jnp.logical_and(i == 0, k == 0))
        def _():
            dg_ref[...] = jnp.zeros_like(dg_ref)

        @pl.when(k == nk - 1)
        def _():
            dn = acc[...]
            xv = x_ref[...]
            r = lax.rsqrt(jnp.mean(xv * xv, axis=-1, keepdims=True) + RMS_EPS)
            xhat = xv * r
            dyg = dn * g_ref[...]
            dx_ref[...] = dres_ref[...] + r * (dyg - xhat * jnp.mean(dyg * xhat, axis=-1, keepdims=True))
            dg_ref[...] += jnp.sum(dn * xhat, axis=0, keepdims=True)

    in_specs, operands = [], []
    for s in range(nseg):
        sa, sb = specs(s)
        in_specs += [sa, sb]
        operands += [segs[s][0], segs[s][1]]
    row = pl.BlockSpec((tm, d), lambda i, k: (i, 0))
    in_specs += [row, pl.BlockSpec((1, d), lambda i, k: (0, 0)), row]
    return pl.pallas_call(
        body, out_shape=(jax.ShapeDtypeStruct((t, d), F32), jax.ShapeDtypeStruct((1, d), F32)), grid=(t // tm, nk),
        in_specs=in_specs, out_specs=(row, pl.BlockSpec((1, d), lambda i, k: (0, 0))),
        scratch_shapes=[pltpu.VMEM((tm, d), F32)],
        name=name, compiler_params=_params("arbitrary", "arbitrary"))(*operands, x, g, dres)


def _rms_fwd(x, g, *, name):
    t, d = x.shape
    tm = _tile(t, 512)

    def body(x_ref, g_ref, o_ref):
        xv = x_ref[...]
        r = lax.rsqrt(jnp.mean(xv * xv, axis=-1, keepdims=True) + RMS_EPS)
        o_ref[...] = (xv * r * g_ref[...]).astype(o_ref.dtype)

    row = pl.BlockSpec((tm, d), lambda i: (i, 0))
    return pl.pallas_call(body, out_shape=jax.ShapeDtypeStruct((t, d), BF16), grid=(t // tm,),
                          in_specs=[row, _const_spec((1, d))], out_specs=row, name=name, compiler_params=_params("parallel"))(x, g)


def _ffn_up(n, wgt, wut, *, name):
    t, d = n.shape
    f = wgt.shape[0]
    tm, tn = _tile(t, 512), _tile(f, FFN_COL_TILE, LANES)

    def body(n_ref, wg_ref, wu_ref, g_ref, u_ref, h_ref):
        nv = n_ref[...]
        gv = _dot(nv, wg_ref[...], NT)
        uv = _dot(nv, wu_ref[...], NT)
        g_ref[...] = gv.astype(BF16)
        u_ref[...] = uv.astype(BF16)
        h_ref[...] = (gv * _sigmoid(gv) * uv).astype(BF16)

    w_spec = pl.BlockSpec((tn, d), lambda j, i: (j, 0))
    o_spec = pl.BlockSpec((tm, tn), lambda j, i: (i, j))
    shp = jax.ShapeDtypeStruct((t, f), BF16)
    return pl.pallas_call(body, out_shape=(shp, shp, shp), grid=(f // tn, t // tm),
                          in_specs=[pl.BlockSpec((tm, d), lambda j, i: (i, 0)), w_spec, w_spec], out_specs=(o_spec, o_spec, o_spec),
                          name=name, compiler_params=_params("parallel", "arbitrary"))(n, wgt, wut)


def _rms_residual(acc, h, gp, weight):
    r = lax.rsqrt(jnp.mean(acc * acc, axis=-1, keepdims=True) + RMS_EPS)
    return h + weight * (acc * r * gp)


def _ffn_down(hid, wd, h_in, gp, *, name):
    t, f = hid.shape
    d = wd.shape[1]
    tm = _tile(t, 256)

    def body(hid_ref, wd_ref, hin_ref, gp_ref, f_ref, hout_ref):
        acc = _dot(hid_ref[...], wd_ref[...], NN)
        f_ref[...] = acc
        hout_ref[...] = _rms_residual(acc, hin_ref[...], gp_ref[...], FFN_RESIDUAL_WEIGHT)

    row = pl.BlockSpec((tm, d), lambda i: (i, 0))
    shp = jax.ShapeDtypeStruct((t, d), F32)
    return pl.pallas_call(body, out_shape=(shp, shp), grid=(t // tm,),
                          in_specs=[pl.BlockSpec((tm, f), lambda i: (i, 0)), _const_spec((f, d)), row, _const_spec((1, d))],
                          out_specs=(row, row), name=name, compiler_params=_params("parallel"))(hid, wd, h_in, gp)


def _post_bwd(dh, f, gp, weight, *, name):
    t, d = f.shape
    tm = _tile(t, 512)

    def body(dh_ref, f_ref, gp_ref, df_ref, dgp_ref):
        @pl.when(pl.program_id(0) == 0)
        def _():
            dgp_ref[...] = jnp.zeros_like(dgp_ref)
        fv = f_ref[...]
        dy = weight * dh_ref[...]
        r = lax.rsqrt(jnp.mean(fv * fv, axis=-1, keepdims=True) + RMS_EPS)
        fhat = fv * r
        dyg = dy * gp_ref[...]
        df_ref[...] = (r * (dyg - fhat * jnp.mean(dyg * fhat, axis=-1, keepdims=True))).astype(BF16)
        dgp_ref[...] += jnp.sum(dy * fhat, axis=0, keepdims=True)

    row = pl.BlockSpec((tm, d), lambda i: (i, 0))
    return pl.pallas_call(body, out_shape=(jax.ShapeDtypeStruct((t, d), BF16), jax.ShapeDtypeStruct((1, d), F32)), grid=(t // tm,),
                          in_specs=[row, row, _const_spec((1, d))], out_specs=(row, _const_spec((1, d))),
                          name=name, compiler_params=_params("arbitrary"))(dh, f, gp)


def _ffn_dhid(df, wd, g, u, *, name):
    t, d = df.shape
    f = wd.shape[0]
    tm, tn = _tile(t, 512), _tile(f, FFN_COL_TILE, LANES)

    def body(df_ref, wd_ref, g_ref, u_ref, dg_ref, du_ref):
        dh = _dot(df_ref[...], wd_ref[...], NT)
        gv = g_ref[...].astype(F32)
        uv = u_ref[...].astype(F32)
        sg = _sigmoid(gv)
        dg_ref[...] = (dh * uv * (sg * (1.0 + gv * (1.0 - sg)))).astype(BF16)
        du_ref[...] = (dh * (gv * sg)).astype(BF16)

    o_spec = pl.BlockSpec((tm, tn), lambda j, i: (i, j))
    shp = jax.ShapeDtypeStruct((t, f), BF16)
    return pl.pallas_call(body, out_shape=(shp, shp), grid=(f // tn, t // tm),
                          in_specs=[pl.BlockSpec((tm, d), lambda j, i: (i, 0)), pl.BlockSpec((tn, d), lambda j, i: (j, 0)), o_spec, o_spec],
                          out_specs=(o_spec, o_spec), name=name, compiler_params=_params("parallel", "arbitrary"))(df, wd, g, u)


def _ffn_forward(h_in, g_pre, wgt, wut, wd, g_post, tag):
    n = _rms_fwd(h_in, g_pre, name=f"{tag}_prenorm")
    g, u, hid = _ffn_up(n, wgt, wut, name=f"{tag}_up")
    f, h_out = _ffn_down(hid, wd, h_in, g_post, name=f"{tag}_down")
    return h_out, (h_in, n, g, u, hid, f)


def _ffn_backward(dh_out, saved, g_pre, wgt, wut, wd, g_post, tag):
    h_in, n, g, u, hid, f = saved
    df, dg_post = _post_bwd(dh_out, f, g_post, FFN_RESIDUAL_WEIGHT, name=f"{tag}_post_bwd")
    dgate, dup = _ffn_dhid(df, wd, g, u, name=f"{tag}_dhid")
    d_wd = _mm_tn([hid], df, name=f"{tag}_dwd")
    d_wgt = _mm_tn([dgate], n, name=f"{tag}_dwg")
    d_wut = _mm_tn([dup], n, name=f"{tag}_dwu")
    tk = _tile(wgt.shape[0], FFN_COL_TILE, LANES)
    dh_in, dg_pre = _mm_nn_rmsbwd([(dgate, wgt, tk, 0), (dup, wut, tk, 0)], h_in, g_pre, dh_out, name=f"{tag}_dn")
    return dh_in, dg_pre, d_wgt, d_wut, d_wd, dg_post


def _shift_down(x, k, rows):
    return jnp.where(rows >= k, pltpu.roll(x, k, axis=0), 0.0)


def _shift_up(x, k, rows, n):
    return jnp.where(rows < n - k, pltpu.roll(x, n - k, axis=0), 0.0)


def _conv_pre(xv, w_ref, b_ref, rows):
    pre = xv * w_ref[SSM_CONV - 1:SSM_CONV, :] + b_ref[...]
    for k in range(1, SSM_CONV):
        pre = pre + _shift_down(xv, k, rows) * w_ref[SSM_CONV - 1 - k:SSM_CONV - k, :]
    return pre


def _conv_fwd(xbc, conv_w, conv_b, bl, *, name):
    t, c = xbc.shape
    s = t // bl
    tc = LANES

    def body(x_ref, w_ref, b_ref, o_ref):
        xv = x_ref[...].astype(F32)
        rows = lax.broadcasted_iota(jnp.int32, xv.shape, 0)
        pre = _conv_pre(xv, w_ref, b_ref, rows)
        o_ref[...] = (pre * _sigmoid(pre)).astype(o_ref.dtype)

    blk = pl.BlockSpec((s, tc), lambda b, j: (b, j))
    return pl.pallas_call(body, out_shape=jax.ShapeDtypeStruct((t, c), BF16), grid=(bl, c // tc),
                          in_specs=[blk, pl.BlockSpec((SSM_CONV, tc), lambda b, j: (0, j)), pl.BlockSpec((1, tc), lambda b, j: (0, j))],
                          out_specs=blk, name=name, compiler_params=_params("parallel", "arbitrary"))(xbc, conv_w, conv_b)


def _conv_bwd(dxc, xbc, conv_w, conv_b, bl, *, name):
    t, c = xbc.shape
    s = t // bl
    tc = LANES

    def body(dy_ref, x_ref, w_ref, b_ref, dx_ref, dw_ref, db_ref):
        @pl.when(pl.program_id(1) == 0)
        def _():
            dw_ref[...] = jnp.zeros_like(dw_ref)
            db_ref[...] = jnp.zeros_like(db_ref)
        xv = x_ref[...].astype(F32)
        rows = lax.broadcasted_iota(jnp.int32, xv.shape, 0)
        pre = _conv_pre(xv, w_ref, b_ref, rows)
        sg = _sigmoid(pre)
        dpre = dy_ref[...].astype(F32) * (sg * (1.0 + pre * (1.0 - sg)))
        dx = dpre * w_ref[SSM_CONV - 1:SSM_CONV, :]
        db_ref[...] += jnp.sum(dpre, axis=0, keepdims=True)
        dw_ref[SSM_CONV - 1:SSM_CONV, :] += jnp.sum(dpre * xv, axis=0, keepdims=True)
        for k in range(1, SSM_CONV):
            dx = dx + _shift_up(dpre, k, rows, s) * w_ref[SSM_CONV - 1 - k:SSM_CONV - k, :]
            dw_ref[SSM_CONV - 1 - k:SSM_CONV - k, :] += jnp.sum(dpre * _shift_down(xv, k, rows), axis=0, keepdims=True)
        dx_ref[...] = dx.astype(dx_ref.dtype)

    blk = pl.BlockSpec((s, tc), lambda j, b: (b, j))
    return pl.pallas_call(
        body, out_shape=(jax.ShapeDtypeStruct((t, c), BF16), jax.ShapeDtypeStruct((8, c), F32), jax.ShapeDtypeStruct((1, c), F32)),
        grid=(c // tc, bl),
        in_specs=[blk, blk, pl.BlockSpec((SSM_CONV, tc), lambda j, b: (0, j)), pl.BlockSpec((1, tc), lambda j, b: (0, j))],
        out_specs=(blk, pl.BlockSpec((8, tc), lambda j, b: (0, j)), pl.BlockSpec((1, tc), lambda j, b: (0, j))),
        name=name, compiler_params=_params("parallel", "arbitrary"))(dxc, xbc, conv_w, conv_b)


def _softplus(x):
    return jnp.maximum(x, 0.0) + jnp.log1p(jnp.exp(-jnp.abs(x)))


def _hilo_dot(v, m_b, dims=NN):
    hi = v.astype(BF16)
    lo = (v - hi.astype(F32)).astype(BF16)
    return _dot(hi, m_b, dims) + _dot(lo, m_b, dims)


def _ssd_chunk_common(dtraw_ref, dtb_ref, alog_ref, dsk_ref, d_inner):
    q, p = CHUNK, SSM_HEAD_DIM
    shift = p.bit_length() - 1
    assert 1 << shift == p
    dt = _softplus(dtraw_ref[...] + dtb_ref[...])
    a = -jnp.exp(alog_ref[...])
    ii = lax.broadcasted_iota(jnp.int32, (q, q), 0)
    jj = lax.broadcasted_iota(jnp.int32, (q, q), 1)
    causal = ii >= jj
    tril = jnp.where(causal, 1.0, 0.0).astype(F32)
    triu = jnp.where(ii <= jj, 1.0, 0.0).astype(F32)
    a_cs = _dot_hi(tril, dt * a)
    a_cs_t = a_cs.T
    a_last = a_cs[q - 1:q, :]
    e_col = jnp.exp(a_cs)
    dec_end = jnp.exp(a_last - a_cs)
    head_of_col = lax.shift_right_logical(lax.broadcasted_iota(jnp.int32, (LANES, d_inner), 1), shift)
    spread = (lax.broadcasted_iota(jnp.int32, (LANES, d_inner), 0) == head_of_col).astype(BF16)
    wide = _hilo_dot(jnp.concatenate([dt, e_col, dec_end, jnp.broadcast_to(dsk_ref[...], (8, LANES))], axis=0), spread)
    return dict(dt=dt, a=a, a_cs=a_cs, a_cs_t=a_cs_t, a_last=a_last, dec_end=dec_end, causal=causal, triu=triu,
                dt_e=wide[:q], e_e=wide[q:2 * q], dec_e=wide[2 * q:3 * q], dsk_e=wide[3 * q:3 * q + 1])


def _fill_block_diag(bd_ref, src_ref, hpg, col0=0):
    q, p = CHUNK, SSM_HEAD_DIM
    for hh in range(hpg):
        bd_ref[hh * q:(hh + 1) * q, hh * p:(hh + 1) * p] = src_ref[:, col0 + hh * p:col0 + (hh + 1) * p]


def _lane_onehot(h):
    return (lax.broadcasted_iota(jnp.int32, (1, LANES), 1) == h).astype(F32)


def _ssd_fwd(xc, dt_raw, dt_bias, a_log, d_skip, bl, n_heads, *, name):
    t = xc.shape[0]
    q, p, nst, grp = CHUNK, SSM_HEAD_DIM, SSM_STATE, SSM_GROUPS
    d_inner = n_heads * p
    hpg = n_heads // grp
    gw = hpg * p
    nc = t // bl // q
    assert d_inner % (grp * nst) == 0 and nst == LANES

    def body(xs_ref, b_ref, c_ref, dtraw_ref, dtb_ref, alog_ref, dsk_ref, y_ref, hprev_ref, state, m_all, x_bd, xdt_s):
        @pl.when(jnp.logical_and(pl.program_id(0) == 0, pl.program_id(1) == 0))
        def _():
            x_bd[...] = jnp.zeros_like(x_bd)

        @pl.when(pl.program_id(1) == 0)
        def _():
            state[...] = jnp.zeros_like(state)

        cm = _ssd_chunk_common(dtraw_ref, dtb_ref, alog_ref, dsk_ref, d_inner)
        for g in range(grp):
            cols = slice(g * gw, (g + 1) * gw)
            bg = b_ref[:, g * nst:(g + 1) * nst]
            cg = c_ref[:, g * nst:(g + 1) * nst]
            scores = _dot(cg, bg, NT)
            for hh in range(hpg):
                h = g * hpg + hh
                seg = cm['a_cs'][:, h:h + 1] - cm['a_cs_t'][h:h + 1, :]
                m_all[:, hh * q:(hh + 1) * q] = (scores * jnp.exp(jnp.where(cm['causal'], seg, NEG_INF))).astype(BF16)
            xs = xs_ref[:, cols].astype(F32)
            xdt = xs * cm['dt_e'][:, cols]
            xdt_s[...] = xdt.astype(BF16)
            _fill_block_diag(x_bd, xdt_s, hpg)
            hprev = state[g]
            hprev_ref[g] = hprev
            y = _dot(m_all[...], x_bd[...], NN) + cm['e_e'][:, cols] * _dot(cg, hprev.astype(BF16), NT)
            y_ref[:, cols] = y + cm['dsk_e'][:, cols] * xs
            st = _dot((xdt * cm['dec_e'][:, cols]).astype(BF16), bg, TN)
            for hh in range(hpg):
                h = g * hpg + hh
                rows = slice(hh * p, (hh + 1) * p)
                state[g, rows, :] = jnp.exp(cm['a_last'][:, h:h + 1]) * hprev[rows] + st[rows]

    gn = grp * nst

    def rowmap(b, c):
        return b * nc + c
    vec = pl.BlockSpec((1, LANES), lambda b, c: (0, 0))
    return pl.pallas_call(
        body,
        out_shape=(jax.ShapeDtypeStruct((t, d_inner), F32), jax.ShapeDtypeStruct((t // q, grp, gw, nst), F32)),
        grid=(bl, nc),
        in_specs=[pl.BlockSpec((q, d_inner), lambda b, c: (rowmap(b, c), 0)),
                  pl.BlockSpec((q, gn), lambda b, c: (rowmap(b, c), d_inner // gn)),
                  pl.BlockSpec((q, gn), lambda b, c: (rowmap(b, c), d_inner // gn + 1)),
                  pl.BlockSpec((q, LANES), lambda b, c: (rowmap(b, c), 0)), vec, vec, vec],
        out_specs=(pl.BlockSpec((q, d_inner), lambda b, c: (rowmap(b, c), 0)),
                   pl.BlockSpec((None, grp, gw, nst), lambda b, c: (rowmap(b, c), 0, 0, 0))),
        scratch_shapes=[pltpu.VMEM((grp, gw, nst), F32), pltpu.VMEM((q, hpg * q), BF16), pltpu.VMEM((hpg * q, gw), BF16),
                        pltpu.VMEM((q, gw), BF16)],
        name=name, compiler_params=_params("arbitrary", "arbitrary"))(xc, xc, xc, dt_raw, dt_bias, a_log, d_skip)


def _ssd_bwd(dy, xc, dt_raw, hprev_all, dt_bias, a_log, d_skip, bl, n_heads, *, name):
    t, c_dim = xc.shape
    q, p, nst, grp = CHUNK, SSM_HEAD_DIM, SSM_STATE, SSM_GROUPS
    d_inner = n_heads * p
    hpg = n_heads // grp
    gw = hpg * p
    nc = t // bl // q
    gn = grp * nst
    shift = p.bit_length() - 1

    def body(dy_ref, xs_ref, b_ref, c_ref, dtraw_ref, hprev_ref, dtb_ref, alog_ref, dsk_ref,
             dxc_ref, ddtraw_ref, ddtb_ref, dalog_ref, ddsk_ref, dstate, m_all, mt_all, x_bd, dy_bd, xdt_s):
        @pl.when(jnp.logical_and(pl.program_id(0) == 0, pl.program_id(1) == 0))
        def _():
            ddtb_ref[...] = jnp.zeros_like(ddtb_ref)
            dalog_ref[...] = jnp.zeros_like(dalog_ref)
            ddsk_ref[...] = jnp.zeros_like(ddsk_ref)
            x_bd[...] = jnp.zeros_like(x_bd)
            dy_bd[...] = jnp.zeros_like(dy_bd)

        @pl.when(pl.program_id(1) == 0)
        def _():
            dstate[...] = jnp.zeros_like(dstate)

        cm = _ssd_chunk_common(dtraw_ref, dtb_ref, alog_ref, dsk_ref, d_inner)
        causal = cm['causal']
        upper = cm['triu'] > 0.5
        seg_row = lax.shift_right_logical(lax.broadcasted_iota(jnp.int32, (gw, LANES), 0), shift)
        seg_lane = lax.broadcasted_iota(jnp.int32, (gw, LANES), 1)
        sums = jnp.zeros((5 * q, LANES), F32)
        state_dot = jnp.zeros((1, LANES), F32)
        for g in range(grp):
            cols = slice(g * gw, (g + 1) * gw)
            seg_sum = (seg_row + g * hpg == seg_lane).astype(BF16)
            bg = b_ref[:, g * nst:(g + 1) * nst]
            cg = c_ref[:, g * nst:(g + 1) * nst]
            scores = _dot(cg, bg, NT)
            scores_t = _dot(bg, cg, NT)
            xs = xs_ref[:, cols].astype(F32)
            xdt = xs * cm['dt_e'][:, cols]
            xdt_s[...] = xdt.astype(BF16)
            _fill_block_diag(x_bd, xdt_s, hpg)
            _fill_block_diag(dy_bd, dy_ref, hpg, g * gw)
            dy_b = dy_ref[:, cols]
            dyf = dy_b.astype(F32)
            dm_all = _dot(dy_b, x_bd[...], NT)
            dscores = jnp.zeros((q, q), F32)
            for hh in range(hpg):
                h = g * hpg + hh
                blk = slice(hh * q, (hh + 1) * q)
                seg = cm['a_cs'][:, h:h + 1] - cm['a_cs_t'][h:h + 1, :]
                decay = jnp.exp(jnp.where(causal, seg, NEG_INF))
                m_all[:, blk] = (scores * decay).astype(BF16)
                mt_all[:, blk] = (scores_t * jnp.exp(jnp.where(upper, -seg, NEG_INF))).astype(BF16)
                dscores = dscores + dm_all[:, blk] * decay
            hprev = hprev_ref[g]
            hprev_b = hprev.astype(BF16)
            dhn = dstate[g]
            dhn_b = dhn.astype(BF16)
            e_e, dec_e = cm['e_e'][:, cols], cm['dec_e'][:, cols]
            y_scan = _dot(m_all[...], x_bd[...], NN) + e_e * _dot(cg, hprev_b, NT)
            dye_b = (dyf * e_e).astype(BF16)
            dcg = _dot(dye_b, hprev_b, NN)
            dhp = _dot(dye_b, cg, TN)
            bdh = _dot(bg, dhn_b, NT)
            dbg = _dot((xdt * dec_e).astype(BF16), dhn_b, NN)
            dx_diag = _dot(mt_all[...], dy_bd[...], NN)
            dx = dec_e * bdh + dx_diag
            ds_b = dscores.astype(BF16)
            dcg = dcg + _dot(ds_b, bg, NN)
            dbg = dbg + _dot(ds_b, cg, TN)
            x_rounded = xdt_s[...].astype(F32)
            sums = sums + _hilo_dot(jnp.concatenate([dyf * y_scan, xdt * bdh, x_rounded * dx_diag, dx * xs, dyf * xs], axis=0), seg_sum)
            state_dot = state_dot + jnp.sum(_hilo_dot(dhn * hprev, seg_sum, TN), axis=0, keepdims=True)
            dxc_ref[:, cols] = (dx * cm['dt_e'][:, cols] + cm['dsk_e'][:, cols] * dyf).astype(dxc_ref.dtype)
            dxc_ref[:, d_inner + g * nst:d_inner + (g + 1) * nst] = dbg.astype(dxc_ref.dtype)
            dxc_ref[:, d_inner + gn + g * nst:d_inner + gn + (g + 1) * nst] = dcg.astype(dxc_ref.dtype)
            for hh in range(hpg):
                h = g * hpg + hh
                rows = slice(hh * p, (hh + 1) * p)
                dstate[g, rows, :] = jnp.exp(cm['a_last'][:, h:h + 1]) * dhn[rows] + dhp[rows]
        s_y, s_end, s_diag, s_dt, s_skip = (sums[k * q:(k + 1) * q] for k in range(5))
        dt, a, dec_end = cm['dt'], cm['a'], cm['dec_end']
        last_row = (lax.broadcasted_iota(jnp.int32, (q, 1), 0) == q - 1).astype(F32)
        da_last = jnp.sum(dec_end * s_end, axis=0, keepdims=True) + jnp.exp(cm['a_last']) * state_dot
        da = s_y - dec_end * s_end - s_diag + last_row * da_last
        ddta = _dot_hi(cm['triu'], da)
        ddt = s_dt + ddta * a
        d_a = jnp.sum(ddta * dt, axis=0, keepdims=True)
        ddt_raw = ddt * _sigmoid(dtraw_ref[...] + dtb_ref[...])
        ddtraw_ref[...] = ddt_raw
        ddtb_ref[...] += jnp.sum(ddt_raw, axis=0, keepdims=True)
        dalog_ref[...] += d_a * a
        ddsk_ref[...] += jnp.sum(s_skip, axis=0, keepdims=True)

    def rowmap(b, c):
        return b * nc + (nc - 1 - c)
    vec = pl.BlockSpec((1, LANES), lambda b, c: (0, 0))
    vec_shape = jax.ShapeDtypeStruct((1, LANES), F32)
    return pl.pallas_call(
        body,
        out_shape=(jax.ShapeDtypeStruct((t, c_dim), BF16), jax.ShapeDtypeStruct((t, LANES), F32), vec_shape, vec_shape, vec_shape),
        grid=(bl, nc),
        in_specs=[pl.BlockSpec((q, d_inner), lambda b, c: (rowmap(b, c), 0)),
                  pl.BlockSpec((q, d_inner), lambda b, c: (rowmap(b, c), 0)),
                  pl.BlockSpec((q, gn), lambda b, c: (rowmap(b, c), d_inner // gn)),
                  pl.BlockSpec((q, gn), lambda b, c: (rowmap(b, c), d_inner // gn + 1)),
                  pl.BlockSpec((q, LANES), lambda b, c: (rowmap(b, c), 0)),
                  pl.BlockSpec((None, grp, gw, nst), lambda b, c: (rowmap(b, c), 0, 0, 0)), vec, vec, vec],
        out_specs=(pl.BlockSpec((q, c_dim), lambda b, c: (rowmap(b, c), 0)),
                   pl.BlockSpec((q, LANES), lambda b, c: (rowmap(b, c), 0)), vec, vec, vec),
        scratch_shapes=[pltpu.VMEM((grp, gw, nst), F32), pltpu.VMEM((q, hpg * q), BF16), pltpu.VMEM((q, hpg * q), BF16),
                        pltpu.VMEM((hpg * q, gw), BF16), pltpu.VMEM((hpg * q, gw), BF16), pltpu.VMEM((q, gw), BF16)],
        name=name, compiler_params=_params("arbitrary", "arbitrary"))(dy, xc, xc, xc, dt_raw, hprev_all, dt_bias, a_log, d_skip)


def _gated_norm_fwd(y, z, ng, *, name):
    t, d = y.shape
    tm = _tile(t, 256)
    gw = d // SSM_GROUPS

    def body(y_ref, z_ref, ng_ref, o_ref):
        for g in range(SSM_GROUPS):
            sl = slice(g * gw, (g + 1) * gw)
            zv = z_ref[:, sl].astype(F32)
            yg = y_ref[:, sl] * (zv * _sigmoid(zv))
            r = lax.rsqrt(jnp.mean(yg * yg, axis=-1, keepdims=True) + RMS_EPS)
            o_ref[:, sl] = (yg * r * ng_ref[:, sl]).astype(o_ref.dtype)

    row = pl.BlockSpec((tm, d), lambda i: (i, 0))
    return pl.pallas_call(body, out_shape=jax.ShapeDtypeStruct((t, d), BF16), grid=(t // tm,),
                          in_specs=[row, row, _const_spec((1, d))], out_specs=row, name=name, compiler_params=_params("parallel"))(y, z, ng)


def _gated_norm_bwd(dyn, y, z, ng, *, name):
    t, d = y.shape
    tm = _tile(t, 256)
    gw = d // SSM_GROUPS

    def body(dyn_ref, y_ref, z_ref, ng_ref, dy_ref, dz_ref, dng_ref):
        @pl.when(pl.program_id(0) == 0)
        def _():
            dng_ref[...] = jnp.zeros_like(dng_ref)
        for g in range(SSM_GROUPS):
            sl = slice(g * gw, (g + 1) * gw)
            zv = z_ref[:, sl].astype(F32)
            yv = y_ref[:, sl]
            sg = _sigmoid(zv)
            sz = zv * sg
            yg = yv * sz
            r = lax.rsqrt(jnp.mean(yg * yg, axis=-1, keepdims=True) + RMS_EPS)
            yhat = yg * r
            dn = dyn_ref[:, sl].astype(F32)
            dyg_n = dn * ng_ref[:, sl]
            dyg = r * (dyg_n - yhat * jnp.mean(dyg_n * yhat, axis=-1, keepdims=True))
            dy_ref[:, sl] = (dyg * sz).astype(dy_ref.dtype)
            dz_ref[:, sl] = (dyg * yv * (sg * (1.0 + zv * (1.0 - sg)))).astype(dz_ref.dtype)
            dng_ref[:, sl] += jnp.sum(dn * yhat, axis=0, keepdims=True)

    row = pl.BlockSpec((tm, d), lambda i: (i, 0))
    shp = jax.ShapeDtypeStruct((t, d), BF16)
    return pl.pallas_call(body, out_shape=(shp, shp, jax.ShapeDtypeStruct((1, d), F32)), grid=(t // tm,),
                          in_specs=[row, row, row, _const_spec((1, d))], out_specs=(row, row, _const_spec((1, d))),
                          name=name, compiler_params=_params("arbitrary"))(dyn, y, z, ng)


def _bucket_onehot():
    blk = CHUNK
    qi = jnp.arange(blk)[:, None]
    kj = jnp.arange(2 * blk)[None, :]
    dist = jnp.maximum(qi + blk - kj, 0)
    max_exact = REL_BUCKETS // 2
    d = jnp.maximum(dist, 1).astype(F32)
    large = max_exact + (jnp.log(d / max_exact) / math.log(REL_MAX_DISTANCE / max_exact) * (REL_BUCKETS - max_exact)).astype(jnp.int32)
    large = jnp.minimum(large, REL_BUCKETS - 1)
    bucket = jnp.where(dist < max_exact, dist, large).reshape(-1)
    return (bucket[None, :] == jnp.arange(REL_BUCKETS)[:, None]).astype(F32)


def _small_mm_hi(a, b, dims, *, name):
    def body(a_ref, b_ref, o_ref):
        o_ref[...] = _dot_hi(a_ref[...], b_ref[...], dims)
    n = b.shape[0] if dims == NT else b.shape[1]
    return pl.pallas_call(body, out_shape=jax.ShapeDtypeStruct((a.shape[0], n), F32), name=name)(a, b)


def _attn_band_mask(n, rep):
    blk = CHUNK
    ii = lax.broadcasted_iota(jnp.int32, (rep * blk, 2 * blk), 0) & (blk - 1)
    jj = lax.broadcasted_iota(jnp.int32, (rep * blk, 2 * blk), 1)
    dist = ii + blk - jj
    in_window = jnp.logical_and(dist >= 0, dist < blk)
    return jnp.logical_and(in_window, jnp.logical_or(jj >= blk, n > 0))


def _attn_fwd(q, kv, bias, sinks, bl, *, name):
    t, qd = q.shape
    blk, hd = CHUNK, ATTN_HEAD_DIM
    kvd = ATTN_KV_HEADS * hd
    rep = ATTN_Q_HEADS // ATTN_KV_HEADS
    nb = t // bl // blk
    scale = hd ** -0.5

    def body(q_ref, kp_ref, kc_ref, vp_ref, vc_ref, bias_ref, sink_ref, o_ref, lse_ref):
        n = pl.program_id(1)
        mask = _attn_band_mask(n, rep)
        lse = jnp.zeros((blk, LANES), F32)
        for kvh in range(ATTN_KV_HEADS):
            ks = slice(kvh * hd, (kvh + 1) * hd)
            heads = range(kvh * rep, (kvh + 1) * rep)
            qs = jnp.concatenate([q_ref[:, h * hd:(h + 1) * hd] for h in heads], axis=0)
            kk = jnp.concatenate([kp_ref[:, ks], kc_ref[:, ks]], axis=0)
            vv = jnp.concatenate([vp_ref[:, ks], vc_ref[:, ks]], axis=0)
            bias = bias_ref[kvh * rep:(kvh + 1) * rep].reshape(rep * blk, 2 * blk)
            s = jnp.where(mask, _dot(qs, kk, NT) * scale + bias, NEG_INF)
            sink = jnp.concatenate([jnp.broadcast_to(sink_ref[:, h:h + 1], (blk, 1)) for h in heads], axis=0)
            m = jnp.maximum(jnp.max(s, axis=1, keepdims=True), sink)
            p = jnp.exp(s - m)
            den = jnp.sum(p, axis=1, keepdims=True) + jnp.exp(sink - m)
            o = _dot((p * (1.0 / den)).astype(BF16), vv, NN)
            lse_s = m + jnp.log(den)
            for r, h in enumerate(heads):
                o_ref[:, h * hd:(h + 1) * hd] = o[r * blk:(r + 1) * blk].astype(o_ref.dtype)
                lse = lse + lse_s[r * blk:(r + 1) * blk] * _lane_onehot(h)
        lse_ref[...] = lse

    def cur(b, n):
        return b * nb + n

    def prev(b, n):
        return b * nb + jnp.maximum(n - 1, 0)
    return pl.pallas_call(
        body, out_shape=(jax.ShapeDtypeStruct((t, qd), BF16), jax.ShapeDtypeStruct((t, LANES), F32)), grid=(bl, nb),
        in_specs=[pl.BlockSpec((blk, qd), lambda b, n: (cur(b, n), 0)),
                  pl.BlockSpec((blk, kvd), lambda b, n: (prev(b, n), 0)), pl.BlockSpec((blk, kvd), lambda b, n: (cur(b, n), 0)),
                  pl.BlockSpec((blk, kvd), lambda b, n: (prev(b, n), 1)), pl.BlockSpec((blk, kvd), lambda b, n: (cur(b, n), 1)),
                  _const_spec(bias.shape), _const_spec((1, LANES))],
        out_specs=(pl.BlockSpec((blk, qd), lambda b, n: (cur(b, n), 0)), pl.BlockSpec((blk, LANES), lambda b, n: (cur(b, n), 0))),
        name=name, compiler_params=_params("parallel", "arbitrary"))(q, kv, kv, kv, kv, bias, sinks)


def _attn_bwd(do, q, kv, lse, bias, sinks, bl, *, name):
    t, qd = q.shape
    blk, hd = CHUNK, ATTN_HEAD_DIM
    kvd = ATTN_KV_HEADS * hd
    rep = ATTN_Q_HEADS // ATTN_KV_HEADS
    s_len = t // bl
    nb = s_len // blk
    scale = hd ** -0.5

    def body(do_ref, q_ref, kp_ref, kc_ref, vp_ref, vc_ref, lse_ref, bias_ref, sink_ref, dq_ref, dkv_ref, dbias_ref, dsink_ref):
        n = pl.program_id(1)

        @pl.when(jnp.logical_and(pl.program_id(0) == 0, n == 0))
        def _():
            dbias_ref[...] = jnp.zeros_like(dbias_ref)
            dsink_ref[...] = jnp.zeros_like(dsink_ref)

        mask = _attn_band_mask(n, rep)
        r_cur = pl.multiple_of(n * blk, blk)
        r_prev = pl.multiple_of(jnp.maximum(n - 1, 0) * blk, blk)
        dsink = jnp.zeros((1, LANES), F32)
        for kvh in range(ATTN_KV_HEADS):
            ks = slice(kvh * hd, (kvh + 1) * hd)
            heads = range(kvh * rep, (kvh + 1) * rep)
            qs = jnp.concatenate([q_ref[:, h * hd:(h + 1) * hd] for h in heads], axis=0)
            dos = jnp.concatenate([do_ref[:, h * hd:(h + 1) * hd] for h in heads], axis=0)
            kk = jnp.concatenate([kp_ref[:, ks], kc_ref[:, ks]], axis=0)
            vv = jnp.concatenate([vp_ref[:, ks], vc_ref[:, ks]], axis=0)
            bias = bias_ref[kvh * rep:(kvh + 1) * rep].reshape(rep * blk, 2 * blk)
            lse = jnp.concatenate([lse_ref[:, h:h + 1] for h in heads], axis=0)
            sink = jnp.concatenate([jnp.broadcast_to(sink_ref[:, h:h + 1], (blk, 1)) for h in heads], axis=0)
            p = jnp.exp(jnp.where(mask, _dot(qs, kk, NT) * scale + bias, NEG_INF) - lse)
            dp = _dot(dos, vv, NT)
            delta = jnp.sum(p * dp, axis=1, keepdims=True)
            ds = p * (dp - delta)
            dsink_rows = jnp.exp(sink - lse) * delta
            ds_b = ds.astype(BF16)
            dq_s = _dot(ds_b, kk, NN) * scale
            dkk = _dot(ds_b, qs, TN) * scale
            dvv = _dot(p.astype(BF16), dos, TN)
            for r, h in enumerate(heads):
                rows = slice(r * blk, (r + 1) * blk)
                dbias_ref[h] += ds[rows]
                dq_ref[:, h * hd:(h + 1) * hd] = dq_s[rows].astype(dq_ref.dtype)
                dsink = dsink - jnp.sum(dsink_rows[rows], axis=0, keepdims=True) * _lane_onehot(h)
            vs = slice(kvd + kvh * hd, kvd + (kvh + 1) * hd)
            dkv_ref[pl.ds(r_cur, blk), ks] = dkk[blk:]
            dkv_ref[pl.ds(r_cur, blk), vs] = dvv[blk:]

            @pl.when(n > 0)
            def _():
                dkv_ref[pl.ds(r_prev, blk), ks] += dkk[:blk]
                dkv_ref[pl.ds(r_prev, blk), vs] += dvv[:blk]
        dsink_ref[...] += dsink

    def cur(b, n):
        return b * nb + n

    def prev(b, n):
        return b * nb + jnp.maximum(n - 1, 0)
    qspec = pl.BlockSpec((blk, qd), lambda b, n: (cur(b, n), 0))
    return pl.pallas_call(
        body,
        out_shape=(jax.ShapeDtypeStruct((t, qd), BF16), jax.ShapeDtypeStruct((t, 2 * kvd), F32),
                   jax.ShapeDtypeStruct(bias.shape, F32), jax.ShapeDtypeStruct((1, LANES), F32)),
        grid=(bl, nb),
        in_specs=[qspec, qspec,
                  pl.BlockSpec((blk, kvd), lambda b, n: (prev(b, n), 0)), pl.BlockSpec((blk, kvd), lambda b, n: (cur(b, n), 0)),
                  pl.BlockSpec((blk, kvd), lambda b, n: (prev(b, n), 1)), pl.BlockSpec((blk, kvd), lambda b, n: (cur(b, n), 1)),
                  pl.BlockSpec((blk, LANES), lambda b, n: (cur(b, n), 0)), _const_spec(bias.shape), _const_spec((1, LANES))],
        out_specs=(qspec, pl.BlockSpec((s_len, 2 * kvd), lambda b, n: (b, 0)), _const_spec(bias.shape), _const_spec((1, LANES))),
        name=name, compiler_params=_params("arbitrary", "arbitrary"))(do, q, kv, kv, kv, kv, lse, bias, sinks)


def _merge_fwd(yn, o, gs, ga, w_ssm, w_attn, w_out, h_in, g_post, *, name):
    t, d = h_in.shape
    tm = _tile(t, 256)

    def body(yn_ref, o_ref, gs_ref, ga_ref, ws_ref, wa_ref, wo_ref, hin_ref, gp_ref, ys_ref, ya_ref, mg_ref, mix_ref, hout_ref):
        ys = _dot(yn_ref[...], ws_ref[...], NN)
        ya = _dot(o_ref[...], wa_ref[...], NN)
        merged = (_sigmoid(gs_ref[...].astype(F32)) * ys + _sigmoid(ga_ref[...].astype(F32)) * ya).astype(BF16)
        mix = _dot(merged, wo_ref[...], NN)
        ys_ref[...] = ys.astype(BF16)
        ya_ref[...] = ya.astype(BF16)
        mg_ref[...] = merged
        mix_ref[...] = mix
        hout_ref[...] = _rms_residual(mix, hin_ref[...], gp_ref[...], 1.0)

    def row(w):
        return pl.BlockSpec((tm, w), lambda i: (i, 0))
    bshape = jax.ShapeDtypeStruct((t, d), BF16)
    fshape = jax.ShapeDtypeStruct((t, d), F32)
    return pl.pallas_call(
        body, out_shape=(bshape, bshape, bshape, fshape, fshape), grid=(t // tm,),
        in_specs=[row(yn.shape[1]), row(o.shape[1]), row(d), row(d), _const_spec(w_ssm.shape), _const_spec(w_attn.shape),
                  _const_spec(w_out.shape), row(d), _const_spec((1, d))],
        out_specs=(row(d),) * 5, name=name, compiler_params=_params("parallel"))(yn, o, gs, ga, w_ssm, w_attn, w_out, h_in, g_post)


def _merge_bwd(dh, mix, g_post, gs, ga, ys, ya, w_ssm, w_attn, w_out, *, name):
    t, d = mix.shape
    tm = _tile(t, 256)
    d_ssm, d_attn = w_ssm.shape[0], w_attn.shape[0]

    def body(dh_ref, mix_ref, gp_ref, gs_ref, ga_ref, ys_ref, ya_ref, ws_ref, wa_ref, wo_ref,
             dmix_ref, dys_ref, dya_ref, dgs_ref, dga_ref, dyn_ref, do_ref, dgp_ref):
        @pl.when(pl.program_id(0) == 0)
        def _():
            dgp_ref[...] = jnp.zeros_like(dgp_ref)
        mv = mix_ref[...]
        dy = dh_ref[...]
        r = lax.rsqrt(jnp.mean(mv * mv, axis=-1, keepdims=True) + RMS_EPS)
        mhat = mv * r
        dyg = dy * gp_ref[...]
        dmix = (r * (dyg - mhat * jnp.mean(dyg * mhat, axis=-1, keepdims=True))).astype(BF16)
        dgp_ref[...] += jnp.sum(dy * mhat, axis=0, keepdims=True)
        dmix_ref[...] = dmix
        dmerged = _dot(dmix, wo_ref[...], NT)
        sgs = _sigmoid(gs_ref[...].astype(F32))
        sga = _sigmoid(ga_ref[...].astype(F32))
        dys = (dmerged * sgs).astype(BF16)
        dya = (dmerged * sga).astype(BF16)
        dys_ref[...] = dys
        dya_ref[...] = dya
        dgs_ref[...] = (dmerged * ys_ref[...].astype(F32) * sgs * (1.0 - sgs)).astype(BF16)
        dga_ref[...] = (dmerged * ya_ref[...].astype(F32) * sga * (1.0 - sga)).astype(BF16)
        dyn_ref[...] = _dot(dys, ws_ref[...], NT).astype(BF16)
        do_ref[...] = _dot(dya, wa_ref[...], NT).astype(BF16)

    def row(w):
        return pl.BlockSpec((tm, w), lambda i: (i, 0))

    def bshape(w):
        return jax.ShapeDtypeStruct((t, w), BF16)
    return pl.pallas_call(
        body, out_shape=(bshape(d),) * 5 + (bshape(d_ssm), bshape(d_attn), jax.ShapeDtypeStruct((1, d), F32)), grid=(t // tm,),
        in_specs=[row(d), row(d), _const_spec((1, d)), row(d), row(d), row(d), row(d),
                  _const_spec(w_ssm.shape), _const_spec(w_attn.shape), _const_spec(w_out.shape)],
        out_specs=(row(d),) * 5 + (row(d_ssm), row(d_attn), _const_spec((1, d))),
        name=name, compiler_params=_params("arbitrary"))(dh, mix, g_post, gs, ga, ys, ya, w_ssm, w_attn, w_out)


def _loss_grad(h, target, *, name):
    t, d = h.shape
    tm = _tile(t, 512)

    def body(h_ref, t_ref, dh_ref, loss_ref):
        @pl.when(pl.program_id(0) == 0)
        def _():
            loss_ref[...] = jnp.zeros_like(loss_ref)
        e = h_ref[...] - t_ref[...]
        dh_ref[...] = e * (1.0 / d)
        per_row = jnp.sum(e * e, axis=1, keepdims=True) * (1.0 / d)
        loss_ref[...] += 0.5 * jnp.sum(per_row, axis=0, keepdims=True)

    row = pl.BlockSpec((tm, d), lambda i: (i, 0))
    return pl.pallas_call(body, out_shape=(jax.ShapeDtypeStruct((t, d), F32), jax.ShapeDtypeStruct((1, LANES), F32)), grid=(t // tm,),
                          in_specs=[row, row], out_specs=(row, _const_spec((1, LANES))),
                          name=name, compiler_params=_params("arbitrary"))(h, target)


def _adamw(w, g, m, v, *, name):
    r, c = w.shape
    tm = _tile(r, 256)
    c1 = 1.0 - ADAM_B1 ** ADAM_STEP
    c2 = 1.0 - ADAM_B2 ** ADAM_STEP

    def body(w_ref, g_ref, m_ref, v_ref, d_ref, mo_ref, vo_ref):
        gv = g_ref[...]
        mn = ADAM_B1 * m_ref[...] + (1.0 - ADAM_B1) * gv
        vn = ADAM_B2 * v_ref[...] + (1.0 - ADAM_B2) * (gv * gv)
        mo_ref[...] = mn
        vo_ref[...] = vn
        d_ref[...] = -ADAM_LR * ((mn / c1) / (jnp.sqrt(vn / c2) + ADAM_EPS) + ADAM_WD * w_ref[...])

    blk = pl.BlockSpec((tm, c), lambda i: (i, 0))
    shp = jax.ShapeDtypeStruct((r, c), F32)
    return pl.pallas_call(body, out_shape=(shp, shp, shp), grid=(r // tm,), in_specs=[blk] * 4, out_specs=(blk,) * 3,
                          name=name, compiler_params=_params("parallel"))(w, g, m, v)


def _position():
    return lax.axis_index("x"), lax.axis_index("y"), lax.axis_index("c")


def _all_gather(shards, *, name):
    na = len(shards)

    def body(*refs):
        ins, outs = refs[:na], refs[na:2 * na]
        send_sems, recv_sems, local_sems = refs[2 * na:]
        x, y, c = _position()
        me, sibling = (x, y, c), (x, y, 1 - c)
        chips = [(1 - x, y), (x, 1 - y), (1 - x, 1 - y)]

        def slot(a, pos):
            return outs[a].at[4 * pos[0] + 2 * pos[1] + pos[2]]

        def copy(a, k, block, to, src=None):
            return pltpu.make_async_remote_copy(
                src_ref=slot(a, block) if src is None else src, dst_ref=slot(a, block),
                send_sem=send_sems.at[a, k], recv_sem=recv_sems.at[a, k], device_id=to, device_id_type=MESH)

        mine = [pltpu.make_async_copy(ins[a], slot(a, me), local_sems.at[a]) for a in range(na)]
        for cp in mine:
            cp.start()
        first = []
        for a in range(na):
            first.append(copy(a, 0, me, sibling, src=ins[a]))
            first += [copy(a, 1 + j, me, (*chip, c), src=ins[a]) for j, chip in enumerate(chips)]
        for cp in first:
            cp.start()
        passed = []
        for a in range(na):
            for j, chip in enumerate(chips):
                copy(a, 1 + j, (*chip, c), me).wait_recv()
                fwd = copy(a, 4 + j, (*chip, c), sibling)
                fwd.start()
                passed.append(fwd)
        for a in range(na):
            copy(a, 0, sibling, me).wait_recv()
            for j, chip in enumerate(chips):
                copy(a, 4 + j, (*chip, 1 - c), me).wait_recv()
        for cp in first + passed:
            cp.wait_send()
        for cp in mine:
            cp.wait()

    hbm = pl.BlockSpec(memory_space=pl.ANY)
    return pl.pallas_call(
        body, out_shape=tuple(jax.ShapeDtypeStruct((N_DEV,) + s.shape, s.dtype) for s in shards),
        in_specs=[hbm] * na, out_specs=tuple([hbm] * na),
        scratch_shapes=[pltpu.SemaphoreType.DMA((na, 7)), pltpu.SemaphoreType.DMA((na, 7)), pltpu.SemaphoreType.DMA((na,))],
        name=name)(*shards)


def _exchange_blocks(arrays, *, name):
    na = len(arrays)

    def body(*refs):
        ins, outs = refs[:na], refs[na:2 * na]
        send_sems, recv_sems = refs[2 * na:]
        x, y, c = _position()
        copies = []
        for a in range(na):
            for k in range(7):
                flip = k + 1
                peer = (x ^ (flip >> 2), y ^ ((flip >> 1) & 1), c ^ (flip & 1))
                peer_block = 4 * peer[0] + 2 * peer[1] + peer[2]
                cp = pltpu.make_async_remote_copy(
                    src_ref=ins[a].at[peer_block], dst_ref=outs[a].at[k],
                    send_sem=send_sems.at[a, k], recv_sem=recv_sems.at[a, k], device_id=peer, device_id_type=MESH)
                cp.start()
                copies.append(cp)
        for cp in copies:
            cp.wait()

    hbm = pl.BlockSpec(memory_space=pl.ANY)
    return pl.pallas_call(
        body, out_shape=tuple(jax.ShapeDtypeStruct((7,) + s.shape[1:], s.dtype) for s in arrays),
        in_specs=[hbm] * na, out_specs=tuple([hbm] * na),
        scratch_shapes=[pltpu.SemaphoreType.DMA((na, 7)), pltpu.SemaphoreType.DMA((na, 7))],
        name=name)(*arrays)


def _reduce_blocks(own, recv, *, name):
    r, c = own.shape
    tm = _tile(r, 256)

    def body(own_ref, recv_ref, o_ref):
        acc = own_ref[...]
        for k in range(7):
            acc = acc + recv_ref[k].astype(F32)
        o_ref[...] = acc

    return pl.pallas_call(
        body, out_shape=jax.ShapeDtypeStruct((r, c), F32), grid=(r // tm,),
        in_specs=[pl.BlockSpec((tm, c), lambda i: (i, 0)), pl.BlockSpec((7, tm, c), lambda i: (0, i, 0))],
        out_specs=pl.BlockSpec((tm, c), lambda i: (i, 0)), name=name, compiler_params=_params("parallel"))(own, recv)


def _all_reduce_small(vec, *, name):
    r, c = vec.shape

    def body(v_ref, o_ref, buf, send_sems, recv_sems):
        x, y, c_ = _position()
        me = 4 * x + 2 * y + c_
        buf[me] = v_ref[...]
        copies = []
        for k in range(7):
            flip = k + 1
            peer = (x ^ (flip >> 2), y ^ ((flip >> 1) & 1), c_ ^ (flip & 1))
            cp = pltpu.make_async_remote_copy(
                src_ref=v_ref, dst_ref=buf.at[me], send_sem=send_sems.at[k], recv_sem=recv_sems.at[k],
                device_id=peer, device_id_type=MESH)
            cp.start()
            copies.append(cp)
        for cp in copies:
            cp.wait()
        acc = buf[0]
        for d in range(1, N_DEV):
            acc = acc + buf[d]
        o_ref[...] = acc

    vm = pl.BlockSpec(memory_space=pltpu.VMEM)
    return pl.pallas_call(
        body, out_shape=jax.ShapeDtypeStruct((r, c), F32), in_specs=[vm], out_specs=vm,
        scratch_shapes=[pltpu.VMEM((N_DEV, r, c), F32), pltpu.SemaphoreType.DMA((7,)), pltpu.SemaphoreType.DMA((7,))],
        name=name)(vec)


def _pad_lanes(v, width=LANES):
    return jnp.pad(v, ((0, 0), (0, width - v.shape[1])))


def kernel(x, ffn1_pre_g, ffn1_w_gate, ffn1_w_up, ffn1_w_down, ffn1_post_g, mix_pre_g, w_in, conv_w, conv_b, dt_bias, a_log, d_skip, ssm_norm_g, w_ssm_proj, attn_sinks, rel_bias_table, w_attn_proj, w_out, mix_post_g, ffn2_pre_g, ffn2_w_gate, ffn2_w_up, ffn2_w_down, ffn2_post_g, loss_target, m_ffn1_pre_g, m_ffn1_w_gate, m_ffn1_w_up, m_ffn1_w_down, m_ffn1_post_g, m_mix_pre_g, m_w_in, m_conv_w, m_conv_b, m_dt_bias, m_a_log, m_d_skip, m_ssm_norm_g, m_w_ssm_proj, m_attn_sinks, m_rel_bias_table, m_w_attn_proj, m_w_out, m_mix_post_g, m_ffn2_pre_g, m_ffn2_w_gate, m_ffn2_w_up, m_ffn2_w_down, m_ffn2_post_g, v_ffn1_pre_g, v_ffn1_w_gate, v_ffn1_w_up, v_ffn1_w_down, v_ffn1_post_g, v_mix_pre_g, v_w_in, v_conv_w, v_conv_b, v_dt_bias, v_a_log, v_d_skip, v_ssm_norm_g, v_w_ssm_proj, v_attn_sinks, v_rel_bias_table, v_w_attn_proj, v_w_out, v_mix_post_g, v_ffn2_pre_g, v_ffn2_w_gate, v_ffn2_w_up, v_ffn2_w_down, v_ffn2_post_g):
    args = dict(locals())
    weight_names = ['ffn1_pre_g', 'ffn1_w_gate', 'ffn1_w_up', 'ffn1_w_down', 'ffn1_post_g', 'mix_pre_g', 'w_in', 'conv_w', 'conv_b',
                    'dt_bias', 'a_log', 'd_skip', 'ssm_norm_g', 'w_ssm_proj', 'attn_sinks', 'rel_bias_table', 'w_attn_proj', 'w_out',
                    'mix_post_g', 'ffn2_pre_g', 'ffn2_w_gate', 'ffn2_w_up', 'ffn2_w_down', 'ffn2_post_g']
    col_sharded = ('ffn1_w_gate', 'ffn1_w_up', 'w_in', 'ffn2_w_gate', 'ffn2_w_up')
    row_sharded = ('ffn1_w_down', 'w_ssm_proj', 'w_attn_proj', 'w_out', 'ffn2_w_down')
    big = col_sharded + row_sharded

    bl, s_len, d = x.shape
    t = bl * s_len
    d_inner = ssm_norm_g.shape[1]
    n_heads = dt_bias.shape[1]
    gn = SSM_GROUPS * SSM_STATE
    conv_dim = d_inner + 2 * gn
    q_dim = ATTN_Q_HEADS * ATTN_HEAD_DIM
    kv_dim = ATTN_KV_HEADS * ATTN_HEAD_DIM

    def local_2d(name, a):
        a = a[0]
        return a.T if name in col_sharded else a

    shards = [local_2d(n, args[n]).astype(BF16) for n in big] + [conv_w[0]]
    gathered = _all_gather(shards, name="gather_weights")
    full = {n: g.reshape(N_DEV * g.shape[1], g.shape[2]) for n, g in zip(big, gathered[:-1])}
    conv_w_full = jnp.transpose(gathered[-1], (1, 0, 2)).reshape(SSM_CONV, conv_dim)

    win_t = full['w_in']
    dt_lo = 2 * d + d_inner + conv_dim
    n_main = win_t.shape[0] - n_heads
    win_p = jnp.concatenate([win_t[:dt_lo], win_t[dt_lo + n_heads:], win_t[dt_lo:dt_lo + n_heads],
                             jnp.zeros((LANES - n_heads, d), BF16)], axis=0)
    off = {'gs': 0, 'ga': d, 'z': 2 * d, 'xbc': 2 * d + d_inner, 'q': dt_lo, 'kv': dt_lo + q_dim, 'dt': n_main}

    x2 = x.reshape(t, d)
    tgt2 = loss_target.reshape(t, d)

    h1, saved1 = _ffn_forward(x2, ffn1_pre_g, full['ffn1_w_gate'], full['ffn1_w_up'], full['ffn1_w_down'], ffn1_post_g, "ffn1")

    u = _rms_fwd(h1, mix_pre_g, name="mix_prenorm")

    def proj(key, width, out_dtype=BF16):
        tn = _tile(math.gcd(width, off[key]) if off[key] else width, 1024, LANES)
        assert off[key] % tn == 0 and width % tn == 0
        return _mm_nt(u, win_p, row_off=off[key] // tn, n_out=width, out_dtype=out_dtype, tn=tn, name=f"proj_{key}")

    gs, ga, z, xbc = proj('gs', d), proj('ga', d), proj('z', d_inner), proj('xbc', conv_dim)
    q, kv = proj('q', q_dim), proj('kv', 2 * kv_dim)
    dt_raw = proj('dt', LANES, F32)

    dtb_p, alog_p, dsk_p, sinks_p = _pad_lanes(dt_bias), _pad_lanes(a_log), _pad_lanes(d_skip), _pad_lanes(attn_sinks)
    xc = _conv_fwd(xbc, conv_w_full, conv_b, bl, name="conv_fwd")
    y, hprev = _ssd_fwd(xc, dt_raw, dtb_p, alog_p, dsk_p, bl, n_heads, name="ssd_fwd")
    yn = _gated_norm_fwd(y, z, ssm_norm_g, name="gated_norm_fwd")

    onehot = _bucket_onehot()
    bias = _small_mm_hi(rel_bias_table.T, onehot, NN, name="rel_bias").reshape(ATTN_Q_HEADS, CHUNK, 2 * CHUNK)
    o, lse = _attn_fwd(q, kv, bias, sinks_p, bl, name="attn_fwd")

    ys, ya, merged, mix, h2 = _merge_fwd(yn, o, gs, ga, full['w_ssm_proj'], full['w_attn_proj'], full['w_out'], h1, mix_post_g,
                                         name="merge_fwd")

    h3, saved2 = _ffn_forward(h2, ffn2_pre_g, full['ffn2_w_gate'], full['ffn2_w_up'], full['ffn2_w_down'], ffn2_post_g, "ffn2")

    dh3, loss_vec = _loss_grad(h3, tgt2, name="loss")
    loss = lax.psum(loss_vec[0, 0], ("x", "y", "c"))

    grads = {}
    dh2, grads['ffn2_pre_g'], d_wgt2, d_wut2, d_wd2, grads['ffn2_post_g'] = _ffn_backward(
        dh3, saved2, ffn2_pre_g, full['ffn2_w_gate'], full['ffn2_w_up'], full['ffn2_w_down'], ffn2_post_g, "ffn2")

    dmix, dys, dya, dgs, dga, dyn, do, grads['mix_post_g'] = _merge_bwd(
        dh2, mix, mix_post_g, gs, ga, ys, ya, full['w_ssm_proj'], full['w_attn_proj'], full['w_out'], name="merge_bwd")
    d_wout = _mm_tn([merged], dmix, name="dw_out")
    d_wssm = _mm_tn([yn], dys, name="dw_ssm")
    d_wattn = _mm_tn([o], dya, name="dw_attn")

    dq, dkv, dbias, dsinks = _attn_bwd(do, q, kv, lse, bias, sinks_p, bl, name="attn_bwd")
    d_table = _small_mm_hi(onehot, dbias.reshape(ATTN_Q_HEADS, -1), NT, name="rel_bias_bwd")

    dy, dz, grads['ssm_norm_g'] = _gated_norm_bwd(dyn, y, z, ssm_norm_g, name="gated_norm_bwd")
    dxc, ddt_raw, ddtb, dalog, ddsk = _ssd_bwd(dy, xc, dt_raw, hprev, dtb_p, alog_p, dsk_p, bl, n_heads, name="ssd_bwd")
    dxbc, dconv_w8, grads['conv_b'] = _conv_bwd(dxc, xbc, conv_w_full, conv_b, bl, name="conv_bwd")

    proj_grads = [dgs, dga, dz, dxbc, dq, dkv]
    d_win_main = _mm_tn(proj_grads, u, name="dw_in")
    d_win_dt = _mm_tn([ddt_raw], u, name="dw_in_dt")
    d_win_t = jnp.concatenate([d_win_main[:dt_lo], d_win_dt[:n_heads], d_win_main[dt_lo:]], axis=0)

    tk = _tile(math.gcd(*[g_.shape[1] for g_ in proj_grads]), 512, LANES)
    segs = [(g_, win_p, tk, off[k_] // tk) for g_, k_ in zip(proj_grads, ('gs', 'ga', 'z', 'xbc', 'q', 'kv'))]
    segs.append((ddt_raw, win_p, LANES, off['dt'] // LANES))
    dh1, grads['mix_pre_g'] = _mm_nn_rmsbwd(segs, h1, mix_pre_g, dh2, name="mix_du")

    dx2, grads['ffn1_pre_g'], d_wgt1, d_wut1, d_wd1, grads['ffn1_post_g'] = _ffn_backward(
        dh1, saved1, ffn1_pre_g, full['ffn1_w_gate'], full['ffn1_w_up'], full['ffn1_w_down'], ffn1_post_g, "ffn1")

    big_grads = {'ffn1_w_gate': d_wgt1, 'ffn1_w_up': d_wut1, 'w_in': d_win_t, 'ffn2_w_gate': d_wgt2, 'ffn2_w_up': d_wut2,
                 'ffn1_w_down': d_wd1, 'w_ssm_proj': d_wssm, 'w_attn_proj': d_wattn, 'w_out': d_wout, 'ffn2_w_down': d_wd2}
    stacked = [big_grads[n].reshape(N_DEV, big_grads[n].shape[0] // N_DEV, big_grads[n].shape[1]) for n in big]
    dconv_w_blocks = jnp.transpose(dconv_w8[:SSM_CONV].reshape(SSM_CONV, N_DEV, conv_dim // N_DEV), (1, 0, 2))
    received = _exchange_blocks([s.astype(BF16) for s in stacked] + [dconv_w_blocks], name="scatter_grads")
    me = 4 * lax.axis_index("x") + 2 * lax.axis_index("y") + lax.axis_index("c")

    def own_block(a):
        return lax.dynamic_index_in_dim(a, me, 0, keepdims=False)
    reduced = {n: _reduce_blocks(own_block(own), rec, name=f"reduce_{n}") for n, own, rec in zip(big, stacked, received[:-1])}
    conv_sum = _reduce_blocks(own_block(dconv_w_blocks), received[-1], name="reduce_conv_w")

    grads['dt_bias'], grads['a_log'], grads['d_skip'] = ddtb[:, :n_heads], dalog[:, :n_heads], ddsk[:, :n_heads]
    grads['attn_sinks'] = dsinks[:, :ATTN_Q_HEADS]
    grads['rel_bias_table'] = d_table
    small = [n for n in weight_names if n not in big and n != 'conv_w']
    flat = jnp.concatenate([grads[n].reshape(-1) for n in small])
    n_small = flat.shape[0]
    rows = -(-n_small // (8 * LANES)) * 8
    flat = jnp.pad(flat, (0, rows * LANES - n_small)).reshape(rows, LANES)
    summed = _all_reduce_small(flat, name="allreduce_small").reshape(-1)
    pos = 0
    for n in small:
        size = grads[n].size
        grads[n] = summed[pos:pos + size].reshape(args[n].shape)
        pos += size

    out_g, out_d, out_m, out_v = {}, {}, {}, {}
    for n in big:
        w2, m2, v2 = local_2d(n, args[n]), local_2d(n, args['m_' + n]), local_2d(n, args['v_' + n])
        dlt, mn, vn = _adamw(w2, reduced[n], m2, v2, name=f"adamw_{n}")

        def back(a, n=n):
            return (a.T if n in col_sharded else a)[None]
        out_g[n], out_d[n], out_m[n], out_v[n] = back(reduced[n]), back(dlt), back(mn), back(vn)

    def pack(prefix):
        vals = [(grads[n] if prefix == 'g' else args[prefix + n]).reshape(-1) for n in small]
        vals.append((conv_sum if prefix == 'g' else args[prefix + 'conv_w']).reshape(-1))
        flat_ = jnp.concatenate(vals)
        rows_ = -(-flat_.shape[0] // (8 * LANES)) * 8
        return jnp.pad(flat_, (0, rows_ * LANES - flat_.shape[0])).reshape(rows_, LANES)

    g_small = pack('g')
    d_small, m_small, v_small = _adamw(pack(''), g_small, pack('m_'), pack('v_'), name="adamw_small")
    pos = 0
    for n in small + ['conv_w']:
        shape = args[n].shape
        size = int(np.prod(shape))
        for dst, src in ((out_g, g_small), (out_d, d_small), (out_m, m_small), (out_v, v_small)):
            dst[n] = src.reshape(-1)[pos:pos + size].reshape(shape)
        pos += size

    grad_x = dx2.reshape(bl, s_len, d)
    return (loss, grad_x, *[out_g[n] for n in weight_names], *[out_d[n] for n in weight_names],
            *[out_m[n] for n in weight_names], *[out_v[n] for n in weight_names])
```

```python
import functools
import math

import numpy as np
import jax
import jax.numpy as jnp
from jax import lax
from jax.experimental import pallas as pl
from jax.experimental.pallas import tpu as pltpu

F32 = jnp.float32
BF16 = jnp.bfloat16
MESH = pl.DeviceIdType.MESH
N_DEV = 8

SSM_HEAD_DIM = 64
SSM_GROUPS = 4
SSM_STATE = 128
SSM_CONV = 4
CHUNK = 128
ATTN_HEAD_DIM = 64
ATTN_Q_HEADS = 16
ATTN_KV_HEADS = 4
REL_BUCKETS = 32
REL_MAX_DISTANCE = 128
RMS_EPS = 1e-6
FFN_RESIDUAL_WEIGHT = 0.5
ADAM_LR, ADAM_B1, ADAM_B2, ADAM_EPS, ADAM_WD, ADAM_STEP = 0.001, 0.9, 0.999, 1e-08, 0.01, 10

LANES = 128
VMEM_LIMIT_BYTES = 56 * 1024 * 1024
FFN_COL_TILE = 1408

NEG_INF = float("-inf")


def _params(*sem):
    return pltpu.CompilerParams(dimension_semantics=sem, vmem_limit_bytes=VMEM_LIMIT_BYTES)


def _tile(n, pref, mult=8):
    if n <= pref:
        return n
    t = (pref // mult) * mult
    while t >= mult:
        if n % t == 0:
            return t
        t -= mult
    return n


def _sigmoid(x):
    return 1.0 / (1.0 + jnp.exp(-x))


def _dot(a, b, dims):
    return lax.dot_general(a, b, (dims, ((), ())), preferred_element_type=F32)


NN = ((1,), (0,))
NT = ((1,), (1,))
TN = ((0,), (0,))


def _dot_hi(a, b, dims=NN):
    return lax.dot_general(a, b, (dims, ((), ())), preferred_element_type=F32, precision=lax.Precision.HIGHEST)


def _const_spec(shape):
    nd = len(shape)
    return pl.BlockSpec(shape, lambda *_: (0,) * nd)


def _resident_spec(shape):
    nd = len(shape)
    return pl.BlockSpec(shape, lambda *_: (0,) * nd, pipeline_mode=pl.Buffered(1))


class _Exchange:
    def __init__(self, arrays, out_shape, scratch, start, finish):
        self.arrays, self.out_shape, self.scratch, self.start, self.finish = arrays, out_shape, scratch, start, finish


def _hosted_call(body, *, grid, in_specs, out_specs, out_shape, scratch_shapes, operands, side, name):
    in_specs, out_specs, out_shape, scratch_shapes = list(in_specs), list(out_specs), list(out_shape), list(scratch_shapes)
    sem = ("arbitrary",) * len(grid)
    if side is None:
        outs = pl.pallas_call(body, out_shape=tuple(out_shape), grid=grid, in_specs=in_specs, out_specs=tuple(out_specs),
                              scratch_shapes=scratch_shapes, name=name, compiler_params=_params(*sem))(*operands)
        return tuple(outs), ()
    n_in, n_out, n_scr = len(in_specs), len(out_shape), len(scratch_shapes)
    s_in, s_out = len(side.arrays), len(side.out_shape)

    def wrapped(*refs):
        refs = list(refs)
        main_in, side_in = refs[:n_in], refs[n_in:n_in + s_in]
        o0 = n_in + s_in
        main_out, side_out = refs[o0:o0 + n_out], refs[o0 + n_out:o0 + n_out + s_out]
        c0 = o0 + n_out + s_out
        main_scr, side_scr = refs[c0:c0 + n_scr], refs[c0 + n_scr:]
        ids = [pl.program_id(ax) for ax in range(len(grid))]
        first = functools.reduce(jnp.logical_and, [i == 0 for i in ids])
        last = functools.reduce(jnp.logical_and, [i == g - 1 for i, g in zip(ids, grid)])

        @pl.when(first)
        def _():
            side.start(side_in, side_out, side_scr)

        body(*main_in, *main_out, *main_scr)

        @pl.when(last)
        def _():
            side.finish(side_in, side_out, side_scr)

    hbm = pl.BlockSpec(memory_space=pl.ANY)
    outs = pl.pallas_call(
        wrapped, out_shape=tuple(out_shape + list(side.out_shape)), grid=grid,
        in_specs=in_specs + [hbm] * s_in, out_specs=tuple(out_specs + [hbm] * s_out),
        scratch_shapes=scratch_shapes + list(side.scratch), name=name, compiler_params=_params(*sem))(*operands, *side.arrays)
    return tuple(outs[:n_out]), tuple(outs[n_out:])


def _mm_nt(a, bt, *, row_off=0, n_out=None, out_dtype=BF16, tm=512, tn=512, name):
    m, k = a.shape
    n_out = bt.shape[0] if n_out is None else n_out
    tm, tn = _tile(m, tm), _tile(n_out, tn, LANES)

    def body(a_ref, b_ref, o_ref):
        o_ref[...] = _dot(a_ref[...].astype(BF16), b_ref[...].astype(BF16), NT).astype(o_ref.dtype)

    return pl.pallas_call(
        body, out_shape=jax.ShapeDtypeStruct((m, n_out), out_dtype), grid=(n_out // tn, m // tm),
        in_specs=[pl.BlockSpec((tm, k), lambda j, i: (i, 0)), pl.BlockSpec((tn, k), lambda j, i: (j + row_off, 0))],
        out_specs=pl.BlockSpec((tm, tn), lambda j, i: (i, j)),
        name=name, compiler_params=_params("parallel", "arbitrary"))(a, bt)


def _mm_tn(a_list, b, *, out_dtype=F32, tm=1408, tk=1024, name):
    t, n = b.shape
    tk = _tile(t, tk)
    nk = t // tk
    widths = [a.shape[1] for a in a_list]
    tm = _tile(math.gcd(*widths), tm, LANES)
    assert all(w % tm == 0 for w in widths)
    starts = np.cumsum([0] + [w // tm for w in widths])
    nseg = len(a_list)

    def a_spec(s):
        lo, hi = int(starts[s]), int(starts[s + 1])

        def idx(i, k):
            active = jnp.logical_and(i >= lo, i < hi)
            return (jnp.where(active, k, 0), jnp.clip(i - lo, 0, hi - lo - 1))
        return pl.BlockSpec((tk, tm), idx)

    def body(*refs):
        a_refs, b_ref, o_ref, acc = refs[:nseg], refs[nseg], refs[nseg + 1], refs[nseg + 2]
        i, k = pl.program_id(0), pl.program_id(1)

        @pl.when(k == 0)
        def _():
            acc[...] = jnp.zeros_like(acc)

        bv = b_ref[...].astype(BF16)
        for s in range(nseg):
            lo, hi = int(starts[s]), int(starts[s + 1])

            @pl.when(jnp.logical_and(i >= lo, i < hi))
            def _(s=s):
                acc[...] += _dot(a_refs[s][...].astype(BF16), bv, TN)

        @pl.when(k == nk - 1)
        def _():
            o_ref[...] = acc[...].astype(o_ref.dtype)

    return pl.pallas_call(
        body, out_shape=jax.ShapeDtypeStruct((int(starts[-1]) * tm, n), out_dtype), grid=(int(starts[-1]), nk),
        in_specs=[a_spec(s) for s in range(nseg)] + [pl.BlockSpec((tk, n), lambda i, k: (k, 0))],
        out_specs=pl.BlockSpec((tm, n), lambda i, k: (i, 0)),
        scratch_shapes=[pltpu.VMEM((tm, n), F32)],
        name=name, compiler_params=_params("parallel", "arbitrary"))(*a_list, b)


def _mm_nn_rmsbwd(segs, weights, x, g, dres, *, tm=256, side=None, name):
    t, d = x.shape
    tm = _tile(t, tm)
    nseg, nw = len(segs), len(weights)

    def body(*refs):
        a_refs, w_refs = refs[:nseg], refs[nseg:nseg + nw]
        x_ref, g_ref, dres_ref, dx_ref, dg_ref = refs[nseg + nw:]

        @pl.when(pl.program_id(0) == 0)
        def _():
            dg_ref[...] = jnp.zeros_like(dg_ref)

        dn = None
        for s, (a, w_idx, row0) in enumerate(segs):
            part = _dot(a_refs[s][...].astype(BF16), w_refs[w_idx][row0:row0 + a.shape[1], :], NN)
            dn = part if dn is None else dn + part
        xv = x_ref[...]
        r = lax.rsqrt(jnp.mean(xv * xv, axis=-1, keepdims=True) + RMS_EPS)
        xhat = xv * r
        dyg = dn * g_ref[...]
        dx_ref[...] = dres_ref[...] + r * (dyg - xhat * jnp.mean(dyg * xhat, axis=-1, keepdims=True))
        dg_ref[...] += jnp.sum(dn * xhat, axis=0, keepdims=True)

    row = pl.BlockSpec((tm, d), lambda i: (i, 0))
    in_specs = [pl.BlockSpec((tm, a.shape[1]), lambda i: (i, 0)) for a, _, _ in segs]
    in_specs += [_resident_spec(w.shape) for w in weights] + [row, _const_spec((1, d)), row]
    (dx, dg), extra = _hosted_call(
        body, grid=(t // tm,), in_specs=in_specs, out_specs=[row, _const_spec((1, d))],
        out_shape=[jax.ShapeDtypeStruct((t, d), F32), jax.ShapeDtypeStruct((1, d), F32)], scratch_shapes=[],
        operands=[a for a, _, _ in segs] + list(weights) + [x, g, dres], side=side, name=name)
    return dx, dg, extra


def _rms_fwd(x, g, *, name):
    t, d = x.shape
    tm = _tile(t, 512)

    def body(x_ref, g_ref, o_ref):
        xv = x_ref[...]
        r = lax.rsqrt(jnp.mean(xv * xv, axis=-1, keepdims=True) + RMS_EPS)
        o_ref[...] = (xv * r * g_ref[...]).astype(o_ref.dtype)

    row = pl.BlockSpec((tm, d), lambda i: (i, 0))
    return pl.pallas_call(body, out_shape=jax.ShapeDtypeStruct((t, d), BF16), grid=(t // tm,),
                          in_specs=[row, _const_spec((1, d))], out_specs=row, name=name, compiler_params=_params("parallel"))(x, g)


def _ffn_up(n, wgt, wut, *, side=None, name):
    t, d = n.shape
    f = wgt.shape[0]
    tm, tn = _tile(t, 512), _tile(f, FFN_COL_TILE, LANES)

    def body(n_ref, wg_ref, wu_ref, g_ref, u_ref, h_ref):
        nv = n_ref[...]
        gv = _dot(nv, wg_ref[...], NT)
        uv = _dot(nv, wu_ref[...], NT)
        g_ref[...] = gv.astype(BF16)
        u_ref[...] = uv.astype(BF16)
        h_ref[...] = (gv * _sigmoid(gv) * uv).astype(BF16)

    w_spec = pl.BlockSpec((tn, d), lambda j, i: (j, 0))
    o_spec = pl.BlockSpec((tm, tn), lambda j, i: (i, j))
    shp = jax.ShapeDtypeStruct((t, f), BF16)
    return _hosted_call(body, grid=(f // tn, t // tm), in_specs=[pl.BlockSpec((tm, d), lambda j, i: (i, 0)), w_spec, w_spec],
                        out_specs=[o_spec, o_spec, o_spec], out_shape=[shp, shp, shp], scratch_shapes=[], operands=[n, wgt, wut],
                        side=side, name=name)


def _rms_residual(acc, h, gp, weight):
    r = lax.rsqrt(jnp.mean(acc * acc, axis=-1, keepdims=True) + RMS_EPS)
    return h + weight * (acc * r * gp)


def _ffn_down(hid, wd, h_in, gp, *, side=None, name):
    t, f = hid.shape
    d = wd.shape[1]
    tm = _tile(t, 256)

    def body(hid_ref, wd_ref, hin_ref, gp_ref, f_ref, hout_ref):
        acc = _dot(hid_ref[...], wd_ref[...], NN)
        f_ref[...] = acc
        hout_ref[...] = _rms_residual(acc, hin_ref[...], gp_ref[...], FFN_RESIDUAL_WEIGHT)

    row = pl.BlockSpec((tm, d), lambda i: (i, 0))
    shp = jax.ShapeDtypeStruct((t, d), F32)
    return _hosted_call(body, grid=(t // tm,),
                        in_specs=[pl.BlockSpec((tm, f), lambda i: (i, 0)), _resident_spec((f, d)), row, _const_spec((1, d))],
                        out_specs=[row, row], out_shape=[shp, shp], scratch_shapes=[], operands=[hid, wd, h_in, gp],
                        side=side, name=name)


def _post_bwd(dh, f, gp, weight, *, name):
    t, d = f.shape
    tm = _tile(t, 512)

    def body(dh_ref, f_ref, gp_ref, df_ref, dgp_ref):
        @pl.when(pl.program_id(0) == 0)
        def _():
            dgp_ref[...] = jnp.zeros_like(dgp_ref)
        fv = f_ref[...]
        dy = weight * dh_ref[...]
        r = lax.rsqrt(jnp.mean(fv * fv, axis=-1, keepdims=True) + RMS_EPS)
        fhat = fv * r
        dyg = dy * gp_ref[...]
        df_ref[...] = (r * (dyg - fhat * jnp.mean(dyg * fhat, axis=-1, keepdims=True))).astype(BF16)
        dgp_ref[...] += jnp.sum(dy * fhat, axis=0, keepdims=True)

    row = pl.BlockSpec((tm, d), lambda i: (i, 0))
    return pl.pallas_call(body, out_shape=(jax.ShapeDtypeStruct((t, d), BF16), jax.ShapeDtypeStruct((1, d), F32)), grid=(t // tm,),
                          in_specs=[row, row, _const_spec((1, d))], out_specs=(row, _const_spec((1, d))),
                          name=name, compiler_params=_params("arbitrary"))(dh, f, gp)


def _ffn_dhid(df, wd, g, u, *, name):
    t, d = df.shape
    f = wd.shape[0]
    tm, tn = _tile(t, 512), _tile(f, FFN_COL_TILE, LANES)

    def body(df_ref, wd_ref, g_ref, u_ref, dg_ref, du_ref):
        dh = _dot(df_ref[...], wd_ref[...], NT)
        gv = g_ref[...].astype(F32)
        uv = u_ref[...].astype(F32)
        sg = _sigmoid(gv)
        dg_ref[...] = (dh * uv * (sg * (1.0 + gv * (1.0 - sg)))).astype(BF16)
        du_ref[...] = (dh * (gv * sg)).astype(BF16)

    o_spec = pl.BlockSpec((tm, tn), lambda j, i: (i, j))
    shp = jax.ShapeDtypeStruct((t, f), BF16)
    return pl.pallas_call(body, out_shape=(shp, shp), grid=(f // tn, t // tm),
                          in_specs=[pl.BlockSpec((tm, d), lambda j, i: (i, 0)), pl.BlockSpec((tn, d), lambda j, i: (j, 0)), o_spec, o_spec],
                          out_specs=(o_spec, o_spec), name=name, compiler_params=_params("parallel", "arbitrary"))(df, wd, g, u)


def _ffn_forward(h_in, g_pre, wgt, wut, wd, g_post, tag, side_up=None, side_down=None):
    n = _rms_fwd(h_in, g_pre, name=f"{tag}_prenorm")
    (g, u, hid), got_up = _ffn_up(n, wgt, wut, side=side_up, name=f"{tag}_up")
    (f, h_out), got_down = _ffn_down(hid, wd, h_in, g_post, side=side_down, name=f"{tag}_down")
    return h_out, (h_in, n, g, u, hid, f), got_up, got_down


def _ffn_backward(dh_out, saved, g_pre, wgt, wut, wd, g_post, tag, make_side=None):
    h_in, n, g, u, hid, f = saved
    df, dg_post = _post_bwd(dh_out, f, g_post, FFN_RESIDUAL_WEIGHT, name=f"{tag}_post_bwd")
    dgate, dup = _ffn_dhid(df, wd, g, u, name=f"{tag}_dhid")
    d_wd = _mm_tn([hid], df, name=f"{tag}_dwd")
    d_wgt = _mm_tn([dgate], n, name=f"{tag}_dwg")
    d_wut = _mm_tn([dup], n, name=f"{tag}_dwu")
    side = None if make_side is None else make_side(d_wgt, d_wut, d_wd)
    dh_in, dg_pre, got = _mm_nn_rmsbwd([(dgate, 0, 0), (dup, 1, 0)], [wgt, wut], h_in, g_pre, dh_out, side=side, name=f"{tag}_dn")
    return dh_in, dg_pre, d_wgt, d_wut, d_wd, dg_post, got


def _shift_down(x, k, rows):
    return jnp.where(rows >= k, pltpu.roll(x, k, axis=0), 0.0)


def _shift_up(x, k, rows, n):
    return jnp.where(rows < n - k, pltpu.roll(x, n - k, axis=0), 0.0)


def _conv_pre(xv, w_ref, b_ref, rows):
    pre = xv * w_ref[SSM_CONV - 1:SSM_CONV, :] + b_ref[...]
    for k in range(1, SSM_CONV):
        pre = pre + _shift_down(xv, k, rows) * w_ref[SSM_CONV - 1 - k:SSM_CONV - k, :]
    return pre


def _conv_fwd(xbc, conv_w, conv_b, bl, *, name):
    t, c = xbc.shape
    s = t // bl
    tc = LANES

    def body(x_ref, w_ref, b_ref, o_ref):
        xv = x_ref[...].astype(F32)
        rows = lax.broadcasted_iota(jnp.int32, xv.shape, 0)
        pre = _conv_pre(xv, w_ref, b_ref, rows)
        o_ref[...] = (pre * _sigmoid(pre)).astype(o_ref.dtype)

    blk = pl.BlockSpec((s, tc), lambda b, j: (b, j))
    return pl.pallas_call(body, out_shape=jax.ShapeDtypeStruct((t, c), BF16), grid=(bl, c // tc),
                          in_specs=[blk, pl.BlockSpec((SSM_CONV, tc), lambda b, j: (0, j)), pl.BlockSpec((1, tc), lambda b, j: (0, j))],
                          out_specs=blk, name=name, compiler_params=_params("parallel", "arbitrary"))(xbc, conv_w, conv_b)


def _conv_bwd(dxc, xbc, conv_w, conv_b, bl, *, name):
    t, c = xbc.shape
    s = t // bl
    tc = LANES

    def body(dy_ref, x_ref, w_ref, b_ref, dx_ref, dw_ref, db_ref):
        @pl.when(pl.program_id(1) == 0)
        def _():
            dw_ref[...] = jnp.zeros_like(dw_ref)
            db_ref[...] = jnp.zeros_like(db_ref)
        xv = x_ref[...].astype(F32)
        rows = lax.broadcasted_iota(jnp.int32, xv.shape, 0)
        pre = _conv_pre(xv, w_ref, b_ref, rows)
        sg = _sigmoid(pre)
        dpre = dy_ref[...].astype(F32) * (sg * (1.0 + pre * (1.0 - sg)))
        dx = dpre * w_ref[SSM_CONV - 1:SSM_CONV, :]
        db_ref[...] += jnp.sum(dpre, axis=0, keepdims=True)
        dw_ref[SSM_CONV - 1:SSM_CONV, :] += jnp.sum(dpre * xv, axis=0, keepdims=True)
        for k in range(1, SSM_CONV):
            dx = dx + _shift_up(dpre, k, rows, s) * w_ref[SSM_CONV - 1 - k:SSM_CONV - k, :]
            dw_ref[SSM_CONV - 1 - k:SSM_CONV - k, :] += jnp.sum(dpre * _shift_down(xv, k, rows), axis=0, keepdims=True)
        dx_ref[...] = dx.astype(dx_ref.dtype)

    blk = pl.BlockSpec((s, tc), lambda j, b: (b, j))
    return pl.pallas_call(
        body, out_shape=(jax.ShapeDtypeStruct((t, c), BF16), jax.ShapeDtypeStruct((8, c), F32), jax.ShapeDtypeStruct((1, c), F32)),
        grid=(c // tc, bl),
        in_specs=[blk, blk, pl.BlockSpec((SSM_CONV, tc), lambda j, b: (0, j)), pl.BlockSpec((1, tc), lambda j, b: (0, j))],
        out_specs=(blk, pl.BlockSpec((8, tc), lambda j, b: (0, j)), pl.BlockSpec((1, tc), lambda j, b: (0, j))),
        name=name, compiler_params=_params("parallel", "arbitrary"))(dxc, xbc, conv_w, conv_b)


def _softplus(x):
    return jnp.maximum(x, 0.0) + jnp.log1p(jnp.exp(-jnp.abs(x)))


def _hilo_dot(v, m_b, dims=NN):
    hi = v.astype(BF16)
    lo = (v - hi.astype(F32)).astype(BF16)
    return _dot(hi, m_b, dims) + _dot(lo, m_b, dims)


def _ssd_chunk_common(dtraw_ref, dtb_ref, alog_ref, dsk_ref, d_inner):
    q, p = CHUNK, SSM_HEAD_DIM
    shift = p.bit_length() - 1
    assert 1 << shift == p
    dt = _softplus(dtraw_ref[...] + dtb_ref[...])
    a = -jnp.exp(alog_ref[...])
    ii = lax.broadcasted_iota(jnp.int32, (q, q), 0)
    jj = lax.broadcasted_iota(jnp.int32, (q, q), 1)
    causal = ii >= jj
    tril = jnp.where(causal, 1.0, 0.0).astype(F32)
    triu = jnp.where(ii <= jj, 1.0, 0.0).astype(F32)
    a_cs = _dot_hi(tril, dt * a)
    a_cs_t = a_cs.T
    a_last = a_cs[q - 1:q, :]
    e_col = jnp.exp(a_cs)
    dec_end = jnp.exp(a_last - a_cs)
    head_of_col = lax.shift_right_logical(lax.broadcasted_iota(jnp.int32, (LANES, d_inner), 1), shift)
    spread = (lax.broadcasted_iota(jnp.int32, (LANES, d_inner), 0) == head_of_col).astype(BF16)
    wide = _hilo_dot(jnp.concatenate([dt, e_col, dec_end, jnp.broadcast_to(dsk_ref[...], (8, LANES))], axis=0), spread)
    return dict(dt=dt, a=a, a_cs=a_cs, a_cs_t=a_cs_t, a_last=a_last, dec_end=dec_end, causal=causal, triu=triu,
                dt_e=wide[:q], e_e=wide[q:2 * q], dec_e=wide[2 * q:3 * q], dsk_e=wide[3 * q:3 * q + 1])


def _fill_block_diag(bd_ref, src_ref, hpg, col0=0):
    q, p = CHUNK, SSM_HEAD_DIM
    for hh in range(hpg):
        bd_ref[hh * q:(hh + 1) * q, hh * p:(hh + 1) * p] = src_ref[:, col0 + hh * p:col0 + (hh + 1) * p]


def _lane_onehot(h):
    return (lax.broadcasted_iota(jnp.int32, (1, LANES), 1) == h).astype(F32)


def _ssd_fwd(xc, dt_raw, dt_bias, a_log, d_skip, bl, n_heads, *, side=None, name):
    t = xc.shape[0]
    q, p, nst, grp = CHUNK, SSM_HEAD_DIM, SSM_STATE, SSM_GROUPS
    d_inner = n_heads * p
    hpg = n_heads // grp
    gw = hpg * p
    nc = t // bl // q
    assert d_inner % (grp * nst) == 0 and nst == LANES

    def body(xs_ref, b_ref, c_ref, dtraw_ref, dtb_ref, alog_ref, dsk_ref, y_ref, hprev_ref, state, m_all, x_bd, xdt_s):
        @pl.when(jnp.logical_and(pl.program_id(0) == 0, pl.program_id(1) == 0))
        def _():
            x_bd[...] = jnp.zeros_like(x_bd)

        @pl.when(pl.program_id(1) == 0)
        def _():
            state[...] = jnp.zeros_like(state)

        cm = _ssd_chunk_common(dtraw_ref, dtb_ref, alog_ref, dsk_ref, d_inner)
        for g in range(grp):
            cols = slice(g * gw, (g + 1) * gw)
            bg = b_ref[:, g * nst:(g + 1) * nst]
            cg = c_ref[:, g * nst:(g + 1) * nst]
            scores = _dot(cg, bg, NT)
            for hh in range(hpg):
                h = g * hpg + hh
                seg = cm['a_cs'][:, h:h + 1] - cm['a_cs_t'][h:h + 1, :]
                m_all[:, hh * q:(hh + 1) * q] = (scores * jnp.exp(jnp.where(cm['causal'], seg, NEG_INF))).astype(BF16)
            xs = xs_ref[:, cols].astype(F32)
            xdt = xs * cm['dt_e'][:, cols]
            xdt_s[...] = xdt.astype(BF16)
            _fill_block_diag(x_bd, xdt_s, hpg)
            hprev = state[g]
            hprev_ref[g] = hprev
            y = _dot(m_all[...], x_bd[...], NN) + cm['e_e'][:, cols] * _dot(cg, hprev.astype(BF16), NT)
            y_ref[:, cols] = y + cm['dsk_e'][:, cols] * xs
            st = _dot((xdt * cm['dec_e'][:, cols]).astype(BF16), bg, TN)
            for hh in range(hpg):
                h = g * hpg + hh
                rows = slice(hh * p, (hh + 1) * p)
                state[g, rows, :] = jnp.exp(cm['a_last'][:, h:h + 1]) * hprev[rows] + st[rows]

    gn = grp * nst

    def rowmap(b, c):
        return b * nc + c
    vec = pl.BlockSpec((1, LANES), lambda b, c: (0, 0))
    return _hosted_call(
        body,
        out_shape=[jax.ShapeDtypeStruct((t, d_inner), F32), jax.ShapeDtypeStruct((t // q, grp, gw, nst), F32)],
        grid=(bl, nc),
        in_specs=[pl.BlockSpec((q, d_inner), lambda b, c: (rowmap(b, c), 0)),
                  pl.BlockSpec((q, gn), lambda b, c: (rowmap(b, c), d_inner // gn)),
                  pl.BlockSpec((q, gn), lambda b, c: (rowmap(b, c), d_inner // gn + 1)),
                  pl.BlockSpec((q, LANES), lambda b, c: (rowmap(b, c), 0)), vec, vec, vec],
        out_specs=[pl.BlockSpec((q, d_inner), lambda b, c: (rowmap(b, c), 0)),
                   pl.BlockSpec((None, grp, gw, nst), lambda b, c: (rowmap(b, c), 0, 0, 0))],
        scratch_shapes=[pltpu.VMEM((grp, gw, nst), F32), pltpu.VMEM((q, hpg * q), BF16), pltpu.VMEM((hpg * q, gw), BF16),
                        pltpu.VMEM((q, gw), BF16)],
        operands=[xc, xc, xc, dt_raw, dt_bias, a_log, d_skip], side=side, name=name)


def _ssd_bwd(dy, xc, dt_raw, hprev_all, dt_bias, a_log, d_skip, bl, n_heads, *, side=None, name):
    t, c_dim = xc.shape
    q, p, nst, grp = CHUNK, SSM_HEAD_DIM, SSM_STATE, SSM_GROUPS
    d_inner = n_heads * p
    hpg = n_heads // grp
    gw = hpg * p
    nc = t // bl // q
    gn = grp * nst
    shift = p.bit_length() - 1

    def body(dy_ref, xs_ref, b_ref, c_ref, dtraw_ref, hprev_ref, dtb_ref, alog_ref, dsk_ref,
             dxc_ref, ddtraw_ref, ddtb_ref, dalog_ref, ddsk_ref, dstate, m_all, mt_all, x_bd, dy_bd, xdt_s):
        @pl.when(jnp.logical_and(pl.program_id(0) == 0, pl.program_id(1) == 0))
        def _():
            ddtb_ref[...] = jnp.zeros_like(ddtb_ref)
            dalog_ref[...] = jnp.zeros_like(dalog_ref)
            ddsk_ref[...] = jnp.zeros_like(ddsk_ref)
            x_bd[...] = jnp.zeros_like(x_bd)
            dy_bd[...] = jnp.zeros_like(dy_bd)

        @pl.when(pl.program_id(1) == 0)
        def _():
            dstate[...] = jnp.zeros_like(dstate)

        cm = _ssd_chunk_common(dtraw_ref, dtb_ref, alog_ref, dsk_ref, d_inner)
        causal = cm['causal']
        upper = cm['triu'] > 0.5
        seg_row = lax.shift_right_logical(lax.broadcasted_iota(jnp.int32, (gw, LANES), 0), shift)
        seg_lane = lax.broadcasted_iota(jnp.int32, (gw, LANES), 1)
        sums = jnp.zeros((5 * q, LANES), F32)
        state_dot = jnp.zeros((1, LANES), F32)
        for g in range(grp):
            cols = slice(g * gw, (g + 1) * gw)
            seg_sum = (seg_row + g * hpg == seg_lane).astype(BF16)
            bg = b_ref[:, g * nst:(g + 1) * nst]
            cg = c_ref[:, g * nst:(g + 1) * nst]
            scores = _dot(cg, bg, NT)
            scores_t = _dot(bg, cg, NT)
            xs = xs_ref[:, cols].astype(F32)
            xdt = xs * cm['dt_e'][:, cols]
            xdt_s[...] = xdt.astype(BF16)
            _fill_block_diag(x_bd, xdt_s, hpg)
            _fill_block_diag(dy_bd, dy_ref, hpg, g * gw)
            dy_b = dy_ref[:, cols]
            dyf = dy_b.astype(F32)
            dm_all = _dot(dy_b, x_bd[...], NT)
            dscores = jnp.zeros((q, q), F32)
            for hh in range(hpg):
                h = g * hpg + hh
                blk = slice(hh * q, (hh + 1) * q)
                seg = cm['a_cs'][:, h:h + 1] - cm['a_cs_t'][h:h + 1, :]
                decay = jnp.exp(jnp.where(causal, seg, NEG_INF))
                m_all[:, blk] = (scores * decay).astype(BF16)
                mt_all[:, blk] = (scores_t * jnp.exp(jnp.where(upper, -seg, NEG_INF))).astype(BF16)
                dscores = dscores + dm_all[:, blk] * decay
            hprev = hprev_ref[g]
            hprev_b = hprev.astype(BF16)
            dhn = dstate[g]
            dhn_b = dhn.astype(BF16)
            e_e, dec_e = cm['e_e'][:, cols], cm['dec_e'][:, cols]
            y_scan = _dot(m_all[...], x_bd[...], NN) + e_e * _dot(cg, hprev_b, NT)
            dye_b = (dyf * e_e).astype(BF16)
            dcg = _dot(dye_b, hprev_b, NN)
            dhp = _dot(dye_b, cg, TN)
            bdh = _dot(bg, dhn_b, NT)
            dbg = _dot((xdt * dec_e).astype(BF16), dhn_b, NN)
            dx_diag = _dot(mt_all[...], dy_bd[...], NN)
            dx = dec_e * bdh + dx_diag
            ds_b = dscores.astype(BF16)
            dcg = dcg + _dot(ds_b, bg, NN)
            dbg = dbg + _dot(ds_b, cg, TN)
            x_rounded = xdt_s[...].astype(F32)
            sums = sums + _hilo_dot(jnp.concatenate([dyf * y_scan, xdt * bdh, x_rounded * dx_diag, dx * xs, dyf * xs], axis=0), seg_sum)
            state_dot = state_dot + jnp.sum(_hilo_dot(dhn * hprev, seg_sum, TN), axis=0, keepdims=True)
            dxc_ref[:, cols] = (dx * cm['dt_e'][:, cols] + cm['dsk_e'][:, cols] * dyf).astype(dxc_ref.dtype)
            dxc_ref[:, d_inner + g * nst:d_inner + (g + 1) * nst] = dbg.astype(dxc_ref.dtype)
            dxc_ref[:, d_inner + gn + g * nst:d_inner + gn + (g + 1) * nst] = dcg.astype(dxc_ref.dtype)
            for hh in range(hpg):
                h = g * hpg + hh
                rows = slice(hh * p, (hh + 1) * p)
                dstate[g, rows, :] = jnp.exp(cm['a_last'][:, h:h + 1]) * dhn[rows] + dhp[rows]
        s_y, s_end, s_diag, s_dt, s_skip = (sums[k * q:(k + 1) * q] for k in range(5))
        dt, a, dec_end = cm['dt'], cm['a'], cm['dec_end']
        last_row = (lax.broadcasted_iota(jnp.int32, (q, 1), 0) == q - 1).astype(F32)
        da_last = jnp.sum(dec_end * s_end, axis=0, keepdims=True) + jnp.exp(cm['a_last']) * state_dot
        da = s_y - dec_end * s_end - s_diag + last_row * da_last
        ddta = _dot_hi(cm['triu'], da)
        ddt = s_dt + ddta * a
        d_a = jnp.sum(ddta * dt, axis=0, keepdims=True)
        ddt_raw = ddt * _sigmoid(dtraw_ref[...] + dtb_ref[...])
        ddtraw_ref[...] = ddt_raw
        ddtb_ref[...] += jnp.sum(ddt_raw, axis=0, keepdims=True)
        dalog_ref[...] += d_a * a
        ddsk_ref[...] += jnp.sum(s_skip, axis=0, keepdims=True)

    def rowmap(b, c):
        return b * nc + (nc - 1 - c)
    vec = pl.BlockSpec((1, LANES), lambda b, c: (0, 0))
    vec_shape = jax.ShapeDtypeStruct((1, LANES), F32)
    return _hosted_call(
        body,
        out_shape=[jax.ShapeDtypeStruct((t, c_dim), BF16), jax.ShapeDtypeStruct((t, LANES), F32), vec_shape, vec_shape, vec_shape],
        grid=(bl, nc),
        in_specs=[pl.BlockSpec((q, d_inner), lambda b, c: (rowmap(b, c), 0)),
                  pl.BlockSpec((q, d_inner), lambda b, c: (rowmap(b, c), 0)),
                  pl.BlockSpec((q, gn), lambda b, c: (rowmap(b, c), d_inner // gn)),
                  pl.BlockSpec((q, gn), lambda b, c: (rowmap(b, c), d_inner // gn + 1)),
                  pl.BlockSpec((q, LANES), lambda b, c: (rowmap(b, c), 0)),
                  pl.BlockSpec((None, grp, gw, nst), lambda b, c: (rowmap(b, c), 0, 0, 0)), vec, vec, vec],
        out_specs=[pl.BlockSpec((q, c_dim), lambda b, c: (rowmap(b, c), 0)),
                   pl.BlockSpec((q, LANES), lambda b, c: (rowmap(b, c), 0)), vec, vec, vec],
        scratch_shapes=[pltpu.VMEM((grp, gw, nst), F32), pltpu.VMEM((q, hpg * q), BF16), pltpu.VMEM((q, hpg * q), BF16),
                        pltpu.VMEM((hpg * q, gw), BF16), pltpu.VMEM((hpg * q, gw), BF16), pltpu.VMEM((q, gw), BF16)],
        operands=[dy, xc, xc, xc, dt_raw, hprev_all, dt_bias, a_log, d_skip], side=side, name=name)


def _gated_norm_fwd(y, z, ng, *, name):
    t, d = y.shape
    tm = _tile(t, 256)
    gw = d // SSM_GROUPS

    def body(y_ref, z_ref, ng_ref, o_ref):
        for g in range(SSM_GROUPS):
            sl = slice(g * gw, (g + 1) * gw)
            zv = z_ref[:, sl].astype(F32)
            yg = y_ref[:, sl] * (zv * _sigmoid(zv))
            r = lax.rsqrt(jnp.mean(yg * yg, axis=-1, keepdims=True) + RMS_EPS)
            o_ref[:, sl] = (yg * r * ng_ref[:, sl]).astype(o_ref.dtype)

    row = pl.BlockSpec((tm, d), lambda i: (i, 0))
    return pl.pallas_call(body, out_shape=jax.ShapeDtypeStruct((t, d), BF16), grid=(t // tm,),
                          in_specs=[row, row, _const_spec((1, d))], out_specs=row, name=name, compiler_params=_params("parallel"))(y, z, ng)


def _gated_norm_bwd(dyn, y, z, ng, *, name):
    t, d = y.shape
    tm = _tile(t, 256)
    gw = d // SSM_GROUPS

    def body(dyn_ref, y_ref, z_ref, ng_ref, dy_ref, dz_ref, dng_ref):
        @pl.when(pl.program_id(0) == 0)
        def _():
            dng_ref[...] = jnp.zeros_like(dng_ref)
        for g in range(SSM_GROUPS):
            sl = slice(g * gw, (g + 1) * gw)
            zv = z_ref[:, sl].astype(F32)
            yv = y_ref[:, sl]
            sg = _sigmoid(zv)
            sz = zv * sg
            yg = yv * sz
            r = lax.rsqrt(jnp.mean(yg * yg, axis=-1, keepdims=True) + RMS_EPS)
            yhat = yg * r
            dn = dyn_ref[:, sl].astype(F32)
            dyg_n = dn * ng_ref[:, sl]
            dyg = r * (dyg_n - yhat * jnp.mean(dyg_n * yhat, axis=-1, keepdims=True))
            dy_ref[:, sl] = (dyg * sz).astype(dy_ref.dtype)
            dz_ref[:, sl] = (dyg * yv * (sg * (1.0 + zv * (1.0 - sg)))).astype(dz_ref.dtype)
            dng_ref[:, sl] += jnp.sum(dn * yhat, axis=0, keepdims=True)

    row = pl.BlockSpec((tm, d), lambda i: (i, 0))
    shp = jax.ShapeDtypeStruct((t, d), BF16)
    return pl.pallas_call(body, out_shape=(shp, shp, jax.ShapeDtypeStruct((1, d), F32)), grid=(t // tm,),
                          in_specs=[row, row, row, _const_spec((1, d))], out_specs=(row, row, _const_spec((1, d))),
                          name=name, compiler_params=_params("arbitrary"))(dyn, y, z, ng)


def _bucket_onehot():
    blk = CHUNK
    qi = jnp.arange(blk)[:, None]
    kj = jnp.arange(2 * blk)[None, :]
    dist = jnp.maximum(qi + blk - kj, 0)
    max_exact = REL_BUCKETS // 2
    d = jnp.maximum(dist, 1).astype(F32)
    large = max_exact + (jnp.log(d / max_exact) / math.log(REL_MAX_DISTANCE / max_exact) * (REL_BUCKETS - max_exact)).astype(jnp.int32)
    large = jnp.minimum(large, REL_BUCKETS - 1)
    bucket = jnp.where(dist < max_exact, dist, large).reshape(-1)
    return (bucket[None, :] == jnp.arange(REL_BUCKETS)[:, None]).astype(F32)


def _small_mm_hi(a, b, dims, *, name):
    def body(a_ref, b_ref, o_ref):
        o_ref[...] = _dot_hi(a_ref[...], b_ref[...], dims)
    n = b.shape[0] if dims == NT else b.shape[1]
    return pl.pallas_call(body, out_shape=jax.ShapeDtypeStruct((a.shape[0], n), F32), name=name)(a, b)


def _attn_band_mask(n, rep):
    blk = CHUNK
    ii = lax.broadcasted_iota(jnp.int32, (rep * blk, 2 * blk), 0) & (blk - 1)
    jj = lax.broadcasted_iota(jnp.int32, (rep * blk, 2 * blk), 1)
    dist = ii + blk - jj
    in_window = jnp.logical_and(dist >= 0, dist < blk)
    return jnp.logical_and(in_window, jnp.logical_or(jj >= blk, n > 0))


def _attn_fwd(q, kv, bias, sinks, bl, *, name):
    t, qd = q.shape
    blk, hd = CHUNK, ATTN_HEAD_DIM
    kvd = ATTN_KV_HEADS * hd
    rep = ATTN_Q_HEADS // ATTN_KV_HEADS
    nb = t // bl // blk
    scale = hd ** -0.5

    def body(q_ref, kp_ref, kc_ref, vp_ref, vc_ref, bias_ref, sink_ref, o_ref, lse_ref):
        n = pl.program_id(1)
        mask = _attn_band_mask(n, rep)
        lse = jnp.zeros((blk, LANES), F32)
        for kvh in range(ATTN_KV_HEADS):
            ks = slice(kvh * hd, (kvh + 1) * hd)
            heads = range(kvh * rep, (kvh + 1) * rep)
            qs = jnp.concatenate([q_ref[:, h * hd:(h + 1) * hd] for h in heads], axis=0)
            kk = jnp.concatenate([kp_ref[:, ks], kc_ref[:, ks]], axis=0)
            vv = jnp.concatenate([vp_ref[:, ks], vc_ref[:, ks]], axis=0)
            bias = bias_ref[kvh * rep:(kvh + 1) * rep].reshape(rep * blk, 2 * blk)
            s = jnp.where(mask, _dot(qs, kk, NT) * scale + bias, NEG_INF)
            sink = jnp.concatenate([jnp.broadcast_to(sink_ref[:, h:h + 1], (blk, 1)) for h in heads], axis=0)
            m = jnp.maximum(jnp.max(s, axis=1, keepdims=True), sink)
            p = jnp.exp(s - m)
            den = jnp.sum(p, axis=1, keepdims=True) + jnp.exp(sink - m)
            o = _dot((p * (1.0 / den)).astype(BF16), vv, NN)
            lse_s = m + jnp.log(den)
            for r, h in enumerate(heads):
                o_ref[:, h * hd:(h + 1) * hd] = o[r * blk:(r + 1) * blk].astype(o_ref.dtype)
                lse = lse + lse_s[r * blk:(r + 1) * blk] * _lane_onehot(h)
        lse_ref[...] = lse

    def cur(b, n):
        return b * nb + n

    def prev(b, n):
        return b * nb + jnp.maximum(n - 1, 0)
    return pl.pallas_call(
        body, out_shape=(jax.ShapeDtypeStruct((t, qd), BF16), jax.ShapeDtypeStruct((t, LANES), F32)), grid=(bl, nb),
        in_specs=[pl.BlockSpec((blk, qd), lambda b, n: (cur(b, n), 0)),
                  pl.BlockSpec((blk, kvd), lambda b, n: (prev(b, n), 0)), pl.BlockSpec((blk, kvd), lambda b, n: (cur(b, n), 0)),
                  pl.BlockSpec((blk, kvd), lambda b, n: (prev(b, n), 1)), pl.BlockSpec((blk, kvd), lambda b, n: (cur(b, n), 1)),
                  _const_spec(bias.shape), _const_spec((1, LANES))],
        out_specs=(pl.BlockSpec((blk, qd), lambda b, n: (cur(b, n), 0)), pl.BlockSpec((blk, LANES), lambda b, n: (cur(b, n), 0))),
        name=name, compiler_params=_params("parallel", "arbitrary"))(q, kv, kv, kv, kv, bias, sinks)


def _attn_bwd(do, q, kv, lse, bias, sinks, bl, *, name):
    t, qd = q.shape
    blk, hd = CHUNK, ATTN_HEAD_DIM
    kvd = ATTN_KV_HEADS * hd
    rep = ATTN_Q_HEADS // ATTN_KV_HEADS
    s_len = t // bl
    nb = s_len // blk
    scale = hd ** -0.5

    def body(do_ref, q_ref, kp_ref, kc_ref, vp_ref, vc_ref, lse_ref, bias_ref, sink_ref, dq_ref, dkv_ref, dbias_ref, dsink_ref):
        n = pl.program_id(1)

        @pl.when(jnp.logical_and(pl.program_id(0) == 0, n == 0))
        def _():
            dbias_ref[...] = jnp.zeros_like(dbias_ref)
            dsink_ref[...] = jnp.zeros_like(dsink_ref)

        mask = _attn_band_mask(n, rep)
        r_cur = pl.multiple_of(n * blk, blk)
        r_prev = pl.multiple_of(jnp.maximum(n - 1, 0) * blk, blk)
        dsink = jnp.zeros((1, LANES), F32)
        for kvh in range(ATTN_KV_HEADS):
            ks = slice(kvh * hd, (kvh + 1) * hd)
            heads = range(kvh * rep, (kvh + 1) * rep)
            qs = jnp.concatenate([q_ref[:, h * hd:(h + 1) * hd] for h in heads], axis=0)
            dos = jnp.concatenate([do_ref[:, h * hd:(h + 1) * hd] for h in heads], axis=0)
            kk = jnp.concatenate([kp_ref[:, ks], kc_ref[:, ks]], axis=0)
            vv = jnp.concatenate([vp_ref[:, ks], vc_ref[:, ks]], axis=0)
            bias = bias_ref[kvh * rep:(kvh + 1) * rep].reshape(rep * blk, 2 * blk)
            lse = jnp.concatenate([lse_ref[:, h:h + 1] for h in heads], axis=0)
            sink = jnp.concatenate([jnp.broadcast_to(sink_ref[:, h:h + 1], (blk, 1)) for h in heads], axis=0)
            p = jnp.exp(jnp.where(mask, _dot(qs, kk, NT) * scale + bias, NEG_INF) - lse)
            dp = _dot(dos, vv, NT)
            delta = jnp.sum(p * dp, axis=1, keepdims=True)
            ds = p * (dp - delta)
            dsink_rows = jnp.exp(sink - lse) * delta
            ds_b = ds.astype(BF16)
            dq_s = _dot(ds_b, kk, NN) * scale
            dkk = _dot(ds_b, qs, TN) * scale
            dvv = _dot(p.astype(BF16), dos, TN)
            for r, h in enumerate(heads):
                rows = slice(r * blk, (r + 1) * blk)
                dbias_ref[h] += ds[rows]
                dq_ref[:, h * hd:(h + 1) * hd] = dq_s[rows].astype(dq_ref.dtype)
                dsink = dsink - jnp.sum(dsink_rows[rows], axis=0, keepdims=True) * _lane_onehot(h)
            vs = slice(kvd + kvh * hd, kvd + (kvh + 1) * hd)
            dkv_ref[pl.ds(r_cur, blk), ks] = dkk[blk:]
            dkv_ref[pl.ds(r_cur, blk), vs] = dvv[blk:]

            @pl.when(n > 0)
            def _():
                dkv_ref[pl.ds(r_prev, blk), ks] += dkk[:blk]
                dkv_ref[pl.ds(r_prev, blk), vs] += dvv[:blk]
        dsink_ref[...] += dsink

    def cur(b, n):
        return b * nb + n

    def prev(b, n):
        return b * nb + jnp.maximum(n - 1, 0)
    qspec = pl.BlockSpec((blk, qd), lambda b, n: (cur(b, n), 0))
    return pl.pallas_call(
        body,
        out_shape=(jax.ShapeDtypeStruct((t, qd), BF16), jax.ShapeDtypeStruct((t, 2 * kvd), F32),
                   jax.ShapeDtypeStruct(bias.shape, F32), jax.ShapeDtypeStruct((1, LANES), F32)),
        grid=(bl, nb),
        in_specs=[qspec, qspec,
                  pl.BlockSpec((blk, kvd), lambda b, n: (prev(b, n), 0)), pl.BlockSpec((blk, kvd), lambda b, n: (cur(b, n), 0)),
                  pl.BlockSpec((blk, kvd), lambda b, n: (prev(b, n), 1)), pl.BlockSpec((blk, kvd), lambda b, n: (cur(b, n), 1)),
                  pl.BlockSpec((blk, LANES), lambda b, n: (cur(b, n), 0)), _const_spec(bias.shape), _const_spec((1, LANES))],
        out_specs=(qspec, pl.BlockSpec((s_len, 2 * kvd), lambda b, n: (b, 0)), _const_spec(bias.shape), _const_spec((1, LANES))),
        name=name, compiler_params=_params("arbitrary", "arbitrary"))(do, q, kv, kv, kv, kv, lse, bias, sinks)


def _merge_fwd(yn, o, gs, ga, w_ssm, w_attn, w_out, h_in, g_post, *, name):
    t, d = h_in.shape
    tm = _tile(t, 256)

    def body(yn_ref, o_ref, gs_ref, ga_ref, ws_ref, wa_ref, wo_ref, hin_ref, gp_ref, ys_ref, ya_ref, mg_ref, mix_ref, hout_ref):
        ys = _dot(yn_ref[...], ws_ref[...], NN)
        ya = _dot(o_ref[...], wa_ref[...], NN)
        merged = (_sigmoid(gs_ref[...].astype(F32)) * ys + _sigmoid(ga_ref[...].astype(F32)) * ya).astype(BF16)
        mix = _dot(merged, wo_ref[...], NN)
        ys_ref[...] = ys.astype(BF16)
        ya_ref[...] = ya.astype(BF16)
        mg_ref[...] = merged
        mix_ref[...] = mix
        hout_ref[...] = _rms_residual(mix, hin_ref[...], gp_ref[...], 1.0)

    def row(w):
        return pl.BlockSpec((tm, w), lambda i: (i, 0))
    bshape = jax.ShapeDtypeStruct((t, d), BF16)
    fshape = jax.ShapeDtypeStruct((t, d), F32)
    return pl.pallas_call(
        body, out_shape=(bshape, bshape, bshape, fshape, fshape), grid=(t // tm,),
        in_specs=[row(yn.shape[1]), row(o.shape[1]), row(d), row(d), _const_spec(w_ssm.shape), _const_spec(w_attn.shape),
                  _const_spec(w_out.shape), row(d), _const_spec((1, d))],
        out_specs=(row(d),) * 5, name=name, compiler_params=_params("parallel"))(yn, o, gs, ga, w_ssm, w_attn, w_out, h_in, g_post)


def _merge_bwd(dh, mix, g_post, gs, ga, ys, ya, w_ssm, w_attn, w_out, *, name):
    t, d = mix.shape
    tm = _tile(t, 256)
    d_ssm, d_attn = w_ssm.shape[0], w_attn.shape[0]

    def body(dh_ref, mix_ref, gp_ref, gs_ref, ga_ref, ys_ref, ya_ref, ws_ref, wa_ref, wo_ref,
             dmix_ref, dys_ref, dya_ref, dgs_ref, dga_ref, dyn_ref, do_ref, dgp_ref):
        @pl.when(pl.program_id(0) == 0)
        def _():
            dgp_ref[...] = jnp.zeros_like(dgp_ref)
        mv = mix_ref[...]
        dy = dh_ref[...]
        r = lax.rsqrt(jnp.mean(mv * mv, axis=-1, keepdims=True) + RMS_EPS)
        mhat = mv * r
        dyg = dy * gp_ref[...]
        dmix = (r * (dyg - mhat * jnp.mean(dyg * mhat, axis=-1, keepdims=True))).astype(BF16)
        dgp_ref[...] += jnp.sum(dy * mhat, axis=0, keepdims=True)
        dmix_ref[...] = dmix
        dmerged = _dot(dmix, wo_ref[...], NT)
        sgs = _sigmoid(gs_ref[...].astype(F32))
        sga = _sigmoid(ga_ref[...].astype(F32))
        dys = (dmerged * sgs).astype(BF16)
        dya = (dmerged * sga).astype(BF16)
        dys_ref[...] = dys
        dya_ref[...] = dya
        dgs_ref[...] = (dmerged * ys_ref[...].astype(F32) * sgs * (1.0 - sgs)).astype(BF16)
        dga_ref[...] = (dmerged * ya_ref[...].astype(F32) * sga * (1.0 - sga)).astype(BF16)
        dyn_ref[...] = _dot(dys, ws_ref[...], NT).astype(BF16)
        do_ref[...] = _dot(dya, wa_ref[...], NT).astype(BF16)

    def row(w):
        return pl.BlockSpec((tm, w), lambda i: (i, 0))

    def bshape(w):
        return jax.ShapeDtypeStruct((t, w), BF16)
    return pl.pallas_call(
        body, out_shape=(bshape(d),) * 5 + (bshape(d_ssm), bshape(d_attn), jax.ShapeDtypeStruct((1, d), F32)), grid=(t // tm,),
        in_specs=[row(d), row(d), _const_spec((1, d)), row(d), row(d), row(d), row(d),
                  _const_spec(w_ssm.shape), _const_spec(w_attn.shape), _const_spec(w_out.shape)],
        out_specs=(row(d),) * 5 + (row(d_ssm), row(d_attn), _const_spec((1, d))),
        name=name, compiler_params=_params("arbitrary"))(dh, mix, g_post, gs, ga, ys, ya, w_ssm, w_attn, w_out)


def _loss_grad(h, target, *, name):
    t, d = h.shape
    tm = _tile(t, 512)

    def body(h_ref, t_ref, dh_ref, loss_ref):
        @pl.when(pl.program_id(0) == 0)
        def _():
            loss_ref[...] = jnp.zeros_like(loss_ref)
        e = h_ref[...] - t_ref[...]
        dh_ref[...] = e * (1.0 / d)
        per_row = jnp.sum(e * e, axis=1, keepdims=True) * (1.0 / d)
        loss_ref[...] += 0.5 * jnp.sum(per_row, axis=0, keepdims=True)

    row = pl.BlockSpec((tm, d), lambda i: (i, 0))
    return pl.pallas_call(body, out_shape=(jax.ShapeDtypeStruct((t, d), F32), jax.ShapeDtypeStruct((1, LANES), F32)), grid=(t // tm,),
                          in_specs=[row, row], out_specs=(row, _const_spec((1, LANES))),
                          name=name, compiler_params=_params("arbitrary"))(h, target)


def _adamw(w, g, m, v, *, name):
    r, c = w.shape
    tm = _tile(r, 256)
    c1 = 1.0 - ADAM_B1 ** ADAM_STEP
    c2 = 1.0 - ADAM_B2 ** ADAM_STEP

    def body(w_ref, g_ref, m_ref, v_ref, d_ref, mo_ref, vo_ref):
        gv = g_ref[...]
        mn = ADAM_B1 * m_ref[...] + (1.0 - ADAM_B1) * gv
        vn = ADAM_B2 * v_ref[...] + (1.0 - ADAM_B2) * (gv * gv)
        mo_ref[...] = mn
        vo_ref[...] = vn
        d_ref[...] = -ADAM_LR * ((mn / c1) / (jnp.sqrt(vn / c2) + ADAM_EPS) + ADAM_WD * w_ref[...])

    blk = pl.BlockSpec((tm, c), lambda i: (i, 0))
    shp = jax.ShapeDtypeStruct((r, c), F32)
    return pl.pallas_call(body, out_shape=(shp, shp, shp), grid=(r // tm,), in_specs=[blk] * 4, out_specs=(blk,) * 3,
                          name=name, compiler_params=_params("parallel"))(w, g, m, v)


def _position():
    return lax.axis_index("x"), lax.axis_index("y"), lax.axis_index("c")


def _gather_exchange(shards):
    na = len(shards)

    def plan(ins, outs, sems):
        send_sems, recv_sems, local_sems = sems
        x, y, c = _position()
        me, sibling = (x, y, c), (x, y, 1 - c)
        chips = [(1 - x, y), (x, 1 - y), (1 - x, 1 - y)]

        def slot(a, pos):
            return outs[a].at[4 * pos[0] + 2 * pos[1] + pos[2]]

        def copy(a, k, block, to, src=None):
            return pltpu.make_async_remote_copy(
                src_ref=slot(a, block) if src is None else src, dst_ref=slot(a, block),
                send_sem=send_sems.at[a, k], recv_sem=recv_sems.at[a, k], device_id=to, device_id_type=MESH)

        mine = [pltpu.make_async_copy(ins[a], slot(a, me), local_sems.at[a]) for a in range(na)]
        first = []
        for a in range(na):
            first.append(copy(a, 0, me, sibling, src=ins[a]))
            first += [copy(a, 1 + j, me, (*chip, c), src=ins[a]) for j, chip in enumerate(chips)]
        return me, sibling, chips, copy, mine, first

    def start(ins, outs, sems):
        *_, mine, first = plan(ins, outs, sems)
        for cp in mine + first:
            cp.start()

    def finish(ins, outs, sems):
        me, sibling, chips, copy, mine, first = plan(ins, outs, sems)
        c = me[2]
        passed = []
        for a in range(na):
            for j, chip in enumerate(chips):
                copy(a, 1 + j, (*chip, c), me).wait_recv()
                fwd = copy(a, 4 + j, (*chip, c), sibling)
                fwd.start()
                passed.append(fwd)
        for a in range(na):
            copy(a, 0, sibling, me).wait_recv()
            for j, chip in enumerate(chips):
                copy(a, 4 + j, (*chip, 1 - c), me).wait_recv()
        for cp in first + passed:
            cp.wait_send()
        for cp in mine:
            cp.wait()

    return _Exchange(list(shards), [jax.ShapeDtypeStruct((N_DEV,) + s.shape, s.dtype) for s in shards],
                     [pltpu.SemaphoreType.DMA((na, 7)), pltpu.SemaphoreType.DMA((na, 7)), pltpu.SemaphoreType.DMA((na,))],
                     start, finish)


def _scatter_exchange(arrays):
    na = len(arrays)

    def copies(ins, outs, sems):
        send_sems, recv_sems = sems
        x, y, c = _position()
        out = []
        for a in range(na):
            for k in range(7):
                flip = k + 1
                peer = (x ^ (flip >> 2), y ^ ((flip >> 1) & 1), c ^ (flip & 1))
                peer_block = 4 * peer[0] + 2 * peer[1] + peer[2]
                out.append(pltpu.make_async_remote_copy(
                    src_ref=ins[a].at[peer_block], dst_ref=outs[a].at[k],
                    send_sem=send_sems.at[a, k], recv_sem=recv_sems.at[a, k], device_id=peer, device_id_type=MESH))
        return out

    def start(ins, outs, sems):
        for cp in copies(ins, outs, sems):
            cp.start()

    def finish(ins, outs, sems):
        for cp in copies(ins, outs, sems):
            cp.wait()

    return _Exchange(list(arrays), [jax.ShapeDtypeStruct((7,) + s.shape[1:], s.dtype) for s in arrays],
                     [pltpu.SemaphoreType.DMA((na, 7)), pltpu.SemaphoreType.DMA((na, 7))], start, finish)


def _exchange_alone(side, *, name):
    n_in = len(side.arrays)
    n_out = len(side.out_shape)

    def body(*refs):
        ins, outs, sems = refs[:n_in], refs[n_in:n_in + n_out], refs[n_in + n_out:]
        side.start(ins, outs, sems)
        side.finish(ins, outs, sems)

    hbm = pl.BlockSpec(memory_space=pl.ANY)
    return pl.pallas_call(body, out_shape=tuple(side.out_shape), in_specs=[hbm] * n_in, out_specs=tuple([hbm] * n_out),
                          scratch_shapes=list(side.scratch), name=name)(*side.arrays)


def _reduce_blocks(own, recv, *, name):
    r, c = own.shape
    tm = _tile(r, 256)

    def body(own_ref, recv_ref, o_ref):
        acc = own_ref[...]
        for k in range(7):
            acc = acc + recv_ref[k].astype(F32)
        o_ref[...] = acc

    return pl.pallas_call(
        body, out_shape=jax.ShapeDtypeStruct((r, c), F32), grid=(r // tm,),
        in_specs=[pl.BlockSpec((tm, c), lambda i: (i, 0)), pl.BlockSpec((7, tm, c), lambda i: (0, i, 0))],
        out_specs=pl.BlockSpec((tm, c), lambda i: (i, 0)), name=name, compiler_params=_params("parallel"))(own, recv)


def _all_reduce_small(vec, *, name):
    r, c = vec.shape

    def body(v_ref, o_ref, buf, send_sems, recv_sems):
        x, y, c_ = _position()
        me = 4 * x + 2 * y + c_
        buf[me] = v_ref[...]
        copies = []
        for k in range(7):
            flip = k + 1
            peer = (x ^ (flip >> 2), y ^ ((flip >> 1) & 1), c_ ^ (flip & 1))
            cp = pltpu.make_async_remote_copy(
                src_ref=v_ref, dst_ref=buf.at[me], send_sem=send_sems.at[k], recv_sem=recv_sems.at[k],
                device_id=peer, device_id_type=MESH)
            cp.start()
            copies.append(cp)
        for cp in copies:
            cp.wait()
        acc = buf[0]
        for d in range(1, N_DEV):
            acc = acc + buf[d]
        o_ref[...] = acc

    vm = pl.BlockSpec(memory_space=pltpu.VMEM)
    return pl.pallas_call(
        body, out_shape=jax.ShapeDtypeStruct((r, c), F32), in_specs=[vm], out_specs=vm,
        scratch_shapes=[pltpu.VMEM((N_DEV, r, c), F32), pltpu.SemaphoreType.DMA((7,)), pltpu.SemaphoreType.DMA((7,))],
        name=name)(vec)


def _pad_lanes(v, width=LANES):
    return jnp.pad(v, ((0, 0), (0, width - v.shape[1])))


def kernel(x, ffn1_pre_g, ffn1_w_gate, ffn1_w_up, ffn1_w_down, ffn1_post_g, mix_pre_g, w_in, conv_w, conv_b, dt_bias, a_log, d_skip, ssm_norm_g, w_ssm_proj, attn_sinks, rel_bias_table, w_attn_proj, w_out, mix_post_g, ffn2_pre_g, ffn2_w_gate, ffn2_w_up, ffn2_w_down, ffn2_post_g, loss_target, m_ffn1_pre_g, m_ffn1_w_gate, m_ffn1_w_up, m_ffn1_w_down, m_ffn1_post_g, m_mix_pre_g, m_w_in, m_conv_w, m_conv_b, m_dt_bias, m_a_log, m_d_skip, m_ssm_norm_g, m_w_ssm_proj, m_attn_sinks, m_rel_bias_table, m_w_attn_proj, m_w_out, m_mix_post_g, m_ffn2_pre_g, m_ffn2_w_gate, m_ffn2_w_up, m_ffn2_w_down, m_ffn2_post_g, v_ffn1_pre_g, v_ffn1_w_gate, v_ffn1_w_up, v_ffn1_w_down, v_ffn1_post_g, v_mix_pre_g, v_w_in, v_conv_w, v_conv_b, v_dt_bias, v_a_log, v_d_skip, v_ssm_norm_g, v_w_ssm_proj, v_attn_sinks, v_rel_bias_table, v_w_attn_proj, v_w_out, v_mix_post_g, v_ffn2_pre_g, v_ffn2_w_gate, v_ffn2_w_up, v_ffn2_w_down, v_ffn2_post_g):
    args = dict(locals())
    weight_names = ['ffn1_pre_g', 'ffn1_w_gate', 'ffn1_w_up', 'ffn1_w_down', 'ffn1_post_g', 'mix_pre_g', 'w_in', 'conv_w', 'conv_b',
                    'dt_bias', 'a_log', 'd_skip', 'ssm_norm_g', 'w_ssm_proj', 'attn_sinks', 'rel_bias_table', 'w_attn_proj', 'w_out',
                    'mix_post_g', 'ffn2_pre_g', 'ffn2_w_gate', 'ffn2_w_up', 'ffn2_w_down', 'ffn2_post_g']
    col_sharded = ('ffn1_w_gate', 'ffn1_w_up', 'w_in', 'ffn2_w_gate', 'ffn2_w_up')
    row_sharded = ('ffn1_w_down', 'w_ssm_proj', 'w_attn_proj', 'w_out', 'ffn2_w_down')
    big = col_sharded + row_sharded

    bl, s_len, d = x.shape
    t = bl * s_len
    d_inner = ssm_norm_g.shape[1]
    n_heads = dt_bias.shape[1]
    gn = SSM_GROUPS * SSM_STATE
    conv_dim = d_inner + 2 * gn
    q_dim = ATTN_Q_HEADS * ATTN_HEAD_DIM
    kv_dim = ATTN_KV_HEADS * ATTN_HEAD_DIM

    def local_2d(name, a):
        a = a[0]
        return a.T if name in col_sharded else a

    ffn1_names = ('ffn1_w_gate', 'ffn1_w_up', 'ffn1_w_down')
    ffn2_names = ('ffn2_w_gate', 'ffn2_w_up', 'ffn2_w_down')
    mixer_names = ('w_ssm_proj', 'w_attn_proj', 'w_out')

    def shard(n):
        return local_2d(n, args[n]).astype(BF16)

    def rows(g):
        return g.reshape(N_DEV * g.shape[1], g.shape[2])

    full = {n: rows(g) for n, g in zip(ffn1_names, _exchange_alone(_gather_exchange([shard(n) for n in ffn1_names]), name="gather_ffn1"))}

    x2 = x.reshape(t, d)
    tgt2 = loss_target.reshape(t, d)

    h1, saved1, got_in, got_mixer = _ffn_forward(
        x2, ffn1_pre_g, full['ffn1_w_gate'], full['ffn1_w_up'], full['ffn1_w_down'], ffn1_post_g, "ffn1",
        side_up=_gather_exchange([shard('w_in'), conv_w[0]]), side_down=_gather_exchange([shard(n) for n in mixer_names]))
    full.update({n: rows(g) for n, g in zip(mixer_names, got_mixer)})
    conv_w_full = jnp.transpose(got_in[1], (1, 0, 2)).reshape(SSM_CONV, conv_dim)

    win_t = rows(got_in[0])
    dt_lo = 2 * d + d_inner + conv_dim
    n_main = win_t.shape[0] - n_heads
    win_p = jnp.concatenate([win_t[:dt_lo], win_t[dt_lo + n_heads:], win_t[dt_lo:dt_lo + n_heads],
                             jnp.zeros((LANES - n_heads, d), BF16)], axis=0)
    off = {'gs': 0, 'ga': d, 'z': 2 * d, 'xbc': 2 * d + d_inner, 'q': dt_lo, 'kv': dt_lo + q_dim, 'dt': n_main}

    u = _rms_fwd(h1, mix_pre_g, name="mix_prenorm")

    def proj(key, width, out_dtype=BF16):
        tn = _tile(math.gcd(width, off[key]) if off[key] else width, 1024, LANES)
        assert off[key] % tn == 0 and width % tn == 0
        return _mm_nt(u, win_p, row_off=off[key] // tn, n_out=width, out_dtype=out_dtype, tn=tn, name=f"proj_{key}")

    gs, ga, z, xbc = proj('gs', d), proj('ga', d), proj('z', d_inner), proj('xbc', conv_dim)
    q, kv = proj('q', q_dim), proj('kv', 2 * kv_dim)
    dt_raw = proj('dt', LANES, F32)

    dtb_p, alog_p, dsk_p, sinks_p = _pad_lanes(dt_bias), _pad_lanes(a_log), _pad_lanes(d_skip), _pad_lanes(attn_sinks)
    xc = _conv_fwd(xbc, conv_w_full, conv_b, bl, name="conv_fwd")
    (y, hprev), got_ffn2 = _ssd_fwd(xc, dt_raw, dtb_p, alog_p, dsk_p, bl, n_heads,
                                    side=_gather_exchange([shard(n) for n in ffn2_names]), name="ssd_fwd")
    full.update({n: rows(g) for n, g in zip(ffn2_names, got_ffn2)})
    yn = _gated_norm_fwd(y, z, ssm_norm_g, name="gated_norm_fwd")

    onehot = _bucket_onehot()
    bias = _small_mm_hi(rel_bias_table.T, onehot, NN, name="rel_bias").reshape(ATTN_Q_HEADS, CHUNK, 2 * CHUNK)
    o, lse = _attn_fwd(q, kv, bias, sinks_p, bl, name="attn_fwd")

    ys, ya, merged, mix, h2 = _merge_fwd(yn, o, gs, ga, full['w_ssm_proj'], full['w_attn_proj'], full['w_out'], h1, mix_post_g,
                                         name="merge_fwd")

    h3, saved2, _, _ = _ffn_forward(h2, ffn2_pre_g, full['ffn2_w_gate'], full['ffn2_w_up'], full['ffn2_w_down'], ffn2_post_g, "ffn2")

    dh3, loss_vec = _loss_grad(h3, tgt2, name="loss")
    loss = lax.psum(loss_vec[0, 0], ("x", "y", "c"))

    def stack(g_):
        return g_.reshape(N_DEV, g_.shape[0] // N_DEV, g_.shape[1])

    def scatter_side(blocks):
        return _scatter_exchange([b_.astype(BF16) for b_ in blocks])

    grads, own, received = {}, {}, {}
    dh2, grads['ffn2_pre_g'], d_wgt2, d_wut2, d_wd2, grads['ffn2_post_g'], _ = _ffn_backward(
        dh3, saved2, ffn2_pre_g, full['ffn2_w_gate'], full['ffn2_w_up'], full['ffn2_w_down'], ffn2_post_g, "ffn2")

    dmix, dys, dya, dgs, dga, dyn, do, grads['mix_post_g'] = _merge_bwd(
        dh2, mix, mix_post_g, gs, ga, ys, ya, full['w_ssm_proj'], full['w_attn_proj'], full['w_out'], name="merge_bwd")
    d_wout = _mm_tn([merged], dmix, name="dw_out")
    d_wssm = _mm_tn([yn], dys, name="dw_ssm")
    d_wattn = _mm_tn([o], dya, name="dw_attn")

    dq, dkv, dbias, dsinks = _attn_bwd(do, q, kv, lse, bias, sinks_p, bl, name="attn_bwd")
    d_table = _small_mm_hi(onehot, dbias.reshape(ATTN_Q_HEADS, -1), NT, name="rel_bias_bwd")

    dy, dz, grads['ssm_norm_g'] = _gated_norm_bwd(dyn, y, z, ssm_norm_g, name="gated_norm_bwd")

    first_group = ffn2_names + mixer_names
    own.update(zip(first_group, map(stack, (d_wgt2, d_wut2, d_wd2, d_wssm, d_wattn, d_wout))))
    (dxc, ddt_raw, ddtb, dalog, ddsk), got = _ssd_bwd(dy, xc, dt_raw, hprev, dtb_p, alog_p, dsk_p, bl, n_heads,
                                                      side=scatter_side([own[n] for n in first_group]), name="ssd_bwd")
    received.update(zip(first_group, got))
    dxbc, dconv_w8, grads['conv_b'] = _conv_bwd(dxc, xbc, conv_w_full, conv_b, bl, name="conv_bwd")

    proj_grads = [dgs, dga, dz, dxbc, dq, dkv]
    d_win_main = _mm_tn(proj_grads, u, name="dw_in")
    d_win_dt = _mm_tn([ddt_raw], u, name="dw_in_dt")
    own['w_in'] = stack(jnp.concatenate([d_win_main[:dt_lo], d_win_dt[:n_heads], d_win_main[dt_lo:]], axis=0))
    own['conv_w'] = jnp.transpose(dconv_w8[:SSM_CONV].reshape(SSM_CONV, N_DEV, conv_dim // N_DEV), (1, 0, 2))

    segs = [(g_, 0, off[k_]) for g_, k_ in zip(proj_grads + [ddt_raw], ('gs', 'ga', 'z', 'xbc', 'q', 'kv', 'dt'))]
    dh1, grads['mix_pre_g'], got = _mm_nn_rmsbwd(segs, [win_p], h1, mix_pre_g, dh2,
                                                 side=_scatter_exchange([own['w_in'].astype(BF16), own['conv_w']]), name="mix_du")
    received.update(zip(('w_in', 'conv_w'), got))

    def ffn1_side(d_wgt, d_wut, d_wd):
        own.update(zip(ffn1_names, map(stack, (d_wgt, d_wut, d_wd))))
        return scatter_side([own[n] for n in ffn1_names])
    dx2, grads['ffn1_pre_g'], _, _, _, grads['ffn1_post_g'], got = _ffn_backward(
        dh1, saved1, ffn1_pre_g, full['ffn1_w_gate'], full['ffn1_w_up'], full['ffn1_w_down'], ffn1_post_g, "ffn1", make_side=ffn1_side)
    received.update(zip(ffn1_names, got))

    me = 4 * lax.axis_index("x") + 2 * lax.axis_index("y") + lax.axis_index("c")

    def own_block(a):
        return lax.dynamic_index_in_dim(a, me, 0, keepdims=False)
    reduced = {n: _reduce_blocks(own_block(own[n]), received[n], name=f"reduce_{n}") for n in big}
    conv_sum = _reduce_blocks(own_block(own['conv_w']), received['conv_w'], name="reduce_conv_w")

    grads['dt_bias'], grads['a_log'], grads['d_skip'] = ddtb[:, :n_heads], dalog[:, :n_heads], ddsk[:, :n_heads]
    grads['attn_sinks'] = dsinks[:, :ATTN_Q_HEADS]
    grads['rel_bias_table'] = d_table
    small = [n for n in weight_names if n not in big and n != 'conv_w']
    flat = jnp.concatenate([grads[n].reshape(-1) for n in small])
    n_small = flat.shape[0]
    rows = -(-n_small // (8 * LANES)) * 8
    flat = jnp.pad(flat, (0, rows * LANES - n_small)).reshape(rows, LANES)
    summed = _all_reduce_small(flat, name="allreduce_small").reshape(-1)
    pos = 0
    for n in small:
        size = grads[n].size
        grads[n] = summed[pos:pos + size].reshape(args[n].shape)
        pos += size

    out_g, out_d, out_m, out_v = {}, {}, {}, {}
    for n in big:
        w2, m2, v2 = local_2d(n, args[n]), local_2d(n, args['m_' + n]), local_2d(n, args['v_' + n])
        dlt, mn, vn = _adamw(w2, reduced[n], m2, v2, name=f"adamw_{n}")

        def back(a, n=n):
            return (a.T if n in col_sharded else a)[None]
        out_g[n], out_d[n], out_m[n], out_v[n] = back(reduced[n]), back(dlt), back(mn), back(vn)

    def pack(prefix):
        vals = [(grads[n] if prefix == 'g' else args[prefix + n]).reshape(-1) for n in small]
        vals.append((conv_sum if prefix == 'g' else args[prefix + 'conv_w']).reshape(-1))
        flat_ = jnp.concatenate(vals)
        rows_ = -(-flat_.shape[0] // (8 * LANES)) * 8
        return jnp.pad(flat_, (0, rows_ * LANES - flat_.shape[0])).reshape(rows_, LANES)

    g_small = pack('g')
    d_small, m_small, v_small = _adamw(pack(''), g_small, pack('m_'), pack('v_'), name="adamw_small")
    pos = 0
    for n in small + ['conv_w']:
        shape = args[n].shape
        size = int(np.prod(shape))
        for dst, src in ((out_g, g_small), (out_d, d_small), (out_m, m_small), (out_v, v_small)):
            dst[n] = src.reshape(-1)[pos:pos + size].reshape(shape)
        pos += size

    grad_x = dx2.reshape(bl, s_len, d)
    return (loss, grad_x, *[out_g[n] for n in weight_names], *[out_d[n] for n in weight_names],
            *[out_m[n] for n in weight_names], *[out_v[n] for n in weight_names])
```

```python
import functools
import math

import numpy as np
import jax
import jax.numpy as jnp
from jax import lax
from jax.experimental import pallas as pl
from jax.experimental.pallas import tpu as pltpu

F32 = jnp.float32
BF16 = jnp.bfloat16
MESH = pl.DeviceIdType.MESH
N_DEV = 8

SSM_HEAD_DIM = 64
SSM_GROUPS = 4
SSM_STATE = 128
SSM_CONV = 4
CHUNK = 128
ATTN_HEAD_DIM = 64
ATTN_Q_HEADS = 16
ATTN_KV_HEADS = 4
REL_BUCKETS = 32
REL_MAX_DISTANCE = 128
RMS_EPS = 1e-6
FFN_RESIDUAL_WEIGHT = 0.5
ADAM_LR, ADAM_B1, ADAM_B2, ADAM_EPS, ADAM_WD, ADAM_STEP = 0.001, 0.9, 0.999, 1e-08, 0.01, 10

LANES = 128
VMEM_LIMIT_BYTES = 56 * 1024 * 1024
FFN_COL_TILE = 1408

NEG_INF = float("-inf")


def _params(*sem):
    return pltpu.CompilerParams(dimension_semantics=sem, vmem_limit_bytes=VMEM_LIMIT_BYTES)


def _tile(n, pref, mult=8):
    if n <= pref:
        return n
    t = (pref // mult) * mult
    while t >= mult:
        if n % t == 0:
            return t
        t -= mult
    return n


def _sigmoid(x):
    return 1.0 / (1.0 + jnp.exp(-x))


def _dot(a, b, dims):
    return lax.dot_general(a, b, (dims, ((), ())), preferred_element_type=F32)


NN = ((1,), (0,))
NT = ((1,), (1,))
TN = ((0,), (0,))


def _dot_hi(a, b, dims=NN):
    return lax.dot_general(a, b, (dims, ((), ())), preferred_element_type=F32, precision=lax.Precision.HIGHEST)


def _const_spec(shape):
    nd = len(shape)
    return pl.BlockSpec(shape, lambda *_: (0,) * nd)


def _resident_spec(shape):
    nd = len(shape)
    return pl.BlockSpec(shape, lambda *_: (0,) * nd, pipeline_mode=pl.Buffered(1))


class _Exchange:
    def __init__(self, arrays, out_shape, scratch, start, finish):
        self.arrays, self.out_shape, self.scratch, self.start, self.finish = arrays, out_shape, scratch, start, finish


def _hosted_call(body, *, grid, in_specs, out_specs, out_shape, scratch_shapes, operands, side, name):
    in_specs, out_specs, out_shape, scratch_shapes = list(in_specs), list(out_specs), list(out_shape), list(scratch_shapes)
    sem = ("arbitrary",) * len(grid)
    if side is None:
        outs = pl.pallas_call(body, out_shape=tuple(out_shape), grid=grid, in_specs=in_specs, out_specs=tuple(out_specs),
                              scratch_shapes=scratch_shapes, name=name, compiler_params=_params(*sem))(*operands)
        return tuple(outs), ()
    n_in, n_out, n_scr = len(in_specs), len(out_shape), len(scratch_shapes)
    s_in, s_out = len(side.arrays), len(side.out_shape)

    def wrapped(*refs):
        refs = list(refs)
        main_in, side_in = refs[:n_in], refs[n_in:n_in + s_in]
        o0 = n_in + s_in
        main_out, side_out = refs[o0:o0 + n_out], refs[o0 + n_out:o0 + n_out + s_out]
        c0 = o0 + n_out + s_out
        main_scr, side_scr = refs[c0:c0 + n_scr], refs[c0 + n_scr:]
        ids = [pl.program_id(ax) for ax in range(len(grid))]
        first = functools.reduce(jnp.logical_and, [i == 0 for i in ids])
        last = functools.reduce(jnp.logical_and, [i == g - 1 for i, g in zip(ids, grid)])

        @pl.when(first)
        def _():
            side.start(side_in, side_out, side_scr)

        body(*main_in, *main_out, *main_scr)

        @pl.when(last)
        def _():
            side.finish(side_in, side_out, side_scr)

    hbm = pl.BlockSpec(memory_space=pl.ANY)
    outs = pl.pallas_call(
        wrapped, out_shape=tuple(out_shape + list(side.out_shape)), grid=grid,
        in_specs=in_specs + [hbm] * s_in, out_specs=tuple(out_specs + [hbm] * s_out),
        scratch_shapes=scratch_shapes + list(side.scratch), name=name, compiler_params=_params(*sem))(*operands, *side.arrays)
    return tuple(outs[:n_out]), tuple(outs[n_out:])


def _mm_nt(a, bt, *, row_off=0, n_out=None, out_dtype=BF16, tm=512, tn=512, name):
    m, k = a.shape
    n_out = bt.shape[0] if n_out is None else n_out
    tm, tn = _tile(m, tm), _tile(n_out, tn, LANES)

    def body(a_ref, b_ref, o_ref):
        o_ref[...] = _dot(a_ref[...].astype(BF16), b_ref[...].astype(BF16), NT).astype(o_ref.dtype)

    return pl.pallas_call(
        body, out_shape=jax.ShapeDtypeStruct((m, n_out), out_dtype), grid=(n_out // tn, m // tm),
        in_specs=[pl.BlockSpec((tm, k), lambda j, i: (i, 0)), pl.BlockSpec((tn, k), lambda j, i: (j + row_off, 0))],
        out_specs=pl.BlockSpec((tm, tn), lambda j, i: (i, j)),
        name=name, compiler_params=_params("parallel", "arbitrary"))(a, bt)


def _mm_tn(a_list, b, *, tm=1408, tk=1024, side=None, name):
    t, n = b.shape
    tk = _tile(t, tk)
    nk = t // tk
    widths = [a.shape[1] for a in a_list]
    tm = _tile(math.gcd(*widths), tm, LANES)
    assert all(w % tm == 0 for w in widths)
    starts = np.cumsum([0] + [w // tm for w in widths])
    nseg = len(a_list)

    def a_spec(s):
        lo, hi = int(starts[s]), int(starts[s + 1])

        def idx(i, k):
            active = jnp.logical_and(i >= lo, i < hi)
            return (jnp.where(active, k, 0), jnp.clip(i - lo, 0, hi - lo - 1))
        return pl.BlockSpec((tk, tm), idx)

    def body(*refs):
        a_refs, b_ref, o_ref, o16_ref, acc = refs[:nseg], refs[nseg], refs[nseg + 1], refs[nseg + 2], refs[nseg + 3]
        i, k = pl.program_id(0), pl.program_id(1)

        @pl.when(k == 0)
        def _():
            acc[...] = jnp.zeros_like(acc)

        bv = b_ref[...].astype(BF16)
        for s in range(nseg):
            lo, hi = int(starts[s]), int(starts[s + 1])

            @pl.when(jnp.logical_and(i >= lo, i < hi))
            def _(s=s):
                acc[...] += _dot(a_refs[s][...].astype(BF16), bv, TN)

        @pl.when(k == nk - 1)
        def _():
            o_ref[...] = acc[...]
            o16_ref[...] = acc[...].astype(BF16)

    rows = int(starts[-1]) * tm
    o_spec = pl.BlockSpec((tm, n), lambda i, k: (i, 0))
    (o32, o16), got = _hosted_call(
        body, out_shape=[jax.ShapeDtypeStruct((rows, n), F32), jax.ShapeDtypeStruct((rows, n), BF16)], grid=(int(starts[-1]), nk),
        in_specs=[a_spec(s) for s in range(nseg)] + [pl.BlockSpec((tk, n), lambda i, k: (k, 0))],
        out_specs=[o_spec, o_spec], scratch_shapes=[pltpu.VMEM((tm, n), F32)], operands=list(a_list) + [b], side=side, name=name)
    return o32, o16, got


def _mm_nn_rmsbwd(segs, weights, x, g, dres, *, tm=256, side=None, name):
    t, d = x.shape
    tm = _tile(t, tm)
    nseg, nw = len(segs), len(weights)

    def body(*refs):
        a_refs, w_refs = refs[:nseg], refs[nseg:nseg + nw]
        x_ref, g_ref, dres_ref, dx_ref, dg_ref = refs[nseg + nw:]

        @pl.when(pl.program_id(0) == 0)
        def _():
            dg_ref[...] = jnp.zeros_like(dg_ref)

        dn = None
        for s, (a, w_idx, row0) in enumerate(segs):
            part = _dot(a_refs[s][...].astype(BF16), w_refs[w_idx][row0:row0 + a.shape[1], :], NN)
            dn = part if dn is None else dn + part
        xv = x_ref[...]
        r = lax.rsqrt(jnp.mean(xv * xv, axis=-1, keepdims=True) + RMS_EPS)
        xhat = xv * r
        dyg = dn * g_ref[...]
        dx_ref[...] = dres_ref[...] + r * (dyg - xhat * jnp.mean(dyg * xhat, axis=-1, keepdims=True))
        dg_ref[...] += jnp.sum(dn * xhat, axis=0, keepdims=True)

    row = pl.BlockSpec((tm, d), lambda i: (i, 0))
    in_specs = [pl.BlockSpec((tm, a.shape[1]), lambda i: (i, 0)) for a, _, _ in segs]
    in_specs += [_resident_spec(w.shape) for w in weights] + [row, _const_spec((1, d)), row]
    (dx, dg), extra = _hosted_call(
        body, grid=(t // tm,), in_specs=in_specs, out_specs=[row, _const_spec((1, d))],
        out_shape=[jax.ShapeDtypeStruct((t, d), F32), jax.ShapeDtypeStruct((1, d), F32)], scratch_shapes=[],
        operands=[a for a, _, _ in segs] + list(weights) + [x, g, dres], side=side, name=name)
    return dx, dg, extra


def _rms_fwd(x, g, *, name):
    t, d = x.shape
    tm = _tile(t, 512)

    def body(x_ref, g_ref, o_ref):
        xv = x_ref[...]
        r = lax.rsqrt(jnp.mean(xv * xv, axis=-1, keepdims=True) + RMS_EPS)
        o_ref[...] = (xv * r * g_ref[...]).astype(o_ref.dtype)

    row = pl.BlockSpec((tm, d), lambda i: (i, 0))
    return pl.pallas_call(body, out_shape=jax.ShapeDtypeStruct((t, d), BF16), grid=(t // tm,),
                          in_specs=[row, _const_spec((1, d))], out_specs=row, name=name, compiler_params=_params("parallel"))(x, g)


def _col_chunks(width, chunk=4 * LANES):
    return [(c0, min(c0 + chunk, width)) for c0 in range(0, width, chunk)]


def _ffn_up(n, wgt, wut, *, side=None, name):
    t, d = n.shape
    f = wgt.shape[0]
    tm, tn = _tile(t, 512), _tile(f, FFN_COL_TILE, LANES)

    def body(n_ref, wg_ref, wu_ref, g_ref, u_ref, h_ref):
        nv = n_ref[...]
        for c0, c1 in _col_chunks(tn):
            gv = _dot(nv, wg_ref[c0:c1, :], NT)
            uv = _dot(nv, wu_ref[c0:c1, :], NT)
            g_ref[:, c0:c1] = gv.astype(BF16)
            u_ref[:, c0:c1] = uv.astype(BF16)
            h_ref[:, c0:c1] = (gv * _sigmoid(gv) * uv).astype(BF16)

    w_spec = pl.BlockSpec((tn, d), lambda j, i: (j, 0))
    o_spec = pl.BlockSpec((tm, tn), lambda j, i: (i, j))
    shp = jax.ShapeDtypeStruct((t, f), BF16)
    return _hosted_call(body, grid=(f // tn, t // tm), in_specs=[pl.BlockSpec((tm, d), lambda j, i: (i, 0)), w_spec, w_spec],
                        out_specs=[o_spec, o_spec, o_spec], out_shape=[shp, shp, shp], scratch_shapes=[], operands=[n, wgt, wut],
                        side=side, name=name)


def _rms_residual(acc, h, gp, weight):
    r = lax.rsqrt(jnp.mean(acc * acc, axis=-1, keepdims=True) + RMS_EPS)
    return h + weight * (acc * r * gp)


def _ffn_down(hid, wd, h_in, gp, *, side=None, name):
    t, f = hid.shape
    d = wd.shape[1]
    tm = _tile(t, 256)

    def body(hid_ref, wd_ref, hin_ref, gp_ref, f_ref, hout_ref):
        acc = _dot(hid_ref[...], wd_ref[...], NN)
        f_ref[...] = acc
        hout_ref[...] = _rms_residual(acc, hin_ref[...], gp_ref[...], FFN_RESIDUAL_WEIGHT)

    row = pl.BlockSpec((tm, d), lambda i: (i, 0))
    shp = jax.ShapeDtypeStruct((t, d), F32)
    return _hosted_call(body, grid=(t // tm,),
                        in_specs=[pl.BlockSpec((tm, f), lambda i: (i, 0)), _resident_spec((f, d)), row, _const_spec((1, d))],
                        out_specs=[row, row], out_shape=[shp, shp], scratch_shapes=[], operands=[hid, wd, h_in, gp],
                        side=side, name=name)


def _post_bwd(dh, f, gp, weight, *, name):
    t, d = f.shape
    tm = _tile(t, 512)

    def body(dh_ref, f_ref, gp_ref, df_ref, dgp_ref):
        @pl.when(pl.program_id(0) == 0)
        def _():
            dgp_ref[...] = jnp.zeros_like(dgp_ref)
        fv = f_ref[...]
        dy = weight * dh_ref[...]
        r = lax.rsqrt(jnp.mean(fv * fv, axis=-1, keepdims=True) + RMS_EPS)
        fhat = fv * r
        dyg = dy * gp_ref[...]
        df_ref[...] = (r * (dyg - fhat * jnp.mean(dyg * fhat, axis=-1, keepdims=True))).astype(BF16)
        dgp_ref[...] += jnp.sum(dy * fhat, axis=0, keepdims=True)

    row = pl.BlockSpec((tm, d), lambda i: (i, 0))
    return pl.pallas_call(body, out_shape=(jax.ShapeDtypeStruct((t, d), BF16), jax.ShapeDtypeStruct((1, d), F32)), grid=(t // tm,),
                          in_specs=[row, row, _const_spec((1, d))], out_specs=(row, _const_spec((1, d))),
                          name=name, compiler_params=_params("arbitrary"))(dh, f, gp)


def _ffn_dhid(df, wd, g, u, *, name):
    t, d = df.shape
    f = wd.shape[0]
    tm, tn = _tile(t, 512), _tile(f, FFN_COL_TILE, LANES)

    def body(df_ref, wd_ref, g_ref, u_ref, dg_ref, du_ref):
        dfv = df_ref[...]
        for c0, c1 in _col_chunks(tn):
            dh = _dot(dfv, wd_ref[c0:c1, :], NT)
            gv = g_ref[:, c0:c1].astype(F32)
            uv = u_ref[:, c0:c1].astype(F32)
            sg = _sigmoid(gv)
            dg_ref[:, c0:c1] = (dh * uv * (sg * (1.0 + gv * (1.0 - sg)))).astype(BF16)
            du_ref[:, c0:c1] = (dh * (gv * sg)).astype(BF16)

    o_spec = pl.BlockSpec((tm, tn), lambda j, i: (i, j))
    shp = jax.ShapeDtypeStruct((t, f), BF16)
    return pl.pallas_call(body, out_shape=(shp, shp), grid=(f // tn, t // tm),
                          in_specs=[pl.BlockSpec((tm, d), lambda j, i: (i, 0)), pl.BlockSpec((tn, d), lambda j, i: (j, 0)), o_spec, o_spec],
                          out_specs=(o_spec, o_spec), name=name, compiler_params=_params("parallel", "arbitrary"))(df, wd, g, u)


def _ffn_forward(h_in, g_pre, wgt, wut, wd, g_post, tag, side_up=None, side_down=None):
    n = _rms_fwd(h_in, g_pre, name=f"{tag}_prenorm")
    (g, u, hid), got_up = _ffn_up(n, wgt, wut, side=side_up, name=f"{tag}_up")
    wd = wd(got_up) if callable(wd) else wd
    (f, h_out), got_down = _ffn_down(hid, wd, h_in, g_post, side=side_down, name=f"{tag}_down")
    return h_out, (h_in, n, g, u, hid, f), got_up, got_down


def _stack8(g):
    return g.reshape(N_DEV, g.shape[0] // N_DEV, g.shape[1])


def _ffn_backward(dh_out, saved, g_pre, wgt, wut, wd, g_post, tag, chain=False):
    h_in, n, g, u, hid, f = saved

    def side_of(grad16):
        return _scatter_exchange([_stack8(grad16)]) if chain else None

    df, dg_post = _post_bwd(dh_out, f, g_post, FFN_RESIDUAL_WEIGHT, name=f"{tag}_post_bwd")
    dgate, dup = _ffn_dhid(df, wd, g, u, name=f"{tag}_dhid")
    d_wd, d_wd16, _ = _mm_tn([hid], df, name=f"{tag}_dwd")
    d_wgt, d_wgt16, got_wd = _mm_tn([dgate], n, side=side_of(d_wd16), name=f"{tag}_dwg")
    d_wut, d_wut16, got_wg = _mm_tn([dup], n, side=side_of(d_wgt16), name=f"{tag}_dwu")
    dh_in, dg_pre, got_wu = _mm_nn_rmsbwd([(dgate, 0, 0), (dup, 1, 0)], [wgt, wut], h_in, g_pre, dh_out, side=side_of(d_wut16),
                                          name=f"{tag}_dn")
    received = (got_wg[0], got_wu[0], got_wd[0]) if chain else None
    return dh_in, dg_pre, dg_post, (d_wgt, d_wut, d_wd), (d_wgt16, d_wut16, d_wd16), received


CONV_ROWS = 128
HALO = 8


def _taps(w_ref):
    return [w_ref[k:k + 1, :] for k in range(SSM_CONV)]


def _conv_chunk(x_ref, r0, tail, taps, bias):
    xv = x_ref[pl.ds(r0, CONV_ROWS), :].astype(F32)
    ext = jnp.concatenate([tail, xv], axis=0)
    shifted = [xv] + [pltpu.roll(ext, k, axis=0)[HALO:] for k in range(1, SSM_CONV)]
    pre = bias + shifted[0] * taps[SSM_CONV - 1]
    for k in range(1, SSM_CONV):
        pre = pre + shifted[k] * taps[SSM_CONV - 1 - k]
    return xv, shifted, pre


def _fold_rows(a):
    return functools.reduce(jnp.add, [a[i:i + 8] for i in range(0, a.shape[0], 8)])


def _conv_fwd(xbc, conv_w, conv_b, bl, *, name):
    t, c = xbc.shape
    s = t // bl
    tc = LANES
    assert s % CONV_ROWS == 0

    def body(x_ref, w_ref, b_ref, o_ref):
        taps, bias = _taps(w_ref), b_ref[...]

        def step(r, tail):
            r0 = pl.multiple_of(r * CONV_ROWS, CONV_ROWS)
            xv, _, pre = _conv_chunk(x_ref, r0, tail, taps, bias)
            o_ref[pl.ds(r0, CONV_ROWS), :] = (pre * _sigmoid(pre)).astype(o_ref.dtype)
            return xv[CONV_ROWS - HALO:]

        lax.fori_loop(0, s // CONV_ROWS, step, jnp.zeros((HALO, tc), F32))

    blk = pl.BlockSpec((s, tc), lambda b, j: (b, j))
    return pl.pallas_call(body, out_shape=jax.ShapeDtypeStruct((t, c), BF16), grid=(bl, c // tc),
                          in_specs=[blk, pl.BlockSpec((SSM_CONV, tc), lambda b, j: (0, j)), pl.BlockSpec((1, tc), lambda b, j: (0, j))],
                          out_specs=blk, name=name, compiler_params=_params("parallel", "arbitrary"))(xbc, conv_w, conv_b)


def _conv_bwd(dxc, xbc, conv_w, conv_b, bl, *, name):
    t, c = xbc.shape
    s = t // bl
    tc = LANES
    n_steps = s // CONV_ROWS

    def body(dy_ref, x_ref, w_ref, b_ref, dx_ref, dw_ref, db_ref, dpre_s):
        @pl.when(pl.program_id(1) == 0)
        def _():
            dw_ref[...] = jnp.zeros_like(dw_ref)
            db_ref[...] = jnp.zeros_like(db_ref)

        taps, bias = _taps(w_ref), b_ref[...]
        zero8 = jnp.zeros((HALO, tc), F32)

        def forward(r, carry):
            tail, sums = carry
            r0 = pl.multiple_of(r * CONV_ROWS, CONV_ROWS)
            xv, shifted, pre = _conv_chunk(x_ref, r0, tail, taps, bias)
            sg = _sigmoid(pre)
            dpre = dy_ref[pl.ds(r0, CONV_ROWS), :].astype(F32) * (sg * (1.0 + pre * (1.0 - sg)))
            dpre_s[pl.ds(r0, CONV_ROWS), :] = dpre
            sums = tuple(acc + _fold_rows(dpre * sh) for acc, sh in zip(sums[:-1], shifted)) + (sums[-1] + _fold_rows(dpre),)
            return xv[CONV_ROWS - HALO:], sums

        _, sums = lax.fori_loop(0, n_steps, forward, (zero8, (zero8,) * (SSM_CONV + 1)))
        dpre_s[pl.ds(s, HALO), :] = zero8
        for k in range(SSM_CONV):
            dw_ref[SSM_CONV - 1 - k:SSM_CONV - k, :] += jnp.sum(sums[k], axis=0, keepdims=True)
        db_ref[...] += jnp.sum(sums[-1], axis=0, keepdims=True)

        def backward(r, _):
            r0 = pl.multiple_of(r * CONV_ROWS, CONV_ROWS)
            cur = dpre_s[pl.ds(r0, CONV_ROWS + HALO), :]
            dx = cur[:CONV_ROWS] * taps[SSM_CONV - 1]
            for k in range(1, SSM_CONV):
                dx = dx + pltpu.roll(cur, CONV_ROWS + HALO - k, axis=0)[:CONV_ROWS] * taps[SSM_CONV - 1 - k]
            dx_ref[pl.ds(r0, CONV_ROWS), :] = dx.astype(dx_ref.dtype)
            return 0

        lax.fori_loop(0, n_steps, backward, 0)

    blk = pl.BlockSpec((s, tc), lambda j, b: (b, j))
    return pl.pallas_call(
        body, out_shape=(jax.ShapeDtypeStruct((t, c), BF16), jax.ShapeDtypeStruct((8, c), F32), jax.ShapeDtypeStruct((1, c), F32)),
        grid=(c // tc, bl),
        in_specs=[blk, blk, pl.BlockSpec((SSM_CONV, tc), lambda j, b: (0, j)), pl.BlockSpec((1, tc), lambda j, b: (0, j))],
        out_specs=(blk, pl.BlockSpec((8, tc), lambda j, b: (0, j)), pl.BlockSpec((1, tc), lambda j, b: (0, j))),
        scratch_shapes=[pltpu.VMEM((s + HALO, tc), F32)],
        name=name, compiler_params=_params("parallel", "arbitrary"))(dxc, xbc, conv_w, conv_b)


def _softplus(x):
    return jnp.maximum(x, 0.0) + jnp.log1p(jnp.exp(-jnp.abs(x)))


def _hilo_dot(v, m_b, dims=NN):
    hi = v.astype(BF16)
    lo = (v - hi.astype(F32)).astype(BF16)
    return _dot(hi, m_b, dims) + _dot(lo, m_b, dims)


def _ssd_chunk_common(dtraw_ref, dtb_ref, alog_ref, dsk_ref, d_inner):
    q, p = CHUNK, SSM_HEAD_DIM
    shift = p.bit_length() - 1
    assert 1 << shift == p
    dt = _softplus(dtraw_ref[...] + dtb_ref[...])
    a = -jnp.exp(alog_ref[...])
    ii = lax.broadcasted_iota(jnp.int32, (q, q), 0)
    jj = lax.broadcasted_iota(jnp.int32, (q, q), 1)
    causal = ii >= jj
    tril = jnp.where(causal, 1.0, 0.0).astype(F32)
    triu = jnp.where(ii <= jj, 1.0, 0.0).astype(F32)
    a_cs = _dot_hi(tril, dt * a)
    a_cs_t = a_cs.T
    a_last = a_cs[q - 1:q, :]
    e_col = jnp.exp(a_cs)
    dec_end = jnp.exp(a_last - a_cs)
    head_of_col = lax.shift_right_logical(lax.broadcasted_iota(jnp.int32, (LANES, d_inner), 1), shift)
    spread = (lax.broadcasted_iota(jnp.int32, (LANES, d_inner), 0) == head_of_col).astype(BF16)
    wide = _hilo_dot(jnp.concatenate([dt, e_col, dec_end, jnp.broadcast_to(dsk_ref[...], (8, LANES))], axis=0), spread)
    return dict(dt=dt, a=a, a_cs=a_cs, a_cs_t=a_cs_t, a_last=a_last, dec_end=dec_end, causal=causal, triu=triu,
                dt_e=wide[:q], e_e=wide[q:2 * q], dec_e=wide[2 * q:3 * q], dsk_e=wide[3 * q:3 * q + 1])


def _fill_block_diag(bd_ref, src_ref, hpg, col0=0):
    q, p = CHUNK, SSM_HEAD_DIM
    for hh in range(hpg):
        bd_ref[hh * q:(hh + 1) * q, hh * p:(hh + 1) * p] = src_ref[:, col0 + hh * p:col0 + (hh + 1) * p]


def _lane_onehot(h):
    return (lax.broadcasted_iota(jnp.int32, (1, LANES), 1) == h).astype(F32)


def _ssd_fwd(xc, dt_raw, dt_bias, a_log, d_skip, bl, n_heads, *, side=None, name):
    t = xc.shape[0]
    q, p, nst, grp = CHUNK, SSM_HEAD_DIM, SSM_STATE, SSM_GROUPS
    d_inner = n_heads * p
    hpg = n_heads // grp
    gw = hpg * p
    nc = t // bl // q
    assert d_inner % (grp * nst) == 0 and nst == LANES

    def body(xs_ref, b_ref, c_ref, dtraw_ref, dtb_ref, alog_ref, dsk_ref, y_ref, hprev_ref, state, m_all, x_bd, xdt_s):
        @pl.when(jnp.logical_and(pl.program_id(0) == 0, pl.program_id(1) == 0))
        def _():
            x_bd[...] = jnp.zeros_like(x_bd)

        @pl.when(pl.program_id(1) == 0)
        def _():
            state[...] = jnp.zeros_like(state)

        cm = _ssd_chunk_common(dtraw_ref, dtb_ref, alog_ref, dsk_ref, d_inner)
        for g in range(grp):
            cols = slice(g * gw, (g + 1) * gw)
            bg = b_ref[:, g * nst:(g + 1) * nst]
            cg = c_ref[:, g * nst:(g + 1) * nst]
            scores = _dot(cg, bg, NT)
            for hh in range(hpg):
                h = g * hpg + hh
                seg = cm['a_cs'][:, h:h + 1] - cm['a_cs_t'][h:h + 1, :]
                m_all[:, hh * q:(hh + 1) * q] = (scores * jnp.exp(jnp.where(cm['causal'], seg, NEG_INF))).astype(BF16)
            xs = xs_ref[:, cols].astype(F32)
            xdt = xs * cm['dt_e'][:, cols]
            xdt_s[...] = xdt.astype(BF16)
            _fill_block_diag(x_bd, xdt_s, hpg)
            hprev = state[g]
            hprev_ref[g] = hprev
            y = _dot(m_all[...], x_bd[...], NN) + cm['e_e'][:, cols] * _dot(cg, hprev.astype(BF16), NT)
            y_ref[:, cols] = y + cm['dsk_e'][:, cols] * xs
            st = _dot((xdt * cm['dec_e'][:, cols]).astype(BF16), bg, TN)
            for hh in range(hpg):
                h = g * hpg + hh
                rows = slice(hh * p, (hh + 1) * p)
                state[g, rows, :] = jnp.exp(cm['a_last'][:, h:h + 1]) * hprev[rows] + st[rows]

    gn = grp * nst

    def rowmap(b, c):
        return b * nc + c
    vec = pl.BlockSpec((1, LANES), lambda b, c: (0, 0))
    return _hosted_call(
        body,
        out_shape=[jax.ShapeDtypeStruct((t, d_inner), F32), jax.ShapeDtypeStruct((t // q, grp, gw, nst), F32)],
        grid=(bl, nc),
        in_specs=[pl.BlockSpec((q, d_inner), lambda b, c: (rowmap(b, c), 0)),
                  pl.BlockSpec((q, gn), lambda b, c: (rowmap(b, c), d_inner // gn)),
                  pl.BlockSpec((q, gn), lambda b, c: (rowmap(b, c), d_inner // gn + 1)),
                  pl.BlockSpec((q, LANES), lambda b, c: (rowmap(b, c), 0)), vec, vec, vec],
        out_specs=[pl.BlockSpec((q, d_inner), lambda b, c: (rowmap(b, c), 0)),
                   pl.BlockSpec((None, grp, gw, nst), lambda b, c: (rowmap(b, c), 0, 0, 0))],
        scratch_shapes=[pltpu.VMEM((grp, gw, nst), F32), pltpu.VMEM((q, hpg * q), BF16), pltpu.VMEM((hpg * q, gw), BF16),
                        pltpu.VMEM((q, gw), BF16)],
        operands=[xc, xc, xc, dt_raw, dt_bias, a_log, d_skip], side=side, name=name)


def _ssd_bwd(dy, xc, dt_raw, hprev_all, dt_bias, a_log, d_skip, bl, n_heads, *, side=None, name):
    t, c_dim = xc.shape
    q, p, nst, grp = CHUNK, SSM_HEAD_DIM, SSM_STATE, SSM_GROUPS
    d_inner = n_heads * p
    hpg = n_heads // grp
    gw = hpg * p
    nc = t // bl // q
    gn = grp * nst
    shift = p.bit_length() - 1

    def body(dy_ref, xs_ref, b_ref, c_ref, dtraw_ref, hprev_ref, dtb_ref, alog_ref, dsk_ref,
             dxc_ref, ddtraw_ref, ddtb_ref, dalog_ref, ddsk_ref, dstate, m_all, mt_all, x_bd, dy_bd, xdt_s):
        @pl.when(jnp.logical_and(pl.program_id(0) == 0, pl.program_id(1) == 0))
        def _():
            ddtb_ref[...] = jnp.zeros_like(ddtb_ref)
            dalog_ref[...] = jnp.zeros_like(dalog_ref)
            ddsk_ref[...] = jnp.zeros_like(ddsk_ref)
            x_bd[...] = jnp.zeros_like(x_bd)
            dy_bd[...] = jnp.zeros_like(dy_bd)

        @pl.when(pl.program_id(1) == 0)
        def _():
            dstate[...] = jnp.zeros_like(dstate)

        cm = _ssd_chunk_common(dtraw_ref, dtb_ref, alog_ref, dsk_ref, d_inner)
        causal = cm['causal']
        upper = cm['triu'] > 0.5
        seg_row = lax.shift_right_logical(lax.broadcasted_iota(jnp.int32, (gw, LANES), 0), shift)
        seg_lane = lax.broadcasted_iota(jnp.int32, (gw, LANES), 1)
        sums = jnp.zeros((5 * q, LANES), F32)
        state_dot = jnp.zeros((1, LANES), F32)
        for g in range(grp):
            cols = slice(g * gw, (g + 1) * gw)
            seg_sum = (seg_row + g * hpg == seg_lane).astype(BF16)
            bg = b_ref[:, g * nst:(g + 1) * nst]
            cg = c_ref[:, g * nst:(g + 1) * nst]
            scores = _dot(cg, bg, NT)
            scores_t = _dot(bg, cg, NT)
            xs = xs_ref[:, cols].astype(F32)
            xdt = xs * cm['dt_e'][:, cols]
            xdt_s[...] = xdt.astype(BF16)
            _fill_block_diag(x_bd, xdt_s, hpg)
            _fill_block_diag(dy_bd, dy_ref, hpg, g * gw)
            dy_b = dy_ref[:, cols]
            dyf = dy_b.astype(F32)
            dm_all = _dot(dy_b, x_bd[...], NT)
            dscores = jnp.zeros((q, q), F32)
            for hh in range(hpg):
                h = g * hpg + hh
                blk = slice(hh * q, (hh + 1) * q)
                seg = cm['a_cs'][:, h:h + 1] - cm['a_cs_t'][h:h + 1, :]
                decay = jnp.exp(jnp.where(causal, seg, NEG_INF))
                m_all[:, blk] = (scores * decay).astype(BF16)
                mt_all[:, blk] = (scores_t * jnp.exp(jnp.where(upper, -seg, NEG_INF))).astype(BF16)
                dscores = dscores + dm_all[:, blk] * decay
            hprev = hprev_ref[g]
            hprev_b = hprev.astype(BF16)
            dhn = dstate[g]
            dhn_b = dhn.astype(BF16)
            e_e, dec_e = cm['e_e'][:, cols], cm['dec_e'][:, cols]
            y_scan = _dot(m_all[...], x_bd[...], NN) + e_e * _dot(cg, hprev_b, NT)
            dye_b = (dyf * e_e).astype(BF16)
            dcg = _dot(dye_b, hprev_b, NN)
            dhp = _dot(dye_b, cg, TN)
            bdh = _dot(bg, dhn_b, NT)
            dbg = _dot((xdt * dec_e).astype(BF16), dhn_b, NN)
            dx_diag = _dot(mt_all[...], dy_bd[...], NN)
            dx = dec_e * bdh + dx_diag
            ds_b = dscores.astype(BF16)
            dcg = dcg + _dot(ds_b, bg, NN)
            dbg = dbg + _dot(ds_b, cg, TN)
            x_rounded = xdt_s[...].astype(F32)
            sums = sums + _hilo_dot(jnp.concatenate([dyf * y_scan, xdt * bdh, x_rounded * dx_diag, dx * xs, dyf * xs], axis=0), seg_sum)
            state_dot = state_dot + jnp.sum(_hilo_dot(dhn * hprev, seg_sum, TN), axis=0, keepdims=True)
            dxc_ref[:, cols] = (dx * cm['dt_e'][:, cols] + cm['dsk_e'][:, cols] * dyf).astype(dxc_ref.dtype)
            dxc_ref[:, d_inner + g * nst:d_inner + (g + 1) * nst] = dbg.astype(dxc_ref.dtype)
            dxc_ref[:, d_inner + gn + g * nst:d_inner + gn + (g + 1) * nst] = dcg.astype(dxc_ref.dtype)
            for hh in range(hpg):
                h = g * hpg + hh
                rows = slice(hh * p, (hh + 1) * p)
                dstate[g, rows, :] = jnp.exp(cm['a_last'][:, h:h + 1]) * dhn[rows] + dhp[rows]
        s_y, s_end, s_diag, s_dt, s_skip = (sums[k * q:(k + 1) * q] for k in range(5))
        dt, a, dec_end = cm['dt'], cm['a'], cm['dec_end']
        last_row = (lax.broadcasted_iota(jnp.int32, (q, 1), 0) == q - 1).astype(F32)
        da_last = jnp.sum(dec_end * s_end, axis=0, keepdims=True) + jnp.exp(cm['a_last']) * state_dot
        da = s_y - dec_end * s_end - s_diag + last_row * da_last
        ddta = _dot_hi(cm['triu'], da)
        ddt = s_dt + ddta * a
        d_a = jnp.sum(ddta * dt, axis=0, keepdims=True)
        ddt_raw = ddt * _sigmoid(dtraw_ref[...] + dtb_ref[...])
        ddtraw_ref[...] = ddt_raw
        ddtb_ref[...] += jnp.sum(ddt_raw, axis=0, keepdims=True)
        dalog_ref[...] += d_a * a
        ddsk_ref[...] += jnp.sum(s_skip, axis=0, keepdims=True)

    def rowmap(b, c):
        return b * nc + (nc - 1 - c)
    vec = pl.BlockSpec((1, LANES), lambda b, c: (0, 0))
    vec_shape = jax.ShapeDtypeStruct((1, LANES), F32)
    return _hosted_call(
        body,
        out_shape=[jax.ShapeDtypeStruct((t, c_dim), BF16), jax.ShapeDtypeStruct((t, LANES), F32), vec_shape, vec_shape, vec_shape],
        grid=(bl, nc),
        in_specs=[pl.BlockSpec((q, d_inner), lambda b, c: (rowmap(b, c), 0)),
                  pl.BlockSpec((q, d_inner), lambda b, c: (rowmap(b, c), 0)),
                  pl.BlockSpec((q, gn), lambda b, c: (rowmap(b, c), d_inner // gn)),
                  pl.BlockSpec((q, gn), lambda b, c: (rowmap(b, c), d_inner // gn + 1)),
                  pl.BlockSpec((q, LANES), lambda b, c: (rowmap(b, c), 0)),
                  pl.BlockSpec((None, grp, gw, nst), lambda b, c: (rowmap(b, c), 0, 0, 0)), vec, vec, vec],
        out_specs=[pl.BlockSpec((q, c_dim), lambda b, c: (rowmap(b, c), 0)),
                   pl.BlockSpec((q, LANES), lambda b, c: (rowmap(b, c), 0)), vec, vec, vec],
        scratch_shapes=[pltpu.VMEM((grp, gw, nst), F32), pltpu.VMEM((q, hpg * q), BF16), pltpu.VMEM((q, hpg * q), BF16),
                        pltpu.VMEM((hpg * q, gw), BF16), pltpu.VMEM((hpg * q, gw), BF16), pltpu.VMEM((q, gw), BF16)],
        operands=[dy, xc, xc, xc, dt_raw, hprev_all, dt_bias, a_log, d_skip], side=side, name=name)


def _gated_norm_fwd(y, z, ng, *, name):
    t, d = y.shape
    tm = _tile(t, 256)
    gw = d // SSM_GROUPS

    def body(y_ref, z_ref, ng_ref, o_ref):
        for g in range(SSM_GROUPS):
            sl = slice(g * gw, (g + 1) * gw)
            zv = z_ref[:, sl].astype(F32)
            yg = y_ref[:, sl] * (zv * _sigmoid(zv))
            r = lax.rsqrt(jnp.mean(yg * yg, axis=-1, keepdims=True) + RMS_EPS)
            o_ref[:, sl] = (yg * r * ng_ref[:, sl]).astype(o_ref.dtype)

    row = pl.BlockSpec((tm, d), lambda i: (i, 0))
    return pl.pallas_call(body, out_shape=jax.ShapeDtypeStruct((t, d), BF16), grid=(t // tm,),
                          in_specs=[row, row, _const_spec((1, d))], out_specs=row, name=name, compiler_params=_params("parallel"))(y, z, ng)


def _gated_norm_bwd(dyn, y, z, ng, *, name):
    t, d = y.shape
    tm = _tile(t, 256)
    gw = d // SSM_GROUPS

    def body(dyn_ref, y_ref, z_ref, ng_ref, dy_ref, dz_ref, dng_ref):
        @pl.when(pl.program_id(0) == 0)
        def _():
            dng_ref[...] = jnp.zeros_like(dng_ref)
        for g in range(SSM_GROUPS):
            sl = slice(g * gw, (g + 1) * gw)
            zv = z_ref[:, sl].astype(F32)
            yv = y_ref[:, sl]
            sg = _sigmoid(zv)
            sz = zv * sg
            yg = yv * sz
            r = lax.rsqrt(jnp.mean(yg * yg, axis=-1, keepdims=True) + RMS_EPS)
            yhat = yg * r
            dn = dyn_ref[:, sl].astype(F32)
            dyg_n = dn * ng_ref[:, sl]
            dyg = r * (dyg_n - yhat * jnp.mean(dyg_n * yhat, axis=-1, keepdims=True))
            dy_ref[:, sl] = (dyg * sz).astype(dy_ref.dtype)
            dz_ref[:, sl] = (dyg * yv * (sg * (1.0 + zv * (1.0 - sg)))).astype(dz_ref.dtype)
            dng_ref[:, sl] += jnp.sum(dn * yhat, axis=0, keepdims=True)

    row = pl.BlockSpec((tm, d), lambda i: (i, 0))
    shp = jax.ShapeDtypeStruct((t, d), BF16)
    return pl.pallas_call(body, out_shape=(shp, shp, jax.ShapeDtypeStruct((1, d), F32)), grid=(t // tm,),
                          in_specs=[row, row, row, _const_spec((1, d))], out_specs=(row, row, _const_spec((1, d))),
                          name=name, compiler_params=_params("arbitrary"))(dyn, y, z, ng)


def _bucket_onehot():
    blk = CHUNK
    qi = jnp.arange(blk)[:, None]
    kj = jnp.arange(2 * blk)[None, :]
    dist = jnp.maximum(qi + blk - kj, 0)
    max_exact = REL_BUCKETS // 2
    d = jnp.maximum(dist, 1).astype(F32)
    large = max_exact + (jnp.log(d / max_exact) / math.log(REL_MAX_DISTANCE / max_exact) * (REL_BUCKETS - max_exact)).astype(jnp.int32)
    large = jnp.minimum(large, REL_BUCKETS - 1)
    bucket = jnp.where(dist < max_exact, dist, large).reshape(-1)
    return (bucket[None, :] == jnp.arange(REL_BUCKETS)[:, None]).astype(F32)


def _small_mm_hi(a, b, dims, *, name):
    def body(a_ref, b_ref, o_ref):
        o_ref[...] = _dot_hi(a_ref[...], b_ref[...], dims)
    n = b.shape[0] if dims == NT else b.shape[1]
    return pl.pallas_call(body, out_shape=jax.ShapeDtypeStruct((a.shape[0], n), F32), name=name)(a, b)


def _attn_band_mask(n, rep):
    blk = CHUNK
    ii = lax.broadcasted_iota(jnp.int32, (rep * blk, 2 * blk), 0) & (blk - 1)
    jj = lax.broadcasted_iota(jnp.int32, (rep * blk, 2 * blk), 1)
    dist = ii + blk - jj
    in_window = jnp.logical_and(dist >= 0, dist < blk)
    return jnp.logical_and(in_window, jnp.logical_or(jj >= blk, n > 0))


def _attn_fwd(q, kv, bias, sinks, bl, *, name):
    t, qd = q.shape
    blk, hd = CHUNK, ATTN_HEAD_DIM
    kvd = ATTN_KV_HEADS * hd
    rep = ATTN_Q_HEADS // ATTN_KV_HEADS
    nb = t // bl // blk
    scale = hd ** -0.5

    def body(q_ref, kp_ref, kc_ref, vp_ref, vc_ref, bias_ref, sink_ref, o_ref, lse_ref):
        n = pl.program_id(1)
        mask = _attn_band_mask(n, rep)
        lse = jnp.zeros((blk, LANES), F32)
        for kvh in range(ATTN_KV_HEADS):
            ks = slice(kvh * hd, (kvh + 1) * hd)
            heads = range(kvh * rep, (kvh + 1) * rep)
            qs = jnp.concatenate([q_ref[:, h * hd:(h + 1) * hd] for h in heads], axis=0)
            kk = jnp.concatenate([kp_ref[:, ks], kc_ref[:, ks]], axis=0)
            vv = jnp.concatenate([vp_ref[:, ks], vc_ref[:, ks]], axis=0)
            bias = bias_ref[kvh * rep:(kvh + 1) * rep].reshape(rep * blk, 2 * blk)
            s = jnp.where(mask, _dot(qs, kk, NT) * scale + bias, NEG_INF)
            sink = jnp.concatenate([jnp.broadcast_to(sink_ref[:, h:h + 1], (blk, 1)) for h in heads], axis=0)
            m = jnp.maximum(jnp.max(s, axis=1, keepdims=True), sink)
            p = jnp.exp(s - m)
            den = jnp.sum(p, axis=1, keepdims=True) + jnp.exp(sink - m)
            o = _dot((p * (1.0 / den)).astype(BF16), vv, NN)
            lse_s = m + jnp.log(den)
            for r, h in enumerate(heads):
                o_ref[:, h * hd:(h + 1) * hd] = o[r * blk:(r + 1) * blk].astype(o_ref.dtype)
                lse = lse + lse_s[r * blk:(r + 1) * blk] * _lane_onehot(h)
        lse_ref[...] = lse

    def cur(b, n):
        return b * nb + n

    def prev(b, n):
        return b * nb + jnp.maximum(n - 1, 0)
    return pl.pallas_call(
        body, out_shape=(jax.ShapeDtypeStruct((t, qd), BF16), jax.ShapeDtypeStruct((t, LANES), F32)), grid=(bl, nb),
        in_specs=[pl.BlockSpec((blk, qd), lambda b, n: (cur(b, n), 0)),
                  pl.BlockSpec((blk, kvd), lambda b, n: (prev(b, n), 0)), pl.BlockSpec((blk, kvd), lambda b, n: (cur(b, n), 0)),
                  pl.BlockSpec((blk, kvd), lambda b, n: (prev(b, n), 1)), pl.BlockSpec((blk, kvd), lambda b, n: (cur(b, n), 1)),
                  _const_spec(bias.shape), _const_spec((1, LANES))],
        out_specs=(pl.BlockSpec((blk, qd), lambda b, n: (cur(b, n), 0)), pl.BlockSpec((blk, LANES), lambda b, n: (cur(b, n), 0))),
        name=name, compiler_params=_params("parallel", "arbitrary"))(q, kv, kv, kv, kv, bias, sinks)


def _attn_bwd(do, q, kv, lse, bias, sinks, bl, *, name):
    t, qd = q.shape
    blk, hd = CHUNK, ATTN_HEAD_DIM
    kvd = ATTN_KV_HEADS * hd
    rep = ATTN_Q_HEADS // ATTN_KV_HEADS
    s_len = t // bl
    nb = s_len // blk
    scale = hd ** -0.5

    def body(do_ref, q_ref, kp_ref, kc_ref, vp_ref, vc_ref, lse_ref, bias_ref, sink_ref, dq_ref, dkv_ref, dbias_ref, dsink_ref):
        n = pl.program_id(1)

        @pl.when(jnp.logical_and(pl.program_id(0) == 0, n == 0))
        def _():
            dbias_ref[...] = jnp.zeros_like(dbias_ref)
            dsink_ref[...] = jnp.zeros_like(dsink_ref)

        mask = _attn_band_mask(n, rep)
        r_cur = pl.multiple_of(n * blk, blk)
        r_prev = pl.multiple_of(jnp.maximum(n - 1, 0) * blk, blk)
        dsink = jnp.zeros((1, LANES), F32)
        for kvh in range(ATTN_KV_HEADS):
            ks = slice(kvh * hd, (kvh + 1) * hd)
            heads = range(kvh * rep, (kvh + 1) * rep)
            qs = jnp.concatenate([q_ref[:, h * hd:(h + 1) * hd] for h in heads], axis=0)
            dos = jnp.concatenate([do_ref[:, h * hd:(h + 1) * hd] for h in heads], axis=0)
            kk = jnp.concatenate([kp_ref[:, ks], kc_ref[:, ks]], axis=0)
            vv = jnp.concatenate([vp_ref[:, ks], vc_ref[:, ks]], axis=0)
            bias = bias_ref[kvh * rep:(kvh + 1) * rep].reshape(rep * blk, 2 * blk)
            lse = jnp.concatenate([lse_ref[:, h:h + 1] for h in heads], axis=0)
            sink = jnp.concatenate([jnp.broadcast_to(sink_ref[:, h:h + 1], (blk, 1)) for h in heads], axis=0)
            p = jnp.exp(jnp.where(mask, _dot(qs, kk, NT) * scale + bias, NEG_INF) - lse)
            dp = _dot(dos, vv, NT)
            delta = jnp.sum(p * dp, axis=1, keepdims=True)
            ds = p * (dp - delta)
            dsink_rows = jnp.exp(sink - lse) * delta
            ds_b = ds.astype(BF16)
            dq_s = _dot(ds_b, kk, NN) * scale
            dkk = _dot(ds_b, qs, TN) * scale
            dvv = _dot(p.astype(BF16), dos, TN)
            for r, h in enumerate(heads):
                rows = slice(r * blk, (r + 1) * blk)
                dbias_ref[h] += ds[rows]
                dq_ref[:, h * hd:(h + 1) * hd] = dq_s[rows].astype(dq_ref.dtype)
                dsink = dsink - jnp.sum(dsink_rows[rows], axis=0, keepdims=True) * _lane_onehot(h)
            vs = slice(kvd + kvh * hd, kvd + (kvh + 1) * hd)
            dkv_ref[pl.ds(r_cur, blk), ks] = dkk[blk:]
            dkv_ref[pl.ds(r_cur, blk), vs] = dvv[blk:]

            @pl.when(n > 0)
            def _():
                dkv_ref[pl.ds(r_prev, blk), ks] += dkk[:blk]
                dkv_ref[pl.ds(r_prev, blk), vs] += dvv[:blk]
        dsink_ref[...] += dsink

    def cur(b, n):
        return b * nb + n

    def prev(b, n):
        return b * nb + jnp.maximum(n - 1, 0)
    qspec = pl.BlockSpec((blk, qd), lambda b, n: (cur(b, n), 0))
    return pl.pallas_call(
        body,
        out_shape=(jax.ShapeDtypeStruct((t, qd), BF16), jax.ShapeDtypeStruct((t, 2 * kvd), F32),
                   jax.ShapeDtypeStruct(bias.shape, F32), jax.ShapeDtypeStruct((1, LANES), F32)),
        grid=(bl, nb),
        in_specs=[qspec, qspec,
                  pl.BlockSpec((blk, kvd), lambda b, n: (prev(b, n), 0)), pl.BlockSpec((blk, kvd), lambda b, n: (cur(b, n), 0)),
                  pl.BlockSpec((blk, kvd), lambda b, n: (prev(b, n), 1)), pl.BlockSpec((blk, kvd), lambda b, n: (cur(b, n), 1)),
                  pl.BlockSpec((blk, LANES), lambda b, n: (cur(b, n), 0)), _const_spec(bias.shape), _const_spec((1, LANES))],
        out_specs=(qspec, pl.BlockSpec((s_len, 2 * kvd), lambda b, n: (b, 0)), _const_spec(bias.shape), _const_spec((1, LANES))),
        name=name, compiler_params=_params("arbitrary", "arbitrary"))(do, q, kv, kv, kv, kv, lse, bias, sinks)


def _merge_fwd(yn, o, gs, ga, w_ssm, w_attn, w_out, h_in, g_post, *, name):
    t, d = h_in.shape
    tm = _tile(t, 256)

    def body(yn_ref, o_ref, gs_ref, ga_ref, ws_ref, wa_ref, wo_ref, hin_ref, gp_ref, ys_ref, ya_ref, mg_ref, mix_ref, hout_ref):
        ys = _dot(yn_ref[...], ws_ref[...], NN)
        ya = _dot(o_ref[...], wa_ref[...], NN)
        merged = (_sigmoid(gs_ref[...].astype(F32)) * ys + _sigmoid(ga_ref[...].astype(F32)) * ya).astype(BF16)
        mix = _dot(merged, wo_ref[...], NN)
        ys_ref[...] = ys.astype(BF16)
        ya_ref[...] = ya.astype(BF16)
        mg_ref[...] = merged
        mix_ref[...] = mix
        hout_ref[...] = _rms_residual(mix, hin_ref[...], gp_ref[...], 1.0)

    def row(w):
        return pl.BlockSpec((tm, w), lambda i: (i, 0))
    bshape = jax.ShapeDtypeStruct((t, d), BF16)
    fshape = jax.ShapeDtypeStruct((t, d), F32)
    return pl.pallas_call(
        body, out_shape=(bshape, bshape, bshape, fshape, fshape), grid=(t // tm,),
        in_specs=[row(yn.shape[1]), row(o.shape[1]), row(d), row(d), _const_spec(w_ssm.shape), _const_spec(w_attn.shape),
                  _const_spec(w_out.shape), row(d), _const_spec((1, d))],
        out_specs=(row(d),) * 5, name=name, compiler_params=_params("parallel"))(yn, o, gs, ga, w_ssm, w_attn, w_out, h_in, g_post)


def _merge_bwd(dh, mix, g_post, gs, ga, ys, ya, w_ssm, w_attn, w_out, *, name):
    t, d = mix.shape
    tm = _tile(t, 256)
    d_ssm, d_attn = w_ssm.shape[0], w_attn.shape[0]

    def body(dh_ref, mix_ref, gp_ref, gs_ref, ga_ref, ys_ref, ya_ref, ws_ref, wa_ref, wo_ref,
             dmix_ref, dys_ref, dya_ref, dgs_ref, dga_ref, dyn_ref, do_ref, dgp_ref):
        @pl.when(pl.program_id(0) == 0)
        def _():
            dgp_ref[...] = jnp.zeros_like(dgp_ref)
        mv = mix_ref[...]
        dy = dh_ref[...]
        r = lax.rsqrt(jnp.mean(mv * mv, axis=-1, keepdims=True) + RMS_EPS)
        mhat = mv * r
        dyg = dy * gp_ref[...]
        dmix = (r * (dyg - mhat * jnp.mean(dyg * mhat, axis=-1, keepdims=True))).astype(BF16)
        dgp_ref[...] += jnp.sum(dy * mhat, axis=0, keepdims=True)
        dmix_ref[...] = dmix
        dmerged = _dot(dmix, wo_ref[...], NT)
        sgs = _sigmoid(gs_ref[...].astype(F32))
        sga = _sigmoid(ga_ref[...].astype(F32))
        dys = (dmerged * sgs).astype(BF16)
        dya = (dmerged * sga).astype(BF16)
        dys_ref[...] = dys
        dya_ref[...] = dya
        dgs_ref[...] = (dmerged * ys_ref[...].astype(F32) * sgs * (1.0 - sgs)).astype(BF16)
        dga_ref[...] = (dmerged * ya_ref[...].astype(F32) * sga * (1.0 - sga)).astype(BF16)
        dyn_ref[...] = _dot(dys, ws_ref[...], NT).astype(BF16)
        do_ref[...] = _dot(dya, wa_ref[...], NT).astype(BF16)

    def row(w):
        return pl.BlockSpec((tm, w), lambda i: (i, 0))

    def bshape(w):
        return jax.ShapeDtypeStruct((t, w), BF16)
    return pl.pallas_call(
        body, out_shape=(bshape(d),) * 5 + (bshape(d_ssm), bshape(d_attn), jax.ShapeDtypeStruct((1, d), F32)), grid=(t // tm,),
        in_specs=[row(d), row(d), _const_spec((1, d)), row(d), row(d), row(d), row(d),
                  _const_spec(w_ssm.shape), _const_spec(w_attn.shape), _const_spec(w_out.shape)],
        out_specs=(row(d),) * 5 + (row(d_ssm), row(d_attn), _const_spec((1, d))),
        name=name, compiler_params=_params("arbitrary"))(dh, mix, g_post, gs, ga, ys, ya, w_ssm, w_attn, w_out)


def _loss_grad(h, target, *, name):
    t, d = h.shape
    tm = _tile(t, 512)

    def body(h_ref, t_ref, dh_ref, loss_ref):
        @pl.when(pl.program_id(0) == 0)
        def _():
            loss_ref[...] = jnp.zeros_like(loss_ref)
        e = h_ref[...] - t_ref[...]
        dh_ref[...] = e * (1.0 / d)
        per_row = jnp.sum(e * e, axis=1, keepdims=True) * (1.0 / d)
        loss_ref[...] += 0.5 * jnp.sum(per_row, axis=0, keepdims=True)

    row = pl.BlockSpec((tm, d), lambda i: (i, 0))
    return pl.pallas_call(body, out_shape=(jax.ShapeDtypeStruct((t, d), F32), jax.ShapeDtypeStruct((1, LANES), F32)), grid=(t // tm,),
                          in_specs=[row, row], out_specs=(row, _const_spec((1, LANES))),
                          name=name, compiler_params=_params("arbitrary"))(h, target)


def _adamw(w, g, m, v, *, name):
    r, c = w.shape
    tm = _tile(r, 256)
    c1 = 1.0 - ADAM_B1 ** ADAM_STEP
    c2 = 1.0 - ADAM_B2 ** ADAM_STEP

    def body(w_ref, g_ref, m_ref, v_ref, d_ref, mo_ref, vo_ref):
        gv = g_ref[...]
        mn = ADAM_B1 * m_ref[...] + (1.0 - ADAM_B1) * gv
        vn = ADAM_B2 * v_ref[...] + (1.0 - ADAM_B2) * (gv * gv)
        mo_ref[...] = mn
        vo_ref[...] = vn
        d_ref[...] = -ADAM_LR * ((mn / c1) / (jnp.sqrt(vn / c2) + ADAM_EPS) + ADAM_WD * w_ref[...])

    blk = pl.BlockSpec((tm, c), lambda i: (i, 0))
    shp = jax.ShapeDtypeStruct((r, c), F32)
    return pl.pallas_call(body, out_shape=(shp, shp, shp), grid=(r // tm,), in_specs=[blk] * 4, out_specs=(blk,) * 3,
                          name=name, compiler_params=_params("parallel"))(w, g, m, v)


def _position():
    return lax.axis_index("x"), lax.axis_index("y"), lax.axis_index("c")


def _gather_exchange(shards):
    na = len(shards)

    def plan(ins, outs, sems):
        send_sems, recv_sems, local_sems = sems
        x, y, c = _position()
        me, sibling = (x, y, c), (x, y, 1 - c)
        chips = [(1 - x, y), (x, 1 - y), (1 - x, 1 - y)]

        def slot(a, pos):
            return outs[a].at[4 * pos[0] + 2 * pos[1] + pos[2]]

        def copy(a, k, block, to, src=None):
            return pltpu.make_async_remote_copy(
                src_ref=slot(a, block) if src is None else src, dst_ref=slot(a, block),
                send_sem=send_sems.at[a, k], recv_sem=recv_sems.at[a, k], device_id=to, device_id_type=MESH)

        mine = [pltpu.make_async_copy(ins[a], slot(a, me), local_sems.at[a]) for a in range(na)]
        first = []
        for a in range(na):
            first.append(copy(a, 0, me, sibling, src=ins[a]))
            first += [copy(a, 1 + j, me, (*chip, c), src=ins[a]) for j, chip in enumerate(chips)]
        return me, sibling, chips, copy, mine, first

    def start(ins, outs, sems):
        *_, mine, first = plan(ins, outs, sems)
        for cp in mine + first:
            cp.start()

    def finish(ins, outs, sems):
        me, sibling, chips, copy, mine, first = plan(ins, outs, sems)
        c = me[2]
        passed = []
        for a in range(na):
            for j, chip in enumerate(chips):
                copy(a, 1 + j, (*chip, c), me).wait_recv()
                fwd = copy(a, 4 + j, (*chip, c), sibling)
                fwd.start()
                passed.append(fwd)
        for a in range(na):
            copy(a, 0, sibling, me).wait_recv()
            for j, chip in enumerate(chips):
                copy(a, 4 + j, (*chip, 1 - c), me).wait_recv()
        for cp in first + passed:
            cp.wait_send()
        for cp in mine:
            cp.wait()

    return _Exchange(list(shards), [jax.ShapeDtypeStruct((N_DEV,) + s.shape, s.dtype) for s in shards],
                     [pltpu.SemaphoreType.DMA((na, 7)), pltpu.SemaphoreType.DMA((na, 7)), pltpu.SemaphoreType.DMA((na,))],
                     start, finish)


def _scatter_exchange(arrays):
    na = len(arrays)

    def copies(ins, outs, sems):
        send_sems, recv_sems = sems
        x, y, c = _position()
        out = []
        for a in range(na):
            for k in range(7):
                flip = k + 1
                peer = (x ^ (flip >> 2), y ^ ((flip >> 1) & 1), c ^ (flip & 1))
                peer_block = 4 * peer[0] + 2 * peer[1] + peer[2]
                out.append(pltpu.make_async_remote_copy(
                    src_ref=ins[a].at[peer_block], dst_ref=outs[a].at[k],
                    send_sem=send_sems.at[a, k], recv_sem=recv_sems.at[a, k], device_id=peer, device_id_type=MESH))
        return out

    def start(ins, outs, sems):
        for cp in copies(ins, outs, sems):
            cp.start()

    def finish(ins, outs, sems):
        for cp in copies(ins, outs, sems):
            cp.wait()

    return _Exchange(list(arrays), [jax.ShapeDtypeStruct((7,) + s.shape[1:], s.dtype) for s in arrays],
                     [pltpu.SemaphoreType.DMA((na, 7)), pltpu.SemaphoreType.DMA((na, 7))], start, finish)


def _exchange_alone(side, *, name):
    n_in = len(side.arrays)
    n_out = len(side.out_shape)

    def body(*refs):
        ins, outs, sems = refs[:n_in], refs[n_in:n_in + n_out], refs[n_in + n_out:]
        side.start(ins, outs, sems)
        side.finish(ins, outs, sems)

    hbm = pl.BlockSpec(memory_space=pl.ANY)
    return pl.pallas_call(body, out_shape=tuple(side.out_shape), in_specs=[hbm] * n_in, out_specs=tuple([hbm] * n_out),
                          scratch_shapes=list(side.scratch), name=name)(*side.arrays)


def _reduce_blocks(own, recv, *, name):
    r, c = own.shape
    tm = _tile(r, 256)

    def body(own_ref, recv_ref, o_ref):
        acc = own_ref[...]
        for k in range(7):
            acc = acc + recv_ref[k].astype(F32)
        o_ref[...] = acc

    return pl.pallas_call(
        body, out_shape=jax.ShapeDtypeStruct((r, c), F32), grid=(r // tm,),
        in_specs=[pl.BlockSpec((tm, c), lambda i: (i, 0)), pl.BlockSpec((7, tm, c), lambda i: (0, i, 0))],
        out_specs=pl.BlockSpec((tm, c), lambda i: (i, 0)), name=name, compiler_params=_params("parallel"))(own, recv)


def _all_reduce_small(vec, *, name):
    r, c = vec.shape

    def body(v_ref, o_ref, buf, send_sems, recv_sems):
        x, y, c_ = _position()
        me = 4 * x + 2 * y + c_
        buf[me] = v_ref[...]
        copies = []
        for k in range(7):
            flip = k + 1
            peer = (x ^ (flip >> 2), y ^ ((flip >> 1) & 1), c_ ^ (flip & 1))
            cp = pltpu.make_async_remote_copy(
                src_ref=v_ref, dst_ref=buf.at[me], send_sem=send_sems.at[k], recv_sem=recv_sems.at[k],
                device_id=peer, device_id_type=MESH)
            cp.start()
            copies.append(cp)
        for cp in copies:
            cp.wait()
        acc = buf[0]
        for d in range(1, N_DEV):
            acc = acc + buf[d]
        o_ref[...] = acc

    vm = pl.BlockSpec(memory_space=pltpu.VMEM)
    return pl.pallas_call(
        body, out_shape=jax.ShapeDtypeStruct((r, c), F32), in_specs=[vm], out_specs=vm,
        scratch_shapes=[pltpu.VMEM((N_DEV, r, c), F32), pltpu.SemaphoreType.DMA((7,)), pltpu.SemaphoreType.DMA((7,))],
        name=name)(vec)


def _pad_lanes(v, width=LANES):
    return jnp.pad(v, ((0, 0), (0, width - v.shape[1])))


def kernel(x, ffn1_pre_g, ffn1_w_gate, ffn1_w_up, ffn1_w_down, ffn1_post_g, mix_pre_g, w_in, conv_w, conv_b, dt_bias, a_log, d_skip, ssm_norm_g, w_ssm_proj, attn_sinks, rel_bias_table, w_attn_proj, w_out, mix_post_g, ffn2_pre_g, ffn2_w_gate, ffn2_w_up, ffn2_w_down, ffn2_post_g, loss_target, m_ffn1_pre_g, m_ffn1_w_gate, m_ffn1_w_up, m_ffn1_w_down, m_ffn1_post_g, m_mix_pre_g, m_w_in, m_conv_w, m_conv_b, m_dt_bias, m_a_log, m_d_skip, m_ssm_norm_g, m_w_ssm_proj, m_attn_sinks, m_rel_bias_table, m_w_attn_proj, m_w_out, m_mix_post_g, m_ffn2_pre_g, m_ffn2_w_gate, m_ffn2_w_up, m_ffn2_w_down, m_ffn2_post_g, v_ffn1_pre_g, v_ffn1_w_gate, v_ffn1_w_up, v_ffn1_w_down, v_ffn1_post_g, v_mix_pre_g, v_w_in, v_conv_w, v_conv_b, v_dt_bias, v_a_log, v_d_skip, v_ssm_norm_g, v_w_ssm_proj, v_attn_sinks, v_rel_bias_table, v_w_attn_proj, v_w_out, v_mix_post_g, v_ffn2_pre_g, v_ffn2_w_gate, v_ffn2_w_up, v_ffn2_w_down, v_ffn2_post_g):
    args = dict(locals())
    weight_names = ['ffn1_pre_g', 'ffn1_w_gate', 'ffn1_w_up', 'ffn1_w_down', 'ffn1_post_g', 'mix_pre_g', 'w_in', 'conv_w', 'conv_b',
                    'dt_bias', 'a_log', 'd_skip', 'ssm_norm_g', 'w_ssm_proj', 'attn_sinks', 'rel_bias_table', 'w_attn_proj', 'w_out',
                    'mix_post_g', 'ffn2_pre_g', 'ffn2_w_gate', 'ffn2_w_up', 'ffn2_w_down', 'ffn2_post_g']
    col_sharded = ('ffn1_w_gate', 'ffn1_w_up', 'w_in', 'ffn2_w_gate', 'ffn2_w_up')
    row_sharded = ('ffn1_w_down', 'w_ssm_proj', 'w_attn_proj', 'w_out', 'ffn2_w_down')
    big = col_sharded + row_sharded

    bl, s_len, d = x.shape
    t = bl * s_len
    d_inner = ssm_norm_g.shape[1]
    n_heads = dt_bias.shape[1]
    gn = SSM_GROUPS * SSM_STATE
    conv_dim = d_inner + 2 * gn
    q_dim = ATTN_Q_HEADS * ATTN_HEAD_DIM
    kv_dim = ATTN_KV_HEADS * ATTN_HEAD_DIM

    def local_2d(name, a):
        a = a[0]
        return a.T if name in col_sharded else a

    ffn1_names = ('ffn1_w_gate', 'ffn1_w_up', 'ffn1_w_down')
    ffn2_names = ('ffn2_w_gate', 'ffn2_w_up', 'ffn2_w_down')
    mixer_names = ('w_ssm_proj', 'w_attn_proj', 'w_out')

    def shard(n):
        return local_2d(n, args[n]).astype(BF16)

    def rows(g):
        return g.reshape(N_DEV * g.shape[1], g.shape[2])

    first_names = ffn1_names[:2]
    full = {n: rows(g) for n, g in zip(first_names, _exchange_alone(_gather_exchange([shard(n) for n in first_names]), name="gather_ffn1"))}

    x2 = x.reshape(t, d)
    tgt2 = loss_target.reshape(t, d)

    h1, saved1, got_in, got_mixer = _ffn_forward(
        x2, ffn1_pre_g, full['ffn1_w_gate'], full['ffn1_w_up'], lambda got: rows(got[2]), ffn1_post_g, "ffn1",
        side_up=_gather_exchange([shard('w_in'), conv_w[0], shard('ffn1_w_down')]),
        side_down=_gather_exchange([shard(n) for n in mixer_names]))
    full['ffn1_w_down'] = rows(got_in[2])
    full.update({n: rows(g) for n, g in zip(mixer_names, got_mixer)})
    conv_w_full = jnp.transpose(got_in[1], (1, 0, 2)).reshape(SSM_CONV, conv_dim)

    win_t = rows(got_in[0])
    dt_lo = 2 * d + d_inner + conv_dim
    n_main = win_t.shape[0] - n_heads
    win_p = jnp.concatenate([win_t[:dt_lo], win_t[dt_lo + n_heads:], win_t[dt_lo:dt_lo + n_heads],
                             jnp.zeros((LANES - n_heads, d), BF16)], axis=0)
    off = {'gs': 0, 'ga': d, 'z': 2 * d, 'xbc': 2 * d + d_inner, 'q': dt_lo, 'kv': dt_lo + q_dim, 'dt': n_main}

    u = _rms_fwd(h1, mix_pre_g, name="mix_prenorm")

    def proj(key, width, out_dtype=BF16):
        tn = _tile(math.gcd(width, off[key]) if off[key] else width, 1024, LANES)
        assert off[key] % tn == 0 and width % tn == 0
        return _mm_nt(u, win_p, row_off=off[key] // tn, n_out=width, out_dtype=out_dtype, tn=tn, name=f"proj_{key}")

    gs, ga, z, xbc = proj('gs', d), proj('ga', d), proj('z', d_inner), proj('xbc', conv_dim)
    q, kv = proj('q', q_dim), proj('kv', 2 * kv_dim)
    dt_raw = proj('dt', LANES, F32)

    dtb_p, alog_p, dsk_p, sinks_p = _pad_lanes(dt_bias), _pad_lanes(a_log), _pad_lanes(d_skip), _pad_lanes(attn_sinks)
    xc = _conv_fwd(xbc, conv_w_full, conv_b, bl, name="conv_fwd")
    (y, hprev), got_ffn2 = _ssd_fwd(xc, dt_raw, dtb_p, alog_p, dsk_p, bl, n_heads,
                                    side=_gather_exchange([shard(n) for n in ffn2_names]), name="ssd_fwd")
    full.update({n: rows(g) for n, g in zip(ffn2_names, got_ffn2)})
    yn = _gated_norm_fwd(y, z, ssm_norm_g, name="gated_norm_fwd")

    onehot = _bucket_onehot()
    bias = _small_mm_hi(rel_bias_table.T, onehot, NN, name="rel_bias").reshape(ATTN_Q_HEADS, CHUNK, 2 * CHUNK)
    o, lse = _attn_fwd(q, kv, bias, sinks_p, bl, name="attn_fwd")

    ys, ya, merged, mix, h2 = _merge_fwd(yn, o, gs, ga, full['w_ssm_proj'], full['w_attn_proj'], full['w_out'], h1, mix_post_g,
                                         name="merge_fwd")

    h3, saved2, _, _ = _ffn_forward(h2, ffn2_pre_g, full['ffn2_w_gate'], full['ffn2_w_up'], full['ffn2_w_down'], ffn2_post_g, "ffn2")

    dh3, loss_vec = _loss_grad(h3, tgt2, name="loss")
    loss = lax.psum(loss_vec[0, 0], ("x", "y", "c"))

    grads, own, wire, received = {}, {}, {}, {}
    dh2, grads['ffn2_pre_g'], grads['ffn2_post_g'], g32, g16, _ = _ffn_backward(
        dh3, saved2, ffn2_pre_g, full['ffn2_w_gate'], full['ffn2_w_up'], full['ffn2_w_down'], ffn2_post_g, "ffn2")
    own.update(zip(ffn2_names, map(_stack8, g32)))
    wire.update(zip(ffn2_names, map(_stack8, g16)))

    dmix, dys, dya, dgs, dga, dyn, do, grads['mix_post_g'] = _merge_bwd(
        dh2, mix, mix_post_g, gs, ga, ys, ya, full['w_ssm_proj'], full['w_attn_proj'], full['w_out'], name="merge_bwd")
    for n, (lhs, rhs) in zip(mixer_names, ((yn, dys), (o, dya), (merged, dmix))):
        g32_, g16_, _ = _mm_tn([lhs], rhs, name=f"d{n}")
        own[n], wire[n] = _stack8(g32_), _stack8(g16_)

    dq, dkv, dbias, dsinks = _attn_bwd(do, q, kv, lse, bias, sinks_p, bl, name="attn_bwd")
    d_table = _small_mm_hi(onehot, dbias.reshape(ATTN_Q_HEADS, -1), NT, name="rel_bias_bwd")

    dy, dz, grads['ssm_norm_g'] = _gated_norm_bwd(dyn, y, z, ssm_norm_g, name="gated_norm_bwd")

    first_group = ffn2_names + mixer_names
    (dxc, ddt_raw, ddtb, dalog, ddsk), got = _ssd_bwd(dy, xc, dt_raw, hprev, dtb_p, alog_p, dsk_p, bl, n_heads,
                                                      side=_scatter_exchange([wire[n] for n in first_group]), name="ssd_bwd")
    received.update(zip(first_group, got))
    dxbc, dconv_w8, grads['conv_b'] = _conv_bwd(dxc, xbc, conv_w_full, conv_b, bl, name="conv_bwd")

    wide32, wide16, _ = _mm_tn([dgs, dga, dz, dxbc, dq], u, name="dw_in")
    kv32, kv16, _ = _mm_tn([dkv], u, name="dw_in_kv")
    dt32, dt16, _ = _mm_tn([ddt_raw], u, name="dw_in_dt")

    def original_order(wide, kv_part, dt_part):
        return _stack8(jnp.concatenate([wide[:dt_lo], dt_part[:n_heads], wide[dt_lo:], kv_part], axis=0))
    own['w_in'], wire['w_in'] = original_order(wide32, kv32, dt32), original_order(wide16, kv16, dt16)
    own['conv_w'] = jnp.transpose(dconv_w8[:SSM_CONV].reshape(SSM_CONV, N_DEV, conv_dim // N_DEV), (1, 0, 2))

    segs = [(g_, 0, off[k_]) for g_, k_ in zip([dgs, dga, dz, dxbc, dq, dkv, ddt_raw], ('gs', 'ga', 'z', 'xbc', 'q', 'kv', 'dt'))]
    dh1, grads['mix_pre_g'], got = _mm_nn_rmsbwd(segs, [win_p], h1, mix_pre_g, dh2,
                                                 side=_scatter_exchange([wire['w_in'], own['conv_w']]), name="mix_du")
    received.update(zip(('w_in', 'conv_w'), got))

    dx2, grads['ffn1_pre_g'], grads['ffn1_post_g'], g32, _, got = _ffn_backward(
        dh1, saved1, ffn1_pre_g, full['ffn1_w_gate'], full['ffn1_w_up'], full['ffn1_w_down'], ffn1_post_g, "ffn1", chain=True)
    own.update(zip(ffn1_names, map(_stack8, g32)))
    received.update(zip(ffn1_names, got))

    me = 4 * lax.axis_index("x") + 2 * lax.axis_index("y") + lax.axis_index("c")

    def own_block(a):
        return lax.dynamic_index_in_dim(a, me, 0, keepdims=False)
    reduced = {n: _reduce_blocks(own_block(own[n]), received[n], name=f"reduce_{n}") for n in big}
    conv_sum = _reduce_blocks(own_block(own['conv_w']), received['conv_w'], name="reduce_conv_w")

    grads['dt_bias'], grads['a_log'], grads['d_skip'] = ddtb[:, :n_heads], dalog[:, :n_heads], ddsk[:, :n_heads]
    grads['attn_sinks'] = dsinks[:, :ATTN_Q_HEADS]
    grads['rel_bias_table'] = d_table
    small = [n for n in weight_names if n not in big and n != 'conv_w']
    flat = jnp.concatenate([grads[n].reshape(-1) for n in small])
    n_small = flat.shape[0]
    n_rows = -(-n_small // (8 * LANES)) * 8
    flat = jnp.pad(flat, (0, n_rows * LANES - n_small)).reshape(n_rows, LANES)
    summed = _all_reduce_small(flat, name="allreduce_small").reshape(-1)
    pos = 0
    for n in small:
        size = grads[n].size
        grads[n] = summed[pos:pos + size].reshape(args[n].shape)
        pos += size

    out_g, out_d, out_m, out_v = {}, {}, {}, {}
    for n in big:
        w2, m2, v2 = local_2d(n, args[n]), local_2d(n, args['m_' + n]), local_2d(n, args['v_' + n])
        dlt, mn, vn = _adamw(w2, reduced[n], m2, v2, name=f"adamw_{n}")

        def back(a, n=n):
            return (a.T if n in col_sharded else a)[None]
        out_g[n], out_d[n], out_m[n], out_v[n] = back(reduced[n]), back(dlt), back(mn), back(vn)

    def pack(prefix):
        vals = [(grads[n] if prefix == 'g' else args[prefix + n]).reshape(-1) for n in small]
        vals.append((conv_sum if prefix == 'g' else args[prefix + 'conv_w']).reshape(-1))
        flat_ = jnp.concatenate(vals)
        rows_ = -(-flat_.shape[0] // (8 * LANES)) * 8
        return jnp.pad(flat_, (0, rows_ * LANES - flat_.shape[0])).reshape(rows_, LANES)

    g_small = pack('g')
    d_small, m_small, v_small = _adamw(pack(''), g_small, pack('m_'), pack('v_'), name="adamw_small")
    pos = 0
    for n in small + ['conv_w']:
        shape = args[n].shape
        size = int(np.prod(shape))
        for dst, src in ((out_g, g_small), (out_d, d_small), (out_m, m_small), (out_v, v_small)):
            dst[n] = src.reshape(-1)[pos:pos + size].reshape(shape)
        pos += size

    grad_x = dx2.reshape(bl, s_len, d)
    return (loss, grad_x, *[out_g[n] for n in weight_names], *[out_d[n] for n in weight_names],
            *[out_m[n] for n in weight_names], *[out_v[n] for n in weight_names])
```

```python
import functools
import math

import numpy as np
import jax
import jax.numpy as jnp
from jax import lax
from jax.experimental import pallas as pl
from jax.experimental.pallas import tpu as pltpu

F32 = jnp.float32
BF16 = jnp.bfloat16
MESH = pl.DeviceIdType.MESH
N_DEV = 8

SSM_HEAD_DIM = 64
SSM_GROUPS = 4
SSM_STATE = 128
SSM_CONV = 4
CHUNK = 128
ATTN_HEAD_DIM = 64
ATTN_Q_HEADS = 16
ATTN_KV_HEADS = 4
REL_BUCKETS = 32
REL_MAX_DISTANCE = 128
RMS_EPS = 1e-6
FFN_RESIDUAL_WEIGHT = 0.5
ADAM_LR, ADAM_B1, ADAM_B2, ADAM_EPS, ADAM_WD, ADAM_STEP = 0.001, 0.9, 0.999, 1e-08, 0.01, 10

LANES = 128
VMEM_LIMIT_BYTES = 56 * 1024 * 1024
FFN_COL_TILE = 1408

NEG_INF = float("-inf")


def _params(*sem):
    return pltpu.CompilerParams(dimension_semantics=sem, vmem_limit_bytes=VMEM_LIMIT_BYTES)


def _tile(n, pref, mult=8):
    if n <= pref:
        return n
    t = (pref // mult) * mult
    while t >= mult:
        if n % t == 0:
            return t
        t -= mult
    return n


def _sigmoid(x):
    return 1.0 / (1.0 + jnp.exp(-x))


def _dot(a, b, dims):
    return lax.dot_general(a, b, (dims, ((), ())), preferred_element_type=F32)


NN = ((1,), (0,))
NT = ((1,), (1,))
TN = ((0,), (0,))


def _dot_hi(a, b, dims=NN):
    return lax.dot_general(a, b, (dims, ((), ())), preferred_element_type=F32, precision=lax.Precision.HIGHEST)


def _const_spec(shape):
    nd = len(shape)
    return pl.BlockSpec(shape, lambda *_: (0,) * nd)


def _resident_spec(shape):
    nd = len(shape)
    return pl.BlockSpec(shape, lambda *_: (0,) * nd, pipeline_mode=pl.Buffered(1))


class _Exchange:
    def __init__(self, arrays, out_shape, scratch, start, finish):
        self.arrays, self.out_shape, self.scratch, self.start, self.finish = arrays, out_shape, scratch, start, finish


def _hosted_call(body, *, grid, in_specs, out_specs, out_shape, scratch_shapes, operands, side, name):
    in_specs, out_specs, out_shape, scratch_shapes = list(in_specs), list(out_specs), list(out_shape), list(scratch_shapes)
    sem = ("arbitrary",) * len(grid)
    if side is None:
        outs = pl.pallas_call(body, out_shape=tuple(out_shape), grid=grid, in_specs=in_specs, out_specs=tuple(out_specs),
                              scratch_shapes=scratch_shapes, name=name, compiler_params=_params(*sem))(*operands)
        return tuple(outs), ()
    n_in, n_out, n_scr = len(in_specs), len(out_shape), len(scratch_shapes)
    s_in, s_out = len(side.arrays), len(side.out_shape)

    def wrapped(*refs):
        refs = list(refs)
        main_in, side_in = refs[:n_in], refs[n_in:n_in + s_in]
        o0 = n_in + s_in
        main_out, side_out = refs[o0:o0 + n_out], refs[o0 + n_out:o0 + n_out + s_out]
        c0 = o0 + n_out + s_out
        main_scr, side_scr = refs[c0:c0 + n_scr], refs[c0 + n_scr:]
        ids = [pl.program_id(ax) for ax in range(len(grid))]
        first = functools.reduce(jnp.logical_and, [i == 0 for i in ids])
        last = functools.reduce(jnp.logical_and, [i == g - 1 for i, g in zip(ids, grid)])

        @pl.when(first)
        def _():
            side.start(side_in, side_out, side_scr)

        body(*main_in, *main_out, *main_scr)

        @pl.when(last)
        def _():
            side.finish(side_in, side_out, side_scr)

    hbm = pl.BlockSpec(memory_space=pl.ANY)
    outs = pl.pallas_call(
        wrapped, out_shape=tuple(out_shape + list(side.out_shape)), grid=grid,
        in_specs=in_specs + [hbm] * s_in, out_specs=tuple(out_specs + [hbm] * s_out),
        scratch_shapes=scratch_shapes + list(side.scratch), name=name, compiler_params=_params(*sem))(*operands, *side.arrays)
    return tuple(outs[:n_out]), tuple(outs[n_out:])


def _mm_nt(a, bt, *, row_off=0, n_out=None, out_dtype=BF16, tm=512, tn=512, name):
    m, k = a.shape
    n_out = bt.shape[0] if n_out is None else n_out
    tm, tn = _tile(m, tm), _tile(n_out, tn, LANES)

    def body(a_ref, b_ref, o_ref):
        o_ref[...] = _dot(a_ref[...].astype(BF16), b_ref[...].astype(BF16), NT).astype(o_ref.dtype)

    return pl.pallas_call(
        body, out_shape=jax.ShapeDtypeStruct((m, n_out), out_dtype), grid=(n_out // tn, m // tm),
        in_specs=[pl.BlockSpec((tm, k), lambda j, i: (i, 0)), pl.BlockSpec((tn, k), lambda j, i: (j + row_off, 0))],
        out_specs=pl.BlockSpec((tm, tn), lambda j, i: (i, j)),
        name=name, compiler_params=_params("parallel", "arbitrary"))(a, bt)


def _mm_tn(a_list, b, *, tm=1408, tk=1024, side=None, name):
    t, n = b.shape
    tk = _tile(t, tk)
    nk = t // tk
    widths = [a.shape[1] for a in a_list]
    tm = _tile(math.gcd(*widths), tm, LANES)
    assert all(w % tm == 0 for w in widths)
    starts = np.cumsum([0] + [w // tm for w in widths])
    nseg = len(a_list)

    def a_spec(s):
        lo, hi = int(starts[s]), int(starts[s + 1])

        def idx(i, k):
            active = jnp.logical_and(i >= lo, i < hi)
            return (jnp.where(active, k, 0), jnp.clip(i - lo, 0, hi - lo - 1))
        return pl.BlockSpec((tk, tm), idx)

    def body(*refs):
        a_refs, b_ref, o_ref, o16_ref, acc = refs[:nseg], refs[nseg], refs[nseg + 1], refs[nseg + 2], refs[nseg + 3]
        i, k = pl.program_id(0), pl.program_id(1)

        @pl.when(k == 0)
        def _():
            acc[...] = jnp.zeros_like(acc)

        bv = b_ref[...].astype(BF16)
        for s in range(nseg):
            lo, hi = int(starts[s]), int(starts[s + 1])

            @pl.when(jnp.logical_and(i >= lo, i < hi))
            def _(s=s):
                acc[...] += _dot(a_refs[s][...].astype(BF16), bv, TN)

        @pl.when(k == nk - 1)
        def _():
            o_ref[...] = acc[...]
            o16_ref[...] = acc[...].astype(BF16)

    rows = int(starts[-1]) * tm
    o_spec = pl.BlockSpec((tm, n), lambda i, k: (i, 0))
    (o32, o16), got = _hosted_call(
        body, out_shape=[jax.ShapeDtypeStruct((rows, n), F32), jax.ShapeDtypeStruct((rows, n), BF16)], grid=(int(starts[-1]), nk),
        in_specs=[a_spec(s) for s in range(nseg)] + [pl.BlockSpec((tk, n), lambda i, k: (k, 0))],
        out_specs=[o_spec, o_spec], scratch_shapes=[pltpu.VMEM((tm, n), F32)], operands=list(a_list) + [b], side=side, name=name)
    return o32, o16, got


def _mm_nn_rmsbwd(segs, weights, x, g, dres, *, tm=256, side=None, name):
    t, d = x.shape
    tm = _tile(t, tm)
    nseg, nw = len(segs), len(weights)

    def body(*refs):
        a_refs, w_refs = refs[:nseg], refs[nseg:nseg + nw]
        x_ref, g_ref, dres_ref, dx_ref, dg_ref = refs[nseg + nw:]

        @pl.when(pl.program_id(0) == 0)
        def _():
            dg_ref[...] = jnp.zeros_like(dg_ref)

        dn = None
        for s, (a, w_idx, row0) in enumerate(segs):
            part = _dot(a_refs[s][...].astype(BF16), w_refs[w_idx][row0:row0 + a.shape[1], :], NN)
            dn = part if dn is None else dn + part
        xv = x_ref[...]
        r = lax.rsqrt(jnp.mean(xv * xv, axis=-1, keepdims=True) + RMS_EPS)
        xhat = xv * r
        dyg = dn * g_ref[...]
        dx_ref[...] = dres_ref[...] + r * (dyg - xhat * jnp.mean(dyg * xhat, axis=-1, keepdims=True))
        dg_ref[...] += jnp.sum(dn * xhat, axis=0, keepdims=True)

    row = pl.BlockSpec((tm, d), lambda i: (i, 0))
    in_specs = [pl.BlockSpec((tm, a.shape[1]), lambda i: (i, 0)) for a, _, _ in segs]
    in_specs += [_resident_spec(w.shape) for w in weights] + [row, _const_spec((1, d)), row]
    (dx, dg), extra = _hosted_call(
        body, grid=(t // tm,), in_specs=in_specs, out_specs=[row, _const_spec((1, d))],
        out_shape=[jax.ShapeDtypeStruct((t, d), F32), jax.ShapeDtypeStruct((1, d), F32)], scratch_shapes=[],
        operands=[a for a, _, _ in segs] + list(weights) + [x, g, dres], side=side, name=name)
    return dx, dg, extra


def _rms_fwd(x, g, *, name):
    t, d = x.shape
    tm = _tile(t, 512)

    def body(x_ref, g_ref, o_ref):
        xv = x_ref[...]
        r = lax.rsqrt(jnp.mean(xv * xv, axis=-1, keepdims=True) + RMS_EPS)
        o_ref[...] = (xv * r * g_ref[...]).astype(o_ref.dtype)

    row = pl.BlockSpec((tm, d), lambda i: (i, 0))
    return pl.pallas_call(body, out_shape=jax.ShapeDtypeStruct((t, d), BF16), grid=(t // tm,),
                          in_specs=[row, _const_spec((1, d))], out_specs=row, name=name, compiler_params=_params("parallel"))(x, g)


def _col_chunks(width, chunk=4 * LANES):
    return [(c0, min(c0 + chunk, width)) for c0 in range(0, width, chunk)]


def _ffn_up(n, wgt, wut, *, side=None, name):
    t, d = n.shape
    f = wgt.shape[0]
    tm, tn = _tile(t, 512), _tile(f, FFN_COL_TILE, LANES)

    def body(n_ref, wg_ref, wu_ref, g_ref, u_ref, h_ref):
        nv = n_ref[...]
        for c0, c1 in _col_chunks(tn):
            gv = _dot(nv, wg_ref[c0:c1, :], NT)
            uv = _dot(nv, wu_ref[c0:c1, :], NT)
            g_ref[:, c0:c1] = gv.astype(BF16)
            u_ref[:, c0:c1] = uv.astype(BF16)
            h_ref[:, c0:c1] = (gv * _sigmoid(gv) * uv).astype(BF16)

    w_spec = pl.BlockSpec((tn, d), lambda j, i: (j, 0))
    o_spec = pl.BlockSpec((tm, tn), lambda j, i: (i, j))
    shp = jax.ShapeDtypeStruct((t, f), BF16)
    return _hosted_call(body, grid=(f // tn, t // tm), in_specs=[pl.BlockSpec((tm, d), lambda j, i: (i, 0)), w_spec, w_spec],
                        out_specs=[o_spec, o_spec, o_spec], out_shape=[shp, shp, shp], scratch_shapes=[], operands=[n, wgt, wut],
                        side=side, name=name)


def _rms_residual(acc, h, gp, weight):
    r = lax.rsqrt(jnp.mean(acc * acc, axis=-1, keepdims=True) + RMS_EPS)
    return h + weight * (acc * r * gp)


def _ffn_down(hid, wd, h_in, gp, *, side=None, name):
    t, f = hid.shape
    d = wd.shape[1]
    tm = _tile(t, 256)

    def body(hid_ref, wd_ref, hin_ref, gp_ref, f_ref, hout_ref):
        acc = _dot(hid_ref[...], wd_ref[...], NN)
        f_ref[...] = acc
        hout_ref[...] = _rms_residual(acc, hin_ref[...], gp_ref[...], FFN_RESIDUAL_WEIGHT)

    row = pl.BlockSpec((tm, d), lambda i: (i, 0))
    shp = jax.ShapeDtypeStruct((t, d), F32)
    return _hosted_call(body, grid=(t // tm,),
                        in_specs=[pl.BlockSpec((tm, f), lambda i: (i, 0)), _resident_spec((f, d)), row, _const_spec((1, d))],
                        out_specs=[row, row], out_shape=[shp, shp], scratch_shapes=[], operands=[hid, wd, h_in, gp],
                        side=side, name=name)


def _post_bwd(dh, f, gp, weight, *, name):
    t, d = f.shape
    tm = _tile(t, 512)

    def body(dh_ref, f_ref, gp_ref, df_ref, dgp_ref):
        @pl.when(pl.program_id(0) == 0)
        def _():
            dgp_ref[...] = jnp.zeros_like(dgp_ref)
        fv = f_ref[...]
        dy = weight * dh_ref[...]
        r = lax.rsqrt(jnp.mean(fv * fv, axis=-1, keepdims=True) + RMS_EPS)
        fhat = fv * r
        dyg = dy * gp_ref[...]
        df_ref[...] = (r * (dyg - fhat * jnp.mean(dyg * fhat, axis=-1, keepdims=True))).astype(BF16)
        dgp_ref[...] += jnp.sum(dy * fhat, axis=0, keepdims=True)

    row = pl.BlockSpec((tm, d), lambda i: (i, 0))
    return pl.pallas_call(body, out_shape=(jax.ShapeDtypeStruct((t, d), BF16), jax.ShapeDtypeStruct((1, d), F32)), grid=(t // tm,),
                          in_specs=[row, row, _const_spec((1, d))], out_specs=(row, _const_spec((1, d))),
                          name=name, compiler_params=_params("arbitrary"))(dh, f, gp)


def _ffn_dhid(df, wd, g, u, *, name):
    t, d = df.shape
    f = wd.shape[0]
    tm, tn = _tile(t, 512), _tile(f, FFN_COL_TILE, LANES)

    def body(df_ref, wd_ref, g_ref, u_ref, dg_ref, du_ref):
        dfv = df_ref[...]
        for c0, c1 in _col_chunks(tn):
            dh = _dot(dfv, wd_ref[c0:c1, :], NT)
            gv = g_ref[:, c0:c1].astype(F32)
            uv = u_ref[:, c0:c1].astype(F32)
            sg = _sigmoid(gv)
            dg_ref[:, c0:c1] = (dh * uv * (sg * (1.0 + gv * (1.0 - sg)))).astype(BF16)
            du_ref[:, c0:c1] = (dh * (gv * sg)).astype(BF16)

    o_spec = pl.BlockSpec((tm, tn), lambda j, i: (i, j))
    shp = jax.ShapeDtypeStruct((t, f), BF16)
    return pl.pallas_call(body, out_shape=(shp, shp), grid=(f // tn, t // tm),
                          in_specs=[pl.BlockSpec((tm, d), lambda j, i: (i, 0)), pl.BlockSpec((tn, d), lambda j, i: (j, 0)), o_spec, o_spec],
                          out_specs=(o_spec, o_spec), name=name, compiler_params=_params("parallel", "arbitrary"))(df, wd, g, u)


def _ffn_forward(h_in, g_pre, wgt, wut, wd, g_post, tag, side_up=None, side_down=None):
    n = _rms_fwd(h_in, g_pre, name=f"{tag}_prenorm")
    (g, u, hid), got_up = _ffn_up(n, wgt, wut, side=side_up, name=f"{tag}_up")
    wd = wd(got_up) if callable(wd) else wd
    (f, h_out), got_down = _ffn_down(hid, wd, h_in, g_post, side=side_down, name=f"{tag}_down")
    return h_out, (h_in, n, g, u, hid, f), got_up, got_down


def _stack8(g):
    return g.reshape(N_DEV, g.shape[0] // N_DEV, g.shape[1])


def _ffn_backward(dh_out, saved, g_pre, wgt, wut, wd, g_post, tag, chain=False):
    h_in, n, g, u, hid, f = saved

    def side_of(grad16):
        return _scatter_exchange([_stack8(grad16)]) if chain else None

    df, dg_post = _post_bwd(dh_out, f, g_post, FFN_RESIDUAL_WEIGHT, name=f"{tag}_post_bwd")
    dgate, dup = _ffn_dhid(df, wd, g, u, name=f"{tag}_dhid")
    d_wd, d_wd16, _ = _mm_tn([hid], df, name=f"{tag}_dwd")
    d_wgt, d_wgt16, got_wd = _mm_tn([dgate], n, side=side_of(d_wd16), name=f"{tag}_dwg")
    d_wut, d_wut16, got_wg = _mm_tn([dup], n, side=side_of(d_wgt16), name=f"{tag}_dwu")
    dh_in, dg_pre, got_wu = _mm_nn_rmsbwd([(dgate, 0, 0), (dup, 1, 0)], [wgt, wut], h_in, g_pre, dh_out, side=side_of(d_wut16),
                                          name=f"{tag}_dn")
    received = (got_wg[0], got_wu[0], got_wd[0]) if chain else None
    return dh_in, dg_pre, dg_post, (d_wgt, d_wut, d_wd), (d_wgt16, d_wut16, d_wd16), received


CONV_ROWS = 128
HALO = 8


def _taps(w_ref):
    return [w_ref[k:k + 1, :] for k in range(SSM_CONV)]


def _conv_chunk(x_ref, xs, r0, taps, bias):
    xs[HALO + r0:HALO + r0 + CONV_ROWS, :] = x_ref[r0:r0 + CONV_ROWS, :].astype(F32)
    shifted = [xs[HALO + r0 - k:HALO + r0 - k + CONV_ROWS, :] for k in range(SSM_CONV)]
    pre = bias + shifted[0] * taps[SSM_CONV - 1]
    for k in range(1, SSM_CONV):
        pre = pre + shifted[k] * taps[SSM_CONV - 1 - k]
    return shifted, pre


def _fold_rows(a):
    return functools.reduce(jnp.add, [a[i:i + 8] for i in range(0, a.shape[0], 8)])


def _conv_fwd(xbc, conv_w, conv_b, bl, *, name):
    t, c = xbc.shape
    s = t // bl
    tc = LANES
    assert s % CONV_ROWS == 0

    def body(x_ref, w_ref, b_ref, o_ref, xs):
        taps, bias = _taps(w_ref), b_ref[...]
        xs[0:HALO, :] = jnp.zeros((HALO, tc), F32)
        for r0 in range(0, s, CONV_ROWS):
            _, pre = _conv_chunk(x_ref, xs, r0, taps, bias)
            o_ref[r0:r0 + CONV_ROWS, :] = (pre * _sigmoid(pre)).astype(o_ref.dtype)

    blk = pl.BlockSpec((s, tc), lambda b, j: (b, j))
    return pl.pallas_call(body, out_shape=jax.ShapeDtypeStruct((t, c), BF16), grid=(bl, c // tc),
                          in_specs=[blk, pl.BlockSpec((SSM_CONV, tc), lambda b, j: (0, j)), pl.BlockSpec((1, tc), lambda b, j: (0, j))],
                          out_specs=blk, scratch_shapes=[pltpu.VMEM((HALO + s, tc), F32)],
                          name=name, compiler_params=_params("parallel", "arbitrary"))(xbc, conv_w, conv_b)


def _conv_bwd(dxc, xbc, conv_w, conv_b, bl, *, name):
    t, c = xbc.shape
    s = t // bl
    tc = LANES

    def body(dy_ref, x_ref, w_ref, b_ref, dx_ref, dw_ref, db_ref, xs, dpre_s):
        @pl.when(pl.program_id(1) == 0)
        def _():
            dw_ref[...] = jnp.zeros_like(dw_ref)
            db_ref[...] = jnp.zeros_like(db_ref)

        taps, bias = _taps(w_ref), b_ref[...]
        zero8 = jnp.zeros((HALO, tc), F32)
        xs[0:HALO, :] = zero8
        dpre_s[s:s + HALO, :] = zero8
        sums = [zero8] * (SSM_CONV + 1)
        for r0 in range(0, s, CONV_ROWS):
            shifted, pre = _conv_chunk(x_ref, xs, r0, taps, bias)
            sg = _sigmoid(pre)
            dpre = dy_ref[r0:r0 + CONV_ROWS, :].astype(F32) * (sg * (1.0 + pre * (1.0 - sg)))
            dpre_s[r0:r0 + CONV_ROWS, :] = dpre
            sums = [acc + _fold_rows(dpre * sh) for acc, sh in zip(sums[:-1], shifted)] + [sums[-1] + _fold_rows(dpre)]
        for k in range(SSM_CONV):
            dw_ref[SSM_CONV - 1 - k:SSM_CONV - k, :] += jnp.sum(sums[k], axis=0, keepdims=True)
        db_ref[...] += jnp.sum(sums[-1], axis=0, keepdims=True)
        for r0 in range(0, s, CONV_ROWS):
            dx = dpre_s[r0:r0 + CONV_ROWS, :] * taps[SSM_CONV - 1]
            for k in range(1, SSM_CONV):
                dx = dx + dpre_s[r0 + k:r0 + k + CONV_ROWS, :] * taps[SSM_CONV - 1 - k]
            dx_ref[r0:r0 + CONV_ROWS, :] = dx.astype(dx_ref.dtype)

    blk = pl.BlockSpec((s, tc), lambda j, b: (b, j))
    return pl.pallas_call(
        body, out_shape=(jax.ShapeDtypeStruct((t, c), BF16), jax.ShapeDtypeStruct((8, c), F32), jax.ShapeDtypeStruct((1, c), F32)),
        grid=(c // tc, bl),
        in_specs=[blk, blk, pl.BlockSpec((SSM_CONV, tc), lambda j, b: (0, j)), pl.BlockSpec((1, tc), lambda j, b: (0, j))],
        out_specs=(blk, pl.BlockSpec((8, tc), lambda j, b: (0, j)), pl.BlockSpec((1, tc), lambda j, b: (0, j))),
        scratch_shapes=[pltpu.VMEM((HALO + s, tc), F32), pltpu.VMEM((s + HALO, tc), F32)],
        name=name, compiler_params=_params("parallel", "arbitrary"))(dxc, xbc, conv_w, conv_b)


def _softplus(x):
    return jnp.maximum(x, 0.0) + jnp.log1p(jnp.exp(-jnp.abs(x)))


def _hilo_dot(v, m_b, dims=NN):
    hi = v.astype(BF16)
    lo = (v - hi.astype(F32)).astype(BF16)
    return _dot(hi, m_b, dims) + _dot(lo, m_b, dims)


def _ssd_chunk_common(dtraw_ref, dtb_ref, alog_ref, dsk_ref, d_inner):
    q, p = CHUNK, SSM_HEAD_DIM
    shift = p.bit_length() - 1
    assert 1 << shift == p
    dt = _softplus(dtraw_ref[...] + dtb_ref[...])
    a = -jnp.exp(alog_ref[...])
    ii = lax.broadcasted_iota(jnp.int32, (q, q), 0)
    jj = lax.broadcasted_iota(jnp.int32, (q, q), 1)
    causal = ii >= jj
    tril = jnp.where(causal, 1.0, 0.0).astype(F32)
    triu = jnp.where(ii <= jj, 1.0, 0.0).astype(F32)
    a_cs = _dot_hi(tril, dt * a)
    a_cs_t = a_cs.T
    a_last = a_cs[q - 1:q, :]
    e_col = jnp.exp(a_cs)
    dec_end = jnp.exp(a_last - a_cs)
    head_of_col = lax.shift_right_logical(lax.broadcasted_iota(jnp.int32, (LANES, d_inner), 1), shift)
    spread = (lax.broadcasted_iota(jnp.int32, (LANES, d_inner), 0) == head_of_col).astype(BF16)
    wide = _hilo_dot(jnp.concatenate([dt, e_col, dec_end, jnp.broadcast_to(dsk_ref[...], (8, LANES))], axis=0), spread)
    return dict(dt=dt, a=a, a_cs=a_cs, a_cs_t=a_cs_t, a_last=a_last, dec_end=dec_end, causal=causal, triu=triu,
                dt_e=wide[:q], e_e=wide[q:2 * q], dec_e=wide[2 * q:3 * q], dsk_e=wide[3 * q:3 * q + 1])


def _fill_block_diag(bd_ref, src_ref, hpg, col0=0):
    q, p = CHUNK, SSM_HEAD_DIM
    for hh in range(hpg):
        bd_ref[hh * q:(hh + 1) * q, hh * p:(hh + 1) * p] = src_ref[:, col0 + hh * p:col0 + (hh + 1) * p]


def _lane_onehot(h):
    return (lax.broadcasted_iota(jnp.int32, (1, LANES), 1) == h).astype(F32)


def _ssd_fwd(xc, dt_raw, dt_bias, a_log, d_skip, bl, n_heads, *, side=None, name):
    t = xc.shape[0]
    q, p, nst, grp = CHUNK, SSM_HEAD_DIM, SSM_STATE, SSM_GROUPS
    d_inner = n_heads * p
    hpg = n_heads // grp
    gw = hpg * p
    nc = t // bl // q
    assert d_inner % (grp * nst) == 0 and nst == LANES

    def body(xs_ref, b_ref, c_ref, dtraw_ref, dtb_ref, alog_ref, dsk_ref, y_ref, hprev_ref, state, m_all, x_bd, xdt_s):
        @pl.when(jnp.logical_and(pl.program_id(0) == 0, pl.program_id(1) == 0))
        def _():
            x_bd[...] = jnp.zeros_like(x_bd)

        @pl.when(pl.program_id(1) == 0)
        def _():
            state[...] = jnp.zeros_like(state)

        cm = _ssd_chunk_common(dtraw_ref, dtb_ref, alog_ref, dsk_ref, d_inner)
        for g in range(grp):
            cols = slice(g * gw, (g + 1) * gw)
            bg = b_ref[:, g * nst:(g + 1) * nst]
            cg = c_ref[:, g * nst:(g + 1) * nst]
            scores = _dot(cg, bg, NT)
            for hh in range(hpg):
                h = g * hpg + hh
                seg = cm['a_cs'][:, h:h + 1] - cm['a_cs_t'][h:h + 1, :]
                m_all[:, hh * q:(hh + 1) * q] = (scores * jnp.exp(jnp.where(cm['causal'], seg, NEG_INF))).astype(BF16)
            xs = xs_ref[:, cols].astype(F32)
            xdt = xs * cm['dt_e'][:, cols]
            xdt_s[...] = xdt.astype(BF16)
            _fill_block_diag(x_bd, xdt_s, hpg)
            hprev = state[g]
            hprev_ref[g] = hprev
            y = _dot(m_all[...], x_bd[...], NN) + cm['e_e'][:, cols] * _dot(cg, hprev.astype(BF16), NT)
            y_ref[:, cols] = y + cm['dsk_e'][:, cols] * xs
            st = _dot((xdt * cm['dec_e'][:, cols]).astype(BF16), bg, TN)
            for hh in range(hpg):
                h = g * hpg + hh
                rows = slice(hh * p, (hh + 1) * p)
                state[g, rows, :] = jnp.exp(cm['a_last'][:, h:h + 1]) * hprev[rows] + st[rows]

    gn = grp * nst

    def rowmap(b, c):
        return b * nc + c
    vec = pl.BlockSpec((1, LANES), lambda b, c: (0, 0))
    return _hosted_call(
        body,
        out_shape=[jax.ShapeDtypeStruct((t, d_inner), F32), jax.ShapeDtypeStruct((t // q, grp, gw, nst), F32)],
        grid=(bl, nc),
        in_specs=[pl.BlockSpec((q, d_inner), lambda b, c: (rowmap(b, c), 0)),
                  pl.BlockSpec((q, gn), lambda b, c: (rowmap(b, c), d_inner // gn)),
                  pl.BlockSpec((q, gn), lambda b, c: (rowmap(b, c), d_inner // gn + 1)),
                  pl.BlockSpec((q, LANES), lambda b, c: (rowmap(b, c), 0)), vec, vec, vec],
        out_specs=[pl.BlockSpec((q, d_inner), lambda b, c: (rowmap(b, c), 0)),
                   pl.BlockSpec((None, grp, gw, nst), lambda b, c: (rowmap(b, c), 0, 0, 0))],
        scratch_shapes=[pltpu.VMEM((grp, gw, nst), F32), pltpu.VMEM((q, hpg * q), BF16), pltpu.VMEM((hpg * q, gw), BF16),
                        pltpu.VMEM((q, gw), BF16)],
        operands=[xc, xc, xc, dt_raw, dt_bias, a_log, d_skip], side=side, name=name)


def _ssd_bwd(dy, y, xc, dt_raw, hprev_all, dt_bias, a_log, d_skip, bl, n_heads, *, side=None, name):
    t, c_dim = xc.shape
    q, p, nst, grp = CHUNK, SSM_HEAD_DIM, SSM_STATE, SSM_GROUPS
    d_inner = n_heads * p
    hpg = n_heads // grp
    gw = hpg * p
    nc = t // bl // q
    gn = grp * nst
    shift = p.bit_length() - 1

    def body(dy_ref, y_ref, xs_ref, b_ref, c_ref, dtraw_ref, hprev_ref, dtb_ref, alog_ref, dsk_ref,
             dxc_ref, ddtraw_ref, ddtb_ref, dalog_ref, ddsk_ref, dstate, mt_all, x_bd, dy_bd, xdt_s):
        @pl.when(jnp.logical_and(pl.program_id(0) == 0, pl.program_id(1) == 0))
        def _():
            ddtb_ref[...] = jnp.zeros_like(ddtb_ref)
            dalog_ref[...] = jnp.zeros_like(dalog_ref)
            ddsk_ref[...] = jnp.zeros_like(ddsk_ref)
            x_bd[...] = jnp.zeros_like(x_bd)
            dy_bd[...] = jnp.zeros_like(dy_bd)

        @pl.when(pl.program_id(1) == 0)
        def _():
            dstate[...] = jnp.zeros_like(dstate)

        cm = _ssd_chunk_common(dtraw_ref, dtb_ref, alog_ref, dsk_ref, d_inner)
        causal = cm['causal']
        upper = cm['triu'] > 0.5
        seg_row = lax.shift_right_logical(lax.broadcasted_iota(jnp.int32, (gw, LANES), 0), shift)
        seg_lane = lax.broadcasted_iota(jnp.int32, (gw, LANES), 1)
        sums = jnp.zeros((5 * q, LANES), F32)
        state_dot = jnp.zeros((1, LANES), F32)
        for g in range(grp):
            cols = slice(g * gw, (g + 1) * gw)
            seg_sum = (seg_row + g * hpg == seg_lane).astype(BF16)
            bg = b_ref[:, g * nst:(g + 1) * nst]
            cg = c_ref[:, g * nst:(g + 1) * nst]
            scores_t = _dot(bg, cg, NT)
            xs = xs_ref[:, cols].astype(F32)
            xdt = xs * cm['dt_e'][:, cols]
            xdt_s[...] = xdt.astype(BF16)
            _fill_block_diag(x_bd, xdt_s, hpg)
            _fill_block_diag(dy_bd, dy_ref, hpg, g * gw)
            dy_b = dy_ref[:, cols]
            dyf = dy_b.astype(F32)
            dm_all = _dot(dy_b, x_bd[...], NT)
            dscores = jnp.zeros((q, q), F32)
            for hh in range(hpg):
                h = g * hpg + hh
                blk = slice(hh * q, (hh + 1) * q)
                seg = cm['a_cs'][:, h:h + 1] - cm['a_cs_t'][h:h + 1, :]
                mt_all[:, blk] = (scores_t * jnp.exp(jnp.where(upper, -seg, NEG_INF))).astype(BF16)
                dscores = dscores + dm_all[:, blk] * jnp.exp(jnp.where(causal, seg, NEG_INF))
            hprev = hprev_ref[g]
            hprev_b = hprev.astype(BF16)
            dhn = dstate[g]
            dhn_b = dhn.astype(BF16)
            e_e, dec_e = cm['e_e'][:, cols], cm['dec_e'][:, cols]
            y_scan = y_ref[:, cols] - cm['dsk_e'][:, cols] * xs
            dye_b = (dyf * e_e).astype(BF16)
            dcg = _dot(dye_b, hprev_b, NN)
            dhp = _dot(dye_b, cg, TN)
            bdh = _dot(bg, dhn_b, NT)
            dbg = _dot((xdt * dec_e).astype(BF16), dhn_b, NN)
            dx_diag = _dot(mt_all[...], dy_bd[...], NN)
            dx = dec_e * bdh + dx_diag
            ds_b = dscores.astype(BF16)
            dcg = dcg + _dot(ds_b, bg, NN)
            dbg = dbg + _dot(ds_b, cg, TN)
            x_rounded = xdt_s[...].astype(F32)
            sums = sums + _hilo_dot(jnp.concatenate([dyf * y_scan, xdt * bdh, x_rounded * dx_diag, dx * xs, dyf * xs], axis=0), seg_sum)
            state_dot = state_dot + jnp.sum(_hilo_dot(dhn * hprev, seg_sum, TN), axis=0, keepdims=True)
            dxc_ref[:, cols] = (dx * cm['dt_e'][:, cols] + cm['dsk_e'][:, cols] * dyf).astype(dxc_ref.dtype)
            dxc_ref[:, d_inner + g * nst:d_inner + (g + 1) * nst] = dbg.astype(dxc_ref.dtype)
            dxc_ref[:, d_inner + gn + g * nst:d_inner + gn + (g + 1) * nst] = dcg.astype(dxc_ref.dtype)
            for hh in range(hpg):
                h = g * hpg + hh
                rows = slice(hh * p, (hh + 1) * p)
                dstate[g, rows, :] = jnp.exp(cm['a_last'][:, h:h + 1]) * dhn[rows] + dhp[rows]
        s_y, s_end, s_diag, s_dt, s_skip = (sums[k * q:(k + 1) * q] for k in range(5))
        dt, a, dec_end = cm['dt'], cm['a'], cm['dec_end']
        last_row = (lax.broadcasted_iota(jnp.int32, (q, 1), 0) == q - 1).astype(F32)
        da_last = jnp.sum(dec_end * s_end, axis=0, keepdims=True) + jnp.exp(cm['a_last']) * state_dot
        da = s_y - dec_end * s_end - s_diag + last_row * da_last
        ddta = _dot_hi(cm['triu'], da)
        ddt = s_dt + ddta * a
        d_a = jnp.sum(ddta * dt, axis=0, keepdims=True)
        ddt_raw = ddt * _sigmoid(dtraw_ref[...] + dtb_ref[...])
        ddtraw_ref[...] = ddt_raw
        ddtb_ref[...] += jnp.sum(ddt_raw, axis=0, keepdims=True)
        dalog_ref[...] += d_a * a
        ddsk_ref[...] += jnp.sum(s_skip, axis=0, keepdims=True)

    def rowmap(b, c):
        return b * nc + (nc - 1 - c)
    vec = pl.BlockSpec((1, LANES), lambda b, c: (0, 0))
    vec_shape = jax.ShapeDtypeStruct((1, LANES), F32)
    return _hosted_call(
        body,
        out_shape=[jax.ShapeDtypeStruct((t, c_dim), BF16), jax.ShapeDtypeStruct((t, LANES), F32), vec_shape, vec_shape, vec_shape],
        grid=(bl, nc),
        in_specs=[pl.BlockSpec((q, d_inner), lambda b, c: (rowmap(b, c), 0)),
                  pl.BlockSpec((q, d_inner), lambda b, c: (rowmap(b, c), 0)),
                  pl.BlockSpec((q, d_inner), lambda b, c: (rowmap(b, c), 0)),
                  pl.BlockSpec((q, gn), lambda b, c: (rowmap(b, c), d_inner // gn)),
                  pl.BlockSpec((q, gn), lambda b, c: (rowmap(b, c), d_inner // gn + 1)),
                  pl.BlockSpec((q, LANES), lambda b, c: (rowmap(b, c), 0)),
                  pl.BlockSpec((None, grp, gw, nst), lambda b, c: (rowmap(b, c), 0, 0, 0)), vec, vec, vec],
        out_specs=[pl.BlockSpec((q, c_dim), lambda b, c: (rowmap(b, c), 0)),
                   pl.BlockSpec((q, LANES), lambda b, c: (rowmap(b, c), 0)), vec, vec, vec],
        scratch_shapes=[pltpu.VMEM((grp, gw, nst), F32), pltpu.VMEM((q, hpg * q), BF16),
                        pltpu.VMEM((hpg * q, gw), BF16), pltpu.VMEM((hpg * q, gw), BF16), pltpu.VMEM((q, gw), BF16)],
        operands=[dy, y, xc, xc, xc, dt_raw, hprev_all, dt_bias, a_log, d_skip], side=side, name=name)


def _gated_norm_fwd(y, z, ng, *, name):
    t, d = y.shape
    tm = _tile(t, 256)
    gw = d // SSM_GROUPS

    def body(y_ref, z_ref, ng_ref, o_ref):
        for g in range(SSM_GROUPS):
            sl = slice(g * gw, (g + 1) * gw)
            zv = z_ref[:, sl].astype(F32)
            yg = y_ref[:, sl] * (zv * _sigmoid(zv))
            r = lax.rsqrt(jnp.mean(yg * yg, axis=-1, keepdims=True) + RMS_EPS)
            o_ref[:, sl] = (yg * r * ng_ref[:, sl]).astype(o_ref.dtype)

    row = pl.BlockSpec((tm, d), lambda i: (i, 0))
    return pl.pallas_call(body, out_shape=jax.ShapeDtypeStruct((t, d), BF16), grid=(t // tm,),
                          in_specs=[row, row, _const_spec((1, d))], out_specs=row, name=name, compiler_params=_params("parallel"))(y, z, ng)


def _gated_norm_bwd(dyn, y, z, ng, *, name):
    t, d = y.shape
    tm = _tile(t, 256)
    gw = d // SSM_GROUPS

    def body(dyn_ref, y_ref, z_ref, ng_ref, dy_ref, dz_ref, dng_ref):
        @pl.when(pl.program_id(0) == 0)
        def _():
            dng_ref[...] = jnp.zeros_like(dng_ref)
        for g in range(SSM_GROUPS):
            sl = slice(g * gw, (g + 1) * gw)
            zv = z_ref[:, sl].astype(F32)
            yv = y_ref[:, sl]
            sg = _sigmoid(zv)
            sz = zv * sg
            yg = yv * sz
            r = lax.rsqrt(jnp.mean(yg * yg, axis=-1, keepdims=True) + RMS_EPS)
            yhat = yg * r
            dn = dyn_ref[:, sl].astype(F32)
            dyg_n = dn * ng_ref[:, sl]
            dyg = r * (dyg_n - yhat * jnp.mean(dyg_n * yhat, axis=-1, keepdims=True))
            dy_ref[:, sl] = (dyg * sz).astype(dy_ref.dtype)
            dz_ref[:, sl] = (dyg * yv * (sg * (1.0 + zv * (1.0 - sg)))).astype(dz_ref.dtype)
            dng_ref[:, sl] += jnp.sum(dn * yhat, axis=0, keepdims=True)

    row = pl.BlockSpec((tm, d), lambda i: (i, 0))
    shp = jax.ShapeDtypeStruct((t, d), BF16)
    return pl.pallas_call(body, out_shape=(shp, shp, jax.ShapeDtypeStruct((1, d), F32)), grid=(t // tm,),
                          in_specs=[row, row, row, _const_spec((1, d))], out_specs=(row, row, _const_spec((1, d))),
                          name=name, compiler_params=_params("arbitrary"))(dyn, y, z, ng)


def _bucket_onehot():
    blk = CHUNK
    qi = jnp.arange(blk)[:, None]
    kj = jnp.arange(2 * blk)[None, :]
    dist = jnp.maximum(qi + blk - kj, 0)
    max_exact = REL_BUCKETS // 2
    d = jnp.maximum(dist, 1).astype(F32)
    large = max_exact + (jnp.log(d / max_exact) / math.log(REL_MAX_DISTANCE / max_exact) * (REL_BUCKETS - max_exact)).astype(jnp.int32)
    large = jnp.minimum(large, REL_BUCKETS - 1)
    bucket = jnp.where(dist < max_exact, dist, large).reshape(-1)
    return (bucket[None, :] == jnp.arange(REL_BUCKETS)[:, None]).astype(F32)


def _small_mm_hi(a, b, dims, *, name):
    def body(a_ref, b_ref, o_ref):
        o_ref[...] = _dot_hi(a_ref[...], b_ref[...], dims)
    n = b.shape[0] if dims == NT else b.shape[1]
    return pl.pallas_call(body, out_shape=jax.ShapeDtypeStruct((a.shape[0], n), F32), name=name)(a, b)


def _attn_band_mask(n, rep):
    blk = CHUNK
    ii = lax.broadcasted_iota(jnp.int32, (rep * blk, 2 * blk), 0) & (blk - 1)
    jj = lax.broadcasted_iota(jnp.int32, (rep * blk, 2 * blk), 1)
    dist = ii + blk - jj
    in_window = jnp.logical_and(dist >= 0, dist < blk)
    return jnp.logical_and(in_window, jnp.logical_or(jj >= blk, n > 0))


def _attn_fwd(q, kv, bias, sinks, bl, *, name):
    t, qd = q.shape
    blk, hd = CHUNK, ATTN_HEAD_DIM
    kvd = ATTN_KV_HEADS * hd
    rep = ATTN_Q_HEADS // ATTN_KV_HEADS
    nb = t // bl // blk
    scale = hd ** -0.5

    def body(q_ref, kp_ref, kc_ref, vp_ref, vc_ref, bias_ref, sink_ref, o_ref, lse_ref):
        n = pl.program_id(1)
        mask = _attn_band_mask(n, rep)
        lse = jnp.zeros((blk, LANES), F32)
        for kvh in range(ATTN_KV_HEADS):
            ks = slice(kvh * hd, (kvh + 1) * hd)
            heads = range(kvh * rep, (kvh + 1) * rep)
            qs = jnp.concatenate([q_ref[:, h * hd:(h + 1) * hd] for h in heads], axis=0)
            kk = jnp.concatenate([kp_ref[:, ks], kc_ref[:, ks]], axis=0)
            vv = jnp.concatenate([vp_ref[:, ks], vc_ref[:, ks]], axis=0)
            bias = bias_ref[kvh * rep:(kvh + 1) * rep].reshape(rep * blk, 2 * blk)
            s = jnp.where(mask, _dot(qs, kk, NT) * scale + bias, NEG_INF)
            sink = jnp.concatenate([jnp.broadcast_to(sink_ref[:, h:h + 1], (blk, 1)) for h in heads], axis=0)
            m = jnp.maximum(jnp.max(s, axis=1, keepdims=True), sink)
            p = jnp.exp(s - m)
            den = jnp.sum(p, axis=1, keepdims=True) + jnp.exp(sink - m)
            o = _dot((p * (1.0 / den)).astype(BF16), vv, NN)
            lse_s = m + jnp.log(den)
            for r, h in enumerate(heads):
                o_ref[:, h * hd:(h + 1) * hd] = o[r * blk:(r + 1) * blk].astype(o_ref.dtype)
                lse = lse + lse_s[r * blk:(r + 1) * blk] * _lane_onehot(h)
        lse_ref[...] = lse

    def cur(b, n):
        return b * nb + n

    def prev(b, n):
        return b * nb + jnp.maximum(n - 1, 0)
    return pl.pallas_call(
        body, out_shape=(jax.ShapeDtypeStruct((t, qd), BF16), jax.ShapeDtypeStruct((t, LANES), F32)), grid=(bl, nb),
        in_specs=[pl.BlockSpec((blk, qd), lambda b, n: (cur(b, n), 0)),
                  pl.BlockSpec((blk, kvd), lambda b, n: (prev(b, n), 0)), pl.BlockSpec((blk, kvd), lambda b, n: (cur(b, n), 0)),
                  pl.BlockSpec((blk, kvd), lambda b, n: (prev(b, n), 1)), pl.BlockSpec((blk, kvd), lambda b, n: (cur(b, n), 1)),
                  _const_spec(bias.shape), _const_spec((1, LANES))],
        out_specs=(pl.BlockSpec((blk, qd), lambda b, n: (cur(b, n), 0)), pl.BlockSpec((blk, LANES), lambda b, n: (cur(b, n), 0))),
        name=name, compiler_params=_params("parallel", "arbitrary"))(q, kv, kv, kv, kv, bias, sinks)


def _attn_bwd(do, q, kv, lse, bias, sinks, bl, *, name):
    t, qd = q.shape
    blk, hd = CHUNK, ATTN_HEAD_DIM
    kvd = ATTN_KV_HEADS * hd
    rep = ATTN_Q_HEADS // ATTN_KV_HEADS
    s_len = t // bl
    nb = s_len // blk
    scale = hd ** -0.5

    def body(do_ref, q_ref, kp_ref, kc_ref, vp_ref, vc_ref, lse_ref, bias_ref, sink_ref, dq_ref, dkv_ref, dbias_ref, dsink_ref):
        n = pl.program_id(1)

        @pl.when(jnp.logical_and(pl.program_id(0) == 0, n == 0))
        def _():
            dbias_ref[...] = jnp.zeros_like(dbias_ref)
            dsink_ref[...] = jnp.zeros_like(dsink_ref)

        mask = _attn_band_mask(n, rep)
        r_cur = pl.multiple_of(n * blk, blk)
        r_prev = pl.multiple_of(jnp.maximum(n - 1, 0) * blk, blk)
        dsink = jnp.zeros((1, LANES), F32)
        for kvh in range(ATTN_KV_HEADS):
            ks = slice(kvh * hd, (kvh + 1) * hd)
            heads = range(kvh * rep, (kvh + 1) * rep)
            qs = jnp.concatenate([q_ref[:, h * hd:(h + 1) * hd] for h in heads], axis=0)
            dos = jnp.concatenate([do_ref[:, h * hd:(h + 1) * hd] for h in heads], axis=0)
            kk = jnp.concatenate([kp_ref[:, ks], kc_ref[:, ks]], axis=0)
            vv = jnp.concatenate([vp_ref[:, ks], vc_ref[:, ks]], axis=0)
            bias = bias_ref[kvh * rep:(kvh + 1) * rep].reshape(rep * blk, 2 * blk)
            lse = jnp.concatenate([lse_ref[:, h:h + 1] for h in heads], axis=0)
            sink = jnp.concatenate([jnp.broadcast_to(sink_ref[:, h:h + 1], (blk, 1)) for h in heads], axis=0)
            p = jnp.exp(jnp.where(mask, _dot(qs, kk, NT) * scale + bias, NEG_INF) - lse)
            dp = _dot(dos, vv, NT)
            delta = jnp.sum(p * dp, axis=1, keepdims=True)
            ds = p * (dp - delta)
            dsink_rows = jnp.exp(sink - lse) * delta
            ds_b = ds.astype(BF16)
            dq_s = _dot(ds_b, kk, NN) * scale
            dkk = _dot(ds_b, qs, TN) * scale
            dvv = _dot(p.astype(BF16), dos, TN)
            for r, h in enumerate(heads):
                rows = slice(r * blk, (r + 1) * blk)
                dbias_ref[h] += ds[rows]
                dq_ref[:, h * hd:(h + 1) * hd] = dq_s[rows].astype(dq_ref.dtype)
                dsink = dsink - jnp.sum(dsink_rows[rows], axis=0, keepdims=True) * _lane_onehot(h)
            vs = slice(kvd + kvh * hd, kvd + (kvh + 1) * hd)
            dkv_ref[pl.ds(r_cur, blk), ks] = dkk[blk:]
            dkv_ref[pl.ds(r_cur, blk), vs] = dvv[blk:]

            @pl.when(n > 0)
            def _():
                dkv_ref[pl.ds(r_prev, blk), ks] += dkk[:blk]
                dkv_ref[pl.ds(r_prev, blk), vs] += dvv[:blk]
        dsink_ref[...] += dsink

    def cur(b, n):
        return b * nb + n

    def prev(b, n):
        return b * nb + jnp.maximum(n - 1, 0)
    qspec = pl.BlockSpec((blk, qd), lambda b, n: (cur(b, n), 0))
    return pl.pallas_call(
        body,
        out_shape=(jax.ShapeDtypeStruct((t, qd), BF16), jax.ShapeDtypeStruct((t, 2 * kvd), F32),
                   jax.ShapeDtypeStruct(bias.shape, F32), jax.ShapeDtypeStruct((1, LANES), F32)),
        grid=(bl, nb),
        in_specs=[qspec, qspec,
                  pl.BlockSpec((blk, kvd), lambda b, n: (prev(b, n), 0)), pl.BlockSpec((blk, kvd), lambda b, n: (cur(b, n), 0)),
                  pl.BlockSpec((blk, kvd), lambda b, n: (prev(b, n), 1)), pl.BlockSpec((blk, kvd), lambda b, n: (cur(b, n), 1)),
                  pl.BlockSpec((blk, LANES), lambda b, n: (cur(b, n), 0)), _const_spec(bias.shape), _const_spec((1, LANES))],
        out_specs=(qspec, pl.BlockSpec((s_len, 2 * kvd), lambda b, n: (b, 0)), _const_spec(bias.shape), _const_spec((1, LANES))),
        name=name, compiler_params=_params("arbitrary", "arbitrary"))(do, q, kv, kv, kv, kv, lse, bias, sinks)


def _merge_fwd(yn, o, gs, ga, w_ssm, w_attn, w_out, h_in, g_post, *, name):
    t, d = h_in.shape
    tm = _tile(t, 256)

    def body(yn_ref, o_ref, gs_ref, ga_ref, ws_ref, wa_ref, wo_ref, hin_ref, gp_ref, ys_ref, ya_ref, mg_ref, mix_ref, hout_ref):
        ys = _dot(yn_ref[...], ws_ref[...], NN)
        ya = _dot(o_ref[...], wa_ref[...], NN)
        merged = (_sigmoid(gs_ref[...].astype(F32)) * ys + _sigmoid(ga_ref[...].astype(F32)) * ya).astype(BF16)
        mix = _dot(merged, wo_ref[...], NN)
        ys_ref[...] = ys.astype(BF16)
        ya_ref[...] = ya.astype(BF16)
        mg_ref[...] = merged
        mix_ref[...] = mix
        hout_ref[...] = _rms_residual(mix, hin_ref[...], gp_ref[...], 1.0)

    def row(w):
        return pl.BlockSpec((tm, w), lambda i: (i, 0))
    bshape = jax.ShapeDtypeStruct((t, d), BF16)
    fshape = jax.ShapeDtypeStruct((t, d), F32)
    return pl.pallas_call(
        body, out_shape=(bshape, bshape, bshape, fshape, fshape), grid=(t // tm,),
        in_specs=[row(yn.shape[1]), row(o.shape[1]), row(d), row(d), _const_spec(w_ssm.shape), _const_spec(w_attn.shape),
                  _const_spec(w_out.shape), row(d), _const_spec((1, d))],
        out_specs=(row(d),) * 5, name=name, compiler_params=_params("parallel"))(yn, o, gs, ga, w_ssm, w_attn, w_out, h_in, g_post)


def _merge_bwd(dh, mix, g_post, gs, ga, ys, ya, w_ssm, w_attn, w_out, *, name):
    t, d = mix.shape
    tm = _tile(t, 256)
    d_ssm, d_attn = w_ssm.shape[0], w_attn.shape[0]

    def body(dh_ref, mix_ref, gp_ref, gs_ref, ga_ref, ys_ref, ya_ref, ws_ref, wa_ref, wo_ref,
             dmix_ref, dys_ref, dya_ref, dgs_ref, dga_ref, dyn_ref, do_ref, dgp_ref):
        @pl.when(pl.program_id(0) == 0)
        def _():
            dgp_ref[...] = jnp.zeros_like(dgp_ref)
        mv = mix_ref[...]
        dy = dh_ref[...]
        r = lax.rsqrt(jnp.mean(mv * mv, axis=-1, keepdims=True) + RMS_EPS)
        mhat = mv * r
        dyg = dy * gp_ref[...]
        dmix = (r * (dyg - mhat * jnp.mean(dyg * mhat, axis=-1, keepdims=True))).astype(BF16)
        dgp_ref[...] += jnp.sum(dy * mhat, axis=0, keepdims=True)
        dmix_ref[...] = dmix
        dmerged = _dot(dmix, wo_ref[...], NT)
        sgs = _sigmoid(gs_ref[...].astype(F32))
        sga = _sigmoid(ga_ref[...].astype(F32))
        dys = (dmerged * sgs).astype(BF16)
        dya = (dmerged * sga).astype(BF16)
        dys_ref[...] = dys
        dya_ref[...] = dya
        dgs_ref[...] = (dmerged * ys_ref[...].astype(F32) * sgs * (1.0 - sgs)).astype(BF16)
        dga_ref[...] = (dmerged * ya_ref[...].astype(F32) * sga * (1.0 - sga)).astype(BF16)
        dyn_ref[...] = _dot(dys, ws_ref[...], NT).astype(BF16)
        do_ref[...] = _dot(dya, wa_ref[...], NT).astype(BF16)

    def row(w):
        return pl.BlockSpec((tm, w), lambda i: (i, 0))

    def bshape(w):
        return jax.ShapeDtypeStruct((t, w), BF16)
    return pl.pallas_call(
        body, out_shape=(bshape(d),) * 5 + (bshape(d_ssm), bshape(d_attn), jax.ShapeDtypeStruct((1, d), F32)), grid=(t // tm,),
        in_specs=[row(d), row(d), _const_spec((1, d)), row(d), row(d), row(d), row(d),
                  _const_spec(w_ssm.shape), _const_spec(w_attn.shape), _const_spec(w_out.shape)],
        out_specs=(row(d),) * 5 + (row(d_ssm), row(d_attn), _const_spec((1, d))),
        name=name, compiler_params=_params("arbitrary"))(dh, mix, g_post, gs, ga, ys, ya, w_ssm, w_attn, w_out)


def _loss_grad(h, target, *, name):
    t, d = h.shape
    tm = _tile(t, 512)

    def body(h_ref, t_ref, dh_ref, loss_ref):
        @pl.when(pl.program_id(0) == 0)
        def _():
            loss_ref[...] = jnp.zeros_like(loss_ref)
        e = h_ref[...] - t_ref[...]
        dh_ref[...] = e * (1.0 / d)
        per_row = jnp.sum(e * e, axis=1, keepdims=True) * (1.0 / d)
        loss_ref[...] += 0.5 * jnp.sum(per_row, axis=0, keepdims=True)

    row = pl.BlockSpec((tm, d), lambda i: (i, 0))
    return pl.pallas_call(body, out_shape=(jax.ShapeDtypeStruct((t, d), F32), jax.ShapeDtypeStruct((1, LANES), F32)), grid=(t // tm,),
                          in_specs=[row, row], out_specs=(row, _const_spec((1, LANES))),
                          name=name, compiler_params=_params("arbitrary"))(h, target)


def _adamw(w, g, m, v, *, name):
    r, c = w.shape
    tm = _tile(r, 256)
    c1 = 1.0 - ADAM_B1 ** ADAM_STEP
    c2 = 1.0 - ADAM_B2 ** ADAM_STEP

    def body(w_ref, g_ref, m_ref, v_ref, d_ref, mo_ref, vo_ref):
        gv = g_ref[...]
        mn = ADAM_B1 * m_ref[...] + (1.0 - ADAM_B1) * gv
        vn = ADAM_B2 * v_ref[...] + (1.0 - ADAM_B2) * (gv * gv)
        mo_ref[...] = mn
        vo_ref[...] = vn
        d_ref[...] = -ADAM_LR * ((mn / c1) / (jnp.sqrt(vn / c2) + ADAM_EPS) + ADAM_WD * w_ref[...])

    blk = pl.BlockSpec((tm, c), lambda i: (i, 0))
    shp = jax.ShapeDtypeStruct((r, c), F32)
    return pl.pallas_call(body, out_shape=(shp, shp, shp), grid=(r // tm,), in_specs=[blk] * 4, out_specs=(blk,) * 3,
                          name=name, compiler_params=_params("parallel"))(w, g, m, v)


def _position():
    return lax.axis_index("x"), lax.axis_index("y"), lax.axis_index("c")


def _gather_exchange(shards):
    na = len(shards)

    def plan(ins, outs, sems):
        send_sems, recv_sems, local_sems = sems
        x, y, c = _position()
        me, sibling = (x, y, c), (x, y, 1 - c)
        chips = [(1 - x, y), (x, 1 - y), (1 - x, 1 - y)]

        def slot(a, pos):
            return outs[a].at[4 * pos[0] + 2 * pos[1] + pos[2]]

        def copy(a, k, block, to, src=None):
            return pltpu.make_async_remote_copy(
                src_ref=slot(a, block) if src is None else src, dst_ref=slot(a, block),
                send_sem=send_sems.at[a, k], recv_sem=recv_sems.at[a, k], device_id=to, device_id_type=MESH)

        mine = [pltpu.make_async_copy(ins[a], slot(a, me), local_sems.at[a]) for a in range(na)]
        first = []
        for a in range(na):
            first.append(copy(a, 0, me, sibling, src=ins[a]))
            first += [copy(a, 1 + j, me, (*chip, c), src=ins[a]) for j, chip in enumerate(chips)]
        return me, sibling, chips, copy, mine, first

    def start(ins, outs, sems):
        *_, mine, first = plan(ins, outs, sems)
        for cp in mine + first:
            cp.start()

    def finish(ins, outs, sems):
        me, sibling, chips, copy, mine, first = plan(ins, outs, sems)
        c = me[2]
        passed = []
        for a in range(na):
            for j, chip in enumerate(chips):
                copy(a, 1 + j, (*chip, c), me).wait_recv()
                fwd = copy(a, 4 + j, (*chip, c), sibling)
                fwd.start()
                passed.append(fwd)
        for a in range(na):
            copy(a, 0, sibling, me).wait_recv()
            for j, chip in enumerate(chips):
                copy(a, 4 + j, (*chip, 1 - c), me).wait_recv()
        for cp in first + passed:
            cp.wait_send()
        for cp in mine:
            cp.wait()

    return _Exchange(list(shards), [jax.ShapeDtypeStruct((N_DEV,) + s.shape, s.dtype) for s in shards],
                     [pltpu.SemaphoreType.DMA((na, 7)), pltpu.SemaphoreType.DMA((na, 7)), pltpu.SemaphoreType.DMA((na,))],
                     start, finish)


def _scatter_exchange(arrays):
    na = len(arrays)

    def copies(ins, outs, sems):
        send_sems, recv_sems = sems
        x, y, c = _position()
        out = []
        for a in range(na):
            for k in range(7):
                flip = k + 1
                peer = (x ^ (flip >> 2), y ^ ((flip >> 1) & 1), c ^ (flip & 1))
                peer_block = 4 * peer[0] + 2 * peer[1] + peer[2]
                out.append(pltpu.make_async_remote_copy(
                    src_ref=ins[a].at[peer_block], dst_ref=outs[a].at[k],
                    send_sem=send_sems.at[a, k], recv_sem=recv_sems.at[a, k], device_id=peer, device_id_type=MESH))
        return out

    def start(ins, outs, sems):
        for cp in copies(ins, outs, sems):
            cp.start()

    def finish(ins, outs, sems):
        for cp in copies(ins, outs, sems):
            cp.wait()

    return _Exchange(list(arrays), [jax.ShapeDtypeStruct((7,) + s.shape[1:], s.dtype) for s in arrays],
                     [pltpu.SemaphoreType.DMA((na, 7)), pltpu.SemaphoreType.DMA((na, 7))], start, finish)


def _exchange_alone(side, *, name):
    n_in = len(side.arrays)
    n_out = len(side.out_shape)

    def body(*refs):
        ins, outs, sems = refs[:n_in], refs[n_in:n_in + n_out], refs[n_in + n_out:]
        side.start(ins, outs, sems)
        side.finish(ins, outs, sems)

    hbm = pl.BlockSpec(memory_space=pl.ANY)
    return pl.pallas_call(body, out_shape=tuple(side.out_shape), in_specs=[hbm] * n_in, out_specs=tuple([hbm] * n_out),
                          scratch_shapes=list(side.scratch), name=name)(*side.arrays)


def _reduce_blocks(own, recv, *, name):
    r, c = own.shape
    tm = _tile(r, 256)

    def body(own_ref, recv_ref, o_ref):
        acc = own_ref[...]
        for k in range(7):
            acc = acc + recv_ref[k].astype(F32)
        o_ref[...] = acc

    return pl.pallas_call(
        body, out_shape=jax.ShapeDtypeStruct((r, c), F32), grid=(r // tm,),
        in_specs=[pl.BlockSpec((tm, c), lambda i: (i, 0)), pl.BlockSpec((7, tm, c), lambda i: (0, i, 0))],
        out_specs=pl.BlockSpec((tm, c), lambda i: (i, 0)), name=name, compiler_params=_params("parallel"))(own, recv)


def _all_reduce_small(vec, *, name):
    r, c = vec.shape

    def body(v_ref, o_ref, buf, send_sems, recv_sems):
        x, y, c_ = _position()
        me = 4 * x + 2 * y + c_
        buf[me] = v_ref[...]
        copies = []
        for k in range(7):
            flip = k + 1
            peer = (x ^ (flip >> 2), y ^ ((flip >> 1) & 1), c_ ^ (flip & 1))
            cp = pltpu.make_async_remote_copy(
                src_ref=v_ref, dst_ref=buf.at[me], send_sem=send_sems.at[k], recv_sem=recv_sems.at[k],
                device_id=peer, device_id_type=MESH)
            cp.start()
            copies.append(cp)
        for cp in copies:
            cp.wait()
        acc = buf[0]
        for d in range(1, N_DEV):
            acc = acc + buf[d]
        o_ref[...] = acc

    vm = pl.BlockSpec(memory_space=pltpu.VMEM)
    return pl.pallas_call(
        body, out_shape=jax.ShapeDtypeStruct((r, c), F32), in_specs=[vm], out_specs=vm,
        scratch_shapes=[pltpu.VMEM((N_DEV, r, c), F32), pltpu.SemaphoreType.DMA((7,)), pltpu.SemaphoreType.DMA((7,))],
        name=name)(vec)


def _pad_lanes(v, width=LANES):
    return jnp.pad(v, ((0, 0), (0, width - v.shape[1])))


def kernel(x, ffn1_pre_g, ffn1_w_gate, ffn1_w_up, ffn1_w_down, ffn1_post_g, mix_pre_g, w_in, conv_w, conv_b, dt_bias, a_log, d_skip, ssm_norm_g, w_ssm_proj, attn_sinks, rel_bias_table, w_attn_proj, w_out, mix_post_g, ffn2_pre_g, ffn2_w_gate, ffn2_w_up, ffn2_w_down, ffn2_post_g, loss_target, m_ffn1_pre_g, m_ffn1_w_gate, m_ffn1_w_up, m_ffn1_w_down, m_ffn1_post_g, m_mix_pre_g, m_w_in, m_conv_w, m_conv_b, m_dt_bias, m_a_log, m_d_skip, m_ssm_norm_g, m_w_ssm_proj, m_attn_sinks, m_rel_bias_table, m_w_attn_proj, m_w_out, m_mix_post_g, m_ffn2_pre_g, m_ffn2_w_gate, m_ffn2_w_up, m_ffn2_w_down, m_ffn2_post_g, v_ffn1_pre_g, v_ffn1_w_gate, v_ffn1_w_up, v_ffn1_w_down, v_ffn1_post_g, v_mix_pre_g, v_w_in, v_conv_w, v_conv_b, v_dt_bias, v_a_log, v_d_skip, v_ssm_norm_g, v_w_ssm_proj, v_attn_sinks, v_rel_bias_table, v_w_attn_proj, v_w_out, v_mix_post_g, v_ffn2_pre_g, v_ffn2_w_gate, v_ffn2_w_up, v_ffn2_w_down, v_ffn2_post_g):
    args = dict(locals())
    weight_names = ['ffn1_pre_g', 'ffn1_w_gate', 'ffn1_w_up', 'ffn1_w_down', 'ffn1_post_g', 'mix_pre_g', 'w_in', 'conv_w', 'conv_b',
                    'dt_bias', 'a_log', 'd_skip', 'ssm_norm_g', 'w_ssm_proj', 'attn_sinks', 'rel_bias_table', 'w_attn_proj', 'w_out',
                    'mix_post_g', 'ffn2_pre_g', 'ffn2_w_gate', 'ffn2_w_up', 'ffn2_w_down', 'ffn2_post_g']
    col_sharded = ('ffn1_w_gate', 'ffn1_w_up', 'w_in', 'ffn2_w_gate', 'ffn2_w_up')
    row_sharded = ('ffn1_w_down', 'w_ssm_proj', 'w_attn_proj', 'w_out', 'ffn2_w_down')
    big = col_sharded + row_sharded

    bl, s_len, d = x.shape
    t = bl * s_len
    d_inner = ssm_norm_g.shape[1]
    n_heads = dt_bias.shape[1]
    gn = SSM_GROUPS * SSM_STATE
    conv_dim = d_inner + 2 * gn
    q_dim = ATTN_Q_HEADS * ATTN_HEAD_DIM
    kv_dim = ATTN_KV_HEADS * ATTN_HEAD_DIM

    def local_2d(name, a):
        a = a[0]
        return a.T if name in col_sharded else a

    ffn1_names = ('ffn1_w_gate', 'ffn1_w_up', 'ffn1_w_down')
    ffn2_names = ('ffn2_w_gate', 'ffn2_w_up', 'ffn2_w_down')
    mixer_names = ('w_ssm_proj', 'w_attn_proj', 'w_out')

    def shard(n):
        return local_2d(n, args[n]).astype(BF16)

    def rows(g):
        return g.reshape(N_DEV * g.shape[1], g.shape[2])

    first_names = ffn1_names[:2]
    full = {n: rows(g) for n, g in zip(first_names, _exchange_alone(_gather_exchange([shard(n) for n in first_names]), name="gather_ffn1"))}

    x2 = x.reshape(t, d)
    tgt2 = loss_target.reshape(t, d)

    h1, saved1, got_in, got_mixer = _ffn_forward(
        x2, ffn1_pre_g, full['ffn1_w_gate'], full['ffn1_w_up'], lambda got: rows(got[2]), ffn1_post_g, "ffn1",
        side_up=_gather_exchange([shard('w_in'), conv_w[0], shard('ffn1_w_down')]),
        side_down=_gather_exchange([shard(n) for n in mixer_names]))
    full['ffn1_w_down'] = rows(got_in[2])
    full.update({n: rows(g) for n, g in zip(mixer_names, got_mixer)})
    conv_w_full = jnp.transpose(got_in[1], (1, 0, 2)).reshape(SSM_CONV, conv_dim)

    win_t = rows(got_in[0])
    dt_lo = 2 * d + d_inner + conv_dim
    n_main = win_t.shape[0] - n_heads
    win_p = jnp.concatenate([win_t[:dt_lo], win_t[dt_lo + n_heads:], win_t[dt_lo:dt_lo + n_heads],
                             jnp.zeros((LANES - n_heads, d), BF16)], axis=0)
    off = {'gs': 0, 'ga': d, 'z': 2 * d, 'xbc': 2 * d + d_inner, 'q': dt_lo, 'kv': dt_lo + q_dim, 'dt': n_main}

    u = _rms_fwd(h1, mix_pre_g, name="mix_prenorm")

    def proj(key, width, out_dtype=BF16):
        tn = _tile(math.gcd(width, off[key]) if off[key] else width, 1024, LANES)
        assert off[key] % tn == 0 and width % tn == 0
        return _mm_nt(u, win_p, row_off=off[key] // tn, n_out=width, out_dtype=out_dtype, tn=tn, name=f"proj_{key}")

    gs, ga, z, xbc = proj('gs', d), proj('ga', d), proj('z', d_inner), proj('xbc', conv_dim)
    q, kv = proj('q', q_dim), proj('kv', 2 * kv_dim)
    dt_raw = proj('dt', LANES, F32)

    dtb_p, alog_p, dsk_p, sinks_p = _pad_lanes(dt_bias), _pad_lanes(a_log), _pad_lanes(d_skip), _pad_lanes(attn_sinks)
    xc = _conv_fwd(xbc, conv_w_full, conv_b, bl, name="conv_fwd")
    (y, hprev), got_ffn2 = _ssd_fwd(xc, dt_raw, dtb_p, alog_p, dsk_p, bl, n_heads,
                                    side=_gather_exchange([shard(n) for n in ffn2_names]), name="ssd_fwd")
    full.update({n: rows(g) for n, g in zip(ffn2_names, got_ffn2)})
    yn = _gated_norm_fwd(y, z, ssm_norm_g, name="gated_norm_fwd")

    onehot = _bucket_onehot()
    bias = _small_mm_hi(rel_bias_table.T, onehot, NN, name="rel_bias").reshape(ATTN_Q_HEADS, CHUNK, 2 * CHUNK)
    o, lse = _attn_fwd(q, kv, bias, sinks_p, bl, name="attn_fwd")

    ys, ya, merged, mix, h2 = _merge_fwd(yn, o, gs, ga, full['w_ssm_proj'], full['w_attn_proj'], full['w_out'], h1, mix_post_g,
                                         name="merge_fwd")

    h3, saved2, _, _ = _ffn_forward(h2, ffn2_pre_g, full['ffn2_w_gate'], full['ffn2_w_up'], full['ffn2_w_down'], ffn2_post_g, "ffn2")

    dh3, loss_vec = _loss_grad(h3, tgt2, name="loss")
    loss = lax.psum(loss_vec[0, 0], ("x", "y", "c"))

    grads, own, wire, received = {}, {}, {}, {}
    dh2, grads['ffn2_pre_g'], grads['ffn2_post_g'], g32, g16, _ = _ffn_backward(
        dh3, saved2, ffn2_pre_g, full['ffn2_w_gate'], full['ffn2_w_up'], full['ffn2_w_down'], ffn2_post_g, "ffn2")
    own.update(zip(ffn2_names, map(_stack8, g32)))
    wire.update(zip(ffn2_names, map(_stack8, g16)))

    dmix, dys, dya, dgs, dga, dyn, do, grads['mix_post_g'] = _merge_bwd(
        dh2, mix, mix_post_g, gs, ga, ys, ya, full['w_ssm_proj'], full['w_attn_proj'], full['w_out'], name="merge_bwd")
    for n, (lhs, rhs) in zip(mixer_names, ((yn, dys), (o, dya), (merged, dmix))):
        g32_, g16_, _ = _mm_tn([lhs], rhs, name=f"d{n}")
        own[n], wire[n] = _stack8(g32_), _stack8(g16_)

    dq, dkv, dbias, dsinks = _attn_bwd(do, q, kv, lse, bias, sinks_p, bl, name="attn_bwd")
    d_table = _small_mm_hi(onehot, dbias.reshape(ATTN_Q_HEADS, -1), NT, name="rel_bias_bwd")

    dy, dz, grads['ssm_norm_g'] = _gated_norm_bwd(dyn, y, z, ssm_norm_g, name="gated_norm_bwd")

    first_group = ffn2_names + mixer_names
    (dxc, ddt_raw, ddtb, dalog, ddsk), got = _ssd_bwd(dy, y, xc, dt_raw, hprev, dtb_p, alog_p, dsk_p, bl, n_heads,
                                                      side=_scatter_exchange([wire[n] for n in first_group]), name="ssd_bwd")
    received.update(zip(first_group, got))
    dxbc, dconv_w8, grads['conv_b'] = _conv_bwd(dxc, xbc, conv_w_full, conv_b, bl, name="conv_bwd")

    wide32, wide16, _ = _mm_tn([dgs, dga, dz, dxbc, dq], u, name="dw_in")
    kv32, kv16, _ = _mm_tn([dkv], u, name="dw_in_kv")
    dt32, dt16, _ = _mm_tn([ddt_raw], u, name="dw_in_dt")

    def original_order(wide, kv_part, dt_part):
        return _stack8(jnp.concatenate([wide[:dt_lo], dt_part[:n_heads], wide[dt_lo:], kv_part], axis=0))
    own['w_in'], wire['w_in'] = original_order(wide32, kv32, dt32), original_order(wide16, kv16, dt16)
    own['conv_w'] = jnp.transpose(dconv_w8[:SSM_CONV].reshape(SSM_CONV, N_DEV, conv_dim // N_DEV), (1, 0, 2))

    segs = [(g_, 0, off[k_]) for g_, k_ in zip([dgs, dga, dz, dxbc, dq, dkv, ddt_raw], ('gs', 'ga', 'z', 'xbc', 'q', 'kv', 'dt'))]
    dh1, grads['mix_pre_g'], got = _mm_nn_rmsbwd(segs, [win_p], h1, mix_pre_g, dh2,
                                                 side=_scatter_exchange([wire['w_in'], own['conv_w']]), name="mix_du")
    received.update(zip(('w_in', 'conv_w'), got))

    dx2, grads['ffn1_pre_g'], grads['ffn1_post_g'], g32, _, got = _ffn_backward(
        dh1, saved1, ffn1_pre_g, full['ffn1_w_gate'], full['ffn1_w_up'], full['ffn1_w_down'], ffn1_post_g, "ffn1", chain=True)
    own.update(zip(ffn1_names, map(_stack8, g32)))
    received.update(zip(ffn1_names, got))

    me = 4 * lax.axis_index("x") + 2 * lax.axis_index("y") + lax.axis_index("c")

    def own_block(a):
        return lax.dynamic_index_in_dim(a, me, 0, keepdims=False)
    reduced = {n: _reduce_blocks(own_block(own[n]), received[n], name=f"reduce_{n}") for n in big}
    conv_sum = _reduce_blocks(own_block(own['conv_w']), received['conv_w'], name="reduce_conv_w")

    grads['dt_bias'], grads['a_log'], grads['d_skip'] = ddtb[:, :n_heads], dalog[:, :n_heads], ddsk[:, :n_heads]
    grads['attn_sinks'] = dsinks[:, :ATTN_Q_HEADS]
    grads['rel_bias_table'] = d_table
    small = [n for n in weight_names if n not in big and n != 'conv_w']
    flat = jnp.concatenate([grads[n].reshape(-1) for n in small])
    n_small = flat.shape[0]
    n_rows = -(-n_small // (8 * LANES)) * 8
    flat = jnp.pad(flat, (0, n_rows * LANES - n_small)).reshape(n_rows, LANES)
    summed = _all_reduce_small(flat, name="allreduce_small").reshape(-1)
    pos = 0
    for n in small:
        size = grads[n].size
        grads[n] = summed[pos:pos + size].reshape(args[n].shape)
        pos += size

    out_g, out_d, out_m, out_v = {}, {}, {}, {}
    for n in big:
        w2, m2, v2 = local_2d(n, args[n]), local_2d(n, args['m_' + n]), local_2d(n, args['v_' + n])
        dlt, mn, vn = _adamw(w2, reduced[n], m2, v2, name=f"adamw_{n}")

        def back(a, n=n):
            return (a.T if n in col_sharded else a)[None]
        out_g[n], out_d[n], out_m[n], out_v[n] = back(reduced[n]), back(dlt), back(mn), back(vn)

    def pack(prefix):
        vals = [(grads[n] if prefix == 'g' else args[prefix + n]).reshape(-1) for n in small]
        vals.append((conv_sum if prefix == 'g' else args[prefix + 'conv_w']).reshape(-1))
        flat_ = jnp.concatenate(vals)
        rows_ = -(-flat_.shape[0] // (8 * LANES)) * 8
        return jnp.pad(flat_, (0, rows_ * LANES - flat_.shape[0])).reshape(rows_, LANES)

    g_small = pack('g')
    d_small, m_small, v_small = _adamw(pack(''), g_small, pack('m_'), pack('v_'), name="adamw_small")
    pos = 0
    for n in small + ['conv_w']:
        shape = args[n].shape
        size = int(np.prod(shape))
        for dst, src in ((out_g, g_small), (out_d, d_small), (out_m, m_small), (out_v, v_small)):
            dst[n] = src.reshape(-1)[pos:pos + size].reshape(shape)
        pos += size

    grad_x = dx2.reshape(bl, s_len, d)
    return (loss, grad_x, *[out_g[n] for n in weight_names], *[out_d[n] for n in weight_names],
            *[out_m[n] for n in weight_names], *[out_v[n] for n in weight_names])
```

```python
import functools
import math

import numpy as np
import jax
import jax.numpy as jnp
from jax import lax
from jax.experimental import pallas as pl
from jax.experimental.pallas import tpu as pltpu

F32 = jnp.float32
BF16 = jnp.bfloat16
MESH = pl.DeviceIdType.MESH
N_DEV = 8

SSM_HEAD_DIM = 64
SSM_GROUPS = 4
SSM_STATE = 128
SSM_CONV = 4
CHUNK = 128
ATTN_HEAD_DIM = 64
ATTN_Q_HEADS = 16
ATTN_KV_HEADS = 4
REL_BUCKETS = 32
REL_MAX_DISTANCE = 128
RMS_EPS = 1e-6
FFN_RESIDUAL_WEIGHT = 0.5
ADAM_LR, ADAM_B1, ADAM_B2, ADAM_EPS, ADAM_WD, ADAM_STEP = 0.001, 0.9, 0.999, 1e-08, 0.01, 10

LANES = 128
VMEM_LIMIT_BYTES = 56 * 1024 * 1024
FFN_COL_TILE = 1408

NEG_INF = float("-inf")


def _params(*sem):
    return pltpu.CompilerParams(dimension_semantics=sem, vmem_limit_bytes=VMEM_LIMIT_BYTES)


def _tile(n, pref, mult=8):
    if n <= pref:
        return n
    t = (pref // mult) * mult
    while t >= mult:
        if n % t == 0:
            return t
        t -= mult
    return n


def _sigmoid(x):
    return 1.0 / (1.0 + jnp.exp(-x))


def _dot(a, b, dims):
    return lax.dot_general(a, b, (dims, ((), ())), preferred_element_type=F32)


NN = ((1,), (0,))
NT = ((1,), (1,))
TN = ((0,), (0,))


def _dot_hi(a, b, dims=NN):
    return lax.dot_general(a, b, (dims, ((), ())), preferred_element_type=F32, precision=lax.Precision.HIGHEST)


def _const_spec(shape):
    nd = len(shape)
    return pl.BlockSpec(shape, lambda *_: (0,) * nd)


def _resident_spec(shape):
    nd = len(shape)
    return pl.BlockSpec(shape, lambda *_: (0,) * nd, pipeline_mode=pl.Buffered(1))


class _Exchange:
    def __init__(self, arrays, out_shape, scratch, start, finish):
        self.arrays, self.out_shape, self.scratch, self.start, self.finish = arrays, out_shape, scratch, start, finish


def _hosted_call(body, *, grid, in_specs, out_specs, out_shape, scratch_shapes, operands, side, name):
    in_specs, out_specs, out_shape, scratch_shapes = list(in_specs), list(out_specs), list(out_shape), list(scratch_shapes)
    sem = ("arbitrary",) * len(grid)
    if side is None:
        outs = pl.pallas_call(body, out_shape=tuple(out_shape), grid=grid, in_specs=in_specs, out_specs=tuple(out_specs),
                              scratch_shapes=scratch_shapes, name=name, compiler_params=_params(*sem))(*operands)
        return tuple(outs), ()
    n_in, n_out, n_scr = len(in_specs), len(out_shape), len(scratch_shapes)
    s_in, s_out = len(side.arrays), len(side.out_shape)

    def wrapped(*refs):
        refs = list(refs)
        main_in, side_in = refs[:n_in], refs[n_in:n_in + s_in]
        o0 = n_in + s_in
        main_out, side_out = refs[o0:o0 + n_out], refs[o0 + n_out:o0 + n_out + s_out]
        c0 = o0 + n_out + s_out
        main_scr, side_scr = refs[c0:c0 + n_scr], refs[c0 + n_scr:]
        ids = [pl.program_id(ax) for ax in range(len(grid))]
        first = functools.reduce(jnp.logical_and, [i == 0 for i in ids])
        last = functools.reduce(jnp.logical_and, [i == g - 1 for i, g in zip(ids, grid)])

        @pl.when(first)
        def _():
            side.start(side_in, side_out, side_scr)

        body(*main_in, *main_out, *main_scr)

        @pl.when(last)
        def _():
            side.finish(side_in, side_out, side_scr)

    hbm = pl.BlockSpec(memory_space=pl.ANY)
    outs = pl.pallas_call(
        wrapped, out_shape=tuple(out_shape + list(side.out_shape)), grid=grid,
        in_specs=in_specs + [hbm] * s_in, out_specs=tuple(out_specs + [hbm] * s_out),
        scratch_shapes=scratch_shapes + list(side.scratch), name=name, compiler_params=_params(*sem))(*operands, *side.arrays)
    return tuple(outs[:n_out]), tuple(outs[n_out:])


def _mm_nt(a, bt, *, row_off=0, n_out=None, out_dtype=BF16, tm=512, tn=512, name):
    m, k = a.shape
    n_out = bt.shape[0] if n_out is None else n_out
    tm, tn = _tile(m, tm), _tile(n_out, tn, LANES)

    def body(a_ref, b_ref, o_ref):
        o_ref[...] = _dot(a_ref[...].astype(BF16), b_ref[...].astype(BF16), NT).astype(o_ref.dtype)

    return pl.pallas_call(
        body, out_shape=jax.ShapeDtypeStruct((m, n_out), out_dtype), grid=(n_out // tn, m // tm),
        in_specs=[pl.BlockSpec((tm, k), lambda j, i: (i, 0)), pl.BlockSpec((tn, k), lambda j, i: (j + row_off, 0))],
        out_specs=pl.BlockSpec((tm, tn), lambda j, i: (i, j)),
        name=name, compiler_params=_params("parallel", "arbitrary"))(a, bt)


def _mm_tn(a_list, b, *, tm=1408, tk=1024, side=None, name):
    t, n = b.shape
    tk = _tile(t, tk)
    nk = t // tk
    widths = [a.shape[1] for a in a_list]
    tm = _tile(math.gcd(*widths), tm, LANES)
    assert all(w % tm == 0 for w in widths)
    starts = np.cumsum([0] + [w // tm for w in widths])
    nseg = len(a_list)

    def a_spec(s):
        lo, hi = int(starts[s]), int(starts[s + 1])

        def idx(i, k):
            active = jnp.logical_and(i >= lo, i < hi)
            return (jnp.where(active, k, 0), jnp.clip(i - lo, 0, hi - lo - 1))
        return pl.BlockSpec((tk, tm), idx)

    def body(*refs):
        a_refs, b_ref, o_ref, o16_ref, acc = refs[:nseg], refs[nseg], refs[nseg + 1], refs[nseg + 2], refs[nseg + 3]
        i, k = pl.program_id(0), pl.program_id(1)

        @pl.when(k == 0)
        def _():
            acc[...] = jnp.zeros_like(acc)

        bv = b_ref[...].astype(BF16)
        for s in range(nseg):
            lo, hi = int(starts[s]), int(starts[s + 1])

            @pl.when(jnp.logical_and(i >= lo, i < hi))
            def _(s=s):
                acc[...] += _dot(a_refs[s][...].astype(BF16), bv, TN)

        @pl.when(k == nk - 1)
        def _():
            o_ref[...] = acc[...]
            o16_ref[...] = acc[...].astype(BF16)

    rows = int(starts[-1]) * tm
    o_spec = pl.BlockSpec((tm, n), lambda i, k: (i, 0))
    (o32, o16), got = _hosted_call(
        body, out_shape=[jax.ShapeDtypeStruct((rows, n), F32), jax.ShapeDtypeStruct((rows, n), BF16)], grid=(int(starts[-1]), nk),
        in_specs=[a_spec(s) for s in range(nseg)] + [pl.BlockSpec((tk, n), lambda i, k: (k, 0))],
        out_specs=[o_spec, o_spec], scratch_shapes=[pltpu.VMEM((tm, n), F32)], operands=list(a_list) + [b], side=side, name=name)
    return o32, o16, got


def _mm_nn_rmsbwd(segs, weights, x, g, dres, *, tm=256, side=None, name):
    t, d = x.shape
    tm = _tile(t, tm)
    nseg, nw = len(segs), len(weights)

    def body(*refs):
        a_refs, w_refs = refs[:nseg], refs[nseg:nseg + nw]
        x_ref, g_ref, dres_ref, dx_ref, dg_ref = refs[nseg + nw:]

        @pl.when(pl.program_id(0) == 0)
        def _():
            dg_ref[...] = jnp.zeros_like(dg_ref)

        dn = None
        for s, (a, w_idx, row0) in enumerate(segs):
            part = _dot(a_refs[s][...].astype(BF16), w_refs[w_idx][row0:row0 + a.shape[1], :], NN)
            dn = part if dn is None else dn + part
        xv = x_ref[...]
        r = lax.rsqrt(jnp.mean(xv * xv, axis=-1, keepdims=True) + RMS_EPS)
        xhat = xv * r
        dyg = dn * g_ref[...]
        dx_ref[...] = dres_ref[...] + r * (dyg - xhat * jnp.mean(dyg * xhat, axis=-1, keepdims=True))
        dg_ref[...] += jnp.sum(dn * xhat, axis=0, keepdims=True)

    row = pl.BlockSpec((tm, d), lambda i: (i, 0))
    in_specs = [pl.BlockSpec((tm, a.shape[1]), lambda i: (i, 0)) for a, _, _ in segs]
    in_specs += [_resident_spec(w.shape) for w in weights] + [row, _const_spec((1, d)), row]
    (dx, dg), extra = _hosted_call(
        body, grid=(t // tm,), in_specs=in_specs, out_specs=[row, _const_spec((1, d))],
        out_shape=[jax.ShapeDtypeStruct((t, d), F32), jax.ShapeDtypeStruct((1, d), F32)], scratch_shapes=[],
        operands=[a for a, _, _ in segs] + list(weights) + [x, g, dres], side=side, name=name)
    return dx, dg, extra


def _rms_fwd(x, g, *, name):
    t, d = x.shape
    tm = _tile(t, 512)

    def body(x_ref, g_ref, o_ref):
        xv = x_ref[...]
        r = lax.rsqrt(jnp.mean(xv * xv, axis=-1, keepdims=True) + RMS_EPS)
        o_ref[...] = (xv * r * g_ref[...]).astype(o_ref.dtype)

    row = pl.BlockSpec((tm, d), lambda i: (i, 0))
    return pl.pallas_call(body, out_shape=jax.ShapeDtypeStruct((t, d), BF16), grid=(t // tm,),
                          in_specs=[row, _const_spec((1, d))], out_specs=row, name=name, compiler_params=_params("parallel"))(x, g)


def _col_chunks(width, chunk=4 * LANES):
    return [(c0, min(c0 + chunk, width)) for c0 in range(0, width, chunk)]


def _ffn_up(n, wgt, wut, *, side=None, name):
    t, d = n.shape
    f = wgt.shape[0]
    tm, tn = _tile(t, 512), _tile(f, FFN_COL_TILE, LANES)

    def body(n_ref, wg_ref, wu_ref, g_ref, u_ref, h_ref):
        nv = n_ref[...]
        for c0, c1 in _col_chunks(tn):
            gv = _dot(nv, wg_ref[c0:c1, :], NT)
            uv = _dot(nv, wu_ref[c0:c1, :], NT)
            g_ref[:, c0:c1] = gv.astype(BF16)
            u_ref[:, c0:c1] = uv.astype(BF16)
            h_ref[:, c0:c1] = (gv * _sigmoid(gv) * uv).astype(BF16)

    w_spec = pl.BlockSpec((tn, d), lambda j, i: (j, 0))
    o_spec = pl.BlockSpec((tm, tn), lambda j, i: (i, j))
    shp = jax.ShapeDtypeStruct((t, f), BF16)
    return _hosted_call(body, grid=(f // tn, t // tm), in_specs=[pl.BlockSpec((tm, d), lambda j, i: (i, 0)), w_spec, w_spec],
                        out_specs=[o_spec, o_spec, o_spec], out_shape=[shp, shp, shp], scratch_shapes=[], operands=[n, wgt, wut],
                        side=side, name=name)


def _rms_residual(acc, h, gp, weight):
    r = lax.rsqrt(jnp.mean(acc * acc, axis=-1, keepdims=True) + RMS_EPS)
    return h + weight * (acc * r * gp)


def _ffn_down(hid, wd, h_in, gp, *, side=None, name):
    t, f = hid.shape
    d = wd.shape[1]
    tm = _tile(t, 256)

    def body(hid_ref, wd_ref, hin_ref, gp_ref, f_ref, hout_ref):
        acc = _dot(hid_ref[...], wd_ref[...], NN)
        f_ref[...] = acc
        hout_ref[...] = _rms_residual(acc, hin_ref[...], gp_ref[...], FFN_RESIDUAL_WEIGHT)

    row = pl.BlockSpec((tm, d), lambda i: (i, 0))
    shp = jax.ShapeDtypeStruct((t, d), F32)
    return _hosted_call(body, grid=(t // tm,),
                        in_specs=[pl.BlockSpec((tm, f), lambda i: (i, 0)), _resident_spec((f, d)), row, _const_spec((1, d))],
                        out_specs=[row, row], out_shape=[shp, shp], scratch_shapes=[], operands=[hid, wd, h_in, gp],
                        side=side, name=name)


def _post_bwd(dh, f, gp, weight, *, name):
    t, d = f.shape
    tm = _tile(t, 512)

    def body(dh_ref, f_ref, gp_ref, df_ref, dgp_ref):
        @pl.when(pl.program_id(0) == 0)
        def _():
            dgp_ref[...] = jnp.zeros_like(dgp_ref)
        fv = f_ref[...]
        dy = weight * dh_ref[...]
        r = lax.rsqrt(jnp.mean(fv * fv, axis=-1, keepdims=True) + RMS_EPS)
        fhat = fv * r
        dyg = dy * gp_ref[...]
        df_ref[...] = (r * (dyg - fhat * jnp.mean(dyg * fhat, axis=-1, keepdims=True))).astype(BF16)
        dgp_ref[...] += jnp.sum(dy * fhat, axis=0, keepdims=True)

    row = pl.BlockSpec((tm, d), lambda i: (i, 0))
    return pl.pallas_call(body, out_shape=(jax.ShapeDtypeStruct((t, d), BF16), jax.ShapeDtypeStruct((1, d), F32)), grid=(t // tm,),
                          in_specs=[row, row, _const_spec((1, d))], out_specs=(row, _const_spec((1, d))),
                          name=name, compiler_params=_params("arbitrary"))(dh, f, gp)


def _ffn_dhid(df, wd, g, u, *, name):
    t, d = df.shape
    f = wd.shape[0]
    tm, tn = _tile(t, 512), _tile(f, FFN_COL_TILE, LANES)

    def body(df_ref, wd_ref, g_ref, u_ref, dg_ref, du_ref):
        dfv = df_ref[...]
        for c0, c1 in _col_chunks(tn):
            dh = _dot(dfv, wd_ref[c0:c1, :], NT)
            gv = g_ref[:, c0:c1].astype(F32)
            uv = u_ref[:, c0:c1].astype(F32)
            sg = _sigmoid(gv)
            dg_ref[:, c0:c1] = (dh * uv * (sg * (1.0 + gv * (1.0 - sg)))).astype(BF16)
            du_ref[:, c0:c1] = (dh * (gv * sg)).astype(BF16)

    o_spec = pl.BlockSpec((tm, tn), lambda j, i: (i, j))
    shp = jax.ShapeDtypeStruct((t, f), BF16)
    return pl.pallas_call(body, out_shape=(shp, shp), grid=(f // tn, t // tm),
                          in_specs=[pl.BlockSpec((tm, d), lambda j, i: (i, 0)), pl.BlockSpec((tn, d), lambda j, i: (j, 0)), o_spec, o_spec],
                          out_specs=(o_spec, o_spec), name=name, compiler_params=_params("parallel", "arbitrary"))(df, wd, g, u)


def _ffn_forward(h_in, g_pre, wgt, wut, wd, g_post, tag, side_up=None, side_down=None):
    n = _rms_fwd(h_in, g_pre, name=f"{tag}_prenorm")
    (g, u, hid), got_up = _ffn_up(n, wgt, wut, side=side_up, name=f"{tag}_up")
    wd = wd(got_up) if callable(wd) else wd
    (f, h_out), got_down = _ffn_down(hid, wd, h_in, g_post, side=side_down, name=f"{tag}_down")
    return h_out, (h_in, n, g, u, hid, f), got_up, got_down


def _stack8(g):
    return g.reshape(N_DEV, g.shape[0] // N_DEV, g.shape[1])


def _ffn_backward(dh_out, saved, g_pre, wgt, wut, wd, g_post, tag, chain=False):
    h_in, n, g, u, hid, f = saved

    def side_of(grad16):
        return _scatter_exchange([_stack8(grad16)]) if chain else None

    df, dg_post = _post_bwd(dh_out, f, g_post, FFN_RESIDUAL_WEIGHT, name=f"{tag}_post_bwd")
    dgate, dup = _ffn_dhid(df, wd, g, u, name=f"{tag}_dhid")
    d_wd, d_wd16, _ = _mm_tn([hid], df, name=f"{tag}_dwd")
    d_wgt, d_wgt16, got_wd = _mm_tn([dgate], n, side=side_of(d_wd16), name=f"{tag}_dwg")
    d_wut, d_wut16, got_wg = _mm_tn([dup], n, side=side_of(d_wgt16), name=f"{tag}_dwu")
    dh_in, dg_pre, got_wu = _mm_nn_rmsbwd([(dgate, 0, 0), (dup, 1, 0)], [wgt, wut], h_in, g_pre, dh_out, side=side_of(d_wut16),
                                          name=f"{tag}_dn")
    received = (got_wg[0], got_wu[0], got_wd[0]) if chain else None
    return dh_in, dg_pre, dg_post, (d_wgt, d_wut, d_wd), (d_wgt16, d_wut16, d_wd16), received


CONV_ROWS = 128
HALO = 8


def _taps(w_ref):
    return [w_ref[k:k + 1, :] for k in range(SSM_CONV)]


def _conv_chunk(x_ref, xs, r0, taps, bias):
    xs[HALO + r0:HALO + r0 + CONV_ROWS, :] = x_ref[r0:r0 + CONV_ROWS, :].astype(F32)
    shifted = [xs[HALO + r0 - k:HALO + r0 - k + CONV_ROWS, :] for k in range(SSM_CONV)]
    pre = bias + shifted[0] * taps[SSM_CONV - 1]
    for k in range(1, SSM_CONV):
        pre = pre + shifted[k] * taps[SSM_CONV - 1 - k]
    return shifted, pre


def _fold_rows(a):
    return functools.reduce(jnp.add, [a[i:i + 8] for i in range(0, a.shape[0], 8)])


def _conv_fwd(xbc, conv_w, conv_b, bl, *, name):
    t, c = xbc.shape
    s = t // bl
    tc = LANES
    assert s % CONV_ROWS == 0

    def body(x_ref, w_ref, b_ref, o_ref, xs):
        taps, bias = _taps(w_ref), b_ref[...]
        xs[0:HALO, :] = jnp.zeros((HALO, tc), F32)
        for r0 in range(0, s, CONV_ROWS):
            _, pre = _conv_chunk(x_ref, xs, r0, taps, bias)
            o_ref[r0:r0 + CONV_ROWS, :] = (pre * _sigmoid(pre)).astype(o_ref.dtype)

    blk = pl.BlockSpec((s, tc), lambda b, j: (b, j))
    return pl.pallas_call(body, out_shape=jax.ShapeDtypeStruct((t, c), BF16), grid=(bl, c // tc),
                          in_specs=[blk, pl.BlockSpec((SSM_CONV, tc), lambda b, j: (0, j)), pl.BlockSpec((1, tc), lambda b, j: (0, j))],
                          out_specs=blk, scratch_shapes=[pltpu.VMEM((HALO + s, tc), F32)],
                          name=name, compiler_params=_params("parallel", "arbitrary"))(xbc, conv_w, conv_b)


def _conv_bwd(dxc, xbc, conv_w, conv_b, bl, *, name):
    t, c = xbc.shape
    s = t // bl
    tc = LANES

    def body(dy_ref, x_ref, w_ref, b_ref, dx_ref, dw_ref, db_ref, xs, dpre_s):
        @pl.when(pl.program_id(1) == 0)
        def _():
            dw_ref[...] = jnp.zeros_like(dw_ref)
            db_ref[...] = jnp.zeros_like(db_ref)

        taps, bias = _taps(w_ref), b_ref[...]
        zero8 = jnp.zeros((HALO, tc), F32)
        xs[0:HALO, :] = zero8
        dpre_s[s:s + HALO, :] = zero8
        sums = [zero8] * (SSM_CONV + 1)
        for r0 in range(0, s, CONV_ROWS):
            shifted, pre = _conv_chunk(x_ref, xs, r0, taps, bias)
            sg = _sigmoid(pre)
            dpre = dy_ref[r0:r0 + CONV_ROWS, :].astype(F32) * (sg * (1.0 + pre * (1.0 - sg)))
            dpre_s[r0:r0 + CONV_ROWS, :] = dpre
            sums = [acc + _fold_rows(dpre * sh) for acc, sh in zip(sums[:-1], shifted)] + [sums[-1] + _fold_rows(dpre)]
        for k in range(SSM_CONV):
            dw_ref[SSM_CONV - 1 - k:SSM_CONV - k, :] += jnp.sum(sums[k], axis=0, keepdims=True)
        db_ref[...] += jnp.sum(sums[-1], axis=0, keepdims=True)
        for r0 in range(0, s, CONV_ROWS):
            dx = dpre_s[r0:r0 + CONV_ROWS, :] * taps[SSM_CONV - 1]
            for k in range(1, SSM_CONV):
                dx = dx + dpre_s[r0 + k:r0 + k + CONV_ROWS, :] * taps[SSM_CONV - 1 - k]
            dx_ref[r0:r0 + CONV_ROWS, :] = dx.astype(dx_ref.dtype)

    blk = pl.BlockSpec((s, tc), lambda j, b: (b, j))
    return pl.pallas_call(
        body, out_shape=(jax.ShapeDtypeStruct((t, c), BF16), jax.ShapeDtypeStruct((8, c), F32), jax.ShapeDtypeStruct((1, c), F32)),
        grid=(c // tc, bl),
        in_specs=[blk, blk, pl.BlockSpec((SSM_CONV, tc), lambda j, b: (0, j)), pl.BlockSpec((1, tc), lambda j, b: (0, j))],
        out_specs=(blk, pl.BlockSpec((8, tc), lambda j, b: (0, j)), pl.BlockSpec((1, tc), lambda j, b: (0, j))),
        scratch_shapes=[pltpu.VMEM((HALO + s, tc), F32), pltpu.VMEM((s + HALO, tc), F32)],
        name=name, compiler_params=_params("parallel", "arbitrary"))(dxc, xbc, conv_w, conv_b)


def _softplus(x):
    return jnp.maximum(x, 0.0) + jnp.log1p(jnp.exp(-jnp.abs(x)))


def _hilo_dot(v, m_b, dims=NN):
    hi = v.astype(BF16)
    lo = (v - hi.astype(F32)).astype(BF16)
    return _dot(hi, m_b, dims) + _dot(lo, m_b, dims)


def _ssd_chunk_common(dtraw_ref, dtb_ref, alog_ref, dsk_ref, d_inner):
    q, p = CHUNK, SSM_HEAD_DIM
    shift = p.bit_length() - 1
    assert 1 << shift == p
    dt = _softplus(dtraw_ref[...] + dtb_ref[...])
    a = -jnp.exp(alog_ref[...])
    ii = lax.broadcasted_iota(jnp.int32, (q, q), 0)
    jj = lax.broadcasted_iota(jnp.int32, (q, q), 1)
    causal = ii >= jj
    tril = jnp.where(causal, 1.0, 0.0).astype(F32)
    triu = jnp.where(ii <= jj, 1.0, 0.0).astype(F32)
    a_cs = _dot_hi(tril, dt * a)
    a_cs_t = a_cs.T
    a_last = a_cs[q - 1:q, :]
    e_col = jnp.exp(a_cs)
    dec_end = jnp.exp(a_last - a_cs)
    head_of_col = lax.shift_right_logical(lax.broadcasted_iota(jnp.int32, (LANES, d_inner), 1), shift)
    spread = (lax.broadcasted_iota(jnp.int32, (LANES, d_inner), 0) == head_of_col).astype(BF16)
    wide = _hilo_dot(jnp.concatenate([dt, e_col, dec_end, jnp.broadcast_to(dsk_ref[...], (8, LANES))], axis=0), spread)
    return dict(dt=dt, a=a, a_cs=a_cs, a_cs_t=a_cs_t, a_last=a_last, dec_end=dec_end, causal=causal, triu=triu,
                dt_e=wide[:q], e_e=wide[q:2 * q], dec_e=wide[2 * q:3 * q], dsk_e=wide[3 * q:3 * q + 1])


def _fill_block_diag(bd_ref, src_ref, hpg, col0=0):
    q, p = CHUNK, SSM_HEAD_DIM
    for hh in range(hpg):
        bd_ref[hh * q:(hh + 1) * q, hh * p:(hh + 1) * p] = src_ref[:, col0 + hh * p:col0 + (hh + 1) * p]


def _lane_onehot(h):
    return (lax.broadcasted_iota(jnp.int32, (1, LANES), 1) == h).astype(F32)


def _ssd_fwd(xc, dt_raw, dt_bias, a_log, d_skip, bl, n_heads, *, side=None, name):
    t = xc.shape[0]
    q, p, nst, grp = CHUNK, SSM_HEAD_DIM, SSM_STATE, SSM_GROUPS
    d_inner = n_heads * p
    hpg = n_heads // grp
    gw = hpg * p
    nc = t // bl // q
    assert d_inner % (grp * nst) == 0 and nst == LANES

    def body(xs_ref, b_ref, c_ref, dtraw_ref, dtb_ref, alog_ref, dsk_ref, y_ref, hprev_ref, state, m_all, x_bd, xdt_s):
        @pl.when(jnp.logical_and(pl.program_id(0) == 0, pl.program_id(1) == 0))
        def _():
            x_bd[...] = jnp.zeros_like(x_bd)

        @pl.when(pl.program_id(1) == 0)
        def _():
            state[...] = jnp.zeros_like(state)

        cm = _ssd_chunk_common(dtraw_ref, dtb_ref, alog_ref, dsk_ref, d_inner)
        for g in range(grp):
            cols = slice(g * gw, (g + 1) * gw)
            bg = b_ref[:, g * nst:(g + 1) * nst]
            cg = c_ref[:, g * nst:(g + 1) * nst]
            scores = _dot(cg, bg, NT)
            for hh in range(hpg):
                h = g * hpg + hh
                seg = cm['a_cs'][:, h:h + 1] - cm['a_cs_t'][h:h + 1, :]
                m_all[:, hh * q:(hh + 1) * q] = (scores * jnp.exp(jnp.where(cm['causal'], seg, NEG_INF))).astype(BF16)
            xs = xs_ref[:, cols].astype(F32)
            xdt = xs * cm['dt_e'][:, cols]
            xdt_s[...] = xdt.astype(BF16)
            _fill_block_diag(x_bd, xdt_s, hpg)
            hprev = state[g]
            hprev_ref[g] = hprev
            y = _dot(m_all[...], x_bd[...], NN) + cm['e_e'][:, cols] * _dot(cg, hprev.astype(BF16), NT)
            y_ref[:, cols] = y + cm['dsk_e'][:, cols] * xs
            st = _dot((xdt * cm['dec_e'][:, cols]).astype(BF16), bg, TN)
            for hh in range(hpg):
                h = g * hpg + hh
                rows = slice(hh * p, (hh + 1) * p)
                state[g, rows, :] = jnp.exp(cm['a_last'][:, h:h + 1]) * hprev[rows] + st[rows]

    gn = grp * nst

    def rowmap(b, c):
        return b * nc + c
    vec = pl.BlockSpec((1, LANES), lambda b, c: (0, 0))
    return _hosted_call(
        body,
        out_shape=[jax.ShapeDtypeStruct((t, d_inner), F32), jax.ShapeDtypeStruct((t // q, grp, gw, nst), F32)],
        grid=(bl, nc),
        in_specs=[pl.BlockSpec((q, d_inner), lambda b, c: (rowmap(b, c), 0)),
                  pl.BlockSpec((q, gn), lambda b, c: (rowmap(b, c), d_inner // gn)),
                  pl.BlockSpec((q, gn), lambda b, c: (rowmap(b, c), d_inner // gn + 1)),
                  pl.BlockSpec((q, LANES), lambda b, c: (rowmap(b, c), 0)), vec, vec, vec],
        out_specs=[pl.BlockSpec((q, d_inner), lambda b, c: (rowmap(b, c), 0)),
                   pl.BlockSpec((None, grp, gw, nst), lambda b, c: (rowmap(b, c), 0, 0, 0))],
        scratch_shapes=[pltpu.VMEM((grp, gw, nst), F32), pltpu.VMEM((q, hpg * q), BF16), pltpu.VMEM((hpg * q, gw), BF16),
                        pltpu.VMEM((q, gw), BF16)],
        operands=[xc, xc, xc, dt_raw, dt_bias, a_log, d_skip], side=side, name=name)


def _ssd_bwd(dy, y, xc, dt_raw, hprev_all, dt_bias, a_log, d_skip, bl, n_heads, *, side=None, name):
    t, c_dim = xc.shape
    q, p, nst, grp = CHUNK, SSM_HEAD_DIM, SSM_STATE, SSM_GROUPS
    d_inner = n_heads * p
    hpg = n_heads // grp
    gw = hpg * p
    nc = t // bl // q
    gn = grp * nst
    shift = p.bit_length() - 1

    def body(dy_ref, y_ref, xs_ref, b_ref, c_ref, dtraw_ref, hprev_ref, dtb_ref, alog_ref, dsk_ref,
             dxc_ref, ddtraw_ref, ddtb_ref, dalog_ref, ddsk_ref, dstate, mt_all, x_bd, dy_bd, xdt_s):
        @pl.when(jnp.logical_and(pl.program_id(0) == 0, pl.program_id(1) == 0))
        def _():
            ddtb_ref[...] = jnp.zeros_like(ddtb_ref)
            dalog_ref[...] = jnp.zeros_like(dalog_ref)
            ddsk_ref[...] = jnp.zeros_like(ddsk_ref)
            x_bd[...] = jnp.zeros_like(x_bd)
            dy_bd[...] = jnp.zeros_like(dy_bd)

        @pl.when(pl.program_id(1) == 0)
        def _():
            dstate[...] = jnp.zeros_like(dstate)

        cm = _ssd_chunk_common(dtraw_ref, dtb_ref, alog_ref, dsk_ref, d_inner)
        causal = cm['causal']
        upper = cm['triu'] > 0.5
        seg_row = lax.shift_right_logical(lax.broadcasted_iota(jnp.int32, (gw, LANES), 0), shift)
        seg_lane = lax.broadcasted_iota(jnp.int32, (gw, LANES), 1)
        sums = jnp.zeros((5 * q, LANES), F32)
        state_dot = jnp.zeros((1, LANES), F32)
        for g in range(grp):
            cols = slice(g * gw, (g + 1) * gw)
            seg_sum = (seg_row + g * hpg == seg_lane).astype(BF16)
            bg = b_ref[:, g * nst:(g + 1) * nst]
            cg = c_ref[:, g * nst:(g + 1) * nst]
            scores_t = _dot(bg, cg, NT)
            xs = xs_ref[:, cols].astype(F32)
            xdt = xs * cm['dt_e'][:, cols]
            xdt_s[...] = xdt.astype(BF16)
            _fill_block_diag(x_bd, xdt_s, hpg)
            _fill_block_diag(dy_bd, dy_ref, hpg, g * gw)
            dy_b = dy_ref[:, cols]
            dyf = dy_b.astype(F32)
            dm_all = _dot(dy_b, x_bd[...], NT)
            dscores = jnp.zeros((q, q), F32)
            for hh in range(hpg):
                h = g * hpg + hh
                blk = slice(hh * q, (hh + 1) * q)
                seg = cm['a_cs'][:, h:h + 1] - cm['a_cs_t'][h:h + 1, :]
                mt_all[:, blk] = (scores_t * jnp.exp(jnp.where(upper, -seg, NEG_INF))).astype(BF16)
                dscores = dscores + dm_all[:, blk] * jnp.exp(jnp.where(causal, seg, NEG_INF))
            hprev = hprev_ref[g]
            hprev_b = hprev.astype(BF16)
            dhn = dstate[g]
            dhn_b = dhn.astype(BF16)
            e_e, dec_e = cm['e_e'][:, cols], cm['dec_e'][:, cols]
            y_scan = y_ref[:, cols] - cm['dsk_e'][:, cols] * xs
            dye_b = (dyf * e_e).astype(BF16)
            dcg = _dot(dye_b, hprev_b, NN)
            dhp = _dot(dye_b, cg, TN)
            bdh = _dot(bg, dhn_b, NT)
            dbg = _dot((xdt * dec_e).astype(BF16), dhn_b, NN)
            dx_diag = _dot(mt_all[...], dy_bd[...], NN)
            dx = dec_e * bdh + dx_diag
            ds_b = dscores.astype(BF16)
            dcg = dcg + _dot(ds_b, bg, NN)
            dbg = dbg + _dot(ds_b, cg, TN)
            x_rounded = xdt_s[...].astype(F32)
            sums = sums + _hilo_dot(jnp.concatenate([dyf * y_scan, xdt * bdh, x_rounded * dx_diag, dx * xs, dyf * xs], axis=0), seg_sum)
            state_dot = state_dot + jnp.sum(_hilo_dot(dhn * hprev, seg_sum, TN), axis=0, keepdims=True)
            dxc_ref[:, cols] = (dx * cm['dt_e'][:, cols] + cm['dsk_e'][:, cols] * dyf).astype(dxc_ref.dtype)
            dxc_ref[:, d_inner + g * nst:d_inner + (g + 1) * nst] = dbg.astype(dxc_ref.dtype)
            dxc_ref[:, d_inner + gn + g * nst:d_inner + gn + (g + 1) * nst] = dcg.astype(dxc_ref.dtype)
            for hh in range(hpg):
                h = g * hpg + hh
                rows = slice(hh * p, (hh + 1) * p)
                dstate[g, rows, :] = jnp.exp(cm['a_last'][:, h:h + 1]) * dhn[rows] + dhp[rows]
        s_y, s_end, s_diag, s_dt, s_skip = (sums[k * q:(k + 1) * q] for k in range(5))
        dt, a, dec_end = cm['dt'], cm['a'], cm['dec_end']
        last_row = (lax.broadcasted_iota(jnp.int32, (q, 1), 0) == q - 1).astype(F32)
        da_last = jnp.sum(dec_end * s_end, axis=0, keepdims=True) + jnp.exp(cm['a_last']) * state_dot
        da = s_y - dec_end * s_end - s_diag + last_row * da_last
        ddta = _dot_hi(cm['triu'], da)
        ddt = s_dt + ddta * a
        d_a = jnp.sum(ddta * dt, axis=0, keepdims=True)
        ddt_raw = ddt * _sigmoid(dtraw_ref[...] + dtb_ref[...])
        ddtraw_ref[...] = ddt_raw
        ddtb_ref[...] += jnp.sum(ddt_raw, axis=0, keepdims=True)
        dalog_ref[...] += d_a * a
        ddsk_ref[...] += jnp.sum(s_skip, axis=0, keepdims=True)

    def rowmap(b, c):
        return b * nc + (nc - 1 - c)
    vec = pl.BlockSpec((1, LANES), lambda b, c: (0, 0))
    vec_shape = jax.ShapeDtypeStruct((1, LANES), F32)
    return _hosted_call(
        body,
        out_shape=[jax.ShapeDtypeStruct((t, c_dim), BF16), jax.ShapeDtypeStruct((t, LANES), F32), vec_shape, vec_shape, vec_shape],
        grid=(bl, nc),
        in_specs=[pl.BlockSpec((q, d_inner), lambda b, c: (rowmap(b, c), 0)),
                  pl.BlockSpec((q, d_inner), lambda b, c: (rowmap(b, c), 0)),
                  pl.BlockSpec((q, d_inner), lambda b, c: (rowmap(b, c), 0)),
                  pl.BlockSpec((q, gn), lambda b, c: (rowmap(b, c), d_inner // gn)),
                  pl.BlockSpec((q, gn), lambda b, c: (rowmap(b, c), d_inner // gn + 1)),
                  pl.BlockSpec((q, LANES), lambda b, c: (rowmap(b, c), 0)),
                  pl.BlockSpec((None, grp, gw, nst), lambda b, c: (rowmap(b, c), 0, 0, 0)), vec, vec, vec],
        out_specs=[pl.BlockSpec((q, c_dim), lambda b, c: (rowmap(b, c), 0)),
                   pl.BlockSpec((q, LANES), lambda b, c: (rowmap(b, c), 0)), vec, vec, vec],
        scratch_shapes=[pltpu.VMEM((grp, gw, nst), F32), pltpu.VMEM((q, hpg * q), BF16),
                        pltpu.VMEM((hpg * q, gw), BF16), pltpu.VMEM((hpg * q, gw), BF16), pltpu.VMEM((q, gw), BF16)],
        operands=[dy, y, xc, xc, xc, dt_raw, hprev_all, dt_bias, a_log, d_skip], side=side, name=name)


def _gated_norm_fwd(y, z, ng, *, name):
    t, d = y.shape
    tm = _tile(t, 256)
    gw = d // SSM_GROUPS

    def body(y_ref, z_ref, ng_ref, o_ref):
        for g in range(SSM_GROUPS):
            sl = slice(g * gw, (g + 1) * gw)
            zv = z_ref[:, sl].astype(F32)
            yg = y_ref[:, sl] * (zv * _sigmoid(zv))
            r = lax.rsqrt(jnp.mean(yg * yg, axis=-1, keepdims=True) + RMS_EPS)
            o_ref[:, sl] = (yg * r * ng_ref[:, sl]).astype(o_ref.dtype)

    row = pl.BlockSpec((tm, d), lambda i: (i, 0))
    return pl.pallas_call(body, out_shape=jax.ShapeDtypeStruct((t, d), BF16), grid=(t // tm,),
                          in_specs=[row, row, _const_spec((1, d))], out_specs=row, name=name, compiler_params=_params("parallel"))(y, z, ng)


def _gated_norm_bwd(dyn, y, z, ng, *, name):
    t, d = y.shape
    tm = _tile(t, 256)
    gw = d // SSM_GROUPS

    def body(dyn_ref, y_ref, z_ref, ng_ref, dy_ref, dz_ref, dng_ref):
        @pl.when(pl.program_id(0) == 0)
        def _():
            dng_ref[...] = jnp.zeros_like(dng_ref)
        for g in range(SSM_GROUPS):
            sl = slice(g * gw, (g + 1) * gw)
            zv = z_ref[:, sl].astype(F32)
            yv = y_ref[:, sl]
            sg = _sigmoid(zv)
            sz = zv * sg
            yg = yv * sz
            r = lax.rsqrt(jnp.mean(yg * yg, axis=-1, keepdims=True) + RMS_EPS)
            yhat = yg * r
            dn = dyn_ref[:, sl].astype(F32)
            dyg_n = dn * ng_ref[:, sl]
            dyg = r * (dyg_n - yhat * jnp.mean(dyg_n * yhat, axis=-1, keepdims=True))
            dy_ref[:, sl] = (dyg * sz).astype(dy_ref.dtype)
            dz_ref[:, sl] = (dyg * yv * (sg * (1.0 + zv * (1.0 - sg)))).astype(dz_ref.dtype)
            dng_ref[:, sl] += jnp.sum(dn * yhat, axis=0, keepdims=True)

    row = pl.BlockSpec((tm, d), lambda i: (i, 0))
    shp = jax.ShapeDtypeStruct((t, d), BF16)
    return pl.pallas_call(body, out_shape=(shp, shp, jax.ShapeDtypeStruct((1, d), F32)), grid=(t // tm,),
                          in_specs=[row, row, row, _const_spec((1, d))], out_specs=(row, row, _const_spec((1, d))),
                          name=name, compiler_params=_params("arbitrary"))(dyn, y, z, ng)


def _bucket_onehot():
    blk = CHUNK
    qi = jnp.arange(blk)[:, None]
    kj = jnp.arange(2 * blk)[None, :]
    dist = jnp.maximum(qi + blk - kj, 0)
    max_exact = REL_BUCKETS // 2
    d = jnp.maximum(dist, 1).astype(F32)
    large = max_exact + (jnp.log(d / max_exact) / math.log(REL_MAX_DISTANCE / max_exact) * (REL_BUCKETS - max_exact)).astype(jnp.int32)
    large = jnp.minimum(large, REL_BUCKETS - 1)
    bucket = jnp.where(dist < max_exact, dist, large).reshape(-1)
    return (bucket[None, :] == jnp.arange(REL_BUCKETS)[:, None]).astype(F32)


def _small_mm_hi(a, b, dims, *, name):
    def body(a_ref, b_ref, o_ref):
        o_ref[...] = _dot_hi(a_ref[...], b_ref[...], dims)
    n = b.shape[0] if dims == NT else b.shape[1]
    return pl.pallas_call(body, out_shape=jax.ShapeDtypeStruct((a.shape[0], n), F32), name=name)(a, b)


def _attn_band_mask(n, rep):
    blk = CHUNK
    ii = lax.broadcasted_iota(jnp.int32, (rep * blk, 2 * blk), 0) & (blk - 1)
    jj = lax.broadcasted_iota(jnp.int32, (rep * blk, 2 * blk), 1)
    dist = ii + blk - jj
    in_window = jnp.logical_and(dist >= 0, dist < blk)
    return jnp.logical_and(in_window, jnp.logical_or(jj >= blk, n > 0))


def _attn_fwd(q, kv, bias, sinks, bl, *, name):
    t, qd = q.shape
    blk, hd = CHUNK, ATTN_HEAD_DIM
    kvd = ATTN_KV_HEADS * hd
    rep = ATTN_Q_HEADS // ATTN_KV_HEADS
    nb = t // bl // blk
    scale = hd ** -0.5

    def body(q_ref, kp_ref, kc_ref, vp_ref, vc_ref, bias_ref, sink_ref, o_ref, lse_ref):
        n = pl.program_id(1)
        mask = _attn_band_mask(n, rep)
        lse = jnp.zeros((blk, LANES), F32)
        for kvh in range(ATTN_KV_HEADS):
            ks = slice(kvh * hd, (kvh + 1) * hd)
            heads = range(kvh * rep, (kvh + 1) * rep)
            qs = jnp.concatenate([q_ref[:, h * hd:(h + 1) * hd] for h in heads], axis=0)
            kk = jnp.concatenate([kp_ref[:, ks], kc_ref[:, ks]], axis=0)
            vv = jnp.concatenate([vp_ref[:, ks], vc_ref[:, ks]], axis=0)
            bias = bias_ref[kvh * rep:(kvh + 1) * rep].reshape(rep * blk, 2 * blk)
            s = jnp.where(mask, _dot(qs, kk, NT) * scale + bias, NEG_INF)
            sink = jnp.concatenate([jnp.broadcast_to(sink_ref[:, h:h + 1], (blk, 1)) for h in heads], axis=0)
            m = jnp.maximum(jnp.max(s, axis=1, keepdims=True), sink)
            p = jnp.exp(s - m)
            den = jnp.sum(p, axis=1, keepdims=True) + jnp.exp(sink - m)
            o = _dot((p * (1.0 / den)).astype(BF16), vv, NN)
            lse_s = m + jnp.log(den)
            for r, h in enumerate(heads):
                o_ref[:, h * hd:(h + 1) * hd] = o[r * blk:(r + 1) * blk].astype(o_ref.dtype)
                lse = lse + lse_s[r * blk:(r + 1) * blk] * _lane_onehot(h)
        lse_ref[...] = lse

    def cur(b, n):
        return b * nb + n

    def prev(b, n):
        return b * nb + jnp.maximum(n - 1, 0)
    return pl.pallas_call(
        body, out_shape=(jax.ShapeDtypeStruct((t, qd), BF16), jax.ShapeDtypeStruct((t, LANES), F32)), grid=(bl, nb),
        in_specs=[pl.BlockSpec((blk, qd), lambda b, n: (cur(b, n), 0)),
                  pl.BlockSpec((blk, kvd), lambda b, n: (prev(b, n), 0)), pl.BlockSpec((blk, kvd), lambda b, n: (cur(b, n), 0)),
                  pl.BlockSpec((blk, kvd), lambda b, n: (prev(b, n), 1)), pl.BlockSpec((blk, kvd), lambda b, n: (cur(b, n), 1)),
                  _const_spec(bias.shape), _const_spec((1, LANES))],
        out_specs=(pl.BlockSpec((blk, qd), lambda b, n: (cur(b, n), 0)), pl.BlockSpec((blk, LANES), lambda b, n: (cur(b, n), 0))),
        name=name, compiler_params=_params("parallel", "arbitrary"))(q, kv, kv, kv, kv, bias, sinks)


def _attn_bwd(do, q, kv, lse, bias, sinks, bl, *, name):
    t, qd = q.shape
    blk, hd = CHUNK, ATTN_HEAD_DIM
    kvd = ATTN_KV_HEADS * hd
    rep = ATTN_Q_HEADS // ATTN_KV_HEADS
    s_len = t // bl
    nb = s_len // blk
    scale = hd ** -0.5

    def body(do_ref, q_ref, kp_ref, kc_ref, vp_ref, vc_ref, lse_ref, bias_ref, sink_ref, dq_ref, dkv_ref, dbias_ref, dsink_ref):
        n = pl.program_id(1)

        @pl.when(jnp.logical_and(pl.program_id(0) == 0, n == 0))
        def _():
            dbias_ref[...] = jnp.zeros_like(dbias_ref)
            dsink_ref[...] = jnp.zeros_like(dsink_ref)

        mask = _attn_band_mask(n, rep)
        r_cur = pl.multiple_of(n * blk, blk)
        r_prev = pl.multiple_of(jnp.maximum(n - 1, 0) * blk, blk)
        dsink = jnp.zeros((1, LANES), F32)
        for kvh in range(ATTN_KV_HEADS):
            ks = slice(kvh * hd, (kvh + 1) * hd)
            heads = range(kvh * rep, (kvh + 1) * rep)
            qs = jnp.concatenate([q_ref[:, h * hd:(h + 1) * hd] for h in heads], axis=0)
            dos = jnp.concatenate([do_ref[:, h * hd:(h + 1) * hd] for h in heads], axis=0)
            kk = jnp.concatenate([kp_ref[:, ks], kc_ref[:, ks]], axis=0)
            vv = jnp.concatenate([vp_ref[:, ks], vc_ref[:, ks]], axis=0)
            bias = bias_ref[kvh * rep:(kvh + 1) * rep].reshape(rep * blk, 2 * blk)
            lse = jnp.concatenate([lse_ref[:, h:h + 1] for h in heads], axis=0)
            sink = jnp.concatenate([jnp.broadcast_to(sink_ref[:, h:h + 1], (blk, 1)) for h in heads], axis=0)
            p = jnp.exp(jnp.where(mask, _dot(qs, kk, NT) * scale + bias, NEG_INF) - lse)
            dp = _dot(dos, vv, NT)
            delta = jnp.sum(p * dp, axis=1, keepdims=True)
            ds = p * (dp - delta)
            dsink_rows = jnp.exp(sink - lse) * delta
            ds_b = ds.astype(BF16)
            dq_s = _dot(ds_b, kk, NN) * scale
            dkk = _dot(ds_b, qs, TN) * scale
            dvv = _dot(p.astype(BF16), dos, TN)
            for r, h in enumerate(heads):
                rows = slice(r * blk, (r + 1) * blk)
                dbias_ref[h] += ds[rows]
                dq_ref[:, h * hd:(h + 1) * hd] = dq_s[rows].astype(dq_ref.dtype)
                dsink = dsink - jnp.sum(dsink_rows[rows], axis=0, keepdims=True) * _lane_onehot(h)
            vs = slice(kvd + kvh * hd, kvd + (kvh + 1) * hd)
            dkv_ref[pl.ds(r_cur, blk), ks] = dkk[blk:]
            dkv_ref[pl.ds(r_cur, blk), vs] = dvv[blk:]

            @pl.when(n > 0)
            def _():
                dkv_ref[pl.ds(r_prev, blk), ks] += dkk[:blk]
                dkv_ref[pl.ds(r_prev, blk), vs] += dvv[:blk]
        dsink_ref[...] += dsink

    def cur(b, n):
        return b * nb + n

    def prev(b, n):
        return b * nb + jnp.maximum(n - 1, 0)
    qspec = pl.BlockSpec((blk, qd), lambda b, n: (cur(b, n), 0))
    return pl.pallas_call(
        body,
        out_shape=(jax.ShapeDtypeStruct((t, qd), BF16), jax.ShapeDtypeStruct((t, 2 * kvd), F32),
                   jax.ShapeDtypeStruct(bias.shape, F32), jax.ShapeDtypeStruct((1, LANES), F32)),
        grid=(bl, nb),
        in_specs=[qspec, qspec,
                  pl.BlockSpec((blk, kvd), lambda b, n: (prev(b, n), 0)), pl.BlockSpec((blk, kvd), lambda b, n: (cur(b, n), 0)),
                  pl.BlockSpec((blk, kvd), lambda b, n: (prev(b, n), 1)), pl.BlockSpec((blk, kvd), lambda b, n: (cur(b, n), 1)),
                  pl.BlockSpec((blk, LANES), lambda b, n: (cur(b, n), 0)), _const_spec(bias.shape), _const_spec((1, LANES))],
        out_specs=(qspec, pl.BlockSpec((s_len, 2 * kvd), lambda b, n: (b, 0)), _const_spec(bias.shape), _const_spec((1, LANES))),
        name=name, compiler_params=_params("arbitrary", "arbitrary"))(do, q, kv, kv, kv, kv, lse, bias, sinks)


def _merge_fwd(yn, o, gs, ga, w_ssm, w_attn, w_out, h_in, g_post, *, name):
    t, d = h_in.shape
    tm = _tile(t, 256)

    def body(yn_ref, o_ref, gs_ref, ga_ref, ws_ref, wa_ref, wo_ref, hin_ref, gp_ref, ys_ref, ya_ref, mg_ref, mix_ref, hout_ref):
        ys = _dot(yn_ref[...], ws_ref[...], NN)
        ya = _dot(o_ref[...], wa_ref[...], NN)
        merged = (_sigmoid(gs_ref[...].astype(F32)) * ys + _sigmoid(ga_ref[...].astype(F32)) * ya).astype(BF16)
        mix = _dot(merged, wo_ref[...], NN)
        ys_ref[...] = ys.astype(BF16)
        ya_ref[...] = ya.astype(BF16)
        mg_ref[...] = merged
        mix_ref[...] = mix
        hout_ref[...] = _rms_residual(mix, hin_ref[...], gp_ref[...], 1.0)

    def row(w):
        return pl.BlockSpec((tm, w), lambda i: (i, 0))
    bshape = jax.ShapeDtypeStruct((t, d), BF16)
    fshape = jax.ShapeDtypeStruct((t, d), F32)
    return pl.pallas_call(
        body, out_shape=(bshape, bshape, bshape, fshape, fshape), grid=(t // tm,),
        in_specs=[row(yn.shape[1]), row(o.shape[1]), row(d), row(d), _const_spec(w_ssm.shape), _const_spec(w_attn.shape),
                  _const_spec(w_out.shape), row(d), _const_spec((1, d))],
        out_specs=(row(d),) * 5, name=name, compiler_params=_params("parallel"))(yn, o, gs, ga, w_ssm, w_attn, w_out, h_in, g_post)


def _merge_bwd(dh, mix, g_post, gs, ga, ys, ya, w_ssm, w_attn, w_out, *, name):
    t, d = mix.shape
    tm = _tile(t, 256)
    d_ssm, d_attn = w_ssm.shape[0], w_attn.shape[0]

    def body(dh_ref, mix_ref, gp_ref, gs_ref, ga_ref, ys_ref, ya_ref, ws_ref, wa_ref, wo_ref,
             dmix_ref, dys_ref, dya_ref, dgs_ref, dga_ref, dyn_ref, do_ref, dgp_ref):
        @pl.when(pl.program_id(0) == 0)
        def _():
            dgp_ref[...] = jnp.zeros_like(dgp_ref)
        mv = mix_ref[...]
        dy = dh_ref[...]
        r = lax.rsqrt(jnp.mean(mv * mv, axis=-1, keepdims=True) + RMS_EPS)
        mhat = mv * r
        dyg = dy * gp_ref[...]
        dmix = (r * (dyg - mhat * jnp.mean(dyg * mhat, axis=-1, keepdims=True))).astype(BF16)
        dgp_ref[...] += jnp.sum(dy * mhat, axis=0, keepdims=True)
        dmix_ref[...] = dmix
        dmerged = _dot(dmix, wo_ref[...], NT)
        sgs = _sigmoid(gs_ref[...].astype(F32))
        sga = _sigmoid(ga_ref[...].astype(F32))
        dys = (dmerged * sgs).astype(BF16)
        dya = (dmerged * sga).astype(BF16)
        dys_ref[...] = dys
        dya_ref[...] = dya
        dgs_ref[...] = (dmerged * ys_ref[...].astype(F32) * sgs * (1.0 - sgs)).astype(BF16)
        dga_ref[...] = (dmerged * ya_ref[...].astype(F32) * sga * (1.0 - sga)).astype(BF16)
        dyn_ref[...] = _dot(dys, ws_ref[...], NT).astype(BF16)
        do_ref[...] = _dot(dya, wa_ref[...], NT).astype(BF16)

    def row(w):
        return pl.BlockSpec((tm, w), lambda i: (i, 0))

    def bshape(w):
        return jax.ShapeDtypeStruct((t, w), BF16)
    return pl.pallas_call(
        body, out_shape=(bshape(d),) * 5 + (bshape(d_ssm), bshape(d_attn), jax.ShapeDtypeStruct((1, d), F32)), grid=(t // tm,),
        in_specs=[row(d), row(d), _const_spec((1, d)), row(d), row(d), row(d), row(d),
                  _const_spec(w_ssm.shape), _const_spec(w_attn.shape), _const_spec(w_out.shape)],
        out_specs=(row(d),) * 5 + (row(d_ssm), row(d_attn), _const_spec((1, d))),
        name=name, compiler_params=_params("arbitrary"))(dh, mix, g_post, gs, ga, ys, ya, w_ssm, w_attn, w_out)


def _loss_grad(h, target, *, name):
    t, d = h.shape
    tm = _tile(t, 512)

    def body(h_ref, t_ref, dh_ref, loss_ref):
        @pl.when(pl.program_id(0) == 0)
        def _():
            loss_ref[...] = jnp.zeros_like(loss_ref)
        e = h_ref[...] - t_ref[...]
        dh_ref[...] = e * (1.0 / d)
        per_row = jnp.sum(e * e, axis=1, keepdims=True) * (1.0 / d)
        loss_ref[...] += 0.5 * jnp.sum(per_row, axis=0, keepdims=True)

    row = pl.BlockSpec((tm, d), lambda i: (i, 0))
    return pl.pallas_call(body, out_shape=(jax.ShapeDtypeStruct((t, d), F32), jax.ShapeDtypeStruct((1, LANES), F32)), grid=(t // tm,),
                          in_specs=[row, row], out_specs=(row, _const_spec((1, LANES))),
                          name=name, compiler_params=_params("arbitrary"))(h, target)


def _adamw(w, g, m, v, *, name):
    r, c = w.shape
    tm = _tile(r, 256)
    c1 = 1.0 - ADAM_B1 ** ADAM_STEP
    c2 = 1.0 - ADAM_B2 ** ADAM_STEP

    def body(w_ref, g_ref, m_ref, v_ref, d_ref, mo_ref, vo_ref):
        gv = g_ref[...]
        mn = ADAM_B1 * m_ref[...] + (1.0 - ADAM_B1) * gv
        vn = ADAM_B2 * v_ref[...] + (1.0 - ADAM_B2) * (gv * gv)
        mo_ref[...] = mn
        vo_ref[...] = vn
        d_ref[...] = -ADAM_LR * ((mn / c1) / (jnp.sqrt(vn / c2) + ADAM_EPS) + ADAM_WD * w_ref[...])

    blk = pl.BlockSpec((tm, c), lambda i: (i, 0))
    shp = jax.ShapeDtypeStruct((r, c), F32)
    return pl.pallas_call(body, out_shape=(shp, shp, shp), grid=(r // tm,), in_specs=[blk] * 4, out_specs=(blk,) * 3,
                          name=name, compiler_params=_params("parallel"))(w, g, m, v)


def _position():
    return lax.axis_index("x"), lax.axis_index("y"), lax.axis_index("c")


def _gather_exchange(shards):
    na = len(shards)

    def plan(ins, outs, sems):
        send_sems, recv_sems, local_sems = sems
        x, y, c = _position()
        me, sibling = (x, y, c), (x, y, 1 - c)
        chips = [(1 - x, y), (x, 1 - y), (1 - x, 1 - y)]

        def slot(a, pos):
            return outs[a].at[4 * pos[0] + 2 * pos[1] + pos[2]]

        def copy(a, k, block, to, src=None):
            return pltpu.make_async_remote_copy(
                src_ref=slot(a, block) if src is None else src, dst_ref=slot(a, block),
                send_sem=send_sems.at[a, k], recv_sem=recv_sems.at[a, k], device_id=to, device_id_type=MESH)

        mine = [pltpu.make_async_copy(ins[a], slot(a, me), local_sems.at[a]) for a in range(na)]
        first = []
        for a in range(na):
            first.append(copy(a, 0, me, sibling, src=ins[a]))
            first += [copy(a, 1 + j, me, (*chip, c), src=ins[a]) for j, chip in enumerate(chips)]
        return me, sibling, chips, copy, mine, first

    def start(ins, outs, sems):
        *_, mine, first = plan(ins, outs, sems)
        for cp in mine + first:
            cp.start()

    def finish(ins, outs, sems):
        me, sibling, chips, copy, mine, first = plan(ins, outs, sems)
        c = me[2]
        passed = []
        for a in range(na):
            for j, chip in enumerate(chips):
                copy(a, 1 + j, (*chip, c), me).wait_recv()
                fwd = copy(a, 4 + j, (*chip, c), sibling)
                fwd.start()
                passed.append(fwd)
        for a in range(na):
            copy(a, 0, sibling, me).wait_recv()
            for j, chip in enumerate(chips):
                copy(a, 4 + j, (*chip, 1 - c), me).wait_recv()
        for cp in first + passed:
            cp.wait_send()
        for cp in mine:
            cp.wait()

    return _Exchange(list(shards), [jax.ShapeDtypeStruct((N_DEV,) + s.shape, s.dtype) for s in shards],
                     [pltpu.SemaphoreType.DMA((na, 7)), pltpu.SemaphoreType.DMA((na, 7)), pltpu.SemaphoreType.DMA((na,))],
                     start, finish)


def _scatter_exchange(arrays):
    na = len(arrays)

    def copies(ins, outs, sems):
        send_sems, recv_sems = sems
        x, y, c = _position()
        out = []
        for a in range(na):
            for k in range(7):
                flip = k + 1
                peer = (x ^ (flip >> 2), y ^ ((flip >> 1) & 1), c ^ (flip & 1))
                peer_block = 4 * peer[0] + 2 * peer[1] + peer[2]
                out.append(pltpu.make_async_remote_copy(
                    src_ref=ins[a].at[peer_block], dst_ref=outs[a].at[k],
                    send_sem=send_sems.at[a, k], recv_sem=recv_sems.at[a, k], device_id=peer, device_id_type=MESH))
        return out

    def start(ins, outs, sems):
        for cp in copies(ins, outs, sems):
            cp.start()

    def finish(ins, outs, sems):
        for cp in copies(ins, outs, sems):
            cp.wait()

    return _Exchange(list(arrays), [jax.ShapeDtypeStruct((7,) + s.shape[1:], s.dtype) for s in arrays],
                     [pltpu.SemaphoreType.DMA((na, 7)), pltpu.SemaphoreType.DMA((na, 7))], start, finish)


def _exchange_alone(side, *, name):
    n_in = len(side.arrays)
    n_out = len(side.out_shape)

    def body(*refs):
        ins, outs, sems = refs[:n_in], refs[n_in:n_in + n_out], refs[n_in + n_out:]
        side.start(ins, outs, sems)
        side.finish(ins, outs, sems)

    hbm = pl.BlockSpec(memory_space=pl.ANY)
    return pl.pallas_call(body, out_shape=tuple(side.out_shape), in_specs=[hbm] * n_in, out_specs=tuple([hbm] * n_out),
                          scratch_shapes=list(side.scratch), name=name)(*side.arrays)


def _reduce_blocks(own, recv, *, name):
    r, c = own.shape
    tm = _tile(r, 256)

    def body(own_ref, recv_ref, o_ref):
        acc = own_ref[...]
        for k in range(7):
            acc = acc + recv_ref[k].astype(F32)
        o_ref[...] = acc

    return pl.pallas_call(
        body, out_shape=jax.ShapeDtypeStruct((r, c), F32), grid=(r // tm,),
        in_specs=[pl.BlockSpec((tm, c), lambda i: (i, 0)), pl.BlockSpec((7, tm, c), lambda i: (0, i, 0))],
        out_specs=pl.BlockSpec((tm, c), lambda i: (i, 0)), name=name, compiler_params=_params("parallel"))(own, recv)


def _all_reduce_small(vec, *, name):
    r, c = vec.shape

    def body(v_ref, o_ref, buf, send_sems, recv_sems):
        x, y, c_ = _position()
        me = 4 * x + 2 * y + c_
        buf[me] = v_ref[...]
        copies = []
        for k in range(7):
            flip = k + 1
            peer = (x ^ (flip >> 2), y ^ ((flip >> 1) & 1), c_ ^ (flip & 1))
            cp = pltpu.make_async_remote_copy(
                src_ref=v_ref, dst_ref=buf.at[me], send_sem=send_sems.at[k], recv_sem=recv_sems.at[k],
                device_id=peer, device_id_type=MESH)
            cp.start()
            copies.append(cp)
        for cp in copies:
            cp.wait()
        acc = buf[0]
        for d in range(1, N_DEV):
            acc = acc + buf[d]
        o_ref[...] = acc

    vm = pl.BlockSpec(memory_space=pltpu.VMEM)
    return pl.pallas_call(
        body, out_shape=jax.ShapeDtypeStruct((r, c), F32), in_specs=[vm], out_specs=vm,
        scratch_shapes=[pltpu.VMEM((N_DEV, r, c), F32), pltpu.SemaphoreType.DMA((7,)), pltpu.SemaphoreType.DMA((7,))],
        name=name)(vec)


def _pad_lanes(v, width=LANES):
    return jnp.pad(v, ((0, 0), (0, width - v.shape[1])))


def kernel(x, ffn1_pre_g, ffn1_w_gate, ffn1_w_up, ffn1_w_down, ffn1_post_g, mix_pre_g, w_in, conv_w, conv_b, dt_bias, a_log, d_skip, ssm_norm_g, w_ssm_proj, attn_sinks, rel_bias_table, w_attn_proj, w_out, mix_post_g, ffn2_pre_g, ffn2_w_gate, ffn2_w_up, ffn2_w_down, ffn2_post_g, loss_target, m_ffn1_pre_g, m_ffn1_w_gate, m_ffn1_w_up, m_ffn1_w_down, m_ffn1_post_g, m_mix_pre_g, m_w_in, m_conv_w, m_conv_b, m_dt_bias, m_a_log, m_d_skip, m_ssm_norm_g, m_w_ssm_proj, m_attn_sinks, m_rel_bias_table, m_w_attn_proj, m_w_out, m_mix_post_g, m_ffn2_pre_g, m_ffn2_w_gate, m_ffn2_w_up, m_ffn2_w_down, m_ffn2_post_g, v_ffn1_pre_g, v_ffn1_w_gate, v_ffn1_w_up, v_ffn1_w_down, v_ffn1_post_g, v_mix_pre_g, v_w_in, v_conv_w, v_conv_b, v_dt_bias, v_a_log, v_d_skip, v_ssm_norm_g, v_w_ssm_proj, v_attn_sinks, v_rel_bias_table, v_w_attn_proj, v_w_out, v_mix_post_g, v_ffn2_pre_g, v_ffn2_w_gate, v_ffn2_w_up, v_ffn2_w_down, v_ffn2_post_g):
    args = dict(locals())
    weight_names = ['ffn1_pre_g', 'ffn1_w_gate', 'ffn1_w_up', 'ffn1_w_down', 'ffn1_post_g', 'mix_pre_g', 'w_in', 'conv_w', 'conv_b',
                    'dt_bias', 'a_log', 'd_skip', 'ssm_norm_g', 'w_ssm_proj', 'attn_sinks', 'rel_bias_table', 'w_attn_proj', 'w_out',
                    'mix_post_g', 'ffn2_pre_g', 'ffn2_w_gate', 'ffn2_w_up', 'ffn2_w_down', 'ffn2_post_g']
    col_sharded = ('ffn1_w_gate', 'ffn1_w_up', 'w_in', 'ffn2_w_gate', 'ffn2_w_up')
    row_sharded = ('ffn1_w_down', 'w_ssm_proj', 'w_attn_proj', 'w_out', 'ffn2_w_down')
    big = col_sharded + row_sharded

    bl, s_len, d = x.shape
    t = bl * s_len
    d_inner = ssm_norm_g.shape[1]
    n_heads = dt_bias.shape[1]
    gn = SSM_GROUPS * SSM_STATE
    conv_dim = d_inner + 2 * gn
    q_dim = ATTN_Q_HEADS * ATTN_HEAD_DIM
    kv_dim = ATTN_KV_HEADS * ATTN_HEAD_DIM

    def local_2d(name, a):
        a = a[0]
        return a.T if name in col_sharded else a

    ffn1_names = ('ffn1_w_gate', 'ffn1_w_up', 'ffn1_w_down')
    ffn2_names = ('ffn2_w_gate', 'ffn2_w_up', 'ffn2_w_down')
    mixer_names = ('w_ssm_proj', 'w_attn_proj', 'w_out')

    def shard(n):
        return local_2d(n, args[n]).astype(BF16)

    def rows(g):
        return g.reshape(N_DEV * g.shape[1], g.shape[2])

    first_names = ffn1_names[:2]
    full = {n: rows(g) for n, g in zip(first_names, _exchange_alone(_gather_exchange([shard(n) for n in first_names]), name="gather_ffn1"))}

    x2 = x.reshape(t, d)
    tgt2 = loss_target.reshape(t, d)

    h1, saved1, got_in, got_mixer = _ffn_forward(
        x2, ffn1_pre_g, full['ffn1_w_gate'], full['ffn1_w_up'], lambda got: rows(got[2]), ffn1_post_g, "ffn1",
        side_up=_gather_exchange([shard('w_in'), conv_w[0], shard('ffn1_w_down')]),
        side_down=_gather_exchange([shard(n) for n in mixer_names]))
    full['ffn1_w_down'] = rows(got_in[2])
    full.update({n: rows(g) for n, g in zip(mixer_names, got_mixer)})
    conv_w_full = jnp.transpose(got_in[1], (1, 0, 2)).reshape(SSM_CONV, conv_dim)

    win3 = got_in[0]
    blk_rows = win3.shape[1]
    in_cols = N_DEV * blk_rows
    dt_lo = 2 * d + d_inner + conv_dim
    n_main = in_cols - n_heads

    def orig_rows(lo, hi):
        out = []
        for j in range(N_DEV):
            a, b = max(lo, j * blk_rows), min(hi, (j + 1) * blk_rows)
            if a < b:
                out.append(win3[j, a - j * blk_rows:b - j * blk_rows])
        return out
    win_p = jnp.concatenate(orig_rows(0, dt_lo) + orig_rows(dt_lo + n_heads, in_cols) + orig_rows(dt_lo, dt_lo + n_heads)
                            + [jnp.zeros((LANES - n_heads, d), BF16)], axis=0)
    off = {'gs': 0, 'ga': d, 'z': 2 * d, 'xbc': 2 * d + d_inner, 'q': dt_lo, 'kv': dt_lo + q_dim, 'dt': n_main}

    u = _rms_fwd(h1, mix_pre_g, name="mix_prenorm")

    def proj(key, width, out_dtype=BF16):
        tn = _tile(math.gcd(width, off[key]) if off[key] else width, 1024, LANES)
        assert off[key] % tn == 0 and width % tn == 0
        return _mm_nt(u, win_p, row_off=off[key] // tn, n_out=width, out_dtype=out_dtype, tn=tn, name=f"proj_{key}")

    gs, ga, z, xbc = proj('gs', d), proj('ga', d), proj('z', d_inner), proj('xbc', conv_dim)
    q, kv = proj('q', q_dim), proj('kv', 2 * kv_dim)
    dt_raw = proj('dt', LANES, F32)

    dtb_p, alog_p, dsk_p, sinks_p = _pad_lanes(dt_bias), _pad_lanes(a_log), _pad_lanes(d_skip), _pad_lanes(attn_sinks)
    xc = _conv_fwd(xbc, conv_w_full, conv_b, bl, name="conv_fwd")
    (y, hprev), got_ffn2 = _ssd_fwd(xc, dt_raw, dtb_p, alog_p, dsk_p, bl, n_heads,
                                    side=_gather_exchange([shard(n) for n in ffn2_names]), name="ssd_fwd")
    full.update({n: rows(g) for n, g in zip(ffn2_names, got_ffn2)})
    yn = _gated_norm_fwd(y, z, ssm_norm_g, name="gated_norm_fwd")

    onehot = _bucket_onehot()
    bias = _small_mm_hi(rel_bias_table.T, onehot, NN, name="rel_bias").reshape(ATTN_Q_HEADS, CHUNK, 2 * CHUNK)
    o, lse = _attn_fwd(q, kv, bias, sinks_p, bl, name="attn_fwd")

    ys, ya, merged, mix, h2 = _merge_fwd(yn, o, gs, ga, full['w_ssm_proj'], full['w_attn_proj'], full['w_out'], h1, mix_post_g,
                                         name="merge_fwd")

    h3, saved2, _, _ = _ffn_forward(h2, ffn2_pre_g, full['ffn2_w_gate'], full['ffn2_w_up'], full['ffn2_w_down'], ffn2_post_g, "ffn2")

    dh3, loss_vec = _loss_grad(h3, tgt2, name="loss")
    loss = lax.psum(loss_vec[0, 0], ("x", "y", "c"))

    grads, own, wire, received = {}, {}, {}, {}
    dh2, grads['ffn2_pre_g'], grads['ffn2_post_g'], g32, g16, _ = _ffn_backward(
        dh3, saved2, ffn2_pre_g, full['ffn2_w_gate'], full['ffn2_w_up'], full['ffn2_w_down'], ffn2_post_g, "ffn2")
    own.update(zip(ffn2_names, map(_stack8, g32)))
    wire.update(zip(ffn2_names, map(_stack8, g16)))

    dmix, dys, dya, dgs, dga, dyn, do, grads['mix_post_g'] = _merge_bwd(
        dh2, mix, mix_post_g, gs, ga, ys, ya, full['w_ssm_proj'], full['w_attn_proj'], full['w_out'], name="merge_bwd")
    for n, (lhs, rhs) in zip(mixer_names, ((yn, dys), (o, dya), (merged, dmix))):
        g32_, g16_, _ = _mm_tn([lhs], rhs, name=f"d{n}")
        own[n], wire[n] = _stack8(g32_), _stack8(g16_)

    dq, dkv, dbias, dsinks = _attn_bwd(do, q, kv, lse, bias, sinks_p, bl, name="attn_bwd")
    d_table = _small_mm_hi(onehot, dbias.reshape(ATTN_Q_HEADS, -1), NT, name="rel_bias_bwd")

    dy, dz, grads['ssm_norm_g'] = _gated_norm_bwd(dyn, y, z, ssm_norm_g, name="gated_norm_bwd")

    first_group = ffn2_names + mixer_names
    (dxc, ddt_raw, ddtb, dalog, ddsk), got = _ssd_bwd(dy, y, xc, dt_raw, hprev, dtb_p, alog_p, dsk_p, bl, n_heads,
                                                      side=_scatter_exchange([wire[n] for n in first_group]), name="ssd_bwd")
    received.update(zip(first_group, got))
    dxbc, dconv_w8, grads['conv_b'] = _conv_bwd(dxc, xbc, conv_w_full, conv_b, bl, name="conv_bwd")

    wide32, wide16, _ = _mm_tn([dgs, dga, dz, dxbc, dq], u, name="dw_in")
    kv32, kv16, _ = _mm_tn([dkv], u, name="dw_in_kv")
    dt32, dt16, _ = _mm_tn([ddt_raw], u, name="dw_in_dt")

    def original_order(wide, kv_part, dt_part):
        return [wide[:dt_lo], dt_part[:n_heads], wide[dt_lo:], kv_part]

    def stack_blocks(pieces):
        starts = np.cumsum([0] + [p_.shape[0] for p_ in pieces])
        blocks = []
        for j in range(N_DEV):
            lo, hi = j * blk_rows, (j + 1) * blk_rows
            parts = [p_[max(lo, s0) - s0:min(hi, s0 + p_.shape[0]) - s0] for p_, s0 in zip(pieces, starts)
                     if max(lo, s0) < min(hi, s0 + p_.shape[0])]
            blocks.append(parts[0] if len(parts) == 1 else jnp.concatenate(parts, axis=0))
        return jnp.stack(blocks)

    me = 4 * lax.axis_index("x") + 2 * lax.axis_index("y") + lax.axis_index("c")
    wire['w_in'] = stack_blocks(original_order(wide16, kv16, dt16))
    own_w_in = lax.dynamic_slice_in_dim(jnp.concatenate(original_order(wide32, kv32, dt32), axis=0), me * blk_rows, blk_rows)
    own['conv_w'] = jnp.transpose(dconv_w8[:SSM_CONV].reshape(SSM_CONV, N_DEV, conv_dim // N_DEV), (1, 0, 2))

    segs = [(g_, 0, off[k_]) for g_, k_ in zip([dgs, dga, dz, dxbc, dq, dkv, ddt_raw], ('gs', 'ga', 'z', 'xbc', 'q', 'kv', 'dt'))]
    dh1, grads['mix_pre_g'], got = _mm_nn_rmsbwd(segs, [win_p], h1, mix_pre_g, dh2,
                                                 side=_scatter_exchange([wire['w_in'], own['conv_w']]), name="mix_du")
    received.update(zip(('w_in', 'conv_w'), got))

    dx2, grads['ffn1_pre_g'], grads['ffn1_post_g'], g32, _, got = _ffn_backward(
        dh1, saved1, ffn1_pre_g, full['ffn1_w_gate'], full['ffn1_w_up'], full['ffn1_w_down'], ffn1_post_g, "ffn1", chain=True)
    own.update(zip(ffn1_names, map(_stack8, g32)))
    received.update(zip(ffn1_names, got))

    def own_block(a):
        return lax.dynamic_index_in_dim(a, me, 0, keepdims=False)
    reduced = {n: _reduce_blocks(own_w_in if n == 'w_in' else own_block(own[n]), received[n], name=f"reduce_{n}") for n in big}
    conv_sum = _reduce_blocks(own_block(own['conv_w']), received['conv_w'], name="reduce_conv_w")

    grads['dt_bias'], grads['a_log'], grads['d_skip'] = ddtb[:, :n_heads], dalog[:, :n_heads], ddsk[:, :n_heads]
    grads['attn_sinks'] = dsinks[:, :ATTN_Q_HEADS]
    grads['rel_bias_table'] = d_table
    small = [n for n in weight_names if n not in big and n != 'conv_w']
    flat = jnp.concatenate([grads[n].reshape(-1) for n in small])
    n_small = flat.shape[0]
    n_rows = -(-n_small // (8 * LANES)) * 8
    flat = jnp.pad(flat, (0, n_rows * LANES - n_small)).reshape(n_rows, LANES)
    summed = _all_reduce_small(flat, name="allreduce_small").reshape(-1)
    pos = 0
    for n in small:
        size = grads[n].size
        grads[n] = summed[pos:pos + size].reshape(args[n].shape)
        pos += size

    out_g, out_d, out_m, out_v = {}, {}, {}, {}
    for n in big:
        g_nat = reduced[n].T if n in col_sharded else reduced[n]
        dlt, mn, vn = _adamw(args[n][0], g_nat, args['m_' + n][0], args['v_' + n][0], name=f"adamw_{n}")
        out_g[n], out_d[n], out_m[n], out_v[n] = g_nat[None], dlt[None], mn[None], vn[None]

    def pack(prefix):
        vals = [(grads[n] if prefix == 'g' else args[prefix + n]).reshape(-1) for n in small]
        vals.append((conv_sum if prefix == 'g' else args[prefix + 'conv_w']).reshape(-1))
        flat_ = jnp.concatenate(vals)
        rows_ = -(-flat_.shape[0] // (8 * LANES)) * 8
        return jnp.pad(flat_, (0, rows_ * LANES - flat_.shape[0])).reshape(rows_, LANES)

    g_small = pack('g')
    d_small, m_small, v_small = _adamw(pack(''), g_small, pack('m_'), pack('v_'), name="adamw_small")
    pos = 0
    for n in small + ['conv_w']:
        shape = args[n].shape
        size = int(np.prod(shape))
        for dst, src in ((out_g, g_small), (out_d, d_small), (out_m, m_small), (out_v, v_small)):
            dst[n] = src.reshape(-1)[pos:pos + size].reshape(shape)
        pos += size

    grad_x = dx2.reshape(bl, s_len, d)
    return (loss, grad_x, *[out_g[n] for n in weight_names], *[out_d[n] for n in weight_names],
            *[out_m[n] for n in weight_names], *[out_v[n] for n in weight_names])
```

```python
import functools
import math

import numpy as np
import jax
import jax.numpy as jnp
from jax import lax
from jax.experimental import pallas as pl
from jax.experimental.pallas import tpu as pltpu

F32 = jnp.float32
BF16 = jnp.bfloat16
MESH = pl.DeviceIdType.MESH
N_DEV = 8

SSM_HEAD_DIM = 64
SSM_GROUPS = 4
SSM_STATE = 128
SSM_CONV = 4
CHUNK = 128
ATTN_HEAD_DIM = 64
ATTN_Q_HEADS = 16
ATTN_KV_HEADS = 4
REL_BUCKETS = 32
REL_MAX_DISTANCE = 128
RMS_EPS = 1e-6
FFN_RESIDUAL_WEIGHT = 0.5
ADAM_LR, ADAM_B1, ADAM_B2, ADAM_EPS, ADAM_WD, ADAM_STEP = 0.001, 0.9, 0.999, 1e-08, 0.01, 10

LANES = 128
VMEM_LIMIT_BYTES = 56 * 1024 * 1024
FFN_COL_TILE = 1408

NEG_INF = float("-inf")


def _params(*sem):
    return pltpu.CompilerParams(dimension_semantics=sem, vmem_limit_bytes=VMEM_LIMIT_BYTES)


def _tile(n, pref, mult=8):
    if n <= pref:
        return n
    t = (pref // mult) * mult
    while t >= mult:
        if n % t == 0:
            return t
        t -= mult
    return n


def _sigmoid(x):
    return 1.0 / (1.0 + jnp.exp(-x))


def _dot(a, b, dims):
    return lax.dot_general(a, b, (dims, ((), ())), preferred_element_type=F32)


NN = ((1,), (0,))
NT = ((1,), (1,))
TN = ((0,), (0,))


def _dot_hi(a, b, dims=NN):
    return lax.dot_general(a, b, (dims, ((), ())), preferred_element_type=F32, precision=lax.Precision.HIGHEST)


def _const_spec(shape):
    nd = len(shape)
    return pl.BlockSpec(shape, lambda *_: (0,) * nd)


def _resident_spec(shape):
    nd = len(shape)
    return pl.BlockSpec(shape, lambda *_: (0,) * nd, pipeline_mode=pl.Buffered(1))


class _Exchange:
    def __init__(self, arrays, out_shape, scratch, start, finish):
        self.arrays, self.out_shape, self.scratch, self.start, self.finish = arrays, out_shape, scratch, start, finish


def _hosted_call(body, *, grid, in_specs, out_specs, out_shape, scratch_shapes, operands, side, name):
    in_specs, out_specs, out_shape, scratch_shapes = list(in_specs), list(out_specs), list(out_shape), list(scratch_shapes)
    sem = ("arbitrary",) * len(grid)
    if side is None:
        outs = pl.pallas_call(body, out_shape=tuple(out_shape), grid=grid, in_specs=in_specs, out_specs=tuple(out_specs),
                              scratch_shapes=scratch_shapes, name=name, compiler_params=_params(*sem))(*operands)
        return tuple(outs), ()
    n_in, n_out, n_scr = len(in_specs), len(out_shape), len(scratch_shapes)
    s_in, s_out = len(side.arrays), len(side.out_shape)

    def wrapped(*refs):
        refs = list(refs)
        main_in, side_in = refs[:n_in], refs[n_in:n_in + s_in]
        o0 = n_in + s_in
        main_out, side_out = refs[o0:o0 + n_out], refs[o0 + n_out:o0 + n_out + s_out]
        c0 = o0 + n_out + s_out
        main_scr, side_scr = refs[c0:c0 + n_scr], refs[c0 + n_scr:]
        ids = [pl.program_id(ax) for ax in range(len(grid))]
        first = functools.reduce(jnp.logical_and, [i == 0 for i in ids])
        last = functools.reduce(jnp.logical_and, [i == g - 1 for i, g in zip(ids, grid)])

        @pl.when(first)
        def _():
            side.start(side_in, side_out, side_scr)

        body(*main_in, *main_out, *main_scr)

        @pl.when(last)
        def _():
            side.finish(side_in, side_out, side_scr)

    hbm = pl.BlockSpec(memory_space=pl.ANY)
    outs = pl.pallas_call(
        wrapped, out_shape=tuple(out_shape + list(side.out_shape)), grid=grid,
        in_specs=in_specs + [hbm] * s_in, out_specs=tuple(out_specs + [hbm] * s_out),
        scratch_shapes=scratch_shapes + list(side.scratch), name=name, compiler_params=_params(*sem))(*operands, *side.arrays)
    return tuple(outs[:n_out]), tuple(outs[n_out:])


def _proj_all(h, g, w, segs, *, name):
    t, d = h.shape
    tm = _tile(t, 512)

    def body(h_ref, g_ref, w_ref, u_ref, *o_refs):
        hv = h_ref[...]
        r = lax.rsqrt(jnp.mean(hv * hv, axis=-1, keepdims=True) + RMS_EPS)
        uv = (hv * r * g_ref[...]).astype(BF16)
        u_ref[...] = uv
        for (row0, width, _), o_ref in zip(segs, o_refs):
            for c0, c1 in _col_chunks(width, 8 * LANES):
                o_ref[:, c0:c1] = _dot(uv, w_ref[row0 + c0:row0 + c1, :], NT).astype(o_ref.dtype)

    row = pl.BlockSpec((tm, d), lambda i: (i, 0))
    outs = pl.pallas_call(
        body, out_shape=(jax.ShapeDtypeStruct((t, d), BF16),) + tuple(jax.ShapeDtypeStruct((t, wd_), dt_) for _, wd_, dt_ in segs),
        grid=(t // tm,), in_specs=[row, _const_spec((1, d)), _resident_spec(w.shape)],
        out_specs=(row,) + tuple(pl.BlockSpec((tm, wd_), lambda i: (i, 0)) for _, wd_, _ in segs),
        name=name, compiler_params=_params("parallel"))(h, g, w)
    return outs[0], outs[1:]


def _mm_tn(a_list, b, *, tm=1408, tk=1024, side=None, name):
    t, n = b.shape
    tk = _tile(t, tk)
    nk = t // tk
    widths = [a.shape[1] for a in a_list]
    tm = _tile(math.gcd(*widths), tm, LANES)
    assert all(w % tm == 0 for w in widths)
    starts = np.cumsum([0] + [w // tm for w in widths])
    nseg = len(a_list)

    def a_spec(s):
        lo, hi = int(starts[s]), int(starts[s + 1])

        def idx(i, k):
            active = jnp.logical_and(i >= lo, i < hi)
            return (jnp.where(active, k, 0), jnp.clip(i - lo, 0, hi - lo - 1))
        return pl.BlockSpec((tk, tm), idx)

    def body(*refs):
        a_refs, b_ref, o_ref, o16_ref, acc = refs[:nseg], refs[nseg], refs[nseg + 1], refs[nseg + 2], refs[nseg + 3]
        i, k = pl.program_id(0), pl.program_id(1)

        @pl.when(k == 0)
        def _():
            acc[...] = jnp.zeros_like(acc)

        bv = b_ref[...].astype(BF16)
        for s in range(nseg):
            lo, hi = int(starts[s]), int(starts[s + 1])

            @pl.when(jnp.logical_and(i >= lo, i < hi))
            def _(s=s):
                acc[...] += _dot(a_refs[s][...].astype(BF16), bv, TN)

        @pl.when(k == nk - 1)
        def _():
            o_ref[...] = acc[...]
            o16_ref[...] = acc[...].astype(BF16)

    rows = int(starts[-1]) * tm
    o_spec = pl.BlockSpec((tm, n), lambda i, k: (i, 0))
    (o32, o16), got = _hosted_call(
        body, out_shape=[jax.ShapeDtypeStruct((rows, n), F32), jax.ShapeDtypeStruct((rows, n), BF16)], grid=(int(starts[-1]), nk),
        in_specs=[a_spec(s) for s in range(nseg)] + [pl.BlockSpec((tk, n), lambda i, k: (k, 0))],
        out_specs=[o_spec, o_spec], scratch_shapes=[pltpu.VMEM((tm, n), F32)], operands=list(a_list) + [b], side=side, name=name)
    return o32, o16, got


def _mm_nn_rmsbwd(segs, weights, x, g, dres, *, tm=256, side=None, name):
    t, d = x.shape
    tm = _tile(t, tm)
    nseg, nw = len(segs), len(weights)

    def body(*refs):
        a_refs, w_refs = refs[:nseg], refs[nseg:nseg + nw]
        x_ref, g_ref, dres_ref, dx_ref, dg_ref = refs[nseg + nw:]

        @pl.when(pl.program_id(0) == 0)
        def _():
            dg_ref[...] = jnp.zeros_like(dg_ref)

        dn = None
        for s, (a, w_idx, row0) in enumerate(segs):
            part = _dot(a_refs[s][...].astype(BF16), w_refs[w_idx][row0:row0 + a.shape[1], :], NN)
            dn = part if dn is None else dn + part
        xv = x_ref[...]
        r = lax.rsqrt(jnp.mean(xv * xv, axis=-1, keepdims=True) + RMS_EPS)
        xhat = xv * r
        dyg = dn * g_ref[...]
        dx_ref[...] = dres_ref[...] + r * (dyg - xhat * jnp.mean(dyg * xhat, axis=-1, keepdims=True))
        dg_ref[...] += jnp.sum(dn * xhat, axis=0, keepdims=True)

    row = pl.BlockSpec((tm, d), lambda i: (i, 0))
    in_specs = [pl.BlockSpec((tm, a.shape[1]), lambda i: (i, 0)) for a, _, _ in segs]
    in_specs += [_resident_spec(w.shape) for w in weights] + [row, _const_spec((1, d)), row]
    (dx, dg), extra = _hosted_call(
        body, grid=(t // tm,), in_specs=in_specs, out_specs=[row, _const_spec((1, d))],
        out_shape=[jax.ShapeDtypeStruct((t, d), F32), jax.ShapeDtypeStruct((1, d), F32)], scratch_shapes=[],
        operands=[a for a, _, _ in segs] + list(weights) + [x, g, dres], side=side, name=name)
    return dx, dg, extra


def _col_chunks(width, chunk=4 * LANES):
    return [(c0, min(c0 + chunk, width)) for c0 in range(0, width, chunk)]


def _ffn_up(h, g_pre, wgt, wut, *, side=None, name):
    t, d = h.shape
    f = wgt.shape[0]
    tm, tn = _tile(t, 512), _tile(f, FFN_COL_TILE, LANES)

    def body(h_ref, gpre_ref, wg_ref, wu_ref, n_ref, g_ref, u_ref, hid_ref, n_s):
        @pl.when(pl.program_id(1) == 0)
        def _():
            hv = h_ref[...]
            r = lax.rsqrt(jnp.mean(hv * hv, axis=-1, keepdims=True) + RMS_EPS)
            nv = (hv * r * gpre_ref[...]).astype(BF16)
            n_s[...] = nv
            n_ref[...] = nv

        nv = n_s[...]
        gv = _dot(nv, wg_ref[...], NT)
        uv = _dot(nv, wu_ref[...], NT)
        g_ref[...] = gv.astype(BF16)
        u_ref[...] = uv.astype(BF16)
        hid_ref[...] = (gv * _sigmoid(gv) * uv).astype(BF16)

    row = pl.BlockSpec((tm, d), lambda i, j: (i, 0))
    w_spec = pl.BlockSpec((tn, d), lambda i, j: (j, 0))
    o_spec = pl.BlockSpec((tm, tn), lambda i, j: (i, j))
    shp = jax.ShapeDtypeStruct((t, f), BF16)
    return _hosted_call(body, grid=(t // tm, f // tn), in_specs=[row, _const_spec((1, d)), w_spec, w_spec],
                        out_specs=[row, o_spec, o_spec, o_spec], out_shape=[jax.ShapeDtypeStruct((t, d), BF16), shp, shp, shp],
                        scratch_shapes=[pltpu.VMEM((tm, d), BF16)], operands=[h, g_pre, wgt, wut], side=side, name=name)


def _rms_residual(acc, h, gp, weight):
    r = lax.rsqrt(jnp.mean(acc * acc, axis=-1, keepdims=True) + RMS_EPS)
    return h + weight * (acc * r * gp)


def _ffn_down(hid, wd, h_in, gp, *, target=None, side=None, name):
    t, f = hid.shape
    d = wd.shape[1]
    tm = _tile(t, 256)
    row = pl.BlockSpec((tm, d), lambda i: (i, 0))
    shp = jax.ShapeDtypeStruct((t, d), F32)
    in_specs = [pl.BlockSpec((tm, f), lambda i: (i, 0)), _resident_spec((f, d)), row, _const_spec((1, d))]

    if target is None:
        def body(hid_ref, wd_ref, hin_ref, gp_ref, f_ref, hout_ref):
            acc = _dot(hid_ref[...], wd_ref[...], NN)
            f_ref[...] = acc
            hout_ref[...] = _rms_residual(acc, hin_ref[...], gp_ref[...], FFN_RESIDUAL_WEIGHT)

        return _hosted_call(body, grid=(t // tm,), in_specs=in_specs, out_specs=[row, row], out_shape=[shp, shp], scratch_shapes=[],
                            operands=[hid, wd, h_in, gp], side=side, name=name)

    def body_loss(hid_ref, wd_ref, hin_ref, gp_ref, tgt_ref, f_ref, dh_ref, loss_ref):
        @pl.when(pl.program_id(0) == 0)
        def _():
            loss_ref[...] = jnp.zeros_like(loss_ref)
        acc = _dot(hid_ref[...], wd_ref[...], NN)
        f_ref[...] = acc
        e = _rms_residual(acc, hin_ref[...], gp_ref[...], FFN_RESIDUAL_WEIGHT) - tgt_ref[...]
        dh_ref[...] = e * (1.0 / d)
        per_row = jnp.sum(e * e, axis=1, keepdims=True) * (1.0 / d)
        loss_ref[...] += 0.5 * jnp.sum(per_row, axis=0, keepdims=True)

    return _hosted_call(body_loss, grid=(t // tm,), in_specs=in_specs + [row], out_specs=[row, row, _const_spec((1, LANES))],
                        out_shape=[shp, shp, jax.ShapeDtypeStruct((1, LANES), F32)], scratch_shapes=[],
                        operands=[hid, wd, h_in, gp, target], side=side, name=name)


def _ffn_dhid(dh_out, f_pre, gp, wd, g, u, *, name):
    t, d = f_pre.shape
    f = wd.shape[0]
    tm, tn = _tile(t, 512), _tile(f, FFN_COL_TILE, LANES)

    def body(dho_ref, f_ref, gp_ref, wd_ref, g_ref, u_ref, df_ref, dg_ref, du_ref, dgp_ref, df_s):
        @pl.when(jnp.logical_and(pl.program_id(0) == 0, pl.program_id(1) == 0))
        def _():
            dgp_ref[...] = jnp.zeros_like(dgp_ref)

        @pl.when(pl.program_id(1) == 0)
        def _():
            fv = f_ref[...]
            dy = FFN_RESIDUAL_WEIGHT * dho_ref[...]
            r = lax.rsqrt(jnp.mean(fv * fv, axis=-1, keepdims=True) + RMS_EPS)
            fhat = fv * r
            dyg = dy * gp_ref[...]
            dfv = (r * (dyg - fhat * jnp.mean(dyg * fhat, axis=-1, keepdims=True))).astype(BF16)
            df_s[...] = dfv
            df_ref[...] = dfv
            dgp_ref[...] += jnp.sum(dy * fhat, axis=0, keepdims=True)

        dh = _dot(df_s[...], wd_ref[...], NT)
        gv = g_ref[...].astype(F32)
        uv = u_ref[...].astype(F32)
        sg = _sigmoid(gv)
        silu = gv * sg
        dg_ref[...] = (dh * uv * (sg + silu * (1.0 - sg))).astype(BF16)
        du_ref[...] = (dh * silu).astype(BF16)

    row = pl.BlockSpec((tm, d), lambda i, j: (i, 0))
    o_spec = pl.BlockSpec((tm, tn), lambda i, j: (i, j))
    vec = pl.BlockSpec((1, d), lambda i, j: (0, 0))
    shp = jax.ShapeDtypeStruct((t, f), BF16)
    return pl.pallas_call(
        body, out_shape=(jax.ShapeDtypeStruct((t, d), BF16), shp, shp, jax.ShapeDtypeStruct((1, d), F32)), grid=(t // tm, f // tn),
        in_specs=[row, row, vec, pl.BlockSpec((tn, d), lambda i, j: (j, 0)), o_spec, o_spec],
        out_specs=(row, o_spec, o_spec, vec), scratch_shapes=[pltpu.VMEM((tm, d), BF16)],
        name=name, compiler_params=_params("arbitrary", "arbitrary"))(dh_out, f_pre, gp, wd, g, u)


def _ffn_forward(h_in, g_pre, wgt, wut, wd, g_post, tag, side_up=None, side_down=None, target=None):
    (n, g, u, hid), got_up = _ffn_up(h_in, g_pre, wgt, wut, side=side_up, name=f"{tag}_up")
    wd = wd(got_up) if callable(wd) else wd
    outs, got_down = _ffn_down(hid, wd, h_in, g_post, target=target, side=side_down, name=f"{tag}_down")
    return outs[1:], (h_in, n, g, u, hid, outs[0]), got_up, got_down


def _stack8(g):
    return g.reshape(N_DEV, g.shape[0] // N_DEV, g.shape[1])


def _ffn_backward(dh_out, saved, g_pre, wgt, wut, wd, g_post, tag, chain=False):
    h_in, n, g, u, hid, f = saved

    def side_of(grad16):
        return _scatter_exchange([_stack8(grad16)]) if chain else None

    df, dgate, dup, dg_post = _ffn_dhid(dh_out, f, g_post, wd, g, u, name=f"{tag}_dhid")
    d_wd, d_wd16, _ = _mm_tn([hid], df, name=f"{tag}_dwd")
    d_wgt, d_wgt16, got_wd = _mm_tn([dgate], n, side=side_of(d_wd16), name=f"{tag}_dwg")
    d_wut, d_wut16, got_wg = _mm_tn([dup], n, side=side_of(d_wgt16), name=f"{tag}_dwu")
    dh_in, dg_pre, got_wu = _mm_nn_rmsbwd([(dgate, 0, 0), (dup, 1, 0)], [wgt, wut], h_in, g_pre, dh_out, side=side_of(d_wut16),
                                          name=f"{tag}_dn")
    received = (got_wg[0], got_wu[0], got_wd[0]) if chain else None
    return dh_in, dg_pre, dg_post, (d_wgt, d_wut, d_wd), (d_wgt16, d_wut16, d_wd16), received


CONV_ROWS = 128
HALO = 8


def _taps(w_ref):
    return [w_ref[k:k + 1, :] for k in range(SSM_CONV)]


def _conv_chunk(x_ref, xs, r0, taps, bias):
    xs[HALO + r0:HALO + r0 + CONV_ROWS, :] = x_ref[r0:r0 + CONV_ROWS, :].astype(F32)
    shifted = [xs[HALO + r0 - k:HALO + r0 - k + CONV_ROWS, :] for k in range(SSM_CONV)]
    pre = bias + shifted[0] * taps[SSM_CONV - 1]
    for k in range(1, SSM_CONV):
        pre = pre + shifted[k] * taps[SSM_CONV - 1 - k]
    return shifted, pre


def _fold_rows(a):
    return functools.reduce(jnp.add, [a[i:i + 8] for i in range(0, a.shape[0], 8)])


def _conv_fwd(xbc, conv_w, conv_b, bl, *, name):
    t, c = xbc.shape
    s = t // bl
    tc = LANES
    assert s % CONV_ROWS == 0

    def body(x_ref, w_ref, b_ref, o_ref, xs):
        taps, bias = _taps(w_ref), b_ref[...]
        xs[0:HALO, :] = jnp.zeros((HALO, tc), F32)
        for r0 in range(0, s, CONV_ROWS):
            _, pre = _conv_chunk(x_ref, xs, r0, taps, bias)
            o_ref[r0:r0 + CONV_ROWS, :] = (pre * _sigmoid(pre)).astype(o_ref.dtype)

    blk = pl.BlockSpec((s, tc), lambda b, j: (b, j))
    return pl.pallas_call(body, out_shape=jax.ShapeDtypeStruct((t, c), BF16), grid=(bl, c // tc),
                          in_specs=[blk, pl.BlockSpec((SSM_CONV, tc), lambda b, j: (0, j)), pl.BlockSpec((1, tc), lambda b, j: (0, j))],
                          out_specs=blk, scratch_shapes=[pltpu.VMEM((HALO + s, tc), F32)],
                          name=name, compiler_params=_params("parallel", "arbitrary"))(xbc, conv_w, conv_b)


def _conv_bwd(dxc, xbc, conv_w, conv_b, bl, *, name):
    t, c = xbc.shape
    s = t // bl
    tc = LANES

    def body(dy_ref, x_ref, w_ref, b_ref, dx_ref, dw_ref, db_ref, xs, dpre_s):
        @pl.when(pl.program_id(1) == 0)
        def _():
            dw_ref[...] = jnp.zeros_like(dw_ref)
            db_ref[...] = jnp.zeros_like(db_ref)

        taps, bias = _taps(w_ref), b_ref[...]
        zero8 = jnp.zeros((HALO, tc), F32)
        xs[0:HALO, :] = zero8
        dpre_s[s:s + HALO, :] = zero8
        sums = [zero8] * (SSM_CONV + 1)
        for r0 in range(0, s, CONV_ROWS):
            shifted, pre = _conv_chunk(x_ref, xs, r0, taps, bias)
            sg = _sigmoid(pre)
            dpre = dy_ref[r0:r0 + CONV_ROWS, :].astype(F32) * (sg * (1.0 + pre * (1.0 - sg)))
            dpre_s[r0:r0 + CONV_ROWS, :] = dpre
            sums = [acc + _fold_rows(dpre * sh) for acc, sh in zip(sums[:-1], shifted)] + [sums[-1] + _fold_rows(dpre)]
        for k in range(SSM_CONV):
            dw_ref[SSM_CONV - 1 - k:SSM_CONV - k, :] += jnp.sum(sums[k], axis=0, keepdims=True)
        db_ref[...] += jnp.sum(sums[-1], axis=0, keepdims=True)
        for r0 in range(0, s, CONV_ROWS):
            dx = dpre_s[r0:r0 + CONV_ROWS, :] * taps[SSM_CONV - 1]
            for k in range(1, SSM_CONV):
                dx = dx + dpre_s[r0 + k:r0 + k + CONV_ROWS, :] * taps[SSM_CONV - 1 - k]
            dx_ref[r0:r0 + CONV_ROWS, :] = dx.astype(dx_ref.dtype)

    blk = pl.BlockSpec((s, tc), lambda j, b: (b, j))
    return pl.pallas_call(
        body, out_shape=(jax.ShapeDtypeStruct((t, c), BF16), jax.ShapeDtypeStruct((8, c), F32), jax.ShapeDtypeStruct((1, c), F32)),
        grid=(c // tc, bl),
        in_specs=[blk, blk, pl.BlockSpec((SSM_CONV, tc), lambda j, b: (0, j)), pl.BlockSpec((1, tc), lambda j, b: (0, j))],
        out_specs=(blk, pl.BlockSpec((8, tc), lambda j, b: (0, j)), pl.BlockSpec((1, tc), lambda j, b: (0, j))),
        scratch_shapes=[pltpu.VMEM((HALO + s, tc), F32), pltpu.VMEM((s + HALO, tc), F32)],
        name=name, compiler_params=_params("parallel", "arbitrary"))(dxc, xbc, conv_w, conv_b)


def _softplus(x):
    return jnp.maximum(x, 0.0) + jnp.log1p(jnp.exp(-jnp.abs(x)))


def _hilo_dot(v, m_b, dims=NN):
    hi = v.astype(BF16)
    lo = (v - hi.astype(F32)).astype(BF16)
    return _dot(hi, m_b, dims) + _dot(lo, m_b, dims)


def _ssd_chunk_common(dtraw_ref, dtb_ref, alog_ref, dsk_ref, d_inner):
    q, p = CHUNK, SSM_HEAD_DIM
    shift = p.bit_length() - 1
    assert 1 << shift == p
    dt = _softplus(dtraw_ref[...] + dtb_ref[...])
    a = -jnp.exp(alog_ref[...])
    ii = lax.broadcasted_iota(jnp.int32, (q, q), 0)
    jj = lax.broadcasted_iota(jnp.int32, (q, q), 1)
    causal = ii >= jj
    tril = jnp.where(causal, 1.0, 0.0).astype(F32)
    triu = jnp.where(ii <= jj, 1.0, 0.0).astype(F32)
    a_cs = _dot_hi(tril, dt * a)
    a_cs_t = a_cs.T
    a_last = a_cs[q - 1:q, :]
    e_col = jnp.exp(a_cs)
    dec_end = jnp.exp(a_last - a_cs)
    head_of_col = lax.shift_right_logical(lax.broadcasted_iota(jnp.int32, (LANES, d_inner), 1), shift)
    spread = (lax.broadcasted_iota(jnp.int32, (LANES, d_inner), 0) == head_of_col).astype(BF16)
    wide = _hilo_dot(jnp.concatenate([dt, e_col, dec_end, jnp.broadcast_to(dsk_ref[...], (8, LANES))], axis=0), spread)
    return dict(dt=dt, a=a, a_cs=a_cs, a_cs_t=a_cs_t, a_last=a_last, dec_end=dec_end, causal=causal, triu=triu,
                dt_e=wide[:q], e_e=wide[q:2 * q], dec_e=wide[2 * q:3 * q], dsk_e=wide[3 * q:3 * q + 1])


def _fill_block_diag(bd_ref, src_ref, hpg, col0=0):
    q, p = CHUNK, SSM_HEAD_DIM
    for hh in range(hpg):
        bd_ref[hh * q:(hh + 1) * q, hh * p:(hh + 1) * p] = src_ref[:, col0 + hh * p:col0 + (hh + 1) * p]


def _lane_onehot(h):
    return (lax.broadcasted_iota(jnp.int32, (1, LANES), 1) == h).astype(F32)


def _ssd_fwd(xc, dt_raw, dt_bias, a_log, d_skip, bl, n_heads, *, side=None, name):
    t = xc.shape[0]
    q, p, nst, grp = CHUNK, SSM_HEAD_DIM, SSM_STATE, SSM_GROUPS
    d_inner = n_heads * p
    hpg = n_heads // grp
    gw = hpg * p
    nc = t // bl // q
    assert d_inner % (grp * nst) == 0 and nst == LANES

    def body(xs_ref, b_ref, c_ref, dtraw_ref, dtb_ref, alog_ref, dsk_ref, y_ref, hprev_ref, state, m_all, x_bd, xdt_s):
        @pl.when(jnp.logical_and(pl.program_id(0) == 0, pl.program_id(1) == 0))
        def _():
            x_bd[...] = jnp.zeros_like(x_bd)

        @pl.when(pl.program_id(1) == 0)
        def _():
            state[...] = jnp.zeros_like(state)

        cm = _ssd_chunk_common(dtraw_ref, dtb_ref, alog_ref, dsk_ref, d_inner)
        for g in range(grp):
            cols = slice(g * gw, (g + 1) * gw)
            bg = b_ref[:, g * nst:(g + 1) * nst]
            cg = c_ref[:, g * nst:(g + 1) * nst]
            scores = _dot(cg, bg, NT)
            for hh in range(hpg):
                h = g * hpg + hh
                seg = cm['a_cs'][:, h:h + 1] - cm['a_cs_t'][h:h + 1, :]
                m_all[:, hh * q:(hh + 1) * q] = (scores * jnp.exp(jnp.where(cm['causal'], seg, NEG_INF))).astype(BF16)
            xs = xs_ref[:, cols].astype(F32)
            xdt = xs * cm['dt_e'][:, cols]
            xdt_s[...] = xdt.astype(BF16)
            _fill_block_diag(x_bd, xdt_s, hpg)
            hprev = state[g]
            hprev_ref[g] = hprev
            y = _dot(m_all[...], x_bd[...], NN) + cm['e_e'][:, cols] * _dot(cg, hprev.astype(BF16), NT)
            y_ref[:, cols] = y + cm['dsk_e'][:, cols] * xs
            st = _dot((xdt * cm['dec_e'][:, cols]).astype(BF16), bg, TN)
            for hh in range(hpg):
                h = g * hpg + hh
                rows = slice(hh * p, (hh + 1) * p)
                state[g, rows, :] = jnp.exp(cm['a_last'][:, h:h + 1]) * hprev[rows] + st[rows]

    gn = grp * nst

    def rowmap(b, c):
        return b * nc + c
    vec = pl.BlockSpec((1, LANES), lambda b, c: (0, 0))
    return _hosted_call(
        body,
        out_shape=[jax.ShapeDtypeStruct((t, d_inner), F32), jax.ShapeDtypeStruct((t // q, grp, gw, nst), F32)],
        grid=(bl, nc),
        in_specs=[pl.BlockSpec((q, d_inner), lambda b, c: (rowmap(b, c), 0)),
                  pl.BlockSpec((q, gn), lambda b, c: (rowmap(b, c), d_inner // gn)),
                  pl.BlockSpec((q, gn), lambda b, c: (rowmap(b, c), d_inner // gn + 1)),
                  pl.BlockSpec((q, LANES), lambda b, c: (rowmap(b, c), 0)), vec, vec, vec],
        out_specs=[pl.BlockSpec((q, d_inner), lambda b, c: (rowmap(b, c), 0)),
                   pl.BlockSpec((None, grp, gw, nst), lambda b, c: (rowmap(b, c), 0, 0, 0))],
        scratch_shapes=[pltpu.VMEM((grp, gw, nst), F32), pltpu.VMEM((q, hpg * q), BF16), pltpu.VMEM((hpg * q, gw), BF16),
                        pltpu.VMEM((q, gw), BF16)],
        operands=[xc, xc, xc, dt_raw, dt_bias, a_log, d_skip], side=side, name=name)


def _ssd_bwd(dy, y, xc, dt_raw, hprev_all, dt_bias, a_log, d_skip, bl, n_heads, *, side=None, name):
    t, c_dim = xc.shape
    q, p, nst, grp = CHUNK, SSM_HEAD_DIM, SSM_STATE, SSM_GROUPS
    d_inner = n_heads * p
    hpg = n_heads // grp
    gw = hpg * p
    nc = t // bl // q
    gn = grp * nst
    shift = p.bit_length() - 1

    def body(dy_ref, y_ref, xs_ref, b_ref, c_ref, dtraw_ref, hprev_ref, dtb_ref, alog_ref, dsk_ref,
             dxc_ref, ddtraw_ref, ddtb_ref, dalog_ref, ddsk_ref, dstate, mt_all, x_bd, dy_bd, xdt_s):
        @pl.when(jnp.logical_and(pl.program_id(0) == 0, pl.program_id(1) == 0))
        def _():
            ddtb_ref[...] = jnp.zeros_like(ddtb_ref)
            dalog_ref[...] = jnp.zeros_like(dalog_ref)
            ddsk_ref[...] = jnp.zeros_like(ddsk_ref)
            x_bd[...] = jnp.zeros_like(x_bd)
            dy_bd[...] = jnp.zeros_like(dy_bd)

        @pl.when(pl.program_id(1) == 0)
        def _():
            dstate[...] = jnp.zeros_like(dstate)

        cm = _ssd_chunk_common(dtraw_ref, dtb_ref, alog_ref, dsk_ref, d_inner)
        causal = cm['causal']
        upper = cm['triu'] > 0.5
        seg_row = lax.shift_right_logical(lax.broadcasted_iota(jnp.int32, (gw, LANES), 0), shift)
        seg_lane = lax.broadcasted_iota(jnp.int32, (gw, LANES), 1)
        sums = jnp.zeros((5 * q, LANES), F32)
        state_dot = jnp.zeros((1, LANES), F32)
        for g in range(grp):
            cols = slice(g * gw, (g + 1) * gw)
            seg_sum = (seg_row + g * hpg == seg_lane).astype(BF16)
            bg = b_ref[:, g * nst:(g + 1) * nst]
            cg = c_ref[:, g * nst:(g + 1) * nst]
            scores_t = _dot(bg, cg, NT)
            xs = xs_ref[:, cols].astype(F32)
            xdt = xs * cm['dt_e'][:, cols]
            xdt_s[...] = xdt.astype(BF16)
            _fill_block_diag(x_bd, xdt_s, hpg)
            _fill_block_diag(dy_bd, dy_ref, hpg, g * gw)
            dy_b = dy_ref[:, cols]
            dyf = dy_b.astype(F32)
            dm_all = _dot(dy_b, x_bd[...], NT)
            dscores = jnp.zeros((q, q), F32)
            for hh in range(hpg):
                h = g * hpg + hh
                blk = slice(hh * q, (hh + 1) * q)
                seg = cm['a_cs'][:, h:h + 1] - cm['a_cs_t'][h:h + 1, :]
                mt_all[:, blk] = (scores_t * jnp.exp(jnp.where(upper, -seg, NEG_INF))).astype(BF16)
                dscores = dscores + dm_all[:, blk] * jnp.exp(jnp.where(causal, seg, NEG_INF))
            hprev = hprev_ref[g]
            hprev_b = hprev.astype(BF16)
            dhn = dstate[g]
            dhn_b = dhn.astype(BF16)
            e_e, dec_e = cm['e_e'][:, cols], cm['dec_e'][:, cols]
            y_scan = y_ref[:, cols] - cm['dsk_e'][:, cols] * xs
            dye_b = (dyf * e_e).astype(BF16)
            dcg = _dot(dye_b, hprev_b, NN)
            dhp = _dot(dye_b, cg, TN)
            bdh = _dot(bg, dhn_b, NT)
            dbg = _dot((xdt * dec_e).astype(BF16), dhn_b, NN)
            dx_diag = _dot(mt_all[...], dy_bd[...], NN)
            dx = dec_e * bdh + dx_diag
            ds_b = dscores.astype(BF16)
            dcg = dcg + _dot(ds_b, bg, NN)
            dbg = dbg + _dot(ds_b, cg, TN)
            x_rounded = xdt_s[...].astype(F32)
            sums = sums + _hilo_dot(jnp.concatenate([dyf * y_scan, xdt * bdh, x_rounded * dx_diag, dx * xs, dyf * xs], axis=0), seg_sum)
            state_dot = state_dot + jnp.sum(_hilo_dot(dhn * hprev, seg_sum, TN), axis=0, keepdims=True)
            dxc_ref[:, cols] = (dx * cm['dt_e'][:, cols] + cm['dsk_e'][:, cols] * dyf).astype(dxc_ref.dtype)
            dxc_ref[:, d_inner + g * nst:d_inner + (g + 1) * nst] = dbg.astype(dxc_ref.dtype)
            dxc_ref[:, d_inner + gn + g * nst:d_inner + gn + (g + 1) * nst] = dcg.astype(dxc_ref.dtype)
            for hh in range(hpg):
                h = g * hpg + hh
                rows = slice(hh * p, (hh + 1) * p)
                dstate[g, rows, :] = jnp.exp(cm['a_last'][:, h:h + 1]) * dhn[rows] + dhp[rows]
        s_y, s_end, s_diag, s_dt, s_skip = (sums[k * q:(k + 1) * q] for k in range(5))
        dt, a, dec_end = cm['dt'], cm['a'], cm['dec_end']
        last_row = (lax.broadcasted_iota(jnp.int32, (q, 1), 0) == q - 1).astype(F32)
        da_last = jnp.sum(dec_end * s_end, axis=0, keepdims=True) + jnp.exp(cm['a_last']) * state_dot
        da = s_y - dec_end * s_end - s_diag + last_row * da_last
        ddta = _dot_hi(cm['triu'], da)
        ddt = s_dt + ddta * a
        d_a = jnp.sum(ddta * dt, axis=0, keepdims=True)
        ddt_raw = ddt * _sigmoid(dtraw_ref[...] + dtb_ref[...])
        ddtraw_ref[...] = ddt_raw
        ddtb_ref[...] += jnp.sum(ddt_raw, axis=0, keepdims=True)
        dalog_ref[...] += d_a * a
        ddsk_ref[...] += jnp.sum(s_skip, axis=0, keepdims=True)

    def rowmap(b, c):
        return b * nc + (nc - 1 - c)
    vec = pl.BlockSpec((1, LANES), lambda b, c: (0, 0))
    vec_shape = jax.ShapeDtypeStruct((1, LANES), F32)
    return _hosted_call(
        body,
        out_shape=[jax.ShapeDtypeStruct((t, c_dim), BF16), jax.ShapeDtypeStruct((t, LANES), F32), vec_shape, vec_shape, vec_shape],
        grid=(bl, nc),
        in_specs=[pl.BlockSpec((q, d_inner), lambda b, c: (rowmap(b, c), 0)),
                  pl.BlockSpec((q, d_inner), lambda b, c: (rowmap(b, c), 0)),
                  pl.BlockSpec((q, d_inner), lambda b, c: (rowmap(b, c), 0)),
                  pl.BlockSpec((q, gn), lambda b, c: (rowmap(b, c), d_inner // gn)),
                  pl.BlockSpec((q, gn), lambda b, c: (rowmap(b, c), d_inner // gn + 1)),
                  pl.BlockSpec((q, LANES), lambda b, c: (rowmap(b, c), 0)),
                  pl.BlockSpec((None, grp, gw, nst), lambda b, c: (rowmap(b, c), 0, 0, 0)), vec, vec, vec],
        out_specs=[pl.BlockSpec((q, c_dim), lambda b, c: (rowmap(b, c), 0)),
                   pl.BlockSpec((q, LANES), lambda b, c: (rowmap(b, c), 0)), vec, vec, vec],
        scratch_shapes=[pltpu.VMEM((grp, gw, nst), F32), pltpu.VMEM((q, hpg * q), BF16),
                        pltpu.VMEM((hpg * q, gw), BF16), pltpu.VMEM((hpg * q, gw), BF16), pltpu.VMEM((q, gw), BF16)],
        operands=[dy, y, xc, xc, xc, dt_raw, hprev_all, dt_bias, a_log, d_skip], side=side, name=name)


def _gated_norm_fwd(y, z, ng, *, name):
    t, d = y.shape
    tm = _tile(t, 256)
    gw = d // SSM_GROUPS

    def body(y_ref, z_ref, ng_ref, o_ref):
        for g in range(SSM_GROUPS):
            sl = slice(g * gw, (g + 1) * gw)
            zv = z_ref[:, sl].astype(F32)
            yg = y_ref[:, sl] * (zv * _sigmoid(zv))
            r = lax.rsqrt(jnp.mean(yg * yg, axis=-1, keepdims=True) + RMS_EPS)
            o_ref[:, sl] = (yg * r * ng_ref[:, sl]).astype(o_ref.dtype)

    row = pl.BlockSpec((tm, d), lambda i: (i, 0))
    return pl.pallas_call(body, out_shape=jax.ShapeDtypeStruct((t, d), BF16), grid=(t // tm,),
                          in_specs=[row, row, _const_spec((1, d))], out_specs=row, name=name, compiler_params=_params("parallel"))(y, z, ng)


def _gated_norm_bwd(dyn, y, z, ng, *, name):
    t, d = y.shape
    tm = _tile(t, 256)
    gw = d // SSM_GROUPS

    def body(dyn_ref, y_ref, z_ref, ng_ref, dy_ref, dz_ref, dng_ref):
        @pl.when(pl.program_id(0) == 0)
        def _():
            dng_ref[...] = jnp.zeros_like(dng_ref)
        for g in range(SSM_GROUPS):
            sl = slice(g * gw, (g + 1) * gw)
            zv = z_ref[:, sl].astype(F32)
            yv = y_ref[:, sl]
            sg = _sigmoid(zv)
            sz = zv * sg
            yg = yv * sz
            r = lax.rsqrt(jnp.mean(yg * yg, axis=-1, keepdims=True) + RMS_EPS)
            yhat = yg * r
            dn = dyn_ref[:, sl].astype(F32)
            dyg_n = dn * ng_ref[:, sl]
            dyg = r * (dyg_n - yhat * jnp.mean(dyg_n * yhat, axis=-1, keepdims=True))
            dy_ref[:, sl] = (dyg * sz).astype(dy_ref.dtype)
            dz_ref[:, sl] = (dyg * yv * (sg * (1.0 + zv * (1.0 - sg)))).astype(dz_ref.dtype)
            dng_ref[:, sl] += jnp.sum(dn * yhat, axis=0, keepdims=True)

    row = pl.BlockSpec((tm, d), lambda i: (i, 0))
    shp = jax.ShapeDtypeStruct((t, d), BF16)
    return pl.pallas_call(body, out_shape=(shp, shp, jax.ShapeDtypeStruct((1, d), F32)), grid=(t // tm,),
                          in_specs=[row, row, row, _const_spec((1, d))], out_specs=(row, row, _const_spec((1, d))),
                          name=name, compiler_params=_params("arbitrary"))(dyn, y, z, ng)


def _bucket_onehot():
    blk = CHUNK
    qi = jnp.arange(blk)[:, None]
    kj = jnp.arange(2 * blk)[None, :]
    dist = jnp.maximum(qi + blk - kj, 0)
    max_exact = REL_BUCKETS // 2
    d = jnp.maximum(dist, 1).astype(F32)
    large = max_exact + (jnp.log(d / max_exact) / math.log(REL_MAX_DISTANCE / max_exact) * (REL_BUCKETS - max_exact)).astype(jnp.int32)
    large = jnp.minimum(large, REL_BUCKETS - 1)
    bucket = jnp.where(dist < max_exact, dist, large).reshape(-1)
    return (bucket[None, :] == jnp.arange(REL_BUCKETS)[:, None]).astype(F32)


def _small_mm_hi(a, b, dims, *, name):
    def body(a_ref, b_ref, o_ref):
        o_ref[...] = _dot_hi(a_ref[...], b_ref[...], dims)
    n = b.shape[0] if dims == NT else b.shape[1]
    return pl.pallas_call(body, out_shape=jax.ShapeDtypeStruct((a.shape[0], n), F32), name=name)(a, b)


def _attn_band_mask(n, rep):
    blk = CHUNK
    ii = lax.broadcasted_iota(jnp.int32, (rep * blk, 2 * blk), 0) & (blk - 1)
    jj = lax.broadcasted_iota(jnp.int32, (rep * blk, 2 * blk), 1)
    dist = ii + blk - jj
    in_window = jnp.logical_and(dist >= 0, dist < blk)
    return jnp.logical_and(in_window, jnp.logical_or(jj >= blk, n > 0))


def _attn_fwd(q, kv, bias, sinks, bl, *, name):
    t, qd = q.shape
    blk, hd = CHUNK, ATTN_HEAD_DIM
    kvd = ATTN_KV_HEADS * hd
    rep = ATTN_Q_HEADS // ATTN_KV_HEADS
    nb = t // bl // blk
    scale = hd ** -0.5

    def body(q_ref, kp_ref, kc_ref, vp_ref, vc_ref, bias_ref, sink_ref, o_ref, lse_ref):
        n = pl.program_id(1)
        mask = _attn_band_mask(n, rep)
        lse = jnp.zeros((blk, LANES), F32)
        for kvh in range(ATTN_KV_HEADS):
            ks = slice(kvh * hd, (kvh + 1) * hd)
            heads = range(kvh * rep, (kvh + 1) * rep)
            qs = jnp.concatenate([q_ref[:, h * hd:(h + 1) * hd] for h in heads], axis=0)
            kk = jnp.concatenate([kp_ref[:, ks], kc_ref[:, ks]], axis=0)
            vv = jnp.concatenate([vp_ref[:, ks], vc_ref[:, ks]], axis=0)
            bias = bias_ref[kvh * rep:(kvh + 1) * rep].reshape(rep * blk, 2 * blk)
            s = jnp.where(mask, _dot(qs, kk, NT) * scale + bias, NEG_INF)
            sink = jnp.concatenate([jnp.broadcast_to(sink_ref[:, h:h + 1], (blk, 1)) for h in heads], axis=0)
            m = jnp.maximum(jnp.max(s, axis=1, keepdims=True), sink)
            p = jnp.exp(s - m)
            den = jnp.sum(p, axis=1, keepdims=True) + jnp.exp(sink - m)
            o = _dot((p * (1.0 / den)).astype(BF16), vv, NN)
            lse_s = m + jnp.log(den)
            for r, h in enumerate(heads):
                o_ref[:, h * hd:(h + 1) * hd] = o[r * blk:(r + 1) * blk].astype(o_ref.dtype)
                lse = lse + lse_s[r * blk:(r + 1) * blk] * _lane_onehot(h)
        lse_ref[...] = lse

    def cur(b, n):
        return b * nb + n

    def prev(b, n):
        return b * nb + jnp.maximum(n - 1, 0)
    return pl.pallas_call(
        body, out_shape=(jax.ShapeDtypeStruct((t, qd), BF16), jax.ShapeDtypeStruct((t, LANES), F32)), grid=(bl, nb),
        in_specs=[pl.BlockSpec((blk, qd), lambda b, n: (cur(b, n), 0)),
                  pl.BlockSpec((blk, kvd), lambda b, n: (prev(b, n), 0)), pl.BlockSpec((blk, kvd), lambda b, n: (cur(b, n), 0)),
                  pl.BlockSpec((blk, kvd), lambda b, n: (prev(b, n), 1)), pl.BlockSpec((blk, kvd), lambda b, n: (cur(b, n), 1)),
                  _const_spec(bias.shape), _const_spec((1, LANES))],
        out_specs=(pl.BlockSpec((blk, qd), lambda b, n: (cur(b, n), 0)), pl.BlockSpec((blk, LANES), lambda b, n: (cur(b, n), 0))),
        name=name, compiler_params=_params("parallel", "arbitrary"))(q, kv, kv, kv, kv, bias, sinks)


def _attn_bwd(do, q, kv, lse, bias, sinks, bl, *, name):
    t, qd = q.shape
    blk, hd = CHUNK, ATTN_HEAD_DIM
    kvd = ATTN_KV_HEADS * hd
    rep = ATTN_Q_HEADS // ATTN_KV_HEADS
    s_len = t // bl
    nb = s_len // blk
    scale = hd ** -0.5

    def body(do_ref, q_ref, kp_ref, kc_ref, vp_ref, vc_ref, lse_ref, bias_ref, sink_ref, dq_ref, dkv_ref, dbias_ref, dsink_ref):
        n = pl.program_id(1)

        @pl.when(jnp.logical_and(pl.program_id(0) == 0, n == 0))
        def _():
            dbias_ref[...] = jnp.zeros_like(dbias_ref)
            dsink_ref[...] = jnp.zeros_like(dsink_ref)

        mask = _attn_band_mask(n, rep)
        r_cur = pl.multiple_of(n * blk, blk)
        r_prev = pl.multiple_of(jnp.maximum(n - 1, 0) * blk, blk)
        dsink = jnp.zeros((1, LANES), F32)
        for kvh in range(ATTN_KV_HEADS):
            ks = slice(kvh * hd, (kvh + 1) * hd)
            heads = range(kvh * rep, (kvh + 1) * rep)
            qs = jnp.concatenate([q_ref[:, h * hd:(h + 1) * hd] for h in heads], axis=0)
            dos = jnp.concatenate([do_ref[:, h * hd:(h + 1) * hd] for h in heads], axis=0)
            kk = jnp.concatenate([kp_ref[:, ks], kc_ref[:, ks]], axis=0)
            vv = jnp.concatenate([vp_ref[:, ks], vc_ref[:, ks]], axis=0)
            bias = bias_ref[kvh * rep:(kvh + 1) * rep].reshape(rep * blk, 2 * blk)
            lse = jnp.concatenate([lse_ref[:, h:h + 1] for h in heads], axis=0)
            sink = jnp.concatenate([jnp.broadcast_to(sink_ref[:, h:h + 1], (blk, 1)) for h in heads], axis=0)
            p = jnp.exp(jnp.where(mask, _dot(qs, kk, NT) * scale + bias, NEG_INF) - lse)
            dp = _dot(dos, vv, NT)
            delta = jnp.sum(p * dp, axis=1, keepdims=True)
            ds = p * (dp - delta)
            dsink_rows = jnp.exp(sink - lse) * delta
            ds_b = ds.astype(BF16)
            dq_s = _dot(ds_b, kk, NN) * scale
            dkk = _dot(ds_b, qs, TN) * scale
            dvv = _dot(p.astype(BF16), dos, TN)
            for r, h in enumerate(heads):
                rows = slice(r * blk, (r + 1) * blk)
                dbias_ref[h] += ds[rows]
                dq_ref[:, h * hd:(h + 1) * hd] = dq_s[rows].astype(dq_ref.dtype)
                dsink = dsink - jnp.sum(dsink_rows[rows], axis=0, keepdims=True) * _lane_onehot(h)
            vs = slice(kvd + kvh * hd, kvd + (kvh + 1) * hd)
            dkv_ref[pl.ds(r_cur, blk), ks] = dkk[blk:]
            dkv_ref[pl.ds(r_cur, blk), vs] = dvv[blk:]

            @pl.when(n > 0)
            def _():
                dkv_ref[pl.ds(r_prev, blk), ks] += dkk[:blk]
                dkv_ref[pl.ds(r_prev, blk), vs] += dvv[:blk]
        dsink_ref[...] += dsink

    def cur(b, n):
        return b * nb + n

    def prev(b, n):
        return b * nb + jnp.maximum(n - 1, 0)
    qspec = pl.BlockSpec((blk, qd), lambda b, n: (cur(b, n), 0))
    return pl.pallas_call(
        body,
        out_shape=(jax.ShapeDtypeStruct((t, qd), BF16), jax.ShapeDtypeStruct((t, 2 * kvd), F32),
                   jax.ShapeDtypeStruct(bias.shape, F32), jax.ShapeDtypeStruct((1, LANES), F32)),
        grid=(bl, nb),
        in_specs=[qspec, qspec,
                  pl.BlockSpec((blk, kvd), lambda b, n: (prev(b, n), 0)), pl.BlockSpec((blk, kvd), lambda b, n: (cur(b, n), 0)),
                  pl.BlockSpec((blk, kvd), lambda b, n: (prev(b, n), 1)), pl.BlockSpec((blk, kvd), lambda b, n: (cur(b, n), 1)),
                  pl.BlockSpec((blk, LANES), lambda b, n: (cur(b, n), 0)), _const_spec(bias.shape), _const_spec((1, LANES))],
        out_specs=(qspec, pl.BlockSpec((s_len, 2 * kvd), lambda b, n: (b, 0)), _const_spec(bias.shape), _const_spec((1, LANES))),
        name=name, compiler_params=_params("arbitrary", "arbitrary"))(do, q, kv, kv, kv, kv, lse, bias, sinks)


def _merge_fwd(yn, o, gs, ga, w_ssm, w_attn, w_out, h_in, g_post, *, name):
    t, d = h_in.shape
    tm = _tile(t, 256)

    def body(yn_ref, o_ref, gs_ref, ga_ref, ws_ref, wa_ref, wo_ref, hin_ref, gp_ref, ys_ref, ya_ref, mg_ref, mix_ref, hout_ref):
        ys = _dot(yn_ref[...], ws_ref[...], NN)
        ya = _dot(o_ref[...], wa_ref[...], NN)
        merged = (_sigmoid(gs_ref[...].astype(F32)) * ys + _sigmoid(ga_ref[...].astype(F32)) * ya).astype(BF16)
        mix = _dot(merged, wo_ref[...], NN)
        ys_ref[...] = ys.astype(BF16)
        ya_ref[...] = ya.astype(BF16)
        mg_ref[...] = merged
        mix_ref[...] = mix
        hout_ref[...] = _rms_residual(mix, hin_ref[...], gp_ref[...], 1.0)

    def row(w):
        return pl.BlockSpec((tm, w), lambda i: (i, 0))
    bshape = jax.ShapeDtypeStruct((t, d), BF16)
    fshape = jax.ShapeDtypeStruct((t, d), F32)
    return pl.pallas_call(
        body, out_shape=(bshape, bshape, bshape, fshape, fshape), grid=(t // tm,),
        in_specs=[row(yn.shape[1]), row(o.shape[1]), row(d), row(d), _const_spec(w_ssm.shape), _const_spec(w_attn.shape),
                  _const_spec(w_out.shape), row(d), _const_spec((1, d))],
        out_specs=(row(d),) * 5, name=name, compiler_params=_params("parallel"))(yn, o, gs, ga, w_ssm, w_attn, w_out, h_in, g_post)


def _merge_bwd(dh, mix, g_post, gs, ga, ys, ya, w_ssm, w_attn, w_out, *, name):
    t, d = mix.shape
    tm = _tile(t, 256)
    d_ssm, d_attn = w_ssm.shape[0], w_attn.shape[0]

    def body(dh_ref, mix_ref, gp_ref, gs_ref, ga_ref, ys_ref, ya_ref, ws_ref, wa_ref, wo_ref,
             dmix_ref, dys_ref, dya_ref, dgs_ref, dga_ref, dyn_ref, do_ref, dgp_ref):
        @pl.when(pl.program_id(0) == 0)
        def _():
            dgp_ref[...] = jnp.zeros_like(dgp_ref)
        mv = mix_ref[...]
        dy = dh_ref[...]
        r = lax.rsqrt(jnp.mean(mv * mv, axis=-1, keepdims=True) + RMS_EPS)
        mhat = mv * r
        dyg = dy * gp_ref[...]
        dmix = (r * (dyg - mhat * jnp.mean(dyg * mhat, axis=-1, keepdims=True))).astype(BF16)
        dgp_ref[...] += jnp.sum(dy * mhat, axis=0, keepdims=True)
        dmix_ref[...] = dmix
        dmerged = _dot(dmix, wo_ref[...], NT)
        sgs = _sigmoid(gs_ref[...].astype(F32))
        sga = _sigmoid(ga_ref[...].astype(F32))
        dys = (dmerged * sgs).astype(BF16)
        dya = (dmerged * sga).astype(BF16)
        dys_ref[...] = dys
        dya_ref[...] = dya
        dgs_ref[...] = (dmerged * ys_ref[...].astype(F32) * sgs * (1.0 - sgs)).astype(BF16)
        dga_ref[...] = (dmerged * ya_ref[...].astype(F32) * sga * (1.0 - sga)).astype(BF16)
        dyn_ref[...] = _dot(dys, ws_ref[...], NT).astype(BF16)
        do_ref[...] = _dot(dya, wa_ref[...], NT).astype(BF16)

    def row(w):
        return pl.BlockSpec((tm, w), lambda i: (i, 0))

    def bshape(w):
        return jax.ShapeDtypeStruct((t, w), BF16)
    return pl.pallas_call(
        body, out_shape=(bshape(d),) * 5 + (bshape(d_ssm), bshape(d_attn), jax.ShapeDtypeStruct((1, d), F32)), grid=(t // tm,),
        in_specs=[row(d), row(d), _const_spec((1, d)), row(d), row(d), row(d), row(d),
                  _const_spec(w_ssm.shape), _const_spec(w_attn.shape), _const_spec(w_out.shape)],
        out_specs=(row(d),) * 5 + (row(d_ssm), row(d_attn), _const_spec((1, d))),
        name=name, compiler_params=_params("arbitrary"))(dh, mix, g_post, gs, ga, ys, ya, w_ssm, w_attn, w_out)


def _adamw(w, g, m, v, *, name):
    r, c = w.shape
    tm = _tile(r, 256)
    c1 = 1.0 - ADAM_B1 ** ADAM_STEP
    c2 = 1.0 - ADAM_B2 ** ADAM_STEP

    def body(w_ref, g_ref, m_ref, v_ref, d_ref, mo_ref, vo_ref):
        gv = g_ref[...]
        mn = ADAM_B1 * m_ref[...] + (1.0 - ADAM_B1) * gv
        vn = ADAM_B2 * v_ref[...] + (1.0 - ADAM_B2) * (gv * gv)
        mo_ref[...] = mn
        vo_ref[...] = vn
        d_ref[...] = -ADAM_LR * ((mn / c1) / (jnp.sqrt(vn / c2) + ADAM_EPS) + ADAM_WD * w_ref[...])

    blk = pl.BlockSpec((tm, c), lambda i: (i, 0))
    shp = jax.ShapeDtypeStruct((r, c), F32)
    return pl.pallas_call(body, out_shape=(shp, shp, shp), grid=(r // tm,), in_specs=[blk] * 4, out_specs=(blk,) * 3,
                          name=name, compiler_params=_params("parallel"))(w, g, m, v)


def _position():
    return lax.axis_index("x"), lax.axis_index("y"), lax.axis_index("c")


def _gather_exchange(shards):
    na = len(shards)

    def plan(ins, outs, sems):
        send_sems, recv_sems, local_sems = sems
        x, y, c = _position()
        me, sibling = (x, y, c), (x, y, 1 - c)
        chips = [(1 - x, y), (x, 1 - y), (1 - x, 1 - y)]

        def slot(a, pos):
            return outs[a].at[4 * pos[0] + 2 * pos[1] + pos[2]]

        def copy(a, k, block, to, src=None):
            return pltpu.make_async_remote_copy(
                src_ref=slot(a, block) if src is None else src, dst_ref=slot(a, block),
                send_sem=send_sems.at[a, k], recv_sem=recv_sems.at[a, k], device_id=to, device_id_type=MESH)

        mine = [pltpu.make_async_copy(ins[a], slot(a, me), local_sems.at[a]) for a in range(na)]
        first = []
        for a in range(na):
            first.append(copy(a, 0, me, sibling, src=ins[a]))
            first += [copy(a, 1 + j, me, (*chip, c), src=ins[a]) for j, chip in enumerate(chips)]
        return me, sibling, chips, copy, mine, first

    def start(ins, outs, sems):
        *_, mine, first = plan(ins, outs, sems)
        for cp in mine + first:
            cp.start()

    def finish(ins, outs, sems):
        me, sibling, chips, copy, mine, first = plan(ins, outs, sems)
        c = me[2]
        passed = []
        for a in range(na):
            for j, chip in enumerate(chips):
                copy(a, 1 + j, (*chip, c), me).wait_recv()
                fwd = copy(a, 4 + j, (*chip, c), sibling)
                fwd.start()
                passed.append(fwd)
        for a in range(na):
            copy(a, 0, sibling, me).wait_recv()
            for j, chip in enumerate(chips):
                copy(a, 4 + j, (*chip, 1 - c), me).wait_recv()
        for cp in first + passed:
            cp.wait_send()
        for cp in mine:
            cp.wait()

    return _Exchange(list(shards), [jax.ShapeDtypeStruct((N_DEV,) + s.shape, s.dtype) for s in shards],
                     [pltpu.SemaphoreType.DMA((na, 7)), pltpu.SemaphoreType.DMA((na, 7)), pltpu.SemaphoreType.DMA((na,))],
                     start, finish)


def _scatter_exchange(arrays):
    na = len(arrays)

    def copies(ins, outs, sems):
        send_sems, recv_sems = sems
        x, y, c = _position()
        out = []
        for a in range(na):
            for k in range(7):
                flip = k + 1
                peer = (x ^ (flip >> 2), y ^ ((flip >> 1) & 1), c ^ (flip & 1))
                peer_block = 4 * peer[0] + 2 * peer[1] + peer[2]
                out.append(pltpu.make_async_remote_copy(
                    src_ref=ins[a].at[peer_block], dst_ref=outs[a].at[k],
                    send_sem=send_sems.at[a, k], recv_sem=recv_sems.at[a, k], device_id=peer, device_id_type=MESH))
        return out

    def start(ins, outs, sems):
        for cp in copies(ins, outs, sems):
            cp.start()

    def finish(ins, outs, sems):
        for cp in copies(ins, outs, sems):
            cp.wait()

    return _Exchange(list(arrays), [jax.ShapeDtypeStruct((7,) + s.shape[1:], s.dtype) for s in arrays],
                     [pltpu.SemaphoreType.DMA((na, 7)), pltpu.SemaphoreType.DMA((na, 7))], start, finish)


def _exchange_alone(side, *, name):
    n_in = len(side.arrays)
    n_out = len(side.out_shape)

    def body(*refs):
        ins, outs, sems = refs[:n_in], refs[n_in:n_in + n_out], refs[n_in + n_out:]
        side.start(ins, outs, sems)
        side.finish(ins, outs, sems)

    hbm = pl.BlockSpec(memory_space=pl.ANY)
    return pl.pallas_call(body, out_shape=tuple(side.out_shape), in_specs=[hbm] * n_in, out_specs=tuple([hbm] * n_out),
                          scratch_shapes=list(side.scratch), name=name)(*side.arrays)


def _reduce_blocks(own, recv, *, name):
    r, c = own.shape
    tm = _tile(r, 256)

    def body(own_ref, recv_ref, o_ref):
        acc = own_ref[...]
        for k in range(7):
            acc = acc + recv_ref[k].astype(F32)
        o_ref[...] = acc

    return pl.pallas_call(
        body, out_shape=jax.ShapeDtypeStruct((r, c), F32), grid=(r // tm,),
        in_specs=[pl.BlockSpec((tm, c), lambda i: (i, 0)), pl.BlockSpec((7, tm, c), lambda i: (0, i, 0))],
        out_specs=pl.BlockSpec((tm, c), lambda i: (i, 0)), name=name, compiler_params=_params("parallel"))(own, recv)


def _all_reduce_small(vec, *, name):
    r, c = vec.shape

    def body(v_ref, o_ref, buf, send_sems, recv_sems):
        x, y, c_ = _position()
        me = 4 * x + 2 * y + c_
        buf[me] = v_ref[...]
        copies = []
        for k in range(7):
            flip = k + 1
            peer = (x ^ (flip >> 2), y ^ ((flip >> 1) & 1), c_ ^ (flip & 1))
            cp = pltpu.make_async_remote_copy(
                src_ref=v_ref, dst_ref=buf.at[me], send_sem=send_sems.at[k], recv_sem=recv_sems.at[k],
                device_id=peer, device_id_type=MESH)
            cp.start()
            copies.append(cp)
        for cp in copies:
            cp.wait()
        acc = buf[0]
        for d in range(1, N_DEV):
            acc = acc + buf[d]
        o_ref[...] = acc

    vm = pl.BlockSpec(memory_space=pltpu.VMEM)
    return pl.pallas_call(
        body, out_shape=jax.ShapeDtypeStruct((r, c), F32), in_specs=[vm], out_specs=vm,
        scratch_shapes=[pltpu.VMEM((N_DEV, r, c), F32), pltpu.SemaphoreType.DMA((7,)), pltpu.SemaphoreType.DMA((7,))],
        name=name)(vec)


def _pad_lanes(v, width=LANES):
    return jnp.pad(v, ((0, 0), (0, width - v.shape[1])))


def kernel(x, ffn1_pre_g, ffn1_w_gate, ffn1_w_up, ffn1_w_down, ffn1_post_g, mix_pre_g, w_in, conv_w, conv_b, dt_bias, a_log, d_skip, ssm_norm_g, w_ssm_proj, attn_sinks, rel_bias_table, w_attn_proj, w_out, mix_post_g, ffn2_pre_g, ffn2_w_gate, ffn2_w_up, ffn2_w_down, ffn2_post_g, loss_target, m_ffn1_pre_g, m_ffn1_w_gate, m_ffn1_w_up, m_ffn1_w_down, m_ffn1_post_g, m_mix_pre_g, m_w_in, m_conv_w, m_conv_b, m_dt_bias, m_a_log, m_d_skip, m_ssm_norm_g, m_w_ssm_proj, m_attn_sinks, m_rel_bias_table, m_w_attn_proj, m_w_out, m_mix_post_g, m_ffn2_pre_g, m_ffn2_w_gate, m_ffn2_w_up, m_ffn2_w_down, m_ffn2_post_g, v_ffn1_pre_g, v_ffn1_w_gate, v_ffn1_w_up, v_ffn1_w_down, v_ffn1_post_g, v_mix_pre_g, v_w_in, v_conv_w, v_conv_b, v_dt_bias, v_a_log, v_d_skip, v_ssm_norm_g, v_w_ssm_proj, v_attn_sinks, v_rel_bias_table, v_w_attn_proj, v_w_out, v_mix_post_g, v_ffn2_pre_g, v_ffn2_w_gate, v_ffn2_w_up, v_ffn2_w_down, v_ffn2_post_g):
    args = dict(locals())
    weight_names = ['ffn1_pre_g', 'ffn1_w_gate', 'ffn1_w_up', 'ffn1_w_down', 'ffn1_post_g', 'mix_pre_g', 'w_in', 'conv_w', 'conv_b',
                    'dt_bias', 'a_log', 'd_skip', 'ssm_norm_g', 'w_ssm_proj', 'attn_sinks', 'rel_bias_table', 'w_attn_proj', 'w_out',
                    'mix_post_g', 'ffn2_pre_g', 'ffn2_w_gate', 'ffn2_w_up', 'ffn2_w_down', 'ffn2_post_g']
    col_sharded = ('ffn1_w_gate', 'ffn1_w_up', 'w_in', 'ffn2_w_gate', 'ffn2_w_up')
    row_sharded = ('ffn1_w_down', 'w_ssm_proj', 'w_attn_proj', 'w_out', 'ffn2_w_down')
    big = col_sharded + row_sharded

    bl, s_len, d = x.shape
    t = bl * s_len
    d_inner = ssm_norm_g.shape[1]
    n_heads = dt_bias.shape[1]
    gn = SSM_GROUPS * SSM_STATE
    conv_dim = d_inner + 2 * gn
    q_dim = ATTN_Q_HEADS * ATTN_HEAD_DIM
    kv_dim = ATTN_KV_HEADS * ATTN_HEAD_DIM

    def local_2d(name, a):
        a = a[0]
        return a.T if name in col_sharded else a

    ffn1_names = ('ffn1_w_gate', 'ffn1_w_up', 'ffn1_w_down')
    ffn2_names = ('ffn2_w_gate', 'ffn2_w_up', 'ffn2_w_down')
    mixer_names = ('w_ssm_proj', 'w_attn_proj', 'w_out')

    def shard(n):
        return local_2d(n, args[n]).astype(BF16)

    def rows(g):
        return g.reshape(N_DEV * g.shape[1], g.shape[2])

    first_names = ffn1_names[:2]
    full = {n: rows(g) for n, g in zip(first_names, _exchange_alone(_gather_exchange([shard(n) for n in first_names]), name="gather_ffn1"))}

    x2 = x.reshape(t, d)
    tgt2 = loss_target.reshape(t, d)

    (h1,), saved1, got_in, got_mixer = _ffn_forward(
        x2, ffn1_pre_g, full['ffn1_w_gate'], full['ffn1_w_up'], lambda got: rows(got[2]), ffn1_post_g, "ffn1",
        side_up=_gather_exchange([shard('w_in'), conv_w[0], shard('ffn1_w_down')]),
        side_down=_gather_exchange([shard(n) for n in mixer_names]))
    full['ffn1_w_down'] = rows(got_in[2])
    full.update({n: rows(g) for n, g in zip(mixer_names, got_mixer)})
    conv_w_full = jnp.transpose(got_in[1], (1, 0, 2)).reshape(SSM_CONV, conv_dim)

    win_t = rows(got_in[0])
    dt_lo = 2 * d + d_inner + conv_dim
    n_main = win_t.shape[0] - n_heads
    win_p = jnp.concatenate([win_t[:dt_lo], win_t[dt_lo + n_heads:], win_t[dt_lo:dt_lo + n_heads],
                             jnp.zeros((LANES - n_heads, d), BF16)], axis=0)
    off = {'gs': 0, 'ga': d, 'z': 2 * d, 'xbc': 2 * d + d_inner, 'q': dt_lo, 'kv': dt_lo + q_dim, 'dt': n_main}

    u, (gs, ga, z, xbc, q, kv, dt_raw) = _proj_all(
        h1, mix_pre_g, win_p,
        [(off['gs'], d, BF16), (off['ga'], d, BF16), (off['z'], d_inner, BF16), (off['xbc'], conv_dim, BF16),
         (off['q'], q_dim, BF16), (off['kv'], 2 * kv_dim, BF16), (off['dt'], LANES, F32)], name="mix_proj")

    dtb_p, alog_p, dsk_p, sinks_p = _pad_lanes(dt_bias), _pad_lanes(a_log), _pad_lanes(d_skip), _pad_lanes(attn_sinks)
    xc = _conv_fwd(xbc, conv_w_full, conv_b, bl, name="conv_fwd")
    (y, hprev), got_ffn2 = _ssd_fwd(xc, dt_raw, dtb_p, alog_p, dsk_p, bl, n_heads,
                                    side=_gather_exchange([shard(n) for n in ffn2_names]), name="ssd_fwd")
    full.update({n: rows(g) for n, g in zip(ffn2_names, got_ffn2)})
    yn = _gated_norm_fwd(y, z, ssm_norm_g, name="gated_norm_fwd")

    onehot = _bucket_onehot()
    bias = _small_mm_hi(rel_bias_table.T, onehot, NN, name="rel_bias").reshape(ATTN_Q_HEADS, CHUNK, 2 * CHUNK)
    o, lse = _attn_fwd(q, kv, bias, sinks_p, bl, name="attn_fwd")

    ys, ya, merged, mix, h2 = _merge_fwd(yn, o, gs, ga, full['w_ssm_proj'], full['w_attn_proj'], full['w_out'], h1, mix_post_g,
                                         name="merge_fwd")

    (dh3, loss_vec), saved2, _, _ = _ffn_forward(h2, ffn2_pre_g, full['ffn2_w_gate'], full['ffn2_w_up'], full['ffn2_w_down'],
                                                 ffn2_post_g, "ffn2", target=tgt2)
    loss = lax.psum(loss_vec[0, 0], ("x", "y", "c"))

    grads, own, wire, received = {}, {}, {}, {}
    dh2, grads['ffn2_pre_g'], grads['ffn2_post_g'], g32, g16, _ = _ffn_backward(
        dh3, saved2, ffn2_pre_g, full['ffn2_w_gate'], full['ffn2_w_up'], full['ffn2_w_down'], ffn2_post_g, "ffn2")
    own.update(zip(ffn2_names, map(_stack8, g32)))
    wire.update(zip(ffn2_names, map(_stack8, g16)))

    dmix, dys, dya, dgs, dga, dyn, do, grads['mix_post_g'] = _merge_bwd(
        dh2, mix, mix_post_g, gs, ga, ys, ya, full['w_ssm_proj'], full['w_attn_proj'], full['w_out'], name="merge_bwd")
    for n, (lhs, rhs) in zip(mixer_names, ((yn, dys), (o, dya), (merged, dmix))):
        g32_, g16_, _ = _mm_tn([lhs], rhs, name=f"d{n}")
        own[n], wire[n] = _stack8(g32_), _stack8(g16_)

    dq, dkv, dbias, dsinks = _attn_bwd(do, q, kv, lse, bias, sinks_p, bl, name="attn_bwd")
    d_table = _small_mm_hi(onehot, dbias.reshape(ATTN_Q_HEADS, -1), NT, name="rel_bias_bwd")

    dy, dz, grads['ssm_norm_g'] = _gated_norm_bwd(dyn, y, z, ssm_norm_g, name="gated_norm_bwd")

    first_group = ffn2_names + mixer_names
    (dxc, ddt_raw, ddtb, dalog, ddsk), got = _ssd_bwd(dy, y, xc, dt_raw, hprev, dtb_p, alog_p, dsk_p, bl, n_heads,
                                                      side=_scatter_exchange([wire[n] for n in first_group]), name="ssd_bwd")
    received.update(zip(first_group, got))
    dxbc, dconv_w8, grads['conv_b'] = _conv_bwd(dxc, xbc, conv_w_full, conv_b, bl, name="conv_bwd")

    wide32, wide16, _ = _mm_tn([dgs, dga, dz, dxbc, dq], u, name="dw_in")
    kv32, kv16, _ = _mm_tn([dkv], u, name="dw_in_kv")
    dt32, dt16, _ = _mm_tn([ddt_raw], u, name="dw_in_dt")

    def original_order(wide, kv_part, dt_part):
        return _stack8(jnp.concatenate([wide[:dt_lo], dt_part[:n_heads], wide[dt_lo:], kv_part], axis=0))
    own['w_in'], wire['w_in'] = original_order(wide32, kv32, dt32), original_order(wide16, kv16, dt16)
    own['conv_w'] = jnp.transpose(dconv_w8[:SSM_CONV].reshape(SSM_CONV, N_DEV, conv_dim // N_DEV), (1, 0, 2))

    segs = [(g_, 0, off[k_]) for g_, k_ in zip([dgs, dga, dz, dxbc, dq, dkv, ddt_raw], ('gs', 'ga', 'z', 'xbc', 'q', 'kv', 'dt'))]
    dh1, grads['mix_pre_g'], got = _mm_nn_rmsbwd(segs, [win_p], h1, mix_pre_g, dh2,
                                                 side=_scatter_exchange([wire['w_in'], own['conv_w']]), name="mix_du")
    received.update(zip(('w_in', 'conv_w'), got))

    dx2, grads['ffn1_pre_g'], grads['ffn1_post_g'], g32, _, got = _ffn_backward(
        dh1, saved1, ffn1_pre_g, full['ffn1_w_gate'], full['ffn1_w_up'], full['ffn1_w_down'], ffn1_post_g, "ffn1", chain=True)
    own.update(zip(ffn1_names, map(_stack8, g32)))
    received.update(zip(ffn1_names, got))

    me = 4 * lax.axis_index("x") + 2 * lax.axis_index("y") + lax.axis_index("c")

    def own_block(a):
        return lax.dynamic_index_in_dim(a, me, 0, keepdims=False)
    reduced = {n: _reduce_blocks(own_block(own[n]), received[n], name=f"reduce_{n}") for n in big}
    conv_sum = _reduce_blocks(own_block(own['conv_w']), received['conv_w'], name="reduce_conv_w")

    grads['dt_bias'], grads['a_log'], grads['d_skip'] = ddtb[:, :n_heads], dalog[:, :n_heads], ddsk[:, :n_heads]
    grads['attn_sinks'] = dsinks[:, :ATTN_Q_HEADS]
    grads['rel_bias_table'] = d_table
    small = [n for n in weight_names if n not in big and n != 'conv_w']
    flat = jnp.concatenate([grads[n].reshape(-1) for n in small])
    n_small = flat.shape[0]
    n_rows = -(-n_small // (8 * LANES)) * 8
    flat = jnp.pad(flat, (0, n_rows * LANES - n_small)).reshape(n_rows, LANES)
    summed = _all_reduce_small(flat, name="allreduce_small").reshape(-1)
    pos = 0
    for n in small:
        size = grads[n].size
        grads[n] = summed[pos:pos + size].reshape(args[n].shape)
        pos += size

    out_g, out_d, out_m, out_v = {}, {}, {}, {}
    for n in big:
        w2, m2, v2 = local_2d(n, args[n]), local_2d(n, args['m_' + n]), local_2d(n, args['v_' + n])
        dlt, mn, vn = _adamw(w2, reduced[n], m2, v2, name=f"adamw_{n}")

        def back(a, n=n):
            return (a.T if n in col_sharded else a)[None]
        out_g[n], out_d[n], out_m[n], out_v[n] = back(reduced[n]), back(dlt), back(mn), back(vn)

    def pack(prefix):
        vals = [(grads[n] if prefix == 'g' else args[prefix + n]).reshape(-1) for n in small]
        vals.append((conv_sum if prefix == 'g' else args[prefix + 'conv_w']).reshape(-1))
        flat_ = jnp.concatenate(vals)
        rows_ = -(-flat_.shape[0] // (8 * LANES)) * 8
        return jnp.pad(flat_, (0, rows_ * LANES - flat_.shape[0])).reshape(rows_, LANES)

    g_small = pack('g')
    d_small, m_small, v_small = _adamw(pack(''), g_small, pack('m_'), pack('v_'), name="adamw_small")
    pos = 0
    for n in small + ['conv_w']:
        shape = args[n].shape
        size = int(np.prod(shape))
        for dst, src in ((out_g, g_small), (out_d, d_small), (out_m, m_small), (out_v, v_small)):
            dst[n] = src.reshape(-1)[pos:pos + size].reshape(shape)
        pos += size

    grad_x = dx2.reshape(bl, s_len, d)
    return (loss, grad_x, *[out_g[n] for n in weight_names], *[out_d[n] for n in weight_names],
            *[out_m[n] for n in weight_names], *[out_v[n] for n in weight_names])
```

```python
import functools
import math

import numpy as np
import jax
import jax.numpy as jnp
from jax import lax
from jax.experimental import pallas as pl
from jax.experimental.pallas import tpu as pltpu

F32 = jnp.float32
BF16 = jnp.bfloat16
MESH = pl.DeviceIdType.MESH
N_DEV = 8

SSM_HEAD_DIM = 64
SSM_GROUPS = 4
SSM_STATE = 128
SSM_CONV = 4
CHUNK = 128
ATTN_HEAD_DIM = 64
ATTN_Q_HEADS = 16
ATTN_KV_HEADS = 4
REL_BUCKETS = 32
REL_MAX_DISTANCE = 128
RMS_EPS = 1e-6
FFN_RESIDUAL_WEIGHT = 0.5
ADAM_LR, ADAM_B1, ADAM_B2, ADAM_EPS, ADAM_WD, ADAM_STEP = 0.001, 0.9, 0.999, 1e-08, 0.01, 10

LANES = 128
VMEM_LIMIT_BYTES = 56 * 1024 * 1024
FFN_COL_TILE = 1408

NEG_INF = float("-inf")


def _params(*sem):
    return pltpu.CompilerParams(dimension_semantics=sem, vmem_limit_bytes=VMEM_LIMIT_BYTES)


def _tile(n, pref, mult=8):
    if n <= pref:
        return n
    t = (pref // mult) * mult
    while t >= mult:
        if n % t == 0:
            return t
        t -= mult
    return n


def _sigmoid(x):
    return 1.0 / (1.0 + jnp.exp(-x))


def _dot(a, b, dims):
    return lax.dot_general(a, b, (dims, ((), ())), preferred_element_type=F32)


NN = ((1,), (0,))
NT = ((1,), (1,))
TN = ((0,), (0,))


def _dot_hi(a, b, dims=NN):
    return lax.dot_general(a, b, (dims, ((), ())), preferred_element_type=F32, precision=lax.Precision.HIGHEST)


def _const_spec(shape):
    nd = len(shape)
    return pl.BlockSpec(shape, lambda *_: (0,) * nd)


def _resident_spec(shape):
    nd = len(shape)
    return pl.BlockSpec(shape, lambda *_: (0,) * nd, pipeline_mode=pl.Buffered(1))


class _Exchange:
    def __init__(self, arrays, out_shape, scratch, start, finish):
        self.arrays, self.out_shape, self.scratch, self.start, self.finish = arrays, out_shape, scratch, start, finish


def _hosted_call(body, *, grid, in_specs, out_specs, out_shape, scratch_shapes, operands, side, name):
    in_specs, out_specs, out_shape, scratch_shapes = list(in_specs), list(out_specs), list(out_shape), list(scratch_shapes)
    sem = ("arbitrary",) * len(grid)
    if side is None:
        outs = pl.pallas_call(body, out_shape=tuple(out_shape), grid=grid, in_specs=in_specs, out_specs=tuple(out_specs),
                              scratch_shapes=scratch_shapes, name=name, compiler_params=_params(*sem))(*operands)
        return tuple(outs), ()
    n_in, n_out, n_scr = len(in_specs), len(out_shape), len(scratch_shapes)
    s_in, s_out = len(side.arrays), len(side.out_shape)

    def wrapped(*refs):
        refs = list(refs)
        main_in, side_in = refs[:n_in], refs[n_in:n_in + s_in]
        o0 = n_in + s_in
        main_out, side_out = refs[o0:o0 + n_out], refs[o0 + n_out:o0 + n_out + s_out]
        c0 = o0 + n_out + s_out
        main_scr, side_scr = refs[c0:c0 + n_scr], refs[c0 + n_scr:]
        ids = [pl.program_id(ax) for ax in range(len(grid))]
        first = functools.reduce(jnp.logical_and, [i == 0 for i in ids])
        last = functools.reduce(jnp.logical_and, [i == g - 1 for i, g in zip(ids, grid)])

        @pl.when(first)
        def _():
            side.start(side_in, side_out, side_scr)

        body(*main_in, *main_out, *main_scr)

        @pl.when(last)
        def _():
            side.finish(side_in, side_out, side_scr)

    hbm = pl.BlockSpec(memory_space=pl.ANY)
    outs = pl.pallas_call(
        wrapped, out_shape=tuple(out_shape + list(side.out_shape)), grid=grid,
        in_specs=in_specs + [hbm] * s_in, out_specs=tuple(out_specs + [hbm] * s_out),
        scratch_shapes=scratch_shapes + list(side.scratch), name=name, compiler_params=_params(*sem))(*operands, *side.arrays)
    return tuple(outs[:n_out]), tuple(outs[n_out:])


def _proj_all(h, g, w, segs, *, name):
    t, d = h.shape
    tm = _tile(t, 512)

    def body(h_ref, g_ref, w_ref, u_ref, *o_refs):
        hv = h_ref[...]
        r = lax.rsqrt(jnp.mean(hv * hv, axis=-1, keepdims=True) + RMS_EPS)
        uv = (hv * r * g_ref[...]).astype(BF16)
        u_ref[...] = uv
        for (row0, width, _), o_ref in zip(segs, o_refs):
            for c0, c1 in _col_chunks(width, 8 * LANES):
                o_ref[:, c0:c1] = _dot(uv, w_ref[row0 + c0:row0 + c1, :], NT).astype(o_ref.dtype)

    row = pl.BlockSpec((tm, d), lambda i: (i, 0))
    outs = pl.pallas_call(
        body, out_shape=(jax.ShapeDtypeStruct((t, d), BF16),) + tuple(jax.ShapeDtypeStruct((t, wd_), dt_) for _, wd_, dt_ in segs),
        grid=(t // tm,), in_specs=[row, _const_spec((1, d)), _resident_spec(w.shape)],
        out_specs=(row,) + tuple(pl.BlockSpec((tm, wd_), lambda i: (i, 0)) for _, wd_, _ in segs),
        name=name, compiler_params=_params("parallel"))(h, g, w)
    return outs[0], outs[1:]


def _mm_tn(a_list, b, *, tm=1408, tk=2048, side=None, name):
    t, n = b.shape
    tk = _tile(t, tk if len(a_list) == 1 else tk // 2)
    nk = t // tk
    widths = [a.shape[1] for a in a_list]
    tm = _tile(math.gcd(*widths), tm, LANES)
    assert all(w % tm == 0 for w in widths)
    starts = np.cumsum([0] + [w // tm for w in widths])
    nseg = len(a_list)

    def a_spec(s):
        lo, hi = int(starts[s]), int(starts[s + 1])

        def idx(i, k):
            active = jnp.logical_and(i >= lo, i < hi)
            return (jnp.where(active, k, 0), jnp.clip(i - lo, 0, hi - lo - 1))
        return pl.BlockSpec((tk, tm), idx)

    def body(*refs):
        a_refs, b_ref, o_ref, o16_ref, acc = refs[:nseg], refs[nseg], refs[nseg + 1], refs[nseg + 2], refs[nseg + 3]
        i, k = pl.program_id(0), pl.program_id(1)

        @pl.when(k == 0)
        def _():
            acc[...] = jnp.zeros_like(acc)

        bv = b_ref[...].astype(BF16)
        for s in range(nseg):
            lo, hi = int(starts[s]), int(starts[s + 1])

            @pl.when(jnp.logical_and(i >= lo, i < hi))
            def _(s=s):
                acc[...] += _dot(a_refs[s][...].astype(BF16), bv, TN)

        @pl.when(k == nk - 1)
        def _():
            o_ref[...] = acc[...]
            o16_ref[...] = acc[...].astype(BF16)

    rows = int(starts[-1]) * tm
    o_spec = pl.BlockSpec((tm, n), lambda i, k: (i, 0))
    (o32, o16), got = _hosted_call(
        body, out_shape=[jax.ShapeDtypeStruct((rows, n), F32), jax.ShapeDtypeStruct((rows, n), BF16)], grid=(int(starts[-1]), nk),
        in_specs=[a_spec(s) for s in range(nseg)] + [pl.BlockSpec((tk, n), lambda i, k: (k, 0))],
        out_specs=[o_spec, o_spec], scratch_shapes=[pltpu.VMEM((tm, n), F32)], operands=list(a_list) + [b], side=side, name=name)
    return o32, o16, got


def _mm_nn_rmsbwd(segs, weights, x, g, dres, *, tm=256, side=None, name):
    t, d = x.shape
    tm = _tile(t, tm)
    nseg, nw = len(segs), len(weights)

    def body(*refs):
        a_refs, w_refs = refs[:nseg], refs[nseg:nseg + nw]
        x_ref, g_ref, dres_ref, dx_ref, dg_ref = refs[nseg + nw:]

        @pl.when(pl.program_id(0) == 0)
        def _():
            dg_ref[...] = jnp.zeros_like(dg_ref)

        dn = None
        for s, (a, w_idx, row0) in enumerate(segs):
            part = _dot(a_refs[s][...].astype(BF16), w_refs[w_idx][row0:row0 + a.shape[1], :], NN)
            dn = part if dn is None else dn + part
        xv = x_ref[...]
        r = lax.rsqrt(jnp.mean(xv * xv, axis=-1, keepdims=True) + RMS_EPS)
        xhat = xv * r
        dyg = dn * g_ref[...]
        dx_ref[...] = dres_ref[...] + r * (dyg - xhat * jnp.mean(dyg * xhat, axis=-1, keepdims=True))
        dg_ref[...] += jnp.sum(dn * xhat, axis=0, keepdims=True)

    row = pl.BlockSpec((tm, d), lambda i: (i, 0))
    in_specs = [pl.BlockSpec((tm, a.shape[1]), lambda i: (i, 0)) for a, _, _ in segs]
    in_specs += [_resident_spec(w.shape) for w in weights] + [row, _const_spec((1, d)), row]
    (dx, dg), extra = _hosted_call(
        body, grid=(t // tm,), in_specs=in_specs, out_specs=[row, _const_spec((1, d))],
        out_shape=[jax.ShapeDtypeStruct((t, d), F32), jax.ShapeDtypeStruct((1, d), F32)], scratch_shapes=[],
        operands=[a for a, _, _ in segs] + list(weights) + [x, g, dres], side=side, name=name)
    return dx, dg, extra


def _col_chunks(width, chunk=4 * LANES):
    return [(c0, min(c0 + chunk, width)) for c0 in range(0, width, chunk)]


def _rms_fwd(x, g, *, name):
    t, d = x.shape
    tm = _tile(t, 512)

    def body(x_ref, g_ref, o_ref):
        xv = x_ref[...]
        r = lax.rsqrt(jnp.mean(xv * xv, axis=-1, keepdims=True) + RMS_EPS)
        o_ref[...] = (xv * r * g_ref[...]).astype(o_ref.dtype)

    row = pl.BlockSpec((tm, d), lambda i: (i, 0))
    return pl.pallas_call(body, out_shape=jax.ShapeDtypeStruct((t, d), BF16), grid=(t // tm,),
                          in_specs=[row, _const_spec((1, d))], out_specs=row, name=name, compiler_params=_params("parallel"))(x, g)


def _ffn_up(n, wgt, wut, *, side=None, name):
    t, d = n.shape
    f = wgt.shape[0]
    tm, tn = _tile(t, 512), _tile(f, FFN_COL_TILE, LANES)

    def body(n_ref, wg_ref, wu_ref, g_ref, u_ref, h_ref):
        nv = n_ref[...]
        gv = _dot(nv, wg_ref[...], NT)
        uv = _dot(nv, wu_ref[...], NT)
        g_ref[...] = gv.astype(BF16)
        u_ref[...] = uv.astype(BF16)
        h_ref[...] = (gv * _sigmoid(gv) * uv).astype(BF16)

    w_spec = pl.BlockSpec((tn, d), lambda j, i: (j, 0))
    o_spec = pl.BlockSpec((tm, tn), lambda j, i: (i, j))
    shp = jax.ShapeDtypeStruct((t, f), BF16)
    return _hosted_call(body, grid=(f // tn, t // tm), in_specs=[pl.BlockSpec((tm, d), lambda j, i: (i, 0)), w_spec, w_spec],
                        out_specs=[o_spec, o_spec, o_spec], out_shape=[shp, shp, shp], scratch_shapes=[], operands=[n, wgt, wut],
                        side=side, name=name)


def _rms_residual(acc, h, gp, weight):
    r = lax.rsqrt(jnp.mean(acc * acc, axis=-1, keepdims=True) + RMS_EPS)
    return h + weight * (acc * r * gp)


def _ffn_down(hid, wd, h_in, gp, *, target=None, side=None, name):
    t, f = hid.shape
    d = wd.shape[1]
    tm = _tile(t, 256)
    row = pl.BlockSpec((tm, d), lambda i: (i, 0))
    shp = jax.ShapeDtypeStruct((t, d), F32)
    in_specs = [pl.BlockSpec((tm, f), lambda i: (i, 0)), _resident_spec((f, d)), row, _const_spec((1, d))]

    if target is None:
        def body(hid_ref, wd_ref, hin_ref, gp_ref, f_ref, hout_ref):
            acc = _dot(hid_ref[...], wd_ref[...], NN)
            f_ref[...] = acc
            hout_ref[...] = _rms_residual(acc, hin_ref[...], gp_ref[...], FFN_RESIDUAL_WEIGHT)

        return _hosted_call(body, grid=(t // tm,), in_specs=in_specs, out_specs=[row, row], out_shape=[shp, shp], scratch_shapes=[],
                            operands=[hid, wd, h_in, gp], side=side, name=name)

    def body_loss(hid_ref, wd_ref, hin_ref, gp_ref, tgt_ref, f_ref, dh_ref, loss_ref):
        @pl.when(pl.program_id(0) == 0)
        def _():
            loss_ref[...] = jnp.zeros_like(loss_ref)
        acc = _dot(hid_ref[...], wd_ref[...], NN)
        f_ref[...] = acc
        e = _rms_residual(acc, hin_ref[...], gp_ref[...], FFN_RESIDUAL_WEIGHT) - tgt_ref[...]
        dh_ref[...] = e * (1.0 / d)
        per_row = jnp.sum(e * e, axis=1, keepdims=True) * (1.0 / d)
        loss_ref[...] += 0.5 * jnp.sum(per_row, axis=0, keepdims=True)

    return _hosted_call(body_loss, grid=(t // tm,), in_specs=in_specs + [row], out_specs=[row, row, _const_spec((1, LANES))],
                        out_shape=[shp, shp, jax.ShapeDtypeStruct((1, LANES), F32)], scratch_shapes=[],
                        operands=[hid, wd, h_in, gp, target], side=side, name=name)


def _post_bwd(dh, f, gp, weight, *, name):
    t, d = f.shape
    tm = _tile(t, 512)

    def body(dh_ref, f_ref, gp_ref, df_ref, dgp_ref):
        @pl.when(pl.program_id(0) == 0)
        def _():
            dgp_ref[...] = jnp.zeros_like(dgp_ref)
        fv = f_ref[...]
        dy = weight * dh_ref[...]
        r = lax.rsqrt(jnp.mean(fv * fv, axis=-1, keepdims=True) + RMS_EPS)
        fhat = fv * r
        dyg = dy * gp_ref[...]
        df_ref[...] = (r * (dyg - fhat * jnp.mean(dyg * fhat, axis=-1, keepdims=True))).astype(BF16)
        dgp_ref[...] += jnp.sum(dy * fhat, axis=0, keepdims=True)

    row = pl.BlockSpec((tm, d), lambda i: (i, 0))
    return pl.pallas_call(body, out_shape=(jax.ShapeDtypeStruct((t, d), BF16), jax.ShapeDtypeStruct((1, d), F32)), grid=(t // tm,),
                          in_specs=[row, row, _const_spec((1, d))], out_specs=(row, _const_spec((1, d))),
                          name=name, compiler_params=_params("arbitrary"))(dh, f, gp)


def _ffn_dhid(df, wd, g, u, *, name):
    t, d = df.shape
    f = wd.shape[0]
    tm, tn = _tile(t, 512), _tile(f, FFN_COL_TILE, LANES)

    def body(df_ref, wd_ref, g_ref, u_ref, dg_ref, du_ref):
        dh = _dot(df_ref[...], wd_ref[...], NT)
        gv = g_ref[...].astype(F32)
        uv = u_ref[...].astype(F32)
        sg = _sigmoid(gv)
        silu = gv * sg
        dg_ref[...] = (dh * uv * (sg + silu * (1.0 - sg))).astype(BF16)
        du_ref[...] = (dh * silu).astype(BF16)

    o_spec = pl.BlockSpec((tm, tn), lambda j, i: (i, j))
    shp = jax.ShapeDtypeStruct((t, f), BF16)
    return pl.pallas_call(body, out_shape=(shp, shp), grid=(f // tn, t // tm),
                          in_specs=[pl.BlockSpec((tm, d), lambda j, i: (i, 0)), pl.BlockSpec((tn, d), lambda j, i: (j, 0)), o_spec, o_spec],
                          out_specs=(o_spec, o_spec), name=name, compiler_params=_params("parallel", "arbitrary"))(df, wd, g, u)


def _ffn_forward(h_in, g_pre, wgt, wut, wd, g_post, tag, side_up=None, side_down=None, target=None):
    n = _rms_fwd(h_in, g_pre, name=f"{tag}_prenorm")
    (g, u, hid), got_up = _ffn_up(n, wgt, wut, side=side_up, name=f"{tag}_up")
    wd = wd(got_up) if callable(wd) else wd
    outs, got_down = _ffn_down(hid, wd, h_in, g_post, target=target, side=side_down, name=f"{tag}_down")
    return outs[1:], (h_in, n, g, u, hid, outs[0]), got_up, got_down


def _stack8(g):
    return g.reshape(N_DEV, g.shape[0] // N_DEV, g.shape[1])


def _ffn_backward(dh_out, saved, g_pre, wgt, wut, wd, g_post, tag, chain=False):
    h_in, n, g, u, hid, f = saved

    def side_of(grad16):
        return _scatter_exchange([_stack8(grad16)]) if chain else None

    df, dg_post = _post_bwd(dh_out, f, g_post, FFN_RESIDUAL_WEIGHT, name=f"{tag}_post_bwd")
    dgate, dup = _ffn_dhid(df, wd, g, u, name=f"{tag}_dhid")
    d_wd, d_wd16, _ = _mm_tn([hid], df, name=f"{tag}_dwd")
    d_wgt, d_wgt16, got_wd = _mm_tn([dgate], n, side=side_of(d_wd16), name=f"{tag}_dwg")
    d_wut, d_wut16, got_wg = _mm_tn([dup], n, side=side_of(d_wgt16), name=f"{tag}_dwu")
    dh_in, dg_pre, got_wu = _mm_nn_rmsbwd([(dgate, 0, 0), (dup, 1, 0)], [wgt, wut], h_in, g_pre, dh_out, side=side_of(d_wut16),
                                          name=f"{tag}_dn")
    received = (got_wg[0], got_wu[0], got_wd[0]) if chain else None
    return dh_in, dg_pre, dg_post, (d_wgt, d_wut, d_wd), (d_wgt16, d_wut16, d_wd16), received


CONV_ROWS = 128
HALO = 8


def _taps(w_ref):
    return [w_ref[k:k + 1, :] for k in range(SSM_CONV)]


def _conv_chunk(x_ref, xs, r0, taps, bias):
    xs[HALO + r0:HALO + r0 + CONV_ROWS, :] = x_ref[r0:r0 + CONV_ROWS, :].astype(F32)
    shifted = [xs[HALO + r0 - k:HALO + r0 - k + CONV_ROWS, :] for k in range(SSM_CONV)]
    pre = bias + shifted[0] * taps[SSM_CONV - 1]
    for k in range(1, SSM_CONV):
        pre = pre + shifted[k] * taps[SSM_CONV - 1 - k]
    return shifted, pre


def _fold_rows(a):
    return functools.reduce(jnp.add, [a[i:i + 8] for i in range(0, a.shape[0], 8)])


def _conv_fwd(xbc, conv_w, conv_b, bl, *, name):
    t, c = xbc.shape
    s = t // bl
    tc = LANES
    assert s % CONV_ROWS == 0

    def body(x_ref, w_ref, b_ref, o_ref, xs):
        taps, bias = _taps(w_ref), b_ref[...]
        xs[0:HALO, :] = jnp.zeros((HALO, tc), F32)
        for r0 in range(0, s, CONV_ROWS):
            _, pre = _conv_chunk(x_ref, xs, r0, taps, bias)
            o_ref[r0:r0 + CONV_ROWS, :] = (pre * _sigmoid(pre)).astype(o_ref.dtype)

    blk = pl.BlockSpec((s, tc), lambda b, j: (b, j))
    return pl.pallas_call(body, out_shape=jax.ShapeDtypeStruct((t, c), BF16), grid=(bl, c // tc),
                          in_specs=[blk, pl.BlockSpec((SSM_CONV, tc), lambda b, j: (0, j)), pl.BlockSpec((1, tc), lambda b, j: (0, j))],
                          out_specs=blk, scratch_shapes=[pltpu.VMEM((HALO + s, tc), F32)],
                          name=name, compiler_params=_params("parallel", "arbitrary"))(xbc, conv_w, conv_b)


def _conv_bwd(dxc, xbc, conv_w, conv_b, bl, *, name):
    t, c = xbc.shape
    s = t // bl
    tc = LANES

    def body(dy_ref, x_ref, w_ref, b_ref, dx_ref, dw_ref, db_ref, xs, dpre_s):
        @pl.when(pl.program_id(1) == 0)
        def _():
            dw_ref[...] = jnp.zeros_like(dw_ref)
            db_ref[...] = jnp.zeros_like(db_ref)

        taps, bias = _taps(w_ref), b_ref[...]
        zero8 = jnp.zeros((HALO, tc), F32)
        xs[0:HALO, :] = zero8
        dpre_s[s:s + HALO, :] = zero8
        sums = [zero8] * (SSM_CONV + 1)
        for r0 in range(0, s, CONV_ROWS):
            shifted, pre = _conv_chunk(x_ref, xs, r0, taps, bias)
            sg = _sigmoid(pre)
            dpre = dy_ref[r0:r0 + CONV_ROWS, :].astype(F32) * (sg * (1.0 + pre * (1.0 - sg)))
            dpre_s[r0:r0 + CONV_ROWS, :] = dpre
            sums = [acc + _fold_rows(dpre * sh) for acc, sh in zip(sums[:-1], shifted)] + [sums[-1] + _fold_rows(dpre)]
        for k in range(SSM_CONV):
            dw_ref[SSM_CONV - 1 - k:SSM_CONV - k, :] += jnp.sum(sums[k], axis=0, keepdims=True)
        db_ref[...] += jnp.sum(sums[-1], axis=0, keepdims=True)
        for r0 in range(0, s, CONV_ROWS):
            dx = dpre_s[r0:r0 + CONV_ROWS, :] * taps[SSM_CONV - 1]
            for k in range(1, SSM_CONV):
                dx = dx + dpre_s[r0 + k:r0 + k + CONV_ROWS, :] * taps[SSM_CONV - 1 - k]
            dx_ref[r0:r0 + CONV_ROWS, :] = dx.astype(dx_ref.dtype)

    blk = pl.BlockSpec((s, tc), lambda j, b: (b, j))
    return pl.pallas_call(
        body, out_shape=(jax.ShapeDtypeStruct((t, c), BF16), jax.ShapeDtypeStruct((8, c), F32), jax.ShapeDtypeStruct((1, c), F32)),
        grid=(c // tc, bl),
        in_specs=[blk, blk, pl.BlockSpec((SSM_CONV, tc), lambda j, b: (0, j)), pl.BlockSpec((1, tc), lambda j, b: (0, j))],
        out_specs=(blk, pl.BlockSpec((8, tc), lambda j, b: (0, j)), pl.BlockSpec((1, tc), lambda j, b: (0, j))),
        scratch_shapes=[pltpu.VMEM((HALO + s, tc), F32), pltpu.VMEM((s + HALO, tc), F32)],
        name=name, compiler_params=_params("parallel", "arbitrary"))(dxc, xbc, conv_w, conv_b)


def _softplus(x):
    return jnp.maximum(x, 0.0) + jnp.log1p(jnp.exp(-jnp.abs(x)))


def _hilo_dot(v, m_b, dims=NN):
    hi = v.astype(BF16)
    lo = (v - hi.astype(F32)).astype(BF16)
    return _dot(hi, m_b, dims) + _dot(lo, m_b, dims)


def _ssd_chunk_common(dtraw_ref, dtb_ref, alog_ref, dsk_ref, d_inner):
    q, p = CHUNK, SSM_HEAD_DIM
    shift = p.bit_length() - 1
    assert 1 << shift == p
    dt = _softplus(dtraw_ref[...] + dtb_ref[...])
    a = -jnp.exp(alog_ref[...])
    ii = lax.broadcasted_iota(jnp.int32, (q, q), 0)
    jj = lax.broadcasted_iota(jnp.int32, (q, q), 1)
    causal = ii >= jj
    tril = jnp.where(causal, 1.0, 0.0).astype(F32)
    triu = jnp.where(ii <= jj, 1.0, 0.0).astype(F32)
    a_cs = _dot_hi(tril, dt * a)
    a_cs_t = a_cs.T
    a_last = a_cs[q - 1:q, :]
    e_col = jnp.exp(a_cs)
    dec_end = jnp.exp(a_last - a_cs)
    head_of_col = lax.shift_right_logical(lax.broadcasted_iota(jnp.int32, (LANES, d_inner), 1), shift)
    spread = (lax.broadcasted_iota(jnp.int32, (LANES, d_inner), 0) == head_of_col).astype(BF16)
    wide = _hilo_dot(jnp.concatenate([dt, e_col, dec_end, jnp.broadcast_to(dsk_ref[...], (8, LANES))], axis=0), spread)
    return dict(dt=dt, a=a, a_cs=a_cs, a_cs_t=a_cs_t, a_last=a_last, dec_end=dec_end, causal=causal, triu=triu,
                dt_e=wide[:q], e_e=wide[q:2 * q], dec_e=wide[2 * q:3 * q], dsk_e=wide[3 * q:3 * q + 1])


def _fill_block_diag(bd_ref, src_ref, hpg, col0=0):
    q, p = CHUNK, SSM_HEAD_DIM
    for hh in range(hpg):
        bd_ref[hh * q:(hh + 1) * q, hh * p:(hh + 1) * p] = src_ref[:, col0 + hh * p:col0 + (hh + 1) * p]


def _lane_onehot(h):
    return (lax.broadcasted_iota(jnp.int32, (1, LANES), 1) == h).astype(F32)


def _ssd_fwd(xc, dt_raw, dt_bias, a_log, d_skip, bl, n_heads, *, side=None, name):
    t = xc.shape[0]
    q, p, nst, grp = CHUNK, SSM_HEAD_DIM, SSM_STATE, SSM_GROUPS
    d_inner = n_heads * p
    hpg = n_heads // grp
    gw = hpg * p
    nc = t // bl // q
    assert d_inner % (grp * nst) == 0 and nst == LANES

    def body(xs_ref, b_ref, c_ref, dtraw_ref, dtb_ref, alog_ref, dsk_ref, y_ref, hprev_ref, state, m_all, x_bd, xdt_s):
        @pl.when(jnp.logical_and(pl.program_id(0) == 0, pl.program_id(1) == 0))
        def _():
            x_bd[...] = jnp.zeros_like(x_bd)

        @pl.when(pl.program_id(1) == 0)
        def _():
            state[...] = jnp.zeros_like(state)

        cm = _ssd_chunk_common(dtraw_ref, dtb_ref, alog_ref, dsk_ref, d_inner)
        for g in range(grp):
            cols = slice(g * gw, (g + 1) * gw)
            bg = b_ref[:, g * nst:(g + 1) * nst]
            cg = c_ref[:, g * nst:(g + 1) * nst]
            scores = _dot(cg, bg, NT)
            for hh in range(hpg):
                h = g * hpg + hh
                seg = cm['a_cs'][:, h:h + 1] - cm['a_cs_t'][h:h + 1, :]
                m_all[:, hh * q:(hh + 1) * q] = (scores * jnp.exp(jnp.where(cm['causal'], seg, NEG_INF))).astype(BF16)
            xs = xs_ref[:, cols].astype(F32)
            xdt = xs * cm['dt_e'][:, cols]
            xdt_s[...] = xdt.astype(BF16)
            _fill_block_diag(x_bd, xdt_s, hpg)
            hprev = state[g]
            hprev_ref[g] = hprev
            y = _dot(m_all[...], x_bd[...], NN) + cm['e_e'][:, cols] * _dot(cg, hprev.astype(BF16), NT)
            y_ref[:, cols] = y + cm['dsk_e'][:, cols] * xs
            st = _dot((xdt * cm['dec_e'][:, cols]).astype(BF16), bg, TN)
            for hh in range(hpg):
                h = g * hpg + hh
                rows = slice(hh * p, (hh + 1) * p)
                state[g, rows, :] = jnp.exp(cm['a_last'][:, h:h + 1]) * hprev[rows] + st[rows]

    gn = grp * nst

    def rowmap(b, c):
        return b * nc + c
    vec = pl.BlockSpec((1, LANES), lambda b, c: (0, 0))
    return _hosted_call(
        body,
        out_shape=[jax.ShapeDtypeStruct((t, d_inner), F32), jax.ShapeDtypeStruct((t // q, grp, gw, nst), F32)],
        grid=(bl, nc),
        in_specs=[pl.BlockSpec((q, d_inner), lambda b, c: (rowmap(b, c), 0)),
                  pl.BlockSpec((q, gn), lambda b, c: (rowmap(b, c), d_inner // gn)),
                  pl.BlockSpec((q, gn), lambda b, c: (rowmap(b, c), d_inner // gn + 1)),
                  pl.BlockSpec((q, LANES), lambda b, c: (rowmap(b, c), 0)), vec, vec, vec],
        out_specs=[pl.BlockSpec((q, d_inner), lambda b, c: (rowmap(b, c), 0)),
                   pl.BlockSpec((None, grp, gw, nst), lambda b, c: (rowmap(b, c), 0, 0, 0))],
        scratch_shapes=[pltpu.VMEM((grp, gw, nst), F32), pltpu.VMEM((q, hpg * q), BF16), pltpu.VMEM((hpg * q, gw), BF16),
                        pltpu.VMEM((q, gw), BF16)],
        operands=[xc, xc, xc, dt_raw, dt_bias, a_log, d_skip], side=side, name=name)


def _ssd_bwd(dy, y, xc, dt_raw, hprev_all, dt_bias, a_log, d_skip, bl, n_heads, *, side=None, name):
    t, c_dim = xc.shape
    q, p, nst, grp = CHUNK, SSM_HEAD_DIM, SSM_STATE, SSM_GROUPS
    d_inner = n_heads * p
    hpg = n_heads // grp
    gw = hpg * p
    nc = t // bl // q
    gn = grp * nst
    shift = p.bit_length() - 1

    def body(dy_ref, y_ref, xs_ref, b_ref, c_ref, dtraw_ref, hprev_ref, dtb_ref, alog_ref, dsk_ref,
             dxc_ref, ddtraw_ref, ddtb_ref, dalog_ref, ddsk_ref, dstate, mt_all, x_bd, dy_bd, xdt_s):
        @pl.when(jnp.logical_and(pl.program_id(0) == 0, pl.program_id(1) == 0))
        def _():
            ddtb_ref[...] = jnp.zeros_like(ddtb_ref)
            dalog_ref[...] = jnp.zeros_like(dalog_ref)
            ddsk_ref[...] = jnp.zeros_like(ddsk_ref)
            x_bd[...] = jnp.zeros_like(x_bd)
            dy_bd[...] = jnp.zeros_like(dy_bd)

        @pl.when(pl.program_id(1) == 0)
        def _():
            dstate[...] = jnp.zeros_like(dstate)

        cm = _ssd_chunk_common(dtraw_ref, dtb_ref, alog_ref, dsk_ref, d_inner)
        causal = cm['causal']
        upper = cm['triu'] > 0.5
        seg_row = lax.shift_right_logical(lax.broadcasted_iota(jnp.int32, (gw, LANES), 0), shift)
        seg_lane = lax.broadcasted_iota(jnp.int32, (gw, LANES), 1)
        sums = jnp.zeros((5 * q, LANES), F32)
        state_dot = jnp.zeros((1, LANES), F32)
        for g in range(grp):
            cols = slice(g * gw, (g + 1) * gw)
            seg_sum = (seg_row + g * hpg == seg_lane).astype(BF16)
            bg = b_ref[:, g * nst:(g + 1) * nst]
            cg = c_ref[:, g * nst:(g + 1) * nst]
            scores_t = _dot(bg, cg, NT)
            xs = xs_ref[:, cols].astype(F32)
            xdt = xs * cm['dt_e'][:, cols]
            xdt_s[...] = xdt.astype(BF16)
            _fill_block_diag(x_bd, xdt_s, hpg)
            _fill_block_diag(dy_bd, dy_ref, hpg, g * gw)
            dy_b = dy_ref[:, cols]
            dyf = dy_b.astype(F32)
            dm_all = _dot(dy_b, x_bd[...], NT)
            dscores = jnp.zeros((q, q), F32)
            for hh in range(hpg):
                h = g * hpg + hh
                blk = slice(hh * q, (hh + 1) * q)
                seg = cm['a_cs'][:, h:h + 1] - cm['a_cs_t'][h:h + 1, :]
                mt_all[:, blk] = (scores_t * jnp.exp(jnp.where(upper, -seg, NEG_INF))).astype(BF16)
                dscores = dscores + dm_all[:, blk] * jnp.exp(jnp.where(causal, seg, NEG_INF))
            hprev = hprev_ref[g]
            hprev_b = hprev.astype(BF16)
            dhn = dstate[g]
            dhn_b = dhn.astype(BF16)
            e_e, dec_e = cm['e_e'][:, cols], cm['dec_e'][:, cols]
            y_scan = y_ref[:, cols] - cm['dsk_e'][:, cols] * xs
            dye_b = (dyf * e_e).astype(BF16)
            dcg = _dot(dye_b, hprev_b, NN)
            dhp = _dot(dye_b, cg, TN)
            bdh = _dot(bg, dhn_b, NT)
            dbg = _dot((xdt * dec_e).astype(BF16), dhn_b, NN)
            dx_diag = _dot(mt_all[...], dy_bd[...], NN)
            dx = dec_e * bdh + dx_diag
            ds_b = dscores.astype(BF16)
            dcg = dcg + _dot(ds_b, bg, NN)
            dbg = dbg + _dot(ds_b, cg, TN)
            x_rounded = xdt_s[...].astype(F32)
            sums = sums + _hilo_dot(jnp.concatenate([dyf * y_scan, xdt * bdh, x_rounded * dx_diag, dx * xs, dyf * xs], axis=0), seg_sum)
            state_dot = state_dot + jnp.sum(_hilo_dot(dhn * hprev, seg_sum, TN), axis=0, keepdims=True)
            dxc_ref[:, cols] = (dx * cm['dt_e'][:, cols] + cm['dsk_e'][:, cols] * dyf).astype(dxc_ref.dtype)
            dxc_ref[:, d_inner + g * nst:d_inner + (g + 1) * nst] = dbg.astype(dxc_ref.dtype)
            dxc_ref[:, d_inner + gn + g * nst:d_inner + gn + (g + 1) * nst] = dcg.astype(dxc_ref.dtype)
            for hh in range(hpg):
                h = g * hpg + hh
                rows = slice(hh * p, (hh + 1) * p)
                dstate[g, rows, :] = jnp.exp(cm['a_last'][:, h:h + 1]) * dhn[rows] + dhp[rows]
        s_y, s_end, s_diag, s_dt, s_skip = (sums[k * q:(k + 1) * q] for k in range(5))
        dt, a, dec_end = cm['dt'], cm['a'], cm['dec_end']
        last_row = (lax.broadcasted_iota(jnp.int32, (q, 1), 0) == q - 1).astype(F32)
        da_last = jnp.sum(dec_end * s_end, axis=0, keepdims=True) + jnp.exp(cm['a_last']) * state_dot
        da = s_y - dec_end * s_end - s_diag + last_row * da_last
        ddta = _dot_hi(cm['triu'], da)
        ddt = s_dt + ddta * a
        d_a = jnp.sum(ddta * dt, axis=0, keepdims=True)
        ddt_raw = ddt * _sigmoid(dtraw_ref[...] + dtb_ref[...])
        ddtraw_ref[...] = ddt_raw
        ddtb_ref[...] += jnp.sum(ddt_raw, axis=0, keepdims=True)
        dalog_ref[...] += d_a * a
        ddsk_ref[...] += jnp.sum(s_skip, axis=0, keepdims=True)

    def rowmap(b, c):
        return b * nc + (nc - 1 - c)
    vec = pl.BlockSpec((1, LANES), lambda b, c: (0, 0))
    vec_shape = jax.ShapeDtypeStruct((1, LANES), F32)
    return _hosted_call(
        body,
        out_shape=[jax.ShapeDtypeStruct((t, c_dim), BF16), jax.ShapeDtypeStruct((t, LANES), F32), vec_shape, vec_shape, vec_shape],
        grid=(bl, nc),
        in_specs=[pl.BlockSpec((q, d_inner), lambda b, c: (rowmap(b, c), 0)),
                  pl.BlockSpec((q, d_inner), lambda b, c: (rowmap(b, c), 0)),
                  pl.BlockSpec((q, d_inner), lambda b, c: (rowmap(b, c), 0)),
                  pl.BlockSpec((q, gn), lambda b, c: (rowmap(b, c), d_inner // gn)),
                  pl.BlockSpec((q, gn), lambda b, c: (rowmap(b, c), d_inner // gn + 1)),
                  pl.BlockSpec((q, LANES), lambda b, c: (rowmap(b, c), 0)),
                  pl.BlockSpec((None, grp, gw, nst), lambda b, c: (rowmap(b, c), 0, 0, 0)), vec, vec, vec],
        out_specs=[pl.BlockSpec((q, c_dim), lambda b, c: (rowmap(b, c), 0)),
                   pl.BlockSpec((q, LANES), lambda b, c: (rowmap(b, c), 0)), vec, vec, vec],
        scratch_shapes=[pltpu.VMEM((grp, gw, nst), F32), pltpu.VMEM((q, hpg * q), BF16),
                        pltpu.VMEM((hpg * q, gw), BF16), pltpu.VMEM((hpg * q, gw), BF16), pltpu.VMEM((q, gw), BF16)],
        operands=[dy, y, xc, xc, xc, dt_raw, hprev_all, dt_bias, a_log, d_skip], side=side, name=name)


def _gated_norm_fwd(y, z, ng, *, name):
    t, d = y.shape
    tm = _tile(t, 256)
    gw = d // SSM_GROUPS

    def body(y_ref, z_ref, ng_ref, o_ref):
        for g in range(SSM_GROUPS):
            sl = slice(g * gw, (g + 1) * gw)
            zv = z_ref[:, sl].astype(F32)
            yg = y_ref[:, sl] * (zv * _sigmoid(zv))
            r = lax.rsqrt(jnp.mean(yg * yg, axis=-1, keepdims=True) + RMS_EPS)
            o_ref[:, sl] = (yg * r * ng_ref[:, sl]).astype(o_ref.dtype)

    row = pl.BlockSpec((tm, d), lambda i: (i, 0))
    return pl.pallas_call(body, out_shape=jax.ShapeDtypeStruct((t, d), BF16), grid=(t // tm,),
                          in_specs=[row, row, _const_spec((1, d))], out_specs=row, name=name, compiler_params=_params("parallel"))(y, z, ng)


def _gated_norm_bwd(dyn, y, z, ng, *, name):
    t, d = y.shape
    tm = _tile(t, 256)
    gw = d // SSM_GROUPS

    def body(dyn_ref, y_ref, z_ref, ng_ref, dy_ref, dz_ref, dng_ref):
        @pl.when(pl.program_id(0) == 0)
        def _():
            dng_ref[...] = jnp.zeros_like(dng_ref)
        for g in range(SSM_GROUPS):
            sl = slice(g * gw, (g + 1) * gw)
            zv = z_ref[:, sl].astype(F32)
            yv = y_ref[:, sl]
            sg = _sigmoid(zv)
            sz = zv * sg
            yg = yv * sz
            r = lax.rsqrt(jnp.mean(yg * yg, axis=-1, keepdims=True) + RMS_EPS)
            yhat = yg * r
            dn = dyn_ref[:, sl].astype(F32)
            dyg_n = dn * ng_ref[:, sl]
            dyg = r * (dyg_n - yhat * jnp.mean(dyg_n * yhat, axis=-1, keepdims=True))
            dy_ref[:, sl] = (dyg * sz).astype(dy_ref.dtype)
            dz_ref[:, sl] = (dyg * yv * (sg * (1.0 + zv * (1.0 - sg)))).astype(dz_ref.dtype)
            dng_ref[:, sl] += jnp.sum(dn * yhat, axis=0, keepdims=True)

    row = pl.BlockSpec((tm, d), lambda i: (i, 0))
    shp = jax.ShapeDtypeStruct((t, d), BF16)
    return pl.pallas_call(body, out_shape=(shp, shp, jax.ShapeDtypeStruct((1, d), F32)), grid=(t // tm,),
                          in_specs=[row, row, row, _const_spec((1, d))], out_specs=(row, row, _const_spec((1, d))),
                          name=name, compiler_params=_params("arbitrary"))(dyn, y, z, ng)


def _bucket_onehot():
    blk = CHUNK
    qi = jnp.arange(blk)[:, None]
    kj = jnp.arange(2 * blk)[None, :]
    dist = jnp.maximum(qi + blk - kj, 0)
    max_exact = REL_BUCKETS // 2
    d = jnp.maximum(dist, 1).astype(F32)
    large = max_exact + (jnp.log(d / max_exact) / math.log(REL_MAX_DISTANCE / max_exact) * (REL_BUCKETS - max_exact)).astype(jnp.int32)
    large = jnp.minimum(large, REL_BUCKETS - 1)
    bucket = jnp.where(dist < max_exact, dist, large).reshape(-1)
    return (bucket[None, :] == jnp.arange(REL_BUCKETS)[:, None]).astype(F32)


def _small_mm_hi(a, b, dims, *, name):
    def body(a_ref, b_ref, o_ref):
        o_ref[...] = _dot_hi(a_ref[...], b_ref[...], dims)
    n = b.shape[0] if dims == NT else b.shape[1]
    return pl.pallas_call(body, out_shape=jax.ShapeDtypeStruct((a.shape[0], n), F32), name=name)(a, b)


def _attn_band_mask(n, rep):
    blk = CHUNK
    ii = lax.broadcasted_iota(jnp.int32, (rep * blk, 2 * blk), 0) & (blk - 1)
    jj = lax.broadcasted_iota(jnp.int32, (rep * blk, 2 * blk), 1)
    dist = ii + blk - jj
    in_window = jnp.logical_and(dist >= 0, dist < blk)
    return jnp.logical_and(in_window, jnp.logical_or(jj >= blk, n > 0))


def _attn_fwd(q, kv, bias, sinks, bl, *, name):
    t, qd = q.shape
    blk, hd = CHUNK, ATTN_HEAD_DIM
    kvd = ATTN_KV_HEADS * hd
    rep = ATTN_Q_HEADS // ATTN_KV_HEADS
    nb = t // bl // blk
    scale = hd ** -0.5

    def body(q_ref, kp_ref, kc_ref, vp_ref, vc_ref, bias_ref, sink_ref, o_ref, lse_ref):
        n = pl.program_id(1)
        mask = _attn_band_mask(n, rep)
        lse = jnp.zeros((blk, LANES), F32)
        for kvh in range(ATTN_KV_HEADS):
            ks = slice(kvh * hd, (kvh + 1) * hd)
            heads = range(kvh * rep, (kvh + 1) * rep)
            qs = jnp.concatenate([q_ref[:, h * hd:(h + 1) * hd] for h in heads], axis=0)
            kk = jnp.concatenate([kp_ref[:, ks], kc_ref[:, ks]], axis=0)
            vv = jnp.concatenate([vp_ref[:, ks], vc_ref[:, ks]], axis=0)
            bias = bias_ref[kvh * rep:(kvh + 1) * rep].reshape(rep * blk, 2 * blk)
            s = jnp.where(mask, _dot(qs, kk, NT) * scale + bias, NEG_INF)
            sink = jnp.concatenate([jnp.broadcast_to(sink_ref[:, h:h + 1], (blk, 1)) for h in heads], axis=0)
            m = jnp.maximum(jnp.max(s, axis=1, keepdims=True), sink)
            p = jnp.exp(s - m)
            den = jnp.sum(p, axis=1, keepdims=True) + jnp.exp(sink - m)
            o = _dot((p * (1.0 / den)).astype(BF16), vv, NN)
            lse_s = m + jnp.log(den)
            for r, h in enumerate(heads):
                o_ref[:, h * hd:(h + 1) * hd] = o[r * blk:(r + 1) * blk].astype(o_ref.dtype)
                lse = lse + lse_s[r * blk:(r + 1) * blk] * _lane_onehot(h)
        lse_ref[...] = lse

    def cur(b, n):
        return b * nb + n

    def prev(b, n):
        return b * nb + jnp.maximum(n - 1, 0)
    return pl.pallas_call(
        body, out_shape=(jax.ShapeDtypeStruct((t, qd), BF16), jax.ShapeDtypeStruct((t, LANES), F32)), grid=(bl, nb),
        in_specs=[pl.BlockSpec((blk, qd), lambda b, n: (cur(b, n), 0)),
                  pl.BlockSpec((blk, kvd), lambda b, n: (prev(b, n), 0)), pl.BlockSpec((blk, kvd), lambda b, n: (cur(b, n), 0)),
                  pl.BlockSpec((blk, kvd), lambda b, n: (prev(b, n), 1)), pl.BlockSpec((blk, kvd), lambda b, n: (cur(b, n), 1)),
                  _const_spec(bias.shape), _const_spec((1, LANES))],
        out_specs=(pl.BlockSpec((blk, qd), lambda b, n: (cur(b, n), 0)), pl.BlockSpec((blk, LANES), lambda b, n: (cur(b, n), 0))),
        name=name, compiler_params=_params("parallel", "arbitrary"))(q, kv, kv, kv, kv, bias, sinks)


def _attn_bwd(do, q, kv, lse, bias, sinks, bl, *, name):
    t, qd = q.shape
    blk, hd = CHUNK, ATTN_HEAD_DIM
    kvd = ATTN_KV_HEADS * hd
    rep = ATTN_Q_HEADS // ATTN_KV_HEADS
    s_len = t // bl
    nb = s_len // blk
    scale = hd ** -0.5

    def body(do_ref, q_ref, kp_ref, kc_ref, vp_ref, vc_ref, lse_ref, bias_ref, sink_ref, dq_ref, dkv_ref, dbias_ref, dsink_ref):
        n = pl.program_id(1)

        @pl.when(jnp.logical_and(pl.program_id(0) == 0, n == 0))
        def _():
            dbias_ref[...] = jnp.zeros_like(dbias_ref)
            dsink_ref[...] = jnp.zeros_like(dsink_ref)

        mask = _attn_band_mask(n, rep)
        r_cur = pl.multiple_of(n * blk, blk)
        r_prev = pl.multiple_of(jnp.maximum(n - 1, 0) * blk, blk)
        dsink = jnp.zeros((1, LANES), F32)
        for kvh in range(ATTN_KV_HEADS):
            ks = slice(kvh * hd, (kvh + 1) * hd)
            heads = range(kvh * rep, (kvh + 1) * rep)
            qs = jnp.concatenate([q_ref[:, h * hd:(h + 1) * hd] for h in heads], axis=0)
            dos = jnp.concatenate([do_ref[:, h * hd:(h + 1) * hd] for h in heads], axis=0)
            kk = jnp.concatenate([kp_ref[:, ks], kc_ref[:, ks]], axis=0)
            vv = jnp.concatenate([vp_ref[:, ks], vc_ref[:, ks]], axis=0)
            bias = bias_ref[kvh * rep:(kvh + 1) * rep].reshape(rep * blk, 2 * blk)
            lse = jnp.concatenate([lse_ref[:, h:h + 1] for h in heads], axis=0)
            sink = jnp.concatenate([jnp.broadcast_to(sink_ref[:, h:h + 1], (blk, 1)) for h in heads], axis=0)
            p = jnp.exp(jnp.where(mask, _dot(qs, kk, NT) * scale + bias, NEG_INF) - lse)
            dp = _dot(dos, vv, NT)
            delta = jnp.sum(p * dp, axis=1, keepdims=True)
            ds = p * (dp - delta)
            dsink_rows = jnp.exp(sink - lse) * delta
            ds_b = ds.astype(BF16)
            dq_s = _dot(ds_b, kk, NN) * scale
            dkk = _dot(ds_b, qs, TN) * scale
            dvv = _dot(p.astype(BF16), dos, TN)
            for r, h in enumerate(heads):
                rows = slice(r * blk, (r + 1) * blk)
                dbias_ref[h] += ds[rows]
                dq_ref[:, h * hd:(h + 1) * hd] = dq_s[rows].astype(dq_ref.dtype)
                dsink = dsink - jnp.sum(dsink_rows[rows], axis=0, keepdims=True) * _lane_onehot(h)
            vs = slice(kvd + kvh * hd, kvd + (kvh + 1) * hd)
            dkv_ref[pl.ds(r_cur, blk), ks] = dkk[blk:]
            dkv_ref[pl.ds(r_cur, blk), vs] = dvv[blk:]

            @pl.when(n > 0)
            def _():
                dkv_ref[pl.ds(r_prev, blk), ks] += dkk[:blk]
                dkv_ref[pl.ds(r_prev, blk), vs] += dvv[:blk]
        dsink_ref[...] += dsink

    def cur(b, n):
        return b * nb + n

    def prev(b, n):
        return b * nb + jnp.maximum(n - 1, 0)
    qspec = pl.BlockSpec((blk, qd), lambda b, n: (cur(b, n), 0))
    return pl.pallas_call(
        body,
        out_shape=(jax.ShapeDtypeStruct((t, qd), BF16), jax.ShapeDtypeStruct((t, 2 * kvd), F32),
                   jax.ShapeDtypeStruct(bias.shape, F32), jax.ShapeDtypeStruct((1, LANES), F32)),
        grid=(bl, nb),
        in_specs=[qspec, qspec,
                  pl.BlockSpec((blk, kvd), lambda b, n: (prev(b, n), 0)), pl.BlockSpec((blk, kvd), lambda b, n: (cur(b, n), 0)),
                  pl.BlockSpec((blk, kvd), lambda b, n: (prev(b, n), 1)), pl.BlockSpec((blk, kvd), lambda b, n: (cur(b, n), 1)),
                  pl.BlockSpec((blk, LANES), lambda b, n: (cur(b, n), 0)), _const_spec(bias.shape), _const_spec((1, LANES))],
        out_specs=(qspec, pl.BlockSpec((s_len, 2 * kvd), lambda b, n: (b, 0)), _const_spec(bias.shape), _const_spec((1, LANES))),
        name=name, compiler_params=_params("arbitrary", "arbitrary"))(do, q, kv, kv, kv, kv, lse, bias, sinks)


def _merge_fwd(yn, o, gs, ga, w_ssm, w_attn, w_out, h_in, g_post, *, name):
    t, d = h_in.shape
    tm = _tile(t, 256)

    def body(yn_ref, o_ref, gs_ref, ga_ref, ws_ref, wa_ref, wo_ref, hin_ref, gp_ref, ys_ref, ya_ref, mg_ref, mix_ref, hout_ref):
        ys = _dot(yn_ref[...], ws_ref[...], NN)
        ya = _dot(o_ref[...], wa_ref[...], NN)
        merged = (_sigmoid(gs_ref[...].astype(F32)) * ys + _sigmoid(ga_ref[...].astype(F32)) * ya).astype(BF16)
        mix = _dot(merged, wo_ref[...], NN)
        ys_ref[...] = ys.astype(BF16)
        ya_ref[...] = ya.astype(BF16)
        mg_ref[...] = merged
        mix_ref[...] = mix
        hout_ref[...] = _rms_residual(mix, hin_ref[...], gp_ref[...], 1.0)

    def row(w):
        return pl.BlockSpec((tm, w), lambda i: (i, 0))
    bshape = jax.ShapeDtypeStruct((t, d), BF16)
    fshape = jax.ShapeDtypeStruct((t, d), F32)
    return pl.pallas_call(
        body, out_shape=(bshape, bshape, bshape, fshape, fshape), grid=(t // tm,),
        in_specs=[row(yn.shape[1]), row(o.shape[1]), row(d), row(d), _resident_spec(w_ssm.shape), _resident_spec(w_attn.shape),
                  _resident_spec(w_out.shape), row(d), _const_spec((1, d))],
        out_specs=(row(d),) * 5, name=name, compiler_params=_params("parallel"))(yn, o, gs, ga, w_ssm, w_attn, w_out, h_in, g_post)


def _merge_bwd(dh, mix, g_post, gs, ga, ys, ya, w_ssm, w_attn, w_out, *, name):
    t, d = mix.shape
    tm = _tile(t, 256)
    d_ssm, d_attn = w_ssm.shape[0], w_attn.shape[0]

    def body(dh_ref, mix_ref, gp_ref, gs_ref, ga_ref, ys_ref, ya_ref, ws_ref, wa_ref, wo_ref,
             dmix_ref, dys_ref, dya_ref, dgs_ref, dga_ref, dyn_ref, do_ref, dgp_ref):
        @pl.when(pl.program_id(0) == 0)
        def _():
            dgp_ref[...] = jnp.zeros_like(dgp_ref)
        mv = mix_ref[...]
        dy = dh_ref[...]
        r = lax.rsqrt(jnp.mean(mv * mv, axis=-1, keepdims=True) + RMS_EPS)
        mhat = mv * r
        dyg = dy * gp_ref[...]
        dmix = (r * (dyg - mhat * jnp.mean(dyg * mhat, axis=-1, keepdims=True))).astype(BF16)
        dgp_ref[...] += jnp.sum(dy * mhat, axis=0, keepdims=True)
        dmix_ref[...] = dmix
        dmerged = _dot(dmix, wo_ref[...], NT)
        sgs = _sigmoid(gs_ref[...].astype(F32))
        sga = _sigmoid(ga_ref[...].astype(F32))
        dys = (dmerged * sgs).astype(BF16)
        dya = (dmerged * sga).astype(BF16)
        dys_ref[...] = dys
        dya_ref[...] = dya
        dgs_ref[...] = (dmerged * ys_ref[...].astype(F32) * sgs * (1.0 - sgs)).astype(BF16)
        dga_ref[...] = (dmerged * ya_ref[...].astype(F32) * sga * (1.0 - sga)).astype(BF16)
        dyn_ref[...] = _dot(dys, ws_ref[...], NT).astype(BF16)
        do_ref[...] = _dot(dya, wa_ref[...], NT).astype(BF16)

    def row(w):
        return pl.BlockSpec((tm, w), lambda i: (i, 0))

    def bshape(w):
        return jax.ShapeDtypeStruct((t, w), BF16)
    return pl.pallas_call(
        body, out_shape=(bshape(d),) * 5 + (bshape(d_ssm), bshape(d_attn), jax.ShapeDtypeStruct((1, d), F32)), grid=(t // tm,),
        in_specs=[row(d), row(d), _const_spec((1, d)), row(d), row(d), row(d), row(d),
                  _resident_spec(w_ssm.shape), _resident_spec(w_attn.shape), _resident_spec(w_out.shape)],
        out_specs=(row(d),) * 5 + (row(d_ssm), row(d_attn), _const_spec((1, d))),
        name=name, compiler_params=_params("arbitrary"))(dh, mix, g_post, gs, ga, ys, ya, w_ssm, w_attn, w_out)


def _adamw(w, g, m, v, *, name):
    r, c = w.shape
    tm = _tile(r, 256)
    c1 = 1.0 - ADAM_B1 ** ADAM_STEP
    c2 = 1.0 - ADAM_B2 ** ADAM_STEP

    def body(w_ref, g_ref, m_ref, v_ref, d_ref, mo_ref, vo_ref):
        gv = g_ref[...]
        mn = ADAM_B1 * m_ref[...] + (1.0 - ADAM_B1) * gv
        vn = ADAM_B2 * v_ref[...] + (1.0 - ADAM_B2) * (gv * gv)
        mo_ref[...] = mn
        vo_ref[...] = vn
        d_ref[...] = -ADAM_LR * ((mn / c1) / (jnp.sqrt(vn / c2) + ADAM_EPS) + ADAM_WD * w_ref[...])

    blk = pl.BlockSpec((tm, c), lambda i: (i, 0))
    shp = jax.ShapeDtypeStruct((r, c), F32)
    return pl.pallas_call(body, out_shape=(shp, shp, shp), grid=(r // tm,), in_specs=[blk] * 4, out_specs=(blk,) * 3,
                          name=name, compiler_params=_params("parallel"))(w, g, m, v)


def _position():
    return lax.axis_index("x"), lax.axis_index("y"), lax.axis_index("c")


def _gather_exchange(shards):
    na = len(shards)

    def plan(ins, outs, sems):
        send_sems, recv_sems, local_sems = sems
        x, y, c = _position()
        me, sibling = (x, y, c), (x, y, 1 - c)
        chips = [(1 - x, y), (x, 1 - y), (1 - x, 1 - y)]

        def slot(a, pos):
            return outs[a].at[4 * pos[0] + 2 * pos[1] + pos[2]]

        def copy(a, k, block, to, src=None):
            return pltpu.make_async_remote_copy(
                src_ref=slot(a, block) if src is None else src, dst_ref=slot(a, block),
                send_sem=send_sems.at[a, k], recv_sem=recv_sems.at[a, k], device_id=to, device_id_type=MESH)

        mine = [pltpu.make_async_copy(ins[a], slot(a, me), local_sems.at[a]) for a in range(na)]
        first = []
        for a in range(na):
            first.append(copy(a, 0, me, sibling, src=ins[a]))
            first += [copy(a, 1 + j, me, (*chip, c), src=ins[a]) for j, chip in enumerate(chips)]
        return me, sibling, chips, copy, mine, first

    def start(ins, outs, sems):
        *_, mine, first = plan(ins, outs, sems)
        for cp in mine + first:
            cp.start()

    def finish(ins, outs, sems):
        me, sibling, chips, copy, mine, first = plan(ins, outs, sems)
        c = me[2]
        passed = []
        for a in range(na):
            for j, chip in enumerate(chips):
                copy(a, 1 + j, (*chip, c), me).wait_recv()
                fwd = copy(a, 4 + j, (*chip, c), sibling)
                fwd.start()
                passed.append(fwd)
        for a in range(na):
            copy(a, 0, sibling, me).wait_recv()
            for j, chip in enumerate(chips):
                copy(a, 4 + j, (*chip, 1 - c), me).wait_recv()
        for cp in first + passed:
            cp.wait_send()
        for cp in mine:
            cp.wait()

    return _Exchange(list(shards), [jax.ShapeDtypeStruct((N_DEV,) + s.shape, s.dtype) for s in shards],
                     [pltpu.SemaphoreType.DMA((na, 7)), pltpu.SemaphoreType.DMA((na, 7)), pltpu.SemaphoreType.DMA((na,))],
                     start, finish)


def _scatter_exchange(arrays):
    na = len(arrays)

    def copies(ins, outs, sems):
        send_sems, recv_sems = sems
        x, y, c = _position()
        out = []
        for a in range(na):
            for k in range(7):
                flip = k + 1
                peer = (x ^ (flip >> 2), y ^ ((flip >> 1) & 1), c ^ (flip & 1))
                peer_block = 4 * peer[0] + 2 * peer[1] + peer[2]
                out.append(pltpu.make_async_remote_copy(
                    src_ref=ins[a].at[peer_block], dst_ref=outs[a].at[k],
                    send_sem=send_sems.at[a, k], recv_sem=recv_sems.at[a, k], device_id=peer, device_id_type=MESH))
        return out

    def start(ins, outs, sems):
        for cp in copies(ins, outs, sems):
            cp.start()

    def finish(ins, outs, sems):
        for cp in copies(ins, outs, sems):
            cp.wait()

    return _Exchange(list(arrays), [jax.ShapeDtypeStruct((7,) + s.shape[1:], s.dtype) for s in arrays],
                     [pltpu.SemaphoreType.DMA((na, 7)), pltpu.SemaphoreType.DMA((na, 7))], start, finish)


def _exchange_alone(side, *, name):
    n_in = len(side.arrays)
    n_out = len(side.out_shape)

    def body(*refs):
        ins, outs, sems = refs[:n_in], refs[n_in:n_in + n_out], refs[n_in + n_out:]
        side.start(ins, outs, sems)
        side.finish(ins, outs, sems)

    hbm = pl.BlockSpec(memory_space=pl.ANY)
    return pl.pallas_call(body, out_shape=tuple(side.out_shape), in_specs=[hbm] * n_in, out_specs=tuple([hbm] * n_out),
                          scratch_shapes=list(side.scratch), name=name)(*side.arrays)


def _reduce_blocks(own, recv, *, name):
    r, c = own.shape
    tm = _tile(r, 256)

    def body(own_ref, recv_ref, o_ref):
        acc = own_ref[...]
        for k in range(7):
            acc = acc + recv_ref[k].astype(F32)
        o_ref[...] = acc

    return pl.pallas_call(
        body, out_shape=jax.ShapeDtypeStruct((r, c), F32), grid=(r // tm,),
        in_specs=[pl.BlockSpec((tm, c), lambda i: (i, 0)), pl.BlockSpec((7, tm, c), lambda i: (0, i, 0))],
        out_specs=pl.BlockSpec((tm, c), lambda i: (i, 0)), name=name, compiler_params=_params("parallel"))(own, recv)


def _all_reduce_small(vec, *, name):
    r, c = vec.shape

    def body(v_ref, o_ref, buf, send_sems, recv_sems):
        x, y, c_ = _position()
        me = 4 * x + 2 * y + c_
        buf[me] = v_ref[...]
        copies = []
        for k in range(7):
            flip = k + 1
            peer = (x ^ (flip >> 2), y ^ ((flip >> 1) & 1), c_ ^ (flip & 1))
            cp = pltpu.make_async_remote_copy(
                src_ref=v_ref, dst_ref=buf.at[me], send_sem=send_sems.at[k], recv_sem=recv_sems.at[k],
                device_id=peer, device_id_type=MESH)
            cp.start()
            copies.append(cp)
        for cp in copies:
            cp.wait()
        acc = buf[0]
        for d in range(1, N_DEV):
            acc = acc + buf[d]
        o_ref[...] = acc

    vm = pl.BlockSpec(memory_space=pltpu.VMEM)
    return pl.pallas_call(
        body, out_shape=jax.ShapeDtypeStruct((r, c), F32), in_specs=[vm], out_specs=vm,
        scratch_shapes=[pltpu.VMEM((N_DEV, r, c), F32), pltpu.SemaphoreType.DMA((7,)), pltpu.SemaphoreType.DMA((7,))],
        name=name)(vec)


def _pad_lanes(v, width=LANES):
    return jnp.pad(v, ((0, 0), (0, width - v.shape[1])))


def kernel(x, ffn1_pre_g, ffn1_w_gate, ffn1_w_up, ffn1_w_down, ffn1_post_g, mix_pre_g, w_in, conv_w, conv_b, dt_bias, a_log, d_skip, ssm_norm_g, w_ssm_proj, attn_sinks, rel_bias_table, w_attn_proj, w_out, mix_post_g, ffn2_pre_g, ffn2_w_gate, ffn2_w_up, ffn2_w_down, ffn2_post_g, loss_target, m_ffn1_pre_g, m_ffn1_w_gate, m_ffn1_w_up, m_ffn1_w_down, m_ffn1_post_g, m_mix_pre_g, m_w_in, m_conv_w, m_conv_b, m_dt_bias, m_a_log, m_d_skip, m_ssm_norm_g, m_w_ssm_proj, m_attn_sinks, m_rel_bias_table, m_w_attn_proj, m_w_out, m_mix_post_g, m_ffn2_pre_g, m_ffn2_w_gate, m_ffn2_w_up, m_ffn2_w_down, m_ffn2_post_g, v_ffn1_pre_g, v_ffn1_w_gate, v_ffn1_w_up, v_ffn1_w_down, v_ffn1_post_g, v_mix_pre_g, v_w_in, v_conv_w, v_conv_b, v_dt_bias, v_a_log, v_d_skip, v_ssm_norm_g, v_w_ssm_proj, v_attn_sinks, v_rel_bias_table, v_w_attn_proj, v_w_out, v_mix_post_g, v_ffn2_pre_g, v_ffn2_w_gate, v_ffn2_w_up, v_ffn2_w_down, v_ffn2_post_g):
    args = dict(locals())
    weight_names = ['ffn1_pre_g', 'ffn1_w_gate', 'ffn1_w_up', 'ffn1_w_down', 'ffn1_post_g', 'mix_pre_g', 'w_in', 'conv_w', 'conv_b',
                    'dt_bias', 'a_log', 'd_skip', 'ssm_norm_g', 'w_ssm_proj', 'attn_sinks', 'rel_bias_table', 'w_attn_proj', 'w_out',
                    'mix_post_g', 'ffn2_pre_g', 'ffn2_w_gate', 'ffn2_w_up', 'ffn2_w_down', 'ffn2_post_g']
    col_sharded = ('ffn1_w_gate', 'ffn1_w_up', 'w_in', 'ffn2_w_gate', 'ffn2_w_up')
    row_sharded = ('ffn1_w_down', 'w_ssm_proj', 'w_attn_proj', 'w_out', 'ffn2_w_down')
    big = col_sharded + row_sharded

    bl, s_len, d = x.shape
    t = bl * s_len
    d_inner = ssm_norm_g.shape[1]
    n_heads = dt_bias.shape[1]
    gn = SSM_GROUPS * SSM_STATE
    conv_dim = d_inner + 2 * gn
    q_dim = ATTN_Q_HEADS * ATTN_HEAD_DIM
    kv_dim = ATTN_KV_HEADS * ATTN_HEAD_DIM

    def local_2d(name, a):
        a = a[0]
        return a.T if name in col_sharded else a

    ffn1_names = ('ffn1_w_gate', 'ffn1_w_up', 'ffn1_w_down')
    ffn2_names = ('ffn2_w_gate', 'ffn2_w_up', 'ffn2_w_down')
    mixer_names = ('w_ssm_proj', 'w_attn_proj', 'w_out')

    def shard(n):
        return local_2d(n, args[n]).astype(BF16)

    def rows(g):
        return g.reshape(N_DEV * g.shape[1], g.shape[2])

    first_names = ffn1_names[:2]
    full = {n: rows(g) for n, g in zip(first_names, _exchange_alone(_gather_exchange([shard(n) for n in first_names]), name="gather_ffn1"))}

    x2 = x.reshape(t, d)
    tgt2 = loss_target.reshape(t, d)

    (h1,), saved1, got_in, got_mixer = _ffn_forward(
        x2, ffn1_pre_g, full['ffn1_w_gate'], full['ffn1_w_up'], lambda got: rows(got[2]), ffn1_post_g, "ffn1",
        side_up=_gather_exchange([shard('w_in'), conv_w[0], shard('ffn1_w_down')]),
        side_down=_gather_exchange([shard(n) for n in mixer_names]))
    full['ffn1_w_down'] = rows(got_in[2])
    full.update({n: rows(g) for n, g in zip(mixer_names, got_mixer)})
    conv_w_full = jnp.transpose(got_in[1], (1, 0, 2)).reshape(SSM_CONV, conv_dim)

    win_t = rows(got_in[0])
    dt_lo = 2 * d + d_inner + conv_dim
    n_main = win_t.shape[0] - n_heads
    win_p = jnp.concatenate([win_t[:dt_lo], win_t[dt_lo + n_heads:], win_t[dt_lo:dt_lo + n_heads],
                             jnp.zeros((LANES - n_heads, d), BF16)], axis=0)
    off = {'gs': 0, 'ga': d, 'z': 2 * d, 'xbc': 2 * d + d_inner, 'q': dt_lo, 'kv': dt_lo + q_dim, 'dt': n_main}

    u, (gs, ga, z, xbc, q, kv, dt_raw) = _proj_all(
        h1, mix_pre_g, win_p,
        [(off['gs'], d, BF16), (off['ga'], d, BF16), (off['z'], d_inner, BF16), (off['xbc'], conv_dim, BF16),
         (off['q'], q_dim, BF16), (off['kv'], 2 * kv_dim, BF16), (off['dt'], LANES, F32)], name="mix_proj")

    dtb_p, alog_p, dsk_p, sinks_p = _pad_lanes(dt_bias), _pad_lanes(a_log), _pad_lanes(d_skip), _pad_lanes(attn_sinks)
    xc = _conv_fwd(xbc, conv_w_full, conv_b, bl, name="conv_fwd")
    (y, hprev), got_ffn2 = _ssd_fwd(xc, dt_raw, dtb_p, alog_p, dsk_p, bl, n_heads,
                                    side=_gather_exchange([shard(n) for n in ffn2_names]), name="ssd_fwd")
    full.update({n: rows(g) for n, g in zip(ffn2_names, got_ffn2)})
    yn = _gated_norm_fwd(y, z, ssm_norm_g, name="gated_norm_fwd")

    onehot = _bucket_onehot()
    bias = _small_mm_hi(rel_bias_table.T, onehot, NN, name="rel_bias").reshape(ATTN_Q_HEADS, CHUNK, 2 * CHUNK)
    o, lse = _attn_fwd(q, kv, bias, sinks_p, bl, name="attn_fwd")

    ys, ya, merged, mix, h2 = _merge_fwd(yn, o, gs, ga, full['w_ssm_proj'], full['w_attn_proj'], full['w_out'], h1, mix_post_g,
                                         name="merge_fwd")

    (dh3, loss_vec), saved2, _, _ = _ffn_forward(h2, ffn2_pre_g, full['ffn2_w_gate'], full['ffn2_w_up'], full['ffn2_w_down'],
                                                 ffn2_post_g, "ffn2", target=tgt2)
    loss = lax.psum(loss_vec[0, 0], ("x", "y", "c"))

    grads, own, wire, received = {}, {}, {}, {}
    dh2, grads['ffn2_pre_g'], grads['ffn2_post_g'], g32, g16, _ = _ffn_backward(
        dh3, saved2, ffn2_pre_g, full['ffn2_w_gate'], full['ffn2_w_up'], full['ffn2_w_down'], ffn2_post_g, "ffn2")
    own.update(zip(ffn2_names, map(_stack8, g32)))
    wire.update(zip(ffn2_names, map(_stack8, g16)))

    dmix, dys, dya, dgs, dga, dyn, do, grads['mix_post_g'] = _merge_bwd(
        dh2, mix, mix_post_g, gs, ga, ys, ya, full['w_ssm_proj'], full['w_attn_proj'], full['w_out'], name="merge_bwd")
    for n, (lhs, rhs) in zip(mixer_names, ((yn, dys), (o, dya), (merged, dmix))):
        g32_, g16_, _ = _mm_tn([lhs], rhs, name=f"d{n}")
        own[n], wire[n] = _stack8(g32_), _stack8(g16_)

    dq, dkv, dbias, dsinks = _attn_bwd(do, q, kv, lse, bias, sinks_p, bl, name="attn_bwd")
    d_table = _small_mm_hi(onehot, dbias.reshape(ATTN_Q_HEADS, -1), NT, name="rel_bias_bwd")

    dy, dz, grads['ssm_norm_g'] = _gated_norm_bwd(dyn, y, z, ssm_norm_g, name="gated_norm_bwd")

    first_group = ffn2_names + mixer_names
    (dxc, ddt_raw, ddtb, dalog, ddsk), got = _ssd_bwd(dy, y, xc, dt_raw, hprev, dtb_p, alog_p, dsk_p, bl, n_heads,
                                                      side=_scatter_exchange([wire[n] for n in first_group]), name="ssd_bwd")
    received.update(zip(first_group, got))
    dxbc, dconv_w8, grads['conv_b'] = _conv_bwd(dxc, xbc, conv_w_full, conv_b, bl, name="conv_bwd")

    wide32, wide16, _ = _mm_tn([dgs, dga, dz, dxbc, dq], u, name="dw_in")
    kv32, kv16, _ = _mm_tn([dkv], u, name="dw_in_kv")
    dt32, dt16, _ = _mm_tn([ddt_raw], u, name="dw_in_dt")

    def original_order(wide, kv_part, dt_part):
        return _stack8(jnp.concatenate([wide[:dt_lo], dt_part[:n_heads], wide[dt_lo:], kv_part], axis=0))
    own['w_in'], wire['w_in'] = original_order(wide32, kv32, dt32), original_order(wide16, kv16, dt16)
    own['conv_w'] = jnp.transpose(dconv_w8[:SSM_CONV].reshape(SSM_CONV, N_DEV, conv_dim // N_DEV), (1, 0, 2))

    segs = [(g_, 0, off[k_]) for g_, k_ in zip([dgs, dga, dz, dxbc, dq, dkv, ddt_raw], ('gs', 'ga', 'z', 'xbc', 'q', 'kv', 'dt'))]
    dh1, grads['mix_pre_g'], got = _mm_nn_rmsbwd(segs, [win_p], h1, mix_pre_g, dh2,
                                                 side=_scatter_exchange([wire['w_in'], own['conv_w']]), name="mix_du")
    received.update(zip(('w_in', 'conv_w'), got))

    dx2, grads['ffn1_pre_g'], grads['ffn1_post_g'], g32, _, got = _ffn_backward(
        dh1, saved1, ffn1_pre_g, full['ffn1_w_gate'], full['ffn1_w_up'], full['ffn1_w_down'], ffn1_post_g, "ffn1", chain=True)
    own.update(zip(ffn1_names, map(_stack8, g32)))
    received.update(zip(ffn1_names, got))

    me = 4 * lax.axis_index("x") + 2 * lax.axis_index("y") + lax.axis_index("c")

    def own_block(a):
        return lax.dynamic_index_in_dim(a, me, 0, keepdims=False)
    reduced = {n: _reduce_blocks(own_block(own[n]), received[n], name=f"reduce_{n}") for n in big}
    conv_sum = _reduce_blocks(own_block(own['conv_w']), received['conv_w'], name="reduce_conv_w")

    grads['dt_bias'], grads['a_log'], grads['d_skip'] = ddtb[:, :n_heads], dalog[:, :n_heads], ddsk[:, :n_heads]
    grads['attn_sinks'] = dsinks[:, :ATTN_Q_HEADS]
    grads['rel_bias_table'] = d_table
    small = [n for n in weight_names if n not in big and n != 'conv_w']
    flat = jnp.concatenate([grads[n].reshape(-1) for n in small])
    n_small = flat.shape[0]
    n_rows = -(-n_small // (8 * LANES)) * 8
    flat = jnp.pad(flat, (0, n_rows * LANES - n_small)).reshape(n_rows, LANES)
    summed = _all_reduce_small(flat, name="allreduce_small").reshape(-1)
    pos = 0
    for n in small:
        size = grads[n].size
        grads[n] = summed[pos:pos + size].reshape(args[n].shape)
        pos += size

    out_g, out_d, out_m, out_v = {}, {}, {}, {}
    for n in big:
        w2, m2, v2 = local_2d(n, args[n]), local_2d(n, args['m_' + n]), local_2d(n, args['v_' + n])
        dlt, mn, vn = _adamw(w2, reduced[n], m2, v2, name=f"adamw_{n}")

        def back(a, n=n):
            return (a.T if n in col_sharded else a)[None]
        out_g[n], out_d[n], out_m[n], out_v[n] = back(reduced[n]), back(dlt), back(mn), back(vn)

    def pack(prefix):
        vals = [(grads[n] if prefix == 'g' else args[prefix + n]).reshape(-1) for n in small]
        vals.append((conv_sum if prefix == 'g' else args[prefix + 'conv_w']).reshape(-1))
        flat_ = jnp.concatenate(vals)
        rows_ = -(-flat_.shape[0] // (8 * LANES)) * 8
        return jnp.pad(flat_, (0, rows_ * LANES - flat_.shape[0])).reshape(rows_, LANES)

    g_small = pack('g')
    d_small, m_small, v_small = _adamw(pack(''), g_small, pack('m_'), pack('v_'), name="adamw_small")
    pos = 0
    for n in small + ['conv_w']:
        shape = args[n].shape
        size = int(np.prod(shape))
        for dst, src in ((out_g, g_small), (out_d, d_small), (out_m, m_small), (out_v, v_small)):
            dst[n] = src.reshape(-1)[pos:pos + size].reshape(shape)
        pos += size

    grad_x = dx2.reshape(bl, s_len, d)
    return (loss, grad_x, *[out_g[n] for n in weight_names], *[out_d[n] for n in weight_names],
            *[out_m[n] for n in weight_names], *[out_v[n] for n in weight_names])
```

```python
import functools
import math

import numpy as np
import jax
import jax.numpy as jnp
from jax import lax
from jax.experimental import pallas as pl
from jax.experimental.pallas import tpu as pltpu

F32 = jnp.float32
BF16 = jnp.bfloat16
MESH = pl.DeviceIdType.MESH
N_DEV = 8

SSM_HEAD_DIM = 64
SSM_GROUPS = 4
SSM_STATE = 128
SSM_CONV = 4
CHUNK = 128
ATTN_HEAD_DIM = 64
ATTN_Q_HEADS = 16
ATTN_KV_HEADS = 4
REL_BUCKETS = 32
REL_MAX_DISTANCE = 128
RMS_EPS = 1e-6
FFN_RESIDUAL_WEIGHT = 0.5
ADAM_LR, ADAM_B1, ADAM_B2, ADAM_EPS, ADAM_WD, ADAM_STEP = 0.001, 0.9, 0.999, 1e-08, 0.01, 10

LANES = 128
VMEM_LIMIT_BYTES = 56 * 1024 * 1024
FFN_COL_TILE = 1408

NEG_INF = float("-inf")


def _params(*sem):
    return pltpu.CompilerParams(dimension_semantics=sem, vmem_limit_bytes=VMEM_LIMIT_BYTES)


def _tile(n, pref, mult=8):
    if n <= pref:
        return n
    t = (pref // mult) * mult
    while t >= mult:
        if n % t == 0:
            return t
        t -= mult
    return n


def _sigmoid(x):
    return 1.0 / (1.0 + jnp.exp(-x))


def _dot(a, b, dims):
    return lax.dot_general(a, b, (dims, ((), ())), preferred_element_type=F32)


NN = ((1,), (0,))
NT = ((1,), (1,))
TN = ((0,), (0,))


def _dot_hi(a, b, dims=NN):
    return lax.dot_general(a, b, (dims, ((), ())), preferred_element_type=F32, precision=lax.Precision.HIGHEST)


def _const_spec(shape):
    nd = len(shape)
    return pl.BlockSpec(shape, lambda *_: (0,) * nd)


def _resident_spec(shape):
    nd = len(shape)
    return pl.BlockSpec(shape, lambda *_: (0,) * nd, pipeline_mode=pl.Buffered(1))


class _Exchange:
    def __init__(self, arrays, out_shape, scratch, start, finish):
        self.arrays, self.out_shape, self.scratch, self.start, self.finish = arrays, out_shape, scratch, start, finish


def _hosted_call(body, *, grid, in_specs, out_specs, out_shape, scratch_shapes, operands, side, name):
    in_specs, out_specs, out_shape, scratch_shapes = list(in_specs), list(out_specs), list(out_shape), list(scratch_shapes)
    sem = ("arbitrary",) * len(grid)
    if side is None:
        outs = pl.pallas_call(body, out_shape=tuple(out_shape), grid=grid, in_specs=in_specs, out_specs=tuple(out_specs),
                              scratch_shapes=scratch_shapes, name=name, compiler_params=_params(*sem))(*operands)
        return tuple(outs), ()
    n_in, n_out, n_scr = len(in_specs), len(out_shape), len(scratch_shapes)
    s_in, s_out = len(side.arrays), len(side.out_shape)

    def wrapped(*refs):
        refs = list(refs)
        main_in, side_in = refs[:n_in], refs[n_in:n_in + s_in]
        o0 = n_in + s_in
        main_out, side_out = refs[o0:o0 + n_out], refs[o0 + n_out:o0 + n_out + s_out]
        c0 = o0 + n_out + s_out
        main_scr, side_scr = refs[c0:c0 + n_scr], refs[c0 + n_scr:]
        ids = [pl.program_id(ax) for ax in range(len(grid))]
        first = functools.reduce(jnp.logical_and, [i == 0 for i in ids])
        last = functools.reduce(jnp.logical_and, [i == g - 1 for i, g in zip(ids, grid)])

        @pl.when(first)
        def _():
            side.start(side_in, side_out, side_scr)

        body(*main_in, *main_out, *main_scr)

        @pl.when(last)
        def _():
            side.finish(side_in, side_out, side_scr)

    hbm = pl.BlockSpec(memory_space=pl.ANY)
    outs = pl.pallas_call(
        wrapped, out_shape=tuple(out_shape + list(side.out_shape)), grid=grid,
        in_specs=in_specs + [hbm] * s_in, out_specs=tuple(out_specs + [hbm] * s_out),
        scratch_shapes=scratch_shapes + list(side.scratch), name=name, compiler_params=_params(*sem))(*operands, *side.arrays)
    return tuple(outs[:n_out]), tuple(outs[n_out:])


def _proj_all(h, g, w, segs, *, name):
    t, d = h.shape
    tm = _tile(t, 512)

    def body(h_ref, g_ref, w_ref, u_ref, *o_refs):
        hv = h_ref[...]
        r = lax.rsqrt(jnp.mean(hv * hv, axis=-1, keepdims=True) + RMS_EPS)
        uv = (hv * r * g_ref[...]).astype(BF16)
        u_ref[...] = uv
        for (row0, width, _), o_ref in zip(segs, o_refs):
            for c0, c1 in _col_chunks(width, 8 * LANES):
                o_ref[:, c0:c1] = _dot(uv, w_ref[row0 + c0:row0 + c1, :], NT).astype(o_ref.dtype)

    row = pl.BlockSpec((tm, d), lambda i: (i, 0))
    outs = pl.pallas_call(
        body, out_shape=(jax.ShapeDtypeStruct((t, d), BF16),) + tuple(jax.ShapeDtypeStruct((t, wd_), dt_) for _, wd_, dt_ in segs),
        grid=(t // tm,), in_specs=[row, _const_spec((1, d)), _resident_spec(w.shape)],
        out_specs=(row,) + tuple(pl.BlockSpec((tm, wd_), lambda i: (i, 0)) for _, wd_, _ in segs),
        name=name, compiler_params=_params("parallel"))(h, g, w)
    return outs[0], outs[1:]


def _mm_tn(a_list, b, *, tm=1408, tk=2048, side=None, name):
    t, n = b.shape
    tk = _tile(t, tk if len(a_list) == 1 else tk // 2)
    nk = t // tk
    widths = [a.shape[1] for a in a_list]
    tm = _tile(math.gcd(*widths), tm, LANES)
    assert all(w % tm == 0 for w in widths)
    starts = np.cumsum([0] + [w // tm for w in widths])
    nseg = len(a_list)

    def a_spec(s):
        lo, hi = int(starts[s]), int(starts[s + 1])

        def idx(i, k):
            active = jnp.logical_and(i >= lo, i < hi)
            return (jnp.where(active, k, 0), jnp.clip(i - lo, 0, hi - lo - 1))
        return pl.BlockSpec((tk, tm), idx)

    def body(*refs):
        a_refs, b_ref, o_ref, o16_ref, acc = refs[:nseg], refs[nseg], refs[nseg + 1], refs[nseg + 2], refs[nseg + 3]
        i, k = pl.program_id(0), pl.program_id(1)

        @pl.when(k == 0)
        def _():
            acc[...] = jnp.zeros_like(acc)

        bv = b_ref[...].astype(BF16)
        for s in range(nseg):
            lo, hi = int(starts[s]), int(starts[s + 1])

            @pl.when(jnp.logical_and(i >= lo, i < hi))
            def _(s=s):
                acc[...] += _dot(a_refs[s][...].astype(BF16), bv, TN)

        @pl.when(k == nk - 1)
        def _():
            o_ref[...] = acc[...]
            o16_ref[...] = acc[...].astype(BF16)

    rows = int(starts[-1]) * tm
    o_spec = pl.BlockSpec((tm, n), lambda i, k: (i, 0))
    (o32, o16), got = _hosted_call(
        body, out_shape=[jax.ShapeDtypeStruct((rows, n), F32), jax.ShapeDtypeStruct((rows, n), BF16)], grid=(int(starts[-1]), nk),
        in_specs=[a_spec(s) for s in range(nseg)] + [pl.BlockSpec((tk, n), lambda i, k: (k, 0))],
        out_specs=[o_spec, o_spec], scratch_shapes=[pltpu.VMEM((tm, n), F32)], operands=list(a_list) + [b], side=side, name=name)
    return o32, o16, got


def _mm_nn_rmsbwd(segs, weights, x, g, dres, *, tm=256, side=None, name):
    t, d = x.shape
    tm = _tile(t, tm)
    nseg, nw = len(segs), len(weights)

    def body(*refs):
        a_refs, w_refs = refs[:nseg], refs[nseg:nseg + nw]
        x_ref, g_ref, dres_ref, dx_ref, dg_ref = refs[nseg + nw:]

        @pl.when(pl.program_id(0) == 0)
        def _():
            dg_ref[...] = jnp.zeros_like(dg_ref)

        dn = None
        for s, (a, w_idx, row0) in enumerate(segs):
            part = _dot(a_refs[s][...].astype(BF16), w_refs[w_idx][row0:row0 + a.shape[1], :], NN)
            dn = part if dn is None else dn + part
        xv = x_ref[...]
        r = lax.rsqrt(jnp.mean(xv * xv, axis=-1, keepdims=True) + RMS_EPS)
        xhat = xv * r
        dyg = dn * g_ref[...]
        dx_ref[...] = dres_ref[...] + r * (dyg - xhat * jnp.mean(dyg * xhat, axis=-1, keepdims=True))
        dg_ref[...] += jnp.sum(dn * xhat, axis=0, keepdims=True)

    row = pl.BlockSpec((tm, d), lambda i: (i, 0))
    in_specs = [pl.BlockSpec((tm, a.shape[1]), lambda i: (i, 0)) for a, _, _ in segs]
    in_specs += [_resident_spec(w.shape) for w in weights] + [row, _const_spec((1, d)), row]
    (dx, dg), extra = _hosted_call(
        body, grid=(t // tm,), in_specs=in_specs, out_specs=[row, _const_spec((1, d))],
        out_shape=[jax.ShapeDtypeStruct((t, d), F32), jax.ShapeDtypeStruct((1, d), F32)], scratch_shapes=[],
        operands=[a for a, _, _ in segs] + list(weights) + [x, g, dres], side=side, name=name)
    return dx, dg, extra


def _col_chunks(width, chunk=4 * LANES):
    return [(c0, min(c0 + chunk, width)) for c0 in range(0, width, chunk)]


def _rms_fwd(x, g, *, name):
    t, d = x.shape
    tm = _tile(t, 512)

    def body(x_ref, g_ref, o_ref):
        xv = x_ref[...]
        r = lax.rsqrt(jnp.mean(xv * xv, axis=-1, keepdims=True) + RMS_EPS)
        o_ref[...] = (xv * r * g_ref[...]).astype(o_ref.dtype)

    row = pl.BlockSpec((tm, d), lambda i: (i, 0))
    return pl.pallas_call(body, out_shape=jax.ShapeDtypeStruct((t, d), BF16), grid=(t // tm,),
                          in_specs=[row, _const_spec((1, d))], out_specs=row, name=name, compiler_params=_params("parallel"))(x, g)


def _ffn_up(n, wgt, wut, *, side=None, name):
    t, d = n.shape
    f = wgt.shape[0]
    tm, tn = _tile(t, 512), _tile(f, FFN_COL_TILE, LANES)

    def body(n_ref, wg_ref, wu_ref, g_ref, u_ref, h_ref):
        nv = n_ref[...]
        gv = _dot(nv, wg_ref[...], NT)
        uv = _dot(nv, wu_ref[...], NT)
        g_ref[...] = gv.astype(BF16)
        u_ref[...] = uv.astype(BF16)
        h_ref[...] = (gv * _sigmoid(gv) * uv).astype(BF16)

    w_spec = pl.BlockSpec((tn, d), lambda j, i: (j, 0))
    o_spec = pl.BlockSpec((tm, tn), lambda j, i: (i, j))
    shp = jax.ShapeDtypeStruct((t, f), BF16)
    return _hosted_call(body, grid=(f // tn, t // tm), in_specs=[pl.BlockSpec((tm, d), lambda j, i: (i, 0)), w_spec, w_spec],
                        out_specs=[o_spec, o_spec, o_spec], out_shape=[shp, shp, shp], scratch_shapes=[], operands=[n, wgt, wut],
                        side=side, name=name)


def _rms_residual(acc, h, gp, weight):
    r = lax.rsqrt(jnp.mean(acc * acc, axis=-1, keepdims=True) + RMS_EPS)
    return h + weight * (acc * r * gp)


def _ffn_down(hid, wd, h_in, gp, *, target=None, side=None, name):
    t, f = hid.shape
    d = wd.shape[1]
    tm = _tile(t, 256)
    row = pl.BlockSpec((tm, d), lambda i: (i, 0))
    shp = jax.ShapeDtypeStruct((t, d), F32)
    in_specs = [pl.BlockSpec((tm, f), lambda i: (i, 0)), _resident_spec((f, d)), row, _const_spec((1, d))]

    if target is None:
        def body(hid_ref, wd_ref, hin_ref, gp_ref, f_ref, hout_ref):
            acc = _dot(hid_ref[...], wd_ref[...], NN)
            f_ref[...] = acc
            hout_ref[...] = _rms_residual(acc, hin_ref[...], gp_ref[...], FFN_RESIDUAL_WEIGHT)

        return _hosted_call(body, grid=(t // tm,), in_specs=in_specs, out_specs=[row, row], out_shape=[shp, shp], scratch_shapes=[],
                            operands=[hid, wd, h_in, gp], side=side, name=name)

    def body_loss(hid_ref, wd_ref, hin_ref, gp_ref, tgt_ref, f_ref, dh_ref, loss_ref):
        @pl.when(pl.program_id(0) == 0)
        def _():
            loss_ref[...] = jnp.zeros_like(loss_ref)
        acc = _dot(hid_ref[...], wd_ref[...], NN)
        f_ref[...] = acc
        e = _rms_residual(acc, hin_ref[...], gp_ref[...], FFN_RESIDUAL_WEIGHT) - tgt_ref[...]
        dh_ref[...] = e * (1.0 / d)
        per_row = jnp.sum(e * e, axis=1, keepdims=True) * (1.0 / d)
        loss_ref[...] += 0.5 * jnp.sum(per_row, axis=0, keepdims=True)

    return _hosted_call(body_loss, grid=(t // tm,), in_specs=in_specs + [row], out_specs=[row, row, _const_spec((1, LANES))],
                        out_shape=[shp, shp, jax.ShapeDtypeStruct((1, LANES), F32)], scratch_shapes=[],
                        operands=[hid, wd, h_in, gp, target], side=side, name=name)


def _post_bwd(dh, f, gp, weight, *, name):
    t, d = f.shape
    tm = _tile(t, 512)

    def body(dh_ref, f_ref, gp_ref, df_ref, dgp_ref):
        @pl.when(pl.program_id(0) == 0)
        def _():
            dgp_ref[...] = jnp.zeros_like(dgp_ref)
        fv = f_ref[...]
        dy = weight * dh_ref[...]
        r = lax.rsqrt(jnp.mean(fv * fv, axis=-1, keepdims=True) + RMS_EPS)
        fhat = fv * r
        dyg = dy * gp_ref[...]
        df_ref[...] = (r * (dyg - fhat * jnp.mean(dyg * fhat, axis=-1, keepdims=True))).astype(BF16)
        dgp_ref[...] += jnp.sum(dy * fhat, axis=0, keepdims=True)

    row = pl.BlockSpec((tm, d), lambda i: (i, 0))
    return pl.pallas_call(body, out_shape=(jax.ShapeDtypeStruct((t, d), BF16), jax.ShapeDtypeStruct((1, d), F32)), grid=(t // tm,),
                          in_specs=[row, row, _const_spec((1, d))], out_specs=(row, _const_spec((1, d))),
                          name=name, compiler_params=_params("arbitrary"))(dh, f, gp)


def _ffn_dhid(df, wd, g, u, *, name):
    t, d = df.shape
    f = wd.shape[0]
    tm, tn = _tile(t, 512), _tile(f, FFN_COL_TILE, LANES)

    def body(df_ref, wd_ref, g_ref, u_ref, dg_ref, du_ref):
        dh = _dot(df_ref[...], wd_ref[...], NT)
        gv = g_ref[...].astype(F32)
        uv = u_ref[...].astype(F32)
        sg = _sigmoid(gv)
        silu = gv * sg
        dg_ref[...] = (dh * uv * (sg + silu * (1.0 - sg))).astype(BF16)
        du_ref[...] = (dh * silu).astype(BF16)

    o_spec = pl.BlockSpec((tm, tn), lambda j, i: (i, j))
    shp = jax.ShapeDtypeStruct((t, f), BF16)
    return pl.pallas_call(body, out_shape=(shp, shp), grid=(f // tn, t // tm),
                          in_specs=[pl.BlockSpec((tm, d), lambda j, i: (i, 0)), pl.BlockSpec((tn, d), lambda j, i: (j, 0)), o_spec, o_spec],
                          out_specs=(o_spec, o_spec), name=name, compiler_params=_params("parallel", "arbitrary"))(df, wd, g, u)


def _ffn_forward(h_in, g_pre, wgt, wut, wd, g_post, tag, side_up=None, side_down=None, target=None):
    n = _rms_fwd(h_in, g_pre, name=f"{tag}_prenorm")
    (g, u, hid), got_up = _ffn_up(n, wgt, wut, side=side_up, name=f"{tag}_up")
    wd = wd(got_up) if callable(wd) else wd
    outs, got_down = _ffn_down(hid, wd, h_in, g_post, target=target, side=side_down, name=f"{tag}_down")
    return outs[1:], (h_in, n, g, u, hid, outs[0]), got_up, got_down


def _stack8(g):
    return g.reshape(N_DEV, g.shape[0] // N_DEV, g.shape[1])


def _ffn_backward(dh_out, saved, g_pre, wgt, wut, wd, g_post, tag, chain=False):
    h_in, n, g, u, hid, f = saved

    def side_of(grad16):
        return _scatter_exchange([_stack8(grad16)]) if chain else None

    df, dg_post = _post_bwd(dh_out, f, g_post, FFN_RESIDUAL_WEIGHT, name=f"{tag}_post_bwd")
    dgate, dup = _ffn_dhid(df, wd, g, u, name=f"{tag}_dhid")
    d_wd, d_wd16, _ = _mm_tn([hid], df, name=f"{tag}_dwd")
    d_wgt, d_wgt16, got_wd = _mm_tn([dgate], n, side=side_of(d_wd16), name=f"{tag}_dwg")
    d_wut, d_wut16, got_wg = _mm_tn([dup], n, side=side_of(d_wgt16), name=f"{tag}_dwu")
    dh_in, dg_pre, got_wu = _mm_nn_rmsbwd([(dgate, 0, 0), (dup, 1, 0)], [wgt, wut], h_in, g_pre, dh_out, side=side_of(d_wut16),
                                          name=f"{tag}_dn")
    received = (got_wg[0], got_wu[0], got_wd[0]) if chain else None
    return dh_in, dg_pre, dg_post, (d_wgt, d_wut, d_wd), (d_wgt16, d_wut16, d_wd16), received


CONV_ROWS = 128
HALO = 8


def _taps(w_ref):
    return [w_ref[k:k + 1, :] for k in range(SSM_CONV)]


def _conv_chunk(x_ref, xs, r0, taps, bias):
    xs[HALO + r0:HALO + r0 + CONV_ROWS, :] = x_ref[r0:r0 + CONV_ROWS, :].astype(F32)
    shifted = [xs[HALO + r0 - k:HALO + r0 - k + CONV_ROWS, :] for k in range(SSM_CONV)]
    pre = bias + shifted[0] * taps[SSM_CONV - 1]
    for k in range(1, SSM_CONV):
        pre = pre + shifted[k] * taps[SSM_CONV - 1 - k]
    return shifted, pre


def _fold_rows(a):
    return functools.reduce(jnp.add, [a[i:i + 8] for i in range(0, a.shape[0], 8)])


def _conv_fwd(xbc, conv_w, conv_b, bl, *, name):
    t, c = xbc.shape
    s = t // bl
    tc = LANES
    assert s % CONV_ROWS == 0

    def body(x_ref, w_ref, b_ref, o_ref, xs):
        taps, bias = _taps(w_ref), b_ref[...]
        xs[0:HALO, :] = jnp.zeros((HALO, tc), F32)
        for r0 in range(0, s, CONV_ROWS):
            _, pre = _conv_chunk(x_ref, xs, r0, taps, bias)
            o_ref[r0:r0 + CONV_ROWS, :] = (pre * _sigmoid(pre)).astype(o_ref.dtype)

    blk = pl.BlockSpec((s, tc), lambda b, j: (b, j))
    return pl.pallas_call(body, out_shape=jax.ShapeDtypeStruct((t, c), BF16), grid=(bl, c // tc),
                          in_specs=[blk, pl.BlockSpec((SSM_CONV, tc), lambda b, j: (0, j)), pl.BlockSpec((1, tc), lambda b, j: (0, j))],
                          out_specs=blk, scratch_shapes=[pltpu.VMEM((HALO + s, tc), F32)],
                          name=name, compiler_params=_params("parallel", "arbitrary"))(xbc, conv_w, conv_b)


def _conv_bwd(dxc, xbc, conv_w, conv_b, bl, *, name):
    t, c = xbc.shape
    s = t // bl
    tc = LANES

    def body(dy_ref, x_ref, w_ref, b_ref, dx_ref, dw_ref, db_ref, xs, dpre_s):
        @pl.when(pl.program_id(1) == 0)
        def _():
            dw_ref[...] = jnp.zeros_like(dw_ref)
            db_ref[...] = jnp.zeros_like(db_ref)

        taps, bias = _taps(w_ref), b_ref[...]
        zero8 = jnp.zeros((HALO, tc), F32)
        xs[0:HALO, :] = zero8
        dpre_s[s:s + HALO, :] = zero8
        sums = [zero8] * (SSM_CONV + 1)
        for r0 in range(0, s, CONV_ROWS):
            shifted, pre = _conv_chunk(x_ref, xs, r0, taps, bias)
            sg = _sigmoid(pre)
            dpre = dy_ref[r0:r0 + CONV_ROWS, :].astype(F32) * (sg * (1.0 + pre * (1.0 - sg)))
            dpre_s[r0:r0 + CONV_ROWS, :] = dpre
            sums = [acc + _fold_rows(dpre * sh) for acc, sh in zip(sums[:-1], shifted)] + [sums[-1] + _fold_rows(dpre)]
        for k in range(SSM_CONV):
            dw_ref[SSM_CONV - 1 - k:SSM_CONV - k, :] += jnp.sum(sums[k], axis=0, keepdims=True)
        db_ref[...] += jnp.sum(sums[-1], axis=0, keepdims=True)
        for r0 in range(0, s, CONV_ROWS):
            dx = dpre_s[r0:r0 + CONV_ROWS, :] * taps[SSM_CONV - 1]
            for k in range(1, SSM_CONV):
                dx = dx + dpre_s[r0 + k:r0 + k + CONV_ROWS, :] * taps[SSM_CONV - 1 - k]
            dx_ref[r0:r0 + CONV_ROWS, :] = dx.astype(dx_ref.dtype)

    blk = pl.BlockSpec((s, tc), lambda j, b: (b, j))
    return pl.pallas_call(
        body, out_shape=(jax.ShapeDtypeStruct((t, c), BF16), jax.ShapeDtypeStruct((8, c), F32), jax.ShapeDtypeStruct((1, c), F32)),
        grid=(c // tc, bl),
        in_specs=[blk, blk, pl.BlockSpec((SSM_CONV, tc), lambda j, b: (0, j)), pl.BlockSpec((1, tc), lambda j, b: (0, j))],
        out_specs=(blk, pl.BlockSpec((8, tc), lambda j, b: (0, j)), pl.BlockSpec((1, tc), lambda j, b: (0, j))),
        scratch_shapes=[pltpu.VMEM((HALO + s, tc), F32), pltpu.VMEM((s + HALO, tc), F32)],
        name=name, compiler_params=_params("parallel", "arbitrary"))(dxc, xbc, conv_w, conv_b)


def _softplus(x):
    return jnp.maximum(x, 0.0) + jnp.log1p(jnp.exp(-jnp.abs(x)))


def _hilo_dot(v, m_b, dims=NN):
    hi = v.astype(BF16)
    lo = (v - hi.astype(F32)).astype(BF16)
    return _dot(hi, m_b, dims) + _dot(lo, m_b, dims)


def _ssd_chunk_common(dtraw_ref, dtb_ref, alog_ref, dsk_ref, d_inner):
    q, p = CHUNK, SSM_HEAD_DIM
    shift = p.bit_length() - 1
    assert 1 << shift == p
    dt = _softplus(dtraw_ref[...] + dtb_ref[...])
    a = -jnp.exp(alog_ref[...])
    ii = lax.broadcasted_iota(jnp.int32, (q, q), 0)
    jj = lax.broadcasted_iota(jnp.int32, (q, q), 1)
    causal = ii >= jj
    tril = jnp.where(causal, 1.0, 0.0).astype(F32)
    triu = jnp.where(ii <= jj, 1.0, 0.0).astype(F32)
    a_cs = _dot_hi(tril, dt * a)
    a_cs_t = a_cs.T
    a_last = a_cs[q - 1:q, :]
    e_col = jnp.exp(a_cs)
    dec_end = jnp.exp(a_last - a_cs)
    head_of_col = lax.shift_right_logical(lax.broadcasted_iota(jnp.int32, (LANES, d_inner), 1), shift)
    spread = (lax.broadcasted_iota(jnp.int32, (LANES, d_inner), 0) == head_of_col).astype(BF16)
    wide = _hilo_dot(jnp.concatenate([dt, e_col, dec_end, jnp.broadcast_to(dsk_ref[...], (8, LANES))], axis=0), spread)
    return dict(dt=dt, a=a, a_cs=a_cs, a_cs_t=a_cs_t, a_last=a_last, dec_end=dec_end, causal=causal, triu=triu,
                dt_e=wide[:q], e_e=wide[q:2 * q], dec_e=wide[2 * q:3 * q], dsk_e=wide[3 * q:3 * q + 1])


def _fill_block_diag(bd_ref, src_ref, hpg, col0=0):
    q, p = CHUNK, SSM_HEAD_DIM
    for hh in range(hpg):
        bd_ref[hh * q:(hh + 1) * q, hh * p:(hh + 1) * p] = src_ref[:, col0 + hh * p:col0 + (hh + 1) * p]


def _lane_onehot(h):
    return (lax.broadcasted_iota(jnp.int32, (1, LANES), 1) == h).astype(F32)


def _ssd_fwd(xc, dt_raw, dt_bias, a_log, d_skip, bl, n_heads, *, side=None, name):
    t = xc.shape[0]
    q, p, nst, grp = CHUNK, SSM_HEAD_DIM, SSM_STATE, SSM_GROUPS
    d_inner = n_heads * p
    hpg = n_heads // grp
    gw = hpg * p
    nc = t // bl // q
    assert d_inner % (grp * nst) == 0 and nst == LANES

    def body(xs_ref, b_ref, c_ref, dtraw_ref, dtb_ref, alog_ref, dsk_ref, y_ref, hprev_ref, state, m_all, x_bd, xdt_s):
        @pl.when(jnp.logical_and(pl.program_id(0) == 0, pl.program_id(1) == 0))
        def _():
            x_bd[...] = jnp.zeros_like(x_bd)

        @pl.when(pl.program_id(1) == 0)
        def _():
            state[...] = jnp.zeros_like(state)

        cm = _ssd_chunk_common(dtraw_ref, dtb_ref, alog_ref, dsk_ref, d_inner)
        for g in range(grp):
            cols = slice(g * gw, (g + 1) * gw)
            bg = b_ref[:, g * nst:(g + 1) * nst]
            cg = c_ref[:, g * nst:(g + 1) * nst]
            scores = _dot(cg, bg, NT)
            for hh in range(hpg):
                h = g * hpg + hh
                seg = cm['a_cs'][:, h:h + 1] - cm['a_cs_t'][h:h + 1, :]
                m_all[:, hh * q:(hh + 1) * q] = (scores * jnp.exp(jnp.where(cm['causal'], seg, NEG_INF))).astype(BF16)
            xs = xs_ref[:, cols].astype(F32)
            xdt = xs * cm['dt_e'][:, cols]
            xdt_s[...] = xdt.astype(BF16)
            _fill_block_diag(x_bd, xdt_s, hpg)
            hprev = state[g]
            hprev_ref[g] = hprev
            y = _dot(m_all[...], x_bd[...], NN) + cm['e_e'][:, cols] * _dot(cg, hprev.astype(BF16), NT)
            y_ref[:, cols] = y + cm['dsk_e'][:, cols] * xs
            st = _dot((xdt * cm['dec_e'][:, cols]).astype(BF16), bg, TN)
            for hh in range(hpg):
                h = g * hpg + hh
                rows = slice(hh * p, (hh + 1) * p)
                state[g, rows, :] = jnp.exp(cm['a_last'][:, h:h + 1]) * hprev[rows] + st[rows]

    gn = grp * nst

    def rowmap(b, c):
        return b * nc + c
    vec = pl.BlockSpec((1, LANES), lambda b, c: (0, 0))
    return _hosted_call(
        body,
        out_shape=[jax.ShapeDtypeStruct((t, d_inner), F32), jax.ShapeDtypeStruct((t // q, grp, gw, nst), F32)],
        grid=(bl, nc),
        in_specs=[pl.BlockSpec((q, d_inner), lambda b, c: (rowmap(b, c), 0)),
                  pl.BlockSpec((q, gn), lambda b, c: (rowmap(b, c), d_inner // gn)),
                  pl.BlockSpec((q, gn), lambda b, c: (rowmap(b, c), d_inner // gn + 1)),
                  pl.BlockSpec((q, LANES), lambda b, c: (rowmap(b, c), 0)), vec, vec, vec],
        out_specs=[pl.BlockSpec((q, d_inner), lambda b, c: (rowmap(b, c), 0)),
                   pl.BlockSpec((None, grp, gw, nst), lambda b, c: (rowmap(b, c), 0, 0, 0))],
        scratch_shapes=[pltpu.VMEM((grp, gw, nst), F32), pltpu.VMEM((q, hpg * q), BF16), pltpu.VMEM((hpg * q, gw), BF16),
                        pltpu.VMEM((q, gw), BF16)],
        operands=[xc, xc, xc, dt_raw, dt_bias, a_log, d_skip], side=side, name=name)


def _ssd_bwd(dy, y, xc, dt_raw, hprev_all, dt_bias, a_log, d_skip, bl, n_heads, *, side=None, name):
    t, c_dim = xc.shape
    q, p, nst, grp = CHUNK, SSM_HEAD_DIM, SSM_STATE, SSM_GROUPS
    d_inner = n_heads * p
    hpg = n_heads // grp
    gw = hpg * p
    nc = t // bl // q
    gn = grp * nst
    shift = p.bit_length() - 1

    def body(dy_ref, y_ref, xs_ref, b_ref, c_ref, dtraw_ref, hprev_ref, dtb_ref, alog_ref, dsk_ref,
             dxc_ref, ddtraw_ref, ddtb_ref, dalog_ref, ddsk_ref, dstate, mt_all, x_bd, dy_bd, xdt_s):
        @pl.when(jnp.logical_and(pl.program_id(0) == 0, pl.program_id(1) == 0))
        def _():
            ddtb_ref[...] = jnp.zeros_like(ddtb_ref)
            dalog_ref[...] = jnp.zeros_like(dalog_ref)
            ddsk_ref[...] = jnp.zeros_like(ddsk_ref)
            x_bd[...] = jnp.zeros_like(x_bd)
            dy_bd[...] = jnp.zeros_like(dy_bd)

        @pl.when(pl.program_id(1) == 0)
        def _():
            dstate[...] = jnp.zeros_like(dstate)

        cm = _ssd_chunk_common(dtraw_ref, dtb_ref, alog_ref, dsk_ref, d_inner)
        causal = cm['causal']
        upper = cm['triu'] > 0.5
        seg_row = lax.shift_right_logical(lax.broadcasted_iota(jnp.int32, (gw, LANES), 0), shift)
        seg_lane = lax.broadcasted_iota(jnp.int32, (gw, LANES), 1)
        sums = jnp.zeros((5 * q, LANES), F32)
        state_dot = jnp.zeros((1, LANES), F32)
        for g in range(grp):
            cols = slice(g * gw, (g + 1) * gw)
            seg_sum = (seg_row + g * hpg == seg_lane).astype(BF16)
            bg = b_ref[:, g * nst:(g + 1) * nst]
            cg = c_ref[:, g * nst:(g + 1) * nst]
            scores_t = _dot(bg, cg, NT)
            xs = xs_ref[:, cols].astype(F32)
            xdt = xs * cm['dt_e'][:, cols]
            xdt_s[...] = xdt.astype(BF16)
            _fill_block_diag(x_bd, xdt_s, hpg)
            _fill_block_diag(dy_bd, dy_ref, hpg, g * gw)
            dy_b = dy_ref[:, cols]
            dyf = dy_b.astype(F32)
            dm_all = _dot(dy_b, x_bd[...], NT)
            dscores = jnp.zeros((q, q), F32)
            for hh in range(hpg):
                h = g * hpg + hh
                blk = slice(hh * q, (hh + 1) * q)
                seg = cm['a_cs'][:, h:h + 1] - cm['a_cs_t'][h:h + 1, :]
                mt_all[:, blk] = (scores_t * jnp.exp(jnp.where(upper, -seg, NEG_INF))).astype(BF16)
                dscores = dscores + dm_all[:, blk] * jnp.exp(jnp.where(causal, seg, NEG_INF))
            hprev = hprev_ref[g]
            hprev_b = hprev.astype(BF16)
            dhn = dstate[g]
            dhn_b = dhn.astype(BF16)
            e_e, dec_e = cm['e_e'][:, cols], cm['dec_e'][:, cols]
            y_scan = y_ref[:, cols] - cm['dsk_e'][:, cols] * xs
            dye_b = (dyf * e_e).astype(BF16)
            dcg = _dot(dye_b, hprev_b, NN)
            dhp = _dot(dye_b, cg, TN)
            bdh = _dot(bg, dhn_b, NT)
            dbg = _dot((xdt * dec_e).astype(BF16), dhn_b, NN)
            dx_diag = _dot(mt_all[...], dy_bd[...], NN)
            dx = dec_e * bdh + dx_diag
            ds_b = dscores.astype(BF16)
            dcg = dcg + _dot(ds_b, bg, NN)
            dbg = dbg + _dot(ds_b, cg, TN)
            x_rounded = xdt_s[...].astype(F32)
            sums = sums + _hilo_dot(jnp.concatenate([dyf * y_scan, xdt * bdh, x_rounded * dx_diag, dx * xs, dyf * xs], axis=0), seg_sum)
            state_dot = state_dot + jnp.sum(_hilo_dot(dhn * hprev, seg_sum, TN), axis=0, keepdims=True)
            dxc_ref[:, cols] = (dx * cm['dt_e'][:, cols] + cm['dsk_e'][:, cols] * dyf).astype(dxc_ref.dtype)
            dxc_ref[:, d_inner + g * nst:d_inner + (g + 1) * nst] = dbg.astype(dxc_ref.dtype)
            dxc_ref[:, d_inner + gn + g * nst:d_inner + gn + (g + 1) * nst] = dcg.astype(dxc_ref.dtype)
            for hh in range(hpg):
                h = g * hpg + hh
                rows = slice(hh * p, (hh + 1) * p)
                dstate[g, rows, :] = jnp.exp(cm['a_last'][:, h:h + 1]) * dhn[rows] + dhp[rows]
        s_y, s_end, s_diag, s_dt, s_skip = (sums[k * q:(k + 1) * q] for k in range(5))
        dt, a, dec_end = cm['dt'], cm['a'], cm['dec_end']
        last_row = (lax.broadcasted_iota(jnp.int32, (q, 1), 0) == q - 1).astype(F32)
        da_last = jnp.sum(dec_end * s_end, axis=0, keepdims=True) + jnp.exp(cm['a_last']) * state_dot
        da = s_y - dec_end * s_end - s_diag + last_row * da_last
        ddta = _dot_hi(cm['triu'], da)
        ddt = s_dt + ddta * a
        d_a = jnp.sum(ddta * dt, axis=0, keepdims=True)
        ddt_raw = ddt * _sigmoid(dtraw_ref[...] + dtb_ref[...])
        ddtraw_ref[...] = ddt_raw
        ddtb_ref[...] += jnp.sum(ddt_raw, axis=0, keepdims=True)
        dalog_ref[...] += d_a * a
        ddsk_ref[...] += jnp.sum(s_skip, axis=0, keepdims=True)

    def rowmap(b, c):
        return b * nc + (nc - 1 - c)
    vec = pl.BlockSpec((1, LANES), lambda b, c: (0, 0))
    vec_shape = jax.ShapeDtypeStruct((1, LANES), F32)
    return _hosted_call(
        body,
        out_shape=[jax.ShapeDtypeStruct((t, c_dim), BF16), jax.ShapeDtypeStruct((t, LANES), F32), vec_shape, vec_shape, vec_shape],
        grid=(bl, nc),
        in_specs=[pl.BlockSpec((q, d_inner), lambda b, c: (rowmap(b, c), 0)),
                  pl.BlockSpec((q, d_inner), lambda b, c: (rowmap(b, c), 0)),
                  pl.BlockSpec((q, d_inner), lambda b, c: (rowmap(b, c), 0)),
                  pl.BlockSpec((q, gn), lambda b, c: (rowmap(b, c), d_inner // gn)),
                  pl.BlockSpec((q, gn), lambda b, c: (rowmap(b, c), d_inner // gn + 1)),
                  pl.BlockSpec((q, LANES), lambda b, c: (rowmap(b, c), 0)),
                  pl.BlockSpec((None, grp, gw, nst), lambda b, c: (rowmap(b, c), 0, 0, 0)), vec, vec, vec],
        out_specs=[pl.BlockSpec((q, c_dim), lambda b, c: (rowmap(b, c), 0)),
                   pl.BlockSpec((q, LANES), lambda b, c: (rowmap(b, c), 0)), vec, vec, vec],
        scratch_shapes=[pltpu.VMEM((grp, gw, nst), F32), pltpu.VMEM((q, hpg * q), BF16),
                        pltpu.VMEM((hpg * q, gw), BF16), pltpu.VMEM((hpg * q, gw), BF16), pltpu.VMEM((q, gw), BF16)],
        operands=[dy, y, xc, xc, xc, dt_raw, hprev_all, dt_bias, a_log, d_skip], side=side, name=name)


def _gated_norm_fwd(y, z, ng, *, name):
    t, d = y.shape
    tm = _tile(t, 256)
    gw = d // SSM_GROUPS

    def body(y_ref, z_ref, ng_ref, o_ref):
        for g in range(SSM_GROUPS):
            sl = slice(g * gw, (g + 1) * gw)
            zv = z_ref[:, sl].astype(F32)
            yg = y_ref[:, sl] * (zv * _sigmoid(zv))
            r = lax.rsqrt(jnp.mean(yg * yg, axis=-1, keepdims=True) + RMS_EPS)
            o_ref[:, sl] = (yg * r * ng_ref[:, sl]).astype(o_ref.dtype)

    row = pl.BlockSpec((tm, d), lambda i: (i, 0))
    return pl.pallas_call(body, out_shape=jax.ShapeDtypeStruct((t, d), BF16), grid=(t // tm,),
                          in_specs=[row, row, _const_spec((1, d))], out_specs=row, name=name, compiler_params=_params("parallel"))(y, z, ng)


def _gated_norm_bwd(dyn, y, z, ng, *, name):
    t, d = y.shape
    tm = _tile(t, 256)
    gw = d // SSM_GROUPS

    def body(dyn_ref, y_ref, z_ref, ng_ref, dy_ref, dz_ref, dng_ref):
        @pl.when(pl.program_id(0) == 0)
        def _():
            dng_ref[...] = jnp.zeros_like(dng_ref)
        for g in range(SSM_GROUPS):
            sl = slice(g * gw, (g + 1) * gw)
            zv = z_ref[:, sl].astype(F32)
            yv = y_ref[:, sl]
            sg = _sigmoid(zv)
            sz = zv * sg
            yg = yv * sz
            r = lax.rsqrt(jnp.mean(yg * yg, axis=-1, keepdims=True) + RMS_EPS)
            yhat = yg * r
            dn = dyn_ref[:, sl].astype(F32)
            dyg_n = dn * ng_ref[:, sl]
            dyg = r * (dyg_n - yhat * jnp.mean(dyg_n * yhat, axis=-1, keepdims=True))
            dy_ref[:, sl] = (dyg * sz).astype(dy_ref.dtype)
            dz_ref[:, sl] = (dyg * yv * (sg * (1.0 + zv * (1.0 - sg)))).astype(dz_ref.dtype)
            dng_ref[:, sl] += jnp.sum(dn * yhat, axis=0, keepdims=True)

    row = pl.BlockSpec((tm, d), lambda i: (i, 0))
    shp = jax.ShapeDtypeStruct((t, d), BF16)
    return pl.pallas_call(body, out_shape=(shp, shp, jax.ShapeDtypeStruct((1, d), F32)), grid=(t // tm,),
                          in_specs=[row, row, row, _const_spec((1, d))], out_specs=(row, row, _const_spec((1, d))),
                          name=name, compiler_params=_params("arbitrary"))(dyn, y, z, ng)


def _bucket_onehot():
    blk = CHUNK
    qi = jnp.arange(blk)[:, None]
    kj = jnp.arange(2 * blk)[None, :]
    dist = jnp.maximum(qi + blk - kj, 0)
    max_exact = REL_BUCKETS // 2
    d = jnp.maximum(dist, 1).astype(F32)
    large = max_exact + (jnp.log(d / max_exact) / math.log(REL_MAX_DISTANCE / max_exact) * (REL_BUCKETS - max_exact)).astype(jnp.int32)
    large = jnp.minimum(large, REL_BUCKETS - 1)
    bucket = jnp.where(dist < max_exact, dist, large).reshape(-1)
    return (bucket[None, :] == jnp.arange(REL_BUCKETS)[:, None]).astype(F32)


def _small_mm_hi(a, b, dims, *, name):
    def body(a_ref, b_ref, o_ref):
        o_ref[...] = _dot_hi(a_ref[...], b_ref[...], dims)
    n = b.shape[0] if dims == NT else b.shape[1]
    return pl.pallas_call(body, out_shape=jax.ShapeDtypeStruct((a.shape[0], n), F32), name=name)(a, b)


def _attn_band_mask_t(n, rep):
    blk = CHUNK
    jj = lax.broadcasted_iota(jnp.int32, (2 * blk, rep * blk), 0)
    ii = lax.broadcasted_iota(jnp.int32, (2 * blk, rep * blk), 1) & (blk - 1)
    dist = ii + blk - jj
    in_window = jnp.logical_and(dist >= 0, dist < blk)
    return jnp.logical_and(in_window, jnp.logical_or(jj >= blk, n > 0))


def _sink_row(sink_ref, heads):
    return jnp.concatenate([jnp.broadcast_to(sink_ref[:, h:h + 1], (1, CHUNK)) for h in heads], axis=1)


def _attn_fwd(q, kv, bias_t, sinks, bl, *, name):
    t, qd = q.shape
    blk, hd = CHUNK, ATTN_HEAD_DIM
    kvd = ATTN_KV_HEADS * hd
    rep = ATTN_Q_HEADS // ATTN_KV_HEADS
    nb = t // bl // blk
    scale = hd ** -0.5

    def body(q_ref, kp_ref, kc_ref, vp_ref, vc_ref, bias_ref, sink_ref, o_ref, lse_ref):
        n = pl.program_id(1)
        mask = _attn_band_mask_t(n, rep)
        for kvh in range(ATTN_KV_HEADS):
            ks = slice(kvh * hd, (kvh + 1) * hd)
            heads = range(kvh * rep, (kvh + 1) * rep)
            qs = jnp.concatenate([q_ref[:, h * hd:(h + 1) * hd] for h in heads], axis=0)
            kk = jnp.concatenate([kp_ref[:, ks], kc_ref[:, ks]], axis=0)
            vv = jnp.concatenate([vp_ref[:, ks], vc_ref[:, ks]], axis=0)
            s = jnp.where(mask, _dot(kk, qs, NT) * scale + bias_ref[kvh], NEG_INF)
            sink = _sink_row(sink_ref, heads)
            m = jnp.maximum(jnp.max(s, axis=0, keepdims=True), sink)
            p = jnp.exp(s - m)
            den = jnp.sum(p, axis=0, keepdims=True) + jnp.exp(sink - m)
            o = _dot((p * (1.0 / den)).astype(BF16), vv, TN)
            lse = m + jnp.log(den)
            for r, h in enumerate(heads):
                o_ref[:, h * hd:(h + 1) * hd] = o[r * blk:(r + 1) * blk].astype(o_ref.dtype)
                lse_ref[h:h + 1, :] = lse[:, r * blk:(r + 1) * blk]

    def cur(b, n):
        return b * nb + n

    def prev(b, n):
        return b * nb + jnp.maximum(n - 1, 0)
    return pl.pallas_call(
        body, out_shape=(jax.ShapeDtypeStruct((t, qd), BF16), jax.ShapeDtypeStruct((t // blk * ATTN_Q_HEADS, blk), F32)), grid=(bl, nb),
        in_specs=[pl.BlockSpec((blk, qd), lambda b, n: (cur(b, n), 0)),
                  pl.BlockSpec((blk, kvd), lambda b, n: (prev(b, n), 0)), pl.BlockSpec((blk, kvd), lambda b, n: (cur(b, n), 0)),
                  pl.BlockSpec((blk, kvd), lambda b, n: (prev(b, n), 1)), pl.BlockSpec((blk, kvd), lambda b, n: (cur(b, n), 1)),
                  _const_spec(bias_t.shape), _const_spec((1, LANES))],
        out_specs=(pl.BlockSpec((blk, qd), lambda b, n: (cur(b, n), 0)),
                   pl.BlockSpec((ATTN_Q_HEADS, blk), lambda b, n: (cur(b, n), 0))),
        name=name, compiler_params=_params("parallel", "arbitrary"))(q, kv, kv, kv, kv, bias_t, sinks)


def _attn_bwd(do, q, kv, lse, bias_t, sinks, bl, *, name):
    t, qd = q.shape
    blk, hd = CHUNK, ATTN_HEAD_DIM
    kvd = ATTN_KV_HEADS * hd
    rep = ATTN_Q_HEADS // ATTN_KV_HEADS
    s_len = t // bl
    nb = s_len // blk
    scale = hd ** -0.5

    def body(do_ref, q_ref, kp_ref, kc_ref, vp_ref, vc_ref, lse_ref, bias_ref, sink_ref, dq_ref, dkv_ref, dbias_ref, dsink_ref):
        n = pl.program_id(1)

        @pl.when(jnp.logical_and(pl.program_id(0) == 0, n == 0))
        def _():
            dbias_ref[...] = jnp.zeros_like(dbias_ref)
            dsink_ref[...] = jnp.zeros_like(dsink_ref)

        mask = _attn_band_mask_t(n, rep)
        r_cur = pl.multiple_of(n * blk, blk)
        r_prev = pl.multiple_of(jnp.maximum(n - 1, 0) * blk, blk)
        dsink = jnp.zeros((1, LANES), F32)
        for kvh in range(ATTN_KV_HEADS):
            ks = slice(kvh * hd, (kvh + 1) * hd)
            heads = range(kvh * rep, (kvh + 1) * rep)
            qs = jnp.concatenate([q_ref[:, h * hd:(h + 1) * hd] for h in heads], axis=0)
            dos = jnp.concatenate([do_ref[:, h * hd:(h + 1) * hd] for h in heads], axis=0)
            kk = jnp.concatenate([kp_ref[:, ks], kc_ref[:, ks]], axis=0)
            vv = jnp.concatenate([vp_ref[:, ks], vc_ref[:, ks]], axis=0)
            lse = jnp.concatenate([lse_ref[h:h + 1, :] for h in heads], axis=1)
            p = jnp.exp(jnp.where(mask, _dot(kk, qs, NT) * scale + bias_ref[kvh], NEG_INF) - lse)
            dp = _dot(vv, dos, NT)
            delta = jnp.sum(p * dp, axis=0, keepdims=True)
            ds = p * (dp - delta)
            dsink_row = jnp.exp(_sink_row(sink_ref, heads) - lse) * delta
            dbias_ref[kvh] += ds
            ds_b = ds.astype(BF16)
            dq_s = _dot(ds_b, kk, TN) * scale
            dkk = _dot(ds_b, qs, NN) * scale
            dvv = _dot(p.astype(BF16), dos, NN)
            for r, h in enumerate(heads):
                dq_ref[:, h * hd:(h + 1) * hd] = dq_s[r * blk:(r + 1) * blk].astype(dq_ref.dtype)
                dsink = dsink - jnp.sum(dsink_row[:, r * blk:(r + 1) * blk], axis=1, keepdims=True) * _lane_onehot(h)
            vs = slice(kvd + kvh * hd, kvd + (kvh + 1) * hd)
            dkv_ref[pl.ds(r_cur, blk), ks] = dkk[blk:]
            dkv_ref[pl.ds(r_cur, blk), vs] = dvv[blk:]

            @pl.when(n > 0)
            def _():
                dkv_ref[pl.ds(r_prev, blk), ks] += dkk[:blk]
                dkv_ref[pl.ds(r_prev, blk), vs] += dvv[:blk]
        dsink_ref[...] += dsink

    def cur(b, n):
        return b * nb + n

    def prev(b, n):
        return b * nb + jnp.maximum(n - 1, 0)
    qspec = pl.BlockSpec((blk, qd), lambda b, n: (cur(b, n), 0))
    return pl.pallas_call(
        body,
        out_shape=(jax.ShapeDtypeStruct((t, qd), BF16), jax.ShapeDtypeStruct((t, 2 * kvd), F32),
                   jax.ShapeDtypeStruct(bias_t.shape, F32), jax.ShapeDtypeStruct((1, LANES), F32)),
        grid=(bl, nb),
        in_specs=[qspec, qspec,
                  pl.BlockSpec((blk, kvd), lambda b, n: (prev(b, n), 0)), pl.BlockSpec((blk, kvd), lambda b, n: (cur(b, n), 0)),
                  pl.BlockSpec((blk, kvd), lambda b, n: (prev(b, n), 1)), pl.BlockSpec((blk, kvd), lambda b, n: (cur(b, n), 1)),
                  pl.BlockSpec((ATTN_Q_HEADS, blk), lambda b, n: (cur(b, n), 0)), _const_spec(bias_t.shape), _const_spec((1, LANES))],
        out_specs=(qspec, pl.BlockSpec((s_len, 2 * kvd), lambda b, n: (b, 0)), _const_spec(bias_t.shape), _const_spec((1, LANES))),
        name=name, compiler_params=_params("arbitrary", "arbitrary"))(do, q, kv, kv, kv, kv, lse, bias_t, sinks)


def _merge_fwd(yn, o, gs, ga, w_ssm, w_attn, w_out, h_in, g_post, *, name):
    t, d = h_in.shape
    tm = _tile(t, 256)

    def body(yn_ref, o_ref, gs_ref, ga_ref, ws_ref, wa_ref, wo_ref, hin_ref, gp_ref, ys_ref, ya_ref, mg_ref, mix_ref, hout_ref):
        ys = _dot(yn_ref[...], ws_ref[...], NN)
        ya = _dot(o_ref[...], wa_ref[...], NN)
        merged = (_sigmoid(gs_ref[...].astype(F32)) * ys + _sigmoid(ga_ref[...].astype(F32)) * ya).astype(BF16)
        mix = _dot(merged, wo_ref[...], NN)
        ys_ref[...] = ys.astype(BF16)
        ya_ref[...] = ya.astype(BF16)
        mg_ref[...] = merged
        mix_ref[...] = mix
        hout_ref[...] = _rms_residual(mix, hin_ref[...], gp_ref[...], 1.0)

    def row(w):
        return pl.BlockSpec((tm, w), lambda i: (i, 0))
    bshape = jax.ShapeDtypeStruct((t, d), BF16)
    fshape = jax.ShapeDtypeStruct((t, d), F32)
    return pl.pallas_call(
        body, out_shape=(bshape, bshape, bshape, fshape, fshape), grid=(t // tm,),
        in_specs=[row(yn.shape[1]), row(o.shape[1]), row(d), row(d), _resident_spec(w_ssm.shape), _resident_spec(w_attn.shape),
                  _resident_spec(w_out.shape), row(d), _const_spec((1, d))],
        out_specs=(row(d),) * 5, name=name, compiler_params=_params("parallel"))(yn, o, gs, ga, w_ssm, w_attn, w_out, h_in, g_post)


def _merge_bwd(dh, mix, g_post, gs, ga, ys, ya, w_ssm, w_attn, w_out, *, name):
    t, d = mix.shape
    tm = _tile(t, 256)
    d_ssm, d_attn = w_ssm.shape[0], w_attn.shape[0]

    def body(dh_ref, mix_ref, gp_ref, gs_ref, ga_ref, ys_ref, ya_ref, ws_ref, wa_ref, wo_ref,
             dmix_ref, dys_ref, dya_ref, dgs_ref, dga_ref, dyn_ref, do_ref, dgp_ref):
        @pl.when(pl.program_id(0) == 0)
        def _():
            dgp_ref[...] = jnp.zeros_like(dgp_ref)
        mv = mix_ref[...]
        dy = dh_ref[...]
        r = lax.rsqrt(jnp.mean(mv * mv, axis=-1, keepdims=True) + RMS_EPS)
        mhat = mv * r
        dyg = dy * gp_ref[...]
        dmix = (r * (dyg - mhat * jnp.mean(dyg * mhat, axis=-1, keepdims=True))).astype(BF16)
        dgp_ref[...] += jnp.sum(dy * mhat, axis=0, keepdims=True)
        dmix_ref[...] = dmix
        dmerged = _dot(dmix, wo_ref[...], NT)
        sgs = _sigmoid(gs_ref[...].astype(F32))
        sga = _sigmoid(ga_ref[...].astype(F32))
        dys = (dmerged * sgs).astype(BF16)
        dya = (dmerged * sga).astype(BF16)
        dys_ref[...] = dys
        dya_ref[...] = dya
        dgs_ref[...] = (dmerged * ys_ref[...].astype(F32) * sgs * (1.0 - sgs)).astype(BF16)
        dga_ref[...] = (dmerged * ya_ref[...].astype(F32) * sga * (1.0 - sga)).astype(BF16)
        dyn_ref[...] = _dot(dys, ws_ref[...], NT).astype(BF16)
        do_ref[...] = _dot(dya, wa_ref[...], NT).astype(BF16)

    def row(w):
        return pl.BlockSpec((tm, w), lambda i: (i, 0))

    def bshape(w):
        return jax.ShapeDtypeStruct((t, w), BF16)
    return pl.pallas_call(
        body, out_shape=(bshape(d),) * 5 + (bshape(d_ssm), bshape(d_attn), jax.ShapeDtypeStruct((1, d), F32)), grid=(t // tm,),
        in_specs=[row(d), row(d), _const_spec((1, d)), row(d), row(d), row(d), row(d),
                  _resident_spec(w_ssm.shape), _resident_spec(w_attn.shape), _resident_spec(w_out.shape)],
        out_specs=(row(d),) * 5 + (row(d_ssm), row(d_attn), _const_spec((1, d))),
        name=name, compiler_params=_params("arbitrary"))(dh, mix, g_post, gs, ga, ys, ya, w_ssm, w_attn, w_out)


def _adamw(w, g, m, v, *, name):
    r, c = w.shape
    tm = _tile(r, 256)
    c1 = 1.0 - ADAM_B1 ** ADAM_STEP
    c2 = 1.0 - ADAM_B2 ** ADAM_STEP

    def body(w_ref, g_ref, m_ref, v_ref, d_ref, mo_ref, vo_ref):
        gv = g_ref[...]
        mn = ADAM_B1 * m_ref[...] + (1.0 - ADAM_B1) * gv
        vn = ADAM_B2 * v_ref[...] + (1.0 - ADAM_B2) * (gv * gv)
        mo_ref[...] = mn
        vo_ref[...] = vn
        d_ref[...] = -ADAM_LR * ((mn / c1) / (jnp.sqrt(vn / c2) + ADAM_EPS) + ADAM_WD * w_ref[...])

    blk = pl.BlockSpec((tm, c), lambda i: (i, 0))
    shp = jax.ShapeDtypeStruct((r, c), F32)
    return pl.pallas_call(body, out_shape=(shp, shp, shp), grid=(r // tm,), in_specs=[blk] * 4, out_specs=(blk,) * 3,
                          name=name, compiler_params=_params("parallel"))(w, g, m, v)


def _position():
    return lax.axis_index("x"), lax.axis_index("y"), lax.axis_index("c")


def _gather_exchange(shards):
    na = len(shards)

    def plan(ins, outs, sems):
        send_sems, recv_sems, local_sems = sems
        x, y, c = _position()
        me, sibling = (x, y, c), (x, y, 1 - c)
        chips = [(1 - x, y), (x, 1 - y), (1 - x, 1 - y)]

        def slot(a, pos):
            return outs[a].at[4 * pos[0] + 2 * pos[1] + pos[2]]

        def copy(a, k, block, to, src=None):
            return pltpu.make_async_remote_copy(
                src_ref=slot(a, block) if src is None else src, dst_ref=slot(a, block),
                send_sem=send_sems.at[a, k], recv_sem=recv_sems.at[a, k], device_id=to, device_id_type=MESH)

        mine = [pltpu.make_async_copy(ins[a], slot(a, me), local_sems.at[a]) for a in range(na)]
        first = []
        for a in range(na):
            first.append(copy(a, 0, me, sibling, src=ins[a]))
            first += [copy(a, 1 + j, me, (*chip, c), src=ins[a]) for j, chip in enumerate(chips)]
        return me, sibling, chips, copy, mine, first

    def start(ins, outs, sems):
        *_, mine, first = plan(ins, outs, sems)
        for cp in mine + first:
            cp.start()

    def finish(ins, outs, sems):
        me, sibling, chips, copy, mine, first = plan(ins, outs, sems)
        c = me[2]
        passed = []
        for a in range(na):
            for j, chip in enumerate(chips):
                copy(a, 1 + j, (*chip, c), me).wait_recv()
                fwd = copy(a, 4 + j, (*chip, c), sibling)
                fwd.start()
                passed.append(fwd)
        for a in range(na):
            copy(a, 0, sibling, me).wait_recv()
            for j, chip in enumerate(chips):
                copy(a, 4 + j, (*chip, 1 - c), me).wait_recv()
        for cp in first + passed:
            cp.wait_send()
        for cp in mine:
            cp.wait()

    return _Exchange(list(shards), [jax.ShapeDtypeStruct((N_DEV,) + s.shape, s.dtype) for s in shards],
                     [pltpu.SemaphoreType.DMA((na, 7)), pltpu.SemaphoreType.DMA((na, 7)), pltpu.SemaphoreType.DMA((na,))],
                     start, finish)


def _scatter_exchange(arrays):
    na = len(arrays)

    def copies(ins, outs, sems):
        send_sems, recv_sems = sems
        x, y, c = _position()
        out = []
        for a in range(na):
            for k in range(7):
                flip = k + 1
                peer = (x ^ (flip >> 2), y ^ ((flip >> 1) & 1), c ^ (flip & 1))
                peer_block = 4 * peer[0] + 2 * peer[1] + peer[2]
                out.append(pltpu.make_async_remote_copy(
                    src_ref=ins[a].at[peer_block], dst_ref=outs[a].at[k],
                    send_sem=send_sems.at[a, k], recv_sem=recv_sems.at[a, k], device_id=peer, device_id_type=MESH))
        return out

    def start(ins, outs, sems):
        for cp in copies(ins, outs, sems):
            cp.start()

    def finish(ins, outs, sems):
        for cp in copies(ins, outs, sems):
            cp.wait()

    return _Exchange(list(arrays), [jax.ShapeDtypeStruct((7,) + s.shape[1:], s.dtype) for s in arrays],
                     [pltpu.SemaphoreType.DMA((na, 7)), pltpu.SemaphoreType.DMA((na, 7))], start, finish)


def _exchange_alone(side, *, name):
    n_in = len(side.arrays)
    n_out = len(side.out_shape)

    def body(*refs):
        ins, outs, sems = refs[:n_in], refs[n_in:n_in + n_out], refs[n_in + n_out:]
        side.start(ins, outs, sems)
        side.finish(ins, outs, sems)

    hbm = pl.BlockSpec(memory_space=pl.ANY)
    return pl.pallas_call(body, out_shape=tuple(side.out_shape), in_specs=[hbm] * n_in, out_specs=tuple([hbm] * n_out),
                          scratch_shapes=list(side.scratch), name=name)(*side.arrays)


def _reduce_blocks(own, recv, *, name):
    r, c = own.shape
    tm = _tile(r, 256)

    def body(own_ref, recv_ref, o_ref):
        acc = own_ref[...]
        for k in range(7):
            acc = acc + recv_ref[k].astype(F32)
        o_ref[...] = acc

    return pl.pallas_call(
        body, out_shape=jax.ShapeDtypeStruct((r, c), F32), grid=(r // tm,),
        in_specs=[pl.BlockSpec((tm, c), lambda i: (i, 0)), pl.BlockSpec((7, tm, c), lambda i: (0, i, 0))],
        out_specs=pl.BlockSpec((tm, c), lambda i: (i, 0)), name=name, compiler_params=_params("parallel"))(own, recv)


def _all_reduce_small(vec, *, name):
    r, c = vec.shape

    def body(v_ref, o_ref, buf, send_sems, recv_sems):
        x, y, c_ = _position()
        me = 4 * x + 2 * y + c_
        buf[me] = v_ref[...]
        copies = []
        for k in range(7):
            flip = k + 1
            peer = (x ^ (flip >> 2), y ^ ((flip >> 1) & 1), c_ ^ (flip & 1))
            cp = pltpu.make_async_remote_copy(
                src_ref=v_ref, dst_ref=buf.at[me], send_sem=send_sems.at[k], recv_sem=recv_sems.at[k],
                device_id=peer, device_id_type=MESH)
            cp.start()
            copies.append(cp)
        for cp in copies:
            cp.wait()
        acc = buf[0]
        for d in range(1, N_DEV):
            acc = acc + buf[d]
        o_ref[...] = acc

    vm = pl.BlockSpec(memory_space=pltpu.VMEM)
    return pl.pallas_call(
        body, out_shape=jax.ShapeDtypeStruct((r, c), F32), in_specs=[vm], out_specs=vm,
        scratch_shapes=[pltpu.VMEM((N_DEV, r, c), F32), pltpu.SemaphoreType.DMA((7,)), pltpu.SemaphoreType.DMA((7,))],
        name=name)(vec)


def _pad_lanes(v, width=LANES):
    return jnp.pad(v, ((0, 0), (0, width - v.shape[1])))


def kernel(x, ffn1_pre_g, ffn1_w_gate, ffn1_w_up, ffn1_w_down, ffn1_post_g, mix_pre_g, w_in, conv_w, conv_b, dt_bias, a_log, d_skip, ssm_norm_g, w_ssm_proj, attn_sinks, rel_bias_table, w_attn_proj, w_out, mix_post_g, ffn2_pre_g, ffn2_w_gate, ffn2_w_up, ffn2_w_down, ffn2_post_g, loss_target, m_ffn1_pre_g, m_ffn1_w_gate, m_ffn1_w_up, m_ffn1_w_down, m_ffn1_post_g, m_mix_pre_g, m_w_in, m_conv_w, m_conv_b, m_dt_bias, m_a_log, m_d_skip, m_ssm_norm_g, m_w_ssm_proj, m_attn_sinks, m_rel_bias_table, m_w_attn_proj, m_w_out, m_mix_post_g, m_ffn2_pre_g, m_ffn2_w_gate, m_ffn2_w_up, m_ffn2_w_down, m_ffn2_post_g, v_ffn1_pre_g, v_ffn1_w_gate, v_ffn1_w_up, v_ffn1_w_down, v_ffn1_post_g, v_mix_pre_g, v_w_in, v_conv_w, v_conv_b, v_dt_bias, v_a_log, v_d_skip, v_ssm_norm_g, v_w_ssm_proj, v_attn_sinks, v_rel_bias_table, v_w_attn_proj, v_w_out, v_mix_post_g, v_ffn2_pre_g, v_ffn2_w_gate, v_ffn2_w_up, v_ffn2_w_down, v_ffn2_post_g):
    args = dict(locals())
    weight_names = ['ffn1_pre_g', 'ffn1_w_gate', 'ffn1_w_up', 'ffn1_w_down', 'ffn1_post_g', 'mix_pre_g', 'w_in', 'conv_w', 'conv_b',
                    'dt_bias', 'a_log', 'd_skip', 'ssm_norm_g', 'w_ssm_proj', 'attn_sinks', 'rel_bias_table', 'w_attn_proj', 'w_out',
                    'mix_post_g', 'ffn2_pre_g', 'ffn2_w_gate', 'ffn2_w_up', 'ffn2_w_down', 'ffn2_post_g']
    col_sharded = ('ffn1_w_gate', 'ffn1_w_up', 'w_in', 'ffn2_w_gate', 'ffn2_w_up')
    row_sharded = ('ffn1_w_down', 'w_ssm_proj', 'w_attn_proj', 'w_out', 'ffn2_w_down')
    big = col_sharded + row_sharded

    bl, s_len, d = x.shape
    t = bl * s_len
    d_inner = ssm_norm_g.shape[1]
    n_heads = dt_bias.shape[1]
    gn = SSM_GROUPS * SSM_STATE
    conv_dim = d_inner + 2 * gn
    q_dim = ATTN_Q_HEADS * ATTN_HEAD_DIM
    kv_dim = ATTN_KV_HEADS * ATTN_HEAD_DIM

    def local_2d(name, a):
        a = a[0]
        return a.T if name in col_sharded else a

    ffn1_names = ('ffn1_w_gate', 'ffn1_w_up', 'ffn1_w_down')
    ffn2_names = ('ffn2_w_gate', 'ffn2_w_up', 'ffn2_w_down')
    mixer_names = ('w_ssm_proj', 'w_attn_proj', 'w_out')

    def shard(n):
        return local_2d(n, args[n]).astype(BF16)

    def rows(g):
        return g.reshape(N_DEV * g.shape[1], g.shape[2])

    first_names = ffn1_names[:2]
    full = {n: rows(g) for n, g in zip(first_names, _exchange_alone(_gather_exchange([shard(n) for n in first_names]), name="gather_ffn1"))}

    x2 = x.reshape(t, d)
    tgt2 = loss_target.reshape(t, d)

    (h1,), saved1, got_in, got_mixer = _ffn_forward(
        x2, ffn1_pre_g, full['ffn1_w_gate'], full['ffn1_w_up'], lambda got: rows(got[2]), ffn1_post_g, "ffn1",
        side_up=_gather_exchange([shard('w_in'), conv_w[0], shard('ffn1_w_down')]),
        side_down=_gather_exchange([shard(n) for n in mixer_names]))
    full['ffn1_w_down'] = rows(got_in[2])
    full.update({n: rows(g) for n, g in zip(mixer_names, got_mixer)})
    conv_w_full = jnp.transpose(got_in[1], (1, 0, 2)).reshape(SSM_CONV, conv_dim)

    win_t = rows(got_in[0])
    dt_lo = 2 * d + d_inner + conv_dim
    n_main = win_t.shape[0] - n_heads
    win_p = jnp.concatenate([win_t[:dt_lo], win_t[dt_lo + n_heads:], win_t[dt_lo:dt_lo + n_heads],
                             jnp.zeros((LANES - n_heads, d), BF16)], axis=0)
    off = {'gs': 0, 'ga': d, 'z': 2 * d, 'xbc': 2 * d + d_inner, 'q': dt_lo, 'kv': dt_lo + q_dim, 'dt': n_main}

    u, (gs, ga, z, xbc, q, kv, dt_raw) = _proj_all(
        h1, mix_pre_g, win_p,
        [(off['gs'], d, BF16), (off['ga'], d, BF16), (off['z'], d_inner, BF16), (off['xbc'], conv_dim, BF16),
         (off['q'], q_dim, BF16), (off['kv'], 2 * kv_dim, BF16), (off['dt'], LANES, F32)], name="mix_proj")

    dtb_p, alog_p, dsk_p, sinks_p = _pad_lanes(dt_bias), _pad_lanes(a_log), _pad_lanes(d_skip), _pad_lanes(attn_sinks)
    xc = _conv_fwd(xbc, conv_w_full, conv_b, bl, name="conv_fwd")
    (y, hprev), got_ffn2 = _ssd_fwd(xc, dt_raw, dtb_p, alog_p, dsk_p, bl, n_heads,
                                    side=_gather_exchange([shard(n) for n in ffn2_names]), name="ssd_fwd")
    full.update({n: rows(g) for n, g in zip(ffn2_names, got_ffn2)})
    yn = _gated_norm_fwd(y, z, ssm_norm_g, name="gated_norm_fwd")

    onehot = _bucket_onehot()
    rep = ATTN_Q_HEADS // ATTN_KV_HEADS
    bias = _small_mm_hi(rel_bias_table.T, onehot, NN, name="rel_bias")
    bias_t = jnp.transpose(bias.reshape(ATTN_KV_HEADS, rep, CHUNK, 2 * CHUNK), (0, 3, 1, 2)).reshape(ATTN_KV_HEADS, 2 * CHUNK, rep * CHUNK)
    o, lse = _attn_fwd(q, kv, bias_t, sinks_p, bl, name="attn_fwd")

    ys, ya, merged, mix, h2 = _merge_fwd(yn, o, gs, ga, full['w_ssm_proj'], full['w_attn_proj'], full['w_out'], h1, mix_post_g,
                                         name="merge_fwd")

    (dh3, loss_vec), saved2, _, _ = _ffn_forward(h2, ffn2_pre_g, full['ffn2_w_gate'], full['ffn2_w_up'], full['ffn2_w_down'],
                                                 ffn2_post_g, "ffn2", target=tgt2)
    loss = lax.psum(loss_vec[0, 0], ("x", "y", "c"))

    grads, own, wire, received = {}, {}, {}, {}
    dh2, grads['ffn2_pre_g'], grads['ffn2_post_g'], g32, g16, _ = _ffn_backward(
        dh3, saved2, ffn2_pre_g, full['ffn2_w_gate'], full['ffn2_w_up'], full['ffn2_w_down'], ffn2_post_g, "ffn2")
    own.update(zip(ffn2_names, map(_stack8, g32)))
    wire.update(zip(ffn2_names, map(_stack8, g16)))

    dmix, dys, dya, dgs, dga, dyn, do, grads['mix_post_g'] = _merge_bwd(
        dh2, mix, mix_post_g, gs, ga, ys, ya, full['w_ssm_proj'], full['w_attn_proj'], full['w_out'], name="merge_bwd")
    for n, (lhs, rhs) in zip(mixer_names, ((yn, dys), (o, dya), (merged, dmix))):
        g32_, g16_, _ = _mm_tn([lhs], rhs, name=f"d{n}")
        own[n], wire[n] = _stack8(g32_), _stack8(g16_)

    dq, dkv, dbias_t, dsinks = _attn_bwd(do, q, kv, lse, bias_t, sinks_p, bl, name="attn_bwd")
    dbias = jnp.transpose(dbias_t.reshape(ATTN_KV_HEADS, 2 * CHUNK, rep, CHUNK), (0, 2, 3, 1)).reshape(ATTN_Q_HEADS, -1)
    d_table = _small_mm_hi(onehot, dbias, NT, name="rel_bias_bwd")

    dy, dz, grads['ssm_norm_g'] = _gated_norm_bwd(dyn, y, z, ssm_norm_g, name="gated_norm_bwd")

    first_group = ffn2_names + mixer_names
    (dxc, ddt_raw, ddtb, dalog, ddsk), got = _ssd_bwd(dy, y, xc, dt_raw, hprev, dtb_p, alog_p, dsk_p, bl, n_heads,
                                                      side=_scatter_exchange([wire[n] for n in first_group]), name="ssd_bwd")
    received.update(zip(first_group, got))
    dxbc, dconv_w8, grads['conv_b'] = _conv_bwd(dxc, xbc, conv_w_full, conv_b, bl, name="conv_bwd")

    wide32, wide16, _ = _mm_tn([dgs, dga, dz, dxbc, dq], u, name="dw_in")
    kv32, kv16, _ = _mm_tn([dkv], u, name="dw_in_kv")
    dt32, dt16, _ = _mm_tn([ddt_raw], u, name="dw_in_dt")

    def original_order(wide, kv_part, dt_part):
        return _stack8(jnp.concatenate([wide[:dt_lo], dt_part[:n_heads], wide[dt_lo:], kv_part], axis=0))
    own['w_in'], wire['w_in'] = original_order(wide32, kv32, dt32), original_order(wide16, kv16, dt16)
    own['conv_w'] = jnp.transpose(dconv_w8[:SSM_CONV].reshape(SSM_CONV, N_DEV, conv_dim // N_DEV), (1, 0, 2))

    segs = [(g_, 0, off[k_]) for g_, k_ in zip([dgs, dga, dz, dxbc, dq, dkv, ddt_raw], ('gs', 'ga', 'z', 'xbc', 'q', 'kv', 'dt'))]
    dh1, grads['mix_pre_g'], got = _mm_nn_rmsbwd(segs, [win_p], h1, mix_pre_g, dh2,
                                                 side=_scatter_exchange([wire['w_in'], own['conv_w']]), name="mix_du")
    received.update(zip(('w_in', 'conv_w'), got))

    dx2, grads['ffn1_pre_g'], grads['ffn1_post_g'], g32, _, got = _ffn_backward(
        dh1, saved1, ffn1_pre_g, full['ffn1_w_gate'], full['ffn1_w_up'], full['ffn1_w_down'], ffn1_post_g, "ffn1", chain=True)
    own.update(zip(ffn1_names, map(_stack8, g32)))
    received.update(zip(ffn1_names, got))

    me = 4 * lax.axis_index("x") + 2 * lax.axis_index("y") + lax.axis_index("c")

    def own_block(a):
        return lax.dynamic_index_in_dim(a, me, 0, keepdims=False)
    reduced = {n: _reduce_blocks(own_block(own[n]), received[n], name=f"reduce_{n}") for n in big}
    conv_sum = _reduce_blocks(own_block(own['conv_w']), received['conv_w'], name="reduce_conv_w")

    grads['dt_bias'], grads['a_log'], grads['d_skip'] = ddtb[:, :n_heads], dalog[:, :n_heads], ddsk[:, :n_heads]
    grads['attn_sinks'] = dsinks[:, :ATTN_Q_HEADS]
    grads['rel_bias_table'] = d_table
    small = [n for n in weight_names if n not in big and n != 'conv_w']
    flat = jnp.concatenate([grads[n].reshape(-1) for n in small])
    n_small = flat.shape[0]
    n_rows = -(-n_small // (8 * LANES)) * 8
    flat = jnp.pad(flat, (0, n_rows * LANES - n_small)).reshape(n_rows, LANES)
    summed = _all_reduce_small(flat, name="allreduce_small").reshape(-1)
    pos = 0
    for n in small:
        size = grads[n].size
        grads[n] = summed[pos:pos + size].reshape(args[n].shape)
        pos += size

    out_g, out_d, out_m, out_v = {}, {}, {}, {}
    for n in big:
        w2, m2, v2 = local_2d(n, args[n]), local_2d(n, args['m_' + n]), local_2d(n, args['v_' + n])
        dlt, mn, vn = _adamw(w2, reduced[n], m2, v2, name=f"adamw_{n}")

        def back(a, n=n):
            return (a.T if n in col_sharded else a)[None]
        out_g[n], out_d[n], out_m[n], out_v[n] = back(reduced[n]), back(dlt), back(mn), back(vn)

    def pack(prefix):
        vals = [(grads[n] if prefix == 'g' else args[prefix + n]).reshape(-1) for n in small]
        vals.append((conv_sum if prefix == 'g' else args[prefix + 'conv_w']).reshape(-1))
        flat_ = jnp.concatenate(vals)
        rows_ = -(-flat_.shape[0] // (8 * LANES)) * 8
        return jnp.pad(flat_, (0, rows_ * LANES - flat_.shape[0])).reshape(rows_, LANES)

    g_small = pack('g')
    d_small, m_small, v_small = _adamw(pack(''), g_small, pack('m_'), pack('v_'), name="adamw_small")
    pos = 0
    for n in small + ['conv_w']:
        shape = args[n].shape
        size = int(np.prod(shape))
        for dst, src in ((out_g, g_small), (out_d, d_small), (out_m, m_small), (out_v, v_small)):
            dst[n] = src.reshape(-1)[pos:pos + size].reshape(shape)
        pos += size

    grad_x = dx2.reshape(bl, s_len, d)
    return (loss, grad_x, *[out_g[n] for n in weight_names], *[out_d[n] for n in weight_names],
            *[out_m[n] for n in weight_names], *[out_v[n] for n in weight_names])
```

```python
import functools
import math

import numpy as np
import jax
import jax.numpy as jnp
from jax import lax
from jax.experimental import pallas as pl
from jax.experimental.pallas import tpu as pltpu

F32 = jnp.float32
BF16 = jnp.bfloat16
MESH = pl.DeviceIdType.MESH
N_DEV = 8

SSM_HEAD_DIM = 64
SSM_GROUPS = 4
SSM_STATE = 128
SSM_CONV = 4
CHUNK = 128
ATTN_HEAD_DIM = 64
ATTN_Q_HEADS = 16
ATTN_KV_HEADS = 4
REL_BUCKETS = 32
REL_MAX_DISTANCE = 128
RMS_EPS = 1e-6
FFN_RESIDUAL_WEIGHT = 0.5
ADAM_LR, ADAM_B1, ADAM_B2, ADAM_EPS, ADAM_WD, ADAM_STEP = 0.001, 0.9, 0.999, 1e-08, 0.01, 10

LANES = 128
VMEM_LIMIT_BYTES = 56 * 1024 * 1024
FFN_COL_TILE = 1408
SSD_HEAD_BATCH = 4

NEG_INF = float("-inf")


def _params(*sem):
    return pltpu.CompilerParams(dimension_semantics=sem, vmem_limit_bytes=VMEM_LIMIT_BYTES)


def _tile(n, pref, mult=8):
    if n <= pref:
        return n
    t = (pref // mult) * mult
    while t >= mult:
        if n % t == 0:
            return t
        t -= mult
    return n


def _sigmoid(x):
    return 1.0 / (1.0 + jnp.exp(-x))


def _dot(a, b, dims):
    return lax.dot_general(a, b, (dims, ((), ())), preferred_element_type=F32)


NN = ((1,), (0,))
NT = ((1,), (1,))
TN = ((0,), (0,))


def _dot_hi(a, b, dims=NN):
    return lax.dot_general(a, b, (dims, ((), ())), preferred_element_type=F32, precision=lax.Precision.HIGHEST)


def _const_spec(shape):
    nd = len(shape)
    return pl.BlockSpec(shape, lambda *_: (0,) * nd)


def _resident_spec(shape):
    nd = len(shape)
    return pl.BlockSpec(shape, lambda *_: (0,) * nd, pipeline_mode=pl.Buffered(1))


class _Exchange:
    def __init__(self, arrays, out_shape, scratch, start, finish):
        self.arrays, self.out_shape, self.scratch, self.start, self.finish = arrays, out_shape, scratch, start, finish


def _hosted_call(body, *, grid, in_specs, out_specs, out_shape, scratch_shapes, operands, side, name):
    in_specs, out_specs, out_shape, scratch_shapes = list(in_specs), list(out_specs), list(out_shape), list(scratch_shapes)
    sem = ("arbitrary",) * len(grid)
    if side is None:
        outs = pl.pallas_call(body, out_shape=tuple(out_shape), grid=grid, in_specs=in_specs, out_specs=tuple(out_specs),
                              scratch_shapes=scratch_shapes, name=name, compiler_params=_params(*sem))(*operands)
        return tuple(outs), ()
    n_in, n_out, n_scr = len(in_specs), len(out_shape), len(scratch_shapes)
    s_in, s_out = len(side.arrays), len(side.out_shape)

    def wrapped(*refs):
        refs = list(refs)
        main_in, side_in = refs[:n_in], refs[n_in:n_in + s_in]
        o0 = n_in + s_in
        main_out, side_out = refs[o0:o0 + n_out], refs[o0 + n_out:o0 + n_out + s_out]
        c0 = o0 + n_out + s_out
        main_scr, side_scr = refs[c0:c0 + n_scr], refs[c0 + n_scr:]
        ids = [pl.program_id(ax) for ax in range(len(grid))]
        first = functools.reduce(jnp.logical_and, [i == 0 for i in ids])
        last = functools.reduce(jnp.logical_and, [i == g - 1 for i, g in zip(ids, grid)])

        @pl.when(first)
        def _():
            side.start(side_in, side_out, side_scr)

        body(*main_in, *main_out, *main_scr)

        @pl.when(last)
        def _():
            side.finish(side_in, side_out, side_scr)

    hbm = pl.BlockSpec(memory_space=pl.ANY)
    outs = pl.pallas_call(
        wrapped, out_shape=tuple(out_shape + list(side.out_shape)), grid=grid,
        in_specs=in_specs + [hbm] * s_in, out_specs=tuple(out_specs + [hbm] * s_out),
        scratch_shapes=scratch_shapes + list(side.scratch), name=name, compiler_params=_params(*sem))(*operands, *side.arrays)
    return tuple(outs[:n_out]), tuple(outs[n_out:])


def _proj_all(h, g, w, segs, *, name):
    t, d = h.shape
    tm = _tile(t, 512)

    def body(h_ref, g_ref, w_ref, u_ref, *o_refs):
        hv = h_ref[...]
        r = lax.rsqrt(jnp.mean(hv * hv, axis=-1, keepdims=True) + RMS_EPS)
        uv = (hv * r * g_ref[...]).astype(BF16)
        u_ref[...] = uv
        for (row0, width, _), o_ref in zip(segs, o_refs):
            for c0, c1 in _col_chunks(width, 8 * LANES):
                o_ref[:, c0:c1] = _dot(uv, w_ref[row0 + c0:row0 + c1, :], NT).astype(o_ref.dtype)

    row = pl.BlockSpec((tm, d), lambda i: (i, 0))
    outs = pl.pallas_call(
        body, out_shape=(jax.ShapeDtypeStruct((t, d), BF16),) + tuple(jax.ShapeDtypeStruct((t, wd_), dt_) for _, wd_, dt_ in segs),
        grid=(t // tm,), in_specs=[row, _const_spec((1, d)), _resident_spec(w.shape)],
        out_specs=(row,) + tuple(pl.BlockSpec((tm, wd_), lambda i: (i, 0)) for _, wd_, _ in segs),
        name=name, compiler_params=_params("parallel"))(h, g, w)
    return outs[0], outs[1:]


def _mm_tn(a_list, b, *, tm=1408, tk=2048, side=None, name):
    t, n = b.shape
    tk = _tile(t, tk if len(a_list) == 1 else tk // 2)
    nk = t // tk
    widths = [a.shape[1] for a in a_list]
    tm = _tile(math.gcd(*widths), tm, LANES)
    assert all(w % tm == 0 for w in widths)
    starts = np.cumsum([0] + [w // tm for w in widths])
    nseg = len(a_list)

    def a_spec(s):
        lo, hi = int(starts[s]), int(starts[s + 1])

        def idx(i, k):
            active = jnp.logical_and(i >= lo, i < hi)
            return (jnp.where(active, k, 0), jnp.clip(i - lo, 0, hi - lo - 1))
        return pl.BlockSpec((tk, tm), idx)

    def body(*refs):
        a_refs, b_ref, o_ref, o16_ref, acc = refs[:nseg], refs[nseg], refs[nseg + 1], refs[nseg + 2], refs[nseg + 3]
        i, k = pl.program_id(0), pl.program_id(1)

        @pl.when(k == 0)
        def _():
            acc[...] = jnp.zeros_like(acc)

        bv = b_ref[...].astype(BF16)
        for s in range(nseg):
            lo, hi = int(starts[s]), int(starts[s + 1])

            @pl.when(jnp.logical_and(i >= lo, i < hi))
            def _(s=s):
                acc[...] += _dot(a_refs[s][...].astype(BF16), bv, TN)

        @pl.when(k == nk - 1)
        def _():
            o_ref[...] = acc[...]
            o16_ref[...] = acc[...].astype(BF16)

    rows = int(starts[-1]) * tm
    o_spec = pl.BlockSpec((tm, n), lambda i, k: (i, 0))
    (o32, o16), got = _hosted_call(
        body, out_shape=[jax.ShapeDtypeStruct((rows, n), F32), jax.ShapeDtypeStruct((rows, n), BF16)], grid=(int(starts[-1]), nk),
        in_specs=[a_spec(s) for s in range(nseg)] + [pl.BlockSpec((tk, n), lambda i, k: (k, 0))],
        out_specs=[o_spec, o_spec], scratch_shapes=[pltpu.VMEM((tm, n), F32)], operands=list(a_list) + [b], side=side, name=name)
    return o32, o16, got


def _mm_nn_rmsbwd(segs, weights, x, g, dres, *, tm=256, side=None, name):
    t, d = x.shape
    tm = _tile(t, tm)
    nseg, nw = len(segs), len(weights)

    def body(*refs):
        a_refs, w_refs = refs[:nseg], refs[nseg:nseg + nw]
        x_ref, g_ref, dres_ref, dx_ref, dg_ref = refs[nseg + nw:]

        @pl.when(pl.program_id(0) == 0)
        def _():
            dg_ref[...] = jnp.zeros_like(dg_ref)

        dn = None
        for s, (a, w_idx, row0) in enumerate(segs):
            part = _dot(a_refs[s][...].astype(BF16), w_refs[w_idx][row0:row0 + a.shape[1], :], NN)
            dn = part if dn is None else dn + part
        xv = x_ref[...]
        r = lax.rsqrt(jnp.mean(xv * xv, axis=-1, keepdims=True) + RMS_EPS)
        xhat = xv * r
        dyg = dn * g_ref[...]
        dx_ref[...] = dres_ref[...] + r * (dyg - xhat * jnp.mean(dyg * xhat, axis=-1, keepdims=True))
        dg_ref[...] += jnp.sum(dn * xhat, axis=0, keepdims=True)

    row = pl.BlockSpec((tm, d), lambda i: (i, 0))
    in_specs = [pl.BlockSpec((tm, a.shape[1]), lambda i: (i, 0)) for a, _, _ in segs]
    in_specs += [_resident_spec(w.shape) for w in weights] + [row, _const_spec((1, d)), row]
    (dx, dg), extra = _hosted_call(
        body, grid=(t // tm,), in_specs=in_specs, out_specs=[row, _const_spec((1, d))],
        out_shape=[jax.ShapeDtypeStruct((t, d), F32), jax.ShapeDtypeStruct((1, d), F32)], scratch_shapes=[],
        operands=[a for a, _, _ in segs] + list(weights) + [x, g, dres], side=side, name=name)
    return dx, dg, extra


def _col_chunks(width, chunk=4 * LANES):
    return [(c0, min(c0 + chunk, width)) for c0 in range(0, width, chunk)]


def _rms_fwd(x, g, *, name):
    t, d = x.shape
    tm = _tile(t, 512)

    def body(x_ref, g_ref, o_ref):
        xv = x_ref[...]
        r = lax.rsqrt(jnp.mean(xv * xv, axis=-1, keepdims=True) + RMS_EPS)
        o_ref[...] = (xv * r * g_ref[...]).astype(o_ref.dtype)

    row = pl.BlockSpec((tm, d), lambda i: (i, 0))
    return pl.pallas_call(body, out_shape=jax.ShapeDtypeStruct((t, d), BF16), grid=(t // tm,),
                          in_specs=[row, _const_spec((1, d))], out_specs=row, name=name, compiler_params=_params("parallel"))(x, g)


def _ffn_up(n, wgt, wut, *, side=None, name):
    t, d = n.shape
    f = wgt.shape[0]
    tm, tn = _tile(t, 512), _tile(f, FFN_COL_TILE, LANES)

    def body(n_ref, wg_ref, wu_ref, g_ref, u_ref, h_ref):
        nv = n_ref[...]
        gv = _dot(nv, wg_ref[...], NT)
        uv = _dot(nv, wu_ref[...], NT)
        g_ref[...] = gv.astype(BF16)
        u_ref[...] = uv.astype(BF16)
        h_ref[...] = (gv * _sigmoid(gv) * uv).astype(BF16)

    w_spec = pl.BlockSpec((tn, d), lambda j, i: (j, 0))
    o_spec = pl.BlockSpec((tm, tn), lambda j, i: (i, j))
    shp = jax.ShapeDtypeStruct((t, f), BF16)
    return _hosted_call(body, grid=(f // tn, t // tm), in_specs=[pl.BlockSpec((tm, d), lambda j, i: (i, 0)), w_spec, w_spec],
                        out_specs=[o_spec, o_spec, o_spec], out_shape=[shp, shp, shp], scratch_shapes=[], operands=[n, wgt, wut],
                        side=side, name=name)


def _rms_residual(acc, h, gp, weight):
    r = lax.rsqrt(jnp.mean(acc * acc, axis=-1, keepdims=True) + RMS_EPS)
    return h + weight * (acc * r * gp)


def _ffn_down(hid, wd, h_in, gp, *, target=None, side=None, name):
    t, f = hid.shape
    d = wd.shape[1]
    tm = _tile(t, 256)
    row = pl.BlockSpec((tm, d), lambda i: (i, 0))
    shp = jax.ShapeDtypeStruct((t, d), F32)
    in_specs = [pl.BlockSpec((tm, f), lambda i: (i, 0)), _resident_spec((f, d)), row, _const_spec((1, d))]

    if target is None:
        def body(hid_ref, wd_ref, hin_ref, gp_ref, f_ref, hout_ref):
            acc = _dot(hid_ref[...], wd_ref[...], NN)
            f_ref[...] = acc
            hout_ref[...] = _rms_residual(acc, hin_ref[...], gp_ref[...], FFN_RESIDUAL_WEIGHT)

        return _hosted_call(body, grid=(t // tm,), in_specs=in_specs, out_specs=[row, row], out_shape=[shp, shp], scratch_shapes=[],
                            operands=[hid, wd, h_in, gp], side=side, name=name)

    def body_loss(hid_ref, wd_ref, hin_ref, gp_ref, tgt_ref, f_ref, dh_ref, loss_ref):
        @pl.when(pl.program_id(0) == 0)
        def _():
            loss_ref[...] = jnp.zeros_like(loss_ref)
        acc = _dot(hid_ref[...], wd_ref[...], NN)
        f_ref[...] = acc
        e = _rms_residual(acc, hin_ref[...], gp_ref[...], FFN_RESIDUAL_WEIGHT) - tgt_ref[...]
        dh_ref[...] = e * (1.0 / d)
        per_row = jnp.sum(e * e, axis=1, keepdims=True) * (1.0 / d)
        loss_ref[...] += 0.5 * jnp.sum(per_row, axis=0, keepdims=True)

    return _hosted_call(body_loss, grid=(t // tm,), in_specs=in_specs + [row], out_specs=[row, row, _const_spec((1, LANES))],
                        out_shape=[shp, shp, jax.ShapeDtypeStruct((1, LANES), F32)], scratch_shapes=[],
                        operands=[hid, wd, h_in, gp, target], side=side, name=name)


def _post_bwd(dh, f, gp, weight, *, name):
    t, d = f.shape
    tm = _tile(t, 512)

    def body(dh_ref, f_ref, gp_ref, df_ref, dgp_ref):
        @pl.when(pl.program_id(0) == 0)
        def _():
            dgp_ref[...] = jnp.zeros_like(dgp_ref)
        fv = f_ref[...]
        dy = weight * dh_ref[...]
        r = lax.rsqrt(jnp.mean(fv * fv, axis=-1, keepdims=True) + RMS_EPS)
        fhat = fv * r
        dyg = dy * gp_ref[...]
        df_ref[...] = (r * (dyg - fhat * jnp.mean(dyg * fhat, axis=-1, keepdims=True))).astype(BF16)
        dgp_ref[...] += jnp.sum(dy * fhat, axis=0, keepdims=True)

    row = pl.BlockSpec((tm, d), lambda i: (i, 0))
    return pl.pallas_call(body, out_shape=(jax.ShapeDtypeStruct((t, d), BF16), jax.ShapeDtypeStruct((1, d), F32)), grid=(t // tm,),
                          in_specs=[row, row, _const_spec((1, d))], out_specs=(row, _const_spec((1, d))),
                          name=name, compiler_params=_params("arbitrary"))(dh, f, gp)


def _ffn_dhid(df, wd, g, u, *, name):
    t, d = df.shape
    f = wd.shape[0]
    tm, tn = _tile(t, 512), _tile(f, FFN_COL_TILE, LANES)

    def body(df_ref, wd_ref, g_ref, u_ref, dg_ref, du_ref):
        dh = _dot(df_ref[...], wd_ref[...], NT)
        gv = g_ref[...].astype(F32)
        uv = u_ref[...].astype(F32)
        sg = _sigmoid(gv)
        silu = gv * sg
        dg_ref[...] = (dh * uv * (sg + silu * (1.0 - sg))).astype(BF16)
        du_ref[...] = (dh * silu).astype(BF16)

    o_spec = pl.BlockSpec((tm, tn), lambda j, i: (i, j))
    shp = jax.ShapeDtypeStruct((t, f), BF16)
    return pl.pallas_call(body, out_shape=(shp, shp), grid=(f // tn, t // tm),
                          in_specs=[pl.BlockSpec((tm, d), lambda j, i: (i, 0)), pl.BlockSpec((tn, d), lambda j, i: (j, 0)), o_spec, o_spec],
                          out_specs=(o_spec, o_spec), name=name, compiler_params=_params("parallel", "arbitrary"))(df, wd, g, u)


def _ffn_forward(h_in, g_pre, wgt, wut, wd, g_post, tag, side_up=None, side_down=None, target=None):
    n = _rms_fwd(h_in, g_pre, name=f"{tag}_prenorm")
    (g, u, hid), got_up = _ffn_up(n, wgt, wut, side=side_up, name=f"{tag}_up")
    wd = wd(got_up) if callable(wd) else wd
    outs, got_down = _ffn_down(hid, wd, h_in, g_post, target=target, side=side_down, name=f"{tag}_down")
    return outs[1:], (h_in, n, g, u, hid, outs[0]), got_up, got_down


def _stack8(g):
    return g.reshape(N_DEV, g.shape[0] // N_DEV, g.shape[1])


def _ffn_backward(dh_out, saved, g_pre, wgt, wut, wd, g_post, tag, chain=False):
    h_in, n, g, u, hid, f = saved

    def side_of(grad16):
        return _scatter_exchange([_stack8(grad16)]) if chain else None

    df, dg_post = _post_bwd(dh_out, f, g_post, FFN_RESIDUAL_WEIGHT, name=f"{tag}_post_bwd")
    dgate, dup = _ffn_dhid(df, wd, g, u, name=f"{tag}_dhid")
    d_wd, d_wd16, _ = _mm_tn([hid], df, name=f"{tag}_dwd")
    d_wgt, d_wgt16, got_wd = _mm_tn([dgate], n, side=side_of(d_wd16), name=f"{tag}_dwg")
    d_wut, d_wut16, got_wg = _mm_tn([dup], n, side=side_of(d_wgt16), name=f"{tag}_dwu")
    dh_in, dg_pre, got_wu = _mm_nn_rmsbwd([(dgate, 0, 0), (dup, 1, 0)], [wgt, wut], h_in, g_pre, dh_out, side=side_of(d_wut16),
                                          name=f"{tag}_dn")
    received = (got_wg[0], got_wu[0], got_wd[0]) if chain else None
    return dh_in, dg_pre, dg_post, (d_wgt, d_wut, d_wd), (d_wgt16, d_wut16, d_wd16), received


CONV_ROWS = 128
HALO = 8


def _taps(w_ref):
    return [w_ref[k:k + 1, :] for k in range(SSM_CONV)]


def _conv_chunk(x_ref, xs, r0, taps, bias):
    xs[HALO + r0:HALO + r0 + CONV_ROWS, :] = x_ref[r0:r0 + CONV_ROWS, :].astype(F32)
    shifted = [xs[HALO + r0 - k:HALO + r0 - k + CONV_ROWS, :] for k in range(SSM_CONV)]
    pre = bias + shifted[0] * taps[SSM_CONV - 1]
    for k in range(1, SSM_CONV):
        pre = pre + shifted[k] * taps[SSM_CONV - 1 - k]
    return shifted, pre


def _fold_rows(a):
    return functools.reduce(jnp.add, [a[i:i + 8] for i in range(0, a.shape[0], 8)])


def _conv_fwd(xbc, conv_w, conv_b, bl, *, name):
    t, c = xbc.shape
    s = t // bl
    tc = LANES
    assert s % CONV_ROWS == 0

    def body(x_ref, w_ref, b_ref, o_ref, xs):
        taps, bias = _taps(w_ref), b_ref[...]
        xs[0:HALO, :] = jnp.zeros((HALO, tc), F32)
        for r0 in range(0, s, CONV_ROWS):
            _, pre = _conv_chunk(x_ref, xs, r0, taps, bias)
            o_ref[r0:r0 + CONV_ROWS, :] = (pre * _sigmoid(pre)).astype(o_ref.dtype)

    blk = pl.BlockSpec((s, tc), lambda b, j: (b, j))
    return pl.pallas_call(body, out_shape=jax.ShapeDtypeStruct((t, c), BF16), grid=(bl, c // tc),
                          in_specs=[blk, pl.BlockSpec((SSM_CONV, tc), lambda b, j: (0, j)), pl.BlockSpec((1, tc), lambda b, j: (0, j))],
                          out_specs=blk, scratch_shapes=[pltpu.VMEM((HALO + s, tc), F32)],
                          name=name, compiler_params=_params("parallel", "arbitrary"))(xbc, conv_w, conv_b)


def _conv_bwd(dxc, xbc, conv_w, conv_b, bl, *, name):
    t, c = xbc.shape
    s = t // bl
    tc = LANES

    def body(dy_ref, x_ref, w_ref, b_ref, dx_ref, dw_ref, db_ref, xs, dpre_s):
        @pl.when(pl.program_id(1) == 0)
        def _():
            dw_ref[...] = jnp.zeros_like(dw_ref)
            db_ref[...] = jnp.zeros_like(db_ref)

        taps, bias = _taps(w_ref), b_ref[...]
        zero8 = jnp.zeros((HALO, tc), F32)
        xs[0:HALO, :] = zero8
        dpre_s[s:s + HALO, :] = zero8
        sums = [zero8] * (SSM_CONV + 1)
        for r0 in range(0, s, CONV_ROWS):
            shifted, pre = _conv_chunk(x_ref, xs, r0, taps, bias)
            sg = _sigmoid(pre)
            dpre = dy_ref[r0:r0 + CONV_ROWS, :].astype(F32) * (sg * (1.0 + pre * (1.0 - sg)))
            dpre_s[r0:r0 + CONV_ROWS, :] = dpre
            sums = [acc + _fold_rows(dpre * sh) for acc, sh in zip(sums[:-1], shifted)] + [sums[-1] + _fold_rows(dpre)]
        for k in range(SSM_CONV):
            dw_ref[SSM_CONV - 1 - k:SSM_CONV - k, :] += jnp.sum(sums[k], axis=0, keepdims=True)
        db_ref[...] += jnp.sum(sums[-1], axis=0, keepdims=True)
        for r0 in range(0, s, CONV_ROWS):
            dx = dpre_s[r0:r0 + CONV_ROWS, :] * taps[SSM_CONV - 1]
            for k in range(1, SSM_CONV):
                dx = dx + dpre_s[r0 + k:r0 + k + CONV_ROWS, :] * taps[SSM_CONV - 1 - k]
            dx_ref[r0:r0 + CONV_ROWS, :] = dx.astype(dx_ref.dtype)

    blk = pl.BlockSpec((s, tc), lambda j, b: (b, j))
    return pl.pallas_call(
        body, out_shape=(jax.ShapeDtypeStruct((t, c), BF16), jax.ShapeDtypeStruct((8, c), F32), jax.ShapeDtypeStruct((1, c), F32)),
        grid=(c // tc, bl),
        in_specs=[blk, blk, pl.BlockSpec((SSM_CONV, tc), lambda j, b: (0, j)), pl.BlockSpec((1, tc), lambda j, b: (0, j))],
        out_specs=(blk, pl.BlockSpec((8, tc), lambda j, b: (0, j)), pl.BlockSpec((1, tc), lambda j, b: (0, j))),
        scratch_shapes=[pltpu.VMEM((HALO + s, tc), F32), pltpu.VMEM((s + HALO, tc), F32)],
        name=name, compiler_params=_params("parallel", "arbitrary"))(dxc, xbc, conv_w, conv_b)


def _softplus(x):
    return jnp.maximum(x, 0.0) + jnp.log1p(jnp.exp(-jnp.abs(x)))


def _hilo_dot(v, m_b, dims=NN):
    hi = v.astype(BF16)
    lo = (v - hi.astype(F32)).astype(BF16)
    return _dot(hi, m_b, dims) + _dot(lo, m_b, dims)


def _ssd_chunk_common(dtraw_ref, dtb_ref, alog_ref, dsk_ref, d_inner):
    q, p = CHUNK, SSM_HEAD_DIM
    shift = p.bit_length() - 1
    assert 1 << shift == p
    dt = _softplus(dtraw_ref[...] + dtb_ref[...])
    a = -jnp.exp(alog_ref[...])
    ii = lax.broadcasted_iota(jnp.int32, (q, q), 0)
    jj = lax.broadcasted_iota(jnp.int32, (q, q), 1)
    causal = ii >= jj
    tril = jnp.where(causal, 1.0, 0.0).astype(F32)
    triu = jnp.where(ii <= jj, 1.0, 0.0).astype(F32)
    a_cs = _dot_hi(tril, dt * a)
    a_cs_t = a_cs.T
    a_last = a_cs[q - 1:q, :]
    e_col = jnp.exp(a_cs)
    dec_end = jnp.exp(a_last - a_cs)
    head_of_col = lax.shift_right_logical(lax.broadcasted_iota(jnp.int32, (LANES, d_inner), 1), shift)
    spread = (lax.broadcasted_iota(jnp.int32, (LANES, d_inner), 0) == head_of_col).astype(BF16)
    wide = _hilo_dot(jnp.concatenate([dt, e_col, dec_end, jnp.broadcast_to(dsk_ref[...], (8, LANES))], axis=0), spread)
    return dict(dt=dt, a=a, a_cs=a_cs, a_cs_t=a_cs_t, a_last=a_last, dec_end=dec_end, causal=causal, triu=triu,
                dt_e=wide[:q], e_e=wide[q:2 * q], dec_e=wide[2 * q:3 * q], dsk_e=wide[3 * q:3 * q + 1])


def _fill_block_diag(bd_ref, src_ref, hpg, col0=0):
    q, p = CHUNK, SSM_HEAD_DIM
    for hh in range(hpg):
        bd_ref[hh * q:(hh + 1) * q, hh * p:(hh + 1) * p] = src_ref[:, col0 + hh * p:col0 + (hh + 1) * p]


def _lane_onehot(h):
    return (lax.broadcasted_iota(jnp.int32, (1, LANES), 1) == h).astype(F32)


def _ssd_fwd(xc, dt_raw, dt_bias, a_log, d_skip, bl, n_heads, *, side=None, name):
    t = xc.shape[0]
    q, p, nst, grp = CHUNK, SSM_HEAD_DIM, SSM_STATE, SSM_GROUPS
    d_inner = n_heads * p
    hpg = n_heads // grp
    hb = min(hpg, SSD_HEAD_BATCH)
    gw = hpg * p
    nc = t // bl // q
    assert d_inner % (grp * nst) == 0 and nst == LANES and hpg % hb == 0

    def body(xs_ref, b_ref, c_ref, dtraw_ref, dtb_ref, alog_ref, dsk_ref, y_ref, hprev_ref, state, m_all, x_bd, xdt_s):
        @pl.when(jnp.logical_and(pl.program_id(0) == 0, pl.program_id(1) == 0))
        def _():
            x_bd[...] = jnp.zeros_like(x_bd)

        @pl.when(pl.program_id(1) == 0)
        def _():
            state[...] = jnp.zeros_like(state)

        cm = _ssd_chunk_common(dtraw_ref, dtb_ref, alog_ref, dsk_ref, d_inner)
        for g in range(grp):
            cols = slice(g * gw, (g + 1) * gw)
            bg = b_ref[:, g * nst:(g + 1) * nst]
            cg = c_ref[:, g * nst:(g + 1) * nst]
            scores = _dot(cg, bg, NT)
            xs = xs_ref[:, cols].astype(F32)
            xdt = xs * cm['dt_e'][:, cols]
            xdt_s[...] = xdt.astype(BF16)
            y_parts = []
            for sub in range(hpg // hb):
                for k in range(hb):
                    h = g * hpg + sub * hb + k
                    seg = cm['a_cs'][:, h:h + 1] - cm['a_cs_t'][h:h + 1, :]
                    m_all[:, k * q:(k + 1) * q] = (scores * jnp.exp(jnp.where(cm['causal'], seg, NEG_INF))).astype(BF16)
                _fill_block_diag(x_bd, xdt_s, hb, sub * hb * p)
                y_parts.append(_dot(m_all[...], x_bd[...], NN))
            hprev = state[g]
            hprev_ref[g] = hprev
            y = jnp.concatenate(y_parts, axis=1) + cm['e_e'][:, cols] * _dot(cg, hprev.astype(BF16), NT)
            y_ref[:, cols] = y + cm['dsk_e'][:, cols] * xs
            st = _dot((xdt * cm['dec_e'][:, cols]).astype(BF16), bg, TN)
            for hh in range(hpg):
                h = g * hpg + hh
                rows = slice(hh * p, (hh + 1) * p)
                state[g, rows, :] = jnp.exp(cm['a_last'][:, h:h + 1]) * hprev[rows] + st[rows]

    gn = grp * nst

    def rowmap(b, c):
        return b * nc + c
    vec = pl.BlockSpec((1, LANES), lambda b, c: (0, 0))
    return _hosted_call(
        body,
        out_shape=[jax.ShapeDtypeStruct((t, d_inner), F32), jax.ShapeDtypeStruct((t // q, grp, gw, nst), F32)],
        grid=(bl, nc),
        in_specs=[pl.BlockSpec((q, d_inner), lambda b, c: (rowmap(b, c), 0)),
                  pl.BlockSpec((q, gn), lambda b, c: (rowmap(b, c), d_inner // gn)),
                  pl.BlockSpec((q, gn), lambda b, c: (rowmap(b, c), d_inner // gn + 1)),
                  pl.BlockSpec((q, LANES), lambda b, c: (rowmap(b, c), 0)), vec, vec, vec],
        out_specs=[pl.BlockSpec((q, d_inner), lambda b, c: (rowmap(b, c), 0)),
                   pl.BlockSpec((None, grp, gw, nst), lambda b, c: (rowmap(b, c), 0, 0, 0))],
        scratch_shapes=[pltpu.VMEM((grp, gw, nst), F32), pltpu.VMEM((q, hb * q), BF16), pltpu.VMEM((hb * q, hb * p), BF16),
                        pltpu.VMEM((q, gw), BF16)],
        operands=[xc, xc, xc, dt_raw, dt_bias, a_log, d_skip], side=side, name=name)


def _ssd_bwd(dy, y, xc, dt_raw, hprev_all, dt_bias, a_log, d_skip, bl, n_heads, *, side=None, name):
    t, c_dim = xc.shape
    q, p, nst, grp = CHUNK, SSM_HEAD_DIM, SSM_STATE, SSM_GROUPS
    d_inner = n_heads * p
    hpg = n_heads // grp
    hb = min(hpg, SSD_HEAD_BATCH)
    gw = hpg * p
    nc = t // bl // q
    gn = grp * nst
    shift = p.bit_length() - 1

    def body(dy_ref, y_ref, xs_ref, b_ref, c_ref, dtraw_ref, hprev_ref, dtb_ref, alog_ref, dsk_ref,
             dxc_ref, ddtraw_ref, ddtb_ref, dalog_ref, ddsk_ref, dstate, mt_all, x_bd, dy_bd, xdt_s):
        @pl.when(jnp.logical_and(pl.program_id(0) == 0, pl.program_id(1) == 0))
        def _():
            ddtb_ref[...] = jnp.zeros_like(ddtb_ref)
            dalog_ref[...] = jnp.zeros_like(dalog_ref)
            ddsk_ref[...] = jnp.zeros_like(ddsk_ref)
            x_bd[...] = jnp.zeros_like(x_bd)
            dy_bd[...] = jnp.zeros_like(dy_bd)

        @pl.when(pl.program_id(1) == 0)
        def _():
            dstate[...] = jnp.zeros_like(dstate)

        cm = _ssd_chunk_common(dtraw_ref, dtb_ref, alog_ref, dsk_ref, d_inner)
        causal = cm['causal']
        upper = cm['triu'] > 0.5
        seg_row = lax.shift_right_logical(lax.broadcasted_iota(jnp.int32, (gw, LANES), 0), shift)
        seg_lane = lax.broadcasted_iota(jnp.int32, (gw, LANES), 1)
        sums = jnp.zeros((5 * q, LANES), F32)
        state_dot = jnp.zeros((1, LANES), F32)
        for g in range(grp):
            cols = slice(g * gw, (g + 1) * gw)
            seg_sum = (seg_row + g * hpg == seg_lane).astype(BF16)
            bg = b_ref[:, g * nst:(g + 1) * nst]
            cg = c_ref[:, g * nst:(g + 1) * nst]
            scores_t = _dot(bg, cg, NT)
            xs = xs_ref[:, cols].astype(F32)
            xdt = xs * cm['dt_e'][:, cols]
            xdt_s[...] = xdt.astype(BF16)
            dyf = dy_ref[:, cols].astype(F32)
            dscores = jnp.zeros((q, q), F32)
            dx_parts = []
            for sub in range(hpg // hb):
                c0 = sub * hb * p
                _fill_block_diag(x_bd, xdt_s, hb, c0)
                _fill_block_diag(dy_bd, dy_ref, hb, g * gw + c0)
                dm_all = _dot(dy_ref[:, g * gw + c0:g * gw + c0 + hb * p], x_bd[...], NT)
                for k in range(hb):
                    h = g * hpg + sub * hb + k
                    blk = slice(k * q, (k + 1) * q)
                    seg = cm['a_cs'][:, h:h + 1] - cm['a_cs_t'][h:h + 1, :]
                    mt_all[:, blk] = (scores_t * jnp.exp(jnp.where(upper, -seg, NEG_INF))).astype(BF16)
                    dscores = dscores + dm_all[:, blk] * jnp.exp(jnp.where(causal, seg, NEG_INF))
                dx_parts.append(_dot(mt_all[...], dy_bd[...], NN))
            hprev = hprev_ref[g]
            hprev_b = hprev.astype(BF16)
            dhn = dstate[g]
            dhn_b = dhn.astype(BF16)
            e_e, dec_e = cm['e_e'][:, cols], cm['dec_e'][:, cols]
            y_scan = y_ref[:, cols] - cm['dsk_e'][:, cols] * xs
            dye_b = (dyf * e_e).astype(BF16)
            dcg = _dot(dye_b, hprev_b, NN)
            dhp = _dot(dye_b, cg, TN)
            bdh = _dot(bg, dhn_b, NT)
            dbg = _dot((xdt * dec_e).astype(BF16), dhn_b, NN)
            dx_diag = jnp.concatenate(dx_parts, axis=1)
            dx = dec_e * bdh + dx_diag
            ds_b = dscores.astype(BF16)
            dcg = dcg + _dot(ds_b, bg, NN)
            dbg = dbg + _dot(ds_b, cg, TN)
            x_rounded = xdt_s[...].astype(F32)
            sums = sums + _hilo_dot(jnp.concatenate([dyf * y_scan, xdt * bdh, x_rounded * dx_diag, dx * xs, dyf * xs], axis=0), seg_sum)
            state_dot = state_dot + jnp.sum(_hilo_dot(dhn * hprev, seg_sum, TN), axis=0, keepdims=True)
            dxc_ref[:, cols] = (dx * cm['dt_e'][:, cols] + cm['dsk_e'][:, cols] * dyf).astype(dxc_ref.dtype)
            dxc_ref[:, d_inner + g * nst:d_inner + (g + 1) * nst] = dbg.astype(dxc_ref.dtype)
            dxc_ref[:, d_inner + gn + g * nst:d_inner + gn + (g + 1) * nst] = dcg.astype(dxc_ref.dtype)
            for hh in range(hpg):
                h = g * hpg + hh
                rows = slice(hh * p, (hh + 1) * p)
                dstate[g, rows, :] = jnp.exp(cm['a_last'][:, h:h + 1]) * dhn[rows] + dhp[rows]
        s_y, s_end, s_diag, s_dt, s_skip = (sums[k * q:(k + 1) * q] for k in range(5))
        dt, a, dec_end = cm['dt'], cm['a'], cm['dec_end']
        last_row = (lax.broadcasted_iota(jnp.int32, (q, 1), 0) == q - 1).astype(F32)
        da_last = jnp.sum(dec_end * s_end, axis=0, keepdims=True) + jnp.exp(cm['a_last']) * state_dot
        da = s_y - dec_end * s_end - s_diag + last_row * da_last
        ddta = _dot_hi(cm['triu'], da)
        ddt = s_dt + ddta * a
        d_a = jnp.sum(ddta * dt, axis=0, keepdims=True)
        ddt_raw = ddt * _sigmoid(dtraw_ref[...] + dtb_ref[...])
        ddtraw_ref[...] = ddt_raw
        ddtb_ref[...] += jnp.sum(ddt_raw, axis=0, keepdims=True)
        dalog_ref[...] += d_a * a
        ddsk_ref[...] += jnp.sum(s_skip, axis=0, keepdims=True)

    def rowmap(b, c):
        return b * nc + (nc - 1 - c)
    vec = pl.BlockSpec((1, LANES), lambda b, c: (0, 0))
    vec_shape = jax.ShapeDtypeStruct((1, LANES), F32)
    return _hosted_call(
        body,
        out_shape=[jax.ShapeDtypeStruct((t, c_dim), BF16), jax.ShapeDtypeStruct((t, LANES), F32), vec_shape, vec_shape, vec_shape],
        grid=(bl, nc),
        in_specs=[pl.BlockSpec((q, d_inner), lambda b, c: (rowmap(b, c), 0)),
                  pl.BlockSpec((q, d_inner), lambda b, c: (rowmap(b, c), 0)),
                  pl.BlockSpec((q, d_inner), lambda b, c: (rowmap(b, c), 0)),
                  pl.BlockSpec((q, gn), lambda b, c: (rowmap(b, c), d_inner // gn)),
                  pl.BlockSpec((q, gn), lambda b, c: (rowmap(b, c), d_inner // gn + 1)),
                  pl.BlockSpec((q, LANES), lambda b, c: (rowmap(b, c), 0)),
                  pl.BlockSpec((None, grp, gw, nst), lambda b, c: (rowmap(b, c), 0, 0, 0)), vec, vec, vec],
        out_specs=[pl.BlockSpec((q, c_dim), lambda b, c: (rowmap(b, c), 0)),
                   pl.BlockSpec((q, LANES), lambda b, c: (rowmap(b, c), 0)), vec, vec, vec],
        scratch_shapes=[pltpu.VMEM((grp, gw, nst), F32), pltpu.VMEM((q, hb * q), BF16),
                        pltpu.VMEM((hb * q, hb * p), BF16), pltpu.VMEM((hb * q, hb * p), BF16), pltpu.VMEM((q, gw), BF16)],
        operands=[dy, y, xc, xc, xc, dt_raw, hprev_all, dt_bias, a_log, d_skip], side=side, name=name)


def _gated_norm_fwd(y, z, ng, *, name):
    t, d = y.shape
    tm = _tile(t, 256)
    gw = d // SSM_GROUPS

    def body(y_ref, z_ref, ng_ref, o_ref):
        for g in range(SSM_GROUPS):
            sl = slice(g * gw, (g + 1) * gw)
            zv = z_ref[:, sl].astype(F32)
            yg = y_ref[:, sl] * (zv * _sigmoid(zv))
            r = lax.rsqrt(jnp.mean(yg * yg, axis=-1, keepdims=True) + RMS_EPS)
            o_ref[:, sl] = (yg * r * ng_ref[:, sl]).astype(o_ref.dtype)

    row = pl.BlockSpec((tm, d), lambda i: (i, 0))
    return pl.pallas_call(body, out_shape=jax.ShapeDtypeStruct((t, d), BF16), grid=(t // tm,),
                          in_specs=[row, row, _const_spec((1, d))], out_specs=row, name=name, compiler_params=_params("parallel"))(y, z, ng)


def _gated_norm_bwd(dyn, y, z, ng, *, name):
    t, d = y.shape
    tm = _tile(t, 256)
    gw = d // SSM_GROUPS

    def body(dyn_ref, y_ref, z_ref, ng_ref, dy_ref, dz_ref, dng_ref):
        @pl.when(pl.program_id(0) == 0)
        def _():
            dng_ref[...] = jnp.zeros_like(dng_ref)
        for g in range(SSM_GROUPS):
            sl = slice(g * gw, (g + 1) * gw)
            zv = z_ref[:, sl].astype(F32)
            yv = y_ref[:, sl]
            sg = _sigmoid(zv)
            sz = zv * sg
            yg = yv * sz
            r = lax.rsqrt(jnp.mean(yg * yg, axis=-1, keepdims=True) + RMS_EPS)
            yhat = yg * r
            dn = dyn_ref[:, sl].astype(F32)
            dyg_n = dn * ng_ref[:, sl]
            dyg = r * (dyg_n - yhat * jnp.mean(dyg_n * yhat, axis=-1, keepdims=True))
            dy_ref[:, sl] = (dyg * sz).astype(dy_ref.dtype)
            dz_ref[:, sl] = (dyg * yv * (sg * (1.0 + zv * (1.0 - sg)))).astype(dz_ref.dtype)
            dng_ref[:, sl] += jnp.sum(dn * yhat, axis=0, keepdims=True)

    row = pl.BlockSpec((tm, d), lambda i: (i, 0))
    shp = jax.ShapeDtypeStruct((t, d), BF16)
    return pl.pallas_call(body, out_shape=(shp, shp, jax.ShapeDtypeStruct((1, d), F32)), grid=(t // tm,),
                          in_specs=[row, row, row, _const_spec((1, d))], out_specs=(row, row, _const_spec((1, d))),
                          name=name, compiler_params=_params("arbitrary"))(dyn, y, z, ng)


def _bucket_onehot():
    blk = CHUNK
    qi = jnp.arange(blk)[:, None]
    kj = jnp.arange(2 * blk)[None, :]
    dist = jnp.maximum(qi + blk - kj, 0)
    max_exact = REL_BUCKETS // 2
    d = jnp.maximum(dist, 1).astype(F32)
    large = max_exact + (jnp.log(d / max_exact) / math.log(REL_MAX_DISTANCE / max_exact) * (REL_BUCKETS - max_exact)).astype(jnp.int32)
    large = jnp.minimum(large, REL_BUCKETS - 1)
    bucket = jnp.where(dist < max_exact, dist, large).reshape(-1)
    return (bucket[None, :] == jnp.arange(REL_BUCKETS)[:, None]).astype(F32)


def _small_mm_hi(a, b, dims, *, name):
    def body(a_ref, b_ref, o_ref):
        o_ref[...] = _dot_hi(a_ref[...], b_ref[...], dims)
    n = b.shape[0] if dims == NT else b.shape[1]
    return pl.pallas_call(body, out_shape=jax.ShapeDtypeStruct((a.shape[0], n), F32), name=name)(a, b)


def _attn_band_mask_t(n, rep):
    blk = CHUNK
    jj = lax.broadcasted_iota(jnp.int32, (2 * blk, rep * blk), 0)
    ii = lax.broadcasted_iota(jnp.int32, (2 * blk, rep * blk), 1) & (blk - 1)
    dist = ii + blk - jj
    in_window = jnp.logical_and(dist >= 0, dist < blk)
    return jnp.logical_and(in_window, jnp.logical_or(jj >= blk, n > 0))


def _sink_row(sink_ref, heads):
    return jnp.concatenate([jnp.broadcast_to(sink_ref[:, h:h + 1], (1, CHUNK)) for h in heads], axis=1)


def _attn_fwd(q, kv, bias_t, sinks, bl, *, name):
    t, qd = q.shape
    blk, hd = CHUNK, ATTN_HEAD_DIM
    kvd = ATTN_KV_HEADS * hd
    rep = ATTN_Q_HEADS // ATTN_KV_HEADS
    nb = t // bl // blk
    scale = hd ** -0.5

    def body(q_ref, kp_ref, kc_ref, vp_ref, vc_ref, bias_ref, sink_ref, o_ref, lse_ref):
        n = pl.program_id(1)
        mask = _attn_band_mask_t(n, rep)
        for kvh in range(ATTN_KV_HEADS):
            ks = slice(kvh * hd, (kvh + 1) * hd)
            heads = range(kvh * rep, (kvh + 1) * rep)
            qs = jnp.concatenate([q_ref[:, h * hd:(h + 1) * hd] for h in heads], axis=0)
            kk = jnp.concatenate([kp_ref[:, ks], kc_ref[:, ks]], axis=0)
            vv = jnp.concatenate([vp_ref[:, ks], vc_ref[:, ks]], axis=0)
            s = jnp.where(mask, _dot(kk, qs, NT) * scale + bias_ref[kvh], NEG_INF)
            sink = _sink_row(sink_ref, heads)
            m = jnp.maximum(jnp.max(s, axis=0, keepdims=True), sink)
            p = jnp.exp(s - m)
            den = jnp.sum(p, axis=0, keepdims=True) + jnp.exp(sink - m)
            o = _dot((p * (1.0 / den)).astype(BF16), vv, TN)
            lse = m + jnp.log(den)
            for r, h in enumerate(heads):
                o_ref[:, h * hd:(h + 1) * hd] = o[r * blk:(r + 1) * blk].astype(o_ref.dtype)
                lse_ref[h:h + 1, :] = lse[:, r * blk:(r + 1) * blk]

    def cur(b, n):
        return b * nb + n

    def prev(b, n):
        return b * nb + jnp.maximum(n - 1, 0)
    return pl.pallas_call(
        body, out_shape=(jax.ShapeDtypeStruct((t, qd), BF16), jax.ShapeDtypeStruct((t // blk * ATTN_Q_HEADS, blk), F32)), grid=(bl, nb),
        in_specs=[pl.BlockSpec((blk, qd), lambda b, n: (cur(b, n), 0)),
                  pl.BlockSpec((blk, kvd), lambda b, n: (prev(b, n), 0)), pl.BlockSpec((blk, kvd), lambda b, n: (cur(b, n), 0)),
                  pl.BlockSpec((blk, kvd), lambda b, n: (prev(b, n), 1)), pl.BlockSpec((blk, kvd), lambda b, n: (cur(b, n), 1)),
                  _const_spec(bias_t.shape), _const_spec((1, LANES))],
        out_specs=(pl.BlockSpec((blk, qd), lambda b, n: (cur(b, n), 0)),
                   pl.BlockSpec((ATTN_Q_HEADS, blk), lambda b, n: (cur(b, n), 0))),
        name=name, compiler_params=_params("parallel", "arbitrary"))(q, kv, kv, kv, kv, bias_t, sinks)


def _attn_bwd(do, q, kv, lse, bias_t, sinks, bl, *, name):
    t, qd = q.shape
    blk, hd = CHUNK, ATTN_HEAD_DIM
    kvd = ATTN_KV_HEADS * hd
    rep = ATTN_Q_HEADS // ATTN_KV_HEADS
    s_len = t // bl
    nb = s_len // blk
    scale = hd ** -0.5

    def body(do_ref, q_ref, kp_ref, kc_ref, vp_ref, vc_ref, lse_ref, bias_ref, sink_ref, dq_ref, dkv_ref, dbias_ref, dsink_ref):
        n = pl.program_id(1)

        @pl.when(jnp.logical_and(pl.program_id(0) == 0, n == 0))
        def _():
            dbias_ref[...] = jnp.zeros_like(dbias_ref)
            dsink_ref[...] = jnp.zeros_like(dsink_ref)

        mask = _attn_band_mask_t(n, rep)
        r_cur = pl.multiple_of(n * blk, blk)
        r_prev = pl.multiple_of(jnp.maximum(n - 1, 0) * blk, blk)
        dsink = jnp.zeros((1, LANES), F32)
        for kvh in range(ATTN_KV_HEADS):
            ks = slice(kvh * hd, (kvh + 1) * hd)
            heads = range(kvh * rep, (kvh + 1) * rep)
            qs = jnp.concatenate([q_ref[:, h * hd:(h + 1) * hd] for h in heads], axis=0)
            dos = jnp.concatenate([do_ref[:, h * hd:(h + 1) * hd] for h in heads], axis=0)
            kk = jnp.concatenate([kp_ref[:, ks], kc_ref[:, ks]], axis=0)
            vv = jnp.concatenate([vp_ref[:, ks], vc_ref[:, ks]], axis=0)
            lse = jnp.concatenate([lse_ref[h:h + 1, :] for h in heads], axis=1)
            p = jnp.exp(jnp.where(mask, _dot(kk, qs, NT) * scale + bias_ref[kvh], NEG_INF) - lse)
            dp = _dot(vv, dos, NT)
            delta = jnp.sum(p * dp, axis=0, keepdims=True)
            ds = p * (dp - delta)
            dsink_row = jnp.exp(_sink_row(sink_ref, heads) - lse) * delta
            dbias_ref[kvh] += ds
            ds_b = ds.astype(BF16)
            dq_s = _dot(ds_b, kk, TN) * scale
            dkk = _dot(ds_b, qs, NN) * scale
            dvv = _dot(p.astype(BF16), dos, NN)
            for r, h in enumerate(heads):
                dq_ref[:, h * hd:(h + 1) * hd] = dq_s[r * blk:(r + 1) * blk].astype(dq_ref.dtype)
                dsink = dsink - jnp.sum(dsink_row[:, r * blk:(r + 1) * blk], axis=1, keepdims=True) * _lane_onehot(h)
            vs = slice(kvd + kvh * hd, kvd + (kvh + 1) * hd)
            dkv_ref[pl.ds(r_cur, blk), ks] = dkk[blk:]
            dkv_ref[pl.ds(r_cur, blk), vs] = dvv[blk:]

            @pl.when(n > 0)
            def _():
                dkv_ref[pl.ds(r_prev, blk), ks] += dkk[:blk]
                dkv_ref[pl.ds(r_prev, blk), vs] += dvv[:blk]
        dsink_ref[...] += dsink

    def cur(b, n):
        return b * nb + n

    def prev(b, n):
        return b * nb + jnp.maximum(n - 1, 0)
    qspec = pl.BlockSpec((blk, qd), lambda b, n: (cur(b, n), 0))
    return pl.pallas_call(
        body,
        out_shape=(jax.ShapeDtypeStruct((t, qd), BF16), jax.ShapeDtypeStruct((t, 2 * kvd), F32),
                   jax.ShapeDtypeStruct(bias_t.shape, F32), jax.ShapeDtypeStruct((1, LANES), F32)),
        grid=(bl, nb),
        in_specs=[qspec, qspec,
                  pl.BlockSpec((blk, kvd), lambda b, n: (prev(b, n), 0)), pl.BlockSpec((blk, kvd), lambda b, n: (cur(b, n), 0)),
                  pl.BlockSpec((blk, kvd), lambda b, n: (prev(b, n), 1)), pl.BlockSpec((blk, kvd), lambda b, n: (cur(b, n), 1)),
                  pl.BlockSpec((ATTN_Q_HEADS, blk), lambda b, n: (cur(b, n), 0)), _const_spec(bias_t.shape), _const_spec((1, LANES))],
        out_specs=(qspec, pl.BlockSpec((s_len, 2 * kvd), lambda b, n: (b, 0)), _const_spec(bias_t.shape), _const_spec((1, LANES))),
        name=name, compiler_params=_params("arbitrary", "arbitrary"))(do, q, kv, kv, kv, kv, lse, bias_t, sinks)


def _merge_fwd(yn, o, gs, ga, w_ssm, w_attn, w_out, h_in, g_post, *, name):
    t, d = h_in.shape
    tm = _tile(t, 256)

    def body(yn_ref, o_ref, gs_ref, ga_ref, ws_ref, wa_ref, wo_ref, hin_ref, gp_ref, ys_ref, ya_ref, mg_ref, mix_ref, hout_ref):
        ys = _dot(yn_ref[...], ws_ref[...], NN)
        ya = _dot(o_ref[...], wa_ref[...], NN)
        merged = (_sigmoid(gs_ref[...].astype(F32)) * ys + _sigmoid(ga_ref[...].astype(F32)) * ya).astype(BF16)
        mix = _dot(merged, wo_ref[...], NN)
        ys_ref[...] = ys.astype(BF16)
        ya_ref[...] = ya.astype(BF16)
        mg_ref[...] = merged
        mix_ref[...] = mix
        hout_ref[...] = _rms_residual(mix, hin_ref[...], gp_ref[...], 1.0)

    def row(w):
        return pl.BlockSpec((tm, w), lambda i: (i, 0))
    bshape = jax.ShapeDtypeStruct((t, d), BF16)
    fshape = jax.ShapeDtypeStruct((t, d), F32)
    return pl.pallas_call(
        body, out_shape=(bshape, bshape, bshape, fshape, fshape), grid=(t // tm,),
        in_specs=[row(yn.shape[1]), row(o.shape[1]), row(d), row(d), _resident_spec(w_ssm.shape), _resident_spec(w_attn.shape),
                  _resident_spec(w_out.shape), row(d), _const_spec((1, d))],
        out_specs=(row(d),) * 5, name=name, compiler_params=_params("parallel"))(yn, o, gs, ga, w_ssm, w_attn, w_out, h_in, g_post)


def _merge_bwd(dh, mix, g_post, gs, ga, ys, ya, w_ssm, w_attn, w_out, *, name):
    t, d = mix.shape
    tm = _tile(t, 256)
    d_ssm, d_attn = w_ssm.shape[0], w_attn.shape[0]

    def body(dh_ref, mix_ref, gp_ref, gs_ref, ga_ref, ys_ref, ya_ref, ws_ref, wa_ref, wo_ref,
             dmix_ref, dys_ref, dya_ref, dgs_ref, dga_ref, dyn_ref, do_ref, dgp_ref):
        @pl.when(pl.program_id(0) == 0)
        def _():
            dgp_ref[...] = jnp.zeros_like(dgp_ref)
        mv = mix_ref[...]
        dy = dh_ref[...]
        r = lax.rsqrt(jnp.mean(mv * mv, axis=-1, keepdims=True) + RMS_EPS)
        mhat = mv * r
        dyg = dy * gp_ref[...]
        dmix = (r * (dyg - mhat * jnp.mean(dyg * mhat, axis=-1, keepdims=True))).astype(BF16)
        dgp_ref[...] += jnp.sum(dy * mhat, axis=0, keepdims=True)
        dmix_ref[...] = dmix
        dmerged = _dot(dmix, wo_ref[...], NT)
        sgs = _sigmoid(gs_ref[...].astype(F32))
        sga = _sigmoid(ga_ref[...].astype(F32))
        dys = (dmerged * sgs).astype(BF16)
        dya = (dmerged * sga).astype(BF16)
        dys_ref[...] = dys
        dya_ref[...] = dya
        dgs_ref[...] = (dmerged * ys_ref[...].astype(F32) * sgs * (1.0 - sgs)).astype(BF16)
        dga_ref[...] = (dmerged * ya_ref[...].astype(F32) * sga * (1.0 - sga)).astype(BF16)
        dyn_ref[...] = _dot(dys, ws_ref[...], NT).astype(BF16)
        do_ref[...] = _dot(dya, wa_ref[...], NT).astype(BF16)

    def row(w):
        return pl.BlockSpec((tm, w), lambda i: (i, 0))

    def bshape(w):
        return jax.ShapeDtypeStruct((t, w), BF16)
    return pl.pallas_call(
        body, out_shape=(bshape(d),) * 5 + (bshape(d_ssm), bshape(d_attn), jax.ShapeDtypeStruct((1, d), F32)), grid=(t // tm,),
        in_specs=[row(d), row(d), _const_spec((1, d)), row(d), row(d), row(d), row(d),
                  _resident_spec(w_ssm.shape), _resident_spec(w_attn.shape), _resident_spec(w_out.shape)],
        out_specs=(row(d),) * 5 + (row(d_ssm), row(d_attn), _const_spec((1, d))),
        name=name, compiler_params=_params("arbitrary"))(dh, mix, g_post, gs, ga, ys, ya, w_ssm, w_attn, w_out)


def _adamw(w, g, m, v, *, name):
    r, c = w.shape
    tm = _tile(r, 256)
    c1 = 1.0 - ADAM_B1 ** ADAM_STEP
    c2 = 1.0 - ADAM_B2 ** ADAM_STEP

    def body(w_ref, g_ref, m_ref, v_ref, d_ref, mo_ref, vo_ref):
        gv = g_ref[...]
        mn = ADAM_B1 * m_ref[...] + (1.0 - ADAM_B1) * gv
        vn = ADAM_B2 * v_ref[...] + (1.0 - ADAM_B2) * (gv * gv)
        mo_ref[...] = mn
        vo_ref[...] = vn
        d_ref[...] = -ADAM_LR * ((mn / c1) / (jnp.sqrt(vn / c2) + ADAM_EPS) + ADAM_WD * w_ref[...])

    blk = pl.BlockSpec((tm, c), lambda i: (i, 0))
    shp = jax.ShapeDtypeStruct((r, c), F32)
    return pl.pallas_call(body, out_shape=(shp, shp, shp), grid=(r // tm,), in_specs=[blk] * 4, out_specs=(blk,) * 3,
                          name=name, compiler_params=_params("parallel"))(w, g, m, v)


def _position():
    return lax.axis_index("x"), lax.axis_index("y"), lax.axis_index("c")


def _gather_exchange(shards):
    na = len(shards)

    def plan(ins, outs, sems):
        send_sems, recv_sems, local_sems = sems
        x, y, c = _position()
        me, sibling = (x, y, c), (x, y, 1 - c)
        chips = [(1 - x, y), (x, 1 - y), (1 - x, 1 - y)]

        def slot(a, pos):
            return outs[a].at[4 * pos[0] + 2 * pos[1] + pos[2]]

        def copy(a, k, block, to, src=None):
            return pltpu.make_async_remote_copy(
                src_ref=slot(a, block) if src is None else src, dst_ref=slot(a, block),
                send_sem=send_sems.at[a, k], recv_sem=recv_sems.at[a, k], device_id=to, device_id_type=MESH)

        mine = [pltpu.make_async_copy(ins[a], slot(a, me), local_sems.at[a]) for a in range(na)]
        first = []
        for a in range(na):
            first.append(copy(a, 0, me, sibling, src=ins[a]))
            first += [copy(a, 1 + j, me, (*chip, c), src=ins[a]) for j, chip in enumerate(chips)]
        return me, sibling, chips, copy, mine, first

    def start(ins, outs, sems):
        *_, mine, first = plan(ins, outs, sems)
        for cp in mine + first:
            cp.start()

    def finish(ins, outs, sems):
        me, sibling, chips, copy, mine, first = plan(ins, outs, sems)
        c = me[2]
        passed = []
        for a in range(na):
            for j, chip in enumerate(chips):
                copy(a, 1 + j, (*chip, c), me).wait_recv()
                fwd = copy(a, 4 + j, (*chip, c), sibling)
                fwd.start()
                passed.append(fwd)
        for a in range(na):
            copy(a, 0, sibling, me).wait_recv()
            for j, chip in enumerate(chips):
                copy(a, 4 + j, (*chip, 1 - c), me).wait_recv()
        for cp in first + passed:
            cp.wait_send()
        for cp in mine:
            cp.wait()

    return _Exchange(list(shards), [jax.ShapeDtypeStruct((N_DEV,) + s.shape, s.dtype) for s in shards],
                     [pltpu.SemaphoreType.DMA((na, 7)), pltpu.SemaphoreType.DMA((na, 7)), pltpu.SemaphoreType.DMA((na,))],
                     start, finish)


def _scatter_exchange(arrays):
    na = len(arrays)

    def copies(ins, outs, sems):
        send_sems, recv_sems = sems
        x, y, c = _position()
        out = []
        for a in range(na):
            for k in range(7):
                flip = k + 1
                peer = (x ^ (flip >> 2), y ^ ((flip >> 1) & 1), c ^ (flip & 1))
                peer_block = 4 * peer[0] + 2 * peer[1] + peer[2]
                out.append(pltpu.make_async_remote_copy(
                    src_ref=ins[a].at[peer_block], dst_ref=outs[a].at[k],
                    send_sem=send_sems.at[a, k], recv_sem=recv_sems.at[a, k], device_id=peer, device_id_type=MESH))
        return out

    def start(ins, outs, sems):
        for cp in copies(ins, outs, sems):
            cp.start()

    def finish(ins, outs, sems):
        for cp in copies(ins, outs, sems):
            cp.wait()

    return _Exchange(list(arrays), [jax.ShapeDtypeStruct((7,) + s.shape[1:], s.dtype) for s in arrays],
                     [pltpu.SemaphoreType.DMA((na, 7)), pltpu.SemaphoreType.DMA((na, 7))], start, finish)


def _exchange_alone(side, *, name):
    n_in = len(side.arrays)
    n_out = len(side.out_shape)

    def body(*refs):
        ins, outs, sems = refs[:n_in], refs[n_in:n_in + n_out], refs[n_in + n_out:]
        side.start(ins, outs, sems)
        side.finish(ins, outs, sems)

    hbm = pl.BlockSpec(memory_space=pl.ANY)
    return pl.pallas_call(body, out_shape=tuple(side.out_shape), in_specs=[hbm] * n_in, out_specs=tuple([hbm] * n_out),
                          scratch_shapes=list(side.scratch), name=name)(*side.arrays)


def _reduce_blocks(own, recv, *, name):
    r, c = own.shape
    tm = _tile(r, 256)

    def body(own_ref, recv_ref, o_ref):
        acc = own_ref[...]
        for k in range(7):
            acc = acc + recv_ref[k].astype(F32)
        o_ref[...] = acc

    return pl.pallas_call(
        body, out_shape=jax.ShapeDtypeStruct((r, c), F32), grid=(r // tm,),
        in_specs=[pl.BlockSpec((tm, c), lambda i: (i, 0)), pl.BlockSpec((7, tm, c), lambda i: (0, i, 0))],
        out_specs=pl.BlockSpec((tm, c), lambda i: (i, 0)), name=name, compiler_params=_params("parallel"))(own, recv)


def _all_reduce_small(vec, *, name):
    r, c = vec.shape

    def body(v_ref, o_ref, buf, send_sems, recv_sems):
        x, y, c_ = _position()
        me = 4 * x + 2 * y + c_
        buf[me] = v_ref[...]
        copies = []
        for k in range(7):
            flip = k + 1
            peer = (x ^ (flip >> 2), y ^ ((flip >> 1) & 1), c_ ^ (flip & 1))
            cp = pltpu.make_async_remote_copy(
                src_ref=v_ref, dst_ref=buf.at[me], send_sem=send_sems.at[k], recv_sem=recv_sems.at[k],
                device_id=peer, device_id_type=MESH)
            cp.start()
            copies.append(cp)
        for cp in copies:
            cp.wait()
        acc = buf[0]
        for d in range(1, N_DEV):
            acc = acc + buf[d]
        o_ref[...] = acc

    vm = pl.BlockSpec(memory_space=pltpu.VMEM)
    return pl.pallas_call(
        body, out_shape=jax.ShapeDtypeStruct((r, c), F32), in_specs=[vm], out_specs=vm,
        scratch_shapes=[pltpu.VMEM((N_DEV, r, c), F32), pltpu.SemaphoreType.DMA((7,)), pltpu.SemaphoreType.DMA((7,))],
        name=name)(vec)


def _pad_lanes(v, width=LANES):
    return jnp.pad(v, ((0, 0), (0, width - v.shape[1])))


def kernel(x, ffn1_pre_g, ffn1_w_gate, ffn1_w_up, ffn1_w_down, ffn1_post_g, mix_pre_g, w_in, conv_w, conv_b, dt_bias, a_log, d_skip, ssm_norm_g, w_ssm_proj, attn_sinks, rel_bias_table, w_attn_proj, w_out, mix_post_g, ffn2_pre_g, ffn2_w_gate, ffn2_w_up, ffn2_w_down, ffn2_post_g, loss_target, m_ffn1_pre_g, m_ffn1_w_gate, m_ffn1_w_up, m_ffn1_w_down, m_ffn1_post_g, m_mix_pre_g, m_w_in, m_conv_w, m_conv_b, m_dt_bias, m_a_log, m_d_skip, m_ssm_norm_g, m_w_ssm_proj, m_attn_sinks, m_rel_bias_table, m_w_attn_proj, m_w_out, m_mix_post_g, m_ffn2_pre_g, m_ffn2_w_gate, m_ffn2_w_up, m_ffn2_w_down, m_ffn2_post_g, v_ffn1_pre_g, v_ffn1_w_gate, v_ffn1_w_up, v_ffn1_w_down, v_ffn1_post_g, v_mix_pre_g, v_w_in, v_conv_w, v_conv_b, v_dt_bias, v_a_log, v_d_skip, v_ssm_norm_g, v_w_ssm_proj, v_attn_sinks, v_rel_bias_table, v_w_attn_proj, v_w_out, v_mix_post_g, v_ffn2_pre_g, v_ffn2_w_gate, v_ffn2_w_up, v_ffn2_w_down, v_ffn2_post_g):
    args = dict(locals())
    weight_names = ['ffn1_pre_g', 'ffn1_w_gate', 'ffn1_w_up', 'ffn1_w_down', 'ffn1_post_g', 'mix_pre_g', 'w_in', 'conv_w', 'conv_b',
                    'dt_bias', 'a_log', 'd_skip', 'ssm_norm_g', 'w_ssm_proj', 'attn_sinks', 'rel_bias_table', 'w_attn_proj', 'w_out',
                    'mix_post_g', 'ffn2_pre_g', 'ffn2_w_gate', 'ffn2_w_up', 'ffn2_w_down', 'ffn2_post_g']
    col_sharded = ('ffn1_w_gate', 'ffn1_w_up', 'w_in', 'ffn2_w_gate', 'ffn2_w_up')
    row_sharded = ('ffn1_w_down', 'w_ssm_proj', 'w_attn_proj', 'w_out', 'ffn2_w_down')
    big = col_sharded + row_sharded

    bl, s_len, d = x.shape
    t = bl * s_len
    d_inner = ssm_norm_g.shape[1]
    n_heads = dt_bias.shape[1]
    gn = SSM_GROUPS * SSM_STATE
    conv_dim = d_inner + 2 * gn
    q_dim = ATTN_Q_HEADS * ATTN_HEAD_DIM
    kv_dim = ATTN_KV_HEADS * ATTN_HEAD_DIM

    def local_2d(name, a):
        a = a[0]
        return a.T if name in col_sharded else a

    ffn1_names = ('ffn1_w_gate', 'ffn1_w_up', 'ffn1_w_down')
    ffn2_names = ('ffn2_w_gate', 'ffn2_w_up', 'ffn2_w_down')
    mixer_names = ('w_ssm_proj', 'w_attn_proj', 'w_out')

    def shard(n):
        return local_2d(n, args[n]).astype(BF16)

    def rows(g):
        return g.reshape(N_DEV * g.shape[1], g.shape[2])

    first_names = ffn1_names[:2]
    full = {n: rows(g) for n, g in zip(first_names, _exchange_alone(_gather_exchange([shard(n) for n in first_names]), name="gather_ffn1"))}

    x2 = x.reshape(t, d)
    tgt2 = loss_target.reshape(t, d)

    (h1,), saved1, got_in, got_mixer = _ffn_forward(
        x2, ffn1_pre_g, full['ffn1_w_gate'], full['ffn1_w_up'], lambda got: rows(got[2]), ffn1_post_g, "ffn1",
        side_up=_gather_exchange([shard('w_in'), conv_w[0], shard('ffn1_w_down')]),
        side_down=_gather_exchange([shard(n) for n in mixer_names]))
    full['ffn1_w_down'] = rows(got_in[2])
    full.update({n: rows(g) for n, g in zip(mixer_names, got_mixer)})
    conv_w_full = jnp.transpose(got_in[1], (1, 0, 2)).reshape(SSM_CONV, conv_dim)

    win_t = rows(got_in[0])
    dt_lo = 2 * d + d_inner + conv_dim
    n_main = win_t.shape[0] - n_heads
    win_p = jnp.concatenate([win_t[:dt_lo], win_t[dt_lo + n_heads:], win_t[dt_lo:dt_lo + n_heads],
                             jnp.zeros((LANES - n_heads, d), BF16)], axis=0)
    off = {'gs': 0, 'ga': d, 'z': 2 * d, 'xbc': 2 * d + d_inner, 'q': dt_lo, 'kv': dt_lo + q_dim, 'dt': n_main}

    u, (gs, ga, z, xbc, q, kv, dt_raw) = _proj_all(
        h1, mix_pre_g, win_p,
        [(off['gs'], d, BF16), (off['ga'], d, BF16), (off['z'], d_inner, BF16), (off['xbc'], conv_dim, BF16),
         (off['q'], q_dim, BF16), (off['kv'], 2 * kv_dim, BF16), (off['dt'], LANES, F32)], name="mix_proj")

    dtb_p, alog_p, dsk_p, sinks_p = _pad_lanes(dt_bias), _pad_lanes(a_log), _pad_lanes(d_skip), _pad_lanes(attn_sinks)
    xc = _conv_fwd(xbc, conv_w_full, conv_b, bl, name="conv_fwd")
    (y, hprev), got_ffn2 = _ssd_fwd(xc, dt_raw, dtb_p, alog_p, dsk_p, bl, n_heads,
                                    side=_gather_exchange([shard(n) for n in ffn2_names]), name="ssd_fwd")
    full.update({n: rows(g) for n, g in zip(ffn2_names, got_ffn2)})
    yn = _gated_norm_fwd(y, z, ssm_norm_g, name="gated_norm_fwd")

    onehot = _bucket_onehot()
    rep = ATTN_Q_HEADS // ATTN_KV_HEADS
    bias = _small_mm_hi(rel_bias_table.T, onehot, NN, name="rel_bias")
    bias_t = jnp.transpose(bias.reshape(ATTN_KV_HEADS, rep, CHUNK, 2 * CHUNK), (0, 3, 1, 2)).reshape(ATTN_KV_HEADS, 2 * CHUNK, rep * CHUNK)
    o, lse = _attn_fwd(q, kv, bias_t, sinks_p, bl, name="attn_fwd")

    ys, ya, merged, mix, h2 = _merge_fwd(yn, o, gs, ga, full['w_ssm_proj'], full['w_attn_proj'], full['w_out'], h1, mix_post_g,
                                         name="merge_fwd")

    (dh3, loss_vec), saved2, _, _ = _ffn_forward(h2, ffn2_pre_g, full['ffn2_w_gate'], full['ffn2_w_up'], full['ffn2_w_down'],
                                                 ffn2_post_g, "ffn2", target=tgt2)
    loss = lax.psum(loss_vec[0, 0], ("x", "y", "c"))

    grads, own, wire, received = {}, {}, {}, {}
    dh2, grads['ffn2_pre_g'], grads['ffn2_post_g'], g32, g16, _ = _ffn_backward(
        dh3, saved2, ffn2_pre_g, full['ffn2_w_gate'], full['ffn2_w_up'], full['ffn2_w_down'], ffn2_post_g, "ffn2")
    own.update(zip(ffn2_names, map(_stack8, g32)))
    wire.update(zip(ffn2_names, map(_stack8, g16)))

    dmix, dys, dya, dgs, dga, dyn, do, grads['mix_post_g'] = _merge_bwd(
        dh2, mix, mix_post_g, gs, ga, ys, ya, full['w_ssm_proj'], full['w_attn_proj'], full['w_out'], name="merge_bwd")
    for n, (lhs, rhs) in zip(mixer_names, ((yn, dys), (o, dya), (merged, dmix))):
        g32_, g16_, _ = _mm_tn([lhs], rhs, name=f"d{n}")
        own[n], wire[n] = _stack8(g32_), _stack8(g16_)

    dq, dkv, dbias_t, dsinks = _attn_bwd(do, q, kv, lse, bias_t, sinks_p, bl, name="attn_bwd")
    dbias = jnp.transpose(dbias_t.reshape(ATTN_KV_HEADS, 2 * CHUNK, rep, CHUNK), (0, 2, 3, 1)).reshape(ATTN_Q_HEADS, -1)
    d_table = _small_mm_hi(onehot, dbias, NT, name="rel_bias_bwd")

    dy, dz, grads['ssm_norm_g'] = _gated_norm_bwd(dyn, y, z, ssm_norm_g, name="gated_norm_bwd")

    first_group = ffn2_names + mixer_names
    (dxc, ddt_raw, ddtb, dalog, ddsk), got = _ssd_bwd(dy, y, xc, dt_raw, hprev, dtb_p, alog_p, dsk_p, bl, n_heads,
                                                      side=_scatter_exchange([wire[n] for n in first_group]), name="ssd_bwd")
    received.update(zip(first_group, got))
    dxbc, dconv_w8, grads['conv_b'] = _conv_bwd(dxc, xbc, conv_w_full, conv_b, bl, name="conv_bwd")

    wide32, wide16, _ = _mm_tn([dgs, dga, dz, dxbc, dq], u, name="dw_in")
    kv32, kv16, _ = _mm_tn([dkv], u, name="dw_in_kv")
    dt32, dt16, _ = _mm_tn([ddt_raw], u, name="dw_in_dt")

    def original_order(wide, kv_part, dt_part):
        return _stack8(jnp.concatenate([wide[:dt_lo], dt_part[:n_heads], wide[dt_lo:], kv_part], axis=0))
    own['w_in'], wire['w_in'] = original_order(wide32, kv32, dt32), original_order(wide16, kv16, dt16)
    own['conv_w'] = jnp.transpose(dconv_w8[:SSM_CONV].reshape(SSM_CONV, N_DEV, conv_dim // N_DEV), (1, 0, 2))

    segs = [(g_, 0, off[k_]) for g_, k_ in zip([dgs, dga, dz, dxbc, dq, dkv, ddt_raw], ('gs', 'ga', 'z', 'xbc', 'q', 'kv', 'dt'))]
    dh1, grads['mix_pre_g'], got = _mm_nn_rmsbwd(segs, [win_p], h1, mix_pre_g, dh2,
                                                 side=_scatter_exchange([wire['w_in'], own['conv_w']]), name="mix_du")
    received.update(zip(('w_in', 'conv_w'), got))

    dx2, grads['ffn1_pre_g'], grads['ffn1_post_g'], g32, _, got = _ffn_backward(
        dh1, saved1, ffn1_pre_g, full['ffn1_w_gate'], full['ffn1_w_up'], full['ffn1_w_down'], ffn1_post_g, "ffn1", chain=True)
    own.update(zip(ffn1_names, map(_stack8, g32)))
    received.update(zip(ffn1_names, got))

    me = 4 * lax.axis_index("x") + 2 * lax.axis_index("y") + lax.axis_index("c")

    def own_block(a):
        return lax.dynamic_index_in_dim(a, me, 0, keepdims=False)
    reduced = {n: _reduce_blocks(own_block(own[n]), received[n], name=f"reduce_{n}") for n in big}
    conv_sum = _reduce_blocks(own_block(own['conv_w']), received['conv_w'], name="reduce_conv_w")

    grads['dt_bias'], grads['a_log'], grads['d_skip'] = ddtb[:, :n_heads], dalog[:, :n_heads], ddsk[:, :n_heads]
    grads['attn_sinks'] = dsinks[:, :ATTN_Q_HEADS]
    grads['rel_bias_table'] = d_table
    small = [n for n in weight_names if n not in big and n != 'conv_w']
    flat = jnp.concatenate([grads[n].reshape(-1) for n in small])
    n_small = flat.shape[0]
    n_rows = -(-n_small // (8 * LANES)) * 8
    flat = jnp.pad(flat, (0, n_rows * LANES - n_small)).reshape(n_rows, LANES)
    summed = _all_reduce_small(flat, name="allreduce_small").reshape(-1)
    pos = 0
    for n in small:
        size = grads[n].size
        grads[n] = summed[pos:pos + size].reshape(args[n].shape)
        pos += size

    out_g, out_d, out_m, out_v = {}, {}, {}, {}
    for n in big:
        w2, m2, v2 = local_2d(n, args[n]), local_2d(n, args['m_' + n]), local_2d(n, args['v_' + n])
        dlt, mn, vn = _adamw(w2, reduced[n], m2, v2, name=f"adamw_{n}")

        def back(a, n=n):
            return (a.T if n in col_sharded else a)[None]
        out_g[n], out_d[n], out_m[n], out_v[n] = back(reduced[n]), back(dlt), back(mn), back(vn)

    def pack(prefix):
        vals = [(grads[n] if prefix == 'g' else args[prefix + n]).reshape(-1) for n in small]
        vals.append((conv_sum if prefix == 'g' else args[prefix + 'conv_w']).reshape(-1))
        flat_ = jnp.concatenate(vals)
        rows_ = -(-flat_.shape[0] // (8 * LANES)) * 8
        return jnp.pad(flat_, (0, rows_ * LANES - flat_.shape[0])).reshape(rows_, LANES)

    g_small = pack('g')
    d_small, m_small, v_small = _adamw(pack(''), g_small, pack('m_'), pack('v_'), name="adamw_small")
    pos = 0
    for n in small + ['conv_w']:
        shape = args[n].shape
        size = int(np.prod(shape))
        for dst, src in ((out_g, g_small), (out_d, d_small), (out_m, m_small), (out_v, v_small)):
            dst[n] = src.reshape(-1)[pos:pos + size].reshape(shape)
        pos += size

    grad_x = dx2.reshape(bl, s_len, d)
    return (loss, grad_x, *[out_g[n] for n in weight_names], *[out_d[n] for n in weight_names],
            *[out_m[n] for n in weight_names], *[out_v[n] for n in weight_names])
```

```python
import functools
import math

import numpy as np
import jax
import jax.numpy as jnp
from jax import lax
from jax.experimental import pallas as pl
from jax.experimental.pallas import tpu as pltpu

F32 = jnp.float32
BF16 = jnp.bfloat16
MESH = pl.DeviceIdType.MESH
N_DEV = 8

SSM_HEAD_DIM = 64
SSM_GROUPS = 4
SSM_STATE = 128
SSM_CONV = 4
CHUNK = 128
ATTN_HEAD_DIM = 64
ATTN_Q_HEADS = 16
ATTN_KV_HEADS = 4
REL_BUCKETS = 32
REL_MAX_DISTANCE = 128
RMS_EPS = 1e-6
FFN_RESIDUAL_WEIGHT = 0.5
ADAM_LR, ADAM_B1, ADAM_B2, ADAM_EPS, ADAM_WD, ADAM_STEP = 0.001, 0.9, 0.999, 1e-08, 0.01, 10

LANES = 128
VMEM_LIMIT_BYTES = 56 * 1024 * 1024
FFN_COL_TILE = 1408
SSD_HEAD_BATCH = 2

NEG_INF = float("-inf")


def _params(*sem):
    return pltpu.CompilerParams(dimension_semantics=sem, vmem_limit_bytes=VMEM_LIMIT_BYTES)


def _tile(n, pref, mult=8):
    if n <= pref:
        return n
    t = (pref // mult) * mult
    while t >= mult:
        if n % t == 0:
            return t
        t -= mult
    return n


def _sigmoid(x):
    return 1.0 / (1.0 + jnp.exp(-x))


def _dot(a, b, dims):
    return lax.dot_general(a, b, (dims, ((), ())), preferred_element_type=F32)


NN = ((1,), (0,))
NT = ((1,), (1,))
TN = ((0,), (0,))


def _dot_hi(a, b, dims=NN):
    return lax.dot_general(a, b, (dims, ((), ())), preferred_element_type=F32, precision=lax.Precision.HIGHEST)


def _const_spec(shape):
    nd = len(shape)
    return pl.BlockSpec(shape, lambda *_: (0,) * nd)


def _resident_spec(shape):
    nd = len(shape)
    return pl.BlockSpec(shape, lambda *_: (0,) * nd, pipeline_mode=pl.Buffered(1))


class _Exchange:
    def __init__(self, arrays, out_shape, scratch, start, finish):
        self.arrays, self.out_shape, self.scratch, self.start, self.finish = arrays, out_shape, scratch, start, finish


def _hosted_call(body, *, grid, in_specs, out_specs, out_shape, scratch_shapes, operands, side, name):
    in_specs, out_specs, out_shape, scratch_shapes = list(in_specs), list(out_specs), list(out_shape), list(scratch_shapes)
    sem = ("arbitrary",) * len(grid)
    if side is None:
        outs = pl.pallas_call(body, out_shape=tuple(out_shape), grid=grid, in_specs=in_specs, out_specs=tuple(out_specs),
                              scratch_shapes=scratch_shapes, name=name, compiler_params=_params(*sem))(*operands)
        return tuple(outs), ()
    n_in, n_out, n_scr = len(in_specs), len(out_shape), len(scratch_shapes)
    s_in, s_out = len(side.arrays), len(side.out_shape)

    def wrapped(*refs):
        refs = list(refs)
        main_in, side_in = refs[:n_in], refs[n_in:n_in + s_in]
        o0 = n_in + s_in
        main_out, side_out = refs[o0:o0 + n_out], refs[o0 + n_out:o0 + n_out + s_out]
        c0 = o0 + n_out + s_out
        main_scr, side_scr = refs[c0:c0 + n_scr], refs[c0 + n_scr:]
        ids = [pl.program_id(ax) for ax in range(len(grid))]
        first = functools.reduce(jnp.logical_and, [i == 0 for i in ids])
        last = functools.reduce(jnp.logical_and, [i == g - 1 for i, g in zip(ids, grid)])

        @pl.when(first)
        def _():
            side.start(side_in, side_out, side_scr)

        body(*main_in, *main_out, *main_scr)

        @pl.when(last)
        def _():
            side.finish(side_in, side_out, side_scr)

    hbm = pl.BlockSpec(memory_space=pl.ANY)
    outs = pl.pallas_call(
        wrapped, out_shape=tuple(out_shape + list(side.out_shape)), grid=grid,
        in_specs=in_specs + [hbm] * s_in, out_specs=tuple(out_specs + [hbm] * s_out),
        scratch_shapes=scratch_shapes + list(side.scratch), name=name, compiler_params=_params(*sem))(*operands, *side.arrays)
    return tuple(outs[:n_out]), tuple(outs[n_out:])


def _proj_all(h, g, w, segs, *, name):
    t, d = h.shape
    tm = _tile(t, 512)
    segs = [(row0, wd_, max(wd_, LANES), dt_) for row0, wd_, dt_ in segs]
    assert all(row0 + out_w <= w.shape[0] for row0, _, out_w, _ in segs)

    def body(h_ref, g_ref, w_ref, u_ref, *o_refs):
        hv = h_ref[...]
        r = lax.rsqrt(jnp.mean(hv * hv, axis=-1, keepdims=True) + RMS_EPS)
        uv = (hv * r * g_ref[...]).astype(BF16)
        u_ref[...] = uv
        for (row0, width, out_w, _), o_ref in zip(segs, o_refs):
            for c0, c1 in _col_chunks(out_w, 8 * LANES):
                part = _dot(uv, w_ref[row0 + c0:row0 + c1, :], NT)
                if width < out_w:
                    part = jnp.where(lax.broadcasted_iota(jnp.int32, part.shape, 1) < width, part, 0.0)
                o_ref[:, c0:c1] = part.astype(o_ref.dtype)

    row = pl.BlockSpec((tm, d), lambda i: (i, 0))
    outs = pl.pallas_call(
        body, out_shape=(jax.ShapeDtypeStruct((t, d), BF16),) + tuple(jax.ShapeDtypeStruct((t, ow), dt_) for _, _, ow, dt_ in segs),
        grid=(t // tm,), in_specs=[row, _const_spec((1, d)), _resident_spec(w.shape)],
        out_specs=(row,) + tuple(pl.BlockSpec((tm, ow), lambda i: (i, 0)) for _, _, ow, _ in segs),
        name=name, compiler_params=_params("parallel"))(h, g, w)
    return outs[0], outs[1:]


def _mm_tn(a_list, b, *, tm=1408, tk=2048, side=None, name):
    t, n = b.shape
    tk = _tile(t, tk if len(a_list) == 1 else tk // 2)
    nk = t // tk
    widths = [a.shape[1] for a in a_list]
    tm = _tile(math.gcd(*widths), tm, LANES)
    assert all(w % tm == 0 for w in widths)
    starts = np.cumsum([0] + [w // tm for w in widths])
    nseg = len(a_list)

    def a_spec(s):
        lo, hi = int(starts[s]), int(starts[s + 1])

        def idx(i, k):
            active = jnp.logical_and(i >= lo, i < hi)
            return (jnp.where(active, k, 0), jnp.clip(i - lo, 0, hi - lo - 1))
        return pl.BlockSpec((tk, tm), idx)

    def body(*refs):
        a_refs, b_ref, o_ref, o16_ref, acc = refs[:nseg], refs[nseg], refs[nseg + 1], refs[nseg + 2], refs[nseg + 3]
        i, k = pl.program_id(0), pl.program_id(1)

        @pl.when(k == 0)
        def _():
            acc[...] = jnp.zeros_like(acc)

        bv = b_ref[...].astype(BF16)
        for s in range(nseg):
            lo, hi = int(starts[s]), int(starts[s + 1])

            @pl.when(jnp.logical_and(i >= lo, i < hi))
            def _(s=s):
                acc[...] += _dot(a_refs[s][...].astype(BF16), bv, TN)

        @pl.when(k == nk - 1)
        def _():
            o_ref[...] = acc[...]
            o16_ref[...] = acc[...].astype(BF16)

    rows = int(starts[-1]) * tm
    o_spec = pl.BlockSpec((tm, n), lambda i, k: (i, 0))
    (o32, o16), got = _hosted_call(
        body, out_shape=[jax.ShapeDtypeStruct((rows, n), F32), jax.ShapeDtypeStruct((rows, n), BF16)], grid=(int(starts[-1]), nk),
        in_specs=[a_spec(s) for s in range(nseg)] + [pl.BlockSpec((tk, n), lambda i, k: (k, 0))],
        out_specs=[o_spec, o_spec], scratch_shapes=[pltpu.VMEM((tm, n), F32)], operands=list(a_list) + [b], side=side, name=name)
    return o32, o16, got


def _mm_nn_rmsbwd(segs, weights, x, g, dres, *, tm=256, side=None, name):
    t, d = x.shape
    tm = _tile(t, tm)
    nseg, nw = len(segs), len(weights)

    def body(*refs):
        a_refs, w_refs = refs[:nseg], refs[nseg:nseg + nw]
        x_ref, g_ref, dres_ref, dx_ref, dg_ref = refs[nseg + nw:]

        @pl.when(pl.program_id(0) == 0)
        def _():
            dg_ref[...] = jnp.zeros_like(dg_ref)

        dn = None
        for s, (a, w_idx, row0) in enumerate(segs):
            part = _dot(a_refs[s][...].astype(BF16), w_refs[w_idx][row0:row0 + a.shape[1], :], NN)
            dn = part if dn is None else dn + part
        xv = x_ref[...]
        r = lax.rsqrt(jnp.mean(xv * xv, axis=-1, keepdims=True) + RMS_EPS)
        xhat = xv * r
        dyg = dn * g_ref[...]
        dx_ref[...] = dres_ref[...] + r * (dyg - xhat * jnp.mean(dyg * xhat, axis=-1, keepdims=True))
        dg_ref[...] += jnp.sum(dn * xhat, axis=0, keepdims=True)

    row = pl.BlockSpec((tm, d), lambda i: (i, 0))
    in_specs = [pl.BlockSpec((tm, a.shape[1]), lambda i: (i, 0)) for a, _, _ in segs]
    in_specs += [_resident_spec(w.shape) for w in weights] + [row, _const_spec((1, d)), row]
    (dx, dg), extra = _hosted_call(
        body, grid=(t // tm,), in_specs=in_specs, out_specs=[row, _const_spec((1, d))],
        out_shape=[jax.ShapeDtypeStruct((t, d), F32), jax.ShapeDtypeStruct((1, d), F32)], scratch_shapes=[],
        operands=[a for a, _, _ in segs] + list(weights) + [x, g, dres], side=side, name=name)
    return dx, dg, extra


def _col_chunks(width, chunk=4 * LANES):
    return [(c0, min(c0 + chunk, width)) for c0 in range(0, width, chunk)]


def _rms_fwd(x, g, *, name):
    t, d = x.shape
    tm = _tile(t, 512)

    def body(x_ref, g_ref, o_ref):
        xv = x_ref[...]
        r = lax.rsqrt(jnp.mean(xv * xv, axis=-1, keepdims=True) + RMS_EPS)
        o_ref[...] = (xv * r * g_ref[...]).astype(o_ref.dtype)

    row = pl.BlockSpec((tm, d), lambda i: (i, 0))
    return pl.pallas_call(body, out_shape=jax.ShapeDtypeStruct((t, d), BF16), grid=(t // tm,),
                          in_specs=[row, _const_spec((1, d))], out_specs=row, name=name, compiler_params=_params("parallel"))(x, g)


def _ffn_up(n, wgt, wut, *, side=None, name):
    t, d = n.shape
    f = wgt.shape[0]
    tm, tn = _tile(t, 512), _tile(f, FFN_COL_TILE, LANES)

    def body(n_ref, wg_ref, wu_ref, g_ref, u_ref, h_ref):
        nv = n_ref[...]
        gv = _dot(nv, wg_ref[...], NT)
        uv = _dot(nv, wu_ref[...], NT)
        g_ref[...] = gv.astype(BF16)
        u_ref[...] = uv.astype(BF16)
        h_ref[...] = (gv * _sigmoid(gv) * uv).astype(BF16)

    w_spec = pl.BlockSpec((tn, d), lambda j, i: (j, 0))
    o_spec = pl.BlockSpec((tm, tn), lambda j, i: (i, j))
    shp = jax.ShapeDtypeStruct((t, f), BF16)
    return _hosted_call(body, grid=(f // tn, t // tm), in_specs=[pl.BlockSpec((tm, d), lambda j, i: (i, 0)), w_spec, w_spec],
                        out_specs=[o_spec, o_spec, o_spec], out_shape=[shp, shp, shp], scratch_shapes=[], operands=[n, wgt, wut],
                        side=side, name=name)


def _rms_residual(acc, h, gp, weight):
    r = lax.rsqrt(jnp.mean(acc * acc, axis=-1, keepdims=True) + RMS_EPS)
    return h + weight * (acc * r * gp)


def _ffn_down(hid, wd, h_in, gp, *, target=None, side=None, name):
    t, f = hid.shape
    d = wd.shape[1]
    tm = _tile(t, 256)
    row = pl.BlockSpec((tm, d), lambda i: (i, 0))
    shp = jax.ShapeDtypeStruct((t, d), F32)
    in_specs = [pl.BlockSpec((tm, f), lambda i: (i, 0)), _resident_spec((f, d)), row, _const_spec((1, d))]

    if target is None:
        def body(hid_ref, wd_ref, hin_ref, gp_ref, f_ref, hout_ref):
            acc = _dot(hid_ref[...], wd_ref[...], NN)
            f_ref[...] = acc
            hout_ref[...] = _rms_residual(acc, hin_ref[...], gp_ref[...], FFN_RESIDUAL_WEIGHT)

        return _hosted_call(body, grid=(t // tm,), in_specs=in_specs, out_specs=[row, row], out_shape=[shp, shp], scratch_shapes=[],
                            operands=[hid, wd, h_in, gp], side=side, name=name)

    def body_loss(hid_ref, wd_ref, hin_ref, gp_ref, tgt_ref, f_ref, dh_ref, loss_ref):
        @pl.when(pl.program_id(0) == 0)
        def _():
            loss_ref[...] = jnp.zeros_like(loss_ref)
        acc = _dot(hid_ref[...], wd_ref[...], NN)
        f_ref[...] = acc
        e = _rms_residual(acc, hin_ref[...], gp_ref[...], FFN_RESIDUAL_WEIGHT) - tgt_ref[...]
        dh_ref[...] = e * (1.0 / d)
        per_row = jnp.sum(e * e, axis=1, keepdims=True) * (1.0 / d)
        loss_ref[...] += 0.5 * jnp.sum(per_row, axis=0, keepdims=True)

    return _hosted_call(body_loss, grid=(t // tm,), in_specs=in_specs + [row], out_specs=[row, row, _const_spec((1, LANES))],
                        out_shape=[shp, shp, jax.ShapeDtypeStruct((1, LANES), F32)], scratch_shapes=[],
                        operands=[hid, wd, h_in, gp, target], side=side, name=name)


def _post_bwd(dh, f, gp, weight, *, name):
    t, d = f.shape
    tm = _tile(t, 512)

    def body(dh_ref, f_ref, gp_ref, df_ref, dgp_ref):
        @pl.when(pl.program_id(0) == 0)
        def _():
            dgp_ref[...] = jnp.zeros_like(dgp_ref)
        fv = f_ref[...]
        dy = weight * dh_ref[...]
        r = lax.rsqrt(jnp.mean(fv * fv, axis=-1, keepdims=True) + RMS_EPS)
        fhat = fv * r
        dyg = dy * gp_ref[...]
        df_ref[...] = (r * (dyg - fhat * jnp.mean(dyg * fhat, axis=-1, keepdims=True))).astype(BF16)
        dgp_ref[...] += jnp.sum(dy * fhat, axis=0, keepdims=True)

    row = pl.BlockSpec((tm, d), lambda i: (i, 0))
    return pl.pallas_call(body, out_shape=(jax.ShapeDtypeStruct((t, d), BF16), jax.ShapeDtypeStruct((1, d), F32)), grid=(t // tm,),
                          in_specs=[row, row, _const_spec((1, d))], out_specs=(row, _const_spec((1, d))),
                          name=name, compiler_params=_params("arbitrary"))(dh, f, gp)


def _ffn_dhid(df, wd, g, u, *, name):
    t, d = df.shape
    f = wd.shape[0]
    tm, tn = _tile(t, 512), _tile(f, FFN_COL_TILE, LANES)

    def body(df_ref, wd_ref, g_ref, u_ref, dg_ref, du_ref):
        dh = _dot(df_ref[...], wd_ref[...], NT)
        gv = g_ref[...].astype(F32)
        uv = u_ref[...].astype(F32)
        sg = _sigmoid(gv)
        silu = gv * sg
        dg_ref[...] = (dh * uv * (sg + silu * (1.0 - sg))).astype(BF16)
        du_ref[...] = (dh * silu).astype(BF16)

    o_spec = pl.BlockSpec((tm, tn), lambda j, i: (i, j))
    shp = jax.ShapeDtypeStruct((t, f), BF16)
    return pl.pallas_call(body, out_shape=(shp, shp), grid=(f // tn, t // tm),
                          in_specs=[pl.BlockSpec((tm, d), lambda j, i: (i, 0)), pl.BlockSpec((tn, d), lambda j, i: (j, 0)), o_spec, o_spec],
                          out_specs=(o_spec, o_spec), name=name, compiler_params=_params("parallel", "arbitrary"))(df, wd, g, u)


def _ffn_forward(h_in, g_pre, wgt, wut, wd, g_post, tag, side_up=None, side_down=None, target=None):
    n = _rms_fwd(h_in, g_pre, name=f"{tag}_prenorm")
    (g, u, hid), got_up = _ffn_up(n, wgt, wut, side=side_up, name=f"{tag}_up")
    wd = wd(got_up) if callable(wd) else wd
    outs, got_down = _ffn_down(hid, wd, h_in, g_post, target=target, side=side_down, name=f"{tag}_down")
    return outs[1:], (h_in, n, g, u, hid, outs[0]), got_up, got_down


def _stack8(g):
    return g.reshape(N_DEV, g.shape[0] // N_DEV, g.shape[1])


def _ffn_backward(dh_out, saved, g_pre, wgt, wut, wd, g_post, tag, chain=False):
    h_in, n, g, u, hid, f = saved

    def side_of(grad16):
        return _scatter_exchange([_stack8(grad16)]) if chain else None

    df, dg_post = _post_bwd(dh_out, f, g_post, FFN_RESIDUAL_WEIGHT, name=f"{tag}_post_bwd")
    dgate, dup = _ffn_dhid(df, wd, g, u, name=f"{tag}_dhid")
    d_wd, d_wd16, _ = _mm_tn([hid], df, name=f"{tag}_dwd")
    d_wgt, d_wgt16, got_wd = _mm_tn([dgate], n, side=side_of(d_wd16), name=f"{tag}_dwg")
    d_wut, d_wut16, got_wg = _mm_tn([dup], n, side=side_of(d_wgt16), name=f"{tag}_dwu")
    dh_in, dg_pre, got_wu = _mm_nn_rmsbwd([(dgate, 0, 0), (dup, 1, 0)], [wgt, wut], h_in, g_pre, dh_out, side=side_of(d_wut16),
                                          name=f"{tag}_dn")
    received = (got_wg[0], got_wu[0], got_wd[0]) if chain else None
    return dh_in, dg_pre, dg_post, (d_wgt, d_wut, d_wd), (d_wgt16, d_wut16, d_wd16), received


CONV_ROWS = 128
HALO = 8


def _taps(w_ref):
    return [w_ref[k:k + 1, :] for k in range(SSM_CONV)]


def _conv_chunk(x_ref, xs, r0, taps, bias):
    xs[HALO + r0:HALO + r0 + CONV_ROWS, :] = x_ref[r0:r0 + CONV_ROWS, :].astype(F32)
    shifted = [xs[HALO + r0 - k:HALO + r0 - k + CONV_ROWS, :] for k in range(SSM_CONV)]
    pre = bias + shifted[0] * taps[SSM_CONV - 1]
    for k in range(1, SSM_CONV):
        pre = pre + shifted[k] * taps[SSM_CONV - 1 - k]
    return shifted, pre


def _fold_rows(a):
    return functools.reduce(jnp.add, [a[i:i + 8] for i in range(0, a.shape[0], 8)])


def _conv_fwd(xbc, conv_w, conv_b, bl, *, name):
    t, c = xbc.shape
    s = t // bl
    tc = LANES
    assert s % CONV_ROWS == 0

    def body(x_ref, w_ref, b_ref, o_ref, xs):
        taps, bias = _taps(w_ref), b_ref[...]
        xs[0:HALO, :] = jnp.zeros((HALO, tc), F32)
        for r0 in range(0, s, CONV_ROWS):
            _, pre = _conv_chunk(x_ref, xs, r0, taps, bias)
            o_ref[r0:r0 + CONV_ROWS, :] = (pre * _sigmoid(pre)).astype(o_ref.dtype)

    blk = pl.BlockSpec((s, tc), lambda b, j: (b, j))
    return pl.pallas_call(body, out_shape=jax.ShapeDtypeStruct((t, c), BF16), grid=(bl, c // tc),
                          in_specs=[blk, pl.BlockSpec((SSM_CONV, tc), lambda b, j: (0, j)), pl.BlockSpec((1, tc), lambda b, j: (0, j))],
                          out_specs=blk, scratch_shapes=[pltpu.VMEM((HALO + s, tc), F32)],
                          name=name, compiler_params=_params("parallel", "arbitrary"))(xbc, conv_w, conv_b)


def _conv_bwd(dxc, xbc, conv_w, conv_b, bl, *, name):
    t, c = xbc.shape
    s = t // bl
    tc = LANES

    def body(dy_ref, x_ref, w_ref, b_ref, dx_ref, dw_ref, db_ref, xs, dpre_s):
        @pl.when(pl.program_id(1) == 0)
        def _():
            dw_ref[...] = jnp.zeros_like(dw_ref)
            db_ref[...] = jnp.zeros_like(db_ref)

        taps, bias = _taps(w_ref), b_ref[...]
        zero8 = jnp.zeros((HALO, tc), F32)
        xs[0:HALO, :] = zero8
        dpre_s[s:s + HALO, :] = zero8
        sums = [zero8] * (SSM_CONV + 1)
        for r0 in range(0, s, CONV_ROWS):
            shifted, pre = _conv_chunk(x_ref, xs, r0, taps, bias)
            sg = _sigmoid(pre)
            dpre = dy_ref[r0:r0 + CONV_ROWS, :].astype(F32) * (sg * (1.0 + pre * (1.0 - sg)))
            dpre_s[r0:r0 + CONV_ROWS, :] = dpre
            sums = [acc + _fold_rows(dpre * sh) for acc, sh in zip(sums[:-1], shifted)] + [sums[-1] + _fold_rows(dpre)]
        for k in range(SSM_CONV):
            dw_ref[SSM_CONV - 1 - k:SSM_CONV - k, :] += jnp.sum(sums[k], axis=0, keepdims=True)
        db_ref[...] += jnp.sum(sums[-1], axis=0, keepdims=True)
        for r0 in range(0, s, CONV_ROWS):
            dx = dpre_s[r0:r0 + CONV_ROWS, :] * taps[SSM_CONV - 1]
            for k in range(1, SSM_CONV):
                dx = dx + dpre_s[r0 + k:r0 + k + CONV_ROWS, :] * taps[SSM_CONV - 1 - k]
            dx_ref[r0:r0 + CONV_ROWS, :] = dx.astype(dx_ref.dtype)

    blk = pl.BlockSpec((s, tc), lambda j, b: (b, j))
    return pl.pallas_call(
        body, out_shape=(jax.ShapeDtypeStruct((t, c), BF16), jax.ShapeDtypeStruct((8, c), F32), jax.ShapeDtypeStruct((1, c), F32)),
        grid=(c // tc, bl),
        in_specs=[blk, blk, pl.BlockSpec((SSM_CONV, tc), lambda j, b: (0, j)), pl.BlockSpec((1, tc), lambda j, b: (0, j))],
        out_specs=(blk, pl.BlockSpec((8, tc), lambda j, b: (0, j)), pl.BlockSpec((1, tc), lambda j, b: (0, j))),
        scratch_shapes=[pltpu.VMEM((HALO + s, tc), F32), pltpu.VMEM((s + HALO, tc), F32)],
        name=name, compiler_params=_params("parallel", "arbitrary"))(dxc, xbc, conv_w, conv_b)


def _softplus(x):
    return jnp.maximum(x, 0.0) + jnp.log1p(jnp.exp(-jnp.abs(x)))


def _hilo_dot(v, m_b, dims=NN):
    hi = v.astype(BF16)
    lo = (v - hi.astype(F32)).astype(BF16)
    return _dot(hi, m_b, dims) + _dot(lo, m_b, dims)


def _ssd_chunk_common(dtraw_ref, dtb_ref, alog_ref, dsk_ref, d_inner):
    q, p = CHUNK, SSM_HEAD_DIM
    shift = p.bit_length() - 1
    assert 1 << shift == p
    dt = _softplus(dtraw_ref[...] + dtb_ref[...])
    a = -jnp.exp(alog_ref[...])
    ii = lax.broadcasted_iota(jnp.int32, (q, q), 0)
    jj = lax.broadcasted_iota(jnp.int32, (q, q), 1)
    causal = ii >= jj
    tril = jnp.where(causal, 1.0, 0.0).astype(F32)
    triu = jnp.where(ii <= jj, 1.0, 0.0).astype(F32)
    a_cs = _dot_hi(tril, dt * a)
    a_cs_t = a_cs.T
    a_last = a_cs[q - 1:q, :]
    e_col = jnp.exp(a_cs)
    dec_end = jnp.exp(a_last - a_cs)
    head_of_col = lax.shift_right_logical(lax.broadcasted_iota(jnp.int32, (LANES, d_inner), 1), shift)
    spread = (lax.broadcasted_iota(jnp.int32, (LANES, d_inner), 0) == head_of_col).astype(BF16)
    wide = _hilo_dot(jnp.concatenate([dt, e_col, dec_end, jnp.broadcast_to(dsk_ref[...], (8, LANES))], axis=0), spread)
    return dict(dt=dt, a=a, a_cs=a_cs, a_cs_t=a_cs_t, a_last=a_last, dec_end=dec_end, causal=causal, triu=triu,
                dt_e=wide[:q], e_e=wide[q:2 * q], dec_e=wide[2 * q:3 * q], dsk_e=wide[3 * q:3 * q + 1])


def _fill_block_diag(bd_ref, src_ref, hpg, col0=0):
    q, p = CHUNK, SSM_HEAD_DIM
    for hh in range(hpg):
        bd_ref[hh * q:(hh + 1) * q, hh * p:(hh + 1) * p] = src_ref[:, col0 + hh * p:col0 + (hh + 1) * p]


def _lane_onehot(h):
    return (lax.broadcasted_iota(jnp.int32, (1, LANES), 1) == h).astype(F32)


def _ssd_fwd(xc, dt_raw, dt_bias, a_log, d_skip, bl, n_heads, *, side=None, name):
    t = xc.shape[0]
    q, p, nst, grp = CHUNK, SSM_HEAD_DIM, SSM_STATE, SSM_GROUPS
    d_inner = n_heads * p
    hpg = n_heads // grp
    hb = min(hpg, SSD_HEAD_BATCH)
    gw = hpg * p
    nc = t // bl // q
    assert d_inner % (grp * nst) == 0 and nst == LANES and hpg % hb == 0

    def body(xs_ref, b_ref, c_ref, dtraw_ref, dtb_ref, alog_ref, dsk_ref, y_ref, hprev_ref, state, m_all, x_bd, xdt_s):
        @pl.when(jnp.logical_and(pl.program_id(0) == 0, pl.program_id(1) == 0))
        def _():
            x_bd[...] = jnp.zeros_like(x_bd)

        @pl.when(pl.program_id(1) == 0)
        def _():
            state[...] = jnp.zeros_like(state)

        cm = _ssd_chunk_common(dtraw_ref, dtb_ref, alog_ref, dsk_ref, d_inner)
        for g in range(grp):
            cols = slice(g * gw, (g + 1) * gw)
            bg = b_ref[:, g * nst:(g + 1) * nst]
            cg = c_ref[:, g * nst:(g + 1) * nst]
            scores = _dot(cg, bg, NT)
            xs = xs_ref[:, cols].astype(F32)
            xdt = xs * cm['dt_e'][:, cols]
            xdt_s[...] = xdt.astype(BF16)
            y_parts = []
            for sub in range(hpg // hb):
                for k in range(hb):
                    h = g * hpg + sub * hb + k
                    seg = cm['a_cs'][:, h:h + 1] - cm['a_cs_t'][h:h + 1, :]
                    m_all[:, k * q:(k + 1) * q] = (scores * jnp.exp(jnp.where(cm['causal'], seg, NEG_INF))).astype(BF16)
                _fill_block_diag(x_bd, xdt_s, hb, sub * hb * p)
                y_parts.append(_dot(m_all[...], x_bd[...], NN))
            hprev = state[g]
            hprev_ref[g] = hprev
            y = jnp.concatenate(y_parts, axis=1) + cm['e_e'][:, cols] * _dot(cg, hprev.astype(BF16), NT)
            y_ref[:, cols] = y + cm['dsk_e'][:, cols] * xs
            st = _dot((xdt * cm['dec_e'][:, cols]).astype(BF16), bg, TN)
            for hh in range(hpg):
                h = g * hpg + hh
                rows = slice(hh * p, (hh + 1) * p)
                state[g, rows, :] = jnp.exp(cm['a_last'][:, h:h + 1]) * hprev[rows] + st[rows]

    gn = grp * nst

    def rowmap(b, c):
        return b * nc + c
    vec = pl.BlockSpec((1, LANES), lambda b, c: (0, 0))
    return _hosted_call(
        body,
        out_shape=[jax.ShapeDtypeStruct((t, d_inner), F32), jax.ShapeDtypeStruct((t // q, grp, gw, nst), F32)],
        grid=(bl, nc),
        in_specs=[pl.BlockSpec((q, d_inner), lambda b, c: (rowmap(b, c), 0)),
                  pl.BlockSpec((q, gn), lambda b, c: (rowmap(b, c), d_inner // gn)),
                  pl.BlockSpec((q, gn), lambda b, c: (rowmap(b, c), d_inner // gn + 1)),
                  pl.BlockSpec((q, LANES), lambda b, c: (rowmap(b, c), 0)), vec, vec, vec],
        out_specs=[pl.BlockSpec((q, d_inner), lambda b, c: (rowmap(b, c), 0)),
                   pl.BlockSpec((None, grp, gw, nst), lambda b, c: (rowmap(b, c), 0, 0, 0))],
        scratch_shapes=[pltpu.VMEM((grp, gw, nst), F32), pltpu.VMEM((q, hb * q), BF16), pltpu.VMEM((hb * q, hb * p), BF16),
                        pltpu.VMEM((q, gw), BF16)],
        operands=[xc, xc, xc, dt_raw, dt_bias, a_log, d_skip], side=side, name=name)


def _ssd_bwd(dy, y, xc, dt_raw, hprev_all, dt_bias, a_log, d_skip, bl, n_heads, *, side=None, name):
    t, c_dim = xc.shape
    q, p, nst, grp = CHUNK, SSM_HEAD_DIM, SSM_STATE, SSM_GROUPS
    d_inner = n_heads * p
    hpg = n_heads // grp
    hb = min(hpg, SSD_HEAD_BATCH)
    gw = hpg * p
    nc = t // bl // q
    gn = grp * nst
    shift = p.bit_length() - 1

    def body(dy_ref, y_ref, xs_ref, b_ref, c_ref, dtraw_ref, hprev_ref, dtb_ref, alog_ref, dsk_ref,
             dxc_ref, ddtraw_ref, ddtb_ref, dalog_ref, ddsk_ref, dstate, mt_all, x_bd, dy_bd, xdt_s):
        @pl.when(jnp.logical_and(pl.program_id(0) == 0, pl.program_id(1) == 0))
        def _():
            ddtb_ref[...] = jnp.zeros_like(ddtb_ref)
            dalog_ref[...] = jnp.zeros_like(dalog_ref)
            ddsk_ref[...] = jnp.zeros_like(ddsk_ref)
            x_bd[...] = jnp.zeros_like(x_bd)
            dy_bd[...] = jnp.zeros_like(dy_bd)

        @pl.when(pl.program_id(1) == 0)
        def _():
            dstate[...] = jnp.zeros_like(dstate)

        cm = _ssd_chunk_common(dtraw_ref, dtb_ref, alog_ref, dsk_ref, d_inner)
        causal = cm['causal']
        upper = cm['triu'] > 0.5
        seg_row = lax.shift_right_logical(lax.broadcasted_iota(jnp.int32, (gw, LANES), 0), shift)
        seg_lane = lax.broadcasted_iota(jnp.int32, (gw, LANES), 1)
        sums = jnp.zeros((5 * q, LANES), F32)
        state_dot = jnp.zeros((1, LANES), F32)
        for g in range(grp):
            cols = slice(g * gw, (g + 1) * gw)
            seg_sum = (seg_row + g * hpg == seg_lane).astype(BF16)
            bg = b_ref[:, g * nst:(g + 1) * nst]
            cg = c_ref[:, g * nst:(g + 1) * nst]
            scores_t = _dot(bg, cg, NT)
            xs = xs_ref[:, cols].astype(F32)
            xdt = xs * cm['dt_e'][:, cols]
            xdt_s[...] = xdt.astype(BF16)
            dyf = dy_ref[:, cols].astype(F32)
            dscores = jnp.zeros((q, q), F32)
            dx_parts = []
            for sub in range(hpg // hb):
                c0 = sub * hb * p
                _fill_block_diag(x_bd, xdt_s, hb, c0)
                _fill_block_diag(dy_bd, dy_ref, hb, g * gw + c0)
                dm_all = _dot(dy_ref[:, g * gw + c0:g * gw + c0 + hb * p], x_bd[...], NT)
                for k in range(hb):
                    h = g * hpg + sub * hb + k
                    blk = slice(k * q, (k + 1) * q)
                    seg = cm['a_cs'][:, h:h + 1] - cm['a_cs_t'][h:h + 1, :]
                    mt_all[:, blk] = (scores_t * jnp.exp(jnp.where(upper, -seg, NEG_INF))).astype(BF16)
                    dscores = dscores + dm_all[:, blk] * jnp.exp(jnp.where(causal, seg, NEG_INF))
                dx_parts.append(_dot(mt_all[...], dy_bd[...], NN))
            hprev = hprev_ref[g]
            hprev_b = hprev.astype(BF16)
            dhn = dstate[g]
            dhn_b = dhn.astype(BF16)
            e_e, dec_e = cm['e_e'][:, cols], cm['dec_e'][:, cols]
            y_scan = y_ref[:, cols] - cm['dsk_e'][:, cols] * xs
            dye_b = (dyf * e_e).astype(BF16)
            dcg = _dot(dye_b, hprev_b, NN)
            dhp = _dot(dye_b, cg, TN)
            bdh = _dot(bg, dhn_b, NT)
            dbg = _dot((xdt * dec_e).astype(BF16), dhn_b, NN)
            dx_diag = jnp.concatenate(dx_parts, axis=1)
            dx = dec_e * bdh + dx_diag
            ds_b = dscores.astype(BF16)
            dcg = dcg + _dot(ds_b, bg, NN)
            dbg = dbg + _dot(ds_b, cg, TN)
            x_rounded = xdt_s[...].astype(F32)
            sums = sums + _hilo_dot(jnp.concatenate([dyf * y_scan, xdt * bdh, x_rounded * dx_diag, dx * xs, dyf * xs], axis=0), seg_sum)
            state_dot = state_dot + jnp.sum(_hilo_dot(dhn * hprev, seg_sum, TN), axis=0, keepdims=True)
            dxc_ref[:, cols] = (dx * cm['dt_e'][:, cols] + cm['dsk_e'][:, cols] * dyf).astype(dxc_ref.dtype)
            dxc_ref[:, d_inner + g * nst:d_inner + (g + 1) * nst] = dbg.astype(dxc_ref.dtype)
            dxc_ref[:, d_inner + gn + g * nst:d_inner + gn + (g + 1) * nst] = dcg.astype(dxc_ref.dtype)
            for hh in range(hpg):
                h = g * hpg + hh
                rows = slice(hh * p, (hh + 1) * p)
                dstate[g, rows, :] = jnp.exp(cm['a_last'][:, h:h + 1]) * dhn[rows] + dhp[rows]
        s_y, s_end, s_diag, s_dt, s_skip = (sums[k * q:(k + 1) * q] for k in range(5))
        dt, a, dec_end = cm['dt'], cm['a'], cm['dec_end']
        last_row = (lax.broadcasted_iota(jnp.int32, (q, 1), 0) == q - 1).astype(F32)
        da_last = jnp.sum(dec_end * s_end, axis=0, keepdims=True) + jnp.exp(cm['a_last']) * state_dot
        da = s_y - dec_end * s_end - s_diag + last_row * da_last
        ddta = _dot_hi(cm['triu'], da)
        ddt = s_dt + ddta * a
        d_a = jnp.sum(ddta * dt, axis=0, keepdims=True)
        ddt_raw = ddt * _sigmoid(dtraw_ref[...] + dtb_ref[...])
        ddtraw_ref[...] = ddt_raw
        ddtb_ref[...] += jnp.sum(ddt_raw, axis=0, keepdims=True)
        dalog_ref[...] += d_a * a
        ddsk_ref[...] += jnp.sum(s_skip, axis=0, keepdims=True)

    def rowmap(b, c):
        return b * nc + (nc - 1 - c)
    vec = pl.BlockSpec((1, LANES), lambda b, c: (0, 0))
    vec_shape = jax.ShapeDtypeStruct((1, LANES), F32)
    return _hosted_call(
        body,
        out_shape=[jax.ShapeDtypeStruct((t, c_dim), BF16), jax.ShapeDtypeStruct((t, LANES), F32), vec_shape, vec_shape, vec_shape],
        grid=(bl, nc),
        in_specs=[pl.BlockSpec((q, d_inner), lambda b, c: (rowmap(b, c), 0)),
                  pl.BlockSpec((q, d_inner), lambda b, c: (rowmap(b, c), 0)),
                  pl.BlockSpec((q, d_inner), lambda b, c: (rowmap(b, c), 0)),
                  pl.BlockSpec((q, gn), lambda b, c: (rowmap(b, c), d_inner // gn)),
                  pl.BlockSpec((q, gn), lambda b, c: (rowmap(b, c), d_inner // gn + 1)),
                  pl.BlockSpec((q, LANES), lambda b, c: (rowmap(b, c), 0)),
                  pl.BlockSpec((None, grp, gw, nst), lambda b, c: (rowmap(b, c), 0, 0, 0)), vec, vec, vec],
        out_specs=[pl.BlockSpec((q, c_dim), lambda b, c: (rowmap(b, c), 0)),
                   pl.BlockSpec((q, LANES), lambda b, c: (rowmap(b, c), 0)), vec, vec, vec],
        scratch_shapes=[pltpu.VMEM((grp, gw, nst), F32), pltpu.VMEM((q, hb * q), BF16),
                        pltpu.VMEM((hb * q, hb * p), BF16), pltpu.VMEM((hb * q, hb * p), BF16), pltpu.VMEM((q, gw), BF16)],
        operands=[dy, y, xc, xc, xc, dt_raw, hprev_all, dt_bias, a_log, d_skip], side=side, name=name)


def _gated_norm_fwd(y, z, ng, *, name):
    t, d = y.shape
    tm = _tile(t, 256)
    gw = d // SSM_GROUPS

    def body(y_ref, z_ref, ng_ref, o_ref):
        for g in range(SSM_GROUPS):
            sl = slice(g * gw, (g + 1) * gw)
            zv = z_ref[:, sl].astype(F32)
            yg = y_ref[:, sl] * (zv * _sigmoid(zv))
            r = lax.rsqrt(jnp.mean(yg * yg, axis=-1, keepdims=True) + RMS_EPS)
            o_ref[:, sl] = (yg * r * ng_ref[:, sl]).astype(o_ref.dtype)

    row = pl.BlockSpec((tm, d), lambda i: (i, 0))
    return pl.pallas_call(body, out_shape=jax.ShapeDtypeStruct((t, d), BF16), grid=(t // tm,),
                          in_specs=[row, row, _const_spec((1, d))], out_specs=row, name=name, compiler_params=_params("parallel"))(y, z, ng)


def _gated_norm_bwd(dyn, y, z, ng, *, name):
    t, d = y.shape
    tm = _tile(t, 256)
    gw = d // SSM_GROUPS

    def body(dyn_ref, y_ref, z_ref, ng_ref, dy_ref, dz_ref, dng_ref):
        @pl.when(pl.program_id(0) == 0)
        def _():
            dng_ref[...] = jnp.zeros_like(dng_ref)
        for g in range(SSM_GROUPS):
            sl = slice(g * gw, (g + 1) * gw)
            zv = z_ref[:, sl].astype(F32)
            yv = y_ref[:, sl]
            sg = _sigmoid(zv)
            sz = zv * sg
            yg = yv * sz
            r = lax.rsqrt(jnp.mean(yg * yg, axis=-1, keepdims=True) + RMS_EPS)
            yhat = yg * r
            dn = dyn_ref[:, sl].astype(F32)
            dyg_n = dn * ng_ref[:, sl]
            dyg = r * (dyg_n - yhat * jnp.mean(dyg_n * yhat, axis=-1, keepdims=True))
            dy_ref[:, sl] = (dyg * sz).astype(dy_ref.dtype)
            dz_ref[:, sl] = (dyg * yv * (sg * (1.0 + zv * (1.0 - sg)))).astype(dz_ref.dtype)
            dng_ref[:, sl] += jnp.sum(dn * yhat, axis=0, keepdims=True)

    row = pl.BlockSpec((tm, d), lambda i: (i, 0))
    shp = jax.ShapeDtypeStruct((t, d), BF16)
    return pl.pallas_call(body, out_shape=(shp, shp, jax.ShapeDtypeStruct((1, d), F32)), grid=(t // tm,),
                          in_specs=[row, row, row, _const_spec((1, d))], out_specs=(row, row, _const_spec((1, d))),
                          name=name, compiler_params=_params("arbitrary"))(dyn, y, z, ng)


def _bucket_onehot():
    blk = CHUNK
    qi = jnp.arange(blk)[:, None]
    kj = jnp.arange(2 * blk)[None, :]
    dist = jnp.maximum(qi + blk - kj, 0)
    max_exact = REL_BUCKETS // 2
    d = jnp.maximum(dist, 1).astype(F32)
    large = max_exact + (jnp.log(d / max_exact) / math.log(REL_MAX_DISTANCE / max_exact) * (REL_BUCKETS - max_exact)).astype(jnp.int32)
    large = jnp.minimum(large, REL_BUCKETS - 1)
    bucket = jnp.where(dist < max_exact, dist, large).reshape(-1)
    return (bucket[None, :] == jnp.arange(REL_BUCKETS)[:, None]).astype(F32)


def _small_mm_hi(a, b, dims, *, name):
    def body(a_ref, b_ref, o_ref):
        o_ref[...] = _dot_hi(a_ref[...], b_ref[...], dims)
    n = b.shape[0] if dims == NT else b.shape[1]
    return pl.pallas_call(body, out_shape=jax.ShapeDtypeStruct((a.shape[0], n), F32), name=name)(a, b)


def _attn_band_mask_t(n, rep):
    blk = CHUNK
    jj = lax.broadcasted_iota(jnp.int32, (2 * blk, rep * blk), 0)
    ii = lax.broadcasted_iota(jnp.int32, (2 * blk, rep * blk), 1) & (blk - 1)
    dist = ii + blk - jj
    in_window = jnp.logical_and(dist >= 0, dist < blk)
    return jnp.logical_and(in_window, jnp.logical_or(jj >= blk, n > 0))


def _sink_row(sink_ref, heads):
    return jnp.concatenate([jnp.broadcast_to(sink_ref[:, h:h + 1], (1, CHUNK)) for h in heads], axis=1)


def _attn_fwd(q, kv, bias_t, sinks, bl, *, name):
    t, qd = q.shape
    blk, hd = CHUNK, ATTN_HEAD_DIM
    kvd = ATTN_KV_HEADS * hd
    rep = ATTN_Q_HEADS // ATTN_KV_HEADS
    nb = t // bl // blk
    scale = hd ** -0.5

    def body(q_ref, kp_ref, kc_ref, vp_ref, vc_ref, bias_ref, sink_ref, o_ref, lse_ref):
        n = pl.program_id(1)
        mask = _attn_band_mask_t(n, rep)
        for kvh in range(ATTN_KV_HEADS):
            ks = slice(kvh * hd, (kvh + 1) * hd)
            heads = range(kvh * rep, (kvh + 1) * rep)
            qs = jnp.concatenate([q_ref[:, h * hd:(h + 1) * hd] for h in heads], axis=0)
            kk = jnp.concatenate([kp_ref[:, ks], kc_ref[:, ks]], axis=0)
            vv = jnp.concatenate([vp_ref[:, ks], vc_ref[:, ks]], axis=0)
            s = jnp.where(mask, _dot(kk, qs, NT) * scale + bias_ref[kvh], NEG_INF)
            sink = _sink_row(sink_ref, heads)
            m = jnp.maximum(jnp.max(s, axis=0, keepdims=True), sink)
            p = jnp.exp(s - m)
            den = jnp.sum(p, axis=0, keepdims=True) + jnp.exp(sink - m)
            o = _dot((p * (1.0 / den)).astype(BF16), vv, TN)
            lse = m + jnp.log(den)
            for r, h in enumerate(heads):
                o_ref[:, h * hd:(h + 1) * hd] = o[r * blk:(r + 1) * blk].astype(o_ref.dtype)
                lse_ref[h:h + 1, :] = lse[:, r * blk:(r + 1) * blk]

    def cur(b, n):
        return b * nb + n

    def prev(b, n):
        return b * nb + jnp.maximum(n - 1, 0)
    return pl.pallas_call(
        body, out_shape=(jax.ShapeDtypeStruct((t, qd), BF16), jax.ShapeDtypeStruct((t // blk * ATTN_Q_HEADS, blk), F32)), grid=(bl, nb),
        in_specs=[pl.BlockSpec((blk, qd), lambda b, n: (cur(b, n), 0)),
                  pl.BlockSpec((blk, kvd), lambda b, n: (prev(b, n), 0)), pl.BlockSpec((blk, kvd), lambda b, n: (cur(b, n), 0)),
                  pl.BlockSpec((blk, kvd), lambda b, n: (prev(b, n), 1)), pl.BlockSpec((blk, kvd), lambda b, n: (cur(b, n), 1)),
                  _const_spec(bias_t.shape), _const_spec((1, LANES))],
        out_specs=(pl.BlockSpec((blk, qd), lambda b, n: (cur(b, n), 0)),
                   pl.BlockSpec((ATTN_Q_HEADS, blk), lambda b, n: (cur(b, n), 0))),
        name=name, compiler_params=_params("parallel", "arbitrary"))(q, kv, kv, kv, kv, bias_t, sinks)


def _attn_bwd(do, q, kv, lse, bias_t, sinks, bl, *, name):
    t, qd = q.shape
    blk, hd = CHUNK, ATTN_HEAD_DIM
    kvd = ATTN_KV_HEADS * hd
    rep = ATTN_Q_HEADS // ATTN_KV_HEADS
    s_len = t // bl
    nb = s_len // blk
    scale = hd ** -0.5

    def body(do_ref, q_ref, kp_ref, kc_ref, vp_ref, vc_ref, lse_ref, bias_ref, sink_ref, dq_ref, dkv_ref, dbias_ref, dsink_ref):
        n = pl.program_id(1)

        @pl.when(jnp.logical_and(pl.program_id(0) == 0, n == 0))
        def _():
            dbias_ref[...] = jnp.zeros_like(dbias_ref)
            dsink_ref[...] = jnp.zeros_like(dsink_ref)

        mask = _attn_band_mask_t(n, rep)
        r_cur = pl.multiple_of(n * blk, blk)
        r_prev = pl.multiple_of(jnp.maximum(n - 1, 0) * blk, blk)
        dsink = jnp.zeros((1, LANES), F32)
        for kvh in range(ATTN_KV_HEADS):
            ks = slice(kvh * hd, (kvh + 1) * hd)
            heads = range(kvh * rep, (kvh + 1) * rep)
            qs = jnp.concatenate([q_ref[:, h * hd:(h + 1) * hd] for h in heads], axis=0)
            dos = jnp.concatenate([do_ref[:, h * hd:(h + 1) * hd] for h in heads], axis=0)
            kk = jnp.concatenate([kp_ref[:, ks], kc_ref[:, ks]], axis=0)
            vv = jnp.concatenate([vp_ref[:, ks], vc_ref[:, ks]], axis=0)
            lse = jnp.concatenate([lse_ref[h:h + 1, :] for h in heads], axis=1)
            p = jnp.exp(jnp.where(mask, _dot(kk, qs, NT) * scale + bias_ref[kvh], NEG_INF) - lse)
            dp = _dot(vv, dos, NT)
            delta = jnp.sum(p * dp, axis=0, keepdims=True)
            ds = p * (dp - delta)
            dsink_row = jnp.exp(_sink_row(sink_ref, heads) - lse) * delta
            dbias_ref[kvh] += ds
            ds_b = ds.astype(BF16)
            dq_s = _dot(ds_b, kk, TN) * scale
            dkk = _dot(ds_b, qs, NN) * scale
            dvv = _dot(p.astype(BF16), dos, NN)
            for r, h in enumerate(heads):
                dq_ref[:, h * hd:(h + 1) * hd] = dq_s[r * blk:(r + 1) * blk].astype(dq_ref.dtype)
                dsink = dsink - jnp.sum(dsink_row[:, r * blk:(r + 1) * blk], axis=1, keepdims=True) * _lane_onehot(h)
            vs = slice(kvd + kvh * hd, kvd + (kvh + 1) * hd)
            dkv_ref[pl.ds(r_cur, blk), ks] = dkk[blk:]
            dkv_ref[pl.ds(r_cur, blk), vs] = dvv[blk:]

            @pl.when(n > 0)
            def _():
                dkv_ref[pl.ds(r_prev, blk), ks] += dkk[:blk]
                dkv_ref[pl.ds(r_prev, blk), vs] += dvv[:blk]
        dsink_ref[...] += dsink

    def cur(b, n):
        return b * nb + n

    def prev(b, n):
        return b * nb + jnp.maximum(n - 1, 0)
    qspec = pl.BlockSpec((blk, qd), lambda b, n: (cur(b, n), 0))
    return pl.pallas_call(
        body,
        out_shape=(jax.ShapeDtypeStruct((t, qd), BF16), jax.ShapeDtypeStruct((t, 2 * kvd), F32),
                   jax.ShapeDtypeStruct(bias_t.shape, F32), jax.ShapeDtypeStruct((1, LANES), F32)),
        grid=(bl, nb),
        in_specs=[qspec, qspec,
                  pl.BlockSpec((blk, kvd), lambda b, n: (prev(b, n), 0)), pl.BlockSpec((blk, kvd), lambda b, n: (cur(b, n), 0)),
                  pl.BlockSpec((blk, kvd), lambda b, n: (prev(b, n), 1)), pl.BlockSpec((blk, kvd), lambda b, n: (cur(b, n), 1)),
                  pl.BlockSpec((ATTN_Q_HEADS, blk), lambda b, n: (cur(b, n), 0)), _const_spec(bias_t.shape), _const_spec((1, LANES))],
        out_specs=(qspec, pl.BlockSpec((s_len, 2 * kvd), lambda b, n: (b, 0)), _const_spec(bias_t.shape), _const_spec((1, LANES))),
        name=name, compiler_params=_params("arbitrary", "arbitrary"))(do, q, kv, kv, kv, kv, lse, bias_t, sinks)


def _merge_fwd(yn, o, gs, ga, w_ssm, w_attn, w_out, h_in, g_post, *, name):
    t, d = h_in.shape
    tm = _tile(t, 256)

    def body(yn_ref, o_ref, gs_ref, ga_ref, ws_ref, wa_ref, wo_ref, hin_ref, gp_ref, ys_ref, ya_ref, mg_ref, mix_ref, hout_ref):
        ys = _dot(yn_ref[...], ws_ref[...], NN)
        ya = _dot(o_ref[...], wa_ref[...], NN)
        merged = (_sigmoid(gs_ref[...].astype(F32)) * ys + _sigmoid(ga_ref[...].astype(F32)) * ya).astype(BF16)
        mix = _dot(merged, wo_ref[...], NN)
        ys_ref[...] = ys.astype(BF16)
        ya_ref[...] = ya.astype(BF16)
        mg_ref[...] = merged
        mix_ref[...] = mix
        hout_ref[...] = _rms_residual(mix, hin_ref[...], gp_ref[...], 1.0)

    def row(w):
        return pl.BlockSpec((tm, w), lambda i: (i, 0))
    bshape = jax.ShapeDtypeStruct((t, d), BF16)
    fshape = jax.ShapeDtypeStruct((t, d), F32)
    return pl.pallas_call(
        body, out_shape=(bshape, bshape, bshape, fshape, fshape), grid=(t // tm,),
        in_specs=[row(yn.shape[1]), row(o.shape[1]), row(d), row(d), _resident_spec(w_ssm.shape), _resident_spec(w_attn.shape),
                  _resident_spec(w_out.shape), row(d), _const_spec((1, d))],
        out_specs=(row(d),) * 5, name=name, compiler_params=_params("parallel"))(yn, o, gs, ga, w_ssm, w_attn, w_out, h_in, g_post)


def _merge_bwd(dh, mix, g_post, gs, ga, ys, ya, w_ssm, w_attn, w_out, *, name):
    t, d = mix.shape
    tm = _tile(t, 256)
    d_ssm, d_attn = w_ssm.shape[0], w_attn.shape[0]

    def body(dh_ref, mix_ref, gp_ref, gs_ref, ga_ref, ys_ref, ya_ref, ws_ref, wa_ref, wo_ref,
             dmix_ref, dys_ref, dya_ref, dgs_ref, dga_ref, dyn_ref, do_ref, dgp_ref):
        @pl.when(pl.program_id(0) == 0)
        def _():
            dgp_ref[...] = jnp.zeros_like(dgp_ref)
        mv = mix_ref[...]
        dy = dh_ref[...]
        r = lax.rsqrt(jnp.mean(mv * mv, axis=-1, keepdims=True) + RMS_EPS)
        mhat = mv * r
        dyg = dy * gp_ref[...]
        dmix = (r * (dyg - mhat * jnp.mean(dyg * mhat, axis=-1, keepdims=True))).astype(BF16)
        dgp_ref[...] += jnp.sum(dy * mhat, axis=0, keepdims=True)
        dmix_ref[...] = dmix
        dmerged = _dot(dmix, wo_ref[...], NT)
        sgs = _sigmoid(gs_ref[...].astype(F32))
        sga = _sigmoid(ga_ref[...].astype(F32))
        dys = (dmerged * sgs).astype(BF16)
        dya = (dmerged * sga).astype(BF16)
        dys_ref[...] = dys
        dya_ref[...] = dya
        dgs_ref[...] = (dmerged * ys_ref[...].astype(F32) * sgs * (1.0 - sgs)).astype(BF16)
        dga_ref[...] = (dmerged * ya_ref[...].astype(F32) * sga * (1.0 - sga)).astype(BF16)
        dyn_ref[...] = _dot(dys, ws_ref[...], NT).astype(BF16)
        do_ref[...] = _dot(dya, wa_ref[...], NT).astype(BF16)

    def row(w):
        return pl.BlockSpec((tm, w), lambda i: (i, 0))

    def bshape(w):
        return jax.ShapeDtypeStruct((t, w), BF16)
    return pl.pallas_call(
        body, out_shape=(bshape(d),) * 5 + (bshape(d_ssm), bshape(d_attn), jax.ShapeDtypeStruct((1, d), F32)), grid=(t // tm,),
        in_specs=[row(d), row(d), _const_spec((1, d)), row(d), row(d), row(d), row(d),
                  _resident_spec(w_ssm.shape), _resident_spec(w_attn.shape), _resident_spec(w_out.shape)],
        out_specs=(row(d),) * 5 + (row(d_ssm), row(d_attn), _const_spec((1, d))),
        name=name, compiler_params=_params("arbitrary"))(dh, mix, g_post, gs, ga, ys, ya, w_ssm, w_attn, w_out)


def _adamw(w, g, m, v, *, name):
    r, c = w.shape
    tm = _tile(r, 256)
    c1 = 1.0 - ADAM_B1 ** ADAM_STEP
    c2 = 1.0 - ADAM_B2 ** ADAM_STEP

    def body(w_ref, g_ref, m_ref, v_ref, d_ref, mo_ref, vo_ref):
        gv = g_ref[...]
        mn = ADAM_B1 * m_ref[...] + (1.0 - ADAM_B1) * gv
        vn = ADAM_B2 * v_ref[...] + (1.0 - ADAM_B2) * (gv * gv)
        mo_ref[...] = mn
        vo_ref[...] = vn
        d_ref[...] = -ADAM_LR * ((mn / c1) / (jnp.sqrt(vn / c2) + ADAM_EPS) + ADAM_WD * w_ref[...])

    blk = pl.BlockSpec((tm, c), lambda i: (i, 0))
    shp = jax.ShapeDtypeStruct((r, c), F32)
    return pl.pallas_call(body, out_shape=(shp, shp, shp), grid=(r // tm,), in_specs=[blk] * 4, out_specs=(blk,) * 3,
                          name=name, compiler_params=_params("parallel"))(w, g, m, v)


def _position():
    return lax.axis_index("x"), lax.axis_index("y"), lax.axis_index("c")


def _gather_exchange(shards):
    na = len(shards)

    def plan(ins, outs, sems):
        send_sems, recv_sems, local_sems = sems
        x, y, c = _position()
        me, sibling = (x, y, c), (x, y, 1 - c)
        chips = [(1 - x, y), (x, 1 - y), (1 - x, 1 - y)]

        def slot(a, pos):
            return outs[a].at[4 * pos[0] + 2 * pos[1] + pos[2]]

        def copy(a, k, block, to, src=None):
            return pltpu.make_async_remote_copy(
                src_ref=slot(a, block) if src is None else src, dst_ref=slot(a, block),
                send_sem=send_sems.at[a, k], recv_sem=recv_sems.at[a, k], device_id=to, device_id_type=MESH)

        mine = [pltpu.make_async_copy(ins[a], slot(a, me), local_sems.at[a]) for a in range(na)]
        first = []
        for a in range(na):
            first.append(copy(a, 0, me, sibling, src=ins[a]))
            first += [copy(a, 1 + j, me, (*chip, c), src=ins[a]) for j, chip in enumerate(chips)]
        return me, sibling, chips, copy, mine, first

    def start(ins, outs, sems):
        *_, mine, first = plan(ins, outs, sems)
        for cp in mine + first:
            cp.start()

    def finish(ins, outs, sems):
        me, sibling, chips, copy, mine, first = plan(ins, outs, sems)
        c = me[2]
        passed = []
        for a in range(na):
            for j, chip in enumerate(chips):
                copy(a, 1 + j, (*chip, c), me).wait_recv()
                fwd = copy(a, 4 + j, (*chip, c), sibling)
                fwd.start()
                passed.append(fwd)
        for a in range(na):
            copy(a, 0, sibling, me).wait_recv()
            for j, chip in enumerate(chips):
                copy(a, 4 + j, (*chip, 1 - c), me).wait_recv()
        for cp in first + passed:
            cp.wait_send()
        for cp in mine:
            cp.wait()

    return _Exchange(list(shards), [jax.ShapeDtypeStruct((N_DEV,) + s.shape, s.dtype) for s in shards],
                     [pltpu.SemaphoreType.DMA((na, 7)), pltpu.SemaphoreType.DMA((na, 7)), pltpu.SemaphoreType.DMA((na,))],
                     start, finish)


def _scatter_exchange(arrays):
    na = len(arrays)

    def copies(ins, outs, sems):
        send_sems, recv_sems = sems
        x, y, c = _position()
        out = []
        for a in range(na):
            for k in range(7):
                flip = k + 1
                peer = (x ^ (flip >> 2), y ^ ((flip >> 1) & 1), c ^ (flip & 1))
                peer_block = 4 * peer[0] + 2 * peer[1] + peer[2]
                out.append(pltpu.make_async_remote_copy(
                    src_ref=ins[a].at[peer_block], dst_ref=outs[a].at[k],
                    send_sem=send_sems.at[a, k], recv_sem=recv_sems.at[a, k], device_id=peer, device_id_type=MESH))
        return out

    def start(ins, outs, sems):
        for cp in copies(ins, outs, sems):
            cp.start()

    def finish(ins, outs, sems):
        for cp in copies(ins, outs, sems):
            cp.wait()

    return _Exchange(list(arrays), [jax.ShapeDtypeStruct((7,) + s.shape[1:], s.dtype) for s in arrays],
                     [pltpu.SemaphoreType.DMA((na, 7)), pltpu.SemaphoreType.DMA((na, 7))], start, finish)


def _exchange_alone(side, *, name):
    n_in = len(side.arrays)
    n_out = len(side.out_shape)

    def body(*refs):
        ins, outs, sems = refs[:n_in], refs[n_in:n_in + n_out], refs[n_in + n_out:]
        side.start(ins, outs, sems)
        side.finish(ins, outs, sems)

    hbm = pl.BlockSpec(memory_space=pl.ANY)
    return pl.pallas_call(body, out_shape=tuple(side.out_shape), in_specs=[hbm] * n_in, out_specs=tuple([hbm] * n_out),
                          scratch_shapes=list(side.scratch), name=name)(*side.arrays)


def _reduce_blocks(own, recv, *, name):
    r, c = own.shape
    tm = _tile(r, 256)

    def body(own_ref, recv_ref, o_ref):
        acc = own_ref[...]
        for k in range(7):
            acc = acc + recv_ref[k].astype(F32)
        o_ref[...] = acc

    return pl.pallas_call(
        body, out_shape=jax.ShapeDtypeStruct((r, c), F32), grid=(r // tm,),
        in_specs=[pl.BlockSpec((tm, c), lambda i: (i, 0)), pl.BlockSpec((7, tm, c), lambda i: (0, i, 0))],
        out_specs=pl.BlockSpec((tm, c), lambda i: (i, 0)), name=name, compiler_params=_params("parallel"))(own, recv)


def _all_reduce_small(vec, *, name):
    r, c = vec.shape

    def body(v_ref, o_ref, buf, send_sems, recv_sems):
        x, y, c_ = _position()
        me = 4 * x + 2 * y + c_
        buf[me] = v_ref[...]
        copies = []
        for k in range(7):
            flip = k + 1
            peer = (x ^ (flip >> 2), y ^ ((flip >> 1) & 1), c_ ^ (flip & 1))
            cp = pltpu.make_async_remote_copy(
                src_ref=v_ref, dst_ref=buf.at[me], send_sem=send_sems.at[k], recv_sem=recv_sems.at[k],
                device_id=peer, device_id_type=MESH)
            cp.start()
            copies.append(cp)
        for cp in copies:
            cp.wait()
        acc = buf[0]
        for d in range(1, N_DEV):
            acc = acc + buf[d]
        o_ref[...] = acc

    vm = pl.BlockSpec(memory_space=pltpu.VMEM)
    return pl.pallas_call(
        body, out_shape=jax.ShapeDtypeStruct((r, c), F32), in_specs=[vm], out_specs=vm,
        scratch_shapes=[pltpu.VMEM((N_DEV, r, c), F32), pltpu.SemaphoreType.DMA((7,)), pltpu.SemaphoreType.DMA((7,))],
        name=name)(vec)


def _pad_lanes(v, width=LANES):
    return jnp.pad(v, ((0, 0), (0, width - v.shape[1])))


def kernel(x, ffn1_pre_g, ffn1_w_gate, ffn1_w_up, ffn1_w_down, ffn1_post_g, mix_pre_g, w_in, conv_w, conv_b, dt_bias, a_log, d_skip, ssm_norm_g, w_ssm_proj, attn_sinks, rel_bias_table, w_attn_proj, w_out, mix_post_g, ffn2_pre_g, ffn2_w_gate, ffn2_w_up, ffn2_w_down, ffn2_post_g, loss_target, m_ffn1_pre_g, m_ffn1_w_gate, m_ffn1_w_up, m_ffn1_w_down, m_ffn1_post_g, m_mix_pre_g, m_w_in, m_conv_w, m_conv_b, m_dt_bias, m_a_log, m_d_skip, m_ssm_norm_g, m_w_ssm_proj, m_attn_sinks, m_rel_bias_table, m_w_attn_proj, m_w_out, m_mix_post_g, m_ffn2_pre_g, m_ffn2_w_gate, m_ffn2_w_up, m_ffn2_w_down, m_ffn2_post_g, v_ffn1_pre_g, v_ffn1_w_gate, v_ffn1_w_up, v_ffn1_w_down, v_ffn1_post_g, v_mix_pre_g, v_w_in, v_conv_w, v_conv_b, v_dt_bias, v_a_log, v_d_skip, v_ssm_norm_g, v_w_ssm_proj, v_attn_sinks, v_rel_bias_table, v_w_attn_proj, v_w_out, v_mix_post_g, v_ffn2_pre_g, v_ffn2_w_gate, v_ffn2_w_up, v_ffn2_w_down, v_ffn2_post_g):
    args = dict(locals())
    weight_names = ['ffn1_pre_g', 'ffn1_w_gate', 'ffn1_w_up', 'ffn1_w_down', 'ffn1_post_g', 'mix_pre_g', 'w_in', 'conv_w', 'conv_b',
                    'dt_bias', 'a_log', 'd_skip', 'ssm_norm_g', 'w_ssm_proj', 'attn_sinks', 'rel_bias_table', 'w_attn_proj', 'w_out',
                    'mix_post_g', 'ffn2_pre_g', 'ffn2_w_gate', 'ffn2_w_up', 'ffn2_w_down', 'ffn2_post_g']
    col_sharded = ('ffn1_w_gate', 'ffn1_w_up', 'w_in', 'ffn2_w_gate', 'ffn2_w_up')
    row_sharded = ('ffn1_w_down', 'w_ssm_proj', 'w_attn_proj', 'w_out', 'ffn2_w_down')
    big = col_sharded + row_sharded

    bl, s_len, d = x.shape
    t = bl * s_len
    d_inner = ssm_norm_g.shape[1]
    n_heads = dt_bias.shape[1]
    gn = SSM_GROUPS * SSM_STATE
    conv_dim = d_inner + 2 * gn
    q_dim = ATTN_Q_HEADS * ATTN_HEAD_DIM
    kv_dim = ATTN_KV_HEADS * ATTN_HEAD_DIM

    def local_2d(name, a):
        a = a[0]
        return a.T if name in col_sharded else a

    ffn1_names = ('ffn1_w_gate', 'ffn1_w_up', 'ffn1_w_down')
    ffn2_names = ('ffn2_w_gate', 'ffn2_w_up', 'ffn2_w_down')
    mixer_names = ('w_ssm_proj', 'w_attn_proj', 'w_out')

    def shard(n):
        return local_2d(n, args[n]).astype(BF16)

    def rows(g):
        return g.reshape(N_DEV * g.shape[1], g.shape[2])

    first_names = ffn1_names[:2]
    full = {n: rows(g) for n, g in zip(first_names, _exchange_alone(_gather_exchange([shard(n) for n in first_names]), name="gather_ffn1"))}

    x2 = x.reshape(t, d)
    tgt2 = loss_target.reshape(t, d)

    (h1,), saved1, got_in, got_mixer = _ffn_forward(
        x2, ffn1_pre_g, full['ffn1_w_gate'], full['ffn1_w_up'], lambda got: rows(got[2]), ffn1_post_g, "ffn1",
        side_up=_gather_exchange([shard('w_in'), conv_w[0], shard('ffn1_w_down')]),
        side_down=_gather_exchange([shard(n) for n in mixer_names]))
    full['ffn1_w_down'] = rows(got_in[2])
    full.update({n: rows(g) for n, g in zip(mixer_names, got_mixer)})
    conv_w_full = jnp.transpose(got_in[1], (1, 0, 2)).reshape(SSM_CONV, conv_dim)

    win_t = rows(got_in[0])
    dt_lo = 2 * d + d_inner + conv_dim
    off = {'gs': 0, 'ga': d, 'z': 2 * d, 'xbc': 2 * d + d_inner, 'dt': dt_lo, 'q': dt_lo + n_heads, 'kv': dt_lo + n_heads + q_dim}
    assert all(o_ % 16 == 0 for o_ in off.values())

    u, (gs, ga, z, xbc, q, kv, dt_raw) = _proj_all(
        h1, mix_pre_g, win_t,
        [(off['gs'], d, BF16), (off['ga'], d, BF16), (off['z'], d_inner, BF16), (off['xbc'], conv_dim, BF16),
         (off['q'], q_dim, BF16), (off['kv'], 2 * kv_dim, BF16), (off['dt'], n_heads, F32)], name="mix_proj")

    dtb_p, alog_p, dsk_p, sinks_p = _pad_lanes(dt_bias), _pad_lanes(a_log), _pad_lanes(d_skip), _pad_lanes(attn_sinks)
    xc = _conv_fwd(xbc, conv_w_full, conv_b, bl, name="conv_fwd")
    (y, hprev), got_ffn2 = _ssd_fwd(xc, dt_raw, dtb_p, alog_p, dsk_p, bl, n_heads,
                                    side=_gather_exchange([shard(n) for n in ffn2_names]), name="ssd_fwd")
    full.update({n: rows(g) for n, g in zip(ffn2_names, got_ffn2)})
    yn = _gated_norm_fwd(y, z, ssm_norm_g, name="gated_norm_fwd")

    onehot = _bucket_onehot()
    rep = ATTN_Q_HEADS // ATTN_KV_HEADS
    bias = _small_mm_hi(rel_bias_table.T, onehot, NN, name="rel_bias")
    bias_t = jnp.transpose(bias.reshape(ATTN_KV_HEADS, rep, CHUNK, 2 * CHUNK), (0, 3, 1, 2)).reshape(ATTN_KV_HEADS, 2 * CHUNK, rep * CHUNK)
    o, lse = _attn_fwd(q, kv, bias_t, sinks_p, bl, name="attn_fwd")

    ys, ya, merged, mix, h2 = _merge_fwd(yn, o, gs, ga, full['w_ssm_proj'], full['w_attn_proj'], full['w_out'], h1, mix_post_g,
                                         name="merge_fwd")

    (dh3, loss_vec), saved2, _, _ = _ffn_forward(h2, ffn2_pre_g, full['ffn2_w_gate'], full['ffn2_w_up'], full['ffn2_w_down'],
                                                 ffn2_post_g, "ffn2", target=tgt2)
    loss = lax.psum(loss_vec[0, 0], ("x", "y", "c"))

    grads, own, wire, received = {}, {}, {}, {}
    dh2, grads['ffn2_pre_g'], grads['ffn2_post_g'], g32, g16, _ = _ffn_backward(
        dh3, saved2, ffn2_pre_g, full['ffn2_w_gate'], full['ffn2_w_up'], full['ffn2_w_down'], ffn2_post_g, "ffn2")
    own.update(zip(ffn2_names, map(_stack8, g32)))
    wire.update(zip(ffn2_names, map(_stack8, g16)))

    dmix, dys, dya, dgs, dga, dyn, do, grads['mix_post_g'] = _merge_bwd(
        dh2, mix, mix_post_g, gs, ga, ys, ya, full['w_ssm_proj'], full['w_attn_proj'], full['w_out'], name="merge_bwd")
    for n, (lhs, rhs) in zip(mixer_names, ((yn, dys), (o, dya), (merged, dmix))):
        g32_, g16_, _ = _mm_tn([lhs], rhs, name=f"d{n}")
        own[n], wire[n] = _stack8(g32_), _stack8(g16_)

    dq, dkv, dbias_t, dsinks = _attn_bwd(do, q, kv, lse, bias_t, sinks_p, bl, name="attn_bwd")
    dbias = jnp.transpose(dbias_t.reshape(ATTN_KV_HEADS, 2 * CHUNK, rep, CHUNK), (0, 2, 3, 1)).reshape(ATTN_Q_HEADS, -1)
    d_table = _small_mm_hi(onehot, dbias, NT, name="rel_bias_bwd")

    dy, dz, grads['ssm_norm_g'] = _gated_norm_bwd(dyn, y, z, ssm_norm_g, name="gated_norm_bwd")

    first_group = ffn2_names + mixer_names
    (dxc, ddt_raw, ddtb, dalog, ddsk), got = _ssd_bwd(dy, y, xc, dt_raw, hprev, dtb_p, alog_p, dsk_p, bl, n_heads,
                                                      side=_scatter_exchange([wire[n] for n in first_group]), name="ssd_bwd")
    received.update(zip(first_group, got))
    dxbc, dconv_w8, grads['conv_b'] = _conv_bwd(dxc, xbc, conv_w_full, conv_b, bl, name="conv_bwd")

    wide32, wide16, _ = _mm_tn([dgs, dga, dz, dxbc, dq], u, name="dw_in")
    kv32, kv16, _ = _mm_tn([dkv], u, name="dw_in_kv")
    dt32, dt16, _ = _mm_tn([ddt_raw], u, name="dw_in_dt")

    def original_order(wide, kv_part, dt_part):
        return jnp.concatenate([wide[:dt_lo], dt_part[:n_heads], wide[dt_lo:], kv_part], axis=0)
    me = 4 * lax.axis_index("x") + 2 * lax.axis_index("y") + lax.axis_index("c")
    blk_rows = win_t.shape[0] // N_DEV
    wire['w_in'] = _stack8(original_order(wide16, kv16, dt16))
    own_w_in = lax.dynamic_slice_in_dim(original_order(wide32, kv32, dt32), me * blk_rows, blk_rows)
    own['conv_w'] = jnp.transpose(dconv_w8[:SSM_CONV].reshape(SSM_CONV, N_DEV, conv_dim // N_DEV), (1, 0, 2))

    segs = [(g_, 0, off[k_]) for g_, k_ in zip([dgs, dga, dz, dxbc, dq, dkv, ddt_raw], ('gs', 'ga', 'z', 'xbc', 'q', 'kv', 'dt'))]
    dh1, grads['mix_pre_g'], got = _mm_nn_rmsbwd(segs, [win_t], h1, mix_pre_g, dh2,
                                                 side=_scatter_exchange([wire['w_in'], own['conv_w']]), name="mix_du")
    received.update(zip(('w_in', 'conv_w'), got))

    dx2, grads['ffn1_pre_g'], grads['ffn1_post_g'], g32, _, got = _ffn_backward(
        dh1, saved1, ffn1_pre_g, full['ffn1_w_gate'], full['ffn1_w_up'], full['ffn1_w_down'], ffn1_post_g, "ffn1", chain=True)
    own.update(zip(ffn1_names, map(_stack8, g32)))
    received.update(zip(ffn1_names, got))

    def own_block(a):
        return lax.dynamic_index_in_dim(a, me, 0, keepdims=False)
    reduced = {n: _reduce_blocks(own_w_in if n == 'w_in' else own_block(own[n]), received[n], name=f"reduce_{n}") for n in big}
    conv_sum = _reduce_blocks(own_block(own['conv_w']), received['conv_w'], name="reduce_conv_w")

    grads['dt_bias'], grads['a_log'], grads['d_skip'] = ddtb[:, :n_heads], dalog[:, :n_heads], ddsk[:, :n_heads]
    grads['attn_sinks'] = dsinks[:, :ATTN_Q_HEADS]
    grads['rel_bias_table'] = d_table
    small = [n for n in weight_names if n not in big and n != 'conv_w']
    flat = jnp.concatenate([grads[n].reshape(-1) for n in small])
    n_small = flat.shape[0]
    n_rows = -(-n_small // (8 * LANES)) * 8
    flat = jnp.pad(flat, (0, n_rows * LANES - n_small)).reshape(n_rows, LANES)
    summed = _all_reduce_small(flat, name="allreduce_small").reshape(-1)
    pos = 0
    for n in small:
        size = grads[n].size
        grads[n] = summed[pos:pos + size].reshape(args[n].shape)
        pos += size

    out_g, out_d, out_m, out_v = {}, {}, {}, {}
    for n in big:
        w2, m2, v2 = local_2d(n, args[n]), local_2d(n, args['m_' + n]), local_2d(n, args['v_' + n])
        dlt, mn, vn = _adamw(w2, reduced[n], m2, v2, name=f"adamw_{n}")

        def back(a, n=n):
            return (a.T if n in col_sharded else a)[None]
        out_g[n], out_d[n], out_m[n], out_v[n] = back(reduced[n]), back(dlt), back(mn), back(vn)

    def pack(prefix):
        vals = [(grads[n] if prefix == 'g' else args[prefix + n]).reshape(-1) for n in small]
        vals.append((conv_sum if prefix == 'g' else args[prefix + 'conv_w']).reshape(-1))
        flat_ = jnp.concatenate(vals)
        rows_ = -(-flat_.shape[0] // (8 * LANES)) * 8
        return jnp.pad(flat_, (0, rows_ * LANES - flat_.shape[0])).reshape(rows_, LANES)

    g_small = pack('g')
    d_small, m_small, v_small = _adamw(pack(''), g_small, pack('m_'), pack('v_'), name="adamw_small")
    pos = 0
    for n in small + ['conv_w']:
        shape = args[n].shape
        size = int(np.prod(shape))
        for dst, src in ((out_g, g_small), (out_d, d_small), (out_m, m_small), (out_v, v_small)):
            dst[n] = src.reshape(-1)[pos:pos + size].reshape(shape)
        pos += size

    grad_x = dx2.reshape(bl, s_len, d)
    return (loss, grad_x, *[out_g[n] for n in weight_names], *[out_d[n] for n in weight_names],
            *[out_m[n] for n in weight_names], *[out_v[n] for n in weight_names])
```

```python
import functools
import math

import numpy as np
import jax
import jax.numpy as jnp
from jax import lax
from jax.experimental import pallas as pl
from jax.experimental.pallas import tpu as pltpu

F32 = jnp.float32
BF16 = jnp.bfloat16
MESH = pl.DeviceIdType.MESH
N_DEV = 8

SSM_HEAD_DIM = 64
SSM_GROUPS = 4
SSM_STATE = 128
SSM_CONV = 4
CHUNK = 128
ATTN_HEAD_DIM = 64
ATTN_Q_HEADS = 16
ATTN_KV_HEADS = 4
REL_BUCKETS = 32
REL_MAX_DISTANCE = 128
RMS_EPS = 1e-6
FFN_RESIDUAL_WEIGHT = 0.5
ADAM_LR, ADAM_B1, ADAM_B2, ADAM_EPS, ADAM_WD, ADAM_STEP = 0.001, 0.9, 0.999, 1e-08, 0.01, 10

LANES = 128
VMEM_LIMIT_BYTES = 56 * 1024 * 1024
FFN_COL_TILE = 1408
SSD_HEAD_BATCH = 2

NEG_INF = float("-inf")


def _params(*sem):
    return pltpu.CompilerParams(dimension_semantics=sem, vmem_limit_bytes=VMEM_LIMIT_BYTES)


def _tile(n, pref, mult=8):
    if n <= pref:
        return n
    t = (pref // mult) * mult
    while t >= mult:
        if n % t == 0:
            return t
        t -= mult
    return n


def _sigmoid(x):
    return 1.0 / (1.0 + jnp.exp(-x))


def _dot(a, b, dims):
    return lax.dot_general(a, b, (dims, ((), ())), preferred_element_type=F32)


NN = ((1,), (0,))
NT = ((1,), (1,))
TN = ((0,), (0,))


def _dot_hi(a, b, dims=NN):
    return lax.dot_general(a, b, (dims, ((), ())), preferred_element_type=F32, precision=lax.Precision.HIGHEST)


def _const_spec(shape):
    nd = len(shape)
    return pl.BlockSpec(shape, lambda *_: (0,) * nd)


def _resident_spec(shape):
    nd = len(shape)
    return pl.BlockSpec(shape, lambda *_: (0,) * nd, pipeline_mode=pl.Buffered(1))


class _Exchange:
    def __init__(self, arrays, out_shape, scratch, start, finish):
        self.arrays, self.out_shape, self.scratch, self.start, self.finish = arrays, out_shape, scratch, start, finish


def _hosted_call(body, *, grid, in_specs, out_specs, out_shape, scratch_shapes, operands, side, name):
    in_specs, out_specs, out_shape, scratch_shapes = list(in_specs), list(out_specs), list(out_shape), list(scratch_shapes)
    sem = ("arbitrary",) * len(grid)
    if side is None:
        outs = pl.pallas_call(body, out_shape=tuple(out_shape), grid=grid, in_specs=in_specs, out_specs=tuple(out_specs),
                              scratch_shapes=scratch_shapes, name=name, compiler_params=_params(*sem))(*operands)
        return tuple(outs), ()
    n_in, n_out, n_scr = len(in_specs), len(out_shape), len(scratch_shapes)
    s_in, s_out = len(side.arrays), len(side.out_shape)

    def wrapped(*refs):
        refs = list(refs)
        main_in, side_in = refs[:n_in], refs[n_in:n_in + s_in]
        o0 = n_in + s_in
        main_out, side_out = refs[o0:o0 + n_out], refs[o0 + n_out:o0 + n_out + s_out]
        c0 = o0 + n_out + s_out
        main_scr, side_scr = refs[c0:c0 + n_scr], refs[c0 + n_scr:]
        ids = [pl.program_id(ax) for ax in range(len(grid))]
        first = functools.reduce(jnp.logical_and, [i == 0 for i in ids])
        last = functools.reduce(jnp.logical_and, [i == g - 1 for i, g in zip(ids, grid)])

        @pl.when(first)
        def _():
            side.start(side_in, side_out, side_scr)

        body(*main_in, *main_out, *main_scr)

        @pl.when(last)
        def _():
            side.finish(side_in, side_out, side_scr)

    hbm = pl.BlockSpec(memory_space=pl.ANY)
    outs = pl.pallas_call(
        wrapped, out_shape=tuple(out_shape + list(side.out_shape)), grid=grid,
        in_specs=in_specs + [hbm] * s_in, out_specs=tuple(out_specs + [hbm] * s_out),
        scratch_shapes=scratch_shapes + list(side.scratch), name=name, compiler_params=_params(*sem))(*operands, *side.arrays)
    return tuple(outs[:n_out]), tuple(outs[n_out:])


def _proj_all(h, g, w, segs, *, name):
    t, d = h.shape
    tm = _tile(t, 512)
    segs = [(row0, wd_, max(wd_, LANES), dt_) for row0, wd_, dt_ in segs]
    assert all(row0 + out_w <= w.shape[0] for row0, _, out_w, _ in segs)

    def body(h_ref, g_ref, w_ref, u_ref, *o_refs):
        hv = h_ref[...]
        r = lax.rsqrt(jnp.mean(hv * hv, axis=-1, keepdims=True) + RMS_EPS)
        uv = (hv * r * g_ref[...]).astype(BF16)
        u_ref[...] = uv
        for (row0, width, out_w, _), o_ref in zip(segs, o_refs):
            for c0, c1 in _col_chunks(out_w, 8 * LANES):
                part = _dot(uv, w_ref[row0 + c0:row0 + c1, :], NT)
                if width < out_w:
                    part = jnp.where(lax.broadcasted_iota(jnp.int32, part.shape, 1) < width, part, 0.0)
                o_ref[:, c0:c1] = part.astype(o_ref.dtype)

    row = pl.BlockSpec((tm, d), lambda i: (i, 0))
    outs = pl.pallas_call(
        body, out_shape=(jax.ShapeDtypeStruct((t, d), BF16),) + tuple(jax.ShapeDtypeStruct((t, ow), dt_) for _, _, ow, dt_ in segs),
        grid=(t // tm,), in_specs=[row, _const_spec((1, d)), _resident_spec(w.shape)],
        out_specs=(row,) + tuple(pl.BlockSpec((tm, ow), lambda i: (i, 0)) for _, _, ow, _ in segs),
        name=name, compiler_params=_params("parallel"))(h, g, w)
    return outs[0], outs[1:]


def _mm_tn(a_list, b, *, tm=1408, tk=2048, side=None, name):
    t, n = b.shape
    tk = _tile(t, tk if len(a_list) == 1 else tk // 2)
    nk = t // tk
    widths = [a.shape[1] for a in a_list]
    tm = _tile(math.gcd(*widths), tm, LANES)
    assert all(w % tm == 0 for w in widths)
    starts = np.cumsum([0] + [w // tm for w in widths])
    nseg = len(a_list)

    def a_spec(s):
        lo, hi = int(starts[s]), int(starts[s + 1])

        def idx(i, k):
            active = jnp.logical_and(i >= lo, i < hi)
            return (jnp.where(active, k, 0), jnp.clip(i - lo, 0, hi - lo - 1))
        return pl.BlockSpec((tk, tm), idx)

    def body(*refs):
        a_refs, b_ref, o_ref, o16_ref, acc = refs[:nseg], refs[nseg], refs[nseg + 1], refs[nseg + 2], refs[nseg + 3]
        i, k = pl.program_id(0), pl.program_id(1)

        @pl.when(k == 0)
        def _():
            acc[...] = jnp.zeros_like(acc)

        bv = b_ref[...].astype(BF16)
        for s in range(nseg):
            lo, hi = int(starts[s]), int(starts[s + 1])

            @pl.when(jnp.logical_and(i >= lo, i < hi))
            def _(s=s):
                acc[...] += _dot(a_refs[s][...].astype(BF16), bv, TN)

        @pl.when(k == nk - 1)
        def _():
            o_ref[...] = acc[...]
            o16_ref[...] = acc[...].astype(BF16)

    rows = int(starts[-1]) * tm
    o_spec = pl.BlockSpec((tm, n), lambda i, k: (i, 0))
    (o32, o16), got = _hosted_call(
        body, out_shape=[jax.ShapeDtypeStruct((rows, n), F32), jax.ShapeDtypeStruct((rows, n), BF16)], grid=(int(starts[-1]), nk),
        in_specs=[a_spec(s) for s in range(nseg)] + [pl.BlockSpec((tk, n), lambda i, k: (k, 0))],
        out_specs=[o_spec, o_spec], scratch_shapes=[pltpu.VMEM((tm, n), F32)], operands=list(a_list) + [b], side=side, name=name)
    return o32, o16, got


def _mm_nn_rmsbwd(segs, weights, x, g, dres, *, tm=512, side=None, name):
    t, d = x.shape
    tm = _tile(t, tm)
    nseg, nw = len(segs), len(weights)

    def body(*refs):
        a_refs, w_refs = refs[:nseg], refs[nseg:nseg + nw]
        x_ref, g_ref, dres_ref, dx_ref, dg_ref = refs[nseg + nw:]

        @pl.when(pl.program_id(0) == 0)
        def _():
            dg_ref[...] = jnp.zeros_like(dg_ref)

        dn = None
        for s, (a, w_idx, row0) in enumerate(segs):
            part = _dot(a_refs[s][...].astype(BF16), w_refs[w_idx][row0:row0 + a.shape[1], :], NN)
            dn = part if dn is None else dn + part
        xv = x_ref[...]
        r = lax.rsqrt(jnp.mean(xv * xv, axis=-1, keepdims=True) + RMS_EPS)
        xhat = xv * r
        dyg = dn * g_ref[...]
        dx_ref[...] = dres_ref[...] + r * (dyg - xhat * jnp.mean(dyg * xhat, axis=-1, keepdims=True))
        dg_ref[...] += jnp.sum(dn * xhat, axis=0, keepdims=True)

    row = pl.BlockSpec((tm, d), lambda i: (i, 0))
    in_specs = [pl.BlockSpec((tm, a.shape[1]), lambda i: (i, 0)) for a, _, _ in segs]
    in_specs += [_resident_spec(w.shape) for w in weights] + [row, _const_spec((1, d)), row]
    (dx, dg), extra = _hosted_call(
        body, grid=(t // tm,), in_specs=in_specs, out_specs=[row, _const_spec((1, d))],
        out_shape=[jax.ShapeDtypeStruct((t, d), F32), jax.ShapeDtypeStruct((1, d), F32)], scratch_shapes=[],
        operands=[a for a, _, _ in segs] + list(weights) + [x, g, dres], side=side, name=name)
    return dx, dg, extra


def _col_chunks(width, chunk=4 * LANES):
    return [(c0, min(c0 + chunk, width)) for c0 in range(0, width, chunk)]


def _rms_fwd(x, g, *, name):
    t, d = x.shape
    tm = _tile(t, 512)

    def body(x_ref, g_ref, o_ref):
        xv = x_ref[...]
        r = lax.rsqrt(jnp.mean(xv * xv, axis=-1, keepdims=True) + RMS_EPS)
        o_ref[...] = (xv * r * g_ref[...]).astype(o_ref.dtype)

    row = pl.BlockSpec((tm, d), lambda i: (i, 0))
    return pl.pallas_call(body, out_shape=jax.ShapeDtypeStruct((t, d), BF16), grid=(t // tm,),
                          in_specs=[row, _const_spec((1, d))], out_specs=row, name=name, compiler_params=_params("parallel"))(x, g)


def _ffn_up(n, wgt, wut, *, side=None, name):
    t, d = n.shape
    f = wgt.shape[0]
    tm, tn = _tile(t, 512), _tile(f, FFN_COL_TILE, LANES)

    def body(n_ref, wg_ref, wu_ref, g_ref, u_ref, h_ref):
        nv = n_ref[...]
        gv = _dot(nv, wg_ref[...], NT)
        uv = _dot(nv, wu_ref[...], NT)
        g_ref[...] = gv.astype(BF16)
        u_ref[...] = uv.astype(BF16)
        h_ref[...] = (gv * _sigmoid(gv) * uv).astype(BF16)

    w_spec = pl.BlockSpec((tn, d), lambda j, i: (j, 0))
    o_spec = pl.BlockSpec((tm, tn), lambda j, i: (i, j))
    shp = jax.ShapeDtypeStruct((t, f), BF16)
    return _hosted_call(body, grid=(f // tn, t // tm), in_specs=[pl.BlockSpec((tm, d), lambda j, i: (i, 0)), w_spec, w_spec],
                        out_specs=[o_spec, o_spec, o_spec], out_shape=[shp, shp, shp], scratch_shapes=[], operands=[n, wgt, wut],
                        side=side, name=name)


def _rms_residual(acc, h, gp, weight):
    r = lax.rsqrt(jnp.mean(acc * acc, axis=-1, keepdims=True) + RMS_EPS)
    return h + weight * (acc * r * gp)


def _ffn_down(hid, wd, h_in, gp, *, target=None, side=None, name):
    t, f = hid.shape
    d = wd.shape[1]
    tm = _tile(t, 256)
    row = pl.BlockSpec((tm, d), lambda i: (i, 0))
    shp = jax.ShapeDtypeStruct((t, d), F32)
    in_specs = [pl.BlockSpec((tm, f), lambda i: (i, 0)), _resident_spec((f, d)), row, _const_spec((1, d))]

    if target is None:
        def body(hid_ref, wd_ref, hin_ref, gp_ref, f_ref, hout_ref):
            acc = _dot(hid_ref[...], wd_ref[...], NN)
            f_ref[...] = acc
            hout_ref[...] = _rms_residual(acc, hin_ref[...], gp_ref[...], FFN_RESIDUAL_WEIGHT)

        return _hosted_call(body, grid=(t // tm,), in_specs=in_specs, out_specs=[row, row], out_shape=[shp, shp], scratch_shapes=[],
                            operands=[hid, wd, h_in, gp], side=side, name=name)

    def body_loss(hid_ref, wd_ref, hin_ref, gp_ref, tgt_ref, f_ref, dh_ref, loss_ref):
        @pl.when(pl.program_id(0) == 0)
        def _():
            loss_ref[...] = jnp.zeros_like(loss_ref)
        acc = _dot(hid_ref[...], wd_ref[...], NN)
        f_ref[...] = acc
        e = _rms_residual(acc, hin_ref[...], gp_ref[...], FFN_RESIDUAL_WEIGHT) - tgt_ref[...]
        dh_ref[...] = e * (1.0 / d)
        per_row = jnp.sum(e * e, axis=1, keepdims=True) * (1.0 / d)
        loss_ref[...] += 0.5 * jnp.sum(per_row, axis=0, keepdims=True)

    return _hosted_call(body_loss, grid=(t // tm,), in_specs=in_specs + [row], out_specs=[row, row, _const_spec((1, LANES))],
                        out_shape=[shp, shp, jax.ShapeDtypeStruct((1, LANES), F32)], scratch_shapes=[],
                        operands=[hid, wd, h_in, gp, target], side=side, name=name)


def _post_bwd(dh, f, gp, weight, *, name):
    t, d = f.shape
    tm = _tile(t, 512)

    def body(dh_ref, f_ref, gp_ref, df_ref, dgp_ref):
        @pl.when(pl.program_id(0) == 0)
        def _():
            dgp_ref[...] = jnp.zeros_like(dgp_ref)
        fv = f_ref[...]
        dy = weight * dh_ref[...]
        r = lax.rsqrt(jnp.mean(fv * fv, axis=-1, keepdims=True) + RMS_EPS)
        fhat = fv * r
        dyg = dy * gp_ref[...]
        df_ref[...] = (r * (dyg - fhat * jnp.mean(dyg * fhat, axis=-1, keepdims=True))).astype(BF16)
        dgp_ref[...] += jnp.sum(dy * fhat, axis=0, keepdims=True)

    row = pl.BlockSpec((tm, d), lambda i: (i, 0))
    return pl.pallas_call(body, out_shape=(jax.ShapeDtypeStruct((t, d), BF16), jax.ShapeDtypeStruct((1, d), F32)), grid=(t // tm,),
                          in_specs=[row, row, _const_spec((1, d))], out_specs=(row, _const_spec((1, d))),
                          name=name, compiler_params=_params("arbitrary"))(dh, f, gp)


def _ffn_dhid(df, wd, g, u, *, name):
    t, d = df.shape
    f = wd.shape[0]
    tm, tn = _tile(t, 512), _tile(f, FFN_COL_TILE, LANES)

    def body(df_ref, wd_ref, g_ref, u_ref, dg_ref, du_ref):
        dh = _dot(df_ref[...], wd_ref[...], NT)
        gv = g_ref[...].astype(F32)
        uv = u_ref[...].astype(F32)
        sg = _sigmoid(gv)
        silu = gv * sg
        dg_ref[...] = (dh * uv * (sg + silu * (1.0 - sg))).astype(BF16)
        du_ref[...] = (dh * silu).astype(BF16)

    o_spec = pl.BlockSpec((tm, tn), lambda j, i: (i, j))
    shp = jax.ShapeDtypeStruct((t, f), BF16)
    return pl.pallas_call(body, out_shape=(shp, shp), grid=(f // tn, t // tm),
                          in_specs=[pl.BlockSpec((tm, d), lambda j, i: (i, 0)), pl.BlockSpec((tn, d), lambda j, i: (j, 0)), o_spec, o_spec],
                          out_specs=(o_spec, o_spec), name=name, compiler_params=_params("parallel", "arbitrary"))(df, wd, g, u)


def _ffn_forward(h_in, g_pre, wgt, wut, wd, g_post, tag, side_up=None, side_down=None, target=None):
    n = _rms_fwd(h_in, g_pre, name=f"{tag}_prenorm")
    (g, u, hid), got_up = _ffn_up(n, wgt, wut, side=side_up, name=f"{tag}_up")
    wd = wd(got_up) if callable(wd) else wd
    outs, got_down = _ffn_down(hid, wd, h_in, g_post, target=target, side=side_down, name=f"{tag}_down")
    return outs[1:], (h_in, n, g, u, hid, outs[0]), got_up, got_down


def _stack8(g):
    return g.reshape(N_DEV, g.shape[0] // N_DEV, g.shape[1])


def _ffn_backward(dh_out, saved, g_pre, wgt, wut, wd, g_post, tag, chain=False):
    h_in, n, g, u, hid, f = saved

    def side_of(grad16):
        return _scatter_exchange([_stack8(grad16)]) if chain else None

    df, dg_post = _post_bwd(dh_out, f, g_post, FFN_RESIDUAL_WEIGHT, name=f"{tag}_post_bwd")
    dgate, dup = _ffn_dhid(df, wd, g, u, name=f"{tag}_dhid")
    d_wd, d_wd16, _ = _mm_tn([hid], df, name=f"{tag}_dwd")
    d_wgt, d_wgt16, got_wd = _mm_tn([dgate], n, side=side_of(d_wd16), name=f"{tag}_dwg")
    d_wut, d_wut16, got_wg = _mm_tn([dup], n, side=side_of(d_wgt16), name=f"{tag}_dwu")
    dh_in, dg_pre, got_wu = _mm_nn_rmsbwd([(dgate, 0, 0), (dup, 1, 0)], [wgt, wut], h_in, g_pre, dh_out, side=side_of(d_wut16),
                                          name=f"{tag}_dn")
    received = (got_wg[0], got_wu[0], got_wd[0]) if chain else None
    return dh_in, dg_pre, dg_post, (d_wgt, d_wut, d_wd), (d_wgt16, d_wut16, d_wd16), received


CONV_ROWS = 128
HALO = 8


def _taps(w_ref):
    return [w_ref[k:k + 1, :] for k in range(SSM_CONV)]


def _conv_chunk(x_ref, xs, r0, taps, bias):
    xs[HALO + r0:HALO + r0 + CONV_ROWS, :] = x_ref[r0:r0 + CONV_ROWS, :].astype(F32)
    shifted = [xs[HALO + r0 - k:HALO + r0 - k + CONV_ROWS, :] for k in range(SSM_CONV)]
    pre = bias + shifted[0] * taps[SSM_CONV - 1]
    for k in range(1, SSM_CONV):
        pre = pre + shifted[k] * taps[SSM_CONV - 1 - k]
    return shifted, pre


def _fold_rows(a):
    return functools.reduce(jnp.add, [a[i:i + 8] for i in range(0, a.shape[0], 8)])


def _conv_fwd(xbc, conv_w, conv_b, bl, *, name):
    t, c = xbc.shape
    s = t // bl
    tc = LANES
    assert s % CONV_ROWS == 0

    def body(x_ref, w_ref, b_ref, o_ref, xs):
        taps, bias = _taps(w_ref), b_ref[...]
        xs[0:HALO, :] = jnp.zeros((HALO, tc), F32)
        for r0 in range(0, s, CONV_ROWS):
            _, pre = _conv_chunk(x_ref, xs, r0, taps, bias)
            o_ref[r0:r0 + CONV_ROWS, :] = (pre * _sigmoid(pre)).astype(o_ref.dtype)

    blk = pl.BlockSpec((s, tc), lambda b, j: (b, j))
    return pl.pallas_call(body, out_shape=jax.ShapeDtypeStruct((t, c), BF16), grid=(bl, c // tc),
                          in_specs=[blk, pl.BlockSpec((SSM_CONV, tc), lambda b, j: (0, j)), pl.BlockSpec((1, tc), lambda b, j: (0, j))],
                          out_specs=blk, scratch_shapes=[pltpu.VMEM((HALO + s, tc), F32)],
                          name=name, compiler_params=_params("parallel", "arbitrary"))(xbc, conv_w, conv_b)


def _conv_bwd(dxc, xbc, conv_w, conv_b, bl, *, name):
    t, c = xbc.shape
    s = t // bl
    tc = LANES

    def body(dy_ref, x_ref, w_ref, b_ref, dx_ref, dw_ref, db_ref, xs, dpre_s):
        @pl.when(pl.program_id(1) == 0)
        def _():
            dw_ref[...] = jnp.zeros_like(dw_ref)
            db_ref[...] = jnp.zeros_like(db_ref)

        taps, bias = _taps(w_ref), b_ref[...]
        zero8 = jnp.zeros((HALO, tc), F32)
        xs[0:HALO, :] = zero8
        dpre_s[s:s + HALO, :] = zero8
        sums = [zero8] * (SSM_CONV + 1)
        for r0 in range(0, s, CONV_ROWS):
            shifted, pre = _conv_chunk(x_ref, xs, r0, taps, bias)
            sg = _sigmoid(pre)
            dpre = dy_ref[r0:r0 + CONV_ROWS, :].astype(F32) * (sg * (1.0 + pre * (1.0 - sg)))
            dpre_s[r0:r0 + CONV_ROWS, :] = dpre
            sums = [acc + _fold_rows(dpre * sh) for acc, sh in zip(sums[:-1], shifted)] + [sums[-1] + _fold_rows(dpre)]
        for k in range(SSM_CONV):
            dw_ref[SSM_CONV - 1 - k:SSM_CONV - k, :] += jnp.sum(sums[k], axis=0, keepdims=True)
        db_ref[...] += jnp.sum(sums[-1], axis=0, keepdims=True)
        for r0 in range(0, s, CONV_ROWS):
            dx = dpre_s[r0:r0 + CONV_ROWS, :] * taps[SSM_CONV - 1]
            for k in range(1, SSM_CONV):
                dx = dx + dpre_s[r0 + k:r0 + k + CONV_ROWS, :] * taps[SSM_CONV - 1 - k]
            dx_ref[r0:r0 + CONV_ROWS, :] = dx.astype(dx_ref.dtype)

    blk = pl.BlockSpec((s, tc), lambda j, b: (b, j))
    return pl.pallas_call(
        body, out_shape=(jax.ShapeDtypeStruct((t, c), BF16), jax.ShapeDtypeStruct((8, c), F32), jax.ShapeDtypeStruct((1, c), F32)),
        grid=(c // tc, bl),
        in_specs=[blk, blk, pl.BlockSpec((SSM_CONV, tc), lambda j, b: (0, j)), pl.BlockSpec((1, tc), lambda j, b: (0, j))],
        out_specs=(blk, pl.BlockSpec((8, tc), lambda j, b: (0, j)), pl.BlockSpec((1, tc), lambda j, b: (0, j))),
        scratch_shapes=[pltpu.VMEM((HALO + s, tc), F32), pltpu.VMEM((s + HALO, tc), F32)],
        name=name, compiler_params=_params("parallel", "arbitrary"))(dxc, xbc, conv_w, conv_b)


def _softplus(x):
    return jnp.maximum(x, 0.0) + jnp.log1p(jnp.exp(-jnp.abs(x)))


def _hilo_dot(v, m_b, dims=NN):
    hi = v.astype(BF16)
    lo = (v - hi.astype(F32)).astype(BF16)
    return _dot(hi, m_b, dims) + _dot(lo, m_b, dims)


def _ssd_chunk_common(dtraw_ref, dtb_ref, alog_ref, dsk_ref, d_inner):
    q, p = CHUNK, SSM_HEAD_DIM
    shift = p.bit_length() - 1
    assert 1 << shift == p
    dt = _softplus(dtraw_ref[...] + dtb_ref[...])
    a = -jnp.exp(alog_ref[...])
    ii = lax.broadcasted_iota(jnp.int32, (q, q), 0)
    jj = lax.broadcasted_iota(jnp.int32, (q, q), 1)
    causal = ii >= jj
    tril = jnp.where(causal, 1.0, 0.0).astype(F32)
    triu = jnp.where(ii <= jj, 1.0, 0.0).astype(F32)
    a_cs = _dot_hi(tril, dt * a)
    a_cs_t = a_cs.T
    a_last = a_cs[q - 1:q, :]
    e_col = jnp.exp(a_cs)
    dec_end = jnp.exp(a_last - a_cs)
    head_of_col = lax.shift_right_logical(lax.broadcasted_iota(jnp.int32, (LANES, d_inner), 1), shift)
    spread = (lax.broadcasted_iota(jnp.int32, (LANES, d_inner), 0) == head_of_col).astype(BF16)
    wide = _hilo_dot(jnp.concatenate([dt, e_col, dec_end, jnp.broadcast_to(dsk_ref[...], (8, LANES))], axis=0), spread)
    return dict(dt=dt, a=a, a_cs=a_cs, a_cs_t=a_cs_t, a_last=a_last, dec_end=dec_end, causal=causal, triu=triu,
                dt_e=wide[:q], e_e=wide[q:2 * q], dec_e=wide[2 * q:3 * q], dsk_e=wide[3 * q:3 * q + 1])


def _fill_block_diag(bd_ref, src_ref, hpg, col0=0):
    q, p = CHUNK, SSM_HEAD_DIM
    for hh in range(hpg):
        bd_ref[hh * q:(hh + 1) * q, hh * p:(hh + 1) * p] = src_ref[:, col0 + hh * p:col0 + (hh + 1) * p]


def _lane_onehot(h):
    return (lax.broadcasted_iota(jnp.int32, (1, LANES), 1) == h).astype(F32)


def _ssd_fwd(xc, dt_raw, dt_bias, a_log, d_skip, bl, n_heads, *, side=None, name):
    t = xc.shape[0]
    q, p, nst, grp = CHUNK, SSM_HEAD_DIM, SSM_STATE, SSM_GROUPS
    d_inner = n_heads * p
    hpg = n_heads // grp
    hb = min(hpg, SSD_HEAD_BATCH)
    gw = hpg * p
    nc = t // bl // q
    assert d_inner % (grp * nst) == 0 and nst == LANES and hpg % hb == 0

    def body(xs_ref, b_ref, c_ref, dtraw_ref, dtb_ref, alog_ref, dsk_ref, y_ref, hprev_ref, state, m_all, x_bd, xdt_s):
        @pl.when(jnp.logical_and(pl.program_id(0) == 0, pl.program_id(1) == 0))
        def _():
            x_bd[...] = jnp.zeros_like(x_bd)

        @pl.when(pl.program_id(1) == 0)
        def _():
            state[...] = jnp.zeros_like(state)

        cm = _ssd_chunk_common(dtraw_ref, dtb_ref, alog_ref, dsk_ref, d_inner)
        for g in range(grp):
            cols = slice(g * gw, (g + 1) * gw)
            bg = b_ref[:, g * nst:(g + 1) * nst]
            cg = c_ref[:, g * nst:(g + 1) * nst]
            scores = _dot(cg, bg, NT)
            xs = xs_ref[:, cols].astype(F32)
            xdt = xs * cm['dt_e'][:, cols]
            xdt_s[...] = xdt.astype(BF16)
            y_parts = []
            for sub in range(hpg // hb):
                for k in range(hb):
                    h = g * hpg + sub * hb + k
                    seg = cm['a_cs'][:, h:h + 1] - cm['a_cs_t'][h:h + 1, :]
                    m_all[:, k * q:(k + 1) * q] = (scores * jnp.exp(jnp.where(cm['causal'], seg, NEG_INF))).astype(BF16)
                _fill_block_diag(x_bd, xdt_s, hb, sub * hb * p)
                y_parts.append(_dot(m_all[...], x_bd[...], NN))
            hprev = state[g]
            hprev_ref[g] = hprev
            y = jnp.concatenate(y_parts, axis=1) + cm['e_e'][:, cols] * _dot(cg, hprev.astype(BF16), NT)
            y_ref[:, cols] = y + cm['dsk_e'][:, cols] * xs
            st = _dot((xdt * cm['dec_e'][:, cols]).astype(BF16), bg, TN)
            for hh in range(hpg):
                h = g * hpg + hh
                rows = slice(hh * p, (hh + 1) * p)
                state[g, rows, :] = jnp.exp(cm['a_last'][:, h:h + 1]) * hprev[rows] + st[rows]

    gn = grp * nst

    def rowmap(b, c):
        return b * nc + c
    vec = pl.BlockSpec((1, LANES), lambda b, c: (0, 0))
    return _hosted_call(
        body,
        out_shape=[jax.ShapeDtypeStruct((t, d_inner), F32), jax.ShapeDtypeStruct((t // q, grp, gw, nst), F32)],
        grid=(bl, nc),
        in_specs=[pl.BlockSpec((q, d_inner), lambda b, c: (rowmap(b, c), 0)),
                  pl.BlockSpec((q, gn), lambda b, c: (rowmap(b, c), d_inner // gn)),
                  pl.BlockSpec((q, gn), lambda b, c: (rowmap(b, c), d_inner // gn + 1)),
                  pl.BlockSpec((q, LANES), lambda b, c: (rowmap(b, c), 0)), vec, vec, vec],
        out_specs=[pl.BlockSpec((q, d_inner), lambda b, c: (rowmap(b, c), 0)),
                   pl.BlockSpec((None, grp, gw, nst), lambda b, c: (rowmap(b, c), 0, 0, 0))],
        scratch_shapes=[pltpu.VMEM((grp, gw, nst), F32), pltpu.VMEM((q, hb * q), BF16), pltpu.VMEM((hb * q, hb * p), BF16),
                        pltpu.VMEM((q, gw), BF16)],
        operands=[xc, xc, xc, dt_raw, dt_bias, a_log, d_skip], side=side, name=name)


def _ssd_bwd(dy, y, xc, dt_raw, hprev_all, dt_bias, a_log, d_skip, bl, n_heads, *, side=None, name):
    t, c_dim = xc.shape
    q, p, nst, grp = CHUNK, SSM_HEAD_DIM, SSM_STATE, SSM_GROUPS
    d_inner = n_heads * p
    hpg = n_heads // grp
    hb = min(hpg, SSD_HEAD_BATCH)
    gw = hpg * p
    nc = t // bl // q
    gn = grp * nst
    shift = p.bit_length() - 1

    def body(dy_ref, y_ref, xs_ref, b_ref, c_ref, dtraw_ref, hprev_ref, dtb_ref, alog_ref, dsk_ref,
             dxc_ref, ddtraw_ref, ddtb_ref, dalog_ref, ddsk_ref, dstate, mt_all, x_bd, dy_bd, xdt_s):
        @pl.when(jnp.logical_and(pl.program_id(0) == 0, pl.program_id(1) == 0))
        def _():
            ddtb_ref[...] = jnp.zeros_like(ddtb_ref)
            dalog_ref[...] = jnp.zeros_like(dalog_ref)
            ddsk_ref[...] = jnp.zeros_like(ddsk_ref)
            x_bd[...] = jnp.zeros_like(x_bd)
            dy_bd[...] = jnp.zeros_like(dy_bd)

        @pl.when(pl.program_id(1) == 0)
        def _():
            dstate[...] = jnp.zeros_like(dstate)

        cm = _ssd_chunk_common(dtraw_ref, dtb_ref, alog_ref, dsk_ref, d_inner)
        causal = cm['causal']
        upper = cm['triu'] > 0.5
        seg_row = lax.shift_right_logical(lax.broadcasted_iota(jnp.int32, (gw, LANES), 0), shift)
        seg_lane = lax.broadcasted_iota(jnp.int32, (gw, LANES), 1)
        sums = jnp.zeros((5 * q, LANES), F32)
        state_dot = jnp.zeros((1, LANES), F32)
        for g in range(grp):
            cols = slice(g * gw, (g + 1) * gw)
            seg_sum = (seg_row + g * hpg == seg_lane).astype(BF16)
            bg = b_ref[:, g * nst:(g + 1) * nst]
            cg = c_ref[:, g * nst:(g + 1) * nst]
            scores_t = _dot(bg, cg, NT)
            xs = xs_ref[:, cols].astype(F32)
            xdt = xs * cm['dt_e'][:, cols]
            xdt_s[...] = xdt.astype(BF16)
            dyf = dy_ref[:, cols].astype(F32)
            dscores = jnp.zeros((q, q), F32)
            dx_parts = []
            for sub in range(hpg // hb):
                c0 = sub * hb * p
                _fill_block_diag(x_bd, xdt_s, hb, c0)
                _fill_block_diag(dy_bd, dy_ref, hb, g * gw + c0)
                dm_all = _dot(dy_ref[:, g * gw + c0:g * gw + c0 + hb * p], x_bd[...], NT)
                for k in range(hb):
                    h = g * hpg + sub * hb + k
                    blk = slice(k * q, (k + 1) * q)
                    seg = cm['a_cs'][:, h:h + 1] - cm['a_cs_t'][h:h + 1, :]
                    mt_all[:, blk] = (scores_t * jnp.exp(jnp.where(upper, -seg, NEG_INF))).astype(BF16)
                    dscores = dscores + dm_all[:, blk] * jnp.exp(jnp.where(causal, seg, NEG_INF))
                dx_parts.append(_dot(mt_all[...], dy_bd[...], NN))
            hprev = hprev_ref[g]
            hprev_b = hprev.astype(BF16)
            dhn = dstate[g]
            dhn_b = dhn.astype(BF16)
            e_e, dec_e = cm['e_e'][:, cols], cm['dec_e'][:, cols]
            y_scan = y_ref[:, cols] - cm['dsk_e'][:, cols] * xs
            dye_b = (dyf * e_e).astype(BF16)
            dcg = _dot(dye_b, hprev_b, NN)
            dhp = _dot(dye_b, cg, TN)
            bdh = _dot(bg, dhn_b, NT)
            dbg = _dot((xdt * dec_e).astype(BF16), dhn_b, NN)
            dx_diag = jnp.concatenate(dx_parts, axis=1)
            dx = dec_e * bdh + dx_diag
            ds_b = dscores.astype(BF16)
            dcg = dcg + _dot(ds_b, bg, NN)
            dbg = dbg + _dot(ds_b, cg, TN)
            x_rounded = xdt_s[...].astype(F32)
            sums = sums + _hilo_dot(jnp.concatenate([dyf * y_scan, xdt * bdh, x_rounded * dx_diag, dx * xs, dyf * xs], axis=0), seg_sum)
            state_dot = state_dot + jnp.sum(_hilo_dot(dhn * hprev, seg_sum, TN), axis=0, keepdims=True)
            dxc_ref[:, cols] = (dx * cm['dt_e'][:, cols] + cm['dsk_e'][:, cols] * dyf).astype(dxc_ref.dtype)
            dxc_ref[:, d_inner + g * nst:d_inner + (g + 1) * nst] = dbg.astype(dxc_ref.dtype)
            dxc_ref[:, d_inner + gn + g * nst:d_inner + gn + (g + 1) * nst] = dcg.astype(dxc_ref.dtype)
            for hh in range(hpg):
                h = g * hpg + hh
                rows = slice(hh * p, (hh + 1) * p)
                dstate[g, rows, :] = jnp.exp(cm['a_last'][:, h:h + 1]) * dhn[rows] + dhp[rows]
        s_y, s_end, s_diag, s_dt, s_skip = (sums[k * q:(k + 1) * q] for k in range(5))
        dt, a, dec_end = cm['dt'], cm['a'], cm['dec_end']
        last_row = (lax.broadcasted_iota(jnp.int32, (q, 1), 0) == q - 1).astype(F32)
        da_last = jnp.sum(dec_end * s_end, axis=0, keepdims=True) + jnp.exp(cm['a_last']) * state_dot
        da = s_y - dec_end * s_end - s_diag + last_row * da_last
        ddta = _dot_hi(cm['triu'], da)
        ddt = s_dt + ddta * a
        d_a = jnp.sum(ddta * dt, axis=0, keepdims=True)
        ddt_raw = ddt * _sigmoid(dtraw_ref[...] + dtb_ref[...])
        ddtraw_ref[...] = ddt_raw
        ddtb_ref[...] += jnp.sum(ddt_raw, axis=0, keepdims=True)
        dalog_ref[...] += d_a * a
        ddsk_ref[...] += jnp.sum(s_skip, axis=0, keepdims=True)

    def rowmap(b, c):
        return b * nc + (nc - 1 - c)
    vec = pl.BlockSpec((1, LANES), lambda b, c: (0, 0))
    vec_shape = jax.ShapeDtypeStruct((1, LANES), F32)
    return _hosted_call(
        body,
        out_shape=[jax.ShapeDtypeStruct((t, c_dim), BF16), jax.ShapeDtypeStruct((t, LANES), F32), vec_shape, vec_shape, vec_shape],
        grid=(bl, nc),
        in_specs=[pl.BlockSpec((q, d_inner), lambda b, c: (rowmap(b, c), 0)),
                  pl.BlockSpec((q, d_inner), lambda b, c: (rowmap(b, c), 0)),
                  pl.BlockSpec((q, d_inner), lambda b, c: (rowmap(b, c), 0)),
                  pl.BlockSpec((q, gn), lambda b, c: (rowmap(b, c), d_inner // gn)),
                  pl.BlockSpec((q, gn), lambda b, c: (rowmap(b, c), d_inner // gn + 1)),
                  pl.BlockSpec((q, LANES), lambda b, c: (rowmap(b, c), 0)),
                  pl.BlockSpec((None, grp, gw, nst), lambda b, c: (rowmap(b, c), 0, 0, 0)), vec, vec, vec],
        out_specs=[pl.BlockSpec((q, c_dim), lambda b, c: (rowmap(b, c), 0)),
                   pl.BlockSpec((q, LANES), lambda b, c: (rowmap(b, c), 0)), vec, vec, vec],
        scratch_shapes=[pltpu.VMEM((grp, gw, nst), F32), pltpu.VMEM((q, hb * q), BF16),
                        pltpu.VMEM((hb * q, hb * p), BF16), pltpu.VMEM((hb * q, hb * p), BF16), pltpu.VMEM((q, gw), BF16)],
        operands=[dy, y, xc, xc, xc, dt_raw, hprev_all, dt_bias, a_log, d_skip], side=side, name=name)


def _gated_norm_fwd(y, z, ng, *, name):
    t, d = y.shape
    tm = _tile(t, 256)
    gw = d // SSM_GROUPS

    def body(y_ref, z_ref, ng_ref, o_ref):
        for g in range(SSM_GROUPS):
            sl = slice(g * gw, (g + 1) * gw)
            zv = z_ref[:, sl].astype(F32)
            yg = y_ref[:, sl] * (zv * _sigmoid(zv))
            r = lax.rsqrt(jnp.mean(yg * yg, axis=-1, keepdims=True) + RMS_EPS)
            o_ref[:, sl] = (yg * r * ng_ref[:, sl]).astype(o_ref.dtype)

    row = pl.BlockSpec((tm, d), lambda i: (i, 0))
    return pl.pallas_call(body, out_shape=jax.ShapeDtypeStruct((t, d), BF16), grid=(t // tm,),
                          in_specs=[row, row, _const_spec((1, d))], out_specs=row, name=name, compiler_params=_params("parallel"))(y, z, ng)


def _gated_norm_bwd(dyn, y, z, ng, *, name):
    t, d = y.shape
    tm = _tile(t, 256)
    gw = d // SSM_GROUPS

    def body(dyn_ref, y_ref, z_ref, ng_ref, dy_ref, dz_ref, dng_ref):
        @pl.when(pl.program_id(0) == 0)
        def _():
            dng_ref[...] = jnp.zeros_like(dng_ref)
        for g in range(SSM_GROUPS):
            sl = slice(g * gw, (g + 1) * gw)
            zv = z_ref[:, sl].astype(F32)
            yv = y_ref[:, sl]
            sg = _sigmoid(zv)
            sz = zv * sg
            yg = yv * sz
            r = lax.rsqrt(jnp.mean(yg * yg, axis=-1, keepdims=True) + RMS_EPS)
            yhat = yg * r
            dn = dyn_ref[:, sl].astype(F32)
            dyg_n = dn * ng_ref[:, sl]
            dyg = r * (dyg_n - yhat * jnp.mean(dyg_n * yhat, axis=-1, keepdims=True))
            dy_ref[:, sl] = (dyg * sz).astype(dy_ref.dtype)
            dz_ref[:, sl] = (dyg * yv * (sg * (1.0 + zv * (1.0 - sg)))).astype(dz_ref.dtype)
            dng_ref[:, sl] += jnp.sum(dn * yhat, axis=0, keepdims=True)

    row = pl.BlockSpec((tm, d), lambda i: (i, 0))
    shp = jax.ShapeDtypeStruct((t, d), BF16)
    return pl.pallas_call(body, out_shape=(shp, shp, jax.ShapeDtypeStruct((1, d), F32)), grid=(t // tm,),
                          in_specs=[row, row, row, _const_spec((1, d))], out_specs=(row, row, _const_spec((1, d))),
                          name=name, compiler_params=_params("arbitrary"))(dyn, y, z, ng)


def _bucket_onehot():
    blk = CHUNK
    qi = jnp.arange(blk)[:, None]
    kj = jnp.arange(2 * blk)[None, :]
    dist = jnp.maximum(qi + blk - kj, 0)
    max_exact = REL_BUCKETS // 2
    d = jnp.maximum(dist, 1).astype(F32)
    large = max_exact + (jnp.log(d / max_exact) / math.log(REL_MAX_DISTANCE / max_exact) * (REL_BUCKETS - max_exact)).astype(jnp.int32)
    large = jnp.minimum(large, REL_BUCKETS - 1)
    bucket = jnp.where(dist < max_exact, dist, large).reshape(-1)
    return (bucket[None, :] == jnp.arange(REL_BUCKETS)[:, None]).astype(F32)


def _small_mm_hi(a, b, dims, *, name):
    def body(a_ref, b_ref, o_ref):
        o_ref[...] = _dot_hi(a_ref[...], b_ref[...], dims)
    n = b.shape[0] if dims == NT else b.shape[1]
    return pl.pallas_call(body, out_shape=jax.ShapeDtypeStruct((a.shape[0], n), F32), name=name)(a, b)


def _attn_band_mask_t(n, rep):
    blk = CHUNK
    jj = lax.broadcasted_iota(jnp.int32, (2 * blk, rep * blk), 0)
    ii = lax.broadcasted_iota(jnp.int32, (2 * blk, rep * blk), 1) & (blk - 1)
    dist = ii + blk - jj
    in_window = jnp.logical_and(dist >= 0, dist < blk)
    return jnp.logical_and(in_window, jnp.logical_or(jj >= blk, n > 0))


def _sink_row(sink_ref, heads):
    return jnp.concatenate([jnp.broadcast_to(sink_ref[:, h:h + 1], (1, CHUNK)) for h in heads], axis=1)


def _attn_fwd(q, kv, bias_t, sinks, bl, *, name):
    t, qd = q.shape
    blk, hd = CHUNK, ATTN_HEAD_DIM
    kvd = ATTN_KV_HEADS * hd
    rep = ATTN_Q_HEADS // ATTN_KV_HEADS
    nb = t // bl // blk
    scale = hd ** -0.5

    def body(q_ref, kp_ref, kc_ref, vp_ref, vc_ref, bias_ref, sink_ref, o_ref, lse_ref):
        n = pl.program_id(1)
        mask = _attn_band_mask_t(n, rep)
        for kvh in range(ATTN_KV_HEADS):
            ks = slice(kvh * hd, (kvh + 1) * hd)
            heads = range(kvh * rep, (kvh + 1) * rep)
            qs = jnp.concatenate([q_ref[:, h * hd:(h + 1) * hd] for h in heads], axis=0)
            kk = jnp.concatenate([kp_ref[:, ks], kc_ref[:, ks]], axis=0)
            vv = jnp.concatenate([vp_ref[:, ks], vc_ref[:, ks]], axis=0)
            s = jnp.where(mask, _dot(kk, qs, NT) * scale + bias_ref[kvh], NEG_INF)
            sink = _sink_row(sink_ref, heads)
            m = jnp.maximum(jnp.max(s, axis=0, keepdims=True), sink)
            p = jnp.exp(s - m)
            den = jnp.sum(p, axis=0, keepdims=True) + jnp.exp(sink - m)
            o = _dot((p * (1.0 / den)).astype(BF16), vv, TN)
            lse = m + jnp.log(den)
            for r, h in enumerate(heads):
                o_ref[:, h * hd:(h + 1) * hd] = o[r * blk:(r + 1) * blk].astype(o_ref.dtype)
                lse_ref[h:h + 1, :] = lse[:, r * blk:(r + 1) * blk]

    def cur(b, n):
        return b * nb + n

    def prev(b, n):
        return b * nb + jnp.maximum(n - 1, 0)
    return pl.pallas_call(
        body, out_shape=(jax.ShapeDtypeStruct((t, qd), BF16), jax.ShapeDtypeStruct((t // blk * ATTN_Q_HEADS, blk), F32)), grid=(bl, nb),
        in_specs=[pl.BlockSpec((blk, qd), lambda b, n: (cur(b, n), 0)),
                  pl.BlockSpec((blk, kvd), lambda b, n: (prev(b, n), 0)), pl.BlockSpec((blk, kvd), lambda b, n: (cur(b, n), 0)),
                  pl.BlockSpec((blk, kvd), lambda b, n: (prev(b, n), 1)), pl.BlockSpec((blk, kvd), lambda b, n: (cur(b, n), 1)),
                  _const_spec(bias_t.shape), _const_spec((1, LANES))],
        out_specs=(pl.BlockSpec((blk, qd), lambda b, n: (cur(b, n), 0)),
                   pl.BlockSpec((ATTN_Q_HEADS, blk), lambda b, n: (cur(b, n), 0))),
        name=name, compiler_params=_params("parallel", "arbitrary"))(q, kv, kv, kv, kv, bias_t, sinks)


def _attn_bwd(do, q, kv, lse, bias_t, sinks, bl, *, name):
    t, qd = q.shape
    blk, hd = CHUNK, ATTN_HEAD_DIM
    kvd = ATTN_KV_HEADS * hd
    rep = ATTN_Q_HEADS // ATTN_KV_HEADS
    s_len = t // bl
    nb = s_len // blk
    scale = hd ** -0.5

    def body(do_ref, q_ref, kp_ref, kc_ref, vp_ref, vc_ref, lse_ref, bias_ref, sink_ref, dq_ref, dkv_ref, dbias_ref, dsink_ref):
        n = pl.program_id(1)

        @pl.when(jnp.logical_and(pl.program_id(0) == 0, n == 0))
        def _():
            dbias_ref[...] = jnp.zeros_like(dbias_ref)
            dsink_ref[...] = jnp.zeros_like(dsink_ref)

        mask = _attn_band_mask_t(n, rep)
        r_cur = pl.multiple_of(n * blk, blk)
        r_prev = pl.multiple_of(jnp.maximum(n - 1, 0) * blk, blk)
        dsink = jnp.zeros((1, LANES), F32)
        for kvh in range(ATTN_KV_HEADS):
            ks = slice(kvh * hd, (kvh + 1) * hd)
            heads = range(kvh * rep, (kvh + 1) * rep)
            qs = jnp.concatenate([q_ref[:, h * hd:(h + 1) * hd] for h in heads], axis=0)
            dos = jnp.concatenate([do_ref[:, h * hd:(h + 1) * hd] for h in heads], axis=0)
            kk = jnp.concatenate([kp_ref[:, ks], kc_ref[:, ks]], axis=0)
            vv = jnp.concatenate([vp_ref[:, ks], vc_ref[:, ks]], axis=0)
            lse = jnp.concatenate([lse_ref[h:h + 1, :] for h in heads], axis=1)
            p = jnp.exp(jnp.where(mask, _dot(kk, qs, NT) * scale + bias_ref[kvh], NEG_INF) - lse)
            dp = _dot(vv, dos, NT)
            delta = jnp.sum(p * dp, axis=0, keepdims=True)
            ds = p * (dp - delta)
            dsink_row = jnp.exp(_sink_row(sink_ref, heads) - lse) * delta
            dbias_ref[kvh] += ds
            ds_b = ds.astype(BF16)
            dq_s = _dot(ds_b, kk, TN) * scale
            dkk = _dot(ds_b, qs, NN) * scale
            dvv = _dot(p.astype(BF16), dos, NN)
            for r, h in enumerate(heads):
                dq_ref[:, h * hd:(h + 1) * hd] = dq_s[r * blk:(r + 1) * blk].astype(dq_ref.dtype)
                dsink = dsink - jnp.sum(dsink_row[:, r * blk:(r + 1) * blk], axis=1, keepdims=True) * _lane_onehot(h)
            vs = slice(kvd + kvh * hd, kvd + (kvh + 1) * hd)
            dkv_ref[pl.ds(r_cur, blk), ks] = dkk[blk:]
            dkv_ref[pl.ds(r_cur, blk), vs] = dvv[blk:]

            @pl.when(n > 0)
            def _():
                dkv_ref[pl.ds(r_prev, blk), ks] += dkk[:blk]
                dkv_ref[pl.ds(r_prev, blk), vs] += dvv[:blk]
        dsink_ref[...] += dsink

    def cur(b, n):
        return b * nb + n

    def prev(b, n):
        return b * nb + jnp.maximum(n - 1, 0)
    qspec = pl.BlockSpec((blk, qd), lambda b, n: (cur(b, n), 0))
    return pl.pallas_call(
        body,
        out_shape=(jax.ShapeDtypeStruct((t, qd), BF16), jax.ShapeDtypeStruct((t, 2 * kvd), F32),
                   jax.ShapeDtypeStruct(bias_t.shape, F32), jax.ShapeDtypeStruct((1, LANES), F32)),
        grid=(bl, nb),
        in_specs=[qspec, qspec,
                  pl.BlockSpec((blk, kvd), lambda b, n: (prev(b, n), 0)), pl.BlockSpec((blk, kvd), lambda b, n: (cur(b, n), 0)),
                  pl.BlockSpec((blk, kvd), lambda b, n: (prev(b, n), 1)), pl.BlockSpec((blk, kvd), lambda b, n: (cur(b, n), 1)),
                  pl.BlockSpec((ATTN_Q_HEADS, blk), lambda b, n: (cur(b, n), 0)), _const_spec(bias_t.shape), _const_spec((1, LANES))],
        out_specs=(qspec, pl.BlockSpec((s_len, 2 * kvd), lambda b, n: (b, 0)), _const_spec(bias_t.shape), _const_spec((1, LANES))),
        name=name, compiler_params=_params("arbitrary", "arbitrary"))(do, q, kv, kv, kv, kv, lse, bias_t, sinks)


def _merge_fwd(yn, o, gs, ga, w_ssm, w_attn, w_out, h_in, g_post, *, name):
    t, d = h_in.shape
    tm = _tile(t, 256)

    def body(yn_ref, o_ref, gs_ref, ga_ref, ws_ref, wa_ref, wo_ref, hin_ref, gp_ref, ys_ref, ya_ref, mg_ref, mix_ref, hout_ref):
        ys = _dot(yn_ref[...], ws_ref[...], NN)
        ya = _dot(o_ref[...], wa_ref[...], NN)
        merged = (_sigmoid(gs_ref[...].astype(F32)) * ys + _sigmoid(ga_ref[...].astype(F32)) * ya).astype(BF16)
        mix = _dot(merged, wo_ref[...], NN)
        ys_ref[...] = ys.astype(BF16)
        ya_ref[...] = ya.astype(BF16)
        mg_ref[...] = merged
        mix_ref[...] = mix
        hout_ref[...] = _rms_residual(mix, hin_ref[...], gp_ref[...], 1.0)

    def row(w):
        return pl.BlockSpec((tm, w), lambda i: (i, 0))
    bshape = jax.ShapeDtypeStruct((t, d), BF16)
    fshape = jax.ShapeDtypeStruct((t, d), F32)
    return pl.pallas_call(
        body, out_shape=(bshape, bshape, bshape, fshape, fshape), grid=(t // tm,),
        in_specs=[row(yn.shape[1]), row(o.shape[1]), row(d), row(d), _resident_spec(w_ssm.shape), _resident_spec(w_attn.shape),
                  _resident_spec(w_out.shape), row(d), _const_spec((1, d))],
        out_specs=(row(d),) * 5, name=name, compiler_params=_params("parallel"))(yn, o, gs, ga, w_ssm, w_attn, w_out, h_in, g_post)


def _merge_bwd(dh, mix, g_post, gs, ga, ys, ya, w_ssm, w_attn, w_out, *, name):
    t, d = mix.shape
    tm = _tile(t, 256)
    d_ssm, d_attn = w_ssm.shape[0], w_attn.shape[0]

    def body(dh_ref, mix_ref, gp_ref, gs_ref, ga_ref, ys_ref, ya_ref, ws_ref, wa_ref, wo_ref,
             dmix_ref, dys_ref, dya_ref, dgs_ref, dga_ref, dyn_ref, do_ref, dgp_ref):
        @pl.when(pl.program_id(0) == 0)
        def _():
            dgp_ref[...] = jnp.zeros_like(dgp_ref)
        mv = mix_ref[...]
        dy = dh_ref[...]
        r = lax.rsqrt(jnp.mean(mv * mv, axis=-1, keepdims=True) + RMS_EPS)
        mhat = mv * r
        dyg = dy * gp_ref[...]
        dmix = (r * (dyg - mhat * jnp.mean(dyg * mhat, axis=-1, keepdims=True))).astype(BF16)
        dgp_ref[...] += jnp.sum(dy * mhat, axis=0, keepdims=True)
        dmix_ref[...] = dmix
        dmerged = _dot(dmix, wo_ref[...], NT)
        sgs = _sigmoid(gs_ref[...].astype(F32))
        sga = _sigmoid(ga_ref[...].astype(F32))
        dys = (dmerged * sgs).astype(BF16)
        dya = (dmerged * sga).astype(BF16)
        dys_ref[...] = dys
        dya_ref[...] = dya
        dgs_ref[...] = (dmerged * ys_ref[...].astype(F32) * sgs * (1.0 - sgs)).astype(BF16)
        dga_ref[...] = (dmerged * ya_ref[...].astype(F32) * sga * (1.0 - sga)).astype(BF16)
        dyn_ref[...] = _dot(dys, ws_ref[...], NT).astype(BF16)
        do_ref[...] = _dot(dya, wa_ref[...], NT).astype(BF16)

    def row(w):
        return pl.BlockSpec((tm, w), lambda i: (i, 0))

    def bshape(w):
        return jax.ShapeDtypeStruct((t, w), BF16)
    return pl.pallas_call(
        body, out_shape=(bshape(d),) * 5 + (bshape(d_ssm), bshape(d_attn), jax.ShapeDtypeStruct((1, d), F32)), grid=(t // tm,),
        in_specs=[row(d), row(d), _const_spec((1, d)), row(d), row(d), row(d), row(d),
                  _resident_spec(w_ssm.shape), _resident_spec(w_attn.shape), _resident_spec(w_out.shape)],
        out_specs=(row(d),) * 5 + (row(d_ssm), row(d_attn), _const_spec((1, d))),
        name=name, compiler_params=_params("arbitrary"))(dh, mix, g_post, gs, ga, ys, ya, w_ssm, w_attn, w_out)


def _adamw(w, g, m, v, *, name):
    r, c = w.shape
    tm = _tile(r, 256)
    c1 = 1.0 - ADAM_B1 ** ADAM_STEP
    c2 = 1.0 - ADAM_B2 ** ADAM_STEP

    def body(w_ref, g_ref, m_ref, v_ref, d_ref, mo_ref, vo_ref):
        gv = g_ref[...]
        mn = ADAM_B1 * m_ref[...] + (1.0 - ADAM_B1) * gv
        vn = ADAM_B2 * v_ref[...] + (1.0 - ADAM_B2) * (gv * gv)
        mo_ref[...] = mn
        vo_ref[...] = vn
        d_ref[...] = -ADAM_LR * ((mn / c1) / (jnp.sqrt(vn / c2) + ADAM_EPS) + ADAM_WD * w_ref[...])

    blk = pl.BlockSpec((tm, c), lambda i: (i, 0))
    shp = jax.ShapeDtypeStruct((r, c), F32)
    return pl.pallas_call(body, out_shape=(shp, shp, shp), grid=(r // tm,), in_specs=[blk] * 4, out_specs=(blk,) * 3,
                          name=name, compiler_params=_params("parallel"))(w, g, m, v)


def _position():
    return lax.axis_index("x"), lax.axis_index("y"), lax.axis_index("c")


def _gather_exchange(shards):
    na = len(shards)

    def plan(ins, outs, sems):
        send_sems, recv_sems, local_sems = sems
        x, y, c = _position()
        me, sibling = (x, y, c), (x, y, 1 - c)
        chips = [(1 - x, y), (x, 1 - y), (1 - x, 1 - y)]

        def slot(a, pos):
            return outs[a].at[4 * pos[0] + 2 * pos[1] + pos[2]]

        def copy(a, k, block, to, src=None):
            return pltpu.make_async_remote_copy(
                src_ref=slot(a, block) if src is None else src, dst_ref=slot(a, block),
                send_sem=send_sems.at[a, k], recv_sem=recv_sems.at[a, k], device_id=to, device_id_type=MESH)

        mine = [pltpu.make_async_copy(ins[a], slot(a, me), local_sems.at[a]) for a in range(na)]
        first = []
        for a in range(na):
            first.append(copy(a, 0, me, sibling, src=ins[a]))
            first += [copy(a, 1 + j, me, (*chip, c), src=ins[a]) for j, chip in enumerate(chips)]
        return me, sibling, chips, copy, mine, first

    def start(ins, outs, sems):
        *_, mine, first = plan(ins, outs, sems)
        for cp in mine + first:
            cp.start()

    def finish(ins, outs, sems):
        me, sibling, chips, copy, mine, first = plan(ins, outs, sems)
        c = me[2]
        passed = []
        for a in range(na):
            for j, chip in enumerate(chips):
                copy(a, 1 + j, (*chip, c), me).wait_recv()
                fwd = copy(a, 4 + j, (*chip, c), sibling)
                fwd.start()
                passed.append(fwd)
        for a in range(na):
            copy(a, 0, sibling, me).wait_recv()
            for j, chip in enumerate(chips):
                copy(a, 4 + j, (*chip, 1 - c), me).wait_recv()
        for cp in first + passed:
            cp.wait_send()
        for cp in mine:
            cp.wait()

    return _Exchange(list(shards), [jax.ShapeDtypeStruct((N_DEV,) + s.shape, s.dtype) for s in shards],
                     [pltpu.SemaphoreType.DMA((na, 7)), pltpu.SemaphoreType.DMA((na, 7)), pltpu.SemaphoreType.DMA((na,))],
                     start, finish)


def _scatter_exchange(arrays):
    na = len(arrays)

    def copies(ins, outs, sems):
        send_sems, recv_sems = sems
        x, y, c = _position()
        out = []
        for a in range(na):
            for k in range(7):
                flip = k + 1
                peer = (x ^ (flip >> 2), y ^ ((flip >> 1) & 1), c ^ (flip & 1))
                peer_block = 4 * peer[0] + 2 * peer[1] + peer[2]
                out.append(pltpu.make_async_remote_copy(
                    src_ref=ins[a].at[peer_block], dst_ref=outs[a].at[k],
                    send_sem=send_sems.at[a, k], recv_sem=recv_sems.at[a, k], device_id=peer, device_id_type=MESH))
        return out

    def start(ins, outs, sems):
        for cp in copies(ins, outs, sems):
            cp.start()

    def finish(ins, outs, sems):
        for cp in copies(ins, outs, sems):
            cp.wait()

    return _Exchange(list(arrays), [jax.ShapeDtypeStruct((7,) + s.shape[1:], s.dtype) for s in arrays],
                     [pltpu.SemaphoreType.DMA((na, 7)), pltpu.SemaphoreType.DMA((na, 7))], start, finish)


def _exchange_alone(side, *, name):
    n_in = len(side.arrays)
    n_out = len(side.out_shape)

    def body(*refs):
        ins, outs, sems = refs[:n_in], refs[n_in:n_in + n_out], refs[n_in + n_out:]
        side.start(ins, outs, sems)
        side.finish(ins, outs, sems)

    hbm = pl.BlockSpec(memory_space=pl.ANY)
    return pl.pallas_call(body, out_shape=tuple(side.out_shape), in_specs=[hbm] * n_in, out_specs=tuple([hbm] * n_out),
                          scratch_shapes=list(side.scratch), name=name)(*side.arrays)


def _reduce_blocks(own, recv, *, name):
    r, c = own.shape
    tm = _tile(r, 256)

    def body(own_ref, recv_ref, o_ref):
        acc = own_ref[...]
        for k in range(7):
            acc = acc + recv_ref[k].astype(F32)
        o_ref[...] = acc

    return pl.pallas_call(
        body, out_shape=jax.ShapeDtypeStruct((r, c), F32), grid=(r // tm,),
        in_specs=[pl.BlockSpec((tm, c), lambda i: (i, 0)), pl.BlockSpec((7, tm, c), lambda i: (0, i, 0))],
        out_specs=pl.BlockSpec((tm, c), lambda i: (i, 0)), name=name, compiler_params=_params("parallel"))(own, recv)


def _all_reduce_small(vec, *, name):
    r, c = vec.shape

    def body(v_ref, o_ref, buf, send_sems, recv_sems):
        x, y, c_ = _position()
        me = 4 * x + 2 * y + c_
        buf[me] = v_ref[...]
        copies = []
        for k in range(7):
            flip = k + 1
            peer = (x ^ (flip >> 2), y ^ ((flip >> 1) & 1), c_ ^ (flip & 1))
            cp = pltpu.make_async_remote_copy(
                src_ref=v_ref, dst_ref=buf.at[me], send_sem=send_sems.at[k], recv_sem=recv_sems.at[k],
                device_id=peer, device_id_type=MESH)
            cp.start()
            copies.append(cp)
        for cp in copies:
            cp.wait()
        acc = buf[0]
        for d in range(1, N_DEV):
            acc = acc + buf[d]
        o_ref[...] = acc

    vm = pl.BlockSpec(memory_space=pltpu.VMEM)
    return pl.pallas_call(
        body, out_shape=jax.ShapeDtypeStruct((r, c), F32), in_specs=[vm], out_specs=vm,
        scratch_shapes=[pltpu.VMEM((N_DEV, r, c), F32), pltpu.SemaphoreType.DMA((7,)), pltpu.SemaphoreType.DMA((7,))],
        name=name)(vec)


def _pad_lanes(v, width=LANES):
    return jnp.pad(v, ((0, 0), (0, width - v.shape[1])))


def kernel(x, ffn1_pre_g, ffn1_w_gate, ffn1_w_up, ffn1_w_down, ffn1_post_g, mix_pre_g, w_in, conv_w, conv_b, dt_bias, a_log, d_skip, ssm_norm_g, w_ssm_proj, attn_sinks, rel_bias_table, w_attn_proj, w_out, mix_post_g, ffn2_pre_g, ffn2_w_gate, ffn2_w_up, ffn2_w_down, ffn2_post_g, loss_target, m_ffn1_pre_g, m_ffn1_w_gate, m_ffn1_w_up, m_ffn1_w_down, m_ffn1_post_g, m_mix_pre_g, m_w_in, m_conv_w, m_conv_b, m_dt_bias, m_a_log, m_d_skip, m_ssm_norm_g, m_w_ssm_proj, m_attn_sinks, m_rel_bias_table, m_w_attn_proj, m_w_out, m_mix_post_g, m_ffn2_pre_g, m_ffn2_w_gate, m_ffn2_w_up, m_ffn2_w_down, m_ffn2_post_g, v_ffn1_pre_g, v_ffn1_w_gate, v_ffn1_w_up, v_ffn1_w_down, v_ffn1_post_g, v_mix_pre_g, v_w_in, v_conv_w, v_conv_b, v_dt_bias, v_a_log, v_d_skip, v_ssm_norm_g, v_w_ssm_proj, v_attn_sinks, v_rel_bias_table, v_w_attn_proj, v_w_out, v_mix_post_g, v_ffn2_pre_g, v_ffn2_w_gate, v_ffn2_w_up, v_ffn2_w_down, v_ffn2_post_g):
    args = dict(locals())
    weight_names = ['ffn1_pre_g', 'ffn1_w_gate', 'ffn1_w_up', 'ffn1_w_down', 'ffn1_post_g', 'mix_pre_g', 'w_in', 'conv_w', 'conv_b',
                    'dt_bias', 'a_log', 'd_skip', 'ssm_norm_g', 'w_ssm_proj', 'attn_sinks', 'rel_bias_table', 'w_attn_proj', 'w_out',
                    'mix_post_g', 'ffn2_pre_g', 'ffn2_w_gate', 'ffn2_w_up', 'ffn2_w_down', 'ffn2_post_g']
    col_sharded = ('ffn1_w_gate', 'ffn1_w_up', 'w_in', 'ffn2_w_gate', 'ffn2_w_up')
    row_sharded = ('ffn1_w_down', 'w_ssm_proj', 'w_attn_proj', 'w_out', 'ffn2_w_down')
    big = col_sharded + row_sharded

    bl, s_len, d = x.shape
    t = bl * s_len
    d_inner = ssm_norm_g.shape[1]
    n_heads = dt_bias.shape[1]
    gn = SSM_GROUPS * SSM_STATE
    conv_dim = d_inner + 2 * gn
    q_dim = ATTN_Q_HEADS * ATTN_HEAD_DIM
    kv_dim = ATTN_KV_HEADS * ATTN_HEAD_DIM

    def local_2d(name, a):
        a = a[0]
        return a.T if name in col_sharded else a

    ffn1_names = ('ffn1_w_gate', 'ffn1_w_up', 'ffn1_w_down')
    ffn2_names = ('ffn2_w_gate', 'ffn2_w_up', 'ffn2_w_down')
    mixer_names = ('w_ssm_proj', 'w_attn_proj', 'w_out')

    def shard(n):
        return local_2d(n, args[n]).astype(BF16)

    def rows(g):
        return g.reshape(N_DEV * g.shape[1], g.shape[2])

    first_names = ffn1_names[:2]
    full = {n: rows(g) for n, g in zip(first_names, _exchange_alone(_gather_exchange([shard(n) for n in first_names]), name="gather_ffn1"))}

    x2 = x.reshape(t, d)
    tgt2 = loss_target.reshape(t, d)

    (h1,), saved1, got_in, _ = _ffn_forward(
        x2, ffn1_pre_g, full['ffn1_w_gate'], full['ffn1_w_up'], lambda got: rows(got[2]), ffn1_post_g, "ffn1",
        side_up=_gather_exchange([shard('w_in'), conv_w[0], shard('ffn1_w_down')]))
    full['ffn1_w_down'] = rows(got_in[2])
    conv_w_full = jnp.transpose(got_in[1], (1, 0, 2)).reshape(SSM_CONV, conv_dim)

    win_t = rows(got_in[0])
    dt_lo = 2 * d + d_inner + conv_dim
    off = {'gs': 0, 'ga': d, 'z': 2 * d, 'xbc': 2 * d + d_inner, 'dt': dt_lo, 'q': dt_lo + n_heads, 'kv': dt_lo + n_heads + q_dim}
    assert all(o_ % 16 == 0 for o_ in off.values())

    u, (gs, ga, z, xbc, q, kv, dt_raw) = _proj_all(
        h1, mix_pre_g, win_t,
        [(off['gs'], d, BF16), (off['ga'], d, BF16), (off['z'], d_inner, BF16), (off['xbc'], conv_dim, BF16),
         (off['q'], q_dim, BF16), (off['kv'], 2 * kv_dim, BF16), (off['dt'], n_heads, F32)], name="mix_proj")

    dtb_p, alog_p, dsk_p, sinks_p = _pad_lanes(dt_bias), _pad_lanes(a_log), _pad_lanes(d_skip), _pad_lanes(attn_sinks)
    xc = _conv_fwd(xbc, conv_w_full, conv_b, bl, name="conv_fwd")
    late_names = mixer_names + ffn2_names
    (y, hprev), got_late = _ssd_fwd(xc, dt_raw, dtb_p, alog_p, dsk_p, bl, n_heads,
                                    side=_gather_exchange([shard(n) for n in late_names]), name="ssd_fwd")
    full.update({n: rows(g) for n, g in zip(late_names, got_late)})
    yn = _gated_norm_fwd(y, z, ssm_norm_g, name="gated_norm_fwd")

    onehot = _bucket_onehot()
    rep = ATTN_Q_HEADS // ATTN_KV_HEADS
    bias = _small_mm_hi(rel_bias_table.T, onehot, NN, name="rel_bias")
    bias_t = jnp.transpose(bias.reshape(ATTN_KV_HEADS, rep, CHUNK, 2 * CHUNK), (0, 3, 1, 2)).reshape(ATTN_KV_HEADS, 2 * CHUNK, rep * CHUNK)
    o, lse = _attn_fwd(q, kv, bias_t, sinks_p, bl, name="attn_fwd")

    ys, ya, merged, mix, h2 = _merge_fwd(yn, o, gs, ga, full['w_ssm_proj'], full['w_attn_proj'], full['w_out'], h1, mix_post_g,
                                         name="merge_fwd")

    (dh3, loss_vec), saved2, _, _ = _ffn_forward(h2, ffn2_pre_g, full['ffn2_w_gate'], full['ffn2_w_up'], full['ffn2_w_down'],
                                                 ffn2_post_g, "ffn2", target=tgt2)
    loss = lax.psum(loss_vec[0, 0], ("x", "y", "c"))

    grads, own, wire, received = {}, {}, {}, {}
    dh2, grads['ffn2_pre_g'], grads['ffn2_post_g'], g32, g16, _ = _ffn_backward(
        dh3, saved2, ffn2_pre_g, full['ffn2_w_gate'], full['ffn2_w_up'], full['ffn2_w_down'], ffn2_post_g, "ffn2")
    own.update(zip(ffn2_names, map(_stack8, g32)))
    wire.update(zip(ffn2_names, map(_stack8, g16)))

    dmix, dys, dya, dgs, dga, dyn, do, grads['mix_post_g'] = _merge_bwd(
        dh2, mix, mix_post_g, gs, ga, ys, ya, full['w_ssm_proj'], full['w_attn_proj'], full['w_out'], name="merge_bwd")
    for n, (lhs, rhs) in zip(mixer_names, ((yn, dys), (o, dya), (merged, dmix))):
        g32_, g16_, _ = _mm_tn([lhs], rhs, name=f"d{n}")
        own[n], wire[n] = _stack8(g32_), _stack8(g16_)

    dq, dkv, dbias_t, dsinks = _attn_bwd(do, q, kv, lse, bias_t, sinks_p, bl, name="attn_bwd")
    dbias = jnp.transpose(dbias_t.reshape(ATTN_KV_HEADS, 2 * CHUNK, rep, CHUNK), (0, 2, 3, 1)).reshape(ATTN_Q_HEADS, -1)
    d_table = _small_mm_hi(onehot, dbias, NT, name="rel_bias_bwd")

    dy, dz, grads['ssm_norm_g'] = _gated_norm_bwd(dyn, y, z, ssm_norm_g, name="gated_norm_bwd")

    first_group = ffn2_names + mixer_names
    (dxc, ddt_raw, ddtb, dalog, ddsk), got = _ssd_bwd(dy, y, xc, dt_raw, hprev, dtb_p, alog_p, dsk_p, bl, n_heads,
                                                      side=_scatter_exchange([wire[n] for n in first_group]), name="ssd_bwd")
    received.update(zip(first_group, got))
    dxbc, dconv_w8, grads['conv_b'] = _conv_bwd(dxc, xbc, conv_w_full, conv_b, bl, name="conv_bwd")

    wide32, wide16, _ = _mm_tn([dgs, dga, dz, dxbc, dq], u, name="dw_in")
    kv32, kv16, _ = _mm_tn([dkv], u, name="dw_in_kv")
    dt32, dt16, _ = _mm_tn([ddt_raw], u, name="dw_in_dt")

    def original_order(wide, kv_part, dt_part):
        return jnp.concatenate([wide[:dt_lo], dt_part[:n_heads], wide[dt_lo:], kv_part], axis=0)
    me = 4 * lax.axis_index("x") + 2 * lax.axis_index("y") + lax.axis_index("c")
    blk_rows = win_t.shape[0] // N_DEV
    wire['w_in'] = _stack8(original_order(wide16, kv16, dt16))
    own_w_in = lax.dynamic_slice_in_dim(original_order(wide32, kv32, dt32), me * blk_rows, blk_rows)
    own['conv_w'] = jnp.transpose(dconv_w8[:SSM_CONV].reshape(SSM_CONV, N_DEV, conv_dim // N_DEV), (1, 0, 2))

    segs = [(g_, 0, off[k_]) for g_, k_ in zip([dgs, dga, dz, dxbc, dq, dkv, ddt_raw], ('gs', 'ga', 'z', 'xbc', 'q', 'kv', 'dt'))]
    dh1, grads['mix_pre_g'], got = _mm_nn_rmsbwd(segs, [win_t], h1, mix_pre_g, dh2,
                                                 side=_scatter_exchange([wire['w_in'], own['conv_w']]), name="mix_du")
    received.update(zip(('w_in', 'conv_w'), got))

    dx2, grads['ffn1_pre_g'], grads['ffn1_post_g'], g32, _, got = _ffn_backward(
        dh1, saved1, ffn1_pre_g, full['ffn1_w_gate'], full['ffn1_w_up'], full['ffn1_w_down'], ffn1_post_g, "ffn1", chain=True)
    own.update(zip(ffn1_names, map(_stack8, g32)))
    received.update(zip(ffn1_names, got))

    def own_block(a):
        return lax.dynamic_index_in_dim(a, me, 0, keepdims=False)
    reduced = {n: _reduce_blocks(own_w_in if n == 'w_in' else own_block(own[n]), received[n], name=f"reduce_{n}") for n in big}
    conv_sum = _reduce_blocks(own_block(own['conv_w']), received['conv_w'], name="reduce_conv_w")

    grads['dt_bias'], grads['a_log'], grads['d_skip'] = ddtb[:, :n_heads], dalog[:, :n_heads], ddsk[:, :n_heads]
    grads['attn_sinks'] = dsinks[:, :ATTN_Q_HEADS]
    grads['rel_bias_table'] = d_table
    small = [n for n in weight_names if n not in big and n != 'conv_w']
    flat = jnp.concatenate([grads[n].reshape(-1) for n in small])
    n_small = flat.shape[0]
    n_rows = -(-n_small // (8 * LANES)) * 8
    flat = jnp.pad(flat, (0, n_rows * LANES - n_small)).reshape(n_rows, LANES)
    summed = _all_reduce_small(flat, name="allreduce_small").reshape(-1)
    pos = 0
    for n in small:
        size = grads[n].size
        grads[n] = summed[pos:pos + size].reshape(args[n].shape)
        pos += size

    out_g, out_d, out_m, out_v = {}, {}, {}, {}
    for n in big:
        w2, m2, v2 = local_2d(n, args[n]), local_2d(n, args['m_' + n]), local_2d(n, args['v_' + n])
        dlt, mn, vn = _adamw(w2, reduced[n], m2, v2, name=f"adamw_{n}")

        def back(a, n=n):
            return (a.T if n in col_sharded else a)[None]
        out_g[n], out_d[n], out_m[n], out_v[n] = back(reduced[n]), back(dlt), back(mn), back(vn)

    def pack(prefix):
        vals = [(grads[n] if prefix == 'g' else args[prefix + n]).reshape(-1) for n in small]
        vals.append((conv_sum if prefix == 'g' else args[prefix + 'conv_w']).reshape(-1))
        flat_ = jnp.concatenate(vals)
        rows_ = -(-flat_.shape[0] // (8 * LANES)) * 8
        return jnp.pad(flat_, (0, rows_ * LANES - flat_.shape[0])).reshape(rows_, LANES)

    g_small = pack('g')
    d_small, m_small, v_small = _adamw(pack(''), g_small, pack('m_'), pack('v_'), name="adamw_small")
    pos = 0
    for n in small + ['conv_w']:
        shape = args[n].shape
        size = int(np.prod(shape))
        for dst, src in ((out_g, g_small), (out_d, d_small), (out_m, m_small), (out_v, v_small)):
            dst[n] = src.reshape(-1)[pos:pos + size].reshape(shape)
        pos += size

    grad_x = dx2.reshape(bl, s_len, d)
    return (loss, grad_x, *[out_g[n] for n in weight_names], *[out_d[n] for n in weight_names],
            *[out_m[n] for n in weight_names], *[out_v[n] for n in weight_names])
```

```python
import functools
import math

import numpy as np
import jax
import jax.numpy as jnp
from jax import lax
from jax.experimental import pallas as pl
from jax.experimental.pallas import tpu as pltpu

F32 = jnp.float32
BF16 = jnp.bfloat16
MESH = pl.DeviceIdType.MESH
N_DEV = 8

SSM_HEAD_DIM = 64
SSM_GROUPS = 4
SSM_STATE = 128
SSM_CONV = 4
CHUNK = 128
ATTN_HEAD_DIM = 64
ATTN_Q_HEADS = 16
ATTN_KV_HEADS = 4
REL_BUCKETS = 32
REL_MAX_DISTANCE = 128
RMS_EPS = 1e-6
FFN_RESIDUAL_WEIGHT = 0.5
ADAM_LR, ADAM_B1, ADAM_B2, ADAM_EPS, ADAM_WD, ADAM_STEP = 0.001, 0.9, 0.999, 1e-08, 0.01, 10

LANES = 128
VMEM_LIMIT_BYTES = 56 * 1024 * 1024
FFN_COL_TILE = 1408
SSD_HEAD_BATCH = 2

NEG_INF = float("-inf")


def _params(*sem):
    return pltpu.CompilerParams(dimension_semantics=sem, vmem_limit_bytes=VMEM_LIMIT_BYTES)


def _tile(n, pref, mult=8):
    if n <= pref:
        return n
    t = (pref // mult) * mult
    while t >= mult:
        if n % t == 0:
            return t
        t -= mult
    return n


def _sigmoid(x):
    return 1.0 / (1.0 + jnp.exp(-x))


def _dot(a, b, dims):
    return lax.dot_general(a, b, (dims, ((), ())), preferred_element_type=F32)


NN = ((1,), (0,))
NT = ((1,), (1,))
TN = ((0,), (0,))


def _dot_hi(a, b, dims=NN):
    return lax.dot_general(a, b, (dims, ((), ())), preferred_element_type=F32, precision=lax.Precision.HIGHEST)


def _const_spec(shape):
    nd = len(shape)
    return pl.BlockSpec(shape, lambda *_: (0,) * nd)


def _resident_spec(shape):
    nd = len(shape)
    return pl.BlockSpec(shape, lambda *_: (0,) * nd, pipeline_mode=pl.Buffered(1))


class _Exchange:
    def __init__(self, arrays, out_shape, scratch, start, finish):
        self.arrays, self.out_shape, self.scratch, self.start, self.finish = arrays, out_shape, scratch, start, finish


def _hosted_call(body, *, grid, in_specs, out_specs, out_shape, scratch_shapes, operands, side, name):
    in_specs, out_specs, out_shape, scratch_shapes = list(in_specs), list(out_specs), list(out_shape), list(scratch_shapes)
    sem = ("arbitrary",) * len(grid)
    if side is None:
        outs = pl.pallas_call(body, out_shape=tuple(out_shape), grid=grid, in_specs=in_specs, out_specs=tuple(out_specs),
                              scratch_shapes=scratch_shapes, name=name, compiler_params=_params(*sem))(*operands)
        return tuple(outs), ()
    n_in, n_out, n_scr = len(in_specs), len(out_shape), len(scratch_shapes)
    s_in, s_out = len(side.arrays), len(side.out_shape)

    def wrapped(*refs):
        refs = list(refs)
        main_in, side_in = refs[:n_in], refs[n_in:n_in + s_in]
        o0 = n_in + s_in
        main_out, side_out = refs[o0:o0 + n_out], refs[o0 + n_out:o0 + n_out + s_out]
        c0 = o0 + n_out + s_out
        main_scr, side_scr = refs[c0:c0 + n_scr], refs[c0 + n_scr:]
        ids = [pl.program_id(ax) for ax in range(len(grid))]
        first = functools.reduce(jnp.logical_and, [i == 0 for i in ids])
        last = functools.reduce(jnp.logical_and, [i == g - 1 for i, g in zip(ids, grid)])

        @pl.when(first)
        def _():
            side.start(side_in, side_out, side_scr)

        body(*main_in, *main_out, *main_scr)

        @pl.when(last)
        def _():
            side.finish(side_in, side_out, side_scr)

    hbm = pl.BlockSpec(memory_space=pl.ANY)
    outs = pl.pallas_call(
        wrapped, out_shape=tuple(out_shape + list(side.out_shape)), grid=grid,
        in_specs=in_specs + [hbm] * s_in, out_specs=tuple(out_specs + [hbm] * s_out),
        scratch_shapes=scratch_shapes + list(side.scratch), name=name, compiler_params=_params(*sem))(*operands, *side.arrays)
    return tuple(outs[:n_out]), tuple(outs[n_out:])


def _proj_all(h, g, w, segs, *, name):
    t, d = h.shape
    tm = _tile(t, 512)
    segs = [(row0, wd_, max(wd_, LANES), dt_) for row0, wd_, dt_ in segs]
    assert all(row0 + out_w <= w.shape[0] for row0, _, out_w, _ in segs)

    def body(h_ref, g_ref, w_ref, u_ref, *o_refs):
        hv = h_ref[...]
        r = lax.rsqrt(jnp.mean(hv * hv, axis=-1, keepdims=True) + RMS_EPS)
        uv = (hv * r * g_ref[...]).astype(BF16)
        u_ref[...] = uv
        for (row0, width, out_w, _), o_ref in zip(segs, o_refs):
            for c0, c1 in _col_chunks(out_w, 8 * LANES):
                part = _dot(uv, w_ref[row0 + c0:row0 + c1, :], NT)
                if width < out_w:
                    part = jnp.where(lax.broadcasted_iota(jnp.int32, part.shape, 1) < width, part, 0.0)
                o_ref[:, c0:c1] = part.astype(o_ref.dtype)

    row = pl.BlockSpec((tm, d), lambda i: (i, 0))
    outs = pl.pallas_call(
        body, out_shape=(jax.ShapeDtypeStruct((t, d), BF16),) + tuple(jax.ShapeDtypeStruct((t, ow), dt_) for _, _, ow, dt_ in segs),
        grid=(t // tm,), in_specs=[row, _const_spec((1, d)), _resident_spec(w.shape)],
        out_specs=(row,) + tuple(pl.BlockSpec((tm, ow), lambda i: (i, 0)) for _, _, ow, _ in segs),
        name=name, compiler_params=_params("parallel"))(h, g, w)
    return outs[0], outs[1:]


def _mm_tn(a_list, b, *, tm=1408, tk=2048, side=None, name):
    t, n = b.shape
    tk = _tile(t, tk if len(a_list) == 1 else tk // 2)
    nk = t // tk
    widths = [a.shape[1] for a in a_list]
    tm = _tile(math.gcd(*widths), tm, LANES)
    assert all(w % tm == 0 for w in widths)
    starts = np.cumsum([0] + [w // tm for w in widths])
    nseg = len(a_list)

    def a_spec(s):
        lo, hi = int(starts[s]), int(starts[s + 1])

        def idx(i, k):
            active = jnp.logical_and(i >= lo, i < hi)
            return (jnp.where(active, k, 0), jnp.clip(i - lo, 0, hi - lo - 1))
        return pl.BlockSpec((tk, tm), idx)

    def body(*refs):
        a_refs, b_ref, o_ref, o16_ref, acc = refs[:nseg], refs[nseg], refs[nseg + 1], refs[nseg + 2], refs[nseg + 3]
        i, k = pl.program_id(0), pl.program_id(1)

        @pl.when(k == 0)
        def _():
            acc[...] = jnp.zeros_like(acc)

        bv = b_ref[...].astype(BF16)
        for s in range(nseg):
            lo, hi = int(starts[s]), int(starts[s + 1])

            @pl.when(jnp.logical_and(i >= lo, i < hi))
            def _(s=s):
                acc[...] += _dot(a_refs[s][...].astype(BF16), bv, TN)

        @pl.when(k == nk - 1)
        def _():
            o_ref[...] = acc[...]
            o16_ref[...] = acc[...].astype(BF16)

    rows = int(starts[-1]) * tm
    o_spec = pl.BlockSpec((tm, n), lambda i, k: (i, 0))
    (o32, o16), got = _hosted_call(
        body, out_shape=[jax.ShapeDtypeStruct((rows, n), F32), jax.ShapeDtypeStruct((rows, n), BF16)], grid=(int(starts[-1]), nk),
        in_specs=[a_spec(s) for s in range(nseg)] + [pl.BlockSpec((tk, n), lambda i, k: (k, 0))],
        out_specs=[o_spec, o_spec], scratch_shapes=[pltpu.VMEM((tm, n), F32)], operands=list(a_list) + [b], side=side, name=name)
    return o32, o16, got


def _mm_nn_rmsbwd(segs, weights, x, g, dres, *, tm=512, side=None, name):
    t, d = x.shape
    tm = _tile(t, tm)
    nseg, nw = len(segs), len(weights)

    def body(*refs):
        a_refs, w_refs = refs[:nseg], refs[nseg:nseg + nw]
        x_ref, g_ref, dres_ref, dx_ref, dg_ref = refs[nseg + nw:]

        @pl.when(pl.program_id(0) == 0)
        def _():
            dg_ref[...] = jnp.zeros_like(dg_ref)

        dn = None
        for s, (a, w_idx, row0) in enumerate(segs):
            part = _dot(a_refs[s][...].astype(BF16), w_refs[w_idx][row0:row0 + a.shape[1], :], NN)
            dn = part if dn is None else dn + part
        xv = x_ref[...]
        r = lax.rsqrt(jnp.mean(xv * xv, axis=-1, keepdims=True) + RMS_EPS)
        xhat = xv * r
        dyg = dn * g_ref[...]
        dx_ref[...] = dres_ref[...] + r * (dyg - xhat * jnp.mean(dyg * xhat, axis=-1, keepdims=True))
        dg_ref[...] += jnp.sum(dn * xhat, axis=0, keepdims=True)

    row = pl.BlockSpec((tm, d), lambda i: (i, 0))
    in_specs = [pl.BlockSpec((tm, a.shape[1]), lambda i: (i, 0)) for a, _, _ in segs]
    in_specs += [_resident_spec(w.shape) for w in weights] + [row, _const_spec((1, d)), row]
    (dx, dg), extra = _hosted_call(
        body, grid=(t // tm,), in_specs=in_specs, out_specs=[row, _const_spec((1, d))],
        out_shape=[jax.ShapeDtypeStruct((t, d), F32), jax.ShapeDtypeStruct((1, d), F32)], scratch_shapes=[],
        operands=[a for a, _, _ in segs] + list(weights) + [x, g, dres], side=side, name=name)
    return dx, dg, extra


def _col_chunks(width, chunk=4 * LANES):
    return [(c0, min(c0 + chunk, width)) for c0 in range(0, width, chunk)]


def _rms_fwd(x, g, *, side=None, name):
    t, d = x.shape
    tm = _tile(t, 512)

    def body(x_ref, g_ref, o_ref):
        xv = x_ref[...]
        r = lax.rsqrt(jnp.mean(xv * xv, axis=-1, keepdims=True) + RMS_EPS)
        o_ref[...] = (xv * r * g_ref[...]).astype(o_ref.dtype)

    row = pl.BlockSpec((tm, d), lambda i: (i, 0))
    (n,), got = _hosted_call(body, grid=(t // tm,), in_specs=[row, _const_spec((1, d))], out_specs=[row],
                             out_shape=[jax.ShapeDtypeStruct((t, d), BF16)], scratch_shapes=[], operands=[x, g], side=side, name=name)
    return n, got


def _ffn_up(n, wgt, wut, *, side=None, name):
    t, d = n.shape
    f = wgt.shape[0]
    tm, tn = _tile(t, 512), _tile(f, FFN_COL_TILE, LANES)

    def body(n_ref, wg_ref, wu_ref, g_ref, u_ref, h_ref):
        nv = n_ref[...]
        gv = _dot(nv, wg_ref[...], NT)
        uv = _dot(nv, wu_ref[...], NT)
        g_ref[...] = gv.astype(BF16)
        u_ref[...] = uv.astype(BF16)
        h_ref[...] = (gv * _sigmoid(gv) * uv).astype(BF16)

    w_spec = pl.BlockSpec((tn, d), lambda j, i: (j, 0))
    o_spec = pl.BlockSpec((tm, tn), lambda j, i: (i, j))
    shp = jax.ShapeDtypeStruct((t, f), BF16)
    return _hosted_call(body, grid=(f // tn, t // tm), in_specs=[pl.BlockSpec((tm, d), lambda j, i: (i, 0)), w_spec, w_spec],
                        out_specs=[o_spec, o_spec, o_spec], out_shape=[shp, shp, shp], scratch_shapes=[], operands=[n, wgt, wut],
                        side=side, name=name)


def _rms_residual(acc, h, gp, weight):
    r = lax.rsqrt(jnp.mean(acc * acc, axis=-1, keepdims=True) + RMS_EPS)
    return h + weight * (acc * r * gp)


def _ffn_down(hid, wd, h_in, gp, *, target=None, side=None, name):
    t, f = hid.shape
    d = wd.shape[1]
    tm = _tile(t, 256)
    row = pl.BlockSpec((tm, d), lambda i: (i, 0))
    shp = jax.ShapeDtypeStruct((t, d), F32)
    in_specs = [pl.BlockSpec((tm, f), lambda i: (i, 0)), _resident_spec((f, d)), row, _const_spec((1, d))]

    if target is None:
        def body(hid_ref, wd_ref, hin_ref, gp_ref, f_ref, hout_ref):
            acc = _dot(hid_ref[...], wd_ref[...], NN)
            f_ref[...] = acc
            hout_ref[...] = _rms_residual(acc, hin_ref[...], gp_ref[...], FFN_RESIDUAL_WEIGHT)

        return _hosted_call(body, grid=(t // tm,), in_specs=in_specs, out_specs=[row, row], out_shape=[shp, shp], scratch_shapes=[],
                            operands=[hid, wd, h_in, gp], side=side, name=name)

    def body_loss(hid_ref, wd_ref, hin_ref, gp_ref, tgt_ref, f_ref, dh_ref, loss_ref):
        @pl.when(pl.program_id(0) == 0)
        def _():
            loss_ref[...] = jnp.zeros_like(loss_ref)
        acc = _dot(hid_ref[...], wd_ref[...], NN)
        f_ref[...] = acc
        e = _rms_residual(acc, hin_ref[...], gp_ref[...], FFN_RESIDUAL_WEIGHT) - tgt_ref[...]
        dh_ref[...] = e * (1.0 / d)
        per_row = jnp.sum(e * e, axis=1, keepdims=True) * (1.0 / d)
        loss_ref[...] += 0.5 * jnp.sum(per_row, axis=0, keepdims=True)

    return _hosted_call(body_loss, grid=(t // tm,), in_specs=in_specs + [row], out_specs=[row, row, _const_spec((1, LANES))],
                        out_shape=[shp, shp, jax.ShapeDtypeStruct((1, LANES), F32)], scratch_shapes=[],
                        operands=[hid, wd, h_in, gp, target], side=side, name=name)


def _post_bwd(dh, f, gp, weight, *, name):
    t, d = f.shape
    tm = _tile(t, 512)

    def body(dh_ref, f_ref, gp_ref, df_ref, dgp_ref):
        @pl.when(pl.program_id(0) == 0)
        def _():
            dgp_ref[...] = jnp.zeros_like(dgp_ref)
        fv = f_ref[...]
        dy = weight * dh_ref[...]
        r = lax.rsqrt(jnp.mean(fv * fv, axis=-1, keepdims=True) + RMS_EPS)
        fhat = fv * r
        dyg = dy * gp_ref[...]
        df_ref[...] = (r * (dyg - fhat * jnp.mean(dyg * fhat, axis=-1, keepdims=True))).astype(BF16)
        dgp_ref[...] += jnp.sum(dy * fhat, axis=0, keepdims=True)

    row = pl.BlockSpec((tm, d), lambda i: (i, 0))
    return pl.pallas_call(body, out_shape=(jax.ShapeDtypeStruct((t, d), BF16), jax.ShapeDtypeStruct((1, d), F32)), grid=(t // tm,),
                          in_specs=[row, row, _const_spec((1, d))], out_specs=(row, _const_spec((1, d))),
                          name=name, compiler_params=_params("arbitrary"))(dh, f, gp)


def _ffn_dhid(df, wd, g, u, *, name):
    t, d = df.shape
    f = wd.shape[0]
    tm, tn = _tile(t, 512), _tile(f, FFN_COL_TILE, LANES)

    def body(df_ref, wd_ref, g_ref, u_ref, dg_ref, du_ref):
        dh = _dot(df_ref[...], wd_ref[...], NT)
        gv = g_ref[...].astype(F32)
        uv = u_ref[...].astype(F32)
        sg = _sigmoid(gv)
        silu = gv * sg
        dg_ref[...] = (dh * uv * (sg + silu * (1.0 - sg))).astype(BF16)
        du_ref[...] = (dh * silu).astype(BF16)

    o_spec = pl.BlockSpec((tm, tn), lambda j, i: (i, j))
    shp = jax.ShapeDtypeStruct((t, f), BF16)
    return pl.pallas_call(body, out_shape=(shp, shp), grid=(f // tn, t // tm),
                          in_specs=[pl.BlockSpec((tm, d), lambda j, i: (i, 0)), pl.BlockSpec((tn, d), lambda j, i: (j, 0)), o_spec, o_spec],
                          out_specs=(o_spec, o_spec), name=name, compiler_params=_params("parallel", "arbitrary"))(df, wd, g, u)


def _ffn_forward(h_in, g_pre, w_up, wd, g_post, tag, side_norm=None, side_up=None, target=None):
    n, got_norm = _rms_fwd(h_in, g_pre, side=side_norm, name=f"{tag}_prenorm")
    wgt, wut = w_up(got_norm) if callable(w_up) else w_up
    (g, u, hid), got_up = _ffn_up(n, wgt, wut, side=side_up, name=f"{tag}_up")
    wd = wd(got_up) if callable(wd) else wd
    outs, _ = _ffn_down(hid, wd, h_in, g_post, target=target, name=f"{tag}_down")
    return outs[1:], (h_in, n, g, u, hid, outs[0]), (wgt, wut, wd), got_up


def _stack8(g):
    return g.reshape(N_DEV, g.shape[0] // N_DEV, g.shape[1])


def _ffn_backward(dh_out, saved, g_pre, wgt, wut, wd, g_post, tag, chain=False):
    h_in, n, g, u, hid, f = saved

    def side_of(grad16):
        return _scatter_exchange([_stack8(grad16)]) if chain else None

    df, dg_post = _post_bwd(dh_out, f, g_post, FFN_RESIDUAL_WEIGHT, name=f"{tag}_post_bwd")
    dgate, dup = _ffn_dhid(df, wd, g, u, name=f"{tag}_dhid")
    d_wd, d_wd16, _ = _mm_tn([hid], df, name=f"{tag}_dwd")
    d_wgt, d_wgt16, got_wd = _mm_tn([dgate], n, side=side_of(d_wd16), name=f"{tag}_dwg")
    d_wut, d_wut16, got_wg = _mm_tn([dup], n, side=side_of(d_wgt16), name=f"{tag}_dwu")
    dh_in, dg_pre, got_wu = _mm_nn_rmsbwd([(dgate, 0, 0), (dup, 1, 0)], [wgt, wut], h_in, g_pre, dh_out, side=side_of(d_wut16),
                                          name=f"{tag}_dn")
    received = (got_wg[0], got_wu[0], got_wd[0]) if chain else None
    return dh_in, dg_pre, dg_post, (d_wgt, d_wut, d_wd), (d_wgt16, d_wut16, d_wd16), received


CONV_ROWS = 128
HALO = 8


def _taps(w_ref):
    return [w_ref[k:k + 1, :] for k in range(SSM_CONV)]


def _conv_chunk(x_ref, xs, r0, taps, bias):
    xs[HALO + r0:HALO + r0 + CONV_ROWS, :] = x_ref[r0:r0 + CONV_ROWS, :].astype(F32)
    shifted = [xs[HALO + r0 - k:HALO + r0 - k + CONV_ROWS, :] for k in range(SSM_CONV)]
    pre = bias + shifted[0] * taps[SSM_CONV - 1]
    for k in range(1, SSM_CONV):
        pre = pre + shifted[k] * taps[SSM_CONV - 1 - k]
    return shifted, pre


def _fold_rows(a):
    return functools.reduce(jnp.add, [a[i:i + 8] for i in range(0, a.shape[0], 8)])


def _conv_fwd(xbc, conv_w, conv_b, bl, *, name):
    t, c = xbc.shape
    s = t // bl
    tc = LANES
    assert s % CONV_ROWS == 0

    def body(x_ref, w_ref, b_ref, o_ref, xs):
        taps, bias = _taps(w_ref), b_ref[...]
        xs[0:HALO, :] = jnp.zeros((HALO, tc), F32)
        for r0 in range(0, s, CONV_ROWS):
            _, pre = _conv_chunk(x_ref, xs, r0, taps, bias)
            o_ref[r0:r0 + CONV_ROWS, :] = (pre * _sigmoid(pre)).astype(o_ref.dtype)

    blk = pl.BlockSpec((s, tc), lambda b, j: (b, j))
    return pl.pallas_call(body, out_shape=jax.ShapeDtypeStruct((t, c), BF16), grid=(bl, c // tc),
                          in_specs=[blk, pl.BlockSpec((SSM_CONV, tc), lambda b, j: (0, j)), pl.BlockSpec((1, tc), lambda b, j: (0, j))],
                          out_specs=blk, scratch_shapes=[pltpu.VMEM((HALO + s, tc), F32)],
                          name=name, compiler_params=_params("parallel", "arbitrary"))(xbc, conv_w, conv_b)


def _conv_bwd(dxc, xbc, conv_w, conv_b, bl, *, name):
    t, c = xbc.shape
    s = t // bl
    tc = LANES

    def body(dy_ref, x_ref, w_ref, b_ref, dx_ref, dw_ref, db_ref, xs, dpre_s):
        @pl.when(pl.program_id(1) == 0)
        def _():
            dw_ref[...] = jnp.zeros_like(dw_ref)
            db_ref[...] = jnp.zeros_like(db_ref)

        taps, bias = _taps(w_ref), b_ref[...]
        zero8 = jnp.zeros((HALO, tc), F32)
        xs[0:HALO, :] = zero8
        dpre_s[s:s + HALO, :] = zero8
        sums = [zero8] * (SSM_CONV + 1)
        for r0 in range(0, s, CONV_ROWS):
            shifted, pre = _conv_chunk(x_ref, xs, r0, taps, bias)
            sg = _sigmoid(pre)
            dpre = dy_ref[r0:r0 + CONV_ROWS, :].astype(F32) * (sg * (1.0 + pre * (1.0 - sg)))
            dpre_s[r0:r0 + CONV_ROWS, :] = dpre
            sums = [acc + _fold_rows(dpre * sh) for acc, sh in zip(sums[:-1], shifted)] + [sums[-1] + _fold_rows(dpre)]
        for k in range(SSM_CONV):
            dw_ref[SSM_CONV - 1 - k:SSM_CONV - k, :] += jnp.sum(sums[k], axis=0, keepdims=True)
        db_ref[...] += jnp.sum(sums[-1], axis=0, keepdims=True)
        for r0 in range(0, s, CONV_ROWS):
            dx = dpre_s[r0:r0 + CONV_ROWS, :] * taps[SSM_CONV - 1]
            for k in range(1, SSM_CONV):
                dx = dx + dpre_s[r0 + k:r0 + k + CONV_ROWS, :] * taps[SSM_CONV - 1 - k]
            dx_ref[r0:r0 + CONV_ROWS, :] = dx.astype(dx_ref.dtype)

    blk = pl.BlockSpec((s, tc), lambda j, b: (b, j))
    return pl.pallas_call(
        body, out_shape=(jax.ShapeDtypeStruct((t, c), BF16), jax.ShapeDtypeStruct((8, c), F32), jax.ShapeDtypeStruct((1, c), F32)),
        grid=(c // tc, bl),
        in_specs=[blk, blk, pl.BlockSpec((SSM_CONV, tc), lambda j, b: (0, j)), pl.BlockSpec((1, tc), lambda j, b: (0, j))],
        out_specs=(blk, pl.BlockSpec((8, tc), lambda j, b: (0, j)), pl.BlockSpec((1, tc), lambda j, b: (0, j))),
        scratch_shapes=[pltpu.VMEM((HALO + s, tc), F32), pltpu.VMEM((s + HALO, tc), F32)],
        name=name, compiler_params=_params("parallel", "arbitrary"))(dxc, xbc, conv_w, conv_b)


def _softplus(x):
    return jnp.maximum(x, 0.0) + jnp.log1p(jnp.exp(-jnp.abs(x)))


def _hilo_dot(v, m_b, dims=NN):
    hi = v.astype(BF16)
    lo = (v - hi.astype(F32)).astype(BF16)
    return _dot(hi, m_b, dims) + _dot(lo, m_b, dims)


def _ssd_chunk_common(dtraw_ref, dtb_ref, alog_ref, dsk_ref, d_inner):
    q, p = CHUNK, SSM_HEAD_DIM
    shift = p.bit_length() - 1
    assert 1 << shift == p
    dt = _softplus(dtraw_ref[...] + dtb_ref[...])
    a = -jnp.exp(alog_ref[...])
    ii = lax.broadcasted_iota(jnp.int32, (q, q), 0)
    jj = lax.broadcasted_iota(jnp.int32, (q, q), 1)
    causal = ii >= jj
    tril = jnp.where(causal, 1.0, 0.0).astype(F32)
    triu = jnp.where(ii <= jj, 1.0, 0.0).astype(F32)
    a_cs = _dot_hi(tril, dt * a)
    a_cs_t = a_cs.T
    a_last = a_cs[q - 1:q, :]
    e_col = jnp.exp(a_cs)
    dec_end = jnp.exp(a_last - a_cs)
    head_of_col = lax.shift_right_logical(lax.broadcasted_iota(jnp.int32, (LANES, d_inner), 1), shift)
    spread = (lax.broadcasted_iota(jnp.int32, (LANES, d_inner), 0) == head_of_col).astype(BF16)
    wide = _hilo_dot(jnp.concatenate([dt, e_col, dec_end, jnp.broadcast_to(dsk_ref[...], (8, LANES))], axis=0), spread)
    return dict(dt=dt, a=a, a_cs=a_cs, a_cs_t=a_cs_t, a_last=a_last, dec_end=dec_end, causal=causal, triu=triu,
                dt_e=wide[:q], e_e=wide[q:2 * q], dec_e=wide[2 * q:3 * q], dsk_e=wide[3 * q:3 * q + 1])


def _fill_block_diag(bd_ref, src_ref, hpg, col0=0):
    q, p = CHUNK, SSM_HEAD_DIM
    for hh in range(hpg):
        bd_ref[hh * q:(hh + 1) * q, hh * p:(hh + 1) * p] = src_ref[:, col0 + hh * p:col0 + (hh + 1) * p]


def _lane_onehot(h):
    return (lax.broadcasted_iota(jnp.int32, (1, LANES), 1) == h).astype(F32)


def _ssd_fwd(xc, dt_raw, dt_bias, a_log, d_skip, bl, n_heads, *, side=None, name):
    t = xc.shape[0]
    q, p, nst, grp = CHUNK, SSM_HEAD_DIM, SSM_STATE, SSM_GROUPS
    d_inner = n_heads * p
    hpg = n_heads // grp
    hb = min(hpg, SSD_HEAD_BATCH)
    gw = hpg * p
    nc = t // bl // q
    assert d_inner % (grp * nst) == 0 and nst == LANES and hpg % hb == 0

    def body(xs_ref, b_ref, c_ref, dtraw_ref, dtb_ref, alog_ref, dsk_ref, y_ref, hprev_ref, state, m_all, x_bd, xdt_s):
        @pl.when(jnp.logical_and(pl.program_id(0) == 0, pl.program_id(1) == 0))
        def _():
            x_bd[...] = jnp.zeros_like(x_bd)

        @pl.when(pl.program_id(1) == 0)
        def _():
            state[...] = jnp.zeros_like(state)

        cm = _ssd_chunk_common(dtraw_ref, dtb_ref, alog_ref, dsk_ref, d_inner)
        for g in range(grp):
            cols = slice(g * gw, (g + 1) * gw)
            bg = b_ref[:, g * nst:(g + 1) * nst]
            cg = c_ref[:, g * nst:(g + 1) * nst]
            scores = _dot(cg, bg, NT)
            xs = xs_ref[:, cols].astype(F32)
            xdt = xs * cm['dt_e'][:, cols]
            xdt_s[...] = xdt.astype(BF16)
            y_parts = []
            for sub in range(hpg // hb):
                for k in range(hb):
                    h = g * hpg + sub * hb + k
                    seg = cm['a_cs'][:, h:h + 1] - cm['a_cs_t'][h:h + 1, :]
                    m_all[:, k * q:(k + 1) * q] = (scores * jnp.exp(jnp.where(cm['causal'], seg, NEG_INF))).astype(BF16)
                _fill_block_diag(x_bd, xdt_s, hb, sub * hb * p)
                y_parts.append(_dot(m_all[...], x_bd[...], NN))
            hprev = state[g]
            hprev_ref[g] = hprev
            y = jnp.concatenate(y_parts, axis=1) + cm['e_e'][:, cols] * _dot(cg, hprev.astype(BF16), NT)
            y_ref[:, cols] = y + cm['dsk_e'][:, cols] * xs
            st = _dot((xdt * cm['dec_e'][:, cols]).astype(BF16), bg, TN)
            for hh in range(hpg):
                h = g * hpg + hh
                rows = slice(hh * p, (hh + 1) * p)
                state[g, rows, :] = jnp.exp(cm['a_last'][:, h:h + 1]) * hprev[rows] + st[rows]

    gn = grp * nst

    def rowmap(b, c):
        return b * nc + c
    vec = pl.BlockSpec((1, LANES), lambda b, c: (0, 0))
    return _hosted_call(
        body,
        out_shape=[jax.ShapeDtypeStruct((t, d_inner), F32), jax.ShapeDtypeStruct((t // q, grp, gw, nst), F32)],
        grid=(bl, nc),
        in_specs=[pl.BlockSpec((q, d_inner), lambda b, c: (rowmap(b, c), 0)),
                  pl.BlockSpec((q, gn), lambda b, c: (rowmap(b, c), d_inner // gn)),
                  pl.BlockSpec((q, gn), lambda b, c: (rowmap(b, c), d_inner // gn + 1)),
                  pl.BlockSpec((q, LANES), lambda b, c: (rowmap(b, c), 0)), vec, vec, vec],
        out_specs=[pl.BlockSpec((q, d_inner), lambda b, c: (rowmap(b, c), 0)),
                   pl.BlockSpec((None, grp, gw, nst), lambda b, c: (rowmap(b, c), 0, 0, 0))],
        scratch_shapes=[pltpu.VMEM((grp, gw, nst), F32), pltpu.VMEM((q, hb * q), BF16), pltpu.VMEM((hb * q, hb * p), BF16),
                        pltpu.VMEM((q, gw), BF16)],
        operands=[xc, xc, xc, dt_raw, dt_bias, a_log, d_skip], side=side, name=name)


def _ssd_bwd(dy, y, xc, dt_raw, hprev_all, dt_bias, a_log, d_skip, bl, n_heads, *, side=None, name):
    t, c_dim = xc.shape
    q, p, nst, grp = CHUNK, SSM_HEAD_DIM, SSM_STATE, SSM_GROUPS
    d_inner = n_heads * p
    hpg = n_heads // grp
    hb = min(hpg, SSD_HEAD_BATCH)
    gw = hpg * p
    nc = t // bl // q
    gn = grp * nst
    shift = p.bit_length() - 1

    def body(dy_ref, y_ref, xs_ref, b_ref, c_ref, dtraw_ref, hprev_ref, dtb_ref, alog_ref, dsk_ref,
             dxc_ref, ddtraw_ref, ddtb_ref, dalog_ref, ddsk_ref, dstate, mt_all, x_bd, dy_bd, xdt_s):
        @pl.when(jnp.logical_and(pl.program_id(0) == 0, pl.program_id(1) == 0))
        def _():
            ddtb_ref[...] = jnp.zeros_like(ddtb_ref)
            dalog_ref[...] = jnp.zeros_like(dalog_ref)
            ddsk_ref[...] = jnp.zeros_like(ddsk_ref)
            x_bd[...] = jnp.zeros_like(x_bd)
            dy_bd[...] = jnp.zeros_like(dy_bd)

        @pl.when(pl.program_id(1) == 0)
        def _():
            dstate[...] = jnp.zeros_like(dstate)

        cm = _ssd_chunk_common(dtraw_ref, dtb_ref, alog_ref, dsk_ref, d_inner)
        causal = cm['causal']
        upper = cm['triu'] > 0.5
        seg_row = lax.shift_right_logical(lax.broadcasted_iota(jnp.int32, (gw, LANES), 0), shift)
        seg_lane = lax.broadcasted_iota(jnp.int32, (gw, LANES), 1)
        sums = jnp.zeros((5 * q, LANES), F32)
        state_dot = jnp.zeros((1, LANES), F32)
        for g in range(grp):
            cols = slice(g * gw, (g + 1) * gw)
            seg_sum = (seg_row + g * hpg == seg_lane).astype(BF16)
            bg = b_ref[:, g * nst:(g + 1) * nst]
            cg = c_ref[:, g * nst:(g + 1) * nst]
            scores_t = _dot(bg, cg, NT)
            xs = xs_ref[:, cols].astype(F32)
            xdt = xs * cm['dt_e'][:, cols]
            xdt_s[...] = xdt.astype(BF16)
            dyf = dy_ref[:, cols].astype(F32)
            dscores = jnp.zeros((q, q), F32)
            dx_parts = []
            for sub in range(hpg // hb):
                c0 = sub * hb * p
                _fill_block_diag(x_bd, xdt_s, hb, c0)
                _fill_block_diag(dy_bd, dy_ref, hb, g * gw + c0)
                dm_all = _dot(dy_ref[:, g * gw + c0:g * gw + c0 + hb * p], x_bd[...], NT)
                for k in range(hb):
                    h = g * hpg + sub * hb + k
                    blk = slice(k * q, (k + 1) * q)
                    seg = cm['a_cs'][:, h:h + 1] - cm['a_cs_t'][h:h + 1, :]
                    mt_all[:, blk] = (scores_t * jnp.exp(jnp.where(upper, -seg, NEG_INF))).astype(BF16)
                    dscores = dscores + dm_all[:, blk] * jnp.exp(jnp.where(causal, seg, NEG_INF))
                dx_parts.append(_dot(mt_all[...], dy_bd[...], NN))
            hprev = hprev_ref[g]
            hprev_b = hprev.astype(BF16)
            dhn = dstate[g]
            dhn_b = dhn.astype(BF16)
            e_e, dec_e = cm['e_e'][:, cols], cm['dec_e'][:, cols]
            y_scan = y_ref[:, cols] - cm['dsk_e'][:, cols] * xs
            dye_b = (dyf * e_e).astype(BF16)
            dcg = _dot(dye_b, hprev_b, NN)
            dhp = _dot(dye_b, cg, TN)
            bdh = _dot(bg, dhn_b, NT)
            dbg = _dot((xdt * dec_e).astype(BF16), dhn_b, NN)
            dx_diag = jnp.concatenate(dx_parts, axis=1)
            dx = dec_e * bdh + dx_diag
            ds_b = dscores.astype(BF16)
            dcg = dcg + _dot(ds_b, bg, NN)
            dbg = dbg + _dot(ds_b, cg, TN)
            x_rounded = xdt_s[...].astype(F32)
            sums = sums + _hilo_dot(jnp.concatenate([dyf * y_scan, xdt * bdh, x_rounded * dx_diag, dx * xs, dyf * xs], axis=0), seg_sum)
            state_dot = state_dot + jnp.sum(_hilo_dot(dhn * hprev, seg_sum, TN), axis=0, keepdims=True)
            dxc_ref[:, cols] = (dx * cm['dt_e'][:, cols] + cm['dsk_e'][:, cols] * dyf).astype(dxc_ref.dtype)
            dxc_ref[:, d_inner + g * nst:d_inner + (g + 1) * nst] = dbg.astype(dxc_ref.dtype)
            dxc_ref[:, d_inner + gn + g * nst:d_inner + gn + (g + 1) * nst] = dcg.astype(dxc_ref.dtype)
            for hh in range(hpg):
                h = g * hpg + hh
                rows = slice(hh * p, (hh + 1) * p)
                dstate[g, rows, :] = jnp.exp(cm['a_last'][:, h:h + 1]) * dhn[rows] + dhp[rows]
        s_y, s_end, s_diag, s_dt, s_skip = (sums[k * q:(k + 1) * q] for k in range(5))
        dt, a, dec_end = cm['dt'], cm['a'], cm['dec_end']
        last_row = (lax.broadcasted_iota(jnp.int32, (q, 1), 0) == q - 1).astype(F32)
        da_last = jnp.sum(dec_end * s_end, axis=0, keepdims=True) + jnp.exp(cm['a_last']) * state_dot
        da = s_y - dec_end * s_end - s_diag + last_row * da_last
        ddta = _dot_hi(cm['triu'], da)
        ddt = s_dt + ddta * a
        d_a = jnp.sum(ddta * dt, axis=0, keepdims=True)
        ddt_raw = ddt * _sigmoid(dtraw_ref[...] + dtb_ref[...])
        ddtraw_ref[...] = ddt_raw
        ddtb_ref[...] += jnp.sum(ddt_raw, axis=0, keepdims=True)
        dalog_ref[...] += d_a * a
        ddsk_ref[...] += jnp.sum(s_skip, axis=0, keepdims=True)

    def rowmap(b, c):
        return b * nc + (nc - 1 - c)
    vec = pl.BlockSpec((1, LANES), lambda b, c: (0, 0))
    vec_shape = jax.ShapeDtypeStruct((1, LANES), F32)
    return _hosted_call(
        body,
        out_shape=[jax.ShapeDtypeStruct((t, c_dim), BF16), jax.ShapeDtypeStruct((t, LANES), F32), vec_shape, vec_shape, vec_shape],
        grid=(bl, nc),
        in_specs=[pl.BlockSpec((q, d_inner), lambda b, c: (rowmap(b, c), 0)),
                  pl.BlockSpec((q, d_inner), lambda b, c: (rowmap(b, c), 0)),
                  pl.BlockSpec((q, d_inner), lambda b, c: (rowmap(b, c), 0)),
                  pl.BlockSpec((q, gn), lambda b, c: (rowmap(b, c), d_inner // gn)),
                  pl.BlockSpec((q, gn), lambda b, c: (rowmap(b, c), d_inner // gn + 1)),
                  pl.BlockSpec((q, LANES), lambda b, c: (rowmap(b, c), 0)),
                  pl.BlockSpec((None, grp, gw, nst), lambda b, c: (rowmap(b, c), 0, 0, 0)), vec, vec, vec],
        out_specs=[pl.BlockSpec((q, c_dim), lambda b, c: (rowmap(b, c), 0)),
                   pl.BlockSpec((q, LANES), lambda b, c: (rowmap(b, c), 0)), vec, vec, vec],
        scratch_shapes=[pltpu.VMEM((grp, gw, nst), F32), pltpu.VMEM((q, hb * q), BF16),
                        pltpu.VMEM((hb * q, hb * p), BF16), pltpu.VMEM((hb * q, hb * p), BF16), pltpu.VMEM((q, gw), BF16)],
        operands=[dy, y, xc, xc, xc, dt_raw, hprev_all, dt_bias, a_log, d_skip], side=side, name=name)


def _gated_norm_fwd(y, z, ng, *, name):
    t, d = y.shape
    tm = _tile(t, 256)
    gw = d // SSM_GROUPS

    def body(y_ref, z_ref, ng_ref, o_ref):
        for g in range(SSM_GROUPS):
            sl = slice(g * gw, (g + 1) * gw)
            zv = z_ref[:, sl].astype(F32)
            yg = y_ref[:, sl] * (zv * _sigmoid(zv))
            r = lax.rsqrt(jnp.mean(yg * yg, axis=-1, keepdims=True) + RMS_EPS)
            o_ref[:, sl] = (yg * r * ng_ref[:, sl]).astype(o_ref.dtype)

    row = pl.BlockSpec((tm, d), lambda i: (i, 0))
    return pl.pallas_call(body, out_shape=jax.ShapeDtypeStruct((t, d), BF16), grid=(t // tm,),
                          in_specs=[row, row, _const_spec((1, d))], out_specs=row, name=name, compiler_params=_params("parallel"))(y, z, ng)


def _gated_norm_bwd(dyn, y, z, ng, *, name):
    t, d = y.shape
    tm = _tile(t, 256)
    gw = d // SSM_GROUPS

    def body(dyn_ref, y_ref, z_ref, ng_ref, dy_ref, dz_ref, dng_ref):
        @pl.when(pl.program_id(0) == 0)
        def _():
            dng_ref[...] = jnp.zeros_like(dng_ref)
        for g in range(SSM_GROUPS):
            sl = slice(g * gw, (g + 1) * gw)
            zv = z_ref[:, sl].astype(F32)
            yv = y_ref[:, sl]
            sg = _sigmoid(zv)
            sz = zv * sg
            yg = yv * sz
            r = lax.rsqrt(jnp.mean(yg * yg, axis=-1, keepdims=True) + RMS_EPS)
            yhat = yg * r
            dn = dyn_ref[:, sl].astype(F32)
            dyg_n = dn * ng_ref[:, sl]
            dyg = r * (dyg_n - yhat * jnp.mean(dyg_n * yhat, axis=-1, keepdims=True))
            dy_ref[:, sl] = (dyg * sz).astype(dy_ref.dtype)
            dz_ref[:, sl] = (dyg * yv * (sg * (1.0 + zv * (1.0 - sg)))).astype(dz_ref.dtype)
            dng_ref[:, sl] += jnp.sum(dn * yhat, axis=0, keepdims=True)

    row = pl.BlockSpec((tm, d), lambda i: (i, 0))
    shp = jax.ShapeDtypeStruct((t, d), BF16)
    return pl.pallas_call(body, out_shape=(shp, shp, jax.ShapeDtypeStruct((1, d), F32)), grid=(t // tm,),
                          in_specs=[row, row, row, _const_spec((1, d))], out_specs=(row, row, _const_spec((1, d))),
                          name=name, compiler_params=_params("arbitrary"))(dyn, y, z, ng)


def _bucket_onehot():
    blk = CHUNK
    qi = jnp.arange(blk)[:, None]
    kj = jnp.arange(2 * blk)[None, :]
    dist = jnp.maximum(qi + blk - kj, 0)
    max_exact = REL_BUCKETS // 2
    d = jnp.maximum(dist, 1).astype(F32)
    large = max_exact + (jnp.log(d / max_exact) / math.log(REL_MAX_DISTANCE / max_exact) * (REL_BUCKETS - max_exact)).astype(jnp.int32)
    large = jnp.minimum(large, REL_BUCKETS - 1)
    bucket = jnp.where(dist < max_exact, dist, large).reshape(-1)
    return (bucket[None, :] == jnp.arange(REL_BUCKETS)[:, None]).astype(F32)


def _small_mm_hi(a, b, dims, *, name):
    def body(a_ref, b_ref, o_ref):
        o_ref[...] = _dot_hi(a_ref[...], b_ref[...], dims)
    n = b.shape[0] if dims == NT else b.shape[1]
    return pl.pallas_call(body, out_shape=jax.ShapeDtypeStruct((a.shape[0], n), F32), name=name)(a, b)


def _attn_band_mask_t(n, rep):
    blk = CHUNK
    jj = lax.broadcasted_iota(jnp.int32, (2 * blk, rep * blk), 0)
    ii = lax.broadcasted_iota(jnp.int32, (2 * blk, rep * blk), 1) & (blk - 1)
    dist = ii + blk - jj
    in_window = jnp.logical_and(dist >= 0, dist < blk)
    return jnp.logical_and(in_window, jnp.logical_or(jj >= blk, n > 0))


def _sink_row(sink_ref, heads):
    return jnp.concatenate([jnp.broadcast_to(sink_ref[:, h:h + 1], (1, CHUNK)) for h in heads], axis=1)


def _attn_fwd(q, kv, bias_t, sinks, bl, *, name):
    t, qd = q.shape
    blk, hd = CHUNK, ATTN_HEAD_DIM
    kvd = ATTN_KV_HEADS * hd
    rep = ATTN_Q_HEADS // ATTN_KV_HEADS
    nb = t // bl // blk
    scale = hd ** -0.5

    def body(q_ref, kp_ref, kc_ref, vp_ref, vc_ref, bias_ref, sink_ref, o_ref, lse_ref):
        n = pl.program_id(1)
        mask = _attn_band_mask_t(n, rep)
        for kvh in range(ATTN_KV_HEADS):
            ks = slice(kvh * hd, (kvh + 1) * hd)
            heads = range(kvh * rep, (kvh + 1) * rep)
            qs = jnp.concatenate([q_ref[:, h * hd:(h + 1) * hd] for h in heads], axis=0)
            kk = jnp.concatenate([kp_ref[:, ks], kc_ref[:, ks]], axis=0)
            vv = jnp.concatenate([vp_ref[:, ks], vc_ref[:, ks]], axis=0)
            s = jnp.where(mask, _dot(kk, qs, NT) * scale + bias_ref[kvh], NEG_INF)
            sink = _sink_row(sink_ref, heads)
            m = jnp.maximum(jnp.max(s, axis=0, keepdims=True), sink)
            p = jnp.exp(s - m)
            den = jnp.sum(p, axis=0, keepdims=True) + jnp.exp(sink - m)
            o = _dot((p * (1.0 / den)).astype(BF16), vv, TN)
            lse = m + jnp.log(den)
            for r, h in enumerate(heads):
                o_ref[:, h * hd:(h + 1) * hd] = o[r * blk:(r + 1) * blk].astype(o_ref.dtype)
                lse_ref[h:h + 1, :] = lse[:, r * blk:(r + 1) * blk]

    def cur(b, n):
        return b * nb + n

    def prev(b, n):
        return b * nb + jnp.maximum(n - 1, 0)
    return pl.pallas_call(
        body, out_shape=(jax.ShapeDtypeStruct((t, qd), BF16), jax.ShapeDtypeStruct((t // blk * ATTN_Q_HEADS, blk), F32)), grid=(bl, nb),
        in_specs=[pl.BlockSpec((blk, qd), lambda b, n: (cur(b, n), 0)),
                  pl.BlockSpec((blk, kvd), lambda b, n: (prev(b, n), 0)), pl.BlockSpec((blk, kvd), lambda b, n: (cur(b, n), 0)),
                  pl.BlockSpec((blk, kvd), lambda b, n: (prev(b, n), 1)), pl.BlockSpec((blk, kvd), lambda b, n: (cur(b, n), 1)),
                  _const_spec(bias_t.shape), _const_spec((1, LANES))],
        out_specs=(pl.BlockSpec((blk, qd), lambda b, n: (cur(b, n), 0)),
                   pl.BlockSpec((ATTN_Q_HEADS, blk), lambda b, n: (cur(b, n), 0))),
        name=name, compiler_params=_params("parallel", "arbitrary"))(q, kv, kv, kv, kv, bias_t, sinks)


def _attn_bwd(do, q, kv, lse, bias_t, sinks, bl, *, name):
    t, qd = q.shape
    blk, hd = CHUNK, ATTN_HEAD_DIM
    kvd = ATTN_KV_HEADS * hd
    rep = ATTN_Q_HEADS // ATTN_KV_HEADS
    s_len = t // bl
    nb = s_len // blk
    scale = hd ** -0.5

    def body(do_ref, q_ref, kp_ref, kc_ref, vp_ref, vc_ref, lse_ref, bias_ref, sink_ref, dq_ref, dkv_ref, dbias_ref, dsink_ref):
        n = pl.program_id(1)

        @pl.when(jnp.logical_and(pl.program_id(0) == 0, n == 0))
        def _():
            dbias_ref[...] = jnp.zeros_like(dbias_ref)
            dsink_ref[...] = jnp.zeros_like(dsink_ref)

        mask = _attn_band_mask_t(n, rep)
        r_cur = pl.multiple_of(n * blk, blk)
        r_prev = pl.multiple_of(jnp.maximum(n - 1, 0) * blk, blk)
        dsink = jnp.zeros((1, LANES), F32)
        for kvh in range(ATTN_KV_HEADS):
            ks = slice(kvh * hd, (kvh + 1) * hd)
            heads = range(kvh * rep, (kvh + 1) * rep)
            qs = jnp.concatenate([q_ref[:, h * hd:(h + 1) * hd] for h in heads], axis=0)
            dos = jnp.concatenate([do_ref[:, h * hd:(h + 1) * hd] for h in heads], axis=0)
            kk = jnp.concatenate([kp_ref[:, ks], kc_ref[:, ks]], axis=0)
            vv = jnp.concatenate([vp_ref[:, ks], vc_ref[:, ks]], axis=0)
            lse = jnp.concatenate([lse_ref[h:h + 1, :] for h in heads], axis=1)
            p = jnp.exp(jnp.where(mask, _dot(kk, qs, NT) * scale + bias_ref[kvh], NEG_INF) - lse)
            dp = _dot(vv, dos, NT)
            delta = jnp.sum(p * dp, axis=0, keepdims=True)
            ds = p * (dp - delta)
            dsink_row = jnp.exp(_sink_row(sink_ref, heads) - lse) * delta
            dbias_ref[kvh] += ds
            ds_b = ds.astype(BF16)
            dq_s = _dot(ds_b, kk, TN) * scale
            dkk = _dot(ds_b, qs, NN) * scale
            dvv = _dot(p.astype(BF16), dos, NN)
            for r, h in enumerate(heads):
                dq_ref[:, h * hd:(h + 1) * hd] = dq_s[r * blk:(r + 1) * blk].astype(dq_ref.dtype)
                dsink = dsink - jnp.sum(dsink_row[:, r * blk:(r + 1) * blk], axis=1, keepdims=True) * _lane_onehot(h)
            vs = slice(kvd + kvh * hd, kvd + (kvh + 1) * hd)
            dkv_ref[pl.ds(r_cur, blk), ks] = dkk[blk:]
            dkv_ref[pl.ds(r_cur, blk), vs] = dvv[blk:]

            @pl.when(n > 0)
            def _():
                dkv_ref[pl.ds(r_prev, blk), ks] += dkk[:blk]
                dkv_ref[pl.ds(r_prev, blk), vs] += dvv[:blk]
        dsink_ref[...] += dsink

    def cur(b, n):
        return b * nb + n

    def prev(b, n):
        return b * nb + jnp.maximum(n - 1, 0)
    qspec = pl.BlockSpec((blk, qd), lambda b, n: (cur(b, n), 0))
    return pl.pallas_call(
        body,
        out_shape=(jax.ShapeDtypeStruct((t, qd), BF16), jax.ShapeDtypeStruct((t, 2 * kvd), F32),
                   jax.ShapeDtypeStruct(bias_t.shape, F32), jax.ShapeDtypeStruct((1, LANES), F32)),
        grid=(bl, nb),
        in_specs=[qspec, qspec,
                  pl.BlockSpec((blk, kvd), lambda b, n: (prev(b, n), 0)), pl.BlockSpec((blk, kvd), lambda b, n: (cur(b, n), 0)),
                  pl.BlockSpec((blk, kvd), lambda b, n: (prev(b, n), 1)), pl.BlockSpec((blk, kvd), lambda b, n: (cur(b, n), 1)),
                  pl.BlockSpec((ATTN_Q_HEADS, blk), lambda b, n: (cur(b, n), 0)), _const_spec(bias_t.shape), _const_spec((1, LANES))],
        out_specs=(qspec, pl.BlockSpec((s_len, 2 * kvd), lambda b, n: (b, 0)), _const_spec(bias_t.shape), _const_spec((1, LANES))),
        name=name, compiler_params=_params("arbitrary", "arbitrary"))(do, q, kv, kv, kv, kv, lse, bias_t, sinks)


def _merge_fwd(yn, o, gs, ga, w_ssm, w_attn, w_out, h_in, g_post, *, name):
    t, d = h_in.shape
    tm = _tile(t, 256)

    def body(yn_ref, o_ref, gs_ref, ga_ref, ws_ref, wa_ref, wo_ref, hin_ref, gp_ref, ys_ref, ya_ref, mg_ref, mix_ref, hout_ref):
        ys = _dot(yn_ref[...], ws_ref[...], NN)
        ya = _dot(o_ref[...], wa_ref[...], NN)
        merged = (_sigmoid(gs_ref[...].astype(F32)) * ys + _sigmoid(ga_ref[...].astype(F32)) * ya).astype(BF16)
        mix = _dot(merged, wo_ref[...], NN)
        ys_ref[...] = ys.astype(BF16)
        ya_ref[...] = ya.astype(BF16)
        mg_ref[...] = merged
        mix_ref[...] = mix
        hout_ref[...] = _rms_residual(mix, hin_ref[...], gp_ref[...], 1.0)

    def row(w):
        return pl.BlockSpec((tm, w), lambda i: (i, 0))
    bshape = jax.ShapeDtypeStruct((t, d), BF16)
    fshape = jax.ShapeDtypeStruct((t, d), F32)
    return pl.pallas_call(
        body, out_shape=(bshape, bshape, bshape, fshape, fshape), grid=(t // tm,),
        in_specs=[row(yn.shape[1]), row(o.shape[1]), row(d), row(d), _resident_spec(w_ssm.shape), _resident_spec(w_attn.shape),
                  _resident_spec(w_out.shape), row(d), _const_spec((1, d))],
        out_specs=(row(d),) * 5, name=name, compiler_params=_params("parallel"))(yn, o, gs, ga, w_ssm, w_attn, w_out, h_in, g_post)


def _merge_bwd(dh, mix, g_post, gs, ga, ys, ya, w_ssm, w_attn, w_out, *, name):
    t, d = mix.shape
    tm = _tile(t, 256)
    d_ssm, d_attn = w_ssm.shape[0], w_attn.shape[0]

    def body(dh_ref, mix_ref, gp_ref, gs_ref, ga_ref, ys_ref, ya_ref, ws_ref, wa_ref, wo_ref,
             dmix_ref, dys_ref, dya_ref, dgs_ref, dga_ref, dyn_ref, do_ref, dgp_ref):
        @pl.when(pl.program_id(0) == 0)
        def _():
            dgp_ref[...] = jnp.zeros_like(dgp_ref)
        mv = mix_ref[...]
        dy = dh_ref[...]
        r = lax.rsqrt(jnp.mean(mv * mv, axis=-1, keepdims=True) + RMS_EPS)
        mhat = mv * r
        dyg = dy * gp_ref[...]
        dmix = (r * (dyg - mhat * jnp.mean(dyg * mhat, axis=-1, keepdims=True))).astype(BF16)
        dgp_ref[...] += jnp.sum(dy * mhat, axis=0, keepdims=True)
        dmix_ref[...] = dmix
        dmerged = _dot(dmix, wo_ref[...], NT)
        sgs = _sigmoid(gs_ref[...].astype(F32))
        sga = _sigmoid(ga_ref[...].astype(F32))
        dys = (dmerged * sgs).astype(BF16)
        dya = (dmerged * sga).astype(BF16)
        dys_ref[...] = dys
        dya_ref[...] = dya
        dgs_ref[...] = (dmerged * ys_ref[...].astype(F32) * sgs * (1.0 - sgs)).astype(BF16)
        dga_ref[...] = (dmerged * ya_ref[...].astype(F32) * sga * (1.0 - sga)).astype(BF16)
        dyn_ref[...] = _dot(dys, ws_ref[...], NT).astype(BF16)
        do_ref[...] = _dot(dya, wa_ref[...], NT).astype(BF16)

    def row(w):
        return pl.BlockSpec((tm, w), lambda i: (i, 0))

    def bshape(w):
        return jax.ShapeDtypeStruct((t, w), BF16)
    return pl.pallas_call(
        body, out_shape=(bshape(d),) * 5 + (bshape(d_ssm), bshape(d_attn), jax.ShapeDtypeStruct((1, d), F32)), grid=(t // tm,),
        in_specs=[row(d), row(d), _const_spec((1, d)), row(d), row(d), row(d), row(d),
                  _resident_spec(w_ssm.shape), _resident_spec(w_attn.shape), _resident_spec(w_out.shape)],
        out_specs=(row(d),) * 5 + (row(d_ssm), row(d_attn), _const_spec((1, d))),
        name=name, compiler_params=_params("arbitrary"))(dh, mix, g_post, gs, ga, ys, ya, w_ssm, w_attn, w_out)


def _adamw(w, g, m, v, *, name):
    r, c = w.shape
    tm = _tile(r, 256)
    c1 = 1.0 - ADAM_B1 ** ADAM_STEP
    c2 = 1.0 - ADAM_B2 ** ADAM_STEP

    def body(w_ref, g_ref, m_ref, v_ref, d_ref, mo_ref, vo_ref):
        gv = g_ref[...]
        mn = ADAM_B1 * m_ref[...] + (1.0 - ADAM_B1) * gv
        vn = ADAM_B2 * v_ref[...] + (1.0 - ADAM_B2) * (gv * gv)
        mo_ref[...] = mn
        vo_ref[...] = vn
        d_ref[...] = -ADAM_LR * ((mn / c1) / (jnp.sqrt(vn / c2) + ADAM_EPS) + ADAM_WD * w_ref[...])

    blk = pl.BlockSpec((tm, c), lambda i: (i, 0))
    shp = jax.ShapeDtypeStruct((r, c), F32)
    return pl.pallas_call(body, out_shape=(shp, shp, shp), grid=(r // tm,), in_specs=[blk] * 4, out_specs=(blk,) * 3,
                          name=name, compiler_params=_params("parallel"))(w, g, m, v)


def _position():
    return lax.axis_index("x"), lax.axis_index("y"), lax.axis_index("c")


def _gather_exchange(shards):
    na = len(shards)

    def plan(ins, outs, sems):
        send_sems, recv_sems, local_sems = sems
        x, y, c = _position()
        me, sibling = (x, y, c), (x, y, 1 - c)
        chips = [(1 - x, y), (x, 1 - y), (1 - x, 1 - y)]

        def slot(a, pos):
            return outs[a].at[4 * pos[0] + 2 * pos[1] + pos[2]]

        def copy(a, k, block, to, src=None):
            return pltpu.make_async_remote_copy(
                src_ref=slot(a, block) if src is None else src, dst_ref=slot(a, block),
                send_sem=send_sems.at[a, k], recv_sem=recv_sems.at[a, k], device_id=to, device_id_type=MESH)

        mine = [pltpu.make_async_copy(ins[a], slot(a, me), local_sems.at[a]) for a in range(na)]
        first = []
        for a in range(na):
            first.append(copy(a, 0, me, sibling, src=ins[a]))
            first += [copy(a, 1 + j, me, (*chip, c), src=ins[a]) for j, chip in enumerate(chips)]
        return me, sibling, chips, copy, mine, first

    def start(ins, outs, sems):
        *_, mine, first = plan(ins, outs, sems)
        for cp in mine + first:
            cp.start()

    def finish(ins, outs, sems):
        me, sibling, chips, copy, mine, first = plan(ins, outs, sems)
        c = me[2]
        passed = []
        for a in range(na):
            for j, chip in enumerate(chips):
                copy(a, 1 + j, (*chip, c), me).wait_recv()
                fwd = copy(a, 4 + j, (*chip, c), sibling)
                fwd.start()
                passed.append(fwd)
        for a in range(na):
            copy(a, 0, sibling, me).wait_recv()
            for j, chip in enumerate(chips):
                copy(a, 4 + j, (*chip, 1 - c), me).wait_recv()
        for cp in first + passed:
            cp.wait_send()
        for cp in mine:
            cp.wait()

    return _Exchange(list(shards), [jax.ShapeDtypeStruct((N_DEV,) + s.shape, s.dtype) for s in shards],
                     [pltpu.SemaphoreType.DMA((na, 7)), pltpu.SemaphoreType.DMA((na, 7)), pltpu.SemaphoreType.DMA((na,))],
                     start, finish)


def _scatter_exchange(arrays):
    na = len(arrays)

    def copies(ins, outs, sems):
        send_sems, recv_sems = sems
        x, y, c = _position()
        out = []
        for a in range(na):
            for k in range(7):
                flip = k + 1
                peer = (x ^ (flip >> 2), y ^ ((flip >> 1) & 1), c ^ (flip & 1))
                peer_block = 4 * peer[0] + 2 * peer[1] + peer[2]
                out.append(pltpu.make_async_remote_copy(
                    src_ref=ins[a].at[peer_block], dst_ref=outs[a].at[k],
                    send_sem=send_sems.at[a, k], recv_sem=recv_sems.at[a, k], device_id=peer, device_id_type=MESH))
        return out

    def start(ins, outs, sems):
        for cp in copies(ins, outs, sems):
            cp.start()

    def finish(ins, outs, sems):
        for cp in copies(ins, outs, sems):
            cp.wait()

    return _Exchange(list(arrays), [jax.ShapeDtypeStruct((7,) + s.shape[1:], s.dtype) for s in arrays],
                     [pltpu.SemaphoreType.DMA((na, 7)), pltpu.SemaphoreType.DMA((na, 7))], start, finish)


def _reduce_blocks(own, recv, *, name):
    r, c = own.shape
    tm = _tile(r, 256)

    def body(own_ref, recv_ref, o_ref):
        acc = own_ref[...]
        for k in range(7):
            acc = acc + recv_ref[k].astype(F32)
        o_ref[...] = acc

    return pl.pallas_call(
        body, out_shape=jax.ShapeDtypeStruct((r, c), F32), grid=(r // tm,),
        in_specs=[pl.BlockSpec((tm, c), lambda i: (i, 0)), pl.BlockSpec((7, tm, c), lambda i: (0, i, 0))],
        out_specs=pl.BlockSpec((tm, c), lambda i: (i, 0)), name=name, compiler_params=_params("parallel"))(own, recv)


def _reduce_adamw(own, recv, w, m, v, *, name):
    r, c = own.shape
    tm = _tile(r, 256)
    c1 = 1.0 - ADAM_B1 ** ADAM_STEP
    c2 = 1.0 - ADAM_B2 ** ADAM_STEP

    def body(own_ref, recv_ref, w_ref, m_ref, v_ref, g_ref, d_ref, mo_ref, vo_ref):
        gv = own_ref[...]
        for k in range(7):
            gv = gv + recv_ref[k].astype(F32)
        g_ref[...] = gv
        mn = ADAM_B1 * m_ref[...] + (1.0 - ADAM_B1) * gv
        vn = ADAM_B2 * v_ref[...] + (1.0 - ADAM_B2) * (gv * gv)
        mo_ref[...] = mn
        vo_ref[...] = vn
        d_ref[...] = -ADAM_LR * ((mn / c1) / (jnp.sqrt(vn / c2) + ADAM_EPS) + ADAM_WD * w_ref[...])

    blk = pl.BlockSpec((tm, c), lambda i: (i, 0))
    shp = jax.ShapeDtypeStruct((r, c), F32)
    return pl.pallas_call(body, out_shape=(shp,) * 4, grid=(r // tm,),
                          in_specs=[blk, pl.BlockSpec((7, tm, c), lambda i: (0, i, 0)), blk, blk, blk], out_specs=(blk,) * 4,
                          name=name, compiler_params=_params("parallel"))(own, recv, w, m, v)


def _all_reduce_small(vec, *, name):
    r, c = vec.shape

    def body(v_ref, o_ref, buf, send_sems, recv_sems):
        x, y, c_ = _position()
        me = 4 * x + 2 * y + c_
        buf[me] = v_ref[...]
        copies = []
        for k in range(7):
            flip = k + 1
            peer = (x ^ (flip >> 2), y ^ ((flip >> 1) & 1), c_ ^ (flip & 1))
            cp = pltpu.make_async_remote_copy(
                src_ref=v_ref, dst_ref=buf.at[me], send_sem=send_sems.at[k], recv_sem=recv_sems.at[k],
                device_id=peer, device_id_type=MESH)
            cp.start()
            copies.append(cp)
        for cp in copies:
            cp.wait()
        acc = buf[0]
        for d in range(1, N_DEV):
            acc = acc + buf[d]
        o_ref[...] = acc

    vm = pl.BlockSpec(memory_space=pltpu.VMEM)
    return pl.pallas_call(
        body, out_shape=jax.ShapeDtypeStruct((r, c), F32), in_specs=[vm], out_specs=vm,
        scratch_shapes=[pltpu.VMEM((N_DEV, r, c), F32), pltpu.SemaphoreType.DMA((7,)), pltpu.SemaphoreType.DMA((7,))],
        name=name)(vec)


def _pad_lanes(v, width=LANES):
    return jnp.pad(v, ((0, 0), (0, width - v.shape[1])))


def kernel(x, ffn1_pre_g, ffn1_w_gate, ffn1_w_up, ffn1_w_down, ffn1_post_g, mix_pre_g, w_in, conv_w, conv_b, dt_bias, a_log, d_skip, ssm_norm_g, w_ssm_proj, attn_sinks, rel_bias_table, w_attn_proj, w_out, mix_post_g, ffn2_pre_g, ffn2_w_gate, ffn2_w_up, ffn2_w_down, ffn2_post_g, loss_target, m_ffn1_pre_g, m_ffn1_w_gate, m_ffn1_w_up, m_ffn1_w_down, m_ffn1_post_g, m_mix_pre_g, m_w_in, m_conv_w, m_conv_b, m_dt_bias, m_a_log, m_d_skip, m_ssm_norm_g, m_w_ssm_proj, m_attn_sinks, m_rel_bias_table, m_w_attn_proj, m_w_out, m_mix_post_g, m_ffn2_pre_g, m_ffn2_w_gate, m_ffn2_w_up, m_ffn2_w_down, m_ffn2_post_g, v_ffn1_pre_g, v_ffn1_w_gate, v_ffn1_w_up, v_ffn1_w_down, v_ffn1_post_g, v_mix_pre_g, v_w_in, v_conv_w, v_conv_b, v_dt_bias, v_a_log, v_d_skip, v_ssm_norm_g, v_w_ssm_proj, v_attn_sinks, v_rel_bias_table, v_w_attn_proj, v_w_out, v_mix_post_g, v_ffn2_pre_g, v_ffn2_w_gate, v_ffn2_w_up, v_ffn2_w_down, v_ffn2_post_g):
    args = dict(locals())
    weight_names = ['ffn1_pre_g', 'ffn1_w_gate', 'ffn1_w_up', 'ffn1_w_down', 'ffn1_post_g', 'mix_pre_g', 'w_in', 'conv_w', 'conv_b',
                    'dt_bias', 'a_log', 'd_skip', 'ssm_norm_g', 'w_ssm_proj', 'attn_sinks', 'rel_bias_table', 'w_attn_proj', 'w_out',
                    'mix_post_g', 'ffn2_pre_g', 'ffn2_w_gate', 'ffn2_w_up', 'ffn2_w_down', 'ffn2_post_g']
    col_sharded = ('ffn1_w_gate', 'ffn1_w_up', 'w_in', 'ffn2_w_gate', 'ffn2_w_up')
    row_sharded = ('ffn1_w_down', 'w_ssm_proj', 'w_attn_proj', 'w_out', 'ffn2_w_down')
    big = col_sharded + row_sharded

    bl, s_len, d = x.shape
    t = bl * s_len
    d_inner = ssm_norm_g.shape[1]
    n_heads = dt_bias.shape[1]
    gn = SSM_GROUPS * SSM_STATE
    conv_dim = d_inner + 2 * gn
    q_dim = ATTN_Q_HEADS * ATTN_HEAD_DIM
    kv_dim = ATTN_KV_HEADS * ATTN_HEAD_DIM

    def local_2d(name, a):
        a = a[0]
        return a.T if name in col_sharded else a

    ffn1_names = ('ffn1_w_gate', 'ffn1_w_up', 'ffn1_w_down')
    ffn2_names = ('ffn2_w_gate', 'ffn2_w_up', 'ffn2_w_down')
    mixer_names = ('w_ssm_proj', 'w_attn_proj', 'w_out')

    def shard(n):
        return local_2d(n, args[n]).astype(BF16)

    def rows(g):
        return g.reshape(N_DEV * g.shape[1], g.shape[2])

    x2 = x.reshape(t, d)
    tgt2 = loss_target.reshape(t, d)
    full = {}

    (h1,), saved1, ffn1_w, got_in = _ffn_forward(
        x2, ffn1_pre_g, lambda got: (rows(got[0]), rows(got[1])), lambda got: rows(got[2]), ffn1_post_g, "ffn1",
        side_norm=_gather_exchange([shard(n) for n in ffn1_names[:2]]),
        side_up=_gather_exchange([shard('w_in'), conv_w[0], shard('ffn1_w_down')]))
    full.update(zip(ffn1_names, ffn1_w))
    conv_w_full = jnp.transpose(got_in[1], (1, 0, 2)).reshape(SSM_CONV, conv_dim)

    win_t = rows(got_in[0])
    dt_lo = 2 * d + d_inner + conv_dim
    off = {'gs': 0, 'ga': d, 'z': 2 * d, 'xbc': 2 * d + d_inner, 'dt': dt_lo, 'q': dt_lo + n_heads, 'kv': dt_lo + n_heads + q_dim}
    assert all(o_ % 16 == 0 for o_ in off.values())

    u, (gs, ga, z, xbc, q, kv, dt_raw) = _proj_all(
        h1, mix_pre_g, win_t,
        [(off['gs'], d, BF16), (off['ga'], d, BF16), (off['z'], d_inner, BF16), (off['xbc'], conv_dim, BF16),
         (off['q'], q_dim, BF16), (off['kv'], 2 * kv_dim, BF16), (off['dt'], n_heads, F32)], name="mix_proj")

    dtb_p, alog_p, dsk_p, sinks_p = _pad_lanes(dt_bias), _pad_lanes(a_log), _pad_lanes(d_skip), _pad_lanes(attn_sinks)
    xc = _conv_fwd(xbc, conv_w_full, conv_b, bl, name="conv_fwd")
    late_names = mixer_names + ffn2_names
    (y, hprev), got_late = _ssd_fwd(xc, dt_raw, dtb_p, alog_p, dsk_p, bl, n_heads,
                                    side=_gather_exchange([shard(n) for n in late_names]), name="ssd_fwd")
    full.update({n: rows(g) for n, g in zip(late_names, got_late)})
    yn = _gated_norm_fwd(y, z, ssm_norm_g, name="gated_norm_fwd")

    onehot = _bucket_onehot()
    rep = ATTN_Q_HEADS // ATTN_KV_HEADS
    bias = _small_mm_hi(rel_bias_table.T, onehot, NN, name="rel_bias")
    bias_t = jnp.transpose(bias.reshape(ATTN_KV_HEADS, rep, CHUNK, 2 * CHUNK), (0, 3, 1, 2)).reshape(ATTN_KV_HEADS, 2 * CHUNK, rep * CHUNK)
    o, lse = _attn_fwd(q, kv, bias_t, sinks_p, bl, name="attn_fwd")

    ys, ya, merged, mix, h2 = _merge_fwd(yn, o, gs, ga, full['w_ssm_proj'], full['w_attn_proj'], full['w_out'], h1, mix_post_g,
                                         name="merge_fwd")

    (dh3, loss_vec), saved2, _, _ = _ffn_forward(h2, ffn2_pre_g, (full['ffn2_w_gate'], full['ffn2_w_up']), full['ffn2_w_down'],
                                                 ffn2_post_g, "ffn2", target=tgt2)
    loss = lax.psum(loss_vec[0, 0], ("x", "y", "c"))

    grads, own, wire, received = {}, {}, {}, {}
    dh2, grads['ffn2_pre_g'], grads['ffn2_post_g'], g32, g16, _ = _ffn_backward(
        dh3, saved2, ffn2_pre_g, full['ffn2_w_gate'], full['ffn2_w_up'], full['ffn2_w_down'], ffn2_post_g, "ffn2")
    own.update(zip(ffn2_names, map(_stack8, g32)))
    wire.update(zip(ffn2_names, map(_stack8, g16)))

    dmix, dys, dya, dgs, dga, dyn, do, grads['mix_post_g'] = _merge_bwd(
        dh2, mix, mix_post_g, gs, ga, ys, ya, full['w_ssm_proj'], full['w_attn_proj'], full['w_out'], name="merge_bwd")
    for n, (lhs, rhs) in zip(mixer_names, ((yn, dys), (o, dya), (merged, dmix))):
        g32_, g16_, _ = _mm_tn([lhs], rhs, name=f"d{n}")
        own[n], wire[n] = _stack8(g32_), _stack8(g16_)

    dq, dkv, dbias_t, dsinks = _attn_bwd(do, q, kv, lse, bias_t, sinks_p, bl, name="attn_bwd")
    dbias = jnp.transpose(dbias_t.reshape(ATTN_KV_HEADS, 2 * CHUNK, rep, CHUNK), (0, 2, 3, 1)).reshape(ATTN_Q_HEADS, -1)
    d_table = _small_mm_hi(onehot, dbias, NT, name="rel_bias_bwd")

    dy, dz, grads['ssm_norm_g'] = _gated_norm_bwd(dyn, y, z, ssm_norm_g, name="gated_norm_bwd")

    first_group = ffn2_names + mixer_names
    (dxc, ddt_raw, ddtb, dalog, ddsk), got = _ssd_bwd(dy, y, xc, dt_raw, hprev, dtb_p, alog_p, dsk_p, bl, n_heads,
                                                      side=_scatter_exchange([wire[n] for n in first_group]), name="ssd_bwd")
    received.update(zip(first_group, got))
    dxbc, dconv_w8, grads['conv_b'] = _conv_bwd(dxc, xbc, conv_w_full, conv_b, bl, name="conv_bwd")

    wide32, wide16, _ = _mm_tn([dgs, dga, dz, dxbc, dq], u, name="dw_in")
    kv32, kv16, _ = _mm_tn([dkv], u, name="dw_in_kv")
    dt32, dt16, _ = _mm_tn([ddt_raw], u, name="dw_in_dt")

    def original_order(wide, kv_part, dt_part):
        return jnp.concatenate([wide[:dt_lo], dt_part[:n_heads], wide[dt_lo:], kv_part], axis=0)
    me = 4 * lax.axis_index("x") + 2 * lax.axis_index("y") + lax.axis_index("c")
    blk_rows = win_t.shape[0] // N_DEV
    wire['w_in'] = _stack8(original_order(wide16, kv16, dt16))
    own_w_in = lax.dynamic_slice_in_dim(original_order(wide32, kv32, dt32), me * blk_rows, blk_rows)
    own['conv_w'] = jnp.transpose(dconv_w8[:SSM_CONV].reshape(SSM_CONV, N_DEV, conv_dim // N_DEV), (1, 0, 2))

    segs = [(g_, 0, off[k_]) for g_, k_ in zip([dgs, dga, dz, dxbc, dq, dkv, ddt_raw], ('gs', 'ga', 'z', 'xbc', 'q', 'kv', 'dt'))]
    dh1, grads['mix_pre_g'], got = _mm_nn_rmsbwd(segs, [win_t], h1, mix_pre_g, dh2,
                                                 side=_scatter_exchange([wire['w_in'], own['conv_w']]), name="mix_du")
    received.update(zip(('w_in', 'conv_w'), got))

    dx2, grads['ffn1_pre_g'], grads['ffn1_post_g'], g32, _, got = _ffn_backward(
        dh1, saved1, ffn1_pre_g, full['ffn1_w_gate'], full['ffn1_w_up'], full['ffn1_w_down'], ffn1_post_g, "ffn1", chain=True)
    own.update(zip(ffn1_names, map(_stack8, g32)))
    received.update(zip(ffn1_names, got))

    def own_block(a):
        return lax.dynamic_index_in_dim(a, me, 0, keepdims=False)
    out_g, out_d, out_m, out_v = {}, {}, {}, {}
    for n in big:
        w2, m2, v2 = local_2d(n, args[n]), local_2d(n, args['m_' + n]), local_2d(n, args['v_' + n])
        results = _reduce_adamw(own_w_in if n == 'w_in' else own_block(own[n]), received[n], w2, m2, v2, name=f"update_{n}")
        out_g[n], out_d[n], out_m[n], out_v[n] = ((a.T if n in col_sharded else a)[None] for a in results)
    conv_sum = _reduce_blocks(own_block(own['conv_w']), received['conv_w'], name="reduce_conv_w")

    grads['dt_bias'], grads['a_log'], grads['d_skip'] = ddtb[:, :n_heads], dalog[:, :n_heads], ddsk[:, :n_heads]
    grads['attn_sinks'] = dsinks[:, :ATTN_Q_HEADS]
    grads['rel_bias_table'] = d_table
    small = [n for n in weight_names if n not in big and n != 'conv_w']
    flat = jnp.concatenate([grads[n].reshape(-1) for n in small])
    n_small = flat.shape[0]
    n_rows = -(-n_small // (8 * LANES)) * 8
    flat = jnp.pad(flat, (0, n_rows * LANES - n_small)).reshape(n_rows, LANES)
    summed = _all_reduce_small(flat, name="allreduce_small").reshape(-1)
    pos = 0
    for n in small:
        size = grads[n].size
        grads[n] = summed[pos:pos + size].reshape(args[n].shape)
        pos += size

    def pack(prefix):
        vals = [(grads[n] if prefix == 'g' else args[prefix + n]).reshape(-1) for n in small]
        vals.append((conv_sum if prefix == 'g' else args[prefix + 'conv_w']).reshape(-1))
        flat_ = jnp.concatenate(vals)
        rows_ = -(-flat_.shape[0] // (8 * LANES)) * 8
        return jnp.pad(flat_, (0, rows_ * LANES - flat_.shape[0])).reshape(rows_, LANES)

    g_small = pack('g')
    d_small, m_small, v_small = _adamw(pack(''), g_small, pack('m_'), pack('v_'), name="adamw_small")
    pos = 0
    for n in small + ['conv_w']:
        shape = args[n].shape
        size = int(np.prod(shape))
        for dst, src in ((out_g, g_small), (out_d, d_small), (out_m, m_small), (out_v, v_small)):
            dst[n] = src.reshape(-1)[pos:pos + size].reshape(shape)
        pos += size

    grad_x = dx2.reshape(bl, s_len, d)
    return (loss, grad_x, *[out_g[n] for n in weight_names], *[out_d[n] for n in weight_names],
            *[out_m[n] for n in weight_names], *[out_v[n] for n in weight_names])
```

```python
import functools
import math

import numpy as np
import jax
import jax.numpy as jnp
from jax import lax
from jax.experimental import pallas as pl
from jax.experimental.pallas import tpu as pltpu

F32 = jnp.float32
BF16 = jnp.bfloat16
MESH = pl.DeviceIdType.MESH
N_DEV = 8

SSM_HEAD_DIM = 64
SSM_GROUPS = 4
SSM_STATE = 128
SSM_CONV = 4
CHUNK = 128
ATTN_HEAD_DIM = 64
ATTN_Q_HEADS = 16
ATTN_KV_HEADS = 4
REL_BUCKETS = 32
REL_MAX_DISTANCE = 128
RMS_EPS = 1e-6
FFN_RESIDUAL_WEIGHT = 0.5
ADAM_LR, ADAM_B1, ADAM_B2, ADAM_EPS, ADAM_WD, ADAM_STEP = 0.001, 0.9, 0.999, 1e-08, 0.01, 10

LANES = 128
VMEM_LIMIT_BYTES = 56 * 1024 * 1024
FFN_COL_TILE = 1408
SSD_HEAD_BATCH = 2

NEG_INF = float("-inf")


def _params(*sem):
    return pltpu.CompilerParams(dimension_semantics=sem, vmem_limit_bytes=VMEM_LIMIT_BYTES)


def _tile(n, pref, mult=8):
    if n <= pref:
        return n
    t = (pref // mult) * mult
    while t >= mult:
        if n % t == 0:
            return t
        t -= mult
    return n


def _sigmoid(x):
    return 1.0 / (1.0 + jnp.exp(-x))


def _dot(a, b, dims):
    return lax.dot_general(a, b, (dims, ((), ())), preferred_element_type=F32)


NN = ((1,), (0,))
NT = ((1,), (1,))
TN = ((0,), (0,))


def _dot_hi(a, b, dims=NN):
    return lax.dot_general(a, b, (dims, ((), ())), preferred_element_type=F32, precision=lax.Precision.HIGHEST)


def _const_spec(shape):
    nd = len(shape)
    return pl.BlockSpec(shape, lambda *_: (0,) * nd)


def _resident_spec(shape):
    nd = len(shape)
    return pl.BlockSpec(shape, lambda *_: (0,) * nd, pipeline_mode=pl.Buffered(1))


class _Exchange:
    def __init__(self, arrays, out_shape, scratch, start, finish):
        self.arrays, self.out_shape, self.scratch, self.start, self.finish = arrays, out_shape, scratch, start, finish


def _hosted_call(body, *, grid, in_specs, out_specs, out_shape, scratch_shapes, operands, side, name):
    in_specs, out_specs, out_shape, scratch_shapes = list(in_specs), list(out_specs), list(out_shape), list(scratch_shapes)
    sem = ("arbitrary",) * len(grid)
    if side is None:
        outs = pl.pallas_call(body, out_shape=tuple(out_shape), grid=grid, in_specs=in_specs, out_specs=tuple(out_specs),
                              scratch_shapes=scratch_shapes, name=name, compiler_params=_params(*sem))(*operands)
        return tuple(outs), ()
    n_in, n_out, n_scr = len(in_specs), len(out_shape), len(scratch_shapes)
    s_in, s_out = len(side.arrays), len(side.out_shape)

    def wrapped(*refs):
        refs = list(refs)
        main_in, side_in = refs[:n_in], refs[n_in:n_in + s_in]
        o0 = n_in + s_in
        main_out, side_out = refs[o0:o0 + n_out], refs[o0 + n_out:o0 + n_out + s_out]
        c0 = o0 + n_out + s_out
        main_scr, side_scr = refs[c0:c0 + n_scr], refs[c0 + n_scr:]
        ids = [pl.program_id(ax) for ax in range(len(grid))]
        first = functools.reduce(jnp.logical_and, [i == 0 for i in ids])
        last = functools.reduce(jnp.logical_and, [i == g - 1 for i, g in zip(ids, grid)])

        @pl.when(first)
        def _():
            side.start(side_in, side_out, side_scr)

        body(*main_in, *main_out, *main_scr)

        @pl.when(last)
        def _():
            side.finish(side_in, side_out, side_scr)

    hbm = pl.BlockSpec(memory_space=pl.ANY)
    outs = pl.pallas_call(
        wrapped, out_shape=tuple(out_shape + list(side.out_shape)), grid=grid,
        in_specs=in_specs + [hbm] * s_in, out_specs=tuple(out_specs + [hbm] * s_out),
        scratch_shapes=scratch_shapes + list(side.scratch), name=name, compiler_params=_params(*sem))(*operands, *side.arrays)
    return tuple(outs[:n_out]), tuple(outs[n_out:])


def _proj_all(h, g, w, segs, *, name):
    t, d = h.shape
    tm = _tile(t, 512)
    segs = [(row0, wd_, max(wd_, LANES), dt_) for row0, wd_, dt_ in segs]
    assert all(row0 + out_w <= w.shape[0] for row0, _, out_w, _ in segs)

    def body(h_ref, g_ref, w_ref, u_ref, *o_refs):
        hv = h_ref[...]
        r = lax.rsqrt(jnp.mean(hv * hv, axis=-1, keepdims=True) + RMS_EPS)
        uv = (hv * r * g_ref[...]).astype(BF16)
        u_ref[...] = uv
        for (row0, width, out_w, _), o_ref in zip(segs, o_refs):
            for c0, c1 in _col_chunks(out_w, 8 * LANES):
                part = _dot(uv, w_ref[row0 + c0:row0 + c1, :], NT)
                if width < out_w:
                    part = jnp.where(lax.broadcasted_iota(jnp.int32, part.shape, 1) < width, part, 0.0)
                o_ref[:, c0:c1] = part.astype(o_ref.dtype)

    row = pl.BlockSpec((tm, d), lambda i: (i, 0))
    outs = pl.pallas_call(
        body, out_shape=(jax.ShapeDtypeStruct((t, d), BF16),) + tuple(jax.ShapeDtypeStruct((t, ow), dt_) for _, _, ow, dt_ in segs),
        grid=(t // tm,), in_specs=[row, _const_spec((1, d)), _resident_spec(w.shape)],
        out_specs=(row,) + tuple(pl.BlockSpec((tm, ow), lambda i: (i, 0)) for _, _, ow, _ in segs),
        name=name, compiler_params=_params("parallel"))(h, g, w)
    return outs[0], outs[1:]


def _mm_tn(a_list, b, *, tm=1408, tk=2048, side=None, name):
    t, n = b.shape
    tk = _tile(t, tk if len(a_list) == 1 else tk // 2)
    nk = t // tk
    widths = [a.shape[1] for a in a_list]
    tm = _tile(math.gcd(*widths), tm, LANES)
    assert all(w % tm == 0 for w in widths)
    starts = np.cumsum([0] + [w // tm for w in widths])
    nseg = len(a_list)

    def a_spec(s):
        lo, hi = int(starts[s]), int(starts[s + 1])

        def idx(i, k):
            active = jnp.logical_and(i >= lo, i < hi)
            return (jnp.where(active, k, 0), jnp.clip(i - lo, 0, hi - lo - 1))
        return pl.BlockSpec((tk, tm), idx)

    def body(*refs):
        a_refs, b_ref, o_ref, o16_ref, acc = refs[:nseg], refs[nseg], refs[nseg + 1], refs[nseg + 2], refs[nseg + 3]
        i, k = pl.program_id(0), pl.program_id(1)

        @pl.when(k == 0)
        def _():
            acc[...] = jnp.zeros_like(acc)

        bv = b_ref[...].astype(BF16)
        for s in range(nseg):
            lo, hi = int(starts[s]), int(starts[s + 1])

            @pl.when(jnp.logical_and(i >= lo, i < hi))
            def _(s=s):
                acc[...] += _dot(a_refs[s][...].astype(BF16), bv, TN)

        @pl.when(k == nk - 1)
        def _():
            o_ref[...] = acc[...]
            o16_ref[...] = acc[...].astype(BF16)

    rows = int(starts[-1]) * tm
    o_spec = pl.BlockSpec((tm, n), lambda i, k: (i, 0))
    (o32, o16), got = _hosted_call(
        body, out_shape=[jax.ShapeDtypeStruct((rows, n), F32), jax.ShapeDtypeStruct((rows, n), BF16)], grid=(int(starts[-1]), nk),
        in_specs=[a_spec(s) for s in range(nseg)] + [pl.BlockSpec((tk, n), lambda i, k: (k, 0))],
        out_specs=[o_spec, o_spec], scratch_shapes=[pltpu.VMEM((tm, n), F32)], operands=list(a_list) + [b], side=side, name=name)
    return o32, o16, got


def _mm_nn_rmsbwd(segs, weights, x, g, dres, *, tm=512, side=None, name):
    t, d = x.shape
    tm = _tile(t, tm)
    nseg, nw = len(segs), len(weights)

    def body(*refs):
        a_refs, w_refs = refs[:nseg], refs[nseg:nseg + nw]
        x_ref, g_ref, dres_ref, dx_ref, dg_ref = refs[nseg + nw:]

        @pl.when(pl.program_id(0) == 0)
        def _():
            dg_ref[...] = jnp.zeros_like(dg_ref)

        dn = None
        for s, (a, w_idx, row0) in enumerate(segs):
            part = _dot(a_refs[s][...].astype(BF16), w_refs[w_idx][row0:row0 + a.shape[1], :], NN)
            dn = part if dn is None else dn + part
        xv = x_ref[...]
        r = lax.rsqrt(jnp.mean(xv * xv, axis=-1, keepdims=True) + RMS_EPS)
        xhat = xv * r
        dyg = dn * g_ref[...]
        dx_ref[...] = dres_ref[...] + r * (dyg - xhat * jnp.mean(dyg * xhat, axis=-1, keepdims=True))
        dg_ref[...] += jnp.sum(dn * xhat, axis=0, keepdims=True)

    row = pl.BlockSpec((tm, d), lambda i: (i, 0))
    in_specs = [pl.BlockSpec((tm, a.shape[1]), lambda i: (i, 0)) for a, _, _ in segs]
    in_specs += [_resident_spec(w.shape) for w in weights] + [row, _const_spec((1, d)), row]
    (dx, dg), extra = _hosted_call(
        body, grid=(t // tm,), in_specs=in_specs, out_specs=[row, _const_spec((1, d))],
        out_shape=[jax.ShapeDtypeStruct((t, d), F32), jax.ShapeDtypeStruct((1, d), F32)], scratch_shapes=[],
        operands=[a for a, _, _ in segs] + list(weights) + [x, g, dres], side=side, name=name)
    return dx, dg, extra


def _col_chunks(width, chunk=4 * LANES):
    return [(c0, min(c0 + chunk, width)) for c0 in range(0, width, chunk)]


def _rms_fwd(x, g, *, side=None, name):
    t, d = x.shape
    tm = _tile(t, 512)

    def body(x_ref, g_ref, o_ref):
        xv = x_ref[...]
        r = lax.rsqrt(jnp.mean(xv * xv, axis=-1, keepdims=True) + RMS_EPS)
        o_ref[...] = (xv * r * g_ref[...]).astype(o_ref.dtype)

    row = pl.BlockSpec((tm, d), lambda i: (i, 0))
    (n,), got = _hosted_call(body, grid=(t // tm,), in_specs=[row, _const_spec((1, d))], out_specs=[row],
                             out_shape=[jax.ShapeDtypeStruct((t, d), BF16)], scratch_shapes=[], operands=[x, g], side=side, name=name)
    return n, got


def _ffn_up(n, wgt, wut, *, side=None, name):
    t, d = n.shape
    f = wgt.shape[0]
    tm, tn = _tile(t, 512), _tile(f, FFN_COL_TILE, LANES)

    def body(n_ref, wg_ref, wu_ref, g_ref, u_ref, h_ref):
        nv = n_ref[...]
        gv = _dot(nv, wg_ref[...], NT)
        uv = _dot(nv, wu_ref[...], NT)
        g_ref[...] = gv.astype(BF16)
        u_ref[...] = uv.astype(BF16)
        h_ref[...] = (gv * _sigmoid(gv) * uv).astype(BF16)

    w_spec = pl.BlockSpec((tn, d), lambda j, i: (j, 0))
    o_spec = pl.BlockSpec((tm, tn), lambda j, i: (i, j))
    shp = jax.ShapeDtypeStruct((t, f), BF16)
    return _hosted_call(body, grid=(f // tn, t // tm), in_specs=[pl.BlockSpec((tm, d), lambda j, i: (i, 0)), w_spec, w_spec],
                        out_specs=[o_spec, o_spec, o_spec], out_shape=[shp, shp, shp], scratch_shapes=[], operands=[n, wgt, wut],
                        side=side, name=name)


def _rms_residual(acc, h, gp, weight):
    r = lax.rsqrt(jnp.mean(acc * acc, axis=-1, keepdims=True) + RMS_EPS)
    return h + weight * (acc * r * gp)


def _ffn_down(hid, wd, h_in, gp, *, target=None, side=None, name):
    t, f = hid.shape
    d = wd.shape[1]
    tm = _tile(t, 256)
    row = pl.BlockSpec((tm, d), lambda i: (i, 0))
    shp = jax.ShapeDtypeStruct((t, d), F32)
    in_specs = [pl.BlockSpec((tm, f), lambda i: (i, 0)), _resident_spec((f, d)), row, _const_spec((1, d))]

    if target is None:
        def body(hid_ref, wd_ref, hin_ref, gp_ref, f_ref, hout_ref):
            acc = _dot(hid_ref[...], wd_ref[...], NN)
            f_ref[...] = acc
            hout_ref[...] = _rms_residual(acc, hin_ref[...], gp_ref[...], FFN_RESIDUAL_WEIGHT)

        return _hosted_call(body, grid=(t // tm,), in_specs=in_specs, out_specs=[row, row], out_shape=[shp, shp], scratch_shapes=[],
                            operands=[hid, wd, h_in, gp], side=side, name=name)

    def body_loss(hid_ref, wd_ref, hin_ref, gp_ref, tgt_ref, f_ref, dh_ref, loss_ref):
        @pl.when(pl.program_id(0) == 0)
        def _():
            loss_ref[...] = jnp.zeros_like(loss_ref)
        acc = _dot(hid_ref[...], wd_ref[...], NN)
        f_ref[...] = acc
        e = _rms_residual(acc, hin_ref[...], gp_ref[...], FFN_RESIDUAL_WEIGHT) - tgt_ref[...]
        dh_ref[...] = e * (1.0 / d)
        per_row = jnp.sum(e * e, axis=1, keepdims=True) * (1.0 / d)
        loss_ref[...] += 0.5 * jnp.sum(per_row, axis=0, keepdims=True)

    return _hosted_call(body_loss, grid=(t // tm,), in_specs=in_specs + [row], out_specs=[row, row, _const_spec((1, LANES))],
                        out_shape=[shp, shp, jax.ShapeDtypeStruct((1, LANES), F32)], scratch_shapes=[],
                        operands=[hid, wd, h_in, gp, target], side=side, name=name)


def _post_bwd(dh, f, gp, weight, *, name):
    t, d = f.shape
    tm = _tile(t, 512)

    def body(dh_ref, f_ref, gp_ref, df_ref, dgp_ref):
        @pl.when(pl.program_id(0) == 0)
        def _():
            dgp_ref[...] = jnp.zeros_like(dgp_ref)
        fv = f_ref[...]
        dy = weight * dh_ref[...]
        r = lax.rsqrt(jnp.mean(fv * fv, axis=-1, keepdims=True) + RMS_EPS)
        fhat = fv * r
        dyg = dy * gp_ref[...]
        df_ref[...] = (r * (dyg - fhat * jnp.mean(dyg * fhat, axis=-1, keepdims=True))).astype(BF16)
        dgp_ref[...] += jnp.sum(dy * fhat, axis=0, keepdims=True)

    row = pl.BlockSpec((tm, d), lambda i: (i, 0))
    return pl.pallas_call(body, out_shape=(jax.ShapeDtypeStruct((t, d), BF16), jax.ShapeDtypeStruct((1, d), F32)), grid=(t // tm,),
                          in_specs=[row, row, _const_spec((1, d))], out_specs=(row, _const_spec((1, d))),
                          name=name, compiler_params=_params("arbitrary"))(dh, f, gp)


def _ffn_dhid(df, wd, g, u, *, name):
    t, d = df.shape
    f = wd.shape[0]
    tm, tn = _tile(t, 512), _tile(f, FFN_COL_TILE, LANES)

    def body(df_ref, wd_ref, g_ref, u_ref, dg_ref, du_ref):
        dh = _dot(df_ref[...], wd_ref[...], NT)
        gv = g_ref[...].astype(F32)
        uv = u_ref[...].astype(F32)
        sg = _sigmoid(gv)
        silu = gv * sg
        dg_ref[...] = (dh * uv * (sg + silu * (1.0 - sg))).astype(BF16)
        du_ref[...] = (dh * silu).astype(BF16)

    o_spec = pl.BlockSpec((tm, tn), lambda j, i: (i, j))
    shp = jax.ShapeDtypeStruct((t, f), BF16)
    return pl.pallas_call(body, out_shape=(shp, shp), grid=(f // tn, t // tm),
                          in_specs=[pl.BlockSpec((tm, d), lambda j, i: (i, 0)), pl.BlockSpec((tn, d), lambda j, i: (j, 0)), o_spec, o_spec],
                          out_specs=(o_spec, o_spec), name=name, compiler_params=_params("parallel", "arbitrary"))(df, wd, g, u)


def _ffn_forward(h_in, g_pre, w_up, wd, g_post, tag, side_norm=None, side_up=None, target=None):
    n, got_norm = _rms_fwd(h_in, g_pre, side=side_norm, name=f"{tag}_prenorm")
    wgt, wut = w_up(got_norm) if callable(w_up) else w_up
    (g, u, hid), got_up = _ffn_up(n, wgt, wut, side=side_up, name=f"{tag}_up")
    wd = wd(got_up) if callable(wd) else wd
    outs, _ = _ffn_down(hid, wd, h_in, g_post, target=target, name=f"{tag}_down")
    return outs[1:], (h_in, n, g, u, hid, outs[0]), (wgt, wut, wd), got_up


def _stack8(g):
    return g.reshape(N_DEV, g.shape[0] // N_DEV, g.shape[1])


def _ffn_backward(dh_out, saved, g_pre, wgt, wut, wd, g_post, tag, chain=False):
    h_in, n, g, u, hid, f = saved

    def side_of(grad16):
        return _scatter_exchange([_stack8(grad16)]) if chain else None

    df, dg_post = _post_bwd(dh_out, f, g_post, FFN_RESIDUAL_WEIGHT, name=f"{tag}_post_bwd")
    dgate, dup = _ffn_dhid(df, wd, g, u, name=f"{tag}_dhid")
    d_wd, d_wd16, _ = _mm_tn([hid], df, name=f"{tag}_dwd")
    d_wgt, d_wgt16, got_wd = _mm_tn([dgate], n, side=side_of(d_wd16), name=f"{tag}_dwg")
    d_wut, d_wut16, got_wg = _mm_tn([dup], n, side=side_of(d_wgt16), name=f"{tag}_dwu")
    dh_in, dg_pre, got_wu = _mm_nn_rmsbwd([(dgate, 0, 0), (dup, 1, 0)], [wgt, wut], h_in, g_pre, dh_out, side=side_of(d_wut16),
                                          name=f"{tag}_dn")
    received = (got_wg[0], got_wu[0], got_wd[0]) if chain else None
    return dh_in, dg_pre, dg_post, (d_wgt, d_wut, d_wd), (d_wgt16, d_wut16, d_wd16), received


CONV_ROWS = 128
HALO = 8


def _taps(w_ref):
    return [w_ref[k:k + 1, :] for k in range(SSM_CONV)]


def _conv_chunk(x_ref, xs, r0, taps, bias):
    xs[HALO + r0:HALO + r0 + CONV_ROWS, :] = x_ref[r0:r0 + CONV_ROWS, :].astype(F32)
    shifted = [xs[HALO + r0 - k:HALO + r0 - k + CONV_ROWS, :] for k in range(SSM_CONV)]
    pre = bias + shifted[0] * taps[SSM_CONV - 1]
    for k in range(1, SSM_CONV):
        pre = pre + shifted[k] * taps[SSM_CONV - 1 - k]
    return shifted, pre


def _fold_rows(a):
    return functools.reduce(jnp.add, [a[i:i + 8] for i in range(0, a.shape[0], 8)])


def _conv_fwd(xbc, conv_w, conv_b, bl, *, name):
    t, c = xbc.shape
    s = t // bl
    tc = LANES
    assert s % CONV_ROWS == 0

    def body(x_ref, w_ref, b_ref, o_ref, xs):
        taps, bias = _taps(w_ref), b_ref[...]
        xs[0:HALO, :] = jnp.zeros((HALO, tc), F32)
        for r0 in range(0, s, CONV_ROWS):
            _, pre = _conv_chunk(x_ref, xs, r0, taps, bias)
            o_ref[r0:r0 + CONV_ROWS, :] = (pre * _sigmoid(pre)).astype(o_ref.dtype)

    blk = pl.BlockSpec((s, tc), lambda b, j: (b, j))
    return pl.pallas_call(body, out_shape=jax.ShapeDtypeStruct((t, c), BF16), grid=(bl, c // tc),
                          in_specs=[blk, pl.BlockSpec((SSM_CONV, tc), lambda b, j: (0, j)), pl.BlockSpec((1, tc), lambda b, j: (0, j))],
                          out_specs=blk, scratch_shapes=[pltpu.VMEM((HALO + s, tc), F32)],
                          name=name, compiler_params=_params("parallel", "arbitrary"))(xbc, conv_w, conv_b)


def _conv_bwd(dxc, xbc, conv_w, conv_b, bl, *, name):
    t, c = xbc.shape
    s = t // bl
    tc = LANES

    def body(dy_ref, x_ref, w_ref, b_ref, dx_ref, dw_ref, db_ref, xs, dpre_s):
        @pl.when(pl.program_id(1) == 0)
        def _():
            dw_ref[...] = jnp.zeros_like(dw_ref)
            db_ref[...] = jnp.zeros_like(db_ref)

        taps, bias = _taps(w_ref), b_ref[...]
        zero8 = jnp.zeros((HALO, tc), F32)
        xs[0:HALO, :] = zero8
        dpre_s[s:s + HALO, :] = zero8
        sums = [zero8] * (SSM_CONV + 1)
        for r0 in range(0, s, CONV_ROWS):
            shifted, pre = _conv_chunk(x_ref, xs, r0, taps, bias)
            sg = _sigmoid(pre)
            dpre = dy_ref[r0:r0 + CONV_ROWS, :].astype(F32) * (sg * (1.0 + pre * (1.0 - sg)))
            dpre_s[r0:r0 + CONV_ROWS, :] = dpre
            sums = [acc + _fold_rows(dpre * sh) for acc, sh in zip(sums[:-1], shifted)] + [sums[-1] + _fold_rows(dpre)]
        for k in range(SSM_CONV):
            dw_ref[SSM_CONV - 1 - k:SSM_CONV - k, :] += jnp.sum(sums[k], axis=0, keepdims=True)
        db_ref[...] += jnp.sum(sums[-1], axis=0, keepdims=True)
        for r0 in range(0, s, CONV_ROWS):
            dx = dpre_s[r0:r0 + CONV_ROWS, :] * taps[SSM_CONV - 1]
            for k in range(1, SSM_CONV):
                dx = dx + dpre_s[r0 + k:r0 + k + CONV_ROWS, :] * taps[SSM_CONV - 1 - k]
            dx_ref[r0:r0 + CONV_ROWS, :] = dx.astype(dx_ref.dtype)

    blk = pl.BlockSpec((s, tc), lambda j, b: (b, j))
    return pl.pallas_call(
        body, out_shape=(jax.ShapeDtypeStruct((t, c), BF16), jax.ShapeDtypeStruct((8, c), F32), jax.ShapeDtypeStruct((1, c), F32)),
        grid=(c // tc, bl),
        in_specs=[blk, blk, pl.BlockSpec((SSM_CONV, tc), lambda j, b: (0, j)), pl.BlockSpec((1, tc), lambda j, b: (0, j))],
        out_specs=(blk, pl.BlockSpec((8, tc), lambda j, b: (0, j)), pl.BlockSpec((1, tc), lambda j, b: (0, j))),
        scratch_shapes=[pltpu.VMEM((HALO + s, tc), F32), pltpu.VMEM((s + HALO, tc), F32)],
        name=name, compiler_params=_params("parallel", "arbitrary"))(dxc, xbc, conv_w, conv_b)


def _softplus(x):
    return jnp.maximum(x, 0.0) + jnp.log1p(jnp.exp(-jnp.abs(x)))


def _hilo_dot(v, m_b, dims=NN):
    hi = v.astype(BF16)
    lo = (v - hi.astype(F32)).astype(BF16)
    return _dot(hi, m_b, dims) + _dot(lo, m_b, dims)


def _ssd_chunk_common(dtraw_ref, dtb_ref, alog_ref, dsk_ref, d_inner):
    q, p = CHUNK, SSM_HEAD_DIM
    shift = p.bit_length() - 1
    assert 1 << shift == p
    dt = _softplus(dtraw_ref[...] + dtb_ref[...])
    a = -jnp.exp(alog_ref[...])
    ii = lax.broadcasted_iota(jnp.int32, (q, q), 0)
    jj = lax.broadcasted_iota(jnp.int32, (q, q), 1)
    causal = ii >= jj
    tril = jnp.where(causal, 1.0, 0.0).astype(F32)
    triu = jnp.where(ii <= jj, 1.0, 0.0).astype(F32)
    a_cs = _dot_hi(tril, dt * a)
    a_cs_t = a_cs.T
    a_last = a_cs[q - 1:q, :]
    e_col = jnp.exp(a_cs)
    dec_end = jnp.exp(a_last - a_cs)
    head_of_col = lax.shift_right_logical(lax.broadcasted_iota(jnp.int32, (LANES, d_inner), 1), shift)
    spread = (lax.broadcasted_iota(jnp.int32, (LANES, d_inner), 0) == head_of_col).astype(BF16)
    exact = jnp.concatenate([dt, jnp.broadcast_to(dsk_ref[...], (8, LANES))], axis=0)
    hi = jnp.concatenate([exact, e_col, dec_end], axis=0).astype(BF16)
    lo = (exact - hi[:q + 8].astype(F32)).astype(BF16)
    wide = _dot(hi, spread, NN)
    fine = wide[:q + 8] + _dot(lo, spread, NN)
    return dict(dt=dt, a=a, a_cs=a_cs, a_cs_t=a_cs_t, a_last=a_last, dec_end=dec_end, causal=causal, triu=triu,
                dt_e=fine[:q], dsk_e=fine[q:q + 1], e_e=wide[q + 8:2 * q + 8], dec_e=wide[2 * q + 8:3 * q + 8])


def _fill_block_diag(bd_ref, src_ref, hpg, col0=0):
    q, p = CHUNK, SSM_HEAD_DIM
    for hh in range(hpg):
        bd_ref[hh * q:(hh + 1) * q, hh * p:(hh + 1) * p] = src_ref[:, col0 + hh * p:col0 + (hh + 1) * p]


def _lane_onehot(h):
    return (lax.broadcasted_iota(jnp.int32, (1, LANES), 1) == h).astype(F32)


def _ssd_fwd(xc, dt_raw, dt_bias, a_log, d_skip, bl, n_heads, *, side=None, name):
    t = xc.shape[0]
    q, p, nst, grp = CHUNK, SSM_HEAD_DIM, SSM_STATE, SSM_GROUPS
    d_inner = n_heads * p
    hpg = n_heads // grp
    hb = min(hpg, SSD_HEAD_BATCH)
    gw = hpg * p
    nc = t // bl // q
    assert d_inner % (grp * nst) == 0 and nst == LANES and hpg % hb == 0

    def body(xs_ref, b_ref, c_ref, dtraw_ref, dtb_ref, alog_ref, dsk_ref, y_ref, hprev_ref, state, m_all, x_bd, xdt_s):
        @pl.when(jnp.logical_and(pl.program_id(0) == 0, pl.program_id(1) == 0))
        def _():
            x_bd[...] = jnp.zeros_like(x_bd)

        @pl.when(pl.program_id(1) == 0)
        def _():
            state[...] = jnp.zeros_like(state)

        cm = _ssd_chunk_common(dtraw_ref, dtb_ref, alog_ref, dsk_ref, d_inner)
        for g in range(grp):
            cols = slice(g * gw, (g + 1) * gw)
            bg = b_ref[:, g * nst:(g + 1) * nst]
            cg = c_ref[:, g * nst:(g + 1) * nst]
            scores = _dot(cg, bg, NT)
            xs = xs_ref[:, cols].astype(F32)
            xdt = xs * cm['dt_e'][:, cols]
            xdt_s[...] = xdt.astype(BF16)
            y_parts = []
            for sub in range(hpg // hb):
                for k in range(hb):
                    h = g * hpg + sub * hb + k
                    seg = cm['a_cs'][:, h:h + 1] - cm['a_cs_t'][h:h + 1, :]
                    m_all[:, k * q:(k + 1) * q] = (scores * jnp.exp(jnp.where(cm['causal'], seg, NEG_INF))).astype(BF16)
                _fill_block_diag(x_bd, xdt_s, hb, sub * hb * p)
                y_parts.append(_dot(m_all[...], x_bd[...], NN))
            hprev = state[g]
            hprev_ref[g] = hprev
            y = jnp.concatenate(y_parts, axis=1) + cm['e_e'][:, cols] * _dot(cg, hprev.astype(BF16), NT)
            y_ref[:, cols] = y + cm['dsk_e'][:, cols] * xs
            st = _dot((xdt * cm['dec_e'][:, cols]).astype(BF16), bg, TN)
            for hh in range(hpg):
                h = g * hpg + hh
                rows = slice(hh * p, (hh + 1) * p)
                state[g, rows, :] = jnp.exp(cm['a_last'][:, h:h + 1]) * hprev[rows] + st[rows]

    gn = grp * nst

    def rowmap(b, c):
        return b * nc + c
    vec = pl.BlockSpec((1, LANES), lambda b, c: (0, 0))
    return _hosted_call(
        body,
        out_shape=[jax.ShapeDtypeStruct((t, d_inner), F32), jax.ShapeDtypeStruct((t // q, grp, gw, nst), F32)],
        grid=(bl, nc),
        in_specs=[pl.BlockSpec((q, d_inner), lambda b, c: (rowmap(b, c), 0)),
                  pl.BlockSpec((q, gn), lambda b, c: (rowmap(b, c), d_inner // gn)),
                  pl.BlockSpec((q, gn), lambda b, c: (rowmap(b, c), d_inner // gn + 1)),
                  pl.BlockSpec((q, LANES), lambda b, c: (rowmap(b, c), 0)), vec, vec, vec],
        out_specs=[pl.BlockSpec((q, d_inner), lambda b, c: (rowmap(b, c), 0)),
                   pl.BlockSpec((None, grp, gw, nst), lambda b, c: (rowmap(b, c), 0, 0, 0))],
        scratch_shapes=[pltpu.VMEM((grp, gw, nst), F32), pltpu.VMEM((q, hb * q), BF16), pltpu.VMEM((hb * q, hb * p), BF16),
                        pltpu.VMEM((q, gw), BF16)],
        operands=[xc, xc, xc, dt_raw, dt_bias, a_log, d_skip], side=side, name=name)


def _ssd_bwd(dy, y, xc, dt_raw, hprev_all, dt_bias, a_log, d_skip, bl, n_heads, *, side=None, name):
    t, c_dim = xc.shape
    q, p, nst, grp = CHUNK, SSM_HEAD_DIM, SSM_STATE, SSM_GROUPS
    d_inner = n_heads * p
    hpg = n_heads // grp
    hb = min(hpg, SSD_HEAD_BATCH)
    gw = hpg * p
    nc = t // bl // q
    gn = grp * nst
    shift = p.bit_length() - 1

    def body(dy_ref, y_ref, xs_ref, b_ref, c_ref, dtraw_ref, hprev_ref, dtb_ref, alog_ref, dsk_ref,
             dxc_ref, ddtraw_ref, ddtb_ref, dalog_ref, ddsk_ref, dstate, mt_all, x_bd, dy_bd, xdt_s):
        @pl.when(jnp.logical_and(pl.program_id(0) == 0, pl.program_id(1) == 0))
        def _():
            ddtb_ref[...] = jnp.zeros_like(ddtb_ref)
            dalog_ref[...] = jnp.zeros_like(dalog_ref)
            ddsk_ref[...] = jnp.zeros_like(ddsk_ref)
            x_bd[...] = jnp.zeros_like(x_bd)
            dy_bd[...] = jnp.zeros_like(dy_bd)

        @pl.when(pl.program_id(1) == 0)
        def _():
            dstate[...] = jnp.zeros_like(dstate)

        cm = _ssd_chunk_common(dtraw_ref, dtb_ref, alog_ref, dsk_ref, d_inner)
        causal = cm['causal']
        upper = cm['triu'] > 0.5
        seg_row = lax.shift_right_logical(lax.broadcasted_iota(jnp.int32, (gw, LANES), 0), shift)
        seg_lane = lax.broadcasted_iota(jnp.int32, (gw, LANES), 1)
        sums = jnp.zeros((5 * q, LANES), F32)
        state_dot = jnp.zeros((1, LANES), F32)
        for g in range(grp):
            cols = slice(g * gw, (g + 1) * gw)
            seg_sum = (seg_row + g * hpg == seg_lane).astype(BF16)
            bg = b_ref[:, g * nst:(g + 1) * nst]
            cg = c_ref[:, g * nst:(g + 1) * nst]
            scores_t = _dot(bg, cg, NT)
            xs = xs_ref[:, cols].astype(F32)
            xdt = xs * cm['dt_e'][:, cols]
            xdt_s[...] = xdt.astype(BF16)
            dyf = dy_ref[:, cols].astype(F32)
            dscores = jnp.zeros((q, q), F32)
            dx_parts = []
            for sub in range(hpg // hb):
                c0 = sub * hb * p
                _fill_block_diag(x_bd, xdt_s, hb, c0)
                _fill_block_diag(dy_bd, dy_ref, hb, g * gw + c0)
                dm_all = _dot(dy_ref[:, g * gw + c0:g * gw + c0 + hb * p], x_bd[...], NT)
                for k in range(hb):
                    h = g * hpg + sub * hb + k
                    blk = slice(k * q, (k + 1) * q)
                    seg = cm['a_cs'][:, h:h + 1] - cm['a_cs_t'][h:h + 1, :]
                    mt_all[:, blk] = (scores_t * jnp.exp(jnp.where(upper, -seg, NEG_INF))).astype(BF16)
                    dscores = dscores + dm_all[:, blk] * jnp.exp(jnp.where(causal, seg, NEG_INF))
                dx_parts.append(_dot(mt_all[...], dy_bd[...], NN))
            hprev = hprev_ref[g]
            hprev_b = hprev.astype(BF16)
            dhn = dstate[g]
            dhn_b = dhn.astype(BF16)
            e_e, dec_e = cm['e_e'][:, cols], cm['dec_e'][:, cols]
            y_scan = y_ref[:, cols] - cm['dsk_e'][:, cols] * xs
            dye_b = (dyf * e_e).astype(BF16)
            dcg = _dot(dye_b, hprev_b, NN)
            dhp = _dot(dye_b, cg, TN)
            bdh = _dot(bg, dhn_b, NT)
            dbg = _dot((xdt * dec_e).astype(BF16), dhn_b, NN)
            dx_diag = jnp.concatenate(dx_parts, axis=1)
            dx = dec_e * bdh + dx_diag
            ds_b = dscores.astype(BF16)
            dcg = dcg + _dot(ds_b, bg, NN)
            dbg = dbg + _dot(ds_b, cg, TN)
            x_rounded = xdt_s[...].astype(F32)
            sums = sums + _hilo_dot(jnp.concatenate([dyf * y_scan, xdt * bdh, x_rounded * dx_diag, dx * xs, dyf * xs], axis=0), seg_sum)
            state_dot = state_dot + jnp.sum(_hilo_dot(dhn * hprev, seg_sum, TN), axis=0, keepdims=True)
            dxc_ref[:, cols] = (dx * cm['dt_e'][:, cols] + cm['dsk_e'][:, cols] * dyf).astype(dxc_ref.dtype)
            dxc_ref[:, d_inner + g * nst:d_inner + (g + 1) * nst] = dbg.astype(dxc_ref.dtype)
            dxc_ref[:, d_inner + gn + g * nst:d_inner + gn + (g + 1) * nst] = dcg.astype(dxc_ref.dtype)
            for hh in range(hpg):
                h = g * hpg + hh
                rows = slice(hh * p, (hh + 1) * p)
                dstate[g, rows, :] = jnp.exp(cm['a_last'][:, h:h + 1]) * dhn[rows] + dhp[rows]
        s_y, s_end, s_diag, s_dt, s_skip = (sums[k * q:(k + 1) * q] for k in range(5))
        dt, a, dec_end = cm['dt'], cm['a'], cm['dec_end']
        last_row = (lax.broadcasted_iota(jnp.int32, (q, 1), 0) == q - 1).astype(F32)
        da_last = jnp.sum(dec_end * s_end, axis=0, keepdims=True) + jnp.exp(cm['a_last']) * state_dot
        da = s_y - dec_end * s_end - s_diag + last_row * da_last
        ddta = _dot_hi(cm['triu'], da)
        ddt = s_dt + ddta * a
        d_a = jnp.sum(ddta * dt, axis=0, keepdims=True)
        ddt_raw = ddt * _sigmoid(dtraw_ref[...] + dtb_ref[...])
        ddtraw_ref[...] = ddt_raw
        ddtb_ref[...] += jnp.sum(ddt_raw, axis=0, keepdims=True)
        dalog_ref[...] += d_a * a
        ddsk_ref[...] += jnp.sum(s_skip, axis=0, keepdims=True)

    def rowmap(b, c):
        return b * nc + (nc - 1 - c)
    vec = pl.BlockSpec((1, LANES), lambda b, c: (0, 0))
    vec_shape = jax.ShapeDtypeStruct((1, LANES), F32)
    return _hosted_call(
        body,
        out_shape=[jax.ShapeDtypeStruct((t, c_dim), BF16), jax.ShapeDtypeStruct((t, LANES), F32), vec_shape, vec_shape, vec_shape],
        grid=(bl, nc),
        in_specs=[pl.BlockSpec((q, d_inner), lambda b, c: (rowmap(b, c), 0)),
                  pl.BlockSpec((q, d_inner), lambda b, c: (rowmap(b, c), 0)),
                  pl.BlockSpec((q, d_inner), lambda b, c: (rowmap(b, c), 0)),
                  pl.BlockSpec((q, gn), lambda b, c: (rowmap(b, c), d_inner // gn)),
                  pl.BlockSpec((q, gn), lambda b, c: (rowmap(b, c), d_inner // gn + 1)),
                  pl.BlockSpec((q, LANES), lambda b, c: (rowmap(b, c), 0)),
                  pl.BlockSpec((None, grp, gw, nst), lambda b, c: (rowmap(b, c), 0, 0, 0)), vec, vec, vec],
        out_specs=[pl.BlockSpec((q, c_dim), lambda b, c: (rowmap(b, c), 0)),
                   pl.BlockSpec((q, LANES), lambda b, c: (rowmap(b, c), 0)), vec, vec, vec],
        scratch_shapes=[pltpu.VMEM((grp, gw, nst), F32), pltpu.VMEM((q, hb * q), BF16),
                        pltpu.VMEM((hb * q, hb * p), BF16), pltpu.VMEM((hb * q, hb * p), BF16), pltpu.VMEM((q, gw), BF16)],
        operands=[dy, y, xc, xc, xc, dt_raw, hprev_all, dt_bias, a_log, d_skip], side=side, name=name)


def _gated_norm_fwd(y, z, ng, *, name):
    t, d = y.shape
    tm = _tile(t, 512)
    gw = d // SSM_GROUPS

    def body(y_ref, z_ref, ng_ref, o_ref):
        for g in range(SSM_GROUPS):
            sl = slice(g * gw, (g + 1) * gw)
            zv = z_ref[:, sl].astype(F32)
            yg = y_ref[:, sl] * (zv * _sigmoid(zv))
            r = lax.rsqrt(jnp.mean(yg * yg, axis=-1, keepdims=True) + RMS_EPS)
            o_ref[:, sl] = (yg * r * ng_ref[:, sl]).astype(o_ref.dtype)

    row = pl.BlockSpec((tm, d), lambda i: (i, 0))
    return pl.pallas_call(body, out_shape=jax.ShapeDtypeStruct((t, d), BF16), grid=(t // tm,),
                          in_specs=[row, row, _const_spec((1, d))], out_specs=row, name=name, compiler_params=_params("parallel"))(y, z, ng)


def _gated_norm_bwd(dyn, y, z, ng, *, name):
    t, d = y.shape
    tm = _tile(t, 512)
    gw = d // SSM_GROUPS

    def body(dyn_ref, y_ref, z_ref, ng_ref, dy_ref, dz_ref, dng_ref):
        @pl.when(pl.program_id(0) == 0)
        def _():
            dng_ref[...] = jnp.zeros_like(dng_ref)
        for g in range(SSM_GROUPS):
            sl = slice(g * gw, (g + 1) * gw)
            zv = z_ref[:, sl].astype(F32)
            yv = y_ref[:, sl]
            sg = _sigmoid(zv)
            sz = zv * sg
            yg = yv * sz
            r = lax.rsqrt(jnp.mean(yg * yg, axis=-1, keepdims=True) + RMS_EPS)
            yhat = yg * r
            dn = dyn_ref[:, sl].astype(F32)
            dyg_n = dn * ng_ref[:, sl]
            dyg = r * (dyg_n - yhat * jnp.mean(dyg_n * yhat, axis=-1, keepdims=True))
            dy_ref[:, sl] = (dyg * sz).astype(dy_ref.dtype)
            dz_ref[:, sl] = (dyg * yv * (sg * (1.0 + zv * (1.0 - sg)))).astype(dz_ref.dtype)
            dng_ref[:, sl] += jnp.sum(dn * yhat, axis=0, keepdims=True)

    row = pl.BlockSpec((tm, d), lambda i: (i, 0))
    shp = jax.ShapeDtypeStruct((t, d), BF16)
    return pl.pallas_call(body, out_shape=(shp, shp, jax.ShapeDtypeStruct((1, d), F32)), grid=(t // tm,),
                          in_specs=[row, row, row, _const_spec((1, d))], out_specs=(row, row, _const_spec((1, d))),
                          name=name, compiler_params=_params("arbitrary"))(dyn, y, z, ng)


def _bucket_onehot():
    blk = CHUNK
    qi = jnp.arange(blk)[:, None]
    kj = jnp.arange(2 * blk)[None, :]
    dist = jnp.maximum(qi + blk - kj, 0)
    max_exact = REL_BUCKETS // 2
    d = jnp.maximum(dist, 1).astype(F32)
    large = max_exact + (jnp.log(d / max_exact) / math.log(REL_MAX_DISTANCE / max_exact) * (REL_BUCKETS - max_exact)).astype(jnp.int32)
    large = jnp.minimum(large, REL_BUCKETS - 1)
    bucket = jnp.where(dist < max_exact, dist, large).reshape(-1)
    return (bucket[None, :] == jnp.arange(REL_BUCKETS)[:, None]).astype(F32)


def _small_mm_hi(a, b, dims, *, name):
    def body(a_ref, b_ref, o_ref):
        o_ref[...] = _dot_hi(a_ref[...], b_ref[...], dims)
    n = b.shape[0] if dims == NT else b.shape[1]
    return pl.pallas_call(body, out_shape=jax.ShapeDtypeStruct((a.shape[0], n), F32), name=name)(a, b)


def _attn_band_mask_t(n, rep):
    blk = CHUNK
    jj = lax.broadcasted_iota(jnp.int32, (2 * blk, rep * blk), 0)
    ii = lax.broadcasted_iota(jnp.int32, (2 * blk, rep * blk), 1) & (blk - 1)
    dist = ii + blk - jj
    in_window = jnp.logical_and(dist >= 0, dist < blk)
    return jnp.logical_and(in_window, jnp.logical_or(jj >= blk, n > 0))


def _sink_row(sink_ref, heads):
    return jnp.concatenate([jnp.broadcast_to(sink_ref[:, h:h + 1], (1, CHUNK)) for h in heads], axis=1)


def _attn_fwd(q, kv, bias_t, sinks, bl, *, name):
    t, qd = q.shape
    blk, hd = CHUNK, ATTN_HEAD_DIM
    kvd = ATTN_KV_HEADS * hd
    rep = ATTN_Q_HEADS // ATTN_KV_HEADS
    nb = t // bl // blk
    scale = hd ** -0.5

    def body(q_ref, kp_ref, kc_ref, vp_ref, vc_ref, bias_ref, sink_ref, o_ref, lse_ref):
        n = pl.program_id(1)
        mask = _attn_band_mask_t(n, rep)
        for kvh in range(ATTN_KV_HEADS):
            ks = slice(kvh * hd, (kvh + 1) * hd)
            heads = range(kvh * rep, (kvh + 1) * rep)
            qs = jnp.concatenate([q_ref[:, h * hd:(h + 1) * hd] for h in heads], axis=0)
            kk = jnp.concatenate([kp_ref[:, ks], kc_ref[:, ks]], axis=0)
            vv = jnp.concatenate([vp_ref[:, ks], vc_ref[:, ks]], axis=0)
            s = jnp.where(mask, _dot(kk, qs, NT) * scale + bias_ref[kvh], NEG_INF)
            sink = _sink_row(sink_ref, heads)
            m = jnp.maximum(jnp.max(s, axis=0, keepdims=True), sink)
            p = jnp.exp(s - m)
            den = jnp.sum(p, axis=0, keepdims=True) + jnp.exp(sink - m)
            o = _dot((p * (1.0 / den)).astype(BF16), vv, TN)
            lse = m + jnp.log(den)
            for r, h in enumerate(heads):
                o_ref[:, h * hd:(h + 1) * hd] = o[r * blk:(r + 1) * blk].astype(o_ref.dtype)
                lse_ref[h:h + 1, :] = lse[:, r * blk:(r + 1) * blk]

    def cur(b, n):
        return b * nb + n

    def prev(b, n):
        return b * nb + jnp.maximum(n - 1, 0)
    return pl.pallas_call(
        body, out_shape=(jax.ShapeDtypeStruct((t, qd), BF16), jax.ShapeDtypeStruct((t // blk * ATTN_Q_HEADS, blk), F32)), grid=(bl, nb),
        in_specs=[pl.BlockSpec((blk, qd), lambda b, n: (cur(b, n), 0)),
                  pl.BlockSpec((blk, kvd), lambda b, n: (prev(b, n), 0)), pl.BlockSpec((blk, kvd), lambda b, n: (cur(b, n), 0)),
                  pl.BlockSpec((blk, kvd), lambda b, n: (prev(b, n), 1)), pl.BlockSpec((blk, kvd), lambda b, n: (cur(b, n), 1)),
                  _const_spec(bias_t.shape), _const_spec((1, LANES))],
        out_specs=(pl.BlockSpec((blk, qd), lambda b, n: (cur(b, n), 0)),
                   pl.BlockSpec((ATTN_Q_HEADS, blk), lambda b, n: (cur(b, n), 0))),
        name=name, compiler_params=_params("parallel", "arbitrary"))(q, kv, kv, kv, kv, bias_t, sinks)


def _attn_bwd(do, q, kv, lse, bias_t, sinks, bl, *, name):
    t, qd = q.shape
    blk, hd = CHUNK, ATTN_HEAD_DIM
    kvd = ATTN_KV_HEADS * hd
    rep = ATTN_Q_HEADS // ATTN_KV_HEADS
    s_len = t // bl
    nb = s_len // blk
    scale = hd ** -0.5

    def body(do_ref, q_ref, kp_ref, kc_ref, vp_ref, vc_ref, lse_ref, bias_ref, sink_ref, dq_ref, dkv_ref, dbias_ref, dsink_ref):
        n = pl.program_id(1)

        @pl.when(jnp.logical_and(pl.program_id(0) == 0, n == 0))
        def _():
            dbias_ref[...] = jnp.zeros_like(dbias_ref)
            dsink_ref[...] = jnp.zeros_like(dsink_ref)

        mask = _attn_band_mask_t(n, rep)
        r_cur = pl.multiple_of(n * blk, blk)
        r_prev = pl.multiple_of(jnp.maximum(n - 1, 0) * blk, blk)
        dsink = jnp.zeros((1, LANES), F32)
        for kvh in range(ATTN_KV_HEADS):
            ks = slice(kvh * hd, (kvh + 1) * hd)
            heads = range(kvh * rep, (kvh + 1) * rep)
            qs = jnp.concatenate([q_ref[:, h * hd:(h + 1) * hd] for h in heads], axis=0)
            dos = jnp.concatenate([do_ref[:, h * hd:(h + 1) * hd] for h in heads], axis=0)
            kk = jnp.concatenate([kp_ref[:, ks], kc_ref[:, ks]], axis=0)
            vv = jnp.concatenate([vp_ref[:, ks], vc_ref[:, ks]], axis=0)
            lse = jnp.concatenate([lse_ref[h:h + 1, :] for h in heads], axis=1)
            p = jnp.exp(jnp.where(mask, _dot(kk, qs, NT) * scale + bias_ref[kvh], NEG_INF) - lse)
            dp = _dot(vv, dos, NT)
            delta = jnp.sum(p * dp, axis=0, keepdims=True)
            ds = p * (dp - delta)
            dsink_row = jnp.exp(_sink_row(sink_ref, heads) - lse) * delta
            dbias_ref[kvh] += ds
            ds_b = ds.astype(BF16)
            dq_s = _dot(ds_b, kk, TN) * scale
            dkk = _dot(ds_b, qs, NN) * scale
            dvv = _dot(p.astype(BF16), dos, NN)
            for r, h in enumerate(heads):
                dq_ref[:, h * hd:(h + 1) * hd] = dq_s[r * blk:(r + 1) * blk].astype(dq_ref.dtype)
                dsink = dsink - jnp.sum(dsink_row[:, r * blk:(r + 1) * blk], axis=1, keepdims=True) * _lane_onehot(h)
            vs = slice(kvd + kvh * hd, kvd + (kvh + 1) * hd)
            dkv_ref[pl.ds(r_cur, blk), ks] = dkk[blk:]
            dkv_ref[pl.ds(r_cur, blk), vs] = dvv[blk:]

            @pl.when(n > 0)
            def _():
                dkv_ref[pl.ds(r_prev, blk), ks] += dkk[:blk]
                dkv_ref[pl.ds(r_prev, blk), vs] += dvv[:blk]
        dsink_ref[...] += dsink

    def cur(b, n):
        return b * nb + n

    def prev(b, n):
        return b * nb + jnp.maximum(n - 1, 0)
    qspec = pl.BlockSpec((blk, qd), lambda b, n: (cur(b, n), 0))
    return pl.pallas_call(
        body,
        out_shape=(jax.ShapeDtypeStruct((t, qd), BF16), jax.ShapeDtypeStruct((t, 2 * kvd), F32),
                   jax.ShapeDtypeStruct(bias_t.shape, F32), jax.ShapeDtypeStruct((1, LANES), F32)),
        grid=(bl, nb),
        in_specs=[qspec, qspec,
                  pl.BlockSpec((blk, kvd), lambda b, n: (prev(b, n), 0)), pl.BlockSpec((blk, kvd), lambda b, n: (cur(b, n), 0)),
                  pl.BlockSpec((blk, kvd), lambda b, n: (prev(b, n), 1)), pl.BlockSpec((blk, kvd), lambda b, n: (cur(b, n), 1)),
                  pl.BlockSpec((ATTN_Q_HEADS, blk), lambda b, n: (cur(b, n), 0)), _const_spec(bias_t.shape), _const_spec((1, LANES))],
        out_specs=(qspec, pl.BlockSpec((s_len, 2 * kvd), lambda b, n: (b, 0)), _const_spec(bias_t.shape), _const_spec((1, LANES))),
        name=name, compiler_params=_params("arbitrary", "arbitrary"))(do, q, kv, kv, kv, kv, lse, bias_t, sinks)


def _merge_fwd(yn, o, gs, ga, w_ssm, w_attn, w_out, h_in, g_post, *, name):
    t, d = h_in.shape
    tm = _tile(t, 256)

    def body(yn_ref, o_ref, gs_ref, ga_ref, ws_ref, wa_ref, wo_ref, hin_ref, gp_ref, ys_ref, ya_ref, mg_ref, mix_ref, hout_ref):
        ys = _dot(yn_ref[...], ws_ref[...], NN)
        ya = _dot(o_ref[...], wa_ref[...], NN)
        merged = (_sigmoid(gs_ref[...].astype(F32)) * ys + _sigmoid(ga_ref[...].astype(F32)) * ya).astype(BF16)
        mix = _dot(merged, wo_ref[...], NN)
        ys_ref[...] = ys.astype(BF16)
        ya_ref[...] = ya.astype(BF16)
        mg_ref[...] = merged
        mix_ref[...] = mix
        hout_ref[...] = _rms_residual(mix, hin_ref[...], gp_ref[...], 1.0)

    def row(w):
        return pl.BlockSpec((tm, w), lambda i: (i, 0))
    bshape = jax.ShapeDtypeStruct((t, d), BF16)
    fshape = jax.ShapeDtypeStruct((t, d), F32)
    return pl.pallas_call(
        body, out_shape=(bshape, bshape, bshape, fshape, fshape), grid=(t // tm,),
        in_specs=[row(yn.shape[1]), row(o.shape[1]), row(d), row(d), _resident_spec(w_ssm.shape), _resident_spec(w_attn.shape),
                  _resident_spec(w_out.shape), row(d), _const_spec((1, d))],
        out_specs=(row(d),) * 5, name=name, compiler_params=_params("parallel"))(yn, o, gs, ga, w_ssm, w_attn, w_out, h_in, g_post)


def _merge_bwd(dh, mix, g_post, gs, ga, ys, ya, w_ssm, w_attn, w_out, *, name):
    t, d = mix.shape
    tm = _tile(t, 256)
    d_ssm, d_attn = w_ssm.shape[0], w_attn.shape[0]

    def body(dh_ref, mix_ref, gp_ref, gs_ref, ga_ref, ys_ref, ya_ref, ws_ref, wa_ref, wo_ref,
             dmix_ref, dys_ref, dya_ref, dgs_ref, dga_ref, dyn_ref, do_ref, dgp_ref):
        @pl.when(pl.program_id(0) == 0)
        def _():
            dgp_ref[...] = jnp.zeros_like(dgp_ref)
        mv = mix_ref[...]
        dy = dh_ref[...]
        r = lax.rsqrt(jnp.mean(mv * mv, axis=-1, keepdims=True) + RMS_EPS)
        mhat = mv * r
        dyg = dy * gp_ref[...]
        dmix = (r * (dyg - mhat * jnp.mean(dyg * mhat, axis=-1, keepdims=True))).astype(BF16)
        dgp_ref[...] += jnp.sum(dy * mhat, axis=0, keepdims=True)
        dmix_ref[...] = dmix
        dmerged = _dot(dmix, wo_ref[...], NT)
        sgs = _sigmoid(gs_ref[...].astype(F32))
        sga = _sigmoid(ga_ref[...].astype(F32))
        dys = (dmerged * sgs).astype(BF16)
        dya = (dmerged * sga).astype(BF16)
        dys_ref[...] = dys
        dya_ref[...] = dya
        dgs_ref[...] = (dmerged * ys_ref[...].astype(F32) * sgs * (1.0 - sgs)).astype(BF16)
        dga_ref[...] = (dmerged * ya_ref[...].astype(F32) * sga * (1.0 - sga)).astype(BF16)
        dyn_ref[...] = _dot(dys, ws_ref[...], NT).astype(BF16)
        do_ref[...] = _dot(dya, wa_ref[...], NT).astype(BF16)

    def row(w):
        return pl.BlockSpec((tm, w), lambda i: (i, 0))

    def bshape(w):
        return jax.ShapeDtypeStruct((t, w), BF16)
    return pl.pallas_call(
        body, out_shape=(bshape(d),) * 5 + (bshape(d_ssm), bshape(d_attn), jax.ShapeDtypeStruct((1, d), F32)), grid=(t // tm,),
        in_specs=[row(d), row(d), _const_spec((1, d)), row(d), row(d), row(d), row(d),
                  _resident_spec(w_ssm.shape), _resident_spec(w_attn.shape), _resident_spec(w_out.shape)],
        out_specs=(row(d),) * 5 + (row(d_ssm), row(d_attn), _const_spec((1, d))),
        name=name, compiler_params=_params("arbitrary"))(dh, mix, g_post, gs, ga, ys, ya, w_ssm, w_attn, w_out)


def _adamw(w, g, m, v, *, name):
    r, c = w.shape
    tm = _tile(r, 256)
    c1 = 1.0 - ADAM_B1 ** ADAM_STEP
    c2 = 1.0 - ADAM_B2 ** ADAM_STEP

    def body(w_ref, g_ref, m_ref, v_ref, d_ref, mo_ref, vo_ref):
        gv = g_ref[...]
        mn = ADAM_B1 * m_ref[...] + (1.0 - ADAM_B1) * gv
        vn = ADAM_B2 * v_ref[...] + (1.0 - ADAM_B2) * (gv * gv)
        mo_ref[...] = mn
        vo_ref[...] = vn
        d_ref[...] = -ADAM_LR * ((mn / c1) / (jnp.sqrt(vn / c2) + ADAM_EPS) + ADAM_WD * w_ref[...])

    blk = pl.BlockSpec((tm, c), lambda i: (i, 0))
    shp = jax.ShapeDtypeStruct((r, c), F32)
    return pl.pallas_call(body, out_shape=(shp, shp, shp), grid=(r // tm,), in_specs=[blk] * 4, out_specs=(blk,) * 3,
                          name=name, compiler_params=_params("parallel"))(w, g, m, v)


def _position():
    return lax.axis_index("x"), lax.axis_index("y"), lax.axis_index("c")


def _gather_exchange(shards):
    na = len(shards)

    def plan(ins, outs, sems):
        send_sems, recv_sems, local_sems = sems
        x, y, c = _position()
        me, sibling = (x, y, c), (x, y, 1 - c)
        chips = [(1 - x, y), (x, 1 - y), (1 - x, 1 - y)]

        def slot(a, pos):
            return outs[a].at[4 * pos[0] + 2 * pos[1] + pos[2]]

        def copy(a, k, block, to, src=None):
            return pltpu.make_async_remote_copy(
                src_ref=slot(a, block) if src is None else src, dst_ref=slot(a, block),
                send_sem=send_sems.at[a, k], recv_sem=recv_sems.at[a, k], device_id=to, device_id_type=MESH)

        mine = [pltpu.make_async_copy(ins[a], slot(a, me), local_sems.at[a]) for a in range(na)]
        first = []
        for a in range(na):
            first.append(copy(a, 0, me, sibling, src=ins[a]))
            first += [copy(a, 1 + j, me, (*chip, c), src=ins[a]) for j, chip in enumerate(chips)]
        return me, sibling, chips, copy, mine, first

    def start(ins, outs, sems):
        *_, mine, first = plan(ins, outs, sems)
        for cp in mine + first:
            cp.start()

    def finish(ins, outs, sems):
        me, sibling, chips, copy, mine, first = plan(ins, outs, sems)
        c = me[2]
        passed = []
        for a in range(na):
            for j, chip in enumerate(chips):
                copy(a, 1 + j, (*chip, c), me).wait_recv()
                fwd = copy(a, 4 + j, (*chip, c), sibling)
                fwd.start()
                passed.append(fwd)
        for a in range(na):
            copy(a, 0, sibling, me).wait_recv()
            for j, chip in enumerate(chips):
                copy(a, 4 + j, (*chip, 1 - c), me).wait_recv()
        for cp in first + passed:
            cp.wait_send()
        for cp in mine:
            cp.wait()

    return _Exchange(list(shards), [jax.ShapeDtypeStruct((N_DEV,) + s.shape, s.dtype) for s in shards],
                     [pltpu.SemaphoreType.DMA((na, 7)), pltpu.SemaphoreType.DMA((na, 7)), pltpu.SemaphoreType.DMA((na,))],
                     start, finish)


def _scatter_exchange(arrays):
    na = len(arrays)

    def copies(ins, outs, sems):
        send_sems, recv_sems = sems
        x, y, c = _position()
        out = []
        for a in range(na):
            for k in range(7):
                flip = k + 1
                peer = (x ^ (flip >> 2), y ^ ((flip >> 1) & 1), c ^ (flip & 1))
                peer_block = 4 * peer[0] + 2 * peer[1] + peer[2]
                out.append(pltpu.make_async_remote_copy(
                    src_ref=ins[a].at[peer_block], dst_ref=outs[a].at[k],
                    send_sem=send_sems.at[a, k], recv_sem=recv_sems.at[a, k], device_id=peer, device_id_type=MESH))
        return out

    def start(ins, outs, sems):
        for cp in copies(ins, outs, sems):
            cp.start()

    def finish(ins, outs, sems):
        for cp in copies(ins, outs, sems):
            cp.wait()

    return _Exchange(list(arrays), [jax.ShapeDtypeStruct((7,) + s.shape[1:], s.dtype) for s in arrays],
                     [pltpu.SemaphoreType.DMA((na, 7)), pltpu.SemaphoreType.DMA((na, 7))], start, finish)


def _reduce_blocks(own, recv, *, name):
    r, c = own.shape
    tm = _tile(r, 256)

    def body(own_ref, recv_ref, o_ref):
        acc = own_ref[...]
        for k in range(7):
            acc = acc + recv_ref[k].astype(F32)
        o_ref[...] = acc

    return pl.pallas_call(
        body, out_shape=jax.ShapeDtypeStruct((r, c), F32), grid=(r // tm,),
        in_specs=[pl.BlockSpec((tm, c), lambda i: (i, 0)), pl.BlockSpec((7, tm, c), lambda i: (0, i, 0))],
        out_specs=pl.BlockSpec((tm, c), lambda i: (i, 0)), name=name, compiler_params=_params("parallel"))(own, recv)


def _reduce_adamw(own, recv, w, m, v, *, name):
    r, c = own.shape
    tm = _tile(r, 256)
    c1 = 1.0 - ADAM_B1 ** ADAM_STEP
    c2 = 1.0 - ADAM_B2 ** ADAM_STEP

    def body(own_ref, recv_ref, w_ref, m_ref, v_ref, g_ref, d_ref, mo_ref, vo_ref):
        gv = own_ref[...]
        for k in range(7):
            gv = gv + recv_ref[k].astype(F32)
        g_ref[...] = gv
        mn = ADAM_B1 * m_ref[...] + (1.0 - ADAM_B1) * gv
        vn = ADAM_B2 * v_ref[...] + (1.0 - ADAM_B2) * (gv * gv)
        mo_ref[...] = mn
        vo_ref[...] = vn
        d_ref[...] = -ADAM_LR * ((mn / c1) / (jnp.sqrt(vn / c2) + ADAM_EPS) + ADAM_WD * w_ref[...])

    blk = pl.BlockSpec((tm, c), lambda i: (i, 0))
    shp = jax.ShapeDtypeStruct((r, c), F32)
    return pl.pallas_call(body, out_shape=(shp,) * 4, grid=(r // tm,),
                          in_specs=[blk, pl.BlockSpec((7, tm, c), lambda i: (0, i, 0)), blk, blk, blk], out_specs=(blk,) * 4,
                          name=name, compiler_params=_params("parallel"))(own, recv, w, m, v)


def _all_reduce_small(vec, *, name):
    r, c = vec.shape

    def body(v_ref, o_ref, buf, send_sems, recv_sems):
        x, y, c_ = _position()
        me = 4 * x + 2 * y + c_
        buf[me] = v_ref[...]
        copies = []
        for k in range(7):
            flip = k + 1
            peer = (x ^ (flip >> 2), y ^ ((flip >> 1) & 1), c_ ^ (flip & 1))
            cp = pltpu.make_async_remote_copy(
                src_ref=v_ref, dst_ref=buf.at[me], send_sem=send_sems.at[k], recv_sem=recv_sems.at[k],
                device_id=peer, device_id_type=MESH)
            cp.start()
            copies.append(cp)
        for cp in copies:
            cp.wait()
        acc = buf[0]
        for d in range(1, N_DEV):
            acc = acc + buf[d]
        o_ref[...] = acc

    vm = pl.BlockSpec(memory_space=pltpu.VMEM)
    return pl.pallas_call(
        body, out_shape=jax.ShapeDtypeStruct((r, c), F32), in_specs=[vm], out_specs=vm,
        scratch_shapes=[pltpu.VMEM((N_DEV, r, c), F32), pltpu.SemaphoreType.DMA((7,)), pltpu.SemaphoreType.DMA((7,))],
        name=name)(vec)


def _pad_lanes(v, width=LANES):
    return jnp.pad(v, ((0, 0), (0, width - v.shape[1])))


def kernel(x, ffn1_pre_g, ffn1_w_gate, ffn1_w_up, ffn1_w_down, ffn1_post_g, mix_pre_g, w_in, conv_w, conv_b, dt_bias, a_log, d_skip, ssm_norm_g, w_ssm_proj, attn_sinks, rel_bias_table, w_attn_proj, w_out, mix_post_g, ffn2_pre_g, ffn2_w_gate, ffn2_w_up, ffn2_w_down, ffn2_post_g, loss_target, m_ffn1_pre_g, m_ffn1_w_gate, m_ffn1_w_up, m_ffn1_w_down, m_ffn1_post_g, m_mix_pre_g, m_w_in, m_conv_w, m_conv_b, m_dt_bias, m_a_log, m_d_skip, m_ssm_norm_g, m_w_ssm_proj, m_attn_sinks, m_rel_bias_table, m_w_attn_proj, m_w_out, m_mix_post_g, m_ffn2_pre_g, m_ffn2_w_gate, m_ffn2_w_up, m_ffn2_w_down, m_ffn2_post_g, v_ffn1_pre_g, v_ffn1_w_gate, v_ffn1_w_up, v_ffn1_w_down, v_ffn1_post_g, v_mix_pre_g, v_w_in, v_conv_w, v_conv_b, v_dt_bias, v_a_log, v_d_skip, v_ssm_norm_g, v_w_ssm_proj, v_attn_sinks, v_rel_bias_table, v_w_attn_proj, v_w_out, v_mix_post_g, v_ffn2_pre_g, v_ffn2_w_gate, v_ffn2_w_up, v_ffn2_w_down, v_ffn2_post_g):
    args = dict(locals())
    weight_names = ['ffn1_pre_g', 'ffn1_w_gate', 'ffn1_w_up', 'ffn1_w_down', 'ffn1_post_g', 'mix_pre_g', 'w_in', 'conv_w', 'conv_b',
                    'dt_bias', 'a_log', 'd_skip', 'ssm_norm_g', 'w_ssm_proj', 'attn_sinks', 'rel_bias_table', 'w_attn_proj', 'w_out',
                    'mix_post_g', 'ffn2_pre_g', 'ffn2_w_gate', 'ffn2_w_up', 'ffn2_w_down', 'ffn2_post_g']
    col_sharded = ('ffn1_w_gate', 'ffn1_w_up', 'w_in', 'ffn2_w_gate', 'ffn2_w_up')
    row_sharded = ('ffn1_w_down', 'w_ssm_proj', 'w_attn_proj', 'w_out', 'ffn2_w_down')
    big = col_sharded + row_sharded

    bl, s_len, d = x.shape
    t = bl * s_len
    d_inner = ssm_norm_g.shape[1]
    n_heads = dt_bias.shape[1]
    gn = SSM_GROUPS * SSM_STATE
    conv_dim = d_inner + 2 * gn
    q_dim = ATTN_Q_HEADS * ATTN_HEAD_DIM
    kv_dim = ATTN_KV_HEADS * ATTN_HEAD_DIM

    def local_2d(name, a):
        a = a[0]
        return a.T if name in col_sharded else a

    ffn1_names = ('ffn1_w_gate', 'ffn1_w_up', 'ffn1_w_down')
    ffn2_names = ('ffn2_w_gate', 'ffn2_w_up', 'ffn2_w_down')
    mixer_names = ('w_ssm_proj', 'w_attn_proj', 'w_out')

    def shard(n):
        return local_2d(n, args[n]).astype(BF16)

    def rows(g):
        return g.reshape(N_DEV * g.shape[1], g.shape[2])

    x2 = x.reshape(t, d)
    tgt2 = loss_target.reshape(t, d)
    full = {}

    (h1,), saved1, ffn1_w, got_in = _ffn_forward(
        x2, ffn1_pre_g, lambda got: (rows(got[0]), rows(got[1])), lambda got: rows(got[2]), ffn1_post_g, "ffn1",
        side_norm=_gather_exchange([shard(n) for n in ffn1_names[:2]]),
        side_up=_gather_exchange([shard('w_in'), conv_w[0], shard('ffn1_w_down')]))
    full.update(zip(ffn1_names, ffn1_w))
    conv_w_full = jnp.transpose(got_in[1], (1, 0, 2)).reshape(SSM_CONV, conv_dim)

    win_t = rows(got_in[0])
    dt_lo = 2 * d + d_inner + conv_dim
    off = {'gs': 0, 'ga': d, 'z': 2 * d, 'xbc': 2 * d + d_inner, 'dt': dt_lo, 'q': dt_lo + n_heads, 'kv': dt_lo + n_heads + q_dim}
    assert all(o_ % 16 == 0 for o_ in off.values())

    u, (gs, ga, z, xbc, q, kv, dt_raw) = _proj_all(
        h1, mix_pre_g, win_t,
        [(off['gs'], d, BF16), (off['ga'], d, BF16), (off['z'], d_inner, BF16), (off['xbc'], conv_dim, BF16),
         (off['q'], q_dim, BF16), (off['kv'], 2 * kv_dim, BF16), (off['dt'], n_heads, F32)], name="mix_proj")

    dtb_p, alog_p, dsk_p, sinks_p = _pad_lanes(dt_bias), _pad_lanes(a_log), _pad_lanes(d_skip), _pad_lanes(attn_sinks)
    xc = _conv_fwd(xbc, conv_w_full, conv_b, bl, name="conv_fwd")
    late_names = mixer_names + ffn2_names
    (y, hprev), got_late = _ssd_fwd(xc, dt_raw, dtb_p, alog_p, dsk_p, bl, n_heads,
                                    side=_gather_exchange([shard(n) for n in late_names]), name="ssd_fwd")
    full.update({n: rows(g) for n, g in zip(late_names, got_late)})
    yn = _gated_norm_fwd(y, z, ssm_norm_g, name="gated_norm_fwd")

    onehot = _bucket_onehot()
    rep = ATTN_Q_HEADS // ATTN_KV_HEADS
    bias = _small_mm_hi(rel_bias_table.T, onehot, NN, name="rel_bias")
    bias_t = jnp.transpose(bias.reshape(ATTN_KV_HEADS, rep, CHUNK, 2 * CHUNK), (0, 3, 1, 2)).reshape(ATTN_KV_HEADS, 2 * CHUNK, rep * CHUNK)
    o, lse = _attn_fwd(q, kv, bias_t, sinks_p, bl, name="attn_fwd")

    ys, ya, merged, mix, h2 = _merge_fwd(yn, o, gs, ga, full['w_ssm_proj'], full['w_attn_proj'], full['w_out'], h1, mix_post_g,
                                         name="merge_fwd")

    (dh3, loss_vec), saved2, _, _ = _ffn_forward(h2, ffn2_pre_g, (full['ffn2_w_gate'], full['ffn2_w_up']), full['ffn2_w_down'],
                                                 ffn2_post_g, "ffn2", target=tgt2)
    loss = lax.psum(loss_vec[0, 0], ("x", "y", "c"))

    grads, own, wire, received = {}, {}, {}, {}
    dh2, grads['ffn2_pre_g'], grads['ffn2_post_g'], g32, g16, _ = _ffn_backward(
        dh3, saved2, ffn2_pre_g, full['ffn2_w_gate'], full['ffn2_w_up'], full['ffn2_w_down'], ffn2_post_g, "ffn2")
    own.update(zip(ffn2_names, map(_stack8, g32)))
    wire.update(zip(ffn2_names, map(_stack8, g16)))

    dmix, dys, dya, dgs, dga, dyn, do, grads['mix_post_g'] = _merge_bwd(
        dh2, mix, mix_post_g, gs, ga, ys, ya, full['w_ssm_proj'], full['w_attn_proj'], full['w_out'], name="merge_bwd")
    for n, (lhs, rhs) in zip(mixer_names, ((yn, dys), (o, dya), (merged, dmix))):
        g32_, g16_, _ = _mm_tn([lhs], rhs, name=f"d{n}")
        own[n], wire[n] = _stack8(g32_), _stack8(g16_)

    dq, dkv, dbias_t, dsinks = _attn_bwd(do, q, kv, lse, bias_t, sinks_p, bl, name="attn_bwd")
    dbias = jnp.transpose(dbias_t.reshape(ATTN_KV_HEADS, 2 * CHUNK, rep, CHUNK), (0, 2, 3, 1)).reshape(ATTN_Q_HEADS, -1)
    d_table = _small_mm_hi(onehot, dbias, NT, name="rel_bias_bwd")

    dy, dz, grads['ssm_norm_g'] = _gated_norm_bwd(dyn, y, z, ssm_norm_g, name="gated_norm_bwd")

    first_group = ffn2_names + mixer_names
    (dxc, ddt_raw, ddtb, dalog, ddsk), got = _ssd_bwd(dy, y, xc, dt_raw, hprev, dtb_p, alog_p, dsk_p, bl, n_heads,
                                                      side=_scatter_exchange([wire[n] for n in first_group]), name="ssd_bwd")
    received.update(zip(first_group, got))
    dxbc, dconv_w8, grads['conv_b'] = _conv_bwd(dxc, xbc, conv_w_full, conv_b, bl, name="conv_bwd")

    wide32, wide16, _ = _mm_tn([dgs, dga, dz, dxbc, dq], u, name="dw_in")
    kv32, kv16, _ = _mm_tn([dkv], u, name="dw_in_kv")
    dt32, dt16, _ = _mm_tn([ddt_raw], u, name="dw_in_dt")

    def original_order(wide, kv_part, dt_part):
        return jnp.concatenate([wide[:dt_lo], dt_part[:n_heads], wide[dt_lo:], kv_part], axis=0)
    me = 4 * lax.axis_index("x") + 2 * lax.axis_index("y") + lax.axis_index("c")
    blk_rows = win_t.shape[0] // N_DEV
    wire['w_in'] = _stack8(original_order(wide16, kv16, dt16))
    own_w_in = lax.dynamic_slice_in_dim(original_order(wide32, kv32, dt32), me * blk_rows, blk_rows)
    own['conv_w'] = jnp.transpose(dconv_w8[:SSM_CONV].reshape(SSM_CONV, N_DEV, conv_dim // N_DEV), (1, 0, 2))

    segs = [(g_, 0, off[k_]) for g_, k_ in zip([dgs, dga, dz, dxbc, dq, dkv, ddt_raw], ('gs', 'ga', 'z', 'xbc', 'q', 'kv', 'dt'))]
    dh1, grads['mix_pre_g'], got = _mm_nn_rmsbwd(segs, [win_t], h1, mix_pre_g, dh2,
                                                 side=_scatter_exchange([wire['w_in'], own['conv_w']]), name="mix_du")
    received.update(zip(('w_in', 'conv_w'), got))

    dx2, grads['ffn1_pre_g'], grads['ffn1_post_g'], g32, _, got = _ffn_backward(
        dh1, saved1, ffn1_pre_g, full['ffn1_w_gate'], full['ffn1_w_up'], full['ffn1_w_down'], ffn1_post_g, "ffn1", chain=True)
    own.update(zip(ffn1_names, map(_stack8, g32)))
    received.update(zip(ffn1_names, got))

    def own_block(a):
        return lax.dynamic_index_in_dim(a, me, 0, keepdims=False)
    out_g, out_d, out_m, out_v = {}, {}, {}, {}
    for n in big:
        w2, m2, v2 = local_2d(n, args[n]), local_2d(n, args['m_' + n]), local_2d(n, args['v_' + n])
        results = _reduce_adamw(own_w_in if n == 'w_in' else own_block(own[n]), received[n], w2, m2, v2, name=f"update_{n}")
        out_g[n], out_d[n], out_m[n], out_v[n] = ((a.T if n in col_sharded else a)[None] for a in results)
    conv_sum = _reduce_blocks(own_block(own['conv_w']), received['conv_w'], name="reduce_conv_w")

    grads['dt_bias'], grads['a_log'], grads['d_skip'] = ddtb[:, :n_heads], dalog[:, :n_heads], ddsk[:, :n_heads]
    grads['attn_sinks'] = dsinks[:, :ATTN_Q_HEADS]
    grads['rel_bias_table'] = d_table
    small = [n for n in weight_names if n not in big and n != 'conv_w']
    flat = jnp.concatenate([grads[n].reshape(-1) for n in small])
    n_small = flat.shape[0]
    n_rows = -(-n_small // (8 * LANES)) * 8
    flat = jnp.pad(flat, (0, n_rows * LANES - n_small)).reshape(n_rows, LANES)
    summed = _all_reduce_small(flat, name="allreduce_small").reshape(-1)
    pos = 0
    for n in small:
        size = grads[n].size
        grads[n] = summed[pos:pos + size].reshape(args[n].shape)
        pos += size

    def pack(prefix):
        vals = [(grads[n] if prefix == 'g' else args[prefix + n]).reshape(-1) for n in small]
        vals.append((conv_sum if prefix == 'g' else args[prefix + 'conv_w']).reshape(-1))
        flat_ = jnp.concatenate(vals)
        rows_ = -(-flat_.shape[0] // (8 * LANES)) * 8
        return jnp.pad(flat_, (0, rows_ * LANES - flat_.shape[0])).reshape(rows_, LANES)

    g_small = pack('g')
    d_small, m_small, v_small = _adamw(pack(''), g_small, pack('m_'), pack('v_'), name="adamw_small")
    pos = 0
    for n in small + ['conv_w']:
        shape = args[n].shape
        size = int(np.prod(shape))
        for dst, src in ((out_g, g_small), (out_d, d_small), (out_m, m_small), (out_v, v_small)):
            dst[n] = src.reshape(-1)[pos:pos + size].reshape(shape)
        pos += size

    grad_x = dx2.reshape(bl, s_len, d)
    return (loss, grad_x, *[out_g[n] for n in weight_names], *[out_d[n] for n in weight_names],
            *[out_m[n] for n in weight_names], *[out_v[n] for n in weight_names])
```

```python
import functools
import math

import numpy as np
import jax
import jax.numpy as jnp
from jax import lax
from jax.experimental import pallas as pl
from jax.experimental.pallas import tpu as pltpu

F32 = jnp.float32
BF16 = jnp.bfloat16
MESH = pl.DeviceIdType.MESH
N_DEV = 8

SSM_HEAD_DIM = 64
SSM_GROUPS = 4
SSM_STATE = 128
SSM_CONV = 4
CHUNK = 128
ATTN_HEAD_DIM = 64
ATTN_Q_HEADS = 16
ATTN_KV_HEADS = 4
REL_BUCKETS = 32
REL_MAX_DISTANCE = 128
RMS_EPS = 1e-6
FFN_RESIDUAL_WEIGHT = 0.5
ADAM_LR, ADAM_B1, ADAM_B2, ADAM_EPS, ADAM_WD, ADAM_STEP = 0.001, 0.9, 0.999, 1e-08, 0.01, 10

LANES = 128
VMEM_LIMIT_BYTES = 56 * 1024 * 1024
FFN_COL_TILE = 1408
SSD_HEAD_BATCH = 2

NEG_INF = float("-inf")


def _params(*sem):
    return pltpu.CompilerParams(dimension_semantics=sem, vmem_limit_bytes=VMEM_LIMIT_BYTES)


def _tile(n, pref, mult=8):
    if n <= pref:
        return n
    t = (pref // mult) * mult
    while t >= mult:
        if n % t == 0:
            return t
        t -= mult
    return n


def _sigmoid(x):
    return 1.0 / (1.0 + jnp.exp(-x))


def _dot(a, b, dims):
    return lax.dot_general(a, b, (dims, ((), ())), preferred_element_type=F32)


NN = ((1,), (0,))
NT = ((1,), (1,))
TN = ((0,), (0,))


def _dot_hi(a, b, dims=NN):
    return lax.dot_general(a, b, (dims, ((), ())), preferred_element_type=F32, precision=lax.Precision.HIGHEST)


def _const_spec(shape):
    nd = len(shape)
    return pl.BlockSpec(shape, lambda *_: (0,) * nd)


def _resident_spec(shape):
    nd = len(shape)
    return pl.BlockSpec(shape, lambda *_: (0,) * nd, pipeline_mode=pl.Buffered(1))


class _Exchange:
    def __init__(self, arrays, out_shape, scratch, start, finish):
        self.arrays, self.out_shape, self.scratch, self.start, self.finish = arrays, out_shape, scratch, start, finish


def _hosted_call(body, *, grid, in_specs, out_specs, out_shape, scratch_shapes, operands, side, name):
    in_specs, out_specs, out_shape, scratch_shapes = list(in_specs), list(out_specs), list(out_shape), list(scratch_shapes)
    sem = ("arbitrary",) * len(grid)
    if side is None:
        outs = pl.pallas_call(body, out_shape=tuple(out_shape), grid=grid, in_specs=in_specs, out_specs=tuple(out_specs),
                              scratch_shapes=scratch_shapes, name=name, compiler_params=_params(*sem))(*operands)
        return tuple(outs), ()
    n_in, n_out, n_scr = len(in_specs), len(out_shape), len(scratch_shapes)
    s_in, s_out = len(side.arrays), len(side.out_shape)

    def wrapped(*refs):
        refs = list(refs)
        main_in, side_in = refs[:n_in], refs[n_in:n_in + s_in]
        o0 = n_in + s_in
        main_out, side_out = refs[o0:o0 + n_out], refs[o0 + n_out:o0 + n_out + s_out]
        c0 = o0 + n_out + s_out
        main_scr, side_scr = refs[c0:c0 + n_scr], refs[c0 + n_scr:]
        ids = [pl.program_id(ax) for ax in range(len(grid))]
        first = functools.reduce(jnp.logical_and, [i == 0 for i in ids])
        last = functools.reduce(jnp.logical_and, [i == g - 1 for i, g in zip(ids, grid)])

        @pl.when(first)
        def _():
            side.start(side_in, side_out, side_scr)

        body(*main_in, *main_out, *main_scr)

        @pl.when(last)
        def _():
            side.finish(side_in, side_out, side_scr)

    hbm = pl.BlockSpec(memory_space=pl.ANY)
    outs = pl.pallas_call(
        wrapped, out_shape=tuple(out_shape + list(side.out_shape)), grid=grid,
        in_specs=in_specs + [hbm] * s_in, out_specs=tuple(out_specs + [hbm] * s_out),
        scratch_shapes=scratch_shapes + list(side.scratch), name=name, compiler_params=_params(*sem))(*operands, *side.arrays)
    return tuple(outs[:n_out]), tuple(outs[n_out:])


def _proj_all(h, g, w, segs, *, name):
    t, d = h.shape
    tm = _tile(t, 512)
    segs = [(row0, wd_, max(wd_, LANES), dt_) for row0, wd_, dt_ in segs]
    assert all(row0 + out_w <= w.shape[0] for row0, _, out_w, _ in segs)

    def body(h_ref, g_ref, w_ref, u_ref, *o_refs):
        hv = h_ref[...]
        r = lax.rsqrt(jnp.mean(hv * hv, axis=-1, keepdims=True) + RMS_EPS)
        uv = (hv * r * g_ref[...]).astype(BF16)
        u_ref[...] = uv
        for (row0, width, out_w, _), o_ref in zip(segs, o_refs):
            for c0, c1 in _col_chunks(out_w, 8 * LANES):
                part = _dot(uv, w_ref[row0 + c0:row0 + c1, :], NT)
                if width < out_w:
                    part = jnp.where(lax.broadcasted_iota(jnp.int32, part.shape, 1) < width, part, 0.0)
                o_ref[:, c0:c1] = part.astype(o_ref.dtype)

    row = pl.BlockSpec((tm, d), lambda i: (i, 0))
    outs = pl.pallas_call(
        body, out_shape=(jax.ShapeDtypeStruct((t, d), BF16),) + tuple(jax.ShapeDtypeStruct((t, ow), dt_) for _, _, ow, dt_ in segs),
        grid=(t // tm,), in_specs=[row, _const_spec((1, d)), _resident_spec(w.shape)],
        out_specs=(row,) + tuple(pl.BlockSpec((tm, ow), lambda i: (i, 0)) for _, _, ow, _ in segs),
        name=name, compiler_params=_params("parallel"))(h, g, w)
    return outs[0], outs[1:]


def _mm_tn(a_list, b, *, tm=1408, tk=2048, side=None, name):
    t, n = b.shape
    tk = _tile(t, tk if len(a_list) == 1 else tk // 2)
    nk = t // tk
    widths = [a.shape[1] for a in a_list]
    tm = _tile(math.gcd(*widths), tm, LANES)
    assert all(w % tm == 0 for w in widths)
    starts = np.cumsum([0] + [w // tm for w in widths])
    nseg = len(a_list)

    def a_spec(s):
        lo, hi = int(starts[s]), int(starts[s + 1])

        def idx(i, k):
            active = jnp.logical_and(i >= lo, i < hi)
            return (jnp.where(active, k, 0), jnp.clip(i - lo, 0, hi - lo - 1))
        return pl.BlockSpec((tk, tm), idx)

    def body(*refs):
        a_refs, b_ref, o_ref, o16_ref, acc = refs[:nseg], refs[nseg], refs[nseg + 1], refs[nseg + 2], refs[nseg + 3]
        i, k = pl.program_id(0), pl.program_id(1)

        @pl.when(k == 0)
        def _():
            acc[...] = jnp.zeros_like(acc)

        bv = b_ref[...].astype(BF16)
        for s in range(nseg):
            lo, hi = int(starts[s]), int(starts[s + 1])

            @pl.when(jnp.logical_and(i >= lo, i < hi))
            def _(s=s):
                acc[...] += _dot(a_refs[s][...].astype(BF16), bv, TN)

        @pl.when(k == nk - 1)
        def _():
            o_ref[...] = acc[...]
            o16_ref[...] = acc[...].astype(BF16)

    rows = int(starts[-1]) * tm
    o_spec = pl.BlockSpec((tm, n), lambda i, k: (i, 0))
    (o32, o16), got = _hosted_call(
        body, out_shape=[jax.ShapeDtypeStruct((rows, n), F32), jax.ShapeDtypeStruct((rows, n), BF16)], grid=(int(starts[-1]), nk),
        in_specs=[a_spec(s) for s in range(nseg)] + [pl.BlockSpec((tk, n), lambda i, k: (k, 0))],
        out_specs=[o_spec, o_spec], scratch_shapes=[pltpu.VMEM((tm, n), F32)], operands=list(a_list) + [b], side=side, name=name)
    return o32, o16, got


def _mm_nn_rmsbwd(segs, weights, x, g, dres, *, tm=512, side=None, name):
    t, d = x.shape
    tm = _tile(t, tm)
    nseg, nw = len(segs), len(weights)

    def body(*refs):
        a_refs, w_refs = refs[:nseg], refs[nseg:nseg + nw]
        x_ref, g_ref, dres_ref, dx_ref, dg_ref = refs[nseg + nw:]

        @pl.when(pl.program_id(0) == 0)
        def _():
            dg_ref[...] = jnp.zeros_like(dg_ref)

        dn = None
        for s, (a, w_idx, row0) in enumerate(segs):
            part = _dot(a_refs[s][...].astype(BF16), w_refs[w_idx][row0:row0 + a.shape[1], :], NN)
            dn = part if dn is None else dn + part
        xv = x_ref[...]
        r = lax.rsqrt(jnp.mean(xv * xv, axis=-1, keepdims=True) + RMS_EPS)
        xhat = xv * r
        dyg = dn * g_ref[...]
        dx_ref[...] = dres_ref[...] + r * (dyg - xhat * jnp.mean(dyg * xhat, axis=-1, keepdims=True))
        dg_ref[...] += jnp.sum(dn * xhat, axis=0, keepdims=True)

    row = pl.BlockSpec((tm, d), lambda i: (i, 0))
    in_specs = [pl.BlockSpec((tm, a.shape[1]), lambda i: (i, 0)) for a, _, _ in segs]
    in_specs += [_resident_spec(w.shape) for w in weights] + [row, _const_spec((1, d)), row]
    (dx, dg), extra = _hosted_call(
        body, grid=(t // tm,), in_specs=in_specs, out_specs=[row, _const_spec((1, d))],
        out_shape=[jax.ShapeDtypeStruct((t, d), F32), jax.ShapeDtypeStruct((1, d), F32)], scratch_shapes=[],
        operands=[a for a, _, _ in segs] + list(weights) + [x, g, dres], side=side, name=name)
    return dx, dg, extra


def _col_chunks(width, chunk=4 * LANES):
    return [(c0, min(c0 + chunk, width)) for c0 in range(0, width, chunk)]


def _rms_fwd(x, g, *, side=None, name):
    t, d = x.shape
    tm = _tile(t, 512)

    def body(x_ref, g_ref, o_ref):
        xv = x_ref[...]
        r = lax.rsqrt(jnp.mean(xv * xv, axis=-1, keepdims=True) + RMS_EPS)
        o_ref[...] = (xv * r * g_ref[...]).astype(o_ref.dtype)

    row = pl.BlockSpec((tm, d), lambda i: (i, 0))
    (n,), got = _hosted_call(body, grid=(t // tm,), in_specs=[row, _const_spec((1, d))], out_specs=[row],
                             out_shape=[jax.ShapeDtypeStruct((t, d), BF16)], scratch_shapes=[], operands=[x, g], side=side, name=name)
    return n, got


def _ffn_up(n, wgt, wut, *, side=None, name):
    t, d = n.shape
    f = wgt.shape[0]
    tm, tn = _tile(t, 512), _tile(f, FFN_COL_TILE, LANES)

    def body(n_ref, wg_ref, wu_ref, g_ref, u_ref, h_ref):
        nv = n_ref[...]
        gv = _dot(nv, wg_ref[...], NT)
        uv = _dot(nv, wu_ref[...], NT)
        g_ref[...] = gv.astype(BF16)
        u_ref[...] = uv.astype(BF16)
        h_ref[...] = (gv * _sigmoid(gv) * uv).astype(BF16)

    w_spec = pl.BlockSpec((tn, d), lambda j, i: (j, 0))
    o_spec = pl.BlockSpec((tm, tn), lambda j, i: (i, j))
    shp = jax.ShapeDtypeStruct((t, f), BF16)
    return _hosted_call(body, grid=(f // tn, t // tm), in_specs=[pl.BlockSpec((tm, d), lambda j, i: (i, 0)), w_spec, w_spec],
                        out_specs=[o_spec, o_spec, o_spec], out_shape=[shp, shp, shp], scratch_shapes=[], operands=[n, wgt, wut],
                        side=side, name=name)


def _rms_residual(acc, h, gp, weight):
    r = lax.rsqrt(jnp.mean(acc * acc, axis=-1, keepdims=True) + RMS_EPS)
    return h + weight * (acc * r * gp)


def _ffn_down(hid, wd, h_in, gp, *, target=None, side=None, name):
    t, f = hid.shape
    d = wd.shape[1]
    tm = _tile(t, 256)
    row = pl.BlockSpec((tm, d), lambda i: (i, 0))
    shp = jax.ShapeDtypeStruct((t, d), F32)
    in_specs = [pl.BlockSpec((tm, f), lambda i: (i, 0)), _resident_spec((f, d)), row, _const_spec((1, d))]

    if target is None:
        def body(hid_ref, wd_ref, hin_ref, gp_ref, f_ref, hout_ref):
            acc = _dot(hid_ref[...], wd_ref[...], NN)
            f_ref[...] = acc
            hout_ref[...] = _rms_residual(acc, hin_ref[...], gp_ref[...], FFN_RESIDUAL_WEIGHT)

        return _hosted_call(body, grid=(t // tm,), in_specs=in_specs, out_specs=[row, row], out_shape=[shp, shp], scratch_shapes=[],
                            operands=[hid, wd, h_in, gp], side=side, name=name)

    def body_loss(hid_ref, wd_ref, hin_ref, gp_ref, tgt_ref, f_ref, dh_ref, loss_ref):
        @pl.when(pl.program_id(0) == 0)
        def _():
            loss_ref[...] = jnp.zeros_like(loss_ref)
        acc = _dot(hid_ref[...], wd_ref[...], NN)
        f_ref[...] = acc
        e = _rms_residual(acc, hin_ref[...], gp_ref[...], FFN_RESIDUAL_WEIGHT) - tgt_ref[...]
        dh_ref[...] = e * (1.0 / d)
        per_row = jnp.sum(e * e, axis=1, keepdims=True) * (1.0 / d)
        loss_ref[...] += 0.5 * jnp.sum(per_row, axis=0, keepdims=True)

    return _hosted_call(body_loss, grid=(t // tm,), in_specs=in_specs + [row], out_specs=[row, row, _const_spec((1, LANES))],
                        out_shape=[shp, shp, jax.ShapeDtypeStruct((1, LANES), F32)], scratch_shapes=[],
                        operands=[hid, wd, h_in, gp, target], side=side, name=name)


def _post_bwd(dh, f, gp, weight, *, name):
    t, d = f.shape
    tm = _tile(t, 512)

    def body(dh_ref, f_ref, gp_ref, df_ref, dgp_ref):
        @pl.when(pl.program_id(0) == 0)
        def _():
            dgp_ref[...] = jnp.zeros_like(dgp_ref)
        fv = f_ref[...]
        dy = weight * dh_ref[...]
        r = lax.rsqrt(jnp.mean(fv * fv, axis=-1, keepdims=True) + RMS_EPS)
        fhat = fv * r
        dyg = dy * gp_ref[...]
        df_ref[...] = (r * (dyg - fhat * jnp.mean(dyg * fhat, axis=-1, keepdims=True))).astype(BF16)
        dgp_ref[...] += jnp.sum(dy * fhat, axis=0, keepdims=True)

    row = pl.BlockSpec((tm, d), lambda i: (i, 0))
    return pl.pallas_call(body, out_shape=(jax.ShapeDtypeStruct((t, d), BF16), jax.ShapeDtypeStruct((1, d), F32)), grid=(t // tm,),
                          in_specs=[row, row, _const_spec((1, d))], out_specs=(row, _const_spec((1, d))),
                          name=name, compiler_params=_params("arbitrary"))(dh, f, gp)


def _ffn_dhid(df, wd, g, u, *, name):
    t, d = df.shape
    f = wd.shape[0]
    tm, tn = _tile(t, 512), _tile(f, FFN_COL_TILE, LANES)

    def body(df_ref, wd_ref, g_ref, u_ref, dg_ref, du_ref):
        dh = _dot(df_ref[...], wd_ref[...], NT)
        gv = g_ref[...].astype(F32)
        uv = u_ref[...].astype(F32)
        sg = _sigmoid(gv)
        silu = gv * sg
        dg_ref[...] = (dh * uv * (sg + silu * (1.0 - sg))).astype(BF16)
        du_ref[...] = (dh * silu).astype(BF16)

    o_spec = pl.BlockSpec((tm, tn), lambda j, i: (i, j))
    shp = jax.ShapeDtypeStruct((t, f), BF16)
    return pl.pallas_call(body, out_shape=(shp, shp), grid=(f // tn, t // tm),
                          in_specs=[pl.BlockSpec((tm, d), lambda j, i: (i, 0)), pl.BlockSpec((tn, d), lambda j, i: (j, 0)), o_spec, o_spec],
                          out_specs=(o_spec, o_spec), name=name, compiler_params=_params("parallel", "arbitrary"))(df, wd, g, u)


def _ffn_forward(h_in, g_pre, w_up, wd, g_post, tag, side_norm=None, side_up=None, target=None):
    n, got_norm = _rms_fwd(h_in, g_pre, side=side_norm, name=f"{tag}_prenorm")
    wgt, wut = w_up(got_norm) if callable(w_up) else w_up
    (g, u, hid), got_up = _ffn_up(n, wgt, wut, side=side_up, name=f"{tag}_up")
    wd = wd(got_up) if callable(wd) else wd
    outs, _ = _ffn_down(hid, wd, h_in, g_post, target=target, name=f"{tag}_down")
    return outs[1:], (h_in, n, g, u, hid, outs[0]), (wgt, wut, wd), got_up


def _stack8(g):
    return g.reshape(N_DEV, g.shape[0] // N_DEV, g.shape[1])


def _ffn_backward(dh_out, saved, g_pre, wgt, wut, wd, g_post, tag, chain=False):
    h_in, n, g, u, hid, f = saved

    def side_of(grad16):
        return _scatter_exchange([_stack8(grad16)]) if chain else None

    df, dg_post = _post_bwd(dh_out, f, g_post, FFN_RESIDUAL_WEIGHT, name=f"{tag}_post_bwd")
    dgate, dup = _ffn_dhid(df, wd, g, u, name=f"{tag}_dhid")
    d_wd, d_wd16, _ = _mm_tn([hid], df, name=f"{tag}_dwd")
    d_wgt, d_wgt16, got_wd = _mm_tn([dgate], n, side=side_of(d_wd16), name=f"{tag}_dwg")
    d_wut, d_wut16, got_wg = _mm_tn([dup], n, side=side_of(d_wgt16), name=f"{tag}_dwu")
    dh_in, dg_pre, got_wu = _mm_nn_rmsbwd([(dgate, 0, 0), (dup, 1, 0)], [wgt, wut], h_in, g_pre, dh_out, side=side_of(d_wut16),
                                          name=f"{tag}_dn")
    received = (got_wg[0], got_wu[0], got_wd[0]) if chain else None
    return dh_in, dg_pre, dg_post, (d_wgt, d_wut, d_wd), (d_wgt16, d_wut16, d_wd16), received


CONV_ROWS = 128
HALO = 8


def _taps(w_ref):
    return [w_ref[k:k + 1, :] for k in range(SSM_CONV)]


def _conv_chunk(x_ref, xs, r0, taps, bias):
    xs[HALO + r0:HALO + r0 + CONV_ROWS, :] = x_ref[r0:r0 + CONV_ROWS, :].astype(F32)
    shifted = [xs[HALO + r0 - k:HALO + r0 - k + CONV_ROWS, :] for k in range(SSM_CONV)]
    pre = bias + shifted[0] * taps[SSM_CONV - 1]
    for k in range(1, SSM_CONV):
        pre = pre + shifted[k] * taps[SSM_CONV - 1 - k]
    return shifted, pre


def _fold_rows(a):
    return functools.reduce(jnp.add, [a[i:i + 8] for i in range(0, a.shape[0], 8)])


def _conv_fwd(xbc, conv_w, conv_b, bl, *, name):
    t, c = xbc.shape
    s = t // bl
    tc = LANES
    assert s % CONV_ROWS == 0

    def body(x_ref, w_ref, b_ref, o_ref, xs):
        taps, bias = _taps(w_ref), b_ref[...]
        xs[0:HALO, :] = jnp.zeros((HALO, tc), F32)
        for r0 in range(0, s, CONV_ROWS):
            _, pre = _conv_chunk(x_ref, xs, r0, taps, bias)
            o_ref[r0:r0 + CONV_ROWS, :] = (pre * _sigmoid(pre)).astype(o_ref.dtype)

    blk = pl.BlockSpec((s, tc), lambda b, j: (b, j))
    return pl.pallas_call(body, out_shape=jax.ShapeDtypeStruct((t, c), BF16), grid=(bl, c // tc),
                          in_specs=[blk, pl.BlockSpec((SSM_CONV, tc), lambda b, j: (0, j)), pl.BlockSpec((1, tc), lambda b, j: (0, j))],
                          out_specs=blk, scratch_shapes=[pltpu.VMEM((HALO + s, tc), F32)],
                          name=name, compiler_params=_params("parallel", "arbitrary"))(xbc, conv_w, conv_b)


def _conv_bwd(dxc, xbc, conv_w, conv_b, bl, *, name):
    t, c = xbc.shape
    s = t // bl
    tc = LANES

    def body(dy_ref, x_ref, w_ref, b_ref, dx_ref, dw_ref, db_ref, xs, dpre_s):
        @pl.when(pl.program_id(1) == 0)
        def _():
            dw_ref[...] = jnp.zeros_like(dw_ref)
            db_ref[...] = jnp.zeros_like(db_ref)

        taps, bias = _taps(w_ref), b_ref[...]
        zero8 = jnp.zeros((HALO, tc), F32)
        xs[0:HALO, :] = zero8
        dpre_s[s:s + HALO, :] = zero8
        sums = [zero8] * (SSM_CONV + 1)
        for r0 in range(0, s, CONV_ROWS):
            shifted, pre = _conv_chunk(x_ref, xs, r0, taps, bias)
            sg = _sigmoid(pre)
            dpre = dy_ref[r0:r0 + CONV_ROWS, :].astype(F32) * (sg * (1.0 + pre * (1.0 - sg)))
            dpre_s[r0:r0 + CONV_ROWS, :] = dpre
            sums = [acc + _fold_rows(dpre * sh) for acc, sh in zip(sums[:-1], shifted)] + [sums[-1] + _fold_rows(dpre)]
        for k in range(SSM_CONV):
            dw_ref[SSM_CONV - 1 - k:SSM_CONV - k, :] += jnp.sum(sums[k], axis=0, keepdims=True)
        db_ref[...] += jnp.sum(sums[-1], axis=0, keepdims=True)
        for r0 in range(0, s, CONV_ROWS):
            dx = dpre_s[r0:r0 + CONV_ROWS, :] * taps[SSM_CONV - 1]
            for k in range(1, SSM_CONV):
                dx = dx + dpre_s[r0 + k:r0 + k + CONV_ROWS, :] * taps[SSM_CONV - 1 - k]
            dx_ref[r0:r0 + CONV_ROWS, :] = dx.astype(dx_ref.dtype)

    blk = pl.BlockSpec((s, tc), lambda j, b: (b, j))
    return pl.pallas_call(
        body, out_shape=(jax.ShapeDtypeStruct((t, c), BF16), jax.ShapeDtypeStruct((8, c), F32), jax.ShapeDtypeStruct((1, c), F32)),
        grid=(c // tc, bl),
        in_specs=[blk, blk, pl.BlockSpec((SSM_CONV, tc), lambda j, b: (0, j)), pl.BlockSpec((1, tc), lambda j, b: (0, j))],
        out_specs=(blk, pl.BlockSpec((8, tc), lambda j, b: (0, j)), pl.BlockSpec((1, tc), lambda j, b: (0, j))),
        scratch_shapes=[pltpu.VMEM((HALO + s, tc), F32), pltpu.VMEM((s + HALO, tc), F32)],
        name=name, compiler_params=_params("parallel", "arbitrary"))(dxc, xbc, conv_w, conv_b)


def _softplus(x):
    return jnp.maximum(x, 0.0) + jnp.log1p(jnp.exp(-jnp.abs(x)))


def _hilo_dot(v, m_b, dims=NN):
    hi = v.astype(BF16)
    lo = (v - hi.astype(F32)).astype(BF16)
    return _dot(hi, m_b, dims) + _dot(lo, m_b, dims)


def _ssd_chunk_common(dtraw_ref, dtb_ref, alog_ref, dsk_ref, d_inner):
    q, p = CHUNK, SSM_HEAD_DIM
    shift = p.bit_length() - 1
    assert 1 << shift == p
    dt = _softplus(dtraw_ref[...] + dtb_ref[...])
    a = -jnp.exp(alog_ref[...])
    ii = lax.broadcasted_iota(jnp.int32, (q, q), 0)
    jj = lax.broadcasted_iota(jnp.int32, (q, q), 1)
    causal = ii >= jj
    tril = jnp.where(causal, 1.0, 0.0).astype(F32)
    triu = jnp.where(ii <= jj, 1.0, 0.0).astype(F32)
    a_cs = _dot_hi(tril, dt * a)
    a_cs_t = a_cs.T
    a_last = a_cs[q - 1:q, :]
    e_col = jnp.exp(a_cs)
    dec_end = jnp.exp(a_last - a_cs)
    head_of_col = lax.shift_right_logical(lax.broadcasted_iota(jnp.int32, (LANES, d_inner), 1), shift)
    spread = (lax.broadcasted_iota(jnp.int32, (LANES, d_inner), 0) == head_of_col).astype(BF16)
    exact = jnp.concatenate([dt, jnp.broadcast_to(dsk_ref[...], (8, LANES))], axis=0)
    hi = jnp.concatenate([exact, e_col, dec_end], axis=0).astype(BF16)
    lo = (exact - hi[:q + 8].astype(F32)).astype(BF16)
    wide = _dot(hi, spread, NN)
    fine = wide[:q + 8] + _dot(lo, spread, NN)
    return dict(dt=dt, a=a, a_cs=a_cs, a_cs_t=a_cs_t, a_last=a_last, dec_end=dec_end, causal=causal, triu=triu,
                dt_e=fine[:q], dsk_e=fine[q:q + 1], e_e=wide[q + 8:2 * q + 8], dec_e=wide[2 * q + 8:3 * q + 8])


def _fill_block_diag(bd_ref, src_ref, hpg, col0=0):
    q, p = CHUNK, SSM_HEAD_DIM
    for hh in range(hpg):
        bd_ref[hh * q:(hh + 1) * q, hh * p:(hh + 1) * p] = src_ref[:, col0 + hh * p:col0 + (hh + 1) * p]


def _lane_onehot(h):
    return (lax.broadcasted_iota(jnp.int32, (1, LANES), 1) == h).astype(F32)


def _ssd_fwd(xc, dt_raw, dt_bias, a_log, d_skip, bl, n_heads, *, side=None, name):
    t = xc.shape[0]
    q, p, nst, grp = CHUNK, SSM_HEAD_DIM, SSM_STATE, SSM_GROUPS
    d_inner = n_heads * p
    hpg = n_heads // grp
    hb = min(hpg, SSD_HEAD_BATCH)
    gw = hpg * p
    nc = t // bl // q
    assert d_inner % (grp * nst) == 0 and nst == LANES and hpg % hb == 0

    def body(xs_ref, b_ref, c_ref, dtraw_ref, dtb_ref, alog_ref, dsk_ref, y_ref, hprev_ref, state, m_all, x_bd, xdt_s):
        @pl.when(jnp.logical_and(pl.program_id(0) == 0, pl.program_id(1) == 0))
        def _():
            x_bd[...] = jnp.zeros_like(x_bd)

        @pl.when(pl.program_id(1) == 0)
        def _():
            state[...] = jnp.zeros_like(state)

        cm = _ssd_chunk_common(dtraw_ref, dtb_ref, alog_ref, dsk_ref, d_inner)
        for g in range(grp):
            cols = slice(g * gw, (g + 1) * gw)
            bg = b_ref[:, g * nst:(g + 1) * nst]
            cg = c_ref[:, g * nst:(g + 1) * nst]
            scores = _dot(cg, bg, NT)
            xs = xs_ref[:, cols].astype(F32)
            xdt = xs * cm['dt_e'][:, cols]
            xdt_s[...] = xdt.astype(BF16)
            y_parts = []
            for sub in range(hpg // hb):
                for k in range(hb):
                    h = g * hpg + sub * hb + k
                    seg = cm['a_cs'][:, h:h + 1] - cm['a_cs_t'][h:h + 1, :]
                    m_all[:, k * q:(k + 1) * q] = (scores * jnp.exp(jnp.where(cm['causal'], seg, NEG_INF))).astype(BF16)
                _fill_block_diag(x_bd, xdt_s, hb, sub * hb * p)
                y_parts.append(_dot(m_all[...], x_bd[...], NN))
            hprev = state[g]
            hprev_ref[g] = hprev
            y = jnp.concatenate(y_parts, axis=1) + cm['e_e'][:, cols] * _dot(cg, hprev.astype(BF16), NT)
            y_ref[:, cols] = y + cm['dsk_e'][:, cols] * xs
            st = _dot((xdt * cm['dec_e'][:, cols]).astype(BF16), bg, TN)
            for hh in range(hpg):
                h = g * hpg + hh
                rows = slice(hh * p, (hh + 1) * p)
                state[g, rows, :] = jnp.exp(cm['a_last'][:, h:h + 1]) * hprev[rows] + st[rows]

    gn = grp * nst

    def rowmap(b, c):
        return b * nc + c
    vec = pl.BlockSpec((1, LANES), lambda b, c: (0, 0))
    return _hosted_call(
        body,
        out_shape=[jax.ShapeDtypeStruct((t, d_inner), F32), jax.ShapeDtypeStruct((t // q, grp, gw, nst), F32)],
        grid=(bl, nc),
        in_specs=[pl.BlockSpec((q, d_inner), lambda b, c: (rowmap(b, c), 0)),
                  pl.BlockSpec((q, gn), lambda b, c: (rowmap(b, c), d_inner // gn)),
                  pl.BlockSpec((q, gn), lambda b, c: (rowmap(b, c), d_inner // gn + 1)),
                  pl.BlockSpec((q, LANES), lambda b, c: (rowmap(b, c), 0)), vec, vec, vec],
        out_specs=[pl.BlockSpec((q, d_inner), lambda b, c: (rowmap(b, c), 0)),
                   pl.BlockSpec((None, grp, gw, nst), lambda b, c: (rowmap(b, c), 0, 0, 0))],
        scratch_shapes=[pltpu.VMEM((grp, gw, nst), F32), pltpu.VMEM((q, hb * q), BF16), pltpu.VMEM((hb * q, hb * p), BF16),
                        pltpu.VMEM((q, gw), BF16)],
        operands=[xc, xc, xc, dt_raw, dt_bias, a_log, d_skip], side=side, name=name)


def _ssd_bwd(dy, y, xc, dt_raw, hprev_all, dt_bias, a_log, d_skip, bl, n_heads, *, side=None, name):
    t, c_dim = xc.shape
    q, p, nst, grp = CHUNK, SSM_HEAD_DIM, SSM_STATE, SSM_GROUPS
    d_inner = n_heads * p
    hpg = n_heads // grp
    hb = min(hpg, SSD_HEAD_BATCH)
    gw = hpg * p
    nc = t // bl // q
    gn = grp * nst
    shift = p.bit_length() - 1

    def body(dy_ref, y_ref, xs_ref, b_ref, c_ref, dtraw_ref, hprev_ref, dtb_ref, alog_ref, dsk_ref,
             dxc_ref, ddtraw_ref, ddtb_ref, dalog_ref, ddsk_ref, dstate, mt_all, x_bd, dy_bd, xdt_s):
        @pl.when(jnp.logical_and(pl.program_id(0) == 0, pl.program_id(1) == 0))
        def _():
            ddtb_ref[...] = jnp.zeros_like(ddtb_ref)
            dalog_ref[...] = jnp.zeros_like(dalog_ref)
            ddsk_ref[...] = jnp.zeros_like(ddsk_ref)
            x_bd[...] = jnp.zeros_like(x_bd)
            dy_bd[...] = jnp.zeros_like(dy_bd)

        @pl.when(pl.program_id(1) == 0)
        def _():
            dstate[...] = jnp.zeros_like(dstate)

        cm = _ssd_chunk_common(dtraw_ref, dtb_ref, alog_ref, dsk_ref, d_inner)
        causal = cm['causal']
        upper = cm['triu'] > 0.5
        seg_row = lax.shift_right_logical(lax.broadcasted_iota(jnp.int32, (gw, LANES), 0), shift)
        seg_lane = lax.broadcasted_iota(jnp.int32, (gw, LANES), 1)
        sums = jnp.zeros((5 * q, LANES), F32)
        state_dot = jnp.zeros((1, LANES), F32)
        for g in range(grp):
            cols = slice(g * gw, (g + 1) * gw)
            seg_sum = (seg_row + g * hpg == seg_lane).astype(BF16)
            bg = b_ref[:, g * nst:(g + 1) * nst]
            cg = c_ref[:, g * nst:(g + 1) * nst]
            scores_t = _dot(bg, cg, NT)
            xs = xs_ref[:, cols].astype(F32)
            xdt = xs * cm['dt_e'][:, cols]
            xdt_s[...] = xdt.astype(BF16)
            dyf = dy_ref[:, cols].astype(F32)
            dscores = jnp.zeros((q, q), F32)
            dx_parts = []
            for sub in range(hpg // hb):
                c0 = sub * hb * p
                _fill_block_diag(x_bd, xdt_s, hb, c0)
                _fill_block_diag(dy_bd, dy_ref, hb, g * gw + c0)
                dm_all = _dot(dy_ref[:, g * gw + c0:g * gw + c0 + hb * p], x_bd[...], NT)
                for k in range(hb):
                    h = g * hpg + sub * hb + k
                    blk = slice(k * q, (k + 1) * q)
                    seg = cm['a_cs'][:, h:h + 1] - cm['a_cs_t'][h:h + 1, :]
                    mt_all[:, blk] = (scores_t * jnp.exp(jnp.where(upper, -seg, NEG_INF))).astype(BF16)
                    dscores = dscores + dm_all[:, blk] * jnp.exp(jnp.where(causal, seg, NEG_INF))
                dx_parts.append(_dot(mt_all[...], dy_bd[...], NN))
            hprev = hprev_ref[g]
            hprev_b = hprev.astype(BF16)
            dhn = dstate[g]
            dhn_b = dhn.astype(BF16)
            e_e, dec_e = cm['e_e'][:, cols], cm['dec_e'][:, cols]
            y_scan = y_ref[:, cols] - cm['dsk_e'][:, cols] * xs
            dye_b = (dyf * e_e).astype(BF16)
            dcg = _dot(dye_b, hprev_b, NN)
            dhp = _dot(dye_b, cg, TN)
            bdh = _dot(bg, dhn_b, NT)
            dbg = _dot((xdt * dec_e).astype(BF16), dhn_b, NN)
            dx_diag = jnp.concatenate(dx_parts, axis=1)
            dx = dec_e * bdh + dx_diag
            ds_b = dscores.astype(BF16)
            dcg = dcg + _dot(ds_b, bg, NN)
            dbg = dbg + _dot(ds_b, cg, TN)
            x_rounded = xdt_s[...].astype(F32)
            sums = sums + _hilo_dot(jnp.concatenate([dyf * y_scan, xdt * bdh, x_rounded * dx_diag, dx * xs, dyf * xs], axis=0), seg_sum)
            state_dot = state_dot + jnp.sum(_hilo_dot(dhn * hprev, seg_sum, TN), axis=0, keepdims=True)
            dxc_ref[:, cols] = (dx * cm['dt_e'][:, cols] + cm['dsk_e'][:, cols] * dyf).astype(dxc_ref.dtype)
            dxc_ref[:, d_inner + g * nst:d_inner + (g + 1) * nst] = dbg.astype(dxc_ref.dtype)
            dxc_ref[:, d_inner + gn + g * nst:d_inner + gn + (g + 1) * nst] = dcg.astype(dxc_ref.dtype)
            for hh in range(hpg):
                h = g * hpg + hh
                rows = slice(hh * p, (hh + 1) * p)
                dstate[g, rows, :] = jnp.exp(cm['a_last'][:, h:h + 1]) * dhn[rows] + dhp[rows]
        s_y, s_end, s_diag, s_dt, s_skip = (sums[k * q:(k + 1) * q] for k in range(5))
        dt, a, dec_end = cm['dt'], cm['a'], cm['dec_end']
        last_row = (lax.broadcasted_iota(jnp.int32, (q, 1), 0) == q - 1).astype(F32)
        da_last = jnp.sum(dec_end * s_end, axis=0, keepdims=True) + jnp.exp(cm['a_last']) * state_dot
        da = s_y - dec_end * s_end - s_diag + last_row * da_last
        ddta = _dot_hi(cm['triu'], da)
        ddt = s_dt + ddta * a
        d_a = jnp.sum(ddta * dt, axis=0, keepdims=True)
        ddt_raw = ddt * _sigmoid(dtraw_ref[...] + dtb_ref[...])
        ddtraw_ref[...] = ddt_raw
        ddtb_ref[...] += jnp.sum(ddt_raw, axis=0, keepdims=True)
        dalog_ref[...] += d_a * a
        ddsk_ref[...] += jnp.sum(s_skip, axis=0, keepdims=True)

    def rowmap(b, c):
        return b * nc + (nc - 1 - c)
    vec = pl.BlockSpec((1, LANES), lambda b, c: (0, 0))
    vec_shape = jax.ShapeDtypeStruct((1, LANES), F32)
    return _hosted_call(
        body,
        out_shape=[jax.ShapeDtypeStruct((t, c_dim), BF16), jax.ShapeDtypeStruct((t, LANES), F32), vec_shape, vec_shape, vec_shape],
        grid=(bl, nc),
        in_specs=[pl.BlockSpec((q, d_inner), lambda b, c: (rowmap(b, c), 0)),
                  pl.BlockSpec((q, d_inner), lambda b, c: (rowmap(b, c), 0)),
                  pl.BlockSpec((q, d_inner), lambda b, c: (rowmap(b, c), 0)),
                  pl.BlockSpec((q, gn), lambda b, c: (rowmap(b, c), d_inner // gn)),
                  pl.BlockSpec((q, gn), lambda b, c: (rowmap(b, c), d_inner // gn + 1)),
                  pl.BlockSpec((q, LANES), lambda b, c: (rowmap(b, c), 0)),
                  pl.BlockSpec((None, grp, gw, nst), lambda b, c: (rowmap(b, c), 0, 0, 0)), vec, vec, vec],
        out_specs=[pl.BlockSpec((q, c_dim), lambda b, c: (rowmap(b, c), 0)),
                   pl.BlockSpec((q, LANES), lambda b, c: (rowmap(b, c), 0)), vec, vec, vec],
        scratch_shapes=[pltpu.VMEM((grp, gw, nst), F32), pltpu.VMEM((q, hb * q), BF16),
                        pltpu.VMEM((hb * q, hb * p), BF16), pltpu.VMEM((hb * q, hb * p), BF16), pltpu.VMEM((q, gw), BF16)],
        operands=[dy, y, xc, xc, xc, dt_raw, hprev_all, dt_bias, a_log, d_skip], side=side, name=name)


def _gated_norm_fwd(y, z, ng, *, name):
    t, d = y.shape
    tm = _tile(t, 512)
    gw = d // SSM_GROUPS

    def body(y_ref, z_ref, ng_ref, o_ref):
        for g in range(SSM_GROUPS):
            sl = slice(g * gw, (g + 1) * gw)
            zv = z_ref[:, sl].astype(F32)
            yg = y_ref[:, sl] * (zv * _sigmoid(zv))
            r = lax.rsqrt(jnp.mean(yg * yg, axis=-1, keepdims=True) + RMS_EPS)
            o_ref[:, sl] = (yg * r * ng_ref[:, sl]).astype(o_ref.dtype)

    row = pl.BlockSpec((tm, d), lambda i: (i, 0))
    return pl.pallas_call(body, out_shape=jax.ShapeDtypeStruct((t, d), BF16), grid=(t // tm,),
                          in_specs=[row, row, _const_spec((1, d))], out_specs=row, name=name, compiler_params=_params("parallel"))(y, z, ng)


def _gated_norm_bwd(dyn, y, z, ng, *, name):
    t, d = y.shape
    tm = _tile(t, 512)
    gw = d // SSM_GROUPS

    def body(dyn_ref, y_ref, z_ref, ng_ref, dy_ref, dz_ref, dng_ref):
        @pl.when(pl.program_id(0) == 0)
        def _():
            dng_ref[...] = jnp.zeros_like(dng_ref)
        for g in range(SSM_GROUPS):
            sl = slice(g * gw, (g + 1) * gw)
            zv = z_ref[:, sl].astype(F32)
            yv = y_ref[:, sl]
            sg = _sigmoid(zv)
            sz = zv * sg
            yg = yv * sz
            r = lax.rsqrt(jnp.mean(yg * yg, axis=-1, keepdims=True) + RMS_EPS)
            yhat = yg * r
            dn = dyn_ref[:, sl].astype(F32)
            dyg_n = dn * ng_ref[:, sl]
            dyg = r * (dyg_n - yhat * jnp.mean(dyg_n * yhat, axis=-1, keepdims=True))
            dy_ref[:, sl] = (dyg * sz).astype(dy_ref.dtype)
            dz_ref[:, sl] = (dyg * yv * (sg * (1.0 + zv * (1.0 - sg)))).astype(dz_ref.dtype)
            dng_ref[:, sl] += jnp.sum(dn * yhat, axis=0, keepdims=True)

    row = pl.BlockSpec((tm, d), lambda i: (i, 0))
    shp = jax.ShapeDtypeStruct((t, d), BF16)
    return pl.pallas_call(body, out_shape=(shp, shp, jax.ShapeDtypeStruct((1, d), F32)), grid=(t // tm,),
                          in_specs=[row, row, row, _const_spec((1, d))], out_specs=(row, row, _const_spec((1, d))),
                          name=name, compiler_params=_params("arbitrary"))(dyn, y, z, ng)


def _bucket_onehot():
    blk = CHUNK
    qi = jnp.arange(blk)[:, None]
    kj = jnp.arange(2 * blk)[None, :]
    dist = jnp.maximum(qi + blk - kj, 0)
    max_exact = REL_BUCKETS // 2
    d = jnp.maximum(dist, 1).astype(F32)
    large = max_exact + (jnp.log(d / max_exact) / math.log(REL_MAX_DISTANCE / max_exact) * (REL_BUCKETS - max_exact)).astype(jnp.int32)
    large = jnp.minimum(large, REL_BUCKETS - 1)
    bucket = jnp.where(dist < max_exact, dist, large).reshape(-1)
    return (bucket[None, :] == jnp.arange(REL_BUCKETS)[:, None]).astype(F32)


def _small_mm_hi(a, b, dims, *, name):
    def body(a_ref, b_ref, o_ref):
        o_ref[...] = _dot_hi(a_ref[...], b_ref[...], dims)
    n = b.shape[0] if dims == NT else b.shape[1]
    return pl.pallas_call(body, out_shape=jax.ShapeDtypeStruct((a.shape[0], n), F32), name=name)(a, b)


def _attn_band_mask_t(n, rep):
    blk = CHUNK
    jj = lax.broadcasted_iota(jnp.int32, (2 * blk, rep * blk), 0)
    ii = lax.broadcasted_iota(jnp.int32, (2 * blk, rep * blk), 1) & (blk - 1)
    dist = ii + blk - jj
    in_window = jnp.logical_and(dist >= 0, dist < blk)
    return jnp.logical_and(in_window, jnp.logical_or(jj >= blk, n > 0))


def _sink_row(sink_ref, heads):
    return jnp.concatenate([jnp.broadcast_to(sink_ref[:, h:h + 1], (1, CHUNK)) for h in heads], axis=1)


def _attn_fwd(q, kv, bias_t, sinks, bl, *, name):
    t, qd = q.shape
    blk, hd = CHUNK, ATTN_HEAD_DIM
    kvd = ATTN_KV_HEADS * hd
    rep = ATTN_Q_HEADS // ATTN_KV_HEADS
    nb = t // bl // blk
    scale = hd ** -0.5

    def body(q_ref, kp_ref, kc_ref, vp_ref, vc_ref, bias_ref, sink_ref, o_ref, lse_ref):
        n = pl.program_id(1)
        mask = _attn_band_mask_t(n, rep)
        for kvh in range(ATTN_KV_HEADS):
            ks = slice(kvh * hd, (kvh + 1) * hd)
            heads = range(kvh * rep, (kvh + 1) * rep)
            qs = jnp.concatenate([q_ref[:, h * hd:(h + 1) * hd] for h in heads], axis=0)
            kk = jnp.concatenate([kp_ref[:, ks], kc_ref[:, ks]], axis=0)
            vv = jnp.concatenate([vp_ref[:, ks], vc_ref[:, ks]], axis=0)
            s = jnp.where(mask, _dot(kk, qs, NT) * scale + bias_ref[kvh], NEG_INF)
            sink = _sink_row(sink_ref, heads)
            m = jnp.maximum(jnp.max(s, axis=0, keepdims=True), sink)
            p = jnp.exp(s - m)
            den = jnp.sum(p, axis=0, keepdims=True) + jnp.exp(sink - m)
            o = _dot((p * (1.0 / den)).astype(BF16), vv, TN)
            lse = m + jnp.log(den)
            for r, h in enumerate(heads):
                o_ref[:, h * hd:(h + 1) * hd] = o[r * blk:(r + 1) * blk].astype(o_ref.dtype)
                lse_ref[h:h + 1, :] = lse[:, r * blk:(r + 1) * blk]

    def cur(b, n):
        return b * nb + n

    def prev(b, n):
        return b * nb + jnp.maximum(n - 1, 0)
    return pl.pallas_call(
        body, out_shape=(jax.ShapeDtypeStruct((t, qd), BF16), jax.ShapeDtypeStruct((t // blk * ATTN_Q_HEADS, blk), F32)), grid=(bl, nb),
        in_specs=[pl.BlockSpec((blk, qd), lambda b, n: (cur(b, n), 0)),
                  pl.BlockSpec((blk, kvd), lambda b, n: (prev(b, n), 0)), pl.BlockSpec((blk, kvd), lambda b, n: (cur(b, n), 0)),
                  pl.BlockSpec((blk, kvd), lambda b, n: (prev(b, n), 1)), pl.BlockSpec((blk, kvd), lambda b, n: (cur(b, n), 1)),
                  _const_spec(bias_t.shape), _const_spec((1, LANES))],
        out_specs=(pl.BlockSpec((blk, qd), lambda b, n: (cur(b, n), 0)),
                   pl.BlockSpec((ATTN_Q_HEADS, blk), lambda b, n: (cur(b, n), 0))),
        name=name, compiler_params=_params("parallel", "arbitrary"))(q, kv, kv, kv, kv, bias_t, sinks)


def _attn_bwd(do, q, kv, lse, bias_t, sinks, bl, *, name):
    t, qd = q.shape
    blk, hd = CHUNK, ATTN_HEAD_DIM
    kvd = ATTN_KV_HEADS * hd
    rep = ATTN_Q_HEADS // ATTN_KV_HEADS
    s_len = t // bl
    nb = s_len // blk
    scale = hd ** -0.5

    def body(do_ref, q_ref, kp_ref, kc_ref, vp_ref, vc_ref, lse_ref, bias_ref, sink_ref, dq_ref, dkv_ref, dbias_ref, dsink_ref):
        n = pl.program_id(1)

        @pl.when(jnp.logical_and(pl.program_id(0) == 0, n == 0))
        def _():
            dbias_ref[...] = jnp.zeros_like(dbias_ref)
            dsink_ref[...] = jnp.zeros_like(dsink_ref)

        mask = _attn_band_mask_t(n, rep)
        r_cur = pl.multiple_of(n * blk, blk)
        r_prev = pl.multiple_of(jnp.maximum(n - 1, 0) * blk, blk)
        dsink = jnp.zeros((1, LANES), F32)
        for kvh in range(ATTN_KV_HEADS):
            ks = slice(kvh * hd, (kvh + 1) * hd)
            heads = range(kvh * rep, (kvh + 1) * rep)
            qs = jnp.concatenate([q_ref[:, h * hd:(h + 1) * hd] for h in heads], axis=0)
            dos = jnp.concatenate([do_ref[:, h * hd:(h + 1) * hd] for h in heads], axis=0)
            kk = jnp.concatenate([kp_ref[:, ks], kc_ref[:, ks]], axis=0)
            vv = jnp.concatenate([vp_ref[:, ks], vc_ref[:, ks]], axis=0)
            lse = jnp.concatenate([lse_ref[h:h + 1, :] for h in heads], axis=1)
            p = jnp.exp(jnp.where(mask, _dot(kk, qs, NT) * scale + bias_ref[kvh], NEG_INF) - lse)
            dp = _dot(vv, dos, NT)
            delta = jnp.sum(p * dp, axis=0, keepdims=True)
            ds = p * (dp - delta)
            dsink_row = jnp.exp(_sink_row(sink_ref, heads) - lse) * delta
            dbias_ref[kvh] += ds
            ds_b = ds.astype(BF16)
            dq_s = _dot(ds_b, kk, TN) * scale
            dkk = _dot(ds_b, qs, NN) * scale
            dvv = _dot(p.astype(BF16), dos, NN)
            for r, h in enumerate(heads):
                dq_ref[:, h * hd:(h + 1) * hd] = dq_s[r * blk:(r + 1) * blk].astype(dq_ref.dtype)
                dsink = dsink - jnp.sum(dsink_row[:, r * blk:(r + 1) * blk], axis=1, keepdims=True) * _lane_onehot(h)
            vs = slice(kvd + kvh * hd, kvd + (kvh + 1) * hd)
            dkv_ref[pl.ds(r_cur, blk), ks] = dkk[blk:]
            dkv_ref[pl.ds(r_cur, blk), vs] = dvv[blk:]

            @pl.when(n > 0)
            def _():
                dkv_ref[pl.ds(r_prev, blk), ks] += dkk[:blk]
                dkv_ref[pl.ds(r_prev, blk), vs] += dvv[:blk]
        dsink_ref[...] += dsink

    def cur(b, n):
        return b * nb + n

    def prev(b, n):
        return b * nb + jnp.maximum(n - 1, 0)
    qspec = pl.BlockSpec((blk, qd), lambda b, n: (cur(b, n), 0))
    return pl.pallas_call(
        body,
        out_shape=(jax.ShapeDtypeStruct((t, qd), BF16), jax.ShapeDtypeStruct((t, 2 * kvd), F32),
                   jax.ShapeDtypeStruct(bias_t.shape, F32), jax.ShapeDtypeStruct((1, LANES), F32)),
        grid=(bl, nb),
        in_specs=[qspec, qspec,
                  pl.BlockSpec((blk, kvd), lambda b, n: (prev(b, n), 0)), pl.BlockSpec((blk, kvd), lambda b, n: (cur(b, n), 0)),
                  pl.BlockSpec((blk, kvd), lambda b, n: (prev(b, n), 1)), pl.BlockSpec((blk, kvd), lambda b, n: (cur(b, n), 1)),
                  pl.BlockSpec((ATTN_Q_HEADS, blk), lambda b, n: (cur(b, n), 0)), _const_spec(bias_t.shape), _const_spec((1, LANES))],
        out_specs=(qspec, pl.BlockSpec((s_len, 2 * kvd), lambda b, n: (b, 0)), _const_spec(bias_t.shape), _const_spec((1, LANES))),
        name=name, compiler_params=_params("arbitrary", "arbitrary"))(do, q, kv, kv, kv, kv, lse, bias_t, sinks)


def _merge_fwd(yn, o, gs, ga, w_ssm, w_attn, w_out, h_in, g_post, *, name):
    t, d = h_in.shape
    tm = _tile(t, 256)

    def body(yn_ref, o_ref, gs_ref, ga_ref, ws_ref, wa_ref, wo_ref, hin_ref, gp_ref, ys_ref, ya_ref, mg_ref, mix_ref, hout_ref):
        ys = _dot(yn_ref[...], ws_ref[...], NN)
        ya = _dot(o_ref[...], wa_ref[...], NN)
        merged = (_sigmoid(gs_ref[...].astype(F32)) * ys + _sigmoid(ga_ref[...].astype(F32)) * ya).astype(BF16)
        mix = _dot(merged, wo_ref[...], NN)
        ys_ref[...] = ys.astype(BF16)
        ya_ref[...] = ya.astype(BF16)
        mg_ref[...] = merged
        mix_ref[...] = mix
        hout_ref[...] = _rms_residual(mix, hin_ref[...], gp_ref[...], 1.0)

    def row(w):
        return pl.BlockSpec((tm, w), lambda i: (i, 0))
    bshape = jax.ShapeDtypeStruct((t, d), BF16)
    fshape = jax.ShapeDtypeStruct((t, d), F32)
    return pl.pallas_call(
        body, out_shape=(bshape, bshape, bshape, fshape, fshape), grid=(t // tm,),
        in_specs=[row(yn.shape[1]), row(o.shape[1]), row(d), row(d), _resident_spec(w_ssm.shape), _resident_spec(w_attn.shape),
                  _resident_spec(w_out.shape), row(d), _const_spec((1, d))],
        out_specs=(row(d),) * 5, name=name, compiler_params=_params("parallel"))(yn, o, gs, ga, w_ssm, w_attn, w_out, h_in, g_post)


def _merge_bwd(dh, mix, g_post, gs, ga, ys, ya, w_ssm, w_attn, w_out, *, name):
    t, d = mix.shape
    tm = _tile(t, 256)
    d_ssm, d_attn = w_ssm.shape[0], w_attn.shape[0]

    def body(dh_ref, mix_ref, gp_ref, gs_ref, ga_ref, ys_ref, ya_ref, ws_ref, wa_ref, wo_ref,
             dmix_ref, dys_ref, dya_ref, dgs_ref, dga_ref, dyn_ref, do_ref, dgp_ref):
        @pl.when(pl.program_id(0) == 0)
        def _():
            dgp_ref[...] = jnp.zeros_like(dgp_ref)
        mv = mix_ref[...]
        dy = dh_ref[...]
        r = lax.rsqrt(jnp.mean(mv * mv, axis=-1, keepdims=True) + RMS_EPS)
        mhat = mv * r
        dyg = dy * gp_ref[...]
        dmix = (r * (dyg - mhat * jnp.mean(dyg * mhat, axis=-1, keepdims=True))).astype(BF16)
        dgp_ref[...] += jnp.sum(dy * mhat, axis=0, keepdims=True)
        dmix_ref[...] = dmix
        dmerged = _dot(dmix, wo_ref[...], NT)
        sgs = _sigmoid(gs_ref[...].astype(F32))
        sga = _sigmoid(ga_ref[...].astype(F32))
        dys = (dmerged * sgs).astype(BF16)
        dya = (dmerged * sga).astype(BF16)
        dys_ref[...] = dys
        dya_ref[...] = dya
        dgs_ref[...] = (dmerged * ys_ref[...].astype(F32) * sgs * (1.0 - sgs)).astype(BF16)
        dga_ref[...] = (dmerged * ya_ref[...].astype(F32) * sga * (1.0 - sga)).astype(BF16)
        dyn_ref[...] = _dot(dys, ws_ref[...], NT).astype(BF16)
        do_ref[...] = _dot(dya, wa_ref[...], NT).astype(BF16)

    def row(w):
        return pl.BlockSpec((tm, w), lambda i: (i, 0))

    def bshape(w):
        return jax.ShapeDtypeStruct((t, w), BF16)
    return pl.pallas_call(
        body, out_shape=(bshape(d),) * 5 + (bshape(d_ssm), bshape(d_attn), jax.ShapeDtypeStruct((1, d), F32)), grid=(t // tm,),
        in_specs=[row(d), row(d), _const_spec((1, d)), row(d), row(d), row(d), row(d),
                  _resident_spec(w_ssm.shape), _resident_spec(w_attn.shape), _resident_spec(w_out.shape)],
        out_specs=(row(d),) * 5 + (row(d_ssm), row(d_attn), _const_spec((1, d))),
        name=name, compiler_params=_params("arbitrary"))(dh, mix, g_post, gs, ga, ys, ya, w_ssm, w_attn, w_out)


def _position():
    return lax.axis_index("x"), lax.axis_index("y"), lax.axis_index("c")


def _gather_exchange(shards):
    na = len(shards)

    def plan(ins, outs, sems):
        send_sems, recv_sems, local_sems = sems
        x, y, c = _position()
        me, sibling = (x, y, c), (x, y, 1 - c)
        chips = [(1 - x, y), (x, 1 - y), (1 - x, 1 - y)]

        def slot(a, pos):
            return outs[a].at[4 * pos[0] + 2 * pos[1] + pos[2]]

        def copy(a, k, block, to, src=None):
            return pltpu.make_async_remote_copy(
                src_ref=slot(a, block) if src is None else src, dst_ref=slot(a, block),
                send_sem=send_sems.at[a, k], recv_sem=recv_sems.at[a, k], device_id=to, device_id_type=MESH)

        mine = [pltpu.make_async_copy(ins[a], slot(a, me), local_sems.at[a]) for a in range(na)]
        first = []
        for a in range(na):
            first.append(copy(a, 0, me, sibling, src=ins[a]))
            first += [copy(a, 1 + j, me, (*chip, c), src=ins[a]) for j, chip in enumerate(chips)]
        return me, sibling, chips, copy, mine, first

    def start(ins, outs, sems):
        *_, mine, first = plan(ins, outs, sems)
        for cp in mine + first:
            cp.start()

    def finish(ins, outs, sems):
        me, sibling, chips, copy, mine, first = plan(ins, outs, sems)
        c = me[2]
        passed = []
        for a in range(na):
            for j, chip in enumerate(chips):
                copy(a, 1 + j, (*chip, c), me).wait_recv()
                fwd = copy(a, 4 + j, (*chip, c), sibling)
                fwd.start()
                passed.append(fwd)
        for a in range(na):
            copy(a, 0, sibling, me).wait_recv()
            for j, chip in enumerate(chips):
                copy(a, 4 + j, (*chip, 1 - c), me).wait_recv()
        for cp in first + passed:
            cp.wait_send()
        for cp in mine:
            cp.wait()

    return _Exchange(list(shards), [jax.ShapeDtypeStruct((N_DEV,) + s.shape, s.dtype) for s in shards],
                     [pltpu.SemaphoreType.DMA((na, 7)), pltpu.SemaphoreType.DMA((na, 7)), pltpu.SemaphoreType.DMA((na,))],
                     start, finish)


def _scatter_exchange(arrays):
    na = len(arrays)

    def copies(ins, outs, sems):
        send_sems, recv_sems = sems
        x, y, c = _position()
        out = []
        for a in range(na):
            for k in range(7):
                flip = k + 1
                peer = (x ^ (flip >> 2), y ^ ((flip >> 1) & 1), c ^ (flip & 1))
                peer_block = 4 * peer[0] + 2 * peer[1] + peer[2]
                out.append(pltpu.make_async_remote_copy(
                    src_ref=ins[a].at[peer_block], dst_ref=outs[a].at[k],
                    send_sem=send_sems.at[a, k], recv_sem=recv_sems.at[a, k], device_id=peer, device_id_type=MESH))
        return out

    def start(ins, outs, sems):
        for cp in copies(ins, outs, sems):
            cp.start()

    def finish(ins, outs, sems):
        for cp in copies(ins, outs, sems):
            cp.wait()

    return _Exchange(list(arrays), [jax.ShapeDtypeStruct((7,) + s.shape[1:], s.dtype) for s in arrays],
                     [pltpu.SemaphoreType.DMA((na, 7)), pltpu.SemaphoreType.DMA((na, 7))], start, finish)


def _reduce_adamw(own, recv, w, m, v, *, name):
    r, c = own.shape
    tm = _tile(r, 256)
    c1 = 1.0 - ADAM_B1 ** ADAM_STEP
    c2 = 1.0 - ADAM_B2 ** ADAM_STEP

    def body(own_ref, recv_ref, w_ref, m_ref, v_ref, g_ref, d_ref, mo_ref, vo_ref):
        gv = own_ref[...]
        for k in range(7):
            gv = gv + recv_ref[k].astype(F32)
        g_ref[...] = gv
        mn = ADAM_B1 * m_ref[...] + (1.0 - ADAM_B1) * gv
        vn = ADAM_B2 * v_ref[...] + (1.0 - ADAM_B2) * (gv * gv)
        mo_ref[...] = mn
        vo_ref[...] = vn
        d_ref[...] = -ADAM_LR * ((mn / c1) / (jnp.sqrt(vn / c2) + ADAM_EPS) + ADAM_WD * w_ref[...])

    blk = pl.BlockSpec((tm, c), lambda i: (i, 0))
    shp = jax.ShapeDtypeStruct((r, c), F32)
    return pl.pallas_call(body, out_shape=(shp,) * 4, grid=(r // tm,),
                          in_specs=[blk, pl.BlockSpec((7, tm, c), lambda i: (0, i, 0)), blk, blk, blk], out_specs=(blk,) * 4,
                          name=name, compiler_params=_params("parallel"))(own, recv, w, m, v)


SMALL_ROW = 8 * LANES


def _update_replicated(partials, ws, ms, vs, *, name):
    n = len(ws)
    pieces, n_rows = [], 0
    for k in sorted(range(n), key=lambda k_: -ws[k_].shape[0]):
        r, c = ws[k].shape
        assert (r == 1 or c <= SMALL_ROW) and (r == 1 or n_rows % 8 == 0)
        for c0 in range(0, c, SMALL_ROW):
            pieces.append((k, slice(0, r), slice(c0, min(c0 + SMALL_ROW, c)), n_rows))
            n_rows += r
    n_rows = -(-n_rows // 8) * 8
    c1 = 1.0 - ADAM_B1 ** ADAM_STEP
    c2 = 1.0 - ADAM_B2 ** ADAM_STEP

    def body(*refs):
        g_in, w_in_, m_in, v_in = (refs[j * n:(j + 1) * n] for j in range(4))
        g_out, d_out, m_out, v_out = (refs[(4 + j) * n:(5 + j) * n] for j in range(4))
        buf, total, send_sems, recv_sems = refs[8 * n:]
        x, y, c_ = _position()
        me = 4 * x + 2 * y + c_
        buf[me] = jnp.zeros((n_rows, SMALL_ROW), F32)
        for k, rs, cs, row0 in pieces:
            buf[me, row0:row0 + rs.stop, 0:cs.stop - cs.start] = g_in[k][rs, cs]
        copies = []
        for j in range(7):
            flip = j + 1
            peer = (x ^ (flip >> 2), y ^ ((flip >> 1) & 1), c_ ^ (flip & 1))
            cp = pltpu.make_async_remote_copy(
                src_ref=buf.at[me], dst_ref=buf.at[me], send_sem=send_sems.at[j], recv_sem=recv_sems.at[j],
                device_id=peer, device_id_type=MESH)
            cp.start()
            copies.append(cp)
        for cp in copies:
            cp.wait()
        acc = buf[0]
        for dev in range(1, N_DEV):
            acc = acc + buf[dev]
        total[...] = acc
        for k, rs, cs, row0 in pieces:
            gv = total[row0:row0 + rs.stop, 0:cs.stop - cs.start]
            mn = ADAM_B1 * m_in[k][rs, cs] + (1.0 - ADAM_B1) * gv
            vn = ADAM_B2 * v_in[k][rs, cs] + (1.0 - ADAM_B2) * (gv * gv)
            g_out[k][rs, cs] = gv
            m_out[k][rs, cs] = mn
            v_out[k][rs, cs] = vn
            d_out[k][rs, cs] = -ADAM_LR * ((mn / c1) / (jnp.sqrt(vn / c2) + ADAM_EPS) + ADAM_WD * w_in_[k][rs, cs])

    vm = pl.BlockSpec(memory_space=pltpu.VMEM)
    shapes = tuple(jax.ShapeDtypeStruct(w.shape, F32) for w in ws)
    outs = pl.pallas_call(
        body, out_shape=shapes * 4, in_specs=[vm] * (4 * n), out_specs=tuple([vm] * (4 * n)),
        scratch_shapes=[pltpu.VMEM((N_DEV, n_rows, SMALL_ROW), F32), pltpu.VMEM((n_rows, SMALL_ROW), F32),
                        pltpu.SemaphoreType.DMA((7,)), pltpu.SemaphoreType.DMA((7,))],
        name=name)(*partials, *ws, *ms, *vs)
    return tuple(outs[j * n:(j + 1) * n] for j in range(4))


def _pad_lanes(v, width=LANES):
    return jnp.pad(v, ((0, 0), (0, width - v.shape[1])))


def kernel(x, ffn1_pre_g, ffn1_w_gate, ffn1_w_up, ffn1_w_down, ffn1_post_g, mix_pre_g, w_in, conv_w, conv_b, dt_bias, a_log, d_skip, ssm_norm_g, w_ssm_proj, attn_sinks, rel_bias_table, w_attn_proj, w_out, mix_post_g, ffn2_pre_g, ffn2_w_gate, ffn2_w_up, ffn2_w_down, ffn2_post_g, loss_target, m_ffn1_pre_g, m_ffn1_w_gate, m_ffn1_w_up, m_ffn1_w_down, m_ffn1_post_g, m_mix_pre_g, m_w_in, m_conv_w, m_conv_b, m_dt_bias, m_a_log, m_d_skip, m_ssm_norm_g, m_w_ssm_proj, m_attn_sinks, m_rel_bias_table, m_w_attn_proj, m_w_out, m_mix_post_g, m_ffn2_pre_g, m_ffn2_w_gate, m_ffn2_w_up, m_ffn2_w_down, m_ffn2_post_g, v_ffn1_pre_g, v_ffn1_w_gate, v_ffn1_w_up, v_ffn1_w_down, v_ffn1_post_g, v_mix_pre_g, v_w_in, v_conv_w, v_conv_b, v_dt_bias, v_a_log, v_d_skip, v_ssm_norm_g, v_w_ssm_proj, v_attn_sinks, v_rel_bias_table, v_w_attn_proj, v_w_out, v_mix_post_g, v_ffn2_pre_g, v_ffn2_w_gate, v_ffn2_w_up, v_ffn2_w_down, v_ffn2_post_g):
    args = dict(locals())
    weight_names = ['ffn1_pre_g', 'ffn1_w_gate', 'ffn1_w_up', 'ffn1_w_down', 'ffn1_post_g', 'mix_pre_g', 'w_in', 'conv_w', 'conv_b',
                    'dt_bias', 'a_log', 'd_skip', 'ssm_norm_g', 'w_ssm_proj', 'attn_sinks', 'rel_bias_table', 'w_attn_proj', 'w_out',
                    'mix_post_g', 'ffn2_pre_g', 'ffn2_w_gate', 'ffn2_w_up', 'ffn2_w_down', 'ffn2_post_g']
    col_sharded = ('ffn1_w_gate', 'ffn1_w_up', 'w_in', 'ffn2_w_gate', 'ffn2_w_up')
    row_sharded = ('ffn1_w_down', 'w_ssm_proj', 'w_attn_proj', 'w_out', 'ffn2_w_down')
    big = col_sharded + row_sharded

    bl, s_len, d = x.shape
    t = bl * s_len
    d_inner = ssm_norm_g.shape[1]
    n_heads = dt_bias.shape[1]
    gn = SSM_GROUPS * SSM_STATE
    conv_dim = d_inner + 2 * gn
    q_dim = ATTN_Q_HEADS * ATTN_HEAD_DIM
    kv_dim = ATTN_KV_HEADS * ATTN_HEAD_DIM

    def local_2d(name, a):
        a = a[0]
        return a.T if name in col_sharded else a

    ffn1_names = ('ffn1_w_gate', 'ffn1_w_up', 'ffn1_w_down')
    ffn2_names = ('ffn2_w_gate', 'ffn2_w_up', 'ffn2_w_down')
    mixer_names = ('w_ssm_proj', 'w_attn_proj', 'w_out')

    def shard(n):
        return local_2d(n, args[n]).astype(BF16)

    def rows(g):
        return g.reshape(N_DEV * g.shape[1], g.shape[2])

    x2 = x.reshape(t, d)
    tgt2 = loss_target.reshape(t, d)
    full = {}

    (h1,), saved1, ffn1_w, got_in = _ffn_forward(
        x2, ffn1_pre_g, lambda got: (rows(got[0]), rows(got[1])), lambda got: rows(got[2]), ffn1_post_g, "ffn1",
        side_norm=_gather_exchange([shard(n) for n in ffn1_names[:2]]),
        side_up=_gather_exchange([shard('w_in'), conv_w[0], shard('ffn1_w_down')]))
    full.update(zip(ffn1_names, ffn1_w))
    conv_w_full = jnp.transpose(got_in[1], (1, 0, 2)).reshape(SSM_CONV, conv_dim)

    win_t = rows(got_in[0])
    dt_lo = 2 * d + d_inner + conv_dim
    off = {'gs': 0, 'ga': d, 'z': 2 * d, 'xbc': 2 * d + d_inner, 'dt': dt_lo, 'q': dt_lo + n_heads, 'kv': dt_lo + n_heads + q_dim}
    assert all(o_ % 16 == 0 for o_ in off.values())

    u, (gs, ga, z, xbc, q, kv, dt_raw) = _proj_all(
        h1, mix_pre_g, win_t,
        [(off['gs'], d, BF16), (off['ga'], d, BF16), (off['z'], d_inner, BF16), (off['xbc'], conv_dim, BF16),
         (off['q'], q_dim, BF16), (off['kv'], 2 * kv_dim, BF16), (off['dt'], n_heads, F32)], name="mix_proj")

    dtb_p, alog_p, dsk_p, sinks_p = _pad_lanes(dt_bias), _pad_lanes(a_log), _pad_lanes(d_skip), _pad_lanes(attn_sinks)
    xc = _conv_fwd(xbc, conv_w_full, conv_b, bl, name="conv_fwd")
    late_names = mixer_names + ffn2_names
    (y, hprev), got_late = _ssd_fwd(xc, dt_raw, dtb_p, alog_p, dsk_p, bl, n_heads,
                                    side=_gather_exchange([shard(n) for n in late_names]), name="ssd_fwd")
    full.update({n: rows(g) for n, g in zip(late_names, got_late)})
    yn = _gated_norm_fwd(y, z, ssm_norm_g, name="gated_norm_fwd")

    onehot = _bucket_onehot()
    rep = ATTN_Q_HEADS // ATTN_KV_HEADS
    bias = _small_mm_hi(rel_bias_table.T, onehot, NN, name="rel_bias")
    bias_t = jnp.transpose(bias.reshape(ATTN_KV_HEADS, rep, CHUNK, 2 * CHUNK), (0, 3, 1, 2)).reshape(ATTN_KV_HEADS, 2 * CHUNK, rep * CHUNK)
    o, lse = _attn_fwd(q, kv, bias_t, sinks_p, bl, name="attn_fwd")

    ys, ya, merged, mix, h2 = _merge_fwd(yn, o, gs, ga, full['w_ssm_proj'], full['w_attn_proj'], full['w_out'], h1, mix_post_g,
                                         name="merge_fwd")

    (dh3, loss_vec), saved2, _, _ = _ffn_forward(h2, ffn2_pre_g, (full['ffn2_w_gate'], full['ffn2_w_up']), full['ffn2_w_down'],
                                                 ffn2_post_g, "ffn2", target=tgt2)
    loss = lax.psum(loss_vec[0, 0], ("x", "y", "c"))

    grads, own, wire, received = {}, {}, {}, {}
    dh2, grads['ffn2_pre_g'], grads['ffn2_post_g'], g32, g16, _ = _ffn_backward(
        dh3, saved2, ffn2_pre_g, full['ffn2_w_gate'], full['ffn2_w_up'], full['ffn2_w_down'], ffn2_post_g, "ffn2")
    own.update(zip(ffn2_names, map(_stack8, g32)))
    wire.update(zip(ffn2_names, map(_stack8, g16)))

    dmix, dys, dya, dgs, dga, dyn, do, grads['mix_post_g'] = _merge_bwd(
        dh2, mix, mix_post_g, gs, ga, ys, ya, full['w_ssm_proj'], full['w_attn_proj'], full['w_out'], name="merge_bwd")
    for n, (lhs, rhs) in zip(mixer_names, ((yn, dys), (o, dya), (merged, dmix))):
        g32_, g16_, _ = _mm_tn([lhs], rhs, name=f"d{n}")
        own[n], wire[n] = _stack8(g32_), _stack8(g16_)

    dq, dkv, dbias_t, dsinks = _attn_bwd(do, q, kv, lse, bias_t, sinks_p, bl, name="attn_bwd")
    dbias = jnp.transpose(dbias_t.reshape(ATTN_KV_HEADS, 2 * CHUNK, rep, CHUNK), (0, 2, 3, 1)).reshape(ATTN_Q_HEADS, -1)
    d_table = _small_mm_hi(onehot, dbias, NT, name="rel_bias_bwd")

    dy, dz, grads['ssm_norm_g'] = _gated_norm_bwd(dyn, y, z, ssm_norm_g, name="gated_norm_bwd")

    first_group = ffn2_names + mixer_names
    (dxc, ddt_raw, ddtb, dalog, ddsk), got = _ssd_bwd(dy, y, xc, dt_raw, hprev, dtb_p, alog_p, dsk_p, bl, n_heads,
                                                      side=_scatter_exchange([wire[n] for n in first_group]), name="ssd_bwd")
    received.update(zip(first_group, got))
    dxbc, dconv_w8, grads['conv_b'] = _conv_bwd(dxc, xbc, conv_w_full, conv_b, bl, name="conv_bwd")

    wide32, wide16, _ = _mm_tn([dgs, dga, dz, dxbc, dq], u, name="dw_in")
    kv32, kv16, _ = _mm_tn([dkv], u, name="dw_in_kv")
    dt32, dt16, _ = _mm_tn([ddt_raw], u, name="dw_in_dt")

    def original_order(wide, kv_part, dt_part):
        return jnp.concatenate([wide[:dt_lo], dt_part[:n_heads], wide[dt_lo:], kv_part], axis=0)
    me = 4 * lax.axis_index("x") + 2 * lax.axis_index("y") + lax.axis_index("c")
    blk_rows = win_t.shape[0] // N_DEV
    wire['w_in'] = _stack8(original_order(wide16, kv16, dt16))
    own_w_in = lax.dynamic_slice_in_dim(original_order(wide32, kv32, dt32), me * blk_rows, blk_rows)
    own['conv_w'] = jnp.transpose(dconv_w8[:SSM_CONV].reshape(SSM_CONV, N_DEV, conv_dim // N_DEV), (1, 0, 2))

    segs = [(g_, 0, off[k_]) for g_, k_ in zip([dgs, dga, dz, dxbc, dq, dkv, ddt_raw], ('gs', 'ga', 'z', 'xbc', 'q', 'kv', 'dt'))]
    dh1, grads['mix_pre_g'], got = _mm_nn_rmsbwd(segs, [win_t], h1, mix_pre_g, dh2,
                                                 side=_scatter_exchange([wire['w_in'], own['conv_w']]), name="mix_du")
    received.update(zip(('w_in', 'conv_w'), got))

    dx2, grads['ffn1_pre_g'], grads['ffn1_post_g'], g32, _, got = _ffn_backward(
        dh1, saved1, ffn1_pre_g, full['ffn1_w_gate'], full['ffn1_w_up'], full['ffn1_w_down'], ffn1_post_g, "ffn1", chain=True)
    own.update(zip(ffn1_names, map(_stack8, g32)))
    received.update(zip(ffn1_names, got))

    def own_block(a):
        return lax.dynamic_index_in_dim(a, me, 0, keepdims=False)
    out_g, out_d, out_m, out_v = {}, {}, {}, {}
    for n in big:
        w2, m2, v2 = local_2d(n, args[n]), local_2d(n, args['m_' + n]), local_2d(n, args['v_' + n])
        results = _reduce_adamw(own_w_in if n == 'w_in' else own_block(own[n]), received[n], w2, m2, v2, name=f"update_{n}")
        out_g[n], out_d[n], out_m[n], out_v[n] = ((a.T if n in col_sharded else a)[None] for a in results)
    results = _reduce_adamw(own_block(own['conv_w']), received['conv_w'], conv_w[0], m_conv_w[0], v_conv_w[0], name="update_conv_w")
    out_g['conv_w'], out_d['conv_w'], out_m['conv_w'], out_v['conv_w'] = (a[None] for a in results)

    grads['dt_bias'], grads['a_log'], grads['d_skip'], grads['attn_sinks'], grads['rel_bias_table'] = ddtb, dalog, ddsk, dsinks, d_table
    small = [n for n in weight_names if n not in big and n != 'conv_w']
    results = _update_replicated([grads[n] for n in small], [args[n] for n in small], [args['m_' + n] for n in small],
                                 [args['v_' + n] for n in small], name="update_replicated")
    for dst, vals in zip((out_g, out_d, out_m, out_v), results):
        dst.update(zip(small, vals))

    grad_x = dx2.reshape(bl, s_len, d)
    return (loss, grad_x, *[out_g[n] for n in weight_names], *[out_d[n] for n in weight_names],
            *[out_m[n] for n in weight_names], *[out_v[n] for n in weight_names])
```

```python
import functools
import math

import numpy as np
import jax
import jax.numpy as jnp
from jax import lax
from jax.experimental import pallas as pl
from jax.experimental.pallas import tpu as pltpu

F32 = jnp.float32
BF16 = jnp.bfloat16
MESH = pl.DeviceIdType.MESH
N_DEV = 8

SSM_HEAD_DIM = 64
SSM_GROUPS = 4
SSM_STATE = 128
SSM_CONV = 4
CHUNK = 128
ATTN_HEAD_DIM = 64
ATTN_Q_HEADS = 16
ATTN_KV_HEADS = 4
REL_BUCKETS = 32
REL_MAX_DISTANCE = 128
RMS_EPS = 1e-6
FFN_RESIDUAL_WEIGHT = 0.5
ADAM_LR, ADAM_B1, ADAM_B2, ADAM_EPS, ADAM_WD, ADAM_STEP = 0.001, 0.9, 0.999, 1e-08, 0.01, 10

LANES = 128
VMEM_LIMIT_BYTES = 56 * 1024 * 1024
FFN_COL_TILE = 1408
SSD_HEAD_BATCH = 2

NEG_INF = float("-inf")


def _params(*sem):
    return pltpu.CompilerParams(dimension_semantics=sem, vmem_limit_bytes=VMEM_LIMIT_BYTES)


def _tile(n, pref, mult=8):
    if n <= pref:
        return n
    t = (pref // mult) * mult
    while t >= mult:
        if n % t == 0:
            return t
        t -= mult
    return n


def _sigmoid(x):
    return 1.0 / (1.0 + jnp.exp(-x))


def _dot(a, b, dims):
    return lax.dot_general(a, b, (dims, ((), ())), preferred_element_type=F32)


NN = ((1,), (0,))
NT = ((1,), (1,))
TN = ((0,), (0,))


def _dot_hi(a, b, dims=NN):
    return lax.dot_general(a, b, (dims, ((), ())), preferred_element_type=F32, precision=lax.Precision.HIGHEST)


def _const_spec(shape):
    nd = len(shape)
    return pl.BlockSpec(shape, lambda *_: (0,) * nd)


def _resident_spec(shape):
    nd = len(shape)
    return pl.BlockSpec(shape, lambda *_: (0,) * nd, pipeline_mode=pl.Buffered(1))


class _Exchange:
    def __init__(self, arrays, out_shape, scratch, start, finish):
        self.arrays, self.out_shape, self.scratch, self.start, self.finish = arrays, out_shape, scratch, start, finish


def _hosted_call(body, *, grid, in_specs, out_specs, out_shape, scratch_shapes, operands, side, name):
    in_specs, out_specs, out_shape, scratch_shapes = list(in_specs), list(out_specs), list(out_shape), list(scratch_shapes)
    sem = ("arbitrary",) * len(grid)
    if side is None:
        outs = pl.pallas_call(body, out_shape=tuple(out_shape), grid=grid, in_specs=in_specs, out_specs=tuple(out_specs),
                              scratch_shapes=scratch_shapes, name=name, compiler_params=_params(*sem))(*operands)
        return tuple(outs), ()
    n_in, n_out, n_scr = len(in_specs), len(out_shape), len(scratch_shapes)
    s_in, s_out = len(side.arrays), len(side.out_shape)

    def wrapped(*refs):
        refs = list(refs)
        main_in, side_in = refs[:n_in], refs[n_in:n_in + s_in]
        o0 = n_in + s_in
        main_out, side_out = refs[o0:o0 + n_out], refs[o0 + n_out:o0 + n_out + s_out]
        c0 = o0 + n_out + s_out
        main_scr, side_scr = refs[c0:c0 + n_scr], refs[c0 + n_scr:]
        ids = [pl.program_id(ax) for ax in range(len(grid))]
        first = functools.reduce(jnp.logical_and, [i == 0 for i in ids])
        last = functools.reduce(jnp.logical_and, [i == g - 1 for i, g in zip(ids, grid)])

        @pl.when(first)
        def _():
            side.start(side_in, side_out, side_scr)

        body(*main_in, *main_out, *main_scr)

        @pl.when(last)
        def _():
            side.finish(side_in, side_out, side_scr)

    hbm = pl.BlockSpec(memory_space=pl.ANY)
    outs = pl.pallas_call(
        wrapped, out_shape=tuple(out_shape + list(side.out_shape)), grid=grid,
        in_specs=in_specs + [hbm] * s_in, out_specs=tuple(out_specs + [hbm] * s_out),
        scratch_shapes=scratch_shapes + list(side.scratch), name=name, compiler_params=_params(*sem))(*operands, *side.arrays)
    return tuple(outs[:n_out]), tuple(outs[n_out:])


def _proj_all(h, g, w, segs, *, name):
    t, d = h.shape
    tm = _tile(t, 512)
    segs = [(row0, wd_, max(wd_, LANES), dt_) for row0, wd_, dt_ in segs]
    assert all(row0 + out_w <= w.shape[0] for row0, _, out_w, _ in segs)

    def body(h_ref, g_ref, w_ref, u_ref, *o_refs):
        hv = h_ref[...]
        r = lax.rsqrt(jnp.mean(hv * hv, axis=-1, keepdims=True) + RMS_EPS)
        uv = (hv * r * g_ref[...]).astype(BF16)
        u_ref[...] = uv
        for (row0, width, out_w, _), o_ref in zip(segs, o_refs):
            for c0, c1 in _col_chunks(out_w, 8 * LANES):
                part = _dot(uv, w_ref[row0 + c0:row0 + c1, :], NT)
                if width < out_w:
                    part = jnp.where(lax.broadcasted_iota(jnp.int32, part.shape, 1) < width, part, 0.0)
                o_ref[:, c0:c1] = part.astype(o_ref.dtype)

    row = pl.BlockSpec((tm, d), lambda i: (i, 0))
    outs = pl.pallas_call(
        body, out_shape=(jax.ShapeDtypeStruct((t, d), BF16),) + tuple(jax.ShapeDtypeStruct((t, ow), dt_) for _, _, ow, dt_ in segs),
        grid=(t // tm,), in_specs=[row, _const_spec((1, d)), _resident_spec(w.shape)],
        out_specs=(row,) + tuple(pl.BlockSpec((tm, ow), lambda i: (i, 0)) for _, _, ow, _ in segs),
        name=name, compiler_params=_params("parallel"))(h, g, w)
    return outs[0], outs[1:]


def _mm_tn(a_list, b, *, tm=1408, tk=2048, side=None, name):
    t, n = b.shape
    tk = _tile(t, tk if len(a_list) == 1 else tk // 2)
    nk = t // tk
    widths = [a.shape[1] for a in a_list]
    tm = _tile(math.gcd(*widths), tm, LANES)
    assert all(w % tm == 0 for w in widths)
    starts = np.cumsum([0] + [w // tm for w in widths])
    nseg = len(a_list)

    def a_spec(s):
        lo, hi = int(starts[s]), int(starts[s + 1])

        def idx(i, k):
            active = jnp.logical_and(i >= lo, i < hi)
            return (jnp.where(active, k, 0), jnp.clip(i - lo, 0, hi - lo - 1))
        return pl.BlockSpec((tk, tm), idx)

    def body(*refs):
        a_refs, b_ref, o_ref, o16_ref, acc = refs[:nseg], refs[nseg], refs[nseg + 1], refs[nseg + 2], refs[nseg + 3]
        i, k = pl.program_id(0), pl.program_id(1)

        @pl.when(k == 0)
        def _():
            acc[...] = jnp.zeros_like(acc)

        bv = b_ref[...].astype(BF16)
        for s in range(nseg):
            lo, hi = int(starts[s]), int(starts[s + 1])

            @pl.when(jnp.logical_and(i >= lo, i < hi))
            def _(s=s):
                acc[...] += _dot(a_refs[s][...].astype(BF16), bv, TN)

        @pl.when(k == nk - 1)
        def _():
            o_ref[...] = acc[...]
            o16_ref[...] = acc[...].astype(BF16)

    rows = int(starts[-1]) * tm
    o_spec = pl.BlockSpec((tm, n), lambda i, k: (i, 0))
    (o32, o16), got = _hosted_call(
        body, out_shape=[jax.ShapeDtypeStruct((rows, n), F32), jax.ShapeDtypeStruct((rows, n), BF16)], grid=(int(starts[-1]), nk),
        in_specs=[a_spec(s) for s in range(nseg)] + [pl.BlockSpec((tk, n), lambda i, k: (k, 0))],
        out_specs=[o_spec, o_spec], scratch_shapes=[pltpu.VMEM((tm, n), F32)], operands=list(a_list) + [b], side=side, name=name)
    return o32, o16, got


def _mm_nn_rmsbwd(segs, weights, x, g, dres, *, tm=512, side=None, name):
    t, d = x.shape
    tm = _tile(t, tm)
    nseg, nw = len(segs), len(weights)

    def body(*refs):
        a_refs, w_refs = refs[:nseg], refs[nseg:nseg + nw]
        x_ref, g_ref, dres_ref, dx_ref, dg_ref = refs[nseg + nw:]

        @pl.when(pl.program_id(0) == 0)
        def _():
            dg_ref[...] = jnp.zeros_like(dg_ref)

        dn = None
        for s, (a, w_idx, row0) in enumerate(segs):
            part = _dot(a_refs[s][...].astype(BF16), w_refs[w_idx][row0:row0 + a.shape[1], :], NN)
            dn = part if dn is None else dn + part
        xv = x_ref[...]
        r = lax.rsqrt(jnp.mean(xv * xv, axis=-1, keepdims=True) + RMS_EPS)
        xhat = xv * r
        dyg = dn * g_ref[...]
        dx_ref[...] = dres_ref[...] + r * (dyg - xhat * jnp.mean(dyg * xhat, axis=-1, keepdims=True))
        dg_ref[...] += jnp.sum(dn * xhat, axis=0, keepdims=True)

    row = pl.BlockSpec((tm, d), lambda i: (i, 0))
    in_specs = [pl.BlockSpec((tm, a.shape[1]), lambda i: (i, 0)) for a, _, _ in segs]
    in_specs += [_resident_spec(w.shape) for w in weights] + [row, _const_spec((1, d)), row]
    (dx, dg), extra = _hosted_call(
        body, grid=(t // tm,), in_specs=in_specs, out_specs=[row, _const_spec((1, d))],
        out_shape=[jax.ShapeDtypeStruct((t, d), F32), jax.ShapeDtypeStruct((1, d), F32)], scratch_shapes=[],
        operands=[a for a, _, _ in segs] + list(weights) + [x, g, dres], side=side, name=name)
    return dx, dg, extra


def _col_chunks(width, chunk=4 * LANES):
    return [(c0, min(c0 + chunk, width)) for c0 in range(0, width, chunk)]


def _rms_fwd(x, g, *, side=None, name):
    t, d = x.shape
    tm = _tile(t, 512)

    def body(x_ref, g_ref, o_ref):
        xv = x_ref[...]
        r = lax.rsqrt(jnp.mean(xv * xv, axis=-1, keepdims=True) + RMS_EPS)
        o_ref[...] = (xv * r * g_ref[...]).astype(o_ref.dtype)

    row = pl.BlockSpec((tm, d), lambda i: (i, 0))
    (n,), got = _hosted_call(body, grid=(t // tm,), in_specs=[row, _const_spec((1, d))], out_specs=[row],
                             out_shape=[jax.ShapeDtypeStruct((t, d), BF16)], scratch_shapes=[], operands=[x, g], side=side, name=name)
    return n, got


def _ffn_up(n, wgt, wut, *, side=None, name):
    t, d = n.shape
    f = wgt.shape[0]
    tm, tn = _tile(t, 512), _tile(f, FFN_COL_TILE, LANES)

    def body(n_ref, wg_ref, wu_ref, g_ref, u_ref, h_ref):
        nv = n_ref[...]
        gv = _dot(nv, wg_ref[...], NT)
        uv = _dot(nv, wu_ref[...], NT)
        g_ref[...] = gv.astype(BF16)
        u_ref[...] = uv.astype(BF16)
        h_ref[...] = (gv * _sigmoid(gv) * uv).astype(BF16)

    w_spec = pl.BlockSpec((tn, d), lambda j, i: (j, 0))
    o_spec = pl.BlockSpec((tm, tn), lambda j, i: (i, j))
    shp = jax.ShapeDtypeStruct((t, f), BF16)
    return _hosted_call(body, grid=(f // tn, t // tm), in_specs=[pl.BlockSpec((tm, d), lambda j, i: (i, 0)), w_spec, w_spec],
                        out_specs=[o_spec, o_spec, o_spec], out_shape=[shp, shp, shp], scratch_shapes=[], operands=[n, wgt, wut],
                        side=side, name=name)


def _rms_residual(acc, h, gp, weight):
    r = lax.rsqrt(jnp.mean(acc * acc, axis=-1, keepdims=True) + RMS_EPS)
    return h + weight * (acc * r * gp)


def _ffn_down(hid, wd, h_in, gp, *, target=None, side=None, name):
    t, f = hid.shape
    d = wd.shape[1]
    tm = _tile(t, 256)
    row = pl.BlockSpec((tm, d), lambda i: (i, 0))
    shp = jax.ShapeDtypeStruct((t, d), F32)
    in_specs = [pl.BlockSpec((tm, f), lambda i: (i, 0)), _resident_spec((f, d)), row, _const_spec((1, d))]

    if target is None:
        def body(hid_ref, wd_ref, hin_ref, gp_ref, f_ref, hout_ref):
            acc = _dot(hid_ref[...], wd_ref[...], NN)
            f_ref[...] = acc
            hout_ref[...] = _rms_residual(acc, hin_ref[...], gp_ref[...], FFN_RESIDUAL_WEIGHT)

        return _hosted_call(body, grid=(t // tm,), in_specs=in_specs, out_specs=[row, row], out_shape=[shp, shp], scratch_shapes=[],
                            operands=[hid, wd, h_in, gp], side=side, name=name)

    def body_loss(hid_ref, wd_ref, hin_ref, gp_ref, tgt_ref, f_ref, dh_ref, loss_ref):
        @pl.when(pl.program_id(0) == 0)
        def _():
            loss_ref[...] = jnp.zeros_like(loss_ref)
        acc = _dot(hid_ref[...], wd_ref[...], NN)
        f_ref[...] = acc
        e = _rms_residual(acc, hin_ref[...], gp_ref[...], FFN_RESIDUAL_WEIGHT) - tgt_ref[...]
        dh_ref[...] = e * (1.0 / d)
        per_row = jnp.sum(e * e, axis=1, keepdims=True) * (1.0 / d)
        loss_ref[...] += 0.5 * jnp.sum(per_row, axis=0, keepdims=True)

    return _hosted_call(body_loss, grid=(t // tm,), in_specs=in_specs + [row], out_specs=[row, row, _const_spec((1, LANES))],
                        out_shape=[shp, shp, jax.ShapeDtypeStruct((1, LANES), F32)], scratch_shapes=[],
                        operands=[hid, wd, h_in, gp, target], side=side, name=name)


def _post_bwd(dh, f, gp, weight, *, name):
    t, d = f.shape
    tm = _tile(t, 512)

    def body(dh_ref, f_ref, gp_ref, df_ref, dgp_ref):
        @pl.when(pl.program_id(0) == 0)
        def _():
            dgp_ref[...] = jnp.zeros_like(dgp_ref)
        fv = f_ref[...]
        dy = weight * dh_ref[...]
        r = lax.rsqrt(jnp.mean(fv * fv, axis=-1, keepdims=True) + RMS_EPS)
        fhat = fv * r
        dyg = dy * gp_ref[...]
        df_ref[...] = (r * (dyg - fhat * jnp.mean(dyg * fhat, axis=-1, keepdims=True))).astype(BF16)
        dgp_ref[...] += jnp.sum(dy * fhat, axis=0, keepdims=True)

    row = pl.BlockSpec((tm, d), lambda i: (i, 0))
    return pl.pallas_call(body, out_shape=(jax.ShapeDtypeStruct((t, d), BF16), jax.ShapeDtypeStruct((1, d), F32)), grid=(t // tm,),
                          in_specs=[row, row, _const_spec((1, d))], out_specs=(row, _const_spec((1, d))),
                          name=name, compiler_params=_params("arbitrary"))(dh, f, gp)


def _ffn_dhid(df, wd, g, u, *, name):
    t, d = df.shape
    f = wd.shape[0]
    tm, tn = _tile(t, 512), _tile(f, FFN_COL_TILE, LANES)

    def body(df_ref, wd_ref, g_ref, u_ref, dg_ref, du_ref):
        dh = _dot(df_ref[...], wd_ref[...], NT)
        gv = g_ref[...].astype(F32)
        uv = u_ref[...].astype(F32)
        sg = _sigmoid(gv)
        silu = gv * sg
        dg_ref[...] = (dh * uv * (sg + silu * (1.0 - sg))).astype(BF16)
        du_ref[...] = (dh * silu).astype(BF16)

    o_spec = pl.BlockSpec((tm, tn), lambda j, i: (i, j))
    shp = jax.ShapeDtypeStruct((t, f), BF16)
    return pl.pallas_call(body, out_shape=(shp, shp), grid=(f // tn, t // tm),
                          in_specs=[pl.BlockSpec((tm, d), lambda j, i: (i, 0)), pl.BlockSpec((tn, d), lambda j, i: (j, 0)), o_spec, o_spec],
                          out_specs=(o_spec, o_spec), name=name, compiler_params=_params("parallel", "arbitrary"))(df, wd, g, u)


def _ffn_forward(h_in, g_pre, w_up, wd, g_post, tag, side_norm=None, side_up=None, target=None):
    n, got_norm = _rms_fwd(h_in, g_pre, side=side_norm, name=f"{tag}_prenorm")
    wgt, wut = w_up(got_norm) if callable(w_up) else w_up
    (g, u, hid), got_up = _ffn_up(n, wgt, wut, side=side_up, name=f"{tag}_up")
    wd = wd(got_up) if callable(wd) else wd
    outs, _ = _ffn_down(hid, wd, h_in, g_post, target=target, name=f"{tag}_down")
    return outs[1:], (h_in, n, g, u, hid, outs[0]), (wgt, wut, wd), got_up


def _stack8(g):
    return g.reshape(N_DEV, g.shape[0] // N_DEV, g.shape[1])


def _ffn_backward(dh_out, saved, g_pre, wgt, wut, wd, g_post, tag, chain=False):
    h_in, n, g, u, hid, f = saved

    def side_of(grad16):
        return _scatter_exchange([_stack8(grad16)]) if chain else None

    df, dg_post = _post_bwd(dh_out, f, g_post, FFN_RESIDUAL_WEIGHT, name=f"{tag}_post_bwd")
    dgate, dup = _ffn_dhid(df, wd, g, u, name=f"{tag}_dhid")
    d_wd, d_wd16, _ = _mm_tn([hid], df, name=f"{tag}_dwd")
    d_wgt, d_wgt16, got_wd = _mm_tn([dgate], n, side=side_of(d_wd16), name=f"{tag}_dwg")
    d_wut, d_wut16, got_wg = _mm_tn([dup], n, side=side_of(d_wgt16), name=f"{tag}_dwu")
    dh_in, dg_pre, got_wu = _mm_nn_rmsbwd([(dgate, 0, 0), (dup, 1, 0)], [wgt, wut], h_in, g_pre, dh_out, side=side_of(d_wut16),
                                          name=f"{tag}_dn")
    received = (got_wg[0], got_wu[0], got_wd[0]) if chain else None
    return dh_in, dg_pre, dg_post, (d_wgt, d_wut, d_wd), (d_wgt16, d_wut16, d_wd16), received


CONV_ROWS = 128
HALO = 8


def _taps(w_ref):
    return [w_ref[k:k + 1, :] for k in range(SSM_CONV)]


def _conv_chunk(x_ref, xs, r0, taps, bias):
    xs[HALO + r0:HALO + r0 + CONV_ROWS, :] = x_ref[r0:r0 + CONV_ROWS, :].astype(F32)
    shifted = [xs[HALO + r0 - k:HALO + r0 - k + CONV_ROWS, :] for k in range(SSM_CONV)]
    pre = bias + shifted[0] * taps[SSM_CONV - 1]
    for k in range(1, SSM_CONV):
        pre = pre + shifted[k] * taps[SSM_CONV - 1 - k]
    return shifted, pre


def _fold_rows(a):
    return functools.reduce(jnp.add, [a[i:i + 8] for i in range(0, a.shape[0], 8)])


def _conv_fwd(xbc, conv_w, conv_b, bl, *, name):
    t, c = xbc.shape
    s = t // bl
    tc = LANES
    assert s % CONV_ROWS == 0

    def body(x_ref, w_ref, b_ref, o_ref, xs):
        taps, bias = _taps(w_ref), b_ref[...]
        xs[0:HALO, :] = jnp.zeros((HALO, tc), F32)
        for r0 in range(0, s, CONV_ROWS):
            _, pre = _conv_chunk(x_ref, xs, r0, taps, bias)
            o_ref[r0:r0 + CONV_ROWS, :] = (pre * _sigmoid(pre)).astype(o_ref.dtype)

    blk = pl.BlockSpec((s, tc), lambda b, j: (b, j))
    return pl.pallas_call(body, out_shape=jax.ShapeDtypeStruct((t, c), BF16), grid=(bl, c // tc),
                          in_specs=[blk, pl.BlockSpec((SSM_CONV, tc), lambda b, j: (0, j)), pl.BlockSpec((1, tc), lambda b, j: (0, j))],
                          out_specs=blk, scratch_shapes=[pltpu.VMEM((HALO + s, tc), F32)],
                          name=name, compiler_params=_params("parallel", "arbitrary"))(xbc, conv_w, conv_b)


def _conv_bwd(dxc, xbc, conv_w, conv_b, bl, *, name):
    t, c = xbc.shape
    s = t // bl
    tc = LANES

    def body(dy_ref, x_ref, w_ref, b_ref, dx_ref, dw_ref, db_ref, xs, dpre_s):
        @pl.when(pl.program_id(1) == 0)
        def _():
            dw_ref[...] = jnp.zeros_like(dw_ref)
            db_ref[...] = jnp.zeros_like(db_ref)

        taps, bias = _taps(w_ref), b_ref[...]
        zero8 = jnp.zeros((HALO, tc), F32)
        xs[0:HALO, :] = zero8
        dpre_s[s:s + HALO, :] = zero8
        sums = [zero8] * (SSM_CONV + 1)
        for r0 in range(0, s, CONV_ROWS):
            shifted, pre = _conv_chunk(x_ref, xs, r0, taps, bias)
            sg = _sigmoid(pre)
            dpre = dy_ref[r0:r0 + CONV_ROWS, :].astype(F32) * (sg * (1.0 + pre * (1.0 - sg)))
            dpre_s[r0:r0 + CONV_ROWS, :] = dpre
            sums = [acc + _fold_rows(dpre * sh) for acc, sh in zip(sums[:-1], shifted)] + [sums[-1] + _fold_rows(dpre)]
        for k in range(SSM_CONV):
            dw_ref[SSM_CONV - 1 - k:SSM_CONV - k, :] += jnp.sum(sums[k], axis=0, keepdims=True)
        db_ref[...] += jnp.sum(sums[-1], axis=0, keepdims=True)
        for r0 in range(0, s, CONV_ROWS):
            dx = dpre_s[r0:r0 + CONV_ROWS, :] * taps[SSM_CONV - 1]
            for k in range(1, SSM_CONV):
                dx = dx + dpre_s[r0 + k:r0 + k + CONV_ROWS, :] * taps[SSM_CONV - 1 - k]
            dx_ref[r0:r0 + CONV_ROWS, :] = dx.astype(dx_ref.dtype)

    blk = pl.BlockSpec((s, tc), lambda j, b: (b, j))
    return pl.pallas_call(
        body, out_shape=(jax.ShapeDtypeStruct((t, c), BF16), jax.ShapeDtypeStruct((8, c), F32), jax.ShapeDtypeStruct((1, c), F32)),
        grid=(c // tc, bl),
        in_specs=[blk, blk, pl.BlockSpec((SSM_CONV, tc), lambda j, b: (0, j)), pl.BlockSpec((1, tc), lambda j, b: (0, j))],
        out_specs=(blk, pl.BlockSpec((8, tc), lambda j, b: (0, j)), pl.BlockSpec((1, tc), lambda j, b: (0, j))),
        scratch_shapes=[pltpu.VMEM((HALO + s, tc), F32), pltpu.VMEM((s + HALO, tc), F32)],
        name=name, compiler_params=_params("parallel", "arbitrary"))(dxc, xbc, conv_w, conv_b)


def _softplus(x):
    return jnp.maximum(x, 0.0) + jnp.log1p(jnp.exp(-jnp.abs(x)))


def _hilo_dot(v, m_b, dims=NN):
    hi = v.astype(BF16)
    lo = (v - hi.astype(F32)).astype(BF16)
    return _dot(hi, m_b, dims) + _dot(lo, m_b, dims)


def _ssd_chunk_common(dtraw_ref, dtb_ref, alog_ref, dsk_ref, d_inner):
    q, p = CHUNK, SSM_HEAD_DIM
    shift = p.bit_length() - 1
    assert 1 << shift == p
    dt = _softplus(dtraw_ref[...] + dtb_ref[...])
    a = -jnp.exp(alog_ref[...])
    ii = lax.broadcasted_iota(jnp.int32, (q, q), 0)
    jj = lax.broadcasted_iota(jnp.int32, (q, q), 1)
    causal = ii >= jj
    tril = jnp.where(causal, 1.0, 0.0).astype(F32)
    triu = jnp.where(ii <= jj, 1.0, 0.0).astype(F32)
    a_cs = _dot_hi(tril, dt * a)
    a_cs_t = a_cs.T
    a_last = a_cs[q - 1:q, :]
    e_col = jnp.exp(a_cs)
    dec_end = jnp.exp(a_last - a_cs)
    head_of_col = lax.shift_right_logical(lax.broadcasted_iota(jnp.int32, (LANES, d_inner), 1), shift)
    spread = (lax.broadcasted_iota(jnp.int32, (LANES, d_inner), 0) == head_of_col).astype(BF16)
    exact = jnp.concatenate([dt, jnp.broadcast_to(dsk_ref[...], (8, LANES))], axis=0)
    hi = jnp.concatenate([exact, e_col, dec_end], axis=0).astype(BF16)
    lo = (exact - hi[:q + 8].astype(F32)).astype(BF16)
    wide = _dot(hi, spread, NN)
    fine = wide[:q + 8] + _dot(lo, spread, NN)
    return dict(dt=dt, a=a, a_cs=a_cs, a_cs_t=a_cs_t, a_last=a_last, dec_end=dec_end, causal=causal, triu=triu,
                dt_e=fine[:q], dsk_e=fine[q:q + 1], e_e=wide[q + 8:2 * q + 8], dec_e=wide[2 * q + 8:3 * q + 8])


def _fill_block_diag(bd_ref, src_ref, hpg, col0=0):
    q, p = CHUNK, SSM_HEAD_DIM
    for hh in range(hpg):
        bd_ref[hh * q:(hh + 1) * q, hh * p:(hh + 1) * p] = src_ref[:, col0 + hh * p:col0 + (hh + 1) * p]


def _lane_onehot(h):
    return (lax.broadcasted_iota(jnp.int32, (1, LANES), 1) == h).astype(F32)


def _ssd_fwd(xc, dt_raw, dt_bias, a_log, d_skip, bl, n_heads, *, side=None, name):
    t = xc.shape[0]
    q, p, nst, grp = CHUNK, SSM_HEAD_DIM, SSM_STATE, SSM_GROUPS
    d_inner = n_heads * p
    hpg = n_heads // grp
    hb = min(hpg, SSD_HEAD_BATCH)
    gw = hpg * p
    nc = t // bl // q
    assert d_inner % (grp * nst) == 0 and nst == LANES and hpg % hb == 0

    def body(xs_ref, b_ref, c_ref, dtraw_ref, dtb_ref, alog_ref, dsk_ref, y_ref, hprev_ref, state, m_all, x_bd, xdt_s):
        @pl.when(jnp.logical_and(pl.program_id(0) == 0, pl.program_id(1) == 0))
        def _():
            x_bd[...] = jnp.zeros_like(x_bd)

        @pl.when(pl.program_id(1) == 0)
        def _():
            state[...] = jnp.zeros_like(state)

        cm = _ssd_chunk_common(dtraw_ref, dtb_ref, alog_ref, dsk_ref, d_inner)
        for g in range(grp):
            cols = slice(g * gw, (g + 1) * gw)
            bg = b_ref[:, g * nst:(g + 1) * nst]
            cg = c_ref[:, g * nst:(g + 1) * nst]
            scores = _dot(cg, bg, NT)
            xs = xs_ref[:, cols].astype(F32)
            xdt = xs * cm['dt_e'][:, cols]
            xdt_s[...] = xdt.astype(BF16)
            y_parts = []
            for sub in range(hpg // hb):
                for k in range(hb):
                    h = g * hpg + sub * hb + k
                    seg = cm['a_cs'][:, h:h + 1] - cm['a_cs_t'][h:h + 1, :]
                    m_all[:, k * q:(k + 1) * q] = (scores * jnp.exp(jnp.where(cm['causal'], seg, NEG_INF))).astype(BF16)
                _fill_block_diag(x_bd, xdt_s, hb, sub * hb * p)
                y_parts.append(_dot(m_all[...], x_bd[...], NN))
            hprev = state[g]
            hprev_ref[g] = hprev
            y = jnp.concatenate(y_parts, axis=1) + cm['e_e'][:, cols] * _dot(cg, hprev.astype(BF16), NT)
            y_ref[:, cols] = y + cm['dsk_e'][:, cols] * xs
            st = _dot((xdt * cm['dec_e'][:, cols]).astype(BF16), bg, TN)
            for hh in range(hpg):
                h = g * hpg + hh
                rows = slice(hh * p, (hh + 1) * p)
                state[g, rows, :] = jnp.exp(cm['a_last'][:, h:h + 1]) * hprev[rows] + st[rows]

    gn = grp * nst

    def rowmap(b, c):
        return b * nc + c
    vec = pl.BlockSpec((1, LANES), lambda b, c: (0, 0))
    return _hosted_call(
        body,
        out_shape=[jax.ShapeDtypeStruct((t, d_inner), F32), jax.ShapeDtypeStruct((t // q, grp, gw, nst), F32)],
        grid=(bl, nc),
        in_specs=[pl.BlockSpec((q, d_inner), lambda b, c: (rowmap(b, c), 0)),
                  pl.BlockSpec((q, gn), lambda b, c: (rowmap(b, c), d_inner // gn)),
                  pl.BlockSpec((q, gn), lambda b, c: (rowmap(b, c), d_inner // gn + 1)),
                  pl.BlockSpec((q, LANES), lambda b, c: (rowmap(b, c), 0)), vec, vec, vec],
        out_specs=[pl.BlockSpec((q, d_inner), lambda b, c: (rowmap(b, c), 0)),
                   pl.BlockSpec((None, grp, gw, nst), lambda b, c: (rowmap(b, c), 0, 0, 0))],
        scratch_shapes=[pltpu.VMEM((grp, gw, nst), F32), pltpu.VMEM((q, hb * q), BF16), pltpu.VMEM((hb * q, hb * p), BF16),
                        pltpu.VMEM((q, gw), BF16)],
        operands=[xc, xc, xc, dt_raw, dt_bias, a_log, d_skip], side=side, name=name)


def _ssd_bwd(dy, y, xc, dt_raw, hprev_all, dt_bias, a_log, d_skip, bl, n_heads, *, side=None, name):
    t, c_dim = xc.shape
    q, p, nst, grp = CHUNK, SSM_HEAD_DIM, SSM_STATE, SSM_GROUPS
    d_inner = n_heads * p
    hpg = n_heads // grp
    hb = min(hpg, SSD_HEAD_BATCH)
    gw = hpg * p
    nc = t // bl // q
    gn = grp * nst
    shift = p.bit_length() - 1

    def body(dy_ref, y_ref, xs_ref, b_ref, c_ref, dtraw_ref, hprev_ref, dtb_ref, alog_ref, dsk_ref,
             dxc_ref, ddtraw_ref, ddtb_ref, dalog_ref, ddsk_ref, dstate, mt_all, x_bd, dy_bd, xdt_s):
        @pl.when(jnp.logical_and(pl.program_id(0) == 0, pl.program_id(1) == 0))
        def _():
            ddtb_ref[...] = jnp.zeros_like(ddtb_ref)
            dalog_ref[...] = jnp.zeros_like(dalog_ref)
            ddsk_ref[...] = jnp.zeros_like(ddsk_ref)
            x_bd[...] = jnp.zeros_like(x_bd)
            dy_bd[...] = jnp.zeros_like(dy_bd)

        @pl.when(pl.program_id(1) == 0)
        def _():
            dstate[...] = jnp.zeros_like(dstate)

        cm = _ssd_chunk_common(dtraw_ref, dtb_ref, alog_ref, dsk_ref, d_inner)
        causal = cm['causal']
        upper = cm['triu'] > 0.5
        seg_row = lax.shift_right_logical(lax.broadcasted_iota(jnp.int32, (gw, LANES), 0), shift)
        seg_lane = lax.broadcasted_iota(jnp.int32, (gw, LANES), 1)
        sums = jnp.zeros((5 * q, LANES), F32)
        state_dot = jnp.zeros((1, LANES), F32)
        for g in range(grp):
            cols = slice(g * gw, (g + 1) * gw)
            seg_sum = (seg_row + g * hpg == seg_lane).astype(BF16)
            bg = b_ref[:, g * nst:(g + 1) * nst]
            cg = c_ref[:, g * nst:(g + 1) * nst]
            scores_t = _dot(bg, cg, NT)
            xs = xs_ref[:, cols].astype(F32)
            xdt = xs * cm['dt_e'][:, cols]
            xdt_s[...] = xdt.astype(BF16)
            dyf = dy_ref[:, cols].astype(F32)
            dscores = jnp.zeros((q, q), F32)
            dx_parts = []
            for sub in range(hpg // hb):
                c0 = sub * hb * p
                _fill_block_diag(x_bd, xdt_s, hb, c0)
                _fill_block_diag(dy_bd, dy_ref, hb, g * gw + c0)
                dm_all = _dot(dy_ref[:, g * gw + c0:g * gw + c0 + hb * p], x_bd[...], NT)
                for k in range(hb):
                    h = g * hpg + sub * hb + k
                    blk = slice(k * q, (k + 1) * q)
                    seg = cm['a_cs'][:, h:h + 1] - cm['a_cs_t'][h:h + 1, :]
                    mt_all[:, blk] = (scores_t * jnp.exp(jnp.where(upper, -seg, NEG_INF))).astype(BF16)
                    dscores = dscores + dm_all[:, blk] * jnp.exp(jnp.where(causal, seg, NEG_INF))
                dx_parts.append(_dot(mt_all[...], dy_bd[...], NN))
            hprev = hprev_ref[g]
            hprev_b = hprev.astype(BF16)
            dhn = dstate[g]
            dhn_b = dhn.astype(BF16)
            e_e, dec_e = cm['e_e'][:, cols], cm['dec_e'][:, cols]
            y_scan = y_ref[:, cols] - cm['dsk_e'][:, cols] * xs
            dye_b = (dyf * e_e).astype(BF16)
            dcg = _dot(dye_b, hprev_b, NN)
            dhp = _dot(dye_b, cg, TN)
            bdh = _dot(bg, dhn_b, NT)
            dbg = _dot((xdt * dec_e).astype(BF16), dhn_b, NN)
            dx_diag = jnp.concatenate(dx_parts, axis=1)
            dx = dec_e * bdh + dx_diag
            ds_b = dscores.astype(BF16)
            dcg = dcg + _dot(ds_b, bg, NN)
            dbg = dbg + _dot(ds_b, cg, TN)
            x_rounded = xdt_s[...].astype(F32)
            sums = sums + _hilo_dot(jnp.concatenate([dyf * y_scan, xdt * bdh, x_rounded * dx_diag, dx * xs, dyf * xs], axis=0), seg_sum)
            state_dot = state_dot + jnp.sum(_hilo_dot(dhn * hprev, seg_sum, TN), axis=0, keepdims=True)
            dxc_ref[:, cols] = (dx * cm['dt_e'][:, cols] + cm['dsk_e'][:, cols] * dyf).astype(dxc_ref.dtype)
            dxc_ref[:, d_inner + g * nst:d_inner + (g + 1) * nst] = dbg.astype(dxc_ref.dtype)
            dxc_ref[:, d_inner + gn + g * nst:d_inner + gn + (g + 1) * nst] = dcg.astype(dxc_ref.dtype)
            for hh in range(hpg):
                h = g * hpg + hh
                rows = slice(hh * p, (hh + 1) * p)
                dstate[g, rows, :] = jnp.exp(cm['a_last'][:, h:h + 1]) * dhn[rows] + dhp[rows]
        s_y, s_end, s_diag, s_dt, s_skip = (sums[k * q:(k + 1) * q] for k in range(5))
        dt, a, dec_end = cm['dt'], cm['a'], cm['dec_end']
        last_row = (lax.broadcasted_iota(jnp.int32, (q, 1), 0) == q - 1).astype(F32)
        da_last = jnp.sum(dec_end * s_end, axis=0, keepdims=True) + jnp.exp(cm['a_last']) * state_dot
        da = s_y - dec_end * s_end - s_diag + last_row * da_last
        ddta = _dot_hi(cm['triu'], da)
        ddt = s_dt + ddta * a
        d_a = jnp.sum(ddta * dt, axis=0, keepdims=True)
        ddt_raw = ddt * _sigmoid(dtraw_ref[...] + dtb_ref[...])
        ddtraw_ref[...] = ddt_raw
        ddtb_ref[...] += jnp.sum(ddt_raw, axis=0, keepdims=True)
        dalog_ref[...] += d_a * a
        ddsk_ref[...] += jnp.sum(s_skip, axis=0, keepdims=True)

    def rowmap(b, c):
        return b * nc + (nc - 1 - c)
    vec = pl.BlockSpec((1, LANES), lambda b, c: (0, 0))
    vec_shape = jax.ShapeDtypeStruct((1, LANES), F32)
    return _hosted_call(
        body,
        out_shape=[jax.ShapeDtypeStruct((t, c_dim), BF16), jax.ShapeDtypeStruct((t, LANES), F32), vec_shape, vec_shape, vec_shape],
        grid=(bl, nc),
        in_specs=[pl.BlockSpec((q, d_inner), lambda b, c: (rowmap(b, c), 0)),
                  pl.BlockSpec((q, d_inner), lambda b, c: (rowmap(b, c), 0)),
                  pl.BlockSpec((q, d_inner), lambda b, c: (rowmap(b, c), 0)),
                  pl.BlockSpec((q, gn), lambda b, c: (rowmap(b, c), d_inner // gn)),
                  pl.BlockSpec((q, gn), lambda b, c: (rowmap(b, c), d_inner // gn + 1)),
                  pl.BlockSpec((q, LANES), lambda b, c: (rowmap(b, c), 0)),
                  pl.BlockSpec((None, grp, gw, nst), lambda b, c: (rowmap(b, c), 0, 0, 0)), vec, vec, vec],
        out_specs=[pl.BlockSpec((q, c_dim), lambda b, c: (rowmap(b, c), 0)),
                   pl.BlockSpec((q, LANES), lambda b, c: (rowmap(b, c), 0)), vec, vec, vec],
        scratch_shapes=[pltpu.VMEM((grp, gw, nst), F32), pltpu.VMEM((q, hb * q), BF16),
                        pltpu.VMEM((hb * q, hb * p), BF16), pltpu.VMEM((hb * q, hb * p), BF16), pltpu.VMEM((q, gw), BF16)],
        operands=[dy, y, xc, xc, xc, dt_raw, hprev_all, dt_bias, a_log, d_skip], side=side, name=name)


def _gated_norm_fwd(y, z, ng, *, name):
    t, d = y.shape
    tm = _tile(t, 512)
    gw = d // SSM_GROUPS

    def body(y_ref, z_ref, ng_ref, o_ref):
        for g in range(SSM_GROUPS):
            sl = slice(g * gw, (g + 1) * gw)
            zv = z_ref[:, sl].astype(F32)
            yg = y_ref[:, sl] * (zv * _sigmoid(zv))
            r = lax.rsqrt(jnp.mean(yg * yg, axis=-1, keepdims=True) + RMS_EPS)
            o_ref[:, sl] = (yg * r * ng_ref[:, sl]).astype(o_ref.dtype)

    row = pl.BlockSpec((tm, d), lambda i: (i, 0))
    return pl.pallas_call(body, out_shape=jax.ShapeDtypeStruct((t, d), BF16), grid=(t // tm,),
                          in_specs=[row, row, _const_spec((1, d))], out_specs=row, name=name, compiler_params=_params("parallel"))(y, z, ng)


def _gated_norm_bwd(dyn, y, z, ng, *, name):
    t, d = y.shape
    tm = _tile(t, 512)
    gw = d // SSM_GROUPS

    def body(dyn_ref, y_ref, z_ref, ng_ref, dy_ref, dz_ref, dng_ref):
        @pl.when(pl.program_id(0) == 0)
        def _():
            dng_ref[...] = jnp.zeros_like(dng_ref)
        for g in range(SSM_GROUPS):
            sl = slice(g * gw, (g + 1) * gw)
            zv = z_ref[:, sl].astype(F32)
            yv = y_ref[:, sl]
            sg = _sigmoid(zv)
            sz = zv * sg
            yg = yv * sz
            r = lax.rsqrt(jnp.mean(yg * yg, axis=-1, keepdims=True) + RMS_EPS)
            yhat = yg * r
            dn = dyn_ref[:, sl].astype(F32)
            dyg_n = dn * ng_ref[:, sl]
            dyg = r * (dyg_n - yhat * jnp.mean(dyg_n * yhat, axis=-1, keepdims=True))
            dy_ref[:, sl] = (dyg * sz).astype(dy_ref.dtype)
            dz_ref[:, sl] = (dyg * yv * (sg * (1.0 + zv * (1.0 - sg)))).astype(dz_ref.dtype)
            dng_ref[:, sl] += jnp.sum(dn * yhat, axis=0, keepdims=True)

    row = pl.BlockSpec((tm, d), lambda i: (i, 0))
    shp = jax.ShapeDtypeStruct((t, d), BF16)
    return pl.pallas_call(body, out_shape=(shp, shp, jax.ShapeDtypeStruct((1, d), F32)), grid=(t // tm,),
                          in_specs=[row, row, row, _const_spec((1, d))], out_specs=(row, row, _const_spec((1, d))),
                          name=name, compiler_params=_params("arbitrary"))(dyn, y, z, ng)


def _bucket_onehot():
    blk = CHUNK
    qi = jnp.arange(blk)[:, None]
    kj = jnp.arange(2 * blk)[None, :]
    dist = jnp.maximum(qi + blk - kj, 0)
    max_exact = REL_BUCKETS // 2
    d = jnp.maximum(dist, 1).astype(F32)
    large = max_exact + (jnp.log(d / max_exact) / math.log(REL_MAX_DISTANCE / max_exact) * (REL_BUCKETS - max_exact)).astype(jnp.int32)
    large = jnp.minimum(large, REL_BUCKETS - 1)
    bucket = jnp.where(dist < max_exact, dist, large).reshape(-1)
    return (bucket[None, :] == jnp.arange(REL_BUCKETS)[:, None]).astype(F32)


def _small_mm_hi(a, b, dims, *, name):
    def body(a_ref, b_ref, o_ref):
        o_ref[...] = _dot_hi(a_ref[...], b_ref[...], dims)
    n = b.shape[0] if dims == NT else b.shape[1]
    return pl.pallas_call(body, out_shape=jax.ShapeDtypeStruct((a.shape[0], n), F32), name=name)(a, b)


def _attn_band_mask_t(n, rep):
    blk = CHUNK
    jj = lax.broadcasted_iota(jnp.int32, (2 * blk, rep * blk), 0)
    ii = lax.broadcasted_iota(jnp.int32, (2 * blk, rep * blk), 1) & (blk - 1)
    dist = ii + blk - jj
    in_window = jnp.logical_and(dist >= 0, dist < blk)
    return jnp.logical_and(in_window, jnp.logical_or(jj >= blk, n > 0))


def _sink_row(sink_ref, heads):
    return jnp.concatenate([jnp.broadcast_to(sink_ref[:, h:h + 1], (1, CHUNK)) for h in heads], axis=1)


def _attn_fwd(q, kv, bias_t, sinks, bl, *, name):
    t, qd = q.shape
    blk, hd = CHUNK, ATTN_HEAD_DIM
    kvd = ATTN_KV_HEADS * hd
    rep = ATTN_Q_HEADS // ATTN_KV_HEADS
    nb = t // bl // blk
    scale = hd ** -0.5

    def body(q_ref, kp_ref, kc_ref, vp_ref, vc_ref, bias_ref, sink_ref, o_ref, lse_ref):
        n = pl.program_id(1)
        mask = _attn_band_mask_t(n, rep)
        for kvh in range(ATTN_KV_HEADS):
            ks = slice(kvh * hd, (kvh + 1) * hd)
            heads = range(kvh * rep, (kvh + 1) * rep)
            qs = jnp.concatenate([q_ref[:, h * hd:(h + 1) * hd] for h in heads], axis=0)
            kk = jnp.concatenate([kp_ref[:, ks], kc_ref[:, ks]], axis=0)
            vv = jnp.concatenate([vp_ref[:, ks], vc_ref[:, ks]], axis=0)
            s = jnp.where(mask, _dot(kk, qs, NT) * scale + bias_ref[kvh], NEG_INF)
            sink = _sink_row(sink_ref, heads)
            m = jnp.maximum(jnp.max(s, axis=0, keepdims=True), sink)
            p = jnp.exp(s - m)
            den = jnp.sum(p, axis=0, keepdims=True) + jnp.exp(sink - m)
            o = _dot((p * (1.0 / den)).astype(BF16), vv, TN)
            lse = m + jnp.log(den)
            for r, h in enumerate(heads):
                o_ref[:, h * hd:(h + 1) * hd] = o[r * blk:(r + 1) * blk].astype(o_ref.dtype)
                lse_ref[h:h + 1, :] = lse[:, r * blk:(r + 1) * blk]

    def cur(b, n):
        return b * nb + n

    def prev(b, n):
        return b * nb + jnp.maximum(n - 1, 0)
    return pl.pallas_call(
        body, out_shape=(jax.ShapeDtypeStruct((t, qd), BF16), jax.ShapeDtypeStruct((t // blk * ATTN_Q_HEADS, blk), F32)), grid=(bl, nb),
        in_specs=[pl.BlockSpec((blk, qd), lambda b, n: (cur(b, n), 0)),
                  pl.BlockSpec((blk, kvd), lambda b, n: (prev(b, n), 0)), pl.BlockSpec((blk, kvd), lambda b, n: (cur(b, n), 0)),
                  pl.BlockSpec((blk, kvd), lambda b, n: (prev(b, n), 1)), pl.BlockSpec((blk, kvd), lambda b, n: (cur(b, n), 1)),
                  _const_spec(bias_t.shape), _const_spec((1, LANES))],
        out_specs=(pl.BlockSpec((blk, qd), lambda b, n: (cur(b, n), 0)),
                   pl.BlockSpec((ATTN_Q_HEADS, blk), lambda b, n: (cur(b, n), 0))),
        name=name, compiler_params=_params("parallel", "arbitrary"))(q, kv, kv, kv, kv, bias_t, sinks)


def _attn_bwd(do, q, kv, lse, bias_t, sinks, bl, *, name):
    t, qd = q.shape
    blk, hd = CHUNK, ATTN_HEAD_DIM
    kvd = ATTN_KV_HEADS * hd
    rep = ATTN_Q_HEADS // ATTN_KV_HEADS
    s_len = t // bl
    nb = s_len // blk
    scale = hd ** -0.5

    def body(do_ref, q_ref, kp_ref, kc_ref, vp_ref, vc_ref, lse_ref, bias_ref, sink_ref, dq_ref, dkv_ref, dbias_ref, dsink_ref):
        n = pl.program_id(1)

        @pl.when(jnp.logical_and(pl.program_id(0) == 0, n == 0))
        def _():
            dbias_ref[...] = jnp.zeros_like(dbias_ref)
            dsink_ref[...] = jnp.zeros_like(dsink_ref)

        mask = _attn_band_mask_t(n, rep)
        r_cur = pl.multiple_of(n * blk, blk)
        r_prev = pl.multiple_of(jnp.maximum(n - 1, 0) * blk, blk)
        dsink = jnp.zeros((1, LANES), F32)
        for kvh in range(ATTN_KV_HEADS):
            ks = slice(kvh * hd, (kvh + 1) * hd)
            heads = range(kvh * rep, (kvh + 1) * rep)
            qs = jnp.concatenate([q_ref[:, h * hd:(h + 1) * hd] for h in heads], axis=0)
            dos = jnp.concatenate([do_ref[:, h * hd:(h + 1) * hd] for h in heads], axis=0)
            kk = jnp.concatenate([kp_ref[:, ks], kc_ref[:, ks]], axis=0)
            vv = jnp.concatenate([vp_ref[:, ks], vc_ref[:, ks]], axis=0)
            lse = jnp.concatenate([lse_ref[h:h + 1, :] for h in heads], axis=1)
            p = jnp.exp(jnp.where(mask, _dot(kk, qs, NT) * scale + bias_ref[kvh], NEG_INF) - lse)
            dp = _dot(vv, dos, NT)
            delta = jnp.sum(p * dp, axis=0, keepdims=True)
            ds = p * (dp - delta)
            dsink_row = jnp.exp(_sink_row(sink_ref, heads) - lse) * delta
            dbias_ref[kvh] += ds
            ds_b = ds.astype(BF16)
            dq_s = _dot(ds_b, kk, TN) * scale
            dkk = _dot(ds_b, qs, NN) * scale
            dvv = _dot(p.astype(BF16), dos, NN)
            for r, h in enumerate(heads):
                dq_ref[:, h * hd:(h + 1) * hd] = dq_s[r * blk:(r + 1) * blk].astype(dq_ref.dtype)
                dsink = dsink - jnp.sum(dsink_row[:, r * blk:(r + 1) * blk], axis=1, keepdims=True) * _lane_onehot(h)
            vs = slice(kvd + kvh * hd, kvd + (kvh + 1) * hd)
            dkv_ref[pl.ds(r_cur, blk), ks] = dkk[blk:]
            dkv_ref[pl.ds(r_cur, blk), vs] = dvv[blk:]

            @pl.when(n > 0)
            def _():
                dkv_ref[pl.ds(r_prev, blk), ks] += dkk[:blk]
                dkv_ref[pl.ds(r_prev, blk), vs] += dvv[:blk]
        dsink_ref[...] += dsink

    def cur(b, n):
        return b * nb + n

    def prev(b, n):
        return b * nb + jnp.maximum(n - 1, 0)
    qspec = pl.BlockSpec((blk, qd), lambda b, n: (cur(b, n), 0))
    return pl.pallas_call(
        body,
        out_shape=(jax.ShapeDtypeStruct((t, qd), BF16), jax.ShapeDtypeStruct((t, 2 * kvd), F32),
                   jax.ShapeDtypeStruct(bias_t.shape, F32), jax.ShapeDtypeStruct((1, LANES), F32)),
        grid=(bl, nb),
        in_specs=[qspec, qspec,
                  pl.BlockSpec((blk, kvd), lambda b, n: (prev(b, n), 0)), pl.BlockSpec((blk, kvd), lambda b, n: (cur(b, n), 0)),
                  pl.BlockSpec((blk, kvd), lambda b, n: (prev(b, n), 1)), pl.BlockSpec((blk, kvd), lambda b, n: (cur(b, n), 1)),
                  pl.BlockSpec((ATTN_Q_HEADS, blk), lambda b, n: (cur(b, n), 0)), _const_spec(bias_t.shape), _const_spec((1, LANES))],
        out_specs=(qspec, pl.BlockSpec((s_len, 2 * kvd), lambda b, n: (b, 0)), _const_spec(bias_t.shape), _const_spec((1, LANES))),
        name=name, compiler_params=_params("arbitrary", "arbitrary"))(do, q, kv, kv, kv, kv, lse, bias_t, sinks)


def _merge_fwd(yn, o, gs, ga, w_ssm, w_attn, w_out, h_in, g_post, *, name):
    t, d = h_in.shape
    tm = _tile(t, 256)

    def body(yn_ref, o_ref, gs_ref, ga_ref, ws_ref, wa_ref, wo_ref, hin_ref, gp_ref, ys_ref, ya_ref, mg_ref, mix_ref, hout_ref):
        ys = _dot(yn_ref[...], ws_ref[...], NN)
        ya = _dot(o_ref[...], wa_ref[...], NN)
        merged = (_sigmoid(gs_ref[...].astype(F32)) * ys + _sigmoid(ga_ref[...].astype(F32)) * ya).astype(BF16)
        mix = _dot(merged, wo_ref[...], NN)
        ys_ref[...] = ys.astype(BF16)
        ya_ref[...] = ya.astype(BF16)
        mg_ref[...] = merged
        mix_ref[...] = mix
        hout_ref[...] = _rms_residual(mix, hin_ref[...], gp_ref[...], 1.0)

    def row(w):
        return pl.BlockSpec((tm, w), lambda i: (i, 0))
    bshape = jax.ShapeDtypeStruct((t, d), BF16)
    fshape = jax.ShapeDtypeStruct((t, d), F32)
    return pl.pallas_call(
        body, out_shape=(bshape, bshape, bshape, fshape, fshape), grid=(t // tm,),
        in_specs=[row(yn.shape[1]), row(o.shape[1]), row(d), row(d), _resident_spec(w_ssm.shape), _resident_spec(w_attn.shape),
                  _resident_spec(w_out.shape), row(d), _const_spec((1, d))],
        out_specs=(row(d),) * 5, name=name, compiler_params=_params("parallel"))(yn, o, gs, ga, w_ssm, w_attn, w_out, h_in, g_post)


def _merge_bwd(dh, mix, g_post, gs, ga, ys, ya, w_ssm, w_attn, w_out, *, name):
    t, d = mix.shape
    tm = _tile(t, 256)
    d_ssm, d_attn = w_ssm.shape[0], w_attn.shape[0]

    def body(dh_ref, mix_ref, gp_ref, gs_ref, ga_ref, ys_ref, ya_ref, ws_ref, wa_ref, wo_ref,
             dmix_ref, dys_ref, dya_ref, dgs_ref, dga_ref, dyn_ref, do_ref, dgp_ref):
        @pl.when(pl.program_id(0) == 0)
        def _():
            dgp_ref[...] = jnp.zeros_like(dgp_ref)
        mv = mix_ref[...]
        dy = dh_ref[...]
        r = lax.rsqrt(jnp.mean(mv * mv, axis=-1, keepdims=True) + RMS_EPS)
        mhat = mv * r
        dyg = dy * gp_ref[...]
        dmix = (r * (dyg - mhat * jnp.mean(dyg * mhat, axis=-1, keepdims=True))).astype(BF16)
        dgp_ref[...] += jnp.sum(dy * mhat, axis=0, keepdims=True)
        dmix_ref[...] = dmix
        dmerged = _dot(dmix, wo_ref[...], NT)
        sgs = _sigmoid(gs_ref[...].astype(F32))
        sga = _sigmoid(ga_ref[...].astype(F32))
        dys = (dmerged * sgs).astype(BF16)
        dya = (dmerged * sga).astype(BF16)
        dys_ref[...] = dys
        dya_ref[...] = dya
        dgs_ref[...] = (dmerged * ys_ref[...].astype(F32) * sgs * (1.0 - sgs)).astype(BF16)
        dga_ref[...] = (dmerged * ya_ref[...].astype(F32) * sga * (1.0 - sga)).astype(BF16)
        dyn_ref[...] = _dot(dys, ws_ref[...], NT).astype(BF16)
        do_ref[...] = _dot(dya, wa_ref[...], NT).astype(BF16)

    def row(w):
        return pl.BlockSpec((tm, w), lambda i: (i, 0))

    def bshape(w):
        return jax.ShapeDtypeStruct((t, w), BF16)
    return pl.pallas_call(
        body, out_shape=(bshape(d),) * 5 + (bshape(d_ssm), bshape(d_attn), jax.ShapeDtypeStruct((1, d), F32)), grid=(t // tm,),
        in_specs=[row(d), row(d), _const_spec((1, d)), row(d), row(d), row(d), row(d),
                  _resident_spec(w_ssm.shape), _resident_spec(w_attn.shape), _resident_spec(w_out.shape)],
        out_specs=(row(d),) * 5 + (row(d_ssm), row(d_attn), _const_spec((1, d))),
        name=name, compiler_params=_params("arbitrary"))(dh, mix, g_post, gs, ga, ys, ya, w_ssm, w_attn, w_out)


def _position():
    return lax.axis_index("x"), lax.axis_index("y"), lax.axis_index("c")


def _gather_exchange(shards):
    na = len(shards)

    def plan(ins, outs, sems):
        send_sems, recv_sems, local_sems = sems
        x, y, c = _position()
        me, sibling = (x, y, c), (x, y, 1 - c)
        chips = [(1 - x, y), (x, 1 - y), (1 - x, 1 - y)]

        def slot(a, pos):
            return outs[a].at[4 * pos[0] + 2 * pos[1] + pos[2]]

        def copy(a, k, block, to, src=None):
            return pltpu.make_async_remote_copy(
                src_ref=slot(a, block) if src is None else src, dst_ref=slot(a, block),
                send_sem=send_sems.at[a, k], recv_sem=recv_sems.at[a, k], device_id=to, device_id_type=MESH)

        mine = [pltpu.make_async_copy(ins[a], slot(a, me), local_sems.at[a]) for a in range(na)]
        first = []
        for a in range(na):
            first.append(copy(a, 0, me, sibling, src=ins[a]))
            first += [copy(a, 1 + j, me, (*chip, c), src=ins[a]) for j, chip in enumerate(chips)]
        return me, sibling, chips, copy, mine, first

    def start(ins, outs, sems):
        *_, mine, first = plan(ins, outs, sems)
        for cp in mine + first:
            cp.start()

    def finish(ins, outs, sems):
        me, sibling, chips, copy, mine, first = plan(ins, outs, sems)
        c = me[2]
        passed = []
        for a in range(na):
            for j, chip in enumerate(chips):
                copy(a, 1 + j, (*chip, c), me).wait_recv()
                fwd = copy(a, 4 + j, (*chip, c), sibling)
                fwd.start()
                passed.append(fwd)
        for a in range(na):
            copy(a, 0, sibling, me).wait_recv()
            for j, chip in enumerate(chips):
                copy(a, 4 + j, (*chip, 1 - c), me).wait_recv()
        for cp in first + passed:
            cp.wait_send()
        for cp in mine:
            cp.wait()

    return _Exchange(list(shards), [jax.ShapeDtypeStruct((N_DEV,) + s.shape, s.dtype) for s in shards],
                     [pltpu.SemaphoreType.DMA((na, 7)), pltpu.SemaphoreType.DMA((na, 7)), pltpu.SemaphoreType.DMA((na,))],
                     start, finish)


def _scatter_exchange(arrays):
    na = len(arrays)

    def copies(ins, outs, sems):
        send_sems, recv_sems = sems
        x, y, c = _position()
        out = []
        for a in range(na):
            for k in range(7):
                flip = k + 1
                peer = (x ^ (flip >> 2), y ^ ((flip >> 1) & 1), c ^ (flip & 1))
                peer_block = 4 * peer[0] + 2 * peer[1] + peer[2]
                out.append(pltpu.make_async_remote_copy(
                    src_ref=ins[a].at[peer_block], dst_ref=outs[a].at[k],
                    send_sem=send_sems.at[a, k], recv_sem=recv_sems.at[a, k], device_id=peer, device_id_type=MESH))
        return out

    def start(ins, outs, sems):
        for cp in copies(ins, outs, sems):
            cp.start()

    def finish(ins, outs, sems):
        for cp in copies(ins, outs, sems):
            cp.wait()

    return _Exchange(list(arrays), [jax.ShapeDtypeStruct((7,) + s.shape[1:], s.dtype) for s in arrays],
                     [pltpu.SemaphoreType.DMA((na, 7)), pltpu.SemaphoreType.DMA((na, 7))], start, finish)


def _reduce_adamw(own, recv, w, m, v, *, name):
    r, c = own.shape
    tm = _tile(r, 256)
    c1 = 1.0 - ADAM_B1 ** ADAM_STEP
    c2 = 1.0 - ADAM_B2 ** ADAM_STEP

    def body(own_ref, recv_ref, w_ref, m_ref, v_ref, g_ref, d_ref, mo_ref, vo_ref):
        gv = own_ref[...]
        for k in range(7):
            gv = gv + recv_ref[k].astype(F32)
        g_ref[...] = gv
        mn = ADAM_B1 * m_ref[...] + (1.0 - ADAM_B1) * gv
        vn = ADAM_B2 * v_ref[...] + (1.0 - ADAM_B2) * (gv * gv)
        mo_ref[...] = mn
        vo_ref[...] = vn
        d_ref[...] = -ADAM_LR * ((mn / c1) / (jnp.sqrt(vn / c2) + ADAM_EPS) + ADAM_WD * w_ref[...])

    blk = pl.BlockSpec((tm, c), lambda i: (i, 0))
    shp = jax.ShapeDtypeStruct((r, c), F32)
    return pl.pallas_call(body, out_shape=(shp,) * 4, grid=(r // tm,),
                          in_specs=[blk, pl.BlockSpec((7, tm, c), lambda i: (0, i, 0)), blk, blk, blk], out_specs=(blk,) * 4,
                          name=name, compiler_params=_params("parallel"))(own, recv, w, m, v)


SMALL_ROW = 8 * LANES


def _update_replicated(partials, ws, ms, vs, *, name):
    n = len(ws)
    pieces, n_rows = [], 0
    for k in sorted(range(n), key=lambda k_: -ws[k_].shape[0]):
        r, c = ws[k].shape
        assert (r == 1 or c <= SMALL_ROW) and (r == 1 or n_rows % 8 == 0)
        for c0 in range(0, c, SMALL_ROW):
            pieces.append((k, slice(0, r), slice(c0, min(c0 + SMALL_ROW, c)), n_rows))
            n_rows += r
    n_rows = -(-n_rows // 8) * 8
    c1 = 1.0 - ADAM_B1 ** ADAM_STEP
    c2 = 1.0 - ADAM_B2 ** ADAM_STEP

    def reduce_body(*refs):
        g_in, total, buf, send_sems, recv_sems = refs[:n], refs[n], refs[n + 1], refs[n + 2], refs[n + 3]
        x, y, c_ = _position()
        me = 4 * x + 2 * y + c_
        buf[me] = jnp.zeros((n_rows, SMALL_ROW), F32)
        for k, rs, cs, row0 in pieces:
            buf[me, row0:row0 + rs.stop, 0:cs.stop - cs.start] = g_in[k][rs, cs]
        copies = []
        for j in range(7):
            flip = j + 1
            peer = (x ^ (flip >> 2), y ^ ((flip >> 1) & 1), c_ ^ (flip & 1))
            cp = pltpu.make_async_remote_copy(
                src_ref=buf.at[me], dst_ref=buf.at[me], send_sem=send_sems.at[j], recv_sem=recv_sems.at[j],
                device_id=peer, device_id_type=MESH)
            cp.start()
            copies.append(cp)
        for cp in copies:
            cp.wait()
        acc = buf[0]
        for dev in range(1, N_DEV):
            acc = acc + buf[dev]
        total[...] = acc

    def step_body(*refs):
        total = refs[0]
        w_in_, m_in, v_in = (refs[1 + j * n:1 + (j + 1) * n] for j in range(3))
        g_out, d_out, m_out, v_out = (refs[1 + (3 + j) * n:1 + (4 + j) * n] for j in range(4))
        for k, rs, cs, row0 in pieces:
            gv = total[row0:row0 + rs.stop, 0:cs.stop - cs.start]
            mn = ADAM_B1 * m_in[k][rs, cs] + (1.0 - ADAM_B1) * gv
            vn = ADAM_B2 * v_in[k][rs, cs] + (1.0 - ADAM_B2) * (gv * gv)
            g_out[k][rs, cs] = gv
            m_out[k][rs, cs] = mn
            v_out[k][rs, cs] = vn
            d_out[k][rs, cs] = -ADAM_LR * ((mn / c1) / (jnp.sqrt(vn / c2) + ADAM_EPS) + ADAM_WD * w_in_[k][rs, cs])

    vm = pl.BlockSpec(memory_space=pltpu.VMEM)
    total = pl.pallas_call(
        reduce_body, out_shape=jax.ShapeDtypeStruct((n_rows, SMALL_ROW), F32), in_specs=[vm] * n, out_specs=vm,
        scratch_shapes=[pltpu.VMEM((N_DEV, n_rows, SMALL_ROW), F32), pltpu.SemaphoreType.DMA((7,)), pltpu.SemaphoreType.DMA((7,))],
        name=f"{name}_allreduce")(*partials)
    shapes = tuple(jax.ShapeDtypeStruct(w.shape, F32) for w in ws)
    outs = pl.pallas_call(step_body, out_shape=shapes * 4, in_specs=[vm] * (1 + 3 * n), out_specs=tuple([vm] * (4 * n)),
                          name=f"{name}_adamw")(total, *ws, *ms, *vs)
    return tuple(outs[j * n:(j + 1) * n] for j in range(4))


def _pad_lanes(v, width=LANES):
    return jnp.pad(v, ((0, 0), (0, width - v.shape[1])))


def kernel(x, ffn1_pre_g, ffn1_w_gate, ffn1_w_up, ffn1_w_down, ffn1_post_g, mix_pre_g, w_in, conv_w, conv_b, dt_bias, a_log, d_skip, ssm_norm_g, w_ssm_proj, attn_sinks, rel_bias_table, w_attn_proj, w_out, mix_post_g, ffn2_pre_g, ffn2_w_gate, ffn2_w_up, ffn2_w_down, ffn2_post_g, loss_target, m_ffn1_pre_g, m_ffn1_w_gate, m_ffn1_w_up, m_ffn1_w_down, m_ffn1_post_g, m_mix_pre_g, m_w_in, m_conv_w, m_conv_b, m_dt_bias, m_a_log, m_d_skip, m_ssm_norm_g, m_w_ssm_proj, m_attn_sinks, m_rel_bias_table, m_w_attn_proj, m_w_out, m_mix_post_g, m_ffn2_pre_g, m_ffn2_w_gate, m_ffn2_w_up, m_ffn2_w_down, m_ffn2_post_g, v_ffn1_pre_g, v_ffn1_w_gate, v_ffn1_w_up, v_ffn1_w_down, v_ffn1_post_g, v_mix_pre_g, v_w_in, v_conv_w, v_conv_b, v_dt_bias, v_a_log, v_d_skip, v_ssm_norm_g, v_w_ssm_proj, v_attn_sinks, v_rel_bias_table, v_w_attn_proj, v_w_out, v_mix_post_g, v_ffn2_pre_g, v_ffn2_w_gate, v_ffn2_w_up, v_ffn2_w_down, v_ffn2_post_g):
    args = dict(locals())
    weight_names = ['ffn1_pre_g', 'ffn1_w_gate', 'ffn1_w_up', 'ffn1_w_down', 'ffn1_post_g', 'mix_pre_g', 'w_in', 'conv_w', 'conv_b',
                    'dt_bias', 'a_log', 'd_skip', 'ssm_norm_g', 'w_ssm_proj', 'attn_sinks', 'rel_bias_table', 'w_attn_proj', 'w_out',
                    'mix_post_g', 'ffn2_pre_g', 'ffn2_w_gate', 'ffn2_w_up', 'ffn2_w_down', 'ffn2_post_g']
    col_sharded = ('ffn1_w_gate', 'ffn1_w_up', 'w_in', 'ffn2_w_gate', 'ffn2_w_up')
    row_sharded = ('ffn1_w_down', 'w_ssm_proj', 'w_attn_proj', 'w_out', 'ffn2_w_down')
    big = col_sharded + row_sharded

    bl, s_len, d = x.shape
    t = bl * s_len
    d_inner = ssm_norm_g.shape[1]
    n_heads = dt_bias.shape[1]
    gn = SSM_GROUPS * SSM_STATE
    conv_dim = d_inner + 2 * gn
    q_dim = ATTN_Q_HEADS * ATTN_HEAD_DIM
    kv_dim = ATTN_KV_HEADS * ATTN_HEAD_DIM

    def local_2d(name, a):
        a = a[0]
        return a.T if name in col_sharded else a

    ffn1_names = ('ffn1_w_gate', 'ffn1_w_up', 'ffn1_w_down')
    ffn2_names = ('ffn2_w_gate', 'ffn2_w_up', 'ffn2_w_down')
    mixer_names = ('w_ssm_proj', 'w_attn_proj', 'w_out')

    def shard(n):
        return local_2d(n, args[n]).astype(BF16)

    def rows(g):
        return g.reshape(N_DEV * g.shape[1], g.shape[2])

    x2 = x.reshape(t, d)
    tgt2 = loss_target.reshape(t, d)
    full = {}

    (h1,), saved1, ffn1_w, got_in = _ffn_forward(
        x2, ffn1_pre_g, lambda got: (rows(got[0]), rows(got[1])), lambda got: rows(got[2]), ffn1_post_g, "ffn1",
        side_norm=_gather_exchange([shard(n) for n in ffn1_names[:2]]),
        side_up=_gather_exchange([shard('w_in'), conv_w[0], shard('ffn1_w_down')]))
    full.update(zip(ffn1_names, ffn1_w))
    conv_w_full = jnp.transpose(got_in[1], (1, 0, 2)).reshape(SSM_CONV, conv_dim)

    win_t = rows(got_in[0])
    dt_lo = 2 * d + d_inner + conv_dim
    off = {'gs': 0, 'ga': d, 'z': 2 * d, 'xbc': 2 * d + d_inner, 'dt': dt_lo, 'q': dt_lo + n_heads, 'kv': dt_lo + n_heads + q_dim}
    assert all(o_ % 16 == 0 for o_ in off.values())

    u, (gs, ga, z, xbc, q, kv, dt_raw) = _proj_all(
        h1, mix_pre_g, win_t,
        [(off['gs'], d, BF16), (off['ga'], d, BF16), (off['z'], d_inner, BF16), (off['xbc'], conv_dim, BF16),
         (off['q'], q_dim, BF16), (off['kv'], 2 * kv_dim, BF16), (off['dt'], n_heads, F32)], name="mix_proj")

    dtb_p, alog_p, dsk_p, sinks_p = _pad_lanes(dt_bias), _pad_lanes(a_log), _pad_lanes(d_skip), _pad_lanes(attn_sinks)
    xc = _conv_fwd(xbc, conv_w_full, conv_b, bl, name="conv_fwd")
    late_names = mixer_names + ffn2_names
    (y, hprev), got_late = _ssd_fwd(xc, dt_raw, dtb_p, alog_p, dsk_p, bl, n_heads,
                                    side=_gather_exchange([shard(n) for n in late_names]), name="ssd_fwd")
    full.update({n: rows(g) for n, g in zip(late_names, got_late)})
    yn = _gated_norm_fwd(y, z, ssm_norm_g, name="gated_norm_fwd")

    onehot = _bucket_onehot()
    rep = ATTN_Q_HEADS // ATTN_KV_HEADS
    bias = _small_mm_hi(rel_bias_table.T, onehot, NN, name="rel_bias")
    bias_t = jnp.transpose(bias.reshape(ATTN_KV_HEADS, rep, CHUNK, 2 * CHUNK), (0, 3, 1, 2)).reshape(ATTN_KV_HEADS, 2 * CHUNK, rep * CHUNK)
    o, lse = _attn_fwd(q, kv, bias_t, sinks_p, bl, name="attn_fwd")

    ys, ya, merged, mix, h2 = _merge_fwd(yn, o, gs, ga, full['w_ssm_proj'], full['w_attn_proj'], full['w_out'], h1, mix_post_g,
                                         name="merge_fwd")

    (dh3, loss_vec), saved2, _, _ = _ffn_forward(h2, ffn2_pre_g, (full['ffn2_w_gate'], full['ffn2_w_up']), full['ffn2_w_down'],
                                                 ffn2_post_g, "ffn2", target=tgt2)
    loss = lax.psum(loss_vec[0, 0], ("x", "y", "c"))

    grads, own, wire, received = {}, {}, {}, {}
    dh2, grads['ffn2_pre_g'], grads['ffn2_post_g'], g32, g16, _ = _ffn_backward(
        dh3, saved2, ffn2_pre_g, full['ffn2_w_gate'], full['ffn2_w_up'], full['ffn2_w_down'], ffn2_post_g, "ffn2")
    own.update(zip(ffn2_names, map(_stack8, g32)))
    wire.update(zip(ffn2_names, map(_stack8, g16)))

    dmix, dys, dya, dgs, dga, dyn, do, grads['mix_post_g'] = _merge_bwd(
        dh2, mix, mix_post_g, gs, ga, ys, ya, full['w_ssm_proj'], full['w_attn_proj'], full['w_out'], name="merge_bwd")
    for n, (lhs, rhs) in zip(mixer_names, ((yn, dys), (o, dya), (merged, dmix))):
        g32_, g16_, _ = _mm_tn([lhs], rhs, name=f"d{n}")
        own[n], wire[n] = _stack8(g32_), _stack8(g16_)

    dq, dkv, dbias_t, dsinks = _attn_bwd(do, q, kv, lse, bias_t, sinks_p, bl, name="attn_bwd")
    dbias = jnp.transpose(dbias_t.reshape(ATTN_KV_HEADS, 2 * CHUNK, rep, CHUNK), (0, 2, 3, 1)).reshape(ATTN_Q_HEADS, -1)
    d_table = _small_mm_hi(onehot, dbias, NT, name="rel_bias_bwd")

    dy, dz, grads['ssm_norm_g'] = _gated_norm_bwd(dyn, y, z, ssm_norm_g, name="gated_norm_bwd")

    first_group = ffn2_names + mixer_names
    (dxc, ddt_raw, ddtb, dalog, ddsk), got = _ssd_bwd(dy, y, xc, dt_raw, hprev, dtb_p, alog_p, dsk_p, bl, n_heads,
                                                      side=_scatter_exchange([wire[n] for n in first_group]), name="ssd_bwd")
    received.update(zip(first_group, got))
    dxbc, dconv_w8, grads['conv_b'] = _conv_bwd(dxc, xbc, conv_w_full, conv_b, bl, name="conv_bwd")

    wide32, wide16, _ = _mm_tn([dgs, dga, dz, dxbc, dq], u, name="dw_in")
    kv32, kv16, _ = _mm_tn([dkv], u, name="dw_in_kv")
    dt32, dt16, _ = _mm_tn([ddt_raw], u, name="dw_in_dt")

    def original_order(wide, kv_part, dt_part):
        return jnp.concatenate([wide[:dt_lo], dt_part[:n_heads], wide[dt_lo:], kv_part], axis=0)
    me = 4 * lax.axis_index("x") + 2 * lax.axis_index("y") + lax.axis_index("c")
    blk_rows = win_t.shape[0] // N_DEV
    wire['w_in'] = _stack8(original_order(wide16, kv16, dt16))
    own_w_in = lax.dynamic_slice_in_dim(original_order(wide32, kv32, dt32), me * blk_rows, blk_rows)
    own['conv_w'] = jnp.transpose(dconv_w8[:SSM_CONV].reshape(SSM_CONV, N_DEV, conv_dim // N_DEV), (1, 0, 2))

    segs = [(g_, 0, off[k_]) for g_, k_ in zip([dgs, dga, dz, dxbc, dq, dkv, ddt_raw], ('gs', 'ga', 'z', 'xbc', 'q', 'kv', 'dt'))]
    dh1, grads['mix_pre_g'], got = _mm_nn_rmsbwd(segs, [win_t], h1, mix_pre_g, dh2,
                                                 side=_scatter_exchange([wire['w_in'], own['conv_w']]), name="mix_du")
    received.update(zip(('w_in', 'conv_w'), got))

    dx2, grads['ffn1_pre_g'], grads['ffn1_post_g'], g32, _, got = _ffn_backward(
        dh1, saved1, ffn1_pre_g, full['ffn1_w_gate'], full['ffn1_w_up'], full['ffn1_w_down'], ffn1_post_g, "ffn1", chain=True)
    own.update(zip(ffn1_names, map(_stack8, g32)))
    received.update(zip(ffn1_names, got))

    def own_block(a):
        return lax.dynamic_index_in_dim(a, me, 0, keepdims=False)
    out_g, out_d, out_m, out_v = {}, {}, {}, {}
    for n in big:
        w2, m2, v2 = local_2d(n, args[n]), local_2d(n, args['m_' + n]), local_2d(n, args['v_' + n])
        results = _reduce_adamw(own_w_in if n == 'w_in' else own_block(own[n]), received[n], w2, m2, v2, name=f"update_{n}")
        out_g[n], out_d[n], out_m[n], out_v[n] = ((a.T if n in col_sharded else a)[None] for a in results)
    results = _reduce_adamw(own_block(own['conv_w']), received['conv_w'], conv_w[0], m_conv_w[0], v_conv_w[0], name="update_conv_w")
    out_g['conv_w'], out_d['conv_w'], out_m['conv_w'], out_v['conv_w'] = (a[None] for a in results)

    grads['dt_bias'], grads['a_log'], grads['d_skip'], grads['attn_sinks'], grads['rel_bias_table'] = ddtb, dalog, ddsk, dsinks, d_table
    small = [n for n in weight_names if n not in big and n != 'conv_w']
    results = _update_replicated([grads[n] for n in small], [args[n] for n in small], [args['m_' + n] for n in small],
                                 [args['v_' + n] for n in small], name="update_replicated")
    for dst, vals in zip((out_g, out_d, out_m, out_v), results):
        dst.update(zip(small, vals))

    grad_x = dx2.reshape(bl, s_len, d)
    return (loss, grad_x, *[out_g[n] for n in weight_names], *[out_d[n] for n in weight_names],
            *[out_m[n] for n in weight_names], *[out_v[n] for n in weight_names])
```

```python
import functools
import math

import numpy as np
import jax
import jax.numpy as jnp
from jax import lax
from jax.experimental import pallas as pl
from jax.experimental.pallas import tpu as pltpu

F32 = jnp.float32
BF16 = jnp.bfloat16
MESH = pl.DeviceIdType.MESH
N_DEV = 8

SSM_HEAD_DIM = 64
SSM_GROUPS = 4
SSM_STATE = 128
SSM_CONV = 4
CHUNK = 128
ATTN_HEAD_DIM = 64
ATTN_Q_HEADS = 16
ATTN_KV_HEADS = 4
REL_BUCKETS = 32
REL_MAX_DISTANCE = 128
RMS_EPS = 1e-6
FFN_RESIDUAL_WEIGHT = 0.5
ADAM_LR, ADAM_B1, ADAM_B2, ADAM_EPS, ADAM_WD, ADAM_STEP = 0.001, 0.9, 0.999, 1e-08, 0.01, 10

LANES = 128
VMEM_LIMIT_BYTES = 56 * 1024 * 1024
FFN_COL_TILE = 1408
SSD_HEAD_BATCH = 2

NEG_INF = float("-inf")


def _params(*sem):
    return pltpu.CompilerParams(dimension_semantics=sem, vmem_limit_bytes=VMEM_LIMIT_BYTES)


def _tile(n, pref, mult=8):
    if n <= pref:
        return n
    t = (pref // mult) * mult
    while t >= mult:
        if n % t == 0:
            return t
        t -= mult
    return n


def _sigmoid(x):
    return 1.0 / (1.0 + jnp.exp(-x))


def _dot(a, b, dims):
    return lax.dot_general(a, b, (dims, ((), ())), preferred_element_type=F32)


NN = ((1,), (0,))
NT = ((1,), (1,))
TN = ((0,), (0,))


def _dot_hi(a, b, dims=NN):
    return lax.dot_general(a, b, (dims, ((), ())), preferred_element_type=F32, precision=lax.Precision.HIGHEST)


def _const_spec(shape):
    nd = len(shape)
    return pl.BlockSpec(shape, lambda *_: (0,) * nd)


def _resident_spec(shape):
    nd = len(shape)
    return pl.BlockSpec(shape, lambda *_: (0,) * nd, pipeline_mode=pl.Buffered(1))


class _Exchange:
    def __init__(self, arrays, out_shape, scratch, start, finish):
        self.arrays, self.out_shape, self.scratch, self.start, self.finish = arrays, out_shape, scratch, start, finish


def _hosted_call(body, *, grid, in_specs, out_specs, out_shape, scratch_shapes, operands, side, name):
    in_specs, out_specs, out_shape, scratch_shapes = list(in_specs), list(out_specs), list(out_shape), list(scratch_shapes)
    sem = ("arbitrary",) * len(grid)
    if side is None:
        outs = pl.pallas_call(body, out_shape=tuple(out_shape), grid=grid, in_specs=in_specs, out_specs=tuple(out_specs),
                              scratch_shapes=scratch_shapes, name=name, compiler_params=_params(*sem))(*operands)
        return tuple(outs), ()
    n_in, n_out, n_scr = len(in_specs), len(out_shape), len(scratch_shapes)
    s_in, s_out = len(side.arrays), len(side.out_shape)

    def wrapped(*refs):
        refs = list(refs)
        main_in, side_in = refs[:n_in], refs[n_in:n_in + s_in]
        o0 = n_in + s_in
        main_out, side_out = refs[o0:o0 + n_out], refs[o0 + n_out:o0 + n_out + s_out]
        c0 = o0 + n_out + s_out
        main_scr, side_scr = refs[c0:c0 + n_scr], refs[c0 + n_scr:]
        ids = [pl.program_id(ax) for ax in range(len(grid))]
        first = functools.reduce(jnp.logical_and, [i == 0 for i in ids])
        last = functools.reduce(jnp.logical_and, [i == g - 1 for i, g in zip(ids, grid)])

        @pl.when(first)
        def _():
            side.start(side_in, side_out, side_scr)

        body(*main_in, *main_out, *main_scr)

        @pl.when(last)
        def _():
            side.finish(side_in, side_out, side_scr)

    hbm = pl.BlockSpec(memory_space=pl.ANY)
    outs = pl.pallas_call(
        wrapped, out_shape=tuple(out_shape + list(side.out_shape)), grid=grid,
        in_specs=in_specs + [hbm] * s_in, out_specs=tuple(out_specs + [hbm] * s_out),
        scratch_shapes=scratch_shapes + list(side.scratch), name=name, compiler_params=_params(*sem))(*operands, *side.arrays)
    return tuple(outs[:n_out]), tuple(outs[n_out:])


def _proj_all(h, g, w, segs, *, name):
    t, d = h.shape
    tm = _tile(t, 512)
    segs = [(row0, wd_, max(wd_, LANES), dt_) for row0, wd_, dt_ in segs]
    assert all(row0 + out_w <= w.shape[0] for row0, _, out_w, _ in segs)

    def body(h_ref, g_ref, w_ref, u_ref, *o_refs):
        hv = h_ref[...]
        r = lax.rsqrt(jnp.mean(hv * hv, axis=-1, keepdims=True) + RMS_EPS)
        uv = (hv * r * g_ref[...]).astype(BF16)
        u_ref[...] = uv
        for (row0, width, out_w, _), o_ref in zip(segs, o_refs):
            for c0, c1 in _col_chunks(out_w, 8 * LANES):
                part = _dot(uv, w_ref[row0 + c0:row0 + c1, :], NT)
                if width < out_w:
                    part = jnp.where(lax.broadcasted_iota(jnp.int32, part.shape, 1) < width, part, 0.0)
                o_ref[:, c0:c1] = part.astype(o_ref.dtype)

    row = pl.BlockSpec((tm, d), lambda i: (i, 0))
    outs = pl.pallas_call(
        body, out_shape=(jax.ShapeDtypeStruct((t, d), BF16),) + tuple(jax.ShapeDtypeStruct((t, ow), dt_) for _, _, ow, dt_ in segs),
        grid=(t // tm,), in_specs=[row, _const_spec((1, d)), _resident_spec(w.shape)],
        out_specs=(row,) + tuple(pl.BlockSpec((tm, ow), lambda i: (i, 0)) for _, _, ow, _ in segs),
        name=name, compiler_params=_params("parallel"))(h, g, w)
    return outs[0], outs[1:]


def _mm_tn(a_list, b, *, tm=1408, tk=2048, side=None, name):
    t, n = b.shape
    tk = _tile(t, tk if len(a_list) == 1 else tk // 2)
    nk = t // tk
    widths = [a.shape[1] for a in a_list]
    tm = _tile(math.gcd(*widths), tm, LANES)
    assert all(w % tm == 0 for w in widths)
    starts = np.cumsum([0] + [w // tm for w in widths])
    nseg = len(a_list)

    def a_spec(s):
        lo, hi = int(starts[s]), int(starts[s + 1])

        def idx(i, k):
            active = jnp.logical_and(i >= lo, i < hi)
            return (jnp.where(active, k, 0), jnp.clip(i - lo, 0, hi - lo - 1))
        return pl.BlockSpec((tk, tm), idx)

    def body(*refs):
        a_refs, b_ref, o_ref, o16_ref, acc = refs[:nseg], refs[nseg], refs[nseg + 1], refs[nseg + 2], refs[nseg + 3]
        i, k = pl.program_id(0), pl.program_id(1)

        @pl.when(k == 0)
        def _():
            acc[...] = jnp.zeros_like(acc)

        bv = b_ref[...].astype(BF16)
        for s in range(nseg):
            lo, hi = int(starts[s]), int(starts[s + 1])

            @pl.when(jnp.logical_and(i >= lo, i < hi))
            def _(s=s):
                acc[...] += _dot(a_refs[s][...].astype(BF16), bv, TN)

        @pl.when(k == nk - 1)
        def _():
            o_ref[...] = acc[...]
            o16_ref[...] = acc[...].astype(BF16)

    rows = int(starts[-1]) * tm
    o_spec = pl.BlockSpec((tm, n), lambda i, k: (i, 0))
    (o32, o16), got = _hosted_call(
        body, out_shape=[jax.ShapeDtypeStruct((rows, n), F32), jax.ShapeDtypeStruct((rows, n), BF16)], grid=(int(starts[-1]), nk),
        in_specs=[a_spec(s) for s in range(nseg)] + [pl.BlockSpec((tk, n), lambda i, k: (k, 0))],
        out_specs=[o_spec, o_spec], scratch_shapes=[pltpu.VMEM((tm, n), F32)], operands=list(a_list) + [b], side=side, name=name)
    return o32, o16, got


def _mm_nn_rmsbwd(segs, weights, x, g, dres, *, tm=512, side=None, name):
    t, d = x.shape
    tm = _tile(t, tm)
    nseg, nw = len(segs), len(weights)

    def body(*refs):
        a_refs, w_refs = refs[:nseg], refs[nseg:nseg + nw]
        x_ref, g_ref, dres_ref, dx_ref, dg_ref = refs[nseg + nw:]

        @pl.when(pl.program_id(0) == 0)
        def _():
            dg_ref[...] = jnp.zeros_like(dg_ref)

        dn = None
        for s, (a, w_idx, row0) in enumerate(segs):
            part = _dot(a_refs[s][...].astype(BF16), w_refs[w_idx][row0:row0 + a.shape[1], :], NN)
            dn = part if dn is None else dn + part
        xv = x_ref[...]
        r = lax.rsqrt(jnp.mean(xv * xv, axis=-1, keepdims=True) + RMS_EPS)
        xhat = xv * r
        dyg = dn * g_ref[...]
        dx_ref[...] = dres_ref[...] + r * (dyg - xhat * jnp.mean(dyg * xhat, axis=-1, keepdims=True))
        dg_ref[...] += jnp.sum(dn * xhat, axis=0, keepdims=True)

    row = pl.BlockSpec((tm, d), lambda i: (i, 0))
    in_specs = [pl.BlockSpec((tm, a.shape[1]), lambda i: (i, 0)) for a, _, _ in segs]
    in_specs += [_resident_spec(w.shape) for w in weights] + [row, _const_spec((1, d)), row]
    (dx, dg), extra = _hosted_call(
        body, grid=(t // tm,), in_specs=in_specs, out_specs=[row, _const_spec((1, d))],
        out_shape=[jax.ShapeDtypeStruct((t, d), F32), jax.ShapeDtypeStruct((1, d), F32)], scratch_shapes=[],
        operands=[a for a, _, _ in segs] + list(weights) + [x, g, dres], side=side, name=name)
    return dx, dg, extra


def _col_chunks(width, chunk=4 * LANES):
    return [(c0, min(c0 + chunk, width)) for c0 in range(0, width, chunk)]


def _rms_fwd(x, g, *, side=None, name):
    t, d = x.shape
    tm = _tile(t, 512)

    def body(x_ref, g_ref, o_ref):
        xv = x_ref[...]
        r = lax.rsqrt(jnp.mean(xv * xv, axis=-1, keepdims=True) + RMS_EPS)
        o_ref[...] = (xv * r * g_ref[...]).astype(o_ref.dtype)

    row = pl.BlockSpec((tm, d), lambda i: (i, 0))
    (n,), got = _hosted_call(body, grid=(t // tm,), in_specs=[row, _const_spec((1, d))], out_specs=[row],
                             out_shape=[jax.ShapeDtypeStruct((t, d), BF16)], scratch_shapes=[], operands=[x, g], side=side, name=name)
    return n, got


def _ffn_up(n, wgt, wut, *, side=None, name):
    t, d = n.shape
    f = wgt.shape[0]
    tm, tn = _tile(t, 512), _tile(f, FFN_COL_TILE, LANES)

    def body(n_ref, wg_ref, wu_ref, g_ref, u_ref, h_ref):
        nv = n_ref[...]
        gv = _dot(nv, wg_ref[...], NT)
        uv = _dot(nv, wu_ref[...], NT)
        g_ref[...] = gv.astype(BF16)
        u_ref[...] = uv.astype(BF16)
        h_ref[...] = (gv * _sigmoid(gv) * uv).astype(BF16)

    w_spec = pl.BlockSpec((tn, d), lambda j, i: (j, 0))
    o_spec = pl.BlockSpec((tm, tn), lambda j, i: (i, j))
    shp = jax.ShapeDtypeStruct((t, f), BF16)
    return _hosted_call(body, grid=(f // tn, t // tm), in_specs=[pl.BlockSpec((tm, d), lambda j, i: (i, 0)), w_spec, w_spec],
                        out_specs=[o_spec, o_spec, o_spec], out_shape=[shp, shp, shp], scratch_shapes=[], operands=[n, wgt, wut],
                        side=side, name=name)


def _rms_residual(acc, h, gp, weight):
    r = lax.rsqrt(jnp.mean(acc * acc, axis=-1, keepdims=True) + RMS_EPS)
    return h + weight * (acc * r * gp)


def _ffn_down(hid, wd, h_in, gp, *, target=None, side=None, name):
    t, f = hid.shape
    d = wd.shape[1]
    tm = _tile(t, 256)
    row = pl.BlockSpec((tm, d), lambda i: (i, 0))
    shp = jax.ShapeDtypeStruct((t, d), F32)
    in_specs = [pl.BlockSpec((tm, f), lambda i: (i, 0)), _resident_spec((f, d)), row, _const_spec((1, d))]

    if target is None:
        def body(hid_ref, wd_ref, hin_ref, gp_ref, f_ref, hout_ref):
            acc = _dot(hid_ref[...], wd_ref[...], NN)
            f_ref[...] = acc
            hout_ref[...] = _rms_residual(acc, hin_ref[...], gp_ref[...], FFN_RESIDUAL_WEIGHT)

        return _hosted_call(body, grid=(t // tm,), in_specs=in_specs, out_specs=[row, row], out_shape=[shp, shp], scratch_shapes=[],
                            operands=[hid, wd, h_in, gp], side=side, name=name)

    def body_loss(hid_ref, wd_ref, hin_ref, gp_ref, tgt_ref, f_ref, dh_ref, loss_ref):
        @pl.when(pl.program_id(0) == 0)
        def _():
            loss_ref[...] = jnp.zeros_like(loss_ref)
        acc = _dot(hid_ref[...], wd_ref[...], NN)
        f_ref[...] = acc
        e = _rms_residual(acc, hin_ref[...], gp_ref[...], FFN_RESIDUAL_WEIGHT) - tgt_ref[...]
        dh_ref[...] = e * (1.0 / d)
        per_row = jnp.sum(e * e, axis=1, keepdims=True) * (1.0 / d)
        loss_ref[...] += 0.5 * jnp.sum(per_row, axis=0, keepdims=True)

    return _hosted_call(body_loss, grid=(t // tm,), in_specs=in_specs + [row], out_specs=[row, row, _const_spec((1, LANES))],
                        out_shape=[shp, shp, jax.ShapeDtypeStruct((1, LANES), F32)], scratch_shapes=[],
                        operands=[hid, wd, h_in, gp, target], side=side, name=name)


def _post_bwd(dh, f, gp, weight, *, name):
    t, d = f.shape
    tm = _tile(t, 512)

    def body(dh_ref, f_ref, gp_ref, df_ref, dgp_ref):
        @pl.when(pl.program_id(0) == 0)
        def _():
            dgp_ref[...] = jnp.zeros_like(dgp_ref)
        fv = f_ref[...]
        dy = weight * dh_ref[...]
        r = lax.rsqrt(jnp.mean(fv * fv, axis=-1, keepdims=True) + RMS_EPS)
        fhat = fv * r
        dyg = dy * gp_ref[...]
        df_ref[...] = (r * (dyg - fhat * jnp.mean(dyg * fhat, axis=-1, keepdims=True))).astype(BF16)
        dgp_ref[...] += jnp.sum(dy * fhat, axis=0, keepdims=True)

    row = pl.BlockSpec((tm, d), lambda i: (i, 0))
    return pl.pallas_call(body, out_shape=(jax.ShapeDtypeStruct((t, d), BF16), jax.ShapeDtypeStruct((1, d), F32)), grid=(t // tm,),
                          in_specs=[row, row, _const_spec((1, d))], out_specs=(row, _const_spec((1, d))),
                          name=name, compiler_params=_params("arbitrary"))(dh, f, gp)


def _ffn_dhid(df, wd, g, u, *, name):
    t, d = df.shape
    f = wd.shape[0]
    tm, tn = _tile(t, 512), _tile(f, FFN_COL_TILE, LANES)

    def body(df_ref, wd_ref, g_ref, u_ref, dg_ref, du_ref):
        dh = _dot(df_ref[...], wd_ref[...], NT)
        gv = g_ref[...].astype(F32)
        uv = u_ref[...].astype(F32)
        sg = _sigmoid(gv)
        silu = gv * sg
        dg_ref[...] = (dh * uv * (sg + silu * (1.0 - sg))).astype(BF16)
        du_ref[...] = (dh * silu).astype(BF16)

    o_spec = pl.BlockSpec((tm, tn), lambda j, i: (i, j))
    shp = jax.ShapeDtypeStruct((t, f), BF16)
    return pl.pallas_call(body, out_shape=(shp, shp), grid=(f // tn, t // tm),
                          in_specs=[pl.BlockSpec((tm, d), lambda j, i: (i, 0)), pl.BlockSpec((tn, d), lambda j, i: (j, 0)), o_spec, o_spec],
                          out_specs=(o_spec, o_spec), name=name, compiler_params=_params("parallel", "arbitrary"))(df, wd, g, u)


def _ffn_forward(h_in, g_pre, w_up, wd, g_post, tag, side_norm=None, side_up=None, target=None):
    n, got_norm = _rms_fwd(h_in, g_pre, side=side_norm, name=f"{tag}_prenorm")
    wgt, wut = w_up(got_norm) if callable(w_up) else w_up
    (g, u, hid), got_up = _ffn_up(n, wgt, wut, side=side_up, name=f"{tag}_up")
    wd = wd(got_up) if callable(wd) else wd
    outs, _ = _ffn_down(hid, wd, h_in, g_post, target=target, name=f"{tag}_down")
    return outs[1:], (h_in, n, g, u, hid, outs[0]), (wgt, wut, wd), got_up


def _stack8(g):
    return g.reshape(N_DEV, g.shape[0] // N_DEV, g.shape[1])


def _ffn_backward(dh_out, saved, g_pre, wgt, wut, wd, g_post, tag, chain=False):
    h_in, n, g, u, hid, f = saved

    def side_of(grad16):
        return _scatter_exchange([_stack8(grad16)]) if chain else None

    df, dg_post = _post_bwd(dh_out, f, g_post, FFN_RESIDUAL_WEIGHT, name=f"{tag}_post_bwd")
    dgate, dup = _ffn_dhid(df, wd, g, u, name=f"{tag}_dhid")
    d_wd, d_wd16, _ = _mm_tn([hid], df, name=f"{tag}_dwd")
    d_wgt, d_wgt16, got_wd = _mm_tn([dgate], n, side=side_of(d_wd16), name=f"{tag}_dwg")
    d_wut, d_wut16, got_wg = _mm_tn([dup], n, side=side_of(d_wgt16), name=f"{tag}_dwu")
    dh_in, dg_pre, got_wu = _mm_nn_rmsbwd([(dgate, 0, 0), (dup, 1, 0)], [wgt, wut], h_in, g_pre, dh_out, side=side_of(d_wut16),
                                          name=f"{tag}_dn")
    received = (got_wg[0], got_wu[0], got_wd[0]) if chain else None
    return dh_in, dg_pre, dg_post, (d_wgt, d_wut, d_wd), (d_wgt16, d_wut16, d_wd16), received


CONV_ROWS = 128
HALO = 8


def _taps(w_ref):
    return [w_ref[k:k + 1, :] for k in range(SSM_CONV)]


def _conv_chunk(x_ref, xs, r0, taps, bias):
    xs[HALO + r0:HALO + r0 + CONV_ROWS, :] = x_ref[r0:r0 + CONV_ROWS, :].astype(F32)
    shifted = [xs[HALO + r0 - k:HALO + r0 - k + CONV_ROWS, :] for k in range(SSM_CONV)]
    pre = bias + shifted[0] * taps[SSM_CONV - 1]
    for k in range(1, SSM_CONV):
        pre = pre + shifted[k] * taps[SSM_CONV - 1 - k]
    return shifted, pre


def _fold_rows(a):
    return functools.reduce(jnp.add, [a[i:i + 8] for i in range(0, a.shape[0], 8)])


def _conv_fwd(xbc, conv_w, conv_b, bl, *, name):
    t, c = xbc.shape
    s = t // bl
    tc = LANES
    assert s % CONV_ROWS == 0

    def body(x_ref, w_ref, b_ref, o_ref, xs):
        taps, bias = _taps(w_ref), b_ref[...]
        xs[0:HALO, :] = jnp.zeros((HALO, tc), F32)
        for r0 in range(0, s, CONV_ROWS):
            _, pre = _conv_chunk(x_ref, xs, r0, taps, bias)
            o_ref[r0:r0 + CONV_ROWS, :] = (pre * _sigmoid(pre)).astype(o_ref.dtype)

    blk = pl.BlockSpec((s, tc), lambda b, j: (b, j))
    return pl.pallas_call(body, out_shape=jax.ShapeDtypeStruct((t, c), BF16), grid=(bl, c // tc),
                          in_specs=[blk, pl.BlockSpec((SSM_CONV, tc), lambda b, j: (0, j)), pl.BlockSpec((1, tc), lambda b, j: (0, j))],
                          out_specs=blk, scratch_shapes=[pltpu.VMEM((HALO + s, tc), F32)],
                          name=name, compiler_params=_params("parallel", "arbitrary"))(xbc, conv_w, conv_b)


def _conv_bwd(dxc, xbc, conv_w, conv_b, bl, *, name):
    t, c = xbc.shape
    s = t // bl
    tc = LANES

    def body(dy_ref, x_ref, w_ref, b_ref, dx_ref, dw_ref, db_ref, xs, dpre_s):
        @pl.when(pl.program_id(1) == 0)
        def _():
            dw_ref[...] = jnp.zeros_like(dw_ref)
            db_ref[...] = jnp.zeros_like(db_ref)

        taps, bias = _taps(w_ref), b_ref[...]
        zero8 = jnp.zeros((HALO, tc), F32)
        xs[0:HALO, :] = zero8
        dpre_s[s:s + HALO, :] = zero8
        sums = [zero8] * (SSM_CONV + 1)
        for r0 in range(0, s, CONV_ROWS):
            shifted, pre = _conv_chunk(x_ref, xs, r0, taps, bias)
            sg = _sigmoid(pre)
            dpre = dy_ref[r0:r0 + CONV_ROWS, :].astype(F32) * (sg * (1.0 + pre * (1.0 - sg)))
            dpre_s[r0:r0 + CONV_ROWS, :] = dpre
            sums = [acc + _fold_rows(dpre * sh) for acc, sh in zip(sums[:-1], shifted)] + [sums[-1] + _fold_rows(dpre)]
        for k in range(SSM_CONV):
            dw_ref[SSM_CONV - 1 - k:SSM_CONV - k, :] += jnp.sum(sums[k], axis=0, keepdims=True)
        db_ref[...] += jnp.sum(sums[-1], axis=0, keepdims=True)
        for r0 in range(0, s, CONV_ROWS):
            dx = dpre_s[r0:r0 + CONV_ROWS, :] * taps[SSM_CONV - 1]
            for k in range(1, SSM_CONV):
                dx = dx + dpre_s[r0 + k:r0 + k + CONV_ROWS, :] * taps[SSM_CONV - 1 - k]
            dx_ref[r0:r0 + CONV_ROWS, :] = dx.astype(dx_ref.dtype)

    blk = pl.BlockSpec((s, tc), lambda j, b: (b, j))
    return pl.pallas_call(
        body, out_shape=(jax.ShapeDtypeStruct((t, c), BF16), jax.ShapeDtypeStruct((8, c), F32), jax.ShapeDtypeStruct((1, c), F32)),
        grid=(c // tc, bl),
        in_specs=[blk, blk, pl.BlockSpec((SSM_CONV, tc), lambda j, b: (0, j)), pl.BlockSpec((1, tc), lambda j, b: (0, j))],
        out_specs=(blk, pl.BlockSpec((8, tc), lambda j, b: (0, j)), pl.BlockSpec((1, tc), lambda j, b: (0, j))),
        scratch_shapes=[pltpu.VMEM((HALO + s, tc), F32), pltpu.VMEM((s + HALO, tc), F32)],
        name=name, compiler_params=_params("parallel", "arbitrary"))(dxc, xbc, conv_w, conv_b)


def _softplus(x):
    return jnp.maximum(x, 0.0) + jnp.log1p(jnp.exp(-jnp.abs(x)))


def _hilo_dot(v, m_b, dims=NN):
    hi = v.astype(BF16)
    lo = (v - hi.astype(F32)).astype(BF16)
    return _dot(hi, m_b, dims) + _dot(lo, m_b, dims)


def _ssd_chunk_common(dtraw_ref, dtb_ref, alog_ref, dsk_ref, d_inner):
    q, p = CHUNK, SSM_HEAD_DIM
    shift = p.bit_length() - 1
    assert 1 << shift == p
    dt = _softplus(dtraw_ref[...] + dtb_ref[...])
    a = -jnp.exp(alog_ref[...])
    ii = lax.broadcasted_iota(jnp.int32, (q, q), 0)
    jj = lax.broadcasted_iota(jnp.int32, (q, q), 1)
    causal = ii >= jj
    tril = jnp.where(causal, 1.0, 0.0).astype(F32)
    triu = jnp.where(ii <= jj, 1.0, 0.0).astype(F32)
    a_cs = _dot_hi(tril, dt * a)
    a_cs_t = a_cs.T
    a_last = a_cs[q - 1:q, :]
    e_col = jnp.exp(a_cs)
    dec_end = jnp.exp(a_last - a_cs)
    head_of_col = lax.shift_right_logical(lax.broadcasted_iota(jnp.int32, (LANES, d_inner), 1), shift)
    spread = (lax.broadcasted_iota(jnp.int32, (LANES, d_inner), 0) == head_of_col).astype(BF16)
    exact = jnp.concatenate([dt, jnp.broadcast_to(dsk_ref[...], (8, LANES))], axis=0)
    hi = jnp.concatenate([exact, e_col, dec_end], axis=0).astype(BF16)
    lo = (exact - hi[:q + 8].astype(F32)).astype(BF16)
    wide = _dot(hi, spread, NN)
    fine = wide[:q + 8] + _dot(lo, spread, NN)
    return dict(dt=dt, a=a, a_cs=a_cs, a_cs_t=a_cs_t, a_last=a_last, dec_end=dec_end, causal=causal, triu=triu,
                dt_e=fine[:q], dsk_e=fine[q:q + 1], e_e=wide[q + 8:2 * q + 8], dec_e=wide[2 * q + 8:3 * q + 8])


def _fill_block_diag(bd_ref, src_ref, hpg, col0=0):
    q, p = CHUNK, SSM_HEAD_DIM
    for hh in range(hpg):
        bd_ref[hh * q:(hh + 1) * q, hh * p:(hh + 1) * p] = src_ref[:, col0 + hh * p:col0 + (hh + 1) * p]


def _lane_onehot(h):
    return (lax.broadcasted_iota(jnp.int32, (1, LANES), 1) == h).astype(F32)


def _ssd_fwd(xc, dt_raw, dt_bias, a_log, d_skip, bl, n_heads, *, side=None, name):
    t = xc.shape[0]
    q, p, nst, grp = CHUNK, SSM_HEAD_DIM, SSM_STATE, SSM_GROUPS
    d_inner = n_heads * p
    hpg = n_heads // grp
    hb = min(hpg, SSD_HEAD_BATCH)
    gw = hpg * p
    nc = t // bl // q
    assert d_inner % (grp * nst) == 0 and nst == LANES and hpg % hb == 0

    def body(xs_ref, b_ref, c_ref, dtraw_ref, dtb_ref, alog_ref, dsk_ref, y_ref, hprev_ref, state, m_all, x_bd, xdt_s):
        @pl.when(jnp.logical_and(pl.program_id(0) == 0, pl.program_id(1) == 0))
        def _():
            x_bd[...] = jnp.zeros_like(x_bd)

        @pl.when(pl.program_id(1) == 0)
        def _():
            state[...] = jnp.zeros_like(state)

        cm = _ssd_chunk_common(dtraw_ref, dtb_ref, alog_ref, dsk_ref, d_inner)
        for g in range(grp):
            cols = slice(g * gw, (g + 1) * gw)
            bg = b_ref[:, g * nst:(g + 1) * nst]
            cg = c_ref[:, g * nst:(g + 1) * nst]
            scores = _dot(cg, bg, NT)
            xs = xs_ref[:, cols].astype(F32)
            xdt = xs * cm['dt_e'][:, cols]
            xdt_s[...] = xdt.astype(BF16)
            y_parts = []
            for sub in range(hpg // hb):
                for k in range(hb):
                    h = g * hpg + sub * hb + k
                    seg = cm['a_cs'][:, h:h + 1] - cm['a_cs_t'][h:h + 1, :]
                    m_all[:, k * q:(k + 1) * q] = (scores * jnp.exp(jnp.where(cm['causal'], seg, NEG_INF))).astype(BF16)
                _fill_block_diag(x_bd, xdt_s, hb, sub * hb * p)
                y_parts.append(_dot(m_all[...], x_bd[...], NN))
            hprev = state[g]
            hprev_ref[g] = hprev
            y = jnp.concatenate(y_parts, axis=1) + cm['e_e'][:, cols] * _dot(cg, hprev.astype(BF16), NT)
            y_ref[:, cols] = y + cm['dsk_e'][:, cols] * xs
            st = _dot((xdt * cm['dec_e'][:, cols]).astype(BF16), bg, TN)
            for hh in range(hpg):
                h = g * hpg + hh
                rows = slice(hh * p, (hh + 1) * p)
                state[g, rows, :] = jnp.exp(cm['a_last'][:, h:h + 1]) * hprev[rows] + st[rows]

    gn = grp * nst

    def rowmap(b, c):
        return b * nc + c
    vec = pl.BlockSpec((1, LANES), lambda b, c: (0, 0))
    return _hosted_call(
        body,
        out_shape=[jax.ShapeDtypeStruct((t, d_inner), F32), jax.ShapeDtypeStruct((t // q, grp, gw, nst), F32)],
        grid=(bl, nc),
        in_specs=[pl.BlockSpec((q, d_inner), lambda b, c: (rowmap(b, c), 0)),
                  pl.BlockSpec((q, gn), lambda b, c: (rowmap(b, c), d_inner // gn)),
                  pl.BlockSpec((q, gn), lambda b, c: (rowmap(b, c), d_inner // gn + 1)),
                  pl.BlockSpec((q, LANES), lambda b, c: (rowmap(b, c), 0)), vec, vec, vec],
        out_specs=[pl.BlockSpec((q, d_inner), lambda b, c: (rowmap(b, c), 0)),
                   pl.BlockSpec((None, grp, gw, nst), lambda b, c: (rowmap(b, c), 0, 0, 0))],
        scratch_shapes=[pltpu.VMEM((grp, gw, nst), F32), pltpu.VMEM((q, hb * q), BF16), pltpu.VMEM((hb * q, hb * p), BF16),
                        pltpu.VMEM((q, gw), BF16)],
        operands=[xc, xc, xc, dt_raw, dt_bias, a_log, d_skip], side=side, name=name)


def _ssd_bwd(dy, y, xc, dt_raw, hprev_all, dt_bias, a_log, d_skip, bl, n_heads, *, side=None, name):
    t, c_dim = xc.shape
    q, p, nst, grp = CHUNK, SSM_HEAD_DIM, SSM_STATE, SSM_GROUPS
    d_inner = n_heads * p
    hpg = n_heads // grp
    hb = min(hpg, SSD_HEAD_BATCH)
    gw = hpg * p
    nc = t // bl // q
    gn = grp * nst
    shift = p.bit_length() - 1

    def body(dy_ref, y_ref, xs_ref, b_ref, c_ref, dtraw_ref, hprev_ref, dtb_ref, alog_ref, dsk_ref,
             dxc_ref, ddtraw_ref, ddtb_ref, dalog_ref, ddsk_ref, dstate, mt_all, x_bd, dy_bd, xdt_s):
        @pl.when(jnp.logical_and(pl.program_id(0) == 0, pl.program_id(1) == 0))
        def _():
            ddtb_ref[...] = jnp.zeros_like(ddtb_ref)
            dalog_ref[...] = jnp.zeros_like(dalog_ref)
            ddsk_ref[...] = jnp.zeros_like(ddsk_ref)
            x_bd[...] = jnp.zeros_like(x_bd)
            dy_bd[...] = jnp.zeros_like(dy_bd)

        @pl.when(pl.program_id(1) == 0)
        def _():
            dstate[...] = jnp.zeros_like(dstate)

        cm = _ssd_chunk_common(dtraw_ref, dtb_ref, alog_ref, dsk_ref, d_inner)
        causal = cm['causal']
        upper = cm['triu'] > 0.5
        seg_row = lax.shift_right_logical(lax.broadcasted_iota(jnp.int32, (gw, LANES), 0), shift)
        seg_lane = lax.broadcasted_iota(jnp.int32, (gw, LANES), 1)
        sums = jnp.zeros((5 * q, LANES), F32)
        state_dot = jnp.zeros((1, LANES), F32)
        for g in range(grp):
            cols = slice(g * gw, (g + 1) * gw)
            seg_sum = (seg_row + g * hpg == seg_lane).astype(BF16)
            bg = b_ref[:, g * nst:(g + 1) * nst]
            cg = c_ref[:, g * nst:(g + 1) * nst]
            scores_t = _dot(bg, cg, NT)
            xs = xs_ref[:, cols].astype(F32)
            xdt = xs * cm['dt_e'][:, cols]
            xdt_s[...] = xdt.astype(BF16)
            dyf = dy_ref[:, cols].astype(F32)
            dscores = jnp.zeros((q, q), F32)
            dx_parts = []
            for sub in range(hpg // hb):
                c0 = sub * hb * p
                _fill_block_diag(x_bd, xdt_s, hb, c0)
                _fill_block_diag(dy_bd, dy_ref, hb, g * gw + c0)
                dm_all = _dot(dy_ref[:, g * gw + c0:g * gw + c0 + hb * p], x_bd[...], NT)
                for k in range(hb):
                    h = g * hpg + sub * hb + k
                    blk = slice(k * q, (k + 1) * q)
                    seg = cm['a_cs'][:, h:h + 1] - cm['a_cs_t'][h:h + 1, :]
                    mt_all[:, blk] = (scores_t * jnp.exp(jnp.where(upper, -seg, NEG_INF))).astype(BF16)
                    dscores = dscores + dm_all[:, blk] * jnp.exp(jnp.where(causal, seg, NEG_INF))
                dx_parts.append(_dot(mt_all[...], dy_bd[...], NN))
            hprev = hprev_ref[g]
            hprev_b = hprev.astype(BF16)
            dhn = dstate[g]
            dhn_b = dhn.astype(BF16)
            e_e, dec_e = cm['e_e'][:, cols], cm['dec_e'][:, cols]
            y_scan = y_ref[:, cols] - cm['dsk_e'][:, cols] * xs
            dye_b = (dyf * e_e).astype(BF16)
            dcg = _dot(dye_b, hprev_b, NN)
            dhp = _dot(dye_b, cg, TN)
            bdh = _dot(bg, dhn_b, NT)
            dbg = _dot((xdt * dec_e).astype(BF16), dhn_b, NN)
            dx_diag = jnp.concatenate(dx_parts, axis=1)
            dx = dec_e * bdh + dx_diag
            ds_b = dscores.astype(BF16)
            dcg = dcg + _dot(ds_b, bg, NN)
            dbg = dbg + _dot(ds_b, cg, TN)
            x_rounded = xdt_s[...].astype(F32)
            sums = sums + _hilo_dot(jnp.concatenate([dyf * y_scan, xdt * bdh, x_rounded * dx_diag, dx * xs, dyf * xs], axis=0), seg_sum)
            state_dot = state_dot + jnp.sum(_hilo_dot(dhn * hprev, seg_sum, TN), axis=0, keepdims=True)
            dxc_ref[:, cols] = (dx * cm['dt_e'][:, cols] + cm['dsk_e'][:, cols] * dyf).astype(dxc_ref.dtype)
            dxc_ref[:, d_inner + g * nst:d_inner + (g + 1) * nst] = dbg.astype(dxc_ref.dtype)
            dxc_ref[:, d_inner + gn + g * nst:d_inner + gn + (g + 1) * nst] = dcg.astype(dxc_ref.dtype)
            for hh in range(hpg):
                h = g * hpg + hh
                rows = slice(hh * p, (hh + 1) * p)
                dstate[g, rows, :] = jnp.exp(cm['a_last'][:, h:h + 1]) * dhn[rows] + dhp[rows]
        s_y, s_end, s_diag, s_dt, s_skip = (sums[k * q:(k + 1) * q] for k in range(5))
        dt, a, dec_end = cm['dt'], cm['a'], cm['dec_end']
        last_row = (lax.broadcasted_iota(jnp.int32, (q, 1), 0) == q - 1).astype(F32)
        da_last = jnp.sum(dec_end * s_end, axis=0, keepdims=True) + jnp.exp(cm['a_last']) * state_dot
        da = s_y - dec_end * s_end - s_diag + last_row * da_last
        ddta = _dot_hi(cm['triu'], da)
        ddt = s_dt + ddta * a
        d_a = jnp.sum(ddta * dt, axis=0, keepdims=True)
        ddt_raw = ddt * _sigmoid(dtraw_ref[...] + dtb_ref[...])
        ddtraw_ref[...] = ddt_raw
        ddtb_ref[...] += jnp.sum(ddt_raw, axis=0, keepdims=True)
        dalog_ref[...] += d_a * a
        ddsk_ref[...] += jnp.sum(s_skip, axis=0, keepdims=True)

    def rowmap(b, c):
        return b * nc + (nc - 1 - c)
    vec = pl.BlockSpec((1, LANES), lambda b, c: (0, 0))
    vec_shape = jax.ShapeDtypeStruct((1, LANES), F32)
    return _hosted_call(
        body,
        out_shape=[jax.ShapeDtypeStruct((t, c_dim), BF16), jax.ShapeDtypeStruct((t, LANES), F32), vec_shape, vec_shape, vec_shape],
        grid=(bl, nc),
        in_specs=[pl.BlockSpec((q, d_inner), lambda b, c: (rowmap(b, c), 0)),
                  pl.BlockSpec((q, d_inner), lambda b, c: (rowmap(b, c), 0)),
                  pl.BlockSpec((q, d_inner), lambda b, c: (rowmap(b, c), 0)),
                  pl.BlockSpec((q, gn), lambda b, c: (rowmap(b, c), d_inner // gn)),
                  pl.BlockSpec((q, gn), lambda b, c: (rowmap(b, c), d_inner // gn + 1)),
                  pl.BlockSpec((q, LANES), lambda b, c: (rowmap(b, c), 0)),
                  pl.BlockSpec((None, grp, gw, nst), lambda b, c: (rowmap(b, c), 0, 0, 0)), vec, vec, vec],
        out_specs=[pl.BlockSpec((q, c_dim), lambda b, c: (rowmap(b, c), 0)),
                   pl.BlockSpec((q, LANES), lambda b, c: (rowmap(b, c), 0)), vec, vec, vec],
        scratch_shapes=[pltpu.VMEM((grp, gw, nst), F32), pltpu.VMEM((q, hb * q), BF16),
                        pltpu.VMEM((hb * q, hb * p), BF16), pltpu.VMEM((hb * q, hb * p), BF16), pltpu.VMEM((q, gw), BF16)],
        operands=[dy, y, xc, xc, xc, dt_raw, hprev_all, dt_bias, a_log, d_skip], side=side, name=name)


def _gated_norm_bwd(dyn, y, z, ng, *, name):
    t, d = y.shape
    tm = _tile(t, 512)
    gw = d // SSM_GROUPS

    def body(dyn_ref, y_ref, z_ref, ng_ref, dy_ref, dz_ref, dng_ref):
        @pl.when(pl.program_id(0) == 0)
        def _():
            dng_ref[...] = jnp.zeros_like(dng_ref)
        for g in range(SSM_GROUPS):
            sl = slice(g * gw, (g + 1) * gw)
            zv = z_ref[:, sl].astype(F32)
            yv = y_ref[:, sl]
            sg = _sigmoid(zv)
            sz = zv * sg
            yg = yv * sz
            r = lax.rsqrt(jnp.mean(yg * yg, axis=-1, keepdims=True) + RMS_EPS)
            yhat = yg * r
            dn = dyn_ref[:, sl].astype(F32)
            dyg_n = dn * ng_ref[:, sl]
            dyg = r * (dyg_n - yhat * jnp.mean(dyg_n * yhat, axis=-1, keepdims=True))
            dy_ref[:, sl] = (dyg * sz).astype(dy_ref.dtype)
            dz_ref[:, sl] = (dyg * yv * (sg * (1.0 + zv * (1.0 - sg)))).astype(dz_ref.dtype)
            dng_ref[:, sl] += jnp.sum(dn * yhat, axis=0, keepdims=True)

    row = pl.BlockSpec((tm, d), lambda i: (i, 0))
    shp = jax.ShapeDtypeStruct((t, d), BF16)
    return pl.pallas_call(body, out_shape=(shp, shp, jax.ShapeDtypeStruct((1, d), F32)), grid=(t // tm,),
                          in_specs=[row, row, row, _const_spec((1, d))], out_specs=(row, row, _const_spec((1, d))),
                          name=name, compiler_params=_params("arbitrary"))(dyn, y, z, ng)


def _bucket_onehot():
    blk = CHUNK
    qi = jnp.arange(blk)[:, None]
    kj = jnp.arange(2 * blk)[None, :]
    dist = jnp.maximum(qi + blk - kj, 0)
    max_exact = REL_BUCKETS // 2
    d = jnp.maximum(dist, 1).astype(F32)
    large = max_exact + (jnp.log(d / max_exact) / math.log(REL_MAX_DISTANCE / max_exact) * (REL_BUCKETS - max_exact)).astype(jnp.int32)
    large = jnp.minimum(large, REL_BUCKETS - 1)
    bucket = jnp.where(dist < max_exact, dist, large).reshape(-1)
    return (bucket[None, :] == jnp.arange(REL_BUCKETS)[:, None]).astype(F32)


def _small_mm_hi(a, b, dims, *, name):
    def body(a_ref, b_ref, o_ref):
        o_ref[...] = _dot_hi(a_ref[...], b_ref[...], dims)
    n = b.shape[0] if dims == NT else b.shape[1]
    return pl.pallas_call(body, out_shape=jax.ShapeDtypeStruct((a.shape[0], n), F32), name=name)(a, b)


def _attn_band_mask_t(n, rep):
    blk = CHUNK
    jj = lax.broadcasted_iota(jnp.int32, (2 * blk, rep * blk), 0)
    ii = lax.broadcasted_iota(jnp.int32, (2 * blk, rep * blk), 1) & (blk - 1)
    dist = ii + blk - jj
    in_window = jnp.logical_and(dist >= 0, dist < blk)
    return jnp.logical_and(in_window, jnp.logical_or(jj >= blk, n > 0))


def _sink_row(sink_ref, heads):
    return jnp.concatenate([jnp.broadcast_to(sink_ref[:, h:h + 1], (1, CHUNK)) for h in heads], axis=1)


def _attn_fwd(q, kv, bias_t, sinks, bl, *, name):
    t, qd = q.shape
    blk, hd = CHUNK, ATTN_HEAD_DIM
    kvd = ATTN_KV_HEADS * hd
    rep = ATTN_Q_HEADS // ATTN_KV_HEADS
    nb = t // bl // blk
    scale = hd ** -0.5

    def body(q_ref, kp_ref, kc_ref, vp_ref, vc_ref, bias_ref, sink_ref, o_ref, lse_ref):
        n = pl.program_id(1)
        mask = _attn_band_mask_t(n, rep)
        for kvh in range(ATTN_KV_HEADS):
            ks = slice(kvh * hd, (kvh + 1) * hd)
            heads = range(kvh * rep, (kvh + 1) * rep)
            qs = jnp.concatenate([q_ref[:, h * hd:(h + 1) * hd] for h in heads], axis=0)
            kk = jnp.concatenate([kp_ref[:, ks], kc_ref[:, ks]], axis=0)
            vv = jnp.concatenate([vp_ref[:, ks], vc_ref[:, ks]], axis=0)
            s = jnp.where(mask, _dot(kk, qs, NT) * scale + bias_ref[kvh], NEG_INF)
            sink = _sink_row(sink_ref, heads)
            m = jnp.maximum(jnp.max(s, axis=0, keepdims=True), sink)
            p = jnp.exp(s - m)
            den = jnp.sum(p, axis=0, keepdims=True) + jnp.exp(sink - m)
            o = _dot((p * (1.0 / den)).astype(BF16), vv, TN)
            lse = m + jnp.log(den)
            for r, h in enumerate(heads):
                o_ref[:, h * hd:(h + 1) * hd] = o[r * blk:(r + 1) * blk].astype(o_ref.dtype)
                lse_ref[h:h + 1, :] = lse[:, r * blk:(r + 1) * blk]

    def cur(b, n):
        return b * nb + n

    def prev(b, n):
        return b * nb + jnp.maximum(n - 1, 0)
    return pl.pallas_call(
        body, out_shape=(jax.ShapeDtypeStruct((t, qd), BF16), jax.ShapeDtypeStruct((t // blk * ATTN_Q_HEADS, blk), F32)), grid=(bl, nb),
        in_specs=[pl.BlockSpec((blk, qd), lambda b, n: (cur(b, n), 0)),
                  pl.BlockSpec((blk, kvd), lambda b, n: (prev(b, n), 0)), pl.BlockSpec((blk, kvd), lambda b, n: (cur(b, n), 0)),
                  pl.BlockSpec((blk, kvd), lambda b, n: (prev(b, n), 1)), pl.BlockSpec((blk, kvd), lambda b, n: (cur(b, n), 1)),
                  _const_spec(bias_t.shape), _const_spec((1, LANES))],
        out_specs=(pl.BlockSpec((blk, qd), lambda b, n: (cur(b, n), 0)),
                   pl.BlockSpec((ATTN_Q_HEADS, blk), lambda b, n: (cur(b, n), 0))),
        name=name, compiler_params=_params("parallel", "arbitrary"))(q, kv, kv, kv, kv, bias_t, sinks)


def _attn_bwd(do, q, kv, lse, bias_t, sinks, bl, *, name):
    t, qd = q.shape
    blk, hd = CHUNK, ATTN_HEAD_DIM
    kvd = ATTN_KV_HEADS * hd
    rep = ATTN_Q_HEADS // ATTN_KV_HEADS
    s_len = t // bl
    nb = s_len // blk
    scale = hd ** -0.5

    def body(do_ref, q_ref, kp_ref, kc_ref, vp_ref, vc_ref, lse_ref, bias_ref, sink_ref, dq_ref, dkv_ref, dbias_ref, dsink_ref):
        n = pl.program_id(1)

        @pl.when(jnp.logical_and(pl.program_id(0) == 0, n == 0))
        def _():
            dbias_ref[...] = jnp.zeros_like(dbias_ref)
            dsink_ref[...] = jnp.zeros_like(dsink_ref)

        mask = _attn_band_mask_t(n, rep)
        r_cur = pl.multiple_of(n * blk, blk)
        r_prev = pl.multiple_of(jnp.maximum(n - 1, 0) * blk, blk)
        dsink = jnp.zeros((1, LANES), F32)
        for kvh in range(ATTN_KV_HEADS):
            ks = slice(kvh * hd, (kvh + 1) * hd)
            heads = range(kvh * rep, (kvh + 1) * rep)
            qs = jnp.concatenate([q_ref[:, h * hd:(h + 1) * hd] for h in heads], axis=0)
            dos = jnp.concatenate([do_ref[:, h * hd:(h + 1) * hd] for h in heads], axis=0)
            kk = jnp.concatenate([kp_ref[:, ks], kc_ref[:, ks]], axis=0)
            vv = jnp.concatenate([vp_ref[:, ks], vc_ref[:, ks]], axis=0)
            lse = jnp.concatenate([lse_ref[h:h + 1, :] for h in heads], axis=1)
            p = jnp.exp(jnp.where(mask, _dot(kk, qs, NT) * scale + bias_ref[kvh], NEG_INF) - lse)
            dp = _dot(vv, dos, NT)
            delta = jnp.sum(p * dp, axis=0, keepdims=True)
            ds = p * (dp - delta)
            dsink_row = jnp.exp(_sink_row(sink_ref, heads) - lse) * delta
            dbias_ref[kvh] += ds
            ds_b = ds.astype(BF16)
            dq_s = _dot(ds_b, kk, TN) * scale
            dkk = _dot(ds_b, qs, NN) * scale
            dvv = _dot(p.astype(BF16), dos, NN)
            for r, h in enumerate(heads):
                dq_ref[:, h * hd:(h + 1) * hd] = dq_s[r * blk:(r + 1) * blk].astype(dq_ref.dtype)
                dsink = dsink - jnp.sum(dsink_row[:, r * blk:(r + 1) * blk], axis=1, keepdims=True) * _lane_onehot(h)
            vs = slice(kvd + kvh * hd, kvd + (kvh + 1) * hd)
            dkv_ref[pl.ds(r_cur, blk), ks] = dkk[blk:]
            dkv_ref[pl.ds(r_cur, blk), vs] = dvv[blk:]

            @pl.when(n > 0)
            def _():
                dkv_ref[pl.ds(r_prev, blk), ks] += dkk[:blk]
                dkv_ref[pl.ds(r_prev, blk), vs] += dvv[:blk]
        dsink_ref[...] += dsink

    def cur(b, n):
        return b * nb + n

    def prev(b, n):
        return b * nb + jnp.maximum(n - 1, 0)
    qspec = pl.BlockSpec((blk, qd), lambda b, n: (cur(b, n), 0))
    return pl.pallas_call(
        body,
        out_shape=(jax.ShapeDtypeStruct((t, qd), BF16), jax.ShapeDtypeStruct((t, 2 * kvd), F32),
                   jax.ShapeDtypeStruct(bias_t.shape, F32), jax.ShapeDtypeStruct((1, LANES), F32)),
        grid=(bl, nb),
        in_specs=[qspec, qspec,
                  pl.BlockSpec((blk, kvd), lambda b, n: (prev(b, n), 0)), pl.BlockSpec((blk, kvd), lambda b, n: (cur(b, n), 0)),
                  pl.BlockSpec((blk, kvd), lambda b, n: (prev(b, n), 1)), pl.BlockSpec((blk, kvd), lambda b, n: (cur(b, n), 1)),
                  pl.BlockSpec((ATTN_Q_HEADS, blk), lambda b, n: (cur(b, n), 0)), _const_spec(bias_t.shape), _const_spec((1, LANES))],
        out_specs=(qspec, pl.BlockSpec((s_len, 2 * kvd), lambda b, n: (b, 0)), _const_spec(bias_t.shape), _const_spec((1, LANES))),
        name=name, compiler_params=_params("arbitrary", "arbitrary"))(do, q, kv, kv, kv, kv, lse, bias_t, sinks)


def _merge_fwd(y, z, ng, o, gs, ga, w_ssm, w_attn, w_out, h_in, g_post, *, name):
    t, d = h_in.shape
    tm = _tile(t, 256)
    d_ssm = y.shape[1]
    gw = d_ssm // SSM_GROUPS

    def body(y_ref, z_ref, ng_ref, o_ref, gs_ref, ga_ref, ws_ref, wa_ref, wo_ref, hin_ref, gp_ref,
             yn_ref, ys_ref, ya_ref, mg_ref, mix_ref, hout_ref):
        for g in range(SSM_GROUPS):
            sl = slice(g * gw, (g + 1) * gw)
            zv = z_ref[:, sl].astype(F32)
            yg = y_ref[:, sl] * (zv * _sigmoid(zv))
            r = lax.rsqrt(jnp.mean(yg * yg, axis=-1, keepdims=True) + RMS_EPS)
            yn_ref[:, sl] = (yg * r * ng_ref[:, sl]).astype(BF16)
        ys = _dot(yn_ref[...], ws_ref[...], NN)
        ya = _dot(o_ref[...], wa_ref[...], NN)
        merged = (_sigmoid(gs_ref[...].astype(F32)) * ys + _sigmoid(ga_ref[...].astype(F32)) * ya).astype(BF16)
        mix = _dot(merged, wo_ref[...], NN)
        ys_ref[...] = ys.astype(BF16)
        ya_ref[...] = ya.astype(BF16)
        mg_ref[...] = merged
        mix_ref[...] = mix
        hout_ref[...] = _rms_residual(mix, hin_ref[...], gp_ref[...], 1.0)

    def row(w):
        return pl.BlockSpec((tm, w), lambda i: (i, 0))
    bshape = jax.ShapeDtypeStruct((t, d), BF16)
    fshape = jax.ShapeDtypeStruct((t, d), F32)
    return pl.pallas_call(
        body, out_shape=(jax.ShapeDtypeStruct((t, d_ssm), BF16), bshape, bshape, bshape, fshape, fshape), grid=(t // tm,),
        in_specs=[row(d_ssm), row(d_ssm), _const_spec((1, d_ssm)), row(o.shape[1]), row(d), row(d), _resident_spec(w_ssm.shape),
                  _resident_spec(w_attn.shape), _resident_spec(w_out.shape), row(d), _const_spec((1, d))],
        out_specs=(row(d_ssm),) + (row(d),) * 5, name=name,
        compiler_params=_params("parallel"))(y, z, ng, o, gs, ga, w_ssm, w_attn, w_out, h_in, g_post)


def _merge_bwd(dh, mix, g_post, gs, ga, ys, ya, w_ssm, w_attn, w_out, *, name):
    t, d = mix.shape
    tm = _tile(t, 256)
    d_ssm, d_attn = w_ssm.shape[0], w_attn.shape[0]

    def body(dh_ref, mix_ref, gp_ref, gs_ref, ga_ref, ys_ref, ya_ref, ws_ref, wa_ref, wo_ref,
             dmix_ref, dys_ref, dya_ref, dgs_ref, dga_ref, dyn_ref, do_ref, dgp_ref):
        @pl.when(pl.program_id(0) == 0)
        def _():
            dgp_ref[...] = jnp.zeros_like(dgp_ref)
        mv = mix_ref[...]
        dy = dh_ref[...]
        r = lax.rsqrt(jnp.mean(mv * mv, axis=-1, keepdims=True) + RMS_EPS)
        mhat = mv * r
        dyg = dy * gp_ref[...]
        dmix = (r * (dyg - mhat * jnp.mean(dyg * mhat, axis=-1, keepdims=True))).astype(BF16)
        dgp_ref[...] += jnp.sum(dy * mhat, axis=0, keepdims=True)
        dmix_ref[...] = dmix
        dmerged = _dot(dmix, wo_ref[...], NT)
        sgs = _sigmoid(gs_ref[...].astype(F32))
        sga = _sigmoid(ga_ref[...].astype(F32))
        dys = (dmerged * sgs).astype(BF16)
        dya = (dmerged * sga).astype(BF16)
        dys_ref[...] = dys
        dya_ref[...] = dya
        dgs_ref[...] = (dmerged * ys_ref[...].astype(F32) * sgs * (1.0 - sgs)).astype(BF16)
        dga_ref[...] = (dmerged * ya_ref[...].astype(F32) * sga * (1.0 - sga)).astype(BF16)
        dyn_ref[...] = _dot(dys, ws_ref[...], NT).astype(BF16)
        do_ref[...] = _dot(dya, wa_ref[...], NT).astype(BF16)

    def row(w):
        return pl.BlockSpec((tm, w), lambda i: (i, 0))

    def bshape(w):
        return jax.ShapeDtypeStruct((t, w), BF16)
    return pl.pallas_call(
        body, out_shape=(bshape(d),) * 5 + (bshape(d_ssm), bshape(d_attn), jax.ShapeDtypeStruct((1, d), F32)), grid=(t // tm,),
        in_specs=[row(d), row(d), _const_spec((1, d)), row(d), row(d), row(d), row(d),
                  _resident_spec(w_ssm.shape), _resident_spec(w_attn.shape), _resident_spec(w_out.shape)],
        out_specs=(row(d),) * 5 + (row(d_ssm), row(d_attn), _const_spec((1, d))),
        name=name, compiler_params=_params("arbitrary"))(dh, mix, g_post, gs, ga, ys, ya, w_ssm, w_attn, w_out)


def _position():
    return lax.axis_index("x"), lax.axis_index("y"), lax.axis_index("c")


def _gather_exchange(shards):
    na = len(shards)

    def plan(ins, outs, sems):
        send_sems, recv_sems, local_sems = sems
        x, y, c = _position()
        me, sibling = (x, y, c), (x, y, 1 - c)
        chips = [(1 - x, y), (x, 1 - y), (1 - x, 1 - y)]

        def slot(a, pos):
            return outs[a].at[4 * pos[0] + 2 * pos[1] + pos[2]]

        def copy(a, k, block, to, src=None):
            return pltpu.make_async_remote_copy(
                src_ref=slot(a, block) if src is None else src, dst_ref=slot(a, block),
                send_sem=send_sems.at[a, k], recv_sem=recv_sems.at[a, k], device_id=to, device_id_type=MESH)

        mine = [pltpu.make_async_copy(ins[a], slot(a, me), local_sems.at[a]) for a in range(na)]
        first = []
        for a in range(na):
            first.append(copy(a, 0, me, sibling, src=ins[a]))
            first += [copy(a, 1 + j, me, (*chip, c), src=ins[a]) for j, chip in enumerate(chips)]
        return me, sibling, chips, copy, mine, first

    def start(ins, outs, sems):
        *_, mine, first = plan(ins, outs, sems)
        for cp in mine + first:
            cp.start()

    def finish(ins, outs, sems):
        me, sibling, chips, copy, mine, first = plan(ins, outs, sems)
        c = me[2]
        passed = []
        for a in range(na):
            for j, chip in enumerate(chips):
                copy(a, 1 + j, (*chip, c), me).wait_recv()
                fwd = copy(a, 4 + j, (*chip, c), sibling)
                fwd.start()
                passed.append(fwd)
        for a in range(na):
            copy(a, 0, sibling, me).wait_recv()
            for j, chip in enumerate(chips):
                copy(a, 4 + j, (*chip, 1 - c), me).wait_recv()
        for cp in first + passed:
            cp.wait_send()
        for cp in mine:
            cp.wait()

    return _Exchange(list(shards), [jax.ShapeDtypeStruct((N_DEV,) + s.shape, s.dtype) for s in shards],
                     [pltpu.SemaphoreType.DMA((na, 7)), pltpu.SemaphoreType.DMA((na, 7)), pltpu.SemaphoreType.DMA((na,))],
                     start, finish)


def _scatter_exchange(arrays):
    na = len(arrays)

    def copies(ins, outs, sems):
        send_sems, recv_sems = sems
        x, y, c = _position()
        out = []
        for a in range(na):
            for k in range(7):
                flip = k + 1
                peer = (x ^ (flip >> 2), y ^ ((flip >> 1) & 1), c ^ (flip & 1))
                peer_block = 4 * peer[0] + 2 * peer[1] + peer[2]
                out.append(pltpu.make_async_remote_copy(
                    src_ref=ins[a].at[peer_block], dst_ref=outs[a].at[k],
                    send_sem=send_sems.at[a, k], recv_sem=recv_sems.at[a, k], device_id=peer, device_id_type=MESH))
        return out

    def start(ins, outs, sems):
        for cp in copies(ins, outs, sems):
            cp.start()

    def finish(ins, outs, sems):
        for cp in copies(ins, outs, sems):
            cp.wait()

    return _Exchange(list(arrays), [jax.ShapeDtypeStruct((7,) + s.shape[1:], s.dtype) for s in arrays],
                     [pltpu.SemaphoreType.DMA((na, 7)), pltpu.SemaphoreType.DMA((na, 7))], start, finish)


def _reduce_adamw(own, recv, w, m, v, *, name):
    r, c = own.shape
    tm = _tile(r, 256)
    c1 = 1.0 - ADAM_B1 ** ADAM_STEP
    c2 = 1.0 - ADAM_B2 ** ADAM_STEP

    def body(own_ref, recv_ref, w_ref, m_ref, v_ref, g_ref, d_ref, mo_ref, vo_ref):
        gv = own_ref[...]
        for k in range(7):
            gv = gv + recv_ref[k].astype(F32)
        g_ref[...] = gv
        mn = ADAM_B1 * m_ref[...] + (1.0 - ADAM_B1) * gv
        vn = ADAM_B2 * v_ref[...] + (1.0 - ADAM_B2) * (gv * gv)
        mo_ref[...] = mn
        vo_ref[...] = vn
        d_ref[...] = -ADAM_LR * ((mn / c1) / (jnp.sqrt(vn / c2) + ADAM_EPS) + ADAM_WD * w_ref[...])

    blk = pl.BlockSpec((tm, c), lambda i: (i, 0))
    shp = jax.ShapeDtypeStruct((r, c), F32)
    return pl.pallas_call(body, out_shape=(shp,) * 4, grid=(r // tm,),
                          in_specs=[blk, pl.BlockSpec((7, tm, c), lambda i: (0, i, 0)), blk, blk, blk], out_specs=(blk,) * 4,
                          name=name, compiler_params=_params("parallel"))(own, recv, w, m, v)


SMALL_ROW = 8 * LANES


def _update_replicated(partials, ws, ms, vs, *, name):
    n = len(ws)
    pieces, n_rows = [], 0
    for k in sorted(range(n), key=lambda k_: -ws[k_].shape[0]):
        r, c = ws[k].shape
        assert (r == 1 or c <= SMALL_ROW) and (r == 1 or n_rows % 8 == 0)
        for c0 in range(0, c, SMALL_ROW):
            pieces.append((k, slice(0, r), slice(c0, min(c0 + SMALL_ROW, c)), n_rows))
            n_rows += r
    n_rows = -(-n_rows // 8) * 8
    c1 = 1.0 - ADAM_B1 ** ADAM_STEP
    c2 = 1.0 - ADAM_B2 ** ADAM_STEP

    def reduce_body(*refs):
        g_in, total, buf, send_sems, recv_sems = refs[:n], refs[n], refs[n + 1], refs[n + 2], refs[n + 3]
        x, y, c_ = _position()
        me = 4 * x + 2 * y + c_
        buf[me] = jnp.zeros((n_rows, SMALL_ROW), F32)
        for k, rs, cs, row0 in pieces:
            buf[me, row0:row0 + rs.stop, 0:cs.stop - cs.start] = g_in[k][rs, cs]
        copies = []
        for j in range(7):
            flip = j + 1
            peer = (x ^ (flip >> 2), y ^ ((flip >> 1) & 1), c_ ^ (flip & 1))
            cp = pltpu.make_async_remote_copy(
                src_ref=buf.at[me], dst_ref=buf.at[me], send_sem=send_sems.at[j], recv_sem=recv_sems.at[j],
                device_id=peer, device_id_type=MESH)
            cp.start()
            copies.append(cp)
        for cp in copies:
            cp.wait()
        acc = buf[0]
        for dev in range(1, N_DEV):
            acc = acc + buf[dev]
        total[...] = acc

    def step_body(*refs):
        total = refs[0]
        w_in_, m_in, v_in = (refs[1 + j * n:1 + (j + 1) * n] for j in range(3))
        g_out, d_out, m_out, v_out = (refs[1 + (3 + j) * n:1 + (4 + j) * n] for j in range(4))
        for k, rs, cs, row0 in pieces:
            gv = total[row0:row0 + rs.stop, 0:cs.stop - cs.start]
            mn = ADAM_B1 * m_in[k][rs, cs] + (1.0 - ADAM_B1) * gv
            vn = ADAM_B2 * v_in[k][rs, cs] + (1.0 - ADAM_B2) * (gv * gv)
            g_out[k][rs, cs] = gv
            m_out[k][rs, cs] = mn
            v_out[k][rs, cs] = vn
            d_out[k][rs, cs] = -ADAM_LR * ((mn / c1) / (jnp.sqrt(vn / c2) + ADAM_EPS) + ADAM_WD * w_in_[k][rs, cs])

    vm = pl.BlockSpec(memory_space=pltpu.VMEM)
    total = pl.pallas_call(
        reduce_body, out_shape=jax.ShapeDtypeStruct((n_rows, SMALL_ROW), F32), in_specs=[vm] * n, out_specs=vm,
        scratch_shapes=[pltpu.VMEM((N_DEV, n_rows, SMALL_ROW), F32), pltpu.SemaphoreType.DMA((7,)), pltpu.SemaphoreType.DMA((7,))],
        name=f"{name}_allreduce")(*partials)
    shapes = tuple(jax.ShapeDtypeStruct(w.shape, F32) for w in ws)
    outs = pl.pallas_call(step_body, out_shape=shapes * 4, in_specs=[vm] * (1 + 3 * n), out_specs=tuple([vm] * (4 * n)),
                          name=f"{name}_adamw")(total, *ws, *ms, *vs)
    return tuple(outs[j * n:(j + 1) * n] for j in range(4))


def _pad_lanes(v, width=LANES):
    return jnp.pad(v, ((0, 0), (0, width - v.shape[1])))


def kernel(x, ffn1_pre_g, ffn1_w_gate, ffn1_w_up, ffn1_w_down, ffn1_post_g, mix_pre_g, w_in, conv_w, conv_b, dt_bias, a_log, d_skip, ssm_norm_g, w_ssm_proj, attn_sinks, rel_bias_table, w_attn_proj, w_out, mix_post_g, ffn2_pre_g, ffn2_w_gate, ffn2_w_up, ffn2_w_down, ffn2_post_g, loss_target, m_ffn1_pre_g, m_ffn1_w_gate, m_ffn1_w_up, m_ffn1_w_down, m_ffn1_post_g, m_mix_pre_g, m_w_in, m_conv_w, m_conv_b, m_dt_bias, m_a_log, m_d_skip, m_ssm_norm_g, m_w_ssm_proj, m_attn_sinks, m_rel_bias_table, m_w_attn_proj, m_w_out, m_mix_post_g, m_ffn2_pre_g, m_ffn2_w_gate, m_ffn2_w_up, m_ffn2_w_down, m_ffn2_post_g, v_ffn1_pre_g, v_ffn1_w_gate, v_ffn1_w_up, v_ffn1_w_down, v_ffn1_post_g, v_mix_pre_g, v_w_in, v_conv_w, v_conv_b, v_dt_bias, v_a_log, v_d_skip, v_ssm_norm_g, v_w_ssm_proj, v_attn_sinks, v_rel_bias_table, v_w_attn_proj, v_w_out, v_mix_post_g, v_ffn2_pre_g, v_ffn2_w_gate, v_ffn2_w_up, v_ffn2_w_down, v_ffn2_post_g):
    args = dict(locals())
    weight_names = ['ffn1_pre_g', 'ffn1_w_gate', 'ffn1_w_up', 'ffn1_w_down', 'ffn1_post_g', 'mix_pre_g', 'w_in', 'conv_w', 'conv_b',
                    'dt_bias', 'a_log', 'd_skip', 'ssm_norm_g', 'w_ssm_proj', 'attn_sinks', 'rel_bias_table', 'w_attn_proj', 'w_out',
                    'mix_post_g', 'ffn2_pre_g', 'ffn2_w_gate', 'ffn2_w_up', 'ffn2_w_down', 'ffn2_post_g']
    col_sharded = ('ffn1_w_gate', 'ffn1_w_up', 'w_in', 'ffn2_w_gate', 'ffn2_w_up')
    row_sharded = ('ffn1_w_down', 'w_ssm_proj', 'w_attn_proj', 'w_out', 'ffn2_w_down')
    big = col_sharded + row_sharded

    bl, s_len, d = x.shape
    t = bl * s_len
    d_inner = ssm_norm_g.shape[1]
    n_heads = dt_bias.shape[1]
    gn = SSM_GROUPS * SSM_STATE
    conv_dim = d_inner + 2 * gn
    q_dim = ATTN_Q_HEADS * ATTN_HEAD_DIM
    kv_dim = ATTN_KV_HEADS * ATTN_HEAD_DIM

    def local_2d(name, a):
        a = a[0]
        return a.T if name in col_sharded else a

    ffn1_names = ('ffn1_w_gate', 'ffn1_w_up', 'ffn1_w_down')
    ffn2_names = ('ffn2_w_gate', 'ffn2_w_up', 'ffn2_w_down')
    mixer_names = ('w_ssm_proj', 'w_attn_proj', 'w_out')

    def shard(n):
        return local_2d(n, args[n]).astype(BF16)

    def rows(g):
        return g.reshape(N_DEV * g.shape[1], g.shape[2])

    x2 = x.reshape(t, d)
    tgt2 = loss_target.reshape(t, d)
    full = {}

    (h1,), saved1, ffn1_w, got_in = _ffn_forward(
        x2, ffn1_pre_g, lambda got: (rows(got[0]), rows(got[1])), lambda got: rows(got[2]), ffn1_post_g, "ffn1",
        side_norm=_gather_exchange([shard(n) for n in ffn1_names[:2]]),
        side_up=_gather_exchange([shard('w_in'), conv_w[0], shard('ffn1_w_down')]))
    full.update(zip(ffn1_names, ffn1_w))
    conv_w_full = jnp.transpose(got_in[1], (1, 0, 2)).reshape(SSM_CONV, conv_dim)

    win_t = rows(got_in[0])
    dt_lo = 2 * d + d_inner + conv_dim
    off = {'gs': 0, 'ga': d, 'z': 2 * d, 'xbc': 2 * d + d_inner, 'dt': dt_lo, 'q': dt_lo + n_heads, 'kv': dt_lo + n_heads + q_dim}
    assert all(o_ % 16 == 0 for o_ in off.values())

    u, (gs, ga, z, xbc, q, kv, dt_raw) = _proj_all(
        h1, mix_pre_g, win_t,
        [(off['gs'], d, BF16), (off['ga'], d, BF16), (off['z'], d_inner, BF16), (off['xbc'], conv_dim, BF16),
         (off['q'], q_dim, BF16), (off['kv'], 2 * kv_dim, BF16), (off['dt'], n_heads, F32)], name="mix_proj")

    dtb_p, alog_p, dsk_p, sinks_p = _pad_lanes(dt_bias), _pad_lanes(a_log), _pad_lanes(d_skip), _pad_lanes(attn_sinks)
    xc = _conv_fwd(xbc, conv_w_full, conv_b, bl, name="conv_fwd")
    late_names = mixer_names + ffn2_names
    (y, hprev), got_late = _ssd_fwd(xc, dt_raw, dtb_p, alog_p, dsk_p, bl, n_heads,
                                    side=_gather_exchange([shard(n) for n in late_names]), name="ssd_fwd")
    full.update({n: rows(g) for n, g in zip(late_names, got_late)})

    onehot = _bucket_onehot()
    rep = ATTN_Q_HEADS // ATTN_KV_HEADS
    bias = _small_mm_hi(rel_bias_table.T, onehot, NN, name="rel_bias")
    bias_t = jnp.transpose(bias.reshape(ATTN_KV_HEADS, rep, CHUNK, 2 * CHUNK), (0, 3, 1, 2)).reshape(ATTN_KV_HEADS, 2 * CHUNK, rep * CHUNK)
    o, lse = _attn_fwd(q, kv, bias_t, sinks_p, bl, name="attn_fwd")

    yn, ys, ya, merged, mix, h2 = _merge_fwd(y, z, ssm_norm_g, o, gs, ga, full['w_ssm_proj'], full['w_attn_proj'], full['w_out'],
                                             h1, mix_post_g, name="merge_fwd")

    (dh3, loss_vec), saved2, _, _ = _ffn_forward(h2, ffn2_pre_g, (full['ffn2_w_gate'], full['ffn2_w_up']), full['ffn2_w_down'],
                                                 ffn2_post_g, "ffn2", target=tgt2)
    loss = lax.psum(loss_vec[0, 0], ("x", "y", "c"))

    grads, own, wire, received = {}, {}, {}, {}
    dh2, grads['ffn2_pre_g'], grads['ffn2_post_g'], g32, g16, _ = _ffn_backward(
        dh3, saved2, ffn2_pre_g, full['ffn2_w_gate'], full['ffn2_w_up'], full['ffn2_w_down'], ffn2_post_g, "ffn2")
    own.update(zip(ffn2_names, map(_stack8, g32)))
    wire.update(zip(ffn2_names, map(_stack8, g16)))

    dmix, dys, dya, dgs, dga, dyn, do, grads['mix_post_g'] = _merge_bwd(
        dh2, mix, mix_post_g, gs, ga, ys, ya, full['w_ssm_proj'], full['w_attn_proj'], full['w_out'], name="merge_bwd")
    for n, (lhs, rhs) in zip(mixer_names, ((yn, dys), (o, dya), (merged, dmix))):
        g32_, g16_, _ = _mm_tn([lhs], rhs, name=f"d{n}")
        own[n], wire[n] = _stack8(g32_), _stack8(g16_)

    dq, dkv, dbias_t, dsinks = _attn_bwd(do, q, kv, lse, bias_t, sinks_p, bl, name="attn_bwd")
    dbias = jnp.transpose(dbias_t.reshape(ATTN_KV_HEADS, 2 * CHUNK, rep, CHUNK), (0, 2, 3, 1)).reshape(ATTN_Q_HEADS, -1)
    d_table = _small_mm_hi(onehot, dbias, NT, name="rel_bias_bwd")

    dy, dz, grads['ssm_norm_g'] = _gated_norm_bwd(dyn, y, z, ssm_norm_g, name="gated_norm_bwd")

    first_group = ffn2_names + mixer_names
    (dxc, ddt_raw, ddtb, dalog, ddsk), got = _ssd_bwd(dy, y, xc, dt_raw, hprev, dtb_p, alog_p, dsk_p, bl, n_heads,
                                                      side=_scatter_exchange([wire[n] for n in first_group]), name="ssd_bwd")
    received.update(zip(first_group, got))
    dxbc, dconv_w8, grads['conv_b'] = _conv_bwd(dxc, xbc, conv_w_full, conv_b, bl, name="conv_bwd")

    wide32, wide16, _ = _mm_tn([dgs, dga, dz, dxbc, dq], u, name="dw_in")
    kv32, kv16, _ = _mm_tn([dkv], u, name="dw_in_kv")
    dt32, dt16, _ = _mm_tn([ddt_raw], u, name="dw_in_dt")

    def original_order(wide, kv_part, dt_part):
        return jnp.concatenate([wide[:dt_lo], dt_part[:n_heads], wide[dt_lo:], kv_part], axis=0)
    me = 4 * lax.axis_index("x") + 2 * lax.axis_index("y") + lax.axis_index("c")
    blk_rows = win_t.shape[0] // N_DEV
    wire['w_in'] = _stack8(original_order(wide16, kv16, dt16))
    own_w_in = lax.dynamic_slice_in_dim(original_order(wide32, kv32, dt32), me * blk_rows, blk_rows)
    own['conv_w'] = jnp.transpose(dconv_w8[:SSM_CONV].reshape(SSM_CONV, N_DEV, conv_dim // N_DEV), (1, 0, 2))

    segs = [(g_, 0, off[k_]) for g_, k_ in zip([dgs, dga, dz, dxbc, dq, dkv, ddt_raw], ('gs', 'ga', 'z', 'xbc', 'q', 'kv', 'dt'))]
    dh1, grads['mix_pre_g'], got = _mm_nn_rmsbwd(segs, [win_t], h1, mix_pre_g, dh2,
                                                 side=_scatter_exchange([wire['w_in'], own['conv_w']]), name="mix_du")
    received.update(zip(('w_in', 'conv_w'), got))

    dx2, grads['ffn1_pre_g'], grads['ffn1_post_g'], g32, _, got = _ffn_backward(
        dh1, saved1, ffn1_pre_g, full['ffn1_w_gate'], full['ffn1_w_up'], full['ffn1_w_down'], ffn1_post_g, "ffn1", chain=True)
    own.update(zip(ffn1_names, map(_stack8, g32)))
    received.update(zip(ffn1_names, got))

    def own_block(a):
        return lax.dynamic_index_in_dim(a, me, 0, keepdims=False)
    out_g, out_d, out_m, out_v = {}, {}, {}, {}
    for n in big:
        w2, m2, v2 = local_2d(n, args[n]), local_2d(n, args['m_' + n]), local_2d(n, args['v_' + n])
        results = _reduce_adamw(own_w_in if n == 'w_in' else own_block(own[n]), received[n], w2, m2, v2, name=f"update_{n}")
        out_g[n], out_d[n], out_m[n], out_v[n] = ((a.T if n in col_sharded else a)[None] for a in results)
    results = _reduce_adamw(own_block(own['conv_w']), received['conv_w'], conv_w[0], m_conv_w[0], v_conv_w[0], name="update_conv_w")
    out_g['conv_w'], out_d['conv_w'], out_m['conv_w'], out_v['conv_w'] = (a[None] for a in results)

    grads['dt_bias'], grads['a_log'], grads['d_skip'], grads['attn_sinks'], grads['rel_bias_table'] = ddtb, dalog, ddsk, dsinks, d_table
    small = [n for n in weight_names if n not in big and n != 'conv_w']
    results = _update_replicated([grads[n] for n in small], [args[n] for n in small], [args['m_' + n] for n in small],
                                 [args['v_' + n] for n in small], name="update_replicated")
    for dst, vals in zip((out_g, out_d, out_m, out_v), results):
        dst.update(zip(small, vals))

    grad_x = dx2.reshape(bl, s_len, d)
    return (loss, grad_x, *[out_g[n] for n in weight_names], *[out_d[n] for n in weight_names],
            *[out_m[n] for n in weight_names], *[out_v[n] for n in weight_names])
```

```python
import functools
import math

import numpy as np
import jax
import jax.numpy as jnp
from jax import lax
from jax.experimental import pallas as pl
from jax.experimental.pallas import tpu as pltpu

F32 = jnp.float32
BF16 = jnp.bfloat16
MESH = pl.DeviceIdType.MESH
N_DEV = 8

SSM_HEAD_DIM = 64
SSM_GROUPS = 4
SSM_STATE = 128
SSM_CONV = 4
CHUNK = 128
ATTN_HEAD_DIM = 64
ATTN_Q_HEADS = 16
ATTN_KV_HEADS = 4
REL_BUCKETS = 32
REL_MAX_DISTANCE = 128
RMS_EPS = 1e-6
FFN_RESIDUAL_WEIGHT = 0.5
ADAM_LR, ADAM_B1, ADAM_B2, ADAM_EPS, ADAM_WD, ADAM_STEP = 0.001, 0.9, 0.999, 1e-08, 0.01, 10

LANES = 128
VMEM_LIMIT_BYTES = 56 * 1024 * 1024
FFN_COL_TILE = 1408
SSD_HEAD_BATCH = 2

NEG_INF = float("-inf")


def _params(*sem):
    return pltpu.CompilerParams(dimension_semantics=sem, vmem_limit_bytes=VMEM_LIMIT_BYTES)


def _tile(n, pref, mult=8):
    if n <= pref:
        return n
    t = (pref // mult) * mult
    while t >= mult:
        if n % t == 0:
            return t
        t -= mult
    return n


def _sigmoid(x):
    return 1.0 / (1.0 + jnp.exp(-x))


def _dot(a, b, dims):
    return lax.dot_general(a, b, (dims, ((), ())), preferred_element_type=F32)


NN = ((1,), (0,))
NT = ((1,), (1,))
TN = ((0,), (0,))


def _dot_hi(a, b, dims=NN):
    return lax.dot_general(a, b, (dims, ((), ())), preferred_element_type=F32, precision=lax.Precision.HIGHEST)


def _const_spec(shape):
    nd = len(shape)
    return pl.BlockSpec(shape, lambda *_: (0,) * nd)


def _resident_spec(shape):
    nd = len(shape)
    return pl.BlockSpec(shape, lambda *_: (0,) * nd, pipeline_mode=pl.Buffered(1))


class _Exchange:
    def __init__(self, arrays, out_shape, scratch, start, finish):
        self.arrays, self.out_shape, self.scratch, self.start, self.finish = arrays, out_shape, scratch, start, finish


def _hosted_call(body, *, grid, in_specs, out_specs, out_shape, scratch_shapes, operands, side, name):
    in_specs, out_specs, out_shape, scratch_shapes = list(in_specs), list(out_specs), list(out_shape), list(scratch_shapes)
    sem = ("arbitrary",) * len(grid)
    if side is None:
        outs = pl.pallas_call(body, out_shape=tuple(out_shape), grid=grid, in_specs=in_specs, out_specs=tuple(out_specs),
                              scratch_shapes=scratch_shapes, name=name, compiler_params=_params(*sem))(*operands)
        return tuple(outs), ()
    n_in, n_out, n_scr = len(in_specs), len(out_shape), len(scratch_shapes)
    s_in, s_out = len(side.arrays), len(side.out_shape)

    def wrapped(*refs):
        refs = list(refs)
        main_in, side_in = refs[:n_in], refs[n_in:n_in + s_in]
        o0 = n_in + s_in
        main_out, side_out = refs[o0:o0 + n_out], refs[o0 + n_out:o0 + n_out + s_out]
        c0 = o0 + n_out + s_out
        main_scr, side_scr = refs[c0:c0 + n_scr], refs[c0 + n_scr:]
        ids = [pl.program_id(ax) for ax in range(len(grid))]
        first = functools.reduce(jnp.logical_and, [i == 0 for i in ids])
        last = functools.reduce(jnp.logical_and, [i == g - 1 for i, g in zip(ids, grid)])

        @pl.when(first)
        def _():
            side.start(side_in, side_out, side_scr)

        body(*main_in, *main_out, *main_scr)

        @pl.when(last)
        def _():
            side.finish(side_in, side_out, side_scr)

    hbm = pl.BlockSpec(memory_space=pl.ANY)
    outs = pl.pallas_call(
        wrapped, out_shape=tuple(out_shape + list(side.out_shape)), grid=grid,
        in_specs=in_specs + [hbm] * s_in, out_specs=tuple(out_specs + [hbm] * s_out),
        scratch_shapes=scratch_shapes + list(side.scratch), name=name, compiler_params=_params(*sem))(*operands, *side.arrays)
    return tuple(outs[:n_out]), tuple(outs[n_out:])


def _proj_all(h, g, w, segs, *, name):
    t, d = h.shape
    tm = _tile(t, 512)
    segs = [(row0, wd_, max(wd_, LANES), dt_) for row0, wd_, dt_ in segs]
    assert all(row0 + out_w <= w.shape[0] for row0, _, out_w, _ in segs)

    def body(h_ref, g_ref, w_ref, u_ref, *o_refs):
        hv = h_ref[...]
        r = lax.rsqrt(jnp.mean(hv * hv, axis=-1, keepdims=True) + RMS_EPS)
        uv = (hv * r * g_ref[...]).astype(BF16)
        u_ref[...] = uv
        for (row0, width, out_w, _), o_ref in zip(segs, o_refs):
            for c0, c1 in _col_chunks(out_w, 8 * LANES):
                part = _dot(uv, w_ref[row0 + c0:row0 + c1, :], NT)
                if width < out_w:
                    part = jnp.where(lax.broadcasted_iota(jnp.int32, part.shape, 1) < width, part, 0.0)
                o_ref[:, c0:c1] = part.astype(o_ref.dtype)

    row = pl.BlockSpec((tm, d), lambda i: (i, 0))
    outs = pl.pallas_call(
        body, out_shape=(jax.ShapeDtypeStruct((t, d), BF16),) + tuple(jax.ShapeDtypeStruct((t, ow), dt_) for _, _, ow, dt_ in segs),
        grid=(t // tm,), in_specs=[row, _const_spec((1, d)), _resident_spec(w.shape)],
        out_specs=(row,) + tuple(pl.BlockSpec((tm, ow), lambda i: (i, 0)) for _, _, ow, _ in segs),
        name=name, compiler_params=_params("parallel"))(h, g, w)
    return outs[0], outs[1:]


def _mm_tn(a_list, b, *, tm=1408, tk=2048, side=None, name):
    t, n = b.shape
    tk = _tile(t, tk if len(a_list) == 1 else tk // 2)
    nk = t // tk
    widths = [a.shape[1] for a in a_list]
    tm = _tile(math.gcd(*widths), tm, LANES)
    assert all(w % tm == 0 for w in widths)
    starts = np.cumsum([0] + [w // tm for w in widths])
    nseg = len(a_list)

    def a_spec(s):
        lo, hi = int(starts[s]), int(starts[s + 1])

        def idx(i, k):
            active = jnp.logical_and(i >= lo, i < hi)
            return (jnp.where(active, k, 0), jnp.clip(i - lo, 0, hi - lo - 1))
        return pl.BlockSpec((tk, tm), idx)

    def body(*refs):
        a_refs, b_ref, o_ref, o16_ref, acc = refs[:nseg], refs[nseg], refs[nseg + 1], refs[nseg + 2], refs[nseg + 3]
        i, k = pl.program_id(0), pl.program_id(1)

        @pl.when(k == 0)
        def _():
            acc[...] = jnp.zeros_like(acc)

        bv = b_ref[...].astype(BF16)
        for s in range(nseg):
            lo, hi = int(starts[s]), int(starts[s + 1])

            @pl.when(jnp.logical_and(i >= lo, i < hi))
            def _(s=s):
                acc[...] += _dot(a_refs[s][...].astype(BF16), bv, TN)

        @pl.when(k == nk - 1)
        def _():
            o_ref[...] = acc[...]
            o16_ref[...] = acc[...].astype(BF16)

    rows = int(starts[-1]) * tm
    o_spec = pl.BlockSpec((tm, n), lambda i, k: (i, 0))
    (o32, o16), got = _hosted_call(
        body, out_shape=[jax.ShapeDtypeStruct((rows, n), F32), jax.ShapeDtypeStruct((rows, n), BF16)], grid=(int(starts[-1]), nk),
        in_specs=[a_spec(s) for s in range(nseg)] + [pl.BlockSpec((tk, n), lambda i, k: (k, 0))],
        out_specs=[o_spec, o_spec], scratch_shapes=[pltpu.VMEM((tm, n), F32)], operands=list(a_list) + [b], side=side, name=name)
    return o32, o16, got


def _mm_nn_rmsbwd(segs, weights, x, g, dres, *, tm=512, side=None, name):
    t, d = x.shape
    tm = _tile(t, tm)
    nseg, nw = len(segs), len(weights)

    def body(*refs):
        a_refs, w_refs = refs[:nseg], refs[nseg:nseg + nw]
        x_ref, g_ref, dres_ref, dx_ref, dg_ref = refs[nseg + nw:]

        @pl.when(pl.program_id(0) == 0)
        def _():
            dg_ref[...] = jnp.zeros_like(dg_ref)

        dn = None
        for s, (a, w_idx, row0) in enumerate(segs):
            part = _dot(a_refs[s][...].astype(BF16), w_refs[w_idx][row0:row0 + a.shape[1], :], NN)
            dn = part if dn is None else dn + part
        xv = x_ref[...]
        r = lax.rsqrt(jnp.mean(xv * xv, axis=-1, keepdims=True) + RMS_EPS)
        xhat = xv * r
        dyg = dn * g_ref[...]
        dx_ref[...] = dres_ref[...] + r * (dyg - xhat * jnp.mean(dyg * xhat, axis=-1, keepdims=True))
        dg_ref[...] += jnp.sum(dn * xhat, axis=0, keepdims=True)

    row = pl.BlockSpec((tm, d), lambda i: (i, 0))
    in_specs = [pl.BlockSpec((tm, a.shape[1]), lambda i: (i, 0)) for a, _, _ in segs]
    in_specs += [_resident_spec(w.shape) for w in weights] + [row, _const_spec((1, d)), row]
    (dx, dg), extra = _hosted_call(
        body, grid=(t // tm,), in_specs=in_specs, out_specs=[row, _const_spec((1, d))],
        out_shape=[jax.ShapeDtypeStruct((t, d), F32), jax.ShapeDtypeStruct((1, d), F32)], scratch_shapes=[],
        operands=[a for a, _, _ in segs] + list(weights) + [x, g, dres], side=side, name=name)
    return dx, dg, extra


def _col_chunks(width, chunk=4 * LANES):
    return [(c0, min(c0 + chunk, width)) for c0 in range(0, width, chunk)]


def _rms_fwd(x, g, *, side=None, name):
    t, d = x.shape
    tm = _tile(t, 512)

    def body(x_ref, g_ref, o_ref):
        xv = x_ref[...]
        r = lax.rsqrt(jnp.mean(xv * xv, axis=-1, keepdims=True) + RMS_EPS)
        o_ref[...] = (xv * r * g_ref[...]).astype(o_ref.dtype)

    row = pl.BlockSpec((tm, d), lambda i: (i, 0))
    (n,), got = _hosted_call(body, grid=(t // tm,), in_specs=[row, _const_spec((1, d))], out_specs=[row],
                             out_shape=[jax.ShapeDtypeStruct((t, d), BF16)], scratch_shapes=[], operands=[x, g], side=side, name=name)
    return n, got


def _ffn_up(n, wgt, wut, *, side=None, name):
    t, d = n.shape
    f = wgt.shape[0]
    tm, tn = _tile(t, 512), _tile(f, FFN_COL_TILE, LANES)

    def body(n_ref, wg_ref, wu_ref, g_ref, u_ref, h_ref):
        nv = n_ref[...]
        gv = _dot(nv, wg_ref[...], NT)
        uv = _dot(nv, wu_ref[...], NT)
        g_ref[...] = gv.astype(BF16)
        u_ref[...] = uv.astype(BF16)
        h_ref[...] = (gv * _sigmoid(gv) * uv).astype(BF16)

    w_spec = pl.BlockSpec((tn, d), lambda j, i: (j, 0))
    o_spec = pl.BlockSpec((tm, tn), lambda j, i: (i, j))
    shp = jax.ShapeDtypeStruct((t, f), BF16)
    return _hosted_call(body, grid=(f // tn, t // tm), in_specs=[pl.BlockSpec((tm, d), lambda j, i: (i, 0)), w_spec, w_spec],
                        out_specs=[o_spec, o_spec, o_spec], out_shape=[shp, shp, shp], scratch_shapes=[], operands=[n, wgt, wut],
                        side=side, name=name)


def _rms_residual(acc, h, gp, weight):
    r = lax.rsqrt(jnp.mean(acc * acc, axis=-1, keepdims=True) + RMS_EPS)
    return h + weight * (acc * r * gp)


def _ffn_down(hid, wd, h_in, gp, *, target=None, side=None, name):
    t, f = hid.shape
    d = wd.shape[1]
    tm = _tile(t, 256)
    row = pl.BlockSpec((tm, d), lambda i: (i, 0))
    shp = jax.ShapeDtypeStruct((t, d), F32)
    in_specs = [pl.BlockSpec((tm, f), lambda i: (i, 0)), _resident_spec((f, d)), row, _const_spec((1, d))]

    if target is None:
        def body(hid_ref, wd_ref, hin_ref, gp_ref, f_ref, hout_ref):
            acc = _dot(hid_ref[...], wd_ref[...], NN)
            f_ref[...] = acc
            hout_ref[...] = _rms_residual(acc, hin_ref[...], gp_ref[...], FFN_RESIDUAL_WEIGHT)

        return _hosted_call(body, grid=(t // tm,), in_specs=in_specs, out_specs=[row, row], out_shape=[shp, shp], scratch_shapes=[],
                            operands=[hid, wd, h_in, gp], side=side, name=name)

    def body_loss(hid_ref, wd_ref, hin_ref, gp_ref, tgt_ref, f_ref, dh_ref, loss_ref):
        @pl.when(pl.program_id(0) == 0)
        def _():
            loss_ref[...] = jnp.zeros_like(loss_ref)
        acc = _dot(hid_ref[...], wd_ref[...], NN)
        f_ref[...] = acc
        e = _rms_residual(acc, hin_ref[...], gp_ref[...], FFN_RESIDUAL_WEIGHT) - tgt_ref[...]
        dh_ref[...] = e * (1.0 / d)
        per_row = jnp.sum(e * e, axis=1, keepdims=True) * (1.0 / d)
        loss_ref[...] += 0.5 * jnp.sum(per_row, axis=0, keepdims=True)

    return _hosted_call(body_loss, grid=(t // tm,), in_specs=in_specs + [row], out_specs=[row, row, _const_spec((1, LANES))],
                        out_shape=[shp, shp, jax.ShapeDtypeStruct((1, LANES), F32)], scratch_shapes=[],
                        operands=[hid, wd, h_in, gp, target], side=side, name=name)


def _post_bwd(dh, f, gp, weight, *, name):
    t, d = f.shape
    tm = _tile(t, 512)

    def body(dh_ref, f_ref, gp_ref, df_ref, dgp_ref):
        @pl.when(pl.program_id(0) == 0)
        def _():
            dgp_ref[...] = jnp.zeros_like(dgp_ref)
        fv = f_ref[...]
        dy = weight * dh_ref[...]
        r = lax.rsqrt(jnp.mean(fv * fv, axis=-1, keepdims=True) + RMS_EPS)
        fhat = fv * r
        dyg = dy * gp_ref[...]
        df_ref[...] = (r * (dyg - fhat * jnp.mean(dyg * fhat, axis=-1, keepdims=True))).astype(BF16)
        dgp_ref[...] += jnp.sum(dy * fhat, axis=0, keepdims=True)

    row = pl.BlockSpec((tm, d), lambda i: (i, 0))
    return pl.pallas_call(body, out_shape=(jax.ShapeDtypeStruct((t, d), BF16), jax.ShapeDtypeStruct((1, d), F32)), grid=(t // tm,),
                          in_specs=[row, row, _const_spec((1, d))], out_specs=(row, _const_spec((1, d))),
                          name=name, compiler_params=_params("arbitrary"))(dh, f, gp)


def _ffn_dhid(df, wd, g, u, *, name):
    t, d = df.shape
    f = wd.shape[0]
    tm, tn = _tile(t, 512), _tile(f, FFN_COL_TILE, LANES)

    def body(df_ref, wd_ref, g_ref, u_ref, dg_ref, du_ref):
        dh = _dot(df_ref[...], wd_ref[...], NT)
        gv = g_ref[...].astype(F32)
        uv = u_ref[...].astype(F32)
        sg = _sigmoid(gv)
        silu = gv * sg
        dg_ref[...] = (dh * uv * (sg + silu * (1.0 - sg))).astype(BF16)
        du_ref[...] = (dh * silu).astype(BF16)

    o_spec = pl.BlockSpec((tm, tn), lambda j, i: (i, j))
    shp = jax.ShapeDtypeStruct((t, f), BF16)
    return pl.pallas_call(body, out_shape=(shp, shp), grid=(f // tn, t // tm),
                          in_specs=[pl.BlockSpec((tm, d), lambda j, i: (i, 0)), pl.BlockSpec((tn, d), lambda j, i: (j, 0)), o_spec, o_spec],
                          out_specs=(o_spec, o_spec), name=name, compiler_params=_params("parallel", "arbitrary"))(df, wd, g, u)


def _ffn_forward(h_in, g_pre, w_up, wd, g_post, tag, side_norm=None, side_up=None, target=None):
    n, got_norm = _rms_fwd(h_in, g_pre, side=side_norm, name=f"{tag}_prenorm")
    wgt, wut = w_up(got_norm) if callable(w_up) else w_up
    (g, u, hid), got_up = _ffn_up(n, wgt, wut, side=side_up, name=f"{tag}_up")
    wd = wd(got_up) if callable(wd) else wd
    outs, _ = _ffn_down(hid, wd, h_in, g_post, target=target, name=f"{tag}_down")
    return outs[1:], (h_in, n, g, u, hid, outs[0]), (wgt, wut, wd), got_up


def _stack8(g):
    return g.reshape(N_DEV, g.shape[0] // N_DEV, g.shape[1])


def _ffn_backward(dh_out, saved, g_pre, wgt, wut, wd, g_post, tag, chain=False):
    h_in, n, g, u, hid, f = saved

    def side_of(grad16):
        return _scatter_exchange([_stack8(grad16)]) if chain else None

    df, dg_post = _post_bwd(dh_out, f, g_post, FFN_RESIDUAL_WEIGHT, name=f"{tag}_post_bwd")
    dgate, dup = _ffn_dhid(df, wd, g, u, name=f"{tag}_dhid")
    d_wd, d_wd16, _ = _mm_tn([hid], df, name=f"{tag}_dwd")
    d_wgt, d_wgt16, got_wd = _mm_tn([dgate], n, side=side_of(d_wd16), name=f"{tag}_dwg")
    d_wut, d_wut16, got_wg = _mm_tn([dup], n, side=side_of(d_wgt16), name=f"{tag}_dwu")
    dh_in, dg_pre, got_wu = _mm_nn_rmsbwd([(dgate, 0, 0), (dup, 1, 0)], [wgt, wut], h_in, g_pre, dh_out, side=side_of(d_wut16),
                                          name=f"{tag}_dn")
    received = (got_wg[0], got_wu[0], got_wd[0]) if chain else None
    return dh_in, dg_pre, dg_post, (d_wgt, d_wut, d_wd), (d_wgt16, d_wut16, d_wd16), received


CONV_ROWS = 128
HALO = 8


def _taps(w_ref):
    return [w_ref[k:k + 1, :] for k in range(SSM_CONV)]


def _conv_chunk(x_ref, xs, r0, taps, bias):
    xs[HALO + r0:HALO + r0 + CONV_ROWS, :] = x_ref[r0:r0 + CONV_ROWS, :].astype(F32)
    shifted = [xs[HALO + r0 - k:HALO + r0 - k + CONV_ROWS, :] for k in range(SSM_CONV)]
    pre = bias + shifted[0] * taps[SSM_CONV - 1]
    for k in range(1, SSM_CONV):
        pre = pre + shifted[k] * taps[SSM_CONV - 1 - k]
    return shifted, pre


def _fold_rows(a):
    return functools.reduce(jnp.add, [a[i:i + 8] for i in range(0, a.shape[0], 8)])


def _conv_fwd(xbc, conv_w, conv_b, bl, *, name):
    t, c = xbc.shape
    s = t // bl
    tc = LANES
    assert s % CONV_ROWS == 0

    def body(x_ref, w_ref, b_ref, o_ref, xs):
        taps, bias = _taps(w_ref), b_ref[...]
        xs[0:HALO, :] = jnp.zeros((HALO, tc), F32)
        for r0 in range(0, s, CONV_ROWS):
            _, pre = _conv_chunk(x_ref, xs, r0, taps, bias)
            o_ref[r0:r0 + CONV_ROWS, :] = (pre * _sigmoid(pre)).astype(o_ref.dtype)

    blk = pl.BlockSpec((s, tc), lambda b, j: (b, j))
    return pl.pallas_call(body, out_shape=jax.ShapeDtypeStruct((t, c), BF16), grid=(bl, c // tc),
                          in_specs=[blk, pl.BlockSpec((SSM_CONV, tc), lambda b, j: (0, j)), pl.BlockSpec((1, tc), lambda b, j: (0, j))],
                          out_specs=blk, scratch_shapes=[pltpu.VMEM((HALO + s, tc), F32)],
                          name=name, compiler_params=_params("parallel", "arbitrary"))(xbc, conv_w, conv_b)


def _conv_bwd(dxc, xbc, conv_w, conv_b, bl, *, name):
    t, c = xbc.shape
    s = t // bl
    tc = LANES

    def body(dy_ref, x_ref, w_ref, b_ref, dx_ref, dw_ref, db_ref, xs, dpre_s):
        @pl.when(pl.program_id(1) == 0)
        def _():
            dw_ref[...] = jnp.zeros_like(dw_ref)
            db_ref[...] = jnp.zeros_like(db_ref)

        taps, bias = _taps(w_ref), b_ref[...]
        zero8 = jnp.zeros((HALO, tc), F32)
        xs[0:HALO, :] = zero8
        dpre_s[s:s + HALO, :] = zero8
        sums = [zero8] * (SSM_CONV + 1)
        for r0 in range(0, s, CONV_ROWS):
            shifted, pre = _conv_chunk(x_ref, xs, r0, taps, bias)
            sg = _sigmoid(pre)
            dpre = dy_ref[r0:r0 + CONV_ROWS, :].astype(F32) * (sg * (1.0 + pre * (1.0 - sg)))
            dpre_s[r0:r0 + CONV_ROWS, :] = dpre
            sums = [acc + _fold_rows(dpre * sh) for acc, sh in zip(sums[:-1], shifted)] + [sums[-1] + _fold_rows(dpre)]
        for k in range(SSM_CONV):
            dw_ref[SSM_CONV - 1 - k:SSM_CONV - k, :] += jnp.sum(sums[k], axis=0, keepdims=True)
        db_ref[...] += jnp.sum(sums[-1], axis=0, keepdims=True)
        for r0 in range(0, s, CONV_ROWS):
            dx = dpre_s[r0:r0 + CONV_ROWS, :] * taps[SSM_CONV - 1]
            for k in range(1, SSM_CONV):
                dx = dx + dpre_s[r0 + k:r0 + k + CONV_ROWS, :] * taps[SSM_CONV - 1 - k]
            dx_ref[r0:r0 + CONV_ROWS, :] = dx.astype(dx_ref.dtype)

    blk = pl.BlockSpec((s, tc), lambda j, b: (b, j))
    return pl.pallas_call(
        body, out_shape=(jax.ShapeDtypeStruct((t, c), BF16), jax.ShapeDtypeStruct((8, c), F32), jax.ShapeDtypeStruct((1, c), F32)),
        grid=(c // tc, bl),
        in_specs=[blk, blk, pl.BlockSpec((SSM_CONV, tc), lambda j, b: (0, j)), pl.BlockSpec((1, tc), lambda j, b: (0, j))],
        out_specs=(blk, pl.BlockSpec((8, tc), lambda j, b: (0, j)), pl.BlockSpec((1, tc), lambda j, b: (0, j))),
        scratch_shapes=[pltpu.VMEM((HALO + s, tc), F32), pltpu.VMEM((s + HALO, tc), F32)],
        name=name, compiler_params=_params("parallel", "arbitrary"))(dxc, xbc, conv_w, conv_b)


def _softplus(x):
    return jnp.maximum(x, 0.0) + jnp.log1p(jnp.exp(-jnp.abs(x)))


def _hilo_dot(v, m_b, dims=NN):
    hi = v.astype(BF16)
    lo = (v - hi.astype(F32)).astype(BF16)
    return _dot(hi, m_b, dims) + _dot(lo, m_b, dims)


def _ssd_chunk_common(dtraw_ref, dtb_ref, alog_ref, dsk_ref, d_inner):
    q, p = CHUNK, SSM_HEAD_DIM
    shift = p.bit_length() - 1
    assert 1 << shift == p
    dt = _softplus(dtraw_ref[...] + dtb_ref[...])
    a = -jnp.exp(alog_ref[...])
    ii = lax.broadcasted_iota(jnp.int32, (q, q), 0)
    jj = lax.broadcasted_iota(jnp.int32, (q, q), 1)
    causal = ii >= jj
    tril = jnp.where(causal, 1.0, 0.0).astype(F32)
    triu = jnp.where(ii <= jj, 1.0, 0.0).astype(F32)
    a_cs = _dot_hi(tril, dt * a)
    a_cs_t = a_cs.T
    a_last = a_cs[q - 1:q, :]
    e_col = jnp.exp(a_cs)
    dec_end = jnp.exp(a_last - a_cs)
    head_of_col = lax.shift_right_logical(lax.broadcasted_iota(jnp.int32, (LANES, d_inner), 1), shift)
    spread = (lax.broadcasted_iota(jnp.int32, (LANES, d_inner), 0) == head_of_col).astype(BF16)
    exact = jnp.concatenate([dt, jnp.broadcast_to(dsk_ref[...], (8, LANES))], axis=0)
    hi = jnp.concatenate([exact, e_col, dec_end], axis=0).astype(BF16)
    lo = (exact - hi[:q + 8].astype(F32)).astype(BF16)
    wide = _dot(hi, spread, NN)
    fine = wide[:q + 8] + _dot(lo, spread, NN)
    return dict(dt=dt, a=a, a_cs=a_cs, a_cs_t=a_cs_t, a_last=a_last, dec_end=dec_end, causal=causal, triu=triu,
                dt_e=fine[:q], dsk_e=fine[q:q + 1], e_e=wide[q + 8:2 * q + 8], dec_e=wide[2 * q + 8:3 * q + 8])


def _fill_block_diag(bd_ref, src_ref, hpg, col0=0):
    q, p = CHUNK, SSM_HEAD_DIM
    for hh in range(hpg):
        bd_ref[hh * q:(hh + 1) * q, hh * p:(hh + 1) * p] = src_ref[:, col0 + hh * p:col0 + (hh + 1) * p]


def _lane_onehot(h):
    return (lax.broadcasted_iota(jnp.int32, (1, LANES), 1) == h).astype(F32)


def _ssd_fwd(xc, dt_raw, dt_bias, a_log, d_skip, bl, n_heads, *, side=None, name):
    t = xc.shape[0]
    q, p, nst, grp = CHUNK, SSM_HEAD_DIM, SSM_STATE, SSM_GROUPS
    d_inner = n_heads * p
    hpg = n_heads // grp
    hb = min(hpg, SSD_HEAD_BATCH)
    gw = hpg * p
    nc = t // bl // q
    assert d_inner % (grp * nst) == 0 and nst == LANES and hpg % hb == 0

    def body(xs_ref, b_ref, c_ref, dtraw_ref, dtb_ref, alog_ref, dsk_ref, y_ref, hprev_ref, state, m_all, x_bd, xdt_s):
        @pl.when(jnp.logical_and(pl.program_id(0) == 0, pl.program_id(1) == 0))
        def _():
            x_bd[...] = jnp.zeros_like(x_bd)

        @pl.when(pl.program_id(1) == 0)
        def _():
            state[...] = jnp.zeros_like(state)

        cm = _ssd_chunk_common(dtraw_ref, dtb_ref, alog_ref, dsk_ref, d_inner)
        for g in range(grp):
            cols = slice(g * gw, (g + 1) * gw)
            bg = b_ref[:, g * nst:(g + 1) * nst]
            cg = c_ref[:, g * nst:(g + 1) * nst]
            scores = _dot(cg, bg, NT)
            xs = xs_ref[:, cols].astype(F32)
            xdt = xs * cm['dt_e'][:, cols]
            xdt_s[...] = xdt.astype(BF16)
            y_parts = []
            for sub in range(hpg // hb):
                for k in range(hb):
                    h = g * hpg + sub * hb + k
                    seg = cm['a_cs'][:, h:h + 1] - cm['a_cs_t'][h:h + 1, :]
                    m_all[:, k * q:(k + 1) * q] = (scores * jnp.exp(jnp.where(cm['causal'], seg, NEG_INF))).astype(BF16)
                _fill_block_diag(x_bd, xdt_s, hb, sub * hb * p)
                y_parts.append(_dot(m_all[...], x_bd[...], NN))
            hprev = state[g]
            hprev_ref[g] = hprev
            y = jnp.concatenate(y_parts, axis=1) + cm['e_e'][:, cols] * _dot(cg, hprev.astype(BF16), NT)
            y_ref[:, cols] = y + cm['dsk_e'][:, cols] * xs
            st = _dot((xdt * cm['dec_e'][:, cols]).astype(BF16), bg, TN)
            for hh in range(hpg):
                h = g * hpg + hh
                rows = slice(hh * p, (hh + 1) * p)
                state[g, rows, :] = jnp.exp(cm['a_last'][:, h:h + 1]) * hprev[rows] + st[rows]

    gn = grp * nst

    def rowmap(b, c):
        return b * nc + c
    vec = pl.BlockSpec((1, LANES), lambda b, c: (0, 0))
    return _hosted_call(
        body,
        out_shape=[jax.ShapeDtypeStruct((t, d_inner), F32), jax.ShapeDtypeStruct((t // q, grp, gw, nst), F32)],
        grid=(bl, nc),
        in_specs=[pl.BlockSpec((q, d_inner), lambda b, c: (rowmap(b, c), 0)),
                  pl.BlockSpec((q, gn), lambda b, c: (rowmap(b, c), d_inner // gn)),
                  pl.BlockSpec((q, gn), lambda b, c: (rowmap(b, c), d_inner // gn + 1)),
                  pl.BlockSpec((q, LANES), lambda b, c: (rowmap(b, c), 0)), vec, vec, vec],
        out_specs=[pl.BlockSpec((q, d_inner), lambda b, c: (rowmap(b, c), 0)),
                   pl.BlockSpec((None, grp, gw, nst), lambda b, c: (rowmap(b, c), 0, 0, 0))],
        scratch_shapes=[pltpu.VMEM((grp, gw, nst), F32), pltpu.VMEM((q, hb * q), BF16), pltpu.VMEM((hb * q, hb * p), BF16),
                        pltpu.VMEM((q, gw), BF16)],
        operands=[xc, xc, xc, dt_raw, dt_bias, a_log, d_skip], side=side, name=name)


def _ssd_bwd(dy, y, xc, dt_raw, hprev_all, dt_bias, a_log, d_skip, bl, n_heads, *, side=None, name):
    t, c_dim = xc.shape
    q, p, nst, grp = CHUNK, SSM_HEAD_DIM, SSM_STATE, SSM_GROUPS
    d_inner = n_heads * p
    hpg = n_heads // grp
    hb = min(hpg, SSD_HEAD_BATCH)
    gw = hpg * p
    nc = t // bl // q
    gn = grp * nst
    shift = p.bit_length() - 1

    def body(dy_ref, y_ref, xs_ref, b_ref, c_ref, dtraw_ref, hprev_ref, dtb_ref, alog_ref, dsk_ref,
             dxc_ref, ddtraw_ref, ddtb_ref, dalog_ref, ddsk_ref, dstate, mt_all, x_bd, dy_bd, xdt_s):
        @pl.when(jnp.logical_and(pl.program_id(0) == 0, pl.program_id(1) == 0))
        def _():
            ddtb_ref[...] = jnp.zeros_like(ddtb_ref)
            dalog_ref[...] = jnp.zeros_like(dalog_ref)
            ddsk_ref[...] = jnp.zeros_like(ddsk_ref)
            x_bd[...] = jnp.zeros_like(x_bd)
            dy_bd[...] = jnp.zeros_like(dy_bd)

        @pl.when(pl.program_id(1) == 0)
        def _():
            dstate[...] = jnp.zeros_like(dstate)

        cm = _ssd_chunk_common(dtraw_ref, dtb_ref, alog_ref, dsk_ref, d_inner)
        causal = cm['causal']
        upper = cm['triu'] > 0.5
        seg_row = lax.shift_right_logical(lax.broadcasted_iota(jnp.int32, (gw, LANES), 0), shift)
        seg_lane = lax.broadcasted_iota(jnp.int32, (gw, LANES), 1)
        sums = jnp.zeros((5 * q, LANES), F32)
        state_dot = jnp.zeros((1, LANES), F32)
        for g in range(grp):
            cols = slice(g * gw, (g + 1) * gw)
            seg_sum = (seg_row + g * hpg == seg_lane).astype(BF16)
            bg = b_ref[:, g * nst:(g + 1) * nst]
            cg = c_ref[:, g * nst:(g + 1) * nst]
            scores_t = _dot(bg, cg, NT)
            xs = xs_ref[:, cols].astype(F32)
            xdt = xs * cm['dt_e'][:, cols]
            xdt_s[...] = xdt.astype(BF16)
            dyf = dy_ref[:, cols].astype(F32)
            dscores = jnp.zeros((q, q), F32)
            dx_parts = []
            for sub in range(hpg // hb):
                c0 = sub * hb * p
                _fill_block_diag(x_bd, xdt_s, hb, c0)
                _fill_block_diag(dy_bd, dy_ref, hb, g * gw + c0)
                dm_all = _dot(dy_ref[:, g * gw + c0:g * gw + c0 + hb * p], x_bd[...], NT)
                for k in range(hb):
                    h = g * hpg + sub * hb + k
                    blk = slice(k * q, (k + 1) * q)
                    seg = cm['a_cs'][:, h:h + 1] - cm['a_cs_t'][h:h + 1, :]
                    mt_all[:, blk] = (scores_t * jnp.exp(jnp.where(upper, -seg, NEG_INF))).astype(BF16)
                    dscores = dscores + dm_all[:, blk] * jnp.exp(jnp.where(causal, seg, NEG_INF))
                dx_parts.append(_dot(mt_all[...], dy_bd[...], NN))
            hprev = hprev_ref[g]
            hprev_b = hprev.astype(BF16)
            dhn = dstate[g]
            dhn_b = dhn.astype(BF16)
            e_e, dec_e = cm['e_e'][:, cols], cm['dec_e'][:, cols]
            y_scan = y_ref[:, cols] - cm['dsk_e'][:, cols] * xs
            dye_b = (dyf * e_e).astype(BF16)
            dcg = _dot(dye_b, hprev_b, NN)
            dhp = _dot(dye_b, cg, TN)
            bdh = _dot(bg, dhn_b, NT)
            dbg = _dot((xdt * dec_e).astype(BF16), dhn_b, NN)
            dx_diag = jnp.concatenate(dx_parts, axis=1)
            dx = dec_e * bdh + dx_diag
            ds_b = dscores.astype(BF16)
            dcg = dcg + _dot(ds_b, bg, NN)
            dbg = dbg + _dot(ds_b, cg, TN)
            x_rounded = xdt_s[...].astype(F32)
            sums = sums + _hilo_dot(jnp.concatenate([dyf * y_scan, xdt * bdh, x_rounded * dx_diag, dx * xs, dyf * xs], axis=0), seg_sum)
            state_dot = state_dot + jnp.sum(_hilo_dot(dhn * hprev, seg_sum, TN), axis=0, keepdims=True)
            dxc_ref[:, cols] = (dx * cm['dt_e'][:, cols] + cm['dsk_e'][:, cols] * dyf).astype(dxc_ref.dtype)
            dxc_ref[:, d_inner + g * nst:d_inner + (g + 1) * nst] = dbg.astype(dxc_ref.dtype)
            dxc_ref[:, d_inner + gn + g * nst:d_inner + gn + (g + 1) * nst] = dcg.astype(dxc_ref.dtype)
            for hh in range(hpg):
                h = g * hpg + hh
                rows = slice(hh * p, (hh + 1) * p)
                dstate[g, rows, :] = jnp.exp(cm['a_last'][:, h:h + 1]) * dhn[rows] + dhp[rows]
        s_y, s_end, s_diag, s_dt, s_skip = (sums[k * q:(k + 1) * q] for k in range(5))
        dt, a, dec_end = cm['dt'], cm['a'], cm['dec_end']
        last_row = (lax.broadcasted_iota(jnp.int32, (q, 1), 0) == q - 1).astype(F32)
        da_last = jnp.sum(dec_end * s_end, axis=0, keepdims=True) + jnp.exp(cm['a_last']) * state_dot
        da = s_y - dec_end * s_end - s_diag + last_row * da_last
        ddta = _dot_hi(cm['triu'], da)
        ddt = s_dt + ddta * a
        d_a = jnp.sum(ddta * dt, axis=0, keepdims=True)
        ddt_raw = ddt * _sigmoid(dtraw_ref[...] + dtb_ref[...])
        ddtraw_ref[...] = ddt_raw
        ddtb_ref[...] += jnp.sum(ddt_raw, axis=0, keepdims=True)
        dalog_ref[...] += d_a * a
        ddsk_ref[...] += jnp.sum(s_skip, axis=0, keepdims=True)

    def rowmap(b, c):
        return b * nc + (nc - 1 - c)
    vec = pl.BlockSpec((1, LANES), lambda b, c: (0, 0))
    vec_shape = jax.ShapeDtypeStruct((1, LANES), F32)
    return _hosted_call(
        body,
        out_shape=[jax.ShapeDtypeStruct((t, c_dim), BF16), jax.ShapeDtypeStruct((t, LANES), F32), vec_shape, vec_shape, vec_shape],
        grid=(bl, nc),
        in_specs=[pl.BlockSpec((q, d_inner), lambda b, c: (rowmap(b, c), 0)),
                  pl.BlockSpec((q, d_inner), lambda b, c: (rowmap(b, c), 0)),
                  pl.BlockSpec((q, d_inner), lambda b, c: (rowmap(b, c), 0)),
                  pl.BlockSpec((q, gn), lambda b, c: (rowmap(b, c), d_inner // gn)),
                  pl.BlockSpec((q, gn), lambda b, c: (rowmap(b, c), d_inner // gn + 1)),
                  pl.BlockSpec((q, LANES), lambda b, c: (rowmap(b, c), 0)),
                  pl.BlockSpec((None, grp, gw, nst), lambda b, c: (rowmap(b, c), 0, 0, 0)), vec, vec, vec],
        out_specs=[pl.BlockSpec((q, c_dim), lambda b, c: (rowmap(b, c), 0)),
                   pl.BlockSpec((q, LANES), lambda b, c: (rowmap(b, c), 0)), vec, vec, vec],
        scratch_shapes=[pltpu.VMEM((grp, gw, nst), F32), pltpu.VMEM((q, hb * q), BF16),
                        pltpu.VMEM((hb * q, hb * p), BF16), pltpu.VMEM((hb * q, hb * p), BF16), pltpu.VMEM((q, gw), BF16)],
        operands=[dy, y, xc, xc, xc, dt_raw, hprev_all, dt_bias, a_log, d_skip], side=side, name=name)


def _bucket_onehot():
    blk = CHUNK
    qi = jnp.arange(blk)[:, None]
    kj = jnp.arange(2 * blk)[None, :]
    dist = jnp.maximum(qi + blk - kj, 0)
    max_exact = REL_BUCKETS // 2
    d = jnp.maximum(dist, 1).astype(F32)
    large = max_exact + (jnp.log(d / max_exact) / math.log(REL_MAX_DISTANCE / max_exact) * (REL_BUCKETS - max_exact)).astype(jnp.int32)
    large = jnp.minimum(large, REL_BUCKETS - 1)
    bucket = jnp.where(dist < max_exact, dist, large).reshape(-1)
    return (bucket[None, :] == jnp.arange(REL_BUCKETS)[:, None]).astype(F32)


def _small_mm_hi(a, b, dims, *, name):
    def body(a_ref, b_ref, o_ref):
        o_ref[...] = _dot_hi(a_ref[...], b_ref[...], dims)
    n = b.shape[0] if dims == NT else b.shape[1]
    return pl.pallas_call(body, out_shape=jax.ShapeDtypeStruct((a.shape[0], n), F32), name=name)(a, b)


def _attn_band_mask_t(n, rep):
    blk = CHUNK
    jj = lax.broadcasted_iota(jnp.int32, (2 * blk, rep * blk), 0)
    ii = lax.broadcasted_iota(jnp.int32, (2 * blk, rep * blk), 1) & (blk - 1)
    dist = ii + blk - jj
    in_window = jnp.logical_and(dist >= 0, dist < blk)
    return jnp.logical_and(in_window, jnp.logical_or(jj >= blk, n > 0))


def _sink_row(sink_ref, heads):
    return jnp.concatenate([jnp.broadcast_to(sink_ref[:, h:h + 1], (1, CHUNK)) for h in heads], axis=1)


def _attn_fwd(q, kv, bias_t, sinks, bl, *, name):
    t, qd = q.shape
    blk, hd = CHUNK, ATTN_HEAD_DIM
    kvd = ATTN_KV_HEADS * hd
    rep = ATTN_Q_HEADS // ATTN_KV_HEADS
    nb = t // bl // blk
    scale = hd ** -0.5

    def body(q_ref, kp_ref, kc_ref, vp_ref, vc_ref, bias_ref, sink_ref, o_ref, lse_ref):
        n = pl.program_id(1)
        mask = _attn_band_mask_t(n, rep)
        for kvh in range(ATTN_KV_HEADS):
            ks = slice(kvh * hd, (kvh + 1) * hd)
            heads = range(kvh * rep, (kvh + 1) * rep)
            qs = jnp.concatenate([q_ref[:, h * hd:(h + 1) * hd] for h in heads], axis=0)
            kk = jnp.concatenate([kp_ref[:, ks], kc_ref[:, ks]], axis=0)
            vv = jnp.concatenate([vp_ref[:, ks], vc_ref[:, ks]], axis=0)
            s = jnp.where(mask, _dot(kk, qs, NT) * scale + bias_ref[kvh], NEG_INF)
            sink = _sink_row(sink_ref, heads)
            m = jnp.maximum(jnp.max(s, axis=0, keepdims=True), sink)
            p = jnp.exp(s - m)
            den = jnp.sum(p, axis=0, keepdims=True) + jnp.exp(sink - m)
            o = _dot((p * (1.0 / den)).astype(BF16), vv, TN)
            lse = m + jnp.log(den)
            for r, h in enumerate(heads):
                o_ref[:, h * hd:(h + 1) * hd] = o[r * blk:(r + 1) * blk].astype(o_ref.dtype)
                lse_ref[h:h + 1, :] = lse[:, r * blk:(r + 1) * blk]

    def cur(b, n):
        return b * nb + n

    def prev(b, n):
        return b * nb + jnp.maximum(n - 1, 0)
    return pl.pallas_call(
        body, out_shape=(jax.ShapeDtypeStruct((t, qd), BF16), jax.ShapeDtypeStruct((t // blk * ATTN_Q_HEADS, blk), F32)), grid=(bl, nb),
        in_specs=[pl.BlockSpec((blk, qd), lambda b, n: (cur(b, n), 0)),
                  pl.BlockSpec((blk, kvd), lambda b, n: (prev(b, n), 0)), pl.BlockSpec((blk, kvd), lambda b, n: (cur(b, n), 0)),
                  pl.BlockSpec((blk, kvd), lambda b, n: (prev(b, n), 1)), pl.BlockSpec((blk, kvd), lambda b, n: (cur(b, n), 1)),
                  _const_spec(bias_t.shape), _const_spec((1, LANES))],
        out_specs=(pl.BlockSpec((blk, qd), lambda b, n: (cur(b, n), 0)),
                   pl.BlockSpec((ATTN_Q_HEADS, blk), lambda b, n: (cur(b, n), 0))),
        name=name, compiler_params=_params("parallel", "arbitrary"))(q, kv, kv, kv, kv, bias_t, sinks)


def _attn_bwd(do, q, kv, lse, bias_t, sinks, bl, *, name):
    t, qd = q.shape
    blk, hd = CHUNK, ATTN_HEAD_DIM
    kvd = ATTN_KV_HEADS * hd
    rep = ATTN_Q_HEADS // ATTN_KV_HEADS
    s_len = t // bl
    nb = s_len // blk
    scale = hd ** -0.5

    def body(do_ref, q_ref, kp_ref, kc_ref, vp_ref, vc_ref, lse_ref, bias_ref, sink_ref, dq_ref, dkv_ref, dbias_ref, dsink_ref):
        n = pl.program_id(1)

        @pl.when(jnp.logical_and(pl.program_id(0) == 0, n == 0))
        def _():
            dbias_ref[...] = jnp.zeros_like(dbias_ref)
            dsink_ref[...] = jnp.zeros_like(dsink_ref)

        mask = _attn_band_mask_t(n, rep)
        r_cur = pl.multiple_of(n * blk, blk)
        r_prev = pl.multiple_of(jnp.maximum(n - 1, 0) * blk, blk)
        dsink = jnp.zeros((1, LANES), F32)
        for kvh in range(ATTN_KV_HEADS):
            ks = slice(kvh * hd, (kvh + 1) * hd)
            heads = range(kvh * rep, (kvh + 1) * rep)
            qs = jnp.concatenate([q_ref[:, h * hd:(h + 1) * hd] for h in heads], axis=0)
            dos = jnp.concatenate([do_ref[:, h * hd:(h + 1) * hd] for h in heads], axis=0)
            kk = jnp.concatenate([kp_ref[:, ks], kc_ref[:, ks]], axis=0)
            vv = jnp.concatenate([vp_ref[:, ks], vc_ref[:, ks]], axis=0)
            lse = jnp.concatenate([lse_ref[h:h + 1, :] for h in heads], axis=1)
            p = jnp.exp(jnp.where(mask, _dot(kk, qs, NT) * scale + bias_ref[kvh], NEG_INF) - lse)
            dp = _dot(vv, dos, NT)
            delta = jnp.sum(p * dp, axis=0, keepdims=True)
            ds = p * (dp - delta)
            dsink_row = jnp.exp(_sink_row(sink_ref, heads) - lse) * delta
            dbias_ref[kvh] += ds
            ds_b = ds.astype(BF16)
            dq_s = _dot(ds_b, kk, TN) * scale
            dkk = _dot(ds_b, qs, NN) * scale
            dvv = _dot(p.astype(BF16), dos, NN)
            for r, h in enumerate(heads):
                dq_ref[:, h * hd:(h + 1) * hd] = dq_s[r * blk:(r + 1) * blk].astype(dq_ref.dtype)
                dsink = dsink - jnp.sum(dsink_row[:, r * blk:(r + 1) * blk], axis=1, keepdims=True) * _lane_onehot(h)
            vs = slice(kvd + kvh * hd, kvd + (kvh + 1) * hd)
            dkv_ref[pl.ds(r_cur, blk), ks] = dkk[blk:]
            dkv_ref[pl.ds(r_cur, blk), vs] = dvv[blk:]

            @pl.when(n > 0)
            def _():
                dkv_ref[pl.ds(r_prev, blk), ks] += dkk[:blk]
                dkv_ref[pl.ds(r_prev, blk), vs] += dvv[:blk]
        dsink_ref[...] += dsink

    def cur(b, n):
        return b * nb + n

    def prev(b, n):
        return b * nb + jnp.maximum(n - 1, 0)
    qspec = pl.BlockSpec((blk, qd), lambda b, n: (cur(b, n), 0))
    return pl.pallas_call(
        body,
        out_shape=(jax.ShapeDtypeStruct((t, qd), BF16), jax.ShapeDtypeStruct((t, 2 * kvd), F32),
                   jax.ShapeDtypeStruct(bias_t.shape, F32), jax.ShapeDtypeStruct((1, LANES), F32)),
        grid=(bl, nb),
        in_specs=[qspec, qspec,
                  pl.BlockSpec((blk, kvd), lambda b, n: (prev(b, n), 0)), pl.BlockSpec((blk, kvd), lambda b, n: (cur(b, n), 0)),
                  pl.BlockSpec((blk, kvd), lambda b, n: (prev(b, n), 1)), pl.BlockSpec((blk, kvd), lambda b, n: (cur(b, n), 1)),
                  pl.BlockSpec((ATTN_Q_HEADS, blk), lambda b, n: (cur(b, n), 0)), _const_spec(bias_t.shape), _const_spec((1, LANES))],
        out_specs=(qspec, pl.BlockSpec((s_len, 2 * kvd), lambda b, n: (b, 0)), _const_spec(bias_t.shape), _const_spec((1, LANES))),
        name=name, compiler_params=_params("arbitrary", "arbitrary"))(do, q, kv, kv, kv, kv, lse, bias_t, sinks)


def _merge_fwd(y, z, ng, o, gs, ga, w_ssm, w_attn, w_out, h_in, g_post, *, name):
    t, d = h_in.shape
    tm = _tile(t, 256)
    d_ssm = y.shape[1]
    gw = d_ssm // SSM_GROUPS

    def body(y_ref, z_ref, ng_ref, o_ref, gs_ref, ga_ref, ws_ref, wa_ref, wo_ref, hin_ref, gp_ref,
             yn_ref, ys_ref, ya_ref, mg_ref, mix_ref, hout_ref):
        for g in range(SSM_GROUPS):
            sl = slice(g * gw, (g + 1) * gw)
            zv = z_ref[:, sl].astype(F32)
            yg = y_ref[:, sl] * (zv * _sigmoid(zv))
            r = lax.rsqrt(jnp.mean(yg * yg, axis=-1, keepdims=True) + RMS_EPS)
            yn_ref[:, sl] = (yg * r * ng_ref[:, sl]).astype(BF16)
        ys = _dot(yn_ref[...], ws_ref[...], NN)
        ya = _dot(o_ref[...], wa_ref[...], NN)
        merged = (_sigmoid(gs_ref[...].astype(F32)) * ys + _sigmoid(ga_ref[...].astype(F32)) * ya).astype(BF16)
        mix = _dot(merged, wo_ref[...], NN)
        ys_ref[...] = ys.astype(BF16)
        ya_ref[...] = ya.astype(BF16)
        mg_ref[...] = merged
        mix_ref[...] = mix
        hout_ref[...] = _rms_residual(mix, hin_ref[...], gp_ref[...], 1.0)

    def row(w):
        return pl.BlockSpec((tm, w), lambda i: (i, 0))
    bshape = jax.ShapeDtypeStruct((t, d), BF16)
    fshape = jax.ShapeDtypeStruct((t, d), F32)
    return pl.pallas_call(
        body, out_shape=(jax.ShapeDtypeStruct((t, d_ssm), BF16), bshape, bshape, bshape, fshape, fshape), grid=(t // tm,),
        in_specs=[row(d_ssm), row(d_ssm), _const_spec((1, d_ssm)), row(o.shape[1]), row(d), row(d), _resident_spec(w_ssm.shape),
                  _resident_spec(w_attn.shape), _resident_spec(w_out.shape), row(d), _const_spec((1, d))],
        out_specs=(row(d_ssm),) + (row(d),) * 5, name=name,
        compiler_params=_params("parallel"))(y, z, ng, o, gs, ga, w_ssm, w_attn, w_out, h_in, g_post)


def _merge_bwd(dh, mix, g_post, gs, ga, ys, ya, y, z, ng, w_ssm, w_attn, w_out, *, name):
    t, d = mix.shape
    tm = _tile(t, 256)
    d_ssm, d_attn = w_ssm.shape[0], w_attn.shape[0]
    gw = d_ssm // SSM_GROUPS

    def body(dh_ref, mix_ref, gp_ref, gs_ref, ga_ref, ys_ref, ya_ref, y_ref, z_ref, ng_ref, ws_ref, wa_ref, wo_ref,
             dmix_ref, dys_ref, dya_ref, dgs_ref, dga_ref, dy_ref, dz_ref, do_ref, dgp_ref, dng_ref):
        @pl.when(pl.program_id(0) == 0)
        def _():
            dgp_ref[...] = jnp.zeros_like(dgp_ref)
            dng_ref[...] = jnp.zeros_like(dng_ref)
        mv = mix_ref[...]
        dy = dh_ref[...]
        r = lax.rsqrt(jnp.mean(mv * mv, axis=-1, keepdims=True) + RMS_EPS)
        mhat = mv * r
        dyg = dy * gp_ref[...]
        dmix = (r * (dyg - mhat * jnp.mean(dyg * mhat, axis=-1, keepdims=True))).astype(BF16)
        dgp_ref[...] += jnp.sum(dy * mhat, axis=0, keepdims=True)
        dmix_ref[...] = dmix
        dmerged = _dot(dmix, wo_ref[...], NT)
        sgs = _sigmoid(gs_ref[...].astype(F32))
        sga = _sigmoid(ga_ref[...].astype(F32))
        dys = (dmerged * sgs).astype(BF16)
        dya = (dmerged * sga).astype(BF16)
        dys_ref[...] = dys
        dya_ref[...] = dya
        dgs_ref[...] = (dmerged * ys_ref[...].astype(F32) * sgs * (1.0 - sgs)).astype(BF16)
        dga_ref[...] = (dmerged * ya_ref[...].astype(F32) * sga * (1.0 - sga)).astype(BF16)
        do_ref[...] = _dot(dya, wa_ref[...], NT).astype(BF16)
        dyn = _dot(dys, ws_ref[...], NT)
        for g in range(SSM_GROUPS):
            sl = slice(g * gw, (g + 1) * gw)
            zv = z_ref[:, sl].astype(F32)
            yv = y_ref[:, sl]
            sg = _sigmoid(zv)
            sz = zv * sg
            yg = yv * sz
            rg = lax.rsqrt(jnp.mean(yg * yg, axis=-1, keepdims=True) + RMS_EPS)
            yhat = yg * rg
            dn = dyn[:, sl]
            dyg_n = dn * ng_ref[:, sl]
            dyg_g = rg * (dyg_n - yhat * jnp.mean(dyg_n * yhat, axis=-1, keepdims=True))
            dy_ref[:, sl] = (dyg_g * sz).astype(BF16)
            dz_ref[:, sl] = (dyg_g * yv * (sg * (1.0 + zv * (1.0 - sg)))).astype(BF16)
            dng_ref[:, sl] += jnp.sum(dn * yhat, axis=0, keepdims=True)

    def row(w):
        return pl.BlockSpec((tm, w), lambda i: (i, 0))

    def bshape(w):
        return jax.ShapeDtypeStruct((t, w), BF16)
    return pl.pallas_call(
        body, out_shape=(bshape(d),) * 5 + (bshape(d_ssm), bshape(d_ssm), bshape(d_attn), jax.ShapeDtypeStruct((1, d), F32),
                                            jax.ShapeDtypeStruct((1, d_ssm), F32)), grid=(t // tm,),
        in_specs=[row(d), row(d), _const_spec((1, d)), row(d), row(d), row(d), row(d), row(d_ssm), row(d_ssm), _const_spec((1, d_ssm)),
                  _resident_spec(w_ssm.shape), _resident_spec(w_attn.shape), _resident_spec(w_out.shape)],
        out_specs=(row(d),) * 5 + (row(d_ssm), row(d_ssm), row(d_attn), _const_spec((1, d)), _const_spec((1, d_ssm))),
        name=name, compiler_params=_params("arbitrary"))(dh, mix, g_post, gs, ga, ys, ya, y, z, ng, w_ssm, w_attn, w_out)


def _position():
    return lax.axis_index("x"), lax.axis_index("y"), lax.axis_index("c")


def _gather_exchange(shards):
    na = len(shards)

    def plan(ins, outs, sems):
        send_sems, recv_sems, local_sems = sems
        x, y, c = _position()
        me, sibling = (x, y, c), (x, y, 1 - c)
        chips = [(1 - x, y), (x, 1 - y), (1 - x, 1 - y)]

        def slot(a, pos):
            return outs[a].at[4 * pos[0] + 2 * pos[1] + pos[2]]

        def copy(a, k, block, to, src=None):
            return pltpu.make_async_remote_copy(
                src_ref=slot(a, block) if src is None else src, dst_ref=slot(a, block),
                send_sem=send_sems.at[a, k], recv_sem=recv_sems.at[a, k], device_id=to, device_id_type=MESH)

        mine = [pltpu.make_async_copy(ins[a], slot(a, me), local_sems.at[a]) for a in range(na)]
        first = []
        for a in range(na):
            first.append(copy(a, 0, me, sibling, src=ins[a]))
            first += [copy(a, 1 + j, me, (*chip, c), src=ins[a]) for j, chip in enumerate(chips)]
        return me, sibling, chips, copy, mine, first

    def start(ins, outs, sems):
        *_, mine, first = plan(ins, outs, sems)
        for cp in mine + first:
            cp.start()

    def finish(ins, outs, sems):
        me, sibling, chips, copy, mine, first = plan(ins, outs, sems)
        c = me[2]
        passed = []
        for a in range(na):
            for j, chip in enumerate(chips):
                copy(a, 1 + j, (*chip, c), me).wait_recv()
                fwd = copy(a, 4 + j, (*chip, c), sibling)
                fwd.start()
                passed.append(fwd)
        for a in range(na):
            copy(a, 0, sibling, me).wait_recv()
            for j, chip in enumerate(chips):
                copy(a, 4 + j, (*chip, 1 - c), me).wait_recv()
        for cp in first + passed:
            cp.wait_send()
        for cp in mine:
            cp.wait()

    return _Exchange(list(shards), [jax.ShapeDtypeStruct((N_DEV,) + s.shape, s.dtype) for s in shards],
                     [pltpu.SemaphoreType.DMA((na, 7)), pltpu.SemaphoreType.DMA((na, 7)), pltpu.SemaphoreType.DMA((na,))],
                     start, finish)


def _scatter_exchange(arrays):
    na = len(arrays)

    def copies(ins, outs, sems):
        send_sems, recv_sems = sems
        x, y, c = _position()
        out = []
        for a in range(na):
            for k in range(7):
                flip = k + 1
                peer = (x ^ (flip >> 2), y ^ ((flip >> 1) & 1), c ^ (flip & 1))
                peer_block = 4 * peer[0] + 2 * peer[1] + peer[2]
                out.append(pltpu.make_async_remote_copy(
                    src_ref=ins[a].at[peer_block], dst_ref=outs[a].at[k],
                    send_sem=send_sems.at[a, k], recv_sem=recv_sems.at[a, k], device_id=peer, device_id_type=MESH))
        return out

    def start(ins, outs, sems):
        for cp in copies(ins, outs, sems):
            cp.start()

    def finish(ins, outs, sems):
        for cp in copies(ins, outs, sems):
            cp.wait()

    return _Exchange(list(arrays), [jax.ShapeDtypeStruct((7,) + s.shape[1:], s.dtype) for s in arrays],
                     [pltpu.SemaphoreType.DMA((na, 7)), pltpu.SemaphoreType.DMA((na, 7))], start, finish)


def _reduce_adamw(own, recv, w, m, v, *, name):
    r, c = own.shape
    tm = _tile(r, 256)
    c1 = 1.0 - ADAM_B1 ** ADAM_STEP
    c2 = 1.0 - ADAM_B2 ** ADAM_STEP

    def body(own_ref, recv_ref, w_ref, m_ref, v_ref, g_ref, d_ref, mo_ref, vo_ref):
        gv = own_ref[...]
        for k in range(7):
            gv = gv + recv_ref[k].astype(F32)
        g_ref[...] = gv
        mn = ADAM_B1 * m_ref[...] + (1.0 - ADAM_B1) * gv
        vn = ADAM_B2 * v_ref[...] + (1.0 - ADAM_B2) * (gv * gv)
        mo_ref[...] = mn
        vo_ref[...] = vn
        d_ref[...] = -ADAM_LR * ((mn / c1) / (jnp.sqrt(vn / c2) + ADAM_EPS) + ADAM_WD * w_ref[...])

    blk = pl.BlockSpec((tm, c), lambda i: (i, 0))
    shp = jax.ShapeDtypeStruct((r, c), F32)
    return pl.pallas_call(body, out_shape=(shp,) * 4, grid=(r // tm,),
                          in_specs=[blk, pl.BlockSpec((7, tm, c), lambda i: (0, i, 0)), blk, blk, blk], out_specs=(blk,) * 4,
                          name=name, compiler_params=_params("parallel"))(own, recv, w, m, v)


SMALL_ROW = 8 * LANES


def _update_replicated(partials, ws, ms, vs, *, name):
    n = len(ws)
    pieces, n_rows = [], 0
    for k in sorted(range(n), key=lambda k_: -ws[k_].shape[0]):
        r, c = ws[k].shape
        assert (r == 1 or c <= SMALL_ROW) and (r == 1 or n_rows % 8 == 0)
        for c0 in range(0, c, SMALL_ROW):
            pieces.append((k, slice(0, r), slice(c0, min(c0 + SMALL_ROW, c)), n_rows))
            n_rows += r
    n_rows = -(-n_rows // 8) * 8
    c1 = 1.0 - ADAM_B1 ** ADAM_STEP
    c2 = 1.0 - ADAM_B2 ** ADAM_STEP

    def reduce_body(*refs):
        g_in, total, buf, send_sems, recv_sems = refs[:n], refs[n], refs[n + 1], refs[n + 2], refs[n + 3]
        x, y, c_ = _position()
        me = 4 * x + 2 * y + c_
        buf[me] = jnp.zeros((n_rows, SMALL_ROW), F32)
        for k, rs, cs, row0 in pieces:
            buf[me, row0:row0 + rs.stop, 0:cs.stop - cs.start] = g_in[k][rs, cs]
        copies = []
        for j in range(7):
            flip = j + 1
            peer = (x ^ (flip >> 2), y ^ ((flip >> 1) & 1), c_ ^ (flip & 1))
            cp = pltpu.make_async_remote_copy(
                src_ref=buf.at[me], dst_ref=buf.at[me], send_sem=send_sems.at[j], recv_sem=recv_sems.at[j],
                device_id=peer, device_id_type=MESH)
            cp.start()
            copies.append(cp)
        for cp in copies:
            cp.wait()
        acc = buf[0]
        for dev in range(1, N_DEV):
            acc = acc + buf[dev]
        total[...] = acc

    def step_body(*refs):
        total = refs[0]
        w_in_, m_in, v_in = (refs[1 + j * n:1 + (j + 1) * n] for j in range(3))
        g_out, d_out, m_out, v_out = (refs[1 + (3 + j) * n:1 + (4 + j) * n] for j in range(4))
        for k, rs, cs, row0 in pieces:
            gv = total[row0:row0 + rs.stop, 0:cs.stop - cs.start]
            mn = ADAM_B1 * m_in[k][rs, cs] + (1.0 - ADAM_B1) * gv
            vn = ADAM_B2 * v_in[k][rs, cs] + (1.0 - ADAM_B2) * (gv * gv)
            g_out[k][rs, cs] = gv
            m_out[k][rs, cs] = mn
            v_out[k][rs, cs] = vn
            d_out[k][rs, cs] = -ADAM_LR * ((mn / c1) / (jnp.sqrt(vn / c2) + ADAM_EPS) + ADAM_WD * w_in_[k][rs, cs])

    vm = pl.BlockSpec(memory_space=pltpu.VMEM)
    total = pl.pallas_call(
        reduce_body, out_shape=jax.ShapeDtypeStruct((n_rows, SMALL_ROW), F32), in_specs=[vm] * n, out_specs=vm,
        scratch_shapes=[pltpu.VMEM((N_DEV, n_rows, SMALL_ROW), F32), pltpu.SemaphoreType.DMA((7,)), pltpu.SemaphoreType.DMA((7,))],
        name=f"{name}_allreduce")(*partials)
    shapes = tuple(jax.ShapeDtypeStruct(w.shape, F32) for w in ws)
    outs = pl.pallas_call(step_body, out_shape=shapes * 4, in_specs=[vm] * (1 + 3 * n), out_specs=tuple([vm] * (4 * n)),
                          name=f"{name}_adamw")(total, *ws, *ms, *vs)
    return tuple(outs[j * n:(j + 1) * n] for j in range(4))


def _pad_lanes(v, width=LANES):
    return jnp.pad(v, ((0, 0), (0, width - v.shape[1])))


def kernel(x, ffn1_pre_g, ffn1_w_gate, ffn1_w_up, ffn1_w_down, ffn1_post_g, mix_pre_g, w_in, conv_w, conv_b, dt_bias, a_log, d_skip, ssm_norm_g, w_ssm_proj, attn_sinks, rel_bias_table, w_attn_proj, w_out, mix_post_g, ffn2_pre_g, ffn2_w_gate, ffn2_w_up, ffn2_w_down, ffn2_post_g, loss_target, m_ffn1_pre_g, m_ffn1_w_gate, m_ffn1_w_up, m_ffn1_w_down, m_ffn1_post_g, m_mix_pre_g, m_w_in, m_conv_w, m_conv_b, m_dt_bias, m_a_log, m_d_skip, m_ssm_norm_g, m_w_ssm_proj, m_attn_sinks, m_rel_bias_table, m_w_attn_proj, m_w_out, m_mix_post_g, m_ffn2_pre_g, m_ffn2_w_gate, m_ffn2_w_up, m_ffn2_w_down, m_ffn2_post_g, v_ffn1_pre_g, v_ffn1_w_gate, v_ffn1_w_up, v_ffn1_w_down, v_ffn1_post_g, v_mix_pre_g, v_w_in, v_conv_w, v_conv_b, v_dt_bias, v_a_log, v_d_skip, v_ssm_norm_g, v_w_ssm_proj, v_attn_sinks, v_rel_bias_table, v_w_attn_proj, v_w_out, v_mix_post_g, v_ffn2_pre_g, v_ffn2_w_gate, v_ffn2_w_up, v_ffn2_w_down, v_ffn2_post_g):
    args = dict(locals())
    weight_names = ['ffn1_pre_g', 'ffn1_w_gate', 'ffn1_w_up', 'ffn1_w_down', 'ffn1_post_g', 'mix_pre_g', 'w_in', 'conv_w', 'conv_b',
                    'dt_bias', 'a_log', 'd_skip', 'ssm_norm_g', 'w_ssm_proj', 'attn_sinks', 'rel_bias_table', 'w_attn_proj', 'w_out',
                    'mix_post_g', 'ffn2_pre_g', 'ffn2_w_gate', 'ffn2_w_up', 'ffn2_w_down', 'ffn2_post_g']
    col_sharded = ('ffn1_w_gate', 'ffn1_w_up', 'w_in', 'ffn2_w_gate', 'ffn2_w_up')
    row_sharded = ('ffn1_w_down', 'w_ssm_proj', 'w_attn_proj', 'w_out', 'ffn2_w_down')
    big = col_sharded + row_sharded

    bl, s_len, d = x.shape
    t = bl * s_len
    d_inner = ssm_norm_g.shape[1]
    n_heads = dt_bias.shape[1]
    gn = SSM_GROUPS * SSM_STATE
    conv_dim = d_inner + 2 * gn
    q_dim = ATTN_Q_HEADS * ATTN_HEAD_DIM
    kv_dim = ATTN_KV_HEADS * ATTN_HEAD_DIM

    def local_2d(name, a):
        a = a[0]
        return a.T if name in col_sharded else a

    ffn1_names = ('ffn1_w_gate', 'ffn1_w_up', 'ffn1_w_down')
    ffn2_names = ('ffn2_w_gate', 'ffn2_w_up', 'ffn2_w_down')
    mixer_names = ('w_ssm_proj', 'w_attn_proj', 'w_out')

    def shard(n):
        return local_2d(n, args[n]).astype(BF16)

    def rows(g):
        return g.reshape(N_DEV * g.shape[1], g.shape[2])

    x2 = x.reshape(t, d)
    tgt2 = loss_target.reshape(t, d)
    full = {}

    (h1,), saved1, ffn1_w, got_in = _ffn_forward(
        x2, ffn1_pre_g, lambda got: (rows(got[0]), rows(got[1])), lambda got: rows(got[2]), ffn1_post_g, "ffn1",
        side_norm=_gather_exchange([shard(n) for n in ffn1_names[:2]]),
        side_up=_gather_exchange([shard('w_in'), conv_w[0], shard('ffn1_w_down')]))
    full.update(zip(ffn1_names, ffn1_w))
    conv_w_full = jnp.transpose(got_in[1], (1, 0, 2)).reshape(SSM_CONV, conv_dim)

    win_t = rows(got_in[0])
    dt_lo = 2 * d + d_inner + conv_dim
    off = {'gs': 0, 'ga': d, 'z': 2 * d, 'xbc': 2 * d + d_inner, 'dt': dt_lo, 'q': dt_lo + n_heads, 'kv': dt_lo + n_heads + q_dim}
    assert all(o_ % 16 == 0 for o_ in off.values())

    u, (gs, ga, z, xbc, q, kv, dt_raw) = _proj_all(
        h1, mix_pre_g, win_t,
        [(off['gs'], d, BF16), (off['ga'], d, BF16), (off['z'], d_inner, BF16), (off['xbc'], conv_dim, BF16),
         (off['q'], q_dim, BF16), (off['kv'], 2 * kv_dim, BF16), (off['dt'], n_heads, F32)], name="mix_proj")

    dtb_p, alog_p, dsk_p, sinks_p = _pad_lanes(dt_bias), _pad_lanes(a_log), _pad_lanes(d_skip), _pad_lanes(attn_sinks)
    xc = _conv_fwd(xbc, conv_w_full, conv_b, bl, name="conv_fwd")
    late_names = mixer_names + ffn2_names
    (y, hprev), got_late = _ssd_fwd(xc, dt_raw, dtb_p, alog_p, dsk_p, bl, n_heads,
                                    side=_gather_exchange([shard(n) for n in late_names]), name="ssd_fwd")
    full.update({n: rows(g) for n, g in zip(late_names, got_late)})

    onehot = _bucket_onehot()
    rep = ATTN_Q_HEADS // ATTN_KV_HEADS
    bias = _small_mm_hi(rel_bias_table.T, onehot, NN, name="rel_bias")
    bias_t = jnp.transpose(bias.reshape(ATTN_KV_HEADS, rep, CHUNK, 2 * CHUNK), (0, 3, 1, 2)).reshape(ATTN_KV_HEADS, 2 * CHUNK, rep * CHUNK)
    o, lse = _attn_fwd(q, kv, bias_t, sinks_p, bl, name="attn_fwd")

    yn, ys, ya, merged, mix, h2 = _merge_fwd(y, z, ssm_norm_g, o, gs, ga, full['w_ssm_proj'], full['w_attn_proj'], full['w_out'],
                                             h1, mix_post_g, name="merge_fwd")

    (dh3, loss_vec), saved2, _, _ = _ffn_forward(h2, ffn2_pre_g, (full['ffn2_w_gate'], full['ffn2_w_up']), full['ffn2_w_down'],
                                                 ffn2_post_g, "ffn2", target=tgt2)
    loss = lax.psum(loss_vec[0, 0], ("x", "y", "c"))

    grads, own, wire, received = {}, {}, {}, {}
    dh2, grads['ffn2_pre_g'], grads['ffn2_post_g'], g32, g16, _ = _ffn_backward(
        dh3, saved2, ffn2_pre_g, full['ffn2_w_gate'], full['ffn2_w_up'], full['ffn2_w_down'], ffn2_post_g, "ffn2")
    own.update(zip(ffn2_names, map(_stack8, g32)))
    wire.update(zip(ffn2_names, map(_stack8, g16)))

    dmix, dys, dya, dgs, dga, dy, dz, do, grads['mix_post_g'], grads['ssm_norm_g'] = _merge_bwd(
        dh2, mix, mix_post_g, gs, ga, ys, ya, y, z, ssm_norm_g, full['w_ssm_proj'], full['w_attn_proj'], full['w_out'],
        name="merge_bwd")
    for n, (lhs, rhs) in zip(mixer_names, ((yn, dys), (o, dya), (merged, dmix))):
        g32_, g16_, _ = _mm_tn([lhs], rhs, name=f"d{n}")
        own[n], wire[n] = _stack8(g32_), _stack8(g16_)

    dq, dkv, dbias_t, dsinks = _attn_bwd(do, q, kv, lse, bias_t, sinks_p, bl, name="attn_bwd")
    dbias = jnp.transpose(dbias_t.reshape(ATTN_KV_HEADS, 2 * CHUNK, rep, CHUNK), (0, 2, 3, 1)).reshape(ATTN_Q_HEADS, -1)
    d_table = _small_mm_hi(onehot, dbias, NT, name="rel_bias_bwd")


    first_group = ffn2_names + mixer_names
    (dxc, ddt_raw, ddtb, dalog, ddsk), got = _ssd_bwd(dy, y, xc, dt_raw, hprev, dtb_p, alog_p, dsk_p, bl, n_heads,
                                                      side=_scatter_exchange([wire[n] for n in first_group]), name="ssd_bwd")
    received.update(zip(first_group, got))
    dxbc, dconv_w8, grads['conv_b'] = _conv_bwd(dxc, xbc, conv_w_full, conv_b, bl, name="conv_bwd")

    wide32, wide16, _ = _mm_tn([dgs, dga, dz, dxbc, dq], u, name="dw_in")
    kv32, kv16, _ = _mm_tn([dkv], u, name="dw_in_kv")
    dt32, dt16, _ = _mm_tn([ddt_raw], u, name="dw_in_dt")

    def original_order(wide, kv_part, dt_part):
        return jnp.concatenate([wide[:dt_lo], dt_part[:n_heads], wide[dt_lo:], kv_part], axis=0)
    me = 4 * lax.axis_index("x") + 2 * lax.axis_index("y") + lax.axis_index("c")
    blk_rows = win_t.shape[0] // N_DEV
    wire['w_in'] = _stack8(original_order(wide16, kv16, dt16))
    own_w_in = lax.dynamic_slice_in_dim(original_order(wide32, kv32, dt32), me * blk_rows, blk_rows)
    own['conv_w'] = jnp.transpose(dconv_w8[:SSM_CONV].reshape(SSM_CONV, N_DEV, conv_dim // N_DEV), (1, 0, 2))

    segs = [(g_, 0, off[k_]) for g_, k_ in zip([dgs, dga, dz, dxbc, dq, dkv, ddt_raw], ('gs', 'ga', 'z', 'xbc', 'q', 'kv', 'dt'))]
    dh1, grads['mix_pre_g'], got = _mm_nn_rmsbwd(segs, [win_t], h1, mix_pre_g, dh2,
                                                 side=_scatter_exchange([wire['w_in'], own['conv_w']]), name="mix_du")
    received.update(zip(('w_in', 'conv_w'), got))

    dx2, grads['ffn1_pre_g'], grads['ffn1_post_g'], g32, _, got = _ffn_backward(
        dh1, saved1, ffn1_pre_g, full['ffn1_w_gate'], full['ffn1_w_up'], full['ffn1_w_down'], ffn1_post_g, "ffn1", chain=True)
    own.update(zip(ffn1_names, map(_stack8, g32)))
    received.update(zip(ffn1_names, got))

    def own_block(a):
        return lax.dynamic_index_in_dim(a, me, 0, keepdims=False)
    out_g, out_d, out_m, out_v = {}, {}, {}, {}
    for n in big:
        w2, m2, v2 = local_2d(n, args[n]), local_2d(n, args['m_' + n]), local_2d(n, args['v_' + n])
        results = _reduce_adamw(own_w_in if n == 'w_in' else own_block(own[n]), received[n], w2, m2, v2, name=f"update_{n}")
        out_g[n], out_d[n], out_m[n], out_v[n] = ((a.T if n in col_sharded else a)[None] for a in results)
    results = _reduce_adamw(own_block(own['conv_w']), received['conv_w'], conv_w[0], m_conv_w[0], v_conv_w[0], name="update_conv_w")
    out_g['conv_w'], out_d['conv_w'], out_m['conv_w'], out_v['conv_w'] = (a[None] for a in results)

    grads['dt_bias'], grads['a_log'], grads['d_skip'], grads['attn_sinks'], grads['rel_bias_table'] = ddtb, dalog, ddsk, dsinks, d_table
    small = [n for n in weight_names if n not in big and n != 'conv_w']
    results = _update_replicated([grads[n] for n in small], [args[n] for n in small], [args['m_' + n] for n in small],
                                 [args['v_' + n] for n in small], name="update_replicated")
    for dst, vals in zip((out_g, out_d, out_m, out_v), results):
        dst.update(zip(small, vals))

    grad_x = dx2.reshape(bl, s_len, d)
    return (loss, grad_x, *[out_g[n] for n in weight_names], *[out_d[n] for n in weight_names],
            *[out_m[n] for n in weight_names], *[out_v[n] for n in weight_names])
```

```python
import functools
import math

import numpy as np
import jax
import jax.numpy as jnp
from jax import lax
from jax.experimental import pallas as pl
from jax.experimental.pallas import tpu as pltpu

F32 = jnp.float32
BF16 = jnp.bfloat16
MESH = pl.DeviceIdType.MESH
N_DEV = 8

SSM_HEAD_DIM = 64
SSM_GROUPS = 4
SSM_STATE = 128
SSM_CONV = 4
CHUNK = 128
ATTN_HEAD_DIM = 64
ATTN_Q_HEADS = 16
ATTN_KV_HEADS = 4
REL_BUCKETS = 32
REL_MAX_DISTANCE = 128
RMS_EPS = 1e-6
FFN_RESIDUAL_WEIGHT = 0.5
ADAM_LR, ADAM_B1, ADAM_B2, ADAM_EPS, ADAM_WD, ADAM_STEP = 0.001, 0.9, 0.999, 1e-08, 0.01, 10

LANES = 128
VMEM_LIMIT_BYTES = 56 * 1024 * 1024
FFN_COL_TILE = 1408
SSD_HEAD_BATCH = 2

NEG_INF = float("-inf")


def _params(*sem):
    return pltpu.CompilerParams(dimension_semantics=sem, vmem_limit_bytes=VMEM_LIMIT_BYTES)


def _tile(n, pref, mult=8):
    if n <= pref:
        return n
    t = (pref // mult) * mult
    while t >= mult:
        if n % t == 0:
            return t
        t -= mult
    return n


def _sigmoid(x):
    return 1.0 / (1.0 + jnp.exp(-x))


def _dot(a, b, dims):
    return lax.dot_general(a, b, (dims, ((), ())), preferred_element_type=F32)


NN = ((1,), (0,))
NT = ((1,), (1,))
TN = ((0,), (0,))


def _dot_hi(a, b, dims=NN):
    return lax.dot_general(a, b, (dims, ((), ())), preferred_element_type=F32, precision=lax.Precision.HIGHEST)


def _const_spec(shape):
    nd = len(shape)
    return pl.BlockSpec(shape, lambda *_: (0,) * nd)


def _resident_spec(shape):
    nd = len(shape)
    return pl.BlockSpec(shape, lambda *_: (0,) * nd, pipeline_mode=pl.Buffered(1))


class _Exchange:
    def __init__(self, arrays, out_shape, scratch, start, finish):
        self.arrays, self.out_shape, self.scratch, self.start, self.finish = arrays, out_shape, scratch, start, finish


def _hosted_call(body, *, grid, in_specs, out_specs, out_shape, scratch_shapes, operands, side, name):
    in_specs, out_specs, out_shape, scratch_shapes = list(in_specs), list(out_specs), list(out_shape), list(scratch_shapes)
    sem = ("arbitrary",) * len(grid)
    if side is None:
        outs = pl.pallas_call(body, out_shape=tuple(out_shape), grid=grid, in_specs=in_specs, out_specs=tuple(out_specs),
                              scratch_shapes=scratch_shapes, name=name, compiler_params=_params(*sem))(*operands)
        return tuple(outs), ()
    n_in, n_out, n_scr = len(in_specs), len(out_shape), len(scratch_shapes)
    s_in, s_out = len(side.arrays), len(side.out_shape)

    def wrapped(*refs):
        refs = list(refs)
        main_in, side_in = refs[:n_in], refs[n_in:n_in + s_in]
        o0 = n_in + s_in
        main_out, side_out = refs[o0:o0 + n_out], refs[o0 + n_out:o0 + n_out + s_out]
        c0 = o0 + n_out + s_out
        main_scr, side_scr = refs[c0:c0 + n_scr], refs[c0 + n_scr:]
        ids = [pl.program_id(ax) for ax in range(len(grid))]
        first = functools.reduce(jnp.logical_and, [i == 0 for i in ids])
        last = functools.reduce(jnp.logical_and, [i == g - 1 for i, g in zip(ids, grid)])

        @pl.when(first)
        def _():
            side.start(side_in, side_out, side_scr)

        body(*main_in, *main_out, *main_scr)

        @pl.when(last)
        def _():
            side.finish(side_in, side_out, side_scr)

    hbm = pl.BlockSpec(memory_space=pl.ANY)
    outs = pl.pallas_call(
        wrapped, out_shape=tuple(out_shape + list(side.out_shape)), grid=grid,
        in_specs=in_specs + [hbm] * s_in, out_specs=tuple(out_specs + [hbm] * s_out),
        scratch_shapes=scratch_shapes + list(side.scratch), name=name, compiler_params=_params(*sem))(*operands, *side.arrays)
    return tuple(outs[:n_out]), tuple(outs[n_out:])


def _proj_all(h, g, w, segs, *, name):
    t, d = h.shape
    tm = _tile(t, 512)
    segs = [(row0, wd_, max(wd_, LANES), dt_) for row0, wd_, dt_ in segs]
    assert all(row0 + out_w <= w.shape[0] for row0, _, out_w, _ in segs)

    def body(h_ref, g_ref, w_ref, u_ref, *o_refs):
        hv = h_ref[...]
        r = lax.rsqrt(jnp.mean(hv * hv, axis=-1, keepdims=True) + RMS_EPS)
        uv = (hv * r * g_ref[...]).astype(BF16)
        u_ref[...] = uv
        for (row0, width, out_w, _), o_ref in zip(segs, o_refs):
            for c0, c1 in _col_chunks(out_w, 8 * LANES):
                part = _dot(uv, w_ref[row0 + c0:row0 + c1, :], NT)
                if width < out_w:
                    part = jnp.where(lax.broadcasted_iota(jnp.int32, part.shape, 1) < width, part, 0.0)
                o_ref[:, c0:c1] = part.astype(o_ref.dtype)

    row = pl.BlockSpec((tm, d), lambda i: (i, 0))
    outs = pl.pallas_call(
        body, out_shape=(jax.ShapeDtypeStruct((t, d), BF16),) + tuple(jax.ShapeDtypeStruct((t, ow), dt_) for _, _, ow, dt_ in segs),
        grid=(t // tm,), in_specs=[row, _const_spec((1, d)), _resident_spec(w.shape)],
        out_specs=(row,) + tuple(pl.BlockSpec((tm, ow), lambda i: (i, 0)) for _, _, ow, _ in segs),
        name=name, compiler_params=_params("parallel"))(h, g, w)
    return outs[0], outs[1:]


def _mm_tn(a_list, b, *, tm=1408, tk=2048, side=None, name):
    t, n = b.shape
    tk = _tile(t, tk if len(a_list) == 1 else tk // 2)
    nk = t // tk
    widths = [a.shape[1] for a in a_list]
    tm = _tile(math.gcd(*widths), tm, LANES)
    assert all(w % tm == 0 for w in widths)
    starts = np.cumsum([0] + [w // tm for w in widths])
    nseg = len(a_list)

    def a_spec(s):
        lo, hi = int(starts[s]), int(starts[s + 1])

        def idx(i, k):
            active = jnp.logical_and(i >= lo, i < hi)
            return (jnp.where(active, k, 0), jnp.clip(i - lo, 0, hi - lo - 1))
        return pl.BlockSpec((tk, tm), idx)

    def body(*refs):
        a_refs, b_ref, o_ref, o16_ref, acc = refs[:nseg], refs[nseg], refs[nseg + 1], refs[nseg + 2], refs[nseg + 3]
        i, k = pl.program_id(0), pl.program_id(1)

        @pl.when(k == 0)
        def _():
            acc[...] = jnp.zeros_like(acc)

        bv = b_ref[...].astype(BF16)
        for s in range(nseg):
            lo, hi = int(starts[s]), int(starts[s + 1])

            @pl.when(jnp.logical_and(i >= lo, i < hi))
            def _(s=s):
                acc[...] += _dot(a_refs[s][...].astype(BF16), bv, TN)

        @pl.when(k == nk - 1)
        def _():
            o_ref[...] = acc[...]
            o16_ref[...] = acc[...].astype(BF16)

    rows = int(starts[-1]) * tm
    o_spec = pl.BlockSpec((tm, n), lambda i, k: (i, 0))
    (o32, o16), got = _hosted_call(
        body, out_shape=[jax.ShapeDtypeStruct((rows, n), F32), jax.ShapeDtypeStruct((rows, n), BF16)], grid=(int(starts[-1]), nk),
        in_specs=[a_spec(s) for s in range(nseg)] + [pl.BlockSpec((tk, n), lambda i, k: (k, 0))],
        out_specs=[o_spec, o_spec], scratch_shapes=[pltpu.VMEM((tm, n), F32)], operands=list(a_list) + [b], side=side, name=name)
    return o32, o16, got


def _mm_nn_rmsbwd(segs, weights, x, g, dres, *, tm=512, side=None, name):
    t, d = x.shape
    tm = _tile(t, tm)
    nseg, nw = len(segs), len(weights)

    def body(*refs):
        a_refs, w_refs = refs[:nseg], refs[nseg:nseg + nw]
        x_ref, g_ref, dres_ref, dx_ref, dg_ref = refs[nseg + nw:]

        @pl.when(pl.program_id(0) == 0)
        def _():
            dg_ref[...] = jnp.zeros_like(dg_ref)

        dn = None
        for s, (a, w_idx, row0) in enumerate(segs):
            part = _dot(a_refs[s][...].astype(BF16), w_refs[w_idx][row0:row0 + a.shape[1], :], NN)
            dn = part if dn is None else dn + part
        xv = x_ref[...]
        r = lax.rsqrt(jnp.mean(xv * xv, axis=-1, keepdims=True) + RMS_EPS)
        xhat = xv * r
        dyg = dn * g_ref[...]
        dx_ref[...] = dres_ref[...] + r * (dyg - xhat * jnp.mean(dyg * xhat, axis=-1, keepdims=True))
        dg_ref[...] += jnp.sum(dn * xhat, axis=0, keepdims=True)

    row = pl.BlockSpec((tm, d), lambda i: (i, 0))
    in_specs = [pl.BlockSpec((tm, a.shape[1]), lambda i: (i, 0)) for a, _, _ in segs]
    in_specs += [_resident_spec(w.shape) for w in weights] + [row, _const_spec((1, d)), row]
    (dx, dg), extra = _hosted_call(
        body, grid=(t // tm,), in_specs=in_specs, out_specs=[row, _const_spec((1, d))],
        out_shape=[jax.ShapeDtypeStruct((t, d), F32), jax.ShapeDtypeStruct((1, d), F32)], scratch_shapes=[],
        operands=[a for a, _, _ in segs] + list(weights) + [x, g, dres], side=side, name=name)
    return dx, dg, extra


def _col_chunks(width, chunk=4 * LANES):
    return [(c0, min(c0 + chunk, width)) for c0 in range(0, width, chunk)]


def _rms_fwd(x, g, *, side=None, name):
    t, d = x.shape
    tm = _tile(t, 512)

    def body(x_ref, g_ref, o_ref):
        xv = x_ref[...]
        r = lax.rsqrt(jnp.mean(xv * xv, axis=-1, keepdims=True) + RMS_EPS)
        o_ref[...] = (xv * r * g_ref[...]).astype(o_ref.dtype)

    row = pl.BlockSpec((tm, d), lambda i: (i, 0))
    (n,), got = _hosted_call(body, grid=(t // tm,), in_specs=[row, _const_spec((1, d))], out_specs=[row],
                             out_shape=[jax.ShapeDtypeStruct((t, d), BF16)], scratch_shapes=[], operands=[x, g], side=side, name=name)
    return n, got


def _ffn_up(n, wgt, wut, *, side=None, name):
    t, d = n.shape
    f = wgt.shape[0]
    tm, tn = _tile(t, 512), _tile(f, FFN_COL_TILE, LANES)

    def body(n_ref, wg_ref, wu_ref, g_ref, u_ref, h_ref):
        nv = n_ref[...]
        gv = _dot(nv, wg_ref[...], NT)
        uv = _dot(nv, wu_ref[...], NT)
        g_ref[...] = gv.astype(BF16)
        u_ref[...] = uv.astype(BF16)
        h_ref[...] = (gv * _sigmoid(gv) * uv).astype(BF16)

    w_spec = pl.BlockSpec((tn, d), lambda j, i: (j, 0))
    o_spec = pl.BlockSpec((tm, tn), lambda j, i: (i, j))
    shp = jax.ShapeDtypeStruct((t, f), BF16)
    return _hosted_call(body, grid=(f // tn, t // tm), in_specs=[pl.BlockSpec((tm, d), lambda j, i: (i, 0)), w_spec, w_spec],
                        out_specs=[o_spec, o_spec, o_spec], out_shape=[shp, shp, shp], scratch_shapes=[], operands=[n, wgt, wut],
                        side=side, name=name)


def _rms_residual(acc, h, gp, weight):
    r = lax.rsqrt(jnp.mean(acc * acc, axis=-1, keepdims=True) + RMS_EPS)
    return h + weight * (acc * r * gp)


def _ffn_down(hid, wd, h_in, gp, *, target=None, side=None, name):
    t, f = hid.shape
    d = wd.shape[1]
    tm = _tile(t, 256)
    row = pl.BlockSpec((tm, d), lambda i: (i, 0))
    shp = jax.ShapeDtypeStruct((t, d), F32)
    in_specs = [pl.BlockSpec((tm, f), lambda i: (i, 0)), _resident_spec((f, d)), row, _const_spec((1, d))]

    if target is None:
        def body(hid_ref, wd_ref, hin_ref, gp_ref, f_ref, hout_ref):
            acc = _dot(hid_ref[...], wd_ref[...], NN)
            f_ref[...] = acc
            hout_ref[...] = _rms_residual(acc, hin_ref[...], gp_ref[...], FFN_RESIDUAL_WEIGHT)

        return _hosted_call(body, grid=(t // tm,), in_specs=in_specs, out_specs=[row, row], out_shape=[shp, shp], scratch_shapes=[],
                            operands=[hid, wd, h_in, gp], side=side, name=name)

    def body_loss(hid_ref, wd_ref, hin_ref, gp_ref, tgt_ref, f_ref, dh_ref, loss_ref, df_ref, dgp_ref):
        @pl.when(pl.program_id(0) == 0)
        def _():
            loss_ref[...] = jnp.zeros_like(loss_ref)
            dgp_ref[...] = jnp.zeros_like(dgp_ref)
        acc = _dot(hid_ref[...], wd_ref[...], NN)
        f_ref[...] = acc
        r = lax.rsqrt(jnp.mean(acc * acc, axis=-1, keepdims=True) + RMS_EPS)
        fhat = acc * r
        e = hin_ref[...] + FFN_RESIDUAL_WEIGHT * (fhat * gp_ref[...]) - tgt_ref[...]
        dh = e * (1.0 / d)
        dh_ref[...] = dh
        per_row = jnp.sum(e * e, axis=1, keepdims=True) * (1.0 / d)
        loss_ref[...] += 0.5 * jnp.sum(per_row, axis=0, keepdims=True)
        dy = FFN_RESIDUAL_WEIGHT * dh
        dyg = dy * gp_ref[...]
        df_ref[...] = (r * (dyg - fhat * jnp.mean(dyg * fhat, axis=-1, keepdims=True))).astype(BF16)
        dgp_ref[...] += jnp.sum(dy * fhat, axis=0, keepdims=True)

    return _hosted_call(body_loss, grid=(t // tm,), in_specs=in_specs + [row],
                        out_specs=[row, row, _const_spec((1, LANES)), row, _const_spec((1, d))],
                        out_shape=[shp, shp, jax.ShapeDtypeStruct((1, LANES), F32), jax.ShapeDtypeStruct((t, d), BF16),
                                   jax.ShapeDtypeStruct((1, d), F32)],
                        scratch_shapes=[], operands=[hid, wd, h_in, gp, target], side=side, name=name)


def _post_bwd(dh, f, gp, weight, *, name):
    t, d = f.shape
    tm = _tile(t, 512)

    def body(dh_ref, f_ref, gp_ref, df_ref, dgp_ref):
        @pl.when(pl.program_id(0) == 0)
        def _():
            dgp_ref[...] = jnp.zeros_like(dgp_ref)
        fv = f_ref[...]
        dy = weight * dh_ref[...]
        r = lax.rsqrt(jnp.mean(fv * fv, axis=-1, keepdims=True) + RMS_EPS)
        fhat = fv * r
        dyg = dy * gp_ref[...]
        df_ref[...] = (r * (dyg - fhat * jnp.mean(dyg * fhat, axis=-1, keepdims=True))).astype(BF16)
        dgp_ref[...] += jnp.sum(dy * fhat, axis=0, keepdims=True)

    row = pl.BlockSpec((tm, d), lambda i: (i, 0))
    return pl.pallas_call(body, out_shape=(jax.ShapeDtypeStruct((t, d), BF16), jax.ShapeDtypeStruct((1, d), F32)), grid=(t // tm,),
                          in_specs=[row, row, _const_spec((1, d))], out_specs=(row, _const_spec((1, d))),
                          name=name, compiler_params=_params("arbitrary"))(dh, f, gp)


def _ffn_dhid(df, wd, g, u, *, name):
    t, d = df.shape
    f = wd.shape[0]
    tm, tn = _tile(t, 512), _tile(f, FFN_COL_TILE, LANES)

    def body(df_ref, wd_ref, g_ref, u_ref, dg_ref, du_ref):
        dh = _dot(df_ref[...], wd_ref[...], NT)
        gv = g_ref[...].astype(F32)
        uv = u_ref[...].astype(F32)
        sg = _sigmoid(gv)
        silu = gv * sg
        dg_ref[...] = (dh * uv * (sg + silu * (1.0 - sg))).astype(BF16)
        du_ref[...] = (dh * silu).astype(BF16)

    o_spec = pl.BlockSpec((tm, tn), lambda j, i: (i, j))
    shp = jax.ShapeDtypeStruct((t, f), BF16)
    return pl.pallas_call(body, out_shape=(shp, shp), grid=(f // tn, t // tm),
                          in_specs=[pl.BlockSpec((tm, d), lambda j, i: (i, 0)), pl.BlockSpec((tn, d), lambda j, i: (j, 0)), o_spec, o_spec],
                          out_specs=(o_spec, o_spec), name=name, compiler_params=_params("parallel", "arbitrary"))(df, wd, g, u)


def _ffn_forward(h_in, g_pre, w_up, wd, g_post, tag, side_norm=None, side_up=None, target=None, n=None):
    got_norm = ()
    if n is None:
        n, got_norm = _rms_fwd(h_in, g_pre, side=side_norm, name=f"{tag}_prenorm")
    wgt, wut = w_up(got_norm) if callable(w_up) else w_up
    (g, u, hid), got_up = _ffn_up(n, wgt, wut, side=side_up, name=f"{tag}_up")
    wd = wd(got_up) if callable(wd) else wd
    outs, _ = _ffn_down(hid, wd, h_in, g_post, target=target, name=f"{tag}_down")
    return outs[1:], (h_in, n, g, u, hid, outs[0]), (wgt, wut, wd), got_up


def _stack8(g):
    return g.reshape(N_DEV, g.shape[0] // N_DEV, g.shape[1])


def _ffn_backward(dh_out, saved, g_pre, wgt, wut, wd, g_post, tag, chain=False, post=None):
    h_in, n, g, u, hid, f = saved

    def side_of(grad16):
        return _scatter_exchange([_stack8(grad16)]) if chain else None

    df, dg_post = _post_bwd(dh_out, f, g_post, FFN_RESIDUAL_WEIGHT, name=f"{tag}_post_bwd") if post is None else post
    dgate, dup = _ffn_dhid(df, wd, g, u, name=f"{tag}_dhid")
    d_wd, d_wd16, _ = _mm_tn([hid], df, name=f"{tag}_dwd")
    d_wgt, d_wgt16, got_wd = _mm_tn([dgate], n, side=side_of(d_wd16), name=f"{tag}_dwg")
    d_wut, d_wut16, got_wg = _mm_tn([dup], n, side=side_of(d_wgt16), name=f"{tag}_dwu")
    dh_in, dg_pre, got_wu = _mm_nn_rmsbwd([(dgate, 0, 0), (dup, 1, 0)], [wgt, wut], h_in, g_pre, dh_out, side=side_of(d_wut16),
                                          name=f"{tag}_dn")
    received = (got_wg[0], got_wu[0], got_wd[0]) if chain else None
    return dh_in, dg_pre, dg_post, (d_wgt, d_wut, d_wd), (d_wgt16, d_wut16, d_wd16), received


CONV_ROWS = 128
HALO = 8


def _taps(w_ref):
    return [w_ref[k:k + 1, :] for k in range(SSM_CONV)]


def _conv_chunk(x_ref, xs, r0, taps, bias):
    xs[HALO + r0:HALO + r0 + CONV_ROWS, :] = x_ref[r0:r0 + CONV_ROWS, :].astype(F32)
    shifted = [xs[HALO + r0 - k:HALO + r0 - k + CONV_ROWS, :] for k in range(SSM_CONV)]
    pre = bias + shifted[0] * taps[SSM_CONV - 1]
    for k in range(1, SSM_CONV):
        pre = pre + shifted[k] * taps[SSM_CONV - 1 - k]
    return shifted, pre


def _fold_rows(a):
    return functools.reduce(jnp.add, [a[i:i + 8] for i in range(0, a.shape[0], 8)])


def _conv_fwd(xbc, conv_w, conv_b, bl, *, name):
    t, c = xbc.shape
    s = t // bl
    tc = LANES
    assert s % CONV_ROWS == 0

    def body(x_ref, w_ref, b_ref, o_ref, xs):
        taps, bias = _taps(w_ref), b_ref[...]
        xs[0:HALO, :] = jnp.zeros((HALO, tc), F32)
        for r0 in range(0, s, CONV_ROWS):
            _, pre = _conv_chunk(x_ref, xs, r0, taps, bias)
            o_ref[r0:r0 + CONV_ROWS, :] = (pre * _sigmoid(pre)).astype(o_ref.dtype)

    blk = pl.BlockSpec((s, tc), lambda b, j: (b, j))
    return pl.pallas_call(body, out_shape=jax.ShapeDtypeStruct((t, c), BF16), grid=(bl, c // tc),
                          in_specs=[blk, pl.BlockSpec((SSM_CONV, tc), lambda b, j: (0, j)), pl.BlockSpec((1, tc), lambda b, j: (0, j))],
                          out_specs=blk, scratch_shapes=[pltpu.VMEM((HALO + s, tc), F32)],
                          name=name, compiler_params=_params("parallel", "arbitrary"))(xbc, conv_w, conv_b)


def _conv_bwd(dxc, xbc, conv_w, conv_b, bl, *, name):
    t, c = xbc.shape
    s = t // bl
    tc = LANES

    def body(dy_ref, x_ref, w_ref, b_ref, dx_ref, dw_ref, db_ref, xs, dpre_s):
        @pl.when(pl.program_id(1) == 0)
        def _():
            dw_ref[...] = jnp.zeros_like(dw_ref)
            db_ref[...] = jnp.zeros_like(db_ref)

        taps, bias = _taps(w_ref), b_ref[...]
        zero8 = jnp.zeros((HALO, tc), F32)
        xs[0:HALO, :] = zero8
        dpre_s[s:s + HALO, :] = zero8
        sums = [zero8] * (SSM_CONV + 1)
        for r0 in range(0, s, CONV_ROWS):
            shifted, pre = _conv_chunk(x_ref, xs, r0, taps, bias)
            sg = _sigmoid(pre)
            dpre = dy_ref[r0:r0 + CONV_ROWS, :].astype(F32) * (sg * (1.0 + pre * (1.0 - sg)))
            dpre_s[r0:r0 + CONV_ROWS, :] = dpre
            sums = [acc + _fold_rows(dpre * sh) for acc, sh in zip(sums[:-1], shifted)] + [sums[-1] + _fold_rows(dpre)]
        for k in range(SSM_CONV):
            dw_ref[SSM_CONV - 1 - k:SSM_CONV - k, :] += jnp.sum(sums[k], axis=0, keepdims=True)
        db_ref[...] += jnp.sum(sums[-1], axis=0, keepdims=True)
        for r0 in range(0, s, CONV_ROWS):
            dx = dpre_s[r0:r0 + CONV_ROWS, :] * taps[SSM_CONV - 1]
            for k in range(1, SSM_CONV):
                dx = dx + dpre_s[r0 + k:r0 + k + CONV_ROWS, :] * taps[SSM_CONV - 1 - k]
            dx_ref[r0:r0 + CONV_ROWS, :] = dx.astype(dx_ref.dtype)

    blk = pl.BlockSpec((s, tc), lambda j, b: (b, j))
    return pl.pallas_call(
        body, out_shape=(jax.ShapeDtypeStruct((t, c), BF16), jax.ShapeDtypeStruct((8, c), F32), jax.ShapeDtypeStruct((1, c), F32)),
        grid=(c // tc, bl),
        in_specs=[blk, blk, pl.BlockSpec((SSM_CONV, tc), lambda j, b: (0, j)), pl.BlockSpec((1, tc), lambda j, b: (0, j))],
        out_specs=(blk, pl.BlockSpec((8, tc), lambda j, b: (0, j)), pl.BlockSpec((1, tc), lambda j, b: (0, j))),
        scratch_shapes=[pltpu.VMEM((HALO + s, tc), F32), pltpu.VMEM((s + HALO, tc), F32)],
        name=name, compiler_params=_params("parallel", "arbitrary"))(dxc, xbc, conv_w, conv_b)


def _softplus(x):
    return jnp.maximum(x, 0.0) + jnp.log1p(jnp.exp(-jnp.abs(x)))


def _hilo_dot(v, m_b, dims=NN):
    hi = v.astype(BF16)
    lo = (v - hi.astype(F32)).astype(BF16)
    return _dot(hi, m_b, dims) + _dot(lo, m_b, dims)


def _ssd_chunk_common(dtraw_ref, dtb_ref, alog_ref, dsk_ref, d_inner):
    q, p = CHUNK, SSM_HEAD_DIM
    shift = p.bit_length() - 1
    assert 1 << shift == p
    dt = _softplus(dtraw_ref[...] + dtb_ref[...])
    a = -jnp.exp(alog_ref[...])
    ii = lax.broadcasted_iota(jnp.int32, (q, q), 0)
    jj = lax.broadcasted_iota(jnp.int32, (q, q), 1)
    causal = ii >= jj
    tril = jnp.where(causal, 1.0, 0.0).astype(F32)
    triu = jnp.where(ii <= jj, 1.0, 0.0).astype(F32)
    a_cs = _dot_hi(tril, dt * a)
    a_cs_t = a_cs.T
    a_last = a_cs[q - 1:q, :]
    e_col = jnp.exp(a_cs)
    dec_end = jnp.exp(a_last - a_cs)
    head_of_col = lax.shift_right_logical(lax.broadcasted_iota(jnp.int32, (LANES, d_inner), 1), shift)
    spread = (lax.broadcasted_iota(jnp.int32, (LANES, d_inner), 0) == head_of_col).astype(BF16)
    exact = jnp.concatenate([dt, jnp.broadcast_to(dsk_ref[...], (8, LANES))], axis=0)
    hi = jnp.concatenate([exact, e_col, dec_end], axis=0).astype(BF16)
    lo = (exact - hi[:q + 8].astype(F32)).astype(BF16)
    wide = _dot(hi, spread, NN)
    fine = wide[:q + 8] + _dot(lo, spread, NN)
    return dict(dt=dt, a=a, a_cs=a_cs, a_cs_t=a_cs_t, a_last=a_last, dec_end=dec_end, causal=causal, triu=triu,
                dt_e=fine[:q], dsk_e=fine[q:q + 1], e_e=wide[q + 8:2 * q + 8], dec_e=wide[2 * q + 8:3 * q + 8])


def _fill_block_diag(bd_ref, src_ref, hpg, col0=0):
    q, p = CHUNK, SSM_HEAD_DIM
    for hh in range(hpg):
        bd_ref[hh * q:(hh + 1) * q, hh * p:(hh + 1) * p] = src_ref[:, col0 + hh * p:col0 + (hh + 1) * p]


def _lane_onehot(h):
    return (lax.broadcasted_iota(jnp.int32, (1, LANES), 1) == h).astype(F32)


def _ssd_fwd(xc, dt_raw, dt_bias, a_log, d_skip, bl, n_heads, *, side=None, name):
    t = xc.shape[0]
    q, p, nst, grp = CHUNK, SSM_HEAD_DIM, SSM_STATE, SSM_GROUPS
    d_inner = n_heads * p
    hpg = n_heads // grp
    hb = min(hpg, SSD_HEAD_BATCH)
    gw = hpg * p
    nc = t // bl // q
    assert d_inner % (grp * nst) == 0 and nst == LANES and hpg % hb == 0

    def body(xs_ref, b_ref, c_ref, dtraw_ref, dtb_ref, alog_ref, dsk_ref, y_ref, hprev_ref, state, m_all, x_bd, xdt_s):
        @pl.when(jnp.logical_and(pl.program_id(0) == 0, pl.program_id(1) == 0))
        def _():
            x_bd[...] = jnp.zeros_like(x_bd)

        @pl.when(pl.program_id(1) == 0)
        def _():
            state[...] = jnp.zeros_like(state)

        cm = _ssd_chunk_common(dtraw_ref, dtb_ref, alog_ref, dsk_ref, d_inner)
        for g in range(grp):
            cols = slice(g * gw, (g + 1) * gw)
            bg = b_ref[:, g * nst:(g + 1) * nst]
            cg = c_ref[:, g * nst:(g + 1) * nst]
            scores = _dot(cg, bg, NT)
            xs = xs_ref[:, cols].astype(F32)
            xdt = xs * cm['dt_e'][:, cols]
            xdt_s[...] = xdt.astype(BF16)
            y_parts = []
            for sub in range(hpg // hb):
                for k in range(hb):
                    h = g * hpg + sub * hb + k
                    seg = cm['a_cs'][:, h:h + 1] - cm['a_cs_t'][h:h + 1, :]
                    m_all[:, k * q:(k + 1) * q] = (scores * jnp.exp(jnp.where(cm['causal'], seg, NEG_INF))).astype(BF16)
                _fill_block_diag(x_bd, xdt_s, hb, sub * hb * p)
                y_parts.append(_dot(m_all[...], x_bd[...], NN))
            hprev = state[g]
            hprev_ref[g] = hprev
            y = jnp.concatenate(y_parts, axis=1) + cm['e_e'][:, cols] * _dot(cg, hprev.astype(BF16), NT)
            y_ref[:, cols] = y + cm['dsk_e'][:, cols] * xs
            st = _dot((xdt * cm['dec_e'][:, cols]).astype(BF16), bg, TN)
            for hh in range(hpg):
                h = g * hpg + hh
                rows = slice(hh * p, (hh + 1) * p)
                state[g, rows, :] = jnp.exp(cm['a_last'][:, h:h + 1]) * hprev[rows] + st[rows]

    gn = grp * nst

    def rowmap(b, c):
        return b * nc + c
    vec = pl.BlockSpec((1, LANES), lambda b, c: (0, 0))
    return _hosted_call(
        body,
        out_shape=[jax.ShapeDtypeStruct((t, d_inner), F32), jax.ShapeDtypeStruct((t // q, grp, gw, nst), F32)],
        grid=(bl, nc),
        in_specs=[pl.BlockSpec((q, d_inner), lambda b, c: (rowmap(b, c), 0)),
                  pl.BlockSpec((q, gn), lambda b, c: (rowmap(b, c), d_inner // gn)),
                  pl.BlockSpec((q, gn), lambda b, c: (rowmap(b, c), d_inner // gn + 1)),
                  pl.BlockSpec((q, LANES), lambda b, c: (rowmap(b, c), 0)), vec, vec, vec],
        out_specs=[pl.BlockSpec((q, d_inner), lambda b, c: (rowmap(b, c), 0)),
                   pl.BlockSpec((None, grp, gw, nst), lambda b, c: (rowmap(b, c), 0, 0, 0))],
        scratch_shapes=[pltpu.VMEM((grp, gw, nst), F32), pltpu.VMEM((q, hb * q), BF16), pltpu.VMEM((hb * q, hb * p), BF16),
                        pltpu.VMEM((q, gw), BF16)],
        operands=[xc, xc, xc, dt_raw, dt_bias, a_log, d_skip], side=side, name=name)


def _ssd_bwd(dy, y, xc, dt_raw, hprev_all, dt_bias, a_log, d_skip, bl, n_heads, *, side=None, name):
    t, c_dim = xc.shape
    q, p, nst, grp = CHUNK, SSM_HEAD_DIM, SSM_STATE, SSM_GROUPS
    d_inner = n_heads * p
    hpg = n_heads // grp
    hb = min(hpg, SSD_HEAD_BATCH)
    gw = hpg * p
    nc = t // bl // q
    gn = grp * nst
    shift = p.bit_length() - 1

    def body(dy_ref, y_ref, xs_ref, b_ref, c_ref, dtraw_ref, hprev_ref, dtb_ref, alog_ref, dsk_ref,
             dxc_ref, ddtraw_ref, ddtb_ref, dalog_ref, ddsk_ref, dstate, mt_all, x_bd, dy_bd, xdt_s):
        @pl.when(jnp.logical_and(pl.program_id(0) == 0, pl.program_id(1) == 0))
        def _():
            ddtb_ref[...] = jnp.zeros_like(ddtb_ref)
            dalog_ref[...] = jnp.zeros_like(dalog_ref)
            ddsk_ref[...] = jnp.zeros_like(ddsk_ref)
            x_bd[...] = jnp.zeros_like(x_bd)
            dy_bd[...] = jnp.zeros_like(dy_bd)

        @pl.when(pl.program_id(1) == 0)
        def _():
            dstate[...] = jnp.zeros_like(dstate)

        cm = _ssd_chunk_common(dtraw_ref, dtb_ref, alog_ref, dsk_ref, d_inner)
        causal = cm['causal']
        upper = cm['triu'] > 0.5
        seg_row = lax.shift_right_logical(lax.broadcasted_iota(jnp.int32, (gw, LANES), 0), shift)
        seg_lane = lax.broadcasted_iota(jnp.int32, (gw, LANES), 1)
        sums = jnp.zeros((5 * q, LANES), F32)
        state_dot = jnp.zeros((1, LANES), F32)
        for g in range(grp):
            cols = slice(g * gw, (g + 1) * gw)
            seg_sum = (seg_row + g * hpg == seg_lane).astype(BF16)
            bg = b_ref[:, g * nst:(g + 1) * nst]
            cg = c_ref[:, g * nst:(g + 1) * nst]
            scores_t = _dot(bg, cg, NT)
            xs = xs_ref[:, cols].astype(F32)
            xdt = xs * cm['dt_e'][:, cols]
            xdt_s[...] = xdt.astype(BF16)
            dyf = dy_ref[:, cols].astype(F32)
            dscores = jnp.zeros((q, q), F32)
            dx_parts = []
            for sub in range(hpg // hb):
                c0 = sub * hb * p
                _fill_block_diag(x_bd, xdt_s, hb, c0)
                _fill_block_diag(dy_bd, dy_ref, hb, g * gw + c0)
                dm_all = _dot(dy_ref[:, g * gw + c0:g * gw + c0 + hb * p], x_bd[...], NT)
                for k in range(hb):
                    h = g * hpg + sub * hb + k
                    blk = slice(k * q, (k + 1) * q)
                    seg = cm['a_cs'][:, h:h + 1] - cm['a_cs_t'][h:h + 1, :]
                    mt_all[:, blk] = (scores_t * jnp.exp(jnp.where(upper, -seg, NEG_INF))).astype(BF16)
                    dscores = dscores + dm_all[:, blk] * jnp.exp(jnp.where(causal, seg, NEG_INF))
                dx_parts.append(_dot(mt_all[...], dy_bd[...], NN))
            hprev = hprev_ref[g]
            hprev_b = hprev.astype(BF16)
            dhn = dstate[g]
            dhn_b = dhn.astype(BF16)
            e_e, dec_e = cm['e_e'][:, cols], cm['dec_e'][:, cols]
            y_scan = y_ref[:, cols] - cm['dsk_e'][:, cols] * xs
            dye_b = (dyf * e_e).astype(BF16)
            dcg = _dot(dye_b, hprev_b, NN)
            dhp = _dot(dye_b, cg, TN)
            bdh = _dot(bg, dhn_b, NT)
            dbg = _dot((xdt * dec_e).astype(BF16), dhn_b, NN)
            dx_diag = jnp.concatenate(dx_parts, axis=1)
            dx = dec_e * bdh + dx_diag
            ds_b = dscores.astype(BF16)
            dcg = dcg + _dot(ds_b, bg, NN)
            dbg = dbg + _dot(ds_b, cg, TN)
            x_rounded = xdt_s[...].astype(F32)
            sums = sums + _hilo_dot(jnp.concatenate([dyf * y_scan, xdt * bdh, x_rounded * dx_diag, dx * xs, dyf * xs], axis=0), seg_sum)
            state_dot = state_dot + jnp.sum(_hilo_dot(dhn * hprev, seg_sum, TN), axis=0, keepdims=True)
            dxc_ref[:, cols] = (dx * cm['dt_e'][:, cols] + cm['dsk_e'][:, cols] * dyf).astype(dxc_ref.dtype)
            dxc_ref[:, d_inner + g * nst:d_inner + (g + 1) * nst] = dbg.astype(dxc_ref.dtype)
            dxc_ref[:, d_inner + gn + g * nst:d_inner + gn + (g + 1) * nst] = dcg.astype(dxc_ref.dtype)
            for hh in range(hpg):
                h = g * hpg + hh
                rows = slice(hh * p, (hh + 1) * p)
                dstate[g, rows, :] = jnp.exp(cm['a_last'][:, h:h + 1]) * dhn[rows] + dhp[rows]
        s_y, s_end, s_diag, s_dt, s_skip = (sums[k * q:(k + 1) * q] for k in range(5))
        dt, a, dec_end = cm['dt'], cm['a'], cm['dec_end']
        last_row = (lax.broadcasted_iota(jnp.int32, (q, 1), 0) == q - 1).astype(F32)
        da_last = jnp.sum(dec_end * s_end, axis=0, keepdims=True) + jnp.exp(cm['a_last']) * state_dot
        da = s_y - dec_end * s_end - s_diag + last_row * da_last
        ddta = _dot_hi(cm['triu'], da)
        ddt = s_dt + ddta * a
        d_a = jnp.sum(ddta * dt, axis=0, keepdims=True)
        ddt_raw = ddt * _sigmoid(dtraw_ref[...] + dtb_ref[...])
        ddtraw_ref[...] = ddt_raw
        ddtb_ref[...] += jnp.sum(ddt_raw, axis=0, keepdims=True)
        dalog_ref[...] += d_a * a
        ddsk_ref[...] += jnp.sum(s_skip, axis=0, keepdims=True)

    def rowmap(b, c):
        return b * nc + (nc - 1 - c)
    vec = pl.BlockSpec((1, LANES), lambda b, c: (0, 0))
    vec_shape = jax.ShapeDtypeStruct((1, LANES), F32)
    return _hosted_call(
        body,
        out_shape=[jax.ShapeDtypeStruct((t, c_dim), BF16), jax.ShapeDtypeStruct((t, LANES), F32), vec_shape, vec_shape, vec_shape],
        grid=(bl, nc),
        in_specs=[pl.BlockSpec((q, d_inner), lambda b, c: (rowmap(b, c), 0)),
                  pl.BlockSpec((q, d_inner), lambda b, c: (rowmap(b, c), 0)),
                  pl.BlockSpec((q, d_inner), lambda b, c: (rowmap(b, c), 0)),
                  pl.BlockSpec((q, gn), lambda b, c: (rowmap(b, c), d_inner // gn)),
                  pl.BlockSpec((q, gn), lambda b, c: (rowmap(b, c), d_inner // gn + 1)),
                  pl.BlockSpec((q, LANES), lambda b, c: (rowmap(b, c), 0)),
                  pl.BlockSpec((None, grp, gw, nst), lambda b, c: (rowmap(b, c), 0, 0, 0)), vec, vec, vec],
        out_specs=[pl.BlockSpec((q, c_dim), lambda b, c: (rowmap(b, c), 0)),
                   pl.BlockSpec((q, LANES), lambda b, c: (rowmap(b, c), 0)), vec, vec, vec],
        scratch_shapes=[pltpu.VMEM((grp, gw, nst), F32), pltpu.VMEM((q, hb * q), BF16),
                        pltpu.VMEM((hb * q, hb * p), BF16), pltpu.VMEM((hb * q, hb * p), BF16), pltpu.VMEM((q, gw), BF16)],
        operands=[dy, y, xc, xc, xc, dt_raw, hprev_all, dt_bias, a_log, d_skip], side=side, name=name)


def _bucket_onehot():
    blk = CHUNK
    qi = jnp.arange(blk)[:, None]
    kj = jnp.arange(2 * blk)[None, :]
    dist = jnp.maximum(qi + blk - kj, 0)
    max_exact = REL_BUCKETS // 2
    d = jnp.maximum(dist, 1).astype(F32)
    large = max_exact + (jnp.log(d / max_exact) / math.log(REL_MAX_DISTANCE / max_exact) * (REL_BUCKETS - max_exact)).astype(jnp.int32)
    large = jnp.minimum(large, REL_BUCKETS - 1)
    bucket = jnp.where(dist < max_exact, dist, large).reshape(-1)
    return (bucket[None, :] == jnp.arange(REL_BUCKETS)[:, None]).astype(F32)


def _small_mm_hi(a, b, dims, *, name):
    def body(a_ref, b_ref, o_ref):
        o_ref[...] = _dot_hi(a_ref[...], b_ref[...], dims)
    n = b.shape[0] if dims == NT else b.shape[1]
    return pl.pallas_call(body, out_shape=jax.ShapeDtypeStruct((a.shape[0], n), F32), name=name)(a, b)


def _attn_band_mask_t(n, rep):
    blk = CHUNK
    jj = lax.broadcasted_iota(jnp.int32, (2 * blk, rep * blk), 0)
    ii = lax.broadcasted_iota(jnp.int32, (2 * blk, rep * blk), 1) & (blk - 1)
    dist = ii + blk - jj
    in_window = jnp.logical_and(dist >= 0, dist < blk)
    return jnp.logical_and(in_window, jnp.logical_or(jj >= blk, n > 0))


def _sink_row(sink_ref, heads):
    return jnp.concatenate([jnp.broadcast_to(sink_ref[:, h:h + 1], (1, CHUNK)) for h in heads], axis=1)


def _attn_fwd(q, kv, bias_t, sinks, bl, *, name):
    t, qd = q.shape
    blk, hd = CHUNK, ATTN_HEAD_DIM
    kvd = ATTN_KV_HEADS * hd
    rep = ATTN_Q_HEADS // ATTN_KV_HEADS
    nb = t // bl // blk
    scale = hd ** -0.5

    def body(q_ref, kp_ref, kc_ref, vp_ref, vc_ref, bias_ref, sink_ref, o_ref, lse_ref):
        n = pl.program_id(1)
        mask = _attn_band_mask_t(n, rep)
        for kvh in range(ATTN_KV_HEADS):
            ks = slice(kvh * hd, (kvh + 1) * hd)
            heads = range(kvh * rep, (kvh + 1) * rep)
            qs = jnp.concatenate([q_ref[:, h * hd:(h + 1) * hd] for h in heads], axis=0)
            kk = jnp.concatenate([kp_ref[:, ks], kc_ref[:, ks]], axis=0)
            vv = jnp.concatenate([vp_ref[:, ks], vc_ref[:, ks]], axis=0)
            s = jnp.where(mask, _dot(kk, qs, NT) * scale + bias_ref[kvh], NEG_INF)
            sink = _sink_row(sink_ref, heads)
            m = jnp.maximum(jnp.max(s, axis=0, keepdims=True), sink)
            p = jnp.exp(s - m)
            den = jnp.sum(p, axis=0, keepdims=True) + jnp.exp(sink - m)
            o = _dot((p * (1.0 / den)).astype(BF16), vv, TN)
            lse = m + jnp.log(den)
            for r, h in enumerate(heads):
                o_ref[:, h * hd:(h + 1) * hd] = o[r * blk:(r + 1) * blk].astype(o_ref.dtype)
                lse_ref[h:h + 1, :] = lse[:, r * blk:(r + 1) * blk]

    def cur(b, n):
        return b * nb + n

    def prev(b, n):
        return b * nb + jnp.maximum(n - 1, 0)
    return pl.pallas_call(
        body, out_shape=(jax.ShapeDtypeStruct((t, qd), BF16), jax.ShapeDtypeStruct((t // blk * ATTN_Q_HEADS, blk), F32)), grid=(bl, nb),
        in_specs=[pl.BlockSpec((blk, qd), lambda b, n: (cur(b, n), 0)),
                  pl.BlockSpec((blk, kvd), lambda b, n: (prev(b, n), 0)), pl.BlockSpec((blk, kvd), lambda b, n: (cur(b, n), 0)),
                  pl.BlockSpec((blk, kvd), lambda b, n: (prev(b, n), 1)), pl.BlockSpec((blk, kvd), lambda b, n: (cur(b, n), 1)),
                  _const_spec(bias_t.shape), _const_spec((1, LANES))],
        out_specs=(pl.BlockSpec((blk, qd), lambda b, n: (cur(b, n), 0)),
                   pl.BlockSpec((ATTN_Q_HEADS, blk), lambda b, n: (cur(b, n), 0))),
        name=name, compiler_params=_params("parallel", "arbitrary"))(q, kv, kv, kv, kv, bias_t, sinks)


def _attn_bwd(do, q, kv, lse, bias_t, sinks, bl, *, name):
    t, qd = q.shape
    blk, hd = CHUNK, ATTN_HEAD_DIM
    kvd = ATTN_KV_HEADS * hd
    rep = ATTN_Q_HEADS // ATTN_KV_HEADS
    s_len = t // bl
    nb = s_len // blk
    scale = hd ** -0.5

    def body(do_ref, q_ref, kp_ref, kc_ref, vp_ref, vc_ref, lse_ref, bias_ref, sink_ref, dq_ref, dkv_ref, dbias_ref, dsink_ref):
        n = pl.program_id(1)

        @pl.when(jnp.logical_and(pl.program_id(0) == 0, n == 0))
        def _():
            dbias_ref[...] = jnp.zeros_like(dbias_ref)
            dsink_ref[...] = jnp.zeros_like(dsink_ref)

        mask = _attn_band_mask_t(n, rep)
        r_cur = pl.multiple_of(n * blk, blk)
        r_prev = pl.multiple_of(jnp.maximum(n - 1, 0) * blk, blk)
        dsink = jnp.zeros((1, LANES), F32)
        for kvh in range(ATTN_KV_HEADS):
            ks = slice(kvh * hd, (kvh + 1) * hd)
            heads = range(kvh * rep, (kvh + 1) * rep)
            qs = jnp.concatenate([q_ref[:, h * hd:(h + 1) * hd] for h in heads], axis=0)
            dos = jnp.concatenate([do_ref[:, h * hd:(h + 1) * hd] for h in heads], axis=0)
            kk = jnp.concatenate([kp_ref[:, ks], kc_ref[:, ks]], axis=0)
            vv = jnp.concatenate([vp_ref[:, ks], vc_ref[:, ks]], axis=0)
            lse = jnp.concatenate([lse_ref[h:h + 1, :] for h in heads], axis=1)
            p = jnp.exp(jnp.where(mask, _dot(kk, qs, NT) * scale + bias_ref[kvh], NEG_INF) - lse)
            dp = _dot(vv, dos, NT)
            delta = jnp.sum(p * dp, axis=0, keepdims=True)
            ds = p * (dp - delta)
            dsink_row = jnp.exp(_sink_row(sink_ref, heads) - lse) * delta
            dbias_ref[kvh] += ds
            ds_b = ds.astype(BF16)
            dq_s = _dot(ds_b, kk, TN) * scale
            dkk = _dot(ds_b, qs, NN) * scale
            dvv = _dot(p.astype(BF16), dos, NN)
            for r, h in enumerate(heads):
                dq_ref[:, h * hd:(h + 1) * hd] = dq_s[r * blk:(r + 1) * blk].astype(dq_ref.dtype)
                dsink = dsink - jnp.sum(dsink_row[:, r * blk:(r + 1) * blk], axis=1, keepdims=True) * _lane_onehot(h)
            vs = slice(kvd + kvh * hd, kvd + (kvh + 1) * hd)
            dkv_ref[pl.ds(r_cur, blk), ks] = dkk[blk:]
            dkv_ref[pl.ds(r_cur, blk), vs] = dvv[blk:]

            @pl.when(n > 0)
            def _():
                dkv_ref[pl.ds(r_prev, blk), ks] += dkk[:blk]
                dkv_ref[pl.ds(r_prev, blk), vs] += dvv[:blk]
        dsink_ref[...] += dsink

    def cur(b, n):
        return b * nb + n

    def prev(b, n):
        return b * nb + jnp.maximum(n - 1, 0)
    qspec = pl.BlockSpec((blk, qd), lambda b, n: (cur(b, n), 0))
    return pl.pallas_call(
        body,
        out_shape=(jax.ShapeDtypeStruct((t, qd), BF16), jax.ShapeDtypeStruct((t, 2 * kvd), F32),
                   jax.ShapeDtypeStruct(bias_t.shape, F32), jax.ShapeDtypeStruct((1, LANES), F32)),
        grid=(bl, nb),
        in_specs=[qspec, qspec,
                  pl.BlockSpec((blk, kvd), lambda b, n: (prev(b, n), 0)), pl.BlockSpec((blk, kvd), lambda b, n: (cur(b, n), 0)),
                  pl.BlockSpec((blk, kvd), lambda b, n: (prev(b, n), 1)), pl.BlockSpec((blk, kvd), lambda b, n: (cur(b, n), 1)),
                  pl.BlockSpec((ATTN_Q_HEADS, blk), lambda b, n: (cur(b, n), 0)), _const_spec(bias_t.shape), _const_spec((1, LANES))],
        out_specs=(qspec, pl.BlockSpec((s_len, 2 * kvd), lambda b, n: (b, 0)), _const_spec(bias_t.shape), _const_spec((1, LANES))),
        name=name, compiler_params=_params("arbitrary", "arbitrary"))(do, q, kv, kv, kv, kv, lse, bias_t, sinks)


def _merge_fwd(y, z, ng, o, gs, ga, w_ssm, w_attn, w_out, h_in, g_post, g_next, *, name):
    t, d = h_in.shape
    tm = _tile(t, 256)
    d_ssm = y.shape[1]
    gw = d_ssm // SSM_GROUPS

    def body(y_ref, z_ref, ng_ref, o_ref, gs_ref, ga_ref, ws_ref, wa_ref, wo_ref, hin_ref, gp_ref, gn_ref,
             yn_ref, ys_ref, ya_ref, mg_ref, mix_ref, hout_ref, next_ref):
        for g in range(SSM_GROUPS):
            sl = slice(g * gw, (g + 1) * gw)
            zv = z_ref[:, sl].astype(F32)
            yg = y_ref[:, sl] * (zv * _sigmoid(zv))
            r = lax.rsqrt(jnp.mean(yg * yg, axis=-1, keepdims=True) + RMS_EPS)
            yn_ref[:, sl] = (yg * r * ng_ref[:, sl]).astype(BF16)
        ys = _dot(yn_ref[...], ws_ref[...], NN)
        ya = _dot(o_ref[...], wa_ref[...], NN)
        merged = (_sigmoid(gs_ref[...].astype(F32)) * ys + _sigmoid(ga_ref[...].astype(F32)) * ya).astype(BF16)
        mix = _dot(merged, wo_ref[...], NN)
        ys_ref[...] = ys.astype(BF16)
        ya_ref[...] = ya.astype(BF16)
        mg_ref[...] = merged
        mix_ref[...] = mix
        hout = _rms_residual(mix, hin_ref[...], gp_ref[...], 1.0)
        hout_ref[...] = hout
        rn = lax.rsqrt(jnp.mean(hout * hout, axis=-1, keepdims=True) + RMS_EPS)
        next_ref[...] = (hout * rn * gn_ref[...]).astype(BF16)

    def row(w):
        return pl.BlockSpec((tm, w), lambda i: (i, 0))
    bshape = jax.ShapeDtypeStruct((t, d), BF16)
    fshape = jax.ShapeDtypeStruct((t, d), F32)
    return pl.pallas_call(
        body, out_shape=(jax.ShapeDtypeStruct((t, d_ssm), BF16), bshape, bshape, bshape, fshape, fshape, bshape), grid=(t // tm,),
        in_specs=[row(d_ssm), row(d_ssm), _const_spec((1, d_ssm)), row(o.shape[1]), row(d), row(d), _resident_spec(w_ssm.shape),
                  _resident_spec(w_attn.shape), _resident_spec(w_out.shape), row(d), _const_spec((1, d)), _const_spec((1, d))],
        out_specs=(row(d_ssm),) + (row(d),) * 6, name=name,
        compiler_params=_params("parallel"))(y, z, ng, o, gs, ga, w_ssm, w_attn, w_out, h_in, g_post, g_next)


def _merge_bwd(dh, mix, g_post, gs, ga, ys, ya, y, z, ng, w_ssm, w_attn, w_out, *, name):
    t, d = mix.shape
    tm = _tile(t, 256)
    d_ssm, d_attn = w_ssm.shape[0], w_attn.shape[0]
    gw = d_ssm // SSM_GROUPS

    def body(dh_ref, mix_ref, gp_ref, gs_ref, ga_ref, ys_ref, ya_ref, y_ref, z_ref, ng_ref, ws_ref, wa_ref, wo_ref,
             dmix_ref, dys_ref, dya_ref, dgs_ref, dga_ref, dy_ref, dz_ref, do_ref, dgp_ref, dng_ref):
        @pl.when(pl.program_id(0) == 0)
        def _():
            dgp_ref[...] = jnp.zeros_like(dgp_ref)
            dng_ref[...] = jnp.zeros_like(dng_ref)
        mv = mix_ref[...]
        dy = dh_ref[...]
        r = lax.rsqrt(jnp.mean(mv * mv, axis=-1, keepdims=True) + RMS_EPS)
        mhat = mv * r
        dyg = dy * gp_ref[...]
        dmix = (r * (dyg - mhat * jnp.mean(dyg * mhat, axis=-1, keepdims=True))).astype(BF16)
        dgp_ref[...] += jnp.sum(dy * mhat, axis=0, keepdims=True)
        dmix_ref[...] = dmix
        dmerged = _dot(dmix, wo_ref[...], NT)
        sgs = _sigmoid(gs_ref[...].astype(F32))
        sga = _sigmoid(ga_ref[...].astype(F32))
        dys = (dmerged * sgs).astype(BF16)
        dya = (dmerged * sga).astype(BF16)
        dys_ref[...] = dys
        dya_ref[...] = dya
        dgs_ref[...] = (dmerged * ys_ref[...].astype(F32) * sgs * (1.0 - sgs)).astype(BF16)
        dga_ref[...] = (dmerged * ya_ref[...].astype(F32) * sga * (1.0 - sga)).astype(BF16)
        do_ref[...] = _dot(dya, wa_ref[...], NT).astype(BF16)
        dyn = _dot(dys, ws_ref[...], NT)
        for g in range(SSM_GROUPS):
            sl = slice(g * gw, (g + 1) * gw)
            zv = z_ref[:, sl].astype(F32)
            yv = y_ref[:, sl]
            sg = _sigmoid(zv)
            sz = zv * sg
            yg = yv * sz
            rg = lax.rsqrt(jnp.mean(yg * yg, axis=-1, keepdims=True) + RMS_EPS)
            yhat = yg * rg
            dn = dyn[:, sl]
            dyg_n = dn * ng_ref[:, sl]
            dyg_g = rg * (dyg_n - yhat * jnp.mean(dyg_n * yhat, axis=-1, keepdims=True))
            dy_ref[:, sl] = (dyg_g * sz).astype(BF16)
            dz_ref[:, sl] = (dyg_g * yv * (sg * (1.0 + zv * (1.0 - sg)))).astype(BF16)
            dng_ref[:, sl] += jnp.sum(dn * yhat, axis=0, keepdims=True)

    def row(w):
        return pl.BlockSpec((tm, w), lambda i: (i, 0))

    def bshape(w):
        return jax.ShapeDtypeStruct((t, w), BF16)
    return pl.pallas_call(
        body, out_shape=(bshape(d),) * 5 + (bshape(d_ssm), bshape(d_ssm), bshape(d_attn), jax.ShapeDtypeStruct((1, d), F32),
                                            jax.ShapeDtypeStruct((1, d_ssm), F32)), grid=(t // tm,),
        in_specs=[row(d), row(d), _const_spec((1, d)), row(d), row(d), row(d), row(d), row(d_ssm), row(d_ssm), _const_spec((1, d_ssm)),
                  _resident_spec(w_ssm.shape), _resident_spec(w_attn.shape), _resident_spec(w_out.shape)],
        out_specs=(row(d),) * 5 + (row(d_ssm), row(d_ssm), row(d_attn), _const_spec((1, d)), _const_spec((1, d_ssm))),
        name=name, compiler_params=_params("arbitrary"))(dh, mix, g_post, gs, ga, ys, ya, y, z, ng, w_ssm, w_attn, w_out)


def _position():
    return lax.axis_index("x"), lax.axis_index("y"), lax.axis_index("c")


def _gather_exchange(shards):
    na = len(shards)

    def plan(ins, outs, sems):
        send_sems, recv_sems, local_sems = sems
        x, y, c = _position()
        me, sibling = (x, y, c), (x, y, 1 - c)
        chips = [(1 - x, y), (x, 1 - y), (1 - x, 1 - y)]

        def slot(a, pos):
            return outs[a].at[4 * pos[0] + 2 * pos[1] + pos[2]]

        def copy(a, k, block, to, src=None):
            return pltpu.make_async_remote_copy(
                src_ref=slot(a, block) if src is None else src, dst_ref=slot(a, block),
                send_sem=send_sems.at[a, k], recv_sem=recv_sems.at[a, k], device_id=to, device_id_type=MESH)

        mine = [pltpu.make_async_copy(ins[a], slot(a, me), local_sems.at[a]) for a in range(na)]
        first = []
        for a in range(na):
            first.append(copy(a, 0, me, sibling, src=ins[a]))
            first += [copy(a, 1 + j, me, (*chip, c), src=ins[a]) for j, chip in enumerate(chips)]
        return me, sibling, chips, copy, mine, first

    def start(ins, outs, sems):
        *_, mine, first = plan(ins, outs, sems)
        for cp in mine + first:
            cp.start()

    def finish(ins, outs, sems):
        me, sibling, chips, copy, mine, first = plan(ins, outs, sems)
        c = me[2]
        passed = []
        for a in range(na):
            for j, chip in enumerate(chips):
                copy(a, 1 + j, (*chip, c), me).wait_recv()
                fwd = copy(a, 4 + j, (*chip, c), sibling)
                fwd.start()
                passed.append(fwd)
        for a in range(na):
            copy(a, 0, sibling, me).wait_recv()
            for j, chip in enumerate(chips):
                copy(a, 4 + j, (*chip, 1 - c), me).wait_recv()
        for cp in first + passed:
            cp.wait_send()
        for cp in mine:
            cp.wait()

    return _Exchange(list(shards), [jax.ShapeDtypeStruct((N_DEV,) + s.shape, s.dtype) for s in shards],
                     [pltpu.SemaphoreType.DMA((na, 7)), pltpu.SemaphoreType.DMA((na, 7)), pltpu.SemaphoreType.DMA((na,))],
                     start, finish)


def _scatter_exchange(arrays):
    na = len(arrays)

    def copies(ins, outs, sems):
        send_sems, recv_sems = sems
        x, y, c = _position()
        out = []
        for a in range(na):
            for k in range(7):
                flip = k + 1
                peer = (x ^ (flip >> 2), y ^ ((flip >> 1) & 1), c ^ (flip & 1))
                peer_block = 4 * peer[0] + 2 * peer[1] + peer[2]
                out.append(pltpu.make_async_remote_copy(
                    src_ref=ins[a].at[peer_block], dst_ref=outs[a].at[k],
                    send_sem=send_sems.at[a, k], recv_sem=recv_sems.at[a, k], device_id=peer, device_id_type=MESH))
        return out

    def start(ins, outs, sems):
        for cp in copies(ins, outs, sems):
            cp.start()

    def finish(ins, outs, sems):
        for cp in copies(ins, outs, sems):
            cp.wait()

    return _Exchange(list(arrays), [jax.ShapeDtypeStruct((7,) + s.shape[1:], s.dtype) for s in arrays],
                     [pltpu.SemaphoreType.DMA((na, 7)), pltpu.SemaphoreType.DMA((na, 7))], start, finish)


def _reduce_adamw(own, recv, w, m, v, *, name):
    r, c = own.shape
    tm = _tile(r, 256)
    c1 = 1.0 - ADAM_B1 ** ADAM_STEP
    c2 = 1.0 - ADAM_B2 ** ADAM_STEP

    def body(own_ref, recv_ref, w_ref, m_ref, v_ref, g_ref, d_ref, mo_ref, vo_ref):
        gv = own_ref[...]
        for k in range(7):
            gv = gv + recv_ref[k].astype(F32)
        g_ref[...] = gv
        mn = ADAM_B1 * m_ref[...] + (1.0 - ADAM_B1) * gv
        vn = ADAM_B2 * v_ref[...] + (1.0 - ADAM_B2) * (gv * gv)
        mo_ref[...] = mn
        vo_ref[...] = vn
        d_ref[...] = -ADAM_LR * ((mn / c1) / (jnp.sqrt(vn / c2) + ADAM_EPS) + ADAM_WD * w_ref[...])

    blk = pl.BlockSpec((tm, c), lambda i: (i, 0))
    shp = jax.ShapeDtypeStruct((r, c), F32)
    return pl.pallas_call(body, out_shape=(shp,) * 4, grid=(r // tm,),
                          in_specs=[blk, pl.BlockSpec((7, tm, c), lambda i: (0, i, 0)), blk, blk, blk], out_specs=(blk,) * 4,
                          name=name, compiler_params=_params("parallel"))(own, recv, w, m, v)


SMALL_ROW = 8 * LANES


def _update_replicated(partials, ws, ms, vs, *, name):
    n = len(ws)
    pieces, n_rows = [], 0
    for k in sorted(range(n), key=lambda k_: -ws[k_].shape[0]):
        r, c = ws[k].shape
        assert (r == 1 or c <= SMALL_ROW) and (r == 1 or n_rows % 8 == 0)
        for c0 in range(0, c, SMALL_ROW):
            pieces.append((k, slice(0, r), slice(c0, min(c0 + SMALL_ROW, c)), n_rows))
            n_rows += r
    n_rows = -(-n_rows // 8) * 8
    c1 = 1.0 - ADAM_B1 ** ADAM_STEP
    c2 = 1.0 - ADAM_B2 ** ADAM_STEP

    def reduce_body(*refs):
        g_in, total, buf, send_sems, recv_sems = refs[:n], refs[n], refs[n + 1], refs[n + 2], refs[n + 3]
        x, y, c_ = _position()
        me = 4 * x + 2 * y + c_
        buf[me] = jnp.zeros((n_rows, SMALL_ROW), F32)
        for k, rs, cs, row0 in pieces:
            buf[me, row0:row0 + rs.stop, 0:cs.stop - cs.start] = g_in[k][rs, cs]
        copies = []
        for j in range(7):
            flip = j + 1
            peer = (x ^ (flip >> 2), y ^ ((flip >> 1) & 1), c_ ^ (flip & 1))
            cp = pltpu.make_async_remote_copy(
                src_ref=buf.at[me], dst_ref=buf.at[me], send_sem=send_sems.at[j], recv_sem=recv_sems.at[j],
                device_id=peer, device_id_type=MESH)
            cp.start()
            copies.append(cp)
        for cp in copies:
            cp.wait()
        acc = buf[0]
        for dev in range(1, N_DEV):
            acc = acc + buf[dev]
        total[...] = acc

    def step_body(*refs):
        total = refs[0]
        w_in_, m_in, v_in = (refs[1 + j * n:1 + (j + 1) * n] for j in range(3))
        g_out, d_out, m_out, v_out = (refs[1 + (3 + j) * n:1 + (4 + j) * n] for j in range(4))
        for k, rs, cs, row0 in pieces:
            gv = total[row0:row0 + rs.stop, 0:cs.stop - cs.start]
            mn = ADAM_B1 * m_in[k][rs, cs] + (1.0 - ADAM_B1) * gv
            vn = ADAM_B2 * v_in[k][rs, cs] + (1.0 - ADAM_B2) * (gv * gv)
            g_out[k][rs, cs] = gv
            m_out[k][rs, cs] = mn
            v_out[k][rs, cs] = vn
            d_out[k][rs, cs] = -ADAM_LR * ((mn / c1) / (jnp.sqrt(vn / c2) + ADAM_EPS) + ADAM_WD * w_in_[k][rs, cs])

    vm = pl.BlockSpec(memory_space=pltpu.VMEM)
    total = pl.pallas_call(
        reduce_body, out_shape=jax.ShapeDtypeStruct((n_rows, SMALL_ROW), F32), in_specs=[vm] * n, out_specs=vm,
        scratch_shapes=[pltpu.VMEM((N_DEV, n_rows, SMALL_ROW), F32), pltpu.SemaphoreType.DMA((7,)), pltpu.SemaphoreType.DMA((7,))],
        name=f"{name}_allreduce")(*partials)
    shapes = tuple(jax.ShapeDtypeStruct(w.shape, F32) for w in ws)
    outs = pl.pallas_call(step_body, out_shape=shapes * 4, in_specs=[vm] * (1 + 3 * n), out_specs=tuple([vm] * (4 * n)),
                          name=f"{name}_adamw")(total, *ws, *ms, *vs)
    return tuple(outs[j * n:(j + 1) * n] for j in range(4))


def _pad_lanes(v, width=LANES):
    return jnp.pad(v, ((0, 0), (0, width - v.shape[1])))


def kernel(x, ffn1_pre_g, ffn1_w_gate, ffn1_w_up, ffn1_w_down, ffn1_post_g, mix_pre_g, w_in, conv_w, conv_b, dt_bias, a_log, d_skip, ssm_norm_g, w_ssm_proj, attn_sinks, rel_bias_table, w_attn_proj, w_out, mix_post_g, ffn2_pre_g, ffn2_w_gate, ffn2_w_up, ffn2_w_down, ffn2_post_g, loss_target, m_ffn1_pre_g, m_ffn1_w_gate, m_ffn1_w_up, m_ffn1_w_down, m_ffn1_post_g, m_mix_pre_g, m_w_in, m_conv_w, m_conv_b, m_dt_bias, m_a_log, m_d_skip, m_ssm_norm_g, m_w_ssm_proj, m_attn_sinks, m_rel_bias_table, m_w_attn_proj, m_w_out, m_mix_post_g, m_ffn2_pre_g, m_ffn2_w_gate, m_ffn2_w_up, m_ffn2_w_down, m_ffn2_post_g, v_ffn1_pre_g, v_ffn1_w_gate, v_ffn1_w_up, v_ffn1_w_down, v_ffn1_post_g, v_mix_pre_g, v_w_in, v_conv_w, v_conv_b, v_dt_bias, v_a_log, v_d_skip, v_ssm_norm_g, v_w_ssm_proj, v_attn_sinks, v_rel_bias_table, v_w_attn_proj, v_w_out, v_mix_post_g, v_ffn2_pre_g, v_ffn2_w_gate, v_ffn2_w_up, v_ffn2_w_down, v_ffn2_post_g):
    args = dict(locals())
    weight_names = ['ffn1_pre_g', 'ffn1_w_gate', 'ffn1_w_up', 'ffn1_w_down', 'ffn1_post_g', 'mix_pre_g', 'w_in', 'conv_w', 'conv_b',
                    'dt_bias', 'a_log', 'd_skip', 'ssm_norm_g', 'w_ssm_proj', 'attn_sinks', 'rel_bias_table', 'w_attn_proj', 'w_out',
                    'mix_post_g', 'ffn2_pre_g', 'ffn2_w_gate', 'ffn2_w_up', 'ffn2_w_down', 'ffn2_post_g']
    col_sharded = ('ffn1_w_gate', 'ffn1_w_up', 'w_in', 'ffn2_w_gate', 'ffn2_w_up')
    row_sharded = ('ffn1_w_down', 'w_ssm_proj', 'w_attn_proj', 'w_out', 'ffn2_w_down')
    big = col_sharded + row_sharded

    bl, s_len, d = x.shape
    t = bl * s_len
    d_inner = ssm_norm_g.shape[1]
    n_heads = dt_bias.shape[1]
    gn = SSM_GROUPS * SSM_STATE
    conv_dim = d_inner + 2 * gn
    q_dim = ATTN_Q_HEADS * ATTN_HEAD_DIM
    kv_dim = ATTN_KV_HEADS * ATTN_HEAD_DIM

    def local_2d(name, a):
        a = a[0]
        return a.T if name in col_sharded else a

    ffn1_names = ('ffn1_w_gate', 'ffn1_w_up', 'ffn1_w_down')
    ffn2_names = ('ffn2_w_gate', 'ffn2_w_up', 'ffn2_w_down')
    mixer_names = ('w_ssm_proj', 'w_attn_proj', 'w_out')

    def shard(n):
        return local_2d(n, args[n]).astype(BF16)

    def rows(g):
        return g.reshape(N_DEV * g.shape[1], g.shape[2])

    x2 = x.reshape(t, d)
    tgt2 = loss_target.reshape(t, d)
    full = {}

    (h1,), saved1, ffn1_w, got_in = _ffn_forward(
        x2, ffn1_pre_g, lambda got: (rows(got[0]), rows(got[1])), lambda got: rows(got[2]), ffn1_post_g, "ffn1",
        side_norm=_gather_exchange([shard(n) for n in ffn1_names[:2]]),
        side_up=_gather_exchange([shard('w_in'), conv_w[0], shard('ffn1_w_down')]))
    full.update(zip(ffn1_names, ffn1_w))
    conv_w_full = jnp.transpose(got_in[1], (1, 0, 2)).reshape(SSM_CONV, conv_dim)

    win_t = rows(got_in[0])
    dt_lo = 2 * d + d_inner + conv_dim
    off = {'gs': 0, 'ga': d, 'z': 2 * d, 'xbc': 2 * d + d_inner, 'dt': dt_lo, 'q': dt_lo + n_heads, 'kv': dt_lo + n_heads + q_dim}
    assert all(o_ % 16 == 0 for o_ in off.values())

    u, (gs, ga, z, xbc, q, kv, dt_raw) = _proj_all(
        h1, mix_pre_g, win_t,
        [(off['gs'], d, BF16), (off['ga'], d, BF16), (off['z'], d_inner, BF16), (off['xbc'], conv_dim, BF16),
         (off['q'], q_dim, BF16), (off['kv'], 2 * kv_dim, BF16), (off['dt'], n_heads, F32)], name="mix_proj")

    dtb_p, alog_p, dsk_p, sinks_p = _pad_lanes(dt_bias), _pad_lanes(a_log), _pad_lanes(d_skip), _pad_lanes(attn_sinks)
    xc = _conv_fwd(xbc, conv_w_full, conv_b, bl, name="conv_fwd")
    late_names = mixer_names + ffn2_names
    (y, hprev), got_late = _ssd_fwd(xc, dt_raw, dtb_p, alog_p, dsk_p, bl, n_heads,
                                    side=_gather_exchange([shard(n) for n in late_names]), name="ssd_fwd")
    full.update({n: rows(g) for n, g in zip(late_names, got_late)})

    onehot = _bucket_onehot()
    rep = ATTN_Q_HEADS // ATTN_KV_HEADS
    bias = _small_mm_hi(rel_bias_table.T, onehot, NN, name="rel_bias")
    bias_t = jnp.transpose(bias.reshape(ATTN_KV_HEADS, rep, CHUNK, 2 * CHUNK), (0, 3, 1, 2)).reshape(ATTN_KV_HEADS, 2 * CHUNK, rep * CHUNK)
    o, lse = _attn_fwd(q, kv, bias_t, sinks_p, bl, name="attn_fwd")

    yn, ys, ya, merged, mix, h2, n2 = _merge_fwd(y, z, ssm_norm_g, o, gs, ga, full['w_ssm_proj'], full['w_attn_proj'], full['w_out'],
                                                 h1, mix_post_g, ffn2_pre_g, name="merge_fwd")

    (dh3, loss_vec, df2, dg_post2), saved2, _, _ = _ffn_forward(
        h2, ffn2_pre_g, (full['ffn2_w_gate'], full['ffn2_w_up']), full['ffn2_w_down'], ffn2_post_g, "ffn2", target=tgt2, n=n2)
    loss = lax.psum(loss_vec[0, 0], ("x", "y", "c"))

    grads, own, wire, received = {}, {}, {}, {}
    dh2, grads['ffn2_pre_g'], grads['ffn2_post_g'], g32, g16, _ = _ffn_backward(
        dh3, saved2, ffn2_pre_g, full['ffn2_w_gate'], full['ffn2_w_up'], full['ffn2_w_down'], ffn2_post_g, "ffn2",
        post=(df2, dg_post2))
    own.update(zip(ffn2_names, map(_stack8, g32)))
    wire.update(zip(ffn2_names, map(_stack8, g16)))

    dmix, dys, dya, dgs, dga, dy, dz, do, grads['mix_post_g'], grads['ssm_norm_g'] = _merge_bwd(
        dh2, mix, mix_post_g, gs, ga, ys, ya, y, z, ssm_norm_g, full['w_ssm_proj'], full['w_attn_proj'], full['w_out'],
        name="merge_bwd")
    for n, (lhs, rhs) in zip(mixer_names, ((yn, dys), (o, dya), (merged, dmix))):
        g32_, g16_, _ = _mm_tn([lhs], rhs, name=f"d{n}")
        own[n], wire[n] = _stack8(g32_), _stack8(g16_)

    dq, dkv, dbias_t, dsinks = _attn_bwd(do, q, kv, lse, bias_t, sinks_p, bl, name="attn_bwd")
    dbias = jnp.transpose(dbias_t.reshape(ATTN_KV_HEADS, 2 * CHUNK, rep, CHUNK), (0, 2, 3, 1)).reshape(ATTN_Q_HEADS, -1)
    d_table = _small_mm_hi(onehot, dbias, NT, name="rel_bias_bwd")


    first_group = ffn2_names + mixer_names
    (dxc, ddt_raw, ddtb, dalog, ddsk), got = _ssd_bwd(dy, y, xc, dt_raw, hprev, dtb_p, alog_p, dsk_p, bl, n_heads,
                                                      side=_scatter_exchange([wire[n] for n in first_group]), name="ssd_bwd")
    received.update(zip(first_group, got))
    dxbc, dconv_w8, grads['conv_b'] = _conv_bwd(dxc, xbc, conv_w_full, conv_b, bl, name="conv_bwd")

    wide32, wide16, _ = _mm_tn([dgs, dga, dz, dxbc, dq], u, name="dw_in")
    kv32, kv16, _ = _mm_tn([dkv], u, name="dw_in_kv")
    dt32, dt16, _ = _mm_tn([ddt_raw], u, name="dw_in_dt")

    def original_order(wide, kv_part, dt_part):
        return jnp.concatenate([wide[:dt_lo], dt_part[:n_heads], wide[dt_lo:], kv_part], axis=0)
    me = 4 * lax.axis_index("x") + 2 * lax.axis_index("y") + lax.axis_index("c")
    blk_rows = win_t.shape[0] // N_DEV
    wire['w_in'] = _stack8(original_order(wide16, kv16, dt16))
    own_w_in = lax.dynamic_slice_in_dim(original_order(wide32, kv32, dt32), me * blk_rows, blk_rows)
    own['conv_w'] = jnp.transpose(dconv_w8[:SSM_CONV].reshape(SSM_CONV, N_DEV, conv_dim // N_DEV), (1, 0, 2))

    segs = [(g_, 0, off[k_]) for g_, k_ in zip([dgs, dga, dz, dxbc, dq, dkv, ddt_raw], ('gs', 'ga', 'z', 'xbc', 'q', 'kv', 'dt'))]
    dh1, grads['mix_pre_g'], got = _mm_nn_rmsbwd(segs, [win_t], h1, mix_pre_g, dh2,
                                                 side=_scatter_exchange([wire['w_in'], own['conv_w']]), name="mix_du")
    received.update(zip(('w_in', 'conv_w'), got))

    dx2, grads['ffn1_pre_g'], grads['ffn1_post_g'], g32, _, got = _ffn_backward(
        dh1, saved1, ffn1_pre_g, full['ffn1_w_gate'], full['ffn1_w_up'], full['ffn1_w_down'], ffn1_post_g, "ffn1", chain=True)
    own.update(zip(ffn1_names, map(_stack8, g32)))
    received.update(zip(ffn1_names, got))

    def own_block(a):
        return lax.dynamic_index_in_dim(a, me, 0, keepdims=False)
    out_g, out_d, out_m, out_v = {}, {}, {}, {}
    for n in big:
        w2, m2, v2 = local_2d(n, args[n]), local_2d(n, args['m_' + n]), local_2d(n, args['v_' + n])
        results = _reduce_adamw(own_w_in if n == 'w_in' else own_block(own[n]), received[n], w2, m2, v2, name=f"update_{n}")
        out_g[n], out_d[n], out_m[n], out_v[n] = ((a.T if n in col_sharded else a)[None] for a in results)
    results = _reduce_adamw(own_block(own['conv_w']), received['conv_w'], conv_w[0], m_conv_w[0], v_conv_w[0], name="update_conv_w")
    out_g['conv_w'], out_d['conv_w'], out_m['conv_w'], out_v['conv_w'] = (a[None] for a in results)

    grads['dt_bias'], grads['a_log'], grads['d_skip'], grads['attn_sinks'], grads['rel_bias_table'] = ddtb, dalog, ddsk, dsinks, d_table
    small = [n for n in weight_names if n not in big and n != 'conv_w']
    results = _update_replicated([grads[n] for n in small], [args[n] for n in small], [args['m_' + n] for n in small],
                                 [args['v_' + n] for n in small], name="update_replicated")
    for dst, vals in zip((out_g, out_d, out_m, out_v), results):
        dst.update(zip(small, vals))

    grad_x = dx2.reshape(bl, s_len, d)
    return (loss, grad_x, *[out_g[n] for n in weight_names], *[out_d[n] for n in weight_names],
            *[out_m[n] for n in weight_names], *[out_v[n] for n in weight_names])
```

```python
import functools
import math

import numpy as np
import jax
import jax.numpy as jnp
from jax import lax
from jax.experimental import pallas as pl
from jax.experimental.pallas import tpu as pltpu

F32 = jnp.float32
BF16 = jnp.bfloat16
MESH = pl.DeviceIdType.MESH
N_DEV = 8

SSM_HEAD_DIM = 64
SSM_GROUPS = 4
SSM_STATE = 128
SSM_CONV = 4
CHUNK = 128
ATTN_HEAD_DIM = 64
ATTN_Q_HEADS = 16
ATTN_KV_HEADS = 4
REL_BUCKETS = 32
REL_MAX_DISTANCE = 128
RMS_EPS = 1e-6
FFN_RESIDUAL_WEIGHT = 0.5
ADAM_LR, ADAM_B1, ADAM_B2, ADAM_EPS, ADAM_WD, ADAM_STEP = 0.001, 0.9, 0.999, 1e-08, 0.01, 10

LANES = 128
VMEM_LIMIT_BYTES = 56 * 1024 * 1024
FFN_COL_TILE = 1408
SSD_HEAD_BATCH = 2

NEG_INF = float("-inf")


def _params(*sem):
    return pltpu.CompilerParams(dimension_semantics=sem, vmem_limit_bytes=VMEM_LIMIT_BYTES)


def _tile(n, pref, mult=8):
    if n <= pref:
        return n
    t = (pref // mult) * mult
    while t >= mult:
        if n % t == 0:
            return t
        t -= mult
    return n


def _sigmoid(x):
    return 1.0 / (1.0 + jnp.exp(-x))


def _dot(a, b, dims):
    return lax.dot_general(a, b, (dims, ((), ())), preferred_element_type=F32)


NN = ((1,), (0,))
NT = ((1,), (1,))
TN = ((0,), (0,))


def _dot_hi(a, b, dims=NN):
    return lax.dot_general(a, b, (dims, ((), ())), preferred_element_type=F32, precision=lax.Precision.HIGHEST)


def _const_spec(shape):
    nd = len(shape)
    return pl.BlockSpec(shape, lambda *_: (0,) * nd)


def _resident_spec(shape):
    nd = len(shape)
    return pl.BlockSpec(shape, lambda *_: (0,) * nd, pipeline_mode=pl.Buffered(1))


class _Exchange:
    def __init__(self, arrays, out_shape, scratch, start, finish):
        self.arrays, self.out_shape, self.scratch, self.start, self.finish = arrays, out_shape, scratch, start, finish


def _hosted_call(body, *, grid, in_specs, out_specs, out_shape, scratch_shapes, operands, side, name):
    in_specs, out_specs, out_shape, scratch_shapes = list(in_specs), list(out_specs), list(out_shape), list(scratch_shapes)
    sem = ("arbitrary",) * len(grid)
    if side is None:
        outs = pl.pallas_call(body, out_shape=tuple(out_shape), grid=grid, in_specs=in_specs, out_specs=tuple(out_specs),
                              scratch_shapes=scratch_shapes, name=name, compiler_params=_params(*sem))(*operands)
        return tuple(outs), ()
    n_in, n_out, n_scr = len(in_specs), len(out_shape), len(scratch_shapes)
    s_in, s_out = len(side.arrays), len(side.out_shape)

    def wrapped(*refs):
        refs = list(refs)
        main_in, side_in = refs[:n_in], refs[n_in:n_in + s_in]
        o0 = n_in + s_in
        main_out, side_out = refs[o0:o0 + n_out], refs[o0 + n_out:o0 + n_out + s_out]
        c0 = o0 + n_out + s_out
        main_scr, side_scr = refs[c0:c0 + n_scr], refs[c0 + n_scr:]
        ids = [pl.program_id(ax) for ax in range(len(grid))]
        first = functools.reduce(jnp.logical_and, [i == 0 for i in ids])
        last = functools.reduce(jnp.logical_and, [i == g - 1 for i, g in zip(ids, grid)])

        @pl.when(first)
        def _():
            side.start(side_in, side_out, side_scr)

        body(*main_in, *main_out, *main_scr)

        @pl.when(last)
        def _():
            side.finish(side_in, side_out, side_scr)

    hbm = pl.BlockSpec(memory_space=pl.ANY)
    outs = pl.pallas_call(
        wrapped, out_shape=tuple(out_shape + list(side.out_shape)), grid=grid,
        in_specs=in_specs + [hbm] * s_in, out_specs=tuple(out_specs + [hbm] * s_out),
        scratch_shapes=scratch_shapes + list(side.scratch), name=name, compiler_params=_params(*sem))(*operands, *side.arrays)
    return tuple(outs[:n_out]), tuple(outs[n_out:])


def _proj_all(h, g, w, segs, *, name):
    t, d = h.shape
    tm = _tile(t, 512)
    segs = [(row0, wd_, max(wd_, LANES), dt_) for row0, wd_, dt_ in segs]
    assert all(row0 + out_w <= w.shape[0] for row0, _, out_w, _ in segs)

    def body(h_ref, g_ref, w_ref, u_ref, *o_refs):
        hv = h_ref[...]
        r = lax.rsqrt(jnp.mean(hv * hv, axis=-1, keepdims=True) + RMS_EPS)
        uv = (hv * r * g_ref[...]).astype(BF16)
        u_ref[...] = uv
        for (row0, width, out_w, _), o_ref in zip(segs, o_refs):
            for c0, c1 in _col_chunks(out_w, 8 * LANES):
                part = _dot(uv, w_ref[row0 + c0:row0 + c1, :], NT)
                if width < out_w:
                    part = jnp.where(lax.broadcasted_iota(jnp.int32, part.shape, 1) < width, part, 0.0)
                o_ref[:, c0:c1] = part.astype(o_ref.dtype)

    row = pl.BlockSpec((tm, d), lambda i: (i, 0))
    outs = pl.pallas_call(
        body, out_shape=(jax.ShapeDtypeStruct((t, d), BF16),) + tuple(jax.ShapeDtypeStruct((t, ow), dt_) for _, _, ow, dt_ in segs),
        grid=(t // tm,), in_specs=[row, _const_spec((1, d)), _resident_spec(w.shape)],
        out_specs=(row,) + tuple(pl.BlockSpec((tm, ow), lambda i: (i, 0)) for _, _, ow, _ in segs),
        name=name, compiler_params=_params("parallel"))(h, g, w)
    return outs[0], outs[1:]


def _mm_tn(a_list, b, *, tm=1408, tk=2048, side=None, name):
    t, n = b.shape
    tk = _tile(t, tk if len(a_list) == 1 else tk // 2)
    nk = t // tk
    widths = [a.shape[1] for a in a_list]
    tm = _tile(math.gcd(*widths), tm, LANES)
    assert all(w % tm == 0 for w in widths)
    starts = np.cumsum([0] + [w // tm for w in widths])
    nseg = len(a_list)

    def a_spec(s):
        lo, hi = int(starts[s]), int(starts[s + 1])

        def idx(i, k):
            active = jnp.logical_and(i >= lo, i < hi)
            return (jnp.where(active, k, 0), jnp.clip(i - lo, 0, hi - lo - 1))
        return pl.BlockSpec((tk, tm), idx)

    def body(*refs):
        a_refs, b_ref, o_ref, o16_ref, acc = refs[:nseg], refs[nseg], refs[nseg + 1], refs[nseg + 2], refs[nseg + 3]
        i, k = pl.program_id(0), pl.program_id(1)

        @pl.when(k == 0)
        def _():
            acc[...] = jnp.zeros_like(acc)

        bv = b_ref[...].astype(BF16)
        for s in range(nseg):
            lo, hi = int(starts[s]), int(starts[s + 1])

            @pl.when(jnp.logical_and(i >= lo, i < hi))
            def _(s=s):
                acc[...] += _dot(a_refs[s][...].astype(BF16), bv, TN)

        @pl.when(k == nk - 1)
        def _():
            o_ref[...] = acc[...]
            o16_ref[...] = acc[...].astype(BF16)

    rows = int(starts[-1]) * tm
    o_spec = pl.BlockSpec((tm, n), lambda i, k: (i, 0))
    (o32, o16), got = _hosted_call(
        body, out_shape=[jax.ShapeDtypeStruct((rows, n), F32), jax.ShapeDtypeStruct((rows, n), BF16)], grid=(int(starts[-1]), nk),
        in_specs=[a_spec(s) for s in range(nseg)] + [pl.BlockSpec((tk, n), lambda i, k: (k, 0))],
        out_specs=[o_spec, o_spec], scratch_shapes=[pltpu.VMEM((tm, n), F32)], operands=list(a_list) + [b], side=side, name=name)
    return o32, o16, got


def _mm_nn_rmsbwd(segs, weights, x, g, dres, *, below=None, tm=512, side=None, name):
    t, d = x.shape
    tm = _tile(t, tm)
    nseg, nw = len(segs), len(weights)
    n_below = 0 if below is None else 2

    def body(*refs):
        a_refs, w_refs = refs[:nseg], refs[nseg:nseg + nw]
        x_ref, g_ref, dres_ref = refs[nseg + nw:nseg + nw + 3]
        below_refs = refs[nseg + nw + 3:nseg + nw + 3 + n_below]
        dx_ref, dg_ref = refs[nseg + nw + 3 + n_below:nseg + nw + 5 + n_below]

        @pl.when(pl.program_id(0) == 0)
        def _():
            dg_ref[...] = jnp.zeros_like(dg_ref)

        dn = None
        for s, (a, w_idx, row0) in enumerate(segs):
            part = _dot(a_refs[s][...].astype(BF16), w_refs[w_idx][row0:row0 + a.shape[1], :], NN)
            dn = part if dn is None else dn + part
        xv = x_ref[...]
        r = lax.rsqrt(jnp.mean(xv * xv, axis=-1, keepdims=True) + RMS_EPS)
        xhat = xv * r
        dyg = dn * g_ref[...]
        dx = dres_ref[...] + r * (dyg - xhat * jnp.mean(dyg * xhat, axis=-1, keepdims=True))
        dx_ref[...] = dx
        dg_ref[...] += jnp.sum(dn * xhat, axis=0, keepdims=True)
        if below is not None:
            f_ref, gp_ref = below_refs
            df_ref, dgp_ref = refs[nseg + nw + 5 + n_below:]

            @pl.when(pl.program_id(0) == 0)
            def _():
                dgp_ref[...] = jnp.zeros_like(dgp_ref)
            fv = f_ref[...]
            rf = lax.rsqrt(jnp.mean(fv * fv, axis=-1, keepdims=True) + RMS_EPS)
            fhat = fv * rf
            dy = FFN_RESIDUAL_WEIGHT * dx
            dfg = dy * gp_ref[...]
            df_ref[...] = (rf * (dfg - fhat * jnp.mean(dfg * fhat, axis=-1, keepdims=True))).astype(BF16)
            dgp_ref[...] += jnp.sum(dy * fhat, axis=0, keepdims=True)

    row = pl.BlockSpec((tm, d), lambda i: (i, 0))
    vec = _const_spec((1, d))
    in_specs = [pl.BlockSpec((tm, a.shape[1]), lambda i: (i, 0)) for a, _, _ in segs]
    in_specs += [_resident_spec(w.shape) for w in weights] + [row, vec, row] + ([row, vec] if below is not None else [])
    out_specs = [row, vec] + ([row, vec] if below is not None else [])
    out_shape = [jax.ShapeDtypeStruct((t, d), F32), jax.ShapeDtypeStruct((1, d), F32)]
    if below is not None:
        out_shape += [jax.ShapeDtypeStruct((t, d), BF16), jax.ShapeDtypeStruct((1, d), F32)]
    outs, extra = _hosted_call(
        body, grid=(t // tm,), in_specs=in_specs, out_specs=out_specs, out_shape=out_shape, scratch_shapes=[],
        operands=[a for a, _, _ in segs] + list(weights) + [x, g, dres] + (list(below) if below is not None else []),
        side=side, name=name)
    return outs[0], outs[1], (tuple(outs[2:]) if below is not None else None), extra


def _col_chunks(width, chunk=4 * LANES):
    return [(c0, min(c0 + chunk, width)) for c0 in range(0, width, chunk)]


def _rms_fwd(x, g, *, side=None, name):
    t, d = x.shape
    tm = _tile(t, 512)

    def body(x_ref, g_ref, o_ref):
        xv = x_ref[...]
        r = lax.rsqrt(jnp.mean(xv * xv, axis=-1, keepdims=True) + RMS_EPS)
        o_ref[...] = (xv * r * g_ref[...]).astype(o_ref.dtype)

    row = pl.BlockSpec((tm, d), lambda i: (i, 0))
    (n,), got = _hosted_call(body, grid=(t // tm,), in_specs=[row, _const_spec((1, d))], out_specs=[row],
                             out_shape=[jax.ShapeDtypeStruct((t, d), BF16)], scratch_shapes=[], operands=[x, g], side=side, name=name)
    return n, got


def _ffn_up(n, wgt, wut, *, side=None, name):
    t, d = n.shape
    f = wgt.shape[0]
    tm, tn = _tile(t, 512), _tile(f, FFN_COL_TILE, LANES)

    def body(n_ref, wg_ref, wu_ref, g_ref, u_ref, h_ref):
        nv = n_ref[...]
        gv = _dot(nv, wg_ref[...], NT)
        uv = _dot(nv, wu_ref[...], NT)
        g_ref[...] = gv.astype(BF16)
        u_ref[...] = uv.astype(BF16)
        h_ref[...] = (gv * _sigmoid(gv) * uv).astype(BF16)

    w_spec = pl.BlockSpec((tn, d), lambda j, i: (j, 0))
    o_spec = pl.BlockSpec((tm, tn), lambda j, i: (i, j))
    shp = jax.ShapeDtypeStruct((t, f), BF16)
    return _hosted_call(body, grid=(f // tn, t // tm), in_specs=[pl.BlockSpec((tm, d), lambda j, i: (i, 0)), w_spec, w_spec],
                        out_specs=[o_spec, o_spec, o_spec], out_shape=[shp, shp, shp], scratch_shapes=[], operands=[n, wgt, wut],
                        side=side, name=name)


def _rms_residual(acc, h, gp, weight):
    r = lax.rsqrt(jnp.mean(acc * acc, axis=-1, keepdims=True) + RMS_EPS)
    return h + weight * (acc * r * gp)


def _ffn_down(hid, wd, h_in, gp, *, target=None, side=None, name):
    t, f = hid.shape
    d = wd.shape[1]
    tm = _tile(t, 256)
    row = pl.BlockSpec((tm, d), lambda i: (i, 0))
    shp = jax.ShapeDtypeStruct((t, d), F32)
    in_specs = [pl.BlockSpec((tm, f), lambda i: (i, 0)), _resident_spec((f, d)), row, _const_spec((1, d))]

    if target is None:
        def body(hid_ref, wd_ref, hin_ref, gp_ref, f_ref, hout_ref):
            acc = _dot(hid_ref[...], wd_ref[...], NN)
            f_ref[...] = acc
            hout_ref[...] = _rms_residual(acc, hin_ref[...], gp_ref[...], FFN_RESIDUAL_WEIGHT)

        return _hosted_call(body, grid=(t // tm,), in_specs=in_specs, out_specs=[row, row], out_shape=[shp, shp], scratch_shapes=[],
                            operands=[hid, wd, h_in, gp], side=side, name=name)

    def body_loss(hid_ref, wd_ref, hin_ref, gp_ref, tgt_ref, f_ref, dh_ref, loss_ref, df_ref, dgp_ref):
        @pl.when(pl.program_id(0) == 0)
        def _():
            loss_ref[...] = jnp.zeros_like(loss_ref)
            dgp_ref[...] = jnp.zeros_like(dgp_ref)
        acc = _dot(hid_ref[...], wd_ref[...], NN)
        f_ref[...] = acc
        r = lax.rsqrt(jnp.mean(acc * acc, axis=-1, keepdims=True) + RMS_EPS)
        fhat = acc * r
        e = hin_ref[...] + FFN_RESIDUAL_WEIGHT * (fhat * gp_ref[...]) - tgt_ref[...]
        dh = e * (1.0 / d)
        dh_ref[...] = dh
        per_row = jnp.sum(e * e, axis=1, keepdims=True) * (1.0 / d)
        loss_ref[...] += 0.5 * jnp.sum(per_row, axis=0, keepdims=True)
        dy = FFN_RESIDUAL_WEIGHT * dh
        dyg = dy * gp_ref[...]
        df_ref[...] = (r * (dyg - fhat * jnp.mean(dyg * fhat, axis=-1, keepdims=True))).astype(BF16)
        dgp_ref[...] += jnp.sum(dy * fhat, axis=0, keepdims=True)

    return _hosted_call(body_loss, grid=(t // tm,), in_specs=in_specs + [row],
                        out_specs=[row, row, _const_spec((1, LANES)), row, _const_spec((1, d))],
                        out_shape=[shp, shp, jax.ShapeDtypeStruct((1, LANES), F32), jax.ShapeDtypeStruct((t, d), BF16),
                                   jax.ShapeDtypeStruct((1, d), F32)],
                        scratch_shapes=[], operands=[hid, wd, h_in, gp, target], side=side, name=name)


def _ffn_dhid(df, wd, g, u, *, name):
    t, d = df.shape
    f = wd.shape[0]
    tm, tn = _tile(t, 512), _tile(f, FFN_COL_TILE, LANES)

    def body(df_ref, wd_ref, g_ref, u_ref, dg_ref, du_ref):
        dh = _dot(df_ref[...], wd_ref[...], NT)
        gv = g_ref[...].astype(F32)
        uv = u_ref[...].astype(F32)
        sg = _sigmoid(gv)
        silu = gv * sg
        dg_ref[...] = (dh * uv * (sg + silu * (1.0 - sg))).astype(BF16)
        du_ref[...] = (dh * silu).astype(BF16)

    o_spec = pl.BlockSpec((tm, tn), lambda j, i: (i, j))
    shp = jax.ShapeDtypeStruct((t, f), BF16)
    return pl.pallas_call(body, out_shape=(shp, shp), grid=(f // tn, t // tm),
                          in_specs=[pl.BlockSpec((tm, d), lambda j, i: (i, 0)), pl.BlockSpec((tn, d), lambda j, i: (j, 0)), o_spec, o_spec],
                          out_specs=(o_spec, o_spec), name=name, compiler_params=_params("parallel", "arbitrary"))(df, wd, g, u)


def _ffn_forward(h_in, g_pre, w_up, wd, g_post, tag, side_norm=None, side_up=None, target=None, n=None):
    got_norm = ()
    if n is None:
        n, got_norm = _rms_fwd(h_in, g_pre, side=side_norm, name=f"{tag}_prenorm")
    wgt, wut = w_up(got_norm) if callable(w_up) else w_up
    (g, u, hid), got_up = _ffn_up(n, wgt, wut, side=side_up, name=f"{tag}_up")
    wd = wd(got_up) if callable(wd) else wd
    outs, _ = _ffn_down(hid, wd, h_in, g_post, target=target, name=f"{tag}_down")
    return outs[1:], (h_in, n, g, u, hid, outs[0]), (wgt, wut, wd), got_up


def _stack8(g):
    return g.reshape(N_DEV, g.shape[0] // N_DEV, g.shape[1])


def _ffn_backward(dh_out, saved, g_pre, wgt, wut, wd, tag, post, chain=False):
    h_in, n, g, u, hid, _ = saved

    def side_of(grad16):
        return _scatter_exchange([_stack8(grad16)]) if chain else None

    df, dg_post = post
    dgate, dup = _ffn_dhid(df, wd, g, u, name=f"{tag}_dhid")
    d_wd, d_wd16, _ = _mm_tn([hid], df, name=f"{tag}_dwd")
    d_wgt, d_wgt16, got_wd = _mm_tn([dgate], n, side=side_of(d_wd16), name=f"{tag}_dwg")
    d_wut, d_wut16, got_wg = _mm_tn([dup], n, side=side_of(d_wgt16), name=f"{tag}_dwu")
    dh_in, dg_pre, _, got_wu = _mm_nn_rmsbwd([(dgate, 0, 0), (dup, 1, 0)], [wgt, wut], h_in, g_pre, dh_out, side=side_of(d_wut16),
                                             name=f"{tag}_dn")
    received = (got_wg[0], got_wu[0], got_wd[0]) if chain else None
    return dh_in, dg_pre, dg_post, (d_wgt, d_wut, d_wd), (d_wgt16, d_wut16, d_wd16), received


CONV_ROWS = 128
HALO = 8


def _taps(w_ref):
    return [w_ref[k:k + 1, :] for k in range(SSM_CONV)]


def _conv_chunk(x_ref, xs, r0, taps, bias):
    xs[HALO + r0:HALO + r0 + CONV_ROWS, :] = x_ref[r0:r0 + CONV_ROWS, :].astype(F32)
    shifted = [xs[HALO + r0 - k:HALO + r0 - k + CONV_ROWS, :] for k in range(SSM_CONV)]
    pre = bias + shifted[0] * taps[SSM_CONV - 1]
    for k in range(1, SSM_CONV):
        pre = pre + shifted[k] * taps[SSM_CONV - 1 - k]
    return shifted, pre


def _fold_rows(a):
    return functools.reduce(jnp.add, [a[i:i + 8] for i in range(0, a.shape[0], 8)])


def _conv_fwd(xbc, conv_w, conv_b, bl, *, name):
    t, c = xbc.shape
    s = t // bl
    tc = LANES
    assert s % CONV_ROWS == 0

    def body(x_ref, w_ref, b_ref, o_ref, xs):
        taps, bias = _taps(w_ref), b_ref[...]
        xs[0:HALO, :] = jnp.zeros((HALO, tc), F32)
        for r0 in range(0, s, CONV_ROWS):
            _, pre = _conv_chunk(x_ref, xs, r0, taps, bias)
            o_ref[r0:r0 + CONV_ROWS, :] = (pre * _sigmoid(pre)).astype(o_ref.dtype)

    blk = pl.BlockSpec((s, tc), lambda b, j: (b, j))
    return pl.pallas_call(body, out_shape=jax.ShapeDtypeStruct((t, c), BF16), grid=(bl, c // tc),
                          in_specs=[blk, pl.BlockSpec((SSM_CONV, tc), lambda b, j: (0, j)), pl.BlockSpec((1, tc), lambda b, j: (0, j))],
                          out_specs=blk, scratch_shapes=[pltpu.VMEM((HALO + s, tc), F32)],
                          name=name, compiler_params=_params("parallel", "arbitrary"))(xbc, conv_w, conv_b)


def _conv_bwd(dxc, xbc, conv_w, conv_b, bl, *, name):
    t, c = xbc.shape
    s = t // bl
    tc = LANES

    def body(dy_ref, x_ref, w_ref, b_ref, dx_ref, dw_ref, db_ref, xs, dpre_s):
        @pl.when(pl.program_id(1) == 0)
        def _():
            dw_ref[...] = jnp.zeros_like(dw_ref)
            db_ref[...] = jnp.zeros_like(db_ref)

        taps, bias = _taps(w_ref), b_ref[...]
        zero8 = jnp.zeros((HALO, tc), F32)
        xs[0:HALO, :] = zero8
        dpre_s[s:s + HALO, :] = zero8
        sums = [zero8] * (SSM_CONV + 1)
        for r0 in range(0, s, CONV_ROWS):
            shifted, pre = _conv_chunk(x_ref, xs, r0, taps, bias)
            sg = _sigmoid(pre)
            dpre = dy_ref[r0:r0 + CONV_ROWS, :].astype(F32) * (sg * (1.0 + pre * (1.0 - sg)))
            dpre_s[r0:r0 + CONV_ROWS, :] = dpre
            sums = [acc + _fold_rows(dpre * sh) for acc, sh in zip(sums[:-1], shifted)] + [sums[-1] + _fold_rows(dpre)]
        for k in range(SSM_CONV):
            dw_ref[SSM_CONV - 1 - k:SSM_CONV - k, :] += jnp.sum(sums[k], axis=0, keepdims=True)
        db_ref[...] += jnp.sum(sums[-1], axis=0, keepdims=True)
        for r0 in range(0, s, CONV_ROWS):
            dx = dpre_s[r0:r0 + CONV_ROWS, :] * taps[SSM_CONV - 1]
            for k in range(1, SSM_CONV):
                dx = dx + dpre_s[r0 + k:r0 + k + CONV_ROWS, :] * taps[SSM_CONV - 1 - k]
            dx_ref[r0:r0 + CONV_ROWS, :] = dx.astype(dx_ref.dtype)

    blk = pl.BlockSpec((s, tc), lambda j, b: (b, j))
    return pl.pallas_call(
        body, out_shape=(jax.ShapeDtypeStruct((t, c), BF16), jax.ShapeDtypeStruct((8, c), F32), jax.ShapeDtypeStruct((1, c), F32)),
        grid=(c // tc, bl),
        in_specs=[blk, blk, pl.BlockSpec((SSM_CONV, tc), lambda j, b: (0, j)), pl.BlockSpec((1, tc), lambda j, b: (0, j))],
        out_specs=(blk, pl.BlockSpec((8, tc), lambda j, b: (0, j)), pl.BlockSpec((1, tc), lambda j, b: (0, j))),
        scratch_shapes=[pltpu.VMEM((HALO + s, tc), F32), pltpu.VMEM((s + HALO, tc), F32)],
        name=name, compiler_params=_params("parallel", "arbitrary"))(dxc, xbc, conv_w, conv_b)


def _softplus(x):
    return jnp.maximum(x, 0.0) + jnp.log1p(jnp.exp(-jnp.abs(x)))


def _hilo_dot(v, m_b, dims=NN):
    hi = v.astype(BF16)
    lo = (v - hi.astype(F32)).astype(BF16)
    return _dot(hi, m_b, dims) + _dot(lo, m_b, dims)


def _ssd_chunk_common(dtraw_ref, dtb_ref, alog_ref, dsk_ref, d_inner):
    q, p = CHUNK, SSM_HEAD_DIM
    shift = p.bit_length() - 1
    assert 1 << shift == p
    dt = _softplus(dtraw_ref[...] + dtb_ref[...])
    a = -jnp.exp(alog_ref[...])
    ii = lax.broadcasted_iota(jnp.int32, (q, q), 0)
    jj = lax.broadcasted_iota(jnp.int32, (q, q), 1)
    causal = ii >= jj
    tril = jnp.where(causal, 1.0, 0.0).astype(F32)
    triu = jnp.where(ii <= jj, 1.0, 0.0).astype(F32)
    a_cs = _dot_hi(tril, dt * a)
    a_cs_t = a_cs.T
    a_last = a_cs[q - 1:q, :]
    e_col = jnp.exp(a_cs)
    dec_end = jnp.exp(a_last - a_cs)
    head_of_col = lax.shift_right_logical(lax.broadcasted_iota(jnp.int32, (LANES, d_inner), 1), shift)
    spread = (lax.broadcasted_iota(jnp.int32, (LANES, d_inner), 0) == head_of_col).astype(BF16)
    exact = jnp.concatenate([dt, jnp.broadcast_to(dsk_ref[...], (8, LANES))], axis=0)
    hi = jnp.concatenate([exact, e_col, dec_end], axis=0).astype(BF16)
    lo = (exact - hi[:q + 8].astype(F32)).astype(BF16)
    wide = _dot(hi, spread, NN)
    fine = wide[:q + 8] + _dot(lo, spread, NN)
    return dict(dt=dt, a=a, a_cs=a_cs, a_cs_t=a_cs_t, a_last=a_last, dec_end=dec_end, causal=causal, triu=triu,
                dt_e=fine[:q], dsk_e=fine[q:q + 1], e_e=wide[q + 8:2 * q + 8], dec_e=wide[2 * q + 8:3 * q + 8])


def _fill_block_diag(bd_ref, src_ref, hpg, col0=0):
    q, p = CHUNK, SSM_HEAD_DIM
    for hh in range(hpg):
        bd_ref[hh * q:(hh + 1) * q, hh * p:(hh + 1) * p] = src_ref[:, col0 + hh * p:col0 + (hh + 1) * p]


def _lane_onehot(h):
    return (lax.broadcasted_iota(jnp.int32, (1, LANES), 1) == h).astype(F32)


def _ssd_fwd(xc, dt_raw, dt_bias, a_log, d_skip, bl, n_heads, *, side=None, name):
    t = xc.shape[0]
    q, p, nst, grp = CHUNK, SSM_HEAD_DIM, SSM_STATE, SSM_GROUPS
    d_inner = n_heads * p
    hpg = n_heads // grp
    hb = min(hpg, SSD_HEAD_BATCH)
    gw = hpg * p
    nc = t // bl // q
    assert d_inner % (grp * nst) == 0 and nst == LANES and hpg % hb == 0

    def body(xs_ref, b_ref, c_ref, dtraw_ref, dtb_ref, alog_ref, dsk_ref, y_ref, hprev_ref, state, m_all, x_bd, xdt_s):
        @pl.when(jnp.logical_and(pl.program_id(0) == 0, pl.program_id(1) == 0))
        def _():
            x_bd[...] = jnp.zeros_like(x_bd)

        @pl.when(pl.program_id(1) == 0)
        def _():
            state[...] = jnp.zeros_like(state)

        cm = _ssd_chunk_common(dtraw_ref, dtb_ref, alog_ref, dsk_ref, d_inner)
        for g in range(grp):
            cols = slice(g * gw, (g + 1) * gw)
            bg = b_ref[:, g * nst:(g + 1) * nst]
            cg = c_ref[:, g * nst:(g + 1) * nst]
            scores = _dot(cg, bg, NT)
            xs = xs_ref[:, cols].astype(F32)
            xdt = xs * cm['dt_e'][:, cols]
            xdt_s[...] = xdt.astype(BF16)
            y_parts = []
            for sub in range(hpg // hb):
                for k in range(hb):
                    h = g * hpg + sub * hb + k
                    seg = cm['a_cs'][:, h:h + 1] - cm['a_cs_t'][h:h + 1, :]
                    m_all[:, k * q:(k + 1) * q] = (scores * jnp.exp(jnp.where(cm['causal'], seg, NEG_INF))).astype(BF16)
                _fill_block_diag(x_bd, xdt_s, hb, sub * hb * p)
                y_parts.append(_dot(m_all[...], x_bd[...], NN))
            hprev = state[g]
            hprev_ref[g] = hprev
            y = jnp.concatenate(y_parts, axis=1) + cm['e_e'][:, cols] * _dot(cg, hprev.astype(BF16), NT)
            y_ref[:, cols] = y + cm['dsk_e'][:, cols] * xs
            st = _dot((xdt * cm['dec_e'][:, cols]).astype(BF16), bg, TN)
            for hh in range(hpg):
                h = g * hpg + hh
                rows = slice(hh * p, (hh + 1) * p)
                state[g, rows, :] = jnp.exp(cm['a_last'][:, h:h + 1]) * hprev[rows] + st[rows]

    gn = grp * nst

    def rowmap(b, c):
        return b * nc + c
    vec = pl.BlockSpec((1, LANES), lambda b, c: (0, 0))
    return _hosted_call(
        body,
        out_shape=[jax.ShapeDtypeStruct((t, d_inner), F32), jax.ShapeDtypeStruct((t // q, grp, gw, nst), F32)],
        grid=(bl, nc),
        in_specs=[pl.BlockSpec((q, d_inner), lambda b, c: (rowmap(b, c), 0)),
                  pl.BlockSpec((q, gn), lambda b, c: (rowmap(b, c), d_inner // gn)),
                  pl.BlockSpec((q, gn), lambda b, c: (rowmap(b, c), d_inner // gn + 1)),
                  pl.BlockSpec((q, LANES), lambda b, c: (rowmap(b, c), 0)), vec, vec, vec],
        out_specs=[pl.BlockSpec((q, d_inner), lambda b, c: (rowmap(b, c), 0)),
                   pl.BlockSpec((None, grp, gw, nst), lambda b, c: (rowmap(b, c), 0, 0, 0))],
        scratch_shapes=[pltpu.VMEM((grp, gw, nst), F32), pltpu.VMEM((q, hb * q), BF16), pltpu.VMEM((hb * q, hb * p), BF16),
                        pltpu.VMEM((q, gw), BF16)],
        operands=[xc, xc, xc, dt_raw, dt_bias, a_log, d_skip], side=side, name=name)


def _ssd_bwd(dy, y, xc, dt_raw, hprev_all, dt_bias, a_log, d_skip, bl, n_heads, *, side=None, name):
    t, c_dim = xc.shape
    q, p, nst, grp = CHUNK, SSM_HEAD_DIM, SSM_STATE, SSM_GROUPS
    d_inner = n_heads * p
    hpg = n_heads // grp
    hb = min(hpg, SSD_HEAD_BATCH)
    gw = hpg * p
    nc = t // bl // q
    gn = grp * nst
    shift = p.bit_length() - 1

    def body(dy_ref, y_ref, xs_ref, b_ref, c_ref, dtraw_ref, hprev_ref, dtb_ref, alog_ref, dsk_ref,
             dxc_ref, ddtraw_ref, ddtb_ref, dalog_ref, ddsk_ref, dstate, mt_all, x_bd, dy_bd, xdt_s):
        @pl.when(jnp.logical_and(pl.program_id(0) == 0, pl.program_id(1) == 0))
        def _():
            ddtb_ref[...] = jnp.zeros_like(ddtb_ref)
            dalog_ref[...] = jnp.zeros_like(dalog_ref)
            ddsk_ref[...] = jnp.zeros_like(ddsk_ref)
            x_bd[...] = jnp.zeros_like(x_bd)
            dy_bd[...] = jnp.zeros_like(dy_bd)

        @pl.when(pl.program_id(1) == 0)
        def _():
            dstate[...] = jnp.zeros_like(dstate)

        cm = _ssd_chunk_common(dtraw_ref, dtb_ref, alog_ref, dsk_ref, d_inner)
        causal = cm['causal']
        upper = cm['triu'] > 0.5
        seg_row = lax.shift_right_logical(lax.broadcasted_iota(jnp.int32, (gw, LANES), 0), shift)
        seg_lane = lax.broadcasted_iota(jnp.int32, (gw, LANES), 1)
        sums = jnp.zeros((5 * q, LANES), F32)
        state_dot = jnp.zeros((1, LANES), F32)
        for g in range(grp):
            cols = slice(g * gw, (g + 1) * gw)
            seg_sum = (seg_row + g * hpg == seg_lane).astype(BF16)
            bg = b_ref[:, g * nst:(g + 1) * nst]
            cg = c_ref[:, g * nst:(g + 1) * nst]
            scores_t = _dot(bg, cg, NT)
            xs = xs_ref[:, cols].astype(F32)
            xdt = xs * cm['dt_e'][:, cols]
            xdt_s[...] = xdt.astype(BF16)
            dyf = dy_ref[:, cols].astype(F32)
            dscores = jnp.zeros((q, q), F32)
            dx_parts = []
            for sub in range(hpg // hb):
                c0 = sub * hb * p
                _fill_block_diag(x_bd, xdt_s, hb, c0)
                _fill_block_diag(dy_bd, dy_ref, hb, g * gw + c0)
                dm_all = _dot(dy_ref[:, g * gw + c0:g * gw + c0 + hb * p], x_bd[...], NT)
                for k in range(hb):
                    h = g * hpg + sub * hb + k
                    blk = slice(k * q, (k + 1) * q)
                    seg = cm['a_cs'][:, h:h + 1] - cm['a_cs_t'][h:h + 1, :]
                    mt_all[:, blk] = (scores_t * jnp.exp(jnp.where(upper, -seg, NEG_INF))).astype(BF16)
                    dscores = dscores + dm_all[:, blk] * jnp.exp(jnp.where(causal, seg, NEG_INF))
                dx_parts.append(_dot(mt_all[...], dy_bd[...], NN))
            hprev = hprev_ref[g]
            hprev_b = hprev.astype(BF16)
            dhn = dstate[g]
            dhn_b = dhn.astype(BF16)
            e_e, dec_e = cm['e_e'][:, cols], cm['dec_e'][:, cols]
            y_scan = y_ref[:, cols] - cm['dsk_e'][:, cols] * xs
            dye_b = (dyf * e_e).astype(BF16)
            dcg = _dot(dye_b, hprev_b, NN)
            dhp = _dot(dye_b, cg, TN)
            bdh = _dot(bg, dhn_b, NT)
            dbg = _dot((xdt * dec_e).astype(BF16), dhn_b, NN)
            dx_diag = jnp.concatenate(dx_parts, axis=1)
            dx = dec_e * bdh + dx_diag
            ds_b = dscores.astype(BF16)
            dcg = dcg + _dot(ds_b, bg, NN)
            dbg = dbg + _dot(ds_b, cg, TN)
            x_rounded = xdt_s[...].astype(F32)
            sums = sums + _hilo_dot(jnp.concatenate([dyf * y_scan, xdt * bdh, x_rounded * dx_diag, dx * xs, dyf * xs], axis=0), seg_sum)
            state_dot = state_dot + jnp.sum(_hilo_dot(dhn * hprev, seg_sum, TN), axis=0, keepdims=True)
            dxc_ref[:, cols] = (dx * cm['dt_e'][:, cols] + cm['dsk_e'][:, cols] * dyf).astype(dxc_ref.dtype)
            dxc_ref[:, d_inner + g * nst:d_inner + (g + 1) * nst] = dbg.astype(dxc_ref.dtype)
            dxc_ref[:, d_inner + gn + g * nst:d_inner + gn + (g + 1) * nst] = dcg.astype(dxc_ref.dtype)
            for hh in range(hpg):
                h = g * hpg + hh
                rows = slice(hh * p, (hh + 1) * p)
                dstate[g, rows, :] = jnp.exp(cm['a_last'][:, h:h + 1]) * dhn[rows] + dhp[rows]
        s_y, s_end, s_diag, s_dt, s_skip = (sums[k * q:(k + 1) * q] for k in range(5))
        dt, a, dec_end = cm['dt'], cm['a'], cm['dec_end']
        last_row = (lax.broadcasted_iota(jnp.int32, (q, 1), 0) == q - 1).astype(F32)
        da_last = jnp.sum(dec_end * s_end, axis=0, keepdims=True) + jnp.exp(cm['a_last']) * state_dot
        da = s_y - dec_end * s_end - s_diag + last_row * da_last
        ddta = _dot_hi(cm['triu'], da)
        ddt = s_dt + ddta * a
        d_a = jnp.sum(ddta * dt, axis=0, keepdims=True)
        ddt_raw = ddt * _sigmoid(dtraw_ref[...] + dtb_ref[...])
        ddtraw_ref[...] = ddt_raw
        ddtb_ref[...] += jnp.sum(ddt_raw, axis=0, keepdims=True)
        dalog_ref[...] += d_a * a
        ddsk_ref[...] += jnp.sum(s_skip, axis=0, keepdims=True)

    def rowmap(b, c):
        return b * nc + (nc - 1 - c)
    vec = pl.BlockSpec((1, LANES), lambda b, c: (0, 0))
    vec_shape = jax.ShapeDtypeStruct((1, LANES), F32)
    return _hosted_call(
        body,
        out_shape=[jax.ShapeDtypeStruct((t, c_dim), BF16), jax.ShapeDtypeStruct((t, LANES), F32), vec_shape, vec_shape, vec_shape],
        grid=(bl, nc),
        in_specs=[pl.BlockSpec((q, d_inner), lambda b, c: (rowmap(b, c), 0)),
                  pl.BlockSpec((q, d_inner), lambda b, c: (rowmap(b, c), 0)),
                  pl.BlockSpec((q, d_inner), lambda b, c: (rowmap(b, c), 0)),
                  pl.BlockSpec((q, gn), lambda b, c: (rowmap(b, c), d_inner // gn)),
                  pl.BlockSpec((q, gn), lambda b, c: (rowmap(b, c), d_inner // gn + 1)),
                  pl.BlockSpec((q, LANES), lambda b, c: (rowmap(b, c), 0)),
                  pl.BlockSpec((None, grp, gw, nst), lambda b, c: (rowmap(b, c), 0, 0, 0)), vec, vec, vec],
        out_specs=[pl.BlockSpec((q, c_dim), lambda b, c: (rowmap(b, c), 0)),
                   pl.BlockSpec((q, LANES), lambda b, c: (rowmap(b, c), 0)), vec, vec, vec],
        scratch_shapes=[pltpu.VMEM((grp, gw, nst), F32), pltpu.VMEM((q, hb * q), BF16),
                        pltpu.VMEM((hb * q, hb * p), BF16), pltpu.VMEM((hb * q, hb * p), BF16), pltpu.VMEM((q, gw), BF16)],
        operands=[dy, y, xc, xc, xc, dt_raw, hprev_all, dt_bias, a_log, d_skip], side=side, name=name)


def _bucket_onehot():
    blk = CHUNK
    qi = jnp.arange(blk)[:, None]
    kj = jnp.arange(2 * blk)[None, :]
    dist = jnp.maximum(qi + blk - kj, 0)
    max_exact = REL_BUCKETS // 2
    d = jnp.maximum(dist, 1).astype(F32)
    large = max_exact + (jnp.log(d / max_exact) / math.log(REL_MAX_DISTANCE / max_exact) * (REL_BUCKETS - max_exact)).astype(jnp.int32)
    large = jnp.minimum(large, REL_BUCKETS - 1)
    bucket = jnp.where(dist < max_exact, dist, large).reshape(-1)
    return (bucket[None, :] == jnp.arange(REL_BUCKETS)[:, None]).astype(F32)


def _small_mm_hi(a, b, dims, *, name):
    def body(a_ref, b_ref, o_ref):
        o_ref[...] = _dot_hi(a_ref[...], b_ref[...], dims)
    n = b.shape[0] if dims == NT else b.shape[1]
    return pl.pallas_call(body, out_shape=jax.ShapeDtypeStruct((a.shape[0], n), F32), name=name)(a, b)


def _attn_band_mask_t(n, rep):
    blk = CHUNK
    jj = lax.broadcasted_iota(jnp.int32, (2 * blk, rep * blk), 0)
    ii = lax.broadcasted_iota(jnp.int32, (2 * blk, rep * blk), 1) & (blk - 1)
    dist = ii + blk - jj
    in_window = jnp.logical_and(dist >= 0, dist < blk)
    return jnp.logical_and(in_window, jnp.logical_or(jj >= blk, n > 0))


def _sink_row(sink_ref, heads):
    return jnp.concatenate([jnp.broadcast_to(sink_ref[:, h:h + 1], (1, CHUNK)) for h in heads], axis=1)


def _attn_fwd(q, kv, bias_t, sinks, bl, *, name):
    t, qd = q.shape
    blk, hd = CHUNK, ATTN_HEAD_DIM
    kvd = ATTN_KV_HEADS * hd
    rep = ATTN_Q_HEADS // ATTN_KV_HEADS
    nb = t // bl // blk
    scale = hd ** -0.5

    def body(q_ref, kp_ref, kc_ref, vp_ref, vc_ref, bias_ref, sink_ref, o_ref, lse_ref):
        n = pl.program_id(1)
        mask = _attn_band_mask_t(n, rep)
        for kvh in range(ATTN_KV_HEADS):
            ks = slice(kvh * hd, (kvh + 1) * hd)
            heads = range(kvh * rep, (kvh + 1) * rep)
            qs = jnp.concatenate([q_ref[:, h * hd:(h + 1) * hd] for h in heads], axis=0)
            kk = jnp.concatenate([kp_ref[:, ks], kc_ref[:, ks]], axis=0)
            vv = jnp.concatenate([vp_ref[:, ks], vc_ref[:, ks]], axis=0)
            s = jnp.where(mask, _dot(kk, qs, NT) * scale + bias_ref[kvh], NEG_INF)
            sink = _sink_row(sink_ref, heads)
            m = jnp.maximum(jnp.max(s, axis=0, keepdims=True), sink)
            p = jnp.exp(s - m)
            den = jnp.sum(p, axis=0, keepdims=True) + jnp.exp(sink - m)
            o = _dot((p * (1.0 / den)).astype(BF16), vv, TN)
            lse = m + jnp.log(den)
            for r, h in enumerate(heads):
                o_ref[:, h * hd:(h + 1) * hd] = o[r * blk:(r + 1) * blk].astype(o_ref.dtype)
                lse_ref[h:h + 1, :] = lse[:, r * blk:(r + 1) * blk]

    def cur(b, n):
        return b * nb + n

    def prev(b, n):
        return b * nb + jnp.maximum(n - 1, 0)
    return pl.pallas_call(
        body, out_shape=(jax.ShapeDtypeStruct((t, qd), BF16), jax.ShapeDtypeStruct((t // blk * ATTN_Q_HEADS, blk), F32)), grid=(bl, nb),
        in_specs=[pl.BlockSpec((blk, qd), lambda b, n: (cur(b, n), 0)),
                  pl.BlockSpec((blk, kvd), lambda b, n: (prev(b, n), 0)), pl.BlockSpec((blk, kvd), lambda b, n: (cur(b, n), 0)),
                  pl.BlockSpec((blk, kvd), lambda b, n: (prev(b, n), 1)), pl.BlockSpec((blk, kvd), lambda b, n: (cur(b, n), 1)),
                  _const_spec(bias_t.shape), _const_spec((1, LANES))],
        out_specs=(pl.BlockSpec((blk, qd), lambda b, n: (cur(b, n), 0)),
                   pl.BlockSpec((ATTN_Q_HEADS, blk), lambda b, n: (cur(b, n), 0))),
        name=name, compiler_params=_params("parallel", "arbitrary"))(q, kv, kv, kv, kv, bias_t, sinks)


def _attn_bwd(do, q, kv, lse, bias_t, sinks, bl, *, name):
    t, qd = q.shape
    blk, hd = CHUNK, ATTN_HEAD_DIM
    kvd = ATTN_KV_HEADS * hd
    rep = ATTN_Q_HEADS // ATTN_KV_HEADS
    s_len = t // bl
    nb = s_len // blk
    scale = hd ** -0.5

    def body(do_ref, q_ref, kp_ref, kc_ref, vp_ref, vc_ref, lse_ref, bias_ref, sink_ref, dq_ref, dkv_ref, dbias_ref, dsink_ref):
        n = pl.program_id(1)

        @pl.when(jnp.logical_and(pl.program_id(0) == 0, n == 0))
        def _():
            dbias_ref[...] = jnp.zeros_like(dbias_ref)
            dsink_ref[...] = jnp.zeros_like(dsink_ref)

        mask = _attn_band_mask_t(n, rep)
        r_cur = pl.multiple_of(n * blk, blk)
        r_prev = pl.multiple_of(jnp.maximum(n - 1, 0) * blk, blk)
        dsink = jnp.zeros((1, LANES), F32)
        for kvh in range(ATTN_KV_HEADS):
            ks = slice(kvh * hd, (kvh + 1) * hd)
            heads = range(kvh * rep, (kvh + 1) * rep)
            qs = jnp.concatenate([q_ref[:, h * hd:(h + 1) * hd] for h in heads], axis=0)
            dos = jnp.concatenate([do_ref[:, h * hd:(h + 1) * hd] for h in heads], axis=0)
            kk = jnp.concatenate([kp_ref[:, ks], kc_ref[:, ks]], axis=0)
            vv = jnp.concatenate([vp_ref[:, ks], vc_ref[:, ks]], axis=0)
            lse = jnp.concatenate([lse_ref[h:h + 1, :] for h in heads], axis=1)
            p = jnp.exp(jnp.where(mask, _dot(kk, qs, NT) * scale + bias_ref[kvh], NEG_INF) - lse)
            dp = _dot(vv, dos, NT)
            delta = jnp.sum(p * dp, axis=0, keepdims=True)
            ds = p * (dp - delta)
            dsink_row = jnp.exp(_sink_row(sink_ref, heads) - lse) * delta
            dbias_ref[kvh] += ds
            ds_b = ds.astype(BF16)
            dq_s = _dot(ds_b, kk, TN) * scale
            dkk = _dot(ds_b, qs, NN) * scale
            dvv = _dot(p.astype(BF16), dos, NN)
            for r, h in enumerate(heads):
                dq_ref[:, h * hd:(h + 1) * hd] = dq_s[r * blk:(r + 1) * blk].astype(dq_ref.dtype)
                dsink = dsink - jnp.sum(dsink_row[:, r * blk:(r + 1) * blk], axis=1, keepdims=True) * _lane_onehot(h)
            vs = slice(kvd + kvh * hd, kvd + (kvh + 1) * hd)
            dkv_ref[pl.ds(r_cur, blk), ks] = dkk[blk:]
            dkv_ref[pl.ds(r_cur, blk), vs] = dvv[blk:]

            @pl.when(n > 0)
            def _():
                dkv_ref[pl.ds(r_prev, blk), ks] += dkk[:blk]
                dkv_ref[pl.ds(r_prev, blk), vs] += dvv[:blk]
        dsink_ref[...] += dsink

    def cur(b, n):
        return b * nb + n

    def prev(b, n):
        return b * nb + jnp.maximum(n - 1, 0)
    qspec = pl.BlockSpec((blk, qd), lambda b, n: (cur(b, n), 0))
    return pl.pallas_call(
        body,
        out_shape=(jax.ShapeDtypeStruct((t, qd), BF16), jax.ShapeDtypeStruct((t, 2 * kvd), F32),
                   jax.ShapeDtypeStruct(bias_t.shape, F32), jax.ShapeDtypeStruct((1, LANES), F32)),
        grid=(bl, nb),
        in_specs=[qspec, qspec,
                  pl.BlockSpec((blk, kvd), lambda b, n: (prev(b, n), 0)), pl.BlockSpec((blk, kvd), lambda b, n: (cur(b, n), 0)),
                  pl.BlockSpec((blk, kvd), lambda b, n: (prev(b, n), 1)), pl.BlockSpec((blk, kvd), lambda b, n: (cur(b, n), 1)),
                  pl.BlockSpec((ATTN_Q_HEADS, blk), lambda b, n: (cur(b, n), 0)), _const_spec(bias_t.shape), _const_spec((1, LANES))],
        out_specs=(qspec, pl.BlockSpec((s_len, 2 * kvd), lambda b, n: (b, 0)), _const_spec(bias_t.shape), _const_spec((1, LANES))),
        name=name, compiler_params=_params("arbitrary", "arbitrary"))(do, q, kv, kv, kv, kv, lse, bias_t, sinks)


def _merge_fwd(y, z, ng, o, gs, ga, w_ssm, w_attn, w_out, h_in, g_post, g_next, *, name):
    t, d = h_in.shape
    tm = _tile(t, 256)
    d_ssm = y.shape[1]
    gw = d_ssm // SSM_GROUPS

    def body(y_ref, z_ref, ng_ref, o_ref, gs_ref, ga_ref, ws_ref, wa_ref, wo_ref, hin_ref, gp_ref, gn_ref,
             yn_ref, ys_ref, ya_ref, mg_ref, mix_ref, hout_ref, next_ref):
        for g in range(SSM_GROUPS):
            sl = slice(g * gw, (g + 1) * gw)
            zv = z_ref[:, sl].astype(F32)
            yg = y_ref[:, sl] * (zv * _sigmoid(zv))
            r = lax.rsqrt(jnp.mean(yg * yg, axis=-1, keepdims=True) + RMS_EPS)
            yn_ref[:, sl] = (yg * r * ng_ref[:, sl]).astype(BF16)
        ys = _dot(yn_ref[...], ws_ref[...], NN)
        ya = _dot(o_ref[...], wa_ref[...], NN)
        merged = (_sigmoid(gs_ref[...].astype(F32)) * ys + _sigmoid(ga_ref[...].astype(F32)) * ya).astype(BF16)
        mix = _dot(merged, wo_ref[...], NN)
        ys_ref[...] = ys.astype(BF16)
        ya_ref[...] = ya.astype(BF16)
        mg_ref[...] = merged
        mix_ref[...] = mix
        hout = _rms_residual(mix, hin_ref[...], gp_ref[...], 1.0)
        hout_ref[...] = hout
        rn = lax.rsqrt(jnp.mean(hout * hout, axis=-1, keepdims=True) + RMS_EPS)
        next_ref[...] = (hout * rn * gn_ref[...]).astype(BF16)

    def row(w):
        return pl.BlockSpec((tm, w), lambda i: (i, 0))
    bshape = jax.ShapeDtypeStruct((t, d), BF16)
    fshape = jax.ShapeDtypeStruct((t, d), F32)
    return pl.pallas_call(
        body, out_shape=(jax.ShapeDtypeStruct((t, d_ssm), BF16), bshape, bshape, bshape, fshape, fshape, bshape), grid=(t // tm,),
        in_specs=[row(d_ssm), row(d_ssm), _const_spec((1, d_ssm)), row(o.shape[1]), row(d), row(d), _resident_spec(w_ssm.shape),
                  _resident_spec(w_attn.shape), _resident_spec(w_out.shape), row(d), _const_spec((1, d)), _const_spec((1, d))],
        out_specs=(row(d_ssm),) + (row(d),) * 6, name=name,
        compiler_params=_params("parallel"))(y, z, ng, o, gs, ga, w_ssm, w_attn, w_out, h_in, g_post, g_next)


def _merge_bwd(dh, mix, g_post, gs, ga, ys, ya, y, z, ng, w_ssm, w_attn, w_out, *, name):
    t, d = mix.shape
    tm = _tile(t, 256)
    d_ssm, d_attn = w_ssm.shape[0], w_attn.shape[0]
    gw = d_ssm // SSM_GROUPS

    def body(dh_ref, mix_ref, gp_ref, gs_ref, ga_ref, ys_ref, ya_ref, y_ref, z_ref, ng_ref, ws_ref, wa_ref, wo_ref,
             dmix_ref, dys_ref, dya_ref, dgs_ref, dga_ref, dy_ref, dz_ref, do_ref, dgp_ref, dng_ref):
        @pl.when(pl.program_id(0) == 0)
        def _():
            dgp_ref[...] = jnp.zeros_like(dgp_ref)
            dng_ref[...] = jnp.zeros_like(dng_ref)
        mv = mix_ref[...]
        dy = dh_ref[...]
        r = lax.rsqrt(jnp.mean(mv * mv, axis=-1, keepdims=True) + RMS_EPS)
        mhat = mv * r
        dyg = dy * gp_ref[...]
        dmix = (r * (dyg - mhat * jnp.mean(dyg * mhat, axis=-1, keepdims=True))).astype(BF16)
        dgp_ref[...] += jnp.sum(dy * mhat, axis=0, keepdims=True)
        dmix_ref[...] = dmix
        dmerged = _dot(dmix, wo_ref[...], NT)
        sgs = _sigmoid(gs_ref[...].astype(F32))
        sga = _sigmoid(ga_ref[...].astype(F32))
        dys = (dmerged * sgs).astype(BF16)
        dya = (dmerged * sga).astype(BF16)
        dys_ref[...] = dys
        dya_ref[...] = dya
        dgs_ref[...] = (dmerged * ys_ref[...].astype(F32) * sgs * (1.0 - sgs)).astype(BF16)
        dga_ref[...] = (dmerged * ya_ref[...].astype(F32) * sga * (1.0 - sga)).astype(BF16)
        do_ref[...] = _dot(dya, wa_ref[...], NT).astype(BF16)
        dyn = _dot(dys, ws_ref[...], NT)
        for g in range(SSM_GROUPS):
            sl = slice(g * gw, (g + 1) * gw)
            zv = z_ref[:, sl].astype(F32)
            yv = y_ref[:, sl]
            sg = _sigmoid(zv)
            sz = zv * sg
            yg = yv * sz
            rg = lax.rsqrt(jnp.mean(yg * yg, axis=-1, keepdims=True) + RMS_EPS)
            yhat = yg * rg
            dn = dyn[:, sl]
            dyg_n = dn * ng_ref[:, sl]
            dyg_g = rg * (dyg_n - yhat * jnp.mean(dyg_n * yhat, axis=-1, keepdims=True))
            dy_ref[:, sl] = (dyg_g * sz).astype(BF16)
            dz_ref[:, sl] = (dyg_g * yv * (sg * (1.0 + zv * (1.0 - sg)))).astype(BF16)
            dng_ref[:, sl] += jnp.sum(dn * yhat, axis=0, keepdims=True)

    def row(w):
        return pl.BlockSpec((tm, w), lambda i: (i, 0))

    def bshape(w):
        return jax.ShapeDtypeStruct((t, w), BF16)
    return pl.pallas_call(
        body, out_shape=(bshape(d),) * 5 + (bshape(d_ssm), bshape(d_ssm), bshape(d_attn), jax.ShapeDtypeStruct((1, d), F32),
                                            jax.ShapeDtypeStruct((1, d_ssm), F32)), grid=(t // tm,),
        in_specs=[row(d), row(d), _const_spec((1, d)), row(d), row(d), row(d), row(d), row(d_ssm), row(d_ssm), _const_spec((1, d_ssm)),
                  _resident_spec(w_ssm.shape), _resident_spec(w_attn.shape), _resident_spec(w_out.shape)],
        out_specs=(row(d),) * 5 + (row(d_ssm), row(d_ssm), row(d_attn), _const_spec((1, d)), _const_spec((1, d_ssm))),
        name=name, compiler_params=_params("arbitrary"))(dh, mix, g_post, gs, ga, ys, ya, y, z, ng, w_ssm, w_attn, w_out)


def _position():
    return lax.axis_index("x"), lax.axis_index("y"), lax.axis_index("c")


def _gather_exchange(shards):
    na = len(shards)

    def plan(ins, outs, sems):
        send_sems, recv_sems, local_sems = sems
        x, y, c = _position()
        me, sibling = (x, y, c), (x, y, 1 - c)
        chips = [(1 - x, y), (x, 1 - y), (1 - x, 1 - y)]

        def slot(a, pos):
            return outs[a].at[4 * pos[0] + 2 * pos[1] + pos[2]]

        def copy(a, k, block, to, src=None):
            return pltpu.make_async_remote_copy(
                src_ref=slot(a, block) if src is None else src, dst_ref=slot(a, block),
                send_sem=send_sems.at[a, k], recv_sem=recv_sems.at[a, k], device_id=to, device_id_type=MESH)

        mine = [pltpu.make_async_copy(ins[a], slot(a, me), local_sems.at[a]) for a in range(na)]
        first = []
        for a in range(na):
            first.append(copy(a, 0, me, sibling, src=ins[a]))
            first += [copy(a, 1 + j, me, (*chip, c), src=ins[a]) for j, chip in enumerate(chips)]
        return me, sibling, chips, copy, mine, first

    def start(ins, outs, sems):
        *_, mine, first = plan(ins, outs, sems)
        for cp in mine + first:
            cp.start()

    def finish(ins, outs, sems):
        me, sibling, chips, copy, mine, first = plan(ins, outs, sems)
        c = me[2]
        passed = []
        for a in range(na):
            for j, chip in enumerate(chips):
                copy(a, 1 + j, (*chip, c), me).wait_recv()
                fwd = copy(a, 4 + j, (*chip, c), sibling)
                fwd.start()
                passed.append(fwd)
        for a in range(na):
            copy(a, 0, sibling, me).wait_recv()
            for j, chip in enumerate(chips):
                copy(a, 4 + j, (*chip, 1 - c), me).wait_recv()
        for cp in first + passed:
            cp.wait_send()
        for cp in mine:
            cp.wait()

    return _Exchange(list(shards), [jax.ShapeDtypeStruct((N_DEV,) + s.shape, s.dtype) for s in shards],
                     [pltpu.SemaphoreType.DMA((na, 7)), pltpu.SemaphoreType.DMA((na, 7)), pltpu.SemaphoreType.DMA((na,))],
                     start, finish)


def _scatter_exchange(arrays):
    na = len(arrays)

    def copies(ins, outs, sems):
        send_sems, recv_sems = sems
        x, y, c = _position()
        out = []
        for a in range(na):
            for k in range(7):
                flip = k + 1
                peer = (x ^ (flip >> 2), y ^ ((flip >> 1) & 1), c ^ (flip & 1))
                peer_block = 4 * peer[0] + 2 * peer[1] + peer[2]
                out.append(pltpu.make_async_remote_copy(
                    src_ref=ins[a].at[peer_block], dst_ref=outs[a].at[k],
                    send_sem=send_sems.at[a, k], recv_sem=recv_sems.at[a, k], device_id=peer, device_id_type=MESH))
        return out

    def start(ins, outs, sems):
        for cp in copies(ins, outs, sems):
            cp.start()

    def finish(ins, outs, sems):
        for cp in copies(ins, outs, sems):
            cp.wait()

    return _Exchange(list(arrays), [jax.ShapeDtypeStruct((7,) + s.shape[1:], s.dtype) for s in arrays],
                     [pltpu.SemaphoreType.DMA((na, 7)), pltpu.SemaphoreType.DMA((na, 7))], start, finish)


def _reduce_adamw(own, recv, w, m, v, *, name):
    r, c = own.shape
    tm = _tile(r, 256)
    c1 = 1.0 - ADAM_B1 ** ADAM_STEP
    c2 = 1.0 - ADAM_B2 ** ADAM_STEP

    def body(own_ref, recv_ref, w_ref, m_ref, v_ref, g_ref, d_ref, mo_ref, vo_ref):
        gv = own_ref[...]
        for k in range(7):
            gv = gv + recv_ref[k].astype(F32)
        g_ref[...] = gv
        mn = ADAM_B1 * m_ref[...] + (1.0 - ADAM_B1) * gv
        vn = ADAM_B2 * v_ref[...] + (1.0 - ADAM_B2) * (gv * gv)
        mo_ref[...] = mn
        vo_ref[...] = vn
        d_ref[...] = -ADAM_LR * ((mn / c1) / (jnp.sqrt(vn / c2) + ADAM_EPS) + ADAM_WD * w_ref[...])

    blk = pl.BlockSpec((tm, c), lambda i: (i, 0))
    shp = jax.ShapeDtypeStruct((r, c), F32)
    return pl.pallas_call(body, out_shape=(shp,) * 4, grid=(r // tm,),
                          in_specs=[blk, pl.BlockSpec((7, tm, c), lambda i: (0, i, 0)), blk, blk, blk], out_specs=(blk,) * 4,
                          name=name, compiler_params=_params("parallel"))(own, recv, w, m, v)


SMALL_ROW = 8 * LANES


def _update_replicated(partials, ws, ms, vs, *, name):
    n = len(ws)
    pieces, n_rows = [], 0
    for k in sorted(range(n), key=lambda k_: -ws[k_].shape[0]):
        r, c = ws[k].shape
        assert (r == 1 or c <= SMALL_ROW) and (r == 1 or n_rows % 8 == 0)
        for c0 in range(0, c, SMALL_ROW):
            pieces.append((k, slice(0, r), slice(c0, min(c0 + SMALL_ROW, c)), n_rows))
            n_rows += r
    n_rows = -(-n_rows // 8) * 8
    c1 = 1.0 - ADAM_B1 ** ADAM_STEP
    c2 = 1.0 - ADAM_B2 ** ADAM_STEP

    def reduce_body(*refs):
        g_in, total, buf, send_sems, recv_sems = refs[:n], refs[n], refs[n + 1], refs[n + 2], refs[n + 3]
        x, y, c_ = _position()
        me = 4 * x + 2 * y + c_
        buf[me] = jnp.zeros((n_rows, SMALL_ROW), F32)
        for k, rs, cs, row0 in pieces:
            buf[me, row0:row0 + rs.stop, 0:cs.stop - cs.start] = g_in[k][rs, cs]
        copies = []
        for j in range(7):
            flip = j + 1
            peer = (x ^ (flip >> 2), y ^ ((flip >> 1) & 1), c_ ^ (flip & 1))
            cp = pltpu.make_async_remote_copy(
                src_ref=buf.at[me], dst_ref=buf.at[me], send_sem=send_sems.at[j], recv_sem=recv_sems.at[j],
                device_id=peer, device_id_type=MESH)
            cp.start()
            copies.append(cp)
        for cp in copies:
            cp.wait()
        acc = buf[0]
        for dev in range(1, N_DEV):
            acc = acc + buf[dev]
        total[...] = acc

    def step_body(*refs):
        total = refs[0]
        w_in_, m_in, v_in = (refs[1 + j * n:1 + (j + 1) * n] for j in range(3))
        g_out, d_out, m_out, v_out = (refs[1 + (3 + j) * n:1 + (4 + j) * n] for j in range(4))
        for k, rs, cs, row0 in pieces:
            gv = total[row0:row0 + rs.stop, 0:cs.stop - cs.start]
            mn = ADAM_B1 * m_in[k][rs, cs] + (1.0 - ADAM_B1) * gv
            vn = ADAM_B2 * v_in[k][rs, cs] + (1.0 - ADAM_B2) * (gv * gv)
            g_out[k][rs, cs] = gv
            m_out[k][rs, cs] = mn
            v_out[k][rs, cs] = vn
            d_out[k][rs, cs] = -ADAM_LR * ((mn / c1) / (jnp.sqrt(vn / c2) + ADAM_EPS) + ADAM_WD * w_in_[k][rs, cs])

    vm = pl.BlockSpec(memory_space=pltpu.VMEM)
    total = pl.pallas_call(
        reduce_body, out_shape=jax.ShapeDtypeStruct((n_rows, SMALL_ROW), F32), in_specs=[vm] * n, out_specs=vm,
        scratch_shapes=[pltpu.VMEM((N_DEV, n_rows, SMALL_ROW), F32), pltpu.SemaphoreType.DMA((7,)), pltpu.SemaphoreType.DMA((7,))],
        name=f"{name}_allreduce")(*partials)
    shapes = tuple(jax.ShapeDtypeStruct(w.shape, F32) for w in ws)
    outs = pl.pallas_call(step_body, out_shape=shapes * 4, in_specs=[vm] * (1 + 3 * n), out_specs=tuple([vm] * (4 * n)),
                          name=f"{name}_adamw")(total, *ws, *ms, *vs)
    return tuple(outs[j * n:(j + 1) * n] for j in range(4))


def _pad_lanes(v, width=LANES):
    return jnp.pad(v, ((0, 0), (0, width - v.shape[1])))


def kernel(x, ffn1_pre_g, ffn1_w_gate, ffn1_w_up, ffn1_w_down, ffn1_post_g, mix_pre_g, w_in, conv_w, conv_b, dt_bias, a_log, d_skip, ssm_norm_g, w_ssm_proj, attn_sinks, rel_bias_table, w_attn_proj, w_out, mix_post_g, ffn2_pre_g, ffn2_w_gate, ffn2_w_up, ffn2_w_down, ffn2_post_g, loss_target, m_ffn1_pre_g, m_ffn1_w_gate, m_ffn1_w_up, m_ffn1_w_down, m_ffn1_post_g, m_mix_pre_g, m_w_in, m_conv_w, m_conv_b, m_dt_bias, m_a_log, m_d_skip, m_ssm_norm_g, m_w_ssm_proj, m_attn_sinks, m_rel_bias_table, m_w_attn_proj, m_w_out, m_mix_post_g, m_ffn2_pre_g, m_ffn2_w_gate, m_ffn2_w_up, m_ffn2_w_down, m_ffn2_post_g, v_ffn1_pre_g, v_ffn1_w_gate, v_ffn1_w_up, v_ffn1_w_down, v_ffn1_post_g, v_mix_pre_g, v_w_in, v_conv_w, v_conv_b, v_dt_bias, v_a_log, v_d_skip, v_ssm_norm_g, v_w_ssm_proj, v_attn_sinks, v_rel_bias_table, v_w_attn_proj, v_w_out, v_mix_post_g, v_ffn2_pre_g, v_ffn2_w_gate, v_ffn2_w_up, v_ffn2_w_down, v_ffn2_post_g):
    args = dict(locals())
    weight_names = ['ffn1_pre_g', 'ffn1_w_gate', 'ffn1_w_up', 'ffn1_w_down', 'ffn1_post_g', 'mix_pre_g', 'w_in', 'conv_w', 'conv_b',
                    'dt_bias', 'a_log', 'd_skip', 'ssm_norm_g', 'w_ssm_proj', 'attn_sinks', 'rel_bias_table', 'w_attn_proj', 'w_out',
                    'mix_post_g', 'ffn2_pre_g', 'ffn2_w_gate', 'ffn2_w_up', 'ffn2_w_down', 'ffn2_post_g']
    col_sharded = ('ffn1_w_gate', 'ffn1_w_up', 'w_in', 'ffn2_w_gate', 'ffn2_w_up')
    row_sharded = ('ffn1_w_down', 'w_ssm_proj', 'w_attn_proj', 'w_out', 'ffn2_w_down')
    big = col_sharded + row_sharded

    bl, s_len, d = x.shape
    t = bl * s_len
    d_inner = ssm_norm_g.shape[1]
    n_heads = dt_bias.shape[1]
    gn = SSM_GROUPS * SSM_STATE
    conv_dim = d_inner + 2 * gn
    q_dim = ATTN_Q_HEADS * ATTN_HEAD_DIM
    kv_dim = ATTN_KV_HEADS * ATTN_HEAD_DIM

    def local_2d(name, a):
        a = a[0]
        return a.T if name in col_sharded else a

    ffn1_names = ('ffn1_w_gate', 'ffn1_w_up', 'ffn1_w_down')
    ffn2_names = ('ffn2_w_gate', 'ffn2_w_up', 'ffn2_w_down')
    mixer_names = ('w_ssm_proj', 'w_attn_proj', 'w_out')

    def shard(n):
        return local_2d(n, args[n]).astype(BF16)

    def rows(g):
        return g.reshape(N_DEV * g.shape[1], g.shape[2])

    x2 = x.reshape(t, d)
    tgt2 = loss_target.reshape(t, d)
    full = {}

    (h1,), saved1, ffn1_w, got_in = _ffn_forward(
        x2, ffn1_pre_g, lambda got: (rows(got[0]), rows(got[1])), lambda got: rows(got[2]), ffn1_post_g, "ffn1",
        side_norm=_gather_exchange([shard(n) for n in ffn1_names[:2]]),
        side_up=_gather_exchange([shard('w_in'), conv_w[0], shard('ffn1_w_down')]))
    full.update(zip(ffn1_names, ffn1_w))
    conv_w_full = jnp.transpose(got_in[1], (1, 0, 2)).reshape(SSM_CONV, conv_dim)

    win_t = rows(got_in[0])
    dt_lo = 2 * d + d_inner + conv_dim
    off = {'gs': 0, 'ga': d, 'z': 2 * d, 'xbc': 2 * d + d_inner, 'dt': dt_lo, 'q': dt_lo + n_heads, 'kv': dt_lo + n_heads + q_dim}
    assert all(o_ % 16 == 0 for o_ in off.values())

    u, (gs, ga, z, xbc, q, kv, dt_raw) = _proj_all(
        h1, mix_pre_g, win_t,
        [(off['gs'], d, BF16), (off['ga'], d, BF16), (off['z'], d_inner, BF16), (off['xbc'], conv_dim, BF16),
         (off['q'], q_dim, BF16), (off['kv'], 2 * kv_dim, BF16), (off['dt'], n_heads, F32)], name="mix_proj")

    dtb_p, alog_p, dsk_p, sinks_p = _pad_lanes(dt_bias), _pad_lanes(a_log), _pad_lanes(d_skip), _pad_lanes(attn_sinks)
    xc = _conv_fwd(xbc, conv_w_full, conv_b, bl, name="conv_fwd")
    late_names = mixer_names + ffn2_names
    (y, hprev), got_late = _ssd_fwd(xc, dt_raw, dtb_p, alog_p, dsk_p, bl, n_heads,
                                    side=_gather_exchange([shard(n) for n in late_names]), name="ssd_fwd")
    full.update({n: rows(g) for n, g in zip(late_names, got_late)})

    onehot = _bucket_onehot()
    rep = ATTN_Q_HEADS // ATTN_KV_HEADS
    bias = _small_mm_hi(rel_bias_table.T, onehot, NN, name="rel_bias")
    bias_t = jnp.transpose(bias.reshape(ATTN_KV_HEADS, rep, CHUNK, 2 * CHUNK), (0, 3, 1, 2)).reshape(ATTN_KV_HEADS, 2 * CHUNK, rep * CHUNK)
    o, lse = _attn_fwd(q, kv, bias_t, sinks_p, bl, name="attn_fwd")

    yn, ys, ya, merged, mix, h2, n2 = _merge_fwd(y, z, ssm_norm_g, o, gs, ga, full['w_ssm_proj'], full['w_attn_proj'], full['w_out'],
                                                 h1, mix_post_g, ffn2_pre_g, name="merge_fwd")

    (dh3, loss_vec, df2, dg_post2), saved2, _, _ = _ffn_forward(
        h2, ffn2_pre_g, (full['ffn2_w_gate'], full['ffn2_w_up']), full['ffn2_w_down'], ffn2_post_g, "ffn2", target=tgt2, n=n2)
    loss = lax.psum(loss_vec[0, 0], ("x", "y", "c"))

    grads, own, wire, received = {}, {}, {}, {}
    dh2, grads['ffn2_pre_g'], grads['ffn2_post_g'], g32, g16, _ = _ffn_backward(
        dh3, saved2, ffn2_pre_g, full['ffn2_w_gate'], full['ffn2_w_up'], full['ffn2_w_down'], "ffn2", (df2, dg_post2))
    own.update(zip(ffn2_names, map(_stack8, g32)))
    wire.update(zip(ffn2_names, map(_stack8, g16)))

    dmix, dys, dya, dgs, dga, dy, dz, do, grads['mix_post_g'], grads['ssm_norm_g'] = _merge_bwd(
        dh2, mix, mix_post_g, gs, ga, ys, ya, y, z, ssm_norm_g, full['w_ssm_proj'], full['w_attn_proj'], full['w_out'],
        name="merge_bwd")
    for n, (lhs, rhs) in zip(mixer_names, ((yn, dys), (o, dya), (merged, dmix))):
        g32_, g16_, _ = _mm_tn([lhs], rhs, name=f"d{n}")
        own[n], wire[n] = _stack8(g32_), _stack8(g16_)

    dq, dkv, dbias_t, dsinks = _attn_bwd(do, q, kv, lse, bias_t, sinks_p, bl, name="attn_bwd")
    dbias = jnp.transpose(dbias_t.reshape(ATTN_KV_HEADS, 2 * CHUNK, rep, CHUNK), (0, 2, 3, 1)).reshape(ATTN_Q_HEADS, -1)
    d_table = _small_mm_hi(onehot, dbias, NT, name="rel_bias_bwd")


    first_group = ffn2_names + mixer_names
    (dxc, ddt_raw, ddtb, dalog, ddsk), got = _ssd_bwd(dy, y, xc, dt_raw, hprev, dtb_p, alog_p, dsk_p, bl, n_heads,
                                                      side=_scatter_exchange([wire[n] for n in first_group]), name="ssd_bwd")
    received.update(zip(first_group, got))
    dxbc, dconv_w8, grads['conv_b'] = _conv_bwd(dxc, xbc, conv_w_full, conv_b, bl, name="conv_bwd")

    wide32, wide16, _ = _mm_tn([dgs, dga, dz, dxbc, dq], u, name="dw_in")
    kv32, kv16, _ = _mm_tn([dkv], u, name="dw_in_kv")
    dt32, dt16, _ = _mm_tn([ddt_raw], u, name="dw_in_dt")

    def original_order(wide, kv_part, dt_part):
        return jnp.concatenate([wide[:dt_lo], dt_part[:n_heads], wide[dt_lo:], kv_part], axis=0)
    me = 4 * lax.axis_index("x") + 2 * lax.axis_index("y") + lax.axis_index("c")
    blk_rows = win_t.shape[0] // N_DEV
    wire['w_in'] = _stack8(original_order(wide16, kv16, dt16))
    own_w_in = lax.dynamic_slice_in_dim(original_order(wide32, kv32, dt32), me * blk_rows, blk_rows)
    own['conv_w'] = jnp.transpose(dconv_w8[:SSM_CONV].reshape(SSM_CONV, N_DEV, conv_dim // N_DEV), (1, 0, 2))

    segs = [(g_, 0, off[k_]) for g_, k_ in zip([dgs, dga, dz, dxbc, dq, dkv, ddt_raw], ('gs', 'ga', 'z', 'xbc', 'q', 'kv', 'dt'))]
    dh1, grads['mix_pre_g'], post1, got = _mm_nn_rmsbwd(segs, [win_t], h1, mix_pre_g, dh2, below=(saved1[5], ffn1_post_g), tm=256,
                                                        side=_scatter_exchange([wire['w_in'], own['conv_w']]), name="mix_du")
    received.update(zip(('w_in', 'conv_w'), got))

    dx2, grads['ffn1_pre_g'], grads['ffn1_post_g'], g32, _, got = _ffn_backward(
        dh1, saved1, ffn1_pre_g, full['ffn1_w_gate'], full['ffn1_w_up'], full['ffn1_w_down'], "ffn1", post1, chain=True)
    own.update(zip(ffn1_names, map(_stack8, g32)))
    received.update(zip(ffn1_names, got))

    def own_block(a):
        return lax.dynamic_index_in_dim(a, me, 0, keepdims=False)
    out_g, out_d, out_m, out_v = {}, {}, {}, {}
    for n in big:
        w2, m2, v2 = local_2d(n, args[n]), local_2d(n, args['m_' + n]), local_2d(n, args['v_' + n])
        results = _reduce_adamw(own_w_in if n == 'w_in' else own_block(own[n]), received[n], w2, m2, v2, name=f"update_{n}")
        out_g[n], out_d[n], out_m[n], out_v[n] = ((a.T if n in col_sharded else a)[None] for a in results)
    results = _reduce_adamw(own_block(own['conv_w']), received['conv_w'], conv_w[0], m_conv_w[0], v_conv_w[0], name="update_conv_w")
    out_g['conv_w'], out_d['conv_w'], out_m['conv_w'], out_v['conv_w'] = (a[None] for a in results)

    grads['dt_bias'], grads['a_log'], grads['d_skip'], grads['attn_sinks'], grads['rel_bias_table'] = ddtb, dalog, ddsk, dsinks, d_table
    small = [n for n in weight_names if n not in big and n != 'conv_w']
    results = _update_replicated([grads[n] for n in small], [args[n] for n in small], [args['m_' + n] for n in small],
                                 [args['v_' + n] for n in small], name="update_replicated")
    for dst, vals in zip((out_g, out_d, out_m, out_v), results):
        dst.update(zip(small, vals))

    grad_x = dx2.reshape(bl, s_len, d)
    return (loss, grad_x, *[out_g[n] for n in weight_names], *[out_d[n] for n in weight_names],
            *[out_m[n] for n in weight_names], *[out_v[n] for n in weight_names])
```

```python
import functools
import math

import numpy as np
import jax
import jax.numpy as jnp
from jax import lax
from jax.experimental import pallas as pl
from jax.experimental.pallas import tpu as pltpu

F32 = jnp.float32
BF16 = jnp.bfloat16
MESH = pl.DeviceIdType.MESH
N_DEV = 8

SSM_HEAD_DIM = 64
SSM_GROUPS = 4
SSM_STATE = 128
SSM_CONV = 4
CHUNK = 128
ATTN_HEAD_DIM = 64
ATTN_Q_HEADS = 16
ATTN_KV_HEADS = 4
REL_BUCKETS = 32
REL_MAX_DISTANCE = 128
RMS_EPS = 1e-6
FFN_RESIDUAL_WEIGHT = 0.5
ADAM_LR, ADAM_B1, ADAM_B2, ADAM_EPS, ADAM_WD, ADAM_STEP = 0.001, 0.9, 0.999, 1e-08, 0.01, 10

LANES = 128
VMEM_LIMIT_BYTES = 56 * 1024 * 1024
FFN_COL_TILE = 1408
SSD_HEAD_BATCH = 2

NEG_INF = float("-inf")


def _params(*sem):
    return pltpu.CompilerParams(dimension_semantics=sem, vmem_limit_bytes=VMEM_LIMIT_BYTES)


def _tile(n, pref, mult=8):
    if n <= pref:
        return n
    t = (pref // mult) * mult
    while t >= mult:
        if n % t == 0:
            return t
        t -= mult
    return n


def _sigmoid(x):
    return 1.0 / (1.0 + jnp.exp(-x))


def _dot(a, b, dims):
    return lax.dot_general(a, b, (dims, ((), ())), preferred_element_type=F32)


NN = ((1,), (0,))
NT = ((1,), (1,))
TN = ((0,), (0,))


def _dot_hi(a, b, dims=NN):
    return lax.dot_general(a, b, (dims, ((), ())), preferred_element_type=F32, precision=lax.Precision.HIGHEST)


def _const_spec(shape):
    nd = len(shape)
    return pl.BlockSpec(shape, lambda *_: (0,) * nd)


def _resident_spec(shape):
    nd = len(shape)
    return pl.BlockSpec(shape, lambda *_: (0,) * nd, pipeline_mode=pl.Buffered(1))


class _Exchange:
    def __init__(self, arrays, out_shape, scratch, start, finish):
        self.arrays, self.out_shape, self.scratch, self.start, self.finish = arrays, out_shape, scratch, start, finish


def _hosted_call(body, *, grid, in_specs, out_specs, out_shape, scratch_shapes, operands, side, name):
    in_specs, out_specs, out_shape, scratch_shapes = list(in_specs), list(out_specs), list(out_shape), list(scratch_shapes)
    sem = ("arbitrary",) * len(grid)
    if side is None:
        outs = pl.pallas_call(body, out_shape=tuple(out_shape), grid=grid, in_specs=in_specs, out_specs=tuple(out_specs),
                              scratch_shapes=scratch_shapes, name=name, compiler_params=_params(*sem))(*operands)
        return tuple(outs), ()
    n_in, n_out, n_scr = len(in_specs), len(out_shape), len(scratch_shapes)
    s_in, s_out = len(side.arrays), len(side.out_shape)

    def wrapped(*refs):
        refs = list(refs)
        main_in, side_in = refs[:n_in], refs[n_in:n_in + s_in]
        o0 = n_in + s_in
        main_out, side_out = refs[o0:o0 + n_out], refs[o0 + n_out:o0 + n_out + s_out]
        c0 = o0 + n_out + s_out
        main_scr, side_scr = refs[c0:c0 + n_scr], refs[c0 + n_scr:]
        ids = [pl.program_id(ax) for ax in range(len(grid))]
        first = functools.reduce(jnp.logical_and, [i == 0 for i in ids])
        last = functools.reduce(jnp.logical_and, [i == g - 1 for i, g in zip(ids, grid)])

        @pl.when(first)
        def _():
            side.start(side_in, side_out, side_scr)

        body(*main_in, *main_out, *main_scr)

        @pl.when(last)
        def _():
            side.finish(side_in, side_out, side_scr)

    hbm = pl.BlockSpec(memory_space=pl.ANY)
    outs = pl.pallas_call(
        wrapped, out_shape=tuple(out_shape + list(side.out_shape)), grid=grid,
        in_specs=in_specs + [hbm] * s_in, out_specs=tuple(out_specs + [hbm] * s_out),
        scratch_shapes=scratch_shapes + list(side.scratch), name=name, compiler_params=_params(*sem))(*operands, *side.arrays)
    return tuple(outs[:n_out]), tuple(outs[n_out:])


def _proj_all(h, g, w, segs, *, name):
    t, d = h.shape
    tm = _tile(t, 512)
    segs = [(row0, wd_, max(wd_, LANES), dt_) for row0, wd_, dt_ in segs]
    assert all(row0 + out_w <= w.shape[0] for row0, _, out_w, _ in segs)

    def body(h_ref, g_ref, w_ref, u_ref, *o_refs):
        hv = h_ref[...]
        r = lax.rsqrt(jnp.mean(hv * hv, axis=-1, keepdims=True) + RMS_EPS)
        uv = (hv * r * g_ref[...]).astype(BF16)
        u_ref[...] = uv
        for (row0, width, out_w, _), o_ref in zip(segs, o_refs):
            for c0, c1 in _col_chunks(out_w, 8 * LANES):
                part = _dot(uv, w_ref[row0 + c0:row0 + c1, :], NT)
                if width < out_w:
                    part = jnp.where(lax.broadcasted_iota(jnp.int32, part.shape, 1) < width, part, 0.0)
                o_ref[:, c0:c1] = part.astype(o_ref.dtype)

    row = pl.BlockSpec((tm, d), lambda i: (i, 0))
    outs = pl.pallas_call(
        body, out_shape=(jax.ShapeDtypeStruct((t, d), BF16),) + tuple(jax.ShapeDtypeStruct((t, ow), dt_) for _, _, ow, dt_ in segs),
        grid=(t // tm,), in_specs=[row, _const_spec((1, d)), _resident_spec(w.shape)],
        out_specs=(row,) + tuple(pl.BlockSpec((tm, ow), lambda i: (i, 0)) for _, _, ow, _ in segs),
        name=name, compiler_params=_params("parallel"))(h, g, w)
    return outs[0], outs[1:]


def _mm_tn(a_list, b, *, tm=1408, tk=2048, side=None, name):
    t, n = b.shape
    tk = _tile(t, tk if len(a_list) == 1 else tk // 2)
    nk = t // tk
    widths = [a.shape[1] for a in a_list]
    tm = _tile(math.gcd(*widths), tm, LANES)
    assert all(w % tm == 0 for w in widths)
    starts = np.cumsum([0] + [w // tm for w in widths])
    nseg = len(a_list)

    def a_spec(s):
        lo, hi = int(starts[s]), int(starts[s + 1])

        def idx(i, k):
            active = jnp.logical_and(i >= lo, i < hi)
            return (jnp.where(active, k, 0), jnp.clip(i - lo, 0, hi - lo - 1))
        return pl.BlockSpec((tk, tm), idx)

    def body(*refs):
        a_refs, b_ref, o_ref, o16_ref, acc = refs[:nseg], refs[nseg], refs[nseg + 1], refs[nseg + 2], refs[nseg + 3]
        i, k = pl.program_id(0), pl.program_id(1)

        @pl.when(k == 0)
        def _():
            acc[...] = jnp.zeros_like(acc)

        bv = b_ref[...].astype(BF16)
        for s in range(nseg):
            lo, hi = int(starts[s]), int(starts[s + 1])

            @pl.when(jnp.logical_and(i >= lo, i < hi))
            def _(s=s):
                acc[...] += _dot(a_refs[s][...].astype(BF16), bv, TN)

        @pl.when(k == nk - 1)
        def _():
            o_ref[...] = acc[...]
            o16_ref[...] = acc[...].astype(BF16)

    rows = int(starts[-1]) * tm
    o_spec = pl.BlockSpec((tm, n), lambda i, k: (i, 0))
    (o32, o16), got = _hosted_call(
        body, out_shape=[jax.ShapeDtypeStruct((rows, n), F32), jax.ShapeDtypeStruct((rows, n), BF16)], grid=(int(starts[-1]), nk),
        in_specs=[a_spec(s) for s in range(nseg)] + [pl.BlockSpec((tk, n), lambda i, k: (k, 0))],
        out_specs=[o_spec, o_spec], scratch_shapes=[pltpu.VMEM((tm, n), F32)], operands=list(a_list) + [b], side=side, name=name)
    return o32, o16, got


def _mm_nn_rmsbwd(segs, weights, x, g, dres, *, below=None, tm=512, side=None, name):
    t, d = x.shape
    tm = _tile(t, tm)
    nseg, nw = len(segs), len(weights)
    n_below = 0 if below is None else 2

    def body(*refs):
        a_refs, w_refs = refs[:nseg], refs[nseg:nseg + nw]
        x_ref, g_ref, dres_ref = refs[nseg + nw:nseg + nw + 3]
        below_refs = refs[nseg + nw + 3:nseg + nw + 3 + n_below]
        dx_ref, dg_ref = refs[nseg + nw + 3 + n_below:nseg + nw + 5 + n_below]

        @pl.when(pl.program_id(0) == 0)
        def _():
            dg_ref[...] = jnp.zeros_like(dg_ref)

        dn = None
        for s, (a, w_idx, row0) in enumerate(segs):
            part = _dot(a_refs[s][...].astype(BF16), w_refs[w_idx][row0:row0 + a.shape[1], :], NN)
            dn = part if dn is None else dn + part
        xv = x_ref[...]
        r = lax.rsqrt(jnp.mean(xv * xv, axis=-1, keepdims=True) + RMS_EPS)
        xhat = xv * r
        dyg = dn * g_ref[...]
        dx = dres_ref[...] + r * (dyg - xhat * jnp.mean(dyg * xhat, axis=-1, keepdims=True))
        dx_ref[...] = dx
        dg_ref[...] += jnp.sum(dn * xhat, axis=0, keepdims=True)
        if below is not None:
            f_ref, gp_ref = below_refs
            df_ref, dgp_ref = refs[nseg + nw + 5 + n_below:]

            @pl.when(pl.program_id(0) == 0)
            def _():
                dgp_ref[...] = jnp.zeros_like(dgp_ref)
            fv = f_ref[...]
            rf = lax.rsqrt(jnp.mean(fv * fv, axis=-1, keepdims=True) + RMS_EPS)
            fhat = fv * rf
            dy = FFN_RESIDUAL_WEIGHT * dx
            dfg = dy * gp_ref[...]
            df_ref[...] = (rf * (dfg - fhat * jnp.mean(dfg * fhat, axis=-1, keepdims=True))).astype(BF16)
            dgp_ref[...] += jnp.sum(dy * fhat, axis=0, keepdims=True)

    row = pl.BlockSpec((tm, d), lambda i: (i, 0))
    vec = _const_spec((1, d))
    in_specs = [pl.BlockSpec((tm, a.shape[1]), lambda i: (i, 0)) for a, _, _ in segs]
    in_specs += [_resident_spec(w.shape) for w in weights] + [row, vec, row] + ([row, vec] if below is not None else [])
    out_specs = [row, vec] + ([row, vec] if below is not None else [])
    out_shape = [jax.ShapeDtypeStruct((t, d), F32), jax.ShapeDtypeStruct((1, d), F32)]
    if below is not None:
        out_shape += [jax.ShapeDtypeStruct((t, d), BF16), jax.ShapeDtypeStruct((1, d), F32)]
    outs, extra = _hosted_call(
        body, grid=(t // tm,), in_specs=in_specs, out_specs=out_specs, out_shape=out_shape, scratch_shapes=[],
        operands=[a for a, _, _ in segs] + list(weights) + [x, g, dres] + (list(below) if below is not None else []),
        side=side, name=name)
    return outs[0], outs[1], (tuple(outs[2:]) if below is not None else None), extra


def _col_chunks(width, chunk=4 * LANES):
    return [(c0, min(c0 + chunk, width)) for c0 in range(0, width, chunk)]


def _rms_fwd(x, g, *, side=None, name):
    t, d = x.shape
    tm = _tile(t, 512)

    def body(x_ref, g_ref, o_ref):
        xv = x_ref[...]
        r = lax.rsqrt(jnp.mean(xv * xv, axis=-1, keepdims=True) + RMS_EPS)
        o_ref[...] = (xv * r * g_ref[...]).astype(o_ref.dtype)

    row = pl.BlockSpec((tm, d), lambda i: (i, 0))
    (n,), got = _hosted_call(body, grid=(t // tm,), in_specs=[row, _const_spec((1, d))], out_specs=[row],
                             out_shape=[jax.ShapeDtypeStruct((t, d), BF16)], scratch_shapes=[], operands=[x, g], side=side, name=name)
    return n, got


def _ffn_up(n, wgt, wut, *, side=None, name):
    t, d = n.shape
    f = wgt.shape[0]
    tm, tn = _tile(t, 1024), _tile(f, FFN_COL_TILE, LANES)

    def body(n_ref, wg_ref, wu_ref, g_ref, u_ref, h_ref):
        nv = n_ref[...]
        gv = _dot(nv, wg_ref[...], NT)
        uv = _dot(nv, wu_ref[...], NT)
        g_ref[...] = gv.astype(BF16)
        u_ref[...] = uv.astype(BF16)
        h_ref[...] = (gv * _sigmoid(gv) * uv).astype(BF16)

    w_spec = pl.BlockSpec((tn, d), lambda j, i: (j, 0))
    o_spec = pl.BlockSpec((tm, tn), lambda j, i: (i, j))
    shp = jax.ShapeDtypeStruct((t, f), BF16)
    return _hosted_call(body, grid=(f // tn, t // tm), in_specs=[pl.BlockSpec((tm, d), lambda j, i: (i, 0)), w_spec, w_spec],
                        out_specs=[o_spec, o_spec, o_spec], out_shape=[shp, shp, shp], scratch_shapes=[], operands=[n, wgt, wut],
                        side=side, name=name)


def _rms_residual(acc, h, gp, weight):
    r = lax.rsqrt(jnp.mean(acc * acc, axis=-1, keepdims=True) + RMS_EPS)
    return h + weight * (acc * r * gp)


def _ffn_down(hid, wd, h_in, gp, *, target=None, side=None, name):
    t, f = hid.shape
    d = wd.shape[1]
    tm = _tile(t, 256)
    row = pl.BlockSpec((tm, d), lambda i: (i, 0))
    shp = jax.ShapeDtypeStruct((t, d), F32)
    in_specs = [pl.BlockSpec((tm, f), lambda i: (i, 0)), _resident_spec((f, d)), row, _const_spec((1, d))]

    if target is None:
        def body(hid_ref, wd_ref, hin_ref, gp_ref, f_ref, hout_ref):
            acc = _dot(hid_ref[...], wd_ref[...], NN)
            f_ref[...] = acc
            hout_ref[...] = _rms_residual(acc, hin_ref[...], gp_ref[...], FFN_RESIDUAL_WEIGHT)

        return _hosted_call(body, grid=(t // tm,), in_specs=in_specs, out_specs=[row, row], out_shape=[shp, shp], scratch_shapes=[],
                            operands=[hid, wd, h_in, gp], side=side, name=name)

    def body_loss(hid_ref, wd_ref, hin_ref, gp_ref, tgt_ref, f_ref, dh_ref, loss_ref, df_ref, dgp_ref):
        @pl.when(pl.program_id(0) == 0)
        def _():
            loss_ref[...] = jnp.zeros_like(loss_ref)
            dgp_ref[...] = jnp.zeros_like(dgp_ref)
        acc = _dot(hid_ref[...], wd_ref[...], NN)
        f_ref[...] = acc
        r = lax.rsqrt(jnp.mean(acc * acc, axis=-1, keepdims=True) + RMS_EPS)
        fhat = acc * r
        e = hin_ref[...] + FFN_RESIDUAL_WEIGHT * (fhat * gp_ref[...]) - tgt_ref[...]
        dh = e * (1.0 / d)
        dh_ref[...] = dh
        per_row = jnp.sum(e * e, axis=1, keepdims=True) * (1.0 / d)
        loss_ref[...] += 0.5 * jnp.sum(per_row, axis=0, keepdims=True)
        dy = FFN_RESIDUAL_WEIGHT * dh
        dyg = dy * gp_ref[...]
        df_ref[...] = (r * (dyg - fhat * jnp.mean(dyg * fhat, axis=-1, keepdims=True))).astype(BF16)
        dgp_ref[...] += jnp.sum(dy * fhat, axis=0, keepdims=True)

    return _hosted_call(body_loss, grid=(t // tm,), in_specs=in_specs + [row],
                        out_specs=[row, row, _const_spec((1, LANES)), row, _const_spec((1, d))],
                        out_shape=[shp, shp, jax.ShapeDtypeStruct((1, LANES), F32), jax.ShapeDtypeStruct((t, d), BF16),
                                   jax.ShapeDtypeStruct((1, d), F32)],
                        scratch_shapes=[], operands=[hid, wd, h_in, gp, target], side=side, name=name)


def _ffn_dhid(df, wd, g, u, *, name):
    t, d = df.shape
    f = wd.shape[0]
    tm, tn = _tile(t, 512), _tile(f, FFN_COL_TILE, LANES)

    def body(df_ref, wd_ref, g_ref, u_ref, dg_ref, du_ref):
        dh = _dot(df_ref[...], wd_ref[...], NT)
        gv = g_ref[...].astype(F32)
        uv = u_ref[...].astype(F32)
        sg = _sigmoid(gv)
        silu = gv * sg
        dg_ref[...] = (dh * uv * (sg + silu * (1.0 - sg))).astype(BF16)
        du_ref[...] = (dh * silu).astype(BF16)

    o_spec = pl.BlockSpec((tm, tn), lambda j, i: (i, j))
    shp = jax.ShapeDtypeStruct((t, f), BF16)
    return pl.pallas_call(body, out_shape=(shp, shp), grid=(f // tn, t // tm),
                          in_specs=[pl.BlockSpec((tm, d), lambda j, i: (i, 0)), pl.BlockSpec((tn, d), lambda j, i: (j, 0)), o_spec, o_spec],
                          out_specs=(o_spec, o_spec), name=name, compiler_params=_params("parallel", "arbitrary"))(df, wd, g, u)


def _ffn_forward(h_in, g_pre, w_up, wd, g_post, tag, side_norm=None, side_up=None, target=None, n=None):
    got_norm = ()
    if n is None:
        n, got_norm = _rms_fwd(h_in, g_pre, side=side_norm, name=f"{tag}_prenorm")
    wgt, wut = w_up(got_norm) if callable(w_up) else w_up
    (g, u, hid), got_up = _ffn_up(n, wgt, wut, side=side_up, name=f"{tag}_up")
    wd = wd(got_up) if callable(wd) else wd
    outs, _ = _ffn_down(hid, wd, h_in, g_post, target=target, name=f"{tag}_down")
    return outs[1:], (h_in, n, g, u, hid, outs[0]), (wgt, wut, wd), got_up


def _stack8(g):
    return g.reshape(N_DEV, g.shape[0] // N_DEV, g.shape[1])


def _ffn_backward(dh_out, saved, g_pre, wgt, wut, wd, tag, post, chain=False):
    h_in, n, g, u, hid, _ = saved

    def side_of(grad16):
        return _scatter_exchange([_stack8(grad16)]) if chain else None

    df, dg_post = post
    dgate, dup = _ffn_dhid(df, wd, g, u, name=f"{tag}_dhid")
    d_wd, d_wd16, _ = _mm_tn([hid], df, name=f"{tag}_dwd")
    d_wgt, d_wgt16, got_wd = _mm_tn([dgate], n, side=side_of(d_wd16), name=f"{tag}_dwg")
    d_wut, d_wut16, got_wg = _mm_tn([dup], n, side=side_of(d_wgt16), name=f"{tag}_dwu")
    dh_in, dg_pre, _, got_wu = _mm_nn_rmsbwd([(dgate, 0, 0), (dup, 1, 0)], [wgt, wut], h_in, g_pre, dh_out, side=side_of(d_wut16),
                                             name=f"{tag}_dn")
    received = (got_wg[0], got_wu[0], got_wd[0]) if chain else None
    return dh_in, dg_pre, dg_post, (d_wgt, d_wut, d_wd), (d_wgt16, d_wut16, d_wd16), received


CONV_ROWS = 128
HALO = 8


def _taps(w_ref):
    return [w_ref[k:k + 1, :] for k in range(SSM_CONV)]


def _conv_chunk(x_ref, xs, r0, taps, bias):
    xs[HALO + r0:HALO + r0 + CONV_ROWS, :] = x_ref[r0:r0 + CONV_ROWS, :].astype(F32)
    shifted = [xs[HALO + r0 - k:HALO + r0 - k + CONV_ROWS, :] for k in range(SSM_CONV)]
    pre = bias + shifted[0] * taps[SSM_CONV - 1]
    for k in range(1, SSM_CONV):
        pre = pre + shifted[k] * taps[SSM_CONV - 1 - k]
    return shifted, pre


def _fold_rows(a):
    return functools.reduce(jnp.add, [a[i:i + 8] for i in range(0, a.shape[0], 8)])


def _conv_fwd(xbc, conv_w, conv_b, bl, *, name):
    t, c = xbc.shape
    s = t // bl
    tc = LANES
    assert s % CONV_ROWS == 0

    def body(x_ref, w_ref, b_ref, o_ref, xs):
        taps, bias = _taps(w_ref), b_ref[...]
        xs[0:HALO, :] = jnp.zeros((HALO, tc), F32)
        for r0 in range(0, s, CONV_ROWS):
            _, pre = _conv_chunk(x_ref, xs, r0, taps, bias)
            o_ref[r0:r0 + CONV_ROWS, :] = (pre * _sigmoid(pre)).astype(o_ref.dtype)

    blk = pl.BlockSpec((s, tc), lambda b, j: (b, j))
    return pl.pallas_call(body, out_shape=jax.ShapeDtypeStruct((t, c), BF16), grid=(bl, c // tc),
                          in_specs=[blk, pl.BlockSpec((SSM_CONV, tc), lambda b, j: (0, j)), pl.BlockSpec((1, tc), lambda b, j: (0, j))],
                          out_specs=blk, scratch_shapes=[pltpu.VMEM((HALO + s, tc), F32)],
                          name=name, compiler_params=_params("parallel", "arbitrary"))(xbc, conv_w, conv_b)


def _conv_bwd(dxc, xbc, conv_w, conv_b, bl, *, name):
    t, c = xbc.shape
    s = t // bl
    tc = LANES

    def body(dy_ref, x_ref, w_ref, b_ref, dx_ref, dw_ref, db_ref, xs, dpre_s):
        @pl.when(pl.program_id(1) == 0)
        def _():
            dw_ref[...] = jnp.zeros_like(dw_ref)
            db_ref[...] = jnp.zeros_like(db_ref)

        taps, bias = _taps(w_ref), b_ref[...]
        zero8 = jnp.zeros((HALO, tc), F32)
        xs[0:HALO, :] = zero8
        dpre_s[s:s + HALO, :] = zero8
        sums = [zero8] * (SSM_CONV + 1)
        for r0 in range(0, s, CONV_ROWS):
            shifted, pre = _conv_chunk(x_ref, xs, r0, taps, bias)
            sg = _sigmoid(pre)
            dpre = dy_ref[r0:r0 + CONV_ROWS, :].astype(F32) * (sg * (1.0 + pre * (1.0 - sg)))
            dpre_s[r0:r0 + CONV_ROWS, :] = dpre
            sums = [acc + _fold_rows(dpre * sh) for acc, sh in zip(sums[:-1], shifted)] + [sums[-1] + _fold_rows(dpre)]
        for k in range(SSM_CONV):
            dw_ref[SSM_CONV - 1 - k:SSM_CONV - k, :] += jnp.sum(sums[k], axis=0, keepdims=True)
        db_ref[...] += jnp.sum(sums[-1], axis=0, keepdims=True)
        for r0 in range(0, s, CONV_ROWS):
            dx = dpre_s[r0:r0 + CONV_ROWS, :] * taps[SSM_CONV - 1]
            for k in range(1, SSM_CONV):
                dx = dx + dpre_s[r0 + k:r0 + k + CONV_ROWS, :] * taps[SSM_CONV - 1 - k]
            dx_ref[r0:r0 + CONV_ROWS, :] = dx.astype(dx_ref.dtype)

    blk = pl.BlockSpec((s, tc), lambda j, b: (b, j))
    return pl.pallas_call(
        body, out_shape=(jax.ShapeDtypeStruct((t, c), BF16), jax.ShapeDtypeStruct((8, c), F32), jax.ShapeDtypeStruct((1, c), F32)),
        grid=(c // tc, bl),
        in_specs=[blk, blk, pl.BlockSpec((SSM_CONV, tc), lambda j, b: (0, j)), pl.BlockSpec((1, tc), lambda j, b: (0, j))],
        out_specs=(blk, pl.BlockSpec((8, tc), lambda j, b: (0, j)), pl.BlockSpec((1, tc), lambda j, b: (0, j))),
        scratch_shapes=[pltpu.VMEM((HALO + s, tc), F32), pltpu.VMEM((s + HALO, tc), F32)],
        name=name, compiler_params=_params("parallel", "arbitrary"))(dxc, xbc, conv_w, conv_b)


def _softplus(x):
    return jnp.maximum(x, 0.0) + jnp.log1p(jnp.exp(-jnp.abs(x)))


def _hilo_dot(v, m_b, dims=NN):
    hi = v.astype(BF16)
    lo = (v - hi.astype(F32)).astype(BF16)
    return _dot(hi, m_b, dims) + _dot(lo, m_b, dims)


def _ssd_chunk_common(dtraw_ref, dtb_ref, alog_ref, dsk_ref, d_inner):
    q, p = CHUNK, SSM_HEAD_DIM
    shift = p.bit_length() - 1
    assert 1 << shift == p
    dt = _softplus(dtraw_ref[...] + dtb_ref[...])
    a = -jnp.exp(alog_ref[...])
    ii = lax.broadcasted_iota(jnp.int32, (q, q), 0)
    jj = lax.broadcasted_iota(jnp.int32, (q, q), 1)
    causal = ii >= jj
    tril = jnp.where(causal, 1.0, 0.0).astype(F32)
    triu = jnp.where(ii <= jj, 1.0, 0.0).astype(F32)
    a_cs = _dot_hi(tril, dt * a)
    a_cs_t = a_cs.T
    a_last = a_cs[q - 1:q, :]
    e_col = jnp.exp(a_cs)
    dec_end = jnp.exp(a_last - a_cs)
    head_of_col = lax.shift_right_logical(lax.broadcasted_iota(jnp.int32, (LANES, d_inner), 1), shift)
    spread = (lax.broadcasted_iota(jnp.int32, (LANES, d_inner), 0) == head_of_col).astype(BF16)
    exact = jnp.concatenate([dt, jnp.broadcast_to(dsk_ref[...], (8, LANES))], axis=0)
    hi = jnp.concatenate([exact, e_col, dec_end], axis=0).astype(BF16)
    lo = (exact - hi[:q + 8].astype(F32)).astype(BF16)
    wide = _dot(hi, spread, NN)
    fine = wide[:q + 8] + _dot(lo, spread, NN)
    return dict(dt=dt, a=a, a_cs=a_cs, a_cs_t=a_cs_t, a_last=a_last, dec_end=dec_end, causal=causal, triu=triu,
                dt_e=fine[:q], dsk_e=fine[q:q + 1], e_e=wide[q + 8:2 * q + 8], dec_e=wide[2 * q + 8:3 * q + 8])


def _fill_block_diag(bd_ref, src_ref, hpg, col0=0):
    q, p = CHUNK, SSM_HEAD_DIM
    for hh in range(hpg):
        bd_ref[hh * q:(hh + 1) * q, hh * p:(hh + 1) * p] = src_ref[:, col0 + hh * p:col0 + (hh + 1) * p]


def _lane_onehot(h):
    return (lax.broadcasted_iota(jnp.int32, (1, LANES), 1) == h).astype(F32)


def _ssd_fwd(xc, dt_raw, dt_bias, a_log, d_skip, bl, n_heads, *, side=None, name):
    t = xc.shape[0]
    q, p, nst, grp = CHUNK, SSM_HEAD_DIM, SSM_STATE, SSM_GROUPS
    d_inner = n_heads * p
    hpg = n_heads // grp
    hb = min(hpg, SSD_HEAD_BATCH)
    gw = hpg * p
    nc = t // bl // q
    assert d_inner % (grp * nst) == 0 and nst == LANES and hpg % hb == 0

    def body(xs_ref, b_ref, c_ref, dtraw_ref, dtb_ref, alog_ref, dsk_ref, y_ref, hprev_ref, state, m_all, x_bd, xdt_s):
        @pl.when(jnp.logical_and(pl.program_id(0) == 0, pl.program_id(1) == 0))
        def _():
            x_bd[...] = jnp.zeros_like(x_bd)

        @pl.when(pl.program_id(1) == 0)
        def _():
            state[...] = jnp.zeros_like(state)

        cm = _ssd_chunk_common(dtraw_ref, dtb_ref, alog_ref, dsk_ref, d_inner)
        for g in range(grp):
            cols = slice(g * gw, (g + 1) * gw)
            bg = b_ref[:, g * nst:(g + 1) * nst]
            cg = c_ref[:, g * nst:(g + 1) * nst]
            scores = _dot(cg, bg, NT)
            xs = xs_ref[:, cols].astype(F32)
            xdt = xs * cm['dt_e'][:, cols]
            xdt_s[...] = xdt.astype(BF16)
            y_parts = []
            for sub in range(hpg // hb):
                for k in range(hb):
                    h = g * hpg + sub * hb + k
                    seg = cm['a_cs'][:, h:h + 1] - cm['a_cs_t'][h:h + 1, :]
                    m_all[:, k * q:(k + 1) * q] = (scores * jnp.exp(jnp.where(cm['causal'], seg, NEG_INF))).astype(BF16)
                _fill_block_diag(x_bd, xdt_s, hb, sub * hb * p)
                y_parts.append(_dot(m_all[...], x_bd[...], NN))
            hprev = state[g]
            hprev_ref[g] = hprev
            y = jnp.concatenate(y_parts, axis=1) + cm['e_e'][:, cols] * _dot(cg, hprev.astype(BF16), NT)
            y_ref[:, cols] = y + cm['dsk_e'][:, cols] * xs
            st = _dot((xdt * cm['dec_e'][:, cols]).astype(BF16), bg, TN)
            for hh in range(hpg):
                h = g * hpg + hh
                rows = slice(hh * p, (hh + 1) * p)
                state[g, rows, :] = jnp.exp(cm['a_last'][:, h:h + 1]) * hprev[rows] + st[rows]

    gn = grp * nst

    def rowmap(b, c):
        return b * nc + c
    vec = pl.BlockSpec((1, LANES), lambda b, c: (0, 0))
    return _hosted_call(
        body,
        out_shape=[jax.ShapeDtypeStruct((t, d_inner), F32), jax.ShapeDtypeStruct((t // q, grp, gw, nst), F32)],
        grid=(bl, nc),
        in_specs=[pl.BlockSpec((q, d_inner), lambda b, c: (rowmap(b, c), 0)),
                  pl.BlockSpec((q, gn), lambda b, c: (rowmap(b, c), d_inner // gn)),
                  pl.BlockSpec((q, gn), lambda b, c: (rowmap(b, c), d_inner // gn + 1)),
                  pl.BlockSpec((q, LANES), lambda b, c: (rowmap(b, c), 0)), vec, vec, vec],
        out_specs=[pl.BlockSpec((q, d_inner), lambda b, c: (rowmap(b, c), 0)),
                   pl.BlockSpec((None, grp, gw, nst), lambda b, c: (rowmap(b, c), 0, 0, 0))],
        scratch_shapes=[pltpu.VMEM((grp, gw, nst), F32), pltpu.VMEM((q, hb * q), BF16), pltpu.VMEM((hb * q, hb * p), BF16),
                        pltpu.VMEM((q, gw), BF16)],
        operands=[xc, xc, xc, dt_raw, dt_bias, a_log, d_skip], side=side, name=name)


def _ssd_bwd(dy, y, xc, dt_raw, hprev_all, dt_bias, a_log, d_skip, bl, n_heads, *, side=None, name):
    t, c_dim = xc.shape
    q, p, nst, grp = CHUNK, SSM_HEAD_DIM, SSM_STATE, SSM_GROUPS
    d_inner = n_heads * p
    hpg = n_heads // grp
    hb = min(hpg, SSD_HEAD_BATCH)
    gw = hpg * p
    nc = t // bl // q
    gn = grp * nst
    shift = p.bit_length() - 1

    def body(dy_ref, y_ref, xs_ref, b_ref, c_ref, dtraw_ref, hprev_ref, dtb_ref, alog_ref, dsk_ref,
             dxc_ref, ddtraw_ref, ddtb_ref, dalog_ref, ddsk_ref, dstate, mt_all, x_bd, dy_bd, xdt_s):
        @pl.when(jnp.logical_and(pl.program_id(0) == 0, pl.program_id(1) == 0))
        def _():
            ddtb_ref[...] = jnp.zeros_like(ddtb_ref)
            dalog_ref[...] = jnp.zeros_like(dalog_ref)
            ddsk_ref[...] = jnp.zeros_like(ddsk_ref)
            x_bd[...] = jnp.zeros_like(x_bd)
            dy_bd[...] = jnp.zeros_like(dy_bd)

        @pl.when(pl.program_id(1) == 0)
        def _():
            dstate[...] = jnp.zeros_like(dstate)

        cm = _ssd_chunk_common(dtraw_ref, dtb_ref, alog_ref, dsk_ref, d_inner)
        causal = cm['causal']
        upper = cm['triu'] > 0.5
        seg_row = lax.shift_right_logical(lax.broadcasted_iota(jnp.int32, (gw, LANES), 0), shift)
        seg_lane = lax.broadcasted_iota(jnp.int32, (gw, LANES), 1)
        sums = jnp.zeros((5 * q, LANES), F32)
        state_dot = jnp.zeros((1, LANES), F32)
        for g in range(grp):
            cols = slice(g * gw, (g + 1) * gw)
            seg_sum = (seg_row + g * hpg == seg_lane).astype(BF16)
            bg = b_ref[:, g * nst:(g + 1) * nst]
            cg = c_ref[:, g * nst:(g + 1) * nst]
            scores_t = _dot(bg, cg, NT)
            xs = xs_ref[:, cols].astype(F32)
            xdt = xs * cm['dt_e'][:, cols]
            xdt_s[...] = xdt.astype(BF16)
            dyf = dy_ref[:, cols].astype(F32)
            dscores = jnp.zeros((q, q), F32)
            dx_parts = []
            for sub in range(hpg // hb):
                c0 = sub * hb * p
                _fill_block_diag(x_bd, xdt_s, hb, c0)
                _fill_block_diag(dy_bd, dy_ref, hb, g * gw + c0)
                dm_all = _dot(dy_ref[:, g * gw + c0:g * gw + c0 + hb * p], x_bd[...], NT)
                for k in range(hb):
                    h = g * hpg + sub * hb + k
                    blk = slice(k * q, (k + 1) * q)
                    seg = cm['a_cs'][:, h:h + 1] - cm['a_cs_t'][h:h + 1, :]
                    mt_all[:, blk] = (scores_t * jnp.exp(jnp.where(upper, -seg, NEG_INF))).astype(BF16)
                    dscores = dscores + dm_all[:, blk] * jnp.exp(jnp.where(causal, seg, NEG_INF))
                dx_parts.append(_dot(mt_all[...], dy_bd[...], NN))
            hprev = hprev_ref[g]
            hprev_b = hprev.astype(BF16)
            dhn = dstate[g]
            dhn_b = dhn.astype(BF16)
            e_e, dec_e = cm['e_e'][:, cols], cm['dec_e'][:, cols]
            y_scan = y_ref[:, cols] - cm['dsk_e'][:, cols] * xs
            dye_b = (dyf * e_e).astype(BF16)
            dcg = _dot(dye_b, hprev_b, NN)
            dhp = _dot(dye_b, cg, TN)
            bdh = _dot(bg, dhn_b, NT)
            dbg = _dot((xdt * dec_e).astype(BF16), dhn_b, NN)
            dx_diag = jnp.concatenate(dx_parts, axis=1)
            dx = dec_e * bdh + dx_diag
            ds_b = dscores.astype(BF16)
            dcg = dcg + _dot(ds_b, bg, NN)
            dbg = dbg + _dot(ds_b, cg, TN)
            x_rounded = xdt_s[...].astype(F32)
            sums = sums + _hilo_dot(jnp.concatenate([dyf * y_scan, xdt * bdh, x_rounded * dx_diag, dx * xs, dyf * xs], axis=0), seg_sum)
            state_dot = state_dot + jnp.sum(_hilo_dot(dhn * hprev, seg_sum, TN), axis=0, keepdims=True)
            dxc_ref[:, cols] = (dx * cm['dt_e'][:, cols] + cm['dsk_e'][:, cols] * dyf).astype(dxc_ref.dtype)
            dxc_ref[:, d_inner + g * nst:d_inner + (g + 1) * nst] = dbg.astype(dxc_ref.dtype)
            dxc_ref[:, d_inner + gn + g * nst:d_inner + gn + (g + 1) * nst] = dcg.astype(dxc_ref.dtype)
            for hh in range(hpg):
                h = g * hpg + hh
                rows = slice(hh * p, (hh + 1) * p)
                dstate[g, rows, :] = jnp.exp(cm['a_last'][:, h:h + 1]) * dhn[rows] + dhp[rows]
        s_y, s_end, s_diag, s_dt, s_skip = (sums[k * q:(k + 1) * q] for k in range(5))
        dt, a, dec_end = cm['dt'], cm['a'], cm['dec_end']
        last_row = (lax.broadcasted_iota(jnp.int32, (q, 1), 0) == q - 1).astype(F32)
        da_last = jnp.sum(dec_end * s_end, axis=0, keepdims=True) + jnp.exp(cm['a_last']) * state_dot
        da = s_y - dec_end * s_end - s_diag + last_row * da_last
        ddta = _dot_hi(cm['triu'], da)
        ddt = s_dt + ddta * a
        d_a = jnp.sum(ddta * dt, axis=0, keepdims=True)
        ddt_raw = ddt * _sigmoid(dtraw_ref[...] + dtb_ref[...])
        ddtraw_ref[...] = ddt_raw
        ddtb_ref[...] += jnp.sum(ddt_raw, axis=0, keepdims=True)
        dalog_ref[...] += d_a * a
        ddsk_ref[...] += jnp.sum(s_skip, axis=0, keepdims=True)

    def rowmap(b, c):
        return b * nc + (nc - 1 - c)
    vec = pl.BlockSpec((1, LANES), lambda b, c: (0, 0))
    vec_shape = jax.ShapeDtypeStruct((1, LANES), F32)
    return _hosted_call(
        body,
        out_shape=[jax.ShapeDtypeStruct((t, c_dim), BF16), jax.ShapeDtypeStruct((t, LANES), F32), vec_shape, vec_shape, vec_shape],
        grid=(bl, nc),
        in_specs=[pl.BlockSpec((q, d_inner), lambda b, c: (rowmap(b, c), 0)),
                  pl.BlockSpec((q, d_inner), lambda b, c: (rowmap(b, c), 0)),
                  pl.BlockSpec((q, d_inner), lambda b, c: (rowmap(b, c), 0)),
                  pl.BlockSpec((q, gn), lambda b, c: (rowmap(b, c), d_inner // gn)),
                  pl.BlockSpec((q, gn), lambda b, c: (rowmap(b, c), d_inner // gn + 1)),
                  pl.BlockSpec((q, LANES), lambda b, c: (rowmap(b, c), 0)),
                  pl.BlockSpec((None, grp, gw, nst), lambda b, c: (rowmap(b, c), 0, 0, 0)), vec, vec, vec],
        out_specs=[pl.BlockSpec((q, c_dim), lambda b, c: (rowmap(b, c), 0)),
                   pl.BlockSpec((q, LANES), lambda b, c: (rowmap(b, c), 0)), vec, vec, vec],
        scratch_shapes=[pltpu.VMEM((grp, gw, nst), F32), pltpu.VMEM((q, hb * q), BF16),
                        pltpu.VMEM((hb * q, hb * p), BF16), pltpu.VMEM((hb * q, hb * p), BF16), pltpu.VMEM((q, gw), BF16)],
        operands=[dy, y, xc, xc, xc, dt_raw, hprev_all, dt_bias, a_log, d_skip], side=side, name=name)


def _bucket_onehot():
    blk = CHUNK
    qi = jnp.arange(blk)[:, None]
    kj = jnp.arange(2 * blk)[None, :]
    dist = jnp.maximum(qi + blk - kj, 0)
    max_exact = REL_BUCKETS // 2
    d = jnp.maximum(dist, 1).astype(F32)
    large = max_exact + (jnp.log(d / max_exact) / math.log(REL_MAX_DISTANCE / max_exact) * (REL_BUCKETS - max_exact)).astype(jnp.int32)
    large = jnp.minimum(large, REL_BUCKETS - 1)
    bucket = jnp.where(dist < max_exact, dist, large).reshape(-1)
    return (bucket[None, :] == jnp.arange(REL_BUCKETS)[:, None]).astype(F32)


def _small_mm_hi(a, b, dims, *, name):
    def body(a_ref, b_ref, o_ref):
        o_ref[...] = _dot_hi(a_ref[...], b_ref[...], dims)
    n = b.shape[0] if dims == NT else b.shape[1]
    return pl.pallas_call(body, out_shape=jax.ShapeDtypeStruct((a.shape[0], n), F32), name=name)(a, b)


def _attn_band_mask_t(n, rep):
    blk = CHUNK
    jj = lax.broadcasted_iota(jnp.int32, (2 * blk, rep * blk), 0)
    ii = lax.broadcasted_iota(jnp.int32, (2 * blk, rep * blk), 1) & (blk - 1)
    dist = ii + blk - jj
    in_window = jnp.logical_and(dist >= 0, dist < blk)
    return jnp.logical_and(in_window, jnp.logical_or(jj >= blk, n > 0))


def _sink_row(sink_ref, heads):
    return jnp.concatenate([jnp.broadcast_to(sink_ref[:, h:h + 1], (1, CHUNK)) for h in heads], axis=1)


def _attn_fwd(q, kv, bias_t, sinks, bl, *, name):
    t, qd = q.shape
    blk, hd = CHUNK, ATTN_HEAD_DIM
    kvd = ATTN_KV_HEADS * hd
    rep = ATTN_Q_HEADS // ATTN_KV_HEADS
    nb = t // bl // blk
    scale = hd ** -0.5

    def body(q_ref, kp_ref, kc_ref, vp_ref, vc_ref, bias_ref, sink_ref, o_ref, lse_ref):
        n = pl.program_id(1)
        mask = _attn_band_mask_t(n, rep)
        for kvh in range(ATTN_KV_HEADS):
            ks = slice(kvh * hd, (kvh + 1) * hd)
            heads = range(kvh * rep, (kvh + 1) * rep)
            qs = jnp.concatenate([q_ref[:, h * hd:(h + 1) * hd] for h in heads], axis=0)
            kk = jnp.concatenate([kp_ref[:, ks], kc_ref[:, ks]], axis=0)
            vv = jnp.concatenate([vp_ref[:, ks], vc_ref[:, ks]], axis=0)
            s = jnp.where(mask, _dot(kk, qs, NT) * scale + bias_ref[kvh], NEG_INF)
            sink = _sink_row(sink_ref, heads)
            m = jnp.maximum(jnp.max(s, axis=0, keepdims=True), sink)
            p = jnp.exp(s - m)
            den = jnp.sum(p, axis=0, keepdims=True) + jnp.exp(sink - m)
            o = _dot((p * (1.0 / den)).astype(BF16), vv, TN)
            lse = m + jnp.log(den)
            for r, h in enumerate(heads):
                o_ref[:, h * hd:(h + 1) * hd] = o[r * blk:(r + 1) * blk].astype(o_ref.dtype)
                lse_ref[h:h + 1, :] = lse[:, r * blk:(r + 1) * blk]

    def cur(b, n):
        return b * nb + n

    def prev(b, n):
        return b * nb + jnp.maximum(n - 1, 0)
    return pl.pallas_call(
        body, out_shape=(jax.ShapeDtypeStruct((t, qd), BF16), jax.ShapeDtypeStruct((t // blk * ATTN_Q_HEADS, blk), F32)), grid=(bl, nb),
        in_specs=[pl.BlockSpec((blk, qd), lambda b, n: (cur(b, n), 0)),
                  pl.BlockSpec((blk, kvd), lambda b, n: (prev(b, n), 0)), pl.BlockSpec((blk, kvd), lambda b, n: (cur(b, n), 0)),
                  pl.BlockSpec((blk, kvd), lambda b, n: (prev(b, n), 1)), pl.BlockSpec((blk, kvd), lambda b, n: (cur(b, n), 1)),
                  _const_spec(bias_t.shape), _const_spec((1, LANES))],
        out_specs=(pl.BlockSpec((blk, qd), lambda b, n: (cur(b, n), 0)),
                   pl.BlockSpec((ATTN_Q_HEADS, blk), lambda b, n: (cur(b, n), 0))),
        name=name, compiler_params=_params("parallel", "arbitrary"))(q, kv, kv, kv, kv, bias_t, sinks)


def _attn_bwd(do, q, kv, lse, bias_t, sinks, bl, *, name):
    t, qd = q.shape
    blk, hd = CHUNK, ATTN_HEAD_DIM
    kvd = ATTN_KV_HEADS * hd
    rep = ATTN_Q_HEADS // ATTN_KV_HEADS
    s_len = t // bl
    nb = s_len // blk
    scale = hd ** -0.5

    def body(do_ref, q_ref, kp_ref, kc_ref, vp_ref, vc_ref, lse_ref, bias_ref, sink_ref, dq_ref, dkv_ref, dbias_ref, dsink_ref):
        n = pl.program_id(1)

        @pl.when(jnp.logical_and(pl.program_id(0) == 0, n == 0))
        def _():
            dbias_ref[...] = jnp.zeros_like(dbias_ref)
            dsink_ref[...] = jnp.zeros_like(dsink_ref)

        mask = _attn_band_mask_t(n, rep)
        r_cur = pl.multiple_of(n * blk, blk)
        r_prev = pl.multiple_of(jnp.maximum(n - 1, 0) * blk, blk)
        dsink = jnp.zeros((1, LANES), F32)
        for kvh in range(ATTN_KV_HEADS):
            ks = slice(kvh * hd, (kvh + 1) * hd)
            heads = range(kvh * rep, (kvh + 1) * rep)
            qs = jnp.concatenate([q_ref[:, h * hd:(h + 1) * hd] for h in heads], axis=0)
            dos = jnp.concatenate([do_ref[:, h * hd:(h + 1) * hd] for h in heads], axis=0)
            kk = jnp.concatenate([kp_ref[:, ks], kc_ref[:, ks]], axis=0)
            vv = jnp.concatenate([vp_ref[:, ks], vc_ref[:, ks]], axis=0)
            lse = jnp.concatenate([lse_ref[h:h + 1, :] for h in heads], axis=1)
            p = jnp.exp(jnp.where(mask, _dot(kk, qs, NT) * scale + bias_ref[kvh], NEG_INF) - lse)
            dp = _dot(vv, dos, NT)
            delta = jnp.sum(p * dp, axis=0, keepdims=True)
            ds = p * (dp - delta)
            dsink_row = jnp.exp(_sink_row(sink_ref, heads) - lse) * delta
            dbias_ref[kvh] += ds
            ds_b = ds.astype(BF16)
            dq_s = _dot(ds_b, kk, TN) * scale
            dkk = _dot(ds_b, qs, NN) * scale
            dvv = _dot(p.astype(BF16), dos, NN)
            for r, h in enumerate(heads):
                dq_ref[:, h * hd:(h + 1) * hd] = dq_s[r * blk:(r + 1) * blk].astype(dq_ref.dtype)
                dsink = dsink - jnp.sum(dsink_row[:, r * blk:(r + 1) * blk], axis=1, keepdims=True) * _lane_onehot(h)
            vs = slice(kvd + kvh * hd, kvd + (kvh + 1) * hd)
            dkv_ref[pl.ds(r_cur, blk), ks] = dkk[blk:]
            dkv_ref[pl.ds(r_cur, blk), vs] = dvv[blk:]

            @pl.when(n > 0)
            def _():
                dkv_ref[pl.ds(r_prev, blk), ks] += dkk[:blk]
                dkv_ref[pl.ds(r_prev, blk), vs] += dvv[:blk]
        dsink_ref[...] += dsink

    def cur(b, n):
        return b * nb + n

    def prev(b, n):
        return b * nb + jnp.maximum(n - 1, 0)
    qspec = pl.BlockSpec((blk, qd), lambda b, n: (cur(b, n), 0))
    return pl.pallas_call(
        body,
        out_shape=(jax.ShapeDtypeStruct((t, qd), BF16), jax.ShapeDtypeStruct((t, 2 * kvd), F32),
                   jax.ShapeDtypeStruct(bias_t.shape, F32), jax.ShapeDtypeStruct((1, LANES), F32)),
        grid=(bl, nb),
        in_specs=[qspec, qspec,
                  pl.BlockSpec((blk, kvd), lambda b, n: (prev(b, n), 0)), pl.BlockSpec((blk, kvd), lambda b, n: (cur(b, n), 0)),
                  pl.BlockSpec((blk, kvd), lambda b, n: (prev(b, n), 1)), pl.BlockSpec((blk, kvd), lambda b, n: (cur(b, n), 1)),
                  pl.BlockSpec((ATTN_Q_HEADS, blk), lambda b, n: (cur(b, n), 0)), _const_spec(bias_t.shape), _const_spec((1, LANES))],
        out_specs=(qspec, pl.BlockSpec((s_len, 2 * kvd), lambda b, n: (b, 0)), _const_spec(bias_t.shape), _const_spec((1, LANES))),
        name=name, compiler_params=_params("arbitrary", "arbitrary"))(do, q, kv, kv, kv, kv, lse, bias_t, sinks)


def _merge_fwd(y, z, ng, o, gs, ga, w_ssm, w_attn, w_out, h_in, g_post, g_next, *, name):
    t, d = h_in.shape
    tm = _tile(t, 256)
    d_ssm = y.shape[1]
    gw = d_ssm // SSM_GROUPS

    def body(y_ref, z_ref, ng_ref, o_ref, gs_ref, ga_ref, ws_ref, wa_ref, wo_ref, hin_ref, gp_ref, gn_ref,
             yn_ref, ys_ref, ya_ref, mg_ref, mix_ref, hout_ref, next_ref):
        for g in range(SSM_GROUPS):
            sl = slice(g * gw, (g + 1) * gw)
            zv = z_ref[:, sl].astype(F32)
            yg = y_ref[:, sl] * (zv * _sigmoid(zv))
            r = lax.rsqrt(jnp.mean(yg * yg, axis=-1, keepdims=True) + RMS_EPS)
            yn_ref[:, sl] = (yg * r * ng_ref[:, sl]).astype(BF16)
        ys = _dot(yn_ref[...], ws_ref[...], NN)
        ya = _dot(o_ref[...], wa_ref[...], NN)
        merged = (_sigmoid(gs_ref[...].astype(F32)) * ys + _sigmoid(ga_ref[...].astype(F32)) * ya).astype(BF16)
        mix = _dot(merged, wo_ref[...], NN)
        ys_ref[...] = ys.astype(BF16)
        ya_ref[...] = ya.astype(BF16)
        mg_ref[...] = merged
        mix_ref[...] = mix
        hout = _rms_residual(mix, hin_ref[...], gp_ref[...], 1.0)
        hout_ref[...] = hout
        rn = lax.rsqrt(jnp.mean(hout * hout, axis=-1, keepdims=True) + RMS_EPS)
        next_ref[...] = (hout * rn * gn_ref[...]).astype(BF16)

    def row(w):
        return pl.BlockSpec((tm, w), lambda i: (i, 0))
    bshape = jax.ShapeDtypeStruct((t, d), BF16)
    fshape = jax.ShapeDtypeStruct((t, d), F32)
    return pl.pallas_call(
        body, out_shape=(jax.ShapeDtypeStruct((t, d_ssm), BF16), bshape, bshape, bshape, fshape, fshape, bshape), grid=(t // tm,),
        in_specs=[row(d_ssm), row(d_ssm), _const_spec((1, d_ssm)), row(o.shape[1]), row(d), row(d), _resident_spec(w_ssm.shape),
                  _resident_spec(w_attn.shape), _resident_spec(w_out.shape), row(d), _const_spec((1, d)), _const_spec((1, d))],
        out_specs=(row(d_ssm),) + (row(d),) * 6, name=name,
        compiler_params=_params("parallel"))(y, z, ng, o, gs, ga, w_ssm, w_attn, w_out, h_in, g_post, g_next)


def _merge_bwd(dh, mix, g_post, gs, ga, ys, ya, y, z, ng, w_ssm, w_attn, w_out, *, name):
    t, d = mix.shape
    tm = _tile(t, 256)
    d_ssm, d_attn = w_ssm.shape[0], w_attn.shape[0]
    gw = d_ssm // SSM_GROUPS

    def body(dh_ref, mix_ref, gp_ref, gs_ref, ga_ref, ys_ref, ya_ref, y_ref, z_ref, ng_ref, ws_ref, wa_ref, wo_ref,
             dmix_ref, dys_ref, dya_ref, dgs_ref, dga_ref, dy_ref, dz_ref, do_ref, dgp_ref, dng_ref):
        @pl.when(pl.program_id(0) == 0)
        def _():
            dgp_ref[...] = jnp.zeros_like(dgp_ref)
            dng_ref[...] = jnp.zeros_like(dng_ref)
        mv = mix_ref[...]
        dy = dh_ref[...]
        r = lax.rsqrt(jnp.mean(mv * mv, axis=-1, keepdims=True) + RMS_EPS)
        mhat = mv * r
        dyg = dy * gp_ref[...]
        dmix = (r * (dyg - mhat * jnp.mean(dyg * mhat, axis=-1, keepdims=True))).astype(BF16)
        dgp_ref[...] += jnp.sum(dy * mhat, axis=0, keepdims=True)
        dmix_ref[...] = dmix
        dmerged = _dot(dmix, wo_ref[...], NT)
        sgs = _sigmoid(gs_ref[...].astype(F32))
        sga = _sigmoid(ga_ref[...].astype(F32))
        dys = (dmerged * sgs).astype(BF16)
        dya = (dmerged * sga).astype(BF16)
        dys_ref[...] = dys
        dya_ref[...] = dya
        dgs_ref[...] = (dmerged * ys_ref[...].astype(F32) * sgs * (1.0 - sgs)).astype(BF16)
        dga_ref[...] = (dmerged * ya_ref[...].astype(F32) * sga * (1.0 - sga)).astype(BF16)
        do_ref[...] = _dot(dya, wa_ref[...], NT).astype(BF16)
        dyn = _dot(dys, ws_ref[...], NT)
        for g in range(SSM_GROUPS):
            sl = slice(g * gw, (g + 1) * gw)
            zv = z_ref[:, sl].astype(F32)
            yv = y_ref[:, sl]
            sg = _sigmoid(zv)
            sz = zv * sg
            yg = yv * sz
            rg = lax.rsqrt(jnp.mean(yg * yg, axis=-1, keepdims=True) + RMS_EPS)
            yhat = yg * rg
            dn = dyn[:, sl]
            dyg_n = dn * ng_ref[:, sl]
            dyg_g = rg * (dyg_n - yhat * jnp.mean(dyg_n * yhat, axis=-1, keepdims=True))
            dy_ref[:, sl] = (dyg_g * sz).astype(BF16)
            dz_ref[:, sl] = (dyg_g * yv * (sg * (1.0 + zv * (1.0 - sg)))).astype(BF16)
            dng_ref[:, sl] += jnp.sum(dn * yhat, axis=0, keepdims=True)

    def row(w):
        return pl.BlockSpec((tm, w), lambda i: (i, 0))

    def bshape(w):
        return jax.ShapeDtypeStruct((t, w), BF16)
    return pl.pallas_call(
        body, out_shape=(bshape(d),) * 5 + (bshape(d_ssm), bshape(d_ssm), bshape(d_attn), jax.ShapeDtypeStruct((1, d), F32),
                                            jax.ShapeDtypeStruct((1, d_ssm), F32)), grid=(t // tm,),
        in_specs=[row(d), row(d), _const_spec((1, d)), row(d), row(d), row(d), row(d), row(d_ssm), row(d_ssm), _const_spec((1, d_ssm)),
                  _resident_spec(w_ssm.shape), _resident_spec(w_attn.shape), _resident_spec(w_out.shape)],
        out_specs=(row(d),) * 5 + (row(d_ssm), row(d_ssm), row(d_attn), _const_spec((1, d)), _const_spec((1, d_ssm))),
        name=name, compiler_params=_params("arbitrary"))(dh, mix, g_post, gs, ga, ys, ya, y, z, ng, w_ssm, w_attn, w_out)


def _position():
    return lax.axis_index("x"), lax.axis_index("y"), lax.axis_index("c")


def _gather_exchange(shards):
    na = len(shards)

    def plan(ins, outs, sems):
        send_sems, recv_sems, local_sems = sems
        x, y, c = _position()
        me, sibling = (x, y, c), (x, y, 1 - c)
        chips = [(1 - x, y), (x, 1 - y), (1 - x, 1 - y)]

        def slot(a, pos):
            return outs[a].at[4 * pos[0] + 2 * pos[1] + pos[2]]

        def copy(a, k, block, to, src=None):
            return pltpu.make_async_remote_copy(
                src_ref=slot(a, block) if src is None else src, dst_ref=slot(a, block),
                send_sem=send_sems.at[a, k], recv_sem=recv_sems.at[a, k], device_id=to, device_id_type=MESH)

        mine = [pltpu.make_async_copy(ins[a], slot(a, me), local_sems.at[a]) for a in range(na)]
        first = []
        for a in range(na):
            first.append(copy(a, 0, me, sibling, src=ins[a]))
            first += [copy(a, 1 + j, me, (*chip, c), src=ins[a]) for j, chip in enumerate(chips)]
        return me, sibling, chips, copy, mine, first

    def start(ins, outs, sems):
        *_, mine, first = plan(ins, outs, sems)
        for cp in mine + first:
            cp.start()

    def finish(ins, outs, sems):
        me, sibling, chips, copy, mine, first = plan(ins, outs, sems)
        c = me[2]
        passed = []
        for a in range(na):
            for j, chip in enumerate(chips):
                copy(a, 1 + j, (*chip, c), me).wait_recv()
                fwd = copy(a, 4 + j, (*chip, c), sibling)
                fwd.start()
                passed.append(fwd)
        for a in range(na):
            copy(a, 0, sibling, me).wait_recv()
            for j, chip in enumerate(chips):
                copy(a, 4 + j, (*chip, 1 - c), me).wait_recv()
        for cp in first + passed:
            cp.wait_send()
        for cp in mine:
            cp.wait()

    return _Exchange(list(shards), [jax.ShapeDtypeStruct((N_DEV,) + s.shape, s.dtype) for s in shards],
                     [pltpu.SemaphoreType.DMA((na, 7)), pltpu.SemaphoreType.DMA((na, 7)), pltpu.SemaphoreType.DMA((na,))],
                     start, finish)


def _scatter_exchange(arrays):
    na = len(arrays)

    def copies(ins, outs, sems):
        send_sems, recv_sems = sems
        x, y, c = _position()
        out = []
        for a in range(na):
            for k in range(7):
                flip = k + 1
                peer = (x ^ (flip >> 2), y ^ ((flip >> 1) & 1), c ^ (flip & 1))
                peer_block = 4 * peer[0] + 2 * peer[1] + peer[2]
                out.append(pltpu.make_async_remote_copy(
                    src_ref=ins[a].at[peer_block], dst_ref=outs[a].at[k],
                    send_sem=send_sems.at[a, k], recv_sem=recv_sems.at[a, k], device_id=peer, device_id_type=MESH))
        return out

    def start(ins, outs, sems):
        for cp in copies(ins, outs, sems):
            cp.start()

    def finish(ins, outs, sems):
        for cp in copies(ins, outs, sems):
            cp.wait()

    return _Exchange(list(arrays), [jax.ShapeDtypeStruct((7,) + s.shape[1:], s.dtype) for s in arrays],
                     [pltpu.SemaphoreType.DMA((na, 7)), pltpu.SemaphoreType.DMA((na, 7))], start, finish)


def _reduce_adamw(own, recv, w, m, v, *, name):
    r, c = own.shape
    tm = _tile(r, 256)
    c1 = 1.0 - ADAM_B1 ** ADAM_STEP
    c2 = 1.0 - ADAM_B2 ** ADAM_STEP

    def body(own_ref, recv_ref, w_ref, m_ref, v_ref, g_ref, d_ref, mo_ref, vo_ref):
        gv = own_ref[...]
        for k in range(7):
            gv = gv + recv_ref[k].astype(F32)
        g_ref[...] = gv
        mn = ADAM_B1 * m_ref[...] + (1.0 - ADAM_B1) * gv
        vn = ADAM_B2 * v_ref[...] + (1.0 - ADAM_B2) * (gv * gv)
        mo_ref[...] = mn
        vo_ref[...] = vn
        d_ref[...] = -ADAM_LR * ((mn / c1) / (jnp.sqrt(vn / c2) + ADAM_EPS) + ADAM_WD * w_ref[...])

    blk = pl.BlockSpec((tm, c), lambda i: (i, 0))
    shp = jax.ShapeDtypeStruct((r, c), F32)
    return pl.pallas_call(body, out_shape=(shp,) * 4, grid=(r // tm,),
                          in_specs=[blk, pl.BlockSpec((7, tm, c), lambda i: (0, i, 0)), blk, blk, blk], out_specs=(blk,) * 4,
                          name=name, compiler_params=_params("parallel"))(own, recv, w, m, v)


SMALL_ROW = 8 * LANES


def _update_replicated(partials, ws, ms, vs, loss_part, *, name):
    n = len(ws)
    pieces, n_rows = [], 0
    for k in sorted(range(n), key=lambda k_: -ws[k_].shape[0]):
        r, c = ws[k].shape
        assert (r == 1 or c <= SMALL_ROW) and (r == 1 or n_rows % 8 == 0)
        for c0 in range(0, c, SMALL_ROW):
            pieces.append((k, slice(0, r), slice(c0, min(c0 + SMALL_ROW, c)), n_rows))
            n_rows += r
    loss_row = n_rows
    n_rows = -(-(n_rows + 1) // 8) * 8
    c1 = 1.0 - ADAM_B1 ** ADAM_STEP
    c2 = 1.0 - ADAM_B2 ** ADAM_STEP

    def reduce_body(*refs):
        g_in, loss_ref, total, buf, send_sems, recv_sems = refs[:n], refs[n], refs[n + 1], refs[n + 2], refs[n + 3], refs[n + 4]
        x, y, c_ = _position()
        me = 4 * x + 2 * y + c_
        buf[me] = jnp.zeros((n_rows, SMALL_ROW), F32)
        for k, rs, cs, row0 in pieces:
            buf[me, row0:row0 + rs.stop, 0:cs.stop - cs.start] = g_in[k][rs, cs]
        buf[me, loss_row:loss_row + 1, 0:LANES] = loss_ref[...]
        copies = []
        for j in range(7):
            flip = j + 1
            peer = (x ^ (flip >> 2), y ^ ((flip >> 1) & 1), c_ ^ (flip & 1))
            cp = pltpu.make_async_remote_copy(
                src_ref=buf.at[me], dst_ref=buf.at[me], send_sem=send_sems.at[j], recv_sem=recv_sems.at[j],
                device_id=peer, device_id_type=MESH)
            cp.start()
            copies.append(cp)
        for cp in copies:
            cp.wait()
        acc = buf[0]
        for dev in range(1, N_DEV):
            acc = acc + buf[dev]
        total[...] = acc

    def step_body(*refs):
        total = refs[0]
        w_in_, m_in, v_in = (refs[1 + j * n:1 + (j + 1) * n] for j in range(3))
        g_out, d_out, m_out, v_out = (refs[1 + (3 + j) * n:1 + (4 + j) * n] for j in range(4))
        for k, rs, cs, row0 in pieces:
            gv = total[row0:row0 + rs.stop, 0:cs.stop - cs.start]
            mn = ADAM_B1 * m_in[k][rs, cs] + (1.0 - ADAM_B1) * gv
            vn = ADAM_B2 * v_in[k][rs, cs] + (1.0 - ADAM_B2) * (gv * gv)
            g_out[k][rs, cs] = gv
            m_out[k][rs, cs] = mn
            v_out[k][rs, cs] = vn
            d_out[k][rs, cs] = -ADAM_LR * ((mn / c1) / (jnp.sqrt(vn / c2) + ADAM_EPS) + ADAM_WD * w_in_[k][rs, cs])

    vm = pl.BlockSpec(memory_space=pltpu.VMEM)
    total = pl.pallas_call(
        reduce_body, out_shape=jax.ShapeDtypeStruct((n_rows, SMALL_ROW), F32), in_specs=[vm] * (n + 1), out_specs=vm,
        scratch_shapes=[pltpu.VMEM((N_DEV, n_rows, SMALL_ROW), F32), pltpu.SemaphoreType.DMA((7,)), pltpu.SemaphoreType.DMA((7,))],
        name=f"{name}_allreduce")(*partials, loss_part)
    shapes = tuple(jax.ShapeDtypeStruct(w.shape, F32) for w in ws)
    outs = pl.pallas_call(step_body, out_shape=shapes * 4, in_specs=[vm] * (1 + 3 * n), out_specs=tuple([vm] * (4 * n)),
                          name=f"{name}_adamw")(total, *ws, *ms, *vs)
    return tuple(outs[j * n:(j + 1) * n] for j in range(4)) + (total[loss_row, 0],)


def _pad_lanes(v, width=LANES):
    return jnp.pad(v, ((0, 0), (0, width - v.shape[1])))


def kernel(x, ffn1_pre_g, ffn1_w_gate, ffn1_w_up, ffn1_w_down, ffn1_post_g, mix_pre_g, w_in, conv_w, conv_b, dt_bias, a_log, d_skip, ssm_norm_g, w_ssm_proj, attn_sinks, rel_bias_table, w_attn_proj, w_out, mix_post_g, ffn2_pre_g, ffn2_w_gate, ffn2_w_up, ffn2_w_down, ffn2_post_g, loss_target, m_ffn1_pre_g, m_ffn1_w_gate, m_ffn1_w_up, m_ffn1_w_down, m_ffn1_post_g, m_mix_pre_g, m_w_in, m_conv_w, m_conv_b, m_dt_bias, m_a_log, m_d_skip, m_ssm_norm_g, m_w_ssm_proj, m_attn_sinks, m_rel_bias_table, m_w_attn_proj, m_w_out, m_mix_post_g, m_ffn2_pre_g, m_ffn2_w_gate, m_ffn2_w_up, m_ffn2_w_down, m_ffn2_post_g, v_ffn1_pre_g, v_ffn1_w_gate, v_ffn1_w_up, v_ffn1_w_down, v_ffn1_post_g, v_mix_pre_g, v_w_in, v_conv_w, v_conv_b, v_dt_bias, v_a_log, v_d_skip, v_ssm_norm_g, v_w_ssm_proj, v_attn_sinks, v_rel_bias_table, v_w_attn_proj, v_w_out, v_mix_post_g, v_ffn2_pre_g, v_ffn2_w_gate, v_ffn2_w_up, v_ffn2_w_down, v_ffn2_post_g):
    args = dict(locals())
    weight_names = ['ffn1_pre_g', 'ffn1_w_gate', 'ffn1_w_up', 'ffn1_w_down', 'ffn1_post_g', 'mix_pre_g', 'w_in', 'conv_w', 'conv_b',
                    'dt_bias', 'a_log', 'd_skip', 'ssm_norm_g', 'w_ssm_proj', 'attn_sinks', 'rel_bias_table', 'w_attn_proj', 'w_out',
                    'mix_post_g', 'ffn2_pre_g', 'ffn2_w_gate', 'ffn2_w_up', 'ffn2_w_down', 'ffn2_post_g']
    col_sharded = ('ffn1_w_gate', 'ffn1_w_up', 'w_in', 'ffn2_w_gate', 'ffn2_w_up')
    row_sharded = ('ffn1_w_down', 'w_ssm_proj', 'w_attn_proj', 'w_out', 'ffn2_w_down')
    big = col_sharded + row_sharded

    bl, s_len, d = x.shape
    t = bl * s_len
    d_inner = ssm_norm_g.shape[1]
    n_heads = dt_bias.shape[1]
    gn = SSM_GROUPS * SSM_STATE
    conv_dim = d_inner + 2 * gn
    q_dim = ATTN_Q_HEADS * ATTN_HEAD_DIM
    kv_dim = ATTN_KV_HEADS * ATTN_HEAD_DIM

    def local_2d(name, a):
        a = a[0]
        return a.T if name in col_sharded else a

    ffn1_names = ('ffn1_w_gate', 'ffn1_w_up', 'ffn1_w_down')
    ffn2_names = ('ffn2_w_gate', 'ffn2_w_up', 'ffn2_w_down')
    mixer_names = ('w_ssm_proj', 'w_attn_proj', 'w_out')

    def shard(n):
        return local_2d(n, args[n]).astype(BF16)

    def rows(g):
        return g.reshape(N_DEV * g.shape[1], g.shape[2])

    x2 = x.reshape(t, d)
    tgt2 = loss_target.reshape(t, d)
    full = {}

    (h1,), saved1, ffn1_w, got_in = _ffn_forward(
        x2, ffn1_pre_g, lambda got: (rows(got[0]), rows(got[1])), lambda got: rows(got[2]), ffn1_post_g, "ffn1",
        side_norm=_gather_exchange([shard(n) for n in ffn1_names[:2]]),
        side_up=_gather_exchange([shard('w_in'), conv_w[0], shard('ffn1_w_down')]))
    full.update(zip(ffn1_names, ffn1_w))
    conv_w_full = jnp.transpose(got_in[1], (1, 0, 2)).reshape(SSM_CONV, conv_dim)

    win_t = rows(got_in[0])
    dt_lo = 2 * d + d_inner + conv_dim
    off = {'gs': 0, 'ga': d, 'z': 2 * d, 'xbc': 2 * d + d_inner, 'dt': dt_lo, 'q': dt_lo + n_heads, 'kv': dt_lo + n_heads + q_dim}
    assert all(o_ % 16 == 0 for o_ in off.values())

    u, (gs, ga, z, xbc, q, kv, dt_raw) = _proj_all(
        h1, mix_pre_g, win_t,
        [(off['gs'], d, BF16), (off['ga'], d, BF16), (off['z'], d_inner, BF16), (off['xbc'], conv_dim, BF16),
         (off['q'], q_dim, BF16), (off['kv'], 2 * kv_dim, BF16), (off['dt'], n_heads, F32)], name="mix_proj")

    dtb_p, alog_p, dsk_p, sinks_p = _pad_lanes(dt_bias), _pad_lanes(a_log), _pad_lanes(d_skip), _pad_lanes(attn_sinks)
    xc = _conv_fwd(xbc, conv_w_full, conv_b, bl, name="conv_fwd")
    late_names = mixer_names + ffn2_names
    (y, hprev), got_late = _ssd_fwd(xc, dt_raw, dtb_p, alog_p, dsk_p, bl, n_heads,
                                    side=_gather_exchange([shard(n) for n in late_names]), name="ssd_fwd")
    full.update({n: rows(g) for n, g in zip(late_names, got_late)})

    onehot = _bucket_onehot()
    rep = ATTN_Q_HEADS // ATTN_KV_HEADS
    bias = _small_mm_hi(rel_bias_table.T, onehot, NN, name="rel_bias")
    bias_t = jnp.transpose(bias.reshape(ATTN_KV_HEADS, rep, CHUNK, 2 * CHUNK), (0, 3, 1, 2)).reshape(ATTN_KV_HEADS, 2 * CHUNK, rep * CHUNK)
    o, lse = _attn_fwd(q, kv, bias_t, sinks_p, bl, name="attn_fwd")

    yn, ys, ya, merged, mix, h2, n2 = _merge_fwd(y, z, ssm_norm_g, o, gs, ga, full['w_ssm_proj'], full['w_attn_proj'], full['w_out'],
                                                 h1, mix_post_g, ffn2_pre_g, name="merge_fwd")

    (dh3, loss_vec, df2, dg_post2), saved2, _, _ = _ffn_forward(
        h2, ffn2_pre_g, (full['ffn2_w_gate'], full['ffn2_w_up']), full['ffn2_w_down'], ffn2_post_g, "ffn2", target=tgt2, n=n2)

    grads, own, wire, received = {}, {}, {}, {}
    dh2, grads['ffn2_pre_g'], grads['ffn2_post_g'], g32, g16, _ = _ffn_backward(
        dh3, saved2, ffn2_pre_g, full['ffn2_w_gate'], full['ffn2_w_up'], full['ffn2_w_down'], "ffn2", (df2, dg_post2))
    own.update(zip(ffn2_names, map(_stack8, g32)))
    wire.update(zip(ffn2_names, map(_stack8, g16)))

    dmix, dys, dya, dgs, dga, dy, dz, do, grads['mix_post_g'], grads['ssm_norm_g'] = _merge_bwd(
        dh2, mix, mix_post_g, gs, ga, ys, ya, y, z, ssm_norm_g, full['w_ssm_proj'], full['w_attn_proj'], full['w_out'],
        name="merge_bwd")
    for n, (lhs, rhs) in zip(mixer_names, ((yn, dys), (o, dya), (merged, dmix))):
        g32_, g16_, _ = _mm_tn([lhs], rhs, name=f"d{n}")
        own[n], wire[n] = _stack8(g32_), _stack8(g16_)

    dq, dkv, dbias_t, dsinks = _attn_bwd(do, q, kv, lse, bias_t, sinks_p, bl, name="attn_bwd")
    dbias = jnp.transpose(dbias_t.reshape(ATTN_KV_HEADS, 2 * CHUNK, rep, CHUNK), (0, 2, 3, 1)).reshape(ATTN_Q_HEADS, -1)
    d_table = _small_mm_hi(onehot, dbias, NT, name="rel_bias_bwd")


    first_group = ffn2_names + mixer_names
    (dxc, ddt_raw, ddtb, dalog, ddsk), got = _ssd_bwd(dy, y, xc, dt_raw, hprev, dtb_p, alog_p, dsk_p, bl, n_heads,
                                                      side=_scatter_exchange([wire[n] for n in first_group]), name="ssd_bwd")
    received.update(zip(first_group, got))
    dxbc, dconv_w8, grads['conv_b'] = _conv_bwd(dxc, xbc, conv_w_full, conv_b, bl, name="conv_bwd")

    wide32, wide16, _ = _mm_tn([dgs, dga, dz, dxbc, dq], u, name="dw_in")
    kv32, kv16, _ = _mm_tn([dkv], u, name="dw_in_kv")
    dt32, dt16, _ = _mm_tn([ddt_raw], u, name="dw_in_dt")

    def original_order(wide, kv_part, dt_part):
        return jnp.concatenate([wide[:dt_lo], dt_part[:n_heads], wide[dt_lo:], kv_part], axis=0)
    me = 4 * lax.axis_index("x") + 2 * lax.axis_index("y") + lax.axis_index("c")
    blk_rows = win_t.shape[0] // N_DEV
    wire['w_in'] = _stack8(original_order(wide16, kv16, dt16))
    own_w_in = lax.dynamic_slice_in_dim(original_order(wide32, kv32, dt32), me * blk_rows, blk_rows)
    own['conv_w'] = jnp.transpose(dconv_w8[:SSM_CONV].reshape(SSM_CONV, N_DEV, conv_dim // N_DEV), (1, 0, 2))

    segs = [(g_, 0, off[k_]) for g_, k_ in zip([dgs, dga, dz, dxbc, dq, dkv, ddt_raw], ('gs', 'ga', 'z', 'xbc', 'q', 'kv', 'dt'))]
    dh1, grads['mix_pre_g'], post1, got = _mm_nn_rmsbwd(segs, [win_t], h1, mix_pre_g, dh2, below=(saved1[5], ffn1_post_g), tm=256,
                                                        side=_scatter_exchange([wire['w_in'], own['conv_w']]), name="mix_du")
    received.update(zip(('w_in', 'conv_w'), got))

    dx2, grads['ffn1_pre_g'], grads['ffn1_post_g'], g32, _, got = _ffn_backward(
        dh1, saved1, ffn1_pre_g, full['ffn1_w_gate'], full['ffn1_w_up'], full['ffn1_w_down'], "ffn1", post1, chain=True)
    own.update(zip(ffn1_names, map(_stack8, g32)))
    received.update(zip(ffn1_names, got))

    def own_block(a):
        return lax.dynamic_index_in_dim(a, me, 0, keepdims=False)
    out_g, out_d, out_m, out_v = {}, {}, {}, {}
    for n in big:
        w2, m2, v2 = local_2d(n, args[n]), local_2d(n, args['m_' + n]), local_2d(n, args['v_' + n])
        results = _reduce_adamw(own_w_in if n == 'w_in' else own_block(own[n]), received[n], w2, m2, v2, name=f"update_{n}")
        out_g[n], out_d[n], out_m[n], out_v[n] = ((a.T if n in col_sharded else a)[None] for a in results)
    results = _reduce_adamw(own_block(own['conv_w']), received['conv_w'], conv_w[0], m_conv_w[0], v_conv_w[0], name="update_conv_w")
    out_g['conv_w'], out_d['conv_w'], out_m['conv_w'], out_v['conv_w'] = (a[None] for a in results)

    grads['dt_bias'], grads['a_log'], grads['d_skip'], grads['attn_sinks'], grads['rel_bias_table'] = ddtb, dalog, ddsk, dsinks, d_table
    small = [n for n in weight_names if n not in big and n != 'conv_w']
    *results, loss = _update_replicated([grads[n] for n in small], [args[n] for n in small], [args['m_' + n] for n in small],
                                        [args['v_' + n] for n in small], loss_vec, name="update_replicated")
    for dst, vals in zip((out_g, out_d, out_m, out_v), results):
        dst.update(zip(small, vals))

    grad_x = dx2.reshape(bl, s_len, d)
    return (loss, grad_x, *[out_g[n] for n in weight_names], *[out_d[n] for n in weight_names],
            *[out_m[n] for n in weight_names], *[out_v[n] for n in weight_names])
```

```python
import functools
import math

import numpy as np
import jax
import jax.numpy as jnp
from jax import lax
from jax.experimental import pallas as pl
from jax.experimental.pallas import tpu as pltpu

F32 = jnp.float32
BF16 = jnp.bfloat16
MESH = pl.DeviceIdType.MESH
N_DEV = 8

SSM_HEAD_DIM = 64
SSM_GROUPS = 4
SSM_STATE = 128
SSM_CONV = 4
CHUNK = 128
ATTN_HEAD_DIM = 64
ATTN_Q_HEADS = 16
ATTN_KV_HEADS = 4
REL_BUCKETS = 32
REL_MAX_DISTANCE = 128
RMS_EPS = 1e-6
FFN_RESIDUAL_WEIGHT = 0.5
ADAM_LR, ADAM_B1, ADAM_B2, ADAM_EPS, ADAM_WD, ADAM_STEP = 0.001, 0.9, 0.999, 1e-08, 0.01, 10

LANES = 128
VMEM_LIMIT_BYTES = 56 * 1024 * 1024
FFN_COL_TILE = 1408
SSD_HEAD_BATCH = 2

NEG_INF = float("-inf")


def _params(*sem):
    return pltpu.CompilerParams(dimension_semantics=sem, vmem_limit_bytes=VMEM_LIMIT_BYTES)


def _tile(n, pref, mult=8):
    if n <= pref:
        return n
    t = (pref // mult) * mult
    while t >= mult:
        if n % t == 0:
            return t
        t -= mult
    return n


def _sigmoid(x):
    return 1.0 / (1.0 + jnp.exp(-x))


def _dot(a, b, dims):
    return lax.dot_general(a, b, (dims, ((), ())), preferred_element_type=F32)


NN = ((1,), (0,))
NT = ((1,), (1,))
TN = ((0,), (0,))


def _dot_hi(a, b, dims=NN):
    return lax.dot_general(a, b, (dims, ((), ())), preferred_element_type=F32, precision=lax.Precision.HIGHEST)


def _const_spec(shape):
    nd = len(shape)
    return pl.BlockSpec(shape, lambda *_: (0,) * nd)


def _resident_spec(shape):
    nd = len(shape)
    return pl.BlockSpec(shape, lambda *_: (0,) * nd, pipeline_mode=pl.Buffered(1))


class _Exchange:
    def __init__(self, arrays, out_shape, scratch, start, finish):
        self.arrays, self.out_shape, self.scratch, self.start, self.finish = arrays, out_shape, scratch, start, finish


def _hosted_call(body, *, grid, in_specs, out_specs, out_shape, scratch_shapes, operands, side, name):
    in_specs, out_specs, out_shape, scratch_shapes = list(in_specs), list(out_specs), list(out_shape), list(scratch_shapes)
    sem = ("arbitrary",) * len(grid)
    if side is None:
        outs = pl.pallas_call(body, out_shape=tuple(out_shape), grid=grid, in_specs=in_specs, out_specs=tuple(out_specs),
                              scratch_shapes=scratch_shapes, name=name, compiler_params=_params(*sem))(*operands)
        return tuple(outs), ()
    n_in, n_out, n_scr = len(in_specs), len(out_shape), len(scratch_shapes)
    s_in, s_out = len(side.arrays), len(side.out_shape)

    def wrapped(*refs):
        refs = list(refs)
        main_in, side_in = refs[:n_in], refs[n_in:n_in + s_in]
        o0 = n_in + s_in
        main_out, side_out = refs[o0:o0 + n_out], refs[o0 + n_out:o0 + n_out + s_out]
        c0 = o0 + n_out + s_out
        main_scr, side_scr = refs[c0:c0 + n_scr], refs[c0 + n_scr:]
        ids = [pl.program_id(ax) for ax in range(len(grid))]
        first = functools.reduce(jnp.logical_and, [i == 0 for i in ids])
        last = functools.reduce(jnp.logical_and, [i == g - 1 for i, g in zip(ids, grid)])

        @pl.when(first)
        def _():
            side.start(side_in, side_out, side_scr)

        body(*main_in, *main_out, *main_scr)

        @pl.when(last)
        def _():
            side.finish(side_in, side_out, side_scr)

    hbm = pl.BlockSpec(memory_space=pl.ANY)
    outs = pl.pallas_call(
        wrapped, out_shape=tuple(out_shape + list(side.out_shape)), grid=grid,
        in_specs=in_specs + [hbm] * s_in, out_specs=tuple(out_specs + [hbm] * s_out),
        scratch_shapes=scratch_shapes + list(side.scratch), name=name, compiler_params=_params(*sem))(*operands, *side.arrays)
    return tuple(outs[:n_out]), tuple(outs[n_out:])


def _proj_all(h, g, w, segs, *, name):
    t, d = h.shape
    tm = _tile(t, 512)
    segs = [(row0, wd_, max(wd_, LANES), dt_) for row0, wd_, dt_ in segs]
    assert all(row0 + out_w <= w.shape[0] for row0, _, out_w, _ in segs)

    def body(h_ref, g_ref, w_ref, u_ref, *o_refs):
        hv = h_ref[...]
        r = lax.rsqrt(jnp.mean(hv * hv, axis=-1, keepdims=True) + RMS_EPS)
        uv = (hv * r * g_ref[...]).astype(BF16)
        u_ref[...] = uv
        for (row0, width, out_w, _), o_ref in zip(segs, o_refs):
            for c0, c1 in _col_chunks(out_w, 8 * LANES):
                part = _dot(uv, w_ref[row0 + c0:row0 + c1, :], NT)
                if width < out_w:
                    part = jnp.where(lax.broadcasted_iota(jnp.int32, part.shape, 1) < width, part, 0.0)
                o_ref[:, c0:c1] = part.astype(o_ref.dtype)

    row = pl.BlockSpec((tm, d), lambda i: (i, 0))
    outs = pl.pallas_call(
        body, out_shape=(jax.ShapeDtypeStruct((t, d), BF16),) + tuple(jax.ShapeDtypeStruct((t, ow), dt_) for _, _, ow, dt_ in segs),
        grid=(t // tm,), in_specs=[row, _const_spec((1, d)), _resident_spec(w.shape)],
        out_specs=(row,) + tuple(pl.BlockSpec((tm, ow), lambda i: (i, 0)) for _, _, ow, _ in segs),
        name=name, compiler_params=_params("parallel"))(h, g, w)
    return outs[0], outs[1:]


def _mm_tn(a_list, b, *, tm=1408, tk=2048, side=None, name):
    t, n = b.shape
    tk = _tile(t, tk if len(a_list) == 1 else tk // 2)
    nk = t // tk
    widths = [a.shape[1] for a in a_list]
    tm = _tile(math.gcd(*widths), tm, LANES)
    assert all(w % tm == 0 for w in widths)
    starts = np.cumsum([0] + [w // tm for w in widths])
    nseg = len(a_list)

    def a_spec(s):
        lo, hi = int(starts[s]), int(starts[s + 1])

        def idx(i, k):
            active = jnp.logical_and(i >= lo, i < hi)
            return (jnp.where(active, k, 0), jnp.clip(i - lo, 0, hi - lo - 1))
        return pl.BlockSpec((tk, tm), idx)

    def body(*refs):
        a_refs, b_ref, o_ref, o16_ref, acc = refs[:nseg], refs[nseg], refs[nseg + 1], refs[nseg + 2], refs[nseg + 3]
        i, k = pl.program_id(0), pl.program_id(1)

        @pl.when(k == 0)
        def _():
            acc[...] = jnp.zeros_like(acc)

        bv = b_ref[...].astype(BF16)
        for s in range(nseg):
            lo, hi = int(starts[s]), int(starts[s + 1])

            @pl.when(jnp.logical_and(i >= lo, i < hi))
            def _(s=s):
                acc[...] += _dot(a_refs[s][...].astype(BF16), bv, TN)

        @pl.when(k == nk - 1)
        def _():
            o_ref[...] = acc[...]
            o16_ref[...] = acc[...].astype(BF16)

    rows = int(starts[-1]) * tm
    o_spec = pl.BlockSpec((tm, n), lambda i, k: (i, 0))
    (o32, o16), got = _hosted_call(
        body, out_shape=[jax.ShapeDtypeStruct((rows, n), F32), jax.ShapeDtypeStruct((rows, n), BF16)], grid=(int(starts[-1]), nk),
        in_specs=[a_spec(s) for s in range(nseg)] + [pl.BlockSpec((tk, n), lambda i, k: (k, 0))],
        out_specs=[o_spec, o_spec], scratch_shapes=[pltpu.VMEM((tm, n), F32)], operands=list(a_list) + [b], side=side, name=name)
    return o32, o16, got


def _mm_nn_rmsbwd(segs, weights, x, g, dres, *, below=None, tm=512, side=None, name):
    t, d = x.shape
    tm = _tile(t, tm)
    nseg, nw = len(segs), len(weights)
    n_below = 0 if below is None else 2

    def body(*refs):
        a_refs, w_refs = refs[:nseg], refs[nseg:nseg + nw]
        x_ref, g_ref, dres_ref = refs[nseg + nw:nseg + nw + 3]
        below_refs = refs[nseg + nw + 3:nseg + nw + 3 + n_below]
        dx_ref, dg_ref = refs[nseg + nw + 3 + n_below:nseg + nw + 5 + n_below]

        @pl.when(pl.program_id(0) == 0)
        def _():
            dg_ref[...] = jnp.zeros_like(dg_ref)

        dn = None
        for s, (a, w_idx, row0) in enumerate(segs):
            part = _dot(a_refs[s][...].astype(BF16), w_refs[w_idx][row0:row0 + a.shape[1], :], NN)
            dn = part if dn is None else dn + part
        xv = x_ref[...]
        r = lax.rsqrt(jnp.mean(xv * xv, axis=-1, keepdims=True) + RMS_EPS)
        xhat = xv * r
        dyg = dn * g_ref[...]
        dx = dres_ref[...] + r * (dyg - xhat * jnp.mean(dyg * xhat, axis=-1, keepdims=True))
        dx_ref[...] = dx
        dg_ref[...] += jnp.sum(dn * xhat, axis=0, keepdims=True)
        if below is not None:
            f_ref, gp_ref = below_refs
            df_ref, dgp_ref = refs[nseg + nw + 5 + n_below:]

            @pl.when(pl.program_id(0) == 0)
            def _():
                dgp_ref[...] = jnp.zeros_like(dgp_ref)
            fv = f_ref[...]
            rf = lax.rsqrt(jnp.mean(fv * fv, axis=-1, keepdims=True) + RMS_EPS)
            fhat = fv * rf
            dy = FFN_RESIDUAL_WEIGHT * dx
            dfg = dy * gp_ref[...]
            df_ref[...] = (rf * (dfg - fhat * jnp.mean(dfg * fhat, axis=-1, keepdims=True))).astype(BF16)
            dgp_ref[...] += jnp.sum(dy * fhat, axis=0, keepdims=True)

    row = pl.BlockSpec((tm, d), lambda i: (i, 0))
    vec = _const_spec((1, d))
    in_specs = [pl.BlockSpec((tm, a.shape[1]), lambda i: (i, 0)) for a, _, _ in segs]
    in_specs += [_resident_spec(w.shape) for w in weights] + [row, vec, row] + ([row, vec] if below is not None else [])
    out_specs = [row, vec] + ([row, vec] if below is not None else [])
    out_shape = [jax.ShapeDtypeStruct((t, d), F32), jax.ShapeDtypeStruct((1, d), F32)]
    if below is not None:
        out_shape += [jax.ShapeDtypeStruct((t, d), BF16), jax.ShapeDtypeStruct((1, d), F32)]
    outs, extra = _hosted_call(
        body, grid=(t // tm,), in_specs=in_specs, out_specs=out_specs, out_shape=out_shape, scratch_shapes=[],
        operands=[a for a, _, _ in segs] + list(weights) + [x, g, dres] + (list(below) if below is not None else []),
        side=side, name=name)
    return outs[0], outs[1], (tuple(outs[2:]) if below is not None else None), extra


def _col_chunks(width, chunk=4 * LANES):
    return [(c0, min(c0 + chunk, width)) for c0 in range(0, width, chunk)]


def _rms_fwd(x, g, *, side=None, name):
    t, d = x.shape
    tm = _tile(t, 512)

    def body(x_ref, g_ref, o_ref):
        xv = x_ref[...]
        r = lax.rsqrt(jnp.mean(xv * xv, axis=-1, keepdims=True) + RMS_EPS)
        o_ref[...] = (xv * r * g_ref[...]).astype(o_ref.dtype)

    row = pl.BlockSpec((tm, d), lambda i: (i, 0))
    (n,), got = _hosted_call(body, grid=(t // tm,), in_specs=[row, _const_spec((1, d))], out_specs=[row],
                             out_shape=[jax.ShapeDtypeStruct((t, d), BF16)], scratch_shapes=[], operands=[x, g], side=side, name=name)
    return n, got


def _ffn_up(n, wgt, wut, *, side=None, name):
    t, d = n.shape
    f = wgt.shape[0]
    tm, tn = _tile(t, 1024), _tile(f, FFN_COL_TILE, LANES)

    def body(n_ref, wg_ref, wu_ref, g_ref, u_ref, h_ref):
        nv = n_ref[...]
        gv = _dot(nv, wg_ref[...], NT)
        uv = _dot(nv, wu_ref[...], NT)
        g_ref[...] = gv.astype(BF16)
        u_ref[...] = uv.astype(BF16)
        h_ref[...] = (gv * _sigmoid(gv) * uv).astype(BF16)

    w_spec = pl.BlockSpec((tn, d), lambda j, i: (j, 0))
    o_spec = pl.BlockSpec((tm, tn), lambda j, i: (i, j))
    shp = jax.ShapeDtypeStruct((t, f), BF16)
    return _hosted_call(body, grid=(f // tn, t // tm), in_specs=[pl.BlockSpec((tm, d), lambda j, i: (i, 0)), w_spec, w_spec],
                        out_specs=[o_spec, o_spec, o_spec], out_shape=[shp, shp, shp], scratch_shapes=[], operands=[n, wgt, wut],
                        side=side, name=name)


def _rms_residual(acc, h, gp, weight):
    r = lax.rsqrt(jnp.mean(acc * acc, axis=-1, keepdims=True) + RMS_EPS)
    return h + weight * (acc * r * gp)


def _ffn_down(hid, wd, h_in, gp, *, target=None, side=None, name):
    t, f = hid.shape
    d = wd.shape[1]
    tm = _tile(t, 512)
    row = pl.BlockSpec((tm, d), lambda i: (i, 0))
    shp = jax.ShapeDtypeStruct((t, d), F32)
    in_specs = [pl.BlockSpec((tm, f), lambda i: (i, 0)), _resident_spec((f, d)), row, _const_spec((1, d))]

    if target is None:
        def body(hid_ref, wd_ref, hin_ref, gp_ref, f_ref, hout_ref):
            acc = _dot(hid_ref[...], wd_ref[...], NN)
            f_ref[...] = acc
            hout_ref[...] = _rms_residual(acc, hin_ref[...], gp_ref[...], FFN_RESIDUAL_WEIGHT)

        return _hosted_call(body, grid=(t // tm,), in_specs=in_specs, out_specs=[row, row], out_shape=[shp, shp], scratch_shapes=[],
                            operands=[hid, wd, h_in, gp], side=side, name=name)

    def body_loss(hid_ref, wd_ref, hin_ref, gp_ref, tgt_ref, f_ref, dh_ref, loss_ref, df_ref, dgp_ref):
        @pl.when(pl.program_id(0) == 0)
        def _():
            loss_ref[...] = jnp.zeros_like(loss_ref)
            dgp_ref[...] = jnp.zeros_like(dgp_ref)
        acc = _dot(hid_ref[...], wd_ref[...], NN)
        f_ref[...] = acc
        r = lax.rsqrt(jnp.mean(acc * acc, axis=-1, keepdims=True) + RMS_EPS)
        fhat = acc * r
        e = hin_ref[...] + FFN_RESIDUAL_WEIGHT * (fhat * gp_ref[...]) - tgt_ref[...]
        dh = e * (1.0 / d)
        dh_ref[...] = dh
        per_row = jnp.sum(e * e, axis=1, keepdims=True) * (1.0 / d)
        loss_ref[...] += 0.5 * jnp.sum(per_row, axis=0, keepdims=True)
        dy = FFN_RESIDUAL_WEIGHT * dh
        dyg = dy * gp_ref[...]
        df_ref[...] = (r * (dyg - fhat * jnp.mean(dyg * fhat, axis=-1, keepdims=True))).astype(BF16)
        dgp_ref[...] += jnp.sum(dy * fhat, axis=0, keepdims=True)

    return _hosted_call(body_loss, grid=(t // tm,), in_specs=in_specs + [row],
                        out_specs=[row, row, _const_spec((1, LANES)), row, _const_spec((1, d))],
                        out_shape=[shp, shp, jax.ShapeDtypeStruct((1, LANES), F32), jax.ShapeDtypeStruct((t, d), BF16),
                                   jax.ShapeDtypeStruct((1, d), F32)],
                        scratch_shapes=[], operands=[hid, wd, h_in, gp, target], side=side, name=name)


def _ffn_dhid(df, wd, g, u, *, name):
    t, d = df.shape
    f = wd.shape[0]
    tm, tn = _tile(t, 1024), _tile(f, FFN_COL_TILE, LANES)

    def body(df_ref, wd_ref, g_ref, u_ref, dg_ref, du_ref):
        dh = _dot(df_ref[...], wd_ref[...], NT)
        gv = g_ref[...].astype(F32)
        uv = u_ref[...].astype(F32)
        sg = _sigmoid(gv)
        silu = gv * sg
        dg_ref[...] = (dh * uv * (sg + silu * (1.0 - sg))).astype(BF16)
        du_ref[...] = (dh * silu).astype(BF16)

    o_spec = pl.BlockSpec((tm, tn), lambda j, i: (i, j))
    shp = jax.ShapeDtypeStruct((t, f), BF16)
    return pl.pallas_call(body, out_shape=(shp, shp), grid=(f // tn, t // tm),
                          in_specs=[pl.BlockSpec((tm, d), lambda j, i: (i, 0)), pl.BlockSpec((tn, d), lambda j, i: (j, 0)), o_spec, o_spec],
                          out_specs=(o_spec, o_spec), name=name, compiler_params=_params("parallel", "arbitrary"))(df, wd, g, u)


def _ffn_forward(h_in, g_pre, w_up, wd, g_post, tag, side_norm=None, side_up=None, target=None, n=None):
    got_norm = ()
    if n is None:
        n, got_norm = _rms_fwd(h_in, g_pre, side=side_norm, name=f"{tag}_prenorm")
    wgt, wut = w_up(got_norm) if callable(w_up) else w_up
    (g, u, hid), got_up = _ffn_up(n, wgt, wut, side=side_up, name=f"{tag}_up")
    wd = wd(got_up) if callable(wd) else wd
    outs, _ = _ffn_down(hid, wd, h_in, g_post, target=target, name=f"{tag}_down")
    return outs[1:], (h_in, n, g, u, hid, outs[0]), (wgt, wut, wd), got_up


def _stack8(g):
    return g.reshape(N_DEV, g.shape[0] // N_DEV, g.shape[1])


def _ffn_backward(dh_out, saved, g_pre, wgt, wut, wd, tag, post, chain=False):
    h_in, n, g, u, hid, _ = saved

    def side_of(grad16):
        return _scatter_exchange([_stack8(grad16)]) if chain else None

    df, dg_post = post
    dgate, dup = _ffn_dhid(df, wd, g, u, name=f"{tag}_dhid")
    d_wd, d_wd16, _ = _mm_tn([hid], df, name=f"{tag}_dwd")
    d_wgt, d_wgt16, got_wd = _mm_tn([dgate], n, side=side_of(d_wd16), name=f"{tag}_dwg")
    d_wut, d_wut16, got_wg = _mm_tn([dup], n, side=side_of(d_wgt16), name=f"{tag}_dwu")
    dh_in, dg_pre, _, got_wu = _mm_nn_rmsbwd([(dgate, 0, 0), (dup, 1, 0)], [wgt, wut], h_in, g_pre, dh_out, side=side_of(d_wut16),
                                             name=f"{tag}_dn")
    received = (got_wg[0], got_wu[0], got_wd[0]) if chain else None
    return dh_in, dg_pre, dg_post, (d_wgt, d_wut, d_wd), (d_wgt16, d_wut16, d_wd16), received


CONV_ROWS = 128
HALO = 8


def _taps(w_ref):
    return [w_ref[k:k + 1, :] for k in range(SSM_CONV)]


def _conv_chunk(x_ref, xs, r0, taps, bias):
    xs[HALO + r0:HALO + r0 + CONV_ROWS, :] = x_ref[r0:r0 + CONV_ROWS, :].astype(F32)
    shifted = [xs[HALO + r0 - k:HALO + r0 - k + CONV_ROWS, :] for k in range(SSM_CONV)]
    pre = bias + shifted[0] * taps[SSM_CONV - 1]
    for k in range(1, SSM_CONV):
        pre = pre + shifted[k] * taps[SSM_CONV - 1 - k]
    return shifted, pre


def _fold_rows(a):
    return functools.reduce(jnp.add, [a[i:i + 8] for i in range(0, a.shape[0], 8)])


def _conv_fwd(xbc, conv_w, conv_b, bl, *, name):
    t, c = xbc.shape
    s = t // bl
    tc = LANES
    assert s % CONV_ROWS == 0

    def body(x_ref, w_ref, b_ref, o_ref, xs):
        taps, bias = _taps(w_ref), b_ref[...]
        xs[0:HALO, :] = jnp.zeros((HALO, tc), F32)
        for r0 in range(0, s, CONV_ROWS):
            _, pre = _conv_chunk(x_ref, xs, r0, taps, bias)
            o_ref[r0:r0 + CONV_ROWS, :] = (pre * _sigmoid(pre)).astype(o_ref.dtype)

    blk = pl.BlockSpec((s, tc), lambda b, j: (b, j))
    return pl.pallas_call(body, out_shape=jax.ShapeDtypeStruct((t, c), BF16), grid=(bl, c // tc),
                          in_specs=[blk, pl.BlockSpec((SSM_CONV, tc), lambda b, j: (0, j)), pl.BlockSpec((1, tc), lambda b, j: (0, j))],
                          out_specs=blk, scratch_shapes=[pltpu.VMEM((HALO + s, tc), F32)],
                          name=name, compiler_params=_params("parallel", "arbitrary"))(xbc, conv_w, conv_b)


def _conv_bwd(dxc, xbc, conv_w, conv_b, bl, *, name):
    t, c = xbc.shape
    s = t // bl
    tc = LANES

    def body(dy_ref, x_ref, w_ref, b_ref, dx_ref, dw_ref, db_ref, xs, dpre_s):
        @pl.when(pl.program_id(1) == 0)
        def _():
            dw_ref[...] = jnp.zeros_like(dw_ref)
            db_ref[...] = jnp.zeros_like(db_ref)

        taps, bias = _taps(w_ref), b_ref[...]
        zero8 = jnp.zeros((HALO, tc), F32)
        xs[0:HALO, :] = zero8
        dpre_s[s:s + HALO, :] = zero8
        sums = [zero8] * (SSM_CONV + 1)
        for r0 in range(0, s, CONV_ROWS):
            shifted, pre = _conv_chunk(x_ref, xs, r0, taps, bias)
            sg = _sigmoid(pre)
            dpre = dy_ref[r0:r0 + CONV_ROWS, :].astype(F32) * (sg * (1.0 + pre * (1.0 - sg)))
            dpre_s[r0:r0 + CONV_ROWS, :] = dpre
            sums = [acc + _fold_rows(dpre * sh) for acc, sh in zip(sums[:-1], shifted)] + [sums[-1] + _fold_rows(dpre)]
        for k in range(SSM_CONV):
            dw_ref[SSM_CONV - 1 - k:SSM_CONV - k, :] += jnp.sum(sums[k], axis=0, keepdims=True)
        db_ref[...] += jnp.sum(sums[-1], axis=0, keepdims=True)
        for r0 in range(0, s, CONV_ROWS):
            dx = dpre_s[r0:r0 + CONV_ROWS, :] * taps[SSM_CONV - 1]
            for k in range(1, SSM_CONV):
                dx = dx + dpre_s[r0 + k:r0 + k + CONV_ROWS, :] * taps[SSM_CONV - 1 - k]
            dx_ref[r0:r0 + CONV_ROWS, :] = dx.astype(dx_ref.dtype)

    blk = pl.BlockSpec((s, tc), lambda j, b: (b, j))
    return pl.pallas_call(
        body, out_shape=(jax.ShapeDtypeStruct((t, c), BF16), jax.ShapeDtypeStruct((8, c), F32), jax.ShapeDtypeStruct((1, c), F32)),
        grid=(c // tc, bl),
        in_specs=[blk, blk, pl.BlockSpec((SSM_CONV, tc), lambda j, b: (0, j)), pl.BlockSpec((1, tc), lambda j, b: (0, j))],
        out_specs=(blk, pl.BlockSpec((8, tc), lambda j, b: (0, j)), pl.BlockSpec((1, tc), lambda j, b: (0, j))),
        scratch_shapes=[pltpu.VMEM((HALO + s, tc), F32), pltpu.VMEM((s + HALO, tc), F32)],
        name=name, compiler_params=_params("parallel", "arbitrary"))(dxc, xbc, conv_w, conv_b)


def _softplus(x):
    return jnp.maximum(x, 0.0) + jnp.log1p(jnp.exp(-jnp.abs(x)))


def _hilo_dot(v, m_b, dims=NN):
    hi = v.astype(BF16)
    lo = (v - hi.astype(F32)).astype(BF16)
    return _dot(hi, m_b, dims) + _dot(lo, m_b, dims)


def _ssd_chunk_common(dtraw_ref, dtb_ref, alog_ref, dsk_ref, d_inner):
    q, p = CHUNK, SSM_HEAD_DIM
    shift = p.bit_length() - 1
    assert 1 << shift == p
    dt = _softplus(dtraw_ref[...] + dtb_ref[...])
    a = -jnp.exp(alog_ref[...])
    ii = lax.broadcasted_iota(jnp.int32, (q, q), 0)
    jj = lax.broadcasted_iota(jnp.int32, (q, q), 1)
    causal = ii >= jj
    tril = jnp.where(causal, 1.0, 0.0).astype(F32)
    triu = jnp.where(ii <= jj, 1.0, 0.0).astype(F32)
    a_cs = _dot_hi(tril, dt * a)
    a_cs_t = a_cs.T
    a_last = a_cs[q - 1:q, :]
    e_col = jnp.exp(a_cs)
    dec_end = jnp.exp(a_last - a_cs)
    head_of_col = lax.shift_right_logical(lax.broadcasted_iota(jnp.int32, (LANES, d_inner), 1), shift)
    spread = (lax.broadcasted_iota(jnp.int32, (LANES, d_inner), 0) == head_of_col).astype(BF16)
    exact = jnp.concatenate([dt, jnp.broadcast_to(dsk_ref[...], (8, LANES))], axis=0)
    hi = jnp.concatenate([exact, e_col, dec_end], axis=0).astype(BF16)
    lo = (exact - hi[:q + 8].astype(F32)).astype(BF16)
    wide = _dot(hi, spread, NN)
    fine = wide[:q + 8] + _dot(lo, spread, NN)
    return dict(dt=dt, a=a, a_cs=a_cs, a_cs_t=a_cs_t, a_last=a_last, dec_end=dec_end, causal=causal, triu=triu,
                dt_e=fine[:q], dsk_e=fine[q:q + 1], e_e=wide[q + 8:2 * q + 8], dec_e=wide[2 * q + 8:3 * q + 8])


def _fill_block_diag(bd_ref, src_ref, hpg, col0=0):
    q, p = CHUNK, SSM_HEAD_DIM
    for hh in range(hpg):
        bd_ref[hh * q:(hh + 1) * q, hh * p:(hh + 1) * p] = src_ref[:, col0 + hh * p:col0 + (hh + 1) * p]


def _lane_onehot(h):
    return (lax.broadcasted_iota(jnp.int32, (1, LANES), 1) == h).astype(F32)


def _ssd_fwd(xc, dt_raw, dt_bias, a_log, d_skip, bl, n_heads, *, side=None, name):
    t = xc.shape[0]
    q, p, nst, grp = CHUNK, SSM_HEAD_DIM, SSM_STATE, SSM_GROUPS
    d_inner = n_heads * p
    hpg = n_heads // grp
    hb = min(hpg, SSD_HEAD_BATCH)
    gw = hpg * p
    nc = t // bl // q
    assert d_inner % (grp * nst) == 0 and nst == LANES and hpg % hb == 0

    def body(xs_ref, b_ref, c_ref, dtraw_ref, dtb_ref, alog_ref, dsk_ref, y_ref, hprev_ref, state, m_all, x_bd, xdt_s):
        @pl.when(jnp.logical_and(pl.program_id(0) == 0, pl.program_id(1) == 0))
        def _():
            x_bd[...] = jnp.zeros_like(x_bd)

        @pl.when(pl.program_id(1) == 0)
        def _():
            state[...] = jnp.zeros_like(state)

        cm = _ssd_chunk_common(dtraw_ref, dtb_ref, alog_ref, dsk_ref, d_inner)
        for g in range(grp):
            cols = slice(g * gw, (g + 1) * gw)
            bg = b_ref[:, g * nst:(g + 1) * nst]
            cg = c_ref[:, g * nst:(g + 1) * nst]
            scores = _dot(cg, bg, NT)
            xs = xs_ref[:, cols].astype(F32)
            xdt = xs * cm['dt_e'][:, cols]
            xdt_s[...] = xdt.astype(BF16)
            y_parts = []
            for sub in range(hpg // hb):
                for k in range(hb):
                    h = g * hpg + sub * hb + k
                    seg = cm['a_cs'][:, h:h + 1] - cm['a_cs_t'][h:h + 1, :]
                    m_all[:, k * q:(k + 1) * q] = (scores * jnp.exp(jnp.where(cm['causal'], seg, NEG_INF))).astype(BF16)
                _fill_block_diag(x_bd, xdt_s, hb, sub * hb * p)
                y_parts.append(_dot(m_all[...], x_bd[...], NN))
            hprev = state[g]
            hprev_ref[g] = hprev
            y = jnp.concatenate(y_parts, axis=1) + cm['e_e'][:, cols] * _dot(cg, hprev.astype(BF16), NT)
            y_ref[:, cols] = y + cm['dsk_e'][:, cols] * xs
            st = _dot((xdt * cm['dec_e'][:, cols]).astype(BF16), bg, TN)
            for hh in range(hpg):
                h = g * hpg + hh
                rows = slice(hh * p, (hh + 1) * p)
                state[g, rows, :] = jnp.exp(cm['a_last'][:, h:h + 1]) * hprev[rows] + st[rows]

    gn = grp * nst

    def rowmap(b, c):
        return b * nc + c
    vec = pl.BlockSpec((1, LANES), lambda b, c: (0, 0))
    return _hosted_call(
        body,
        out_shape=[jax.ShapeDtypeStruct((t, d_inner), F32), jax.ShapeDtypeStruct((t // q, grp, gw, nst), F32)],
        grid=(bl, nc),
        in_specs=[pl.BlockSpec((q, d_inner), lambda b, c: (rowmap(b, c), 0)),
                  pl.BlockSpec((q, gn), lambda b, c: (rowmap(b, c), d_inner // gn)),
                  pl.BlockSpec((q, gn), lambda b, c: (rowmap(b, c), d_inner // gn + 1)),
                  pl.BlockSpec((q, LANES), lambda b, c: (rowmap(b, c), 0)), vec, vec, vec],
        out_specs=[pl.BlockSpec((q, d_inner), lambda b, c: (rowmap(b, c), 0)),
                   pl.BlockSpec((None, grp, gw, nst), lambda b, c: (rowmap(b, c), 0, 0, 0))],
        scratch_shapes=[pltpu.VMEM((grp, gw, nst), F32), pltpu.VMEM((q, hb * q), BF16), pltpu.VMEM((hb * q, hb * p), BF16),
                        pltpu.VMEM((q, gw), BF16)],
        operands=[xc, xc, xc, dt_raw, dt_bias, a_log, d_skip], side=side, name=name)


def _ssd_bwd(dy, y, xc, dt_raw, hprev_all, dt_bias, a_log, d_skip, bl, n_heads, *, side=None, name):
    t, c_dim = xc.shape
    q, p, nst, grp = CHUNK, SSM_HEAD_DIM, SSM_STATE, SSM_GROUPS
    d_inner = n_heads * p
    hpg = n_heads // grp
    hb = min(hpg, SSD_HEAD_BATCH)
    gw = hpg * p
    nc = t // bl // q
    gn = grp * nst
    shift = p.bit_length() - 1

    def body(dy_ref, y_ref, xs_ref, b_ref, c_ref, dtraw_ref, hprev_ref, dtb_ref, alog_ref, dsk_ref,
             dxc_ref, ddtraw_ref, ddtb_ref, dalog_ref, ddsk_ref, dstate, mt_all, x_bd, dy_bd, xdt_s):
        @pl.when(jnp.logical_and(pl.program_id(0) == 0, pl.program_id(1) == 0))
        def _():
            ddtb_ref[...] = jnp.zeros_like(ddtb_ref)
            dalog_ref[...] = jnp.zeros_like(dalog_ref)
            ddsk_ref[...] = jnp.zeros_like(ddsk_ref)
            x_bd[...] = jnp.zeros_like(x_bd)
            dy_bd[...] = jnp.zeros_like(dy_bd)

        @pl.when(pl.program_id(1) == 0)
        def _():
            dstate[...] = jnp.zeros_like(dstate)

        cm = _ssd_chunk_common(dtraw_ref, dtb_ref, alog_ref, dsk_ref, d_inner)
        causal = cm['causal']
        upper = cm['triu'] > 0.5
        seg_row = lax.shift_right_logical(lax.broadcasted_iota(jnp.int32, (gw, LANES), 0), shift)
        seg_lane = lax.broadcasted_iota(jnp.int32, (gw, LANES), 1)
        sums = jnp.zeros((5 * q, LANES), F32)
        state_dot = jnp.zeros((1, LANES), F32)
        for g in range(grp):
            cols = slice(g * gw, (g + 1) * gw)
            seg_sum = (seg_row + g * hpg == seg_lane).astype(BF16)
            bg = b_ref[:, g * nst:(g + 1) * nst]
            cg = c_ref[:, g * nst:(g + 1) * nst]
            scores_t = _dot(bg, cg, NT)
            xs = xs_ref[:, cols].astype(F32)
            xdt = xs * cm['dt_e'][:, cols]
            xdt_s[...] = xdt.astype(BF16)
            dyf = dy_ref[:, cols].astype(F32)
            dscores = jnp.zeros((q, q), F32)
            dx_parts = []
            for sub in range(hpg // hb):
                c0 = sub * hb * p
                _fill_block_diag(x_bd, xdt_s, hb, c0)
                _fill_block_diag(dy_bd, dy_ref, hb, g * gw + c0)
                dm_all = _dot(dy_ref[:, g * gw + c0:g * gw + c0 + hb * p], x_bd[...], NT)
                for k in range(hb):
                    h = g * hpg + sub * hb + k
                    blk = slice(k * q, (k + 1) * q)
                    seg = cm['a_cs'][:, h:h + 1] - cm['a_cs_t'][h:h + 1, :]
                    mt_all[:, blk] = (scores_t * jnp.exp(jnp.where(upper, -seg, NEG_INF))).astype(BF16)
                    dscores = dscores + dm_all[:, blk] * jnp.exp(jnp.where(causal, seg, NEG_INF))
                dx_parts.append(_dot(mt_all[...], dy_bd[...], NN))
            hprev = hprev_ref[g]
            hprev_b = hprev.astype(BF16)
            dhn = dstate[g]
            dhn_b = dhn.astype(BF16)
            e_e, dec_e = cm['e_e'][:, cols], cm['dec_e'][:, cols]
            y_scan = y_ref[:, cols] - cm['dsk_e'][:, cols] * xs
            dye_b = (dyf * e_e).astype(BF16)
            dcg = _dot(dye_b, hprev_b, NN)
            dhp = _dot(dye_b, cg, TN)
            bdh = _dot(bg, dhn_b, NT)
            dbg = _dot((xdt * dec_e).astype(BF16), dhn_b, NN)
            dx_diag = jnp.concatenate(dx_parts, axis=1)
            dx = dec_e * bdh + dx_diag
            ds_b = dscores.astype(BF16)
            dcg = dcg + _dot(ds_b, bg, NN)
            dbg = dbg + _dot(ds_b, cg, TN)
            x_rounded = xdt_s[...].astype(F32)
            sums = sums + _hilo_dot(jnp.concatenate([dyf * y_scan, xdt * bdh, x_rounded * dx_diag, dx * xs, dyf * xs], axis=0), seg_sum)
            state_dot = state_dot + jnp.sum(_hilo_dot(dhn * hprev, seg_sum, TN), axis=0, keepdims=True)
            dxc_ref[:, cols] = (dx * cm['dt_e'][:, cols] + cm['dsk_e'][:, cols] * dyf).astype(dxc_ref.dtype)
            dxc_ref[:, d_inner + g * nst:d_inner + (g + 1) * nst] = dbg.astype(dxc_ref.dtype)
            dxc_ref[:, d_inner + gn + g * nst:d_inner + gn + (g + 1) * nst] = dcg.astype(dxc_ref.dtype)
            for hh in range(hpg):
                h = g * hpg + hh
                rows = slice(hh * p, (hh + 1) * p)
                dstate[g, rows, :] = jnp.exp(cm['a_last'][:, h:h + 1]) * dhn[rows] + dhp[rows]
        s_y, s_end, s_diag, s_dt, s_skip = (sums[k * q:(k + 1) * q] for k in range(5))
        dt, a, dec_end = cm['dt'], cm['a'], cm['dec_end']
        last_row = (lax.broadcasted_iota(jnp.int32, (q, 1), 0) == q - 1).astype(F32)
        da_last = jnp.sum(dec_end * s_end, axis=0, keepdims=True) + jnp.exp(cm['a_last']) * state_dot
        da = s_y - dec_end * s_end - s_diag + last_row * da_last
        ddta = _dot_hi(cm['triu'], da)
        ddt = s_dt + ddta * a
        d_a = jnp.sum(ddta * dt, axis=0, keepdims=True)
        ddt_raw = ddt * _sigmoid(dtraw_ref[...] + dtb_ref[...])
        ddtraw_ref[...] = ddt_raw
        ddtb_ref[...] += jnp.sum(ddt_raw, axis=0, keepdims=True)
        dalog_ref[...] += d_a * a
        ddsk_ref[...] += jnp.sum(s_skip, axis=0, keepdims=True)

    def rowmap(b, c):
        return b * nc + (nc - 1 - c)
    vec = pl.BlockSpec((1, LANES), lambda b, c: (0, 0))
    vec_shape = jax.ShapeDtypeStruct((1, LANES), F32)
    return _hosted_call(
        body,
        out_shape=[jax.ShapeDtypeStruct((t, c_dim), BF16), jax.ShapeDtypeStruct((t, LANES), F32), vec_shape, vec_shape, vec_shape],
        grid=(bl, nc),
        in_specs=[pl.BlockSpec((q, d_inner), lambda b, c: (rowmap(b, c), 0)),
                  pl.BlockSpec((q, d_inner), lambda b, c: (rowmap(b, c), 0)),
                  pl.BlockSpec((q, d_inner), lambda b, c: (rowmap(b, c), 0)),
                  pl.BlockSpec((q, gn), lambda b, c: (rowmap(b, c), d_inner // gn)),
                  pl.BlockSpec((q, gn), lambda b, c: (rowmap(b, c), d_inner // gn + 1)),
                  pl.BlockSpec((q, LANES), lambda b, c: (rowmap(b, c), 0)),
                  pl.BlockSpec((None, grp, gw, nst), lambda b, c: (rowmap(b, c), 0, 0, 0)), vec, vec, vec],
        out_specs=[pl.BlockSpec((q, c_dim), lambda b, c: (rowmap(b, c), 0)),
                   pl.BlockSpec((q, LANES), lambda b, c: (rowmap(b, c), 0)), vec, vec, vec],
        scratch_shapes=[pltpu.VMEM((grp, gw, nst), F32), pltpu.VMEM((q, hb * q), BF16),
                        pltpu.VMEM((hb * q, hb * p), BF16), pltpu.VMEM((hb * q, hb * p), BF16), pltpu.VMEM((q, gw), BF16)],
        operands=[dy, y, xc, xc, xc, dt_raw, hprev_all, dt_bias, a_log, d_skip], side=side, name=name)


def _bucket_onehot():
    blk = CHUNK
    qi = jnp.arange(blk)[:, None]
    kj = jnp.arange(2 * blk)[None, :]
    dist = jnp.maximum(qi + blk - kj, 0)
    max_exact = REL_BUCKETS // 2
    d = jnp.maximum(dist, 1).astype(F32)
    large = max_exact + (jnp.log(d / max_exact) / math.log(REL_MAX_DISTANCE / max_exact) * (REL_BUCKETS - max_exact)).astype(jnp.int32)
    large = jnp.minimum(large, REL_BUCKETS - 1)
    bucket = jnp.where(dist < max_exact, dist, large).reshape(-1)
    return (bucket[None, :] == jnp.arange(REL_BUCKETS)[:, None]).astype(F32)


def _small_mm_hi(a, b, dims, *, name):
    def body(a_ref, b_ref, o_ref):
        o_ref[...] = _dot_hi(a_ref[...], b_ref[...], dims)
    n = b.shape[0] if dims == NT else b.shape[1]
    return pl.pallas_call(body, out_shape=jax.ShapeDtypeStruct((a.shape[0], n), F32), name=name)(a, b)


def _attn_band_mask_t(n, rep):
    blk = CHUNK
    jj = lax.broadcasted_iota(jnp.int32, (2 * blk, rep * blk), 0)
    ii = lax.broadcasted_iota(jnp.int32, (2 * blk, rep * blk), 1) & (blk - 1)
    dist = ii + blk - jj
    in_window = jnp.logical_and(dist >= 0, dist < blk)
    return jnp.logical_and(in_window, jnp.logical_or(jj >= blk, n > 0))


def _sink_row(sink_ref, heads):
    return jnp.concatenate([jnp.broadcast_to(sink_ref[:, h:h + 1], (1, CHUNK)) for h in heads], axis=1)


def _attn_fwd(q, kv, bias_t, sinks, bl, *, name):
    t, qd = q.shape
    blk, hd = CHUNK, ATTN_HEAD_DIM
    kvd = ATTN_KV_HEADS * hd
    rep = ATTN_Q_HEADS // ATTN_KV_HEADS
    nb = t // bl // blk
    scale = hd ** -0.5

    def body(q_ref, kp_ref, kc_ref, vp_ref, vc_ref, bias_ref, sink_ref, o_ref, lse_ref):
        n = pl.program_id(1)
        mask = _attn_band_mask_t(n, rep)
        for kvh in range(ATTN_KV_HEADS):
            ks = slice(kvh * hd, (kvh + 1) * hd)
            heads = range(kvh * rep, (kvh + 1) * rep)
            qs = jnp.concatenate([q_ref[:, h * hd:(h + 1) * hd] for h in heads], axis=0)
            kk = jnp.concatenate([kp_ref[:, ks], kc_ref[:, ks]], axis=0)
            vv = jnp.concatenate([vp_ref[:, ks], vc_ref[:, ks]], axis=0)
            s = jnp.where(mask, _dot(kk, qs, NT) * scale + bias_ref[kvh], NEG_INF)
            sink = _sink_row(sink_ref, heads)
            m = jnp.maximum(jnp.max(s, axis=0, keepdims=True), sink)
            p = jnp.exp(s - m)
            den = jnp.sum(p, axis=0, keepdims=True) + jnp.exp(sink - m)
            o = _dot((p * (1.0 / den)).astype(BF16), vv, TN)
            lse = m + jnp.log(den)
            for r, h in enumerate(heads):
                o_ref[:, h * hd:(h + 1) * hd] = o[r * blk:(r + 1) * blk].astype(o_ref.dtype)
                lse_ref[h:h + 1, :] = lse[:, r * blk:(r + 1) * blk]

    def cur(b, n):
        return b * nb + n

    def prev(b, n):
        return b * nb + jnp.maximum(n - 1, 0)
    return pl.pallas_call(
        body, out_shape=(jax.ShapeDtypeStruct((t, qd), BF16), jax.ShapeDtypeStruct((t // blk * ATTN_Q_HEADS, blk), F32)), grid=(bl, nb),
        in_specs=[pl.BlockSpec((blk, qd), lambda b, n: (cur(b, n), 0)),
                  pl.BlockSpec((blk, kvd), lambda b, n: (prev(b, n), 0)), pl.BlockSpec((blk, kvd), lambda b, n: (cur(b, n), 0)),
                  pl.BlockSpec((blk, kvd), lambda b, n: (prev(b, n), 1)), pl.BlockSpec((blk, kvd), lambda b, n: (cur(b, n), 1)),
                  _const_spec(bias_t.shape), _const_spec((1, LANES))],
        out_specs=(pl.BlockSpec((blk, qd), lambda b, n: (cur(b, n), 0)),
                   pl.BlockSpec((ATTN_Q_HEADS, blk), lambda b, n: (cur(b, n), 0))),
        name=name, compiler_params=_params("parallel", "arbitrary"))(q, kv, kv, kv, kv, bias_t, sinks)


def _attn_bwd(do, q, kv, lse, bias_t, sinks, bl, *, name):
    t, qd = q.shape
    blk, hd = CHUNK, ATTN_HEAD_DIM
    kvd = ATTN_KV_HEADS * hd
    rep = ATTN_Q_HEADS // ATTN_KV_HEADS
    s_len = t // bl
    nb = s_len // blk
    scale = hd ** -0.5

    def body(do_ref, q_ref, kp_ref, kc_ref, vp_ref, vc_ref, lse_ref, bias_ref, sink_ref, dq_ref, dkv_ref, dbias_ref, dsink_ref):
        n = pl.program_id(1)

        @pl.when(jnp.logical_and(pl.program_id(0) == 0, n == 0))
        def _():
            dbias_ref[...] = jnp.zeros_like(dbias_ref)
            dsink_ref[...] = jnp.zeros_like(dsink_ref)

        mask = _attn_band_mask_t(n, rep)
        r_cur = pl.multiple_of(n * blk, blk)
        r_prev = pl.multiple_of(jnp.maximum(n - 1, 0) * blk, blk)
        dsink = jnp.zeros((1, LANES), F32)
        for kvh in range(ATTN_KV_HEADS):
            ks = slice(kvh * hd, (kvh + 1) * hd)
            heads = range(kvh * rep, (kvh + 1) * rep)
            qs = jnp.concatenate([q_ref[:, h * hd:(h + 1) * hd] for h in heads], axis=0)
            dos = jnp.concatenate([do_ref[:, h * hd:(h + 1) * hd] for h in heads], axis=0)
            kk = jnp.concatenate([kp_ref[:, ks], kc_ref[:, ks]], axis=0)
            vv = jnp.concatenate([vp_ref[:, ks], vc_ref[:, ks]], axis=0)
            lse = jnp.concatenate([lse_ref[h:h + 1, :] for h in heads], axis=1)
            p = jnp.exp(jnp.where(mask, _dot(kk, qs, NT) * scale + bias_ref[kvh], NEG_INF) - lse)
            dp = _dot(vv, dos, NT)
            delta = jnp.sum(p * dp, axis=0, keepdims=True)
            ds = p * (dp - delta)
            dsink_row = jnp.exp(_sink_row(sink_ref, heads) - lse) * delta
            dbias_ref[kvh] += ds
            ds_b = ds.astype(BF16)
            dq_s = _dot(ds_b, kk, TN) * scale
            dkk = _dot(ds_b, qs, NN) * scale
            dvv = _dot(p.astype(BF16), dos, NN)
            for r, h in enumerate(heads):
                dq_ref[:, h * hd:(h + 1) * hd] = dq_s[r * blk:(r + 1) * blk].astype(dq_ref.dtype)
                dsink = dsink - jnp.sum(dsink_row[:, r * blk:(r + 1) * blk], axis=1, keepdims=True) * _lane_onehot(h)
            vs = slice(kvd + kvh * hd, kvd + (kvh + 1) * hd)
            dkv_ref[pl.ds(r_cur, blk), ks] = dkk[blk:]
            dkv_ref[pl.ds(r_cur, blk), vs] = dvv[blk:]

            @pl.when(n > 0)
            def _():
                dkv_ref[pl.ds(r_prev, blk), ks] += dkk[:blk]
                dkv_ref[pl.ds(r_prev, blk), vs] += dvv[:blk]
        dsink_ref[...] += dsink

    def cur(b, n):
        return b * nb + n

    def prev(b, n):
        return b * nb + jnp.maximum(n - 1, 0)
    qspec = pl.BlockSpec((blk, qd), lambda b, n: (cur(b, n), 0))
    return pl.pallas_call(
        body,
        out_shape=(jax.ShapeDtypeStruct((t, qd), BF16), jax.ShapeDtypeStruct((t, 2 * kvd), F32),
                   jax.ShapeDtypeStruct(bias_t.shape, F32), jax.ShapeDtypeStruct((1, LANES), F32)),
        grid=(bl, nb),
        in_specs=[qspec, qspec,
                  pl.BlockSpec((blk, kvd), lambda b, n: (prev(b, n), 0)), pl.BlockSpec((blk, kvd), lambda b, n: (cur(b, n), 0)),
                  pl.BlockSpec((blk, kvd), lambda b, n: (prev(b, n), 1)), pl.BlockSpec((blk, kvd), lambda b, n: (cur(b, n), 1)),
                  pl.BlockSpec((ATTN_Q_HEADS, blk), lambda b, n: (cur(b, n), 0)), _const_spec(bias_t.shape), _const_spec((1, LANES))],
        out_specs=(qspec, pl.BlockSpec((s_len, 2 * kvd), lambda b, n: (b, 0)), _const_spec(bias_t.shape), _const_spec((1, LANES))),
        name=name, compiler_params=_params("arbitrary", "arbitrary"))(do, q, kv, kv, kv, kv, lse, bias_t, sinks)


def _merge_fwd(y, z, ng, o, gs, ga, w_ssm, w_attn, w_out, h_in, g_post, g_next, *, name):
    t, d = h_in.shape
    tm = _tile(t, 256)
    d_ssm = y.shape[1]
    gw = d_ssm // SSM_GROUPS

    def body(y_ref, z_ref, ng_ref, o_ref, gs_ref, ga_ref, ws_ref, wa_ref, wo_ref, hin_ref, gp_ref, gn_ref,
             yn_ref, ys_ref, ya_ref, mg_ref, mix_ref, hout_ref, next_ref):
        for g in range(SSM_GROUPS):
            sl = slice(g * gw, (g + 1) * gw)
            zv = z_ref[:, sl].astype(F32)
            yg = y_ref[:, sl] * (zv * _sigmoid(zv))
            r = lax.rsqrt(jnp.mean(yg * yg, axis=-1, keepdims=True) + RMS_EPS)
            yn_ref[:, sl] = (yg * r * ng_ref[:, sl]).astype(BF16)
        ys = _dot(yn_ref[...], ws_ref[...], NN)
        ya = _dot(o_ref[...], wa_ref[...], NN)
        merged = (_sigmoid(gs_ref[...].astype(F32)) * ys + _sigmoid(ga_ref[...].astype(F32)) * ya).astype(BF16)
        mix = _dot(merged, wo_ref[...], NN)
        ys_ref[...] = ys.astype(BF16)
        ya_ref[...] = ya.astype(BF16)
        mg_ref[...] = merged
        mix_ref[...] = mix
        hout = _rms_residual(mix, hin_ref[...], gp_ref[...], 1.0)
        hout_ref[...] = hout
        rn = lax.rsqrt(jnp.mean(hout * hout, axis=-1, keepdims=True) + RMS_EPS)
        next_ref[...] = (hout * rn * gn_ref[...]).astype(BF16)

    def row(w):
        return pl.BlockSpec((tm, w), lambda i: (i, 0))
    bshape = jax.ShapeDtypeStruct((t, d), BF16)
    fshape = jax.ShapeDtypeStruct((t, d), F32)
    return pl.pallas_call(
        body, out_shape=(jax.ShapeDtypeStruct((t, d_ssm), BF16), bshape, bshape, bshape, fshape, fshape, bshape), grid=(t // tm,),
        in_specs=[row(d_ssm), row(d_ssm), _const_spec((1, d_ssm)), row(o.shape[1]), row(d), row(d), _resident_spec(w_ssm.shape),
                  _resident_spec(w_attn.shape), _resident_spec(w_out.shape), row(d), _const_spec((1, d)), _const_spec((1, d))],
        out_specs=(row(d_ssm),) + (row(d),) * 6, name=name,
        compiler_params=_params("parallel"))(y, z, ng, o, gs, ga, w_ssm, w_attn, w_out, h_in, g_post, g_next)


def _merge_bwd(dh, mix, g_post, gs, ga, ys, ya, y, z, ng, w_ssm, w_attn, w_out, *, name):
    t, d = mix.shape
    tm = _tile(t, 256)
    d_ssm, d_attn = w_ssm.shape[0], w_attn.shape[0]
    gw = d_ssm // SSM_GROUPS

    def body(dh_ref, mix_ref, gp_ref, gs_ref, ga_ref, ys_ref, ya_ref, y_ref, z_ref, ng_ref, ws_ref, wa_ref, wo_ref,
             dmix_ref, dys_ref, dya_ref, dgs_ref, dga_ref, dy_ref, dz_ref, do_ref, dgp_ref, dng_ref):
        @pl.when(pl.program_id(0) == 0)
        def _():
            dgp_ref[...] = jnp.zeros_like(dgp_ref)
            dng_ref[...] = jnp.zeros_like(dng_ref)
        mv = mix_ref[...]
        dy = dh_ref[...]
        r = lax.rsqrt(jnp.mean(mv * mv, axis=-1, keepdims=True) + RMS_EPS)
        mhat = mv * r
        dyg = dy * gp_ref[...]
        dmix = (r * (dyg - mhat * jnp.mean(dyg * mhat, axis=-1, keepdims=True))).astype(BF16)
        dgp_ref[...] += jnp.sum(dy * mhat, axis=0, keepdims=True)
        dmix_ref[...] = dmix
        dmerged = _dot(dmix, wo_ref[...], NT)
        sgs = _sigmoid(gs_ref[...].astype(F32))
        sga = _sigmoid(ga_ref[...].astype(F32))
        dys = (dmerged * sgs).astype(BF16)
        dya = (dmerged * sga).astype(BF16)
        dys_ref[...] = dys
        dya_ref[...] = dya
        dgs_ref[...] = (dmerged * ys_ref[...].astype(F32) * sgs * (1.0 - sgs)).astype(BF16)
        dga_ref[...] = (dmerged * ya_ref[...].astype(F32) * sga * (1.0 - sga)).astype(BF16)
        do_ref[...] = _dot(dya, wa_ref[...], NT).astype(BF16)
        dyn = _dot(dys, ws_ref[...], NT)
        for g in range(SSM_GROUPS):
            sl = slice(g * gw, (g + 1) * gw)
            zv = z_ref[:, sl].astype(F32)
            yv = y_ref[:, sl]
            sg = _sigmoid(zv)
            sz = zv * sg
            yg = yv * sz
            rg = lax.rsqrt(jnp.mean(yg * yg, axis=-1, keepdims=True) + RMS_EPS)
            yhat = yg * rg
            dn = dyn[:, sl]
            dyg_n = dn * ng_ref[:, sl]
            dyg_g = rg * (dyg_n - yhat * jnp.mean(dyg_n * yhat, axis=-1, keepdims=True))
            dy_ref[:, sl] = (dyg_g * sz).astype(BF16)
            dz_ref[:, sl] = (dyg_g * yv * (sg * (1.0 + zv * (1.0 - sg)))).astype(BF16)
            dng_ref[:, sl] += jnp.sum(dn * yhat, axis=0, keepdims=True)

    def row(w):
        return pl.BlockSpec((tm, w), lambda i: (i, 0))

    def bshape(w):
        return jax.ShapeDtypeStruct((t, w), BF16)
    return pl.pallas_call(
        body, out_shape=(bshape(d),) * 5 + (bshape(d_ssm), bshape(d_ssm), bshape(d_attn), jax.ShapeDtypeStruct((1, d), F32),
                                            jax.ShapeDtypeStruct((1, d_ssm), F32)), grid=(t // tm,),
        in_specs=[row(d), row(d), _const_spec((1, d)), row(d), row(d), row(d), row(d), row(d_ssm), row(d_ssm), _const_spec((1, d_ssm)),
                  _resident_spec(w_ssm.shape), _resident_spec(w_attn.shape), _resident_spec(w_out.shape)],
        out_specs=(row(d),) * 5 + (row(d_ssm), row(d_ssm), row(d_attn), _const_spec((1, d)), _const_spec((1, d_ssm))),
        name=name, compiler_params=_params("arbitrary"))(dh, mix, g_post, gs, ga, ys, ya, y, z, ng, w_ssm, w_attn, w_out)


def _position():
    return lax.axis_index("x"), lax.axis_index("y"), lax.axis_index("c")


def _gather_exchange(shards):
    na = len(shards)

    def plan(ins, outs, sems):
        send_sems, recv_sems, local_sems = sems
        x, y, c = _position()
        me, sibling = (x, y, c), (x, y, 1 - c)
        chips = [(1 - x, y), (x, 1 - y), (1 - x, 1 - y)]

        def slot(a, pos):
            return outs[a].at[4 * pos[0] + 2 * pos[1] + pos[2]]

        def copy(a, k, block, to, src=None):
            return pltpu.make_async_remote_copy(
                src_ref=slot(a, block) if src is None else src, dst_ref=slot(a, block),
                send_sem=send_sems.at[a, k], recv_sem=recv_sems.at[a, k], device_id=to, device_id_type=MESH)

        mine = [pltpu.make_async_copy(ins[a], slot(a, me), local_sems.at[a]) for a in range(na)]
        first = []
        for a in range(na):
            first.append(copy(a, 0, me, sibling, src=ins[a]))
            first += [copy(a, 1 + j, me, (*chip, c), src=ins[a]) for j, chip in enumerate(chips)]
        return me, sibling, chips, copy, mine, first

    def start(ins, outs, sems):
        *_, mine, first = plan(ins, outs, sems)
        for cp in mine + first:
            cp.start()

    def finish(ins, outs, sems):
        me, sibling, chips, copy, mine, first = plan(ins, outs, sems)
        c = me[2]
        passed = []
        for a in range(na):
            for j, chip in enumerate(chips):
                copy(a, 1 + j, (*chip, c), me).wait_recv()
                fwd = copy(a, 4 + j, (*chip, c), sibling)
                fwd.start()
                passed.append(fwd)
        for a in range(na):
            copy(a, 0, sibling, me).wait_recv()
            for j, chip in enumerate(chips):
                copy(a, 4 + j, (*chip, 1 - c), me).wait_recv()
        for cp in first + passed:
            cp.wait_send()
        for cp in mine:
            cp.wait()

    return _Exchange(list(shards), [jax.ShapeDtypeStruct((N_DEV,) + s.shape, s.dtype) for s in shards],
                     [pltpu.SemaphoreType.DMA((na, 7)), pltpu.SemaphoreType.DMA((na, 7)), pltpu.SemaphoreType.DMA((na,))],
                     start, finish)


def _scatter_exchange(arrays):
    na = len(arrays)

    def copies(ins, outs, sems):
        send_sems, recv_sems = sems
        x, y, c = _position()
        out = []
        for a in range(na):
            for k in range(7):
                flip = k + 1
                peer = (x ^ (flip >> 2), y ^ ((flip >> 1) & 1), c ^ (flip & 1))
                peer_block = 4 * peer[0] + 2 * peer[1] + peer[2]
                out.append(pltpu.make_async_remote_copy(
                    src_ref=ins[a].at[peer_block], dst_ref=outs[a].at[k],
                    send_sem=send_sems.at[a, k], recv_sem=recv_sems.at[a, k], device_id=peer, device_id_type=MESH))
        return out

    def start(ins, outs, sems):
        for cp in copies(ins, outs, sems):
            cp.start()

    def finish(ins, outs, sems):
        for cp in copies(ins, outs, sems):
            cp.wait()

    return _Exchange(list(arrays), [jax.ShapeDtypeStruct((7,) + s.shape[1:], s.dtype) for s in arrays],
                     [pltpu.SemaphoreType.DMA((na, 7)), pltpu.SemaphoreType.DMA((na, 7))], start, finish)


def _reduce_adamw(own, recv, w, m, v, *, name):
    r, c = own.shape
    tm = _tile(r, 256)
    c1 = 1.0 - ADAM_B1 ** ADAM_STEP
    c2 = 1.0 - ADAM_B2 ** ADAM_STEP

    def body(own_ref, recv_ref, w_ref, m_ref, v_ref, g_ref, d_ref, mo_ref, vo_ref):
        gv = own_ref[...]
        for k in range(7):
            gv = gv + recv_ref[k].astype(F32)
        g_ref[...] = gv
        mn = ADAM_B1 * m_ref[...] + (1.0 - ADAM_B1) * gv
        vn = ADAM_B2 * v_ref[...] + (1.0 - ADAM_B2) * (gv * gv)
        mo_ref[...] = mn
        vo_ref[...] = vn
        d_ref[...] = -ADAM_LR * ((mn / c1) / (jnp.sqrt(vn / c2) + ADAM_EPS) + ADAM_WD * w_ref[...])

    blk = pl.BlockSpec((tm, c), lambda i: (i, 0))
    shp = jax.ShapeDtypeStruct((r, c), F32)
    return pl.pallas_call(body, out_shape=(shp,) * 4, grid=(r // tm,),
                          in_specs=[blk, pl.BlockSpec((7, tm, c), lambda i: (0, i, 0)), blk, blk, blk], out_specs=(blk,) * 4,
                          name=name, compiler_params=_params("parallel"))(own, recv, w, m, v)


SMALL_ROW = 8 * LANES


def _update_replicated(partials, ws, ms, vs, loss_part, *, name):
    n = len(ws)
    pieces, n_rows = [], 0
    for k in sorted(range(n), key=lambda k_: -ws[k_].shape[0]):
        r, c = ws[k].shape
        assert (r == 1 or c <= SMALL_ROW) and (r == 1 or n_rows % 8 == 0)
        for c0 in range(0, c, SMALL_ROW):
            pieces.append((k, slice(0, r), slice(c0, min(c0 + SMALL_ROW, c)), n_rows))
            n_rows += r
    loss_row = n_rows
    n_rows = -(-(n_rows + 1) // 8) * 8
    c1 = 1.0 - ADAM_B1 ** ADAM_STEP
    c2 = 1.0 - ADAM_B2 ** ADAM_STEP

    def reduce_body(*refs):
        g_in, loss_ref, total, buf, send_sems, recv_sems = refs[:n], refs[n], refs[n + 1], refs[n + 2], refs[n + 3], refs[n + 4]
        x, y, c_ = _position()
        me = 4 * x + 2 * y + c_
        buf[me] = jnp.zeros((n_rows, SMALL_ROW), F32)
        for k, rs, cs, row0 in pieces:
            buf[me, row0:row0 + rs.stop, 0:cs.stop - cs.start] = g_in[k][rs, cs]
        buf[me, loss_row:loss_row + 1, 0:LANES] = loss_ref[...]
        copies = []
        for j in range(7):
            flip = j + 1
            peer = (x ^ (flip >> 2), y ^ ((flip >> 1) & 1), c_ ^ (flip & 1))
            cp = pltpu.make_async_remote_copy(
                src_ref=buf.at[me], dst_ref=buf.at[me], send_sem=send_sems.at[j], recv_sem=recv_sems.at[j],
                device_id=peer, device_id_type=MESH)
            cp.start()
            copies.append(cp)
        for cp in copies:
            cp.wait()
        acc = buf[0]
        for dev in range(1, N_DEV):
            acc = acc + buf[dev]
        total[...] = acc

    def step_body(*refs):
        total = refs[0]
        w_in_, m_in, v_in = (refs[1 + j * n:1 + (j + 1) * n] for j in range(3))
        g_out, d_out, m_out, v_out = (refs[1 + (3 + j) * n:1 + (4 + j) * n] for j in range(4))
        for k, rs, cs, row0 in pieces:
            gv = total[row0:row0 + rs.stop, 0:cs.stop - cs.start]
            mn = ADAM_B1 * m_in[k][rs, cs] + (1.0 - ADAM_B1) * gv
            vn = ADAM_B2 * v_in[k][rs, cs] + (1.0 - ADAM_B2) * (gv * gv)
            g_out[k][rs, cs] = gv
            m_out[k][rs, cs] = mn
            v_out[k][rs, cs] = vn
            d_out[k][rs, cs] = -ADAM_LR * ((mn / c1) / (jnp.sqrt(vn / c2) + ADAM_EPS) + ADAM_WD * w_in_[k][rs, cs])

    vm = pl.BlockSpec(memory_space=pltpu.VMEM)
    total = pl.pallas_call(
        reduce_body, out_shape=jax.ShapeDtypeStruct((n_rows, SMALL_ROW), F32), in_specs=[vm] * (n + 1), out_specs=vm,
        scratch_shapes=[pltpu.VMEM((N_DEV, n_rows, SMALL_ROW), F32), pltpu.SemaphoreType.DMA((7,)), pltpu.SemaphoreType.DMA((7,))],
        name=f"{name}_allreduce")(*partials, loss_part)
    shapes = tuple(jax.ShapeDtypeStruct(w.shape, F32) for w in ws)
    outs = pl.pallas_call(step_body, out_shape=shapes * 4, in_specs=[vm] * (1 + 3 * n), out_specs=tuple([vm] * (4 * n)),
                          name=f"{name}_adamw")(total, *ws, *ms, *vs)
    return tuple(outs[j * n:(j + 1) * n] for j in range(4)) + (total[loss_row, 0],)


def _pad_lanes(v, width=LANES):
    return jnp.pad(v, ((0, 0), (0, width - v.shape[1])))


def kernel(x, ffn1_pre_g, ffn1_w_gate, ffn1_w_up, ffn1_w_down, ffn1_post_g, mix_pre_g, w_in, conv_w, conv_b, dt_bias, a_log, d_skip, ssm_norm_g, w_ssm_proj, attn_sinks, rel_bias_table, w_attn_proj, w_out, mix_post_g, ffn2_pre_g, ffn2_w_gate, ffn2_w_up, ffn2_w_down, ffn2_post_g, loss_target, m_ffn1_pre_g, m_ffn1_w_gate, m_ffn1_w_up, m_ffn1_w_down, m_ffn1_post_g, m_mix_pre_g, m_w_in, m_conv_w, m_conv_b, m_dt_bias, m_a_log, m_d_skip, m_ssm_norm_g, m_w_ssm_proj, m_attn_sinks, m_rel_bias_table, m_w_attn_proj, m_w_out, m_mix_post_g, m_ffn2_pre_g, m_ffn2_w_gate, m_ffn2_w_up, m_ffn2_w_down, m_ffn2_post_g, v_ffn1_pre_g, v_ffn1_w_gate, v_ffn1_w_up, v_ffn1_w_down, v_ffn1_post_g, v_mix_pre_g, v_w_in, v_conv_w, v_conv_b, v_dt_bias, v_a_log, v_d_skip, v_ssm_norm_g, v_w_ssm_proj, v_attn_sinks, v_rel_bias_table, v_w_attn_proj, v_w_out, v_mix_post_g, v_ffn2_pre_g, v_ffn2_w_gate, v_ffn2_w_up, v_ffn2_w_down, v_ffn2_post_g):
    args = dict(locals())
    weight_names = ['ffn1_pre_g', 'ffn1_w_gate', 'ffn1_w_up', 'ffn1_w_down', 'ffn1_post_g', 'mix_pre_g', 'w_in', 'conv_w', 'conv_b',
                    'dt_bias', 'a_log', 'd_skip', 'ssm_norm_g', 'w_ssm_proj', 'attn_sinks', 'rel_bias_table', 'w_attn_proj', 'w_out',
                    'mix_post_g', 'ffn2_pre_g', 'ffn2_w_gate', 'ffn2_w_up', 'ffn2_w_down', 'ffn2_post_g']
    col_sharded = ('ffn1_w_gate', 'ffn1_w_up', 'w_in', 'ffn2_w_gate', 'ffn2_w_up')
    row_sharded = ('ffn1_w_down', 'w_ssm_proj', 'w_attn_proj', 'w_out', 'ffn2_w_down')
    big = col_sharded + row_sharded

    bl, s_len, d = x.shape
    t = bl * s_len
    d_inner = ssm_norm_g.shape[1]
    n_heads = dt_bias.shape[1]
    gn = SSM_GROUPS * SSM_STATE
    conv_dim = d_inner + 2 * gn
    q_dim = ATTN_Q_HEADS * ATTN_HEAD_DIM
    kv_dim = ATTN_KV_HEADS * ATTN_HEAD_DIM

    def local_2d(name, a):
        a = a[0]
        return a.T if name in col_sharded else a

    ffn1_names = ('ffn1_w_gate', 'ffn1_w_up', 'ffn1_w_down')
    ffn2_names = ('ffn2_w_gate', 'ffn2_w_up', 'ffn2_w_down')
    mixer_names = ('w_ssm_proj', 'w_attn_proj', 'w_out')

    def shard(n):
        return local_2d(n, args[n]).astype(BF16)

    def rows(g):
        return g.reshape(N_DEV * g.shape[1], g.shape[2])

    x2 = x.reshape(t, d)
    tgt2 = loss_target.reshape(t, d)
    full = {}

    (h1,), saved1, ffn1_w, got_in = _ffn_forward(
        x2, ffn1_pre_g, lambda got: (rows(got[0]), rows(got[1])), lambda got: rows(got[2]), ffn1_post_g, "ffn1",
        side_norm=_gather_exchange([shard(n) for n in ffn1_names[:2]]),
        side_up=_gather_exchange([shard('w_in'), conv_w[0], shard('ffn1_w_down')]))
    full.update(zip(ffn1_names, ffn1_w))
    conv_w_full = jnp.transpose(got_in[1], (1, 0, 2)).reshape(SSM_CONV, conv_dim)

    win_t = rows(got_in[0])
    dt_lo = 2 * d + d_inner + conv_dim
    off = {'gs': 0, 'ga': d, 'z': 2 * d, 'xbc': 2 * d + d_inner, 'dt': dt_lo, 'q': dt_lo + n_heads, 'kv': dt_lo + n_heads + q_dim}
    assert all(o_ % 16 == 0 for o_ in off.values())

    u, (gs, ga, z, xbc, q, kv, dt_raw) = _proj_all(
        h1, mix_pre_g, win_t,
        [(off['gs'], d, BF16), (off['ga'], d, BF16), (off['z'], d_inner, BF16), (off['xbc'], conv_dim, BF16),
         (off['q'], q_dim, BF16), (off['kv'], 2 * kv_dim, BF16), (off['dt'], n_heads, F32)], name="mix_proj")

    dtb_p, alog_p, dsk_p, sinks_p = _pad_lanes(dt_bias), _pad_lanes(a_log), _pad_lanes(d_skip), _pad_lanes(attn_sinks)
    xc = _conv_fwd(xbc, conv_w_full, conv_b, bl, name="conv_fwd")
    late_names = mixer_names + ffn2_names
    (y, hprev), got_late = _ssd_fwd(xc, dt_raw, dtb_p, alog_p, dsk_p, bl, n_heads,
                                    side=_gather_exchange([shard(n) for n in late_names]), name="ssd_fwd")
    full.update({n: rows(g) for n, g in zip(late_names, got_late)})

    onehot = _bucket_onehot()
    rep = ATTN_Q_HEADS // ATTN_KV_HEADS
    bias = _small_mm_hi(rel_bias_table.T, onehot, NN, name="rel_bias")
    bias_t = jnp.transpose(bias.reshape(ATTN_KV_HEADS, rep, CHUNK, 2 * CHUNK), (0, 3, 1, 2)).reshape(ATTN_KV_HEADS, 2 * CHUNK, rep * CHUNK)
    o, lse = _attn_fwd(q, kv, bias_t, sinks_p, bl, name="attn_fwd")

    yn, ys, ya, merged, mix, h2, n2 = _merge_fwd(y, z, ssm_norm_g, o, gs, ga, full['w_ssm_proj'], full['w_attn_proj'], full['w_out'],
                                                 h1, mix_post_g, ffn2_pre_g, name="merge_fwd")

    (dh3, loss_vec, df2, dg_post2), saved2, _, _ = _ffn_forward(
        h2, ffn2_pre_g, (full['ffn2_w_gate'], full['ffn2_w_up']), full['ffn2_w_down'], ffn2_post_g, "ffn2", target=tgt2, n=n2)

    grads, own, wire, received = {}, {}, {}, {}
    dh2, grads['ffn2_pre_g'], grads['ffn2_post_g'], g32, g16, _ = _ffn_backward(
        dh3, saved2, ffn2_pre_g, full['ffn2_w_gate'], full['ffn2_w_up'], full['ffn2_w_down'], "ffn2", (df2, dg_post2))
    own.update(zip(ffn2_names, map(_stack8, g32)))
    wire.update(zip(ffn2_names, map(_stack8, g16)))

    dmix, dys, dya, dgs, dga, dy, dz, do, grads['mix_post_g'], grads['ssm_norm_g'] = _merge_bwd(
        dh2, mix, mix_post_g, gs, ga, ys, ya, y, z, ssm_norm_g, full['w_ssm_proj'], full['w_attn_proj'], full['w_out'],
        name="merge_bwd")
    for n, (lhs, rhs) in zip(mixer_names, ((yn, dys), (o, dya), (merged, dmix))):
        g32_, g16_, _ = _mm_tn([lhs], rhs, name=f"d{n}")
        own[n], wire[n] = _stack8(g32_), _stack8(g16_)

    dq, dkv, dbias_t, dsinks = _attn_bwd(do, q, kv, lse, bias_t, sinks_p, bl, name="attn_bwd")
    dbias = jnp.transpose(dbias_t.reshape(ATTN_KV_HEADS, 2 * CHUNK, rep, CHUNK), (0, 2, 3, 1)).reshape(ATTN_Q_HEADS, -1)
    d_table = _small_mm_hi(onehot, dbias, NT, name="rel_bias_bwd")


    first_group = ffn2_names + mixer_names
    (dxc, ddt_raw, ddtb, dalog, ddsk), got = _ssd_bwd(dy, y, xc, dt_raw, hprev, dtb_p, alog_p, dsk_p, bl, n_heads,
                                                      side=_scatter_exchange([wire[n] for n in first_group]), name="ssd_bwd")
    received.update(zip(first_group, got))
    dxbc, dconv_w8, grads['conv_b'] = _conv_bwd(dxc, xbc, conv_w_full, conv_b, bl, name="conv_bwd")

    wide32, wide16, _ = _mm_tn([dgs, dga, dz, dxbc, dq], u, name="dw_in")
    kv32, kv16, _ = _mm_tn([dkv], u, name="dw_in_kv")
    dt32, dt16, _ = _mm_tn([ddt_raw], u, name="dw_in_dt")

    def original_order(wide, kv_part, dt_part):
        return jnp.concatenate([wide[:dt_lo], dt_part[:n_heads], wide[dt_lo:], kv_part], axis=0)
    me = 4 * lax.axis_index("x") + 2 * lax.axis_index("y") + lax.axis_index("c")
    blk_rows = win_t.shape[0] // N_DEV
    wire['w_in'] = _stack8(original_order(wide16, kv16, dt16))
    own_w_in = lax.dynamic_slice_in_dim(original_order(wide32, kv32, dt32), me * blk_rows, blk_rows)
    own['conv_w'] = jnp.transpose(dconv_w8[:SSM_CONV].reshape(SSM_CONV, N_DEV, conv_dim // N_DEV), (1, 0, 2))

    segs = [(g_, 0, off[k_]) for g_, k_ in zip([dgs, dga, dz, dxbc, dq, dkv, ddt_raw], ('gs', 'ga', 'z', 'xbc', 'q', 'kv', 'dt'))]
    dh1, grads['mix_pre_g'], post1, got = _mm_nn_rmsbwd(segs, [win_t], h1, mix_pre_g, dh2, below=(saved1[5], ffn1_post_g), tm=256,
                                                        side=_scatter_exchange([wire['w_in'], own['conv_w']]), name="mix_du")
    received.update(zip(('w_in', 'conv_w'), got))

    dx2, grads['ffn1_pre_g'], grads['ffn1_post_g'], g32, _, got = _ffn_backward(
        dh1, saved1, ffn1_pre_g, full['ffn1_w_gate'], full['ffn1_w_up'], full['ffn1_w_down'], "ffn1", post1, chain=True)
    own.update(zip(ffn1_names, map(_stack8, g32)))
    received.update(zip(ffn1_names, got))

    def own_block(a):
        return lax.dynamic_index_in_dim(a, me, 0, keepdims=False)
    out_g, out_d, out_m, out_v = {}, {}, {}, {}
    for n in big:
        w2, m2, v2 = local_2d(n, args[n]), local_2d(n, args['m_' + n]), local_2d(n, args['v_' + n])
        results = _reduce_adamw(own_w_in if n == 'w_in' else own_block(own[n]), received[n], w2, m2, v2, name=f"update_{n}")
        out_g[n], out_d[n], out_m[n], out_v[n] = ((a.T if n in col_sharded else a)[None] for a in results)
    results = _reduce_adamw(own_block(own['conv_w']), received['conv_w'], conv_w[0], m_conv_w[0], v_conv_w[0], name="update_conv_w")
    out_g['conv_w'], out_d['conv_w'], out_m['conv_w'], out_v['conv_w'] = (a[None] for a in results)

    grads['dt_bias'], grads['a_log'], grads['d_skip'], grads['attn_sinks'], grads['rel_bias_table'] = ddtb, dalog, ddsk, dsinks, d_table
    small = [n for n in weight_names if n not in big and n != 'conv_w']
    *results, loss = _update_replicated([grads[n] for n in small], [args[n] for n in small], [args['m_' + n] for n in small],
                                        [args['v_' + n] for n in small], loss_vec, name="update_replicated")
    for dst, vals in zip((out_g, out_d, out_m, out_v), results):
        dst.update(zip(small, vals))

    grad_x = dx2.reshape(bl, s_len, d)
    return (loss, grad_x, *[out_g[n] for n in weight_names], *[out_d[n] for n in weight_names],
            *[out_m[n] for n in weight_names], *[out_v[n] for n in weight_names])
```

```python
import functools
import math

import numpy as np
import jax
import jax.numpy as jnp
from jax import lax
from jax.experimental import pallas as pl
from jax.experimental.pallas import tpu as pltpu

F32 = jnp.float32
BF16 = jnp.bfloat16
MESH = pl.DeviceIdType.MESH
N_DEV = 8

SSM_HEAD_DIM = 64
SSM_GROUPS = 4
SSM_STATE = 128
SSM_CONV = 4
CHUNK = 128
ATTN_HEAD_DIM = 64
ATTN_Q_HEADS = 16
ATTN_KV_HEADS = 4
REL_BUCKETS = 32
REL_MAX_DISTANCE = 128
RMS_EPS = 1e-6
FFN_RESIDUAL_WEIGHT = 0.5
ADAM_LR, ADAM_B1, ADAM_B2, ADAM_EPS, ADAM_WD, ADAM_STEP = 0.001, 0.9, 0.999, 1e-08, 0.01, 10

LANES = 128
VMEM_LIMIT_BYTES = 56 * 1024 * 1024
FFN_COL_TILE = 1408
SSD_HEAD_BATCH = 2

NEG_INF = float("-inf")


def _params(*sem):
    return pltpu.CompilerParams(dimension_semantics=sem, vmem_limit_bytes=VMEM_LIMIT_BYTES)


def _tile(n, pref, mult=8):
    if n <= pref:
        return n
    t = (pref // mult) * mult
    while t >= mult:
        if n % t == 0:
            return t
        t -= mult
    return n


def _sigmoid(x):
    return 1.0 / (1.0 + jnp.exp(-x))


def _dot(a, b, dims):
    return lax.dot_general(a, b, (dims, ((), ())), preferred_element_type=F32)


NN = ((1,), (0,))
NT = ((1,), (1,))
TN = ((0,), (0,))


def _dot_hi(a, b, dims=NN):
    return lax.dot_general(a, b, (dims, ((), ())), preferred_element_type=F32, precision=lax.Precision.HIGHEST)


def _const_spec(shape):
    nd = len(shape)
    return pl.BlockSpec(shape, lambda *_: (0,) * nd)


def _resident_spec(shape):
    nd = len(shape)
    return pl.BlockSpec(shape, lambda *_: (0,) * nd, pipeline_mode=pl.Buffered(1))


class _Exchange:
    def __init__(self, arrays, out_shape, scratch, start, finish):
        self.arrays, self.out_shape, self.scratch, self.start, self.finish = arrays, out_shape, scratch, start, finish


def _hosted_call(body, *, grid, in_specs, out_specs, out_shape, scratch_shapes, operands, side, name):
    in_specs, out_specs, out_shape, scratch_shapes = list(in_specs), list(out_specs), list(out_shape), list(scratch_shapes)
    sem = ("arbitrary",) * len(grid)
    if side is None:
        outs = pl.pallas_call(body, out_shape=tuple(out_shape), grid=grid, in_specs=in_specs, out_specs=tuple(out_specs),
                              scratch_shapes=scratch_shapes, name=name, compiler_params=_params(*sem))(*operands)
        return tuple(outs), ()
    n_in, n_out, n_scr = len(in_specs), len(out_shape), len(scratch_shapes)
    s_in, s_out = len(side.arrays), len(side.out_shape)

    def wrapped(*refs):
        refs = list(refs)
        main_in, side_in = refs[:n_in], refs[n_in:n_in + s_in]
        o0 = n_in + s_in
        main_out, side_out = refs[o0:o0 + n_out], refs[o0 + n_out:o0 + n_out + s_out]
        c0 = o0 + n_out + s_out
        main_scr, side_scr = refs[c0:c0 + n_scr], refs[c0 + n_scr:]
        ids = [pl.program_id(ax) for ax in range(len(grid))]
        first = functools.reduce(jnp.logical_and, [i == 0 for i in ids])
        last = functools.reduce(jnp.logical_and, [i == g - 1 for i, g in zip(ids, grid)])

        @pl.when(first)
        def _():
            side.start(side_in, side_out, side_scr)

        body(*main_in, *main_out, *main_scr)

        @pl.when(last)
        def _():
            side.finish(side_in, side_out, side_scr)

    hbm = pl.BlockSpec(memory_space=pl.ANY)
    outs = pl.pallas_call(
        wrapped, out_shape=tuple(out_shape + list(side.out_shape)), grid=grid,
        in_specs=in_specs + [hbm] * s_in, out_specs=tuple(out_specs + [hbm] * s_out),
        scratch_shapes=scratch_shapes + list(side.scratch), name=name, compiler_params=_params(*sem))(*operands, *side.arrays)
    return tuple(outs[:n_out]), tuple(outs[n_out:])


def _proj_all(h, g, w, segs, *, name):
    t, d = h.shape
    tm = _tile(t, 512)
    segs = [(row0, wd_, max(wd_, LANES), dt_) for row0, wd_, dt_ in segs]
    assert all(row0 + out_w <= w.shape[0] for row0, _, out_w, _ in segs)

    def body(h_ref, g_ref, w_ref, u_ref, *o_refs):
        hv = h_ref[...]
        r = lax.rsqrt(jnp.mean(hv * hv, axis=-1, keepdims=True) + RMS_EPS)
        uv = (hv * r * g_ref[...]).astype(BF16)
        u_ref[...] = uv
        for (row0, width, out_w, _), o_ref in zip(segs, o_refs):
            for c0, c1 in _col_chunks(out_w, 8 * LANES):
                part = _dot(uv, w_ref[row0 + c0:row0 + c1, :], NT)
                if width < out_w:
                    part = jnp.where(lax.broadcasted_iota(jnp.int32, part.shape, 1) < width, part, 0.0)
                o_ref[:, c0:c1] = part.astype(o_ref.dtype)

    row = pl.BlockSpec((tm, d), lambda i: (i, 0))
    outs = pl.pallas_call(
        body, out_shape=(jax.ShapeDtypeStruct((t, d), BF16),) + tuple(jax.ShapeDtypeStruct((t, ow), dt_) for _, _, ow, dt_ in segs),
        grid=(t // tm,), in_specs=[row, _const_spec((1, d)), _resident_spec(w.shape)],
        out_specs=(row,) + tuple(pl.BlockSpec((tm, ow), lambda i: (i, 0)) for _, _, ow, _ in segs),
        name=name, compiler_params=_params("parallel"))(h, g, w)
    return outs[0], outs[1:]


def _mm_tn(a_list, b, *, tm=1408, tk=2048, side=None, name):
    t, n = b.shape
    tk = _tile(t, tk if len(a_list) == 1 else tk // 2)
    nk = t // tk
    widths = [a.shape[1] for a in a_list]
    tm = _tile(math.gcd(*widths), tm, LANES)
    assert all(w % tm == 0 for w in widths)
    starts = np.cumsum([0] + [w // tm for w in widths])
    nseg = len(a_list)

    def a_spec(s):
        lo, hi = int(starts[s]), int(starts[s + 1])

        def idx(i, k):
            active = jnp.logical_and(i >= lo, i < hi)
            return (jnp.where(active, k, 0), jnp.clip(i - lo, 0, hi - lo - 1))
        return pl.BlockSpec((tk, tm), idx)

    def body(*refs):
        a_refs, b_ref, o_ref, o16_ref, acc = refs[:nseg], refs[nseg], refs[nseg + 1], refs[nseg + 2], refs[nseg + 3]
        i, k = pl.program_id(0), pl.program_id(1)

        @pl.when(k == 0)
        def _():
            acc[...] = jnp.zeros_like(acc)

        bv = b_ref[...].astype(BF16)
        for s in range(nseg):
            lo, hi = int(starts[s]), int(starts[s + 1])

            @pl.when(jnp.logical_and(i >= lo, i < hi))
            def _(s=s):
                acc[...] += _dot(a_refs[s][...].astype(BF16), bv, TN)

        @pl.when(k == nk - 1)
        def _():
            o_ref[...] = acc[...]
            o16_ref[...] = acc[...].astype(BF16)

    rows = int(starts[-1]) * tm
    o_spec = pl.BlockSpec((tm, n), lambda i, k: (i, 0))
    (o32, o16), got = _hosted_call(
        body, out_shape=[jax.ShapeDtypeStruct((rows, n), F32), jax.ShapeDtypeStruct((rows, n), BF16)], grid=(int(starts[-1]), nk),
        in_specs=[a_spec(s) for s in range(nseg)] + [pl.BlockSpec((tk, n), lambda i, k: (k, 0))],
        out_specs=[o_spec, o_spec], scratch_shapes=[pltpu.VMEM((tm, n), F32)], operands=list(a_list) + [b], side=side, name=name)
    return o32, o16, got


def _mm_nn_rmsbwd(segs, weights, x, g, dres, *, below=None, tm=512, side=None, name):
    t, d = x.shape
    tm = _tile(t, tm)
    nseg, nw = len(segs), len(weights)
    n_below = 0 if below is None else 2

    def body(*refs):
        a_refs, w_refs = refs[:nseg], refs[nseg:nseg + nw]
        x_ref, g_ref, dres_ref = refs[nseg + nw:nseg + nw + 3]
        below_refs = refs[nseg + nw + 3:nseg + nw + 3 + n_below]
        dx_ref, dg_ref = refs[nseg + nw + 3 + n_below:nseg + nw + 5 + n_below]

        @pl.when(pl.program_id(0) == 0)
        def _():
            dg_ref[...] = jnp.zeros_like(dg_ref)

        dn = None
        for s, (a, w_idx, row0) in enumerate(segs):
            part = _dot(a_refs[s][...].astype(BF16), w_refs[w_idx][row0:row0 + a.shape[1], :], NN)
            dn = part if dn is None else dn + part
        xv = x_ref[...]
        r = lax.rsqrt(jnp.mean(xv * xv, axis=-1, keepdims=True) + RMS_EPS)
        xhat = xv * r
        dyg = dn * g_ref[...]
        dx = dres_ref[...] + r * (dyg - xhat * jnp.mean(dyg * xhat, axis=-1, keepdims=True))
        dx_ref[...] = dx
        dg_ref[...] += jnp.sum(dn * xhat, axis=0, keepdims=True)
        if below is not None:
            f_ref, gp_ref = below_refs
            df_ref, dgp_ref = refs[nseg + nw + 5 + n_below:]

            @pl.when(pl.program_id(0) == 0)
            def _():
                dgp_ref[...] = jnp.zeros_like(dgp_ref)
            fv = f_ref[...]
            rf = lax.rsqrt(jnp.mean(fv * fv, axis=-1, keepdims=True) + RMS_EPS)
            fhat = fv * rf
            dy = FFN_RESIDUAL_WEIGHT * dx
            dfg = dy * gp_ref[...]
            df_ref[...] = (rf * (dfg - fhat * jnp.mean(dfg * fhat, axis=-1, keepdims=True))).astype(BF16)
            dgp_ref[...] += jnp.sum(dy * fhat, axis=0, keepdims=True)

    row = pl.BlockSpec((tm, d), lambda i: (i, 0))
    vec = _const_spec((1, d))
    in_specs = [pl.BlockSpec((tm, a.shape[1]), lambda i: (i, 0)) for a, _, _ in segs]
    in_specs += [_resident_spec(w.shape) for w in weights] + [row, vec, row] + ([row, vec] if below is not None else [])
    out_specs = [row, vec] + ([row, vec] if below is not None else [])
    out_shape = [jax.ShapeDtypeStruct((t, d), F32), jax.ShapeDtypeStruct((1, d), F32)]
    if below is not None:
        out_shape += [jax.ShapeDtypeStruct((t, d), BF16), jax.ShapeDtypeStruct((1, d), F32)]
    outs, extra = _hosted_call(
        body, grid=(t // tm,), in_specs=in_specs, out_specs=out_specs, out_shape=out_shape, scratch_shapes=[],
        operands=[a for a, _, _ in segs] + list(weights) + [x, g, dres] + (list(below) if below is not None else []),
        side=side, name=name)
    return outs[0], outs[1], (tuple(outs[2:]) if below is not None else None), extra


def _col_chunks(width, chunk=4 * LANES):
    return [(c0, min(c0 + chunk, width)) for c0 in range(0, width, chunk)]


def _rms_fwd(x, g, *, side=None, name):
    t, d = x.shape
    tm = _tile(t, 512)

    def body(x_ref, g_ref, o_ref):
        xv = x_ref[...]
        r = lax.rsqrt(jnp.mean(xv * xv, axis=-1, keepdims=True) + RMS_EPS)
        o_ref[...] = (xv * r * g_ref[...]).astype(o_ref.dtype)

    row = pl.BlockSpec((tm, d), lambda i: (i, 0))
    (n,), got = _hosted_call(body, grid=(t // tm,), in_specs=[row, _const_spec((1, d))], out_specs=[row],
                             out_shape=[jax.ShapeDtypeStruct((t, d), BF16)], scratch_shapes=[], operands=[x, g], side=side, name=name)
    return n, got


def _ffn_up(n, wgt, wut, *, side=None, name):
    t, d = n.shape
    f = wgt.shape[0]
    tm, tn = _tile(t, 1024), _tile(f, FFN_COL_TILE, LANES)

    def body(n_ref, wg_ref, wu_ref, g_ref, u_ref, h_ref):
        nv = n_ref[...]
        gv = _dot(nv, wg_ref[...], NT)
        uv = _dot(nv, wu_ref[...], NT)
        g_ref[...] = gv.astype(BF16)
        u_ref[...] = uv.astype(BF16)
        h_ref[...] = (gv * _sigmoid(gv) * uv).astype(BF16)

    w_spec = pl.BlockSpec((tn, d), lambda j, i: (j, 0))
    o_spec = pl.BlockSpec((tm, tn), lambda j, i: (i, j))
    shp = jax.ShapeDtypeStruct((t, f), BF16)
    return _hosted_call(body, grid=(f // tn, t // tm), in_specs=[pl.BlockSpec((tm, d), lambda j, i: (i, 0)), w_spec, w_spec],
                        out_specs=[o_spec, o_spec, o_spec], out_shape=[shp, shp, shp], scratch_shapes=[], operands=[n, wgt, wut],
                        side=side, name=name)


def _rms_residual(acc, h, gp, weight):
    r = lax.rsqrt(jnp.mean(acc * acc, axis=-1, keepdims=True) + RMS_EPS)
    return h + weight * (acc * r * gp)


def _ffn_down(hid, wd, h_in, gp, *, target=None, side=None, name):
    t, f = hid.shape
    d = wd.shape[1]
    tm = _tile(t, 512)
    row = pl.BlockSpec((tm, d), lambda i: (i, 0))
    shp = jax.ShapeDtypeStruct((t, d), F32)
    in_specs = [pl.BlockSpec((tm, f), lambda i: (i, 0)), _resident_spec((f, d)), row, _const_spec((1, d))]

    if target is None:
        def body(hid_ref, wd_ref, hin_ref, gp_ref, f_ref, hout_ref):
            acc = _dot(hid_ref[...], wd_ref[...], NN)
            f_ref[...] = acc
            hout_ref[...] = _rms_residual(acc, hin_ref[...], gp_ref[...], FFN_RESIDUAL_WEIGHT)

        return _hosted_call(body, grid=(t // tm,), in_specs=in_specs, out_specs=[row, row], out_shape=[shp, shp], scratch_shapes=[],
                            operands=[hid, wd, h_in, gp], side=side, name=name)

    def body_loss(hid_ref, wd_ref, hin_ref, gp_ref, tgt_ref, f_ref, dh_ref, loss_ref, df_ref, dgp_ref):
        @pl.when(pl.program_id(0) == 0)
        def _():
            loss_ref[...] = jnp.zeros_like(loss_ref)
            dgp_ref[...] = jnp.zeros_like(dgp_ref)
        acc = _dot(hid_ref[...], wd_ref[...], NN)
        f_ref[...] = acc
        r = lax.rsqrt(jnp.mean(acc * acc, axis=-1, keepdims=True) + RMS_EPS)
        fhat = acc * r
        e = hin_ref[...] + FFN_RESIDUAL_WEIGHT * (fhat * gp_ref[...]) - tgt_ref[...]
        dh = e * (1.0 / d)
        dh_ref[...] = dh
        per_row = jnp.sum(e * e, axis=1, keepdims=True) * (1.0 / d)
        loss_ref[...] += 0.5 * jnp.sum(per_row, axis=0, keepdims=True)
        dy = FFN_RESIDUAL_WEIGHT * dh
        dyg = dy * gp_ref[...]
        df_ref[...] = (r * (dyg - fhat * jnp.mean(dyg * fhat, axis=-1, keepdims=True))).astype(BF16)
        dgp_ref[...] += jnp.sum(dy * fhat, axis=0, keepdims=True)

    return _hosted_call(body_loss, grid=(t // tm,), in_specs=in_specs + [row],
                        out_specs=[row, row, _const_spec((1, LANES)), row, _const_spec((1, d))],
                        out_shape=[shp, shp, jax.ShapeDtypeStruct((1, LANES), F32), jax.ShapeDtypeStruct((t, d), BF16),
                                   jax.ShapeDtypeStruct((1, d), F32)],
                        scratch_shapes=[], operands=[hid, wd, h_in, gp, target], side=side, name=name)


def _ffn_dhid(df, wd, g, u, *, name):
    t, d = df.shape
    f = wd.shape[0]
    tm, tn = _tile(t, 1024), _tile(f, FFN_COL_TILE, LANES)

    def body(df_ref, wd_ref, g_ref, u_ref, dg_ref, du_ref):
        dh = _dot(df_ref[...], wd_ref[...], NT)
        gv = g_ref[...].astype(F32)
        uv = u_ref[...].astype(F32)
        sg = _sigmoid(gv)
        silu = gv * sg
        dg_ref[...] = (dh * uv * (sg + silu * (1.0 - sg))).astype(BF16)
        du_ref[...] = (dh * silu).astype(BF16)

    o_spec = pl.BlockSpec((tm, tn), lambda j, i: (i, j))
    shp = jax.ShapeDtypeStruct((t, f), BF16)
    return pl.pallas_call(body, out_shape=(shp, shp), grid=(f // tn, t // tm),
                          in_specs=[pl.BlockSpec((tm, d), lambda j, i: (i, 0)), pl.BlockSpec((tn, d), lambda j, i: (j, 0)), o_spec, o_spec],
                          out_specs=(o_spec, o_spec), name=name, compiler_params=_params("parallel", "arbitrary"))(df, wd, g, u)


def _ffn_forward(h_in, g_pre, w_up, wd, g_post, tag, side_norm=None, side_up=None, target=None, n=None):
    got_norm = ()
    if n is None:
        n, got_norm = _rms_fwd(h_in, g_pre, side=side_norm, name=f"{tag}_prenorm")
    wgt, wut = w_up(got_norm) if callable(w_up) else w_up
    (g, u, hid), got_up = _ffn_up(n, wgt, wut, side=side_up, name=f"{tag}_up")
    wd = wd(got_up) if callable(wd) else wd
    outs, _ = _ffn_down(hid, wd, h_in, g_post, target=target, name=f"{tag}_down")
    return outs[1:], (h_in, n, g, u, hid, outs[0]), (wgt, wut, wd), got_up


def _stack8(g):
    return g.reshape(N_DEV, g.shape[0] // N_DEV, g.shape[1])


def _ffn_backward(dh_out, saved, g_pre, wgt, wut, wd, tag, post, chain=False):
    h_in, n, g, u, hid, _ = saved

    def side_of(grad16):
        return _scatter_exchange([_stack8(grad16)]) if chain else None

    df, dg_post = post
    dgate, dup = _ffn_dhid(df, wd, g, u, name=f"{tag}_dhid")
    d_wd, d_wd16, _ = _mm_tn([hid], df, name=f"{tag}_dwd")
    d_wgt, d_wgt16, got_wd = _mm_tn([dgate], n, side=side_of(d_wd16), name=f"{tag}_dwg")
    d_wut, d_wut16, got_wg = _mm_tn([dup], n, side=side_of(d_wgt16), name=f"{tag}_dwu")
    dh_in, dg_pre, _, got_wu = _mm_nn_rmsbwd([(dgate, 0, 0), (dup, 1, 0)], [wgt, wut], h_in, g_pre, dh_out, side=side_of(d_wut16),
                                             name=f"{tag}_dn")
    received = (got_wg[0], got_wu[0], got_wd[0]) if chain else None
    return dh_in, dg_pre, dg_post, (d_wgt, d_wut, d_wd), (d_wgt16, d_wut16, d_wd16), received


CONV_ROWS = 128
HALO = 8


def _taps(w_ref):
    return [w_ref[k:k + 1, :] for k in range(SSM_CONV)]


def _conv_chunk(x_ref, xs, r0, taps, bias):
    xs[HALO + r0:HALO + r0 + CONV_ROWS, :] = x_ref[r0:r0 + CONV_ROWS, :].astype(F32)
    shifted = [xs[HALO + r0 - k:HALO + r0 - k + CONV_ROWS, :] for k in range(SSM_CONV)]
    pre = bias + shifted[0] * taps[SSM_CONV - 1]
    for k in range(1, SSM_CONV):
        pre = pre + shifted[k] * taps[SSM_CONV - 1 - k]
    return shifted, pre


def _fold_rows(a):
    return functools.reduce(jnp.add, [a[i:i + 8] for i in range(0, a.shape[0], 8)])


def _conv_fwd(xbc, conv_w, conv_b, bl, *, name):
    t, c = xbc.shape
    s = t // bl
    tc = LANES
    assert s % CONV_ROWS == 0

    def body(x_ref, w_ref, b_ref, o_ref, xs):
        taps, bias = _taps(w_ref), b_ref[...]
        xs[0:HALO, :] = jnp.zeros((HALO, tc), F32)
        for r0 in range(0, s, CONV_ROWS):
            _, pre = _conv_chunk(x_ref, xs, r0, taps, bias)
            o_ref[r0:r0 + CONV_ROWS, :] = (pre * _sigmoid(pre)).astype(o_ref.dtype)

    blk = pl.BlockSpec((s, tc), lambda b, j: (b, j))
    return pl.pallas_call(body, out_shape=jax.ShapeDtypeStruct((t, c), BF16), grid=(bl, c // tc),
                          in_specs=[blk, pl.BlockSpec((SSM_CONV, tc), lambda b, j: (0, j)), pl.BlockSpec((1, tc), lambda b, j: (0, j))],
                          out_specs=blk, scratch_shapes=[pltpu.VMEM((HALO + s, tc), F32)],
                          name=name, compiler_params=_params("parallel", "arbitrary"))(xbc, conv_w, conv_b)


def _conv_bwd(dxc, xbc, conv_w, conv_b, bl, *, name):
    t, c = xbc.shape
    s = t // bl
    tc = LANES

    def body(dy_ref, x_ref, w_ref, b_ref, dx_ref, dw_ref, db_ref, xs, dpre_s):
        @pl.when(pl.program_id(1) == 0)
        def _():
            dw_ref[...] = jnp.zeros_like(dw_ref)
            db_ref[...] = jnp.zeros_like(db_ref)

        taps, bias = _taps(w_ref), b_ref[...]
        zero8 = jnp.zeros((HALO, tc), F32)
        xs[0:HALO, :] = zero8
        dpre_s[s:s + HALO, :] = zero8
        sums = [zero8] * (SSM_CONV + 1)
        for r0 in range(0, s, CONV_ROWS):
            shifted, pre = _conv_chunk(x_ref, xs, r0, taps, bias)
            sg = _sigmoid(pre)
            dpre = dy_ref[r0:r0 + CONV_ROWS, :].astype(F32) * (sg * (1.0 + pre * (1.0 - sg)))
            dpre_s[r0:r0 + CONV_ROWS, :] = dpre
            sums = [acc + _fold_rows(dpre * sh) for acc, sh in zip(sums[:-1], shifted)] + [sums[-1] + _fold_rows(dpre)]
        for k in range(SSM_CONV):
            dw_ref[SSM_CONV - 1 - k:SSM_CONV - k, :] += jnp.sum(sums[k], axis=0, keepdims=True)
        db_ref[...] += jnp.sum(sums[-1], axis=0, keepdims=True)
        for r0 in range(0, s, CONV_ROWS):
            dx = dpre_s[r0:r0 + CONV_ROWS, :] * taps[SSM_CONV - 1]
            for k in range(1, SSM_CONV):
                dx = dx + dpre_s[r0 + k:r0 + k + CONV_ROWS, :] * taps[SSM_CONV - 1 - k]
            dx_ref[r0:r0 + CONV_ROWS, :] = dx.astype(dx_ref.dtype)

    blk = pl.BlockSpec((s, tc), lambda j, b: (b, j))
    return pl.pallas_call(
        body, out_shape=(jax.ShapeDtypeStruct((t, c), BF16), jax.ShapeDtypeStruct((8, c), F32), jax.ShapeDtypeStruct((1, c), F32)),
        grid=(c // tc, bl),
        in_specs=[blk, blk, pl.BlockSpec((SSM_CONV, tc), lambda j, b: (0, j)), pl.BlockSpec((1, tc), lambda j, b: (0, j))],
        out_specs=(blk, pl.BlockSpec((8, tc), lambda j, b: (0, j)), pl.BlockSpec((1, tc), lambda j, b: (0, j))),
        scratch_shapes=[pltpu.VMEM((HALO + s, tc), F32), pltpu.VMEM((s + HALO, tc), F32)],
        name=name, compiler_params=_params("parallel", "arbitrary"))(dxc, xbc, conv_w, conv_b)


def _softplus(x):
    return jnp.maximum(x, 0.0) + jnp.log1p(jnp.exp(-jnp.abs(x)))


def _hilo_dot(v, m_b, dims=NN):
    hi = v.astype(BF16)
    lo = (v - hi.astype(F32)).astype(BF16)
    return _dot(hi, m_b, dims) + _dot(lo, m_b, dims)


def _ssd_chunk_common(dtraw_ref, dtb_ref, alog_ref, dsk_ref, d_inner):
    q, p = CHUNK, SSM_HEAD_DIM
    shift = p.bit_length() - 1
    assert 1 << shift == p
    dt = _softplus(dtraw_ref[...] + dtb_ref[...])
    a = -jnp.exp(alog_ref[...])
    ii = lax.broadcasted_iota(jnp.int32, (q, q), 0)
    jj = lax.broadcasted_iota(jnp.int32, (q, q), 1)
    causal = ii >= jj
    tril = jnp.where(causal, 1.0, 0.0).astype(F32)
    triu = jnp.where(ii <= jj, 1.0, 0.0).astype(F32)
    a_cs = _dot_hi(tril, dt * a)
    a_cs_t = a_cs.T
    a_last = a_cs[q - 1:q, :]
    e_col = jnp.exp(a_cs)
    dec_end = jnp.exp(a_last - a_cs)
    head_of_col = lax.shift_right_logical(lax.broadcasted_iota(jnp.int32, (LANES, d_inner), 1), shift)
    spread = (lax.broadcasted_iota(jnp.int32, (LANES, d_inner), 0) == head_of_col).astype(BF16)
    exact = jnp.concatenate([dt, jnp.broadcast_to(dsk_ref[...], (8, LANES))], axis=0)
    hi = jnp.concatenate([exact, e_col, dec_end], axis=0).astype(BF16)
    lo = (exact - hi[:q + 8].astype(F32)).astype(BF16)
    wide = _dot(hi, spread, NN)
    fine = wide[:q + 8] + _dot(lo, spread, NN)
    return dict(dt=dt, a=a, a_cs=a_cs, a_cs_t=a_cs_t, a_last=a_last, dec_end=dec_end, causal=causal, triu=triu,
                dt_e=fine[:q], dsk_e=fine[q:q + 1], e_e=wide[q + 8:2 * q + 8], dec_e=wide[2 * q + 8:3 * q + 8])


def _fill_block_diag(bd_ref, src_ref, hpg, col0=0):
    q, p = CHUNK, SSM_HEAD_DIM
    for hh in range(hpg):
        bd_ref[hh * q:(hh + 1) * q, hh * p:(hh + 1) * p] = src_ref[:, col0 + hh * p:col0 + (hh + 1) * p]


def _lane_onehot(h):
    return (lax.broadcasted_iota(jnp.int32, (1, LANES), 1) == h).astype(F32)


def _ssd_fwd(xc, dt_raw, dt_bias, a_log, d_skip, bl, n_heads, *, side=None, name):
    t = xc.shape[0]
    q, p, nst, grp = CHUNK, SSM_HEAD_DIM, SSM_STATE, SSM_GROUPS
    d_inner = n_heads * p
    hpg = n_heads // grp
    hb = min(hpg, SSD_HEAD_BATCH)
    gw = hpg * p
    nc = t // bl // q
    assert d_inner % (grp * nst) == 0 and nst == LANES and hpg % hb == 0

    def body(xs_ref, b_ref, c_ref, dtraw_ref, dtb_ref, alog_ref, dsk_ref, y_ref, hprev_ref, state, m_all, x_bd, xdt_s):
        @pl.when(jnp.logical_and(pl.program_id(0) == 0, pl.program_id(1) == 0))
        def _():
            x_bd[...] = jnp.zeros_like(x_bd)

        @pl.when(pl.program_id(1) == 0)
        def _():
            state[...] = jnp.zeros_like(state)

        cm = _ssd_chunk_common(dtraw_ref, dtb_ref, alog_ref, dsk_ref, d_inner)
        for g in range(grp):
            cols = slice(g * gw, (g + 1) * gw)
            bg = b_ref[:, g * nst:(g + 1) * nst]
            cg = c_ref[:, g * nst:(g + 1) * nst]
            scores = _dot(cg, bg, NT)
            xs = xs_ref[:, cols].astype(F32)
            xdt = xs * cm['dt_e'][:, cols]
            xdt_s[...] = xdt.astype(BF16)
            y_parts = []
            for sub in range(hpg // hb):
                for k in range(hb):
                    h = g * hpg + sub * hb + k
                    seg = cm['a_cs'][:, h:h + 1] - cm['a_cs_t'][h:h + 1, :]
                    m_all[:, k * q:(k + 1) * q] = (scores * jnp.exp(jnp.where(cm['causal'], seg, NEG_INF))).astype(BF16)
                _fill_block_diag(x_bd, xdt_s, hb, sub * hb * p)
                y_parts.append(_dot(m_all[...], x_bd[...], NN))
            hprev = state[g]
            hprev_ref[g] = hprev
            y = jnp.concatenate(y_parts, axis=1) + cm['e_e'][:, cols] * _dot(cg, hprev.astype(BF16), NT)
            y_ref[:, cols] = y + cm['dsk_e'][:, cols] * xs
            st = _dot((xdt * cm['dec_e'][:, cols]).astype(BF16), bg, TN)
            for hh in range(hpg):
                h = g * hpg + hh
                rows = slice(hh * p, (hh + 1) * p)
                state[g, rows, :] = jnp.exp(cm['a_last'][:, h:h + 1]) * hprev[rows] + st[rows]

    gn = grp * nst

    def rowmap(b, c):
        return b * nc + c
    vec = pl.BlockSpec((1, LANES), lambda b, c: (0, 0))
    return _hosted_call(
        body,
        out_shape=[jax.ShapeDtypeStruct((t, d_inner), F32), jax.ShapeDtypeStruct((t // q, grp, gw, nst), F32)],
        grid=(bl, nc),
        in_specs=[pl.BlockSpec((q, d_inner), lambda b, c: (rowmap(b, c), 0)),
                  pl.BlockSpec((q, gn), lambda b, c: (rowmap(b, c), d_inner // gn)),
                  pl.BlockSpec((q, gn), lambda b, c: (rowmap(b, c), d_inner // gn + 1)),
                  pl.BlockSpec((q, LANES), lambda b, c: (rowmap(b, c), 0)), vec, vec, vec],
        out_specs=[pl.BlockSpec((q, d_inner), lambda b, c: (rowmap(b, c), 0)),
                   pl.BlockSpec((None, grp, gw, nst), lambda b, c: (rowmap(b, c), 0, 0, 0))],
        scratch_shapes=[pltpu.VMEM((grp, gw, nst), F32), pltpu.VMEM((q, hb * q), BF16), pltpu.VMEM((hb * q, hb * p), BF16),
                        pltpu.VMEM((q, gw), BF16)],
        operands=[xc, xc, xc, dt_raw, dt_bias, a_log, d_skip], side=side, name=name)


def _ssd_bwd(dy, y, xc, dt_raw, hprev_all, dt_bias, a_log, d_skip, bl, n_heads, *, side=None, name):
    t, c_dim = xc.shape
    q, p, nst, grp = CHUNK, SSM_HEAD_DIM, SSM_STATE, SSM_GROUPS
    d_inner = n_heads * p
    hpg = n_heads // grp
    hb = min(hpg, SSD_HEAD_BATCH)
    gw = hpg * p
    nc = t // bl // q
    gn = grp * nst
    shift = p.bit_length() - 1

    def body(dy_ref, y_ref, xs_ref, b_ref, c_ref, dtraw_ref, hprev_ref, dtb_ref, alog_ref, dsk_ref,
             dxc_ref, ddtraw_ref, ddtb_ref, dalog_ref, ddsk_ref, dstate, mt_all, x_bd, dy_bd, xdt_s):
        @pl.when(jnp.logical_and(pl.program_id(0) == 0, pl.program_id(1) == 0))
        def _():
            ddtb_ref[...] = jnp.zeros_like(ddtb_ref)
            dalog_ref[...] = jnp.zeros_like(dalog_ref)
            ddsk_ref[...] = jnp.zeros_like(ddsk_ref)
            x_bd[...] = jnp.zeros_like(x_bd)
            dy_bd[...] = jnp.zeros_like(dy_bd)

        @pl.when(pl.program_id(1) == 0)
        def _():
            dstate[...] = jnp.zeros_like(dstate)

        cm = _ssd_chunk_common(dtraw_ref, dtb_ref, alog_ref, dsk_ref, d_inner)
        causal = cm['causal']
        upper = cm['triu'] > 0.5
        seg_row = lax.shift_right_logical(lax.broadcasted_iota(jnp.int32, (gw, LANES), 0), shift)
        seg_lane = lax.broadcasted_iota(jnp.int32, (gw, LANES), 1)
        sums = jnp.zeros((5 * q, LANES), F32)
        state_dot = jnp.zeros((1, LANES), F32)
        for g in range(grp):
            cols = slice(g * gw, (g + 1) * gw)
            seg_sum = (seg_row + g * hpg == seg_lane).astype(BF16)
            bg = b_ref[:, g * nst:(g + 1) * nst]
            cg = c_ref[:, g * nst:(g + 1) * nst]
            scores_t = _dot(bg, cg, NT)
            xs = xs_ref[:, cols].astype(F32)
            xdt = xs * cm['dt_e'][:, cols]
            xdt_s[...] = xdt.astype(BF16)
            dyf = dy_ref[:, cols].astype(F32)
            dscores = jnp.zeros((q, q), F32)
            dx_parts = []
            for sub in range(hpg // hb):
                c0 = sub * hb * p
                _fill_block_diag(x_bd, xdt_s, hb, c0)
                _fill_block_diag(dy_bd, dy_ref, hb, g * gw + c0)
                dm_all = _dot(dy_ref[:, g * gw + c0:g * gw + c0 + hb * p], x_bd[...], NT)
                for k in range(hb):
                    h = g * hpg + sub * hb + k
                    blk = slice(k * q, (k + 1) * q)
                    seg = cm['a_cs'][:, h:h + 1] - cm['a_cs_t'][h:h + 1, :]
                    mt_all[:, blk] = (scores_t * jnp.exp(jnp.where(upper, -seg, NEG_INF))).astype(BF16)
                    dscores = dscores + dm_all[:, blk] * jnp.exp(jnp.where(causal, seg, NEG_INF))
                dx_parts.append(_dot(mt_all[...], dy_bd[...], NN))
            hprev = hprev_ref[g]
            hprev_b = hprev.astype(BF16)
            dhn = dstate[g]
            dhn_b = dhn.astype(BF16)
            e_e, dec_e = cm['e_e'][:, cols], cm['dec_e'][:, cols]
            y_scan = y_ref[:, cols] - cm['dsk_e'][:, cols] * xs
            dye_b = (dyf * e_e).astype(BF16)
            dcg = _dot(dye_b, hprev_b, NN)
            dhp = _dot(dye_b, cg, TN)
            bdh = _dot(bg, dhn_b, NT)
            dbg = _dot((xdt * dec_e).astype(BF16), dhn_b, NN)
            dx_diag = jnp.concatenate(dx_parts, axis=1)
            dx = dec_e * bdh + dx_diag
            ds_b = dscores.astype(BF16)
            dcg = dcg + _dot(ds_b, bg, NN)
            dbg = dbg + _dot(ds_b, cg, TN)
            x_rounded = xdt_s[...].astype(F32)
            sums = sums + _hilo_dot(jnp.concatenate([dyf * y_scan, xdt * bdh, x_rounded * dx_diag, dx * xs, dyf * xs], axis=0), seg_sum)
            state_dot = state_dot + jnp.sum(_hilo_dot(dhn * hprev, seg_sum, TN), axis=0, keepdims=True)
            dxc_ref[:, cols] = (dx * cm['dt_e'][:, cols] + cm['dsk_e'][:, cols] * dyf).astype(dxc_ref.dtype)
            dxc_ref[:, d_inner + g * nst:d_inner + (g + 1) * nst] = dbg.astype(dxc_ref.dtype)
            dxc_ref[:, d_inner + gn + g * nst:d_inner + gn + (g + 1) * nst] = dcg.astype(dxc_ref.dtype)
            for hh in range(hpg):
                h = g * hpg + hh
                rows = slice(hh * p, (hh + 1) * p)
                dstate[g, rows, :] = jnp.exp(cm['a_last'][:, h:h + 1]) * dhn[rows] + dhp[rows]
        s_y, s_end, s_diag, s_dt, s_skip = (sums[k * q:(k + 1) * q] for k in range(5))
        dt, a, dec_end = cm['dt'], cm['a'], cm['dec_end']
        last_row = (lax.broadcasted_iota(jnp.int32, (q, 1), 0) == q - 1).astype(F32)
        da_last = jnp.sum(dec_end * s_end, axis=0, keepdims=True) + jnp.exp(cm['a_last']) * state_dot
        da = s_y - dec_end * s_end - s_diag + last_row * da_last
        ddta = _dot_hi(cm['triu'], da)
        ddt = s_dt + ddta * a
        d_a = jnp.sum(ddta * dt, axis=0, keepdims=True)
        ddt_raw = ddt * _sigmoid(dtraw_ref[...] + dtb_ref[...])
        ddtraw_ref[...] = ddt_raw
        ddtb_ref[...] += jnp.sum(ddt_raw, axis=0, keepdims=True)
        dalog_ref[...] += d_a * a
        ddsk_ref[...] += jnp.sum(s_skip, axis=0, keepdims=True)

    def rowmap(b, c):
        return b * nc + (nc - 1 - c)
    vec = pl.BlockSpec((1, LANES), lambda b, c: (0, 0))
    vec_shape = jax.ShapeDtypeStruct((1, LANES), F32)
    return _hosted_call(
        body,
        out_shape=[jax.ShapeDtypeStruct((t, c_dim), BF16), jax.ShapeDtypeStruct((t, LANES), F32), vec_shape, vec_shape, vec_shape],
        grid=(bl, nc),
        in_specs=[pl.BlockSpec((q, d_inner), lambda b, c: (rowmap(b, c), 0)),
                  pl.BlockSpec((q, d_inner), lambda b, c: (rowmap(b, c), 0)),
                  pl.BlockSpec((q, d_inner), lambda b, c: (rowmap(b, c), 0)),
                  pl.BlockSpec((q, gn), lambda b, c: (rowmap(b, c), d_inner // gn)),
                  pl.BlockSpec((q, gn), lambda b, c: (rowmap(b, c), d_inner // gn + 1)),
                  pl.BlockSpec((q, LANES), lambda b, c: (rowmap(b, c), 0)),
                  pl.BlockSpec((None, grp, gw, nst), lambda b, c: (rowmap(b, c), 0, 0, 0)), vec, vec, vec],
        out_specs=[pl.BlockSpec((q, c_dim), lambda b, c: (rowmap(b, c), 0)),
                   pl.BlockSpec((q, LANES), lambda b, c: (rowmap(b, c), 0)), vec, vec, vec],
        scratch_shapes=[pltpu.VMEM((grp, gw, nst), F32), pltpu.VMEM((q, hb * q), BF16),
                        pltpu.VMEM((hb * q, hb * p), BF16), pltpu.VMEM((hb * q, hb * p), BF16), pltpu.VMEM((q, gw), BF16)],
        operands=[dy, y, xc, xc, xc, dt_raw, hprev_all, dt_bias, a_log, d_skip], side=side, name=name)


def _bucket_onehot():
    blk = CHUNK
    qi = jnp.arange(blk)[:, None]
    kj = jnp.arange(2 * blk)[None, :]
    dist = jnp.maximum(qi + blk - kj, 0)
    max_exact = REL_BUCKETS // 2
    d = jnp.maximum(dist, 1).astype(F32)
    large = max_exact + (jnp.log(d / max_exact) / math.log(REL_MAX_DISTANCE / max_exact) * (REL_BUCKETS - max_exact)).astype(jnp.int32)
    large = jnp.minimum(large, REL_BUCKETS - 1)
    bucket = jnp.where(dist < max_exact, dist, large).reshape(-1)
    return (bucket[None, :] == jnp.arange(REL_BUCKETS)[:, None]).astype(F32)


def _small_mm_hi(a, b, dims, *, name):
    def body(a_ref, b_ref, o_ref):
        o_ref[...] = _dot_hi(a_ref[...], b_ref[...], dims)
    n = b.shape[0] if dims == NT else b.shape[1]
    return pl.pallas_call(body, out_shape=jax.ShapeDtypeStruct((a.shape[0], n), F32), name=name)(a, b)


def _attn_band_mask_t(n, rep):
    blk = CHUNK
    jj = lax.broadcasted_iota(jnp.int32, (2 * blk, rep * blk), 0)
    ii = lax.broadcasted_iota(jnp.int32, (2 * blk, rep * blk), 1) & (blk - 1)
    dist = ii + blk - jj
    in_window = jnp.logical_and(dist >= 0, dist < blk)
    return jnp.logical_and(in_window, jnp.logical_or(jj >= blk, n > 0))


def _sink_row(sink_ref, heads):
    return jnp.concatenate([jnp.broadcast_to(sink_ref[:, h:h + 1], (1, CHUNK)) for h in heads], axis=1)


def _attn_fwd(q, kv, bias_t, sinks, bl, *, name):
    t, qd = q.shape
    blk, hd = CHUNK, ATTN_HEAD_DIM
    kvd = ATTN_KV_HEADS * hd
    rep = ATTN_Q_HEADS // ATTN_KV_HEADS
    nb = t // bl // blk
    scale = hd ** -0.5

    def body(q_ref, kp_ref, kc_ref, vp_ref, vc_ref, bias_ref, sink_ref, o_ref, lse_ref):
        n = pl.program_id(1)
        mask = _attn_band_mask_t(n, rep)
        for kvh in range(ATTN_KV_HEADS):
            ks = slice(kvh * hd, (kvh + 1) * hd)
            heads = range(kvh * rep, (kvh + 1) * rep)
            qs = jnp.concatenate([q_ref[:, h * hd:(h + 1) * hd] for h in heads], axis=0)
            kk = jnp.concatenate([kp_ref[:, ks], kc_ref[:, ks]], axis=0)
            vv = jnp.concatenate([vp_ref[:, ks], vc_ref[:, ks]], axis=0)
            s = jnp.where(mask, _dot(kk, qs, NT) * scale + bias_ref[kvh], NEG_INF)
            sink = _sink_row(sink_ref, heads)
            m = jnp.maximum(jnp.max(s, axis=0, keepdims=True), sink)
            p = jnp.exp(s - m)
            den = jnp.sum(p, axis=0, keepdims=True) + jnp.exp(sink - m)
            o = _dot((p * (1.0 / den)).astype(BF16), vv, TN)
            lse = m + jnp.log(den)
            for r, h in enumerate(heads):
                o_ref[:, h * hd:(h + 1) * hd] = o[r * blk:(r + 1) * blk].astype(o_ref.dtype)
                lse_ref[h:h + 1, :] = lse[:, r * blk:(r + 1) * blk]

    def cur(b, n):
        return b * nb + n

    def prev(b, n):
        return b * nb + jnp.maximum(n - 1, 0)
    return pl.pallas_call(
        body, out_shape=(jax.ShapeDtypeStruct((t, qd), BF16), jax.ShapeDtypeStruct((t // blk * ATTN_Q_HEADS, blk), F32)), grid=(bl, nb),
        in_specs=[pl.BlockSpec((blk, qd), lambda b, n: (cur(b, n), 0)),
                  pl.BlockSpec((blk, kvd), lambda b, n: (prev(b, n), 0)), pl.BlockSpec((blk, kvd), lambda b, n: (cur(b, n), 0)),
                  pl.BlockSpec((blk, kvd), lambda b, n: (prev(b, n), 1)), pl.BlockSpec((blk, kvd), lambda b, n: (cur(b, n), 1)),
                  _const_spec(bias_t.shape), _const_spec((1, LANES))],
        out_specs=(pl.BlockSpec((blk, qd), lambda b, n: (cur(b, n), 0)),
                   pl.BlockSpec((ATTN_Q_HEADS, blk), lambda b, n: (cur(b, n), 0))),
        name=name, compiler_params=_params("parallel", "arbitrary"))(q, kv, kv, kv, kv, bias_t, sinks)


def _attn_bwd(do, q, kv, lse, bias_t, sinks, bl, *, name):
    t, qd = q.shape
    blk, hd = CHUNK, ATTN_HEAD_DIM
    kvd = ATTN_KV_HEADS * hd
    rep = ATTN_Q_HEADS // ATTN_KV_HEADS
    s_len = t // bl
    nb = s_len // blk
    scale = hd ** -0.5

    def body(do_ref, q_ref, kp_ref, kc_ref, vp_ref, vc_ref, lse_ref, bias_ref, sink_ref, dq_ref, dkv_ref, dbias_ref, dsink_ref):
        n = pl.program_id(1)

        @pl.when(jnp.logical_and(pl.program_id(0) == 0, n == 0))
        def _():
            dbias_ref[...] = jnp.zeros_like(dbias_ref)
            dsink_ref[...] = jnp.zeros_like(dsink_ref)

        mask = _attn_band_mask_t(n, rep)
        r_cur = pl.multiple_of(n * blk, blk)
        r_prev = pl.multiple_of(jnp.maximum(n - 1, 0) * blk, blk)
        dsink = jnp.zeros((1, LANES), F32)
        for kvh in range(ATTN_KV_HEADS):
            ks = slice(kvh * hd, (kvh + 1) * hd)
            heads = range(kvh * rep, (kvh + 1) * rep)
            qs = jnp.concatenate([q_ref[:, h * hd:(h + 1) * hd] for h in heads], axis=0)
            dos = jnp.concatenate([do_ref[:, h * hd:(h + 1) * hd] for h in heads], axis=0)
            kk = jnp.concatenate([kp_ref[:, ks], kc_ref[:, ks]], axis=0)
            vv = jnp.concatenate([vp_ref[:, ks], vc_ref[:, ks]], axis=0)
            lse = jnp.concatenate([lse_ref[h:h + 1, :] for h in heads], axis=1)
            p = jnp.exp(jnp.where(mask, _dot(kk, qs, NT) * scale + bias_ref[kvh], NEG_INF) - lse)
            dp = _dot(vv, dos, NT)
            delta = jnp.sum(p * dp, axis=0, keepdims=True)
            ds = p * (dp - delta)
            dsink_row = jnp.exp(_sink_row(sink_ref, heads) - lse) * delta
            dbias_ref[kvh] += ds
            ds_b = ds.astype(BF16)
            dq_s = _dot(ds_b, kk, TN) * scale
            dkk = _dot(ds_b, qs, NN) * scale
            dvv = _dot(p.astype(BF16), dos, NN)
            for r, h in enumerate(heads):
                dq_ref[:, h * hd:(h + 1) * hd] = dq_s[r * blk:(r + 1) * blk].astype(dq_ref.dtype)
                dsink = dsink - jnp.sum(dsink_row[:, r * blk:(r + 1) * blk], axis=1, keepdims=True) * _lane_onehot(h)
            vs = slice(kvd + kvh * hd, kvd + (kvh + 1) * hd)
            dkv_ref[pl.ds(r_cur, blk), ks] = dkk[blk:]
            dkv_ref[pl.ds(r_cur, blk), vs] = dvv[blk:]

            @pl.when(n > 0)
            def _():
                dkv_ref[pl.ds(r_prev, blk), ks] += dkk[:blk]
                dkv_ref[pl.ds(r_prev, blk), vs] += dvv[:blk]
        dsink_ref[...] += dsink

    def cur(b, n):
        return b * nb + n

    def prev(b, n):
        return b * nb + jnp.maximum(n - 1, 0)
    qspec = pl.BlockSpec((blk, qd), lambda b, n: (cur(b, n), 0))
    return pl.pallas_call(
        body,
        out_shape=(jax.ShapeDtypeStruct((t, qd), BF16), jax.ShapeDtypeStruct((t, 2 * kvd), F32),
                   jax.ShapeDtypeStruct(bias_t.shape, F32), jax.ShapeDtypeStruct((1, LANES), F32)),
        grid=(bl, nb),
        in_specs=[qspec, qspec,
                  pl.BlockSpec((blk, kvd), lambda b, n: (prev(b, n), 0)), pl.BlockSpec((blk, kvd), lambda b, n: (cur(b, n), 0)),
                  pl.BlockSpec((blk, kvd), lambda b, n: (prev(b, n), 1)), pl.BlockSpec((blk, kvd), lambda b, n: (cur(b, n), 1)),
                  pl.BlockSpec((ATTN_Q_HEADS, blk), lambda b, n: (cur(b, n), 0)), _const_spec(bias_t.shape), _const_spec((1, LANES))],
        out_specs=(qspec, pl.BlockSpec((s_len, 2 * kvd), lambda b, n: (b, 0)), _const_spec(bias_t.shape), _const_spec((1, LANES))),
        name=name, compiler_params=_params("arbitrary", "arbitrary"))(do, q, kv, kv, kv, kv, lse, bias_t, sinks)


def _merge_fwd(y, z, ng, o, gs, ga, w_ssm, w_attn, w_out, h_in, g_post, g_next, *, name):
    t, d = h_in.shape
    tm = _tile(t, 512)
    d_ssm = y.shape[1]
    gw = d_ssm // SSM_GROUPS

    def body(y_ref, z_ref, ng_ref, o_ref, gs_ref, ga_ref, ws_ref, wa_ref, wo_ref, hin_ref, gp_ref, gn_ref,
             yn_ref, ys_ref, ya_ref, mg_ref, mix_ref, hout_ref, next_ref):
        for g in range(SSM_GROUPS):
            sl = slice(g * gw, (g + 1) * gw)
            zv = z_ref[:, sl].astype(F32)
            yg = y_ref[:, sl] * (zv * _sigmoid(zv))
            r = lax.rsqrt(jnp.mean(yg * yg, axis=-1, keepdims=True) + RMS_EPS)
            yn_ref[:, sl] = (yg * r * ng_ref[:, sl]).astype(BF16)
        ys = _dot(yn_ref[...], ws_ref[...], NN)
        ya = _dot(o_ref[...], wa_ref[...], NN)
        merged = (_sigmoid(gs_ref[...].astype(F32)) * ys + _sigmoid(ga_ref[...].astype(F32)) * ya).astype(BF16)
        mix = _dot(merged, wo_ref[...], NN)
        ys_ref[...] = ys.astype(BF16)
        ya_ref[...] = ya.astype(BF16)
        mg_ref[...] = merged
        mix_ref[...] = mix
        hout = _rms_residual(mix, hin_ref[...], gp_ref[...], 1.0)
        hout_ref[...] = hout
        rn = lax.rsqrt(jnp.mean(hout * hout, axis=-1, keepdims=True) + RMS_EPS)
        next_ref[...] = (hout * rn * gn_ref[...]).astype(BF16)

    def row(w):
        return pl.BlockSpec((tm, w), lambda i: (i, 0))
    bshape = jax.ShapeDtypeStruct((t, d), BF16)
    fshape = jax.ShapeDtypeStruct((t, d), F32)
    return pl.pallas_call(
        body, out_shape=(jax.ShapeDtypeStruct((t, d_ssm), BF16), bshape, bshape, bshape, fshape, fshape, bshape), grid=(t // tm,),
        in_specs=[row(d_ssm), row(d_ssm), _const_spec((1, d_ssm)), row(o.shape[1]), row(d), row(d), _resident_spec(w_ssm.shape),
                  _resident_spec(w_attn.shape), _resident_spec(w_out.shape), row(d), _const_spec((1, d)), _const_spec((1, d))],
        out_specs=(row(d_ssm),) + (row(d),) * 6, name=name,
        compiler_params=_params("parallel"))(y, z, ng, o, gs, ga, w_ssm, w_attn, w_out, h_in, g_post, g_next)


def _merge_bwd(dh, mix, g_post, gs, ga, ys, ya, y, z, ng, w_ssm, w_attn, w_out, *, name):
    t, d = mix.shape
    tm = _tile(t, 256)
    d_ssm, d_attn = w_ssm.shape[0], w_attn.shape[0]
    gw = d_ssm // SSM_GROUPS

    def body(dh_ref, mix_ref, gp_ref, gs_ref, ga_ref, ys_ref, ya_ref, y_ref, z_ref, ng_ref, ws_ref, wa_ref, wo_ref,
             dmix_ref, dys_ref, dya_ref, dgs_ref, dga_ref, dy_ref, dz_ref, do_ref, dgp_ref, dng_ref):
        @pl.when(pl.program_id(0) == 0)
        def _():
            dgp_ref[...] = jnp.zeros_like(dgp_ref)
            dng_ref[...] = jnp.zeros_like(dng_ref)
        mv = mix_ref[...]
        dy = dh_ref[...]
        r = lax.rsqrt(jnp.mean(mv * mv, axis=-1, keepdims=True) + RMS_EPS)
        mhat = mv * r
        dyg = dy * gp_ref[...]
        dmix = (r * (dyg - mhat * jnp.mean(dyg * mhat, axis=-1, keepdims=True))).astype(BF16)
        dgp_ref[...] += jnp.sum(dy * mhat, axis=0, keepdims=True)
        dmix_ref[...] = dmix
        dmerged = _dot(dmix, wo_ref[...], NT)
        sgs = _sigmoid(gs_ref[...].astype(F32))
        sga = _sigmoid(ga_ref[...].astype(F32))
        dys = (dmerged * sgs).astype(BF16)
        dya = (dmerged * sga).astype(BF16)
        dys_ref[...] = dys
        dya_ref[...] = dya
        dgs_ref[...] = (dmerged * ys_ref[...].astype(F32) * sgs * (1.0 - sgs)).astype(BF16)
        dga_ref[...] = (dmerged * ya_ref[...].astype(F32) * sga * (1.0 - sga)).astype(BF16)
        do_ref[...] = _dot(dya, wa_ref[...], NT).astype(BF16)
        dyn = _dot(dys, ws_ref[...], NT)
        for g in range(SSM_GROUPS):
            sl = slice(g * gw, (g + 1) * gw)
            zv = z_ref[:, sl].astype(F32)
            yv = y_ref[:, sl]
            sg = _sigmoid(zv)
            sz = zv * sg
            yg = yv * sz
            rg = lax.rsqrt(jnp.mean(yg * yg, axis=-1, keepdims=True) + RMS_EPS)
            yhat = yg * rg
            dn = dyn[:, sl]
            dyg_n = dn * ng_ref[:, sl]
            dyg_g = rg * (dyg_n - yhat * jnp.mean(dyg_n * yhat, axis=-1, keepdims=True))
            dy_ref[:, sl] = (dyg_g * sz).astype(BF16)
            dz_ref[:, sl] = (dyg_g * yv * (sg * (1.0 + zv * (1.0 - sg)))).astype(BF16)
            dng_ref[:, sl] += jnp.sum(dn * yhat, axis=0, keepdims=True)

    def row(w):
        return pl.BlockSpec((tm, w), lambda i: (i, 0))

    def bshape(w):
        return jax.ShapeDtypeStruct((t, w), BF16)
    return pl.pallas_call(
        body, out_shape=(bshape(d),) * 5 + (bshape(d_ssm), bshape(d_ssm), bshape(d_attn), jax.ShapeDtypeStruct((1, d), F32),
                                            jax.ShapeDtypeStruct((1, d_ssm), F32)), grid=(t // tm,),
        in_specs=[row(d), row(d), _const_spec((1, d)), row(d), row(d), row(d), row(d), row(d_ssm), row(d_ssm), _const_spec((1, d_ssm)),
                  _resident_spec(w_ssm.shape), _resident_spec(w_attn.shape), _resident_spec(w_out.shape)],
        out_specs=(row(d),) * 5 + (row(d_ssm), row(d_ssm), row(d_attn), _const_spec((1, d)), _const_spec((1, d_ssm))),
        name=name, compiler_params=_params("arbitrary"))(dh, mix, g_post, gs, ga, ys, ya, y, z, ng, w_ssm, w_attn, w_out)


def _position():
    return lax.axis_index("x"), lax.axis_index("y"), lax.axis_index("c")


def _gather_exchange(shards):
    na = len(shards)

    def plan(ins, outs, sems):
        send_sems, recv_sems, local_sems = sems
        x, y, c = _position()
        me, sibling = (x, y, c), (x, y, 1 - c)
        chips = [(1 - x, y), (x, 1 - y), (1 - x, 1 - y)]

        def slot(a, pos):
            return outs[a].at[4 * pos[0] + 2 * pos[1] + pos[2]]

        def copy(a, k, block, to, src=None):
            return pltpu.make_async_remote_copy(
                src_ref=slot(a, block) if src is None else src, dst_ref=slot(a, block),
                send_sem=send_sems.at[a, k], recv_sem=recv_sems.at[a, k], device_id=to, device_id_type=MESH)

        mine = [pltpu.make_async_copy(ins[a], slot(a, me), local_sems.at[a]) for a in range(na)]
        first = []
        for a in range(na):
            first.append(copy(a, 0, me, sibling, src=ins[a]))
            first += [copy(a, 1 + j, me, (*chip, c), src=ins[a]) for j, chip in enumerate(chips)]
        return me, sibling, chips, copy, mine, first

    def start(ins, outs, sems):
        *_, mine, first = plan(ins, outs, sems)
        for cp in mine + first:
            cp.start()

    def finish(ins, outs, sems):
        me, sibling, chips, copy, mine, first = plan(ins, outs, sems)
        c = me[2]
        passed = []
        for a in range(na):
            for j, chip in enumerate(chips):
                copy(a, 1 + j, (*chip, c), me).wait_recv()
                fwd = copy(a, 4 + j, (*chip, c), sibling)
                fwd.start()
                passed.append(fwd)
        for a in range(na):
            copy(a, 0, sibling, me).wait_recv()
            for j, chip in enumerate(chips):
                copy(a, 4 + j, (*chip, 1 - c), me).wait_recv()
        for cp in first + passed:
            cp.wait_send()
        for cp in mine:
            cp.wait()

    return _Exchange(list(shards), [jax.ShapeDtypeStruct((N_DEV,) + s.shape, s.dtype) for s in shards],
                     [pltpu.SemaphoreType.DMA((na, 7)), pltpu.SemaphoreType.DMA((na, 7)), pltpu.SemaphoreType.DMA((na,))],
                     start, finish)


def _scatter_exchange(arrays):
    na = len(arrays)

    def copies(ins, outs, sems):
        send_sems, recv_sems = sems
        x, y, c = _position()
        out = []
        for a in range(na):
            for k in range(7):
                flip = k + 1
                peer = (x ^ (flip >> 2), y ^ ((flip >> 1) & 1), c ^ (flip & 1))
                peer_block = 4 * peer[0] + 2 * peer[1] + peer[2]
                out.append(pltpu.make_async_remote_copy(
                    src_ref=ins[a].at[peer_block], dst_ref=outs[a].at[k],
                    send_sem=send_sems.at[a, k], recv_sem=recv_sems.at[a, k], device_id=peer, device_id_type=MESH))
        return out

    def start(ins, outs, sems):
        for cp in copies(ins, outs, sems):
            cp.start()

    def finish(ins, outs, sems):
        for cp in copies(ins, outs, sems):
            cp.wait()

    return _Exchange(list(arrays), [jax.ShapeDtypeStruct((7,) + s.shape[1:], s.dtype) for s in arrays],
                     [pltpu.SemaphoreType.DMA((na, 7)), pltpu.SemaphoreType.DMA((na, 7))], start, finish)


def _reduce_adamw(own, recv, w, m, v, *, name):
    r, c = own.shape
    tm = _tile(r, 256)
    c1 = 1.0 - ADAM_B1 ** ADAM_STEP
    c2 = 1.0 - ADAM_B2 ** ADAM_STEP

    def body(own_ref, recv_ref, w_ref, m_ref, v_ref, g_ref, d_ref, mo_ref, vo_ref):
        gv = own_ref[...]
        for k in range(7):
            gv = gv + recv_ref[k].astype(F32)
        g_ref[...] = gv
        mn = ADAM_B1 * m_ref[...] + (1.0 - ADAM_B1) * gv
        vn = ADAM_B2 * v_ref[...] + (1.0 - ADAM_B2) * (gv * gv)
        mo_ref[...] = mn
        vo_ref[...] = vn
        d_ref[...] = -ADAM_LR * ((mn / c1) / (jnp.sqrt(vn / c2) + ADAM_EPS) + ADAM_WD * w_ref[...])

    blk = pl.BlockSpec((tm, c), lambda i: (i, 0))
    shp = jax.ShapeDtypeStruct((r, c), F32)
    return pl.pallas_call(body, out_shape=(shp,) * 4, grid=(r // tm,),
                          in_specs=[blk, pl.BlockSpec((7, tm, c), lambda i: (0, i, 0)), blk, blk, blk], out_specs=(blk,) * 4,
                          name=name, compiler_params=_params("parallel"))(own, recv, w, m, v)


SMALL_ROW = 8 * LANES


def _update_replicated(partials, ws, ms, vs, loss_part, *, name):
    n = len(ws)
    pieces, n_rows = [], 0
    for k in sorted(range(n), key=lambda k_: -ws[k_].shape[0]):
        r, c = ws[k].shape
        assert (r == 1 or c <= SMALL_ROW) and (r == 1 or n_rows % 8 == 0)
        for c0 in range(0, c, SMALL_ROW):
            pieces.append((k, slice(0, r), slice(c0, min(c0 + SMALL_ROW, c)), n_rows))
            n_rows += r
    loss_row = n_rows
    n_rows = -(-(n_rows + 1) // 8) * 8
    c1 = 1.0 - ADAM_B1 ** ADAM_STEP
    c2 = 1.0 - ADAM_B2 ** ADAM_STEP

    def reduce_body(*refs):
        g_in, loss_ref, total, buf, send_sems, recv_sems = refs[:n], refs[n], refs[n + 1], refs[n + 2], refs[n + 3], refs[n + 4]
        x, y, c_ = _position()
        me = 4 * x + 2 * y + c_
        buf[me] = jnp.zeros((n_rows, SMALL_ROW), F32)
        for k, rs, cs, row0 in pieces:
            buf[me, row0:row0 + rs.stop, 0:cs.stop - cs.start] = g_in[k][rs, cs]
        buf[me, loss_row:loss_row + 1, 0:LANES] = loss_ref[...]
        copies = []
        for j in range(7):
            flip = j + 1
            peer = (x ^ (flip >> 2), y ^ ((flip >> 1) & 1), c_ ^ (flip & 1))
            cp = pltpu.make_async_remote_copy(
                src_ref=buf.at[me], dst_ref=buf.at[me], send_sem=send_sems.at[j], recv_sem=recv_sems.at[j],
                device_id=peer, device_id_type=MESH)
            cp.start()
            copies.append(cp)
        for cp in copies:
            cp.wait()
        acc = buf[0]
        for dev in range(1, N_DEV):
            acc = acc + buf[dev]
        total[...] = acc

    def step_body(*refs):
        total = refs[0]
        w_in_, m_in, v_in = (refs[1 + j * n:1 + (j + 1) * n] for j in range(3))
        g_out, d_out, m_out, v_out = (refs[1 + (3 + j) * n:1 + (4 + j) * n] for j in range(4))
        for k, rs, cs, row0 in pieces:
            gv = total[row0:row0 + rs.stop, 0:cs.stop - cs.start]
            mn = ADAM_B1 * m_in[k][rs, cs] + (1.0 - ADAM_B1) * gv
            vn = ADAM_B2 * v_in[k][rs, cs] + (1.0 - ADAM_B2) * (gv * gv)
            g_out[k][rs, cs] = gv
            m_out[k][rs, cs] = mn
            v_out[k][rs, cs] = vn
            d_out[k][rs, cs] = -ADAM_LR * ((mn / c1) / (jnp.sqrt(vn / c2) + ADAM_EPS) + ADAM_WD * w_in_[k][rs, cs])

    vm = pl.BlockSpec(memory_space=pltpu.VMEM)
    total = pl.pallas_call(
        reduce_body, out_shape=jax.ShapeDtypeStruct((n_rows, SMALL_ROW), F32), in_specs=[vm] * (n + 1), out_specs=vm,
        scratch_shapes=[pltpu.VMEM((N_DEV, n_rows, SMALL_ROW), F32), pltpu.SemaphoreType.DMA((7,)), pltpu.SemaphoreType.DMA((7,))],
        name=f"{name}_allreduce")(*partials, loss_part)
    shapes = tuple(jax.ShapeDtypeStruct(w.shape, F32) for w in ws)
    outs = pl.pallas_call(step_body, out_shape=shapes * 4, in_specs=[vm] * (1 + 3 * n), out_specs=tuple([vm] * (4 * n)),
                          name=f"{name}_adamw")(total, *ws, *ms, *vs)
    return tuple(outs[j * n:(j + 1) * n] for j in range(4)) + (total[loss_row, 0],)


def _pad_lanes(v, width=LANES):
    return jnp.pad(v, ((0, 0), (0, width - v.shape[1])))


def kernel(x, ffn1_pre_g, ffn1_w_gate, ffn1_w_up, ffn1_w_down, ffn1_post_g, mix_pre_g, w_in, conv_w, conv_b, dt_bias, a_log, d_skip, ssm_norm_g, w_ssm_proj, attn_sinks, rel_bias_table, w_attn_proj, w_out, mix_post_g, ffn2_pre_g, ffn2_w_gate, ffn2_w_up, ffn2_w_down, ffn2_post_g, loss_target, m_ffn1_pre_g, m_ffn1_w_gate, m_ffn1_w_up, m_ffn1_w_down, m_ffn1_post_g, m_mix_pre_g, m_w_in, m_conv_w, m_conv_b, m_dt_bias, m_a_log, m_d_skip, m_ssm_norm_g, m_w_ssm_proj, m_attn_sinks, m_rel_bias_table, m_w_attn_proj, m_w_out, m_mix_post_g, m_ffn2_pre_g, m_ffn2_w_gate, m_ffn2_w_up, m_ffn2_w_down, m_ffn2_post_g, v_ffn1_pre_g, v_ffn1_w_gate, v_ffn1_w_up, v_ffn1_w_down, v_ffn1_post_g, v_mix_pre_g, v_w_in, v_conv_w, v_conv_b, v_dt_bias, v_a_log, v_d_skip, v_ssm_norm_g, v_w_ssm_proj, v_attn_sinks, v_rel_bias_table, v_w_attn_proj, v_w_out, v_mix_post_g, v_ffn2_pre_g, v_ffn2_w_gate, v_ffn2_w_up, v_ffn2_w_down, v_ffn2_post_g):
    args = dict(locals())
    weight_names = ['ffn1_pre_g', 'ffn1_w_gate', 'ffn1_w_up', 'ffn1_w_down', 'ffn1_post_g', 'mix_pre_g', 'w_in', 'conv_w', 'conv_b',
                    'dt_bias', 'a_log', 'd_skip', 'ssm_norm_g', 'w_ssm_proj', 'attn_sinks', 'rel_bias_table', 'w_attn_proj', 'w_out',
                    'mix_post_g', 'ffn2_pre_g', 'ffn2_w_gate', 'ffn2_w_up', 'ffn2_w_down', 'ffn2_post_g']
    col_sharded = ('ffn1_w_gate', 'ffn1_w_up', 'w_in', 'ffn2_w_gate', 'ffn2_w_up')
    row_sharded = ('ffn1_w_down', 'w_ssm_proj', 'w_attn_proj', 'w_out', 'ffn2_w_down')
    big = col_sharded + row_sharded

    bl, s_len, d = x.shape
    t = bl * s_len
    d_inner = ssm_norm_g.shape[1]
    n_heads = dt_bias.shape[1]
    gn = SSM_GROUPS * SSM_STATE
    conv_dim = d_inner + 2 * gn
    q_dim = ATTN_Q_HEADS * ATTN_HEAD_DIM
    kv_dim = ATTN_KV_HEADS * ATTN_HEAD_DIM

    def local_2d(name, a):
        a = a[0]
        return a.T if name in col_sharded else a

    ffn1_names = ('ffn1_w_gate', 'ffn1_w_up', 'ffn1_w_down')
    ffn2_names = ('ffn2_w_gate', 'ffn2_w_up', 'ffn2_w_down')
    mixer_names = ('w_ssm_proj', 'w_attn_proj', 'w_out')

    def shard(n):
        return local_2d(n, args[n]).astype(BF16)

    def rows(g):
        return g.reshape(N_DEV * g.shape[1], g.shape[2])

    x2 = x.reshape(t, d)
    tgt2 = loss_target.reshape(t, d)
    full = {}

    (h1,), saved1, ffn1_w, got_in = _ffn_forward(
        x2, ffn1_pre_g, lambda got: (rows(got[0]), rows(got[1])), lambda got: rows(got[2]), ffn1_post_g, "ffn1",
        side_norm=_gather_exchange([shard(n) for n in ffn1_names[:2]]),
        side_up=_gather_exchange([shard('w_in'), conv_w[0], shard('ffn1_w_down')]))
    full.update(zip(ffn1_names, ffn1_w))
    conv_w_full = jnp.transpose(got_in[1], (1, 0, 2)).reshape(SSM_CONV, conv_dim)

    win_t = rows(got_in[0])
    dt_lo = 2 * d + d_inner + conv_dim
    off = {'gs': 0, 'ga': d, 'z': 2 * d, 'xbc': 2 * d + d_inner, 'dt': dt_lo, 'q': dt_lo + n_heads, 'kv': dt_lo + n_heads + q_dim}
    assert all(o_ % 16 == 0 for o_ in off.values())

    u, (gs, ga, z, xbc, q, kv, dt_raw) = _proj_all(
        h1, mix_pre_g, win_t,
        [(off['gs'], d, BF16), (off['ga'], d, BF16), (off['z'], d_inner, BF16), (off['xbc'], conv_dim, BF16),
         (off['q'], q_dim, BF16), (off['kv'], 2 * kv_dim, BF16), (off['dt'], n_heads, F32)], name="mix_proj")

    dtb_p, alog_p, dsk_p, sinks_p = _pad_lanes(dt_bias), _pad_lanes(a_log), _pad_lanes(d_skip), _pad_lanes(attn_sinks)
    xc = _conv_fwd(xbc, conv_w_full, conv_b, bl, name="conv_fwd")
    late_names = mixer_names + ffn2_names
    (y, hprev), got_late = _ssd_fwd(xc, dt_raw, dtb_p, alog_p, dsk_p, bl, n_heads,
                                    side=_gather_exchange([shard(n) for n in late_names]), name="ssd_fwd")
    full.update({n: rows(g) for n, g in zip(late_names, got_late)})

    onehot = _bucket_onehot()
    rep = ATTN_Q_HEADS // ATTN_KV_HEADS
    bias = _small_mm_hi(rel_bias_table.T, onehot, NN, name="rel_bias")
    bias_t = jnp.transpose(bias.reshape(ATTN_KV_HEADS, rep, CHUNK, 2 * CHUNK), (0, 3, 1, 2)).reshape(ATTN_KV_HEADS, 2 * CHUNK, rep * CHUNK)
    o, lse = _attn_fwd(q, kv, bias_t, sinks_p, bl, name="attn_fwd")

    yn, ys, ya, merged, mix, h2, n2 = _merge_fwd(y, z, ssm_norm_g, o, gs, ga, full['w_ssm_proj'], full['w_attn_proj'], full['w_out'],
                                                 h1, mix_post_g, ffn2_pre_g, name="merge_fwd")

    (dh3, loss_vec, df2, dg_post2), saved2, _, _ = _ffn_forward(
        h2, ffn2_pre_g, (full['ffn2_w_gate'], full['ffn2_w_up']), full['ffn2_w_down'], ffn2_post_g, "ffn2", target=tgt2, n=n2)

    grads, own, wire, received = {}, {}, {}, {}
    dh2, grads['ffn2_pre_g'], grads['ffn2_post_g'], g32, g16, _ = _ffn_backward(
        dh3, saved2, ffn2_pre_g, full['ffn2_w_gate'], full['ffn2_w_up'], full['ffn2_w_down'], "ffn2", (df2, dg_post2))
    own.update(zip(ffn2_names, map(_stack8, g32)))
    wire.update(zip(ffn2_names, map(_stack8, g16)))

    dmix, dys, dya, dgs, dga, dy, dz, do, grads['mix_post_g'], grads['ssm_norm_g'] = _merge_bwd(
        dh2, mix, mix_post_g, gs, ga, ys, ya, y, z, ssm_norm_g, full['w_ssm_proj'], full['w_attn_proj'], full['w_out'],
        name="merge_bwd")
    for n, (lhs, rhs) in zip(mixer_names, ((yn, dys), (o, dya), (merged, dmix))):
        g32_, g16_, _ = _mm_tn([lhs], rhs, name=f"d{n}")
        own[n], wire[n] = _stack8(g32_), _stack8(g16_)

    dq, dkv, dbias_t, dsinks = _attn_bwd(do, q, kv, lse, bias_t, sinks_p, bl, name="attn_bwd")
    dbias = jnp.transpose(dbias_t.reshape(ATTN_KV_HEADS, 2 * CHUNK, rep, CHUNK), (0, 2, 3, 1)).reshape(ATTN_Q_HEADS, -1)
    d_table = _small_mm_hi(onehot, dbias, NT, name="rel_bias_bwd")


    first_group = ffn2_names + mixer_names
    (dxc, ddt_raw, ddtb, dalog, ddsk), got = _ssd_bwd(dy, y, xc, dt_raw, hprev, dtb_p, alog_p, dsk_p, bl, n_heads,
                                                      side=_scatter_exchange([wire[n] for n in first_group]), name="ssd_bwd")
    received.update(zip(first_group, got))
    dxbc, dconv_w8, grads['conv_b'] = _conv_bwd(dxc, xbc, conv_w_full, conv_b, bl, name="conv_bwd")

    wide32, wide16, _ = _mm_tn([dgs, dga, dz, dxbc, dq], u, name="dw_in")
    kv32, kv16, _ = _mm_tn([dkv], u, name="dw_in_kv")
    dt32, dt16, _ = _mm_tn([ddt_raw], u, name="dw_in_dt")

    def original_order(wide, kv_part, dt_part):
        return jnp.concatenate([wide[:dt_lo], dt_part[:n_heads], wide[dt_lo:], kv_part], axis=0)
    me = 4 * lax.axis_index("x") + 2 * lax.axis_index("y") + lax.axis_index("c")
    blk_rows = win_t.shape[0] // N_DEV
    wire['w_in'] = _stack8(original_order(wide16, kv16, dt16))
    own_w_in = lax.dynamic_slice_in_dim(original_order(wide32, kv32, dt32), me * blk_rows, blk_rows)
    own['conv_w'] = jnp.transpose(dconv_w8[:SSM_CONV].reshape(SSM_CONV, N_DEV, conv_dim // N_DEV), (1, 0, 2))

    segs = [(g_, 0, off[k_]) for g_, k_ in zip([dgs, dga, dz, dxbc, dq, dkv, ddt_raw], ('gs', 'ga', 'z', 'xbc', 'q', 'kv', 'dt'))]
    dh1, grads['mix_pre_g'], post1, got = _mm_nn_rmsbwd(segs, [win_t], h1, mix_pre_g, dh2, below=(saved1[5], ffn1_post_g), tm=256,
                                                        side=_scatter_exchange([wire['w_in'], own['conv_w']]), name="mix_du")
    received.update(zip(('w_in', 'conv_w'), got))

    dx2, grads['ffn1_pre_g'], grads['ffn1_post_g'], g32, _, got = _ffn_backward(
        dh1, saved1, ffn1_pre_g, full['ffn1_w_gate'], full['ffn1_w_up'], full['ffn1_w_down'], "ffn1", post1, chain=True)
    own.update(zip(ffn1_names, map(_stack8, g32)))
    received.update(zip(ffn1_names, got))

    def own_block(a):
        return lax.dynamic_index_in_dim(a, me, 0, keepdims=False)
    out_g, out_d, out_m, out_v = {}, {}, {}, {}
    for n in big:
        w2, m2, v2 = local_2d(n, args[n]), local_2d(n, args['m_' + n]), local_2d(n, args['v_' + n])
        results = _reduce_adamw(own_w_in if n == 'w_in' else own_block(own[n]), received[n], w2, m2, v2, name=f"update_{n}")
        out_g[n], out_d[n], out_m[n], out_v[n] = ((a.T if n in col_sharded else a)[None] for a in results)
    results = _reduce_adamw(own_block(own['conv_w']), received['conv_w'], conv_w[0], m_conv_w[0], v_conv_w[0], name="update_conv_w")
    out_g['conv_w'], out_d['conv_w'], out_m['conv_w'], out_v['conv_w'] = (a[None] for a in results)

    grads['dt_bias'], grads['a_log'], grads['d_skip'], grads['attn_sinks'], grads['rel_bias_table'] = ddtb, dalog, ddsk, dsinks, d_table
    small = [n for n in weight_names if n not in big and n != 'conv_w']
    *results, loss = _update_replicated([grads[n] for n in small], [args[n] for n in small], [args['m_' + n] for n in small],
                                        [args['v_' + n] for n in small], loss_vec, name="update_replicated")
    for dst, vals in zip((out_g, out_d, out_m, out_v), results):
        dst.update(zip(small, vals))

    grad_x = dx2.reshape(bl, s_len, d)
    return (loss, grad_x, *[out_g[n] for n in weight_names], *[out_d[n] for n in weight_names],
            *[out_m[n] for n in weight_names], *[out_v[n] for n in weight_names])
```
